```python
import jax, jax.numpy as jnp
from jax import lax
import numpy as np

D_MODEL = 2048
BATCH = 8
SEQ = 2048
DEPTH = 2

D_MIX = D_MODEL
GLA_HEADS = 6
GLA_DK = 64
GLA_DV = 128
GLA_GATE_RANK = 16
GLA_GATE_TEMP = 16.0
GLA_CHUNK = 64
GLA_WIDTH = GLA_HEADS * GLA_DV
MLA_HEADS = 6
MLA_Q_LORA = 384
MLA_KV_LORA = 256
MLA_NOPE = 128
MLA_ROPE = 64
MLA_DV = 128
MLA_WIDTH = MLA_HEADS * MLA_DV
ROPE_THETA = 10000.0
Q_BLOCK = 128
CONV_CH = D_MIX - GLA_WIDTH - MLA_WIDTH
CONV_K = 3
EPS = 1e-6

IN_SPLIT_SIZES = (
    GLA_HEADS * GLA_DK,
    GLA_HEADS * GLA_DK,
    GLA_WIDTH,
    2 * GLA_GATE_RANK,
    MLA_Q_LORA,
    MLA_KV_LORA,
    MLA_ROPE,
    CONV_CH,
    CONV_CH,
    CONV_CH,
    D_MIX,
)
IN_DIM = sum(IN_SPLIT_SIZES)

kernel_name = "bidir_hybrid_gla_mla_shortconv_adaln"


def rms_norm(x, g):
    xf = x.astype(jnp.float32)
    y = xf * lax.rsqrt(jnp.mean(xf * xf, axis=-1, keepdims=True) + EPS)
    return (y * g.astype(jnp.float32)).astype(x.dtype)


def rope(x, cos, sin):
    x1, x2 = jnp.split(x.astype(jnp.float32), 2, axis=-1)
    out = jnp.concatenate([x1 * cos - x2 * sin, x2 * cos + x1 * sin], axis=-1)
    return out.astype(x.dtype)


def gla_chunked(q, k, v, log_a, strict):
    bsz, s, h, dk = q.shape
    dv = v.shape[-1]
    n = s // GLA_CHUNK

    def to_chunks(t):
        return t.reshape(bsz, n, GLA_CHUNK, h, t.shape[-1]).transpose(0, 3, 1, 2, 4).astype(jnp.float32)

    q, k, v, g = to_chunks(q), to_chunks(k), to_chunks(v), to_chunks(log_a)
    b = jnp.cumsum(g, axis=3)
    b_last = b[:, :, :, -1:, :]
    q_dec = q * jnp.exp(b)
    k_inv = k * jnp.exp(-b)
    k_to_end = k * jnp.exp(b_last - b)
    scores = jnp.einsum('bhncd,bhnjd->bhncj', q_dec, k_inv)
    mask = jnp.tril(jnp.ones((GLA_CHUNK, GLA_CHUNK), dtype=bool), k=-1 if strict else 0)
    o_intra = jnp.einsum('bhncj,bhnje->bhnce', jnp.where(mask, scores, 0.0), v)
    chunk_kv = jnp.einsum('bhncd,bhnce->bhnde', k_to_end, v)
    chunk_decay = jnp.exp(b_last[:, :, :, 0, :])

    def step(state, inp):
        decay, kv = inp
        return decay[..., None] * state + kv, state

    init = jnp.zeros((bsz, h, dk, dv), jnp.float32)
    _, s_before = lax.scan(step, init, (jnp.moveaxis(chunk_decay, 2, 0), jnp.moveaxis(chunk_kv, 2, 0)))
    s_before = jnp.moveaxis(s_before, 0, 2)
    o = o_intra + jnp.einsum('bhncd,bhnde->bhnce', q_dec, s_before)
    return o.transpose(0, 2, 3, 1, 4).reshape(bsz, s, h, dv)


def mla_attention(q_nope, q_rope, k_nope, k_rope, v):
    bsz, s, h, _ = q_nope.shape
    nb = s // Q_BLOCK
    scale = (MLA_NOPE + MLA_ROPE) ** -0.5

    def blocks(t):
        return t.reshape(bsz, nb, Q_BLOCK, *t.shape[2:]).swapaxes(0, 1)

    def attend(qb):
        qn, qr = qb
        sc = jnp.einsum('bqhd,bkhd->bhqk', qn, k_nope) + jnp.einsum('bqhr,bkr->bhqk', qr, k_rope)
        p = jax.nn.softmax(sc.astype(jnp.float32) * scale, axis=-1).astype(v.dtype)
        return jnp.einsum('bhqk,bkhd->bqhd', p, v)

    o = lax.map(attend, (blocks(q_nope), blocks(q_rope)))
    return o.swapaxes(0, 1).reshape(bsz, s, h * v.shape[-1])


def hybrid_layer(x, c_act, cos, sin, ada_w, ada_b, norm_g, w_in,
                 gla_wg_f, gla_bg_f, gla_wg_b, gla_bg_b, gla_norm_g,
                 mla_q_norm_g, mla_kv_norm_g, mla_w_uq, mla_w_ukv, mla_out_g,
                 conv_w, conv_out_g, w_out):
    bsz, s, _ = x.shape
    shift, scale, gate = jnp.split(c_act @ ada_w + ada_b, 3, axis=-1)
    h = rms_norm(x, norm_g) * (1.0 + scale[:, None, :]) + shift[:, None, :]
    proj = h @ w_in
    split_idx = [int(i) for i in np.cumsum(IN_SPLIT_SIZES)[:-1]]
    (gq, gk, gv, g_lr, mq, mkv, mkr, cb, cc, cx, z) = jnp.split(proj, split_idx, axis=-1)

    q = gq.reshape(bsz, s, GLA_HEADS, GLA_DK) * (GLA_DK ** -0.5)
    k = gk.reshape(bsz, s, GLA_HEADS, GLA_DK)
    v = gv.reshape(bsz, s, GLA_HEADS, GLA_DV)
    lr_f, lr_b = jnp.split(g_lr, 2, axis=-1)
    la_f = jax.nn.log_sigmoid((lr_f @ gla_wg_f + gla_bg_f).astype(jnp.float32)) / GLA_GATE_TEMP
    la_b = jax.nn.log_sigmoid((lr_b @ gla_wg_b + gla_bg_b).astype(jnp.float32)) / GLA_GATE_TEMP
    la_f = la_f.reshape(bsz, s, GLA_HEADS, GLA_DK)
    la_b = la_b.reshape(bsz, s, GLA_HEADS, GLA_DK)
    o_fwd = gla_chunked(q, k, v, la_f, strict=False)
    o_bwd = jnp.flip(gla_chunked(jnp.flip(q, 1), jnp.flip(k, 1), jnp.flip(v, 1), jnp.flip(la_b, 1), strict=True), 1)
    o_gla = rms_norm((o_fwd + o_bwd).astype(x.dtype), gla_norm_g).reshape(bsz, s, GLA_WIDTH)

    cq = rms_norm(mq, mla_q_norm_g)
    qm = (cq @ mla_w_uq).reshape(bsz, s, MLA_HEADS, MLA_NOPE + MLA_ROPE)
    q_nope, q_rope = qm[..., :MLA_NOPE], rope(qm[..., MLA_NOPE:], cos[:, :, None, :], sin[:, :, None, :])
    ckv = rms_norm(mkv, mla_kv_norm_g)
    kv = (ckv @ mla_w_ukv).reshape(bsz, s, MLA_HEADS, MLA_NOPE + MLA_DV)
    k_nope, v_m = kv[..., :MLA_NOPE], kv[..., MLA_NOPE:]
    k_rope = rope(mkr, cos, sin)
    o_mla = rms_norm(mla_attention(q_nope, q_rope, k_nope, k_rope, v_m), mla_out_g)

    u = cc * cx
    up = jnp.pad(u, ((0, 0), (1, 1), (0, 0)))
    conv = up[:, :-2] * conv_w[0] + up[:, 1:-1] * conv_w[1] + up[:, 2:] * conv_w[2]
    o_conv = rms_norm(cb * conv, conv_out_g)

    y = jnp.concatenate([o_gla, o_mla, o_conv], axis=-1) * jax.nn.silu(z)
    return x + gate[:, None, :] * (y @ w_out)


def _fwd_setup_inputs(seed: int = 0) -> dict:
    key = jax.random.key(seed)
    ks = jax.random.split(key, 24)
    f32 = jnp.float32

    def nrm(k, shape, std):
        return std * jax.random.normal(k, shape, f32)

    def gain(k, shape):
        return 1.0 + 0.1 * jax.random.normal(k, shape, f32)

    L = DEPTH
    offsets = jax.random.randint(ks[2], (BATCH, 1), 0, 1024, dtype=jnp.int32)
    positions = offsets + jnp.arange(SEQ, dtype=jnp.int32)[None, :]
    return {
        "x": nrm(ks[0], (BATCH, SEQ, D_MODEL), 1.0),
        "c": nrm(ks[1], (BATCH, D_MODEL), 1.0),
        "positions": positions,
        "ada_w": nrm(ks[3], (L, D_MODEL, 3 * D_MODEL), 0.5 * D_MODEL ** -0.5),
        "ada_b": nrm(ks[4], (L, 3 * D_MODEL), 0.02),
        "norm_g": gain(ks[5], (L, D_MODEL)),
        "w_in": nrm(ks[6], (L, D_MODEL, IN_DIM), D_MODEL ** -0.5),
        "gla_wg_f": nrm(ks[7], (L, GLA_GATE_RANK, GLA_HEADS * GLA_DK), GLA_GATE_RANK ** -0.5),
        "gla_bg_f": nrm(ks[8], (L, GLA_HEADS * GLA_DK), 0.1),
        "gla_wg_b": nrm(ks[9], (L, GLA_GATE_RANK, GLA_HEADS * GLA_DK), GLA_GATE_RANK ** -0.5),
        "gla_bg_b": nrm(ks[10], (L, GLA_HEADS * GLA_DK), 0.1),
        "gla_norm_g": gain(ks[11], (L, GLA_DV)),
        "mla_q_norm_g": gain(ks[12], (L, MLA_Q_LORA)),
        "mla_kv_norm_g": gain(ks[13], (L, MLA_KV_LORA)),
        "mla_w_uq": nrm(ks[14], (L, MLA_Q_LORA, MLA_HEADS * (MLA_NOPE + MLA_ROPE)), MLA_Q_LORA ** -0.5),
        "mla_w_ukv": nrm(ks[15], (L, MLA_KV_LORA, MLA_HEADS * (MLA_NOPE + MLA_DV)), MLA_KV_LORA ** -0.5),
        "mla_out_g": gain(ks[16], (L, MLA_WIDTH)),
        "conv_w": nrm(ks[17], (L, CONV_K, CONV_CH), CONV_K ** -0.5),
        "conv_out_g": gain(ks[18], (L, CONV_CH)),
        "w_out": nrm(ks[19], (L, D_MIX, D_MODEL), D_MIX ** -0.5),
        "final_g": gain(ks[20], (D_MODEL,)),
    }


def _fwd_reference(x, c, positions, ada_w, ada_b, norm_g, w_in,
              gla_wg_f, gla_bg_f, gla_wg_b, gla_bg_b, gla_norm_g,
              mla_q_norm_g, mla_kv_norm_g, mla_w_uq, mla_w_ukv, mla_out_g,
              conv_w, conv_out_g, w_out, final_g):
    inv_freq = ROPE_THETA ** (-jnp.arange(0, MLA_ROPE, 2, dtype=jnp.float32) / MLA_ROPE)
    ang = positions.astype(jnp.float32)[..., None] * inv_freq
    cos, sin = jnp.cos(ang), jnp.sin(ang)
    c_act = jax.nn.silu(c)
    h = x
    for l in range(DEPTH):
        h = hybrid_layer(h, c_act, cos, sin, ada_w[l], ada_b[l], norm_g[l], w_in[l],
                         gla_wg_f[l], gla_bg_f[l], gla_wg_b[l], gla_bg_b[l], gla_norm_g[l],
                         mla_q_norm_g[l], mla_kv_norm_g[l], mla_w_uq[l], mla_w_ukv[l], mla_out_g[l],
                         conv_w[l], conv_out_g[l], w_out[l])
    return rms_norm(h, final_g)


import jax as _jax
import jax.numpy as _jnp

TWIN_FORMAT = 'train_step'
FWD_PARAMS = ['x', 'c', 'positions', 'ada_w', 'ada_b', 'norm_g', 'w_in', 'gla_wg_f', 'gla_bg_f', 'gla_wg_b', 'gla_bg_b', 'gla_norm_g', 'mla_q_norm_g', 'mla_kv_norm_g', 'mla_w_uq', 'mla_w_ukv', 'mla_out_g', 'conv_w', 'conv_out_g', 'w_out', 'final_g']
TWIN_WEIGHTS = ['ada_w', 'ada_b', 'norm_g', 'w_in', 'gla_wg_f', 'gla_bg_f', 'gla_wg_b', 'gla_bg_b', 'gla_norm_g', 'mla_q_norm_g', 'mla_kv_norm_g', 'mla_w_uq', 'mla_w_ukv', 'mla_out_g', 'conv_w', 'conv_out_g', 'w_out', 'final_g']
TWIN_DIFF_INPUT = 'x'
TWIN_INPUTS = ['x', 'c', 'positions', 'ada_w', 'ada_b', 'norm_g', 'w_in', 'gla_wg_f', 'gla_bg_f', 'gla_wg_b', 'gla_bg_b', 'gla_norm_g', 'mla_q_norm_g', 'mla_kv_norm_g', 'mla_w_uq', 'mla_w_ukv', 'mla_out_g', 'conv_w', 'conv_out_g', 'w_out', 'final_g', 'loss_target', 'm_ada_w', 'm_ada_b', 'm_norm_g', 'm_w_in', 'm_gla_wg_f', 'm_gla_bg_f', 'm_gla_wg_b', 'm_gla_bg_b', 'm_gla_norm_g', 'm_mla_q_norm_g', 'm_mla_kv_norm_g', 'm_mla_w_uq', 'm_mla_w_ukv', 'm_mla_out_g', 'm_conv_w', 'm_conv_out_g', 'm_w_out', 'm_final_g', 'v_ada_w', 'v_ada_b', 'v_norm_g', 'v_w_in', 'v_gla_wg_f', 'v_gla_bg_f', 'v_gla_wg_b', 'v_gla_bg_b', 'v_gla_norm_g', 'v_mla_q_norm_g', 'v_mla_kv_norm_g', 'v_mla_w_uq', 'v_mla_w_ukv', 'v_mla_out_g', 'v_conv_w', 'v_conv_out_g', 'v_w_out', 'v_final_g']
TWIN_OUTPUTS = ['loss', 'grad_x', 'grad_ada_w', 'grad_ada_b', 'grad_norm_g', 'grad_w_in', 'grad_gla_wg_f', 'grad_gla_bg_f', 'grad_gla_wg_b', 'grad_gla_bg_b', 'grad_gla_norm_g', 'grad_mla_q_norm_g', 'grad_mla_kv_norm_g', 'grad_mla_w_uq', 'grad_mla_w_ukv', 'grad_mla_out_g', 'grad_conv_w', 'grad_conv_out_g', 'grad_w_out', 'grad_final_g', 'delta_ada_w', 'delta_ada_b', 'delta_norm_g', 'delta_w_in', 'delta_gla_wg_f', 'delta_gla_bg_f', 'delta_gla_wg_b', 'delta_gla_bg_b', 'delta_gla_norm_g', 'delta_mla_q_norm_g', 'delta_mla_kv_norm_g', 'delta_mla_w_uq', 'delta_mla_w_ukv', 'delta_mla_out_g', 'delta_conv_w', 'delta_conv_out_g', 'delta_w_out', 'delta_final_g', 'new_m_ada_w', 'new_m_ada_b', 'new_m_norm_g', 'new_m_w_in', 'new_m_gla_wg_f', 'new_m_gla_bg_f', 'new_m_gla_wg_b', 'new_m_gla_bg_b', 'new_m_gla_norm_g', 'new_m_mla_q_norm_g', 'new_m_mla_kv_norm_g', 'new_m_mla_w_uq', 'new_m_mla_w_ukv', 'new_m_mla_out_g', 'new_m_conv_w', 'new_m_conv_out_g', 'new_m_w_out', 'new_m_final_g', 'new_v_ada_w', 'new_v_ada_b', 'new_v_norm_g', 'new_v_w_in', 'new_v_gla_wg_f', 'new_v_gla_bg_f', 'new_v_gla_wg_b', 'new_v_gla_bg_b', 'new_v_gla_norm_g', 'new_v_mla_q_norm_g', 'new_v_mla_kv_norm_g', 'new_v_mla_w_uq', 'new_v_mla_w_ukv', 'new_v_mla_out_g', 'new_v_conv_w', 'new_v_conv_out_g', 'new_v_w_out', 'new_v_final_g']
TWIN_LEAF_KINDS = {'loss': 'loss', 'grad_x': 'grad_x', 'grad_ada_w': 'grad_w', 'grad_ada_b': 'grad_w', 'grad_norm_g': 'grad_w', 'grad_w_in': 'grad_w', 'grad_gla_wg_f': 'grad_w', 'grad_gla_bg_f': 'grad_w', 'grad_gla_wg_b': 'grad_w', 'grad_gla_bg_b': 'grad_w', 'grad_gla_norm_g': 'grad_w', 'grad_mla_q_norm_g': 'grad_w', 'grad_mla_kv_norm_g': 'grad_w', 'grad_mla_w_uq': 'grad_w', 'grad_mla_w_ukv': 'grad_w', 'grad_mla_out_g': 'grad_w', 'grad_conv_w': 'grad_w', 'grad_conv_out_g': 'grad_w', 'grad_w_out': 'grad_w', 'grad_final_g': 'grad_w', 'delta_ada_w': 'delta_w', 'delta_ada_b': 'delta_w', 'delta_norm_g': 'delta_w', 'delta_w_in': 'delta_w', 'delta_gla_wg_f': 'delta_w', 'delta_gla_bg_f': 'delta_w', 'delta_gla_wg_b': 'delta_w', 'delta_gla_bg_b': 'delta_w', 'delta_gla_norm_g': 'delta_w', 'delta_mla_q_norm_g': 'delta_w', 'delta_mla_kv_norm_g': 'delta_w', 'delta_mla_w_uq': 'delta_w', 'delta_mla_w_ukv': 'delta_w', 'delta_mla_out_g': 'delta_w', 'delta_conv_w': 'delta_w', 'delta_conv_out_g': 'delta_w', 'delta_w_out': 'delta_w', 'delta_final_g': 'delta_w', 'new_m_ada_w': 'new_m', 'new_m_ada_b': 'new_m', 'new_m_norm_g': 'new_m', 'new_m_w_in': 'new_m', 'new_m_gla_wg_f': 'new_m', 'new_m_gla_bg_f': 'new_m', 'new_m_gla_wg_b': 'new_m', 'new_m_gla_bg_b': 'new_m', 'new_m_gla_norm_g': 'new_m', 'new_m_mla_q_norm_g': 'new_m', 'new_m_mla_kv_norm_g': 'new_m', 'new_m_mla_w_uq': 'new_m', 'new_m_mla_w_ukv': 'new_m', 'new_m_mla_out_g': 'new_m', 'new_m_conv_w': 'new_m', 'new_m_conv_out_g': 'new_m', 'new_m_w_out': 'new_m', 'new_m_final_g': 'new_m', 'new_v_ada_w': 'new_v', 'new_v_ada_b': 'new_v', 'new_v_norm_g': 'new_v', 'new_v_w_in': 'new_v', 'new_v_gla_wg_f': 'new_v', 'new_v_gla_bg_f': 'new_v', 'new_v_gla_wg_b': 'new_v', 'new_v_gla_bg_b': 'new_v', 'new_v_gla_norm_g': 'new_v', 'new_v_mla_q_norm_g': 'new_v', 'new_v_mla_kv_norm_g': 'new_v', 'new_v_mla_w_uq': 'new_v', 'new_v_mla_w_ukv': 'new_v', 'new_v_mla_out_g': 'new_v', 'new_v_conv_w': 'new_v', 'new_v_conv_out_g': 'new_v', 'new_v_w_out': 'new_v', 'new_v_final_g': 'new_v'}


def _forward(args):
    return _fwd_reference(*[args[k] for k in FWD_PARAMS])


def _output_shape():
    out = _jax.eval_shape(lambda: _forward(_fwd_setup_inputs(0)))
    return out.shape, out.dtype

N_MICROBATCH = 1
ADAM_LR = 0.001
ADAM_B1 = 0.9
ADAM_B2 = 0.999
ADAM_EPS = 1e-08
ADAM_WD = 0.01
ADAM_STEP = 10
PER_EXAMPLE_BATCH_AXIS = {'x': 0, 'c': 0, 'positions': 0, 'loss_target': 0}
SHARED_INPUTS = []
_WEIGHT_DTYPES = {'ada_w': _jnp.float32, 'ada_b': _jnp.float32, 'norm_g': _jnp.float32, 'w_in': _jnp.float32, 'gla_wg_f': _jnp.float32, 'gla_bg_f': _jnp.float32, 'gla_wg_b': _jnp.float32, 'gla_bg_b': _jnp.float32, 'gla_norm_g': _jnp.float32, 'mla_q_norm_g': _jnp.float32, 'mla_kv_norm_g': _jnp.float32, 'mla_w_uq': _jnp.float32, 'mla_w_ukv': _jnp.float32, 'mla_out_g': _jnp.float32, 'conv_w': _jnp.float32, 'conv_out_g': _jnp.float32, 'w_out': _jnp.float32, 'final_g': _jnp.float32}
MOMENT_SCALE = {'ada_w': 3.361722e-02, 'ada_b': 6.233595e-02, 'norm_g': 2.209763e-02, 'w_in': 1.448315e-02, 'gla_wg_f': 2.622546e-03, 'gla_bg_f': 6.333921e-03, 'gla_wg_b': 2.501815e-03, 'gla_bg_b': 5.920201e-03, 'gla_norm_g': 3.122282e-02, 'mla_q_norm_g': 7.178550e-03, 'mla_kv_norm_g': 2.534326e-02, 'mla_w_uq': 3.976659e-03, 'mla_w_ukv': 9.797466e-03, 'mla_out_g': 1.373787e-02, 'conv_w': 1.385566e-02, 'conv_out_g': 1.470160e-02, 'w_out': 1.327713e-02, 'final_g': 8.081636e+00}


def _to_microbatches(a, axis):
    t = _jnp.moveaxis(a, axis, 0)
    t = t.reshape((N_MICROBATCH, t.shape[0] // N_MICROBATCH) + t.shape[1:])
    return _jnp.moveaxis(t, 1, axis + 1)


def setup_inputs(seed: int = 0) -> dict:
    inp = _fwd_setup_inputs(seed)
    key = _jax.random.fold_in(_jax.random.key(seed), 7919)
    shape, _ = _output_shape()
    out = dict(inp)
    out["loss_target"] = _jax.random.normal(_jax.random.fold_in(key, 0), shape, _jnp.float32)
    for i, name in enumerate(TWIN_WEIGHTS):
        w = inp[name].astype(_jnp.float32)
        if MOMENT_SCALE is None:
            s = _jnp.sqrt(_jnp.mean(_jnp.square(w)) + 1e-30)
        else:
            s = MOMENT_SCALE[name]
        km, kv = _jax.random.split(_jax.random.fold_in(key, i + 1))
        out[name] = w
        out["m_" + name] = s * _jax.random.normal(km, w.shape, _jnp.float32)
        out["v_" + name] = (s * s) * _jax.random.uniform(kv, w.shape, _jnp.float32, 0.5, 1.5)
    if N_MICROBATCH > 1:
        for name, axis in PER_EXAMPLE_BATCH_AXIS.items():
            out[name] = _to_microbatches(out[name], axis)
    return {'x': out['x'], 'c': out['c'], 'positions': out['positions'], 'ada_w': out['ada_w'], 'ada_b': out['ada_b'], 'norm_g': out['norm_g'], 'w_in': out['w_in'], 'gla_wg_f': out['gla_wg_f'], 'gla_bg_f': out['gla_bg_f'], 'gla_wg_b': out['gla_wg_b'], 'gla_bg_b': out['gla_bg_b'], 'gla_norm_g': out['gla_norm_g'], 'mla_q_norm_g': out['mla_q_norm_g'], 'mla_kv_norm_g': out['mla_kv_norm_g'], 'mla_w_uq': out['mla_w_uq'], 'mla_w_ukv': out['mla_w_ukv'], 'mla_out_g': out['mla_out_g'], 'conv_w': out['conv_w'], 'conv_out_g': out['conv_out_g'], 'w_out': out['w_out'], 'final_g': out['final_g'], 'loss_target': out['loss_target'], 'm_ada_w': out['m_ada_w'], 'm_ada_b': out['m_ada_b'], 'm_norm_g': out['m_norm_g'], 'm_w_in': out['m_w_in'], 'm_gla_wg_f': out['m_gla_wg_f'], 'm_gla_bg_f': out['m_gla_bg_f'], 'm_gla_wg_b': out['m_gla_wg_b'], 'm_gla_bg_b': out['m_gla_bg_b'], 'm_gla_norm_g': out['m_gla_norm_g'], 'm_mla_q_norm_g': out['m_mla_q_norm_g'], 'm_mla_kv_norm_g': out['m_mla_kv_norm_g'], 'm_mla_w_uq': out['m_mla_w_uq'], 'm_mla_w_ukv': out['m_mla_w_ukv'], 'm_mla_out_g': out['m_mla_out_g'], 'm_conv_w': out['m_conv_w'], 'm_conv_out_g': out['m_conv_out_g'], 'm_w_out': out['m_w_out'], 'm_final_g': out['m_final_g'], 'v_ada_w': out['v_ada_w'], 'v_ada_b': out['v_ada_b'], 'v_norm_g': out['v_norm_g'], 'v_w_in': out['v_w_in'], 'v_gla_wg_f': out['v_gla_wg_f'], 'v_gla_bg_f': out['v_gla_bg_f'], 'v_gla_wg_b': out['v_gla_wg_b'], 'v_gla_bg_b': out['v_gla_bg_b'], 'v_gla_norm_g': out['v_gla_norm_g'], 'v_mla_q_norm_g': out['v_mla_q_norm_g'], 'v_mla_kv_norm_g': out['v_mla_kv_norm_g'], 'v_mla_w_uq': out['v_mla_w_uq'], 'v_mla_w_ukv': out['v_mla_w_ukv'], 'v_mla_out_g': out['v_mla_out_g'], 'v_conv_w': out['v_conv_w'], 'v_conv_out_g': out['v_conv_out_g'], 'v_w_out': out['v_w_out'], 'v_final_g': out['v_final_g']}


def _loss(weights, diff, rest, loss_target):
    with _jax.named_scope("forward"):
        args = {**rest, TWIN_DIFF_INPUT: diff, **{k: w.astype(_WEIGHT_DTYPES[k]) for k, w in weights.items()}}
        y = _forward(args)
    with _jax.named_scope("loss_head"):
        err = _jnp.square(y.astype(_jnp.float32) - loss_target)
        return 0.5 * _jnp.sum(_jnp.mean(err, axis=-1)) if err.ndim else 0.5 * err


def _adamw(w, g, m, v):
    m = ADAM_B1 * m + (1.0 - ADAM_B1) * g
    v = ADAM_B2 * v + (1.0 - ADAM_B2) * _jnp.square(g)
    m_hat = m / (1.0 - ADAM_B1 ** ADAM_STEP)
    v_hat = v / (1.0 - ADAM_B2 ** ADAM_STEP)
    delta = -ADAM_LR * (m_hat / (_jnp.sqrt(v_hat) + ADAM_EPS) + ADAM_WD * w)
    return delta, m, v


def reference(x, c, positions, ada_w, ada_b, norm_g, w_in, gla_wg_f, gla_bg_f, gla_wg_b, gla_bg_b, gla_norm_g, mla_q_norm_g, mla_kv_norm_g, mla_w_uq, mla_w_ukv, mla_out_g, conv_w, conv_out_g, w_out, final_g, loss_target, m_ada_w, m_ada_b, m_norm_g, m_w_in, m_gla_wg_f, m_gla_bg_f, m_gla_wg_b, m_gla_bg_b, m_gla_norm_g, m_mla_q_norm_g, m_mla_kv_norm_g, m_mla_w_uq, m_mla_w_ukv, m_mla_out_g, m_conv_w, m_conv_out_g, m_w_out, m_final_g, v_ada_w, v_ada_b, v_norm_g, v_w_in, v_gla_wg_f, v_gla_bg_f, v_gla_wg_b, v_gla_bg_b, v_gla_norm_g, v_mla_q_norm_g, v_mla_kv_norm_g, v_mla_w_uq, v_mla_w_ukv, v_mla_out_g, v_conv_w, v_conv_out_g, v_w_out, v_final_g):
    given = dict(x=x, c=c, positions=positions, ada_w=ada_w, ada_b=ada_b, norm_g=norm_g, w_in=w_in, gla_wg_f=gla_wg_f, gla_bg_f=gla_bg_f, gla_wg_b=gla_wg_b, gla_bg_b=gla_bg_b, gla_norm_g=gla_norm_g, mla_q_norm_g=mla_q_norm_g, mla_kv_norm_g=mla_kv_norm_g, mla_w_uq=mla_w_uq, mla_w_ukv=mla_w_ukv, mla_out_g=mla_out_g, conv_w=conv_w, conv_out_g=conv_out_g, w_out=w_out, final_g=final_g, loss_target=loss_target, m_ada_w=m_ada_w, m_ada_b=m_ada_b, m_norm_g=m_norm_g, m_w_in=m_w_in, m_gla_wg_f=m_gla_wg_f, m_gla_bg_f=m_gla_bg_f, m_gla_wg_b=m_gla_wg_b, m_gla_bg_b=m_gla_bg_b, m_gla_norm_g=m_gla_norm_g, m_mla_q_norm_g=m_mla_q_norm_g, m_mla_kv_norm_g=m_mla_kv_norm_g, m_mla_w_uq=m_mla_w_uq, m_mla_w_ukv=m_mla_w_ukv, m_mla_out_g=m_mla_out_g, m_conv_w=m_conv_w, m_conv_out_g=m_conv_out_g, m_w_out=m_w_out, m_final_g=m_final_g, v_ada_w=v_ada_w, v_ada_b=v_ada_b, v_norm_g=v_norm_g, v_w_in=v_w_in, v_gla_wg_f=v_gla_wg_f, v_gla_bg_f=v_gla_bg_f, v_gla_wg_b=v_gla_wg_b, v_gla_bg_b=v_gla_bg_b, v_gla_norm_g=v_gla_norm_g, v_mla_q_norm_g=v_mla_q_norm_g, v_mla_kv_norm_g=v_mla_kv_norm_g, v_mla_w_uq=v_mla_w_uq, v_mla_w_ukv=v_mla_w_ukv, v_mla_out_g=v_mla_out_g, v_conv_w=v_conv_w, v_conv_out_g=v_conv_out_g, v_w_out=v_w_out, v_final_g=v_final_g)
    weights = {n: given[n] for n in TWIN_WEIGHTS}
    shared = {n: given[n] for n in SHARED_INPUTS}
    per_example = {n: given[n] for n in ['x', 'c', 'positions']}
    grad_fn = _jax.value_and_grad(_loss, argnums=(0, 1))

    def one_microbatch(ex, loss_target):
        ex = dict(ex)
        diff = ex.pop(TWIN_DIFF_INPUT)
        return grad_fn(weights, diff, {**shared, **ex}, loss_target)

    if N_MICROBATCH == 1:
        loss, (grad_w, grad_x) = one_microbatch(per_example, given["loss_target"])
    else:
        def body(carry, xs):
            loss_sum, grad_sum = carry
            l_k, (gw_k, gx_k) = one_microbatch(xs[0], xs[1])
            with _jax.named_scope("update"):
                return (loss_sum + l_k, _jax.tree.map(_jnp.add, grad_sum, gw_k)), gx_k

        init = (_jnp.zeros((), _jnp.float32), _jax.tree.map(_jnp.zeros_like, weights))
        (loss, grad_w), grad_x = _jax.lax.scan(body, init, (per_example, given["loss_target"]))
    with _jax.named_scope("update"):
        delta_w, new_m, new_v = {}, {}, {}
        for n in TWIN_WEIGHTS:
            delta_w[n], new_m[n], new_v[n] = _adamw(weights[n], grad_w[n], given["m_" + n], given["v_" + n])
    return (loss, grad_x, *[grad_w[n] for n in TWIN_WEIGHTS], *[delta_w[n] for n in TWIN_WEIGHTS],
            *[new_m[n] for n in TWIN_WEIGHTS], *[new_v[n] for n in TWIN_WEIGHTS])
```

```python
import functools

import jax
import jax.numpy as jnp
from jax import lax
from jax.experimental import pallas as pl
from jax.experimental.pallas import tpu as pltpu

F32 = jnp.float32
MXU = jnp.bfloat16
HI = lax.Precision.HIGHEST
N_DEV = 8
MESH = pl.DeviceIdType.MESH

D_MIX = 2048
GH, GDK, GDV = 6, 64, 128
GW = GH * GDV
GQK = GH * GDK
GRANK = 16
GTEMP = 16.0
CHUNK = 64
MH, MQL, MKVL, MNOPE, MROPE, MDV = 6, 384, 256, 128, 64, 128
MW = MH * MDV
MQW = MH * (MNOPE + MROPE)
MKVW = MH * (MNOPE + MDV)
CONV_CH = 512
ROPE_THETA = 10000.0
EPS = 1e-6
IN_DIM = 5856
OZ, OCB, OCC, OCX, OMKV, OGV, OGQ, OGK, OMQ, OT = 0, 2048, 2560, 3072, 3584, 3840, 4608, 4992, 5376, 5760
PW = 5888
LANES = 128
VMEM_LIMIT = 56 * 1024 * 1024

ADAM_LR, ADAM_B1, ADAM_B2, ADAM_EPS, ADAM_WD, ADAM_STEP = 0.001, 0.9, 0.999, 1e-08, 0.01, 10


def _cp(sem=None):
    return pltpu.CompilerParams(dimension_semantics=sem, vmem_limit_bytes=VMEM_LIMIT)


def _dot(a, b):
    return jnp.dot(a.astype(MXU), b.astype(MXU), preferred_element_type=F32)


def _dot_nt(a, b):
    return lax.dot_general(a.astype(MXU), b.astype(MXU), (((1,), (1,)), ((), ())), preferred_element_type=F32)


def _dot_tn(a, b):
    return lax.dot_general(a.astype(MXU), b.astype(MXU), (((0,), (0,)), ((), ())), preferred_element_type=F32)


def _dotf(a, b):
    return jnp.dot(a, b, precision=HI, preferred_element_type=F32)


def _dotf_nt(a, b):
    return lax.dot_general(a, b, (((1,), (1,)), ((), ())), precision=HI, preferred_element_type=F32)


def _dotf_tn(a, b):
    return lax.dot_general(a, b, (((0,), (0,)), ((), ())), precision=HI, preferred_element_type=F32)


def _rows(s):
    return min(256, s)


def _rms(x, g):
    r = lax.rsqrt(jnp.mean(x * x, axis=-1, keepdims=True) + EPS)
    return x * r * g


def _rms_bwd(dy, x, g):
    r = lax.rsqrt(jnp.mean(x * x, axis=-1, keepdims=True) + EPS)
    xh = x * r
    dxh = dy * g
    dg = jnp.sum(dy * xh, axis=0, keepdims=True)
    dx = r * (dxh - xh * jnp.mean(dxh * xh, axis=-1, keepdims=True))
    return dx, dg


def _sigmoid(z):
    return 1.0 / (1.0 + jnp.exp(-z))


def _matmul(a, b, *, dims, tm, tn, tk, out_dtypes, name, epilogue=None, extras=(), extra_kinds=()):
    if dims == "nn":
        (m, k), n, mul = a.shape, b.shape[1], _dot
    elif dims == "nt":
        (m, k), n, mul = a.shape, b.shape[0], _dot_nt
    else:
        (k, m), n, mul = a.shape, b.shape[1], _dot_tn
    tm, tn, tk = min(tm, m), min(tn, n), min(tk, k)
    assert m % tm == 0 and n % tn == 0 and k % tk == 0, (m, n, k, tm, tn, tk)
    if dims == "nn":
        a_spec = pl.BlockSpec((tm, tk), lambda i, j, kk: (i, kk))
        b_spec = pl.BlockSpec((tk, tn), lambda i, j, kk: (kk, j))
    elif dims == "nt":
        a_spec = pl.BlockSpec((tm, tk), lambda i, j, kk: (i, kk))
        b_spec = pl.BlockSpec((tn, tk), lambda i, j, kk: (j, kk))
    else:
        a_spec = pl.BlockSpec((tk, tm), lambda i, j, kk: (kk, i))
        b_spec = pl.BlockSpec((tk, tn), lambda i, j, kk: (kk, j))
    nk = k // tk
    n_extra = len(extras)
    n_out = len(out_dtypes)
    extra_specs = []
    for kind in extra_kinds:
        if kind == "mn":
            extra_specs.append(pl.BlockSpec((tm, tn), lambda i, j, kk: (i, j)))
        else:
            extra_specs.append(pl.BlockSpec((1, tn), lambda i, j, kk: (0, j)))

    def body(*refs):
        a_ref, b_ref = refs[0], refs[1]
        ex = refs[2:2 + n_extra]
        outs = refs[2 + n_extra:2 + n_extra + n_out]
        acc = refs[-1]
        kk = pl.program_id(2)

        @pl.when(kk == 0)
        def _():
            acc[...] = jnp.zeros_like(acc)

        acc[...] += mul(a_ref[...], b_ref[...])

        @pl.when(kk == nk - 1)
        def _():
            res = acc[...]
            vals = (res,) if epilogue is None else epilogue(res, *[e[...] for e in ex])
            for o, v in zip(outs, vals):
                o[...] = v.astype(o.dtype)

    out_spec = pl.BlockSpec((tm, tn), lambda i, j, kk: (i, j))
    res = pl.pallas_call(
        body, grid=(m // tm, n // tn, nk),
        in_specs=[a_spec, b_spec] + extra_specs,
        out_specs=[out_spec] * n_out,
        out_shape=[jax.ShapeDtypeStruct((m, n), dt) for dt in out_dtypes],
        scratch_shapes=[pltpu.VMEM((tm, tn), F32)],
        name=name, compiler_params=_cp(("parallel", "parallel", "arbitrary")),
    )(a, b, *extras)
    return res


def _norm_mod(x, g, scale, shift, name):
    s, d = x.shape
    tr = _rows(s)

    def body(x_ref, g_ref, sc_ref, sh_ref, h_ref):
        h = _rms(x_ref[...], g_ref[...]) * (1.0 + sc_ref[...]) + sh_ref[...]
        h_ref[...] = h.astype(h_ref.dtype)

    row = pl.BlockSpec((tr, d), lambda i: (i, 0))
    vec = pl.BlockSpec((1, d), lambda i: (0, 0))
    return pl.pallas_call(body, grid=(s // tr,), in_specs=[row, vec, vec, vec], out_specs=row,
                          out_shape=jax.ShapeDtypeStruct((s, d), MXU), name=name,
                          compiler_params=_cp(("parallel",)))(x, g, scale, shift)


def _norm_mod_bwd(d_h, x, d_out, g, scale, name):
    s, d = x.shape
    tr = _rows(s)

    def body(dh_ref, x_ref, do_ref, g_ref, sc_ref, dx_ref, dsh_ref, dsc_ref, dg_ref):
        i = pl.program_id(0)

        @pl.when(i == 0)
        def _():
            dsh_ref[...] = jnp.zeros_like(dsh_ref)
            dsc_ref[...] = jnp.zeros_like(dsc_ref)
            dg_ref[...] = jnp.zeros_like(dg_ref)

        dh = dh_ref[...]
        xv = x_ref[...]
        gv = g_ref[...]
        r = lax.rsqrt(jnp.mean(xv * xv, axis=-1, keepdims=True) + EPS)
        xh = xv * r
        dsh_ref[...] += jnp.sum(dh, axis=0, keepdims=True)
        dsc_ref[...] += jnp.sum(dh * (xh * gv), axis=0, keepdims=True)
        dhn = dh * (1.0 + sc_ref[...])
        dg_ref[...] += jnp.sum(dhn * xh, axis=0, keepdims=True)
        dxh = dhn * gv
        dx_ref[...] = do_ref[...] + r * (dxh - xh * jnp.mean(dxh * xh, axis=-1, keepdims=True))

    row = pl.BlockSpec((tr, d), lambda i: (i, 0))
    vec = pl.BlockSpec((1, d), lambda i: (0, 0))
    vshape = jax.ShapeDtypeStruct((1, d), F32)
    return pl.pallas_call(body, grid=(s // tr,), in_specs=[row, row, row, vec, vec],
                          out_specs=[row, vec, vec, vec],
                          out_shape=[jax.ShapeDtypeStruct((s, d), F32), vshape, vshape, vshape],
                          name=name, compiler_params=_cp(("arbitrary",)))(d_h, x, d_out, g, scale)


def _gate_bwd(d_out, u, gate, name):
    s, d = d_out.shape
    tr = _rows(s)

    def body(do_ref, u_ref, gt_ref, du_ref, dgt_ref):
        @pl.when(pl.program_id(0) == 0)
        def _():
            dgt_ref[...] = jnp.zeros_like(dgt_ref)

        do = do_ref[...]
        du_ref[...] = (do * gt_ref[...]).astype(du_ref.dtype)
        dgt_ref[...] += jnp.sum(do * u_ref[...], axis=0, keepdims=True)

    row = pl.BlockSpec((tr, d), lambda i: (i, 0))
    vec = pl.BlockSpec((1, d), lambda i: (0, 0))
    return pl.pallas_call(body, grid=(s // tr,), in_specs=[row, row, vec], out_specs=[row, vec],
                          out_shape=[jax.ShapeDtypeStruct((s, d), MXU), jax.ShapeDtypeStruct((1, d), F32)],
                          name=name, compiler_params=_cp(("arbitrary",)))(d_out, u, gate)


def _final_loss(x, g, target, name):
    s, d = x.shape
    tr = _rows(s)

    def body(x_ref, g_ref, t_ref, loss_ref, dx_ref, dg_ref):
        @pl.when(pl.program_id(0) == 0)
        def _():
            loss_ref[...] = jnp.zeros_like(loss_ref)
            dg_ref[...] = jnp.zeros_like(dg_ref)

        xv = x_ref[...]
        gv = g_ref[...]
        diff = _rms(xv, gv) - t_ref[...]
        part = 0.5 * jnp.sum(jnp.sum(diff * diff, axis=-1, keepdims=True) / d, axis=0, keepdims=True)
        loss_ref[...] += jnp.broadcast_to(part, loss_ref.shape)
        dx, dg = _rms_bwd(diff / d, xv, gv)
        dx_ref[...] = dx
        dg_ref[...] += dg

    row = pl.BlockSpec((tr, d), lambda i: (i, 0))
    vec = pl.BlockSpec((1, d), lambda i: (0, 0))
    lvec = pl.BlockSpec((1, LANES), lambda i: (0, 0))
    return pl.pallas_call(body, grid=(s // tr,), in_specs=[row, vec, row], out_specs=[lvec, row, vec],
                          out_shape=[jax.ShapeDtypeStruct((1, LANES), F32), jax.ShapeDtypeStruct((s, d), F32),
                                     jax.ShapeDtypeStruct((1, d), F32)],
                          name=name, compiler_params=_cp(("arbitrary",)))(x, g, target)


def _shift_rows(u, s, down):
    ri = lax.broadcasted_iota(jnp.int32, u.shape, 0)
    if down:
        return jnp.where(ri == 0, 0.0, pltpu.roll(u, 1, 0))
    return jnp.where(ri == s - 1, 0.0, pltpu.roll(u, s - 1, 0))


def _conv_fwd(proj, conv_w, name):
    s = proj.shape[0]
    nt = CONV_CH // LANES

    def body(cb_ref, cc_ref, cx_ref, w_ref, pre_ref):
        u = cc_ref[...] * cx_ref[...]
        conv = _shift_rows(u, s, True) * w_ref[0:1, :] + u * w_ref[1:2, :] + _shift_rows(u, s, False) * w_ref[2:3, :]
        pre_ref[...] = cb_ref[...] * conv

    def col(off):
        return pl.BlockSpec((s, LANES), lambda j: (0, off // LANES + j))

    return pl.pallas_call(body, grid=(nt,), in_specs=[col(OCB), col(OCC), col(OCX), pl.BlockSpec((3, LANES), lambda j: (0, j))],
                          out_specs=pl.BlockSpec((s, LANES), lambda j: (0, j)),
                          out_shape=jax.ShapeDtypeStruct((s, CONV_CH), F32), name=name,
                          compiler_params=_cp(("parallel",)))(proj, proj, proj, conv_w)


def _conv_bwd(proj, conv_w, d_pre, name):
    s = proj.shape[0]
    nt = CONV_CH // LANES

    def body(cb_ref, cc_ref, cx_ref, w_ref, dp_ref, dcb_ref, dcc_ref, dcx_ref, dw_ref):
        cc, cx = cc_ref[...], cx_ref[...]
        u = cc * cx
        up, dn = _shift_rows(u, s, True), _shift_rows(u, s, False)
        w0, w1, w2 = w_ref[0:1, :], w_ref[1:2, :], w_ref[2:3, :]
        conv = up * w0 + u * w1 + dn * w2
        dp = dp_ref[...]
        dcb_ref[...] = dp * conv
        dconv = dp * cb_ref[...]
        du = _shift_rows(dconv, s, False) * w0 + dconv * w1 + _shift_rows(dconv, s, True) * w2
        dcc_ref[...] = du * cx
        dcx_ref[...] = du * cc
        dw_ref[0:1, :] = jnp.sum(dconv * up, axis=0, keepdims=True)
        dw_ref[1:2, :] = jnp.sum(dconv * u, axis=0, keepdims=True)
        dw_ref[2:3, :] = jnp.sum(dconv * dn, axis=0, keepdims=True)

    def col(off):
        return pl.BlockSpec((s, LANES), lambda j: (0, off // LANES + j))

    blk = pl.BlockSpec((s, LANES), lambda j: (0, j))
    wblk = pl.BlockSpec((3, LANES), lambda j: (0, j))
    full = jax.ShapeDtypeStruct((s, CONV_CH), F32)
    return pl.pallas_call(body, grid=(nt,), in_specs=[col(OCB), col(OCC), col(OCX), wblk, blk],
                          out_specs=[blk, blk, blk, wblk],
                          out_shape=[full, full, full, jax.ShapeDtypeStruct((3, CONV_CH), F32)],
                          name=name, compiler_params=_cp(("parallel",)))(proj, proj, proj, conv_w, d_pre)


def _gla_gates(q_ref, k_ref, t_ref, wg_ref, bg_ref, reverse):
    q = q_ref[...] * (GDK ** -0.5)
    k = k_ref[...]
    t = t_ref[...]
    a = _dotf(t, wg_ref[...]) + bg_ref[...]
    la = (jnp.minimum(a, 0.0) - jnp.log(1.0 + jnp.exp(-jnp.abs(a)))) / GTEMP
    ri = lax.broadcasted_iota(jnp.int32, (CHUNK, CHUNK), 0)
    ci = lax.broadcasted_iota(jnp.int32, (CHUNK, CHUNK), 1)
    if reverse:
        cum, mask = ci >= ri, ci > ri
    else:
        cum, mask = ci <= ri, ci <= ri
    cumf = cum.astype(F32)
    b = _dotf(cumf, la)
    bl = jnp.sum(la, axis=0, keepdims=True)
    return q, k, t, a, cumf, mask, b, bl


def _gla_specs(s, reverse):
    nc = s // CHUNK

    def row(n):
        return nc - 1 - n if reverse else n

    return nc, row


def _gla_fwd(proj, wg_pad, bg, reverse, name):
    s = proj.shape[0]
    nc, row = _gla_specs(s, reverse)

    def body(q_ref, k_ref, v_ref, t_ref, wg_ref, bg_ref, o_ref, st_ref, state):
        @pl.when(pl.program_id(0) == 0)
        def _():
            state[...] = jnp.zeros_like(state)

        q, k, _, _, _, mask, b, bl = _gla_gates(q_ref, k_ref, t_ref, wg_ref, bg_ref, reverse)
        qd = q * jnp.exp(b)
        ki = k * jnp.exp(-b)
        kte = k * jnp.exp(bl - b)
        decay = jnp.exp(bl)
        lane = lax.broadcasted_iota(jnp.int32, (CHUNK, LANES), 1)
        for h in range(GH):
            p = h // 2
            sl = slice(p * LANES, (p + 1) * LANES)
            lm = (lane < GDK) if h % 2 == 0 else (lane >= GDK)
            qd_h = jnp.where(lm, qd[:, sl], 0.0)
            kte_h = jnp.where(lm, kte[:, sl], 0.0)
            v_h = v_ref[:, h * GDV:(h + 1) * GDV]
            st = state[h]
            a_mat = jnp.where(mask, _dot_nt(qd_h, ki[:, sl]), 0.0)
            o_ref[:, h * GDV:(h + 1) * GDV] = _dot(a_mat, v_h) + _dot_nt(qd_h, st)
            st_ref[0, h] = st
            state[h] = st * decay[:, sl] + _dot_tn(v_h, kte_h)

    return pl.pallas_call(
        body, grid=(nc,),
        in_specs=[pl.BlockSpec((CHUNK, GQK), lambda n: (row(n), OGQ // GQK)),
                  pl.BlockSpec((CHUNK, GQK), lambda n: (row(n), OGK // GQK)),
                  pl.BlockSpec((CHUNK, GW), lambda n: (row(n), OGV // GW)),
                  pl.BlockSpec((CHUNK, LANES), lambda n: (row(n), OT // LANES)),
                  pl.BlockSpec((LANES, GQK), lambda n: (0, 0)),
                  pl.BlockSpec((1, GQK), lambda n: (0, 0))],
        out_specs=[pl.BlockSpec((CHUNK, GW), lambda n: (row(n), 0)),
                   pl.BlockSpec((1, GH, GDV, LANES), lambda n: (n, 0, 0, 0))],
        out_shape=[jax.ShapeDtypeStruct((s, GW), F32), jax.ShapeDtypeStruct((nc, GH, GDV, LANES), F32)],
        scratch_shapes=[pltpu.VMEM((GH, GDV, LANES), F32)],
        name=name, compiler_params=_cp(("arbitrary",)))(proj, proj, proj, proj, wg_pad, bg)


def _gla_bwd(proj, wg_pad, bg, states, d_o, reverse, name):
    s = proj.shape[0]
    nc, row = _gla_specs(s, reverse)

    def body(q_ref, k_ref, v_ref, t_ref, wg_ref, bg_ref, st_ref, do_ref,
             dq_ref, dk_ref, dv_ref, dt_ref, dwg_ref, dbg_ref, dstate):
        @pl.when(pl.program_id(0) == 0)
        def _():
            dstate[...] = jnp.zeros_like(dstate)
            dwg_ref[...] = jnp.zeros_like(dwg_ref)
            dbg_ref[...] = jnp.zeros_like(dbg_ref)

        q, k, t, a, cumf, mask, b, bl = _gla_gates(q_ref, k_ref, t_ref, wg_ref, bg_ref, reverse)
        e, ei, ee = jnp.exp(b), jnp.exp(-b), jnp.exp(bl - b)
        qd, ki, kte = q * e, k * ei, k * ee
        decay = jnp.exp(bl)
        lane = lax.broadcasted_iota(jnp.int32, (CHUNK, LANES), 1)
        da_parts = []
        for p in range(GH // 2):
            sl = slice(p * LANES, (p + 1) * LANES)
            dqd = jnp.zeros((CHUNK, LANES), F32)
            dki = jnp.zeros((CHUNK, LANES), F32)
            dkte = jnp.zeros((CHUNK, LANES), F32)
            ddecay = jnp.zeros((1, LANES), F32)
            for half in range(2):
                h = 2 * p + half
                lm = (lane < GDK) if half == 0 else (lane >= GDK)
                qd_h = jnp.where(lm, qd[:, sl], 0.0)
                ki_h = jnp.where(lm, ki[:, sl], 0.0)
                kte_h = jnp.where(lm, kte[:, sl], 0.0)
                v_h = v_ref[:, h * GDV:(h + 1) * GDV]
                do_h = do_ref[:, h * GDV:(h + 1) * GDV]
                st = st_ref[0, h]
                dst = dstate[h]
                a_mat = jnp.where(mask, _dot_nt(qd_h, ki_h), 0.0)
                da_mat = jnp.where(mask, _dot_nt(do_h, v_h), 0.0)
                dv_ref[:, h * GDV:(h + 1) * GDV] = _dot_tn(a_mat, do_h) + _dot_nt(kte_h, dst)
                dqd += _dot(da_mat, ki_h) + _dot(do_h, st)
                dki += _dot_tn(da_mat, qd_h)
                dkte += _dot(v_h, dst)
                ddecay += jnp.sum(dst * st, axis=0, keepdims=True)
                dstate[h] = dst * decay[:, sl] + _dot_tn(do_h, qd_h)
            dq_ref[:, sl] = dqd * e[:, sl] * (GDK ** -0.5)
            dk_ref[:, sl] = dki * ei[:, sl] + dkte * ee[:, sl]
            db = dqd * qd[:, sl] - dki * ki[:, sl] - dkte * kte[:, sl]
            dbl = jnp.sum(dkte * kte[:, sl], axis=0, keepdims=True) + decay[:, sl] * ddecay
            dla = _dotf_tn(cumf, db) + dbl
            da_parts.append(dla * (1.0 / GTEMP) * _sigmoid(-a[:, sl]))
        da = jnp.concatenate(da_parts, axis=1)
        dt_ref[...] = _dotf_nt(da, wg_ref[...])
        dwg_ref[...] += _dotf_tn(t, da)
        dbg_ref[...] += jnp.sum(da, axis=0, keepdims=True)

    def prow(j):
        return row(nc - 1 - j)

    return pl.pallas_call(
        body, grid=(nc,),
        in_specs=[pl.BlockSpec((CHUNK, GQK), lambda j: (prow(j), OGQ // GQK)),
                  pl.BlockSpec((CHUNK, GQK), lambda j: (prow(j), OGK // GQK)),
                  pl.BlockSpec((CHUNK, GW), lambda j: (prow(j), OGV // GW)),
                  pl.BlockSpec((CHUNK, LANES), lambda j: (prow(j), OT // LANES)),
                  pl.BlockSpec((LANES, GQK), lambda j: (0, 0)),
                  pl.BlockSpec((1, GQK), lambda j: (0, 0)),
                  pl.BlockSpec((1, GH, GDV, LANES), lambda j: (nc - 1 - j, 0, 0, 0)),
                  pl.BlockSpec((CHUNK, GW), lambda j: (prow(j), 0))],
        out_specs=[pl.BlockSpec((CHUNK, GQK), lambda j: (prow(j), 0)),
                   pl.BlockSpec((CHUNK, GQK), lambda j: (prow(j), 0)),
                   pl.BlockSpec((CHUNK, GW), lambda j: (prow(j), 0)),
                   pl.BlockSpec((CHUNK, LANES), lambda j: (prow(j), 0)),
                   pl.BlockSpec((LANES, GQK), lambda j: (0, 0)),
                   pl.BlockSpec((1, GQK), lambda j: (0, 0))],
        out_shape=[jax.ShapeDtypeStruct((s, GQK), F32), jax.ShapeDtypeStruct((s, GQK), F32),
                   jax.ShapeDtypeStruct((s, GW), F32), jax.ShapeDtypeStruct((s, LANES), F32),
                   jax.ShapeDtypeStruct((LANES, GQK), F32), jax.ShapeDtypeStruct((1, GQK), F32)],
        scratch_shapes=[pltpu.VMEM((GH, GDV, LANES), F32)],
        name=name, compiler_params=_cp(("arbitrary",)))(proj, proj, proj, proj, wg_pad, bg, states, d_o)


def _rot_half(x):
    lane = lax.broadcasted_iota(jnp.int32, x.shape, 1)
    first = (lane % MROPE) < (MROPE // 2)
    return jnp.where(first, -pltpu.roll(x, LANES - MROPE // 2, 1), pltpu.roll(x, MROPE // 2, 1))


def _mla_prep(proj, cos, sin, qg, kvg, w_uq, w_ukv, name):
    s = proj.shape[0]
    tr = _rows(s)

    def body(mq_ref, mkv_ref, t_ref, cos_ref, sin_ref, qg_ref, kvg_ref, wuq_ref, wukv_ref, q_ref, k_ref, v_ref):
        cosv, sinv = cos_ref[...], sin_ref[...]
        lane = lax.broadcasted_iota(jnp.int32, (tr, LANES), 1)

        def rope(xv):
            return xv * cosv + _rot_half(xv) * sinv

        qm = _dot(_rms(mq_ref[...], qg_ref[...]), wuq_ref[...])
        kv = _dot(_rms(mkv_ref[...], kvg_ref[...]), wukv_ref[...])
        kr_lo = jnp.where(lane < MROPE, rope(t_ref[...]), 0.0)
        kr_hi = pltpu.roll(kr_lo, MROPE, 1)
        for p in range(MH // 2):
            r = rope(qm[:, MW + p * LANES:MW + (p + 1) * LANES]).astype(q_ref.dtype)
            q_ref[2 * p, :, LANES:] = r
            q_ref[2 * p + 1, :, LANES:] = r
        for h in range(MH):
            q_ref[h, :, :LANES] = qm[:, h * LANES:(h + 1) * LANES].astype(q_ref.dtype)
            k_ref[h, :, :LANES] = kv[:, 2 * h * LANES:(2 * h + 1) * LANES].astype(k_ref.dtype)
            k_ref[h, :, LANES:] = (kr_lo if h % 2 == 0 else kr_hi).astype(k_ref.dtype)
            v_ref[h] = kv[:, (2 * h + 1) * LANES:(2 * h + 2) * LANES].astype(v_ref.dtype)

    def full(shape):
        return pl.BlockSpec(shape, lambda i: (0,) * len(shape))

    return pl.pallas_call(
        body, grid=(s // tr,),
        in_specs=[pl.BlockSpec((tr, MQL), lambda i: (i, OMQ // MQL)),
                  pl.BlockSpec((tr, MKVL), lambda i: (i, OMKV // MKVL)),
                  pl.BlockSpec((tr, LANES), lambda i: (i, OT // LANES)),
                  pl.BlockSpec((tr, LANES), lambda i: (i, 0)),
                  pl.BlockSpec((tr, LANES), lambda i: (i, 0)),
                  full((1, MQL)), full((1, MKVL)), full((MQL, MQW)), full((MKVL, MKVW))],
        out_specs=[pl.BlockSpec((MH, tr, 2 * LANES), lambda i: (0, i, 0)),
                   pl.BlockSpec((MH, tr, 2 * LANES), lambda i: (0, i, 0)),
                   pl.BlockSpec((MH, tr, LANES), lambda i: (0, i, 0))],
        out_shape=[jax.ShapeDtypeStruct((MH, s, 2 * LANES), MXU), jax.ShapeDtypeStruct((MH, s, 2 * LANES), MXU),
                   jax.ShapeDtypeStruct((MH, s, LANES), MXU)],
        name=name, compiler_params=_cp(("parallel",)))(proj, proj, proj, cos, sin, qg, kvg, w_uq, w_ukv)


def _mla_prep_bwd(proj, cos, sin, qg, kvg, w_uq, w_ukv, d_q, d_k, d_v, name):
    s = proj.shape[0]
    tr = _rows(s)

    def body(mq_ref, mkv_ref, cos_ref, sin_ref, qg_ref, kvg_ref, wuq_ref, wukv_ref, dq_ref, dk_ref, dv_ref,
             dmq_ref, dmkv_ref, dt_ref, dwuq_ref, dwukv_ref, dqg_ref, dkvg_ref):
        @pl.when(pl.program_id(0) == 0)
        def _():
            for r in (dwuq_ref, dwukv_ref, dqg_ref, dkvg_ref):
                r[...] = jnp.zeros_like(r)

        cosv, sinv = cos_ref[...], sin_ref[...]
        lane = lax.broadcasted_iota(jnp.int32, (tr, LANES), 1)
        lo = lane < MROPE

        def unrope(dv):
            return dv * cosv - _rot_half(dv * sinv)

        parts = [dq_ref[h, :, :LANES] for h in range(MH)]
        for p in range(MH // 2):
            parts.append(unrope(jnp.where(lo, dq_ref[2 * p, :, LANES:], dq_ref[2 * p + 1, :, LANES:])))
        d_qm = jnp.concatenate(parts, axis=1)
        mq, qgv = mq_ref[...], qg_ref[...]
        cq = _rms(mq, qgv)
        dwuq_ref[...] += _dot_tn(cq, d_qm)
        dmq, dqg = _rms_bwd(_dot_nt(d_qm, wuq_ref[...]), mq, qgv)
        dmq_ref[...] = dmq
        dqg_ref[...] += dqg

        parts = []
        for h in range(MH):
            parts += [dk_ref[h, :, :LANES], dv_ref[h]]
        d_kv = jnp.concatenate(parts, axis=1)
        mkv, kvgv = mkv_ref[...], kvg_ref[...]
        ckv = _rms(mkv, kvgv)
        dwukv_ref[...] += _dot_tn(ckv, d_kv)
        dmkv, dkvg = _rms_bwd(_dot_nt(d_kv, wukv_ref[...]), mkv, kvgv)
        dmkv_ref[...] = dmkv
        dkvg_ref[...] += dkvg

        even = dk_ref[0, :, LANES:] + dk_ref[2, :, LANES:] + dk_ref[4, :, LANES:]
        odd = dk_ref[1, :, LANES:] + dk_ref[3, :, LANES:] + dk_ref[5, :, LANES:]
        d_kr = jnp.where(lo, even, 0.0) + pltpu.roll(jnp.where(lo, 0.0, odd), MROPE, 1)
        dt_ref[...] = jnp.where(lo, unrope(d_kr), 0.0)

    def full(shape):
        return pl.BlockSpec(shape, lambda i: (0,) * len(shape))

    return pl.pallas_call(
        body, grid=(s // tr,),
        in_specs=[pl.BlockSpec((tr, MQL), lambda i: (i, OMQ // MQL)),
                  pl.BlockSpec((tr, MKVL), lambda i: (i, OMKV // MKVL)),
                  pl.BlockSpec((tr, LANES), lambda i: (i, 0)),
                  pl.BlockSpec((tr, LANES), lambda i: (i, 0)),
                  full((1, MQL)), full((1, MKVL)), full((MQL, MQW)), full((MKVL, MKVW)),
                  pl.BlockSpec((MH, tr, 2 * LANES), lambda i: (0, i, 0)),
                  pl.BlockSpec((MH, tr, 2 * LANES), lambda i: (0, i, 0)),
                  pl.BlockSpec((MH, tr, LANES), lambda i: (0, i, 0))],
        out_specs=[pl.BlockSpec((tr, MQL), lambda i: (i, 0)), pl.BlockSpec((tr, MKVL), lambda i: (i, 0)),
                   pl.BlockSpec((tr, LANES), lambda i: (i, 0)),
                   full((MQL, MQW)), full((MKVL, MKVW)), full((1, MQL)), full((1, MKVL))],
        out_shape=[jax.ShapeDtypeStruct((s, MQL), F32), jax.ShapeDtypeStruct((s, MKVL), F32),
                   jax.ShapeDtypeStruct((s, LANES), F32),
                   jax.ShapeDtypeStruct((MQL, MQW), F32), jax.ShapeDtypeStruct((MKVL, MKVW), F32),
                   jax.ShapeDtypeStruct((1, MQL), F32), jax.ShapeDtypeStruct((1, MKVL), F32)],
        name=name, compiler_params=_cp(("arbitrary",)))(proj, proj, cos, sin, qg, kvg, w_uq, w_ukv, d_q, d_k, d_v)


ATT_SCALE = (MNOPE + MROPE) ** -0.5


def _attn_fwd(q, k, v, name):
    s = q.shape[1]
    tq = _rows(s)

    def body(q_ref, k_ref, v_ref, o_ref, lse_ref):
        sc = _dot_nt(q_ref[0], k_ref[0]) * ATT_SCALE
        m = jnp.max(sc, axis=-1, keepdims=True)
        p = jnp.exp(sc - m)
        l = jnp.sum(p, axis=-1, keepdims=True)
        o_ref[...] = _dot(p, v_ref[0]) / l
        lse_ref[0] = m + jnp.log(l)

    return pl.pallas_call(
        body, grid=(MH, s // tq),
        in_specs=[pl.BlockSpec((1, tq, 2 * LANES), lambda h, i: (h, i, 0)),
                  pl.BlockSpec((1, s, 2 * LANES), lambda h, i: (h, 0, 0)),
                  pl.BlockSpec((1, s, LANES), lambda h, i: (h, 0, 0))],
        out_specs=[pl.BlockSpec((tq, LANES), lambda h, i: (i, h)),
                   pl.BlockSpec((1, tq, 1), lambda h, i: (h, i, 0))],
        out_shape=[jax.ShapeDtypeStruct((s, MW), F32), jax.ShapeDtypeStruct((MH, s, 1), F32)],
        name=name, compiler_params=_cp(("parallel", "parallel")))(q, k, v)


def _attn_bwd(q, k, v, o, lse, d_o, name):
    s = q.shape[1]
    tq = _rows(s)

    def body(q_ref, k_ref, v_ref, o_ref, lse_ref, do_ref, dq_ref, dk_ref, dv_ref):
        @pl.when(pl.program_id(1) == 0)
        def _():
            dk_ref[...] = jnp.zeros_like(dk_ref)
            dv_ref[...] = jnp.zeros_like(dv_ref)

        qv, kv, do = q_ref[0], k_ref[0], do_ref[...]
        p = jnp.exp(_dot_nt(qv, kv) * ATT_SCALE - lse_ref[0])
        delta = jnp.sum(do * o_ref[...], axis=-1, keepdims=True)
        ds = p * (_dot_nt(do, v_ref[0]) - delta) * ATT_SCALE
        dq_ref[0] = _dot(ds, kv)
        dk_ref[0] += _dot_tn(ds, qv)
        dv_ref[0] += _dot_tn(p, do)

    return pl.pallas_call(
        body, grid=(MH, s // tq),
        in_specs=[pl.BlockSpec((1, tq, 2 * LANES), lambda h, i: (h, i, 0)),
                  pl.BlockSpec((1, s, 2 * LANES), lambda h, i: (h, 0, 0)),
                  pl.BlockSpec((1, s, LANES), lambda h, i: (h, 0, 0)),
                  pl.BlockSpec((tq, LANES), lambda h, i: (i, h)),
                  pl.BlockSpec((1, tq, 1), lambda h, i: (h, i, 0)),
                  pl.BlockSpec((tq, LANES), lambda h, i: (i, h))],
        out_specs=[pl.BlockSpec((1, tq, 2 * LANES), lambda h, i: (h, i, 0)),
                   pl.BlockSpec((1, s, 2 * LANES), lambda h, i: (h, 0, 0)),
                   pl.BlockSpec((1, s, LANES), lambda h, i: (h, 0, 0))],
        out_shape=[jax.ShapeDtypeStruct((MH, s, 2 * LANES), F32), jax.ShapeDtypeStruct((MH, s, 2 * LANES), F32),
                   jax.ShapeDtypeStruct((MH, s, LANES), F32)],
        name=name, compiler_params=_cp(("parallel", "arbitrary")))(q, k, v, o, lse, d_o)


def _merge_fwd(o_f, o_b, o_att, pre, proj, gng, mog, cog, name):
    s = proj.shape[0]
    tr = _rows(s)

    def body(of_ref, ob_ref, oa_ref, pre_ref, z_ref, gng_ref, mog_ref, cog_ref, y_ref):
        z = z_ref[...]
        sz = z * _sigmoid(z)
        osum = of_ref[...] + ob_ref[...]
        gg = gng_ref[...]
        for h in range(GH):
            sl = slice(h * GDV, (h + 1) * GDV)
            y_ref[:, sl] = (_rms(osum[:, sl], gg) * sz[:, sl]).astype(y_ref.dtype)
        y_ref[:, GW:GW + MW] = (_rms(oa_ref[...], mog_ref[...]) * sz[:, GW:GW + MW]).astype(y_ref.dtype)
        y_ref[:, GW + MW:] = (_rms(pre_ref[...], cog_ref[...]) * sz[:, GW + MW:]).astype(y_ref.dtype)

    def row(w):
        return pl.BlockSpec((tr, w), lambda i: (i, 0))

    def vec(w):
        return pl.BlockSpec((1, w), lambda i: (0, 0))

    return pl.pallas_call(
        body, grid=(s // tr,),
        in_specs=[row(GW), row(GW), row(MW), row(CONV_CH), row(D_MIX), vec(GDV), vec(MW), vec(CONV_CH)],
        out_specs=row(D_MIX), out_shape=jax.ShapeDtypeStruct((s, D_MIX), MXU),
        name=name, compiler_params=_cp(("parallel",)))(o_f, o_b, o_att, pre, proj, gng, mog, cog)


def _merge_bwd(d_y, o_f, o_b, o_att, pre, proj, gng, mog, cog, name):
    s = proj.shape[0]
    tr = _rows(s)

    def body(dy_ref, of_ref, ob_ref, oa_ref, pre_ref, z_ref, gng_ref, mog_ref, cog_ref,
             dz_ref, dos_ref, doa_ref, dpre_ref, dgng_ref, dmog_ref, dcog_ref):
        @pl.when(pl.program_id(0) == 0)
        def _():
            for r in (dgng_ref, dmog_ref, dcog_ref):
                r[...] = jnp.zeros_like(r)

        z, dy = z_ref[...], dy_ref[...]
        sg = _sigmoid(z)
        sz = z * sg
        dsz = sg * (1.0 + z * (1.0 - sg))
        dcat = dy * sz
        dyz = dy * dsz
        osum = of_ref[...] + ob_ref[...]
        gg = gng_ref[...]
        dgg = jnp.zeros_like(gg)
        for h in range(GH):
            sl = slice(h * GDV, (h + 1) * GDV)
            dz_ref[:, sl] = dyz[:, sl] * _rms(osum[:, sl], gg)
            dx, dg = _rms_bwd(dcat[:, sl], osum[:, sl], gg)
            dos_ref[:, sl] = dx
            dgg += dg
        dgng_ref[...] += dgg
        sl = slice(GW, GW + MW)
        oa, mg = oa_ref[...], mog_ref[...]
        dz_ref[:, sl] = dyz[:, sl] * _rms(oa, mg)
        dx, dg = _rms_bwd(dcat[:, sl], oa, mg)
        doa_ref[...] = dx
        dmog_ref[...] += dg
        sl = slice(GW + MW, D_MIX)
        pv, cg = pre_ref[...], cog_ref[...]
        dz_ref[:, sl] = dyz[:, sl] * _rms(pv, cg)
        dx, dg = _rms_bwd(dcat[:, sl], pv, cg)
        dpre_ref[...] = dx
        dcog_ref[...] += dg

    def row(w):
        return pl.BlockSpec((tr, w), lambda i: (i, 0))

    def vec(w):
        return pl.BlockSpec((1, w), lambda i: (0, 0))

    def rs(w):
        return jax.ShapeDtypeStruct((s, w), F32)

    def vs(w):
        return jax.ShapeDtypeStruct((1, w), F32)

    return pl.pallas_call(
        body, grid=(s // tr,),
        in_specs=[row(D_MIX), row(GW), row(GW), row(MW), row(CONV_CH), row(D_MIX), vec(GDV), vec(MW), vec(CONV_CH)],
        out_specs=[row(D_MIX), row(GW), row(MW), row(CONV_CH), vec(GDV), vec(MW), vec(CONV_CH)],
        out_shape=[rs(D_MIX), rs(GW), rs(MW), rs(CONV_CH), vs(GDV), vs(MW), vs(CONV_CH)],
        name=name, compiler_params=_cp(("arbitrary",)))(d_y, o_f, o_b, o_att, pre, proj, gng, mog, cog)


def _assemble_dproj(d_z, d_cb, d_cc, d_cx, d_mkv, dv_f, dv_b, dq_f, dq_b, dk_f, dk_b, d_mq, dt_m, dt_f, dt_b, name):
    s = d_z.shape[0]
    tr = _rows(s)

    def body(dz, dcb, dcc, dcx, dmkv, dvf, dvb, dqf, dqb, dkf, dkb, dmq, dtm, dtf, dtb, out):
        dt = out.dtype
        out[:, OZ:OZ + D_MIX] = dz[...].astype(dt)
        out[:, OCB:OCB + CONV_CH] = dcb[...].astype(dt)
        out[:, OCC:OCC + CONV_CH] = dcc[...].astype(dt)
        out[:, OCX:OCX + CONV_CH] = dcx[...].astype(dt)
        out[:, OMKV:OMKV + MKVL] = dmkv[...].astype(dt)
        out[:, OGV:OGV + GW] = (dvf[...] + dvb[...]).astype(dt)
        out[:, OGQ:OGQ + GQK] = (dqf[...] + dqb[...]).astype(dt)
        out[:, OGK:OGK + GQK] = (dkf[...] + dkb[...]).astype(dt)
        out[:, OMQ:OMQ + MQL] = dmq[...].astype(dt)
        out[:, OT:OT + LANES] = (dtm[...] + dtf[...] + dtb[...]).astype(dt)

    args = (d_z, d_cb, d_cc, d_cx, d_mkv, dv_f, dv_b, dq_f, dq_b, dk_f, dk_b, d_mq, dt_m, dt_f, dt_b)
    return pl.pallas_call(
        body, grid=(s // tr,),
        in_specs=[pl.BlockSpec((tr, a.shape[1]), lambda i: (i, 0)) for a in args],
        out_specs=pl.BlockSpec((tr, PW), lambda i: (i, 0)),
        out_shape=jax.ShapeDtypeStruct((s, PW), MXU), name=name, compiler_params=_cp(("parallel",)))(*args)


def _layer_fwd(x, mod, wt, cos, sin, tag):
    shift, scale, gate = mod
    h = _norm_mod(x, wt["norm_g"], scale, shift, f"norm_mod_{tag}")
    (proj,) = _matmul(h, wt["w_in"], dims="nn", tm=1024, tn=256, tk=2048, out_dtypes=(F32,), name=f"in_proj_{tag}")
    o_f, st_f = _gla_fwd(proj, wt["wg_pad_f"], wt["bg_f"], False, f"gla_fwd_f_{tag}")
    o_b, st_b = _gla_fwd(proj, wt["wg_pad_b"], wt["bg_b"], True, f"gla_fwd_b_{tag}")
    q, k, v = _mla_prep(proj, cos, sin, wt["q_norm_g"], wt["kv_norm_g"], wt["w_uq"], wt["w_ukv"], f"mla_prep_{tag}")
    o_att, lse = _attn_fwd(q, k, v, f"attn_fwd_{tag}")
    pre = _conv_fwd(proj, wt["conv_w"], f"conv_fwd_{tag}")
    y = _merge_fwd(o_f, o_b, o_att, pre, proj, wt["gla_norm_g"], wt["mla_out_g"], wt["conv_out_g"], f"merge_fwd_{tag}")
    x_new, u = _matmul(y, wt["w_out"], dims="nn", tm=1024, tn=512, tk=2048, out_dtypes=(F32, F32),
                       name=f"out_proj_{tag}", epilogue=lambda acc, xv, gv: (xv + gv * acc, acc),
                       extras=(x, gate), extra_kinds=("mn", "n"))
    saved = dict(x=x, h=h, proj=proj, o_f=o_f, o_b=o_b, st_f=st_f, st_b=st_b, q=q, k=k, v=v,
                 o_att=o_att, lse=lse, pre=pre, y=y, u=u)
    return x_new, saved


def _layer_bwd(d_out, sv, mod, wt, cos, sin, tag):
    shift, scale, gate = mod
    proj = sv["proj"]
    d_u, d_gate = _gate_bwd(d_out, sv["u"], gate, f"gate_bwd_{tag}")
    (d_y,) = _matmul(d_u, wt["w_out"], dims="nt", tm=1024, tn=512, tk=2048, out_dtypes=(F32,), name=f"out_proj_dx_{tag}")
    (g_w_out,) = _matmul(sv["y"], d_u, dims="tn", tm=1024, tn=512, tk=2048, out_dtypes=(MXU,), name=f"out_proj_dw_{tag}")
    d_z, d_osum, d_oatt, d_pre, d_gng, d_mog, d_cog = _merge_bwd(
        d_y, sv["o_f"], sv["o_b"], sv["o_att"], sv["pre"], proj, wt["gla_norm_g"], wt["mla_out_g"], wt["conv_out_g"],
        f"merge_bwd_{tag}")
    d_cb, d_cc, d_cx, d_conv_w = _conv_bwd(proj, wt["conv_w"], d_pre, f"conv_bwd_{tag}")
    d_q, d_k, d_v = _attn_bwd(sv["q"], sv["k"], sv["v"], sv["o_att"], sv["lse"], d_oatt, f"attn_bwd_{tag}")
    d_mq, d_mkv, dt_m, g_w_uq, g_w_ukv, d_qg, d_kvg = _mla_prep_bwd(
        proj, cos, sin, wt["q_norm_g"], wt["kv_norm_g"], wt["w_uq"], wt["w_ukv"], d_q, d_k, d_v, f"mla_prep_bwd_{tag}")
    dq_f, dk_f, dv_f, dt_f, d_wg_f, d_bg_f = _gla_bwd(proj, wt["wg_pad_f"], wt["bg_f"], sv["st_f"], d_osum, False,
                                                     f"gla_bwd_f_{tag}")
    dq_b, dk_b, dv_b, dt_b, d_wg_b, d_bg_b = _gla_bwd(proj, wt["wg_pad_b"], wt["bg_b"], sv["st_b"], d_osum, True,
                                                     f"gla_bwd_b_{tag}")
    d_proj = _assemble_dproj(d_z, d_cb, d_cc, d_cx, d_mkv, dv_f, dv_b, dq_f, dq_b, dk_f, dk_b, d_mq, dt_m, dt_f, dt_b,
                             f"assemble_dproj_{tag}")
    (g_w_in,) = _matmul(sv["h"], d_proj, dims="tn", tm=1024, tn=256, tk=2048, out_dtypes=(MXU,), name=f"in_proj_dw_{tag}")
    (d_h,) = _matmul(d_proj, wt["w_in"], dims="nt", tm=1024, tn=512, tk=PW // 2, out_dtypes=(F32,), name=f"in_proj_dx_{tag}")
    d_x, d_shift, d_scale, d_ng = _norm_mod_bwd(d_h, sv["x"], d_out, wt["norm_g"], scale, f"norm_mod_bwd_{tag}")
    grads = dict(w_in=g_w_in, w_out=g_w_out, w_uq=g_w_uq, w_ukv=g_w_ukv, norm_g=d_ng,
                 wg_pad_f=d_wg_f, bg_f=d_bg_f, wg_pad_b=d_wg_b, bg_b=d_bg_b, gla_norm_g=d_gng,
                 q_norm_g=d_qg, kv_norm_g=d_kvg, mla_out_g=d_mog, conv_w=d_conv_w, conv_out_g=d_cog)
    return d_x, (d_shift, d_scale, d_gate), grads


def _perm_in_cols(w):
    pad = jnp.zeros(w.shape[:-1] + (PW - IN_DIM,), w.dtype)
    return jnp.concatenate([w[..., 3808:5856], w[..., 2272:3808], w[..., 1952:2208], w[..., 768:1536], w[..., 0:768],
                            w[..., 1568:1952], w[..., 2208:2272], w[..., 1536:1568], pad], axis=-1)


def _unperm_in_cols(g):
    return jnp.concatenate([g[..., OGQ:OGQ + 2 * GQK], g[..., OGV:OGV + GW], g[..., OT + MROPE:OT + MROPE + 2 * GRANK],
                            g[..., OMQ:OMQ + MQL], g[..., OMKV:OMKV + MKVL], g[..., OT:OT + MROPE],
                            g[..., OCB:OCB + 3 * CONV_CH], g[..., OZ:OZ + D_MIX]], axis=-1)


def _perm_uq_cols(w):
    w3 = w.reshape(w.shape[:-1] + (MH, MNOPE + MROPE))
    return jnp.concatenate([w3[..., :MNOPE].reshape(w.shape[:-1] + (MH * MNOPE,)),
                            w3[..., MNOPE:].reshape(w.shape[:-1] + (MH * MROPE,))], axis=-1)


def _unperm_uq_cols(g):
    nope = g[..., :MH * MNOPE].reshape(g.shape[:-1] + (MH, MNOPE))
    rope = g[..., MH * MNOPE:].reshape(g.shape[:-1] + (MH, MROPE))
    return jnp.concatenate([nope, rope], axis=-1).reshape(g.shape[:-1] + (MQW,))


def _prep_layer_weights(w_in, w_out, w_uq, w_ukv, small):
    def vec(v):
        return v.reshape(1, -1).astype(F32)

    zeros = functools.partial(jnp.zeros, dtype=F32)
    wg_f, wg_b = small["gla_wg_f"].astype(F32), small["gla_wg_b"].astype(F32)
    wg_pad_f = jnp.concatenate([zeros((MROPE, GQK)), wg_f, zeros((LANES - MROPE - GRANK, GQK))], axis=0)
    wg_pad_b = jnp.concatenate([zeros((MROPE + GRANK, GQK)), wg_b, zeros((LANES - MROPE - 2 * GRANK, GQK))], axis=0)
    return dict(w_in=_perm_in_cols(w_in).astype(MXU), w_out=w_out.astype(MXU), w_uq=_perm_uq_cols(w_uq).astype(MXU),
                w_ukv=w_ukv.astype(MXU), norm_g=vec(small["norm_g"]), wg_pad_f=wg_pad_f, wg_pad_b=wg_pad_b,
                bg_f=vec(small["gla_bg_f"]), bg_b=vec(small["gla_bg_b"]), gla_norm_g=vec(small["gla_norm_g"]),
                q_norm_g=vec(small["mla_q_norm_g"]), kv_norm_g=vec(small["mla_kv_norm_g"]),
                mla_out_g=vec(small["mla_out_g"]), conv_w=small["conv_w"].astype(F32),
                conv_out_g=vec(small["conv_out_g"]))


def _natural_grads(gr):
    return dict(w_in=_unperm_in_cols(gr["w_in"]), w_out=gr["w_out"], mla_w_uq=_unperm_uq_cols(gr["w_uq"]),
                mla_w_ukv=gr["w_ukv"], norm_g=gr["norm_g"][0],
                gla_wg_f=gr["wg_pad_f"][MROPE:MROPE + GRANK], gla_bg_f=gr["bg_f"][0],
                gla_wg_b=gr["wg_pad_b"][MROPE + GRANK:MROPE + 2 * GRANK], gla_bg_b=gr["bg_b"][0],
                gla_norm_g=gr["gla_norm_g"][0], mla_q_norm_g=gr["q_norm_g"][0], mla_kv_norm_g=gr["kv_norm_g"][0],
                mla_out_g=gr["mla_out_g"][0], conv_w=gr["conv_w"], conv_out_g=gr["conv_out_g"][0])


def _exchange(arrs, name, scatter, space):
    n = len(arrs)

    def body(*refs):
        ins, outs = refs[:n], refs[n:2 * n]
        send_sems, recv_sems, loc_sems = refs[2 * n:]
        ax, ay, ac = lax.axis_index("x"), lax.axis_index("y"), lax.axis_index("c")
        me = 4 * ax + 2 * ay + ac

        def src(a, to):
            return ins[a].at[to] if scatter else ins[a]

        def remote(a, r, dst_slot):
            px = 1 - ax if r & 4 else ax
            py = 1 - ay if r & 2 else ay
            pc = 1 - ac if r & 1 else ac
            return pltpu.make_async_remote_copy(
                src_ref=src(a, 4 * px + 2 * py + pc), dst_ref=outs[a].at[dst_slot(4 * px + 2 * py + pc)],
                send_sem=send_sems.at[a, r - 1], recv_sem=recv_sems.at[a, r - 1],
                device_id=(px, py, pc), device_id_type=MESH)

        locs = [pltpu.make_async_copy(src(a, me), outs[a].at[me], loc_sems.at[a]) for a in range(n)]
        for cp in locs:
            cp.start()
        sends = [remote(a, r, lambda peer: me) for r in range(1, N_DEV) for a in range(n)]
        for cp in sends:
            cp.start()
        for r in range(1, N_DEV):
            for a in range(n):
                remote(a, r, lambda peer: peer).wait_recv()
        for cp in sends:
            cp.wait_send()
        for cp in locs:
            cp.wait()

    def out_shape(a):
        return jax.ShapeDtypeStruct(a.shape if scatter else (N_DEV,) + a.shape, a.dtype)

    spec = pl.BlockSpec(memory_space=space)
    return pl.pallas_call(
        body, in_specs=[spec] * n, out_specs=[spec] * n, out_shape=[out_shape(a) for a in arrs],
        scratch_shapes=[pltpu.SemaphoreType.DMA((n, N_DEV - 1)), pltpu.SemaphoreType.DMA((n, N_DEV - 1)),
                        pltpu.SemaphoreType.DMA((n,))],
        name=name, compiler_params=pltpu.CompilerParams(vmem_limit_bytes=VMEM_LIMIT))(*arrs)


def _ada_mod(c_all, ada_w, ada_b_cols, name):
    nl, d, wc = ada_w.shape

    def body(c_ref, w_ref, b_ref, ca_ref, mod_ref):
        cv = c_ref[...]
        ca = cv * _sigmoid(cv)
        ca_ref[...] = ca
        mod_ref[0] = _dotf(ca, w_ref[0]) + b_ref[0]

    return pl.pallas_call(
        body, grid=(nl,),
        in_specs=[pl.BlockSpec((N_DEV, d), lambda l: (0, 0)), pl.BlockSpec((1, d, wc), lambda l: (l, 0, 0)),
                  pl.BlockSpec((1, 1, wc), lambda l: (l, 0, 0))],
        out_specs=[pl.BlockSpec((N_DEV, d), lambda l: (0, 0)), pl.BlockSpec((1, N_DEV, wc), lambda l: (l, 0, 0))],
        out_shape=[jax.ShapeDtypeStruct((N_DEV, d), F32), jax.ShapeDtypeStruct((nl, N_DEV, wc), F32)],
        name=name, compiler_params=_cp(("arbitrary",)))(c_all, ada_w, ada_b_cols)


def _adam(w, g, m, v):
    m2 = ADAM_B1 * m + (1.0 - ADAM_B1) * g
    v2 = ADAM_B2 * v + (1.0 - ADAM_B2) * (g * g)
    m_hat = m2 / (1.0 - ADAM_B1 ** ADAM_STEP)
    v_hat = v2 / (1.0 - ADAM_B2 ** ADAM_STEP)
    delta = -ADAM_LR * (m_hat / (jnp.sqrt(v_hat) + ADAM_EPS) + ADAM_WD * w)
    return delta, m2, v2


def _ada_grad_adam(c_act, d_mod, w, m, v, name):
    nl, d, wc = w.shape
    tk = min(512, d)

    def body(c_ref, dm_ref, w_ref, m_ref, v_ref, g_ref, dl_ref, m2_ref, v2_ref):
        g = _dotf_tn(c_ref[...], dm_ref[0])
        delta, m2, v2 = _adam(w_ref[0], g, m_ref[0], v_ref[0])
        g_ref[0], dl_ref[0], m2_ref[0], v2_ref[0] = g, delta, m2, v2

    blk = pl.BlockSpec((1, tk, wc), lambda l, i: (l, i, 0))
    shp = jax.ShapeDtypeStruct(w.shape, F32)
    return pl.pallas_call(
        body, grid=(nl, d // tk),
        in_specs=[pl.BlockSpec((N_DEV, tk), lambda l, i: (0, i)), pl.BlockSpec((1, N_DEV, wc), lambda l, i: (l, 0, 0)),
                  blk, blk, blk],
        out_specs=[blk] * 4, out_shape=[shp] * 4, name=name,
        compiler_params=_cp(("parallel", "parallel")))(c_act, d_mod, w, m, v)


def _adam_big(recv, w, m, v, name):
    nl, r, c = w.shape
    tr = 256 if r % 256 == 0 else r

    def body(rc_ref, w_ref, m_ref, v_ref, g_ref, dl_ref, m2_ref, v2_ref):
        g = rc_ref[0, 0].astype(F32)
        for d in range(1, N_DEV):
            g = g + rc_ref[d, 0].astype(F32)
        delta, m2, v2 = _adam(w_ref[0], g, m_ref[0], v_ref[0])
        g_ref[0], dl_ref[0], m2_ref[0], v2_ref[0] = g, delta, m2, v2

    blk = pl.BlockSpec((1, tr, c), lambda l, i: (l, i, 0))
    shp = jax.ShapeDtypeStruct(w.shape, F32)
    return pl.pallas_call(
        body, grid=(nl, r // tr),
        in_specs=[pl.BlockSpec((N_DEV, 1, tr, c), lambda l, i: (0, l, i, 0)), blk, blk, blk],
        out_specs=[blk] * 4, out_shape=[shp] * 4, name=name,
        compiler_params=_cp(("parallel", "parallel")))(recv, w, m, v)


def _sum_devices(gathered, name):
    _, r, c = gathered.shape

    def body(g_ref, o_ref):
        acc = g_ref[0]
        for d in range(1, N_DEV):
            acc = acc + g_ref[d]
        o_ref[...] = acc

    spec = pl.BlockSpec(memory_space=pltpu.VMEM)
    return pl.pallas_call(body, in_specs=[spec], out_specs=spec, out_shape=jax.ShapeDtypeStruct((r, c), F32),
                          name=name, compiler_params=pltpu.CompilerParams(vmem_limit_bytes=VMEM_LIMIT))(gathered)


def _adam_small(w, g, m, v, name):
    def body(w_ref, g_ref, m_ref, v_ref, dl_ref, m2_ref, v2_ref):
        dl_ref[...], m2_ref[...], v2_ref[...] = _adam(w_ref[...], g_ref[...], m_ref[...], v_ref[...])

    spec = pl.BlockSpec(memory_space=pltpu.VMEM)
    shp = jax.ShapeDtypeStruct(w.shape, F32)
    return pl.pallas_call(body, in_specs=[spec] * 4, out_specs=[spec] * 3, out_shape=[shp] * 3, name=name,
                          compiler_params=pltpu.CompilerParams(vmem_limit_bytes=VMEM_LIMIT))(w, g, m, v)


def _pack(parts):
    flat = jnp.concatenate([p.reshape(-1).astype(F32) for p in parts])
    assert flat.shape[0] % LANES == 0, flat.shape
    return flat.reshape(-1, LANES)


def _unpack(packed, shapes):
    flat = packed.reshape(-1)
    out, off = [], 0
    for shp in shapes:
        size = 1
        for dim in shp:
            size *= dim
        out.append(flat[off:off + size].reshape(shp))
        off += size
    return out


def _gather_cols(g, per):
    g = jnp.moveaxis(g, 0, -2)
    return g.reshape(g.shape[:-2] + (N_DEV * per,))


def _scatter_cols(g, per):
    return jnp.moveaxis(g.reshape(g.shape[:-1] + (N_DEV, per)), -2, 0)


def _my_cols(full, me, per):
    return lax.dynamic_slice_in_dim(full, me * per, per, axis=full.ndim - 1)


def kernel(x, c, positions, ada_w, ada_b, norm_g, w_in, gla_wg_f, gla_bg_f, gla_wg_b, gla_bg_b, gla_norm_g, mla_q_norm_g, mla_kv_norm_g, mla_w_uq, mla_w_ukv, mla_out_g, conv_w, conv_out_g, w_out, final_g, loss_target, m_ada_w, m_ada_b, m_norm_g, m_w_in, m_gla_wg_f, m_gla_bg_f, m_gla_wg_b, m_gla_bg_b, m_gla_norm_g, m_mla_q_norm_g, m_mla_kv_norm_g, m_mla_w_uq, m_mla_w_ukv, m_mla_out_g, m_conv_w, m_conv_out_g, m_w_out, m_final_g, v_ada_w, v_ada_b, v_norm_g, v_w_in, v_gla_wg_f, v_gla_bg_f, v_gla_wg_b, v_gla_bg_b, v_gla_norm_g, v_mla_q_norm_g, v_mla_kv_norm_g, v_mla_w_uq, v_mla_w_ukv, v_mla_out_g, v_conv_w, v_conv_out_g, v_w_out, v_final_g):
    me = 4 * lax.axis_index("x") + 2 * lax.axis_index("y") + lax.axis_index("c")
    nl = ada_w.shape[0]
    s, d = x.shape[1], x.shape[2]
    ada_cols = ada_w.shape[2]
    wgc, cwc = gla_wg_f.shape[2], conv_w.shape[2]

    (g0,) = _exchange([_pack([c, gla_wg_f, gla_wg_b, conv_w])], "gather_small_in", False, pltpu.VMEM)
    g0 = g0.reshape(N_DEV, -1)
    o1, o2, o3 = d, d + gla_wg_f.size, d + 2 * gla_wg_f.size
    c_all = g0[:, :o1]
    wgf_full = _gather_cols(g0[:, o1:o2].reshape((N_DEV,) + gla_wg_f.shape), wgc)
    wgb_full = _gather_cols(g0[:, o2:o3].reshape((N_DEV,) + gla_wg_b.shape), wgc)
    convw_full = _gather_cols(g0[:, o3:].reshape((N_DEV,) + conv_w.shape), cwc)

    ada_b_cols = _my_cols(ada_b, me, ada_cols).reshape(nl, 1, ada_cols)
    c_act, mod_cols = _ada_mod(c_all, ada_w, ada_b_cols, "ada_mod")
    (g1,) = _exchange([_pack([mod_cols])], "gather_mod", False, pltpu.VMEM)
    mod_all = g1.reshape(N_DEV, nl, N_DEV, ada_cols)
    mod_mine = _gather_cols(lax.dynamic_index_in_dim(mod_all, me, axis=2, keepdims=False), ada_cols)

    gw_in, gw_out, gw_uq, gw_ukv = _exchange(
        [w_in.astype(MXU), w_out.astype(MXU), mla_w_uq.astype(MXU), mla_w_ukv.astype(MXU)],
        "gather_weights", False, pltpu.HBM)
    w_in_full = _gather_cols(gw_in, w_in.shape[2])
    w_uq_full = _gather_cols(gw_uq, mla_w_uq.shape[2])
    w_ukv_full = _gather_cols(gw_ukv, mla_w_ukv.shape[2])
    w_out_full = jnp.moveaxis(gw_out, 0, 1).reshape(nl, N_DEV * w_out.shape[1], w_out.shape[2])

    inv_freq = ROPE_THETA ** (-jnp.arange(0, MROPE, 2, dtype=F32) / MROPE)
    ang = positions[0].astype(F32)[:, None] * inv_freq
    cos, sin = jnp.tile(jnp.cos(ang), (1, LANES * 2 // MROPE)), jnp.tile(jnp.sin(ang), (1, LANES * 2 // MROPE))

    layers, mods = [], []
    for l in range(nl):
        small = dict(norm_g=norm_g[l], gla_wg_f=wgf_full[l], gla_bg_f=gla_bg_f[l], gla_wg_b=wgb_full[l],
                     gla_bg_b=gla_bg_b[l], gla_norm_g=gla_norm_g[l], mla_q_norm_g=mla_q_norm_g[l],
                     mla_kv_norm_g=mla_kv_norm_g[l], mla_out_g=mla_out_g[l], conv_w=convw_full[l],
                     conv_out_g=conv_out_g[l])
        layers.append(_prep_layer_weights(w_in_full[l], w_out_full[l], w_uq_full[l], w_ukv_full[l], small))
        mods.append(tuple(mod_mine[l, i * d:(i + 1) * d].reshape(1, d) for i in range(3)))

    h = x[0]
    saved = []
    for l in range(nl):
        h, sv = _layer_fwd(h, mods[l], layers[l], cos, sin, f"l{l}")
        saved.append(sv)
    loss_part, d_h, d_final_g = _final_loss(h, final_g.reshape(1, d), loss_target[0], "final_loss")
    loss = lax.psum(loss_part[0, 0], ("x", "y", "c"))

    d_mods, grads = [None] * nl, [None] * nl
    for l in reversed(range(nl)):
        d_h, d_mods[l], gr = _layer_bwd(d_h, saved[l], mods[l], layers[l], cos, sin, f"l{l}")
        grads[l] = _natural_grads(gr)
    grad_x = d_h[None]

    def stacked(name):
        return jnp.stack([grads[l][name] for l in range(nl)])

    small_names = ["norm_g", "gla_wg_f", "gla_bg_f", "gla_wg_b", "gla_bg_b", "gla_norm_g", "mla_q_norm_g",
                   "mla_kv_norm_g", "mla_out_g", "conv_w", "conv_out_g"]
    d_mod_mine = jnp.stack([jnp.concatenate(d_mods[l], axis=-1)[0] for l in range(nl)])
    parts = [d_mod_mine] + [stacked(n) for n in small_names] + [d_final_g]
    shapes = [p.shape for p in parts]
    (g2,) = _exchange([_pack(parts)], "gather_small_grads", False, pltpu.VMEM)
    d_mod_all = g2.reshape(N_DEV, -1)[:, :d_mod_mine.size].reshape(N_DEV, nl, 3 * d)
    summed = dict(zip(["ada_b"] + small_names + ["final_g"], _unpack(_sum_devices(g2, "sum_small_grads"), shapes)))
    summed["gla_wg_f"] = _my_cols(summed["gla_wg_f"], me, wgc)
    summed["gla_wg_b"] = _my_cols(summed["gla_wg_b"], me, wgc)
    summed["conv_w"] = _my_cols(summed["conv_w"], me, cwc)

    d_mod_cols = jnp.moveaxis(_my_cols(d_mod_all, me, ada_cols), 0, 1)
    out = {}
    out["ada_w"] = _ada_grad_adam(c_act, d_mod_cols, ada_w, m_ada_w, v_ada_w, "ada_grad_adam")

    send_in = _scatter_cols(stacked("w_in"), w_in.shape[2]).astype(MXU)
    send_uq = _scatter_cols(stacked("mla_w_uq"), mla_w_uq.shape[2]).astype(MXU)
    send_ukv = _scatter_cols(stacked("mla_w_ukv"), mla_w_ukv.shape[2]).astype(MXU)
    send_out = jnp.moveaxis(stacked("w_out").reshape(nl, N_DEV, w_out.shape[1], w_out.shape[2]), 1, 0).astype(MXU)
    r_in, r_out, r_uq, r_ukv = _exchange([send_in, send_out, send_uq, send_ukv], "scatter_grads", True, pltpu.HBM)
    out["w_in"] = _adam_big(r_in, w_in, m_w_in, v_w_in, "adam_w_in")
    out["w_out"] = _adam_big(r_out, w_out, m_w_out, v_w_out, "adam_w_out")
    out["mla_w_uq"] = _adam_big(r_uq, mla_w_uq, m_mla_w_uq, v_mla_w_uq, "adam_w_uq")
    out["mla_w_ukv"] = _adam_big(r_ukv, mla_w_ukv, m_mla_w_ukv, v_mla_w_ukv, "adam_w_ukv")

    given = dict(ada_b=(ada_b, m_ada_b, v_ada_b), norm_g=(norm_g, m_norm_g, v_norm_g),
                 gla_wg_f=(gla_wg_f, m_gla_wg_f, v_gla_wg_f), gla_bg_f=(gla_bg_f, m_gla_bg_f, v_gla_bg_f),
                 gla_wg_b=(gla_wg_b, m_gla_wg_b, v_gla_wg_b), gla_bg_b=(gla_bg_b, m_gla_bg_b, v_gla_bg_b),
                 gla_norm_g=(gla_norm_g, m_gla_norm_g, v_gla_norm_g),
                 mla_q_norm_g=(mla_q_norm_g, m_mla_q_norm_g, v_mla_q_norm_g),
                 mla_kv_norm_g=(mla_kv_norm_g, m_mla_kv_norm_g, v_mla_kv_norm_g),
                 mla_out_g=(mla_out_g, m_mla_out_g, v_mla_out_g), conv_w=(conv_w, m_conv_w, v_conv_w),
                 conv_out_g=(conv_out_g, m_conv_out_g, v_conv_out_g), final_g=(final_g, m_final_g, v_final_g))
    names = list(given)
    wshapes = [given[n][0].shape for n in names]
    packed = [_pack([given[n][i] for n in names]) for i in range(3)]
    g_small = _pack([summed[n].reshape(given[n][0].shape) for n in names])
    res = _adam_small(packed[0], g_small, packed[1], packed[2], "adam_small")
    unpacked = [_unpack(r, wshapes) for r in res]
    for i, n in enumerate(names):
        out[n] = (summed[n].reshape(given[n][0].shape), unpacked[0][i], unpacked[1][i], unpacked[2][i])

    order = ["ada_w", "ada_b", "norm_g", "w_in", "gla_wg_f", "gla_bg_f", "gla_wg_b", "gla_bg_b", "gla_norm_g",
             "mla_q_norm_g", "mla_kv_norm_g", "mla_w_uq", "mla_w_ukv", "mla_out_g", "conv_w", "conv_out_g", "w_out",
             "final_g"]
    return (loss, grad_x, *[out[n][0] for n in order], *[out[n][1] for n in order], *[out[n][2] for n in order],
            *[out[n][3] for n in order])
```

```python
import functools

import jax
import jax.numpy as jnp
from jax import lax
from jax.experimental import pallas as pl
from jax.experimental.pallas import tpu as pltpu

F32 = jnp.float32
MXU = jnp.bfloat16
HI = lax.Precision.HIGHEST
N_DEV = 8
MESH = pl.DeviceIdType.MESH

D_MIX = 2048
GH, GDK, GDV = 6, 64, 128
GW = GH * GDV
GQK = GH * GDK
GRANK = 16
GTEMP = 16.0
CHUNK = 64
MH, MQL, MKVL, MNOPE, MROPE, MDV = 6, 384, 256, 128, 64, 128
MW = MH * MDV
MQW = MH * (MNOPE + MROPE)
MKVW = MH * (MNOPE + MDV)
CONV_CH = 512
ROPE_THETA = 10000.0
EPS = 1e-6
IN_DIM = 5856
OZ, OCB, OCC, OCX, OMKV, OGV, OGQ, OGK, OMQ, OT = 0, 2048, 2560, 3072, 3584, 3840, 4608, 4992, 5376, 5760
PW = 5888
LANES = 128
VMEM_LIMIT = 56 * 1024 * 1024

ADAM_LR, ADAM_B1, ADAM_B2, ADAM_EPS, ADAM_WD, ADAM_STEP = 0.001, 0.9, 0.999, 1e-08, 0.01, 10


def _cp(sem=None):
    return pltpu.CompilerParams(dimension_semantics=sem, vmem_limit_bytes=VMEM_LIMIT)


def _dot(a, b):
    return jnp.dot(a.astype(MXU), b.astype(MXU), preferred_element_type=F32)


def _dot_nt(a, b):
    return lax.dot_general(a.astype(MXU), b.astype(MXU), (((1,), (1,)), ((), ())), preferred_element_type=F32)


def _dot_tn(a, b):
    return lax.dot_general(a.astype(MXU), b.astype(MXU), (((0,), (0,)), ((), ())), preferred_element_type=F32)


def _dotf(a, b):
    return jnp.dot(a, b, precision=HI, preferred_element_type=F32)


def _dotf_nt(a, b):
    return lax.dot_general(a, b, (((1,), (1,)), ((), ())), precision=HI, preferred_element_type=F32)


def _dotf_tn(a, b):
    return lax.dot_general(a, b, (((0,), (0,)), ((), ())), precision=HI, preferred_element_type=F32)


def _rows(s):
    return min(256, s)


def _rms(x, g):
    r = lax.rsqrt(jnp.mean(x * x, axis=-1, keepdims=True) + EPS)
    return x * r * g


def _rms_bwd(dy, x, g):
    r = lax.rsqrt(jnp.mean(x * x, axis=-1, keepdims=True) + EPS)
    xh = x * r
    dxh = dy * g
    dg = jnp.sum(dy * xh, axis=0, keepdims=True)
    dx = r * (dxh - xh * jnp.mean(dxh * xh, axis=-1, keepdims=True))
    return dx, dg


def _sigmoid(z):
    return 1.0 / (1.0 + jnp.exp(-z))


def _matmul(a, b, *, dims, tm, tn, tk, out_dtypes, name, epilogue=None, extras=(), extra_kinds=()):
    if dims == "nn":
        (m, k), n, mul = a.shape, b.shape[1], _dot
    elif dims == "nt":
        (m, k), n, mul = a.shape, b.shape[0], _dot_nt
    else:
        (k, m), n, mul = a.shape, b.shape[1], _dot_tn
    tm, tn, tk = min(tm, m), min(tn, n), min(tk, k)
    assert m % tm == 0 and n % tn == 0 and k % tk == 0, (m, n, k, tm, tn, tk)
    if dims == "nn":
        a_spec = pl.BlockSpec((tm, tk), lambda i, j, kk: (i, kk))
        b_spec = pl.BlockSpec((tk, tn), lambda i, j, kk: (kk, j))
    elif dims == "nt":
        a_spec = pl.BlockSpec((tm, tk), lambda i, j, kk: (i, kk))
        b_spec = pl.BlockSpec((tn, tk), lambda i, j, kk: (j, kk))
    else:
        a_spec = pl.BlockSpec((tk, tm), lambda i, j, kk: (kk, i))
        b_spec = pl.BlockSpec((tk, tn), lambda i, j, kk: (kk, j))
    nk = k // tk
    n_extra = len(extras)
    n_out = len(out_dtypes)
    extra_specs = []
    for kind in extra_kinds:
        if kind == "mn":
            extra_specs.append(pl.BlockSpec((tm, tn), lambda i, j, kk: (i, j)))
        else:
            extra_specs.append(pl.BlockSpec((1, tn), lambda i, j, kk: (0, j)))

    def body(*refs):
        a_ref, b_ref = refs[0], refs[1]
        ex = refs[2:2 + n_extra]
        outs = refs[2 + n_extra:2 + n_extra + n_out]
        acc = refs[-1]
        kk = pl.program_id(2)

        @pl.when(kk == 0)
        def _():
            acc[...] = jnp.zeros_like(acc)

        acc[...] += mul(a_ref[...], b_ref[...])

        @pl.when(kk == nk - 1)
        def _():
            res = acc[...]
            vals = (res,) if epilogue is None else epilogue(res, *[e[...] for e in ex])
            for o, v in zip(outs, vals):
                o[...] = v.astype(o.dtype)

    out_spec = pl.BlockSpec((tm, tn), lambda i, j, kk: (i, j))
    res = pl.pallas_call(
        body, grid=(m // tm, n // tn, nk),
        in_specs=[a_spec, b_spec] + extra_specs,
        out_specs=[out_spec] * n_out,
        out_shape=[jax.ShapeDtypeStruct((m, n), dt) for dt in out_dtypes],
        scratch_shapes=[pltpu.VMEM((tm, tn), F32)],
        name=name, compiler_params=_cp(("parallel", "parallel", "arbitrary")),
    )(a, b, *extras)
    return res


def _norm_mod(x, g, scale, shift, name):
    s, d = x.shape
    tr = _rows(s)

    def body(x_ref, g_ref, sc_ref, sh_ref, h_ref):
        h = _rms(x_ref[...], g_ref[...]) * (1.0 + sc_ref[...]) + sh_ref[...]
        h_ref[...] = h.astype(h_ref.dtype)

    row = pl.BlockSpec((tr, d), lambda i: (i, 0))
    vec = pl.BlockSpec((1, d), lambda i: (0, 0))
    return pl.pallas_call(body, grid=(s // tr,), in_specs=[row, vec, vec, vec], out_specs=row,
                          out_shape=jax.ShapeDtypeStruct((s, d), MXU), name=name,
                          compiler_params=_cp(("parallel",)))(x, g, scale, shift)


def _norm_mod_bwd(d_h, x, d_out, g, scale, name):
    s, d = x.shape
    tr = _rows(s)

    def body(dh_ref, x_ref, do_ref, g_ref, sc_ref, dx_ref, dsh_ref, dsc_ref, dg_ref):
        i = pl.program_id(0)

        @pl.when(i == 0)
        def _():
            dsh_ref[...] = jnp.zeros_like(dsh_ref)
            dsc_ref[...] = jnp.zeros_like(dsc_ref)
            dg_ref[...] = jnp.zeros_like(dg_ref)

        dh = dh_ref[...]
        xv = x_ref[...]
        gv = g_ref[...]
        r = lax.rsqrt(jnp.mean(xv * xv, axis=-1, keepdims=True) + EPS)
        xh = xv * r
        dsh_ref[...] += jnp.sum(dh, axis=0, keepdims=True)
        dsc_ref[...] += jnp.sum(dh * (xh * gv), axis=0, keepdims=True)
        dhn = dh * (1.0 + sc_ref[...])
        dg_ref[...] += jnp.sum(dhn * xh, axis=0, keepdims=True)
        dxh = dhn * gv
        dx_ref[...] = do_ref[...] + r * (dxh - xh * jnp.mean(dxh * xh, axis=-1, keepdims=True))

    row = pl.BlockSpec((tr, d), lambda i: (i, 0))
    vec = pl.BlockSpec((1, d), lambda i: (0, 0))
    vshape = jax.ShapeDtypeStruct((1, d), F32)
    return pl.pallas_call(body, grid=(s // tr,), in_specs=[row, row, row, vec, vec],
                          out_specs=[row, vec, vec, vec],
                          out_shape=[jax.ShapeDtypeStruct((s, d), F32), vshape, vshape, vshape],
                          name=name, compiler_params=_cp(("arbitrary",)))(d_h, x, d_out, g, scale)


def _gate_bwd(d_out, u, gate, name):
    s, d = d_out.shape
    tr = _rows(s)

    def body(do_ref, u_ref, gt_ref, du_ref, dgt_ref):
        @pl.when(pl.program_id(0) == 0)
        def _():
            dgt_ref[...] = jnp.zeros_like(dgt_ref)

        do = do_ref[...]
        du_ref[...] = (do * gt_ref[...]).astype(du_ref.dtype)
        dgt_ref[...] += jnp.sum(do * u_ref[...], axis=0, keepdims=True)

    row = pl.BlockSpec((tr, d), lambda i: (i, 0))
    vec = pl.BlockSpec((1, d), lambda i: (0, 0))
    return pl.pallas_call(body, grid=(s // tr,), in_specs=[row, row, vec], out_specs=[row, vec],
                          out_shape=[jax.ShapeDtypeStruct((s, d), MXU), jax.ShapeDtypeStruct((1, d), F32)],
                          name=name, compiler_params=_cp(("arbitrary",)))(d_out, u, gate)


def _final_loss(x, g, target, name):
    s, d = x.shape
    tr = _rows(s)

    def body(x_ref, g_ref, t_ref, loss_ref, dx_ref, dg_ref):
        @pl.when(pl.program_id(0) == 0)
        def _():
            loss_ref[...] = jnp.zeros_like(loss_ref)
            dg_ref[...] = jnp.zeros_like(dg_ref)

        xv = x_ref[...]
        gv = g_ref[...]
        diff = _rms(xv, gv) - t_ref[...]
        part = 0.5 * jnp.sum(jnp.sum(diff * diff, axis=-1, keepdims=True) / d, axis=0, keepdims=True)
        loss_ref[...] += jnp.broadcast_to(part, loss_ref.shape)
        dx, dg = _rms_bwd(diff / d, xv, gv)
        dx_ref[...] = dx
        dg_ref[...] += dg

    row = pl.BlockSpec((tr, d), lambda i: (i, 0))
    vec = pl.BlockSpec((1, d), lambda i: (0, 0))
    lvec = pl.BlockSpec((1, LANES), lambda i: (0, 0))
    return pl.pallas_call(body, grid=(s // tr,), in_specs=[row, vec, row], out_specs=[lvec, row, vec],
                          out_shape=[jax.ShapeDtypeStruct((1, LANES), F32), jax.ShapeDtypeStruct((s, d), F32),
                                     jax.ShapeDtypeStruct((1, d), F32)],
                          name=name, compiler_params=_cp(("arbitrary",)))(x, g, target)


def _shift_rows(u, s, down):
    ri = lax.broadcasted_iota(jnp.int32, u.shape, 0)
    if down:
        return jnp.where(ri == 0, 0.0, pltpu.roll(u, 1, 0))
    return jnp.where(ri == s - 1, 0.0, pltpu.roll(u, s - 1, 0))


def _conv_fwd(proj, conv_w, name):
    s = proj.shape[0]
    nt = CONV_CH // LANES

    def body(cb_ref, cc_ref, cx_ref, w_ref, pre_ref):
        u = cc_ref[...] * cx_ref[...]
        conv = _shift_rows(u, s, True) * w_ref[0:1, :] + u * w_ref[1:2, :] + _shift_rows(u, s, False) * w_ref[2:3, :]
        pre_ref[...] = cb_ref[...] * conv

    def col(off):
        return pl.BlockSpec((s, LANES), lambda j: (0, off // LANES + j))

    return pl.pallas_call(body, grid=(nt,), in_specs=[col(OCB), col(OCC), col(OCX), pl.BlockSpec((3, LANES), lambda j: (0, j))],
                          out_specs=pl.BlockSpec((s, LANES), lambda j: (0, j)),
                          out_shape=jax.ShapeDtypeStruct((s, CONV_CH), F32), name=name,
                          compiler_params=_cp(("parallel",)))(proj, proj, proj, conv_w)


def _conv_bwd(proj, conv_w, d_pre, name):
    s = proj.shape[0]
    nt = CONV_CH // LANES

    def body(cb_ref, cc_ref, cx_ref, w_ref, dp_ref, dcb_ref, dcc_ref, dcx_ref, dw_ref):
        cc, cx = cc_ref[...], cx_ref[...]
        u = cc * cx
        up, dn = _shift_rows(u, s, True), _shift_rows(u, s, False)
        w0, w1, w2 = w_ref[0:1, :], w_ref[1:2, :], w_ref[2:3, :]
        conv = up * w0 + u * w1 + dn * w2
        dp = dp_ref[...]
        dcb_ref[...] = dp * conv
        dconv = dp * cb_ref[...]
        du = _shift_rows(dconv, s, False) * w0 + dconv * w1 + _shift_rows(dconv, s, True) * w2
        dcc_ref[...] = du * cx
        dcx_ref[...] = du * cc
        dw_ref[0:1, :] = jnp.sum(dconv * up, axis=0, keepdims=True)
        dw_ref[1:2, :] = jnp.sum(dconv * u, axis=0, keepdims=True)
        dw_ref[2:3, :] = jnp.sum(dconv * dn, axis=0, keepdims=True)

    def col(off):
        return pl.BlockSpec((s, LANES), lambda j: (0, off // LANES + j))

    blk = pl.BlockSpec((s, LANES), lambda j: (0, j))
    wblk = pl.BlockSpec((3, LANES), lambda j: (0, j))
    full = jax.ShapeDtypeStruct((s, CONV_CH), F32)
    return pl.pallas_call(body, grid=(nt,), in_specs=[col(OCB), col(OCC), col(OCX), wblk, blk],
                          out_specs=[blk, blk, blk, wblk],
                          out_shape=[full, full, full, jax.ShapeDtypeStruct((3, CONV_CH), F32)],
                          name=name, compiler_params=_cp(("parallel",)))(proj, proj, proj, conv_w, d_pre)


def _gla_gates(q_ref, k_ref, t_ref, wg_ref, bg_ref, reverse):
    q = q_ref[...] * (GDK ** -0.5)
    k = k_ref[...]
    t = t_ref[...]
    a = _dotf(t, wg_ref[...]) + bg_ref[...]
    la = (jnp.minimum(a, 0.0) - jnp.log(1.0 + jnp.exp(-jnp.abs(a)))) / GTEMP
    ri = lax.broadcasted_iota(jnp.int32, (CHUNK, CHUNK), 0)
    ci = lax.broadcasted_iota(jnp.int32, (CHUNK, CHUNK), 1)
    if reverse:
        cum, mask = ci >= ri, ci > ri
    else:
        cum, mask = ci <= ri, ci <= ri
    cumf = cum.astype(F32)
    b = _dotf(cumf, la)
    bl = jnp.sum(la, axis=0, keepdims=True)
    return q, k, t, a, cumf, mask, b, bl


def _gla_specs(s, reverse):
    nc = s // CHUNK

    def row(n):
        return nc - 1 - n if reverse else n

    return nc, row


def _gla_fwd(proj, wg_pad, bg, reverse, name):
    s = proj.shape[0]
    nc, row = _gla_specs(s, reverse)

    def body(q_ref, k_ref, v_ref, t_ref, wg_ref, bg_ref, o_ref, st_ref, state):
        @pl.when(pl.program_id(0) == 0)
        def _():
            state[...] = jnp.zeros_like(state)

        q, k, _, _, _, mask, b, bl = _gla_gates(q_ref, k_ref, t_ref, wg_ref, bg_ref, reverse)
        qd = q * jnp.exp(b)
        ki = k * jnp.exp(-b)
        kte = k * jnp.exp(bl - b)
        decay = jnp.exp(bl)
        lane = lax.broadcasted_iota(jnp.int32, (CHUNK, LANES), 1)
        for h in range(GH):
            p = h // 2
            sl = slice(p * LANES, (p + 1) * LANES)
            lm = (lane < GDK) if h % 2 == 0 else (lane >= GDK)
            qd_h = jnp.where(lm, qd[:, sl], 0.0)
            kte_h = jnp.where(lm, kte[:, sl], 0.0)
            v_h = v_ref[:, h * GDV:(h + 1) * GDV]
            st = state[h]
            a_mat = jnp.where(mask, _dot_nt(qd_h, ki[:, sl]), 0.0)
            o_ref[:, h * GDV:(h + 1) * GDV] = _dot(a_mat, v_h) + _dot_nt(qd_h, st)
            st_ref[0, h] = st
            state[h] = st * decay[:, sl] + _dot_tn(v_h, kte_h)

    return pl.pallas_call(
        body, grid=(nc,),
        in_specs=[pl.BlockSpec((CHUNK, GQK), lambda n: (row(n), OGQ // GQK)),
                  pl.BlockSpec((CHUNK, GQK), lambda n: (row(n), OGK // GQK)),
                  pl.BlockSpec((CHUNK, GW), lambda n: (row(n), OGV // GW)),
                  pl.BlockSpec((CHUNK, LANES), lambda n: (row(n), OT // LANES)),
                  pl.BlockSpec((LANES, GQK), lambda n: (0, 0)),
                  pl.BlockSpec((1, GQK), lambda n: (0, 0))],
        out_specs=[pl.BlockSpec((CHUNK, GW), lambda n: (row(n), 0)),
                   pl.BlockSpec((1, GH, GDV, LANES), lambda n: (n, 0, 0, 0))],
        out_shape=[jax.ShapeDtypeStruct((s, GW), F32), jax.ShapeDtypeStruct((nc, GH, GDV, LANES), F32)],
        scratch_shapes=[pltpu.VMEM((GH, GDV, LANES), F32)],
        name=name, compiler_params=_cp(("arbitrary",)))(proj, proj, proj, proj, wg_pad, bg)


def _gla_bwd(proj, wg_pad, bg, states, d_o, reverse, name):
    s = proj.shape[0]
    nc, row = _gla_specs(s, reverse)

    def body(q_ref, k_ref, v_ref, t_ref, wg_ref, bg_ref, st_ref, do_ref,
             dq_ref, dk_ref, dv_ref, dt_ref, dwg_ref, dbg_ref, dstate):
        @pl.when(pl.program_id(0) == 0)
        def _():
            dstate[...] = jnp.zeros_like(dstate)
            dwg_ref[...] = jnp.zeros_like(dwg_ref)
            dbg_ref[...] = jnp.zeros_like(dbg_ref)

        q, k, t, a, cumf, mask, b, bl = _gla_gates(q_ref, k_ref, t_ref, wg_ref, bg_ref, reverse)
        e, ei, ee = jnp.exp(b), jnp.exp(-b), jnp.exp(bl - b)
        qd, ki, kte = q * e, k * ei, k * ee
        decay = jnp.exp(bl)
        lane = lax.broadcasted_iota(jnp.int32, (CHUNK, LANES), 1)
        da_parts = []
        for p in range(GH // 2):
            sl = slice(p * LANES, (p + 1) * LANES)
            dqd = jnp.zeros((CHUNK, LANES), F32)
            dki = jnp.zeros((CHUNK, LANES), F32)
            dkte = jnp.zeros((CHUNK, LANES), F32)
            ddecay = jnp.zeros((1, LANES), F32)
            for half in range(2):
                h = 2 * p + half
                lm = (lane < GDK) if half == 0 else (lane >= GDK)
                qd_h = jnp.where(lm, qd[:, sl], 0.0)
                ki_h = jnp.where(lm, ki[:, sl], 0.0)
                kte_h = jnp.where(lm, kte[:, sl], 0.0)
                v_h = v_ref[:, h * GDV:(h + 1) * GDV]
                do_h = do_ref[:, h * GDV:(h + 1) * GDV]
                st = st_ref[0, h]
                dst = dstate[h]
                a_mat = jnp.where(mask, _dot_nt(qd_h, ki_h), 0.0)
                da_mat = jnp.where(mask, _dot_nt(do_h, v_h), 0.0)
                dv_ref[:, h * GDV:(h + 1) * GDV] = _dot_tn(a_mat, do_h) + _dot_nt(kte_h, dst)
                dqd += _dot(da_mat, ki_h) + _dot(do_h, st)
                dki += _dot_tn(da_mat, qd_h)
                dkte += _dot(v_h, dst)
                ddecay += jnp.sum(dst * st, axis=0, keepdims=True)
                dstate[h] = dst * decay[:, sl] + _dot_tn(do_h, qd_h)
            dq_ref[:, sl] = dqd * e[:, sl] * (GDK ** -0.5)
            dk_ref[:, sl] = dki * ei[:, sl] + dkte * ee[:, sl]
            db = dqd * qd[:, sl] - dki * ki[:, sl] - dkte * kte[:, sl]
            dbl = jnp.sum(dkte * kte[:, sl], axis=0, keepdims=True) + decay[:, sl] * ddecay
            dla = _dotf_tn(cumf, db) + dbl
            da_parts.append(dla * (1.0 / GTEMP) * _sigmoid(-a[:, sl]))
        da = jnp.concatenate(da_parts, axis=1)
        dt_ref[...] = _dotf_nt(da, wg_ref[...])
        dwg_ref[...] += _dotf_tn(t, da)
        dbg_ref[...] += jnp.sum(da, axis=0, keepdims=True)

    def prow(j):
        return row(nc - 1 - j)

    return pl.pallas_call(
        body, grid=(nc,),
        in_specs=[pl.BlockSpec((CHUNK, GQK), lambda j: (prow(j), OGQ // GQK)),
                  pl.BlockSpec((CHUNK, GQK), lambda j: (prow(j), OGK // GQK)),
                  pl.BlockSpec((CHUNK, GW), lambda j: (prow(j), OGV // GW)),
                  pl.BlockSpec((CHUNK, LANES), lambda j: (prow(j), OT // LANES)),
                  pl.BlockSpec((LANES, GQK), lambda j: (0, 0)),
                  pl.BlockSpec((1, GQK), lambda j: (0, 0)),
                  pl.BlockSpec((1, GH, GDV, LANES), lambda j: (nc - 1 - j, 0, 0, 0)),
                  pl.BlockSpec((CHUNK, GW), lambda j: (prow(j), 0))],
        out_specs=[pl.BlockSpec((CHUNK, GQK), lambda j: (prow(j), 0)),
                   pl.BlockSpec((CHUNK, GQK), lambda j: (prow(j), 0)),
                   pl.BlockSpec((CHUNK, GW), lambda j: (prow(j), 0)),
                   pl.BlockSpec((CHUNK, LANES), lambda j: (prow(j), 0)),
                   pl.BlockSpec((LANES, GQK), lambda j: (0, 0)),
                   pl.BlockSpec((1, GQK), lambda j: (0, 0))],
        out_shape=[jax.ShapeDtypeStruct((s, GQK), F32), jax.ShapeDtypeStruct((s, GQK), F32),
                   jax.ShapeDtypeStruct((s, GW), F32), jax.ShapeDtypeStruct((s, LANES), F32),
                   jax.ShapeDtypeStruct((LANES, GQK), F32), jax.ShapeDtypeStruct((1, GQK), F32)],
        scratch_shapes=[pltpu.VMEM((GH, GDV, LANES), F32)],
        name=name, compiler_params=_cp(("arbitrary",)))(proj, proj, proj, proj, wg_pad, bg, states, d_o)


def _rot_half(x):
    lane = lax.broadcasted_iota(jnp.int32, x.shape, 1)
    first = (lane % MROPE) < (MROPE // 2)
    return jnp.where(first, -pltpu.roll(x, LANES - MROPE // 2, 1), pltpu.roll(x, MROPE // 2, 1))


def _mla_prep(proj, cos, sin, qg, kvg, w_uq, w_ukv, name):
    s = proj.shape[0]
    tr = _rows(s)

    def body(mq_ref, mkv_ref, t_ref, cos_ref, sin_ref, qg_ref, kvg_ref, wuq_ref, wukv_ref, q_ref, k_ref, v_ref):
        cosv, sinv = cos_ref[...], sin_ref[...]
        lane = lax.broadcasted_iota(jnp.int32, (tr, LANES), 1)

        def rope(xv):
            return xv * cosv + _rot_half(xv) * sinv

        qm = _dot(_rms(mq_ref[...], qg_ref[...]), wuq_ref[...])
        kv = _dot(_rms(mkv_ref[...], kvg_ref[...]), wukv_ref[...])
        kr_lo = jnp.where(lane < MROPE, rope(t_ref[...]), 0.0)
        kr_hi = pltpu.roll(kr_lo, MROPE, 1)
        for p in range(MH // 2):
            r = rope(qm[:, MW + p * LANES:MW + (p + 1) * LANES]).astype(q_ref.dtype)
            q_ref[2 * p, :, LANES:] = r
            q_ref[2 * p + 1, :, LANES:] = r
        for h in range(MH):
            q_ref[h, :, :LANES] = qm[:, h * LANES:(h + 1) * LANES].astype(q_ref.dtype)
            k_ref[h, :, :LANES] = kv[:, 2 * h * LANES:(2 * h + 1) * LANES].astype(k_ref.dtype)
            k_ref[h, :, LANES:] = (kr_lo if h % 2 == 0 else kr_hi).astype(k_ref.dtype)
            v_ref[h] = kv[:, (2 * h + 1) * LANES:(2 * h + 2) * LANES].astype(v_ref.dtype)

    def full(shape):
        return pl.BlockSpec(shape, lambda i: (0,) * len(shape))

    return pl.pallas_call(
        body, grid=(s // tr,),
        in_specs=[pl.BlockSpec((tr, MQL), lambda i: (i, OMQ // MQL)),
                  pl.BlockSpec((tr, MKVL), lambda i: (i, OMKV // MKVL)),
                  pl.BlockSpec((tr, LANES), lambda i: (i, OT // LANES)),
                  pl.BlockSpec((tr, LANES), lambda i: (i, 0)),
                  pl.BlockSpec((tr, LANES), lambda i: (i, 0)),
                  full((1, MQL)), full((1, MKVL)), full((MQL, MQW)), full((MKVL, MKVW))],
        out_specs=[pl.BlockSpec((MH, tr, 2 * LANES), lambda i: (0, i, 0)),
                   pl.BlockSpec((MH, tr, 2 * LANES), lambda i: (0, i, 0)),
                   pl.BlockSpec((MH, tr, LANES), lambda i: (0, i, 0))],
        out_shape=[jax.ShapeDtypeStruct((MH, s, 2 * LANES), MXU), jax.ShapeDtypeStruct((MH, s, 2 * LANES), MXU),
                   jax.ShapeDtypeStruct((MH, s, LANES), MXU)],
        name=name, compiler_params=_cp(("parallel",)))(proj, proj, proj, cos, sin, qg, kvg, w_uq, w_ukv)


def _mla_prep_bwd(proj, cos, sin, qg, kvg, w_uq, w_ukv, d_q, d_k, d_v, name):
    s = proj.shape[0]
    tr = _rows(s)

    def body(mq_ref, mkv_ref, cos_ref, sin_ref, qg_ref, kvg_ref, wuq_ref, wukv_ref, dq_ref, dk_ref, dv_ref,
             dmq_ref, dmkv_ref, dt_ref, dwuq_ref, dwukv_ref, dqg_ref, dkvg_ref):
        @pl.when(pl.program_id(0) == 0)
        def _():
            for r in (dwuq_ref, dwukv_ref, dqg_ref, dkvg_ref):
                r[...] = jnp.zeros_like(r)

        cosv, sinv = cos_ref[...], sin_ref[...]
        lane = lax.broadcasted_iota(jnp.int32, (tr, LANES), 1)
        lo = lane < MROPE

        def unrope(dv):
            return dv * cosv - _rot_half(dv * sinv)

        parts = [dq_ref[h, :, :LANES] for h in range(MH)]
        for p in range(MH // 2):
            parts.append(unrope(jnp.where(lo, dq_ref[2 * p, :, LANES:], dq_ref[2 * p + 1, :, LANES:])))
        d_qm = jnp.concatenate(parts, axis=1)
        mq, qgv = mq_ref[...], qg_ref[...]
        cq = _rms(mq, qgv)
        dwuq_ref[...] += _dot_tn(cq, d_qm)
        dmq, dqg = _rms_bwd(_dot_nt(d_qm, wuq_ref[...]), mq, qgv)
        dmq_ref[...] = dmq
        dqg_ref[...] += dqg

        parts = []
        for h in range(MH):
            parts += [dk_ref[h, :, :LANES], dv_ref[h]]
        d_kv = jnp.concatenate(parts, axis=1)
        mkv, kvgv = mkv_ref[...], kvg_ref[...]
        ckv = _rms(mkv, kvgv)
        dwukv_ref[...] += _dot_tn(ckv, d_kv)
        dmkv, dkvg = _rms_bwd(_dot_nt(d_kv, wukv_ref[...]), mkv, kvgv)
        dmkv_ref[...] = dmkv
        dkvg_ref[...] += dkvg

        even = dk_ref[0, :, LANES:] + dk_ref[2, :, LANES:] + dk_ref[4, :, LANES:]
        odd = dk_ref[1, :, LANES:] + dk_ref[3, :, LANES:] + dk_ref[5, :, LANES:]
        d_kr = jnp.where(lo, even, 0.0) + pltpu.roll(jnp.where(lo, 0.0, odd), MROPE, 1)
        dt_ref[...] = jnp.where(lo, unrope(d_kr), 0.0)

    def full(shape):
        return pl.BlockSpec(shape, lambda i: (0,) * len(shape))

    return pl.pallas_call(
        body, grid=(s // tr,),
        in_specs=[pl.BlockSpec((tr, MQL), lambda i: (i, OMQ // MQL)),
                  pl.BlockSpec((tr, MKVL), lambda i: (i, OMKV // MKVL)),
                  pl.BlockSpec((tr, LANES), lambda i: (i, 0)),
                  pl.BlockSpec((tr, LANES), lambda i: (i, 0)),
                  full((1, MQL)), full((1, MKVL)), full((MQL, MQW)), full((MKVL, MKVW)),
                  pl.BlockSpec((MH, tr, 2 * LANES), lambda i: (0, i, 0)),
                  pl.BlockSpec((MH, tr, 2 * LANES), lambda i: (0, i, 0)),
                  pl.BlockSpec((MH, tr, LANES), lambda i: (0, i, 0))],
        out_specs=[pl.BlockSpec((tr, MQL), lambda i: (i, 0)), pl.BlockSpec((tr, MKVL), lambda i: (i, 0)),
                   pl.BlockSpec((tr, LANES), lambda i: (i, 0)),
                   full((MQL, MQW)), full((MKVL, MKVW)), full((1, MQL)), full((1, MKVL))],
        out_shape=[jax.ShapeDtypeStruct((s, MQL), F32), jax.ShapeDtypeStruct((s, MKVL), F32),
                   jax.ShapeDtypeStruct((s, LANES), F32),
                   jax.ShapeDtypeStruct((MQL, MQW), F32), jax.ShapeDtypeStruct((MKVL, MKVW), F32),
                   jax.ShapeDtypeStruct((1, MQL), F32), jax.ShapeDtypeStruct((1, MKVL), F32)],
        name=name, compiler_params=_cp(("arbitrary",)))(proj, proj, cos, sin, qg, kvg, w_uq, w_ukv, d_q, d_k, d_v)


ATT_SCALE = (MNOPE + MROPE) ** -0.5


def _attn_fwd(q, k, v, name):
    s = q.shape[1]
    tq = _rows(s)

    def body(q_ref, k_ref, v_ref, o_ref, lse_ref):
        sc = _dot_nt(q_ref[0], k_ref[0]) * ATT_SCALE
        m = jnp.max(sc, axis=-1, keepdims=True)
        p = jnp.exp(sc - m)
        l = jnp.sum(p, axis=-1, keepdims=True)
        o_ref[...] = _dot(p, v_ref[0]) / l
        lse_ref[0] = m + jnp.log(l)

    return pl.pallas_call(
        body, grid=(MH, s // tq),
        in_specs=[pl.BlockSpec((1, tq, 2 * LANES), lambda h, i: (h, i, 0)),
                  pl.BlockSpec((1, s, 2 * LANES), lambda h, i: (h, 0, 0)),
                  pl.BlockSpec((1, s, LANES), lambda h, i: (h, 0, 0))],
        out_specs=[pl.BlockSpec((tq, LANES), lambda h, i: (i, h)),
                   pl.BlockSpec((1, tq, 1), lambda h, i: (h, i, 0))],
        out_shape=[jax.ShapeDtypeStruct((s, MW), F32), jax.ShapeDtypeStruct((MH, s, 1), F32)],
        name=name, compiler_params=_cp(("parallel", "parallel")))(q, k, v)


def _attn_bwd(q, k, v, o, lse, d_o, name):
    s = q.shape[1]
    tq = _rows(s)

    def body(q_ref, k_ref, v_ref, o_ref, lse_ref, do_ref, dq_ref, dk_ref, dv_ref):
        @pl.when(pl.program_id(1) == 0)
        def _():
            dk_ref[...] = jnp.zeros_like(dk_ref)
            dv_ref[...] = jnp.zeros_like(dv_ref)

        qv, kv, do = q_ref[0], k_ref[0], do_ref[...]
        p = jnp.exp(_dot_nt(qv, kv) * ATT_SCALE - lse_ref[0])
        delta = jnp.sum(do * o_ref[...], axis=-1, keepdims=True)
        ds = p * (_dot_nt(do, v_ref[0]) - delta) * ATT_SCALE
        dq_ref[0] = _dot(ds, kv)
        dk_ref[0] += _dot_tn(ds, qv)
        dv_ref[0] += _dot_tn(p, do)

    return pl.pallas_call(
        body, grid=(MH, s // tq),
        in_specs=[pl.BlockSpec((1, tq, 2 * LANES), lambda h, i: (h, i, 0)),
                  pl.BlockSpec((1, s, 2 * LANES), lambda h, i: (h, 0, 0)),
                  pl.BlockSpec((1, s, LANES), lambda h, i: (h, 0, 0)),
                  pl.BlockSpec((tq, LANES), lambda h, i: (i, h)),
                  pl.BlockSpec((1, tq, 1), lambda h, i: (h, i, 0)),
                  pl.BlockSpec((tq, LANES), lambda h, i: (i, h))],
        out_specs=[pl.BlockSpec((1, tq, 2 * LANES), lambda h, i: (h, i, 0)),
                   pl.BlockSpec((1, s, 2 * LANES), lambda h, i: (h, 0, 0)),
                   pl.BlockSpec((1, s, LANES), lambda h, i: (h, 0, 0))],
        out_shape=[jax.ShapeDtypeStruct((MH, s, 2 * LANES), F32), jax.ShapeDtypeStruct((MH, s, 2 * LANES), F32),
                   jax.ShapeDtypeStruct((MH, s, LANES), F32)],
        name=name, compiler_params=_cp(("parallel", "arbitrary")))(q, k, v, o, lse, d_o)


def _merge_fwd(o_f, o_b, o_att, pre, proj, gng, mog, cog, name):
    s = proj.shape[0]
    tr = _rows(s)

    def body(of_ref, ob_ref, oa_ref, pre_ref, z_ref, gng_ref, mog_ref, cog_ref, y_ref):
        z = z_ref[...]
        sz = z * _sigmoid(z)
        osum = of_ref[...] + ob_ref[...]
        gg = gng_ref[...]
        for h in range(GH):
            sl = slice(h * GDV, (h + 1) * GDV)
            y_ref[:, sl] = (_rms(osum[:, sl], gg) * sz[:, sl]).astype(y_ref.dtype)
        y_ref[:, GW:GW + MW] = (_rms(oa_ref[...], mog_ref[...]) * sz[:, GW:GW + MW]).astype(y_ref.dtype)
        y_ref[:, GW + MW:] = (_rms(pre_ref[...], cog_ref[...]) * sz[:, GW + MW:]).astype(y_ref.dtype)

    def row(w):
        return pl.BlockSpec((tr, w), lambda i: (i, 0))

    def vec(w):
        return pl.BlockSpec((1, w), lambda i: (0, 0))

    return pl.pallas_call(
        body, grid=(s // tr,),
        in_specs=[row(GW), row(GW), row(MW), row(CONV_CH), row(D_MIX), vec(GDV), vec(MW), vec(CONV_CH)],
        out_specs=row(D_MIX), out_shape=jax.ShapeDtypeStruct((s, D_MIX), MXU),
        name=name, compiler_params=_cp(("parallel",)))(o_f, o_b, o_att, pre, proj, gng, mog, cog)


def _merge_bwd(d_y, o_f, o_b, o_att, pre, proj, gng, mog, cog, name):
    s = proj.shape[0]
    tr = _rows(s)

    def body(dy_ref, of_ref, ob_ref, oa_ref, pre_ref, z_ref, gng_ref, mog_ref, cog_ref,
             dz_ref, dos_ref, doa_ref, dpre_ref, dgng_ref, dmog_ref, dcog_ref):
        @pl.when(pl.program_id(0) == 0)
        def _():
            for r in (dgng_ref, dmog_ref, dcog_ref):
                r[...] = jnp.zeros_like(r)

        z, dy = z_ref[...], dy_ref[...]
        sg = _sigmoid(z)
        sz = z * sg
        dsz = sg * (1.0 + z * (1.0 - sg))
        dcat = dy * sz
        dyz = dy * dsz
        osum = of_ref[...] + ob_ref[...]
        gg = gng_ref[...]
        dgg = jnp.zeros_like(gg)
        for h in range(GH):
            sl = slice(h * GDV, (h + 1) * GDV)
            dz_ref[:, sl] = dyz[:, sl] * _rms(osum[:, sl], gg)
            dx, dg = _rms_bwd(dcat[:, sl], osum[:, sl], gg)
            dos_ref[:, sl] = dx
            dgg += dg
        dgng_ref[...] += dgg
        sl = slice(GW, GW + MW)
        oa, mg = oa_ref[...], mog_ref[...]
        dz_ref[:, sl] = dyz[:, sl] * _rms(oa, mg)
        dx, dg = _rms_bwd(dcat[:, sl], oa, mg)
        doa_ref[...] = dx
        dmog_ref[...] += dg
        sl = slice(GW + MW, D_MIX)
        pv, cg = pre_ref[...], cog_ref[...]
        dz_ref[:, sl] = dyz[:, sl] * _rms(pv, cg)
        dx, dg = _rms_bwd(dcat[:, sl], pv, cg)
        dpre_ref[...] = dx
        dcog_ref[...] += dg

    def row(w):
        return pl.BlockSpec((tr, w), lambda i: (i, 0))

    def vec(w):
        return pl.BlockSpec((1, w), lambda i: (0, 0))

    def rs(w):
        return jax.ShapeDtypeStruct((s, w), F32)

    def vs(w):
        return jax.ShapeDtypeStruct((1, w), F32)

    return pl.pallas_call(
        body, grid=(s // tr,),
        in_specs=[row(D_MIX), row(GW), row(GW), row(MW), row(CONV_CH), row(D_MIX), vec(GDV), vec(MW), vec(CONV_CH)],
        out_specs=[row(D_MIX), row(GW), row(MW), row(CONV_CH), vec(GDV), vec(MW), vec(CONV_CH)],
        out_shape=[rs(D_MIX), rs(GW), rs(MW), rs(CONV_CH), vs(GDV), vs(MW), vs(CONV_CH)],
        name=name, compiler_params=_cp(("arbitrary",)))(d_y, o_f, o_b, o_att, pre, proj, gng, mog, cog)


def _assemble_dproj(d_z, d_cb, d_cc, d_cx, d_mkv, dv_f, dv_b, dq_f, dq_b, dk_f, dk_b, d_mq, dt_m, dt_f, dt_b, name):
    s = d_z.shape[0]
    tr = _rows(s)

    def body(dz, dcb, dcc, dcx, dmkv, dvf, dvb, dqf, dqb, dkf, dkb, dmq, dtm, dtf, dtb, out):
        dt = out.dtype
        out[:, OZ:OZ + D_MIX] = dz[...].astype(dt)
        out[:, OCB:OCB + CONV_CH] = dcb[...].astype(dt)
        out[:, OCC:OCC + CONV_CH] = dcc[...].astype(dt)
        out[:, OCX:OCX + CONV_CH] = dcx[...].astype(dt)
        out[:, OMKV:OMKV + MKVL] = dmkv[...].astype(dt)
        out[:, OGV:OGV + GW] = (dvf[...] + dvb[...]).astype(dt)
        out[:, OGQ:OGQ + GQK] = (dqf[...] + dqb[...]).astype(dt)
        out[:, OGK:OGK + GQK] = (dkf[...] + dkb[...]).astype(dt)
        out[:, OMQ:OMQ + MQL] = dmq[...].astype(dt)
        out[:, OT:OT + LANES] = (dtm[...] + dtf[...] + dtb[...]).astype(dt)

    args = (d_z, d_cb, d_cc, d_cx, d_mkv, dv_f, dv_b, dq_f, dq_b, dk_f, dk_b, d_mq, dt_m, dt_f, dt_b)
    return pl.pallas_call(
        body, grid=(s // tr,),
        in_specs=[pl.BlockSpec((tr, a.shape[1]), lambda i: (i, 0)) for a in args],
        out_specs=pl.BlockSpec((tr, PW), lambda i: (i, 0)),
        out_shape=jax.ShapeDtypeStruct((s, PW), MXU), name=name, compiler_params=_cp(("parallel",)))(*args)


def _layer_fwd(x, mod, wt, cos, sin, tag):
    shift, scale, gate = mod
    h = _norm_mod(x, wt["norm_g"], scale, shift, f"norm_mod_{tag}")
    (proj,) = _matmul(h, wt["w_in"], dims="nn", tm=1024, tn=256, tk=2048, out_dtypes=(F32,), name=f"in_proj_{tag}")
    o_f, st_f = _gla_fwd(proj, wt["wg_pad_f"], wt["bg_f"], False, f"gla_fwd_f_{tag}")
    o_b, st_b = _gla_fwd(proj, wt["wg_pad_b"], wt["bg_b"], True, f"gla_fwd_b_{tag}")
    q, k, v = _mla_prep(proj, cos, sin, wt["q_norm_g"], wt["kv_norm_g"], wt["w_uq"], wt["w_ukv"], f"mla_prep_{tag}")
    o_att, lse = _attn_fwd(q, k, v, f"attn_fwd_{tag}")
    pre = _conv_fwd(proj, wt["conv_w"], f"conv_fwd_{tag}")
    y = _merge_fwd(o_f, o_b, o_att, pre, proj, wt["gla_norm_g"], wt["mla_out_g"], wt["conv_out_g"], f"merge_fwd_{tag}")
    x_new, u = _matmul(y, wt["w_out"], dims="nn", tm=1024, tn=512, tk=2048, out_dtypes=(F32, F32),
                       name=f"out_proj_{tag}", epilogue=lambda acc, xv, gv: (xv + gv * acc, acc),
                       extras=(x, gate), extra_kinds=("mn", "n"))
    saved = dict(x=x, h=h, proj=proj, o_f=o_f, o_b=o_b, st_f=st_f, st_b=st_b, q=q, k=k, v=v,
                 o_att=o_att, lse=lse, pre=pre, y=y, u=u)
    return x_new, saved


def _layer_bwd(d_out, sv, mod, wt, cos, sin, tag):
    shift, scale, gate = mod
    proj = sv["proj"]
    d_u, d_gate = _gate_bwd(d_out, sv["u"], gate, f"gate_bwd_{tag}")
    (d_y,) = _matmul(d_u, wt["w_out"], dims="nt", tm=1024, tn=512, tk=2048, out_dtypes=(F32,), name=f"out_proj_dx_{tag}")
    (g_w_out,) = _matmul(sv["y"], d_u, dims="tn", tm=1024, tn=512, tk=2048, out_dtypes=(MXU,), name=f"out_proj_dw_{tag}")
    d_z, d_osum, d_oatt, d_pre, d_gng, d_mog, d_cog = _merge_bwd(
        d_y, sv["o_f"], sv["o_b"], sv["o_att"], sv["pre"], proj, wt["gla_norm_g"], wt["mla_out_g"], wt["conv_out_g"],
        f"merge_bwd_{tag}")
    d_cb, d_cc, d_cx, d_conv_w = _conv_bwd(proj, wt["conv_w"], d_pre, f"conv_bwd_{tag}")
    d_q, d_k, d_v = _attn_bwd(sv["q"], sv["k"], sv["v"], sv["o_att"], sv["lse"], d_oatt, f"attn_bwd_{tag}")
    d_mq, d_mkv, dt_m, g_w_uq, g_w_ukv, d_qg, d_kvg = _mla_prep_bwd(
        proj, cos, sin, wt["q_norm_g"], wt["kv_norm_g"], wt["w_uq"], wt["w_ukv"], d_q, d_k, d_v, f"mla_prep_bwd_{tag}")
    dq_f, dk_f, dv_f, dt_f, d_wg_f, d_bg_f = _gla_bwd(proj, wt["wg_pad_f"], wt["bg_f"], sv["st_f"], d_osum, False,
                                                     f"gla_bwd_f_{tag}")
    dq_b, dk_b, dv_b, dt_b, d_wg_b, d_bg_b = _gla_bwd(proj, wt["wg_pad_b"], wt["bg_b"], sv["st_b"], d_osum, True,
                                                     f"gla_bwd_b_{tag}")
    d_proj = _assemble_dproj(d_z, d_cb, d_cc, d_cx, d_mkv, dv_f, dv_b, dq_f, dq_b, dk_f, dk_b, d_mq, dt_m, dt_f, dt_b,
                             f"assemble_dproj_{tag}")
    (g_w_in,) = _matmul(sv["h"], d_proj, dims="tn", tm=1024, tn=256, tk=2048, out_dtypes=(MXU,), name=f"in_proj_dw_{tag}")
    (d_h,) = _matmul(d_proj, wt["w_in"], dims="nt", tm=1024, tn=512, tk=PW // 2, out_dtypes=(F32,), name=f"in_proj_dx_{tag}")
    d_x, d_shift, d_scale, d_ng = _norm_mod_bwd(d_h, sv["x"], d_out, wt["norm_g"], scale, f"norm_mod_bwd_{tag}")
    grads = dict(w_in=g_w_in, w_out=g_w_out, w_uq=g_w_uq, w_ukv=g_w_ukv, norm_g=d_ng,
                 wg_pad_f=d_wg_f, bg_f=d_bg_f, wg_pad_b=d_wg_b, bg_b=d_bg_b, gla_norm_g=d_gng,
                 q_norm_g=d_qg, kv_norm_g=d_kvg, mla_out_g=d_mog, conv_w=d_conv_w, conv_out_g=d_cog)
    return d_x, (d_shift, d_scale, d_gate), grads


def _perm_in_cols(w):
    pad = jnp.zeros(w.shape[:-1] + (PW - IN_DIM,), w.dtype)
    return jnp.concatenate([w[..., 3808:5856], w[..., 2272:3808], w[..., 1952:2208], w[..., 768:1536], w[..., 0:768],
                            w[..., 1568:1952], w[..., 2208:2272], w[..., 1536:1568], pad], axis=-1)


def _unperm_in_cols(g):
    return jnp.concatenate([g[..., OGQ:OGQ + 2 * GQK], g[..., OGV:OGV + GW], g[..., OT + MROPE:OT + MROPE + 2 * GRANK],
                            g[..., OMQ:OMQ + MQL], g[..., OMKV:OMKV + MKVL], g[..., OT:OT + MROPE],
                            g[..., OCB:OCB + 3 * CONV_CH], g[..., OZ:OZ + D_MIX]], axis=-1)


def _perm_uq_cols(w):
    w3 = w.reshape(w.shape[:-1] + (MH, MNOPE + MROPE))
    return jnp.concatenate([w3[..., :MNOPE].reshape(w.shape[:-1] + (MH * MNOPE,)),
                            w3[..., MNOPE:].reshape(w.shape[:-1] + (MH * MROPE,))], axis=-1)


def _unperm_uq_cols(g):
    nope = g[..., :MH * MNOPE].reshape(g.shape[:-1] + (MH, MNOPE))
    rope = g[..., MH * MNOPE:].reshape(g.shape[:-1] + (MH, MROPE))
    return jnp.concatenate([nope, rope], axis=-1).reshape(g.shape[:-1] + (MQW,))


def _prep_layer_weights(w_in, w_out, w_uq, w_ukv, small):
    def vec(v):
        return v.reshape(1, -1).astype(F32)

    zeros = functools.partial(jnp.zeros, dtype=F32)
    wg_f, wg_b = small["gla_wg_f"].astype(F32), small["gla_wg_b"].astype(F32)
    wg_pad_f = jnp.concatenate([zeros((MROPE, GQK)), wg_f, zeros((LANES - MROPE - GRANK, GQK))], axis=0)
    wg_pad_b = jnp.concatenate([zeros((MROPE + GRANK, GQK)), wg_b, zeros((LANES - MROPE - 2 * GRANK, GQK))], axis=0)
    return dict(w_in=_perm_in_cols(w_in).astype(MXU), w_out=w_out.astype(MXU), w_uq=_perm_uq_cols(w_uq).astype(MXU),
                w_ukv=w_ukv.astype(MXU), norm_g=vec(small["norm_g"]), wg_pad_f=wg_pad_f, wg_pad_b=wg_pad_b,
                bg_f=vec(small["gla_bg_f"]), bg_b=vec(small["gla_bg_b"]), gla_norm_g=vec(small["gla_norm_g"]),
                q_norm_g=vec(small["mla_q_norm_g"]), kv_norm_g=vec(small["mla_kv_norm_g"]),
                mla_out_g=vec(small["mla_out_g"]), conv_w=small["conv_w"].astype(F32),
                conv_out_g=vec(small["conv_out_g"]))


def _natural_grads(gr):
    return dict(w_in=_unperm_in_cols(gr["w_in"]), w_out=gr["w_out"], mla_w_uq=_unperm_uq_cols(gr["w_uq"]),
                mla_w_ukv=gr["w_ukv"], norm_g=gr["norm_g"][0],
                gla_wg_f=gr["wg_pad_f"][MROPE:MROPE + GRANK], gla_bg_f=gr["bg_f"][0],
                gla_wg_b=gr["wg_pad_b"][MROPE + GRANK:MROPE + 2 * GRANK], gla_bg_b=gr["bg_b"][0],
                gla_norm_g=gr["gla_norm_g"][0], mla_q_norm_g=gr["q_norm_g"][0], mla_kv_norm_g=gr["kv_norm_g"][0],
                mla_out_g=gr["mla_out_g"][0], conv_w=gr["conv_w"], conv_out_g=gr["conv_out_g"][0])


def _exchange(arrs, name, scatter, space):
    n = len(arrs)

    def body(*refs):
        ins, outs = refs[:n], refs[n:2 * n]
        send_sems, recv_sems, loc_sems = refs[2 * n:]
        ax, ay, ac = lax.axis_index("x"), lax.axis_index("y"), lax.axis_index("c")
        me = 4 * ax + 2 * ay + ac

        def src(a, to):
            return ins[a].at[to] if scatter else ins[a]

        def remote(a, r, dst_slot):
            px = 1 - ax if r & 4 else ax
            py = 1 - ay if r & 2 else ay
            pc = 1 - ac if r & 1 else ac
            return pltpu.make_async_remote_copy(
                src_ref=src(a, 4 * px + 2 * py + pc), dst_ref=outs[a].at[dst_slot(4 * px + 2 * py + pc)],
                send_sem=send_sems.at[a, r - 1], recv_sem=recv_sems.at[a, r - 1],
                device_id=(px, py, pc), device_id_type=MESH)

        locs = [pltpu.make_async_copy(src(a, me), outs[a].at[me], loc_sems.at[a]) for a in range(n)]
        for cp in locs:
            cp.start()
        sends = [remote(a, r, lambda peer: me) for r in range(1, N_DEV) for a in range(n)]
        for cp in sends:
            cp.start()
        for r in range(1, N_DEV):
            for a in range(n):
                remote(a, r, lambda peer: peer).wait_recv()
        for cp in sends:
            cp.wait_send()
        for cp in locs:
            cp.wait()

    def out_shape(a):
        return jax.ShapeDtypeStruct(a.shape if scatter else (N_DEV,) + a.shape, a.dtype)

    spec = pl.BlockSpec(memory_space=space)
    return pl.pallas_call(
        body, in_specs=[spec] * n, out_specs=[spec] * n, out_shape=[out_shape(a) for a in arrs],
        scratch_shapes=[pltpu.SemaphoreType.DMA((n, N_DEV - 1)), pltpu.SemaphoreType.DMA((n, N_DEV - 1)),
                        pltpu.SemaphoreType.DMA((n,))],
        name=name, compiler_params=pltpu.CompilerParams(vmem_limit_bytes=VMEM_LIMIT))(*arrs)


def _peer(r):
    ax, ay, ac = lax.axis_index("x"), lax.axis_index("y"), lax.axis_index("c")
    px = 1 - ax if r & 4 else ax
    py = 1 - ay if r & 2 else ay
    pc = 1 - ac if r & 1 else ac
    return (px, py, pc), 4 * px + 2 * py + pc


def _exchange_start(arrs, after, name, scatter):
    n = len(arrs)

    def body(*refs):
        srcs, lands = refs[:n], refs[n:2 * n]
        send_sems, recv_sems = refs[2 * n + 1], refs[2 * n + 2]
        token = refs[-1]
        me = 4 * lax.axis_index("x") + 2 * lax.axis_index("y") + lax.axis_index("c")
        for r in range(1, N_DEV):
            peer, peer_idx = _peer(r)
            for a in range(n):
                pltpu.make_async_remote_copy(
                    src_ref=srcs[a].at[peer_idx] if scatter else srcs[a], dst_ref=lands[a].at[me],
                    send_sem=send_sems.at[a * (N_DEV - 1) + r - 1], recv_sem=recv_sems.at[a * (N_DEV - 1) + r - 1],
                    device_id=peer, device_id_type=MESH).start()
        token[...] = jnp.zeros_like(token)

    hbm = pl.BlockSpec(memory_space=pltpu.HBM)
    sem = pl.BlockSpec(memory_space=pltpu.SEMAPHORE)
    land_shapes = [a.shape if scatter else (N_DEV,) + a.shape for a in arrs]
    srcs = [pltpu.with_memory_space_constraint(a, pltpu.HBM) for a in arrs]
    lands = [pltpu.with_memory_space_constraint(lax.empty(shp, a.dtype), pltpu.HBM) for shp, a in zip(land_shapes, arrs)]
    res = pl.pallas_call(
        body, name=name,
        in_specs=[hbm] * (2 * n) + [pl.BlockSpec(memory_space=pl.ANY)],
        out_specs=[sem, sem] + [hbm] * (2 * n) + [pl.BlockSpec(memory_space=pltpu.VMEM)],
        out_shape=[pltpu.SemaphoreType.DMA((n * (N_DEV - 1),)), pltpu.SemaphoreType.DMA((n * (N_DEV - 1),))]
        + [pltpu.HBM(a.shape, a.dtype) for a in arrs] + [pltpu.HBM(shp, a.dtype) for shp, a in zip(land_shapes, arrs)]
        + [jax.ShapeDtypeStruct((8, LANES), F32)],
        input_output_aliases={i: 2 + i for i in range(2 * n)},
        compiler_params=pltpu.CompilerParams(has_side_effects=pltpu.SideEffectType.DATAFLOW_SIDE_EFFECTING),
    )(*srcs, *lands, after)
    return res[0], res[1], list(res[2:2 + n]), list(res[2 + n:2 + 2 * n]), res[-1]


def _exchange_wait(handle, after, name, scatter):
    send_sems, recv_sems, srcs, lands, _ = handle
    n = len(srcs)
    after = list(after)

    def body(*refs):
        src_refs, land_refs = refs[:n], refs[n:2 * n]
        ssem, rsem = refs[2 * n], refs[2 * n + 1]
        for r in range(1, N_DEV):
            peer, peer_idx = _peer(r)
            for a in range(n):
                cp = pltpu.make_async_remote_copy(
                    src_ref=src_refs[a].at[peer_idx] if scatter else src_refs[a], dst_ref=land_refs[a].at[peer_idx],
                    send_sem=ssem.at[a * (N_DEV - 1) + r - 1], recv_sem=rsem.at[a * (N_DEV - 1) + r - 1],
                    device_id=peer, device_id_type=MESH)
                cp.wait_send()
                cp.wait_recv()

    hbm = pl.BlockSpec(memory_space=pltpu.HBM)
    sem = pl.BlockSpec(memory_space=pltpu.SEMAPHORE)
    res = pl.pallas_call(
        body, name=name,
        in_specs=[hbm] * (2 * n) + [sem, sem] + [pl.BlockSpec(memory_space=pl.ANY)] * len(after),
        out_specs=[hbm] * (2 * n),
        out_shape=[pltpu.HBM(a.shape, a.dtype) for a in srcs] + [pltpu.HBM(a.shape, a.dtype) for a in lands],
        input_output_aliases={i: i for i in range(2 * n)},
        compiler_params=pltpu.CompilerParams(has_side_effects=pltpu.SideEffectType.DATAFLOW_SIDE_EFFECTING),
    )(*srcs, *lands, send_sems, recv_sems, *after)
    return list(res[n:])


def _ada_mod(c_all, ada_w, ada_b_cols, name):
    nl, d, wc = ada_w.shape

    def body(c_ref, w_ref, b_ref, ca_ref, mod_ref):
        cv = c_ref[...]
        ca = cv * _sigmoid(cv)
        ca_ref[...] = ca
        mod_ref[0] = _dotf(ca, w_ref[0]) + b_ref[0]

    return pl.pallas_call(
        body, grid=(nl,),
        in_specs=[pl.BlockSpec((N_DEV, d), lambda l: (0, 0)), pl.BlockSpec((1, d, wc), lambda l: (l, 0, 0)),
                  pl.BlockSpec((1, 1, wc), lambda l: (l, 0, 0))],
        out_specs=[pl.BlockSpec((N_DEV, d), lambda l: (0, 0)), pl.BlockSpec((1, N_DEV, wc), lambda l: (l, 0, 0))],
        out_shape=[jax.ShapeDtypeStruct((N_DEV, d), F32), jax.ShapeDtypeStruct((nl, N_DEV, wc), F32)],
        name=name, compiler_params=_cp(("arbitrary",)))(c_all, ada_w, ada_b_cols)


def _adam(w, g, m, v):
    m2 = ADAM_B1 * m + (1.0 - ADAM_B1) * g
    v2 = ADAM_B2 * v + (1.0 - ADAM_B2) * (g * g)
    m_hat = m2 / (1.0 - ADAM_B1 ** ADAM_STEP)
    v_hat = v2 / (1.0 - ADAM_B2 ** ADAM_STEP)
    delta = -ADAM_LR * (m_hat / (jnp.sqrt(v_hat) + ADAM_EPS) + ADAM_WD * w)
    return delta, m2, v2


def _ada_grad_adam(c_act, d_mod, w, m, v, name):
    nl, d, wc = w.shape
    tk = min(512, d)

    def body(c_ref, dm_ref, w_ref, m_ref, v_ref, g_ref, dl_ref, m2_ref, v2_ref):
        g = _dotf_tn(c_ref[...], dm_ref[0])
        delta, m2, v2 = _adam(w_ref[0], g, m_ref[0], v_ref[0])
        g_ref[0], dl_ref[0], m2_ref[0], v2_ref[0] = g, delta, m2, v2

    blk = pl.BlockSpec((1, tk, wc), lambda l, i: (l, i, 0))
    shp = jax.ShapeDtypeStruct(w.shape, F32)
    return pl.pallas_call(
        body, grid=(nl, d // tk),
        in_specs=[pl.BlockSpec((N_DEV, tk), lambda l, i: (0, i)), pl.BlockSpec((1, N_DEV, wc), lambda l, i: (l, 0, 0)),
                  blk, blk, blk],
        out_specs=[blk] * 4, out_shape=[shp] * 4, name=name,
        compiler_params=_cp(("parallel", "parallel")))(c_act, d_mod, w, m, v)


def _adam_big(recv, w, m, v, layer, prev, name):
    nl, r, c = w.shape
    tr = 256 if r % 256 == 0 else r

    def body(rc_ref, w_ref, m_ref, v_ref, *rest):
        g_ref, dl_ref, m2_ref, v2_ref = rest[-4:]
        g = rc_ref[0].astype(F32)
        for d in range(1, N_DEV):
            g = g + rc_ref[d].astype(F32)
        delta, m2, v2 = _adam(w_ref[0], g, m_ref[0], v_ref[0])
        g_ref[0], dl_ref[0], m2_ref[0], v2_ref[0] = g, delta, m2, v2

    blk = pl.BlockSpec((1, tr, c), lambda i: (layer, i, 0))
    shp = jax.ShapeDtypeStruct(w.shape, F32)
    prev = () if prev is None else tuple(prev)
    return pl.pallas_call(
        body, grid=(r // tr,),
        in_specs=[pl.BlockSpec((N_DEV, tr, c), lambda i: (0, i, 0)), blk, blk, blk]
        + [pl.BlockSpec(memory_space=pl.ANY)] * len(prev),
        out_specs=[blk] * 4, out_shape=[shp] * 4, name=name,
        input_output_aliases={4 + j: j for j in range(len(prev))},
        compiler_params=_cp(("parallel",)))(recv, w, m, v, *prev)


def _sum_devices(gathered, name):
    _, r, c = gathered.shape

    def body(g_ref, o_ref):
        acc = g_ref[0]
        for d in range(1, N_DEV):
            acc = acc + g_ref[d]
        o_ref[...] = acc

    spec = pl.BlockSpec(memory_space=pltpu.VMEM)
    return pl.pallas_call(body, in_specs=[spec], out_specs=spec, out_shape=jax.ShapeDtypeStruct((r, c), F32),
                          name=name, compiler_params=pltpu.CompilerParams(vmem_limit_bytes=VMEM_LIMIT))(gathered)


def _adam_small(w, g, m, v, name):
    def body(w_ref, g_ref, m_ref, v_ref, dl_ref, m2_ref, v2_ref):
        dl_ref[...], m2_ref[...], v2_ref[...] = _adam(w_ref[...], g_ref[...], m_ref[...], v_ref[...])

    spec = pl.BlockSpec(memory_space=pltpu.VMEM)
    shp = jax.ShapeDtypeStruct(w.shape, F32)
    return pl.pallas_call(body, in_specs=[spec] * 4, out_specs=[spec] * 3, out_shape=[shp] * 3, name=name,
                          compiler_params=pltpu.CompilerParams(vmem_limit_bytes=VMEM_LIMIT))(w, g, m, v)


def _pack(parts):
    flat = jnp.concatenate([p.reshape(-1).astype(F32) for p in parts])
    assert flat.shape[0] % LANES == 0, flat.shape
    return flat.reshape(-1, LANES)


def _unpack(packed, shapes):
    flat = packed.reshape(-1)
    out, off = [], 0
    for shp in shapes:
        size = 1
        for dim in shp:
            size *= dim
        out.append(flat[off:off + size].reshape(shp))
        off += size
    return out


def _gather_cols(g, per):
    g = jnp.moveaxis(g, 0, -2)
    return g.reshape(g.shape[:-2] + (N_DEV * per,))


def _scatter_cols(g, per):
    return jnp.moveaxis(g.reshape(g.shape[:-1] + (N_DEV, per)), -2, 0)


def _my_cols(full, me, per):
    return lax.dynamic_slice_in_dim(full, me * per, per, axis=full.ndim - 1)


def kernel(x, c, positions, ada_w, ada_b, norm_g, w_in, gla_wg_f, gla_bg_f, gla_wg_b, gla_bg_b, gla_norm_g, mla_q_norm_g, mla_kv_norm_g, mla_w_uq, mla_w_ukv, mla_out_g, conv_w, conv_out_g, w_out, final_g, loss_target, m_ada_w, m_ada_b, m_norm_g, m_w_in, m_gla_wg_f, m_gla_bg_f, m_gla_wg_b, m_gla_bg_b, m_gla_norm_g, m_mla_q_norm_g, m_mla_kv_norm_g, m_mla_w_uq, m_mla_w_ukv, m_mla_out_g, m_conv_w, m_conv_out_g, m_w_out, m_final_g, v_ada_w, v_ada_b, v_norm_g, v_w_in, v_gla_wg_f, v_gla_bg_f, v_gla_wg_b, v_gla_bg_b, v_gla_norm_g, v_mla_q_norm_g, v_mla_kv_norm_g, v_mla_w_uq, v_mla_w_ukv, v_mla_out_g, v_conv_w, v_conv_out_g, v_w_out, v_final_g):
    me = 4 * lax.axis_index("x") + 2 * lax.axis_index("y") + lax.axis_index("c")
    nl = ada_w.shape[0]
    s, d = x.shape[1], x.shape[2]
    ada_cols = ada_w.shape[2]
    wgc, cwc = gla_wg_f.shape[2], conv_w.shape[2]

    (g0,) = _exchange([_pack([c, gla_wg_f, gla_wg_b, conv_w])], "gather_small_in", False, pltpu.VMEM)
    g0 = g0.reshape(N_DEV, -1)
    o1, o2, o3 = d, d + gla_wg_f.size, d + 2 * gla_wg_f.size
    c_all = g0[:, :o1]
    wgf_full = _gather_cols(g0[:, o1:o2].reshape((N_DEV,) + gla_wg_f.shape), wgc)
    wgb_full = _gather_cols(g0[:, o2:o3].reshape((N_DEV,) + gla_wg_b.shape), wgc)
    convw_full = _gather_cols(g0[:, o3:].reshape((N_DEV,) + conv_w.shape), cwc)

    ada_b_cols = _my_cols(ada_b, me, ada_cols).reshape(nl, 1, ada_cols)
    c_act, mod_cols = _ada_mod(c_all, ada_w, ada_b_cols, "ada_mod")
    (g1,) = _exchange([_pack([mod_cols])], "gather_mod", False, pltpu.VMEM)
    mod_all = g1.reshape(N_DEV, nl, N_DEV, ada_cols)
    mod_mine = _gather_cols(lax.dynamic_index_in_dim(mod_all, me, axis=2, keepdims=False), ada_cols)

    inv_freq = ROPE_THETA ** (-jnp.arange(0, MROPE, 2, dtype=F32) / MROPE)
    ang = positions[0].astype(F32)[:, None] * inv_freq
    cos, sin = jnp.tile(jnp.cos(ang), (1, LANES * 2 // MROPE)), jnp.tile(jnp.sin(ang), (1, LANES * 2 // MROPE))

    big = [w_in, w_out, mla_w_uq, mla_w_ukv]
    big_names = ["w_in", "w_out", "mla_w_uq", "mla_w_ukv"]

    def local_blocks(l):
        return [w[l].astype(MXU) for w in big]

    def put_own(lands, own):
        return [lax.dynamic_update_index_in_dim(ld, o, me, 0) for ld, o in zip(lands, own)]

    def layer_weights(l, gathered):
        gw_in, gw_out, gw_uq, gw_ukv = gathered
        small = dict(norm_g=norm_g[l], gla_wg_f=wgf_full[l], gla_bg_f=gla_bg_f[l], gla_wg_b=wgb_full[l],
                     gla_bg_b=gla_bg_b[l], gla_norm_g=gla_norm_g[l], mla_q_norm_g=mla_q_norm_g[l],
                     mla_kv_norm_g=mla_kv_norm_g[l], mla_out_g=mla_out_g[l], conv_w=convw_full[l],
                     conv_out_g=conv_out_g[l])
        return _prep_layer_weights(_gather_cols(gw_in, w_in.shape[2]), gw_out.reshape((-1,) + gw_out.shape[2:]),
                                   _gather_cols(gw_uq, mla_w_uq.shape[2]), _gather_cols(gw_ukv, mla_w_ukv.shape[2]),
                                   small)

    gathered = _exchange(local_blocks(0), "gather_weights_l0", False, pltpu.HBM)
    h = x[0]
    saved, layers, mods = [], [], []
    for l in range(nl):
        shift, scale, gate = (mod_mine[l, i * d:(i + 1) * d].reshape(1, d) for i in range(3))
        if l + 1 < nl:
            nxt = local_blocks(l + 1)
            pending = _exchange_start(nxt, gathered[0], f"gather_weights_l{l + 1}_start", False)
            scale = scale + pending[-1][0, 0]
        layers.append(layer_weights(l, gathered))
        mods.append((shift, scale, gate))
        h, sv = _layer_fwd(h, mods[l], layers[l], cos, sin, f"l{l}")
        saved.append(sv)
        if l + 1 < nl:
            gathered = put_own(_exchange_wait(pending, [h], f"gather_weights_l{l + 1}_wait", False), nxt)
    loss_part, d_h, d_final_g = _final_loss(h, final_g.reshape(1, d), loss_target[0], "final_loss")
    loss = lax.psum(loss_part[0, 0], ("x", "y", "c"))

    def grad_sends(g):
        return [_scatter_cols(g["w_in"], w_in.shape[2]).astype(MXU),
                g["w_out"].reshape((N_DEV,) + w_out.shape[1:]).astype(MXU),
                _scatter_cols(g["mla_w_uq"], mla_w_uq.shape[2]).astype(MXU),
                _scatter_cols(g["mla_w_ukv"], mla_w_ukv.shape[2]).astype(MXU)]

    def own_slots(sends):
        return [lax.dynamic_index_in_dim(sd, me, axis=0, keepdims=False) for sd in sends]

    d_mods, grads, recv = [None] * nl, [None] * nl, [None] * nl
    pending, sends = None, None
    for l in reversed(range(nl)):
        shift, scale, gate = mods[l]
        if pending is not None:
            gate = gate + pending[-1][0, 0]
        d_h, d_mods[l], gr = _layer_bwd(d_h, saved[l], (shift, scale, gate), layers[l], cos, sin, f"l{l}")
        grads[l] = _natural_grads(gr)
        if pending is not None:
            recv[l + 1] = put_own(_exchange_wait(pending, [d_h], f"scatter_grads_l{l + 1}_wait", True), own_slots(sends))
        sends = grad_sends(grads[l])
        pending = _exchange_start(sends, d_h if l + 1 == nl else recv[l + 1][0], f"scatter_grads_l{l}_start", True)
    grad_x = d_h[None]

    def stacked(name):
        return jnp.stack([grads[l][name] for l in range(nl)])

    small_names = ["norm_g", "gla_wg_f", "gla_bg_f", "gla_wg_b", "gla_bg_b", "gla_norm_g", "mla_q_norm_g",
                   "mla_kv_norm_g", "mla_out_g", "conv_w", "conv_out_g"]
    d_mod_mine = jnp.stack([jnp.concatenate(d_mods[l], axis=-1)[0] for l in range(nl)])
    parts = [d_mod_mine] + [stacked(n) for n in small_names] + [d_final_g]
    shapes = [p.shape for p in parts]
    (g2,) = _exchange([_pack(parts)], "gather_small_grads", False, pltpu.VMEM)
    d_mod_all = g2.reshape(N_DEV, -1)[:, :d_mod_mine.size].reshape(N_DEV, nl, 3 * d)
    summed = dict(zip(["ada_b"] + small_names + ["final_g"], _unpack(_sum_devices(g2, "sum_small_grads"), shapes)))
    summed["gla_wg_f"] = _my_cols(summed["gla_wg_f"], me, wgc)
    summed["gla_wg_b"] = _my_cols(summed["gla_wg_b"], me, wgc)
    summed["conv_w"] = _my_cols(summed["conv_w"], me, cwc)

    d_mod_cols = jnp.moveaxis(_my_cols(d_mod_all, me, ada_cols), 0, 1)
    out = {}
    out["ada_w"] = _ada_grad_adam(c_act, d_mod_cols, ada_w, m_ada_w, v_ada_w, "ada_grad_adam")

    given = dict(ada_b=(ada_b, m_ada_b, v_ada_b), norm_g=(norm_g, m_norm_g, v_norm_g),
                 gla_wg_f=(gla_wg_f, m_gla_wg_f, v_gla_wg_f), gla_bg_f=(gla_bg_f, m_gla_bg_f, v_gla_bg_f),
                 gla_wg_b=(gla_wg_b, m_gla_wg_b, v_gla_wg_b), gla_bg_b=(gla_bg_b, m_gla_bg_b, v_gla_bg_b),
                 gla_norm_g=(gla_norm_g, m_gla_norm_g, v_gla_norm_g),
                 mla_q_norm_g=(mla_q_norm_g, m_mla_q_norm_g, v_mla_q_norm_g),
                 mla_kv_norm_g=(mla_kv_norm_g, m_mla_kv_norm_g, v_mla_kv_norm_g),
                 mla_out_g=(mla_out_g, m_mla_out_g, v_mla_out_g), conv_w=(conv_w, m_conv_w, v_conv_w),
                 conv_out_g=(conv_out_g, m_conv_out_g, v_conv_out_g), final_g=(final_g, m_final_g, v_final_g))
    names = list(given)
    wshapes = [given[n][0].shape for n in names]
    packed = [_pack([given[n][i] for n in names]) for i in range(3)]
    g_small = _pack([summed[n].reshape(given[n][0].shape) for n in names])
    res = _adam_small(packed[0], g_small, packed[1], packed[2], "adam_small")
    unpacked = [_unpack(r, wshapes) for r in res]
    for i, n in enumerate(names):
        out[n] = (summed[n].reshape(given[n][0].shape), unpacked[0][i], unpacked[1][i], unpacked[2][i])

    moments = dict(w_in=(m_w_in, v_w_in), w_out=(m_w_out, v_w_out), mla_w_uq=(m_mla_w_uq, v_mla_w_uq),
                   mla_w_ukv=(m_mla_w_ukv, v_mla_w_ukv))
    done = [out["ada_w"][0], res[0]]
    for l in reversed(range(nl)):
        if l == 0:
            recv[0] = put_own(_exchange_wait(pending, done, "scatter_grads_l0_wait", True), own_slots(sends))
        for i, n in enumerate(big_names):
            out[n] = _adam_big(recv[l][i], big[i], moments[n][0], moments[n][1], l, out.get(n), f"adam_{n}_l{l}")
        done = done + [out[n][0] for n in big_names]

    order = ["ada_w", "ada_b", "norm_g", "w_in", "gla_wg_f", "gla_bg_f", "gla_wg_b", "gla_bg_b", "gla_norm_g",
             "mla_q_norm_g", "mla_kv_norm_g", "mla_w_uq", "mla_w_ukv", "mla_out_g", "conv_w", "conv_out_g", "w_out",
             "final_g"]
    return (loss, grad_x, *[out[n][0] for n in order], *[out[n][1] for n in order], *[out[n][2] for n in order],
            *[out[n][3] for n in order])
```

```python
import functools

import jax
import jax.numpy as jnp
from jax import lax
from jax.experimental import pallas as pl
from jax.experimental.pallas import tpu as pltpu

F32 = jnp.float32
MXU = jnp.bfloat16
HI = lax.Precision.HIGHEST
N_DEV = 8
MESH = pl.DeviceIdType.MESH

D_MIX = 2048
GH, GDK, GDV = 6, 64, 128
GW = GH * GDV
GQK = GH * GDK
GRANK = 16
GTEMP = 16.0
CHUNK = 64
MH, MQL, MKVL, MNOPE, MROPE, MDV = 6, 384, 256, 128, 64, 128
MW = MH * MDV
MQW = MH * (MNOPE + MROPE)
MKVW = MH * (MNOPE + MDV)
CONV_CH = 512
ROPE_THETA = 10000.0
EPS = 1e-6
IN_DIM = 5856
OZ, OCB, OCC, OCX, OMKV, OGV, OGQ, OGK, OMQ, OT = 0, 2048, 2560, 3072, 3584, 3840, 4608, 4992, 5376, 5760
PW = 5888
LANES = 128
VMEM_LIMIT = 56 * 1024 * 1024

ADAM_LR, ADAM_B1, ADAM_B2, ADAM_EPS, ADAM_WD, ADAM_STEP = 0.001, 0.9, 0.999, 1e-08, 0.01, 10


def _cp(sem=None):
    return pltpu.CompilerParams(dimension_semantics=sem, vmem_limit_bytes=VMEM_LIMIT)


def _dot(a, b):
    return jnp.dot(a.astype(MXU), b.astype(MXU), preferred_element_type=F32)


def _dot_nt(a, b):
    return lax.dot_general(a.astype(MXU), b.astype(MXU), (((1,), (1,)), ((), ())), preferred_element_type=F32)


def _dot_tn(a, b):
    return lax.dot_general(a.astype(MXU), b.astype(MXU), (((0,), (0,)), ((), ())), preferred_element_type=F32)


def _dotf(a, b):
    return jnp.dot(a, b, precision=HI, preferred_element_type=F32)


def _dotf_nt(a, b):
    return lax.dot_general(a, b, (((1,), (1,)), ((), ())), precision=HI, preferred_element_type=F32)


def _dotf_tn(a, b):
    return lax.dot_general(a, b, (((0,), (0,)), ((), ())), precision=HI, preferred_element_type=F32)


def _rows(s):
    return min(256, s)


def _rms(x, g):
    r = lax.rsqrt(jnp.mean(x * x, axis=-1, keepdims=True) + EPS)
    return x * r * g


def _rms_bwd(dy, x, g):
    r = lax.rsqrt(jnp.mean(x * x, axis=-1, keepdims=True) + EPS)
    xh = x * r
    dxh = dy * g
    dg = jnp.sum(dy * xh, axis=0, keepdims=True)
    dx = r * (dxh - xh * jnp.mean(dxh * xh, axis=-1, keepdims=True))
    return dx, dg


def _sigmoid(z):
    return 1.0 / (1.0 + jnp.exp(-z))


def _matmul(a, b, *, dims, tm, tn, tk, out_dtypes, name, epilogue=None, extras=(), extra_kinds=()):
    if dims == "nn":
        (m, k), n, mul = a.shape, b.shape[1], _dot
    elif dims == "nt":
        (m, k), n, mul = a.shape, b.shape[0], _dot_nt
    else:
        (k, m), n, mul = a.shape, b.shape[1], _dot_tn
    tm, tn, tk = min(tm, m), min(tn, n), min(tk, k)
    assert m % tm == 0 and n % tn == 0 and k % tk == 0, (m, n, k, tm, tn, tk)
    if dims == "nn":
        a_spec = pl.BlockSpec((tm, tk), lambda i, j, kk: (i, kk))
        b_spec = pl.BlockSpec((tk, tn), lambda i, j, kk: (kk, j))
    elif dims == "nt":
        a_spec = pl.BlockSpec((tm, tk), lambda i, j, kk: (i, kk))
        b_spec = pl.BlockSpec((tn, tk), lambda i, j, kk: (j, kk))
    else:
        a_spec = pl.BlockSpec((tk, tm), lambda i, j, kk: (kk, i))
        b_spec = pl.BlockSpec((tk, tn), lambda i, j, kk: (kk, j))
    nk = k // tk
    n_extra = len(extras)
    n_out = len(out_dtypes)
    extra_specs = []
    for kind in extra_kinds:
        if kind == "mn":
            extra_specs.append(pl.BlockSpec((tm, tn), lambda i, j, kk: (i, j)))
        else:
            extra_specs.append(pl.BlockSpec((1, tn), lambda i, j, kk: (0, j)))

    def body(*refs):
        a_ref, b_ref = refs[0], refs[1]
        ex = refs[2:2 + n_extra]
        outs = refs[2 + n_extra:2 + n_extra + n_out]
        acc = refs[-1]
        kk = pl.program_id(2)

        @pl.when(kk == 0)
        def _():
            acc[...] = jnp.zeros_like(acc)

        acc[...] += mul(a_ref[...], b_ref[...])

        @pl.when(kk == nk - 1)
        def _():
            res = acc[...]
            vals = (res,) if epilogue is None else epilogue(res, *[e[...] for e in ex])
            for o, v in zip(outs, vals):
                o[...] = v.astype(o.dtype)

    out_spec = pl.BlockSpec((tm, tn), lambda i, j, kk: (i, j))
    res = pl.pallas_call(
        body, grid=(m // tm, n // tn, nk),
        in_specs=[a_spec, b_spec] + extra_specs,
        out_specs=[out_spec] * n_out,
        out_shape=[jax.ShapeDtypeStruct((m, n), dt) for dt in out_dtypes],
        scratch_shapes=[pltpu.VMEM((tm, tn), F32)],
        name=name, compiler_params=_cp(("parallel", "parallel", "arbitrary")),
    )(a, b, *extras)
    return res


def _norm_mod(x, g, scale, shift, name):
    s, d = x.shape
    tr = _rows(s)

    def body(x_ref, g_ref, sc_ref, sh_ref, h_ref):
        h = _rms(x_ref[...], g_ref[...]) * (1.0 + sc_ref[...]) + sh_ref[...]
        h_ref[...] = h.astype(h_ref.dtype)

    row = pl.BlockSpec((tr, d), lambda i: (i, 0))
    vec = pl.BlockSpec((1, d), lambda i: (0, 0))
    return pl.pallas_call(body, grid=(s // tr,), in_specs=[row, vec, vec, vec], out_specs=row,
                          out_shape=jax.ShapeDtypeStruct((s, d), MXU), name=name,
                          compiler_params=_cp(("parallel",)))(x, g, scale, shift)


def _norm_mod_bwd(d_h, x, d_out, g, scale, name):
    s, d = x.shape
    tr = _rows(s)

    def body(dh_ref, x_ref, do_ref, g_ref, sc_ref, dx_ref, dsh_ref, dsc_ref, dg_ref):
        i = pl.program_id(0)

        @pl.when(i == 0)
        def _():
            dsh_ref[...] = jnp.zeros_like(dsh_ref)
            dsc_ref[...] = jnp.zeros_like(dsc_ref)
            dg_ref[...] = jnp.zeros_like(dg_ref)

        dh = dh_ref[...]
        xv = x_ref[...]
        gv = g_ref[...]
        r = lax.rsqrt(jnp.mean(xv * xv, axis=-1, keepdims=True) + EPS)
        xh = xv * r
        dsh_ref[...] += jnp.sum(dh, axis=0, keepdims=True)
        dsc_ref[...] += jnp.sum(dh * (xh * gv), axis=0, keepdims=True)
        dhn = dh * (1.0 + sc_ref[...])
        dg_ref[...] += jnp.sum(dhn * xh, axis=0, keepdims=True)
        dxh = dhn * gv
        dx_ref[...] = do_ref[...] + r * (dxh - xh * jnp.mean(dxh * xh, axis=-1, keepdims=True))

    row = pl.BlockSpec((tr, d), lambda i: (i, 0))
    vec = pl.BlockSpec((1, d), lambda i: (0, 0))
    vshape = jax.ShapeDtypeStruct((1, d), F32)
    return pl.pallas_call(body, grid=(s // tr,), in_specs=[row, row, row, vec, vec],
                          out_specs=[row, vec, vec, vec],
                          out_shape=[jax.ShapeDtypeStruct((s, d), F32), vshape, vshape, vshape],
                          name=name, compiler_params=_cp(("arbitrary",)))(d_h, x, d_out, g, scale)


def _gate_bwd(d_out, u, gate, name):
    s, d = d_out.shape
    tr = _rows(s)

    def body(do_ref, u_ref, gt_ref, du_ref, dgt_ref):
        @pl.when(pl.program_id(0) == 0)
        def _():
            dgt_ref[...] = jnp.zeros_like(dgt_ref)

        do = do_ref[...]
        du_ref[...] = (do * gt_ref[...]).astype(du_ref.dtype)
        dgt_ref[...] += jnp.sum(do * u_ref[...], axis=0, keepdims=True)

    row = pl.BlockSpec((tr, d), lambda i: (i, 0))
    vec = pl.BlockSpec((1, d), lambda i: (0, 0))
    return pl.pallas_call(body, grid=(s // tr,), in_specs=[row, row, vec], out_specs=[row, vec],
                          out_shape=[jax.ShapeDtypeStruct((s, d), MXU), jax.ShapeDtypeStruct((1, d), F32)],
                          name=name, compiler_params=_cp(("arbitrary",)))(d_out, u, gate)


def _final_loss(x, g, target, name):
    s, d = x.shape
    tr = _rows(s)

    def body(x_ref, g_ref, t_ref, loss_ref, dx_ref, dg_ref):
        @pl.when(pl.program_id(0) == 0)
        def _():
            loss_ref[...] = jnp.zeros_like(loss_ref)
            dg_ref[...] = jnp.zeros_like(dg_ref)

        xv = x_ref[...]
        gv = g_ref[...]
        diff = _rms(xv, gv) - t_ref[...]
        part = 0.5 * jnp.sum(jnp.sum(diff * diff, axis=-1, keepdims=True) / d, axis=0, keepdims=True)
        loss_ref[...] += jnp.broadcast_to(part, loss_ref.shape)
        dx, dg = _rms_bwd(diff / d, xv, gv)
        dx_ref[...] = dx
        dg_ref[...] += dg

    row = pl.BlockSpec((tr, d), lambda i: (i, 0))
    vec = pl.BlockSpec((1, d), lambda i: (0, 0))
    lvec = pl.BlockSpec((1, LANES), lambda i: (0, 0))
    return pl.pallas_call(body, grid=(s // tr,), in_specs=[row, vec, row], out_specs=[lvec, row, vec],
                          out_shape=[jax.ShapeDtypeStruct((1, LANES), F32), jax.ShapeDtypeStruct((s, d), F32),
                                     jax.ShapeDtypeStruct((1, d), F32)],
                          name=name, compiler_params=_cp(("arbitrary",)))(x, g, target)


def _shift_rows(u, s, down):
    ri = lax.broadcasted_iota(jnp.int32, u.shape, 0)
    if down:
        return jnp.where(ri == 0, 0.0, pltpu.roll(u, 1, 0))
    return jnp.where(ri == s - 1, 0.0, pltpu.roll(u, s - 1, 0))


def _conv_fwd(proj, conv_w, name):
    s = proj.shape[0]
    nt = CONV_CH // LANES

    def body(cb_ref, cc_ref, cx_ref, w_ref, pre_ref):
        u = cc_ref[...] * cx_ref[...]
        conv = _shift_rows(u, s, True) * w_ref[0:1, :] + u * w_ref[1:2, :] + _shift_rows(u, s, False) * w_ref[2:3, :]
        pre_ref[...] = cb_ref[...] * conv

    def col(off):
        return pl.BlockSpec((s, LANES), lambda j: (0, off // LANES + j))

    return pl.pallas_call(body, grid=(nt,), in_specs=[col(OCB), col(OCC), col(OCX), pl.BlockSpec((3, LANES), lambda j: (0, j))],
                          out_specs=pl.BlockSpec((s, LANES), lambda j: (0, j)),
                          out_shape=jax.ShapeDtypeStruct((s, CONV_CH), F32), name=name,
                          compiler_params=_cp(("parallel",)))(proj, proj, proj, conv_w)


def _conv_bwd(proj, conv_w, d_pre, name):
    s = proj.shape[0]
    nt = CONV_CH // LANES

    def body(cb_ref, cc_ref, cx_ref, w_ref, dp_ref, dcb_ref, dcc_ref, dcx_ref, dw_ref):
        cc, cx = cc_ref[...], cx_ref[...]
        u = cc * cx
        up, dn = _shift_rows(u, s, True), _shift_rows(u, s, False)
        w0, w1, w2 = w_ref[0:1, :], w_ref[1:2, :], w_ref[2:3, :]
        conv = up * w0 + u * w1 + dn * w2
        dp = dp_ref[...]
        dcb_ref[...] = dp * conv
        dconv = dp * cb_ref[...]
        du = _shift_rows(dconv, s, False) * w0 + dconv * w1 + _shift_rows(dconv, s, True) * w2
        dcc_ref[...] = du * cx
        dcx_ref[...] = du * cc
        dw_ref[0:1, :] = jnp.sum(dconv * up, axis=0, keepdims=True)
        dw_ref[1:2, :] = jnp.sum(dconv * u, axis=0, keepdims=True)
        dw_ref[2:3, :] = jnp.sum(dconv * dn, axis=0, keepdims=True)

    def col(off):
        return pl.BlockSpec((s, LANES), lambda j: (0, off // LANES + j))

    blk = pl.BlockSpec((s, LANES), lambda j: (0, j))
    wblk = pl.BlockSpec((3, LANES), lambda j: (0, j))
    full = jax.ShapeDtypeStruct((s, CONV_CH), F32)
    return pl.pallas_call(body, grid=(nt,), in_specs=[col(OCB), col(OCC), col(OCX), wblk, blk],
                          out_specs=[blk, blk, blk, wblk],
                          out_shape=[full, full, full, jax.ShapeDtypeStruct((3, CONV_CH), F32)],
                          name=name, compiler_params=_cp(("parallel",)))(proj, proj, proj, conv_w, d_pre)


GLA_SUB = 4


def _gla_gates(t_ref, wg_ref, bg_ref):
    t = t_ref[...]
    a = _dotf(t, wg_ref[...]) + bg_ref[...]
    la = (jnp.minimum(a, 0.0) - jnp.log(1.0 + jnp.exp(-jnp.abs(a)))) / GTEMP
    return t, a, la


def _gla_masks(reverse):
    ri = lax.broadcasted_iota(jnp.int32, (CHUNK, CHUNK), 0)
    ci = lax.broadcasted_iota(jnp.int32, (CHUNK, CHUNK), 1)
    if reverse:
        cum, mask = ci >= ri, ci > ri
    else:
        cum, mask = ci <= ri, ci <= ri
    return cum.astype(F32), mask


def _gla_specs(s, reverse):
    nsub = min(GLA_SUB, s // CHUNK)
    nsteps = s // (CHUNK * nsub)

    def row(n):
        return nsteps - 1 - n if reverse else n

    def chunk(pi):
        return nsub - 1 - pi if reverse else pi

    return nsub, nsteps, row, chunk


def _gla_fwd(proj, wg_pad, bg, reverse, name):
    s = proj.shape[0]
    nsub, nsteps, row, chunk = _gla_specs(s, reverse)
    rb = nsub * CHUNK

    def body(q_ref, k_ref, v_ref, t_ref, wg_ref, bg_ref, o_ref, st_ref, state):
        @pl.when(pl.program_id(0) == 0)
        def _():
            state[...] = jnp.zeros_like(state)

        _, _, la = _gla_gates(t_ref, wg_ref, bg_ref)
        cumf, mask = _gla_masks(reverse)
        lane = lax.broadcasted_iota(jnp.int32, (CHUNK, LANES), 1)
        for pi in range(nsub):
            rows = slice(chunk(pi) * CHUNK, (chunk(pi) + 1) * CHUNK)
            la_c = la[rows]
            b = _dotf(cumf, la_c)
            bl = jnp.sum(la_c, axis=0, keepdims=True)
            q = q_ref[rows, :] * (GDK ** -0.5)
            k = k_ref[rows, :]
            qd = q * jnp.exp(b)
            ki = k * jnp.exp(-b)
            kte = k * jnp.exp(bl - b)
            decay = jnp.exp(bl)
            for h in range(GH):
                p = h // 2
                sl = slice(p * LANES, (p + 1) * LANES)
                lm = (lane < GDK) if h % 2 == 0 else (lane >= GDK)
                qd_h = jnp.where(lm, qd[:, sl], 0.0)
                kte_h = jnp.where(lm, kte[:, sl], 0.0)
                v_h = v_ref[rows, h * GDV:(h + 1) * GDV]
                st = state[h]
                a_mat = jnp.where(mask, _dot_nt(qd_h, ki[:, sl]), 0.0)
                o_ref[rows, h * GDV:(h + 1) * GDV] = _dot(a_mat, v_h) + _dot_nt(qd_h, st)
                st_ref[pi, h] = st
                state[h] = st * decay[:, sl] + _dot_tn(v_h, kte_h)

    return pl.pallas_call(
        body, grid=(nsteps,),
        in_specs=[pl.BlockSpec((rb, GQK), lambda n: (row(n), OGQ // GQK)),
                  pl.BlockSpec((rb, GQK), lambda n: (row(n), OGK // GQK)),
                  pl.BlockSpec((rb, GW), lambda n: (row(n), OGV // GW)),
                  pl.BlockSpec((rb, LANES), lambda n: (row(n), OT // LANES)),
                  pl.BlockSpec((LANES, GQK), lambda n: (0, 0)),
                  pl.BlockSpec((1, GQK), lambda n: (0, 0))],
        out_specs=[pl.BlockSpec((rb, GW), lambda n: (row(n), 0)),
                   pl.BlockSpec((nsub, GH, GDV, LANES), lambda n: (n, 0, 0, 0))],
        out_shape=[jax.ShapeDtypeStruct((s, GW), F32), jax.ShapeDtypeStruct((s // CHUNK, GH, GDV, LANES), F32)],
        scratch_shapes=[pltpu.VMEM((GH, GDV, LANES), F32)],
        name=name, compiler_params=_cp(("arbitrary",)))(proj, proj, proj, proj, wg_pad, bg)


def _gla_bwd(proj, wg_pad, bg, states, d_o, reverse, name):
    s = proj.shape[0]
    nsub, nsteps, row, chunk = _gla_specs(s, reverse)
    rb = nsub * CHUNK

    def body(q_ref, k_ref, v_ref, t_ref, wg_ref, bg_ref, st_ref, do_ref,
             dq_ref, dk_ref, dv_ref, dt_ref, dwg_ref, dbg_ref, dstate, da_buf):
        @pl.when(pl.program_id(0) == 0)
        def _():
            dstate[...] = jnp.zeros_like(dstate)
            dwg_ref[...] = jnp.zeros_like(dwg_ref)
            dbg_ref[...] = jnp.zeros_like(dbg_ref)

        t, a, la = _gla_gates(t_ref, wg_ref, bg_ref)
        cumf, mask = _gla_masks(reverse)
        lane = lax.broadcasted_iota(jnp.int32, (CHUNK, LANES), 1)
        for pi in reversed(range(nsub)):
            rows = slice(chunk(pi) * CHUNK, (chunk(pi) + 1) * CHUNK)
            la_c = la[rows]
            b = _dotf(cumf, la_c)
            bl = jnp.sum(la_c, axis=0, keepdims=True)
            q = q_ref[rows, :] * (GDK ** -0.5)
            k = k_ref[rows, :]
            e, ei, ee = jnp.exp(b), jnp.exp(-b), jnp.exp(bl - b)
            qd, ki, kte = q * e, k * ei, k * ee
            decay = jnp.exp(bl)
            for p in range(GH // 2):
                sl = slice(p * LANES, (p + 1) * LANES)
                dqd = jnp.zeros((CHUNK, LANES), F32)
                dki = jnp.zeros((CHUNK, LANES), F32)
                dkte = jnp.zeros((CHUNK, LANES), F32)
                ddecay = jnp.zeros((1, LANES), F32)
                for half in range(2):
                    h = 2 * p + half
                    lm = (lane < GDK) if half == 0 else (lane >= GDK)
                    qd_h = jnp.where(lm, qd[:, sl], 0.0)
                    ki_h = jnp.where(lm, ki[:, sl], 0.0)
                    kte_h = jnp.where(lm, kte[:, sl], 0.0)
                    v_h = v_ref[rows, h * GDV:(h + 1) * GDV]
                    do_h = do_ref[rows, h * GDV:(h + 1) * GDV]
                    st = st_ref[pi, h]
                    dst = dstate[h]
                    a_mat = jnp.where(mask, _dot_nt(qd_h, ki_h), 0.0)
                    da_mat = jnp.where(mask, _dot_nt(do_h, v_h), 0.0)
                    dv_ref[rows, h * GDV:(h + 1) * GDV] = _dot_tn(a_mat, do_h) + _dot_nt(kte_h, dst)
                    dqd += _dot(da_mat, ki_h) + _dot(do_h, st)
                    dki += _dot_tn(da_mat, qd_h)
                    dkte += _dot(v_h, dst)
                    ddecay += jnp.sum(dst * st, axis=0, keepdims=True)
                    dstate[h] = dst * decay[:, sl] + _dot_tn(do_h, qd_h)
                dq_ref[rows, sl] = dqd * e[:, sl] * (GDK ** -0.5)
                dk_ref[rows, sl] = dki * ei[:, sl] + dkte * ee[:, sl]
                db = dqd * qd[:, sl] - dki * ki[:, sl] - dkte * kte[:, sl]
                dbl = jnp.sum(dkte * kte[:, sl], axis=0, keepdims=True) + decay[:, sl] * ddecay
                da_buf[rows, sl] = _dotf_tn(cumf, db) + dbl
        da = da_buf[...] * (1.0 / GTEMP) * _sigmoid(-a)
        dt_ref[...] = _dotf_nt(da, wg_ref[...])
        dwg_ref[...] += _dotf_tn(t, da)
        dbg_ref[...] += jnp.sum(da, axis=0, keepdims=True)

    def prow(j):
        return row(nsteps - 1 - j)

    return pl.pallas_call(
        body, grid=(nsteps,),
        in_specs=[pl.BlockSpec((rb, GQK), lambda j: (prow(j), OGQ // GQK)),
                  pl.BlockSpec((rb, GQK), lambda j: (prow(j), OGK // GQK)),
                  pl.BlockSpec((rb, GW), lambda j: (prow(j), OGV // GW)),
                  pl.BlockSpec((rb, LANES), lambda j: (prow(j), OT // LANES)),
                  pl.BlockSpec((LANES, GQK), lambda j: (0, 0)),
                  pl.BlockSpec((1, GQK), lambda j: (0, 0)),
                  pl.BlockSpec((nsub, GH, GDV, LANES), lambda j: (nsteps - 1 - j, 0, 0, 0)),
                  pl.BlockSpec((rb, GW), lambda j: (prow(j), 0))],
        out_specs=[pl.BlockSpec((rb, GQK), lambda j: (prow(j), 0)),
                   pl.BlockSpec((rb, GQK), lambda j: (prow(j), 0)),
                   pl.BlockSpec((rb, GW), lambda j: (prow(j), 0)),
                   pl.BlockSpec((rb, LANES), lambda j: (prow(j), 0)),
                   pl.BlockSpec((LANES, GQK), lambda j: (0, 0)),
                   pl.BlockSpec((1, GQK), lambda j: (0, 0))],
        out_shape=[jax.ShapeDtypeStruct((s, GQK), F32), jax.ShapeDtypeStruct((s, GQK), F32),
                   jax.ShapeDtypeStruct((s, GW), F32), jax.ShapeDtypeStruct((s, LANES), F32),
                   jax.ShapeDtypeStruct((LANES, GQK), F32), jax.ShapeDtypeStruct((1, GQK), F32)],
        scratch_shapes=[pltpu.VMEM((GH, GDV, LANES), F32), pltpu.VMEM((rb, GQK), F32)],
        name=name, compiler_params=_cp(("arbitrary",)))(proj, proj, proj, proj, wg_pad, bg, states, d_o)


def _rot_half(x):
    lane = lax.broadcasted_iota(jnp.int32, x.shape, 1)
    first = (lane % MROPE) < (MROPE // 2)
    return jnp.where(first, -pltpu.roll(x, LANES - MROPE // 2, 1), pltpu.roll(x, MROPE // 2, 1))


def _mla_prep(proj, cos, sin, qg, kvg, w_uq, w_ukv, name):
    s = proj.shape[0]
    tr = _rows(s)

    def body(mq_ref, mkv_ref, t_ref, cos_ref, sin_ref, qg_ref, kvg_ref, wuq_ref, wukv_ref, q_ref, k_ref, v_ref):
        cosv, sinv = cos_ref[...], sin_ref[...]
        lane = lax.broadcasted_iota(jnp.int32, (tr, LANES), 1)

        def rope(xv):
            return xv * cosv + _rot_half(xv) * sinv

        qm = _dot(_rms(mq_ref[...], qg_ref[...]), wuq_ref[...])
        kv = _dot(_rms(mkv_ref[...], kvg_ref[...]), wukv_ref[...])
        kr_lo = jnp.where(lane < MROPE, rope(t_ref[...]), 0.0)
        kr_hi = pltpu.roll(kr_lo, MROPE, 1)
        for p in range(MH // 2):
            r = rope(qm[:, MW + p * LANES:MW + (p + 1) * LANES]).astype(q_ref.dtype)
            q_ref[2 * p, :, LANES:] = r
            q_ref[2 * p + 1, :, LANES:] = r
        for h in range(MH):
            q_ref[h, :, :LANES] = qm[:, h * LANES:(h + 1) * LANES].astype(q_ref.dtype)
            k_ref[h, :, :LANES] = kv[:, 2 * h * LANES:(2 * h + 1) * LANES].astype(k_ref.dtype)
            k_ref[h, :, LANES:] = (kr_lo if h % 2 == 0 else kr_hi).astype(k_ref.dtype)
            v_ref[h] = kv[:, (2 * h + 1) * LANES:(2 * h + 2) * LANES].astype(v_ref.dtype)

    def full(shape):
        return pl.BlockSpec(shape, lambda i: (0,) * len(shape))

    return pl.pallas_call(
        body, grid=(s // tr,),
        in_specs=[pl.BlockSpec((tr, MQL), lambda i: (i, OMQ // MQL)),
                  pl.BlockSpec((tr, MKVL), lambda i: (i, OMKV // MKVL)),
                  pl.BlockSpec((tr, LANES), lambda i: (i, OT // LANES)),
                  pl.BlockSpec((tr, LANES), lambda i: (i, 0)),
                  pl.BlockSpec((tr, LANES), lambda i: (i, 0)),
                  full((1, MQL)), full((1, MKVL)), full((MQL, MQW)), full((MKVL, MKVW))],
        out_specs=[pl.BlockSpec((MH, tr, 2 * LANES), lambda i: (0, i, 0)),
                   pl.BlockSpec((MH, tr, 2 * LANES), lambda i: (0, i, 0)),
                   pl.BlockSpec((MH, tr, LANES), lambda i: (0, i, 0))],
        out_shape=[jax.ShapeDtypeStruct((MH, s, 2 * LANES), MXU), jax.ShapeDtypeStruct((MH, s, 2 * LANES), MXU),
                   jax.ShapeDtypeStruct((MH, s, LANES), MXU)],
        name=name, compiler_params=_cp(("parallel",)))(proj, proj, proj, cos, sin, qg, kvg, w_uq, w_ukv)


def _mla_prep_bwd(proj, cos, sin, qg, kvg, w_uq, w_ukv, d_q, d_k, d_v, name):
    s = proj.shape[0]
    tr = _rows(s)

    def body(mq_ref, mkv_ref, cos_ref, sin_ref, qg_ref, kvg_ref, wuq_ref, wukv_ref, dq_ref, dk_ref, dv_ref,
             dmq_ref, dmkv_ref, dt_ref, dwuq_ref, dwukv_ref, dqg_ref, dkvg_ref):
        @pl.when(pl.program_id(0) == 0)
        def _():
            for r in (dwuq_ref, dwukv_ref, dqg_ref, dkvg_ref):
                r[...] = jnp.zeros_like(r)

        cosv, sinv = cos_ref[...], sin_ref[...]
        lane = lax.broadcasted_iota(jnp.int32, (tr, LANES), 1)
        lo = lane < MROPE

        def unrope(dv):
            return dv * cosv - _rot_half(dv * sinv)

        parts = [dq_ref[h, :, :LANES] for h in range(MH)]
        for p in range(MH // 2):
            parts.append(unrope(jnp.where(lo, dq_ref[2 * p, :, LANES:], dq_ref[2 * p + 1, :, LANES:])))
        d_qm = jnp.concatenate(parts, axis=1)
        mq, qgv = mq_ref[...], qg_ref[...]
        cq = _rms(mq, qgv)
        dwuq_ref[...] += _dot_tn(cq, d_qm)
        dmq, dqg = _rms_bwd(_dot_nt(d_qm, wuq_ref[...]), mq, qgv)
        dmq_ref[...] = dmq
        dqg_ref[...] += dqg

        parts = []
        for h in range(MH):
            parts += [dk_ref[h, :, :LANES], dv_ref[h]]
        d_kv = jnp.concatenate(parts, axis=1)
        mkv, kvgv = mkv_ref[...], kvg_ref[...]
        ckv = _rms(mkv, kvgv)
        dwukv_ref[...] += _dot_tn(ckv, d_kv)
        dmkv, dkvg = _rms_bwd(_dot_nt(d_kv, wukv_ref[...]), mkv, kvgv)
        dmkv_ref[...] = dmkv
        dkvg_ref[...] += dkvg

        even = dk_ref[0, :, LANES:] + dk_ref[2, :, LANES:] + dk_ref[4, :, LANES:]
        odd = dk_ref[1, :, LANES:] + dk_ref[3, :, LANES:] + dk_ref[5, :, LANES:]
        d_kr = jnp.where(lo, even, 0.0) + pltpu.roll(jnp.where(lo, 0.0, odd), MROPE, 1)
        dt_ref[...] = jnp.where(lo, unrope(d_kr), 0.0)

    def full(shape):
        return pl.BlockSpec(shape, lambda i: (0,) * len(shape))

    return pl.pallas_call(
        body, grid=(s // tr,),
        in_specs=[pl.BlockSpec((tr, MQL), lambda i: (i, OMQ // MQL)),
                  pl.BlockSpec((tr, MKVL), lambda i: (i, OMKV // MKVL)),
                  pl.BlockSpec((tr, LANES), lambda i: (i, 0)),
                  pl.BlockSpec((tr, LANES), lambda i: (i, 0)),
                  full((1, MQL)), full((1, MKVL)), full((MQL, MQW)), full((MKVL, MKVW)),
                  pl.BlockSpec((MH, tr, 2 * LANES), lambda i: (0, i, 0)),
                  pl.BlockSpec((MH, tr, 2 * LANES), lambda i: (0, i, 0)),
                  pl.BlockSpec((MH, tr, LANES), lambda i: (0, i, 0))],
        out_specs=[pl.BlockSpec((tr, MQL), lambda i: (i, 0)), pl.BlockSpec((tr, MKVL), lambda i: (i, 0)),
                   pl.BlockSpec((tr, LANES), lambda i: (i, 0)),
                   full((MQL, MQW)), full((MKVL, MKVW)), full((1, MQL)), full((1, MKVL))],
        out_shape=[jax.ShapeDtypeStruct((s, MQL), F32), jax.ShapeDtypeStruct((s, MKVL), F32),
                   jax.ShapeDtypeStruct((s, LANES), F32),
                   jax.ShapeDtypeStruct((MQL, MQW), F32), jax.ShapeDtypeStruct((MKVL, MKVW), F32),
                   jax.ShapeDtypeStruct((1, MQL), F32), jax.ShapeDtypeStruct((1, MKVL), F32)],
        name=name, compiler_params=_cp(("arbitrary",)))(proj, proj, cos, sin, qg, kvg, w_uq, w_ukv, d_q, d_k, d_v)


ATT_SCALE = (MNOPE + MROPE) ** -0.5


def _attn_fwd(q, k, v, name):
    s = q.shape[1]
    tq = _rows(s)

    def body(q_ref, k_ref, v_ref, o_ref, lse_ref):
        sc = _dot_nt(q_ref[0], k_ref[0]) * ATT_SCALE
        m = jnp.max(sc, axis=-1, keepdims=True)
        p = jnp.exp(sc - m)
        l = jnp.sum(p, axis=-1, keepdims=True)
        o_ref[...] = _dot(p, v_ref[0]) / l
        lse_ref[0] = m + jnp.log(l)

    return pl.pallas_call(
        body, grid=(MH, s // tq),
        in_specs=[pl.BlockSpec((1, tq, 2 * LANES), lambda h, i: (h, i, 0)),
                  pl.BlockSpec((1, s, 2 * LANES), lambda h, i: (h, 0, 0)),
                  pl.BlockSpec((1, s, LANES), lambda h, i: (h, 0, 0))],
        out_specs=[pl.BlockSpec((tq, LANES), lambda h, i: (i, h)),
                   pl.BlockSpec((1, tq, 1), lambda h, i: (h, i, 0))],
        out_shape=[jax.ShapeDtypeStruct((s, MW), F32), jax.ShapeDtypeStruct((MH, s, 1), F32)],
        name=name, compiler_params=_cp(("parallel", "parallel")))(q, k, v)


def _attn_bwd(q, k, v, o, lse, d_o, name):
    s = q.shape[1]
    tq = _rows(s)

    def body(q_ref, k_ref, v_ref, o_ref, lse_ref, do_ref, dq_ref, dk_ref, dv_ref):
        @pl.when(pl.program_id(1) == 0)
        def _():
            dk_ref[...] = jnp.zeros_like(dk_ref)
            dv_ref[...] = jnp.zeros_like(dv_ref)

        qv, kv, do = q_ref[0], k_ref[0], do_ref[...]
        p = jnp.exp(_dot_nt(qv, kv) * ATT_SCALE - lse_ref[0])
        delta = jnp.sum(do * o_ref[...], axis=-1, keepdims=True)
        ds = p * (_dot_nt(do, v_ref[0]) - delta) * ATT_SCALE
        dq_ref[0] = _dot(ds, kv)
        dk_ref[0] += _dot_tn(ds, qv)
        dv_ref[0] += _dot_tn(p, do)

    return pl.pallas_call(
        body, grid=(MH, s // tq),
        in_specs=[pl.BlockSpec((1, tq, 2 * LANES), lambda h, i: (h, i, 0)),
                  pl.BlockSpec((1, s, 2 * LANES), lambda h, i: (h, 0, 0)),
                  pl.BlockSpec((1, s, LANES), lambda h, i: (h, 0, 0)),
                  pl.BlockSpec((tq, LANES), lambda h, i: (i, h)),
                  pl.BlockSpec((1, tq, 1), lambda h, i: (h, i, 0)),
                  pl.BlockSpec((tq, LANES), lambda h, i: (i, h))],
        out_specs=[pl.BlockSpec((1, tq, 2 * LANES), lambda h, i: (h, i, 0)),
                   pl.BlockSpec((1, s, 2 * LANES), lambda h, i: (h, 0, 0)),
                   pl.BlockSpec((1, s, LANES), lambda h, i: (h, 0, 0))],
        out_shape=[jax.ShapeDtypeStruct((MH, s, 2 * LANES), F32), jax.ShapeDtypeStruct((MH, s, 2 * LANES), F32),
                   jax.ShapeDtypeStruct((MH, s, LANES), F32)],
        name=name, compiler_params=_cp(("parallel", "arbitrary")))(q, k, v, o, lse, d_o)


def _merge_fwd(o_f, o_b, o_att, pre, proj, gng, mog, cog, name):
    s = proj.shape[0]
    tr = _rows(s)

    def body(of_ref, ob_ref, oa_ref, pre_ref, z_ref, gng_ref, mog_ref, cog_ref, y_ref):
        z = z_ref[...]
        sz = z * _sigmoid(z)
        osum = of_ref[...] + ob_ref[...]
        gg = gng_ref[...]
        for h in range(GH):
            sl = slice(h * GDV, (h + 1) * GDV)
            y_ref[:, sl] = (_rms(osum[:, sl], gg) * sz[:, sl]).astype(y_ref.dtype)
        y_ref[:, GW:GW + MW] = (_rms(oa_ref[...], mog_ref[...]) * sz[:, GW:GW + MW]).astype(y_ref.dtype)
        y_ref[:, GW + MW:] = (_rms(pre_ref[...], cog_ref[...]) * sz[:, GW + MW:]).astype(y_ref.dtype)

    def row(w):
        return pl.BlockSpec((tr, w), lambda i: (i, 0))

    def vec(w):
        return pl.BlockSpec((1, w), lambda i: (0, 0))

    return pl.pallas_call(
        body, grid=(s // tr,),
        in_specs=[row(GW), row(GW), row(MW), row(CONV_CH), row(D_MIX), vec(GDV), vec(MW), vec(CONV_CH)],
        out_specs=row(D_MIX), out_shape=jax.ShapeDtypeStruct((s, D_MIX), MXU),
        name=name, compiler_params=_cp(("parallel",)))(o_f, o_b, o_att, pre, proj, gng, mog, cog)


def _merge_bwd(d_y, o_f, o_b, o_att, pre, proj, gng, mog, cog, name):
    s = proj.shape[0]
    tr = _rows(s)

    def body(dy_ref, of_ref, ob_ref, oa_ref, pre_ref, z_ref, gng_ref, mog_ref, cog_ref,
             dz_ref, dos_ref, doa_ref, dpre_ref, dgng_ref, dmog_ref, dcog_ref):
        @pl.when(pl.program_id(0) == 0)
        def _():
            for r in (dgng_ref, dmog_ref, dcog_ref):
                r[...] = jnp.zeros_like(r)

        z, dy = z_ref[...], dy_ref[...]
        sg = _sigmoid(z)
        sz = z * sg
        dsz = sg * (1.0 + z * (1.0 - sg))
        dcat = dy * sz
        dyz = dy * dsz
        osum = of_ref[...] + ob_ref[...]
        gg = gng_ref[...]
        dgg = jnp.zeros_like(gg)
        for h in range(GH):
            sl = slice(h * GDV, (h + 1) * GDV)
            dz_ref[:, sl] = dyz[:, sl] * _rms(osum[:, sl], gg)
            dx, dg = _rms_bwd(dcat[:, sl], osum[:, sl], gg)
            dos_ref[:, sl] = dx
            dgg += dg
        dgng_ref[...] += dgg
        sl = slice(GW, GW + MW)
        oa, mg = oa_ref[...], mog_ref[...]
        dz_ref[:, sl] = dyz[:, sl] * _rms(oa, mg)
        dx, dg = _rms_bwd(dcat[:, sl], oa, mg)
        doa_ref[...] = dx
        dmog_ref[...] += dg
        sl = slice(GW + MW, D_MIX)
        pv, cg = pre_ref[...], cog_ref[...]
        dz_ref[:, sl] = dyz[:, sl] * _rms(pv, cg)
        dx, dg = _rms_bwd(dcat[:, sl], pv, cg)
        dpre_ref[...] = dx
        dcog_ref[...] += dg

    def row(w):
        return pl.BlockSpec((tr, w), lambda i: (i, 0))

    def vec(w):
        return pl.BlockSpec((1, w), lambda i: (0, 0))

    def rs(w):
        return jax.ShapeDtypeStruct((s, w), F32)

    def vs(w):
        return jax.ShapeDtypeStruct((1, w), F32)

    return pl.pallas_call(
        body, grid=(s // tr,),
        in_specs=[row(D_MIX), row(GW), row(GW), row(MW), row(CONV_CH), row(D_MIX), vec(GDV), vec(MW), vec(CONV_CH)],
        out_specs=[row(D_MIX), row(GW), row(MW), row(CONV_CH), vec(GDV), vec(MW), vec(CONV_CH)],
        out_shape=[rs(D_MIX), rs(GW), rs(MW), rs(CONV_CH), vs(GDV), vs(MW), vs(CONV_CH)],
        name=name, compiler_params=_cp(("arbitrary",)))(d_y, o_f, o_b, o_att, pre, proj, gng, mog, cog)


def _assemble_dproj(d_z, d_cb, d_cc, d_cx, d_mkv, dv_f, dv_b, dq_f, dq_b, dk_f, dk_b, d_mq, dt_m, dt_f, dt_b, name):
    s = d_z.shape[0]
    tr = _rows(s)

    def body(dz, dcb, dcc, dcx, dmkv, dvf, dvb, dqf, dqb, dkf, dkb, dmq, dtm, dtf, dtb, out):
        dt = out.dtype
        out[:, OZ:OZ + D_MIX] = dz[...].astype(dt)
        out[:, OCB:OCB + CONV_CH] = dcb[...].astype(dt)
        out[:, OCC:OCC + CONV_CH] = dcc[...].astype(dt)
        out[:, OCX:OCX + CONV_CH] = dcx[...].astype(dt)
        out[:, OMKV:OMKV + MKVL] = dmkv[...].astype(dt)
        out[:, OGV:OGV + GW] = (dvf[...] + dvb[...]).astype(dt)
        out[:, OGQ:OGQ + GQK] = (dqf[...] + dqb[...]).astype(dt)
        out[:, OGK:OGK + GQK] = (dkf[...] + dkb[...]).astype(dt)
        out[:, OMQ:OMQ + MQL] = dmq[...].astype(dt)
        out[:, OT:OT + LANES] = (dtm[...] + dtf[...] + dtb[...]).astype(dt)

    args = (d_z, d_cb, d_cc, d_cx, d_mkv, dv_f, dv_b, dq_f, dq_b, dk_f, dk_b, d_mq, dt_m, dt_f, dt_b)
    return pl.pallas_call(
        body, grid=(s // tr,),
        in_specs=[pl.BlockSpec((tr, a.shape[1]), lambda i: (i, 0)) for a in args],
        out_specs=pl.BlockSpec((tr, PW), lambda i: (i, 0)),
        out_shape=jax.ShapeDtypeStruct((s, PW), MXU), name=name, compiler_params=_cp(("parallel",)))(*args)


def _layer_fwd(x, mod, wt, cos, sin, tag):
    shift, scale, gate = mod
    h = _norm_mod(x, wt["norm_g"], scale, shift, f"norm_mod_{tag}")
    (proj,) = _matmul(h, wt["w_in"], dims="nn", tm=1024, tn=256, tk=2048, out_dtypes=(F32,), name=f"in_proj_{tag}")
    o_f, st_f = _gla_fwd(proj, wt["wg_pad_f"], wt["bg_f"], False, f"gla_fwd_f_{tag}")
    o_b, st_b = _gla_fwd(proj, wt["wg_pad_b"], wt["bg_b"], True, f"gla_fwd_b_{tag}")
    q, k, v = _mla_prep(proj, cos, sin, wt["q_norm_g"], wt["kv_norm_g"], wt["w_uq"], wt["w_ukv"], f"mla_prep_{tag}")
    o_att, lse = _attn_fwd(q, k, v, f"attn_fwd_{tag}")
    pre = _conv_fwd(proj, wt["conv_w"], f"conv_fwd_{tag}")
    y = _merge_fwd(o_f, o_b, o_att, pre, proj, wt["gla_norm_g"], wt["mla_out_g"], wt["conv_out_g"], f"merge_fwd_{tag}")
    x_new, u = _matmul(y, wt["w_out"], dims="nn", tm=1024, tn=512, tk=2048, out_dtypes=(F32, F32),
                       name=f"out_proj_{tag}", epilogue=lambda acc, xv, gv: (xv + gv * acc, acc),
                       extras=(x, gate), extra_kinds=("mn", "n"))
    saved = dict(x=x, h=h, proj=proj, o_f=o_f, o_b=o_b, st_f=st_f, st_b=st_b, q=q, k=k, v=v,
                 o_att=o_att, lse=lse, pre=pre, y=y, u=u)
    return x_new, saved


def _layer_bwd(d_out, sv, mod, wt, cos, sin, tag):
    shift, scale, gate = mod
    proj = sv["proj"]
    d_u, d_gate = _gate_bwd(d_out, sv["u"], gate, f"gate_bwd_{tag}")
    (d_y,) = _matmul(d_u, wt["w_out"], dims="nt", tm=1024, tn=512, tk=2048, out_dtypes=(F32,), name=f"out_proj_dx_{tag}")
    (g_w_out,) = _matmul(sv["y"], d_u, dims="tn", tm=1024, tn=512, tk=2048, out_dtypes=(MXU,), name=f"out_proj_dw_{tag}")
    d_z, d_osum, d_oatt, d_pre, d_gng, d_mog, d_cog = _merge_bwd(
        d_y, sv["o_f"], sv["o_b"], sv["o_att"], sv["pre"], proj, wt["gla_norm_g"], wt["mla_out_g"], wt["conv_out_g"],
        f"merge_bwd_{tag}")
    d_cb, d_cc, d_cx, d_conv_w = _conv_bwd(proj, wt["conv_w"], d_pre, f"conv_bwd_{tag}")
    d_q, d_k, d_v = _attn_bwd(sv["q"], sv["k"], sv["v"], sv["o_att"], sv["lse"], d_oatt, f"attn_bwd_{tag}")
    d_mq, d_mkv, dt_m, g_w_uq, g_w_ukv, d_qg, d_kvg = _mla_prep_bwd(
        proj, cos, sin, wt["q_norm_g"], wt["kv_norm_g"], wt["w_uq"], wt["w_ukv"], d_q, d_k, d_v, f"mla_prep_bwd_{tag}")
    dq_f, dk_f, dv_f, dt_f, d_wg_f, d_bg_f = _gla_bwd(proj, wt["wg_pad_f"], wt["bg_f"], sv["st_f"], d_osum, False,
                                                     f"gla_bwd_f_{tag}")
    dq_b, dk_b, dv_b, dt_b, d_wg_b, d_bg_b = _gla_bwd(proj, wt["wg_pad_b"], wt["bg_b"], sv["st_b"], d_osum, True,
                                                     f"gla_bwd_b_{tag}")
    d_proj = _assemble_dproj(d_z, d_cb, d_cc, d_cx, d_mkv, dv_f, dv_b, dq_f, dq_b, dk_f, dk_b, d_mq, dt_m, dt_f, dt_b,
                             f"assemble_dproj_{tag}")
    (g_w_in,) = _matmul(sv["h"], d_proj, dims="tn", tm=1024, tn=256, tk=2048, out_dtypes=(MXU,), name=f"in_proj_dw_{tag}")
    (d_h,) = _matmul(d_proj, wt["w_in"], dims="nt", tm=1024, tn=512, tk=PW // 2, out_dtypes=(F32,), name=f"in_proj_dx_{tag}")
    d_x, d_shift, d_scale, d_ng = _norm_mod_bwd(d_h, sv["x"], d_out, wt["norm_g"], scale, f"norm_mod_bwd_{tag}")
    grads = dict(w_in=g_w_in, w_out=g_w_out, w_uq=g_w_uq, w_ukv=g_w_ukv, norm_g=d_ng,
                 wg_pad_f=d_wg_f, bg_f=d_bg_f, wg_pad_b=d_wg_b, bg_b=d_bg_b, gla_norm_g=d_gng,
                 q_norm_g=d_qg, kv_norm_g=d_kvg, mla_out_g=d_mog, conv_w=d_conv_w, conv_out_g=d_cog)
    return d_x, (d_shift, d_scale, d_gate), grads


def _perm_in_cols(w):
    pad = jnp.zeros(w.shape[:-1] + (PW - IN_DIM,), w.dtype)
    return jnp.concatenate([w[..., 3808:5856], w[..., 2272:3808], w[..., 1952:2208], w[..., 768:1536], w[..., 0:768],
                            w[..., 1568:1952], w[..., 2208:2272], w[..., 1536:1568], pad], axis=-1)


def _unperm_in_cols(g):
    return jnp.concatenate([g[..., OGQ:OGQ + 2 * GQK], g[..., OGV:OGV + GW], g[..., OT + MROPE:OT + MROPE + 2 * GRANK],
                            g[..., OMQ:OMQ + MQL], g[..., OMKV:OMKV + MKVL], g[..., OT:OT + MROPE],
                            g[..., OCB:OCB + 3 * CONV_CH], g[..., OZ:OZ + D_MIX]], axis=-1)


IN_SEGS = ((3808, 5856), (2272, 3808), (1952, 2208), (768, 1536), (0, 768), (1568, 1952), (2208, 2272), (1536, 1568))
UQ_SEGS = (tuple((h * (MNOPE + MROPE), h * (MNOPE + MROPE) + MNOPE) for h in range(MH))
           + tuple((h * (MNOPE + MROPE) + MNOPE, (h + 1) * (MNOPE + MROPE)) for h in range(MH)))


def _perm_gathered(g, segs, width):
    per = g.shape[-1]
    parts, total = [], 0
    for a, b in segs:
        c = a
        while c < b:
            j = c // per
            hi = min(b, (j + 1) * per)
            parts.append(g[j, :, c - j * per:hi - j * per])
            c = hi
        total += b - a
    if width > total:
        parts.append(jnp.zeros((g.shape[1], width - total), g.dtype))
    return jnp.concatenate(parts, axis=1)


def _scatter_perm(gp, segs, per):
    offs, o = [], 0
    for a, b in segs:
        offs.append((a, b, o))
        o += b - a
    blocks = []
    for j in range(N_DEV):
        lo, hi = j * per, (j + 1) * per
        pieces = []
        for a, b, o in sorted(offs):
            s0, s1 = max(a, lo), min(b, hi)
            if s0 < s1:
                pieces.append(gp[:, o + s0 - a:o + s1 - a])
        blocks.append(jnp.concatenate(pieces, axis=1))
    return jnp.stack(blocks)


def _perm_uq_cols(w):
    w3 = w.reshape(w.shape[:-1] + (MH, MNOPE + MROPE))
    return jnp.concatenate([w3[..., :MNOPE].reshape(w.shape[:-1] + (MH * MNOPE,)),
                            w3[..., MNOPE:].reshape(w.shape[:-1] + (MH * MROPE,))], axis=-1)


def _unperm_uq_cols(g):
    nope = g[..., :MH * MNOPE].reshape(g.shape[:-1] + (MH, MNOPE))
    rope = g[..., MH * MNOPE:].reshape(g.shape[:-1] + (MH, MROPE))
    return jnp.concatenate([nope, rope], axis=-1).reshape(g.shape[:-1] + (MQW,))


def _prep_layer_weights(w_in, w_out, w_uq, w_ukv, small):
    def vec(v):
        return v.reshape(1, -1).astype(F32)

    zeros = functools.partial(jnp.zeros, dtype=F32)
    wg_f, wg_b = small["gla_wg_f"].astype(F32), small["gla_wg_b"].astype(F32)
    wg_pad_f = jnp.concatenate([zeros((MROPE, GQK)), wg_f, zeros((LANES - MROPE - GRANK, GQK))], axis=0)
    wg_pad_b = jnp.concatenate([zeros((MROPE + GRANK, GQK)), wg_b, zeros((LANES - MROPE - 2 * GRANK, GQK))], axis=0)
    return dict(w_in=w_in.astype(MXU), w_out=w_out.astype(MXU), w_uq=w_uq.astype(MXU),
                w_ukv=w_ukv.astype(MXU), norm_g=vec(small["norm_g"]), wg_pad_f=wg_pad_f, wg_pad_b=wg_pad_b,
                bg_f=vec(small["gla_bg_f"]), bg_b=vec(small["gla_bg_b"]), gla_norm_g=vec(small["gla_norm_g"]),
                q_norm_g=vec(small["mla_q_norm_g"]), kv_norm_g=vec(small["mla_kv_norm_g"]),
                mla_out_g=vec(small["mla_out_g"]), conv_w=small["conv_w"].astype(F32),
                conv_out_g=vec(small["conv_out_g"]))


def _natural_grads(gr):
    return dict(w_in=_unperm_in_cols(gr["w_in"]), w_out=gr["w_out"], mla_w_uq=_unperm_uq_cols(gr["w_uq"]),
                mla_w_ukv=gr["w_ukv"], norm_g=gr["norm_g"][0],
                gla_wg_f=gr["wg_pad_f"][MROPE:MROPE + GRANK], gla_bg_f=gr["bg_f"][0],
                gla_wg_b=gr["wg_pad_b"][MROPE + GRANK:MROPE + 2 * GRANK], gla_bg_b=gr["bg_b"][0],
                gla_norm_g=gr["gla_norm_g"][0], mla_q_norm_g=gr["q_norm_g"][0], mla_kv_norm_g=gr["kv_norm_g"][0],
                mla_out_g=gr["mla_out_g"][0], conv_w=gr["conv_w"], conv_out_g=gr["conv_out_g"][0])


def _exchange(arrs, name, scatter, space):
    n = len(arrs)

    def body(*refs):
        ins, outs = refs[:n], refs[n:2 * n]
        send_sems, recv_sems, loc_sems = refs[2 * n:]
        ax, ay, ac = lax.axis_index("x"), lax.axis_index("y"), lax.axis_index("c")
        me = 4 * ax + 2 * ay + ac

        def src(a, to):
            return ins[a].at[to] if scatter else ins[a]

        def remote(a, r, dst_slot):
            px = 1 - ax if r & 4 else ax
            py = 1 - ay if r & 2 else ay
            pc = 1 - ac if r & 1 else ac
            return pltpu.make_async_remote_copy(
                src_ref=src(a, 4 * px + 2 * py + pc), dst_ref=outs[a].at[dst_slot(4 * px + 2 * py + pc)],
                send_sem=send_sems.at[a, r - 1], recv_sem=recv_sems.at[a, r - 1],
                device_id=(px, py, pc), device_id_type=MESH)

        locs = [pltpu.make_async_copy(src(a, me), outs[a].at[me], loc_sems.at[a]) for a in range(n)]
        for cp in locs:
            cp.start()
        sends = [remote(a, r, lambda peer: me) for r in range(1, N_DEV) for a in range(n)]
        for cp in sends:
            cp.start()
        for r in range(1, N_DEV):
            for a in range(n):
                remote(a, r, lambda peer: peer).wait_recv()
        for cp in sends:
            cp.wait_send()
        for cp in locs:
            cp.wait()

    def out_shape(a):
        return jax.ShapeDtypeStruct(a.shape if scatter else (N_DEV,) + a.shape, a.dtype)

    spec = pl.BlockSpec(memory_space=space)
    return pl.pallas_call(
        body, in_specs=[spec] * n, out_specs=[spec] * n, out_shape=[out_shape(a) for a in arrs],
        scratch_shapes=[pltpu.SemaphoreType.DMA((n, N_DEV - 1)), pltpu.SemaphoreType.DMA((n, N_DEV - 1)),
                        pltpu.SemaphoreType.DMA((n,))],
        name=name, compiler_params=pltpu.CompilerParams(vmem_limit_bytes=VMEM_LIMIT))(*arrs)


def _peer(r):
    ax, ay, ac = lax.axis_index("x"), lax.axis_index("y"), lax.axis_index("c")
    px = 1 - ax if r & 4 else ax
    py = 1 - ay if r & 2 else ay
    pc = 1 - ac if r & 1 else ac
    return (px, py, pc), 4 * px + 2 * py + pc


def _exchange_start(arrs, after, name, scatter):
    n = len(arrs)

    def body(*refs):
        srcs, lands = refs[:n], refs[n:2 * n]
        send_sems, recv_sems = refs[2 * n + 1], refs[2 * n + 2]
        token = refs[-1]
        me = 4 * lax.axis_index("x") + 2 * lax.axis_index("y") + lax.axis_index("c")
        for r in range(1, N_DEV):
            peer, peer_idx = _peer(r)
            for a in range(n):
                pltpu.make_async_remote_copy(
                    src_ref=srcs[a].at[peer_idx] if scatter else srcs[a], dst_ref=lands[a].at[me],
                    send_sem=send_sems.at[a * (N_DEV - 1) + r - 1], recv_sem=recv_sems.at[a * (N_DEV - 1) + r - 1],
                    device_id=peer, device_id_type=MESH).start()
        token[...] = jnp.zeros_like(token)

    hbm = pl.BlockSpec(memory_space=pltpu.HBM)
    sem = pl.BlockSpec(memory_space=pltpu.SEMAPHORE)
    land_shapes = [a.shape if scatter else (N_DEV,) + a.shape for a in arrs]
    srcs = [pltpu.with_memory_space_constraint(a, pltpu.HBM) for a in arrs]
    lands = [pltpu.with_memory_space_constraint(lax.empty(shp, a.dtype), pltpu.HBM) for shp, a in zip(land_shapes, arrs)]
    res = pl.pallas_call(
        body, name=name,
        in_specs=[hbm] * (2 * n) + [pl.BlockSpec(memory_space=pl.ANY)],
        out_specs=[sem, sem] + [hbm] * (2 * n) + [pl.BlockSpec(memory_space=pltpu.VMEM)],
        out_shape=[pltpu.SemaphoreType.DMA((n * (N_DEV - 1),)), pltpu.SemaphoreType.DMA((n * (N_DEV - 1),))]
        + [pltpu.HBM(a.shape, a.dtype) for a in arrs] + [pltpu.HBM(shp, a.dtype) for shp, a in zip(land_shapes, arrs)]
        + [jax.ShapeDtypeStruct((8, LANES), F32)],
        input_output_aliases={i: 2 + i for i in range(2 * n)},
        compiler_params=pltpu.CompilerParams(has_side_effects=pltpu.SideEffectType.DATAFLOW_SIDE_EFFECTING),
    )(*srcs, *lands, after)
    return res[0], res[1], list(res[2:2 + n]), list(res[2 + n:2 + 2 * n]), res[-1]


def _exchange_wait(handle, after, name, scatter):
    send_sems, recv_sems, srcs, lands, _ = handle
    n = len(srcs)
    after = list(after)

    def body(*refs):
        src_refs, land_refs = refs[:n], refs[n:2 * n]
        ssem, rsem = refs[2 * n], refs[2 * n + 1]
        for r in range(1, N_DEV):
            peer, peer_idx = _peer(r)
            for a in range(n):
                cp = pltpu.make_async_remote_copy(
                    src_ref=src_refs[a].at[peer_idx] if scatter else src_refs[a], dst_ref=land_refs[a].at[peer_idx],
                    send_sem=ssem.at[a * (N_DEV - 1) + r - 1], recv_sem=rsem.at[a * (N_DEV - 1) + r - 1],
                    device_id=peer, device_id_type=MESH)
                cp.wait_send()
                cp.wait_recv()

    hbm = pl.BlockSpec(memory_space=pltpu.HBM)
    sem = pl.BlockSpec(memory_space=pltpu.SEMAPHORE)
    res = pl.pallas_call(
        body, name=name,
        in_specs=[hbm] * (2 * n) + [sem, sem] + [pl.BlockSpec(memory_space=pl.ANY)] * len(after),
        out_specs=[hbm] * (2 * n),
        out_shape=[pltpu.HBM(a.shape, a.dtype) for a in srcs] + [pltpu.HBM(a.shape, a.dtype) for a in lands],
        input_output_aliases={i: i for i in range(2 * n)},
        compiler_params=pltpu.CompilerParams(has_side_effects=pltpu.SideEffectType.DATAFLOW_SIDE_EFFECTING),
    )(*srcs, *lands, send_sems, recv_sems, *after)
    return list(res[n:])


def _ada_mod(c_all, ada_w, ada_b_cols, name):
    nl, d, wc = ada_w.shape

    def body(c_ref, w_ref, b_ref, ca_ref, mod_ref):
        cv = c_ref[...]
        ca = cv * _sigmoid(cv)
        ca_ref[...] = ca
        mod_ref[0] = _dotf(ca, w_ref[0]) + b_ref[0]

    return pl.pallas_call(
        body, grid=(nl,),
        in_specs=[pl.BlockSpec((N_DEV, d), lambda l: (0, 0)), pl.BlockSpec((1, d, wc), lambda l: (l, 0, 0)),
                  pl.BlockSpec((1, 1, wc), lambda l: (l, 0, 0))],
        out_specs=[pl.BlockSpec((N_DEV, d), lambda l: (0, 0)), pl.BlockSpec((1, N_DEV, wc), lambda l: (l, 0, 0))],
        out_shape=[jax.ShapeDtypeStruct((N_DEV, d), F32), jax.ShapeDtypeStruct((nl, N_DEV, wc), F32)],
        name=name, compiler_params=_cp(("arbitrary",)))(c_all, ada_w, ada_b_cols)


def _adam(w, g, m, v):
    m2 = ADAM_B1 * m + (1.0 - ADAM_B1) * g
    v2 = ADAM_B2 * v + (1.0 - ADAM_B2) * (g * g)
    m_hat = m2 / (1.0 - ADAM_B1 ** ADAM_STEP)
    v_hat = v2 / (1.0 - ADAM_B2 ** ADAM_STEP)
    delta = -ADAM_LR * (m_hat / (jnp.sqrt(v_hat) + ADAM_EPS) + ADAM_WD * w)
    return delta, m2, v2


def _ada_grad_adam(c_act, d_mod, w, m, v, name):
    nl, d, wc = w.shape
    tk = min(512, d)

    def body(c_ref, dm_ref, w_ref, m_ref, v_ref, g_ref, dl_ref, m2_ref, v2_ref):
        g = _dotf_tn(c_ref[...], dm_ref[0])
        delta, m2, v2 = _adam(w_ref[0], g, m_ref[0], v_ref[0])
        g_ref[0], dl_ref[0], m2_ref[0], v2_ref[0] = g, delta, m2, v2

    blk = pl.BlockSpec((1, tk, wc), lambda l, i: (l, i, 0))
    shp = jax.ShapeDtypeStruct(w.shape, F32)
    return pl.pallas_call(
        body, grid=(nl, d // tk),
        in_specs=[pl.BlockSpec((N_DEV, tk), lambda l, i: (0, i)), pl.BlockSpec((1, N_DEV, wc), lambda l, i: (l, 0, 0)),
                  blk, blk, blk],
        out_specs=[blk] * 4, out_shape=[shp] * 4, name=name,
        compiler_params=_cp(("parallel", "parallel")))(c_act, d_mod, w, m, v)


def _adam_big(recv, w, m, v, layer, prev, name):
    nl, r, c = w.shape
    tr = 256 if r % 256 == 0 else r

    def body(rc_ref, w_ref, m_ref, v_ref, *rest):
        g_ref, dl_ref, m2_ref, v2_ref = rest[-4:]
        g = rc_ref[0].astype(F32)
        for d in range(1, N_DEV):
            g = g + rc_ref[d].astype(F32)
        delta, m2, v2 = _adam(w_ref[0], g, m_ref[0], v_ref[0])
        g_ref[0], dl_ref[0], m2_ref[0], v2_ref[0] = g, delta, m2, v2

    blk = pl.BlockSpec((1, tr, c), lambda i: (layer, i, 0))
    shp = jax.ShapeDtypeStruct(w.shape, F32)
    prev = () if prev is None else tuple(prev)
    return pl.pallas_call(
        body, grid=(r // tr,),
        in_specs=[pl.BlockSpec((N_DEV, tr, c), lambda i: (0, i, 0)), blk, blk, blk]
        + [pl.BlockSpec(memory_space=pl.ANY)] * len(prev),
        out_specs=[blk] * 4, out_shape=[shp] * 4, name=name,
        input_output_aliases={4 + j: j for j in range(len(prev))},
        compiler_params=_cp(("parallel",)))(recv, w, m, v, *prev)


def _sum_devices(gathered, name):
    _, r, c = gathered.shape

    def body(g_ref, o_ref):
        acc = g_ref[0]
        for d in range(1, N_DEV):
            acc = acc + g_ref[d]
        o_ref[...] = acc

    spec = pl.BlockSpec(memory_space=pltpu.VMEM)
    return pl.pallas_call(body, in_specs=[spec], out_specs=spec, out_shape=jax.ShapeDtypeStruct((r, c), F32),
                          name=name, compiler_params=pltpu.CompilerParams(vmem_limit_bytes=VMEM_LIMIT))(gathered)


def _adam_small(w, g, m, v, name):
    def body(w_ref, g_ref, m_ref, v_ref, dl_ref, m2_ref, v2_ref):
        dl_ref[...], m2_ref[...], v2_ref[...] = _adam(w_ref[...], g_ref[...], m_ref[...], v_ref[...])

    spec = pl.BlockSpec(memory_space=pltpu.VMEM)
    shp = jax.ShapeDtypeStruct(w.shape, F32)
    return pl.pallas_call(body, in_specs=[spec] * 4, out_specs=[spec] * 3, out_shape=[shp] * 3, name=name,
                          compiler_params=pltpu.CompilerParams(vmem_limit_bytes=VMEM_LIMIT))(w, g, m, v)


def _pack(parts):
    flat = jnp.concatenate([p.reshape(-1).astype(F32) for p in parts])
    assert flat.shape[0] % LANES == 0, flat.shape
    return flat.reshape(-1, LANES)


def _unpack(packed, shapes):
    flat = packed.reshape(-1)
    out, off = [], 0
    for shp in shapes:
        size = 1
        for dim in shp:
            size *= dim
        out.append(flat[off:off + size].reshape(shp))
        off += size
    return out


def _gather_cols(g, per):
    g = jnp.moveaxis(g, 0, -2)
    return g.reshape(g.shape[:-2] + (N_DEV * per,))


def _scatter_cols(g, per):
    return jnp.moveaxis(g.reshape(g.shape[:-1] + (N_DEV, per)), -2, 0)


def _my_cols(full, me, per):
    return lax.dynamic_slice_in_dim(full, me * per, per, axis=full.ndim - 1)


def kernel(x, c, positions, ada_w, ada_b, norm_g, w_in, gla_wg_f, gla_bg_f, gla_wg_b, gla_bg_b, gla_norm_g, mla_q_norm_g, mla_kv_norm_g, mla_w_uq, mla_w_ukv, mla_out_g, conv_w, conv_out_g, w_out, final_g, loss_target, m_ada_w, m_ada_b, m_norm_g, m_w_in, m_gla_wg_f, m_gla_bg_f, m_gla_wg_b, m_gla_bg_b, m_gla_norm_g, m_mla_q_norm_g, m_mla_kv_norm_g, m_mla_w_uq, m_mla_w_ukv, m_mla_out_g, m_conv_w, m_conv_out_g, m_w_out, m_final_g, v_ada_w, v_ada_b, v_norm_g, v_w_in, v_gla_wg_f, v_gla_bg_f, v_gla_wg_b, v_gla_bg_b, v_gla_norm_g, v_mla_q_norm_g, v_mla_kv_norm_g, v_mla_w_uq, v_mla_w_ukv, v_mla_out_g, v_conv_w, v_conv_out_g, v_w_out, v_final_g):
    me = 4 * lax.axis_index("x") + 2 * lax.axis_index("y") + lax.axis_index("c")
    nl = ada_w.shape[0]
    s, d = x.shape[1], x.shape[2]
    ada_cols = ada_w.shape[2]
    wgc, cwc = gla_wg_f.shape[2], conv_w.shape[2]

    (g0,) = _exchange([_pack([c, gla_wg_f, gla_wg_b, conv_w])], "gather_small_in", False, pltpu.VMEM)
    g0 = g0.reshape(N_DEV, -1)
    o1, o2, o3 = d, d + gla_wg_f.size, d + 2 * gla_wg_f.size
    c_all = g0[:, :o1]
    wgf_full = _gather_cols(g0[:, o1:o2].reshape((N_DEV,) + gla_wg_f.shape), wgc)
    wgb_full = _gather_cols(g0[:, o2:o3].reshape((N_DEV,) + gla_wg_b.shape), wgc)
    convw_full = _gather_cols(g0[:, o3:].reshape((N_DEV,) + conv_w.shape), cwc)

    ada_b_cols = _my_cols(ada_b, me, ada_cols).reshape(nl, 1, ada_cols)
    c_act, mod_cols = _ada_mod(c_all, ada_w, ada_b_cols, "ada_mod")
    (g1,) = _exchange([_pack([mod_cols])], "gather_mod", False, pltpu.VMEM)
    mod_all = g1.reshape(N_DEV, nl, N_DEV, ada_cols)
    mod_mine = _gather_cols(lax.dynamic_index_in_dim(mod_all, me, axis=2, keepdims=False), ada_cols)

    inv_freq = ROPE_THETA ** (-jnp.arange(0, MROPE, 2, dtype=F32) / MROPE)
    ang = positions[0].astype(F32)[:, None] * inv_freq
    cos, sin = jnp.tile(jnp.cos(ang), (1, LANES * 2 // MROPE)), jnp.tile(jnp.sin(ang), (1, LANES * 2 // MROPE))

    big = [w_in, w_out, mla_w_uq, mla_w_ukv]
    big_names = ["w_in", "w_out", "mla_w_uq", "mla_w_ukv"]

    def local_blocks(l):
        return [w[l].astype(MXU) for w in big]

    def put_own(lands, own):
        return [lax.dynamic_update_index_in_dim(ld, o, me, 0) for ld, o in zip(lands, own)]

    def layer_weights(l, gathered):
        gw_in, gw_out, gw_uq, gw_ukv = gathered
        small = dict(norm_g=norm_g[l], gla_wg_f=wgf_full[l], gla_bg_f=gla_bg_f[l], gla_wg_b=wgb_full[l],
                     gla_bg_b=gla_bg_b[l], gla_norm_g=gla_norm_g[l], mla_q_norm_g=mla_q_norm_g[l],
                     mla_kv_norm_g=mla_kv_norm_g[l], mla_out_g=mla_out_g[l], conv_w=convw_full[l],
                     conv_out_g=conv_out_g[l])
        return _prep_layer_weights(_perm_gathered(gw_in, IN_SEGS, PW), gw_out.reshape((-1,) + gw_out.shape[2:]),
                                   _perm_gathered(gw_uq, UQ_SEGS, MQW), _gather_cols(gw_ukv, mla_w_ukv.shape[2]),
                                   small)

    gathered = _exchange(local_blocks(0), "gather_weights_l0", False, pltpu.HBM)
    h = x[0]
    saved, layers, mods = [], [], []
    for l in range(nl):
        shift, scale, gate = (mod_mine[l, i * d:(i + 1) * d].reshape(1, d) for i in range(3))
        if l + 1 < nl:
            nxt = local_blocks(l + 1)
            pending = _exchange_start(nxt, gathered[0], f"gather_weights_l{l + 1}_start", False)
            scale = scale + pending[-1][0, 0]
        layers.append(layer_weights(l, gathered))
        mods.append((shift, scale, gate))
        h, sv = _layer_fwd(h, mods[l], layers[l], cos, sin, f"l{l}")
        saved.append(sv)
        if l + 1 < nl:
            gathered = put_own(_exchange_wait(pending, [h], f"gather_weights_l{l + 1}_wait", False), nxt)
    loss_part, d_h, d_final_g = _final_loss(h, final_g.reshape(1, d), loss_target[0], "final_loss")
    loss = lax.psum(loss_part[0, 0], ("x", "y", "c"))

    def grad_sends(gr):
        return [_scatter_perm(gr["w_in"], IN_SEGS, w_in.shape[2]).astype(MXU),
                gr["w_out"].reshape((N_DEV,) + w_out.shape[1:]).astype(MXU),
                _scatter_perm(gr["w_uq"], UQ_SEGS, mla_w_uq.shape[2]).astype(MXU),
                _scatter_cols(gr["w_ukv"], mla_w_ukv.shape[2]).astype(MXU)]

    def own_slots(sends):
        return [lax.dynamic_index_in_dim(sd, me, axis=0, keepdims=False) for sd in sends]

    d_mods, grads, recv = [None] * nl, [None] * nl, [None] * nl
    pending, sends = None, None
    for l in reversed(range(nl)):
        shift, scale, gate = mods[l]
        if pending is not None:
            gate = gate + pending[-1][0, 0]
        d_h, d_mods[l], gr = _layer_bwd(d_h, saved[l], (shift, scale, gate), layers[l], cos, sin, f"l{l}")
        grads[l] = _natural_grads(gr)
        if pending is not None:
            recv[l + 1] = put_own(_exchange_wait(pending, [d_h], f"scatter_grads_l{l + 1}_wait", True), own_slots(sends))
        sends = grad_sends(gr)
        pending = _exchange_start(sends, d_h if l + 1 == nl else recv[l + 1][0], f"scatter_grads_l{l}_start", True)
    grad_x = d_h[None]

    def stacked(name):
        return jnp.stack([grads[l][name] for l in range(nl)])

    small_names = ["norm_g", "gla_wg_f", "gla_bg_f", "gla_wg_b", "gla_bg_b", "gla_norm_g", "mla_q_norm_g",
                   "mla_kv_norm_g", "mla_out_g", "conv_w", "conv_out_g"]
    d_mod_mine = jnp.stack([jnp.concatenate(d_mods[l], axis=-1)[0] for l in range(nl)])
    parts = [d_mod_mine] + [stacked(n) for n in small_names] + [d_final_g]
    shapes = [p.shape for p in parts]
    (g2,) = _exchange([_pack(parts)], "gather_small_grads", False, pltpu.VMEM)
    d_mod_all = g2.reshape(N_DEV, -1)[:, :d_mod_mine.size].reshape(N_DEV, nl, 3 * d)
    summed = dict(zip(["ada_b"] + small_names + ["final_g"], _unpack(_sum_devices(g2, "sum_small_grads"), shapes)))
    summed["gla_wg_f"] = _my_cols(summed["gla_wg_f"], me, wgc)
    summed["gla_wg_b"] = _my_cols(summed["gla_wg_b"], me, wgc)
    summed["conv_w"] = _my_cols(summed["conv_w"], me, cwc)

    d_mod_cols = jnp.moveaxis(_my_cols(d_mod_all, me, ada_cols), 0, 1)
    out = {}
    out["ada_w"] = _ada_grad_adam(c_act, d_mod_cols, ada_w, m_ada_w, v_ada_w, "ada_grad_adam")

    given = dict(ada_b=(ada_b, m_ada_b, v_ada_b), norm_g=(norm_g, m_norm_g, v_norm_g),
                 gla_wg_f=(gla_wg_f, m_gla_wg_f, v_gla_wg_f), gla_bg_f=(gla_bg_f, m_gla_bg_f, v_gla_bg_f),
                 gla_wg_b=(gla_wg_b, m_gla_wg_b, v_gla_wg_b), gla_bg_b=(gla_bg_b, m_gla_bg_b, v_gla_bg_b),
                 gla_norm_g=(gla_norm_g, m_gla_norm_g, v_gla_norm_g),
                 mla_q_norm_g=(mla_q_norm_g, m_mla_q_norm_g, v_mla_q_norm_g),
                 mla_kv_norm_g=(mla_kv_norm_g, m_mla_kv_norm_g, v_mla_kv_norm_g),
                 mla_out_g=(mla_out_g, m_mla_out_g, v_mla_out_g), conv_w=(conv_w, m_conv_w, v_conv_w),
                 conv_out_g=(conv_out_g, m_conv_out_g, v_conv_out_g), final_g=(final_g, m_final_g, v_final_g))
    names = list(given)
    wshapes = [given[n][0].shape for n in names]
    packed = [_pack([given[n][i] for n in names]) for i in range(3)]
    g_small = _pack([summed[n].reshape(given[n][0].shape) for n in names])
    res = _adam_small(packed[0], g_small, packed[1], packed[2], "adam_small")
    unpacked = [_unpack(r, wshapes) for r in res]
    for i, n in enumerate(names):
        out[n] = (summed[n].reshape(given[n][0].shape), unpacked[0][i], unpacked[1][i], unpacked[2][i])

    moments = dict(w_in=(m_w_in, v_w_in), w_out=(m_w_out, v_w_out), mla_w_uq=(m_mla_w_uq, v_mla_w_uq),
                   mla_w_ukv=(m_mla_w_ukv, v_mla_w_ukv))
    done = [out["ada_w"][0], res[0]]
    for l in reversed(range(nl)):
        if l == 0:
            recv[0] = put_own(_exchange_wait(pending, done, "scatter_grads_l0_wait", True), own_slots(sends))
        for i, n in enumerate(big_names):
            out[n] = _adam_big(recv[l][i], big[i], moments[n][0], moments[n][1], l, out.get(n), f"adam_{n}_l{l}")
        done = done + [out[n][0] for n in big_names]

    order = ["ada_w", "ada_b", "norm_g", "w_in", "gla_wg_f", "gla_bg_f", "gla_wg_b", "gla_bg_b", "gla_norm_g",
             "mla_q_norm_g", "mla_kv_norm_g", "mla_w_uq", "mla_w_ukv", "mla_out_g", "conv_w", "conv_out_g", "w_out",
             "final_g"]
    return (loss, grad_x, *[out[n][0] for n in order], *[out[n][1] for n in order], *[out[n][2] for n in order],
            *[out[n][3] for n in order])
```

```python
import functools

import jax
import jax.numpy as jnp
from jax import lax
from jax.experimental import pallas as pl
from jax.experimental.pallas import tpu as pltpu

F32 = jnp.float32
MXU = jnp.bfloat16
HI = lax.Precision.HIGHEST
N_DEV = 8
MESH = pl.DeviceIdType.MESH

D_MIX = 2048
GH, GDK, GDV = 6, 64, 128
GW = GH * GDV
GQK = GH * GDK
GRANK = 16
GTEMP = 16.0
CHUNK = 64
MH, MQL, MKVL, MNOPE, MROPE, MDV = 6, 384, 256, 128, 64, 128
MW = MH * MDV
MQW = MH * (MNOPE + MROPE)
MKVW = MH * (MNOPE + MDV)
CONV_CH = 512
ROPE_THETA = 10000.0
EPS = 1e-6
IN_DIM = 5856
OZ, OCB, OCC, OCX, OMKV, OGV, OGQ, OGK, OMQ, OT = 0, 2048, 2560, 3072, 3584, 3840, 4608, 4992, 5376, 5760
PW = 5888
LANES = 128
VMEM_LIMIT = 56 * 1024 * 1024

ADAM_LR, ADAM_B1, ADAM_B2, ADAM_EPS, ADAM_WD, ADAM_STEP = 0.001, 0.9, 0.999, 1e-08, 0.01, 10


def _cp(sem=None):
    return pltpu.CompilerParams(dimension_semantics=sem, vmem_limit_bytes=VMEM_LIMIT)


def _dot(a, b):
    return jnp.dot(a.astype(MXU), b.astype(MXU), preferred_element_type=F32)


def _dot_nt(a, b):
    return lax.dot_general(a.astype(MXU), b.astype(MXU), (((1,), (1,)), ((), ())), preferred_element_type=F32)


def _dot_tn(a, b):
    return lax.dot_general(a.astype(MXU), b.astype(MXU), (((0,), (0,)), ((), ())), preferred_element_type=F32)


def _dotf(a, b):
    return jnp.dot(a, b, precision=HI, preferred_element_type=F32)


def _dotf_nt(a, b):
    return lax.dot_general(a, b, (((1,), (1,)), ((), ())), precision=HI, preferred_element_type=F32)


def _dotf_tn(a, b):
    return lax.dot_general(a, b, (((0,), (0,)), ((), ())), precision=HI, preferred_element_type=F32)


def _rows(s):
    return min(256, s)


def _rms(x, g):
    r = lax.rsqrt(jnp.mean(x * x, axis=-1, keepdims=True) + EPS)
    return x * r * g


def _rms_bwd(dy, x, g):
    r = lax.rsqrt(jnp.mean(x * x, axis=-1, keepdims=True) + EPS)
    xh = x * r
    dxh = dy * g
    dg = jnp.sum(dy * xh, axis=0, keepdims=True)
    dx = r * (dxh - xh * jnp.mean(dxh * xh, axis=-1, keepdims=True))
    return dx, dg


def _sigmoid(z):
    return 1.0 / (1.0 + jnp.exp(-z))


def _matmul(a, b, *, dims, tm, tn, tk, out_dtypes, name, epilogue=None, extras=(), extra_kinds=(), after=()):
    if dims == "nn":
        (m, k), n, mul = a.shape, b.shape[1], _dot
    elif dims == "nt":
        (m, k), n, mul = a.shape, b.shape[0], _dot_nt
    else:
        (k, m), n, mul = a.shape, b.shape[1], _dot_tn
    tm, tn, tk = min(tm, m), min(tn, n), min(tk, k)
    assert m % tm == 0 and n % tn == 0 and k % tk == 0, (m, n, k, tm, tn, tk)
    if dims == "nn":
        a_spec = pl.BlockSpec((tm, tk), lambda i, j, kk: (i, kk))
        b_spec = pl.BlockSpec((tk, tn), lambda i, j, kk: (kk, j))
    elif dims == "nt":
        a_spec = pl.BlockSpec((tm, tk), lambda i, j, kk: (i, kk))
        b_spec = pl.BlockSpec((tn, tk), lambda i, j, kk: (j, kk))
    else:
        a_spec = pl.BlockSpec((tk, tm), lambda i, j, kk: (kk, i))
        b_spec = pl.BlockSpec((tk, tn), lambda i, j, kk: (kk, j))
    nk = k // tk
    n_extra = len(extras)
    n_out = len(out_dtypes)
    n_after = len(after)
    extra_specs = []
    for kind in extra_kinds:
        if kind == "mn":
            extra_specs.append(pl.BlockSpec((tm, tn), lambda i, j, kk: (i, j)))
        else:
            extra_specs.append(pl.BlockSpec((1, tn), lambda i, j, kk: (0, j)))

    def body(*refs):
        a_ref, b_ref = refs[0], refs[1]
        ex = refs[2:2 + n_extra]
        outs = refs[2 + n_extra + n_after:2 + n_extra + n_after + n_out]
        acc = refs[-1]
        kk = pl.program_id(2)

        @pl.when(kk == 0)
        def _():
            acc[...] = jnp.zeros_like(acc)

        acc[...] += mul(a_ref[...], b_ref[...])

        @pl.when(kk == nk - 1)
        def _():
            res = acc[...]
            vals = (res,) if epilogue is None else epilogue(res, *[e[...] for e in ex])
            for o, v in zip(outs, vals):
                o[...] = v.astype(o.dtype)

    out_spec = pl.BlockSpec((tm, tn), lambda i, j, kk: (i, j))
    res = pl.pallas_call(
        body, grid=(m // tm, n // tn, nk),
        in_specs=[a_spec, b_spec] + extra_specs + [pl.BlockSpec(memory_space=pl.ANY)] * n_after,
        out_specs=[out_spec] * n_out,
        out_shape=[jax.ShapeDtypeStruct((m, n), dt) for dt in out_dtypes],
        scratch_shapes=[pltpu.VMEM((tm, tn), F32)],
        name=name, compiler_params=_cp(("parallel", "parallel", "arbitrary")),
    )(a, b, *extras, *after)
    return res


def _norm_mod(x, g, scale, shift, name):
    s, d = x.shape
    tr = _rows(s)

    def body(x_ref, g_ref, sc_ref, sh_ref, h_ref):
        h = _rms(x_ref[...], g_ref[...]) * (1.0 + sc_ref[...]) + sh_ref[...]
        h_ref[...] = h.astype(h_ref.dtype)

    row = pl.BlockSpec((tr, d), lambda i: (i, 0))
    vec = pl.BlockSpec((1, d), lambda i: (0, 0))
    return pl.pallas_call(body, grid=(s // tr,), in_specs=[row, vec, vec, vec], out_specs=row,
                          out_shape=jax.ShapeDtypeStruct((s, d), MXU), name=name,
                          compiler_params=_cp(("parallel",)))(x, g, scale, shift)


def _norm_mod_bwd(d_h, x, d_out, g, scale, name):
    s, d = x.shape
    tr = _rows(s)

    def body(dh_ref, x_ref, do_ref, g_ref, sc_ref, dx_ref, dsh_ref, dsc_ref, dg_ref):
        i = pl.program_id(0)

        @pl.when(i == 0)
        def _():
            dsh_ref[...] = jnp.zeros_like(dsh_ref)
            dsc_ref[...] = jnp.zeros_like(dsc_ref)
            dg_ref[...] = jnp.zeros_like(dg_ref)

        dh = dh_ref[...]
        xv = x_ref[...]
        gv = g_ref[...]
        r = lax.rsqrt(jnp.mean(xv * xv, axis=-1, keepdims=True) + EPS)
        xh = xv * r
        dsh_ref[...] += jnp.sum(dh, axis=0, keepdims=True)
        dsc_ref[...] += jnp.sum(dh * (xh * gv), axis=0, keepdims=True)
        dhn = dh * (1.0 + sc_ref[...])
        dg_ref[...] += jnp.sum(dhn * xh, axis=0, keepdims=True)
        dxh = dhn * gv
        dx_ref[...] = do_ref[...] + r * (dxh - xh * jnp.mean(dxh * xh, axis=-1, keepdims=True))

    row = pl.BlockSpec((tr, d), lambda i: (i, 0))
    vec = pl.BlockSpec((1, d), lambda i: (0, 0))
    vshape = jax.ShapeDtypeStruct((1, d), F32)
    return pl.pallas_call(body, grid=(s // tr,), in_specs=[row, row, row, vec, vec],
                          out_specs=[row, vec, vec, vec],
                          out_shape=[jax.ShapeDtypeStruct((s, d), F32), vshape, vshape, vshape],
                          name=name, compiler_params=_cp(("arbitrary",)))(d_h, x, d_out, g, scale)


def _gate_bwd(d_out, u, gate, name):
    s, d = d_out.shape
    tr = _rows(s)

    def body(do_ref, u_ref, gt_ref, du_ref, dgt_ref):
        @pl.when(pl.program_id(0) == 0)
        def _():
            dgt_ref[...] = jnp.zeros_like(dgt_ref)

        do = do_ref[...]
        du_ref[...] = (do * gt_ref[...]).astype(du_ref.dtype)
        dgt_ref[...] += jnp.sum(do * u_ref[...], axis=0, keepdims=True)

    row = pl.BlockSpec((tr, d), lambda i: (i, 0))
    vec = pl.BlockSpec((1, d), lambda i: (0, 0))
    return pl.pallas_call(body, grid=(s // tr,), in_specs=[row, row, vec], out_specs=[row, vec],
                          out_shape=[jax.ShapeDtypeStruct((s, d), MXU), jax.ShapeDtypeStruct((1, d), F32)],
                          name=name, compiler_params=_cp(("arbitrary",)))(d_out, u, gate)


def _final_loss(x, g, target, name):
    s, d = x.shape
    tr = _rows(s)

    def body(x_ref, g_ref, t_ref, loss_ref, dx_ref, dg_ref):
        @pl.when(pl.program_id(0) == 0)
        def _():
            loss_ref[...] = jnp.zeros_like(loss_ref)
            dg_ref[...] = jnp.zeros_like(dg_ref)

        xv = x_ref[...]
        gv = g_ref[...]
        diff = _rms(xv, gv) - t_ref[...]
        part = 0.5 * jnp.sum(jnp.sum(diff * diff, axis=-1, keepdims=True) / d, axis=0, keepdims=True)
        loss_ref[...] += jnp.broadcast_to(part, loss_ref.shape)
        dx, dg = _rms_bwd(diff / d, xv, gv)
        dx_ref[...] = dx
        dg_ref[...] += dg

    row = pl.BlockSpec((tr, d), lambda i: (i, 0))
    vec = pl.BlockSpec((1, d), lambda i: (0, 0))
    lvec = pl.BlockSpec((1, LANES), lambda i: (0, 0))
    return pl.pallas_call(body, grid=(s // tr,), in_specs=[row, vec, row], out_specs=[lvec, row, vec],
                          out_shape=[jax.ShapeDtypeStruct((1, LANES), F32), jax.ShapeDtypeStruct((s, d), F32),
                                     jax.ShapeDtypeStruct((1, d), F32)],
                          name=name, compiler_params=_cp(("arbitrary",)))(x, g, target)


def _shift_rows(u, s, down):
    ri = lax.broadcasted_iota(jnp.int32, u.shape, 0)
    if down:
        return jnp.where(ri == 0, 0.0, pltpu.roll(u, 1, 0))
    return jnp.where(ri == s - 1, 0.0, pltpu.roll(u, s - 1, 0))


def _conv_fwd(proj, conv_w, name):
    s = proj.shape[0]
    nt = CONV_CH // LANES

    def body(cb_ref, cc_ref, cx_ref, w_ref, pre_ref):
        u = cc_ref[...] * cx_ref[...]
        conv = _shift_rows(u, s, True) * w_ref[0:1, :] + u * w_ref[1:2, :] + _shift_rows(u, s, False) * w_ref[2:3, :]
        pre_ref[...] = cb_ref[...] * conv

    def col(off):
        return pl.BlockSpec((s, LANES), lambda j: (0, off // LANES + j))

    return pl.pallas_call(body, grid=(nt,), in_specs=[col(OCB), col(OCC), col(OCX), pl.BlockSpec((3, LANES), lambda j: (0, j))],
                          out_specs=pl.BlockSpec((s, LANES), lambda j: (0, j)),
                          out_shape=jax.ShapeDtypeStruct((s, CONV_CH), F32), name=name,
                          compiler_params=_cp(("parallel",)))(proj, proj, proj, conv_w)


def _conv_bwd(proj, conv_w, d_pre, name):
    s = proj.shape[0]
    nt = CONV_CH // LANES

    def body(cb_ref, cc_ref, cx_ref, w_ref, dp_ref, dcb_ref, dcc_ref, dcx_ref, dw_ref):
        cc, cx = cc_ref[...], cx_ref[...]
        u = cc * cx
        up, dn = _shift_rows(u, s, True), _shift_rows(u, s, False)
        w0, w1, w2 = w_ref[0:1, :], w_ref[1:2, :], w_ref[2:3, :]
        conv = up * w0 + u * w1 + dn * w2
        dp = dp_ref[...]
        dcb_ref[...] = dp * conv
        dconv = dp * cb_ref[...]
        du = _shift_rows(dconv, s, False) * w0 + dconv * w1 + _shift_rows(dconv, s, True) * w2
        dcc_ref[...] = du * cx
        dcx_ref[...] = du * cc
        dw_ref[0:1, :] = jnp.sum(dconv * up, axis=0, keepdims=True)
        dw_ref[1:2, :] = jnp.sum(dconv * u, axis=0, keepdims=True)
        dw_ref[2:3, :] = jnp.sum(dconv * dn, axis=0, keepdims=True)

    def col(off):
        return pl.BlockSpec((s, LANES), lambda j: (0, off // LANES + j))

    blk = pl.BlockSpec((s, LANES), lambda j: (0, j))
    wblk = pl.BlockSpec((3, LANES), lambda j: (0, j))
    full = jax.ShapeDtypeStruct((s, CONV_CH), F32)
    return pl.pallas_call(body, grid=(nt,), in_specs=[col(OCB), col(OCC), col(OCX), wblk, blk],
                          out_specs=[blk, blk, blk, wblk],
                          out_shape=[full, full, full, jax.ShapeDtypeStruct((3, CONV_CH), F32)],
                          name=name, compiler_params=_cp(("parallel",)))(proj, proj, proj, conv_w, d_pre)


GLA_SUB = 4


def _gla_gates(t_ref, wg_ref, bg_ref):
    t = t_ref[...]
    a = _dotf(t, wg_ref[...]) + bg_ref[...]
    la = (jnp.minimum(a, 0.0) - jnp.log(1.0 + jnp.exp(-jnp.abs(a)))) / GTEMP
    return t, a, la


def _gla_masks(reverse):
    ri = lax.broadcasted_iota(jnp.int32, (CHUNK, CHUNK), 0)
    ci = lax.broadcasted_iota(jnp.int32, (CHUNK, CHUNK), 1)
    if reverse:
        cum, mask = ci >= ri, ci > ri
    else:
        cum, mask = ci <= ri, ci <= ri
    return cum.astype(F32), mask


def _gla_specs(s, reverse):
    nsub = min(GLA_SUB, s // CHUNK)
    nsteps = s // (CHUNK * nsub)

    def row(n):
        return nsteps - 1 - n if reverse else n

    def chunk(pi):
        return nsub - 1 - pi if reverse else pi

    return nsub, nsteps, row, chunk


def _gla_fwd(proj, wg_pad, bg, reverse, name):
    s = proj.shape[0]
    nsub, nsteps, row, chunk = _gla_specs(s, reverse)
    rb = nsub * CHUNK

    def body(q_ref, k_ref, v_ref, t_ref, wg_ref, bg_ref, o_ref, st_ref, state):
        @pl.when(pl.program_id(0) == 0)
        def _():
            state[...] = jnp.zeros_like(state)

        _, _, la = _gla_gates(t_ref, wg_ref, bg_ref)
        cumf, mask = _gla_masks(reverse)
        lane = lax.broadcasted_iota(jnp.int32, (CHUNK, LANES), 1)
        for pi in range(nsub):
            rows = slice(chunk(pi) * CHUNK, (chunk(pi) + 1) * CHUNK)
            la_c = la[rows]
            b = _dotf(cumf, la_c)
            bl = jnp.sum(la_c, axis=0, keepdims=True)
            q = q_ref[rows, :] * (GDK ** -0.5)
            k = k_ref[rows, :]
            qd = q * jnp.exp(b)
            ki = k * jnp.exp(-b)
            kte = k * jnp.exp(bl - b)
            decay = jnp.exp(bl)
            for h in range(GH):
                p = h // 2
                sl = slice(p * LANES, (p + 1) * LANES)
                lm = (lane < GDK) if h % 2 == 0 else (lane >= GDK)
                qd_h = jnp.where(lm, qd[:, sl], 0.0)
                kte_h = jnp.where(lm, kte[:, sl], 0.0)
                v_h = v_ref[rows, h * GDV:(h + 1) * GDV]
                st = state[h]
                a_mat = jnp.where(mask, _dot_nt(qd_h, ki[:, sl]), 0.0)
                o_ref[rows, h * GDV:(h + 1) * GDV] = _dot(a_mat, v_h) + _dot_nt(qd_h, st)
                st_ref[pi, h] = st
                state[h] = st * decay[:, sl] + _dot_tn(v_h, kte_h)

    return pl.pallas_call(
        body, grid=(nsteps,),
        in_specs=[pl.BlockSpec((rb, GQK), lambda n: (row(n), OGQ // GQK)),
                  pl.BlockSpec((rb, GQK), lambda n: (row(n), OGK // GQK)),
                  pl.BlockSpec((rb, GW), lambda n: (row(n), OGV // GW)),
                  pl.BlockSpec((rb, LANES), lambda n: (row(n), OT // LANES)),
                  pl.BlockSpec((LANES, GQK), lambda n: (0, 0)),
                  pl.BlockSpec((1, GQK), lambda n: (0, 0))],
        out_specs=[pl.BlockSpec((rb, GW), lambda n: (row(n), 0)),
                   pl.BlockSpec((nsub, GH, GDV, LANES), lambda n: (n, 0, 0, 0))],
        out_shape=[jax.ShapeDtypeStruct((s, GW), F32), jax.ShapeDtypeStruct((s // CHUNK, GH, GDV, LANES), F32)],
        scratch_shapes=[pltpu.VMEM((GH, GDV, LANES), F32)],
        name=name, compiler_params=_cp(("arbitrary",)))(proj, proj, proj, proj, wg_pad, bg)


def _gla_bwd(proj, wg_pad, bg, states, d_o, reverse, name):
    s = proj.shape[0]
    nsub, nsteps, row, chunk = _gla_specs(s, reverse)
    rb = nsub * CHUNK

    def body(q_ref, k_ref, v_ref, t_ref, wg_ref, bg_ref, st_ref, do_ref,
             dq_ref, dk_ref, dv_ref, dt_ref, dwg_ref, dbg_ref, dstate, da_buf):
        @pl.when(pl.program_id(0) == 0)
        def _():
            dstate[...] = jnp.zeros_like(dstate)
            dwg_ref[...] = jnp.zeros_like(dwg_ref)
            dbg_ref[...] = jnp.zeros_like(dbg_ref)

        t, a, la = _gla_gates(t_ref, wg_ref, bg_ref)
        cumf, mask = _gla_masks(reverse)
        lane = lax.broadcasted_iota(jnp.int32, (CHUNK, LANES), 1)
        for pi in reversed(range(nsub)):
            rows = slice(chunk(pi) * CHUNK, (chunk(pi) + 1) * CHUNK)
            la_c = la[rows]
            b = _dotf(cumf, la_c)
            bl = jnp.sum(la_c, axis=0, keepdims=True)
            q = q_ref[rows, :] * (GDK ** -0.5)
            k = k_ref[rows, :]
            e, ei, ee = jnp.exp(b), jnp.exp(-b), jnp.exp(bl - b)
            qd, ki, kte = q * e, k * ei, k * ee
            decay = jnp.exp(bl)
            for p in range(GH // 2):
                sl = slice(p * LANES, (p + 1) * LANES)
                dqd = jnp.zeros((CHUNK, LANES), F32)
                dki = jnp.zeros((CHUNK, LANES), F32)
                dkte = jnp.zeros((CHUNK, LANES), F32)
                ddecay = jnp.zeros((1, LANES), F32)
                for half in range(2):
                    h = 2 * p + half
                    lm = (lane < GDK) if half == 0 else (lane >= GDK)
                    qd_h = jnp.where(lm, qd[:, sl], 0.0)
                    ki_h = jnp.where(lm, ki[:, sl], 0.0)
                    kte_h = jnp.where(lm, kte[:, sl], 0.0)
                    v_h = v_ref[rows, h * GDV:(h + 1) * GDV]
                    do_h = do_ref[rows, h * GDV:(h + 1) * GDV]
                    st = st_ref[pi, h]
                    dst = dstate[h]
                    a_mat = jnp.where(mask, _dot_nt(qd_h, ki_h), 0.0)
                    da_mat = jnp.where(mask, _dot_nt(do_h, v_h), 0.0)
                    dv_ref[rows, h * GDV:(h + 1) * GDV] = _dot_tn(a_mat, do_h) + _dot_nt(kte_h, dst)
                    dqd += _dot(da_mat, ki_h) + _dot(do_h, st)
                    dki += _dot_tn(da_mat, qd_h)
                    dkte += _dot(v_h, dst)
                    ddecay += jnp.sum(dst * st, axis=0, keepdims=True)
                    dstate[h] = dst * decay[:, sl] + _dot_tn(do_h, qd_h)
                dq_ref[rows, sl] = dqd * e[:, sl] * (GDK ** -0.5)
                dk_ref[rows, sl] = dki * ei[:, sl] + dkte * ee[:, sl]
                db = dqd * qd[:, sl] - dki * ki[:, sl] - dkte * kte[:, sl]
                dbl = jnp.sum(dkte * kte[:, sl], axis=0, keepdims=True) + decay[:, sl] * ddecay
                da_buf[rows, sl] = _dotf_tn(cumf, db) + dbl
        da = da_buf[...] * (1.0 / GTEMP) * _sigmoid(-a)
        dt_ref[...] = _dotf_nt(da, wg_ref[...])
        dwg_ref[...] += _dotf_tn(t, da)
        dbg_ref[...] += jnp.sum(da, axis=0, keepdims=True)

    def prow(j):
        return row(nsteps - 1 - j)

    return pl.pallas_call(
        body, grid=(nsteps,),
        in_specs=[pl.BlockSpec((rb, GQK), lambda j: (prow(j), OGQ // GQK)),
                  pl.BlockSpec((rb, GQK), lambda j: (prow(j), OGK // GQK)),
                  pl.BlockSpec((rb, GW), lambda j: (prow(j), OGV // GW)),
                  pl.BlockSpec((rb, LANES), lambda j: (prow(j), OT // LANES)),
                  pl.BlockSpec((LANES, GQK), lambda j: (0, 0)),
                  pl.BlockSpec((1, GQK), lambda j: (0, 0)),
                  pl.BlockSpec((nsub, GH, GDV, LANES), lambda j: (nsteps - 1 - j, 0, 0, 0)),
                  pl.BlockSpec((rb, GW), lambda j: (prow(j), 0))],
        out_specs=[pl.BlockSpec((rb, GQK), lambda j: (prow(j), 0)),
                   pl.BlockSpec((rb, GQK), lambda j: (prow(j), 0)),
                   pl.BlockSpec((rb, GW), lambda j: (prow(j), 0)),
                   pl.BlockSpec((rb, LANES), lambda j: (prow(j), 0)),
                   pl.BlockSpec((LANES, GQK), lambda j: (0, 0)),
                   pl.BlockSpec((1, GQK), lambda j: (0, 0))],
        out_shape=[jax.ShapeDtypeStruct((s, GQK), F32), jax.ShapeDtypeStruct((s, GQK), F32),
                   jax.ShapeDtypeStruct((s, GW), F32), jax.ShapeDtypeStruct((s, LANES), F32),
                   jax.ShapeDtypeStruct((LANES, GQK), F32), jax.ShapeDtypeStruct((1, GQK), F32)],
        scratch_shapes=[pltpu.VMEM((GH, GDV, LANES), F32), pltpu.VMEM((rb, GQK), F32)],
        name=name, compiler_params=_cp(("arbitrary",)))(proj, proj, proj, proj, wg_pad, bg, states, d_o)


def _rot_half(x):
    lane = lax.broadcasted_iota(jnp.int32, x.shape, 1)
    first = (lane % MROPE) < (MROPE // 2)
    return jnp.where(first, -pltpu.roll(x, LANES - MROPE // 2, 1), pltpu.roll(x, MROPE // 2, 1))


def _mla_prep(proj, cos, sin, qg, kvg, w_uq, w_ukv, name):
    s = proj.shape[0]
    tr = _rows(s)

    def body(mq_ref, mkv_ref, t_ref, cos_ref, sin_ref, qg_ref, kvg_ref, wuq_ref, wukv_ref, q_ref, k_ref, v_ref):
        cosv, sinv = cos_ref[...], sin_ref[...]
        lane = lax.broadcasted_iota(jnp.int32, (tr, LANES), 1)

        def rope(xv):
            return xv * cosv + _rot_half(xv) * sinv

        qm = _dot(_rms(mq_ref[...], qg_ref[...]), wuq_ref[...])
        kv = _dot(_rms(mkv_ref[...], kvg_ref[...]), wukv_ref[...])
        kr_lo = jnp.where(lane < MROPE, rope(t_ref[...]), 0.0)
        kr_hi = pltpu.roll(kr_lo, MROPE, 1)
        for p in range(MH // 2):
            r = rope(qm[:, MW + p * LANES:MW + (p + 1) * LANES]).astype(q_ref.dtype)
            q_ref[2 * p, :, LANES:] = r
            q_ref[2 * p + 1, :, LANES:] = r
        for h in range(MH):
            q_ref[h, :, :LANES] = qm[:, h * LANES:(h + 1) * LANES].astype(q_ref.dtype)
            k_ref[h, :, :LANES] = kv[:, 2 * h * LANES:(2 * h + 1) * LANES].astype(k_ref.dtype)
            k_ref[h, :, LANES:] = (kr_lo if h % 2 == 0 else kr_hi).astype(k_ref.dtype)
            v_ref[h] = kv[:, (2 * h + 1) * LANES:(2 * h + 2) * LANES].astype(v_ref.dtype)

    def full(shape):
        return pl.BlockSpec(shape, lambda i: (0,) * len(shape))

    return pl.pallas_call(
        body, grid=(s // tr,),
        in_specs=[pl.BlockSpec((tr, MQL), lambda i: (i, OMQ // MQL)),
                  pl.BlockSpec((tr, MKVL), lambda i: (i, OMKV // MKVL)),
                  pl.BlockSpec((tr, LANES), lambda i: (i, OT // LANES)),
                  pl.BlockSpec((tr, LANES), lambda i: (i, 0)),
                  pl.BlockSpec((tr, LANES), lambda i: (i, 0)),
                  full((1, MQL)), full((1, MKVL)), full((MQL, MQW)), full((MKVL, MKVW))],
        out_specs=[pl.BlockSpec((MH, tr, 2 * LANES), lambda i: (0, i, 0)),
                   pl.BlockSpec((MH, tr, 2 * LANES), lambda i: (0, i, 0)),
                   pl.BlockSpec((MH, tr, LANES), lambda i: (0, i, 0))],
        out_shape=[jax.ShapeDtypeStruct((MH, s, 2 * LANES), MXU), jax.ShapeDtypeStruct((MH, s, 2 * LANES), MXU),
                   jax.ShapeDtypeStruct((MH, s, LANES), MXU)],
        name=name, compiler_params=_cp(("parallel",)))(proj, proj, proj, cos, sin, qg, kvg, w_uq, w_ukv)


def _mla_prep_bwd(proj, cos, sin, qg, kvg, w_uq, w_ukv, d_q, d_k, d_v, name):
    s = proj.shape[0]
    tr = _rows(s)

    def body(mq_ref, mkv_ref, cos_ref, sin_ref, qg_ref, kvg_ref, wuq_ref, wukv_ref, dq_ref, dk_ref, dv_ref,
             dmq_ref, dmkv_ref, dt_ref, dwuq_ref, dwukv_ref, dqg_ref, dkvg_ref):
        @pl.when(pl.program_id(0) == 0)
        def _():
            for r in (dwuq_ref, dwukv_ref, dqg_ref, dkvg_ref):
                r[...] = jnp.zeros_like(r)

        cosv, sinv = cos_ref[...], sin_ref[...]
        lane = lax.broadcasted_iota(jnp.int32, (tr, LANES), 1)
        lo = lane < MROPE

        def unrope(dv):
            return dv * cosv - _rot_half(dv * sinv)

        parts = [dq_ref[h, :, :LANES] for h in range(MH)]
        for p in range(MH // 2):
            parts.append(unrope(jnp.where(lo, dq_ref[2 * p, :, LANES:], dq_ref[2 * p + 1, :, LANES:])))
        d_qm = jnp.concatenate(parts, axis=1)
        mq, qgv = mq_ref[...], qg_ref[...]
        cq = _rms(mq, qgv)
        dwuq_ref[...] += _dot_tn(cq, d_qm)
        dmq, dqg = _rms_bwd(_dot_nt(d_qm, wuq_ref[...]), mq, qgv)
        dmq_ref[...] = dmq
        dqg_ref[...] += dqg

        parts = []
        for h in range(MH):
            parts += [dk_ref[h, :, :LANES], dv_ref[h]]
        d_kv = jnp.concatenate(parts, axis=1)
        mkv, kvgv = mkv_ref[...], kvg_ref[...]
        ckv = _rms(mkv, kvgv)
        dwukv_ref[...] += _dot_tn(ckv, d_kv)
        dmkv, dkvg = _rms_bwd(_dot_nt(d_kv, wukv_ref[...]), mkv, kvgv)
        dmkv_ref[...] = dmkv
        dkvg_ref[...] += dkvg

        even = dk_ref[0, :, LANES:] + dk_ref[2, :, LANES:] + dk_ref[4, :, LANES:]
        odd = dk_ref[1, :, LANES:] + dk_ref[3, :, LANES:] + dk_ref[5, :, LANES:]
        d_kr = jnp.where(lo, even, 0.0) + pltpu.roll(jnp.where(lo, 0.0, odd), MROPE, 1)
        dt_ref[...] = jnp.where(lo, unrope(d_kr), 0.0)

    def full(shape):
        return pl.BlockSpec(shape, lambda i: (0,) * len(shape))

    return pl.pallas_call(
        body, grid=(s // tr,),
        in_specs=[pl.BlockSpec((tr, MQL), lambda i: (i, OMQ // MQL)),
                  pl.BlockSpec((tr, MKVL), lambda i: (i, OMKV // MKVL)),
                  pl.BlockSpec((tr, LANES), lambda i: (i, 0)),
                  pl.BlockSpec((tr, LANES), lambda i: (i, 0)),
                  full((1, MQL)), full((1, MKVL)), full((MQL, MQW)), full((MKVL, MKVW)),
                  pl.BlockSpec((MH, tr, 2 * LANES), lambda i: (0, i, 0)),
                  pl.BlockSpec((MH, tr, 2 * LANES), lambda i: (0, i, 0)),
                  pl.BlockSpec((MH, tr, LANES), lambda i: (0, i, 0))],
        out_specs=[pl.BlockSpec((tr, MQL), lambda i: (i, 0)), pl.BlockSpec((tr, MKVL), lambda i: (i, 0)),
                   pl.BlockSpec((tr, LANES), lambda i: (i, 0)),
                   full((MQL, MQW)), full((MKVL, MKVW)), full((1, MQL)), full((1, MKVL))],
        out_shape=[jax.ShapeDtypeStruct((s, MQL), F32), jax.ShapeDtypeStruct((s, MKVL), F32),
                   jax.ShapeDtypeStruct((s, LANES), F32),
                   jax.ShapeDtypeStruct((MQL, MQW), F32), jax.ShapeDtypeStruct((MKVL, MKVW), F32),
                   jax.ShapeDtypeStruct((1, MQL), F32), jax.ShapeDtypeStruct((1, MKVL), F32)],
        name=name, compiler_params=_cp(("arbitrary",)))(proj, proj, cos, sin, qg, kvg, w_uq, w_ukv, d_q, d_k, d_v)


ATT_SCALE = (MNOPE + MROPE) ** -0.5


def _attn_fwd(q, k, v, name):
    s = q.shape[1]
    tq = _rows(s)

    def body(q_ref, k_ref, v_ref, o_ref, lse_ref):
        sc = _dot_nt(q_ref[0], k_ref[0]) * ATT_SCALE
        m = jnp.max(sc, axis=-1, keepdims=True)
        p = jnp.exp(sc - m)
        l = jnp.sum(p, axis=-1, keepdims=True)
        o_ref[...] = _dot(p, v_ref[0]) / l
        lse_ref[0] = m + jnp.log(l)

    return pl.pallas_call(
        body, grid=(MH, s // tq),
        in_specs=[pl.BlockSpec((1, tq, 2 * LANES), lambda h, i: (h, i, 0)),
                  pl.BlockSpec((1, s, 2 * LANES), lambda h, i: (h, 0, 0)),
                  pl.BlockSpec((1, s, LANES), lambda h, i: (h, 0, 0))],
        out_specs=[pl.BlockSpec((tq, LANES), lambda h, i: (i, h)),
                   pl.BlockSpec((1, tq, 1), lambda h, i: (h, i, 0))],
        out_shape=[jax.ShapeDtypeStruct((s, MW), F32), jax.ShapeDtypeStruct((MH, s, 1), F32)],
        name=name, compiler_params=_cp(("parallel", "parallel")))(q, k, v)


def _attn_bwd(q, k, v, o, lse, d_o, name):
    s = q.shape[1]
    tq = _rows(s)

    def body(q_ref, k_ref, v_ref, o_ref, lse_ref, do_ref, dq_ref, dk_ref, dv_ref):
        @pl.when(pl.program_id(1) == 0)
        def _():
            dk_ref[...] = jnp.zeros_like(dk_ref)
            dv_ref[...] = jnp.zeros_like(dv_ref)

        qv, kv, do = q_ref[0], k_ref[0], do_ref[...]
        p = jnp.exp(_dot_nt(qv, kv) * ATT_SCALE - lse_ref[0])
        delta = jnp.sum(do * o_ref[...], axis=-1, keepdims=True)
        ds = p * (_dot_nt(do, v_ref[0]) - delta) * ATT_SCALE
        dq_ref[0] = _dot(ds, kv)
        dk_ref[0] += _dot_tn(ds, qv)
        dv_ref[0] += _dot_tn(p, do)

    return pl.pallas_call(
        body, grid=(MH, s // tq),
        in_specs=[pl.BlockSpec((1, tq, 2 * LANES), lambda h, i: (h, i, 0)),
                  pl.BlockSpec((1, s, 2 * LANES), lambda h, i: (h, 0, 0)),
                  pl.BlockSpec((1, s, LANES), lambda h, i: (h, 0, 0)),
                  pl.BlockSpec((tq, LANES), lambda h, i: (i, h)),
                  pl.BlockSpec((1, tq, 1), lambda h, i: (h, i, 0)),
                  pl.BlockSpec((tq, LANES), lambda h, i: (i, h))],
        out_specs=[pl.BlockSpec((1, tq, 2 * LANES), lambda h, i: (h, i, 0)),
                   pl.BlockSpec((1, s, 2 * LANES), lambda h, i: (h, 0, 0)),
                   pl.BlockSpec((1, s, LANES), lambda h, i: (h, 0, 0))],
        out_shape=[jax.ShapeDtypeStruct((MH, s, 2 * LANES), F32), jax.ShapeDtypeStruct((MH, s, 2 * LANES), F32),
                   jax.ShapeDtypeStruct((MH, s, LANES), F32)],
        name=name, compiler_params=_cp(("parallel", "arbitrary")))(q, k, v, o, lse, d_o)


def _merge_fwd(o_f, o_b, o_att, pre, proj, gng, mog, cog, name):
    s = proj.shape[0]
    tr = _rows(s)

    def body(of_ref, ob_ref, oa_ref, pre_ref, z_ref, gng_ref, mog_ref, cog_ref, y_ref):
        z = z_ref[...]
        sz = z * _sigmoid(z)
        osum = of_ref[...] + ob_ref[...]
        gg = gng_ref[...]
        for h in range(GH):
            sl = slice(h * GDV, (h + 1) * GDV)
            y_ref[:, sl] = (_rms(osum[:, sl], gg) * sz[:, sl]).astype(y_ref.dtype)
        y_ref[:, GW:GW + MW] = (_rms(oa_ref[...], mog_ref[...]) * sz[:, GW:GW + MW]).astype(y_ref.dtype)
        y_ref[:, GW + MW:] = (_rms(pre_ref[...], cog_ref[...]) * sz[:, GW + MW:]).astype(y_ref.dtype)

    def row(w):
        return pl.BlockSpec((tr, w), lambda i: (i, 0))

    def vec(w):
        return pl.BlockSpec((1, w), lambda i: (0, 0))

    return pl.pallas_call(
        body, grid=(s // tr,),
        in_specs=[row(GW), row(GW), row(MW), row(CONV_CH), row(D_MIX), vec(GDV), vec(MW), vec(CONV_CH)],
        out_specs=row(D_MIX), out_shape=jax.ShapeDtypeStruct((s, D_MIX), MXU),
        name=name, compiler_params=_cp(("parallel",)))(o_f, o_b, o_att, pre, proj, gng, mog, cog)


def _merge_bwd(d_y, o_f, o_b, o_att, pre, proj, gng, mog, cog, name):
    s = proj.shape[0]
    tr = _rows(s)

    def body(dy_ref, of_ref, ob_ref, oa_ref, pre_ref, z_ref, gng_ref, mog_ref, cog_ref,
             dz_ref, dos_ref, doa_ref, dpre_ref, dgng_ref, dmog_ref, dcog_ref):
        @pl.when(pl.program_id(0) == 0)
        def _():
            for r in (dgng_ref, dmog_ref, dcog_ref):
                r[...] = jnp.zeros_like(r)

        z, dy = z_ref[...], dy_ref[...]
        sg = _sigmoid(z)
        sz = z * sg
        dsz = sg * (1.0 + z * (1.0 - sg))
        dcat = dy * sz
        dyz = dy * dsz
        osum = of_ref[...] + ob_ref[...]
        gg = gng_ref[...]
        dgg = jnp.zeros_like(gg)
        for h in range(GH):
            sl = slice(h * GDV, (h + 1) * GDV)
            dz_ref[:, sl] = dyz[:, sl] * _rms(osum[:, sl], gg)
            dx, dg = _rms_bwd(dcat[:, sl], osum[:, sl], gg)
            dos_ref[:, sl] = dx
            dgg += dg
        dgng_ref[...] += dgg
        sl = slice(GW, GW + MW)
        oa, mg = oa_ref[...], mog_ref[...]
        dz_ref[:, sl] = dyz[:, sl] * _rms(oa, mg)
        dx, dg = _rms_bwd(dcat[:, sl], oa, mg)
        doa_ref[...] = dx
        dmog_ref[...] += dg
        sl = slice(GW + MW, D_MIX)
        pv, cg = pre_ref[...], cog_ref[...]
        dz_ref[:, sl] = dyz[:, sl] * _rms(pv, cg)
        dx, dg = _rms_bwd(dcat[:, sl], pv, cg)
        dpre_ref[...] = dx
        dcog_ref[...] += dg

    def row(w):
        return pl.BlockSpec((tr, w), lambda i: (i, 0))

    def vec(w):
        return pl.BlockSpec((1, w), lambda i: (0, 0))

    def rs(w):
        return jax.ShapeDtypeStruct((s, w), F32)

    def vs(w):
        return jax.ShapeDtypeStruct((1, w), F32)

    return pl.pallas_call(
        body, grid=(s // tr,),
        in_specs=[row(D_MIX), row(GW), row(GW), row(MW), row(CONV_CH), row(D_MIX), vec(GDV), vec(MW), vec(CONV_CH)],
        out_specs=[row(D_MIX), row(GW), row(MW), row(CONV_CH), vec(GDV), vec(MW), vec(CONV_CH)],
        out_shape=[rs(D_MIX), rs(GW), rs(MW), rs(CONV_CH), vs(GDV), vs(MW), vs(CONV_CH)],
        name=name, compiler_params=_cp(("arbitrary",)))(d_y, o_f, o_b, o_att, pre, proj, gng, mog, cog)


def _assemble_dproj(d_z, d_cb, d_cc, d_cx, d_mkv, dv_f, dv_b, dq_f, dq_b, dk_f, dk_b, d_mq, dt_m, dt_f, dt_b, name):
    s = d_z.shape[0]
    tr = _rows(s)

    def body(dz, dcb, dcc, dcx, dmkv, dvf, dvb, dqf, dqb, dkf, dkb, dmq, dtm, dtf, dtb, out):
        dt = out.dtype
        out[:, OZ:OZ + D_MIX] = dz[...].astype(dt)
        out[:, OCB:OCB + CONV_CH] = dcb[...].astype(dt)
        out[:, OCC:OCC + CONV_CH] = dcc[...].astype(dt)
        out[:, OCX:OCX + CONV_CH] = dcx[...].astype(dt)
        out[:, OMKV:OMKV + MKVL] = dmkv[...].astype(dt)
        out[:, OGV:OGV + GW] = (dvf[...] + dvb[...]).astype(dt)
        out[:, OGQ:OGQ + GQK] = (dqf[...] + dqb[...]).astype(dt)
        out[:, OGK:OGK + GQK] = (dkf[...] + dkb[...]).astype(dt)
        out[:, OMQ:OMQ + MQL] = dmq[...].astype(dt)
        out[:, OT:OT + LANES] = (dtm[...] + dtf[...] + dtb[...]).astype(dt)

    args = (d_z, d_cb, d_cc, d_cx, d_mkv, dv_f, dv_b, dq_f, dq_b, dk_f, dk_b, d_mq, dt_m, dt_f, dt_b)
    return pl.pallas_call(
        body, grid=(s // tr,),
        in_specs=[pl.BlockSpec((tr, a.shape[1]), lambda i: (i, 0)) for a in args],
        out_specs=pl.BlockSpec((tr, PW), lambda i: (i, 0)),
        out_shape=jax.ShapeDtypeStruct((s, PW), MXU), name=name, compiler_params=_cp(("parallel",)))(*args)


def _layer_fwd(x, mod, wt, cos, sin, tag):
    shift, scale, gate = mod
    h = _norm_mod(x, wt["norm_g"], scale, shift, f"norm_mod_{tag}")
    (proj,) = _matmul(h, wt["w_in"], dims="nn", tm=1024, tn=256, tk=2048, out_dtypes=(F32,), name=f"in_proj_{tag}")
    o_f, st_f = _gla_fwd(proj, wt["wg_pad_f"], wt["bg_f"], False, f"gla_fwd_f_{tag}")
    o_b, st_b = _gla_fwd(proj, wt["wg_pad_b"], wt["bg_b"], True, f"gla_fwd_b_{tag}")
    q, k, v = _mla_prep(proj, cos, sin, wt["q_norm_g"], wt["kv_norm_g"], wt["w_uq"], wt["w_ukv"], f"mla_prep_{tag}")
    o_att, lse = _attn_fwd(q, k, v, f"attn_fwd_{tag}")
    pre = _conv_fwd(proj, wt["conv_w"], f"conv_fwd_{tag}")
    y = _merge_fwd(o_f, o_b, o_att, pre, proj, wt["gla_norm_g"], wt["mla_out_g"], wt["conv_out_g"], f"merge_fwd_{tag}")
    x_new, u = _matmul(y, wt["w_out"], dims="nn", tm=1024, tn=512, tk=2048, out_dtypes=(F32, F32),
                       name=f"out_proj_{tag}", epilogue=lambda acc, xv, gv: (xv + gv * acc, acc),
                       extras=(x, gate), extra_kinds=("mn", "n"))
    saved = dict(x=x, h=h, proj=proj, o_f=o_f, o_b=o_b, st_f=st_f, st_b=st_b, q=q, k=k, v=v,
                 o_att=o_att, lse=lse, pre=pre, y=y, u=u)
    return x_new, saved


def _layer_bwd(d_out, sv, mod, wt, cos, sin, tag, ship=None):
    shift, scale, gate = mod
    proj = sv["proj"]
    d_u, d_gate = _gate_bwd(d_out, sv["u"], gate, f"gate_bwd_{tag}")
    (g_w_out,) = _matmul(sv["y"], d_u, dims="tn", tm=1024, tn=512, tk=2048, out_dtypes=(MXU,), name=f"out_proj_dw_{tag}")
    (d_y,) = _matmul(d_u, wt["w_out"], dims="nt", tm=1024, tn=512, tk=2048, out_dtypes=(F32,), name=f"out_proj_dx_{tag}",
                     after=(g_w_out,))
    d_z, d_osum, d_oatt, d_pre, d_gng, d_mog, d_cog = _merge_bwd(
        d_y, sv["o_f"], sv["o_b"], sv["o_att"], sv["pre"], proj, wt["gla_norm_g"], wt["mla_out_g"], wt["conv_out_g"],
        f"merge_bwd_{tag}")
    d_cb, d_cc, d_cx, d_conv_w = _conv_bwd(proj, wt["conv_w"], d_pre, f"conv_bwd_{tag}")
    d_q, d_k, d_v = _attn_bwd(sv["q"], sv["k"], sv["v"], sv["o_att"], sv["lse"], d_oatt, f"attn_bwd_{tag}")
    d_mq, d_mkv, dt_m, g_w_uq, g_w_ukv, d_qg, d_kvg = _mla_prep_bwd(
        proj, cos, sin, wt["q_norm_g"], wt["kv_norm_g"], wt["w_uq"], wt["w_ukv"], d_q, d_k, d_v, f"mla_prep_bwd_{tag}")
    dq_f, dk_f, dv_f, dt_f, d_wg_f, d_bg_f = _gla_bwd(proj, wt["wg_pad_f"], wt["bg_f"], sv["st_f"], d_osum, False,
                                                     f"gla_bwd_f_{tag}")
    dq_b, dk_b, dv_b, dt_b, d_wg_b, d_bg_b = _gla_bwd(proj, wt["wg_pad_b"], wt["bg_b"], sv["st_b"], d_osum, True,
                                                     f"gla_bwd_b_{tag}")
    d_proj = _assemble_dproj(d_z, d_cb, d_cc, d_cx, d_mkv, dv_f, dv_b, dq_f, dq_b, dk_f, dk_b, d_mq, dt_m, dt_f, dt_b,
                             f"assemble_dproj_{tag}")
    (g_w_in,) = _matmul(sv["h"], d_proj, dims="tn", tm=1024, tn=256, tk=2048, out_dtypes=(MXU,), name=f"in_proj_dw_{tag}")
    big = dict(w_in=g_w_in, w_out=g_w_out, w_uq=g_w_uq, w_ukv=g_w_ukv)
    (d_h,) = _matmul(d_proj, wt["w_in"], dims="nt", tm=1024, tn=512, tk=PW // 2, out_dtypes=(F32,), name=f"in_proj_dx_{tag}",
                     after=(g_w_in,) if ship is None else ship(big))
    d_x, d_shift, d_scale, d_ng = _norm_mod_bwd(d_h, sv["x"], d_out, wt["norm_g"], scale, f"norm_mod_bwd_{tag}")
    grads = dict(w_in=g_w_in, w_out=g_w_out, w_uq=g_w_uq, w_ukv=g_w_ukv, norm_g=d_ng,
                 wg_pad_f=d_wg_f, bg_f=d_bg_f, wg_pad_b=d_wg_b, bg_b=d_bg_b, gla_norm_g=d_gng,
                 q_norm_g=d_qg, kv_norm_g=d_kvg, mla_out_g=d_mog, conv_w=d_conv_w, conv_out_g=d_cog)
    return d_x, (d_shift, d_scale, d_gate), grads


def _perm_in_cols(w):
    pad = jnp.zeros(w.shape[:-1] + (PW - IN_DIM,), w.dtype)
    return jnp.concatenate([w[..., 3808:5856], w[..., 2272:3808], w[..., 1952:2208], w[..., 768:1536], w[..., 0:768],
                            w[..., 1568:1952], w[..., 2208:2272], w[..., 1536:1568], pad], axis=-1)


def _unperm_in_cols(g):
    return jnp.concatenate([g[..., OGQ:OGQ + 2 * GQK], g[..., OGV:OGV + GW], g[..., OT + MROPE:OT + MROPE + 2 * GRANK],
                            g[..., OMQ:OMQ + MQL], g[..., OMKV:OMKV + MKVL], g[..., OT:OT + MROPE],
                            g[..., OCB:OCB + 3 * CONV_CH], g[..., OZ:OZ + D_MIX]], axis=-1)


IN_SEGS = ((3808, 5856), (2272, 3808), (1952, 2208), (768, 1536), (0, 768), (1568, 1952), (2208, 2272), (1536, 1568))
UQ_SEGS = (tuple((h * (MNOPE + MROPE), h * (MNOPE + MROPE) + MNOPE) for h in range(MH))
           + tuple((h * (MNOPE + MROPE) + MNOPE, (h + 1) * (MNOPE + MROPE)) for h in range(MH)))


def _perm_gathered(g, segs, width):
    per = g.shape[-1]
    parts, total = [], 0
    for a, b in segs:
        c = a
        while c < b:
            j = c // per
            hi = min(b, (j + 1) * per)
            parts.append(g[j, :, c - j * per:hi - j * per])
            c = hi
        total += b - a
    if width > total:
        parts.append(jnp.zeros((g.shape[1], width - total), g.dtype))
    return jnp.concatenate(parts, axis=1)


def _scatter_perm(gp, segs, per):
    offs, o = [], 0
    for a, b in segs:
        offs.append((a, b, o))
        o += b - a
    blocks = []
    for j in range(N_DEV):
        lo, hi = j * per, (j + 1) * per
        pieces = []
        for a, b, o in sorted(offs):
            s0, s1 = max(a, lo), min(b, hi)
            if s0 < s1:
                pieces.append(gp[:, o + s0 - a:o + s1 - a])
        blocks.append(jnp.concatenate(pieces, axis=1))
    return jnp.stack(blocks)


def _perm_uq_cols(w):
    w3 = w.reshape(w.shape[:-1] + (MH, MNOPE + MROPE))
    return jnp.concatenate([w3[..., :MNOPE].reshape(w.shape[:-1] + (MH * MNOPE,)),
                            w3[..., MNOPE:].reshape(w.shape[:-1] + (MH * MROPE,))], axis=-1)


def _unperm_uq_cols(g):
    nope = g[..., :MH * MNOPE].reshape(g.shape[:-1] + (MH, MNOPE))
    rope = g[..., MH * MNOPE:].reshape(g.shape[:-1] + (MH, MROPE))
    return jnp.concatenate([nope, rope], axis=-1).reshape(g.shape[:-1] + (MQW,))


def _prep_layer_weights(w_in, w_out, w_uq, w_ukv, small):
    def vec(v):
        return v.reshape(1, -1).astype(F32)

    zeros = functools.partial(jnp.zeros, dtype=F32)
    wg_f, wg_b = small["gla_wg_f"].astype(F32), small["gla_wg_b"].astype(F32)
    wg_pad_f = jnp.concatenate([zeros((MROPE, GQK)), wg_f, zeros((LANES - MROPE - GRANK, GQK))], axis=0)
    wg_pad_b = jnp.concatenate([zeros((MROPE + GRANK, GQK)), wg_b, zeros((LANES - MROPE - 2 * GRANK, GQK))], axis=0)
    return dict(w_in=w_in.astype(MXU), w_out=w_out.astype(MXU), w_uq=w_uq.astype(MXU),
                w_ukv=w_ukv.astype(MXU), norm_g=vec(small["norm_g"]), wg_pad_f=wg_pad_f, wg_pad_b=wg_pad_b,
                bg_f=vec(small["gla_bg_f"]), bg_b=vec(small["gla_bg_b"]), gla_norm_g=vec(small["gla_norm_g"]),
                q_norm_g=vec(small["mla_q_norm_g"]), kv_norm_g=vec(small["mla_kv_norm_g"]),
                mla_out_g=vec(small["mla_out_g"]), conv_w=small["conv_w"].astype(F32),
                conv_out_g=vec(small["conv_out_g"]))


def _natural_grads(gr):
    return dict(w_in=_unperm_in_cols(gr["w_in"]), w_out=gr["w_out"], mla_w_uq=_unperm_uq_cols(gr["w_uq"]),
                mla_w_ukv=gr["w_ukv"], norm_g=gr["norm_g"][0],
                gla_wg_f=gr["wg_pad_f"][MROPE:MROPE + GRANK], gla_bg_f=gr["bg_f"][0],
                gla_wg_b=gr["wg_pad_b"][MROPE + GRANK:MROPE + 2 * GRANK], gla_bg_b=gr["bg_b"][0],
                gla_norm_g=gr["gla_norm_g"][0], mla_q_norm_g=gr["q_norm_g"][0], mla_kv_norm_g=gr["kv_norm_g"][0],
                mla_out_g=gr["mla_out_g"][0], conv_w=gr["conv_w"], conv_out_g=gr["conv_out_g"][0])


def _exchange(arrs, name, scatter, space):
    n = len(arrs)

    def body(*refs):
        ins, outs = refs[:n], refs[n:2 * n]
        send_sems, recv_sems, loc_sems = refs[2 * n:]
        ax, ay, ac = lax.axis_index("x"), lax.axis_index("y"), lax.axis_index("c")
        me = 4 * ax + 2 * ay + ac

        def src(a, to):
            return ins[a].at[to] if scatter else ins[a]

        def remote(a, r, dst_slot):
            px = 1 - ax if r & 4 else ax
            py = 1 - ay if r & 2 else ay
            pc = 1 - ac if r & 1 else ac
            return pltpu.make_async_remote_copy(
                src_ref=src(a, 4 * px + 2 * py + pc), dst_ref=outs[a].at[dst_slot(4 * px + 2 * py + pc)],
                send_sem=send_sems.at[a, r - 1], recv_sem=recv_sems.at[a, r - 1],
                device_id=(px, py, pc), device_id_type=MESH)

        locs = [pltpu.make_async_copy(src(a, me), outs[a].at[me], loc_sems.at[a]) for a in range(n)]
        for cp in locs:
            cp.start()
        sends = [remote(a, r, lambda peer: me) for r in range(1, N_DEV) for a in range(n)]
        for cp in sends:
            cp.start()
        for r in range(1, N_DEV):
            for a in range(n):
                remote(a, r, lambda peer: peer).wait_recv()
        for cp in sends:
            cp.wait_send()
        for cp in locs:
            cp.wait()

    def out_shape(a):
        return jax.ShapeDtypeStruct(a.shape if scatter else (N_DEV,) + a.shape, a.dtype)

    spec = pl.BlockSpec(memory_space=space)
    return pl.pallas_call(
        body, in_specs=[spec] * n, out_specs=[spec] * n, out_shape=[out_shape(a) for a in arrs],
        scratch_shapes=[pltpu.SemaphoreType.DMA((n, N_DEV - 1)), pltpu.SemaphoreType.DMA((n, N_DEV - 1)),
                        pltpu.SemaphoreType.DMA((n,))],
        name=name, compiler_params=pltpu.CompilerParams(vmem_limit_bytes=VMEM_LIMIT))(*arrs)


def _peer(r):
    ax, ay, ac = lax.axis_index("x"), lax.axis_index("y"), lax.axis_index("c")
    px = 1 - ax if r & 4 else ax
    py = 1 - ay if r & 2 else ay
    pc = 1 - ac if r & 1 else ac
    return (px, py, pc), 4 * px + 2 * py + pc


def _slot(rel_div):
    rel, div = rel_div
    idx = _peer(rel)[1]
    return idx if div == 1 else idx // div


AG_SPREAD = tuple((r, None, (0, 1), (r, 1)) for r in (1, 2, 4, 6))
AG_FORWARD = tuple((1, (k, 1), (k, 1), (1 ^ k, 1)) for k in (2, 4, 6))
RS_PAIR = tuple((1, (1 ^ k, 1), (1 ^ k, 2), (k, 2)) for k in (0, 2, 4, 6))
RS_CHIPS = tuple((r, (r, 2), (0, 2), (r, 2)) for r in (2, 4, 6))


def _plan_copies(plan, n, src_refs, land_refs, send_sems, recv_sems, arriving):
    out = []
    for i, (r, src, dst, recv) in enumerate(plan):
        peer = _peer(r)[0]
        for a in range(n):
            out.append(pltpu.make_async_remote_copy(
                src_ref=src_refs[a] if src is None else src_refs[a].at[_slot(src)],
                dst_ref=land_refs[a].at[_slot(recv if arriving else dst)],
                send_sem=send_sems.at[i * n + a], recv_sem=recv_sems.at[i * n + a],
                device_id=peer, device_id_type=MESH))
    return out


def _exchange_hbm(plan, srcs, lands, name):
    n = len(lands)
    fresh = isinstance(lands[0], jax.ShapeDtypeStruct)
    ins = ([] if srcs is None else list(srcs)) + ([] if fresh else list(lands))
    ns = 0 if srcs is None else n

    def body(*refs):
        outs = refs[len(ins):len(ins) + n]
        send_sems, recv_sems = refs[-2:]
        src_refs = refs[:n] if srcs is not None else refs[ns:ns + n]
        sends = _plan_copies(plan, n, src_refs, outs, send_sems, recv_sems, False)
        for cp in sends:
            cp.start()
        for cp in _plan_copies(plan, n, src_refs, outs, send_sems, recv_sems, True):
            cp.wait_recv()
        for cp in sends:
            cp.wait_send()

    hbm = pl.BlockSpec(memory_space=pltpu.HBM)
    k = len(plan) * n
    return pl.pallas_call(
        body, name=name, in_specs=[hbm] * len(ins), out_specs=[hbm] * n,
        out_shape=[jax.ShapeDtypeStruct(a.shape, a.dtype) for a in lands],
        scratch_shapes=[pltpu.SemaphoreType.DMA((k,)), pltpu.SemaphoreType.DMA((k,))],
        input_output_aliases={} if fresh else {ns + i: i for i in range(n)},
        compiler_params=pltpu.CompilerParams(vmem_limit_bytes=VMEM_LIMIT))(*ins)


def _plan_start(plan, srcs, land_shapes, after, name):
    n = len(srcs)

    def body(*refs):
        src_refs, land_refs = refs[:n], refs[n:2 * n]
        send_sems, recv_sems = refs[2 * n + 1], refs[2 * n + 2]
        for cp in _plan_copies(plan, n, src_refs, land_refs, send_sems, recv_sems, False):
            cp.start()
        refs[-1][...] = jnp.zeros_like(refs[-1])

    hbm = pl.BlockSpec(memory_space=pltpu.HBM)
    sem = pl.BlockSpec(memory_space=pltpu.SEMAPHORE)
    k = len(plan) * n
    srcs = [pltpu.with_memory_space_constraint(a, pltpu.HBM) for a in srcs]
    lands = [pltpu.with_memory_space_constraint(lax.empty(shp, a.dtype), pltpu.HBM) for shp, a in zip(land_shapes, srcs)]
    res = pl.pallas_call(
        body, name=name,
        in_specs=[hbm] * (2 * n) + [pl.BlockSpec(memory_space=pl.ANY)],
        out_specs=[sem, sem] + [hbm] * (2 * n) + [pl.BlockSpec(memory_space=pltpu.VMEM)],
        out_shape=[pltpu.SemaphoreType.DMA((k,)), pltpu.SemaphoreType.DMA((k,))]
        + [pltpu.HBM(a.shape, a.dtype) for a in srcs] + [pltpu.HBM(shp, a.dtype) for shp, a in zip(land_shapes, srcs)]
        + [jax.ShapeDtypeStruct((8, LANES), F32)],
        input_output_aliases={i: 2 + i for i in range(2 * n)},
        compiler_params=pltpu.CompilerParams(has_side_effects=pltpu.SideEffectType.DATAFLOW_SIDE_EFFECTING),
    )(*srcs, *lands, after)
    return res[0], res[1], list(res[2:2 + n]), list(res[2 + n:2 + 2 * n]), res[-1]


def _plan_wait(plan, handle, after, name):
    send_sems, recv_sems, srcs, lands, _ = handle
    n = len(srcs)
    after = list(after)

    def body(*refs):
        src_refs, land_refs = refs[:n], refs[n:2 * n]
        ssem, rsem = refs[2 * n], refs[2 * n + 1]
        for cp in _plan_copies(plan, n, src_refs, land_refs, ssem, rsem, False):
            cp.wait_send()
        for cp in _plan_copies(plan, n, src_refs, land_refs, ssem, rsem, True):
            cp.wait_recv()

    hbm = pl.BlockSpec(memory_space=pltpu.HBM)
    sem = pl.BlockSpec(memory_space=pltpu.SEMAPHORE)
    res = pl.pallas_call(
        body, name=name,
        in_specs=[hbm] * (2 * n) + [sem, sem] + [pl.BlockSpec(memory_space=pl.ANY)] * len(after),
        out_specs=[hbm] * (2 * n),
        out_shape=[pltpu.HBM(a.shape, a.dtype) for a in srcs] + [pltpu.HBM(a.shape, a.dtype) for a in lands],
        input_output_aliases={i: i for i in range(2 * n)},
        compiler_params=pltpu.CompilerParams(has_side_effects=pltpu.SideEffectType.DATAFLOW_SIDE_EFFECTING),
    )(*srcs, *lands, send_sems, recv_sems, *after)
    return list(res[n:])


def _pair_sum(send, got, core, name):
    _, r, c = send.shape
    tr = 256 if r % 256 == 0 else r

    def body(core_ref, s_ref, g_ref, o_ref):
        o_ref[0] = (s_ref[0].astype(F32) + g_ref[0].astype(F32)).astype(o_ref.dtype)

    return pl.pallas_call(
        body, name=name,
        grid_spec=pltpu.PrefetchScalarGridSpec(
            num_scalar_prefetch=1, grid=(N_DEV // 2, r // tr),
            in_specs=[pl.BlockSpec((1, tr, c), lambda kc, i, core_ref: (2 * kc + core_ref[0], i, 0)),
                      pl.BlockSpec((1, tr, c), lambda kc, i, core_ref: (kc, i, 0))],
            out_specs=pl.BlockSpec((1, tr, c), lambda kc, i, core_ref: (kc, i, 0))),
        out_shape=jax.ShapeDtypeStruct((N_DEV // 2, r, c), send.dtype),
        compiler_params=_cp(("parallel", "parallel")))(core, send, got)


def _ada_mod(c_all, ada_w, ada_b_cols, name):
    nl, d, wc = ada_w.shape

    def body(c_ref, w_ref, b_ref, ca_ref, mod_ref):
        cv = c_ref[...]
        ca = cv * _sigmoid(cv)
        ca_ref[...] = ca
        mod_ref[0] = _dotf(ca, w_ref[0]) + b_ref[0]

    return pl.pallas_call(
        body, grid=(nl,),
        in_specs=[pl.BlockSpec((N_DEV, d), lambda l: (0, 0)), pl.BlockSpec((1, d, wc), lambda l: (l, 0, 0)),
                  pl.BlockSpec((1, 1, wc), lambda l: (l, 0, 0))],
        out_specs=[pl.BlockSpec((N_DEV, d), lambda l: (0, 0)), pl.BlockSpec((1, N_DEV, wc), lambda l: (l, 0, 0))],
        out_shape=[jax.ShapeDtypeStruct((N_DEV, d), F32), jax.ShapeDtypeStruct((nl, N_DEV, wc), F32)],
        name=name, compiler_params=_cp(("arbitrary",)))(c_all, ada_w, ada_b_cols)


def _adam(w, g, m, v):
    m2 = ADAM_B1 * m + (1.0 - ADAM_B1) * g
    v2 = ADAM_B2 * v + (1.0 - ADAM_B2) * (g * g)
    m_hat = m2 / (1.0 - ADAM_B1 ** ADAM_STEP)
    v_hat = v2 / (1.0 - ADAM_B2 ** ADAM_STEP)
    delta = -ADAM_LR * (m_hat / (jnp.sqrt(v_hat) + ADAM_EPS) + ADAM_WD * w)
    return delta, m2, v2


def _ada_grad_adam(c_act, d_mod, w, m, v, name):
    nl, d, wc = w.shape
    tk = min(512, d)

    def body(c_ref, dm_ref, w_ref, m_ref, v_ref, g_ref, dl_ref, m2_ref, v2_ref):
        g = _dotf_tn(c_ref[...], dm_ref[0])
        delta, m2, v2 = _adam(w_ref[0], g, m_ref[0], v_ref[0])
        g_ref[0], dl_ref[0], m2_ref[0], v2_ref[0] = g, delta, m2, v2

    blk = pl.BlockSpec((1, tk, wc), lambda l, i: (l, i, 0))
    shp = jax.ShapeDtypeStruct(w.shape, F32)
    return pl.pallas_call(
        body, grid=(nl, d // tk),
        in_specs=[pl.BlockSpec((N_DEV, tk), lambda l, i: (0, i)), pl.BlockSpec((1, N_DEV, wc), lambda l, i: (l, 0, 0)),
                  blk, blk, blk],
        out_specs=[blk] * 4, out_shape=[shp] * 4, name=name,
        compiler_params=_cp(("parallel", "parallel")))(c_act, d_mod, w, m, v)


def _adam_big(recv, w, m, v, layer, prev, name):
    nl, r, c = w.shape
    tr = 256 if r % 256 == 0 else r
    nparts = recv.shape[0]

    def body(rc_ref, w_ref, m_ref, v_ref, *rest):
        g_ref, dl_ref, m2_ref, v2_ref = rest[-4:]
        g = rc_ref[0].astype(F32)
        for d in range(1, nparts):
            g = g + rc_ref[d].astype(F32)
        delta, m2, v2 = _adam(w_ref[0], g, m_ref[0], v_ref[0])
        g_ref[0], dl_ref[0], m2_ref[0], v2_ref[0] = g, delta, m2, v2

    blk = pl.BlockSpec((1, tr, c), lambda i: (layer, i, 0))
    shp = jax.ShapeDtypeStruct(w.shape, F32)
    prev = () if prev is None else tuple(prev)
    return pl.pallas_call(
        body, grid=(r // tr,),
        in_specs=[pl.BlockSpec((nparts, tr, c), lambda i: (0, i, 0)), blk, blk, blk]
        + [pl.BlockSpec(memory_space=pl.ANY)] * len(prev),
        out_specs=[blk] * 4, out_shape=[shp] * 4, name=name,
        input_output_aliases={4 + j: j for j in range(len(prev))},
        compiler_params=_cp(("parallel",)))(recv, w, m, v, *prev)


def _sum_devices(gathered, name):
    _, r, c = gathered.shape

    def body(g_ref, o_ref):
        acc = g_ref[0]
        for d in range(1, N_DEV):
            acc = acc + g_ref[d]
        o_ref[...] = acc

    spec = pl.BlockSpec(memory_space=pltpu.VMEM)
    return pl.pallas_call(body, in_specs=[spec], out_specs=spec, out_shape=jax.ShapeDtypeStruct((r, c), F32),
                          name=name, compiler_params=pltpu.CompilerParams(vmem_limit_bytes=VMEM_LIMIT))(gathered)


def _adam_small(w, g, m, v, name):
    def body(w_ref, g_ref, m_ref, v_ref, dl_ref, m2_ref, v2_ref):
        dl_ref[...], m2_ref[...], v2_ref[...] = _adam(w_ref[...], g_ref[...], m_ref[...], v_ref[...])

    spec = pl.BlockSpec(memory_space=pltpu.VMEM)
    shp = jax.ShapeDtypeStruct(w.shape, F32)
    return pl.pallas_call(body, in_specs=[spec] * 4, out_specs=[spec] * 3, out_shape=[shp] * 3, name=name,
                          compiler_params=pltpu.CompilerParams(vmem_limit_bytes=VMEM_LIMIT))(w, g, m, v)


def _pack(parts):
    flat = jnp.concatenate([p.reshape(-1).astype(F32) for p in parts])
    assert flat.shape[0] % LANES == 0, flat.shape
    return flat.reshape(-1, LANES)


def _unpack(packed, shapes):
    flat = packed.reshape(-1)
    out, off = [], 0
    for shp in shapes:
        size = 1
        for dim in shp:
            size *= dim
        out.append(flat[off:off + size].reshape(shp))
        off += size
    return out


def _gather_cols(g, per):
    g = jnp.moveaxis(g, 0, -2)
    return g.reshape(g.shape[:-2] + (N_DEV * per,))


def _scatter_cols(g, per):
    return jnp.moveaxis(g.reshape(g.shape[:-1] + (N_DEV, per)), -2, 0)


def _my_cols(full, me, per):
    return lax.dynamic_slice_in_dim(full, me * per, per, axis=full.ndim - 1)


def kernel(x, c, positions, ada_w, ada_b, norm_g, w_in, gla_wg_f, gla_bg_f, gla_wg_b, gla_bg_b, gla_norm_g, mla_q_norm_g, mla_kv_norm_g, mla_w_uq, mla_w_ukv, mla_out_g, conv_w, conv_out_g, w_out, final_g, loss_target, m_ada_w, m_ada_b, m_norm_g, m_w_in, m_gla_wg_f, m_gla_bg_f, m_gla_wg_b, m_gla_bg_b, m_gla_norm_g, m_mla_q_norm_g, m_mla_kv_norm_g, m_mla_w_uq, m_mla_w_ukv, m_mla_out_g, m_conv_w, m_conv_out_g, m_w_out, m_final_g, v_ada_w, v_ada_b, v_norm_g, v_w_in, v_gla_wg_f, v_gla_bg_f, v_gla_wg_b, v_gla_bg_b, v_gla_norm_g, v_mla_q_norm_g, v_mla_kv_norm_g, v_mla_w_uq, v_mla_w_ukv, v_mla_out_g, v_conv_w, v_conv_out_g, v_w_out, v_final_g):
    me = 4 * lax.axis_index("x") + 2 * lax.axis_index("y") + lax.axis_index("c")
    nl = ada_w.shape[0]
    s, d = x.shape[1], x.shape[2]
    ada_cols = ada_w.shape[2]
    wgc, cwc = gla_wg_f.shape[2], conv_w.shape[2]

    (g0,) = _exchange([_pack([c, gla_wg_f, gla_wg_b, conv_w])], "gather_small_in", False, pltpu.VMEM)
    g0 = g0.reshape(N_DEV, -1)
    o1, o2, o3 = d, d + gla_wg_f.size, d + 2 * gla_wg_f.size
    c_all = g0[:, :o1]
    wgf_full = _gather_cols(g0[:, o1:o2].reshape((N_DEV,) + gla_wg_f.shape), wgc)
    wgb_full = _gather_cols(g0[:, o2:o3].reshape((N_DEV,) + gla_wg_b.shape), wgc)
    convw_full = _gather_cols(g0[:, o3:].reshape((N_DEV,) + conv_w.shape), cwc)

    ada_b_cols = _my_cols(ada_b, me, ada_cols).reshape(nl, 1, ada_cols)
    c_act, mod_cols = _ada_mod(c_all, ada_w, ada_b_cols, "ada_mod")
    (g1,) = _exchange([_pack([mod_cols])], "gather_mod", False, pltpu.VMEM)
    mod_all = g1.reshape(N_DEV, nl, N_DEV, ada_cols)
    mod_mine = _gather_cols(lax.dynamic_index_in_dim(mod_all, me, axis=2, keepdims=False), ada_cols)

    inv_freq = ROPE_THETA ** (-jnp.arange(0, MROPE, 2, dtype=F32) / MROPE)
    ang = positions[0].astype(F32)[:, None] * inv_freq
    cos, sin = jnp.tile(jnp.cos(ang), (1, LANES * 2 // MROPE)), jnp.tile(jnp.sin(ang), (1, LANES * 2 // MROPE))

    big = [w_in, w_out, mla_w_uq, mla_w_ukv]
    big_names = ["w_in", "w_out", "mla_w_uq", "mla_w_ukv"]

    def local_blocks(l):
        return [w[l].astype(MXU) for w in big]

    def put_own(lands, own):
        return [lax.dynamic_update_index_in_dim(ld, o, me, 0) for ld, o in zip(lands, own)]

    def layer_weights(l, gathered):
        gw_in, gw_out, gw_uq, gw_ukv = gathered
        small = dict(norm_g=norm_g[l], gla_wg_f=wgf_full[l], gla_bg_f=gla_bg_f[l], gla_wg_b=wgb_full[l],
                     gla_bg_b=gla_bg_b[l], gla_norm_g=gla_norm_g[l], mla_q_norm_g=mla_q_norm_g[l],
                     mla_kv_norm_g=mla_kv_norm_g[l], mla_out_g=mla_out_g[l], conv_w=convw_full[l],
                     conv_out_g=conv_out_g[l])
        return _prep_layer_weights(_perm_gathered(gw_in, IN_SEGS, PW), gw_out.reshape((-1,) + gw_out.shape[2:]),
                                   _perm_gathered(gw_uq, UQ_SEGS, MQW), _gather_cols(gw_ukv, mla_w_ukv.shape[2]),
                                   small)

    def land_shapes(blocks, slots):
        return [jax.ShapeDtypeStruct((slots,) + b.shape, b.dtype) for b in blocks]

    nxt = local_blocks(0)
    lands = _exchange_hbm(AG_SPREAD, nxt, land_shapes(nxt, N_DEV), "gather_weights_l0_spread")
    gathered = put_own(_exchange_hbm(AG_FORWARD, None, lands, "gather_weights_l0_forward"), nxt)
    h = x[0]
    saved, layers, mods = [], [], []
    for l in range(nl):
        shift, scale, gate = (mod_mine[l, i * d:(i + 1) * d].reshape(1, d) for i in range(3))
        if l + 1 < nl:
            nxt = local_blocks(l + 1)
            pending = _plan_start(AG_SPREAD, nxt, [(N_DEV,) + b.shape for b in nxt], gathered[0],
                                  f"gather_weights_l{l + 1}_start")
            scale = scale + pending[-1][0, 0]
        layers.append(layer_weights(l, gathered))
        mods.append((shift, scale, gate))
        h, sv = _layer_fwd(h, mods[l], layers[l], cos, sin, f"l{l}")
        saved.append(sv)
        if l + 1 < nl:
            lands = _plan_wait(AG_SPREAD, pending, [h], f"gather_weights_l{l + 1}_wait")
            gathered = put_own(_exchange_hbm(AG_FORWARD, None, lands, f"gather_weights_l{l + 1}_forward"), nxt)
    loss_part, d_h, d_final_g = _final_loss(h, final_g.reshape(1, d), loss_target[0], "final_loss")
    loss = lax.psum(loss_part[0, 0], ("x", "y", "c"))

    def grad_sends(gr):
        return [_scatter_perm(gr["w_in"], IN_SEGS, w_in.shape[2]).astype(MXU),
                gr["w_out"].reshape((N_DEV,) + w_out.shape[1:]).astype(MXU),
                _scatter_perm(gr["w_uq"], UQ_SEGS, mla_w_uq.shape[2]).astype(MXU),
                _scatter_cols(gr["w_ukv"], mla_w_ukv.shape[2]).astype(MXU)]

    my_chip = me // 2
    my_core = (me % 2).astype(jnp.int32).reshape(1)

    def chip_sums(gr, tag):
        sends = grad_sends(gr)
        got = _exchange_hbm(RS_PAIR, sends, land_shapes([sd[0] for sd in sends], N_DEV // 2), f"scatter_grads_{tag}_pair")
        return [_pair_sum(sd, gt, my_core, f"pair_sum_{n}_{tag}") for sd, gt, n in zip(sends, got, big_names)]

    def with_own_chip(lands, sums):
        return [lax.dynamic_update_index_in_dim(ld, lax.dynamic_index_in_dim(sm, my_chip, axis=0, keepdims=False),
                                                my_chip, 0) for ld, sm in zip(lands, sums)]

    d_mods, grads, recv = [None] * nl, [None] * nl, [None] * nl
    flight = {}
    for l in reversed(range(nl)):
        def ship(big_grads, l=l):
            if l + 1 in flight:
                pend, sm = flight.pop(l + 1)
                recv[l + 1] = with_own_chip(_plan_wait(RS_CHIPS, pend, list(big_grads.values()),
                                                       f"scatter_grads_l{l + 1}_wait"), sm)
            sm = chip_sums(big_grads, f"l{l}")
            after = recv[l + 1][0] if l + 1 < nl else big_grads["w_in"]
            flight[l] = (_plan_start(RS_CHIPS, sm, [a.shape for a in sm], after, f"scatter_grads_l{l}_start"), sm)
            return (flight[l][0][-1],)

        d_h, d_mods[l], gr = _layer_bwd(d_h, saved[l], mods[l], layers[l], cos, sin, f"l{l}", ship)
        grads[l] = _natural_grads(gr)
    pending, sums = flight.pop(0)
    grad_x = d_h[None]

    def stacked(name):
        return jnp.stack([grads[l][name] for l in range(nl)])

    small_names = ["norm_g", "gla_wg_f", "gla_bg_f", "gla_wg_b", "gla_bg_b", "gla_norm_g", "mla_q_norm_g",
                   "mla_kv_norm_g", "mla_out_g", "conv_w", "conv_out_g"]
    d_mod_mine = jnp.stack([jnp.concatenate(d_mods[l], axis=-1)[0] for l in range(nl)])
    parts = [d_mod_mine] + [stacked(n) for n in small_names] + [d_final_g]
    shapes = [p.shape for p in parts]
    (g2,) = _exchange([_pack(parts)], "gather_small_grads", False, pltpu.VMEM)
    d_mod_all = g2.reshape(N_DEV, -1)[:, :d_mod_mine.size].reshape(N_DEV, nl, 3 * d)
    summed = dict(zip(["ada_b"] + small_names + ["final_g"], _unpack(_sum_devices(g2, "sum_small_grads"), shapes)))
    summed["gla_wg_f"] = _my_cols(summed["gla_wg_f"], me, wgc)
    summed["gla_wg_b"] = _my_cols(summed["gla_wg_b"], me, wgc)
    summed["conv_w"] = _my_cols(summed["conv_w"], me, cwc)

    d_mod_cols = jnp.moveaxis(_my_cols(d_mod_all, me, ada_cols), 0, 1)
    out = {}
    out["ada_w"] = _ada_grad_adam(c_act, d_mod_cols, ada_w, m_ada_w, v_ada_w, "ada_grad_adam")

    given = dict(ada_b=(ada_b, m_ada_b, v_ada_b), norm_g=(norm_g, m_norm_g, v_norm_g),
                 gla_wg_f=(gla_wg_f, m_gla_wg_f, v_gla_wg_f), gla_bg_f=(gla_bg_f, m_gla_bg_f, v_gla_bg_f),
                 gla_wg_b=(gla_wg_b, m_gla_wg_b, v_gla_wg_b), gla_bg_b=(gla_bg_b, m_gla_bg_b, v_gla_bg_b),
                 gla_norm_g=(gla_norm_g, m_gla_norm_g, v_gla_norm_g),
                 mla_q_norm_g=(mla_q_norm_g, m_mla_q_norm_g, v_mla_q_norm_g),
                 mla_kv_norm_g=(mla_kv_norm_g, m_mla_kv_norm_g, v_mla_kv_norm_g),
                 mla_out_g=(mla_out_g, m_mla_out_g, v_mla_out_g), conv_w=(conv_w, m_conv_w, v_conv_w),
                 conv_out_g=(conv_out_g, m_conv_out_g, v_conv_out_g), final_g=(final_g, m_final_g, v_final_g))
    names = list(given)
    wshapes = [given[n][0].shape for n in names]
    packed = [_pack([given[n][i] for n in names]) for i in range(3)]
    g_small = _pack([summed[n].reshape(given[n][0].shape) for n in names])
    res = _adam_small(packed[0], g_small, packed[1], packed[2], "adam_small")
    unpacked = [_unpack(r, wshapes) for r in res]
    for i, n in enumerate(names):
        out[n] = (summed[n].reshape(given[n][0].shape), unpacked[0][i], unpacked[1][i], unpacked[2][i])

    moments = dict(w_in=(m_w_in, v_w_in), w_out=(m_w_out, v_w_out), mla_w_uq=(m_mla_w_uq, v_mla_w_uq),
                   mla_w_ukv=(m_mla_w_ukv, v_mla_w_ukv))
    done = [out["ada_w"][0], res[0]]
    for l in reversed(range(nl)):
        if l == 0:
            recv[0] = with_own_chip(_plan_wait(RS_CHIPS, pending, done, "scatter_grads_l0_wait"), sums)
        for i, n in enumerate(big_names):
            out[n] = _adam_big(recv[l][i], big[i], moments[n][0], moments[n][1], l, out.get(n), f"adam_{n}_l{l}")
        done = done + [out[n][0] for n in big_names]

    order = ["ada_w", "ada_b", "norm_g", "w_in", "gla_wg_f", "gla_bg_f", "gla_wg_b", "gla_bg_b", "gla_norm_g",
             "mla_q_norm_g", "mla_kv_norm_g", "mla_w_uq", "mla_w_ukv", "mla_out_g", "conv_w", "conv_out_g", "w_out",
             "final_g"]
    return (loss, grad_x, *[out[n][0] for n in order], *[out[n][1] for n in order], *[out[n][2] for n in order],
            *[out[n][3] for n in order])
```

```python
import functools

import jax
import jax.numpy as jnp
from jax import lax
from jax.experimental import pallas as pl
from jax.experimental.pallas import tpu as pltpu

F32 = jnp.float32
MXU = jnp.bfloat16
HI = lax.Precision.HIGHEST
N_DEV = 8
MESH = pl.DeviceIdType.MESH

D_MIX = 2048
GH, GDK, GDV = 6, 64, 128
GW = GH * GDV
GQK = GH * GDK
GRANK = 16
GTEMP = 16.0
CHUNK = 64
MH, MQL, MKVL, MNOPE, MROPE, MDV = 6, 384, 256, 128, 64, 128
MW = MH * MDV
MQW = MH * (MNOPE + MROPE)
MKVW = MH * (MNOPE + MDV)
CONV_CH = 512
ROPE_THETA = 10000.0
EPS = 1e-6
IN_DIM = 5856
OZ, OCB, OCC, OCX, OMKV, OGV, OGQ, OGK, OMQ, OT = 0, 2048, 2560, 3072, 3584, 3840, 4608, 4992, 5376, 5760
PW = 5888
LANES = 128
VMEM_LIMIT = 56 * 1024 * 1024

ADAM_LR, ADAM_B1, ADAM_B2, ADAM_EPS, ADAM_WD, ADAM_STEP = 0.001, 0.9, 0.999, 1e-08, 0.01, 10


def _cp(sem=None):
    return pltpu.CompilerParams(dimension_semantics=sem, vmem_limit_bytes=VMEM_LIMIT)


def _dot(a, b):
    return jnp.dot(a.astype(MXU), b.astype(MXU), preferred_element_type=F32)


def _dot_nt(a, b):
    return lax.dot_general(a.astype(MXU), b.astype(MXU), (((1,), (1,)), ((), ())), preferred_element_type=F32)


def _dot_tn(a, b):
    return lax.dot_general(a.astype(MXU), b.astype(MXU), (((0,), (0,)), ((), ())), preferred_element_type=F32)


def _dotf(a, b):
    return jnp.dot(a, b, precision=HI, preferred_element_type=F32)


def _dotf_nt(a, b):
    return lax.dot_general(a, b, (((1,), (1,)), ((), ())), precision=HI, preferred_element_type=F32)


def _dotf_tn(a, b):
    return lax.dot_general(a, b, (((0,), (0,)), ((), ())), precision=HI, preferred_element_type=F32)


def _rows(s):
    return min(256, s)


def _rms(x, g):
    r = lax.rsqrt(jnp.mean(x * x, axis=-1, keepdims=True) + EPS)
    return x * r * g


def _rms_bwd(dy, x, g):
    r = lax.rsqrt(jnp.mean(x * x, axis=-1, keepdims=True) + EPS)
    xh = x * r
    dxh = dy * g
    dg = jnp.sum(dy * xh, axis=0, keepdims=True)
    dx = r * (dxh - xh * jnp.mean(dxh * xh, axis=-1, keepdims=True))
    return dx, dg


def _sigmoid(z):
    return 1.0 / (1.0 + jnp.exp(-z))


def _matmul(a, b, *, dims, tm, tn, tk, out_dtypes, name, epilogue=None, extras=(), extra_kinds=(), after=()):
    if dims == "nn":
        (m, k), n, mul = a.shape, b.shape[1], _dot
    elif dims == "nt":
        (m, k), n, mul = a.shape, b.shape[0], _dot_nt
    else:
        (k, m), n, mul = a.shape, b.shape[1], _dot_tn
    tm, tn, tk = min(tm, m), min(tn, n), min(tk, k)
    assert m % tm == 0 and n % tn == 0 and k % tk == 0, (m, n, k, tm, tn, tk)
    if dims == "nn":
        a_spec = pl.BlockSpec((tm, tk), lambda i, j, kk: (i, kk))
        b_spec = pl.BlockSpec((tk, tn), lambda i, j, kk: (kk, j))
    elif dims == "nt":
        a_spec = pl.BlockSpec((tm, tk), lambda i, j, kk: (i, kk))
        b_spec = pl.BlockSpec((tn, tk), lambda i, j, kk: (j, kk))
    else:
        a_spec = pl.BlockSpec((tk, tm), lambda i, j, kk: (kk, i))
        b_spec = pl.BlockSpec((tk, tn), lambda i, j, kk: (kk, j))
    nk = k // tk
    n_extra = len(extras)
    n_out = len(out_dtypes)
    n_after = len(after)
    extra_specs = []
    for kind in extra_kinds:
        if kind == "mn":
            extra_specs.append(pl.BlockSpec((tm, tn), lambda i, j, kk: (i, j)))
        else:
            extra_specs.append(pl.BlockSpec((1, tn), lambda i, j, kk: (0, j)))

    def body(*refs):
        a_ref, b_ref = refs[0], refs[1]
        ex = refs[2:2 + n_extra]
        outs = refs[2 + n_extra + n_after:2 + n_extra + n_after + n_out]
        acc = refs[-1]
        kk = pl.program_id(2)

        @pl.when(kk == 0)
        def _():
            acc[...] = jnp.zeros_like(acc)

        acc[...] += mul(a_ref[...], b_ref[...])

        @pl.when(kk == nk - 1)
        def _():
            res = acc[...]
            vals = (res,) if epilogue is None else epilogue(res, *[e[...] for e in ex])
            for o, v in zip(outs, vals):
                o[...] = v.astype(o.dtype)

    out_spec = pl.BlockSpec((tm, tn), lambda i, j, kk: (i, j))
    res = pl.pallas_call(
        body, grid=(m // tm, n // tn, nk),
        in_specs=[a_spec, b_spec] + extra_specs + [pl.BlockSpec(memory_space=pl.ANY)] * n_after,
        out_specs=[out_spec] * n_out,
        out_shape=[jax.ShapeDtypeStruct((m, n), dt) for dt in out_dtypes],
        scratch_shapes=[pltpu.VMEM((tm, tn), F32)],
        name=name, compiler_params=_cp(("parallel", "parallel", "arbitrary")),
    )(a, b, *extras, *after)
    return res


def _norm_mod(x, g, scale, shift, name):
    s, d = x.shape
    tr = _rows(s)

    def body(x_ref, g_ref, sc_ref, sh_ref, h_ref):
        h = _rms(x_ref[...], g_ref[...]) * (1.0 + sc_ref[...]) + sh_ref[...]
        h_ref[...] = h.astype(h_ref.dtype)

    row = pl.BlockSpec((tr, d), lambda i: (i, 0))
    vec = pl.BlockSpec((1, d), lambda i: (0, 0))
    return pl.pallas_call(body, grid=(s // tr,), in_specs=[row, vec, vec, vec], out_specs=row,
                          out_shape=jax.ShapeDtypeStruct((s, d), MXU), name=name,
                          compiler_params=_cp(("parallel",)))(x, g, scale, shift)


def _norm_mod_bwd(d_h, x, d_out, g, scale, name):
    s, d = x.shape
    tr = _rows(s)

    def body(dh_ref, x_ref, do_ref, g_ref, sc_ref, dx_ref, dsh_ref, dsc_ref, dg_ref):
        i = pl.program_id(0)

        @pl.when(i == 0)
        def _():
            dsh_ref[...] = jnp.zeros_like(dsh_ref)
            dsc_ref[...] = jnp.zeros_like(dsc_ref)
            dg_ref[...] = jnp.zeros_like(dg_ref)

        dh = dh_ref[...]
        xv = x_ref[...]
        gv = g_ref[...]
        r = lax.rsqrt(jnp.mean(xv * xv, axis=-1, keepdims=True) + EPS)
        xh = xv * r
        dsh_ref[...] += jnp.sum(dh, axis=0, keepdims=True)
        dsc_ref[...] += jnp.sum(dh * (xh * gv), axis=0, keepdims=True)
        dhn = dh * (1.0 + sc_ref[...])
        dg_ref[...] += jnp.sum(dhn * xh, axis=0, keepdims=True)
        dxh = dhn * gv
        dx_ref[...] = do_ref[...] + r * (dxh - xh * jnp.mean(dxh * xh, axis=-1, keepdims=True))

    row = pl.BlockSpec((tr, d), lambda i: (i, 0))
    vec = pl.BlockSpec((1, d), lambda i: (0, 0))
    vshape = jax.ShapeDtypeStruct((1, d), F32)
    return pl.pallas_call(body, grid=(s // tr,), in_specs=[row, row, row, vec, vec],
                          out_specs=[row, vec, vec, vec],
                          out_shape=[jax.ShapeDtypeStruct((s, d), F32), vshape, vshape, vshape],
                          name=name, compiler_params=_cp(("arbitrary",)))(d_h, x, d_out, g, scale)


def _gate_bwd(d_out, u, gate, name):
    s, d = d_out.shape
    tr = _rows(s)

    def body(do_ref, u_ref, gt_ref, du_ref, dgt_ref):
        @pl.when(pl.program_id(0) == 0)
        def _():
            dgt_ref[...] = jnp.zeros_like(dgt_ref)

        do = do_ref[...]
        du_ref[...] = (do * gt_ref[...]).astype(du_ref.dtype)
        dgt_ref[...] += jnp.sum(do * u_ref[...], axis=0, keepdims=True)

    row = pl.BlockSpec((tr, d), lambda i: (i, 0))
    vec = pl.BlockSpec((1, d), lambda i: (0, 0))
    return pl.pallas_call(body, grid=(s // tr,), in_specs=[row, row, vec], out_specs=[row, vec],
                          out_shape=[jax.ShapeDtypeStruct((s, d), MXU), jax.ShapeDtypeStruct((1, d), F32)],
                          name=name, compiler_params=_cp(("arbitrary",)))(d_out, u, gate)


def _final_loss(x, g, target, name):
    s, d = x.shape
    tr = _rows(s)

    def body(x_ref, g_ref, t_ref, loss_ref, dx_ref, dg_ref):
        @pl.when(pl.program_id(0) == 0)
        def _():
            loss_ref[...] = jnp.zeros_like(loss_ref)
            dg_ref[...] = jnp.zeros_like(dg_ref)

        xv = x_ref[...]
        gv = g_ref[...]
        diff = _rms(xv, gv) - t_ref[...]
        part = 0.5 * jnp.sum(jnp.sum(diff * diff, axis=-1, keepdims=True) / d, axis=0, keepdims=True)
        loss_ref[...] += jnp.broadcast_to(part, loss_ref.shape)
        dx, dg = _rms_bwd(diff / d, xv, gv)
        dx_ref[...] = dx
        dg_ref[...] += dg

    row = pl.BlockSpec((tr, d), lambda i: (i, 0))
    vec = pl.BlockSpec((1, d), lambda i: (0, 0))
    lvec = pl.BlockSpec((1, LANES), lambda i: (0, 0))
    return pl.pallas_call(body, grid=(s // tr,), in_specs=[row, vec, row], out_specs=[lvec, row, vec],
                          out_shape=[jax.ShapeDtypeStruct((1, LANES), F32), jax.ShapeDtypeStruct((s, d), F32),
                                     jax.ShapeDtypeStruct((1, d), F32)],
                          name=name, compiler_params=_cp(("arbitrary",)))(x, g, target)


def _shift_rows(u, s, down):
    ri = lax.broadcasted_iota(jnp.int32, u.shape, 0)
    if down:
        return jnp.where(ri == 0, 0.0, pltpu.roll(u, 1, 0))
    return jnp.where(ri == s - 1, 0.0, pltpu.roll(u, s - 1, 0))


def _conv_fwd(proj, conv_w, name):
    s = proj.shape[0]
    nt = CONV_CH // LANES

    def body(cb_ref, cc_ref, cx_ref, w_ref, pre_ref):
        u = cc_ref[...] * cx_ref[...]
        conv = _shift_rows(u, s, True) * w_ref[0:1, :] + u * w_ref[1:2, :] + _shift_rows(u, s, False) * w_ref[2:3, :]
        pre_ref[...] = cb_ref[...] * conv

    def col(off):
        return pl.BlockSpec((s, LANES), lambda j: (0, off // LANES + j))

    return pl.pallas_call(body, grid=(nt,), in_specs=[col(OCB), col(OCC), col(OCX), pl.BlockSpec((3, LANES), lambda j: (0, j))],
                          out_specs=pl.BlockSpec((s, LANES), lambda j: (0, j)),
                          out_shape=jax.ShapeDtypeStruct((s, CONV_CH), F32), name=name,
                          compiler_params=_cp(("parallel",)))(proj, proj, proj, conv_w)


def _conv_bwd(proj, conv_w, d_pre, name):
    s = proj.shape[0]
    nt = CONV_CH // LANES

    def body(cb_ref, cc_ref, cx_ref, w_ref, dp_ref, dcb_ref, dcc_ref, dcx_ref, dw_ref):
        cc, cx = cc_ref[...], cx_ref[...]
        u = cc * cx
        up, dn = _shift_rows(u, s, True), _shift_rows(u, s, False)
        w0, w1, w2 = w_ref[0:1, :], w_ref[1:2, :], w_ref[2:3, :]
        conv = up * w0 + u * w1 + dn * w2
        dp = dp_ref[...]
        dcb_ref[...] = dp * conv
        dconv = dp * cb_ref[...]
        du = _shift_rows(dconv, s, False) * w0 + dconv * w1 + _shift_rows(dconv, s, True) * w2
        dcc_ref[...] = du * cx
        dcx_ref[...] = du * cc
        dw_ref[0:1, :] = jnp.sum(dconv * up, axis=0, keepdims=True)
        dw_ref[1:2, :] = jnp.sum(dconv * u, axis=0, keepdims=True)
        dw_ref[2:3, :] = jnp.sum(dconv * dn, axis=0, keepdims=True)

    def col(off):
        return pl.BlockSpec((s, LANES), lambda j: (0, off // LANES + j))

    blk = pl.BlockSpec((s, LANES), lambda j: (0, j))
    wblk = pl.BlockSpec((3, LANES), lambda j: (0, j))
    full = jax.ShapeDtypeStruct((s, CONV_CH), F32)
    return pl.pallas_call(body, grid=(nt,), in_specs=[col(OCB), col(OCC), col(OCX), wblk, blk],
                          out_specs=[blk, blk, blk, wblk],
                          out_shape=[full, full, full, jax.ShapeDtypeStruct((3, CONV_CH), F32)],
                          name=name, compiler_params=_cp(("parallel",)))(proj, proj, proj, conv_w, d_pre)


GLA_SUB = 4


def _gla_gates(t_ref, wg_ref, bg_ref):
    t = t_ref[...]
    a = _dotf(t, wg_ref[...]) + bg_ref[...]
    la = (jnp.minimum(a, 0.0) - jnp.log(1.0 + jnp.exp(-jnp.abs(a)))) / GTEMP
    return t, a, la


def _gla_masks(reverse):
    ri = lax.broadcasted_iota(jnp.int32, (CHUNK, CHUNK), 0)
    ci = lax.broadcasted_iota(jnp.int32, (CHUNK, CHUNK), 1)
    if reverse:
        cum, mask = ci >= ri, ci > ri
    else:
        cum, mask = ci <= ri, ci <= ri
    return cum.astype(F32), mask


def _gla_specs(s, reverse):
    nsub = min(GLA_SUB, s // CHUNK)
    nsteps = s // (CHUNK * nsub)

    def row(n):
        return nsteps - 1 - n if reverse else n

    def chunk(pi):
        return nsub - 1 - pi if reverse else pi

    return nsub, nsteps, row, chunk


def _gla_fwd(proj, wg_pad, bg, reverse, name):
    s = proj.shape[0]
    nsub, nsteps, row, chunk = _gla_specs(s, reverse)
    rb = nsub * CHUNK

    def body(q_ref, k_ref, v_ref, t_ref, wg_ref, bg_ref, o_ref, st_ref, state):
        @pl.when(pl.program_id(0) == 0)
        def _():
            state[...] = jnp.zeros_like(state)

        _, _, la = _gla_gates(t_ref, wg_ref, bg_ref)
        cumf, mask = _gla_masks(reverse)
        lane = lax.broadcasted_iota(jnp.int32, (CHUNK, LANES), 1)
        for pi in range(nsub):
            rows = slice(chunk(pi) * CHUNK, (chunk(pi) + 1) * CHUNK)
            la_c = la[rows]
            b = _dotf(cumf, la_c)
            bl = jnp.sum(la_c, axis=0, keepdims=True)
            q = q_ref[rows, :] * (GDK ** -0.5)
            k = k_ref[rows, :]
            qd = q * jnp.exp(b)
            ki = k * jnp.exp(-b)
            kte = k * jnp.exp(bl - b)
            decay = jnp.exp(bl)
            for h in range(GH):
                p = h // 2
                sl = slice(p * LANES, (p + 1) * LANES)
                lm = (lane < GDK) if h % 2 == 0 else (lane >= GDK)
                qd_h = jnp.where(lm, qd[:, sl], 0.0)
                kte_h = jnp.where(lm, kte[:, sl], 0.0)
                v_h = v_ref[rows, h * GDV:(h + 1) * GDV]
                st = state[h]
                a_mat = jnp.where(mask, _dot_nt(qd_h, ki[:, sl]), 0.0)
                o_ref[rows, h * GDV:(h + 1) * GDV] = _dot(a_mat, v_h) + _dot_nt(qd_h, st)
                st_ref[pi, h] = st
                state[h] = st * decay[:, sl] + _dot_tn(v_h, kte_h)

    return pl.pallas_call(
        body, grid=(nsteps,),
        in_specs=[pl.BlockSpec((rb, GQK), lambda n: (row(n), OGQ // GQK)),
                  pl.BlockSpec((rb, GQK), lambda n: (row(n), OGK // GQK)),
                  pl.BlockSpec((rb, GW), lambda n: (row(n), OGV // GW)),
                  pl.BlockSpec((rb, LANES), lambda n: (row(n), OT // LANES)),
                  pl.BlockSpec((LANES, GQK), lambda n: (0, 0)),
                  pl.BlockSpec((1, GQK), lambda n: (0, 0))],
        out_specs=[pl.BlockSpec((rb, GW), lambda n: (row(n), 0)),
                   pl.BlockSpec((nsub, GH, GDV, LANES), lambda n: (n, 0, 0, 0))],
        out_shape=[jax.ShapeDtypeStruct((s, GW), F32), jax.ShapeDtypeStruct((s // CHUNK, GH, GDV, LANES), F32)],
        scratch_shapes=[pltpu.VMEM((GH, GDV, LANES), F32)],
        name=name, compiler_params=_cp(("arbitrary",)))(proj, proj, proj, proj, wg_pad, bg)


def _gla_bwd(proj, wg_pad, bg, states, d_o, reverse, name):
    s = proj.shape[0]
    nsub, nsteps, row, chunk = _gla_specs(s, reverse)
    rb = nsub * CHUNK

    def body(q_ref, k_ref, v_ref, t_ref, wg_ref, bg_ref, st_ref, do_ref,
             dq_ref, dk_ref, dv_ref, dt_ref, dwg_ref, dbg_ref, dstate, da_buf):
        @pl.when(pl.program_id(0) == 0)
        def _():
            dstate[...] = jnp.zeros_like(dstate)
            dwg_ref[...] = jnp.zeros_like(dwg_ref)
            dbg_ref[...] = jnp.zeros_like(dbg_ref)

        t, a, la = _gla_gates(t_ref, wg_ref, bg_ref)
        cumf, mask = _gla_masks(reverse)
        lane = lax.broadcasted_iota(jnp.int32, (CHUNK, LANES), 1)
        for pi in reversed(range(nsub)):
            rows = slice(chunk(pi) * CHUNK, (chunk(pi) + 1) * CHUNK)
            la_c = la[rows]
            b = _dotf(cumf, la_c)
            bl = jnp.sum(la_c, axis=0, keepdims=True)
            q = q_ref[rows, :] * (GDK ** -0.5)
            k = k_ref[rows, :]
            e, ei, ee = jnp.exp(b), jnp.exp(-b), jnp.exp(bl - b)
            qd, ki, kte = q * e, k * ei, k * ee
            decay = jnp.exp(bl)
            for p in range(GH // 2):
                sl = slice(p * LANES, (p + 1) * LANES)
                dqd = jnp.zeros((CHUNK, LANES), F32)
                dki = jnp.zeros((CHUNK, LANES), F32)
                dkte = jnp.zeros((CHUNK, LANES), F32)
                ddecay = jnp.zeros((1, LANES), F32)
                for half in range(2):
                    h = 2 * p + half
                    lm = (lane < GDK) if half == 0 else (lane >= GDK)
                    qd_h = jnp.where(lm, qd[:, sl], 0.0)
                    ki_h = jnp.where(lm, ki[:, sl], 0.0)
                    kte_h = jnp.where(lm, kte[:, sl], 0.0)
                    v_h = v_ref[rows, h * GDV:(h + 1) * GDV]
                    do_h = do_ref[rows, h * GDV:(h + 1) * GDV]
                    st = st_ref[pi, h]
                    dst = dstate[h]
                    a_mat = jnp.where(mask, _dot_nt(qd_h, ki_h), 0.0)
                    da_mat = jnp.where(mask, _dot_nt(do_h, v_h), 0.0)
                    dv_ref[rows, h * GDV:(h + 1) * GDV] = _dot_tn(a_mat, do_h) + _dot_nt(kte_h, dst)
                    dqd += _dot(da_mat, ki_h) + _dot(do_h, st)
                    dki += _dot_tn(da_mat, qd_h)
                    dkte += _dot(v_h, dst)
                    ddecay += jnp.sum(dst * st, axis=0, keepdims=True)
                    dstate[h] = dst * decay[:, sl] + _dot_tn(do_h, qd_h)
                dq_ref[rows, sl] = dqd * e[:, sl] * (GDK ** -0.5)
                dk_ref[rows, sl] = dki * ei[:, sl] + dkte * ee[:, sl]
                db = dqd * qd[:, sl] - dki * ki[:, sl] - dkte * kte[:, sl]
                dbl = jnp.sum(dkte * kte[:, sl], axis=0, keepdims=True) + decay[:, sl] * ddecay
                da_buf[rows, sl] = _dotf_tn(cumf, db) + dbl
        da = da_buf[...] * (1.0 / GTEMP) * _sigmoid(-a)
        dt_ref[...] = _dotf_nt(da, wg_ref[...])
        dwg_ref[...] += _dotf_tn(t, da)
        dbg_ref[...] += jnp.sum(da, axis=0, keepdims=True)

    def prow(j):
        return row(nsteps - 1 - j)

    return pl.pallas_call(
        body, grid=(nsteps,),
        in_specs=[pl.BlockSpec((rb, GQK), lambda j: (prow(j), OGQ // GQK)),
                  pl.BlockSpec((rb, GQK), lambda j: (prow(j), OGK // GQK)),
                  pl.BlockSpec((rb, GW), lambda j: (prow(j), OGV // GW)),
                  pl.BlockSpec((rb, LANES), lambda j: (prow(j), OT // LANES)),
                  pl.BlockSpec((LANES, GQK), lambda j: (0, 0)),
                  pl.BlockSpec((1, GQK), lambda j: (0, 0)),
                  pl.BlockSpec((nsub, GH, GDV, LANES), lambda j: (nsteps - 1 - j, 0, 0, 0)),
                  pl.BlockSpec((rb, GW), lambda j: (prow(j), 0))],
        out_specs=[pl.BlockSpec((rb, GQK), lambda j: (prow(j), 0)),
                   pl.BlockSpec((rb, GQK), lambda j: (prow(j), 0)),
                   pl.BlockSpec((rb, GW), lambda j: (prow(j), 0)),
                   pl.BlockSpec((rb, LANES), lambda j: (prow(j), 0)),
                   pl.BlockSpec((LANES, GQK), lambda j: (0, 0)),
                   pl.BlockSpec((1, GQK), lambda j: (0, 0))],
        out_shape=[jax.ShapeDtypeStruct((s, GQK), F32), jax.ShapeDtypeStruct((s, GQK), F32),
                   jax.ShapeDtypeStruct((s, GW), F32), jax.ShapeDtypeStruct((s, LANES), F32),
                   jax.ShapeDtypeStruct((LANES, GQK), F32), jax.ShapeDtypeStruct((1, GQK), F32)],
        scratch_shapes=[pltpu.VMEM((GH, GDV, LANES), F32), pltpu.VMEM((rb, GQK), F32)],
        name=name, compiler_params=_cp(("arbitrary",)))(proj, proj, proj, proj, wg_pad, bg, states, d_o)


def _rot_half(x):
    lane = lax.broadcasted_iota(jnp.int32, x.shape, 1)
    first = (lane % MROPE) < (MROPE // 2)
    return jnp.where(first, -pltpu.roll(x, LANES - MROPE // 2, 1), pltpu.roll(x, MROPE // 2, 1))


def _mla_prep(proj, cos, sin, qg, kvg, w_uq, w_ukv, name):
    s = proj.shape[0]
    tr = _rows(s)

    def body(mq_ref, mkv_ref, t_ref, cos_ref, sin_ref, qg_ref, kvg_ref, wuq_ref, wukv_ref, q_ref, k_ref, v_ref):
        cosv, sinv = cos_ref[...], sin_ref[...]
        lane = lax.broadcasted_iota(jnp.int32, (tr, LANES), 1)

        def rope(xv):
            return xv * cosv + _rot_half(xv) * sinv

        qm = _dot(_rms(mq_ref[...], qg_ref[...]), wuq_ref[...])
        kv = _dot(_rms(mkv_ref[...], kvg_ref[...]), wukv_ref[...])
        kr_lo = jnp.where(lane < MROPE, rope(t_ref[...]), 0.0)
        kr_hi = pltpu.roll(kr_lo, MROPE, 1)
        for p in range(MH // 2):
            r = rope(qm[:, MW + p * LANES:MW + (p + 1) * LANES]).astype(q_ref.dtype)
            q_ref[2 * p, :, LANES:] = r
            q_ref[2 * p + 1, :, LANES:] = r
        for h in range(MH):
            q_ref[h, :, :LANES] = qm[:, h * LANES:(h + 1) * LANES].astype(q_ref.dtype)
            k_ref[h, :, :LANES] = kv[:, 2 * h * LANES:(2 * h + 1) * LANES].astype(k_ref.dtype)
            k_ref[h, :, LANES:] = (kr_lo if h % 2 == 0 else kr_hi).astype(k_ref.dtype)
            v_ref[h] = kv[:, (2 * h + 1) * LANES:(2 * h + 2) * LANES].astype(v_ref.dtype)

    def full(shape):
        return pl.BlockSpec(shape, lambda i: (0,) * len(shape))

    return pl.pallas_call(
        body, grid=(s // tr,),
        in_specs=[pl.BlockSpec((tr, MQL), lambda i: (i, OMQ // MQL)),
                  pl.BlockSpec((tr, MKVL), lambda i: (i, OMKV // MKVL)),
                  pl.BlockSpec((tr, LANES), lambda i: (i, OT // LANES)),
                  pl.BlockSpec((tr, LANES), lambda i: (i, 0)),
                  pl.BlockSpec((tr, LANES), lambda i: (i, 0)),
                  full((1, MQL)), full((1, MKVL)), full((MQL, MQW)), full((MKVL, MKVW))],
        out_specs=[pl.BlockSpec((MH, tr, 2 * LANES), lambda i: (0, i, 0)),
                   pl.BlockSpec((MH, tr, 2 * LANES), lambda i: (0, i, 0)),
                   pl.BlockSpec((MH, tr, LANES), lambda i: (0, i, 0))],
        out_shape=[jax.ShapeDtypeStruct((MH, s, 2 * LANES), MXU), jax.ShapeDtypeStruct((MH, s, 2 * LANES), MXU),
                   jax.ShapeDtypeStruct((MH, s, LANES), MXU)],
        name=name, compiler_params=_cp(("parallel",)))(proj, proj, proj, cos, sin, qg, kvg, w_uq, w_ukv)


def _mla_prep_bwd(proj, cos, sin, qg, kvg, w_uq, w_ukv, d_q, d_k, d_v, name):
    s = proj.shape[0]
    tr = _rows(s)

    def body(mq_ref, mkv_ref, cos_ref, sin_ref, qg_ref, kvg_ref, wuq_ref, wukv_ref, dq_ref, dk_ref, dv_ref,
             dmq_ref, dmkv_ref, dt_ref, dwuq_ref, dwukv_ref, dqg_ref, dkvg_ref):
        @pl.when(pl.program_id(0) == 0)
        def _():
            for r in (dwuq_ref, dwukv_ref, dqg_ref, dkvg_ref):
                r[...] = jnp.zeros_like(r)

        cosv, sinv = cos_ref[...], sin_ref[...]
        lane = lax.broadcasted_iota(jnp.int32, (tr, LANES), 1)
        lo = lane < MROPE

        def unrope(dv):
            return dv * cosv - _rot_half(dv * sinv)

        parts = [dq_ref[h, :, :LANES] for h in range(MH)]
        for p in range(MH // 2):
            parts.append(unrope(jnp.where(lo, dq_ref[2 * p, :, LANES:], dq_ref[2 * p + 1, :, LANES:])))
        d_qm = jnp.concatenate(parts, axis=1)
        mq, qgv = mq_ref[...], qg_ref[...]
        cq = _rms(mq, qgv)
        dwuq_ref[...] += _dot_tn(cq, d_qm)
        dmq, dqg = _rms_bwd(_dot_nt(d_qm, wuq_ref[...]), mq, qgv)
        dmq_ref[...] = dmq
        dqg_ref[...] += dqg

        parts = []
        for h in range(MH):
            parts += [dk_ref[h, :, :LANES], dv_ref[h]]
        d_kv = jnp.concatenate(parts, axis=1)
        mkv, kvgv = mkv_ref[...], kvg_ref[...]
        ckv = _rms(mkv, kvgv)
        dwukv_ref[...] += _dot_tn(ckv, d_kv)
        dmkv, dkvg = _rms_bwd(_dot_nt(d_kv, wukv_ref[...]), mkv, kvgv)
        dmkv_ref[...] = dmkv
        dkvg_ref[...] += dkvg

        even = dk_ref[0, :, LANES:] + dk_ref[2, :, LANES:] + dk_ref[4, :, LANES:]
        odd = dk_ref[1, :, LANES:] + dk_ref[3, :, LANES:] + dk_ref[5, :, LANES:]
        d_kr = jnp.where(lo, even, 0.0) + pltpu.roll(jnp.where(lo, 0.0, odd), MROPE, 1)
        dt_ref[...] = jnp.where(lo, unrope(d_kr), 0.0)

    def full(shape):
        return pl.BlockSpec(shape, lambda i: (0,) * len(shape))

    return pl.pallas_call(
        body, grid=(s // tr,),
        in_specs=[pl.BlockSpec((tr, MQL), lambda i: (i, OMQ // MQL)),
                  pl.BlockSpec((tr, MKVL), lambda i: (i, OMKV // MKVL)),
                  pl.BlockSpec((tr, LANES), lambda i: (i, 0)),
                  pl.BlockSpec((tr, LANES), lambda i: (i, 0)),
                  full((1, MQL)), full((1, MKVL)), full((MQL, MQW)), full((MKVL, MKVW)),
                  pl.BlockSpec((MH, tr, 2 * LANES), lambda i: (0, i, 0)),
                  pl.BlockSpec((MH, tr, 2 * LANES), lambda i: (0, i, 0)),
                  pl.BlockSpec((MH, tr, LANES), lambda i: (0, i, 0))],
        out_specs=[pl.BlockSpec((tr, MQL), lambda i: (i, 0)), pl.BlockSpec((tr, MKVL), lambda i: (i, 0)),
                   pl.BlockSpec((tr, LANES), lambda i: (i, 0)),
                   full((MQL, MQW)), full((MKVL, MKVW)), full((1, MQL)), full((1, MKVL))],
        out_shape=[jax.ShapeDtypeStruct((s, MQL), F32), jax.ShapeDtypeStruct((s, MKVL), F32),
                   jax.ShapeDtypeStruct((s, LANES), F32),
                   jax.ShapeDtypeStruct((MQL, MQW), F32), jax.ShapeDtypeStruct((MKVL, MKVW), F32),
                   jax.ShapeDtypeStruct((1, MQL), F32), jax.ShapeDtypeStruct((1, MKVL), F32)],
        name=name, compiler_params=_cp(("arbitrary",)))(proj, proj, cos, sin, qg, kvg, w_uq, w_ukv, d_q, d_k, d_v)


ATT_SCALE = (MNOPE + MROPE) ** -0.5


def _attn_fwd(q, k, v, name):
    s = q.shape[1]
    tq = _rows(s)

    def body(q_ref, k_ref, v_ref, o_ref, lse_ref):
        sc = _dot_nt(q_ref[0], k_ref[0]) * ATT_SCALE
        m = jnp.max(sc, axis=-1, keepdims=True)
        p = jnp.exp(sc - m)
        l = jnp.sum(p, axis=-1, keepdims=True)
        o_ref[...] = _dot(p, v_ref[0]) / l
        lse_ref[0] = m + jnp.log(l)

    return pl.pallas_call(
        body, grid=(MH, s // tq),
        in_specs=[pl.BlockSpec((1, tq, 2 * LANES), lambda h, i: (h, i, 0)),
                  pl.BlockSpec((1, s, 2 * LANES), lambda h, i: (h, 0, 0)),
                  pl.BlockSpec((1, s, LANES), lambda h, i: (h, 0, 0))],
        out_specs=[pl.BlockSpec((tq, LANES), lambda h, i: (i, h)),
                   pl.BlockSpec((1, tq, 1), lambda h, i: (h, i, 0))],
        out_shape=[jax.ShapeDtypeStruct((s, MW), F32), jax.ShapeDtypeStruct((MH, s, 1), F32)],
        name=name, compiler_params=_cp(("parallel", "parallel")))(q, k, v)


def _attn_bwd(q, k, v, o, lse, d_o, name):
    s = q.shape[1]
    tq = _rows(s)

    def body(q_ref, k_ref, v_ref, o_ref, lse_ref, do_ref, dq_ref, dk_ref, dv_ref):
        @pl.when(pl.program_id(1) == 0)
        def _():
            dk_ref[...] = jnp.zeros_like(dk_ref)
            dv_ref[...] = jnp.zeros_like(dv_ref)

        qv, kv, do = q_ref[0], k_ref[0], do_ref[...]
        p = jnp.exp(_dot_nt(qv, kv) * ATT_SCALE - lse_ref[0])
        delta = jnp.sum(do * o_ref[...], axis=-1, keepdims=True)
        ds = p * (_dot_nt(do, v_ref[0]) - delta) * ATT_SCALE
        dq_ref[0] = _dot(ds, kv)
        dk_ref[0] += _dot_tn(ds, qv)
        dv_ref[0] += _dot_tn(p, do)

    return pl.pallas_call(
        body, grid=(MH, s // tq),
        in_specs=[pl.BlockSpec((1, tq, 2 * LANES), lambda h, i: (h, i, 0)),
                  pl.BlockSpec((1, s, 2 * LANES), lambda h, i: (h, 0, 0)),
                  pl.BlockSpec((1, s, LANES), lambda h, i: (h, 0, 0)),
                  pl.BlockSpec((tq, LANES), lambda h, i: (i, h)),
                  pl.BlockSpec((1, tq, 1), lambda h, i: (h, i, 0)),
                  pl.BlockSpec((tq, LANES), lambda h, i: (i, h))],
        out_specs=[pl.BlockSpec((1, tq, 2 * LANES), lambda h, i: (h, i, 0)),
                   pl.BlockSpec((1, s, 2 * LANES), lambda h, i: (h, 0, 0)),
                   pl.BlockSpec((1, s, LANES), lambda h, i: (h, 0, 0))],
        out_shape=[jax.ShapeDtypeStruct((MH, s, 2 * LANES), F32), jax.ShapeDtypeStruct((MH, s, 2 * LANES), F32),
                   jax.ShapeDtypeStruct((MH, s, LANES), F32)],
        name=name, compiler_params=_cp(("parallel", "arbitrary")))(q, k, v, o, lse, d_o)


def _merge_fwd(o_f, o_b, o_att, pre, proj, gng, mog, cog, name):
    s = proj.shape[0]
    tr = _rows(s)

    def body(of_ref, ob_ref, oa_ref, pre_ref, z_ref, gng_ref, mog_ref, cog_ref, y_ref):
        z = z_ref[...]
        sz = z * _sigmoid(z)
        osum = of_ref[...] + ob_ref[...]
        gg = gng_ref[...]
        for h in range(GH):
            sl = slice(h * GDV, (h + 1) * GDV)
            y_ref[:, sl] = (_rms(osum[:, sl], gg) * sz[:, sl]).astype(y_ref.dtype)
        y_ref[:, GW:GW + MW] = (_rms(oa_ref[...], mog_ref[...]) * sz[:, GW:GW + MW]).astype(y_ref.dtype)
        y_ref[:, GW + MW:] = (_rms(pre_ref[...], cog_ref[...]) * sz[:, GW + MW:]).astype(y_ref.dtype)

    def row(w):
        return pl.BlockSpec((tr, w), lambda i: (i, 0))

    def vec(w):
        return pl.BlockSpec((1, w), lambda i: (0, 0))

    return pl.pallas_call(
        body, grid=(s // tr,),
        in_specs=[row(GW), row(GW), row(MW), row(CONV_CH), row(D_MIX), vec(GDV), vec(MW), vec(CONV_CH)],
        out_specs=row(D_MIX), out_shape=jax.ShapeDtypeStruct((s, D_MIX), MXU),
        name=name, compiler_params=_cp(("parallel",)))(o_f, o_b, o_att, pre, proj, gng, mog, cog)


def _merge_bwd(d_y, o_f, o_b, o_att, pre, proj, gng, mog, cog, name):
    s = proj.shape[0]
    tr = _rows(s)

    def body(dy_ref, of_ref, ob_ref, oa_ref, pre_ref, z_ref, gng_ref, mog_ref, cog_ref,
             dz_ref, dos_ref, doa_ref, dpre_ref, dgng_ref, dmog_ref, dcog_ref):
        @pl.when(pl.program_id(0) == 0)
        def _():
            for r in (dgng_ref, dmog_ref, dcog_ref):
                r[...] = jnp.zeros_like(r)

        z, dy = z_ref[...], dy_ref[...]
        sg = _sigmoid(z)
        sz = z * sg
        dsz = sg * (1.0 + z * (1.0 - sg))
        dcat = dy * sz
        dyz = dy * dsz
        osum = of_ref[...] + ob_ref[...]
        gg = gng_ref[...]
        dgg = jnp.zeros_like(gg)
        for h in range(GH):
            sl = slice(h * GDV, (h + 1) * GDV)
            dz_ref[:, sl] = dyz[:, sl] * _rms(osum[:, sl], gg)
            dx, dg = _rms_bwd(dcat[:, sl], osum[:, sl], gg)
            dos_ref[:, sl] = dx
            dgg += dg
        dgng_ref[...] += dgg
        sl = slice(GW, GW + MW)
        oa, mg = oa_ref[...], mog_ref[...]
        dz_ref[:, sl] = dyz[:, sl] * _rms(oa, mg)
        dx, dg = _rms_bwd(dcat[:, sl], oa, mg)
        doa_ref[...] = dx
        dmog_ref[...] += dg
        sl = slice(GW + MW, D_MIX)
        pv, cg = pre_ref[...], cog_ref[...]
        dz_ref[:, sl] = dyz[:, sl] * _rms(pv, cg)
        dx, dg = _rms_bwd(dcat[:, sl], pv, cg)
        dpre_ref[...] = dx
        dcog_ref[...] += dg

    def row(w):
        return pl.BlockSpec((tr, w), lambda i: (i, 0))

    def vec(w):
        return pl.BlockSpec((1, w), lambda i: (0, 0))

    def rs(w):
        return jax.ShapeDtypeStruct((s, w), F32)

    def vs(w):
        return jax.ShapeDtypeStruct((1, w), F32)

    return pl.pallas_call(
        body, grid=(s // tr,),
        in_specs=[row(D_MIX), row(GW), row(GW), row(MW), row(CONV_CH), row(D_MIX), vec(GDV), vec(MW), vec(CONV_CH)],
        out_specs=[row(D_MIX), row(GW), row(MW), row(CONV_CH), vec(GDV), vec(MW), vec(CONV_CH)],
        out_shape=[rs(D_MIX), rs(GW), rs(MW), rs(CONV_CH), vs(GDV), vs(MW), vs(CONV_CH)],
        name=name, compiler_params=_cp(("arbitrary",)))(d_y, o_f, o_b, o_att, pre, proj, gng, mog, cog)


def _assemble_dproj(d_z, d_cb, d_cc, d_cx, d_mkv, dv_f, dv_b, dq_f, dq_b, dk_f, dk_b, d_mq, dt_m, dt_f, dt_b, name):
    s = d_z.shape[0]
    tr = _rows(s)

    def body(dz, dcb, dcc, dcx, dmkv, dvf, dvb, dqf, dqb, dkf, dkb, dmq, dtm, dtf, dtb, out):
        dt = out.dtype
        out[:, OZ:OZ + D_MIX] = dz[...].astype(dt)
        out[:, OCB:OCB + CONV_CH] = dcb[...].astype(dt)
        out[:, OCC:OCC + CONV_CH] = dcc[...].astype(dt)
        out[:, OCX:OCX + CONV_CH] = dcx[...].astype(dt)
        out[:, OMKV:OMKV + MKVL] = dmkv[...].astype(dt)
        out[:, OGV:OGV + GW] = (dvf[...] + dvb[...]).astype(dt)
        out[:, OGQ:OGQ + GQK] = (dqf[...] + dqb[...]).astype(dt)
        out[:, OGK:OGK + GQK] = (dkf[...] + dkb[...]).astype(dt)
        out[:, OMQ:OMQ + MQL] = dmq[...].astype(dt)
        out[:, OT:OT + LANES] = (dtm[...] + dtf[...] + dtb[...]).astype(dt)

    args = (d_z, d_cb, d_cc, d_cx, d_mkv, dv_f, dv_b, dq_f, dq_b, dk_f, dk_b, d_mq, dt_m, dt_f, dt_b)
    return pl.pallas_call(
        body, grid=(s // tr,),
        in_specs=[pl.BlockSpec((tr, a.shape[1]), lambda i: (i, 0)) for a in args],
        out_specs=pl.BlockSpec((tr, PW), lambda i: (i, 0)),
        out_shape=jax.ShapeDtypeStruct((s, PW), MXU), name=name, compiler_params=_cp(("parallel",)))(*args)


def _layer_fwd(x, mod, wt, cos, sin, tag, late=None):
    shift, scale, gate = mod
    h = _norm_mod(x, wt["norm_g"], scale, shift, f"norm_mod_{tag}")
    (proj,) = _matmul(h, wt["w_in"], dims="nn", tm=1024, tn=256, tk=2048, out_dtypes=(F32,), name=f"in_proj_{tag}")
    if late is not None:
        wt.update(late(proj))
    o_f, st_f = _gla_fwd(proj, wt["wg_pad_f"], wt["bg_f"], False, f"gla_fwd_f_{tag}")
    o_b, st_b = _gla_fwd(proj, wt["wg_pad_b"], wt["bg_b"], True, f"gla_fwd_b_{tag}")
    q, k, v = _mla_prep(proj, cos, sin, wt["q_norm_g"], wt["kv_norm_g"], wt["w_uq"], wt["w_ukv"], f"mla_prep_{tag}")
    o_att, lse = _attn_fwd(q, k, v, f"attn_fwd_{tag}")
    pre = _conv_fwd(proj, wt["conv_w"], f"conv_fwd_{tag}")
    y = _merge_fwd(o_f, o_b, o_att, pre, proj, wt["gla_norm_g"], wt["mla_out_g"], wt["conv_out_g"], f"merge_fwd_{tag}")
    x_new, u = _matmul(y, wt["w_out"], dims="nn", tm=1024, tn=512, tk=2048, out_dtypes=(F32, F32),
                       name=f"out_proj_{tag}", epilogue=lambda acc, xv, gv: (xv + gv * acc, acc),
                       extras=(x, gate), extra_kinds=("mn", "n"))
    saved = dict(x=x, h=h, proj=proj, o_f=o_f, o_b=o_b, st_f=st_f, st_b=st_b, q=q, k=k, v=v,
                 o_att=o_att, lse=lse, pre=pre, y=y, u=u)
    return x_new, saved


def _layer_bwd(d_out, sv, mod, wt, cos, sin, tag, ship=None, dx_first=None):
    shift, scale, gate = mod
    proj = sv["proj"]
    d_u, d_gate = _gate_bwd(d_out, sv["u"], gate, f"gate_bwd_{tag}")
    (g_w_out,) = _matmul(sv["y"], d_u, dims="tn", tm=1024, tn=512, tk=2048, out_dtypes=(MXU,), name=f"out_proj_dw_{tag}")
    (d_y,) = _matmul(d_u, wt["w_out"], dims="nt", tm=1024, tn=512, tk=2048, out_dtypes=(F32,), name=f"out_proj_dx_{tag}",
                     after=(g_w_out,))
    d_z, d_osum, d_oatt, d_pre, d_gng, d_mog, d_cog = _merge_bwd(
        d_y, sv["o_f"], sv["o_b"], sv["o_att"], sv["pre"], proj, wt["gla_norm_g"], wt["mla_out_g"], wt["conv_out_g"],
        f"merge_bwd_{tag}")
    d_cb, d_cc, d_cx, d_conv_w = _conv_bwd(proj, wt["conv_w"], d_pre, f"conv_bwd_{tag}")
    d_q, d_k, d_v = _attn_bwd(sv["q"], sv["k"], sv["v"], sv["o_att"], sv["lse"], d_oatt, f"attn_bwd_{tag}")
    d_mq, d_mkv, dt_m, g_w_uq, g_w_ukv, d_qg, d_kvg = _mla_prep_bwd(
        proj, cos, sin, wt["q_norm_g"], wt["kv_norm_g"], wt["w_uq"], wt["w_ukv"], d_q, d_k, d_v, f"mla_prep_bwd_{tag}")
    dq_f, dk_f, dv_f, dt_f, d_wg_f, d_bg_f = _gla_bwd(proj, wt["wg_pad_f"], wt["bg_f"], sv["st_f"], d_osum, False,
                                                     f"gla_bwd_f_{tag}")
    dq_b, dk_b, dv_b, dt_b, d_wg_b, d_bg_b = _gla_bwd(proj, wt["wg_pad_b"], wt["bg_b"], sv["st_b"], d_osum, True,
                                                     f"gla_bwd_b_{tag}")
    d_proj = _assemble_dproj(d_z, d_cb, d_cc, d_cx, d_mkv, dv_f, dv_b, dq_f, dq_b, dk_f, dk_b, d_mq, dt_m, dt_f, dt_b,
                             f"assemble_dproj_{tag}")
    grads = dict(w_out=g_w_out, w_uq=g_w_uq, w_ukv=g_w_ukv,
                 wg_pad_f=d_wg_f, bg_f=d_bg_f, wg_pad_b=d_wg_b, bg_b=d_bg_b, gla_norm_g=d_gng,
                 q_norm_g=d_qg, kv_norm_g=d_kvg, mla_out_g=d_mog, conv_w=d_conv_w, conv_out_g=d_cog)

    def in_dw(after):
        (g_w_in,) = _matmul(sv["h"], d_proj, dims="tn", tm=1024, tn=256, tk=2048, out_dtypes=(MXU,),
                            name=f"in_proj_dw_{tag}", after=after)
        grads["w_in"] = g_w_in
        return dict(w_in=g_w_in, w_out=g_w_out, w_uq=g_w_uq, w_ukv=g_w_ukv)

    def in_dx(after):
        (d_h,) = _matmul(d_proj, wt["w_in"], dims="nt", tm=1024, tn=512, tk=PW // 2, out_dtypes=(F32,),
                         name=f"in_proj_dx_{tag}", after=after)
        d_x, d_shift, d_scale, d_ng = _norm_mod_bwd(d_h, sv["x"], d_out, wt["norm_g"], scale, f"norm_mod_bwd_{tag}")
        grads["norm_g"] = d_ng
        return d_x, (d_shift, d_scale, d_gate)

    if dx_first is None:
        big = in_dw(())
        d_x, d_mod = in_dx((big["w_in"],) if ship is None else ship(big))
    else:
        d_x, d_mod = in_dx(())
        big = in_dw(dx_first(d_x, d_mod, grads))
        ship(big)
    return d_x, d_mod, grads


def _perm_in_cols(w):
    pad = jnp.zeros(w.shape[:-1] + (PW - IN_DIM,), w.dtype)
    return jnp.concatenate([w[..., 3808:5856], w[..., 2272:3808], w[..., 1952:2208], w[..., 768:1536], w[..., 0:768],
                            w[..., 1568:1952], w[..., 2208:2272], w[..., 1536:1568], pad], axis=-1)


def _unperm_in_cols(g):
    return jnp.concatenate([g[..., OGQ:OGQ + 2 * GQK], g[..., OGV:OGV + GW], g[..., OT + MROPE:OT + MROPE + 2 * GRANK],
                            g[..., OMQ:OMQ + MQL], g[..., OMKV:OMKV + MKVL], g[..., OT:OT + MROPE],
                            g[..., OCB:OCB + 3 * CONV_CH], g[..., OZ:OZ + D_MIX]], axis=-1)


IN_SEGS = ((3808, 5856), (2272, 3808), (1952, 2208), (768, 1536), (0, 768), (1568, 1952), (2208, 2272), (1536, 1568))
UQ_SEGS = (tuple((h * (MNOPE + MROPE), h * (MNOPE + MROPE) + MNOPE) for h in range(MH))
           + tuple((h * (MNOPE + MROPE) + MNOPE, (h + 1) * (MNOPE + MROPE)) for h in range(MH)))


def _perm_gathered(g, segs, width):
    per = g.shape[-1]
    parts, total = [], 0
    for a, b in segs:
        c = a
        while c < b:
            j = c // per
            hi = min(b, (j + 1) * per)
            parts.append(g[j, :, c - j * per:hi - j * per])
            c = hi
        total += b - a
    if width > total:
        parts.append(jnp.zeros((g.shape[1], width - total), g.dtype))
    return jnp.concatenate(parts, axis=1)


def _scatter_perm(gp, segs, per):
    offs, o = [], 0
    for a, b in segs:
        offs.append((a, b, o))
        o += b - a
    blocks = []
    for j in range(N_DEV):
        lo, hi = j * per, (j + 1) * per
        pieces = []
        for a, b, o in sorted(offs):
            s0, s1 = max(a, lo), min(b, hi)
            if s0 < s1:
                pieces.append(gp[:, o + s0 - a:o + s1 - a])
        blocks.append(jnp.concatenate(pieces, axis=1))
    return jnp.stack(blocks)


def _perm_uq_cols(w):
    w3 = w.reshape(w.shape[:-1] + (MH, MNOPE + MROPE))
    return jnp.concatenate([w3[..., :MNOPE].reshape(w.shape[:-1] + (MH * MNOPE,)),
                            w3[..., MNOPE:].reshape(w.shape[:-1] + (MH * MROPE,))], axis=-1)


def _unperm_uq_cols(g):
    nope = g[..., :MH * MNOPE].reshape(g.shape[:-1] + (MH, MNOPE))
    rope = g[..., MH * MNOPE:].reshape(g.shape[:-1] + (MH, MROPE))
    return jnp.concatenate([nope, rope], axis=-1).reshape(g.shape[:-1] + (MQW,))


def _prep_layer_weights(w_in, w_out, w_uq, w_ukv, small):
    def vec(v):
        return v.reshape(1, -1).astype(F32)

    zeros = functools.partial(jnp.zeros, dtype=F32)
    wg_f, wg_b = small["gla_wg_f"].astype(F32), small["gla_wg_b"].astype(F32)
    wg_pad_f = jnp.concatenate([zeros((MROPE, GQK)), wg_f, zeros((LANES - MROPE - GRANK, GQK))], axis=0)
    wg_pad_b = jnp.concatenate([zeros((MROPE + GRANK, GQK)), wg_b, zeros((LANES - MROPE - 2 * GRANK, GQK))], axis=0)
    wt = dict(norm_g=vec(small["norm_g"]), wg_pad_f=wg_pad_f, wg_pad_b=wg_pad_b,
              bg_f=vec(small["gla_bg_f"]), bg_b=vec(small["gla_bg_b"]), gla_norm_g=vec(small["gla_norm_g"]),
              q_norm_g=vec(small["mla_q_norm_g"]), kv_norm_g=vec(small["mla_kv_norm_g"]),
              mla_out_g=vec(small["mla_out_g"]), conv_w=small["conv_w"].astype(F32),
              conv_out_g=vec(small["conv_out_g"]))
    for name, w in (("w_in", w_in), ("w_out", w_out), ("w_uq", w_uq), ("w_ukv", w_ukv)):
        if w is not None:
            wt[name] = w.astype(MXU)
    return wt


def _natural_small(gr):
    return dict(norm_g=gr["norm_g"][0],
                gla_wg_f=gr["wg_pad_f"][MROPE:MROPE + GRANK], gla_bg_f=gr["bg_f"][0],
                gla_wg_b=gr["wg_pad_b"][MROPE + GRANK:MROPE + 2 * GRANK], gla_bg_b=gr["bg_b"][0],
                gla_norm_g=gr["gla_norm_g"][0], mla_q_norm_g=gr["q_norm_g"][0], mla_kv_norm_g=gr["kv_norm_g"][0],
                mla_out_g=gr["mla_out_g"][0], conv_w=gr["conv_w"], conv_out_g=gr["conv_out_g"][0])


def _natural_grads(gr):
    return dict(_natural_small(gr), w_in=_unperm_in_cols(gr["w_in"]), w_out=gr["w_out"],
                mla_w_uq=_unperm_uq_cols(gr["w_uq"]), mla_w_ukv=gr["w_ukv"])


def _exchange(arrs, name, scatter, space):
    n = len(arrs)

    def body(*refs):
        ins, outs = refs[:n], refs[n:2 * n]
        send_sems, recv_sems, loc_sems = refs[2 * n:]
        ax, ay, ac = lax.axis_index("x"), lax.axis_index("y"), lax.axis_index("c")
        me = 4 * ax + 2 * ay + ac

        def src(a, to):
            return ins[a].at[to] if scatter else ins[a]

        def remote(a, r, dst_slot):
            px = 1 - ax if r & 4 else ax
            py = 1 - ay if r & 2 else ay
            pc = 1 - ac if r & 1 else ac
            return pltpu.make_async_remote_copy(
                src_ref=src(a, 4 * px + 2 * py + pc), dst_ref=outs[a].at[dst_slot(4 * px + 2 * py + pc)],
                send_sem=send_sems.at[a, r - 1], recv_sem=recv_sems.at[a, r - 1],
                device_id=(px, py, pc), device_id_type=MESH)

        locs = [pltpu.make_async_copy(src(a, me), outs[a].at[me], loc_sems.at[a]) for a in range(n)]
        for cp in locs:
            cp.start()
        sends = [remote(a, r, lambda peer: me) for r in range(1, N_DEV) for a in range(n)]
        for cp in sends:
            cp.start()
        for r in range(1, N_DEV):
            for a in range(n):
                remote(a, r, lambda peer: peer).wait_recv()
        for cp in sends:
            cp.wait_send()
        for cp in locs:
            cp.wait()

    def out_shape(a):
        return jax.ShapeDtypeStruct(a.shape if scatter else (N_DEV,) + a.shape, a.dtype)

    spec = pl.BlockSpec(memory_space=space)
    return pl.pallas_call(
        body, in_specs=[spec] * n, out_specs=[spec] * n, out_shape=[out_shape(a) for a in arrs],
        scratch_shapes=[pltpu.SemaphoreType.DMA((n, N_DEV - 1)), pltpu.SemaphoreType.DMA((n, N_DEV - 1)),
                        pltpu.SemaphoreType.DMA((n,))],
        name=name, compiler_params=pltpu.CompilerParams(vmem_limit_bytes=VMEM_LIMIT))(*arrs)


def _peer(r):
    ax, ay, ac = lax.axis_index("x"), lax.axis_index("y"), lax.axis_index("c")
    px = 1 - ax if r & 4 else ax
    py = 1 - ay if r & 2 else ay
    pc = 1 - ac if r & 1 else ac
    return (px, py, pc), 4 * px + 2 * py + pc


def _slot(rel_div):
    rel, div = rel_div
    idx = _peer(rel)[1]
    return idx if div == 1 else idx // div


AG_SPREAD = tuple((r, None, (0, 1), (r, 1)) for r in (1, 2, 4, 6))
AG_FORWARD = tuple((1, (k, 1), (k, 1), (1 ^ k, 1)) for k in (2, 4, 6))
RS_PAIR = tuple((1, (1 ^ k, 1), (1 ^ k, 2), (k, 2)) for k in (0, 2, 4, 6))
RS_CHIPS = tuple((r, (r, 2), (0, 2), (r, 2)) for r in (2, 4, 6))


def _plan_copies(plan, n, src_refs, land_refs, send_sems, recv_sems, arriving):
    out = []
    for i, (r, src, dst, recv) in enumerate(plan):
        peer = _peer(r)[0]
        for a in range(n):
            out.append(pltpu.make_async_remote_copy(
                src_ref=src_refs[a] if src is None else src_refs[a].at[_slot(src)],
                dst_ref=land_refs[a].at[_slot(recv if arriving else dst)],
                send_sem=send_sems.at[i * n + a], recv_sem=recv_sems.at[i * n + a],
                device_id=peer, device_id_type=MESH))
    return out


def _exchange_hbm(plan, srcs, lands, name, after=()):
    n = len(lands)
    fresh = isinstance(lands[0], jax.ShapeDtypeStruct)
    ins = ([] if srcs is None else list(srcs)) + ([] if fresh else list(lands))
    ns = 0 if srcs is None else n
    n_data = len(ins)
    ins = ins + list(after)

    def body(*refs):
        outs = refs[len(ins):len(ins) + n]
        send_sems, recv_sems = refs[-2:]
        src_refs = refs[:n] if srcs is not None else refs[ns:ns + n]
        sends = _plan_copies(plan, n, src_refs, outs, send_sems, recv_sems, False)
        for cp in sends:
            cp.start()
        for cp in _plan_copies(plan, n, src_refs, outs, send_sems, recv_sems, True):
            cp.wait_recv()
        for cp in sends:
            cp.wait_send()

    hbm = pl.BlockSpec(memory_space=pltpu.HBM)
    k = len(plan) * n
    return pl.pallas_call(
        body, name=name, in_specs=[hbm] * n_data + [pl.BlockSpec(memory_space=pl.ANY)] * len(after), out_specs=[hbm] * n,
        out_shape=[jax.ShapeDtypeStruct(a.shape, a.dtype) for a in lands],
        scratch_shapes=[pltpu.SemaphoreType.DMA((k,)), pltpu.SemaphoreType.DMA((k,))],
        input_output_aliases={} if fresh else {ns + i: i for i in range(n)},
        compiler_params=pltpu.CompilerParams(vmem_limit_bytes=VMEM_LIMIT))(*ins)


def _plan_start(plan, srcs, land_shapes, after, name):
    n = len(srcs)

    def body(*refs):
        src_refs, land_refs = refs[:n], refs[n:2 * n]
        send_sems, recv_sems = refs[2 * n + 1], refs[2 * n + 2]
        for cp in _plan_copies(plan, n, src_refs, land_refs, send_sems, recv_sems, False):
            cp.start()
        refs[-1][...] = jnp.zeros_like(refs[-1])

    hbm = pl.BlockSpec(memory_space=pltpu.HBM)
    sem = pl.BlockSpec(memory_space=pltpu.SEMAPHORE)
    k = len(plan) * n
    srcs = [pltpu.with_memory_space_constraint(a, pltpu.HBM) for a in srcs]
    lands = [pltpu.with_memory_space_constraint(lax.empty(shp, a.dtype), pltpu.HBM) for shp, a in zip(land_shapes, srcs)]
    res = pl.pallas_call(
        body, name=name,
        in_specs=[hbm] * (2 * n) + [pl.BlockSpec(memory_space=pl.ANY)],
        out_specs=[sem, sem] + [hbm] * (2 * n) + [pl.BlockSpec(memory_space=pltpu.VMEM)],
        out_shape=[pltpu.SemaphoreType.DMA((k,)), pltpu.SemaphoreType.DMA((k,))]
        + [pltpu.HBM(a.shape, a.dtype) for a in srcs] + [pltpu.HBM(shp, a.dtype) for shp, a in zip(land_shapes, srcs)]
        + [jax.ShapeDtypeStruct((8, LANES), F32)],
        input_output_aliases={i: 2 + i for i in range(2 * n)},
        compiler_params=pltpu.CompilerParams(has_side_effects=pltpu.SideEffectType.DATAFLOW_SIDE_EFFECTING),
    )(*srcs, *lands, after)
    return res[0], res[1], list(res[2:2 + n]), list(res[2 + n:2 + 2 * n]), res[-1]


def _plan_wait(plan, handle, after, name):
    send_sems, recv_sems, srcs, lands, _ = handle
    n = len(srcs)
    after = list(after)

    def body(*refs):
        src_refs, land_refs = refs[:n], refs[n:2 * n]
        ssem, rsem = refs[2 * n], refs[2 * n + 1]
        for cp in _plan_copies(plan, n, src_refs, land_refs, ssem, rsem, False):
            cp.wait_send()
        for cp in _plan_copies(plan, n, src_refs, land_refs, ssem, rsem, True):
            cp.wait_recv()

    hbm = pl.BlockSpec(memory_space=pltpu.HBM)
    sem = pl.BlockSpec(memory_space=pltpu.SEMAPHORE)
    res = pl.pallas_call(
        body, name=name,
        in_specs=[hbm] * (2 * n) + [sem, sem] + [pl.BlockSpec(memory_space=pl.ANY)] * len(after),
        out_specs=[hbm] * (2 * n),
        out_shape=[pltpu.HBM(a.shape, a.dtype) for a in srcs] + [pltpu.HBM(a.shape, a.dtype) for a in lands],
        input_output_aliases={i: i for i in range(2 * n)},
        compiler_params=pltpu.CompilerParams(has_side_effects=pltpu.SideEffectType.DATAFLOW_SIDE_EFFECTING),
    )(*srcs, *lands, send_sems, recv_sems, *after)
    return list(res[n:])


def _pair_sum(send, got, core, name):
    _, r, c = send.shape
    tr = 256 if r % 256 == 0 else r

    def body(core_ref, s_ref, g_ref, o_ref):
        o_ref[0] = (s_ref[0].astype(F32) + g_ref[0].astype(F32)).astype(o_ref.dtype)

    return pl.pallas_call(
        body, name=name,
        grid_spec=pltpu.PrefetchScalarGridSpec(
            num_scalar_prefetch=1, grid=(N_DEV // 2, r // tr),
            in_specs=[pl.BlockSpec((1, tr, c), lambda kc, i, core_ref: (2 * kc + core_ref[0], i, 0)),
                      pl.BlockSpec((1, tr, c), lambda kc, i, core_ref: (kc, i, 0))],
            out_specs=pl.BlockSpec((1, tr, c), lambda kc, i, core_ref: (kc, i, 0))),
        out_shape=jax.ShapeDtypeStruct((N_DEV // 2, r, c), send.dtype),
        compiler_params=_cp(("parallel", "parallel")))(core, send, got)


def _ada_mod(c_all, ada_w, ada_b_cols, name):
    nl, d, wc = ada_w.shape

    def body(c_ref, w_ref, b_ref, ca_ref, mod_ref):
        cv = c_ref[...]
        ca = cv * _sigmoid(cv)
        ca_ref[...] = ca
        mod_ref[0] = _dotf(ca, w_ref[0]) + b_ref[0]

    return pl.pallas_call(
        body, grid=(nl,),
        in_specs=[pl.BlockSpec((N_DEV, d), lambda l: (0, 0)), pl.BlockSpec((1, d, wc), lambda l: (l, 0, 0)),
                  pl.BlockSpec((1, 1, wc), lambda l: (l, 0, 0))],
        out_specs=[pl.BlockSpec((N_DEV, d), lambda l: (0, 0)), pl.BlockSpec((1, N_DEV, wc), lambda l: (l, 0, 0))],
        out_shape=[jax.ShapeDtypeStruct((N_DEV, d), F32), jax.ShapeDtypeStruct((nl, N_DEV, wc), F32)],
        name=name, compiler_params=_cp(("arbitrary",)))(c_all, ada_w, ada_b_cols)


def _adam(w, g, m, v):
    m2 = ADAM_B1 * m + (1.0 - ADAM_B1) * g
    v2 = ADAM_B2 * v + (1.0 - ADAM_B2) * (g * g)
    m_hat = m2 / (1.0 - ADAM_B1 ** ADAM_STEP)
    v_hat = v2 / (1.0 - ADAM_B2 ** ADAM_STEP)
    delta = -ADAM_LR * (m_hat / (jnp.sqrt(v_hat) + ADAM_EPS) + ADAM_WD * w)
    return delta, m2, v2


def _ada_grad_adam(c_act, d_mod, w, m, v, name):
    nl, d, wc = w.shape
    tk = min(512, d)

    def body(c_ref, dm_ref, w_ref, m_ref, v_ref, g_ref, dl_ref, m2_ref, v2_ref):
        g = _dotf_tn(c_ref[...], dm_ref[0])
        delta, m2, v2 = _adam(w_ref[0], g, m_ref[0], v_ref[0])
        g_ref[0], dl_ref[0], m2_ref[0], v2_ref[0] = g, delta, m2, v2

    blk = pl.BlockSpec((1, tk, wc), lambda l, i: (l, i, 0))
    shp = jax.ShapeDtypeStruct(w.shape, F32)
    return pl.pallas_call(
        body, grid=(nl, d // tk),
        in_specs=[pl.BlockSpec((N_DEV, tk), lambda l, i: (0, i)), pl.BlockSpec((1, N_DEV, wc), lambda l, i: (l, 0, 0)),
                  blk, blk, blk],
        out_specs=[blk] * 4, out_shape=[shp] * 4, name=name,
        compiler_params=_cp(("parallel", "parallel")))(c_act, d_mod, w, m, v)


def _adam_big(recv, w, m, v, layer, prev, name):
    nl, r, c = w.shape
    tr = 256 if r % 256 == 0 else r
    nparts = recv.shape[0]

    def body(rc_ref, w_ref, m_ref, v_ref, *rest):
        g_ref, dl_ref, m2_ref, v2_ref = rest[-4:]
        g = rc_ref[0].astype(F32)
        for d in range(1, nparts):
            g = g + rc_ref[d].astype(F32)
        delta, m2, v2 = _adam(w_ref[0], g, m_ref[0], v_ref[0])
        g_ref[0], dl_ref[0], m2_ref[0], v2_ref[0] = g, delta, m2, v2

    blk = pl.BlockSpec((1, tr, c), lambda i: (layer, i, 0))
    shp = jax.ShapeDtypeStruct(w.shape, F32)
    prev = () if prev is None else tuple(prev)
    return pl.pallas_call(
        body, grid=(r // tr,),
        in_specs=[pl.BlockSpec((nparts, tr, c), lambda i: (0, i, 0)), blk, blk, blk]
        + [pl.BlockSpec(memory_space=pl.ANY)] * len(prev),
        out_specs=[blk] * 4, out_shape=[shp] * 4, name=name,
        input_output_aliases={4 + j: j for j in range(len(prev))},
        compiler_params=_cp(("parallel",)))(recv, w, m, v, *prev)


def _sum_devices(gathered, name):
    _, r, c = gathered.shape

    def body(g_ref, o_ref):
        acc = g_ref[0]
        for d in range(1, N_DEV):
            acc = acc + g_ref[d]
        o_ref[...] = acc

    spec = pl.BlockSpec(memory_space=pltpu.VMEM)
    return pl.pallas_call(body, in_specs=[spec], out_specs=spec, out_shape=jax.ShapeDtypeStruct((r, c), F32),
                          name=name, compiler_params=pltpu.CompilerParams(vmem_limit_bytes=VMEM_LIMIT))(gathered)


def _adam_small(w, g, m, v, name):
    def body(w_ref, g_ref, m_ref, v_ref, dl_ref, m2_ref, v2_ref):
        dl_ref[...], m2_ref[...], v2_ref[...] = _adam(w_ref[...], g_ref[...], m_ref[...], v_ref[...])

    spec = pl.BlockSpec(memory_space=pltpu.VMEM)
    shp = jax.ShapeDtypeStruct(w.shape, F32)
    return pl.pallas_call(body, in_specs=[spec] * 4, out_specs=[spec] * 3, out_shape=[shp] * 3, name=name,
                          compiler_params=pltpu.CompilerParams(vmem_limit_bytes=VMEM_LIMIT))(w, g, m, v)


def _pack(parts):
    flat = jnp.concatenate([p.reshape(-1).astype(F32) for p in parts])
    assert flat.shape[0] % LANES == 0, flat.shape
    return flat.reshape(-1, LANES)


def _unpack(packed, shapes):
    flat = packed.reshape(-1)
    out, off = [], 0
    for shp in shapes:
        size = 1
        for dim in shp:
            size *= dim
        out.append(flat[off:off + size].reshape(shp))
        off += size
    return out


def _gather_cols(g, per):
    g = jnp.moveaxis(g, 0, -2)
    return g.reshape(g.shape[:-2] + (N_DEV * per,))


def _scatter_cols(g, per):
    return jnp.moveaxis(g.reshape(g.shape[:-1] + (N_DEV, per)), -2, 0)


def _my_cols(full, me, per):
    return lax.dynamic_slice_in_dim(full, me * per, per, axis=full.ndim - 1)


def kernel(x, c, positions, ada_w, ada_b, norm_g, w_in, gla_wg_f, gla_bg_f, gla_wg_b, gla_bg_b, gla_norm_g, mla_q_norm_g, mla_kv_norm_g, mla_w_uq, mla_w_ukv, mla_out_g, conv_w, conv_out_g, w_out, final_g, loss_target, m_ada_w, m_ada_b, m_norm_g, m_w_in, m_gla_wg_f, m_gla_bg_f, m_gla_wg_b, m_gla_bg_b, m_gla_norm_g, m_mla_q_norm_g, m_mla_kv_norm_g, m_mla_w_uq, m_mla_w_ukv, m_mla_out_g, m_conv_w, m_conv_out_g, m_w_out, m_final_g, v_ada_w, v_ada_b, v_norm_g, v_w_in, v_gla_wg_f, v_gla_bg_f, v_gla_wg_b, v_gla_bg_b, v_gla_norm_g, v_mla_q_norm_g, v_mla_kv_norm_g, v_mla_w_uq, v_mla_w_ukv, v_mla_out_g, v_conv_w, v_conv_out_g, v_w_out, v_final_g):
    me = 4 * lax.axis_index("x") + 2 * lax.axis_index("y") + lax.axis_index("c")
    nl = ada_w.shape[0]
    s, d = x.shape[1], x.shape[2]
    ada_cols = ada_w.shape[2]
    wgc, cwc = gla_wg_f.shape[2], conv_w.shape[2]

    (g0,) = _exchange([_pack([c, gla_wg_f, gla_wg_b, conv_w])], "gather_small_in", False, pltpu.VMEM)
    g0 = g0.reshape(N_DEV, -1)
    o1, o2, o3 = d, d + gla_wg_f.size, d + 2 * gla_wg_f.size
    c_all = g0[:, :o1]
    wgf_full = _gather_cols(g0[:, o1:o2].reshape((N_DEV,) + gla_wg_f.shape), wgc)
    wgb_full = _gather_cols(g0[:, o2:o3].reshape((N_DEV,) + gla_wg_b.shape), wgc)
    convw_full = _gather_cols(g0[:, o3:].reshape((N_DEV,) + conv_w.shape), cwc)

    ada_b_cols = _my_cols(ada_b, me, ada_cols).reshape(nl, 1, ada_cols)
    c_act, mod_cols = _ada_mod(c_all, ada_w, ada_b_cols, "ada_mod")
    (g1,) = _exchange([_pack([mod_cols])], "gather_mod", False, pltpu.VMEM)
    mod_all = g1.reshape(N_DEV, nl, N_DEV, ada_cols)
    mod_mine = _gather_cols(lax.dynamic_index_in_dim(mod_all, me, axis=2, keepdims=False), ada_cols)

    inv_freq = ROPE_THETA ** (-jnp.arange(0, MROPE, 2, dtype=F32) / MROPE)
    ang = positions[0].astype(F32)[:, None] * inv_freq
    cos, sin = jnp.tile(jnp.cos(ang), (1, LANES * 2 // MROPE)), jnp.tile(jnp.sin(ang), (1, LANES * 2 // MROPE))

    big = [w_in, w_out, mla_w_uq, mla_w_ukv]
    big_names = ["w_in", "w_out", "mla_w_uq", "mla_w_ukv"]

    def local_blocks(l):
        return [w[l].astype(MXU) for w in big]

    def put_own(lands, own):
        return [lax.dynamic_update_index_in_dim(ld, o, me, 0) for ld, o in zip(lands, own)]

    def layer_weights(l, gw_in=None, gw_out=None, gw_uq=None, gw_ukv=None):
        small = dict(norm_g=norm_g[l], gla_wg_f=wgf_full[l], gla_bg_f=gla_bg_f[l], gla_wg_b=wgb_full[l],
                     gla_bg_b=gla_bg_b[l], gla_norm_g=gla_norm_g[l], mla_q_norm_g=mla_q_norm_g[l],
                     mla_kv_norm_g=mla_kv_norm_g[l], mla_out_g=mla_out_g[l], conv_w=convw_full[l],
                     conv_out_g=conv_out_g[l])
        return _prep_layer_weights(
            None if gw_in is None else _perm_gathered(gw_in, IN_SEGS, PW),
            None if gw_out is None else gw_out.reshape((-1,) + gw_out.shape[2:]),
            None if gw_uq is None else _perm_gathered(gw_uq, UQ_SEGS, MQW),
            None if gw_ukv is None else _gather_cols(gw_ukv, mla_w_ukv.shape[2]), small)

    def land_shapes(blocks, slots):
        return [jax.ShapeDtypeStruct((slots,) + b.shape, b.dtype) for b in blocks]

    def slots_of(blocks):
        return [(N_DEV,) + b.shape for b in blocks]

    def forwarded(lands, blocks, tag):
        return put_own(_exchange_hbm(AG_FORWARD, None, lands, f"gather_{tag}_forward"), blocks)

    first = local_blocks(0)
    lands = _exchange_hbm(AG_SPREAD, first[:1], land_shapes(first[:1], N_DEV), "gather_w_in_l0_spread", (mod_mine,))
    (gw_in,) = forwarded(lands, first[:1], "w_in_l0")
    rest = _plan_start(AG_SPREAD, first[1:], slots_of(first[1:]), gw_in, "gather_rest_l0_start")
    h = x[0]
    saved, layers, mods = [], [], []
    pending = {}
    for l in range(nl):
        shift, scale, gate = (mod_mine[l, i * d:(i + 1) * d].reshape(1, d) for i in range(3))
        nxt = local_blocks(l + 1) if l + 1 < nl else None

        def start_next(after, wt_late, l=l, nxt=nxt):
            if nxt is not None:
                pending[l + 1] = _plan_start(AG_SPREAD, nxt, slots_of(nxt), after, f"gather_weights_l{l + 1}_start")
                wt_late["q_norm_g"] = layers[l]["q_norm_g"] + pending[l + 1][-1][0, 0]
            return wt_late

        if l == 0:
            scale = scale + rest[-1][0, 0]
            layers.append(layer_weights(0, gw_in))

            def late(proj):
                got = forwarded(_plan_wait(AG_SPREAD, rest, [proj], "gather_rest_l0_wait"), first[1:], "rest_l0")
                full = layer_weights(0, None, *got)
                return start_next(got[0], {k: full[k] for k in ("w_out", "w_uq", "w_ukv")})
        else:
            got = forwarded(_plan_wait(AG_SPREAD, pending.pop(l), [h], f"gather_weights_l{l}_wait"), blocks, f"weights_l{l}")
            layers.append(layer_weights(l, *got))

            def late(proj):
                return start_next(proj, {})
        mods.append((shift, scale, gate))
        h, sv = _layer_fwd(h, mods[l], layers[l], cos, sin, f"l{l}", late)
        saved.append(sv)
        blocks = nxt
    loss_part, d_h, d_final_g = _final_loss(h, final_g.reshape(1, d), loss_target[0], "final_loss")
    loss = lax.psum(loss_part[0, 0], ("x", "y", "c"))

    def grad_sends(gr):
        return [_scatter_perm(gr["w_in"], IN_SEGS, w_in.shape[2]).astype(MXU),
                gr["w_out"].reshape((N_DEV,) + w_out.shape[1:]).astype(MXU),
                _scatter_perm(gr["w_uq"], UQ_SEGS, mla_w_uq.shape[2]).astype(MXU),
                _scatter_cols(gr["w_ukv"], mla_w_ukv.shape[2]).astype(MXU)]

    my_chip = me // 2
    my_core = (me % 2).astype(jnp.int32).reshape(1)

    def chip_sums(gr, tag):
        sends = grad_sends(gr)
        got = _exchange_hbm(RS_PAIR, sends, land_shapes([sd[0] for sd in sends], N_DEV // 2), f"scatter_grads_{tag}_pair")
        return [_pair_sum(sd, gt, my_core, f"pair_sum_{n}_{tag}") for sd, gt, n in zip(sends, got, big_names)]

    def with_own_chip(lands, sums):
        return [lax.dynamic_update_index_in_dim(ld, lax.dynamic_index_in_dim(sm, my_chip, axis=0, keepdims=False),
                                                my_chip, 0) for ld, sm in zip(lands, sums)]

    small_names = ["norm_g", "gla_wg_f", "gla_bg_f", "gla_wg_b", "gla_bg_b", "gla_norm_g", "mla_q_norm_g",
                   "mla_kv_norm_g", "mla_out_g", "conv_w", "conv_out_g"]
    d_mods, grads, recv = [None] * nl, [None] * nl, [None] * nl
    flight = {}
    small = {}

    def gather_small(d_x, d_mod0, gr0):
        d_mods[0], grads[0] = d_mod0, _natural_small(gr0)
        d_mod_mine = jnp.stack([jnp.concatenate(d_mods[l], axis=-1)[0] for l in range(nl)])
        parts = [d_mod_mine] + [jnp.stack([grads[l][n] for l in range(nl)]) for n in small_names] + [d_final_g]
        (g2,) = _exchange([_pack(parts)], "gather_small_grads", False, pltpu.VMEM)
        small["d_mod_all"] = g2.reshape(N_DEV, -1)[:, :d_mod_mine.size].reshape(N_DEV, nl, 3 * d)
        small["summed"] = dict(zip(["ada_b"] + small_names + ["final_g"],
                                   _unpack(_sum_devices(g2, "sum_small_grads"), [p.shape for p in parts])))
        return (g2,)

    for l in reversed(range(nl)):
        def ship(big_grads, l=l):
            if l + 1 in flight:
                pend, sm = flight.pop(l + 1)
                recv[l + 1] = with_own_chip(_plan_wait(RS_CHIPS, pend, list(big_grads.values()),
                                                       f"scatter_grads_l{l + 1}_wait"), sm)
            sm = chip_sums(big_grads, f"l{l}")
            after = recv[l + 1][0] if l + 1 < nl else big_grads["w_in"]
            flight[l] = (_plan_start(RS_CHIPS, sm, [a.shape for a in sm], after, f"scatter_grads_l{l}_start"), sm)
            return (flight[l][0][-1],)

        if l > 0:
            d_h, d_mods[l], gr = _layer_bwd(d_h, saved[l], mods[l], layers[l], cos, sin, f"l{l}", ship)
            grads[l] = _natural_small(gr)
        else:
            d_h, _, _ = _layer_bwd(d_h, saved[l], mods[l], layers[l], cos, sin, f"l{l}", ship, gather_small)
    pending, sums = flight.pop(0)
    grad_x = d_h[None]
    summed = small["summed"]
    summed["gla_wg_f"] = _my_cols(summed["gla_wg_f"], me, wgc)
    summed["gla_wg_b"] = _my_cols(summed["gla_wg_b"], me, wgc)
    summed["conv_w"] = _my_cols(summed["conv_w"], me, cwc)

    d_mod_cols = jnp.moveaxis(_my_cols(small["d_mod_all"], me, ada_cols), 0, 1) + pending[-1][0, 0]
    out = {}
    out["ada_w"] = _ada_grad_adam(c_act, d_mod_cols, ada_w, m_ada_w, v_ada_w, "ada_grad_adam")

    given = dict(ada_b=(ada_b, m_ada_b, v_ada_b), norm_g=(norm_g, m_norm_g, v_norm_g),
                 gla_wg_f=(gla_wg_f, m_gla_wg_f, v_gla_wg_f), gla_bg_f=(gla_bg_f, m_gla_bg_f, v_gla_bg_f),
                 gla_wg_b=(gla_wg_b, m_gla_wg_b, v_gla_wg_b), gla_bg_b=(gla_bg_b, m_gla_bg_b, v_gla_bg_b),
                 gla_norm_g=(gla_norm_g, m_gla_norm_g, v_gla_norm_g),
                 mla_q_norm_g=(mla_q_norm_g, m_mla_q_norm_g, v_mla_q_norm_g),
                 mla_kv_norm_g=(mla_kv_norm_g, m_mla_kv_norm_g, v_mla_kv_norm_g),
                 mla_out_g=(mla_out_g, m_mla_out_g, v_mla_out_g), conv_w=(conv_w, m_conv_w, v_conv_w),
                 conv_out_g=(conv_out_g, m_conv_out_g, v_conv_out_g), final_g=(final_g, m_final_g, v_final_g))
    names = list(given)
    wshapes = [given[n][0].shape for n in names]
    packed = [_pack([given[n][i] for n in names]) for i in range(3)]
    g_small = _pack([summed[n].reshape(given[n][0].shape) for n in names])
    res = _adam_small(packed[0], g_small, packed[1], packed[2], "adam_small")
    unpacked = [_unpack(r, wshapes) for r in res]
    for i, n in enumerate(names):
        out[n] = (summed[n].reshape(given[n][0].shape), unpacked[0][i], unpacked[1][i], unpacked[2][i])

    moments = dict(w_in=(m_w_in, v_w_in), w_out=(m_w_out, v_w_out), mla_w_uq=(m_mla_w_uq, v_mla_w_uq),
                   mla_w_ukv=(m_mla_w_ukv, v_mla_w_ukv))
    done = [out["ada_w"][0], res[0]]
    for l in reversed(range(nl)):
        if l == 0:
            recv[0] = with_own_chip(_plan_wait(RS_CHIPS, pending, done, "scatter_grads_l0_wait"), sums)
        for i, n in enumerate(big_names):
            out[n] = _adam_big(recv[l][i], big[i], moments[n][0], moments[n][1], l, out.get(n), f"adam_{n}_l{l}")
        done = done + [out[n][0] for n in big_names]

    order = ["ada_w", "ada_b", "norm_g", "w_in", "gla_wg_f", "gla_bg_f", "gla_wg_b", "gla_bg_b", "gla_norm_g",
             "mla_q_norm_g", "mla_kv_norm_g", "mla_w_uq", "mla_w_ukv", "mla_out_g", "conv_w", "conv_out_g", "w_out",
             "final_g"]
    return (loss, grad_x, *[out[n][0] for n in order], *[out[n][1] for n in order], *[out[n][2] for n in order],
            *[out[n][3] for n in order])
```

```python
import functools

import jax
import jax.numpy as jnp
from jax import lax
from jax.experimental import pallas as pl
from jax.experimental.pallas import tpu as pltpu

F32 = jnp.float32
MXU = jnp.bfloat16
HI = lax.Precision.HIGHEST
N_DEV = 8
MESH = pl.DeviceIdType.MESH

D_MIX = 2048
GH, GDK, GDV = 6, 64, 128
GW = GH * GDV
GQK = GH * GDK
GRANK = 16
GTEMP = 16.0
CHUNK = 64
MH, MQL, MKVL, MNOPE, MROPE, MDV = 6, 384, 256, 128, 64, 128
MW = MH * MDV
MQW = MH * (MNOPE + MROPE)
MKVW = MH * (MNOPE + MDV)
CONV_CH = 512
ROPE_THETA = 10000.0
EPS = 1e-6
IN_DIM = 5856
OZ, OCB, OCC, OCX, OMKV, OGV, OGQ, OGK, OMQ, OT = 0, 2048, 2560, 3072, 3584, 3840, 4608, 4992, 5376, 5760
PW = 5888
LANES = 128
VMEM_LIMIT = 56 * 1024 * 1024

ADAM_LR, ADAM_B1, ADAM_B2, ADAM_EPS, ADAM_WD, ADAM_STEP = 0.001, 0.9, 0.999, 1e-08, 0.01, 10


def _cp(sem=None):
    return pltpu.CompilerParams(dimension_semantics=sem, vmem_limit_bytes=VMEM_LIMIT)


def _dot(a, b):
    return jnp.dot(a.astype(MXU), b.astype(MXU), preferred_element_type=F32)


def _dot_nt(a, b):
    return lax.dot_general(a.astype(MXU), b.astype(MXU), (((1,), (1,)), ((), ())), preferred_element_type=F32)


def _dot_tn(a, b):
    return lax.dot_general(a.astype(MXU), b.astype(MXU), (((0,), (0,)), ((), ())), preferred_element_type=F32)


def _dotf(a, b):
    return jnp.dot(a, b, precision=HI, preferred_element_type=F32)


def _dotf_nt(a, b):
    return lax.dot_general(a, b, (((1,), (1,)), ((), ())), precision=HI, preferred_element_type=F32)


def _dotf_tn(a, b):
    return lax.dot_general(a, b, (((0,), (0,)), ((), ())), precision=HI, preferred_element_type=F32)


def _split3(x):
    hi = x.astype(jnp.bfloat16)
    r1 = x - hi.astype(F32)
    mid = r1.astype(jnp.bfloat16)
    lo = (r1 - mid.astype(F32)).astype(jnp.bfloat16)
    return hi, mid, lo


def _cum_dot(cum, x, transpose=False):
    dn = (((0,), (0,)), ((), ())) if transpose else (((1,), (0,)), ((), ()))
    cb = cum.astype(jnp.bfloat16)
    parts = [lax.dot_general(cb, p, dn, preferred_element_type=F32) for p in _split3(x)]
    return parts[0] + parts[1] + parts[2]


def _rows(s):
    return min(256, s)


def _rms(x, g):
    r = lax.rsqrt(jnp.mean(x * x, axis=-1, keepdims=True) + EPS)
    return x * r * g


def _rms_bwd(dy, x, g):
    r = lax.rsqrt(jnp.mean(x * x, axis=-1, keepdims=True) + EPS)
    xh = x * r
    dxh = dy * g
    dg = jnp.sum(dy * xh, axis=0, keepdims=True)
    dx = r * (dxh - xh * jnp.mean(dxh * xh, axis=-1, keepdims=True))
    return dx, dg


def _sigmoid(z):
    return 1.0 / (1.0 + jnp.exp(-z))


def _matmul(a, b, *, dims, tm, tn, tk, out_dtypes, name, epilogue=None, extras=(), extra_kinds=(), after=()):
    if dims == "nn":
        (m, k), n, mul = a.shape, b.shape[1], _dot
    elif dims == "nt":
        (m, k), n, mul = a.shape, b.shape[0], _dot_nt
    else:
        (k, m), n, mul = a.shape, b.shape[1], _dot_tn
    tm, tn, tk = min(tm, m), min(tn, n), min(tk, k)
    assert m % tm == 0 and n % tn == 0 and k % tk == 0, (m, n, k, tm, tn, tk)
    if dims == "nn":
        a_spec = pl.BlockSpec((tm, tk), lambda i, j, kk: (i, kk))
        b_spec = pl.BlockSpec((tk, tn), lambda i, j, kk: (kk, j))
    elif dims == "nt":
        a_spec = pl.BlockSpec((tm, tk), lambda i, j, kk: (i, kk))
        b_spec = pl.BlockSpec((tn, tk), lambda i, j, kk: (j, kk))
    else:
        a_spec = pl.BlockSpec((tk, tm), lambda i, j, kk: (kk, i))
        b_spec = pl.BlockSpec((tk, tn), lambda i, j, kk: (kk, j))
    nk = k // tk
    n_extra = len(extras)
    n_out = len(out_dtypes)
    n_after = len(after)
    extra_specs = []
    for kind in extra_kinds:
        if kind == "mn":
            extra_specs.append(pl.BlockSpec((tm, tn), lambda i, j, kk: (i, j)))
        else:
            extra_specs.append(pl.BlockSpec((1, tn), lambda i, j, kk: (0, j)))

    def body(*refs):
        a_ref, b_ref = refs[0], refs[1]
        ex = refs[2:2 + n_extra]
        outs = refs[2 + n_extra + n_after:2 + n_extra + n_after + n_out]
        acc = refs[-1]
        kk = pl.program_id(2)

        @pl.when(kk == 0)
        def _():
            acc[...] = jnp.zeros_like(acc)

        acc[...] += mul(a_ref[...], b_ref[...])

        @pl.when(kk == nk - 1)
        def _():
            res = acc[...]
            vals = (res,) if epilogue is None else epilogue(res, *[e[...] for e in ex])
            for o, v in zip(outs, vals):
                o[...] = v.astype(o.dtype)

    out_spec = pl.BlockSpec((tm, tn), lambda i, j, kk: (i, j))
    res = pl.pallas_call(
        body, grid=(m // tm, n // tn, nk),
        in_specs=[a_spec, b_spec] + extra_specs + [pl.BlockSpec(memory_space=pl.ANY)] * n_after,
        out_specs=[out_spec] * n_out,
        out_shape=[jax.ShapeDtypeStruct((m, n), dt) for dt in out_dtypes],
        scratch_shapes=[pltpu.VMEM((tm, tn), F32)],
        name=name, compiler_params=_cp(("parallel", "parallel", "arbitrary")),
    )(a, b, *extras, *after)
    return res


def _norm_mod(x, g, scale, shift, name):
    s, d = x.shape
    tr = _rows(s)

    def body(x_ref, g_ref, sc_ref, sh_ref, h_ref):
        h = _rms(x_ref[...], g_ref[...]) * (1.0 + sc_ref[...]) + sh_ref[...]
        h_ref[...] = h.astype(h_ref.dtype)

    row = pl.BlockSpec((tr, d), lambda i: (i, 0))
    vec = pl.BlockSpec((1, d), lambda i: (0, 0))
    return pl.pallas_call(body, grid=(s // tr,), in_specs=[row, vec, vec, vec], out_specs=row,
                          out_shape=jax.ShapeDtypeStruct((s, d), MXU), name=name,
                          compiler_params=_cp(("parallel",)))(x, g, scale, shift)


def _norm_mod_bwd(d_h, x, d_out, g, scale, name):
    s, d = x.shape
    tr = _rows(s)

    def body(dh_ref, x_ref, do_ref, g_ref, sc_ref, dx_ref, dsh_ref, dsc_ref, dg_ref):
        i = pl.program_id(0)

        @pl.when(i == 0)
        def _():
            dsh_ref[...] = jnp.zeros_like(dsh_ref)
            dsc_ref[...] = jnp.zeros_like(dsc_ref)
            dg_ref[...] = jnp.zeros_like(dg_ref)

        dh = dh_ref[...]
        xv = x_ref[...]
        gv = g_ref[...]
        r = lax.rsqrt(jnp.mean(xv * xv, axis=-1, keepdims=True) + EPS)
        xh = xv * r
        dsh_ref[...] += jnp.sum(dh, axis=0, keepdims=True)
        dsc_ref[...] += jnp.sum(dh * (xh * gv), axis=0, keepdims=True)
        dhn = dh * (1.0 + sc_ref[...])
        dg_ref[...] += jnp.sum(dhn * xh, axis=0, keepdims=True)
        dxh = dhn * gv
        dx_ref[...] = do_ref[...] + r * (dxh - xh * jnp.mean(dxh * xh, axis=-1, keepdims=True))

    row = pl.BlockSpec((tr, d), lambda i: (i, 0))
    vec = pl.BlockSpec((1, d), lambda i: (0, 0))
    vshape = jax.ShapeDtypeStruct((1, d), F32)
    return pl.pallas_call(body, grid=(s // tr,), in_specs=[row, row, row, vec, vec],
                          out_specs=[row, vec, vec, vec],
                          out_shape=[jax.ShapeDtypeStruct((s, d), F32), vshape, vshape, vshape],
                          name=name, compiler_params=_cp(("arbitrary",)))(d_h, x, d_out, g, scale)


def _gate_bwd(d_out, u, gate, name):
    s, d = d_out.shape
    tr = _rows(s)

    def body(do_ref, u_ref, gt_ref, du_ref, dgt_ref):
        @pl.when(pl.program_id(0) == 0)
        def _():
            dgt_ref[...] = jnp.zeros_like(dgt_ref)

        do = do_ref[...]
        du_ref[...] = (do * gt_ref[...]).astype(du_ref.dtype)
        dgt_ref[...] += jnp.sum(do * u_ref[...], axis=0, keepdims=True)

    row = pl.BlockSpec((tr, d), lambda i: (i, 0))
    vec = pl.BlockSpec((1, d), lambda i: (0, 0))
    return pl.pallas_call(body, grid=(s // tr,), in_specs=[row, row, vec], out_specs=[row, vec],
                          out_shape=[jax.ShapeDtypeStruct((s, d), MXU), jax.ShapeDtypeStruct((1, d), F32)],
                          name=name, compiler_params=_cp(("arbitrary",)))(d_out, u, gate)


def _final_loss(x, g, target, name):
    s, d = x.shape
    tr = _rows(s)

    def body(x_ref, g_ref, t_ref, loss_ref, dx_ref, dg_ref):
        @pl.when(pl.program_id(0) == 0)
        def _():
            loss_ref[...] = jnp.zeros_like(loss_ref)
            dg_ref[...] = jnp.zeros_like(dg_ref)

        xv = x_ref[...]
        gv = g_ref[...]
        diff = _rms(xv, gv) - t_ref[...]
        part = 0.5 * jnp.sum(jnp.sum(diff * diff, axis=-1, keepdims=True) / d, axis=0, keepdims=True)
        loss_ref[...] += jnp.broadcast_to(part, loss_ref.shape)
        dx, dg = _rms_bwd(diff / d, xv, gv)
        dx_ref[...] = dx
        dg_ref[...] += dg

    row = pl.BlockSpec((tr, d), lambda i: (i, 0))
    vec = pl.BlockSpec((1, d), lambda i: (0, 0))
    lvec = pl.BlockSpec((1, LANES), lambda i: (0, 0))
    return pl.pallas_call(body, grid=(s // tr,), in_specs=[row, vec, row], out_specs=[lvec, row, vec],
                          out_shape=[jax.ShapeDtypeStruct((1, LANES), F32), jax.ShapeDtypeStruct((s, d), F32),
                                     jax.ShapeDtypeStruct((1, d), F32)],
                          name=name, compiler_params=_cp(("arbitrary",)))(x, g, target)


def _shift_rows(u, s, down):
    ri = lax.broadcasted_iota(jnp.int32, u.shape, 0)
    if down:
        return jnp.where(ri == 0, 0.0, pltpu.roll(u, 1, 0))
    return jnp.where(ri == s - 1, 0.0, pltpu.roll(u, s - 1, 0))


def _conv_fwd(proj, conv_w, name):
    s = proj.shape[0]
    nt = CONV_CH // LANES

    def body(cb_ref, cc_ref, cx_ref, w_ref, pre_ref):
        u = cc_ref[...] * cx_ref[...]
        conv = _shift_rows(u, s, True) * w_ref[0:1, :] + u * w_ref[1:2, :] + _shift_rows(u, s, False) * w_ref[2:3, :]
        pre_ref[...] = cb_ref[...] * conv

    def col(off):
        return pl.BlockSpec((s, LANES), lambda j: (0, off // LANES + j))

    return pl.pallas_call(body, grid=(nt,), in_specs=[col(OCB), col(OCC), col(OCX), pl.BlockSpec((3, LANES), lambda j: (0, j))],
                          out_specs=pl.BlockSpec((s, LANES), lambda j: (0, j)),
                          out_shape=jax.ShapeDtypeStruct((s, CONV_CH), F32), name=name,
                          compiler_params=_cp(("parallel",)))(proj, proj, proj, conv_w)


def _conv_bwd(proj, conv_w, d_pre, name):
    s = proj.shape[0]
    nt = CONV_CH // LANES

    def body(cb_ref, cc_ref, cx_ref, w_ref, dp_ref, dcb_ref, dcc_ref, dcx_ref, dw_ref):
        cc, cx = cc_ref[...], cx_ref[...]
        u = cc * cx
        up, dn = _shift_rows(u, s, True), _shift_rows(u, s, False)
        w0, w1, w2 = w_ref[0:1, :], w_ref[1:2, :], w_ref[2:3, :]
        conv = up * w0 + u * w1 + dn * w2
        dp = dp_ref[...]
        dcb_ref[...] = dp * conv
        dconv = dp * cb_ref[...]
        du = _shift_rows(dconv, s, False) * w0 + dconv * w1 + _shift_rows(dconv, s, True) * w2
        dcc_ref[...] = du * cx
        dcx_ref[...] = du * cc
        dw_ref[0:1, :] = jnp.sum(dconv * up, axis=0, keepdims=True)
        dw_ref[1:2, :] = jnp.sum(dconv * u, axis=0, keepdims=True)
        dw_ref[2:3, :] = jnp.sum(dconv * dn, axis=0, keepdims=True)

    def col(off):
        return pl.BlockSpec((s, LANES), lambda j: (0, off // LANES + j))

    blk = pl.BlockSpec((s, LANES), lambda j: (0, j))
    wblk = pl.BlockSpec((3, LANES), lambda j: (0, j))
    full = jax.ShapeDtypeStruct((s, CONV_CH), F32)
    return pl.pallas_call(body, grid=(nt,), in_specs=[col(OCB), col(OCC), col(OCX), wblk, blk],
                          out_specs=[blk, blk, blk, wblk],
                          out_shape=[full, full, full, jax.ShapeDtypeStruct((3, CONV_CH), F32)],
                          name=name, compiler_params=_cp(("parallel",)))(proj, proj, proj, conv_w, d_pre)


GLA_SUB = 4


def _gla_gates(t_ref, wg_ref, bg_ref):
    t = t_ref[...]
    a = _dot(t, wg_ref[...]) + bg_ref[...]
    la = (jnp.minimum(a, 0.0) - jnp.log(1.0 + jnp.exp(-jnp.abs(a)))) / GTEMP
    return t, a, la


def _gla_masks(reverse):
    ri = lax.broadcasted_iota(jnp.int32, (CHUNK, CHUNK), 0)
    ci = lax.broadcasted_iota(jnp.int32, (CHUNK, CHUNK), 1)
    if reverse:
        cum, mask = ci >= ri, ci > ri
    else:
        cum, mask = ci <= ri, ci <= ri
    return cum.astype(F32), mask


def _gla_specs(s, reverse):
    nsub = min(GLA_SUB, s // CHUNK)
    nsteps = s // (CHUNK * nsub)

    def row(n):
        return nsteps - 1 - n if reverse else n

    def chunk(pi):
        return nsub - 1 - pi if reverse else pi

    return nsub, nsteps, row, chunk


def _gla_fwd(proj, wg_pad, bg, reverse, name):
    s = proj.shape[0]
    nsub, nsteps, row, chunk = _gla_specs(s, reverse)
    rb = nsub * CHUNK

    def body(q_ref, k_ref, v_ref, t_ref, wg_ref, bg_ref, o_ref, st_ref, state):
        @pl.when(pl.program_id(0) == 0)
        def _():
            state[...] = jnp.zeros_like(state)

        _, _, la = _gla_gates(t_ref, wg_ref, bg_ref)
        cumf, mask = _gla_masks(reverse)
        lane = lax.broadcasted_iota(jnp.int32, (CHUNK, LANES), 1)
        for pi in range(nsub):
            rows = slice(chunk(pi) * CHUNK, (chunk(pi) + 1) * CHUNK)
            la_c = la[rows]
            b = _cum_dot(cumf, la_c)
            bl = jnp.sum(la_c, axis=0, keepdims=True)
            q = q_ref[rows, :] * (GDK ** -0.5)
            k = k_ref[rows, :]
            qd = q * jnp.exp(b)
            ki = k * jnp.exp(-b)
            kte = k * jnp.exp(bl - b)
            decay = jnp.exp(bl)
            for h in range(GH):
                p = h // 2
                sl = slice(p * LANES, (p + 1) * LANES)
                lm = (lane < GDK) if h % 2 == 0 else (lane >= GDK)
                qd_h = jnp.where(lm, qd[:, sl], 0.0)
                kte_h = jnp.where(lm, kte[:, sl], 0.0)
                v_h = v_ref[rows, h * GDV:(h + 1) * GDV]
                st = state[h]
                a_mat = jnp.where(mask, _dot_nt(qd_h, ki[:, sl]), 0.0)
                o_ref[rows, h * GDV:(h + 1) * GDV] = _dot(a_mat, v_h) + _dot_nt(qd_h, st)
                st_ref[pi, h] = st
                state[h] = st * decay[:, sl] + _dot_tn(v_h, kte_h)

    return pl.pallas_call(
        body, grid=(nsteps,),
        in_specs=[pl.BlockSpec((rb, GQK), lambda n: (row(n), OGQ // GQK)),
                  pl.BlockSpec((rb, GQK), lambda n: (row(n), OGK // GQK)),
                  pl.BlockSpec((rb, GW), lambda n: (row(n), OGV // GW)),
                  pl.BlockSpec((rb, LANES), lambda n: (row(n), OT // LANES)),
                  pl.BlockSpec((LANES, GQK), lambda n: (0, 0)),
                  pl.BlockSpec((1, GQK), lambda n: (0, 0))],
        out_specs=[pl.BlockSpec((rb, GW), lambda n: (row(n), 0)),
                   pl.BlockSpec((nsub, GH, GDV, LANES), lambda n: (n, 0, 0, 0))],
        out_shape=[jax.ShapeDtypeStruct((s, GW), F32), jax.ShapeDtypeStruct((s // CHUNK, GH, GDV, LANES), F32)],
        scratch_shapes=[pltpu.VMEM((GH, GDV, LANES), F32)],
        name=name, compiler_params=_cp(("arbitrary",)))(proj, proj, proj, proj, wg_pad, bg)


def _gla_bwd(proj, wg_pad, bg, states, d_o, reverse, name):
    s = proj.shape[0]
    nsub, nsteps, row, chunk = _gla_specs(s, reverse)
    rb = nsub * CHUNK

    def body(q_ref, k_ref, v_ref, t_ref, wg_ref, bg_ref, st_ref, do_ref,
             dq_ref, dk_ref, dv_ref, dt_ref, dwg_ref, dbg_ref, dstate, da_buf):
        @pl.when(pl.program_id(0) == 0)
        def _():
            dstate[...] = jnp.zeros_like(dstate)
            dwg_ref[...] = jnp.zeros_like(dwg_ref)
            dbg_ref[...] = jnp.zeros_like(dbg_ref)

        t, a, la = _gla_gates(t_ref, wg_ref, bg_ref)
        cumf, mask = _gla_masks(reverse)
        lane = lax.broadcasted_iota(jnp.int32, (CHUNK, LANES), 1)
        for pi in reversed(range(nsub)):
            rows = slice(chunk(pi) * CHUNK, (chunk(pi) + 1) * CHUNK)
            la_c = la[rows]
            b = _cum_dot(cumf, la_c)
            bl = jnp.sum(la_c, axis=0, keepdims=True)
            q = q_ref[rows, :] * (GDK ** -0.5)
            k = k_ref[rows, :]
            e, ei, ee = jnp.exp(b), jnp.exp(-b), jnp.exp(bl - b)
            qd, ki, kte = q * e, k * ei, k * ee
            decay = jnp.exp(bl)
            for p in range(GH // 2):
                sl = slice(p * LANES, (p + 1) * LANES)
                dqd = jnp.zeros((CHUNK, LANES), F32)
                dki = jnp.zeros((CHUNK, LANES), F32)
                dkte = jnp.zeros((CHUNK, LANES), F32)
                ddecay = jnp.zeros((1, LANES), F32)
                for half in range(2):
                    h = 2 * p + half
                    lm = (lane < GDK) if half == 0 else (lane >= GDK)
                    qd_h = jnp.where(lm, qd[:, sl], 0.0)
                    ki_h = jnp.where(lm, ki[:, sl], 0.0)
                    kte_h = jnp.where(lm, kte[:, sl], 0.0)
                    v_h = v_ref[rows, h * GDV:(h + 1) * GDV]
                    do_h = do_ref[rows, h * GDV:(h + 1) * GDV]
                    st = st_ref[pi, h]
                    dst = dstate[h]
                    a_mat = jnp.where(mask, _dot_nt(qd_h, ki_h), 0.0)
                    da_mat = jnp.where(mask, _dot_nt(do_h, v_h), 0.0)
                    dv_ref[rows, h * GDV:(h + 1) * GDV] = _dot_tn(a_mat, do_h) + _dot_nt(kte_h, dst)
                    dqd += _dot(da_mat, ki_h) + _dot(do_h, st)
                    dki += _dot_tn(da_mat, qd_h)
                    dkte += _dot(v_h, dst)
                    ddecay += jnp.sum(dst * st, axis=0, keepdims=True)
                    dstate[h] = dst * decay[:, sl] + _dot_tn(do_h, qd_h)
                dq_ref[rows, sl] = dqd * e[:, sl] * (GDK ** -0.5)
                dk_ref[rows, sl] = dki * ei[:, sl] + dkte * ee[:, sl]
                db = dqd * qd[:, sl] - dki * ki[:, sl] - dkte * kte[:, sl]
                dbl = jnp.sum(dkte * kte[:, sl], axis=0, keepdims=True) + decay[:, sl] * ddecay
                da_buf[rows, sl] = _cum_dot(cumf, db, True) + dbl
        da = da_buf[...] * (1.0 / GTEMP) * _sigmoid(-a)
        dt_ref[...] = _dot_nt(da, wg_ref[...])
        dwg_ref[...] += _dot_tn(t, da)
        dbg_ref[...] += jnp.sum(da, axis=0, keepdims=True)

    def prow(j):
        return row(nsteps - 1 - j)

    return pl.pallas_call(
        body, grid=(nsteps,),
        in_specs=[pl.BlockSpec((rb, GQK), lambda j: (prow(j), OGQ // GQK)),
                  pl.BlockSpec((rb, GQK), lambda j: (prow(j), OGK // GQK)),
                  pl.BlockSpec((rb, GW), lambda j: (prow(j), OGV // GW)),
                  pl.BlockSpec((rb, LANES), lambda j: (prow(j), OT // LANES)),
                  pl.BlockSpec((LANES, GQK), lambda j: (0, 0)),
                  pl.BlockSpec((1, GQK), lambda j: (0, 0)),
                  pl.BlockSpec((nsub, GH, GDV, LANES), lambda j: (nsteps - 1 - j, 0, 0, 0)),
                  pl.BlockSpec((rb, GW), lambda j: (prow(j), 0))],
        out_specs=[pl.BlockSpec((rb, GQK), lambda j: (prow(j), 0)),
                   pl.BlockSpec((rb, GQK), lambda j: (prow(j), 0)),
                   pl.BlockSpec((rb, GW), lambda j: (prow(j), 0)),
                   pl.BlockSpec((rb, LANES), lambda j: (prow(j), 0)),
                   pl.BlockSpec((LANES, GQK), lambda j: (0, 0)),
                   pl.BlockSpec((1, GQK), lambda j: (0, 0))],
        out_shape=[jax.ShapeDtypeStruct((s, GQK), F32), jax.ShapeDtypeStruct((s, GQK), F32),
                   jax.ShapeDtypeStruct((s, GW), F32), jax.ShapeDtypeStruct((s, LANES), F32),
                   jax.ShapeDtypeStruct((LANES, GQK), F32), jax.ShapeDtypeStruct((1, GQK), F32)],
        scratch_shapes=[pltpu.VMEM((GH, GDV, LANES), F32), pltpu.VMEM((rb, GQK), F32)],
        name=name, compiler_params=_cp(("arbitrary",)))(proj, proj, proj, proj, wg_pad, bg, states, d_o)


def _rot_half(x):
    lane = lax.broadcasted_iota(jnp.int32, x.shape, 1)
    first = (lane % MROPE) < (MROPE // 2)
    return jnp.where(first, -pltpu.roll(x, LANES - MROPE // 2, 1), pltpu.roll(x, MROPE // 2, 1))


def _mla_prep(proj, cos, sin, qg, kvg, w_uq, w_ukv, name):
    s = proj.shape[0]
    tr = _rows(s)

    def body(mq_ref, mkv_ref, t_ref, cos_ref, sin_ref, qg_ref, kvg_ref, wuq_ref, wukv_ref, q_ref, k_ref, v_ref):
        cosv, sinv = cos_ref[...], sin_ref[...]
        lane = lax.broadcasted_iota(jnp.int32, (tr, LANES), 1)

        def rope(xv):
            return xv * cosv + _rot_half(xv) * sinv

        qm = _dot(_rms(mq_ref[...], qg_ref[...]), wuq_ref[...])
        kv = _dot(_rms(mkv_ref[...], kvg_ref[...]), wukv_ref[...])
        kr_lo = jnp.where(lane < MROPE, rope(t_ref[...]), 0.0)
        kr_hi = pltpu.roll(kr_lo, MROPE, 1)
        for p in range(MH // 2):
            r = rope(qm[:, MW + p * LANES:MW + (p + 1) * LANES]).astype(q_ref.dtype)
            q_ref[2 * p, :, LANES:] = r
            q_ref[2 * p + 1, :, LANES:] = r
        for h in range(MH):
            q_ref[h, :, :LANES] = qm[:, h * LANES:(h + 1) * LANES].astype(q_ref.dtype)
            k_ref[h, :, :LANES] = kv[:, 2 * h * LANES:(2 * h + 1) * LANES].astype(k_ref.dtype)
            k_ref[h, :, LANES:] = (kr_lo if h % 2 == 0 else kr_hi).astype(k_ref.dtype)
            v_ref[h] = kv[:, (2 * h + 1) * LANES:(2 * h + 2) * LANES].astype(v_ref.dtype)

    def full(shape):
        return pl.BlockSpec(shape, lambda i: (0,) * len(shape))

    return pl.pallas_call(
        body, grid=(s // tr,),
        in_specs=[pl.BlockSpec((tr, MQL), lambda i: (i, OMQ // MQL)),
                  pl.BlockSpec((tr, MKVL), lambda i: (i, OMKV // MKVL)),
                  pl.BlockSpec((tr, LANES), lambda i: (i, OT // LANES)),
                  pl.BlockSpec((tr, LANES), lambda i: (i, 0)),
                  pl.BlockSpec((tr, LANES), lambda i: (i, 0)),
                  full((1, MQL)), full((1, MKVL)), full((MQL, MQW)), full((MKVL, MKVW))],
        out_specs=[pl.BlockSpec((MH, tr, 2 * LANES), lambda i: (0, i, 0)),
                   pl.BlockSpec((MH, tr, 2 * LANES), lambda i: (0, i, 0)),
                   pl.BlockSpec((MH, tr, LANES), lambda i: (0, i, 0))],
        out_shape=[jax.ShapeDtypeStruct((MH, s, 2 * LANES), MXU), jax.ShapeDtypeStruct((MH, s, 2 * LANES), MXU),
                   jax.ShapeDtypeStruct((MH, s, LANES), MXU)],
        name=name, compiler_params=_cp(("parallel",)))(proj, proj, proj, cos, sin, qg, kvg, w_uq, w_ukv)


def _mla_prep_bwd(proj, cos, sin, qg, kvg, w_uq, w_ukv, d_q, d_k, d_v, name):
    s = proj.shape[0]
    tr = _rows(s)

    def body(mq_ref, mkv_ref, cos_ref, sin_ref, qg_ref, kvg_ref, wuq_ref, wukv_ref, dq_ref, dk_ref, dv_ref,
             dmq_ref, dmkv_ref, dt_ref, dwuq_ref, dwukv_ref, dqg_ref, dkvg_ref):
        @pl.when(pl.program_id(0) == 0)
        def _():
            for r in (dwuq_ref, dwukv_ref, dqg_ref, dkvg_ref):
                r[...] = jnp.zeros_like(r)

        cosv, sinv = cos_ref[...], sin_ref[...]
        lane = lax.broadcasted_iota(jnp.int32, (tr, LANES), 1)
        lo = lane < MROPE

        def unrope(dv):
            return dv * cosv - _rot_half(dv * sinv)

        parts = [dq_ref[h, :, :LANES] for h in range(MH)]
        for p in range(MH // 2):
            parts.append(unrope(jnp.where(lo, dq_ref[2 * p, :, LANES:], dq_ref[2 * p + 1, :, LANES:])))
        d_qm = jnp.concatenate(parts, axis=1)
        mq, qgv = mq_ref[...], qg_ref[...]
        cq = _rms(mq, qgv)
        dwuq_ref[...] += _dot_tn(cq, d_qm)
        dmq, dqg = _rms_bwd(_dot_nt(d_qm, wuq_ref[...]), mq, qgv)
        dmq_ref[...] = dmq
        dqg_ref[...] += dqg

        parts = []
        for h in range(MH):
            parts += [dk_ref[h, :, :LANES], dv_ref[h]]
        d_kv = jnp.concatenate(parts, axis=1)
        mkv, kvgv = mkv_ref[...], kvg_ref[...]
        ckv = _rms(mkv, kvgv)
        dwukv_ref[...] += _dot_tn(ckv, d_kv)
        dmkv, dkvg = _rms_bwd(_dot_nt(d_kv, wukv_ref[...]), mkv, kvgv)
        dmkv_ref[...] = dmkv
        dkvg_ref[...] += dkvg

        even = dk_ref[0, :, LANES:] + dk_ref[2, :, LANES:] + dk_ref[4, :, LANES:]
        odd = dk_ref[1, :, LANES:] + dk_ref[3, :, LANES:] + dk_ref[5, :, LANES:]
        d_kr = jnp.where(lo, even, 0.0) + pltpu.roll(jnp.where(lo, 0.0, odd), MROPE, 1)
        dt_ref[...] = jnp.where(lo, unrope(d_kr), 0.0)

    def full(shape):
        return pl.BlockSpec(shape, lambda i: (0,) * len(shape))

    return pl.pallas_call(
        body, grid=(s // tr,),
        in_specs=[pl.BlockSpec((tr, MQL), lambda i: (i, OMQ // MQL)),
                  pl.BlockSpec((tr, MKVL), lambda i: (i, OMKV // MKVL)),
                  pl.BlockSpec((tr, LANES), lambda i: (i, 0)),
                  pl.BlockSpec((tr, LANES), lambda i: (i, 0)),
                  full((1, MQL)), full((1, MKVL)), full((MQL, MQW)), full((MKVL, MKVW)),
                  pl.BlockSpec((MH, tr, 2 * LANES), lambda i: (0, i, 0)),
                  pl.BlockSpec((MH, tr, 2 * LANES), lambda i: (0, i, 0)),
                  pl.BlockSpec((MH, tr, LANES), lambda i: (0, i, 0))],
        out_specs=[pl.BlockSpec((tr, MQL), lambda i: (i, 0)), pl.BlockSpec((tr, MKVL), lambda i: (i, 0)),
                   pl.BlockSpec((tr, LANES), lambda i: (i, 0)),
                   full((MQL, MQW)), full((MKVL, MKVW)), full((1, MQL)), full((1, MKVL))],
        out_shape=[jax.ShapeDtypeStruct((s, MQL), F32), jax.ShapeDtypeStruct((s, MKVL), F32),
                   jax.ShapeDtypeStruct((s, LANES), F32),
                   jax.ShapeDtypeStruct((MQL, MQW), F32), jax.ShapeDtypeStruct((MKVL, MKVW), F32),
                   jax.ShapeDtypeStruct((1, MQL), F32), jax.ShapeDtypeStruct((1, MKVL), F32)],
        name=name, compiler_params=_cp(("arbitrary",)))(proj, proj, cos, sin, qg, kvg, w_uq, w_ukv, d_q, d_k, d_v)


ATT_SCALE = (MNOPE + MROPE) ** -0.5


def _attn_fwd(q, k, v, name):
    s = q.shape[1]
    tq = _rows(s)

    def body(q_ref, k_ref, v_ref, o_ref, lse_ref):
        sc = _dot_nt(q_ref[0], k_ref[0]) * ATT_SCALE
        m = jnp.max(sc, axis=-1, keepdims=True)
        p = jnp.exp(sc - m)
        l = jnp.sum(p, axis=-1, keepdims=True)
        o_ref[...] = _dot(p, v_ref[0]) / l
        lse_ref[0] = m + jnp.log(l)

    return pl.pallas_call(
        body, grid=(MH, s // tq),
        in_specs=[pl.BlockSpec((1, tq, 2 * LANES), lambda h, i: (h, i, 0)),
                  pl.BlockSpec((1, s, 2 * LANES), lambda h, i: (h, 0, 0)),
                  pl.BlockSpec((1, s, LANES), lambda h, i: (h, 0, 0))],
        out_specs=[pl.BlockSpec((tq, LANES), lambda h, i: (i, h)),
                   pl.BlockSpec((1, tq, 1), lambda h, i: (h, i, 0))],
        out_shape=[jax.ShapeDtypeStruct((s, MW), F32), jax.ShapeDtypeStruct((MH, s, 1), F32)],
        name=name, compiler_params=_cp(("parallel", "parallel")))(q, k, v)


def _attn_bwd(q, k, v, o, lse, d_o, name):
    s = q.shape[1]
    tq = _rows(s)

    def body(q_ref, k_ref, v_ref, o_ref, lse_ref, do_ref, dq_ref, dk_ref, dv_ref):
        @pl.when(pl.program_id(1) == 0)
        def _():
            dk_ref[...] = jnp.zeros_like(dk_ref)
            dv_ref[...] = jnp.zeros_like(dv_ref)

        qv, kv, do = q_ref[0], k_ref[0], do_ref[...]
        p = jnp.exp(_dot_nt(qv, kv) * ATT_SCALE - lse_ref[0])
        delta = jnp.sum(do * o_ref[...], axis=-1, keepdims=True)
        ds = p * (_dot_nt(do, v_ref[0]) - delta) * ATT_SCALE
        dq_ref[0] = _dot(ds, kv)
        dk_ref[0] += _dot_tn(ds, qv)
        dv_ref[0] += _dot_tn(p, do)

    return pl.pallas_call(
        body, grid=(MH, s // tq),
        in_specs=[pl.BlockSpec((1, tq, 2 * LANES), lambda h, i: (h, i, 0)),
                  pl.BlockSpec((1, s, 2 * LANES), lambda h, i: (h, 0, 0)),
                  pl.BlockSpec((1, s, LANES), lambda h, i: (h, 0, 0)),
                  pl.BlockSpec((tq, LANES), lambda h, i: (i, h)),
                  pl.BlockSpec((1, tq, 1), lambda h, i: (h, i, 0)),
                  pl.BlockSpec((tq, LANES), lambda h, i: (i, h))],
        out_specs=[pl.BlockSpec((1, tq, 2 * LANES), lambda h, i: (h, i, 0)),
                   pl.BlockSpec((1, s, 2 * LANES), lambda h, i: (h, 0, 0)),
                   pl.BlockSpec((1, s, LANES), lambda h, i: (h, 0, 0))],
        out_shape=[jax.ShapeDtypeStruct((MH, s, 2 * LANES), F32), jax.ShapeDtypeStruct((MH, s, 2 * LANES), F32),
                   jax.ShapeDtypeStruct((MH, s, LANES), F32)],
        name=name, compiler_params=_cp(("parallel", "arbitrary")))(q, k, v, o, lse, d_o)


def _merge_fwd(o_f, o_b, o_att, pre, proj, gng, mog, cog, name):
    s = proj.shape[0]
    tr = _rows(s)

    def body(of_ref, ob_ref, oa_ref, pre_ref, z_ref, gng_ref, mog_ref, cog_ref, y_ref):
        z = z_ref[...]
        sz = z * _sigmoid(z)
        osum = of_ref[...] + ob_ref[...]
        gg = gng_ref[...]
        for h in range(GH):
            sl = slice(h * GDV, (h + 1) * GDV)
            y_ref[:, sl] = (_rms(osum[:, sl], gg) * sz[:, sl]).astype(y_ref.dtype)
        y_ref[:, GW:GW + MW] = (_rms(oa_ref[...], mog_ref[...]) * sz[:, GW:GW + MW]).astype(y_ref.dtype)
        y_ref[:, GW + MW:] = (_rms(pre_ref[...], cog_ref[...]) * sz[:, GW + MW:]).astype(y_ref.dtype)

    def row(w):
        return pl.BlockSpec((tr, w), lambda i: (i, 0))

    def vec(w):
        return pl.BlockSpec((1, w), lambda i: (0, 0))

    return pl.pallas_call(
        body, grid=(s // tr,),
        in_specs=[row(GW), row(GW), row(MW), row(CONV_CH), row(D_MIX), vec(GDV), vec(MW), vec(CONV_CH)],
        out_specs=row(D_MIX), out_shape=jax.ShapeDtypeStruct((s, D_MIX), MXU),
        name=name, compiler_params=_cp(("parallel",)))(o_f, o_b, o_att, pre, proj, gng, mog, cog)


def _merge_bwd(d_y, o_f, o_b, o_att, pre, proj, gng, mog, cog, name):
    s = proj.shape[0]
    tr = _rows(s)

    def body(dy_ref, of_ref, ob_ref, oa_ref, pre_ref, z_ref, gng_ref, mog_ref, cog_ref,
             dz_ref, dos_ref, doa_ref, dpre_ref, dgng_ref, dmog_ref, dcog_ref):
        @pl.when(pl.program_id(0) == 0)
        def _():
            for r in (dgng_ref, dmog_ref, dcog_ref):
                r[...] = jnp.zeros_like(r)

        z, dy = z_ref[...], dy_ref[...]
        sg = _sigmoid(z)
        sz = z * sg
        dsz = sg * (1.0 + z * (1.0 - sg))
        dcat = dy * sz
        dyz = dy * dsz
        osum = of_ref[...] + ob_ref[...]
        gg = gng_ref[...]
        dgg = jnp.zeros_like(gg)
        for h in range(GH):
            sl = slice(h * GDV, (h + 1) * GDV)
            dz_ref[:, sl] = dyz[:, sl] * _rms(osum[:, sl], gg)
            dx, dg = _rms_bwd(dcat[:, sl], osum[:, sl], gg)
            dos_ref[:, sl] = dx
            dgg += dg
        dgng_ref[...] += dgg
        sl = slice(GW, GW + MW)
        oa, mg = oa_ref[...], mog_ref[...]
        dz_ref[:, sl] = dyz[:, sl] * _rms(oa, mg)
        dx, dg = _rms_bwd(dcat[:, sl], oa, mg)
        doa_ref[...] = dx
        dmog_ref[...] += dg
        sl = slice(GW + MW, D_MIX)
        pv, cg = pre_ref[...], cog_ref[...]
        dz_ref[:, sl] = dyz[:, sl] * _rms(pv, cg)
        dx, dg = _rms_bwd(dcat[:, sl], pv, cg)
        dpre_ref[...] = dx
        dcog_ref[...] += dg

    def row(w):
        return pl.BlockSpec((tr, w), lambda i: (i, 0))

    def vec(w):
        return pl.BlockSpec((1, w), lambda i: (0, 0))

    def rs(w):
        return jax.ShapeDtypeStruct((s, w), F32)

    def vs(w):
        return jax.ShapeDtypeStruct((1, w), F32)

    return pl.pallas_call(
        body, grid=(s // tr,),
        in_specs=[row(D_MIX), row(GW), row(GW), row(MW), row(CONV_CH), row(D_MIX), vec(GDV), vec(MW), vec(CONV_CH)],
        out_specs=[row(D_MIX), row(GW), row(MW), row(CONV_CH), vec(GDV), vec(MW), vec(CONV_CH)],
        out_shape=[rs(D_MIX), rs(GW), rs(MW), rs(CONV_CH), vs(GDV), vs(MW), vs(CONV_CH)],
        name=name, compiler_params=_cp(("arbitrary",)))(d_y, o_f, o_b, o_att, pre, proj, gng, mog, cog)


def _assemble_dproj(d_z, d_cb, d_cc, d_cx, d_mkv, dv_f, dv_b, dq_f, dq_b, dk_f, dk_b, d_mq, dt_m, dt_f, dt_b, name):
    s = d_z.shape[0]
    tr = _rows(s)

    def body(dz, dcb, dcc, dcx, dmkv, dvf, dvb, dqf, dqb, dkf, dkb, dmq, dtm, dtf, dtb, out):
        dt = out.dtype
        out[:, OZ:OZ + D_MIX] = dz[...].astype(dt)
        out[:, OCB:OCB + CONV_CH] = dcb[...].astype(dt)
        out[:, OCC:OCC + CONV_CH] = dcc[...].astype(dt)
        out[:, OCX:OCX + CONV_CH] = dcx[...].astype(dt)
        out[:, OMKV:OMKV + MKVL] = dmkv[...].astype(dt)
        out[:, OGV:OGV + GW] = (dvf[...] + dvb[...]).astype(dt)
        out[:, OGQ:OGQ + GQK] = (dqf[...] + dqb[...]).astype(dt)
        out[:, OGK:OGK + GQK] = (dkf[...] + dkb[...]).astype(dt)
        out[:, OMQ:OMQ + MQL] = dmq[...].astype(dt)
        out[:, OT:OT + LANES] = (dtm[...] + dtf[...] + dtb[...]).astype(dt)

    args = (d_z, d_cb, d_cc, d_cx, d_mkv, dv_f, dv_b, dq_f, dq_b, dk_f, dk_b, d_mq, dt_m, dt_f, dt_b)
    return pl.pallas_call(
        body, grid=(s // tr,),
        in_specs=[pl.BlockSpec((tr, a.shape[1]), lambda i: (i, 0)) for a in args],
        out_specs=pl.BlockSpec((tr, PW), lambda i: (i, 0)),
        out_shape=jax.ShapeDtypeStruct((s, PW), MXU), name=name, compiler_params=_cp(("parallel",)))(*args)


def _layer_fwd(x, mod, wt, cos, sin, tag, late=None):
    shift, scale, gate = mod
    h = _norm_mod(x, wt["norm_g"], scale, shift, f"norm_mod_{tag}")
    (proj,) = _matmul(h, wt["w_in"], dims="nn", tm=1024, tn=256, tk=2048, out_dtypes=(F32,), name=f"in_proj_{tag}")
    if late is not None:
        wt.update(late(proj))
    o_f, st_f = _gla_fwd(proj, wt["wg_pad_f"], wt["bg_f"], False, f"gla_fwd_f_{tag}")
    o_b, st_b = _gla_fwd(proj, wt["wg_pad_b"], wt["bg_b"], True, f"gla_fwd_b_{tag}")
    q, k, v = _mla_prep(proj, cos, sin, wt["q_norm_g"], wt["kv_norm_g"], wt["w_uq"], wt["w_ukv"], f"mla_prep_{tag}")
    o_att, lse = _attn_fwd(q, k, v, f"attn_fwd_{tag}")
    pre = _conv_fwd(proj, wt["conv_w"], f"conv_fwd_{tag}")
    y = _merge_fwd(o_f, o_b, o_att, pre, proj, wt["gla_norm_g"], wt["mla_out_g"], wt["conv_out_g"], f"merge_fwd_{tag}")
    x_new, u = _matmul(y, wt["w_out"], dims="nn", tm=1024, tn=512, tk=2048, out_dtypes=(F32, F32),
                       name=f"out_proj_{tag}", epilogue=lambda acc, xv, gv: (xv + gv * acc, acc),
                       extras=(x, gate), extra_kinds=("mn", "n"))
    saved = dict(x=x, h=h, proj=proj, o_f=o_f, o_b=o_b, st_f=st_f, st_b=st_b, q=q, k=k, v=v,
                 o_att=o_att, lse=lse, pre=pre, y=y, u=u)
    return x_new, saved


def _layer_bwd(d_out, sv, mod, wt, cos, sin, tag, ship=None, dx_first=None):
    shift, scale, gate = mod
    proj = sv["proj"]
    d_u, d_gate = _gate_bwd(d_out, sv["u"], gate, f"gate_bwd_{tag}")
    (g_w_out,) = _matmul(sv["y"], d_u, dims="tn", tm=1024, tn=512, tk=2048, out_dtypes=(MXU,), name=f"out_proj_dw_{tag}")
    (d_y,) = _matmul(d_u, wt["w_out"], dims="nt", tm=1024, tn=512, tk=2048, out_dtypes=(F32,), name=f"out_proj_dx_{tag}",
                     after=(g_w_out,))
    d_z, d_osum, d_oatt, d_pre, d_gng, d_mog, d_cog = _merge_bwd(
        d_y, sv["o_f"], sv["o_b"], sv["o_att"], sv["pre"], proj, wt["gla_norm_g"], wt["mla_out_g"], wt["conv_out_g"],
        f"merge_bwd_{tag}")
    d_cb, d_cc, d_cx, d_conv_w = _conv_bwd(proj, wt["conv_w"], d_pre, f"conv_bwd_{tag}")
    d_q, d_k, d_v = _attn_bwd(sv["q"], sv["k"], sv["v"], sv["o_att"], sv["lse"], d_oatt, f"attn_bwd_{tag}")
    d_mq, d_mkv, dt_m, g_w_uq, g_w_ukv, d_qg, d_kvg = _mla_prep_bwd(
        proj, cos, sin, wt["q_norm_g"], wt["kv_norm_g"], wt["w_uq"], wt["w_ukv"], d_q, d_k, d_v, f"mla_prep_bwd_{tag}")
    dq_f, dk_f, dv_f, dt_f, d_wg_f, d_bg_f = _gla_bwd(proj, wt["wg_pad_f"], wt["bg_f"], sv["st_f"], d_osum, False,
                                                     f"gla_bwd_f_{tag}")
    dq_b, dk_b, dv_b, dt_b, d_wg_b, d_bg_b = _gla_bwd(proj, wt["wg_pad_b"], wt["bg_b"], sv["st_b"], d_osum, True,
                                                     f"gla_bwd_b_{tag}")
    d_proj = _assemble_dproj(d_z, d_cb, d_cc, d_cx, d_mkv, dv_f, dv_b, dq_f, dq_b, dk_f, dk_b, d_mq, dt_m, dt_f, dt_b,
                             f"assemble_dproj_{tag}")
    grads = dict(w_out=g_w_out, w_uq=g_w_uq, w_ukv=g_w_ukv,
                 wg_pad_f=d_wg_f, bg_f=d_bg_f, wg_pad_b=d_wg_b, bg_b=d_bg_b, gla_norm_g=d_gng,
                 q_norm_g=d_qg, kv_norm_g=d_kvg, mla_out_g=d_mog, conv_w=d_conv_w, conv_out_g=d_cog)

    def in_dw(after):
        (g_w_in,) = _matmul(sv["h"], d_proj, dims="tn", tm=1024, tn=256, tk=2048, out_dtypes=(MXU,),
                            name=f"in_proj_dw_{tag}", after=after)
        grads["w_in"] = g_w_in
        return dict(w_in=g_w_in, w_out=g_w_out, w_uq=g_w_uq, w_ukv=g_w_ukv)

    def in_dx(after):
        (d_h,) = _matmul(d_proj, wt["w_in"], dims="nt", tm=1024, tn=512, tk=PW // 2, out_dtypes=(F32,),
                         name=f"in_proj_dx_{tag}", after=after)
        d_x, d_shift, d_scale, d_ng = _norm_mod_bwd(d_h, sv["x"], d_out, wt["norm_g"], scale, f"norm_mod_bwd_{tag}")
        grads["norm_g"] = d_ng
        return d_x, (d_shift, d_scale, d_gate)

    if dx_first is None:
        big = in_dw(())
        d_x, d_mod = in_dx((big["w_in"],) if ship is None else ship(big))
    else:
        d_x, d_mod = in_dx(())
        big = in_dw(dx_first(d_x, d_mod, grads))
        ship(big)
    return d_x, d_mod, grads


def _perm_in_cols(w):
    pad = jnp.zeros(w.shape[:-1] + (PW - IN_DIM,), w.dtype)
    return jnp.concatenate([w[..., 3808:5856], w[..., 2272:3808], w[..., 1952:2208], w[..., 768:1536], w[..., 0:768],
                            w[..., 1568:1952], w[..., 2208:2272], w[..., 1536:1568], pad], axis=-1)


def _unperm_in_cols(g):
    return jnp.concatenate([g[..., OGQ:OGQ + 2 * GQK], g[..., OGV:OGV + GW], g[..., OT + MROPE:OT + MROPE + 2 * GRANK],
                            g[..., OMQ:OMQ + MQL], g[..., OMKV:OMKV + MKVL], g[..., OT:OT + MROPE],
                            g[..., OCB:OCB + 3 * CONV_CH], g[..., OZ:OZ + D_MIX]], axis=-1)


IN_SEGS = ((3808, 5856), (2272, 3808), (1952, 2208), (768, 1536), (0, 768), (1568, 1952), (2208, 2272), (1536, 1568))
UQ_SEGS = (tuple((h * (MNOPE + MROPE), h * (MNOPE + MROPE) + MNOPE) for h in range(MH))
           + tuple((h * (MNOPE + MROPE) + MNOPE, (h + 1) * (MNOPE + MROPE)) for h in range(MH)))


def _perm_gathered(g, segs, width):
    per = g.shape[-1]
    parts, total = [], 0
    for a, b in segs:
        c = a
        while c < b:
            j = c // per
            hi = min(b, (j + 1) * per)
            parts.append(g[j, :, c - j * per:hi - j * per])
            c = hi
        total += b - a
    if width > total:
        parts.append(jnp.zeros((g.shape[1], width - total), g.dtype))
    return jnp.concatenate(parts, axis=1)


def _scatter_perm(gp, segs, per):
    offs, o = [], 0
    for a, b in segs:
        offs.append((a, b, o))
        o += b - a
    blocks = []
    for j in range(N_DEV):
        lo, hi = j * per, (j + 1) * per
        pieces = []
        for a, b, o in sorted(offs):
            s0, s1 = max(a, lo), min(b, hi)
            if s0 < s1:
                pieces.append(gp[:, o + s0 - a:o + s1 - a])
        blocks.append(jnp.concatenate(pieces, axis=1))
    return jnp.stack(blocks)


def _perm_uq_cols(w):
    w3 = w.reshape(w.shape[:-1] + (MH, MNOPE + MROPE))
    return jnp.concatenate([w3[..., :MNOPE].reshape(w.shape[:-1] + (MH * MNOPE,)),
                            w3[..., MNOPE:].reshape(w.shape[:-1] + (MH * MROPE,))], axis=-1)


def _unperm_uq_cols(g):
    nope = g[..., :MH * MNOPE].reshape(g.shape[:-1] + (MH, MNOPE))
    rope = g[..., MH * MNOPE:].reshape(g.shape[:-1] + (MH, MROPE))
    return jnp.concatenate([nope, rope], axis=-1).reshape(g.shape[:-1] + (MQW,))


def _prep_layer_weights(w_in, w_out, w_uq, w_ukv, small):
    def vec(v):
        return v.reshape(1, -1).astype(F32)

    zeros = functools.partial(jnp.zeros, dtype=F32)
    wg_f, wg_b = small["gla_wg_f"].astype(F32), small["gla_wg_b"].astype(F32)
    wg_pad_f = jnp.concatenate([zeros((MROPE, GQK)), wg_f, zeros((LANES - MROPE - GRANK, GQK))], axis=0)
    wg_pad_b = jnp.concatenate([zeros((MROPE + GRANK, GQK)), wg_b, zeros((LANES - MROPE - 2 * GRANK, GQK))], axis=0)
    wt = dict(norm_g=vec(small["norm_g"]), wg_pad_f=wg_pad_f, wg_pad_b=wg_pad_b,
              bg_f=vec(small["gla_bg_f"]), bg_b=vec(small["gla_bg_b"]), gla_norm_g=vec(small["gla_norm_g"]),
              q_norm_g=vec(small["mla_q_norm_g"]), kv_norm_g=vec(small["mla_kv_norm_g"]),
              mla_out_g=vec(small["mla_out_g"]), conv_w=small["conv_w"].astype(F32),
              conv_out_g=vec(small["conv_out_g"]))
    for name, w in (("w_in", w_in), ("w_out", w_out), ("w_uq", w_uq), ("w_ukv", w_ukv)):
        if w is not None:
            wt[name] = w.astype(MXU)
    return wt


def _natural_small(gr):
    return dict(norm_g=gr["norm_g"][0],
                gla_wg_f=gr["wg_pad_f"][MROPE:MROPE + GRANK], gla_bg_f=gr["bg_f"][0],
                gla_wg_b=gr["wg_pad_b"][MROPE + GRANK:MROPE + 2 * GRANK], gla_bg_b=gr["bg_b"][0],
                gla_norm_g=gr["gla_norm_g"][0], mla_q_norm_g=gr["q_norm_g"][0], mla_kv_norm_g=gr["kv_norm_g"][0],
                mla_out_g=gr["mla_out_g"][0], conv_w=gr["conv_w"], conv_out_g=gr["conv_out_g"][0])


def _natural_grads(gr):
    return dict(_natural_small(gr), w_in=_unperm_in_cols(gr["w_in"]), w_out=gr["w_out"],
                mla_w_uq=_unperm_uq_cols(gr["w_uq"]), mla_w_ukv=gr["w_ukv"])


def _exchange(arrs, name, scatter, space):
    n = len(arrs)

    def body(*refs):
        ins, outs = refs[:n], refs[n:2 * n]
        send_sems, recv_sems, loc_sems = refs[2 * n:]
        ax, ay, ac = lax.axis_index("x"), lax.axis_index("y"), lax.axis_index("c")
        me = 4 * ax + 2 * ay + ac

        def src(a, to):
            return ins[a].at[to] if scatter else ins[a]

        def remote(a, r, dst_slot):
            px = 1 - ax if r & 4 else ax
            py = 1 - ay if r & 2 else ay
            pc = 1 - ac if r & 1 else ac
            return pltpu.make_async_remote_copy(
                src_ref=src(a, 4 * px + 2 * py + pc), dst_ref=outs[a].at[dst_slot(4 * px + 2 * py + pc)],
                send_sem=send_sems.at[a, r - 1], recv_sem=recv_sems.at[a, r - 1],
                device_id=(px, py, pc), device_id_type=MESH)

        locs = [pltpu.make_async_copy(src(a, me), outs[a].at[me], loc_sems.at[a]) for a in range(n)]
        for cp in locs:
            cp.start()
        sends = [remote(a, r, lambda peer: me) for r in range(1, N_DEV) for a in range(n)]
        for cp in sends:
            cp.start()
        for r in range(1, N_DEV):
            for a in range(n):
                remote(a, r, lambda peer: peer).wait_recv()
        for cp in sends:
            cp.wait_send()
        for cp in locs:
            cp.wait()

    def out_shape(a):
        return jax.ShapeDtypeStruct(a.shape if scatter else (N_DEV,) + a.shape, a.dtype)

    spec = pl.BlockSpec(memory_space=space)
    return pl.pallas_call(
        body, in_specs=[spec] * n, out_specs=[spec] * n, out_shape=[out_shape(a) for a in arrs],
        scratch_shapes=[pltpu.SemaphoreType.DMA((n, N_DEV - 1)), pltpu.SemaphoreType.DMA((n, N_DEV - 1)),
                        pltpu.SemaphoreType.DMA((n,))],
        name=name, compiler_params=pltpu.CompilerParams(vmem_limit_bytes=VMEM_LIMIT))(*arrs)


def _peer(r):
    ax, ay, ac = lax.axis_index("x"), lax.axis_index("y"), lax.axis_index("c")
    px = 1 - ax if r & 4 else ax
    py = 1 - ay if r & 2 else ay
    pc = 1 - ac if r & 1 else ac
    return (px, py, pc), 4 * px + 2 * py + pc


def _slot(rel_div):
    rel, div = rel_div
    idx = _peer(rel)[1]
    return idx if div == 1 else idx // div


AG_SPREAD = tuple((r, None, (0, 1), (r, 1)) for r in (1, 2, 4, 6))
AG_FORWARD = tuple((1, (k, 1), (k, 1), (1 ^ k, 1)) for k in (2, 4, 6))
RS_PAIR = tuple((1, (1 ^ k, 1), (1 ^ k, 2), (k, 2)) for k in (0, 2, 4, 6))
RS_CHIPS = tuple((r, (r, 2), (0, 2), (r, 2)) for r in (2, 4, 6))


def _plan_copies(plan, n, src_refs, land_refs, send_sems, recv_sems, arriving):
    out = []
    for i, (r, src, dst, recv) in enumerate(plan):
        peer = _peer(r)[0]
        for a in range(n):
            out.append(pltpu.make_async_remote_copy(
                src_ref=src_refs[a] if src is None else src_refs[a].at[_slot(src)],
                dst_ref=land_refs[a].at[_slot(recv if arriving else dst)],
                send_sem=send_sems.at[i * n + a], recv_sem=recv_sems.at[i * n + a],
                device_id=peer, device_id_type=MESH))
    return out


def _exchange_hbm(plan, srcs, lands, name, after=()):
    n = len(lands)
    fresh = isinstance(lands[0], jax.ShapeDtypeStruct)
    ins = ([] if srcs is None else list(srcs)) + ([] if fresh else list(lands))
    ns = 0 if srcs is None else n
    n_data = len(ins)
    ins = ins + list(after)

    def body(*refs):
        outs = refs[len(ins):len(ins) + n]
        send_sems, recv_sems = refs[-2:]
        src_refs = refs[:n] if srcs is not None else refs[ns:ns + n]
        sends = _plan_copies(plan, n, src_refs, outs, send_sems, recv_sems, False)
        for cp in sends:
            cp.start()
        for cp in _plan_copies(plan, n, src_refs, outs, send_sems, recv_sems, True):
            cp.wait_recv()
        for cp in sends:
            cp.wait_send()

    hbm = pl.BlockSpec(memory_space=pltpu.HBM)
    k = len(plan) * n
    return pl.pallas_call(
        body, name=name, in_specs=[hbm] * n_data + [pl.BlockSpec(memory_space=pl.ANY)] * len(after), out_specs=[hbm] * n,
        out_shape=[jax.ShapeDtypeStruct(a.shape, a.dtype) for a in lands],
        scratch_shapes=[pltpu.SemaphoreType.DMA((k,)), pltpu.SemaphoreType.DMA((k,))],
        input_output_aliases={} if fresh else {ns + i: i for i in range(n)},
        compiler_params=pltpu.CompilerParams(vmem_limit_bytes=VMEM_LIMIT))(*ins)


def _plan_start(plan, srcs, land_shapes, after, name):
    n = len(srcs)

    def body(*refs):
        src_refs, land_refs = refs[:n], refs[n:2 * n]
        send_sems, recv_sems = refs[2 * n + 1], refs[2 * n + 2]
        for cp in _plan_copies(plan, n, src_refs, land_refs, send_sems, recv_sems, False):
            cp.start()
        refs[-1][...] = jnp.zeros_like(refs[-1])

    hbm = pl.BlockSpec(memory_space=pltpu.HBM)
    sem = pl.BlockSpec(memory_space=pltpu.SEMAPHORE)
    k = len(plan) * n
    srcs = [pltpu.with_memory_space_constraint(a, pltpu.HBM) for a in srcs]
    lands = [pltpu.with_memory_space_constraint(lax.empty(shp, a.dtype), pltpu.HBM) for shp, a in zip(land_shapes, srcs)]
    res = pl.pallas_call(
        body, name=name,
        in_specs=[hbm] * (2 * n) + [pl.BlockSpec(memory_space=pl.ANY)],
        out_specs=[sem, sem] + [hbm] * (2 * n) + [pl.BlockSpec(memory_space=pltpu.VMEM)],
        out_shape=[pltpu.SemaphoreType.DMA((k,)), pltpu.SemaphoreType.DMA((k,))]
        + [pltpu.HBM(a.shape, a.dtype) for a in srcs] + [pltpu.HBM(shp, a.dtype) for shp, a in zip(land_shapes, srcs)]
        + [jax.ShapeDtypeStruct((8, LANES), F32)],
        input_output_aliases={i: 2 + i for i in range(2 * n)},
        compiler_params=pltpu.CompilerParams(has_side_effects=pltpu.SideEffectType.DATAFLOW_SIDE_EFFECTING),
    )(*srcs, *lands, after)
    return res[0], res[1], list(res[2:2 + n]), list(res[2 + n:2 + 2 * n]), res[-1]


def _plan_wait(plan, handle, after, name):
    send_sems, recv_sems, srcs, lands, _ = handle
    n = len(srcs)
    after = list(after)

    def body(*refs):
        src_refs, land_refs = refs[:n], refs[n:2 * n]
        ssem, rsem = refs[2 * n], refs[2 * n + 1]
        for cp in _plan_copies(plan, n, src_refs, land_refs, ssem, rsem, False):
            cp.wait_send()
        for cp in _plan_copies(plan, n, src_refs, land_refs, ssem, rsem, True):
            cp.wait_recv()

    hbm = pl.BlockSpec(memory_space=pltpu.HBM)
    sem = pl.BlockSpec(memory_space=pltpu.SEMAPHORE)
    res = pl.pallas_call(
        body, name=name,
        in_specs=[hbm] * (2 * n) + [sem, sem] + [pl.BlockSpec(memory_space=pl.ANY)] * len(after),
        out_specs=[hbm] * (2 * n),
        out_shape=[pltpu.HBM(a.shape, a.dtype) for a in srcs] + [pltpu.HBM(a.shape, a.dtype) for a in lands],
        input_output_aliases={i: i for i in range(2 * n)},
        compiler_params=pltpu.CompilerParams(has_side_effects=pltpu.SideEffectType.DATAFLOW_SIDE_EFFECTING),
    )(*srcs, *lands, send_sems, recv_sems, *after)
    return list(res[n:])


def _pair_sum(send, got, core, name):
    _, r, c = send.shape
    tr = 256 if r % 256 == 0 else r

    def body(core_ref, s_ref, g_ref, o_ref):
        o_ref[0] = (s_ref[0].astype(F32) + g_ref[0].astype(F32)).astype(o_ref.dtype)

    return pl.pallas_call(
        body, name=name,
        grid_spec=pltpu.PrefetchScalarGridSpec(
            num_scalar_prefetch=1, grid=(N_DEV // 2, r // tr),
            in_specs=[pl.BlockSpec((1, tr, c), lambda kc, i, core_ref: (2 * kc + core_ref[0], i, 0)),
                      pl.BlockSpec((1, tr, c), lambda kc, i, core_ref: (kc, i, 0))],
            out_specs=pl.BlockSpec((1, tr, c), lambda kc, i, core_ref: (kc, i, 0))),
        out_shape=jax.ShapeDtypeStruct((N_DEV // 2, r, c), send.dtype),
        compiler_params=_cp(("parallel", "parallel")))(core, send, got)


def _ada_mod(c_all, ada_w, ada_b_cols, name):
    nl, d, wc = ada_w.shape

    def body(c_ref, w_ref, b_ref, ca_ref, mod_ref):
        cv = c_ref[...]
        ca = cv * _sigmoid(cv)
        ca_ref[...] = ca
        mod_ref[0] = _dotf(ca, w_ref[0]) + b_ref[0]

    return pl.pallas_call(
        body, grid=(nl,),
        in_specs=[pl.BlockSpec((N_DEV, d), lambda l: (0, 0)), pl.BlockSpec((1, d, wc), lambda l: (l, 0, 0)),
                  pl.BlockSpec((1, 1, wc), lambda l: (l, 0, 0))],
        out_specs=[pl.BlockSpec((N_DEV, d), lambda l: (0, 0)), pl.BlockSpec((1, N_DEV, wc), lambda l: (l, 0, 0))],
        out_shape=[jax.ShapeDtypeStruct((N_DEV, d), F32), jax.ShapeDtypeStruct((nl, N_DEV, wc), F32)],
        name=name, compiler_params=_cp(("arbitrary",)))(c_all, ada_w, ada_b_cols)


def _adam(w, g, m, v):
    m2 = ADAM_B1 * m + (1.0 - ADAM_B1) * g
    v2 = ADAM_B2 * v + (1.0 - ADAM_B2) * (g * g)
    m_hat = m2 / (1.0 - ADAM_B1 ** ADAM_STEP)
    v_hat = v2 / (1.0 - ADAM_B2 ** ADAM_STEP)
    delta = -ADAM_LR * (m_hat / (jnp.sqrt(v_hat) + ADAM_EPS) + ADAM_WD * w)
    return delta, m2, v2


def _ada_grad_adam(c_act, d_mod, w, m, v, name):
    nl, d, wc = w.shape
    tk = min(512, d)

    def body(c_ref, dm_ref, w_ref, m_ref, v_ref, g_ref, dl_ref, m2_ref, v2_ref):
        g = _dotf_tn(c_ref[...], dm_ref[0])
        delta, m2, v2 = _adam(w_ref[0], g, m_ref[0], v_ref[0])
        g_ref[0], dl_ref[0], m2_ref[0], v2_ref[0] = g, delta, m2, v2

    blk = pl.BlockSpec((1, tk, wc), lambda l, i: (l, i, 0))
    shp = jax.ShapeDtypeStruct(w.shape, F32)
    return pl.pallas_call(
        body, grid=(nl, d // tk),
        in_specs=[pl.BlockSpec((N_DEV, tk), lambda l, i: (0, i)), pl.BlockSpec((1, N_DEV, wc), lambda l, i: (l, 0, 0)),
                  blk, blk, blk],
        out_specs=[blk] * 4, out_shape=[shp] * 4, name=name,
        compiler_params=_cp(("parallel", "parallel")))(c_act, d_mod, w, m, v)


def _adam_big(recv, w, m, v, layer, prev, name, after=()):
    nl, r, c = w.shape
    tr = 256 if r % 256 == 0 else r
    nparts = recv.shape[0]

    def body(rc_ref, w_ref, m_ref, v_ref, *rest):
        g_ref, dl_ref, m2_ref, v2_ref = rest[-4:]
        g = rc_ref[0].astype(F32)
        for d in range(1, nparts):
            g = g + rc_ref[d].astype(F32)
        delta, m2, v2 = _adam(w_ref[0], g, m_ref[0], v_ref[0])
        g_ref[0], dl_ref[0], m2_ref[0], v2_ref[0] = g, delta, m2, v2

    blk = pl.BlockSpec((1, tr, c), lambda i: (layer, i, 0))
    shp = jax.ShapeDtypeStruct(w.shape, F32)
    prev = () if prev is None else tuple(prev)
    return pl.pallas_call(
        body, grid=(r // tr,),
        in_specs=[pl.BlockSpec((nparts, tr, c), lambda i: (0, i, 0)), blk, blk, blk]
        + [pl.BlockSpec(memory_space=pl.ANY)] * (len(prev) + len(after)),
        out_specs=[blk] * 4, out_shape=[shp] * 4, name=name,
        input_output_aliases={4 + j: j for j in range(len(prev))},
        compiler_params=_cp(("parallel",)))(recv, w, m, v, *prev, *after)


def _sum_devices(gathered, name):
    _, r, c = gathered.shape

    def body(g_ref, o_ref):
        acc = g_ref[0]
        for d in range(1, N_DEV):
            acc = acc + g_ref[d]
        o_ref[...] = acc

    spec = pl.BlockSpec(memory_space=pltpu.VMEM)
    return pl.pallas_call(body, in_specs=[spec], out_specs=spec, out_shape=jax.ShapeDtypeStruct((r, c), F32),
                          name=name, compiler_params=pltpu.CompilerParams(vmem_limit_bytes=VMEM_LIMIT))(gathered)


def _adam_small(w, g, m, v, name):
    def body(w_ref, g_ref, m_ref, v_ref, dl_ref, m2_ref, v2_ref):
        dl_ref[...], m2_ref[...], v2_ref[...] = _adam(w_ref[...], g_ref[...], m_ref[...], v_ref[...])

    spec = pl.BlockSpec(memory_space=pltpu.VMEM)
    shp = jax.ShapeDtypeStruct(w.shape, F32)
    return pl.pallas_call(body, in_specs=[spec] * 4, out_specs=[spec] * 3, out_shape=[shp] * 3, name=name,
                          compiler_params=pltpu.CompilerParams(vmem_limit_bytes=VMEM_LIMIT))(w, g, m, v)


def _pack(parts):
    flat = jnp.concatenate([p.reshape(-1).astype(F32) for p in parts])
    assert flat.shape[0] % LANES == 0, flat.shape
    return flat.reshape(-1, LANES)


def _unpack(packed, shapes):
    flat = packed.reshape(-1)
    out, off = [], 0
    for shp in shapes:
        size = 1
        for dim in shp:
            size *= dim
        out.append(flat[off:off + size].reshape(shp))
        off += size
    return out


def _gather_cols(g, per):
    g = jnp.moveaxis(g, 0, -2)
    return g.reshape(g.shape[:-2] + (N_DEV * per,))


def _scatter_cols(g, per):
    return jnp.moveaxis(g.reshape(g.shape[:-1] + (N_DEV, per)), -2, 0)


def _my_cols(full, me, per):
    return lax.dynamic_slice_in_dim(full, me * per, per, axis=full.ndim - 1)


def kernel(x, c, positions, ada_w, ada_b, norm_g, w_in, gla_wg_f, gla_bg_f, gla_wg_b, gla_bg_b, gla_norm_g, mla_q_norm_g, mla_kv_norm_g, mla_w_uq, mla_w_ukv, mla_out_g, conv_w, conv_out_g, w_out, final_g, loss_target, m_ada_w, m_ada_b, m_norm_g, m_w_in, m_gla_wg_f, m_gla_bg_f, m_gla_wg_b, m_gla_bg_b, m_gla_norm_g, m_mla_q_norm_g, m_mla_kv_norm_g, m_mla_w_uq, m_mla_w_ukv, m_mla_out_g, m_conv_w, m_conv_out_g, m_w_out, m_final_g, v_ada_w, v_ada_b, v_norm_g, v_w_in, v_gla_wg_f, v_gla_bg_f, v_gla_wg_b, v_gla_bg_b, v_gla_norm_g, v_mla_q_norm_g, v_mla_kv_norm_g, v_mla_w_uq, v_mla_w_ukv, v_mla_out_g, v_conv_w, v_conv_out_g, v_w_out, v_final_g):
    me = 4 * lax.axis_index("x") + 2 * lax.axis_index("y") + lax.axis_index("c")
    nl = ada_w.shape[0]
    s, d = x.shape[1], x.shape[2]
    ada_cols = ada_w.shape[2]
    wgc, cwc = gla_wg_f.shape[2], conv_w.shape[2]

    (g0,) = _exchange([_pack([c, gla_wg_f, gla_wg_b, conv_w])], "gather_small_in", False, pltpu.VMEM)
    g0 = g0.reshape(N_DEV, -1)
    o1, o2, o3 = d, d + gla_wg_f.size, d + 2 * gla_wg_f.size
    c_all = g0[:, :o1]
    wgf_full = _gather_cols(g0[:, o1:o2].reshape((N_DEV,) + gla_wg_f.shape), wgc)
    wgb_full = _gather_cols(g0[:, o2:o3].reshape((N_DEV,) + gla_wg_b.shape), wgc)
    convw_full = _gather_cols(g0[:, o3:].reshape((N_DEV,) + conv_w.shape), cwc)

    ada_b_cols = _my_cols(ada_b, me, ada_cols).reshape(nl, 1, ada_cols)
    c_act, mod_cols = _ada_mod(c_all, ada_w, ada_b_cols, "ada_mod")
    (g1,) = _exchange([_pack([mod_cols])], "gather_mod", False, pltpu.VMEM)
    mod_all = g1.reshape(N_DEV, nl, N_DEV, ada_cols)
    mod_mine = _gather_cols(lax.dynamic_index_in_dim(mod_all, me, axis=2, keepdims=False), ada_cols)

    inv_freq = ROPE_THETA ** (-jnp.arange(0, MROPE, 2, dtype=F32) / MROPE)
    ang = positions[0].astype(F32)[:, None] * inv_freq
    cos, sin = jnp.tile(jnp.cos(ang), (1, LANES * 2 // MROPE)), jnp.tile(jnp.sin(ang), (1, LANES * 2 // MROPE))

    big = [w_in, w_out, mla_w_uq, mla_w_ukv]
    big_names = ["w_in", "w_out", "mla_w_uq", "mla_w_ukv"]

    def local_blocks(l):
        return [w[l].astype(MXU) for w in big]

    def put_own(lands, own):
        return [lax.dynamic_update_index_in_dim(ld, o, me, 0) for ld, o in zip(lands, own)]

    def layer_weights(l, gw_in=None, gw_out=None, gw_uq=None, gw_ukv=None):
        small = dict(norm_g=norm_g[l], gla_wg_f=wgf_full[l], gla_bg_f=gla_bg_f[l], gla_wg_b=wgb_full[l],
                     gla_bg_b=gla_bg_b[l], gla_norm_g=gla_norm_g[l], mla_q_norm_g=mla_q_norm_g[l],
                     mla_kv_norm_g=mla_kv_norm_g[l], mla_out_g=mla_out_g[l], conv_w=convw_full[l],
                     conv_out_g=conv_out_g[l])
        return _prep_layer_weights(
            None if gw_in is None else _perm_gathered(gw_in, IN_SEGS, PW),
            None if gw_out is None else gw_out.reshape((-1,) + gw_out.shape[2:]),
            None if gw_uq is None else _perm_gathered(gw_uq, UQ_SEGS, MQW),
            None if gw_ukv is None else _gather_cols(gw_ukv, mla_w_ukv.shape[2]), small)

    def land_shapes(blocks, slots):
        return [jax.ShapeDtypeStruct((slots,) + b.shape, b.dtype) for b in blocks]

    def slots_of(blocks):
        return [(N_DEV,) + b.shape for b in blocks]

    def forwarded(lands, blocks, tag):
        return put_own(_exchange_hbm(AG_FORWARD, None, lands, f"gather_{tag}_forward"), blocks)

    first = local_blocks(0)
    lands = _exchange_hbm(AG_SPREAD, first[:1], land_shapes(first[:1], N_DEV), "gather_w_in_l0_spread", (mod_mine,))
    (gw_in,) = forwarded(lands, first[:1], "w_in_l0")
    rest = _plan_start(AG_SPREAD, first[1:], slots_of(first[1:]), gw_in, "gather_rest_l0_start")
    h = x[0]
    saved, layers, mods = [], [], []
    pending = {}
    for l in range(nl):
        shift, scale, gate = (mod_mine[l, i * d:(i + 1) * d].reshape(1, d) for i in range(3))
        nxt = local_blocks(l + 1) if l + 1 < nl else None

        def start_next(after, wt_late, l=l, nxt=nxt):
            if nxt is not None:
                pending[l + 1] = _plan_start(AG_SPREAD, nxt, slots_of(nxt), after, f"gather_weights_l{l + 1}_start")
                wt_late["q_norm_g"] = layers[l]["q_norm_g"] + pending[l + 1][-1][0, 0]
            return wt_late

        if l == 0:
            scale = scale + rest[-1][0, 0]
            layers.append(layer_weights(0, gw_in))

            def late(proj):
                got = forwarded(_plan_wait(AG_SPREAD, rest, [proj], "gather_rest_l0_wait"), first[1:], "rest_l0")
                full = layer_weights(0, None, *got)
                return start_next(got[0], {k: full[k] for k in ("w_out", "w_uq", "w_ukv")})
        else:
            got = forwarded(_plan_wait(AG_SPREAD, pending.pop(l), [h], f"gather_weights_l{l}_wait"), blocks, f"weights_l{l}")
            layers.append(layer_weights(l, *got))

            def late(proj):
                return start_next(proj, {})
        mods.append((shift, scale, gate))
        h, sv = _layer_fwd(h, mods[l], layers[l], cos, sin, f"l{l}", late)
        saved.append(sv)
        blocks = nxt
    loss_part, d_h, d_final_g = _final_loss(h, final_g.reshape(1, d), loss_target[0], "final_loss")
    loss = lax.psum(loss_part[0, 0], ("x", "y", "c"))

    def grad_sends(gr):
        return [_scatter_perm(gr["w_in"], IN_SEGS, w_in.shape[2]).astype(MXU),
                gr["w_out"].reshape((N_DEV,) + w_out.shape[1:]).astype(MXU),
                _scatter_perm(gr["w_uq"], UQ_SEGS, mla_w_uq.shape[2]).astype(MXU),
                _scatter_cols(gr["w_ukv"], mla_w_ukv.shape[2]).astype(MXU)]

    my_chip = me // 2
    my_core = (me % 2).astype(jnp.int32).reshape(1)

    def chip_sums(gr, tag):
        sends = grad_sends(gr)
        got = _exchange_hbm(RS_PAIR, sends, land_shapes([sd[0] for sd in sends], N_DEV // 2), f"scatter_grads_{tag}_pair")
        return [_pair_sum(sd, gt, my_core, f"pair_sum_{n}_{tag}") for sd, gt, n in zip(sends, got, big_names)]

    def with_own_chip(lands, sums):
        return [lax.dynamic_update_index_in_dim(ld, lax.dynamic_index_in_dim(sm, my_chip, axis=0, keepdims=False),
                                                my_chip, 0) for ld, sm in zip(lands, sums)]

    small_names = ["norm_g", "gla_wg_f", "gla_bg_f", "gla_wg_b", "gla_bg_b", "gla_norm_g", "mla_q_norm_g",
                   "mla_kv_norm_g", "mla_out_g", "conv_w", "conv_out_g"]
    d_mods, grads, recv = [None] * nl, [None] * nl, [None] * nl
    flight = {}
    small = {}

    def gather_small(d_x, d_mod0, gr0):
        d_mods[0], grads[0] = d_mod0, _natural_small(gr0)
        d_mod_mine = jnp.stack([jnp.concatenate(d_mods[l], axis=-1)[0] for l in range(nl)])
        parts = [d_mod_mine] + [jnp.stack([grads[l][n] for l in range(nl)]) for n in small_names] + [d_final_g]
        (g2,) = _exchange([_pack(parts)], "gather_small_grads", False, pltpu.VMEM)
        small["d_mod_all"] = g2.reshape(N_DEV, -1)[:, :d_mod_mine.size].reshape(N_DEV, nl, 3 * d)
        small["summed"] = dict(zip(["ada_b"] + small_names + ["final_g"],
                                   _unpack(_sum_devices(g2, "sum_small_grads"), [p.shape for p in parts])))
        return (g2,)

    for l in reversed(range(nl)):
        def ship(big_grads, l=l):
            if l + 1 in flight:
                pend, sm = flight.pop(l + 1)
                recv[l + 1] = with_own_chip(_plan_wait(RS_CHIPS, pend, list(big_grads.values()),
                                                       f"scatter_grads_l{l + 1}_wait"), sm)
            sm = chip_sums(big_grads, f"l{l}")
            after = recv[l + 1][0] if l + 1 < nl else big_grads["w_in"]
            flight[l] = (_plan_start(RS_CHIPS, sm, [a.shape for a in sm], after, f"scatter_grads_l{l}_start"), sm)
            return (flight[l][0][-1],)

        if l > 0:
            d_h, d_mods[l], gr = _layer_bwd(d_h, saved[l], mods[l], layers[l], cos, sin, f"l{l}", ship)
            grads[l] = _natural_small(gr)
        else:
            d_h, _, _ = _layer_bwd(d_h, saved[l], mods[l], layers[l], cos, sin, f"l{l}", ship, gather_small)
    pending, sums = flight.pop(0)
    grad_x = d_h[None]
    summed = small["summed"]
    summed["gla_wg_f"] = _my_cols(summed["gla_wg_f"], me, wgc)
    summed["gla_wg_b"] = _my_cols(summed["gla_wg_b"], me, wgc)
    summed["conv_w"] = _my_cols(summed["conv_w"], me, cwc)

    d_mod_cols = jnp.moveaxis(_my_cols(small["d_mod_all"], me, ada_cols), 0, 1) + pending[-1][0, 0]
    out = {}
    out["ada_w"] = _ada_grad_adam(c_act, d_mod_cols, ada_w, m_ada_w, v_ada_w, "ada_grad_adam")

    given = dict(ada_b=(ada_b, m_ada_b, v_ada_b), norm_g=(norm_g, m_norm_g, v_norm_g),
                 gla_wg_f=(gla_wg_f, m_gla_wg_f, v_gla_wg_f), gla_bg_f=(gla_bg_f, m_gla_bg_f, v_gla_bg_f),
                 gla_wg_b=(gla_wg_b, m_gla_wg_b, v_gla_wg_b), gla_bg_b=(gla_bg_b, m_gla_bg_b, v_gla_bg_b),
                 gla_norm_g=(gla_norm_g, m_gla_norm_g, v_gla_norm_g),
                 mla_q_norm_g=(mla_q_norm_g, m_mla_q_norm_g, v_mla_q_norm_g),
                 mla_kv_norm_g=(mla_kv_norm_g, m_mla_kv_norm_g, v_mla_kv_norm_g),
                 mla_out_g=(mla_out_g, m_mla_out_g, v_mla_out_g), conv_w=(conv_w, m_conv_w, v_conv_w),
                 conv_out_g=(conv_out_g, m_conv_out_g, v_conv_out_g), final_g=(final_g, m_final_g, v_final_g))
    names = list(given)
    wshapes = [given[n][0].shape for n in names]
    packed = [_pack([given[n][i] for n in names]) for i in range(3)]
    g_small = _pack([summed[n].reshape(given[n][0].shape) for n in names])
    res = _adam_small(packed[0], g_small, packed[1], packed[2], "adam_small")
    unpacked = [_unpack(r, wshapes) for r in res]
    for i, n in enumerate(names):
        out[n] = (summed[n].reshape(given[n][0].shape), unpacked[0][i], unpacked[1][i], unpacked[2][i])

    moments = dict(w_in=(m_w_in, v_w_in), w_out=(m_w_out, v_w_out), mla_w_uq=(m_mla_w_uq, v_mla_w_uq),
                   mla_w_ukv=(m_mla_w_ukv, v_mla_w_ukv))
    done = [out["ada_w"][0], res[0]]
    for l in reversed(range(nl)):
        if l == 0:
            recv[0] = with_own_chip(_plan_wait(RS_CHIPS, pending, done, "scatter_grads_l0_wait"), sums)
        for i, n in enumerate(big_names):
            out[n] = _adam_big(recv[l][i], big[i], moments[n][0], moments[n][1], l, out.get(n), f"adam_{n}_l{l}",
                               (pending[-1],))
        done = done + [out[n][0] for n in big_names]

    order = ["ada_w", "ada_b", "norm_g", "w_in", "gla_wg_f", "gla_bg_f", "gla_wg_b", "gla_bg_b", "gla_norm_g",
             "mla_q_norm_g", "mla_kv_norm_g", "mla_w_uq", "mla_w_ukv", "mla_out_g", "conv_w", "conv_out_g", "w_out",
             "final_g"]
    return (loss, grad_x, *[out[n][0] for n in order], *[out[n][1] for n in order], *[out[n][2] for n in order],
            *[out[n][3] for n in order])
```

```python
import functools

import jax
import jax.numpy as jnp
from jax import lax
from jax.experimental import pallas as pl
from jax.experimental.pallas import tpu as pltpu

F32 = jnp.float32
MXU = jnp.bfloat16
HI = lax.Precision.HIGHEST
N_DEV = 8
MESH = pl.DeviceIdType.MESH

D_MIX = 2048
GH, GDK, GDV = 6, 64, 128
GW = GH * GDV
GQK = GH * GDK
GRANK = 16
GTEMP = 16.0
CHUNK = 64
MH, MQL, MKVL, MNOPE, MROPE, MDV = 6, 384, 256, 128, 64, 128
MW = MH * MDV
MQW = MH * (MNOPE + MROPE)
MKVW = MH * (MNOPE + MDV)
CONV_CH = 512
ROPE_THETA = 10000.0
EPS = 1e-6
IN_DIM = 5856
OZ, OCB, OCC, OCX, OMKV, OGV, OGQ, OGK, OMQ, OT = 0, 2048, 2560, 3072, 3584, 3840, 4608, 4992, 5376, 5760
PW = 5888
LANES = 128
VMEM_LIMIT = 56 * 1024 * 1024

ADAM_LR, ADAM_B1, ADAM_B2, ADAM_EPS, ADAM_WD, ADAM_STEP = 0.001, 0.9, 0.999, 1e-08, 0.01, 10


def _cp(sem=None):
    return pltpu.CompilerParams(dimension_semantics=sem, vmem_limit_bytes=VMEM_LIMIT)


def _dot(a, b):
    return jnp.dot(a.astype(MXU), b.astype(MXU), preferred_element_type=F32)


def _dot_nt(a, b):
    return lax.dot_general(a.astype(MXU), b.astype(MXU), (((1,), (1,)), ((), ())), preferred_element_type=F32)


def _dot_tn(a, b):
    return lax.dot_general(a.astype(MXU), b.astype(MXU), (((0,), (0,)), ((), ())), preferred_element_type=F32)


def _dotf(a, b):
    return jnp.dot(a, b, precision=HI, preferred_element_type=F32)


def _dotf_nt(a, b):
    return lax.dot_general(a, b, (((1,), (1,)), ((), ())), precision=HI, preferred_element_type=F32)


def _dotf_tn(a, b):
    return lax.dot_general(a, b, (((0,), (0,)), ((), ())), precision=HI, preferred_element_type=F32)


def _split3(x):
    hi = x.astype(jnp.bfloat16)
    r1 = x - hi.astype(F32)
    mid = r1.astype(jnp.bfloat16)
    lo = (r1 - mid.astype(F32)).astype(jnp.bfloat16)
    return hi, mid, lo


def _cum_dot(cum, x, transpose=False):
    dn = (((0,), (0,)), ((), ())) if transpose else (((1,), (0,)), ((), ()))
    cb = cum.astype(jnp.bfloat16)
    parts = [lax.dot_general(cb, p, dn, preferred_element_type=F32) for p in _split3(x)]
    return parts[0] + parts[1] + parts[2]


def _rows(s):
    return min(256, s)


def _rms(x, g):
    r = lax.rsqrt(jnp.mean(x * x, axis=-1, keepdims=True) + EPS)
    return x * r * g


def _rms_bwd(dy, x, g):
    r = lax.rsqrt(jnp.mean(x * x, axis=-1, keepdims=True) + EPS)
    xh = x * r
    dxh = dy * g
    dg = jnp.sum(dy * xh, axis=0, keepdims=True)
    dx = r * (dxh - xh * jnp.mean(dxh * xh, axis=-1, keepdims=True))
    return dx, dg


def _sigmoid(z):
    return 1.0 / (1.0 + jnp.exp(-z))


def _matmul(a, b, *, dims, tm, tn, tk, out_dtypes, name, epilogue=None, extras=(), extra_kinds=(), after=()):
    if dims == "nn":
        (m, k), n, mul = a.shape, b.shape[1], _dot
    elif dims == "nt":
        (m, k), n, mul = a.shape, b.shape[0], _dot_nt
    else:
        (k, m), n, mul = a.shape, b.shape[1], _dot_tn
    tm, tn, tk = min(tm, m), min(tn, n), min(tk, k)
    assert m % tm == 0 and n % tn == 0 and k % tk == 0, (m, n, k, tm, tn, tk)
    if dims == "nn":
        a_spec = pl.BlockSpec((tm, tk), lambda i, j, kk: (i, kk))
        b_spec = pl.BlockSpec((tk, tn), lambda i, j, kk: (kk, j))
    elif dims == "nt":
        a_spec = pl.BlockSpec((tm, tk), lambda i, j, kk: (i, kk))
        b_spec = pl.BlockSpec((tn, tk), lambda i, j, kk: (j, kk))
    else:
        a_spec = pl.BlockSpec((tk, tm), lambda i, j, kk: (kk, i))
        b_spec = pl.BlockSpec((tk, tn), lambda i, j, kk: (kk, j))
    nk = k // tk
    n_extra = len(extras)
    n_out = len(out_dtypes)
    n_after = len(after)
    extra_specs = []
    for kind in extra_kinds:
        if kind == "mn":
            extra_specs.append(pl.BlockSpec((tm, tn), lambda i, j, kk: (i, j)))
        else:
            extra_specs.append(pl.BlockSpec((1, tn), lambda i, j, kk: (0, j)))

    def body(*refs):
        a_ref, b_ref = refs[0], refs[1]
        ex = refs[2:2 + n_extra]
        outs = refs[2 + n_extra + n_after:2 + n_extra + n_after + n_out]
        acc = refs[-1]
        kk = pl.program_id(2)

        @pl.when(kk == 0)
        def _():
            acc[...] = jnp.zeros_like(acc)

        acc[...] += mul(a_ref[...], b_ref[...])

        @pl.when(kk == nk - 1)
        def _():
            res = acc[...]
            vals = (res,) if epilogue is None else epilogue(res, *[e[...] for e in ex])
            for o, v in zip(outs, vals):
                o[...] = v.astype(o.dtype)

    out_spec = pl.BlockSpec((tm, tn), lambda i, j, kk: (i, j))
    res = pl.pallas_call(
        body, grid=(m // tm, n // tn, nk),
        in_specs=[a_spec, b_spec] + extra_specs + [pl.BlockSpec(memory_space=pl.ANY)] * n_after,
        out_specs=[out_spec] * n_out,
        out_shape=[jax.ShapeDtypeStruct((m, n), dt) for dt in out_dtypes],
        scratch_shapes=[pltpu.VMEM((tm, tn), F32)],
        name=name, compiler_params=_cp(("parallel", "parallel", "arbitrary")),
    )(a, b, *extras, *after)
    return res


def _norm_mod(x, g, scale, shift, name):
    s, d = x.shape
    tr = _rows(s)

    def body(x_ref, g_ref, sc_ref, sh_ref, h_ref):
        h = _rms(x_ref[...], g_ref[...]) * (1.0 + sc_ref[...]) + sh_ref[...]
        h_ref[...] = h.astype(h_ref.dtype)

    row = pl.BlockSpec((tr, d), lambda i: (i, 0))
    vec = pl.BlockSpec((1, d), lambda i: (0, 0))
    return pl.pallas_call(body, grid=(s // tr,), in_specs=[row, vec, vec, vec], out_specs=row,
                          out_shape=jax.ShapeDtypeStruct((s, d), MXU), name=name,
                          compiler_params=_cp(("parallel",)))(x, g, scale, shift)


def _norm_mod_bwd(d_h, x, d_out, g, scale, name):
    s, d = x.shape
    tr = _rows(s)

    def body(dh_ref, x_ref, do_ref, g_ref, sc_ref, dx_ref, dsh_ref, dsc_ref, dg_ref):
        i = pl.program_id(0)

        @pl.when(i == 0)
        def _():
            dsh_ref[...] = jnp.zeros_like(dsh_ref)
            dsc_ref[...] = jnp.zeros_like(dsc_ref)
            dg_ref[...] = jnp.zeros_like(dg_ref)

        dh = dh_ref[...]
        xv = x_ref[...]
        gv = g_ref[...]
        r = lax.rsqrt(jnp.mean(xv * xv, axis=-1, keepdims=True) + EPS)
        xh = xv * r
        dsh_ref[...] += jnp.sum(dh, axis=0, keepdims=True)
        dsc_ref[...] += jnp.sum(dh * (xh * gv), axis=0, keepdims=True)
        dhn = dh * (1.0 + sc_ref[...])
        dg_ref[...] += jnp.sum(dhn * xh, axis=0, keepdims=True)
        dxh = dhn * gv
        dx_ref[...] = do_ref[...] + r * (dxh - xh * jnp.mean(dxh * xh, axis=-1, keepdims=True))

    row = pl.BlockSpec((tr, d), lambda i: (i, 0))
    vec = pl.BlockSpec((1, d), lambda i: (0, 0))
    vshape = jax.ShapeDtypeStruct((1, d), F32)
    return pl.pallas_call(body, grid=(s // tr,), in_specs=[row, row, row, vec, vec],
                          out_specs=[row, vec, vec, vec],
                          out_shape=[jax.ShapeDtypeStruct((s, d), F32), vshape, vshape, vshape],
                          name=name, compiler_params=_cp(("arbitrary",)))(d_h, x, d_out, g, scale)


def _gate_bwd(d_out, u, gate, name):
    s, d = d_out.shape
    tr = _rows(s)

    def body(do_ref, u_ref, gt_ref, du_ref, dgt_ref):
        @pl.when(pl.program_id(0) == 0)
        def _():
            dgt_ref[...] = jnp.zeros_like(dgt_ref)

        do = do_ref[...]
        du_ref[...] = (do * gt_ref[...]).astype(du_ref.dtype)
        dgt_ref[...] += jnp.sum(do * u_ref[...], axis=0, keepdims=True)

    row = pl.BlockSpec((tr, d), lambda i: (i, 0))
    vec = pl.BlockSpec((1, d), lambda i: (0, 0))
    return pl.pallas_call(body, grid=(s // tr,), in_specs=[row, row, vec], out_specs=[row, vec],
                          out_shape=[jax.ShapeDtypeStruct((s, d), MXU), jax.ShapeDtypeStruct((1, d), F32)],
                          name=name, compiler_params=_cp(("arbitrary",)))(d_out, u, gate)


def _final_loss(x, g, target, name):
    s, d = x.shape
    tr = _rows(s)

    def body(x_ref, g_ref, t_ref, loss_ref, dx_ref, dg_ref):
        @pl.when(pl.program_id(0) == 0)
        def _():
            loss_ref[...] = jnp.zeros_like(loss_ref)
            dg_ref[...] = jnp.zeros_like(dg_ref)

        xv = x_ref[...]
        gv = g_ref[...]
        diff = _rms(xv, gv) - t_ref[...]
        part = 0.5 * jnp.sum(jnp.sum(diff * diff, axis=-1, keepdims=True) / d, axis=0, keepdims=True)
        loss_ref[...] += jnp.broadcast_to(part, loss_ref.shape)
        dx, dg = _rms_bwd(diff / d, xv, gv)
        dx_ref[...] = dx
        dg_ref[...] += dg

    row = pl.BlockSpec((tr, d), lambda i: (i, 0))
    vec = pl.BlockSpec((1, d), lambda i: (0, 0))
    lvec = pl.BlockSpec((1, LANES), lambda i: (0, 0))
    return pl.pallas_call(body, grid=(s // tr,), in_specs=[row, vec, row], out_specs=[lvec, row, vec],
                          out_shape=[jax.ShapeDtypeStruct((1, LANES), F32), jax.ShapeDtypeStruct((s, d), F32),
                                     jax.ShapeDtypeStruct((1, d), F32)],
                          name=name, compiler_params=_cp(("arbitrary",)))(x, g, target)


def _shift_rows(u, s, down):
    ri = lax.broadcasted_iota(jnp.int32, u.shape, 0)
    if down:
        return jnp.where(ri == 0, 0.0, pltpu.roll(u, 1, 0))
    return jnp.where(ri == s - 1, 0.0, pltpu.roll(u, s - 1, 0))


def _conv_fwd(proj, conv_w, name):
    s = proj.shape[0]
    nt = CONV_CH // LANES

    def body(cb_ref, cc_ref, cx_ref, w_ref, pre_ref):
        u = cc_ref[...] * cx_ref[...]
        conv = _shift_rows(u, s, True) * w_ref[0:1, :] + u * w_ref[1:2, :] + _shift_rows(u, s, False) * w_ref[2:3, :]
        pre_ref[...] = cb_ref[...] * conv

    def col(off):
        return pl.BlockSpec((s, LANES), lambda j: (0, off // LANES + j))

    return pl.pallas_call(body, grid=(nt,), in_specs=[col(OCB), col(OCC), col(OCX), pl.BlockSpec((3, LANES), lambda j: (0, j))],
                          out_specs=pl.BlockSpec((s, LANES), lambda j: (0, j)),
                          out_shape=jax.ShapeDtypeStruct((s, CONV_CH), F32), name=name,
                          compiler_params=_cp(("parallel",)))(proj, proj, proj, conv_w)


def _conv_bwd(proj, conv_w, d_pre, name):
    s = proj.shape[0]
    nt = CONV_CH // LANES

    def body(cb_ref, cc_ref, cx_ref, w_ref, dp_ref, dcb_ref, dcc_ref, dcx_ref, dw_ref):
        cc, cx = cc_ref[...], cx_ref[...]
        u = cc * cx
        up, dn = _shift_rows(u, s, True), _shift_rows(u, s, False)
        w0, w1, w2 = w_ref[0:1, :], w_ref[1:2, :], w_ref[2:3, :]
        conv = up * w0 + u * w1 + dn * w2
        dp = dp_ref[...]
        dcb_ref[...] = dp * conv
        dconv = dp * cb_ref[...]
        du = _shift_rows(dconv, s, False) * w0 + dconv * w1 + _shift_rows(dconv, s, True) * w2
        dcc_ref[...] = du * cx
        dcx_ref[...] = du * cc
        dw_ref[0:1, :] = jnp.sum(dconv * up, axis=0, keepdims=True)
        dw_ref[1:2, :] = jnp.sum(dconv * u, axis=0, keepdims=True)
        dw_ref[2:3, :] = jnp.sum(dconv * dn, axis=0, keepdims=True)

    def col(off):
        return pl.BlockSpec((s, LANES), lambda j: (0, off // LANES + j))

    blk = pl.BlockSpec((s, LANES), lambda j: (0, j))
    wblk = pl.BlockSpec((3, LANES), lambda j: (0, j))
    full = jax.ShapeDtypeStruct((s, CONV_CH), F32)
    return pl.pallas_call(body, grid=(nt,), in_specs=[col(OCB), col(OCC), col(OCX), wblk, blk],
                          out_specs=[blk, blk, blk, wblk],
                          out_shape=[full, full, full, jax.ShapeDtypeStruct((3, CONV_CH), F32)],
                          name=name, compiler_params=_cp(("parallel",)))(proj, proj, proj, conv_w, d_pre)


GLA_SUB = 4


def _gla_gates(t_ref, wg_ref, bg_ref):
    t = t_ref[...]
    a = _dot(t, wg_ref[...]) + bg_ref[...]
    la = (jnp.minimum(a, 0.0) - jnp.log(1.0 + jnp.exp(-jnp.abs(a)))) / GTEMP
    return t, a, la


def _gla_masks(reverse):
    ri = lax.broadcasted_iota(jnp.int32, (CHUNK, CHUNK), 0)
    ci = lax.broadcasted_iota(jnp.int32, (CHUNK, CHUNK), 1)
    if reverse:
        cum, mask = ci >= ri, ci > ri
    else:
        cum, mask = ci <= ri, ci <= ri
    return cum.astype(F32), mask


def _gla_specs(s, reverse):
    nsub = min(GLA_SUB, s // CHUNK)
    nsteps = s // (CHUNK * nsub)

    def row(n):
        return nsteps - 1 - n if reverse else n

    def chunk(pi):
        return nsub - 1 - pi if reverse else pi

    return nsub, nsteps, row, chunk


def _gla_fwd(proj, wg_pad, bg, reverse, name):
    s = proj.shape[0]
    nsub, nsteps, row, chunk = _gla_specs(s, reverse)
    rb = nsub * CHUNK

    def body(q_ref, k_ref, v_ref, t_ref, wg_ref, bg_ref, o_ref, st_ref, state):
        @pl.when(pl.program_id(0) == 0)
        def _():
            state[...] = jnp.zeros_like(state)

        _, _, la = _gla_gates(t_ref, wg_ref, bg_ref)
        cumf, mask = _gla_masks(reverse)
        lane = lax.broadcasted_iota(jnp.int32, (CHUNK, LANES), 1)
        for pi in range(nsub):
            rows = slice(chunk(pi) * CHUNK, (chunk(pi) + 1) * CHUNK)
            la_c = la[rows]
            b = _cum_dot(cumf, la_c)
            bl = jnp.sum(la_c, axis=0, keepdims=True)
            q = q_ref[rows, :] * (GDK ** -0.5)
            k = k_ref[rows, :]
            qd = q * jnp.exp(b)
            ki = k * jnp.exp(-b)
            kte = k * jnp.exp(bl - b)
            decay = jnp.exp(bl)
            for h in range(GH):
                p = h // 2
                sl = slice(p * LANES, (p + 1) * LANES)
                lm = (lane < GDK) if h % 2 == 0 else (lane >= GDK)
                qd_h = jnp.where(lm, qd[:, sl], 0.0)
                kte_h = jnp.where(lm, kte[:, sl], 0.0)
                v_h = v_ref[rows, h * GDV:(h + 1) * GDV]
                st = state[h]
                a_mat = jnp.where(mask, _dot_nt(qd_h, ki[:, sl]), 0.0)
                o_ref[rows, h * GDV:(h + 1) * GDV] = _dot(a_mat, v_h) + _dot_nt(qd_h, st)
                st_ref[pi, h] = st
                state[h] = st * decay[:, sl] + _dot_tn(v_h, kte_h)

    return pl.pallas_call(
        body, grid=(nsteps,),
        in_specs=[pl.BlockSpec((rb, GQK), lambda n: (row(n), OGQ // GQK)),
                  pl.BlockSpec((rb, GQK), lambda n: (row(n), OGK // GQK)),
                  pl.BlockSpec((rb, GW), lambda n: (row(n), OGV // GW)),
                  pl.BlockSpec((rb, LANES), lambda n: (row(n), OT // LANES)),
                  pl.BlockSpec((LANES, GQK), lambda n: (0, 0)),
                  pl.BlockSpec((1, GQK), lambda n: (0, 0))],
        out_specs=[pl.BlockSpec((rb, GW), lambda n: (row(n), 0)),
                   pl.BlockSpec((nsub, GH, GDV, LANES), lambda n: (n, 0, 0, 0))],
        out_shape=[jax.ShapeDtypeStruct((s, GW), F32), jax.ShapeDtypeStruct((s // CHUNK, GH, GDV, LANES), F32)],
        scratch_shapes=[pltpu.VMEM((GH, GDV, LANES), F32)],
        name=name, compiler_params=_cp(("arbitrary",)))(proj, proj, proj, proj, wg_pad, bg)


def _gla_bwd(proj, wg_pad, bg, states, d_o, reverse, name):
    s = proj.shape[0]
    nsub, nsteps, row, chunk = _gla_specs(s, reverse)
    rb = nsub * CHUNK

    def body(q_ref, k_ref, v_ref, t_ref, wg_ref, bg_ref, st_ref, do_ref,
             dq_ref, dk_ref, dv_ref, dt_ref, dwg_ref, dbg_ref, dstate, da_buf):
        @pl.when(pl.program_id(0) == 0)
        def _():
            dstate[...] = jnp.zeros_like(dstate)
            dwg_ref[...] = jnp.zeros_like(dwg_ref)
            dbg_ref[...] = jnp.zeros_like(dbg_ref)

        t, a, la = _gla_gates(t_ref, wg_ref, bg_ref)
        cumf, mask = _gla_masks(reverse)
        lane = lax.broadcasted_iota(jnp.int32, (CHUNK, LANES), 1)
        for pi in reversed(range(nsub)):
            rows = slice(chunk(pi) * CHUNK, (chunk(pi) + 1) * CHUNK)
            la_c = la[rows]
            b = _cum_dot(cumf, la_c)
            bl = jnp.sum(la_c, axis=0, keepdims=True)
            q = q_ref[rows, :] * (GDK ** -0.5)
            k = k_ref[rows, :]
            e, ei, ee = jnp.exp(b), jnp.exp(-b), jnp.exp(bl - b)
            qd, ki, kte = q * e, k * ei, k * ee
            decay = jnp.exp(bl)
            for p in range(GH // 2):
                sl = slice(p * LANES, (p + 1) * LANES)
                dqd = jnp.zeros((CHUNK, LANES), F32)
                dki = jnp.zeros((CHUNK, LANES), F32)
                dkte = jnp.zeros((CHUNK, LANES), F32)
                ddecay = jnp.zeros((1, LANES), F32)
                for half in range(2):
                    h = 2 * p + half
                    lm = (lane < GDK) if half == 0 else (lane >= GDK)
                    qd_h = jnp.where(lm, qd[:, sl], 0.0)
                    ki_h = jnp.where(lm, ki[:, sl], 0.0)
                    kte_h = jnp.where(lm, kte[:, sl], 0.0)
                    v_h = v_ref[rows, h * GDV:(h + 1) * GDV]
                    do_h = do_ref[rows, h * GDV:(h + 1) * GDV]
                    st = st_ref[pi, h]
                    dst = dstate[h]
                    a_mat = jnp.where(mask, _dot_nt(qd_h, ki_h), 0.0)
                    da_mat = jnp.where(mask, _dot_nt(do_h, v_h), 0.0)
                    dv_ref[rows, h * GDV:(h + 1) * GDV] = _dot_tn(a_mat, do_h) + _dot_nt(kte_h, dst)
                    dqd += _dot(da_mat, ki_h) + _dot(do_h, st)
                    dki += _dot_tn(da_mat, qd_h)
                    dkte += _dot(v_h, dst)
                    ddecay += jnp.sum(dst * st, axis=0, keepdims=True)
                    dstate[h] = dst * decay[:, sl] + _dot_tn(do_h, qd_h)
                dq_ref[rows, sl] = dqd * e[:, sl] * (GDK ** -0.5)
                dk_ref[rows, sl] = dki * ei[:, sl] + dkte * ee[:, sl]
                db = dqd * qd[:, sl] - dki * ki[:, sl] - dkte * kte[:, sl]
                dbl = jnp.sum(dkte * kte[:, sl], axis=0, keepdims=True) + decay[:, sl] * ddecay
                da_buf[rows, sl] = _cum_dot(cumf, db, True) + dbl
        da = da_buf[...] * (1.0 / GTEMP) * _sigmoid(-a)
        dt_ref[...] = _dot_nt(da, wg_ref[...])
        dwg_ref[...] += _dot_tn(t, da)
        dbg_ref[...] += jnp.sum(da, axis=0, keepdims=True)

    def prow(j):
        return row(nsteps - 1 - j)

    return pl.pallas_call(
        body, grid=(nsteps,),
        in_specs=[pl.BlockSpec((rb, GQK), lambda j: (prow(j), OGQ // GQK)),
                  pl.BlockSpec((rb, GQK), lambda j: (prow(j), OGK // GQK)),
                  pl.BlockSpec((rb, GW), lambda j: (prow(j), OGV // GW)),
                  pl.BlockSpec((rb, LANES), lambda j: (prow(j), OT // LANES)),
                  pl.BlockSpec((LANES, GQK), lambda j: (0, 0)),
                  pl.BlockSpec((1, GQK), lambda j: (0, 0)),
                  pl.BlockSpec((nsub, GH, GDV, LANES), lambda j: (nsteps - 1 - j, 0, 0, 0)),
                  pl.BlockSpec((rb, GW), lambda j: (prow(j), 0))],
        out_specs=[pl.BlockSpec((rb, GQK), lambda j: (prow(j), 0)),
                   pl.BlockSpec((rb, GQK), lambda j: (prow(j), 0)),
                   pl.BlockSpec((rb, GW), lambda j: (prow(j), 0)),
                   pl.BlockSpec((rb, LANES), lambda j: (prow(j), 0)),
                   pl.BlockSpec((LANES, GQK), lambda j: (0, 0)),
                   pl.BlockSpec((1, GQK), lambda j: (0, 0))],
        out_shape=[jax.ShapeDtypeStruct((s, GQK), F32), jax.ShapeDtypeStruct((s, GQK), F32),
                   jax.ShapeDtypeStruct((s, GW), F32), jax.ShapeDtypeStruct((s, LANES), F32),
                   jax.ShapeDtypeStruct((LANES, GQK), F32), jax.ShapeDtypeStruct((1, GQK), F32)],
        scratch_shapes=[pltpu.VMEM((GH, GDV, LANES), F32), pltpu.VMEM((rb, GQK), F32)],
        name=name, compiler_params=_cp(("arbitrary",)))(proj, proj, proj, proj, wg_pad, bg, states, d_o)


def _rot_half(x):
    lane = lax.broadcasted_iota(jnp.int32, x.shape, 1)
    first = (lane % MROPE) < (MROPE // 2)
    return jnp.where(first, -pltpu.roll(x, LANES - MROPE // 2, 1), pltpu.roll(x, MROPE // 2, 1))


def _mla_prep(proj, cos, sin, qg, kvg, w_uq, w_ukv, name):
    s = proj.shape[0]
    tr = _rows(s)

    def body(mq_ref, mkv_ref, t_ref, cos_ref, sin_ref, qg_ref, kvg_ref, wuq_ref, wukv_ref, q_ref, k_ref, v_ref):
        cosv, sinv = cos_ref[...], sin_ref[...]
        lane = lax.broadcasted_iota(jnp.int32, (tr, LANES), 1)

        def rope(xv):
            return xv * cosv + _rot_half(xv) * sinv

        qm = _dot(_rms(mq_ref[...], qg_ref[...]), wuq_ref[...])
        kv = _dot(_rms(mkv_ref[...], kvg_ref[...]), wukv_ref[...])
        kr_lo = jnp.where(lane < MROPE, rope(t_ref[...]), 0.0)
        kr_hi = pltpu.roll(kr_lo, MROPE, 1)
        for p in range(MH // 2):
            r = rope(qm[:, MW + p * LANES:MW + (p + 1) * LANES]).astype(q_ref.dtype)
            q_ref[2 * p, :, LANES:] = r
            q_ref[2 * p + 1, :, LANES:] = r
        for h in range(MH):
            q_ref[h, :, :LANES] = qm[:, h * LANES:(h + 1) * LANES].astype(q_ref.dtype)
            k_ref[h, :, :LANES] = kv[:, 2 * h * LANES:(2 * h + 1) * LANES].astype(k_ref.dtype)
            k_ref[h, :, LANES:] = (kr_lo if h % 2 == 0 else kr_hi).astype(k_ref.dtype)
            v_ref[h] = kv[:, (2 * h + 1) * LANES:(2 * h + 2) * LANES].astype(v_ref.dtype)

    def full(shape):
        return pl.BlockSpec(shape, lambda i: (0,) * len(shape))

    return pl.pallas_call(
        body, grid=(s // tr,),
        in_specs=[pl.BlockSpec((tr, MQL), lambda i: (i, OMQ // MQL)),
                  pl.BlockSpec((tr, MKVL), lambda i: (i, OMKV // MKVL)),
                  pl.BlockSpec((tr, LANES), lambda i: (i, OT // LANES)),
                  pl.BlockSpec((tr, LANES), lambda i: (i, 0)),
                  pl.BlockSpec((tr, LANES), lambda i: (i, 0)),
                  full((1, MQL)), full((1, MKVL)), full((MQL, MQW)), full((MKVL, MKVW))],
        out_specs=[pl.BlockSpec((MH, tr, 2 * LANES), lambda i: (0, i, 0)),
                   pl.BlockSpec((MH, tr, 2 * LANES), lambda i: (0, i, 0)),
                   pl.BlockSpec((MH, tr, LANES), lambda i: (0, i, 0))],
        out_shape=[jax.ShapeDtypeStruct((MH, s, 2 * LANES), MXU), jax.ShapeDtypeStruct((MH, s, 2 * LANES), MXU),
                   jax.ShapeDtypeStruct((MH, s, LANES), MXU)],
        name=name, compiler_params=_cp(("parallel",)))(proj, proj, proj, cos, sin, qg, kvg, w_uq, w_ukv)


def _mla_prep_bwd(proj, cos, sin, qg, kvg, w_uq, w_ukv, d_q, d_k, d_v, name):
    s = proj.shape[0]
    tr = _rows(s)

    def body(mq_ref, mkv_ref, cos_ref, sin_ref, qg_ref, kvg_ref, wuq_ref, wukv_ref, dq_ref, dk_ref, dv_ref,
             dmq_ref, dmkv_ref, dt_ref, dwuq_ref, dwukv_ref, dqg_ref, dkvg_ref):
        @pl.when(pl.program_id(0) == 0)
        def _():
            for r in (dwuq_ref, dwukv_ref, dqg_ref, dkvg_ref):
                r[...] = jnp.zeros_like(r)

        cosv, sinv = cos_ref[...], sin_ref[...]
        lane = lax.broadcasted_iota(jnp.int32, (tr, LANES), 1)
        lo = lane < MROPE

        def unrope(dv):
            return dv * cosv - _rot_half(dv * sinv)

        parts = [dq_ref[h, :, :LANES] for h in range(MH)]
        for p in range(MH // 2):
            parts.append(unrope(jnp.where(lo, dq_ref[2 * p, :, LANES:], dq_ref[2 * p + 1, :, LANES:])))
        d_qm = jnp.concatenate(parts, axis=1)
        mq, qgv = mq_ref[...], qg_ref[...]
        cq = _rms(mq, qgv)
        dwuq_ref[...] += _dot_tn(cq, d_qm)
        dmq, dqg = _rms_bwd(_dot_nt(d_qm, wuq_ref[...]), mq, qgv)
        dmq_ref[...] = dmq
        dqg_ref[...] += dqg

        parts = []
        for h in range(MH):
            parts += [dk_ref[h, :, :LANES], dv_ref[h]]
        d_kv = jnp.concatenate(parts, axis=1)
        mkv, kvgv = mkv_ref[...], kvg_ref[...]
        ckv = _rms(mkv, kvgv)
        dwukv_ref[...] += _dot_tn(ckv, d_kv)
        dmkv, dkvg = _rms_bwd(_dot_nt(d_kv, wukv_ref[...]), mkv, kvgv)
        dmkv_ref[...] = dmkv
        dkvg_ref[...] += dkvg

        even = dk_ref[0, :, LANES:] + dk_ref[2, :, LANES:] + dk_ref[4, :, LANES:]
        odd = dk_ref[1, :, LANES:] + dk_ref[3, :, LANES:] + dk_ref[5, :, LANES:]
        d_kr = jnp.where(lo, even, 0.0) + pltpu.roll(jnp.where(lo, 0.0, odd), MROPE, 1)
        dt_ref[...] = jnp.where(lo, unrope(d_kr), 0.0)

    def full(shape):
        return pl.BlockSpec(shape, lambda i: (0,) * len(shape))

    return pl.pallas_call(
        body, grid=(s // tr,),
        in_specs=[pl.BlockSpec((tr, MQL), lambda i: (i, OMQ // MQL)),
                  pl.BlockSpec((tr, MKVL), lambda i: (i, OMKV // MKVL)),
                  pl.BlockSpec((tr, LANES), lambda i: (i, 0)),
                  pl.BlockSpec((tr, LANES), lambda i: (i, 0)),
                  full((1, MQL)), full((1, MKVL)), full((MQL, MQW)), full((MKVL, MKVW)),
                  pl.BlockSpec((MH, tr, 2 * LANES), lambda i: (0, i, 0)),
                  pl.BlockSpec((MH, tr, 2 * LANES), lambda i: (0, i, 0)),
                  pl.BlockSpec((MH, tr, LANES), lambda i: (0, i, 0))],
        out_specs=[pl.BlockSpec((tr, MQL), lambda i: (i, 0)), pl.BlockSpec((tr, MKVL), lambda i: (i, 0)),
                   pl.BlockSpec((tr, LANES), lambda i: (i, 0)),
                   full((MQL, MQW)), full((MKVL, MKVW)), full((1, MQL)), full((1, MKVL))],
        out_shape=[jax.ShapeDtypeStruct((s, MQL), F32), jax.ShapeDtypeStruct((s, MKVL), F32),
                   jax.ShapeDtypeStruct((s, LANES), F32),
                   jax.ShapeDtypeStruct((MQL, MQW), F32), jax.ShapeDtypeStruct((MKVL, MKVW), F32),
                   jax.ShapeDtypeStruct((1, MQL), F32), jax.ShapeDtypeStruct((1, MKVL), F32)],
        name=name, compiler_params=_cp(("arbitrary",)))(proj, proj, cos, sin, qg, kvg, w_uq, w_ukv, d_q, d_k, d_v)


ATT_SCALE = (MNOPE + MROPE) ** -0.5


def _attn_fwd(q, k, v, name):
    s = q.shape[1]
    tq = _rows(s)

    def body(q_ref, k_ref, v_ref, o_ref, lse_ref):
        sc = _dot_nt(q_ref[0], k_ref[0]) * ATT_SCALE
        m = jnp.max(sc, axis=-1, keepdims=True)
        p = jnp.exp(sc - m)
        l = jnp.sum(p, axis=-1, keepdims=True)
        o_ref[...] = _dot(p, v_ref[0]) / l
        lse_ref[0] = m + jnp.log(l)

    return pl.pallas_call(
        body, grid=(MH, s // tq),
        in_specs=[pl.BlockSpec((1, tq, 2 * LANES), lambda h, i: (h, i, 0)),
                  pl.BlockSpec((1, s, 2 * LANES), lambda h, i: (h, 0, 0)),
                  pl.BlockSpec((1, s, LANES), lambda h, i: (h, 0, 0))],
        out_specs=[pl.BlockSpec((tq, LANES), lambda h, i: (i, h)),
                   pl.BlockSpec((1, tq, 1), lambda h, i: (h, i, 0))],
        out_shape=[jax.ShapeDtypeStruct((s, MW), F32), jax.ShapeDtypeStruct((MH, s, 1), F32)],
        name=name, compiler_params=_cp(("parallel", "parallel")))(q, k, v)


def _attn_bwd(q, k, v, o, lse, d_o, name):
    s = q.shape[1]
    tq = _rows(s)

    def body(q_ref, k_ref, v_ref, o_ref, lse_ref, do_ref, dq_ref, dk_ref, dv_ref):
        @pl.when(pl.program_id(1) == 0)
        def _():
            dk_ref[...] = jnp.zeros_like(dk_ref)
            dv_ref[...] = jnp.zeros_like(dv_ref)

        qv, kv, do = q_ref[0], k_ref[0], do_ref[...]
        p = jnp.exp(_dot_nt(qv, kv) * ATT_SCALE - lse_ref[0])
        delta = jnp.sum(do * o_ref[...], axis=-1, keepdims=True)
        ds = p * (_dot_nt(do, v_ref[0]) - delta) * ATT_SCALE
        dq_ref[0] = _dot(ds, kv)
        dk_ref[0] += _dot_tn(ds, qv)
        dv_ref[0] += _dot_tn(p, do)

    return pl.pallas_call(
        body, grid=(MH, s // tq),
        in_specs=[pl.BlockSpec((1, tq, 2 * LANES), lambda h, i: (h, i, 0)),
                  pl.BlockSpec((1, s, 2 * LANES), lambda h, i: (h, 0, 0)),
                  pl.BlockSpec((1, s, LANES), lambda h, i: (h, 0, 0)),
                  pl.BlockSpec((tq, LANES), lambda h, i: (i, h)),
                  pl.BlockSpec((1, tq, 1), lambda h, i: (h, i, 0)),
                  pl.BlockSpec((tq, LANES), lambda h, i: (i, h))],
        out_specs=[pl.BlockSpec((1, tq, 2 * LANES), lambda h, i: (h, i, 0)),
                   pl.BlockSpec((1, s, 2 * LANES), lambda h, i: (h, 0, 0)),
                   pl.BlockSpec((1, s, LANES), lambda h, i: (h, 0, 0))],
        out_shape=[jax.ShapeDtypeStruct((MH, s, 2 * LANES), F32), jax.ShapeDtypeStruct((MH, s, 2 * LANES), F32),
                   jax.ShapeDtypeStruct((MH, s, LANES), F32)],
        name=name, compiler_params=_cp(("parallel", "arbitrary")))(q, k, v, o, lse, d_o)


def _merge_fwd(o_f, o_b, o_att, pre, proj, gng, mog, cog, name):
    s = proj.shape[0]
    tr = _rows(s)

    def body(of_ref, ob_ref, oa_ref, pre_ref, z_ref, gng_ref, mog_ref, cog_ref, y_ref):
        z = z_ref[...]
        sz = z * _sigmoid(z)
        osum = of_ref[...] + ob_ref[...]
        gg = gng_ref[...]
        for h in range(GH):
            sl = slice(h * GDV, (h + 1) * GDV)
            y_ref[:, sl] = (_rms(osum[:, sl], gg) * sz[:, sl]).astype(y_ref.dtype)
        y_ref[:, GW:GW + MW] = (_rms(oa_ref[...], mog_ref[...]) * sz[:, GW:GW + MW]).astype(y_ref.dtype)
        y_ref[:, GW + MW:] = (_rms(pre_ref[...], cog_ref[...]) * sz[:, GW + MW:]).astype(y_ref.dtype)

    def row(w):
        return pl.BlockSpec((tr, w), lambda i: (i, 0))

    def vec(w):
        return pl.BlockSpec((1, w), lambda i: (0, 0))

    return pl.pallas_call(
        body, grid=(s // tr,),
        in_specs=[row(GW), row(GW), row(MW), row(CONV_CH), row(D_MIX), vec(GDV), vec(MW), vec(CONV_CH)],
        out_specs=row(D_MIX), out_shape=jax.ShapeDtypeStruct((s, D_MIX), MXU),
        name=name, compiler_params=_cp(("parallel",)))(o_f, o_b, o_att, pre, proj, gng, mog, cog)


def _merge_bwd(d_y, o_f, o_b, o_att, pre, proj, gng, mog, cog, name):
    s = proj.shape[0]
    tr = _rows(s)

    def body(dy_ref, of_ref, ob_ref, oa_ref, pre_ref, z_ref, gng_ref, mog_ref, cog_ref,
             dz_ref, dos_ref, doa_ref, dpre_ref, dgng_ref, dmog_ref, dcog_ref):
        @pl.when(pl.program_id(0) == 0)
        def _():
            for r in (dgng_ref, dmog_ref, dcog_ref):
                r[...] = jnp.zeros_like(r)

        z, dy = z_ref[...], dy_ref[...]
        sg = _sigmoid(z)
        sz = z * sg
        dsz = sg * (1.0 + z * (1.0 - sg))
        dcat = dy * sz
        dyz = dy * dsz
        osum = of_ref[...] + ob_ref[...]
        gg = gng_ref[...]
        dgg = jnp.zeros_like(gg)
        for h in range(GH):
            sl = slice(h * GDV, (h + 1) * GDV)
            dz_ref[:, sl] = dyz[:, sl] * _rms(osum[:, sl], gg)
            dx, dg = _rms_bwd(dcat[:, sl], osum[:, sl], gg)
            dos_ref[:, sl] = dx
            dgg += dg
        dgng_ref[...] += dgg
        sl = slice(GW, GW + MW)
        oa, mg = oa_ref[...], mog_ref[...]
        dz_ref[:, sl] = dyz[:, sl] * _rms(oa, mg)
        dx, dg = _rms_bwd(dcat[:, sl], oa, mg)
        doa_ref[...] = dx
        dmog_ref[...] += dg
        sl = slice(GW + MW, D_MIX)
        pv, cg = pre_ref[...], cog_ref[...]
        dz_ref[:, sl] = dyz[:, sl] * _rms(pv, cg)
        dx, dg = _rms_bwd(dcat[:, sl], pv, cg)
        dpre_ref[...] = dx
        dcog_ref[...] += dg

    def row(w):
        return pl.BlockSpec((tr, w), lambda i: (i, 0))

    def vec(w):
        return pl.BlockSpec((1, w), lambda i: (0, 0))

    def rs(w):
        return jax.ShapeDtypeStruct((s, w), F32)

    def vs(w):
        return jax.ShapeDtypeStruct((1, w), F32)

    return pl.pallas_call(
        body, grid=(s // tr,),
        in_specs=[row(D_MIX), row(GW), row(GW), row(MW), row(CONV_CH), row(D_MIX), vec(GDV), vec(MW), vec(CONV_CH)],
        out_specs=[row(D_MIX), row(GW), row(MW), row(CONV_CH), vec(GDV), vec(MW), vec(CONV_CH)],
        out_shape=[rs(D_MIX), rs(GW), rs(MW), rs(CONV_CH), vs(GDV), vs(MW), vs(CONV_CH)],
        name=name, compiler_params=_cp(("arbitrary",)))(d_y, o_f, o_b, o_att, pre, proj, gng, mog, cog)


def _assemble_dproj(d_z, d_cb, d_cc, d_cx, d_mkv, dv_f, dv_b, dq_f, dq_b, dk_f, dk_b, d_mq, dt_m, dt_f, dt_b, name):
    s = d_z.shape[0]
    tr = _rows(s)

    def body(dz, dcb, dcc, dcx, dmkv, dvf, dvb, dqf, dqb, dkf, dkb, dmq, dtm, dtf, dtb, out):
        dt = out.dtype
        out[:, OZ:OZ + D_MIX] = dz[...].astype(dt)
        out[:, OCB:OCB + CONV_CH] = dcb[...].astype(dt)
        out[:, OCC:OCC + CONV_CH] = dcc[...].astype(dt)
        out[:, OCX:OCX + CONV_CH] = dcx[...].astype(dt)
        out[:, OMKV:OMKV + MKVL] = dmkv[...].astype(dt)
        out[:, OGV:OGV + GW] = (dvf[...] + dvb[...]).astype(dt)
        out[:, OGQ:OGQ + GQK] = (dqf[...] + dqb[...]).astype(dt)
        out[:, OGK:OGK + GQK] = (dkf[...] + dkb[...]).astype(dt)
        out[:, OMQ:OMQ + MQL] = dmq[...].astype(dt)
        out[:, OT:OT + LANES] = (dtm[...] + dtf[...] + dtb[...]).astype(dt)

    args = (d_z, d_cb, d_cc, d_cx, d_mkv, dv_f, dv_b, dq_f, dq_b, dk_f, dk_b, d_mq, dt_m, dt_f, dt_b)
    return pl.pallas_call(
        body, grid=(s // tr,),
        in_specs=[pl.BlockSpec((tr, a.shape[1]), lambda i: (i, 0)) for a in args],
        out_specs=pl.BlockSpec((tr, PW), lambda i: (i, 0)),
        out_shape=jax.ShapeDtypeStruct((s, PW), MXU), name=name, compiler_params=_cp(("parallel",)))(*args)


def _layer_fwd(x, mod, wt, cos, sin, tag, late=None, in_after=()):
    shift, scale, gate = mod
    h = _norm_mod(x, wt["norm_g"], scale, shift, f"norm_mod_{tag}")
    (proj,) = _matmul(h, wt["w_in"], dims="nn", tm=1024, tn=256, tk=2048, out_dtypes=(F32,), name=f"in_proj_{tag}",
                      after=in_after)
    if late is not None:
        wt.update(late(proj))
    o_f, st_f = _gla_fwd(proj, wt["wg_pad_f"], wt["bg_f"], False, f"gla_fwd_f_{tag}")
    o_b, st_b = _gla_fwd(proj, wt["wg_pad_b"], wt["bg_b"], True, f"gla_fwd_b_{tag}")
    q, k, v = _mla_prep(proj, cos, sin, wt["q_norm_g"], wt["kv_norm_g"], wt["w_uq"], wt["w_ukv"], f"mla_prep_{tag}")
    o_att, lse = _attn_fwd(q, k, v, f"attn_fwd_{tag}")
    pre = _conv_fwd(proj, wt["conv_w"], f"conv_fwd_{tag}")
    y = _merge_fwd(o_f, o_b, o_att, pre, proj, wt["gla_norm_g"], wt["mla_out_g"], wt["conv_out_g"], f"merge_fwd_{tag}")
    x_new, u = _matmul(y, wt["w_out"], dims="nn", tm=1024, tn=512, tk=2048, out_dtypes=(F32, F32),
                       name=f"out_proj_{tag}", epilogue=lambda acc, xv, gv: (xv + gv * acc, acc),
                       extras=(x, gate), extra_kinds=("mn", "n"))
    saved = dict(x=x, h=h, proj=proj, o_f=o_f, o_b=o_b, st_f=st_f, st_b=st_b, q=q, k=k, v=v,
                 o_att=o_att, lse=lse, pre=pre, y=y, u=u)
    return x_new, saved


def _layer_bwd(d_out, sv, mod, wt, cos, sin, tag, ship=None, dx_first=None):
    shift, scale, gate = mod
    proj = sv["proj"]
    d_u, d_gate = _gate_bwd(d_out, sv["u"], gate, f"gate_bwd_{tag}")
    (g_w_out,) = _matmul(sv["y"], d_u, dims="tn", tm=1024, tn=512, tk=2048, out_dtypes=(MXU,), name=f"out_proj_dw_{tag}")
    (d_y,) = _matmul(d_u, wt["w_out"], dims="nt", tm=1024, tn=512, tk=2048, out_dtypes=(F32,), name=f"out_proj_dx_{tag}",
                     after=(g_w_out,))
    d_z, d_osum, d_oatt, d_pre, d_gng, d_mog, d_cog = _merge_bwd(
        d_y, sv["o_f"], sv["o_b"], sv["o_att"], sv["pre"], proj, wt["gla_norm_g"], wt["mla_out_g"], wt["conv_out_g"],
        f"merge_bwd_{tag}")
    d_cb, d_cc, d_cx, d_conv_w = _conv_bwd(proj, wt["conv_w"], d_pre, f"conv_bwd_{tag}")
    d_q, d_k, d_v = _attn_bwd(sv["q"], sv["k"], sv["v"], sv["o_att"], sv["lse"], d_oatt, f"attn_bwd_{tag}")
    d_mq, d_mkv, dt_m, g_w_uq, g_w_ukv, d_qg, d_kvg = _mla_prep_bwd(
        proj, cos, sin, wt["q_norm_g"], wt["kv_norm_g"], wt["w_uq"], wt["w_ukv"], d_q, d_k, d_v, f"mla_prep_bwd_{tag}")
    dq_f, dk_f, dv_f, dt_f, d_wg_f, d_bg_f = _gla_bwd(proj, wt["wg_pad_f"], wt["bg_f"], sv["st_f"], d_osum, False,
                                                     f"gla_bwd_f_{tag}")
    dq_b, dk_b, dv_b, dt_b, d_wg_b, d_bg_b = _gla_bwd(proj, wt["wg_pad_b"], wt["bg_b"], sv["st_b"], d_osum, True,
                                                     f"gla_bwd_b_{tag}")
    d_proj = _assemble_dproj(d_z, d_cb, d_cc, d_cx, d_mkv, dv_f, dv_b, dq_f, dq_b, dk_f, dk_b, d_mq, dt_m, dt_f, dt_b,
                             f"assemble_dproj_{tag}")
    grads = dict(w_out=g_w_out, w_uq=g_w_uq, w_ukv=g_w_ukv,
                 wg_pad_f=d_wg_f, bg_f=d_bg_f, wg_pad_b=d_wg_b, bg_b=d_bg_b, gla_norm_g=d_gng,
                 q_norm_g=d_qg, kv_norm_g=d_kvg, mla_out_g=d_mog, conv_w=d_conv_w, conv_out_g=d_cog)

    def in_dw(after):
        (g_w_in,) = _matmul(sv["h"], d_proj, dims="tn", tm=1024, tn=256, tk=2048, out_dtypes=(MXU,),
                            name=f"in_proj_dw_{tag}", after=after)
        grads["w_in"] = g_w_in
        return dict(w_in=g_w_in, w_out=g_w_out, w_uq=g_w_uq, w_ukv=g_w_ukv)

    def in_dx(after):
        (d_h,) = _matmul(d_proj, wt["w_in"], dims="nt", tm=1024, tn=512, tk=PW // 2, out_dtypes=(F32,),
                         name=f"in_proj_dx_{tag}", after=after)
        d_x, d_shift, d_scale, d_ng = _norm_mod_bwd(d_h, sv["x"], d_out, wt["norm_g"], scale, f"norm_mod_bwd_{tag}")
        grads["norm_g"] = d_ng
        return d_x, (d_shift, d_scale, d_gate)

    if dx_first is None:
        big = in_dw(())
        d_x, d_mod = in_dx((big["w_in"],) if ship is None else ship(big))
    else:
        d_x, d_mod = in_dx(())
        big = in_dw(dx_first(d_x, d_mod, grads))
        ship(big)
    return d_x, d_mod, grads


def _perm_in_cols(w):
    pad = jnp.zeros(w.shape[:-1] + (PW - IN_DIM,), w.dtype)
    return jnp.concatenate([w[..., 3808:5856], w[..., 2272:3808], w[..., 1952:2208], w[..., 768:1536], w[..., 0:768],
                            w[..., 1568:1952], w[..., 2208:2272], w[..., 1536:1568], pad], axis=-1)


def _unperm_in_cols(g):
    return jnp.concatenate([g[..., OGQ:OGQ + 2 * GQK], g[..., OGV:OGV + GW], g[..., OT + MROPE:OT + MROPE + 2 * GRANK],
                            g[..., OMQ:OMQ + MQL], g[..., OMKV:OMKV + MKVL], g[..., OT:OT + MROPE],
                            g[..., OCB:OCB + 3 * CONV_CH], g[..., OZ:OZ + D_MIX]], axis=-1)


IN_SEGS = ((3808, 5856), (2272, 3808), (1952, 2208), (768, 1536), (0, 768), (1568, 1952), (2208, 2272), (1536, 1568))
UQ_SEGS = (tuple((h * (MNOPE + MROPE), h * (MNOPE + MROPE) + MNOPE) for h in range(MH))
           + tuple((h * (MNOPE + MROPE) + MNOPE, (h + 1) * (MNOPE + MROPE)) for h in range(MH)))


def _perm_gathered(g, segs, width):
    per = g.shape[-1]
    parts, total = [], 0
    for a, b in segs:
        c = a
        while c < b:
            j = c // per
            hi = min(b, (j + 1) * per)
            parts.append(g[j, :, c - j * per:hi - j * per])
            c = hi
        total += b - a
    if width > total:
        parts.append(jnp.zeros((g.shape[1], width - total), g.dtype))
    return jnp.concatenate(parts, axis=1)


def _scatter_perm(gp, segs, per):
    offs, o = [], 0
    for a, b in segs:
        offs.append((a, b, o))
        o += b - a
    blocks = []
    for j in range(N_DEV):
        lo, hi = j * per, (j + 1) * per
        pieces = []
        for a, b, o in sorted(offs):
            s0, s1 = max(a, lo), min(b, hi)
            if s0 < s1:
                pieces.append(gp[:, o + s0 - a:o + s1 - a])
        blocks.append(jnp.concatenate(pieces, axis=1))
    return jnp.stack(blocks)


def _perm_uq_cols(w):
    w3 = w.reshape(w.shape[:-1] + (MH, MNOPE + MROPE))
    return jnp.concatenate([w3[..., :MNOPE].reshape(w.shape[:-1] + (MH * MNOPE,)),
                            w3[..., MNOPE:].reshape(w.shape[:-1] + (MH * MROPE,))], axis=-1)


def _unperm_uq_cols(g):
    nope = g[..., :MH * MNOPE].reshape(g.shape[:-1] + (MH, MNOPE))
    rope = g[..., MH * MNOPE:].reshape(g.shape[:-1] + (MH, MROPE))
    return jnp.concatenate([nope, rope], axis=-1).reshape(g.shape[:-1] + (MQW,))


def _prep_layer_weights(w_in, w_out, w_uq, w_ukv, small):
    def vec(v):
        return v.reshape(1, -1).astype(F32)

    zeros = functools.partial(jnp.zeros, dtype=F32)
    wg_f, wg_b = small["gla_wg_f"].astype(F32), small["gla_wg_b"].astype(F32)
    wg_pad_f = jnp.concatenate([zeros((MROPE, GQK)), wg_f, zeros((LANES - MROPE - GRANK, GQK))], axis=0)
    wg_pad_b = jnp.concatenate([zeros((MROPE + GRANK, GQK)), wg_b, zeros((LANES - MROPE - 2 * GRANK, GQK))], axis=0)
    wt = dict(norm_g=vec(small["norm_g"]), wg_pad_f=wg_pad_f, wg_pad_b=wg_pad_b,
              bg_f=vec(small["gla_bg_f"]), bg_b=vec(small["gla_bg_b"]), gla_norm_g=vec(small["gla_norm_g"]),
              q_norm_g=vec(small["mla_q_norm_g"]), kv_norm_g=vec(small["mla_kv_norm_g"]),
              mla_out_g=vec(small["mla_out_g"]), conv_w=small["conv_w"].astype(F32),
              conv_out_g=vec(small["conv_out_g"]))
    for name, w in (("w_in", w_in), ("w_out", w_out), ("w_uq", w_uq), ("w_ukv", w_ukv)):
        if w is not None:
            wt[name] = w.astype(MXU)
    return wt


def _natural_small(gr):
    return dict(norm_g=gr["norm_g"][0],
                gla_wg_f=gr["wg_pad_f"][MROPE:MROPE + GRANK], gla_bg_f=gr["bg_f"][0],
                gla_wg_b=gr["wg_pad_b"][MROPE + GRANK:MROPE + 2 * GRANK], gla_bg_b=gr["bg_b"][0],
                gla_norm_g=gr["gla_norm_g"][0], mla_q_norm_g=gr["q_norm_g"][0], mla_kv_norm_g=gr["kv_norm_g"][0],
                mla_out_g=gr["mla_out_g"][0], conv_w=gr["conv_w"], conv_out_g=gr["conv_out_g"][0])


def _natural_grads(gr):
    return dict(_natural_small(gr), w_in=_unperm_in_cols(gr["w_in"]), w_out=gr["w_out"],
                mla_w_uq=_unperm_uq_cols(gr["w_uq"]), mla_w_ukv=gr["w_ukv"])


def _exchange(arrs, name, scatter, space):
    n = len(arrs)

    def body(*refs):
        ins, outs = refs[:n], refs[n:2 * n]
        send_sems, recv_sems, loc_sems = refs[2 * n:]
        ax, ay, ac = lax.axis_index("x"), lax.axis_index("y"), lax.axis_index("c")
        me = 4 * ax + 2 * ay + ac

        def src(a, to):
            return ins[a].at[to] if scatter else ins[a]

        def remote(a, r, dst_slot):
            px = 1 - ax if r & 4 else ax
            py = 1 - ay if r & 2 else ay
            pc = 1 - ac if r & 1 else ac
            return pltpu.make_async_remote_copy(
                src_ref=src(a, 4 * px + 2 * py + pc), dst_ref=outs[a].at[dst_slot(4 * px + 2 * py + pc)],
                send_sem=send_sems.at[a, r - 1], recv_sem=recv_sems.at[a, r - 1],
                device_id=(px, py, pc), device_id_type=MESH)

        locs = [pltpu.make_async_copy(src(a, me), outs[a].at[me], loc_sems.at[a]) for a in range(n)]
        for cp in locs:
            cp.start()
        sends = [remote(a, r, lambda peer: me) for r in range(1, N_DEV) for a in range(n)]
        for cp in sends:
            cp.start()
        for r in range(1, N_DEV):
            for a in range(n):
                remote(a, r, lambda peer: peer).wait_recv()
        for cp in sends:
            cp.wait_send()
        for cp in locs:
            cp.wait()

    def out_shape(a):
        return jax.ShapeDtypeStruct(a.shape if scatter else (N_DEV,) + a.shape, a.dtype)

    spec = pl.BlockSpec(memory_space=space)
    return pl.pallas_call(
        body, in_specs=[spec] * n, out_specs=[spec] * n, out_shape=[out_shape(a) for a in arrs],
        scratch_shapes=[pltpu.SemaphoreType.DMA((n, N_DEV - 1)), pltpu.SemaphoreType.DMA((n, N_DEV - 1)),
                        pltpu.SemaphoreType.DMA((n,))],
        name=name, compiler_params=pltpu.CompilerParams(vmem_limit_bytes=VMEM_LIMIT))(*arrs)


def _peer(r):
    ax, ay, ac = lax.axis_index("x"), lax.axis_index("y"), lax.axis_index("c")
    px = 1 - ax if r & 4 else ax
    py = 1 - ay if r & 2 else ay
    pc = 1 - ac if r & 1 else ac
    return (px, py, pc), 4 * px + 2 * py + pc


def _slot(rel_div):
    rel, div = rel_div
    idx = _peer(rel)[1]
    return idx if div == 1 else idx // div


AG_SPREAD = tuple((r, None, (0, 1), (r, 1)) for r in (1, 2, 4, 6))
AG_FORWARD = tuple((1, (k, 1), (k, 1), (1 ^ k, 1)) for k in (2, 4, 6))
RS_PAIR = tuple((1, (1 ^ k, 1), (1 ^ k, 2), (k, 2)) for k in (0, 2, 4, 6))
RS_CHIPS = tuple((r, (r, 2), (0, 2), (r, 2)) for r in (2, 4, 6))


def _plan_copies(plan, n, src_refs, land_refs, send_sems, recv_sems, arriving):
    out = []
    for i, (r, src, dst, recv) in enumerate(plan):
        peer = _peer(r)[0]
        for a in range(n):
            out.append(pltpu.make_async_remote_copy(
                src_ref=src_refs[a] if src is None else src_refs[a].at[_slot(src)],
                dst_ref=land_refs[a].at[_slot(recv if arriving else dst)],
                send_sem=send_sems.at[i * n + a], recv_sem=recv_sems.at[i * n + a],
                device_id=peer, device_id_type=MESH))
    return out


def _exchange_hbm(plan, srcs, lands, name, after=()):
    n = len(lands)
    fresh = isinstance(lands[0], jax.ShapeDtypeStruct)
    ins = ([] if srcs is None else list(srcs)) + ([] if fresh else list(lands))
    ns = 0 if srcs is None else n
    n_data = len(ins)
    ins = ins + list(after)

    def body(*refs):
        outs = refs[len(ins):len(ins) + n]
        send_sems, recv_sems = refs[-2:]
        src_refs = refs[:n] if srcs is not None else refs[ns:ns + n]
        sends = _plan_copies(plan, n, src_refs, outs, send_sems, recv_sems, False)
        for cp in sends:
            cp.start()
        for cp in _plan_copies(plan, n, src_refs, outs, send_sems, recv_sems, True):
            cp.wait_recv()
        for cp in sends:
            cp.wait_send()

    hbm = pl.BlockSpec(memory_space=pltpu.HBM)
    k = len(plan) * n
    return pl.pallas_call(
        body, name=name, in_specs=[hbm] * n_data + [pl.BlockSpec(memory_space=pl.ANY)] * len(after), out_specs=[hbm] * n,
        out_shape=[jax.ShapeDtypeStruct(a.shape, a.dtype) for a in lands],
        scratch_shapes=[pltpu.SemaphoreType.DMA((k,)), pltpu.SemaphoreType.DMA((k,))],
        input_output_aliases={} if fresh else {ns + i: i for i in range(n)},
        compiler_params=pltpu.CompilerParams(vmem_limit_bytes=VMEM_LIMIT))(*ins)


def _plan_start(plan, srcs, land_shapes, after, name):
    n = len(srcs)

    def body(*refs):
        src_refs, land_refs = refs[:n], refs[n:2 * n]
        send_sems, recv_sems = refs[2 * n + 1], refs[2 * n + 2]
        for cp in _plan_copies(plan, n, src_refs, land_refs, send_sems, recv_sems, False):
            cp.start()
        refs[-1][...] = jnp.zeros_like(refs[-1])

    hbm = pl.BlockSpec(memory_space=pltpu.HBM)
    sem = pl.BlockSpec(memory_space=pltpu.SEMAPHORE)
    k = len(plan) * n
    srcs = [pltpu.with_memory_space_constraint(a, pltpu.HBM) for a in srcs]
    lands = [pltpu.with_memory_space_constraint(lax.empty(shp, a.dtype), pltpu.HBM) for shp, a in zip(land_shapes, srcs)]
    res = pl.pallas_call(
        body, name=name,
        in_specs=[hbm] * (2 * n) + [pl.BlockSpec(memory_space=pl.ANY)],
        out_specs=[sem, sem] + [hbm] * (2 * n) + [pl.BlockSpec(memory_space=pltpu.VMEM)],
        out_shape=[pltpu.SemaphoreType.DMA((k,)), pltpu.SemaphoreType.DMA((k,))]
        + [pltpu.HBM(a.shape, a.dtype) for a in srcs] + [pltpu.HBM(shp, a.dtype) for shp, a in zip(land_shapes, srcs)]
        + [jax.ShapeDtypeStruct((8, LANES), F32)],
        input_output_aliases={i: 2 + i for i in range(2 * n)},
        compiler_params=pltpu.CompilerParams(has_side_effects=pltpu.SideEffectType.DATAFLOW_SIDE_EFFECTING),
    )(*srcs, *lands, after)
    return res[0], res[1], list(res[2:2 + n]), list(res[2 + n:2 + 2 * n]), res[-1]


def _plan_wait(plan, handle, after, name):
    send_sems, recv_sems, srcs, lands, _ = handle
    n = len(srcs)
    after = list(after)

    def body(*refs):
        src_refs, land_refs = refs[:n], refs[n:2 * n]
        ssem, rsem = refs[2 * n], refs[2 * n + 1]
        for cp in _plan_copies(plan, n, src_refs, land_refs, ssem, rsem, False):
            cp.wait_send()
        for cp in _plan_copies(plan, n, src_refs, land_refs, ssem, rsem, True):
            cp.wait_recv()

    hbm = pl.BlockSpec(memory_space=pltpu.HBM)
    sem = pl.BlockSpec(memory_space=pltpu.SEMAPHORE)
    res = pl.pallas_call(
        body, name=name,
        in_specs=[hbm] * (2 * n) + [sem, sem] + [pl.BlockSpec(memory_space=pl.ANY)] * len(after),
        out_specs=[hbm] * (2 * n),
        out_shape=[pltpu.HBM(a.shape, a.dtype) for a in srcs] + [pltpu.HBM(a.shape, a.dtype) for a in lands],
        input_output_aliases={i: i for i in range(2 * n)},
        compiler_params=pltpu.CompilerParams(has_side_effects=pltpu.SideEffectType.DATAFLOW_SIDE_EFFECTING),
    )(*srcs, *lands, send_sems, recv_sems, *after)
    return list(res[n:])


def _pair_sum(send, got, core, name):
    _, r, c = send.shape
    tr = 256 if r % 256 == 0 else r

    def body(core_ref, s_ref, g_ref, o_ref):
        o_ref[0] = (s_ref[0].astype(F32) + g_ref[0].astype(F32)).astype(o_ref.dtype)

    return pl.pallas_call(
        body, name=name,
        grid_spec=pltpu.PrefetchScalarGridSpec(
            num_scalar_prefetch=1, grid=(N_DEV // 2, r // tr),
            in_specs=[pl.BlockSpec((1, tr, c), lambda kc, i, core_ref: (2 * kc + core_ref[0], i, 0)),
                      pl.BlockSpec((1, tr, c), lambda kc, i, core_ref: (kc, i, 0))],
            out_specs=pl.BlockSpec((1, tr, c), lambda kc, i, core_ref: (kc, i, 0))),
        out_shape=jax.ShapeDtypeStruct((N_DEV // 2, r, c), send.dtype),
        compiler_params=_cp(("parallel", "parallel")))(core, send, got)


def _ada_mod(c_all, ada_w, ada_b_cols, name):
    nl, d, wc = ada_w.shape

    def body(c_ref, w_ref, b_ref, ca_ref, mod_ref):
        cv = c_ref[...]
        ca = cv * _sigmoid(cv)
        ca_ref[...] = ca
        mod_ref[0] = _dotf(ca, w_ref[0]) + b_ref[0]

    return pl.pallas_call(
        body, grid=(nl,),
        in_specs=[pl.BlockSpec((N_DEV, d), lambda l: (0, 0)), pl.BlockSpec((1, d, wc), lambda l: (l, 0, 0)),
                  pl.BlockSpec((1, 1, wc), lambda l: (l, 0, 0))],
        out_specs=[pl.BlockSpec((N_DEV, d), lambda l: (0, 0)), pl.BlockSpec((1, N_DEV, wc), lambda l: (l, 0, 0))],
        out_shape=[jax.ShapeDtypeStruct((N_DEV, d), F32), jax.ShapeDtypeStruct((nl, N_DEV, wc), F32)],
        name=name, compiler_params=_cp(("arbitrary",)))(c_all, ada_w, ada_b_cols)


def _adam(w, g, m, v):
    m2 = ADAM_B1 * m + (1.0 - ADAM_B1) * g
    v2 = ADAM_B2 * v + (1.0 - ADAM_B2) * (g * g)
    m_hat = m2 / (1.0 - ADAM_B1 ** ADAM_STEP)
    v_hat = v2 / (1.0 - ADAM_B2 ** ADAM_STEP)
    delta = -ADAM_LR * (m_hat / (jnp.sqrt(v_hat) + ADAM_EPS) + ADAM_WD * w)
    return delta, m2, v2


def _ada_grad_adam(c_act, d_mod, w, m, v, name):
    nl, d, wc = w.shape
    tk = min(512, d)

    def body(c_ref, dm_ref, w_ref, m_ref, v_ref, g_ref, dl_ref, m2_ref, v2_ref):
        g = _dotf_tn(c_ref[...], dm_ref[0])
        delta, m2, v2 = _adam(w_ref[0], g, m_ref[0], v_ref[0])
        g_ref[0], dl_ref[0], m2_ref[0], v2_ref[0] = g, delta, m2, v2

    blk = pl.BlockSpec((1, tk, wc), lambda l, i: (l, i, 0))
    shp = jax.ShapeDtypeStruct(w.shape, F32)
    return pl.pallas_call(
        body, grid=(nl, d // tk),
        in_specs=[pl.BlockSpec((N_DEV, tk), lambda l, i: (0, i)), pl.BlockSpec((1, N_DEV, wc), lambda l, i: (l, 0, 0)),
                  blk, blk, blk],
        out_specs=[blk] * 4, out_shape=[shp] * 4, name=name,
        compiler_params=_cp(("parallel", "parallel")))(c_act, d_mod, w, m, v)


def _adam_big(recv, w, m, v, layer, prev, name, after=()):
    nl, r, c = w.shape
    tr = 256 if r % 256 == 0 else r
    nparts = recv.shape[0]

    def body(rc_ref, w_ref, m_ref, v_ref, *rest):
        g_ref, dl_ref, m2_ref, v2_ref = rest[-4:]
        g = rc_ref[0].astype(F32)
        for d in range(1, nparts):
            g = g + rc_ref[d].astype(F32)
        delta, m2, v2 = _adam(w_ref[0], g, m_ref[0], v_ref[0])
        g_ref[0], dl_ref[0], m2_ref[0], v2_ref[0] = g, delta, m2, v2

    blk = pl.BlockSpec((1, tr, c), lambda i: (layer, i, 0))
    shp = jax.ShapeDtypeStruct(w.shape, F32)
    prev = () if prev is None else tuple(prev)
    return pl.pallas_call(
        body, grid=(r // tr,),
        in_specs=[pl.BlockSpec((nparts, tr, c), lambda i: (0, i, 0)), blk, blk, blk]
        + [pl.BlockSpec(memory_space=pl.ANY)] * (len(prev) + len(after)),
        out_specs=[blk] * 4, out_shape=[shp] * 4, name=name,
        input_output_aliases={4 + j: j for j in range(len(prev))},
        compiler_params=_cp(("parallel",)))(recv, w, m, v, *prev, *after)


def _sum_devices(gathered, name):
    _, r, c = gathered.shape

    def body(g_ref, o_ref):
        acc = g_ref[0]
        for d in range(1, N_DEV):
            acc = acc + g_ref[d]
        o_ref[...] = acc

    spec = pl.BlockSpec(memory_space=pltpu.VMEM)
    return pl.pallas_call(body, in_specs=[spec], out_specs=spec, out_shape=jax.ShapeDtypeStruct((r, c), F32),
                          name=name, compiler_params=pltpu.CompilerParams(vmem_limit_bytes=VMEM_LIMIT))(gathered)


def _adam_small(ws, gs, ms, vs, name):
    n = len(ws)

    def body(*refs):
        for i in range(n):
            w_ref, g_ref, m_ref, v_ref = (refs[k * n + i] for k in range(4))
            dl_ref, m2_ref, v2_ref = (refs[(4 + k) * n + i] for k in range(3))
            dl_ref[...], m2_ref[...], v2_ref[...] = _adam(w_ref[...], g_ref[...], m_ref[...], v_ref[...])

    spec = pl.BlockSpec(memory_space=pltpu.VMEM)
    shapes = [jax.ShapeDtypeStruct(w.shape, F32) for w in ws]
    res = pl.pallas_call(body, in_specs=[spec] * (4 * n), out_specs=[spec] * (3 * n), out_shape=shapes * 3, name=name,
                         compiler_params=pltpu.CompilerParams(vmem_limit_bytes=VMEM_LIMIT))(*ws, *gs, *ms, *vs)
    return res[:n], res[n:2 * n], res[2 * n:]


def _pack(parts):
    flat = jnp.concatenate([p.reshape(-1).astype(F32) for p in parts])
    assert flat.shape[0] % LANES == 0, flat.shape
    return flat.reshape(-1, LANES)


def _unpack(packed, shapes):
    flat = packed.reshape(-1)
    out, off = [], 0
    for shp in shapes:
        size = 1
        for dim in shp:
            size *= dim
        out.append(flat[off:off + size].reshape(shp))
        off += size
    return out


def _gather_cols(g, per):
    g = jnp.moveaxis(g, 0, -2)
    return g.reshape(g.shape[:-2] + (N_DEV * per,))


def _scatter_cols(g, per):
    return jnp.moveaxis(g.reshape(g.shape[:-1] + (N_DEV, per)), -2, 0)


def _my_cols(full, me, per):
    return lax.dynamic_slice_in_dim(full, me * per, per, axis=full.ndim - 1)


def kernel(x, c, positions, ada_w, ada_b, norm_g, w_in, gla_wg_f, gla_bg_f, gla_wg_b, gla_bg_b, gla_norm_g, mla_q_norm_g, mla_kv_norm_g, mla_w_uq, mla_w_ukv, mla_out_g, conv_w, conv_out_g, w_out, final_g, loss_target, m_ada_w, m_ada_b, m_norm_g, m_w_in, m_gla_wg_f, m_gla_bg_f, m_gla_wg_b, m_gla_bg_b, m_gla_norm_g, m_mla_q_norm_g, m_mla_kv_norm_g, m_mla_w_uq, m_mla_w_ukv, m_mla_out_g, m_conv_w, m_conv_out_g, m_w_out, m_final_g, v_ada_w, v_ada_b, v_norm_g, v_w_in, v_gla_wg_f, v_gla_bg_f, v_gla_wg_b, v_gla_bg_b, v_gla_norm_g, v_mla_q_norm_g, v_mla_kv_norm_g, v_mla_w_uq, v_mla_w_ukv, v_mla_out_g, v_conv_w, v_conv_out_g, v_w_out, v_final_g):
    me = 4 * lax.axis_index("x") + 2 * lax.axis_index("y") + lax.axis_index("c")
    nl = ada_w.shape[0]
    s, d = x.shape[1], x.shape[2]
    ada_cols = ada_w.shape[2]
    wgc, cwc = gla_wg_f.shape[2], conv_w.shape[2]

    (g0,) = _exchange([_pack([c, gla_wg_f, gla_wg_b, conv_w])], "gather_small_in", False, pltpu.VMEM)
    g0 = g0.reshape(N_DEV, -1)
    o1, o2, o3 = d, d + gla_wg_f.size, d + 2 * gla_wg_f.size
    c_all = g0[:, :o1]
    wgf_full = _gather_cols(g0[:, o1:o2].reshape((N_DEV,) + gla_wg_f.shape), wgc)
    wgb_full = _gather_cols(g0[:, o2:o3].reshape((N_DEV,) + gla_wg_b.shape), wgc)
    convw_full = _gather_cols(g0[:, o3:].reshape((N_DEV,) + conv_w.shape), cwc)

    ada_b_cols = _my_cols(ada_b, me, ada_cols).reshape(nl, 1, ada_cols)
    c_act, mod_cols = _ada_mod(c_all, ada_w, ada_b_cols, "ada_mod")
    (g1,) = _exchange([_pack([mod_cols])], "gather_mod", False, pltpu.VMEM)
    mod_all = g1.reshape(N_DEV, nl, N_DEV, ada_cols)
    mod_mine = _gather_cols(lax.dynamic_index_in_dim(mod_all, me, axis=2, keepdims=False), ada_cols)

    inv_freq = ROPE_THETA ** (-jnp.arange(0, MROPE, 2, dtype=F32) / MROPE)
    ang = positions[0].astype(F32)[:, None] * inv_freq
    cos, sin = jnp.tile(jnp.cos(ang), (1, LANES * 2 // MROPE)), jnp.tile(jnp.sin(ang), (1, LANES * 2 // MROPE))

    big = [w_in, w_out, mla_w_uq, mla_w_ukv]
    big_names = ["w_in", "w_out", "mla_w_uq", "mla_w_ukv"]

    def local_blocks(l):
        return [w[l].astype(MXU) for w in big]

    def put_own(lands, own):
        return [lax.dynamic_update_index_in_dim(ld, o, me, 0) for ld, o in zip(lands, own)]

    def layer_weights(l, gw_in=None, gw_out=None, gw_uq=None, gw_ukv=None):
        small = dict(norm_g=norm_g[l], gla_wg_f=wgf_full[l], gla_bg_f=gla_bg_f[l], gla_wg_b=wgb_full[l],
                     gla_bg_b=gla_bg_b[l], gla_norm_g=gla_norm_g[l], mla_q_norm_g=mla_q_norm_g[l],
                     mla_kv_norm_g=mla_kv_norm_g[l], mla_out_g=mla_out_g[l], conv_w=convw_full[l],
                     conv_out_g=conv_out_g[l])
        return _prep_layer_weights(
            None if gw_in is None else _perm_gathered(gw_in, IN_SEGS, PW),
            None if gw_out is None else gw_out.reshape((-1,) + gw_out.shape[2:]),
            None if gw_uq is None else _perm_gathered(gw_uq, UQ_SEGS, MQW),
            None if gw_ukv is None else _gather_cols(gw_ukv, mla_w_ukv.shape[2]), small)

    def land_shapes(blocks, slots):
        return [jax.ShapeDtypeStruct((slots,) + b.shape, b.dtype) for b in blocks]

    def slots_of(blocks):
        return [(N_DEV,) + b.shape for b in blocks]

    def forwarded(lands, blocks, tag):
        return put_own(_exchange_hbm(AG_FORWARD, None, lands, f"gather_{tag}_forward"), blocks)

    first = local_blocks(0)
    w_in_start = _plan_start(AG_SPREAD, first[:1], slots_of(first[:1]), mod_mine, "gather_w_in_l0_start")
    adam_w_in = [a + w_in_start[-1][0, 0] for a in (w_in, m_w_in, v_w_in)]
    lands = _plan_wait(AG_SPREAD, w_in_start, adam_w_in, "gather_w_in_l0_wait")
    (gw_in,) = forwarded(lands, first[:1], "w_in_l0")
    rest = _plan_start(AG_SPREAD, first[1:], slots_of(first[1:]), gw_in, "gather_rest_l0_start")
    h = x[0]
    saved, layers, mods = [], [], []
    pending = {}
    for l in range(nl):
        shift, scale, gate = (mod_mine[l, i * d:(i + 1) * d].reshape(1, d) for i in range(3))
        nxt = local_blocks(l + 1) if l + 1 < nl else None

        def start_next(after, wt_late, l=l, nxt=nxt):
            if nxt is not None:
                pending[l + 1] = _plan_start(AG_SPREAD, nxt, slots_of(nxt), after, f"gather_weights_l{l + 1}_start")
                wt_late["q_norm_g"] = layers[l]["q_norm_g"] + pending[l + 1][-1][0, 0]
            return wt_late

        if l == 0:
            in_after = (rest[-1],)
            layers.append(layer_weights(0, gw_in))

            def late(proj):
                got = forwarded(_plan_wait(AG_SPREAD, rest, [proj], "gather_rest_l0_wait"), first[1:], "rest_l0")
                full = layer_weights(0, None, *got)
                return start_next(got[0], {k: full[k] for k in ("w_out", "w_uq", "w_ukv")})
        else:
            got = forwarded(_plan_wait(AG_SPREAD, pending.pop(l), [h], f"gather_weights_l{l}_wait"), blocks, f"weights_l{l}")
            layers.append(layer_weights(l, *got))
            in_after = ()

            def late(proj):
                return start_next(proj, {})
        mods.append((shift, scale, gate))
        h, sv = _layer_fwd(h, mods[l], layers[l], cos, sin, f"l{l}", late, in_after)
        saved.append(sv)
        blocks = nxt
    loss_part, d_h, d_final_g = _final_loss(h, final_g.reshape(1, d), loss_target[0], "final_loss")
    loss = lax.psum(loss_part[0, 0], ("x", "y", "c"))
    shift, scale, gate = mods[-1]
    mods[-1] = (shift, scale, gate + 0.0 * loss)

    def grad_sends(gr):
        return [_scatter_perm(gr["w_in"], IN_SEGS, w_in.shape[2]).astype(MXU),
                gr["w_out"].reshape((N_DEV,) + w_out.shape[1:]).astype(MXU),
                _scatter_perm(gr["w_uq"], UQ_SEGS, mla_w_uq.shape[2]).astype(MXU),
                _scatter_cols(gr["w_ukv"], mla_w_ukv.shape[2]).astype(MXU)]

    my_chip = me // 2
    my_core = (me % 2).astype(jnp.int32).reshape(1)

    def chip_sums(gr, tag):
        sends = grad_sends(gr)
        got = _exchange_hbm(RS_PAIR, sends, land_shapes([sd[0] for sd in sends], N_DEV // 2), f"scatter_grads_{tag}_pair")
        return [_pair_sum(sd, gt, my_core, f"pair_sum_{n}_{tag}") for sd, gt, n in zip(sends, got, big_names)]

    def with_own_chip(lands, sums):
        return [lax.dynamic_update_index_in_dim(ld, lax.dynamic_index_in_dim(sm, my_chip, axis=0, keepdims=False),
                                                my_chip, 0) for ld, sm in zip(lands, sums)]

    small_names = ["norm_g", "gla_wg_f", "gla_bg_f", "gla_wg_b", "gla_bg_b", "gla_norm_g", "mla_q_norm_g",
                   "mla_kv_norm_g", "mla_out_g", "conv_w", "conv_out_g"]
    d_mods, grads, recv = [None] * nl, [None] * nl, [None] * nl
    flight = {}
    small = {}

    def gather_small(d_x, d_mod0, gr0):
        d_mods[0], grads[0] = d_mod0, _natural_small(gr0)
        d_mod_mine = jnp.stack([jnp.concatenate(d_mods[l], axis=-1)[0] for l in range(nl)])
        parts = [d_mod_mine] + [jnp.stack([grads[l][n] for l in range(nl)]) for n in small_names] + [d_final_g]
        (g2,) = _exchange([_pack(parts)], "gather_small_grads", False, pltpu.VMEM)
        small["d_mod_all"] = g2.reshape(N_DEV, -1)[:, :d_mod_mine.size].reshape(N_DEV, nl, 3 * d)
        small["summed"] = dict(zip(["ada_b"] + small_names + ["final_g"],
                                   _unpack(_sum_devices(g2, "sum_small_grads"), [p.shape for p in parts])))
        return (g2,)

    for l in reversed(range(nl)):
        def ship(big_grads, l=l):
            if l + 1 in flight:
                pend, sm = flight.pop(l + 1)
                recv[l + 1] = with_own_chip(_plan_wait(RS_CHIPS, pend, list(big_grads.values()),
                                                       f"scatter_grads_l{l + 1}_wait"), sm)
            sm = chip_sums(big_grads, f"l{l}")
            after = recv[l + 1][0] if l + 1 < nl else big_grads["w_in"]
            flight[l] = (_plan_start(RS_CHIPS, sm, [a.shape for a in sm], after, f"scatter_grads_l{l}_start"), sm)
            return (flight[l][0][-1],)

        if l > 0:
            d_h, d_mods[l], gr = _layer_bwd(d_h, saved[l], mods[l], layers[l], cos, sin, f"l{l}", ship)
            grads[l] = _natural_small(gr)
        else:
            d_h, _, _ = _layer_bwd(d_h, saved[l], mods[l], layers[l], cos, sin, f"l{l}", ship, gather_small)
    pending, sums = flight.pop(0)
    grad_x = d_h[None]
    summed = small["summed"]
    summed["gla_wg_f"] = _my_cols(summed["gla_wg_f"], me, wgc)
    summed["gla_wg_b"] = _my_cols(summed["gla_wg_b"], me, wgc)
    summed["conv_w"] = _my_cols(summed["conv_w"], me, cwc)

    d_mod_cols = jnp.moveaxis(_my_cols(small["d_mod_all"], me, ada_cols), 0, 1) + pending[-1][0, 0]
    out = {}
    out["ada_w"] = _ada_grad_adam(c_act, d_mod_cols, ada_w, m_ada_w, v_ada_w, "ada_grad_adam")

    given = dict(ada_b=(ada_b, m_ada_b, v_ada_b), norm_g=(norm_g, m_norm_g, v_norm_g),
                 gla_wg_f=(gla_wg_f, m_gla_wg_f, v_gla_wg_f), gla_bg_f=(gla_bg_f, m_gla_bg_f, v_gla_bg_f),
                 gla_wg_b=(gla_wg_b, m_gla_wg_b, v_gla_wg_b), gla_bg_b=(gla_bg_b, m_gla_bg_b, v_gla_bg_b),
                 gla_norm_g=(gla_norm_g, m_gla_norm_g, v_gla_norm_g),
                 mla_q_norm_g=(mla_q_norm_g, m_mla_q_norm_g, v_mla_q_norm_g),
                 mla_kv_norm_g=(mla_kv_norm_g, m_mla_kv_norm_g, v_mla_kv_norm_g),
                 mla_out_g=(mla_out_g, m_mla_out_g, v_mla_out_g), conv_w=(conv_w, m_conv_w, v_conv_w),
                 conv_out_g=(conv_out_g, m_conv_out_g, v_conv_out_g), final_g=(final_g, m_final_g, v_final_g))
    names = list(given)

    def two_d(a):
        return a.reshape(1, -1) if a.ndim == 1 else a

    g_nat = [summed[n].reshape(given[n][0].shape) for n in names]
    res = _adam_small([two_d(given[n][0]) for n in names], [two_d(g) for g in g_nat],
                      [two_d(given[n][1]) for n in names], [two_d(given[n][2]) for n in names], "adam_small")
    for i, n in enumerate(names):
        out[n] = (g_nat[i],) + tuple(r[i].reshape(given[n][0].shape) for r in res)

    state = dict(w_in=adam_w_in, w_out=(w_out, m_w_out, v_w_out), mla_w_uq=(mla_w_uq, m_mla_w_uq, v_mla_w_uq),
                 mla_w_ukv=(mla_w_ukv, m_mla_w_ukv, v_mla_w_ukv))
    done = [out["ada_w"][0], res[0][0]]
    for l in reversed(range(nl)):
        if l == 0:
            recv[0] = with_own_chip(_plan_wait(RS_CHIPS, pending, done, "scatter_grads_l0_wait"), sums)
        for i, n in enumerate(big_names):
            out[n] = _adam_big(recv[l][i], *state[n], l, out.get(n), f"adam_{n}_l{l}",
                               (pending[-1],))
        done = done + [out[n][0] for n in big_names]

    order = ["ada_w", "ada_b", "norm_g", "w_in", "gla_wg_f", "gla_bg_f", "gla_wg_b", "gla_bg_b", "gla_norm_g",
             "mla_q_norm_g", "mla_kv_norm_g", "mla_w_uq", "mla_w_ukv", "mla_out_g", "conv_w", "conv_out_g", "w_out",
             "final_g"]
    return (loss, grad_x, *[out[n][0] for n in order], *[out[n][1] for n in order], *[out[n][2] for n in order],
            *[out[n][3] for n in order])
```

```python
import functools

import jax
import jax.numpy as jnp
from jax import lax
from jax.experimental import pallas as pl
from jax.experimental.pallas import tpu as pltpu

F32 = jnp.float32
MXU = jnp.bfloat16
HI = lax.Precision.HIGHEST
N_DEV = 8
MESH = pl.DeviceIdType.MESH

D_MIX = 2048
GH, GDK, GDV = 6, 64, 128
GW = GH * GDV
GQK = GH * GDK
GRANK = 16
GTEMP = 16.0
CHUNK = 64
MH, MQL, MKVL, MNOPE, MROPE, MDV = 6, 384, 256, 128, 64, 128
MW = MH * MDV
MQW = MH * (MNOPE + MROPE)
MKVW = MH * (MNOPE + MDV)
CONV_CH = 512
ROPE_THETA = 10000.0
EPS = 1e-6
IN_DIM = 5856
OZ, OCB, OCC, OCX, OMKV, OGV, OGQ, OGK, OMQ, OT = 0, 2048, 2560, 3072, 3584, 3840, 4608, 4992, 5376, 5760
PW = 5888
LANES = 128
VMEM_LIMIT = 56 * 1024 * 1024

ADAM_LR, ADAM_B1, ADAM_B2, ADAM_EPS, ADAM_WD, ADAM_STEP = 0.001, 0.9, 0.999, 1e-08, 0.01, 10


def _cp(sem=None):
    return pltpu.CompilerParams(dimension_semantics=sem, vmem_limit_bytes=VMEM_LIMIT)


def _dot(a, b):
    return jnp.dot(a.astype(MXU), b.astype(MXU), preferred_element_type=F32)


def _dot_nt(a, b):
    return lax.dot_general(a.astype(MXU), b.astype(MXU), (((1,), (1,)), ((), ())), preferred_element_type=F32)


def _dot_tn(a, b):
    return lax.dot_general(a.astype(MXU), b.astype(MXU), (((0,), (0,)), ((), ())), preferred_element_type=F32)


def _dotf(a, b):
    return jnp.dot(a, b, precision=HI, preferred_element_type=F32)


def _dotf_nt(a, b):
    return lax.dot_general(a, b, (((1,), (1,)), ((), ())), precision=HI, preferred_element_type=F32)


def _dotf_tn(a, b):
    return lax.dot_general(a, b, (((0,), (0,)), ((), ())), precision=HI, preferred_element_type=F32)


def _split3(x):
    hi = x.astype(jnp.bfloat16)
    r1 = x - hi.astype(F32)
    mid = r1.astype(jnp.bfloat16)
    lo = (r1 - mid.astype(F32)).astype(jnp.bfloat16)
    return hi, mid, lo


def _cum_dot(cum, x, transpose=False):
    dn = (((0,), (0,)), ((), ())) if transpose else (((1,), (0,)), ((), ()))
    cb = cum.astype(jnp.bfloat16)
    parts = [lax.dot_general(cb, p, dn, preferred_element_type=F32) for p in _split3(x)]
    return parts[0] + parts[1] + parts[2]


def _rows(s):
    return min(256, s)


def _rms(x, g):
    r = lax.rsqrt(jnp.mean(x * x, axis=-1, keepdims=True) + EPS)
    return x * r * g


def _rms_bwd(dy, x, g):
    r = lax.rsqrt(jnp.mean(x * x, axis=-1, keepdims=True) + EPS)
    xh = x * r
    dxh = dy * g
    dg = jnp.sum(dy * xh, axis=0, keepdims=True)
    dx = r * (dxh - xh * jnp.mean(dxh * xh, axis=-1, keepdims=True))
    return dx, dg


def _sigmoid(z):
    return 1.0 / (1.0 + jnp.exp(-z))


def _matmul(a, b, *, dims, tm, tn, tk, out_dtypes, name, epilogue=None, extras=(), extra_kinds=(), after=()):
    if dims == "nn":
        (m, k), n, mul = a.shape, b.shape[1], _dot
    elif dims == "nt":
        (m, k), n, mul = a.shape, b.shape[0], _dot_nt
    else:
        (k, m), n, mul = a.shape, b.shape[1], _dot_tn
    tm, tn, tk = min(tm, m), min(tn, n), min(tk, k)
    assert m % tm == 0 and n % tn == 0 and k % tk == 0, (m, n, k, tm, tn, tk)
    if dims == "nn":
        a_spec = pl.BlockSpec((tm, tk), lambda i, j, kk: (i, kk))
        b_spec = pl.BlockSpec((tk, tn), lambda i, j, kk: (kk, j))
    elif dims == "nt":
        a_spec = pl.BlockSpec((tm, tk), lambda i, j, kk: (i, kk))
        b_spec = pl.BlockSpec((tn, tk), lambda i, j, kk: (j, kk))
    else:
        a_spec = pl.BlockSpec((tk, tm), lambda i, j, kk: (kk, i))
        b_spec = pl.BlockSpec((tk, tn), lambda i, j, kk: (kk, j))
    nk = k // tk
    n_extra = len(extras)
    n_out = len(out_dtypes)
    n_after = len(after)
    extra_specs = []
    for kind in extra_kinds:
        if kind == "mn":
            extra_specs.append(pl.BlockSpec((tm, tn), lambda i, j, kk: (i, j)))
        else:
            extra_specs.append(pl.BlockSpec((1, tn), lambda i, j, kk: (0, j)))

    def finish(res, ex, outs):
        vals = (res,) if epilogue is None else epilogue(res, *[e[...] for e in ex])
        for o, v in zip(outs, vals):
            o[...] = v.astype(o.dtype)

    def body(*refs):
        a_ref, b_ref = refs[0], refs[1]
        ex = refs[2:2 + n_extra]
        outs = refs[2 + n_extra + n_after:2 + n_extra + n_after + n_out]
        if nk == 1:
            finish(mul(a_ref[...], b_ref[...]), ex, outs)
            return
        acc = refs[-1]
        kk = pl.program_id(2)

        @pl.when(kk == 0)
        def _():
            acc[...] = jnp.zeros_like(acc)

        acc[...] += mul(a_ref[...], b_ref[...])

        @pl.when(kk == nk - 1)
        def _():
            finish(acc[...], ex, outs)

    out_spec = pl.BlockSpec((tm, tn), lambda i, j, kk: (i, j))
    res = pl.pallas_call(
        body, grid=(m // tm, n // tn, nk),
        in_specs=[a_spec, b_spec] + extra_specs + [pl.BlockSpec(memory_space=pl.ANY)] * n_after,
        out_specs=[out_spec] * n_out,
        out_shape=[jax.ShapeDtypeStruct((m, n), dt) for dt in out_dtypes],
        scratch_shapes=[] if nk == 1 else [pltpu.VMEM((tm, tn), F32)],
        name=name, compiler_params=_cp(("parallel", "parallel", "arbitrary")),
    )(a, b, *extras, *after)
    return res


def _norm_mod(x, g, scale, shift, name):
    s, d = x.shape
    tr = _rows(s)

    def body(x_ref, g_ref, sc_ref, sh_ref, h_ref):
        h = _rms(x_ref[...], g_ref[...]) * (1.0 + sc_ref[...]) + sh_ref[...]
        h_ref[...] = h.astype(h_ref.dtype)

    row = pl.BlockSpec((tr, d), lambda i: (i, 0))
    vec = pl.BlockSpec((1, d), lambda i: (0, 0))
    return pl.pallas_call(body, grid=(s // tr,), in_specs=[row, vec, vec, vec], out_specs=row,
                          out_shape=jax.ShapeDtypeStruct((s, d), MXU), name=name,
                          compiler_params=_cp(("parallel",)))(x, g, scale, shift)


def _norm_mod_bwd(d_h, x, d_out, g, scale, name):
    s, d = x.shape
    tr = _rows(s)

    def body(dh_ref, x_ref, do_ref, g_ref, sc_ref, dx_ref, dsh_ref, dsc_ref, dg_ref):
        i = pl.program_id(0)

        @pl.when(i == 0)
        def _():
            dsh_ref[...] = jnp.zeros_like(dsh_ref)
            dsc_ref[...] = jnp.zeros_like(dsc_ref)
            dg_ref[...] = jnp.zeros_like(dg_ref)

        dh = dh_ref[...]
        xv = x_ref[...]
        gv = g_ref[...]
        r = lax.rsqrt(jnp.mean(xv * xv, axis=-1, keepdims=True) + EPS)
        xh = xv * r
        dsh_ref[...] += jnp.sum(dh, axis=0, keepdims=True)
        dsc_ref[...] += jnp.sum(dh * (xh * gv), axis=0, keepdims=True)
        dhn = dh * (1.0 + sc_ref[...])
        dg_ref[...] += jnp.sum(dhn * xh, axis=0, keepdims=True)
        dxh = dhn * gv
        dx_ref[...] = do_ref[...] + r * (dxh - xh * jnp.mean(dxh * xh, axis=-1, keepdims=True))

    row = pl.BlockSpec((tr, d), lambda i: (i, 0))
    vec = pl.BlockSpec((1, d), lambda i: (0, 0))
    vshape = jax.ShapeDtypeStruct((1, d), F32)
    return pl.pallas_call(body, grid=(s // tr,), in_specs=[row, row, row, vec, vec],
                          out_specs=[row, vec, vec, vec],
                          out_shape=[jax.ShapeDtypeStruct((s, d), F32), vshape, vshape, vshape],
                          name=name, compiler_params=_cp(("arbitrary",)))(d_h, x, d_out, g, scale)


def _gate_bwd(d_out, u, gate, name):
    s, d = d_out.shape
    tr = _rows(s)

    def body(do_ref, u_ref, gt_ref, du_ref, dgt_ref):
        @pl.when(pl.program_id(0) == 0)
        def _():
            dgt_ref[...] = jnp.zeros_like(dgt_ref)

        do = do_ref[...]
        du_ref[...] = (do * gt_ref[...]).astype(du_ref.dtype)
        dgt_ref[...] += jnp.sum(do * u_ref[...], axis=0, keepdims=True)

    row = pl.BlockSpec((tr, d), lambda i: (i, 0))
    vec = pl.BlockSpec((1, d), lambda i: (0, 0))
    return pl.pallas_call(body, grid=(s // tr,), in_specs=[row, row, vec], out_specs=[row, vec],
                          out_shape=[jax.ShapeDtypeStruct((s, d), MXU), jax.ShapeDtypeStruct((1, d), F32)],
                          name=name, compiler_params=_cp(("arbitrary",)))(d_out, u, gate)


def _final_loss(x, g, target, name):
    s, d = x.shape
    tr = _rows(s)

    def body(x_ref, g_ref, t_ref, loss_ref, dx_ref, dg_ref):
        @pl.when(pl.program_id(0) == 0)
        def _():
            loss_ref[...] = jnp.zeros_like(loss_ref)
            dg_ref[...] = jnp.zeros_like(dg_ref)

        xv = x_ref[...]
        gv = g_ref[...]
        diff = _rms(xv, gv) - t_ref[...]
        part = 0.5 * jnp.sum(jnp.sum(diff * diff, axis=-1, keepdims=True) / d, axis=0, keepdims=True)
        loss_ref[...] += jnp.broadcast_to(part, loss_ref.shape)
        dx, dg = _rms_bwd(diff / d, xv, gv)
        dx_ref[...] = dx
        dg_ref[...] += dg

    row = pl.BlockSpec((tr, d), lambda i: (i, 0))
    vec = pl.BlockSpec((1, d), lambda i: (0, 0))
    lvec = pl.BlockSpec((1, LANES), lambda i: (0, 0))
    return pl.pallas_call(body, grid=(s // tr,), in_specs=[row, vec, row], out_specs=[lvec, row, vec],
                          out_shape=[jax.ShapeDtypeStruct((1, LANES), F32), jax.ShapeDtypeStruct((s, d), F32),
                                     jax.ShapeDtypeStruct((1, d), F32)],
                          name=name, compiler_params=_cp(("arbitrary",)))(x, g, target)


def _shift_rows(u, s, down):
    ri = lax.broadcasted_iota(jnp.int32, u.shape, 0)
    if down:
        return jnp.where(ri == 0, 0.0, pltpu.roll(u, 1, 0))
    return jnp.where(ri == s - 1, 0.0, pltpu.roll(u, s - 1, 0))


def _conv_fwd(proj, conv_w, name):
    s = proj.shape[0]
    nt = CONV_CH // LANES

    def body(cb_ref, cc_ref, cx_ref, w_ref, pre_ref):
        u = cc_ref[...] * cx_ref[...]
        conv = _shift_rows(u, s, True) * w_ref[0:1, :] + u * w_ref[1:2, :] + _shift_rows(u, s, False) * w_ref[2:3, :]
        pre_ref[...] = cb_ref[...] * conv

    def col(off):
        return pl.BlockSpec((s, LANES), lambda j: (0, off // LANES + j))

    return pl.pallas_call(body, grid=(nt,), in_specs=[col(OCB), col(OCC), col(OCX), pl.BlockSpec((3, LANES), lambda j: (0, j))],
                          out_specs=pl.BlockSpec((s, LANES), lambda j: (0, j)),
                          out_shape=jax.ShapeDtypeStruct((s, CONV_CH), F32), name=name,
                          compiler_params=_cp(("parallel",)))(proj, proj, proj, conv_w)


def _conv_bwd(proj, conv_w, d_pre, name):
    s = proj.shape[0]
    nt = CONV_CH // LANES

    def body(cb_ref, cc_ref, cx_ref, w_ref, dp_ref, dcb_ref, dcc_ref, dcx_ref, dw_ref):
        cc, cx = cc_ref[...], cx_ref[...]
        u = cc * cx
        up, dn = _shift_rows(u, s, True), _shift_rows(u, s, False)
        w0, w1, w2 = w_ref[0:1, :], w_ref[1:2, :], w_ref[2:3, :]
        conv = up * w0 + u * w1 + dn * w2
        dp = dp_ref[...]
        dcb_ref[...] = dp * conv
        dconv = dp * cb_ref[...]
        du = _shift_rows(dconv, s, False) * w0 + dconv * w1 + _shift_rows(dconv, s, True) * w2
        dcc_ref[...] = du * cx
        dcx_ref[...] = du * cc
        dw_ref[0:1, :] = jnp.sum(dconv * up, axis=0, keepdims=True)
        dw_ref[1:2, :] = jnp.sum(dconv * u, axis=0, keepdims=True)
        dw_ref[2:3, :] = jnp.sum(dconv * dn, axis=0, keepdims=True)

    def col(off):
        return pl.BlockSpec((s, LANES), lambda j: (0, off // LANES + j))

    blk = pl.BlockSpec((s, LANES), lambda j: (0, j))
    wblk = pl.BlockSpec((3, LANES), lambda j: (0, j))
    full = jax.ShapeDtypeStruct((s, CONV_CH), F32)
    return pl.pallas_call(body, grid=(nt,), in_specs=[col(OCB), col(OCC), col(OCX), wblk, blk],
                          out_specs=[blk, blk, blk, wblk],
                          out_shape=[full, full, full, jax.ShapeDtypeStruct((3, CONV_CH), F32)],
                          name=name, compiler_params=_cp(("parallel",)))(proj, proj, proj, conv_w, d_pre)


GLA_SUB = 8


def _gla_gates(t_ref, wg_ref, bg_ref):
    t = t_ref[...]
    a = _dot(t, wg_ref[...]) + bg_ref[...]
    la = (jnp.minimum(a, 0.0) - jnp.log(1.0 + jnp.exp(-jnp.abs(a)))) / GTEMP
    return t, a, la


def _gla_masks(reverse):
    ri = lax.broadcasted_iota(jnp.int32, (CHUNK, CHUNK), 0)
    ci = lax.broadcasted_iota(jnp.int32, (CHUNK, CHUNK), 1)
    if reverse:
        cum, mask = ci >= ri, ci > ri
    else:
        cum, mask = ci <= ri, ci <= ri
    return cum.astype(F32), mask


def _gla_specs(s, reverse):
    nsub = min(GLA_SUB, s // CHUNK)
    nsteps = s // (CHUNK * nsub)

    def row(n):
        return nsteps - 1 - n if reverse else n

    def chunk(pi):
        return nsub - 1 - pi if reverse else pi

    return nsub, nsteps, row, chunk


def _gla_fwd(proj, wg_pad, bg, reverse, name):
    s = proj.shape[0]
    nsub, nsteps, row, chunk = _gla_specs(s, reverse)
    rb = nsub * CHUNK

    def body(q_ref, k_ref, v_ref, t_ref, wg_ref, bg_ref, o_ref, st_ref, state):
        @pl.when(pl.program_id(0) == 0)
        def _():
            state[...] = jnp.zeros_like(state)

        _, _, la = _gla_gates(t_ref, wg_ref, bg_ref)
        cumf, mask = _gla_masks(reverse)
        lane = lax.broadcasted_iota(jnp.int32, (CHUNK, LANES), 1)
        for pi in range(nsub):
            rows = slice(chunk(pi) * CHUNK, (chunk(pi) + 1) * CHUNK)
            la_c = la[rows]
            b = _cum_dot(cumf, la_c)
            bl = jnp.sum(la_c, axis=0, keepdims=True)
            q = q_ref[rows, :] * (GDK ** -0.5)
            k = k_ref[rows, :]
            qd = q * jnp.exp(b)
            ki = k * jnp.exp(-b)
            kte = k * jnp.exp(bl - b)
            decay = jnp.exp(bl)
            for h in range(GH):
                p = h // 2
                sl = slice(p * LANES, (p + 1) * LANES)
                lm = (lane < GDK) if h % 2 == 0 else (lane >= GDK)
                qd_h = jnp.where(lm, qd[:, sl], 0.0)
                kte_h = jnp.where(lm, kte[:, sl], 0.0)
                v_h = v_ref[rows, h * GDV:(h + 1) * GDV]
                st = state[h]
                a_mat = jnp.where(mask, _dot_nt(qd_h, ki[:, sl]), 0.0)
                o_ref[rows, h * GDV:(h + 1) * GDV] = _dot(a_mat, v_h) + _dot_nt(qd_h, st)
                st_ref[pi, h] = st
                state[h] = st * decay[:, sl] + _dot_tn(v_h, kte_h)

    return pl.pallas_call(
        body, grid=(nsteps,),
        in_specs=[pl.BlockSpec((rb, GQK), lambda n: (row(n), OGQ // GQK)),
                  pl.BlockSpec((rb, GQK), lambda n: (row(n), OGK // GQK)),
                  pl.BlockSpec((rb, GW), lambda n: (row(n), OGV // GW)),
                  pl.BlockSpec((rb, LANES), lambda n: (row(n), OT // LANES)),
                  pl.BlockSpec((LANES, GQK), lambda n: (0, 0)),
                  pl.BlockSpec((1, GQK), lambda n: (0, 0))],
        out_specs=[pl.BlockSpec((rb, GW), lambda n: (row(n), 0)),
                   pl.BlockSpec((nsub, GH, GDV, LANES), lambda n: (n, 0, 0, 0))],
        out_shape=[jax.ShapeDtypeStruct((s, GW), F32), jax.ShapeDtypeStruct((s // CHUNK, GH, GDV, LANES), F32)],
        scratch_shapes=[pltpu.VMEM((GH, GDV, LANES), F32)],
        name=name, compiler_params=_cp(("arbitrary",)))(proj, proj, proj, proj, wg_pad, bg)


def _gla_bwd(proj, wg_pad, bg, states, d_o, reverse, name):
    s = proj.shape[0]
    nsub, nsteps, row, chunk = _gla_specs(s, reverse)
    rb = nsub * CHUNK

    def body(q_ref, k_ref, v_ref, t_ref, wg_ref, bg_ref, st_ref, do_ref,
             dq_ref, dk_ref, dv_ref, dt_ref, dwg_ref, dbg_ref, dstate, da_buf):
        @pl.when(pl.program_id(0) == 0)
        def _():
            dstate[...] = jnp.zeros_like(dstate)
            dwg_ref[...] = jnp.zeros_like(dwg_ref)
            dbg_ref[...] = jnp.zeros_like(dbg_ref)

        t, a, la = _gla_gates(t_ref, wg_ref, bg_ref)
        cumf, mask = _gla_masks(reverse)
        lane = lax.broadcasted_iota(jnp.int32, (CHUNK, LANES), 1)
        for pi in reversed(range(nsub)):
            rows = slice(chunk(pi) * CHUNK, (chunk(pi) + 1) * CHUNK)
            la_c = la[rows]
            b = _cum_dot(cumf, la_c)
            bl = jnp.sum(la_c, axis=0, keepdims=True)
            q = q_ref[rows, :] * (GDK ** -0.5)
            k = k_ref[rows, :]
            e, ei, ee = jnp.exp(b), jnp.exp(-b), jnp.exp(bl - b)
            qd, ki, kte = q * e, k * ei, k * ee
            decay = jnp.exp(bl)
            for p in range(GH // 2):
                sl = slice(p * LANES, (p + 1) * LANES)
                dqd = jnp.zeros((CHUNK, LANES), F32)
                dki = jnp.zeros((CHUNK, LANES), F32)
                dkte = jnp.zeros((CHUNK, LANES), F32)
                ddecay = jnp.zeros((1, LANES), F32)
                for half in range(2):
                    h = 2 * p + half
                    lm = (lane < GDK) if half == 0 else (lane >= GDK)
                    qd_h = jnp.where(lm, qd[:, sl], 0.0)
                    ki_h = jnp.where(lm, ki[:, sl], 0.0)
                    kte_h = jnp.where(lm, kte[:, sl], 0.0)
                    v_h = v_ref[rows, h * GDV:(h + 1) * GDV]
                    do_h = do_ref[rows, h * GDV:(h + 1) * GDV]
                    st = st_ref[pi, h]
                    dst = dstate[h]
                    a_mat = jnp.where(mask, _dot_nt(qd_h, ki_h), 0.0)
                    da_mat = jnp.where(mask, _dot_nt(do_h, v_h), 0.0)
                    dv_ref[rows, h * GDV:(h + 1) * GDV] = _dot_tn(a_mat, do_h) + _dot_nt(kte_h, dst)
                    dqd += _dot(da_mat, ki_h) + _dot(do_h, st)
                    dki += _dot_tn(da_mat, qd_h)
                    dkte += _dot(v_h, dst)
                    ddecay += jnp.sum(dst * st, axis=0, keepdims=True)
                    dstate[h] = dst * decay[:, sl] + _dot_tn(do_h, qd_h)
                dq_ref[rows, sl] = dqd * e[:, sl] * (GDK ** -0.5)
                dk_ref[rows, sl] = dki * ei[:, sl] + dkte * ee[:, sl]
                db = dqd * qd[:, sl] - dki * ki[:, sl] - dkte * kte[:, sl]
                dbl = jnp.sum(dkte * kte[:, sl], axis=0, keepdims=True) + decay[:, sl] * ddecay
                da_buf[rows, sl] = _cum_dot(cumf, db, True) + dbl
        da = da_buf[...] * (1.0 / GTEMP) * _sigmoid(-a)
        dt_ref[...] = _dot_nt(da, wg_ref[...])
        dwg_ref[...] += _dot_tn(t, da)
        dbg_ref[...] += jnp.sum(da, axis=0, keepdims=True)

    def prow(j):
        return row(nsteps - 1 - j)

    return pl.pallas_call(
        body, grid=(nsteps,),
        in_specs=[pl.BlockSpec((rb, GQK), lambda j: (prow(j), OGQ // GQK)),
                  pl.BlockSpec((rb, GQK), lambda j: (prow(j), OGK // GQK)),
                  pl.BlockSpec((rb, GW), lambda j: (prow(j), OGV // GW)),
                  pl.BlockSpec((rb, LANES), lambda j: (prow(j), OT // LANES)),
                  pl.BlockSpec((LANES, GQK), lambda j: (0, 0)),
                  pl.BlockSpec((1, GQK), lambda j: (0, 0)),
                  pl.BlockSpec((nsub, GH, GDV, LANES), lambda j: (nsteps - 1 - j, 0, 0, 0)),
                  pl.BlockSpec((rb, GW), lambda j: (prow(j), 0))],
        out_specs=[pl.BlockSpec((rb, GQK), lambda j: (prow(j), 0)),
                   pl.BlockSpec((rb, GQK), lambda j: (prow(j), 0)),
                   pl.BlockSpec((rb, GW), lambda j: (prow(j), 0)),
                   pl.BlockSpec((rb, LANES), lambda j: (prow(j), 0)),
                   pl.BlockSpec((LANES, GQK), lambda j: (0, 0)),
                   pl.BlockSpec((1, GQK), lambda j: (0, 0))],
        out_shape=[jax.ShapeDtypeStruct((s, GQK), F32), jax.ShapeDtypeStruct((s, GQK), F32),
                   jax.ShapeDtypeStruct((s, GW), F32), jax.ShapeDtypeStruct((s, LANES), F32),
                   jax.ShapeDtypeStruct((LANES, GQK), F32), jax.ShapeDtypeStruct((1, GQK), F32)],
        scratch_shapes=[pltpu.VMEM((GH, GDV, LANES), F32), pltpu.VMEM((rb, GQK), F32)],
        name=name, compiler_params=_cp(("arbitrary",)))(proj, proj, proj, proj, wg_pad, bg, states, d_o)


def _rot_half(x):
    lane = lax.broadcasted_iota(jnp.int32, x.shape, 1)
    first = (lane % MROPE) < (MROPE // 2)
    return jnp.where(first, -pltpu.roll(x, LANES - MROPE // 2, 1), pltpu.roll(x, MROPE // 2, 1))


def _mla_prep(proj, cos, sin, qg, kvg, w_uq, w_ukv, name):
    s = proj.shape[0]
    tr = _rows(s)

    def body(mq_ref, mkv_ref, t_ref, cos_ref, sin_ref, qg_ref, kvg_ref, wuq_ref, wukv_ref, q_ref, k_ref, v_ref):
        cosv, sinv = cos_ref[...], sin_ref[...]
        lane = lax.broadcasted_iota(jnp.int32, (tr, LANES), 1)

        def rope(xv):
            return xv * cosv + _rot_half(xv) * sinv

        qm = _dot(_rms(mq_ref[...], qg_ref[...]), wuq_ref[...])
        kv = _dot(_rms(mkv_ref[...], kvg_ref[...]), wukv_ref[...])
        kr_lo = jnp.where(lane < MROPE, rope(t_ref[...]), 0.0)
        kr_hi = pltpu.roll(kr_lo, MROPE, 1)
        for p in range(MH // 2):
            r = rope(qm[:, MW + p * LANES:MW + (p + 1) * LANES]).astype(q_ref.dtype)
            q_ref[2 * p, :, LANES:] = r
            q_ref[2 * p + 1, :, LANES:] = r
        for h in range(MH):
            q_ref[h, :, :LANES] = qm[:, h * LANES:(h + 1) * LANES].astype(q_ref.dtype)
            k_ref[h, :, :LANES] = kv[:, 2 * h * LANES:(2 * h + 1) * LANES].astype(k_ref.dtype)
            k_ref[h, :, LANES:] = (kr_lo if h % 2 == 0 else kr_hi).astype(k_ref.dtype)
            v_ref[h] = kv[:, (2 * h + 1) * LANES:(2 * h + 2) * LANES].astype(v_ref.dtype)

    def full(shape):
        return pl.BlockSpec(shape, lambda i: (0,) * len(shape))

    return pl.pallas_call(
        body, grid=(s // tr,),
        in_specs=[pl.BlockSpec((tr, MQL), lambda i: (i, OMQ // MQL)),
                  pl.BlockSpec((tr, MKVL), lambda i: (i, OMKV // MKVL)),
                  pl.BlockSpec((tr, LANES), lambda i: (i, OT // LANES)),
                  pl.BlockSpec((tr, LANES), lambda i: (i, 0)),
                  pl.BlockSpec((tr, LANES), lambda i: (i, 0)),
                  full((1, MQL)), full((1, MKVL)), full((MQL, MQW)), full((MKVL, MKVW))],
        out_specs=[pl.BlockSpec((MH, tr, 2 * LANES), lambda i: (0, i, 0)),
                   pl.BlockSpec((MH, tr, 2 * LANES), lambda i: (0, i, 0)),
                   pl.BlockSpec((MH, tr, LANES), lambda i: (0, i, 0))],
        out_shape=[jax.ShapeDtypeStruct((MH, s, 2 * LANES), MXU), jax.ShapeDtypeStruct((MH, s, 2 * LANES), MXU),
                   jax.ShapeDtypeStruct((MH, s, LANES), MXU)],
        name=name, compiler_params=_cp(("parallel",)))(proj, proj, proj, cos, sin, qg, kvg, w_uq, w_ukv)


def _mla_prep_bwd(proj, cos, sin, qg, kvg, w_uq, w_ukv, d_q, d_k, d_v, name):
    s = proj.shape[0]
    tr = _rows(s)

    def body(mq_ref, mkv_ref, cos_ref, sin_ref, qg_ref, kvg_ref, wuq_ref, wukv_ref, dq_ref, dk_ref, dv_ref,
             dmq_ref, dmkv_ref, dt_ref, dwuq_ref, dwukv_ref, dqg_ref, dkvg_ref):
        @pl.when(pl.program_id(0) == 0)
        def _():
            for r in (dwuq_ref, dwukv_ref, dqg_ref, dkvg_ref):
                r[...] = jnp.zeros_like(r)

        cosv, sinv = cos_ref[...], sin_ref[...]
        lane = lax.broadcasted_iota(jnp.int32, (tr, LANES), 1)
        lo = lane < MROPE

        def unrope(dv):
            return dv * cosv - _rot_half(dv * sinv)

        parts = [dq_ref[h, :, :LANES] for h in range(MH)]
        for p in range(MH // 2):
            parts.append(unrope(jnp.where(lo, dq_ref[2 * p, :, LANES:], dq_ref[2 * p + 1, :, LANES:])))
        d_qm = jnp.concatenate(parts, axis=1)
        mq, qgv = mq_ref[...], qg_ref[...]
        cq = _rms(mq, qgv)
        dwuq_ref[...] += _dot_tn(cq, d_qm)
        dmq, dqg = _rms_bwd(_dot_nt(d_qm, wuq_ref[...]), mq, qgv)
        dmq_ref[...] = dmq
        dqg_ref[...] += dqg

        parts = []
        for h in range(MH):
            parts += [dk_ref[h, :, :LANES], dv_ref[h]]
        d_kv = jnp.concatenate(parts, axis=1)
        mkv, kvgv = mkv_ref[...], kvg_ref[...]
        ckv = _rms(mkv, kvgv)
        dwukv_ref[...] += _dot_tn(ckv, d_kv)
        dmkv, dkvg = _rms_bwd(_dot_nt(d_kv, wukv_ref[...]), mkv, kvgv)
        dmkv_ref[...] = dmkv
        dkvg_ref[...] += dkvg

        even = dk_ref[0, :, LANES:] + dk_ref[2, :, LANES:] + dk_ref[4, :, LANES:]
        odd = dk_ref[1, :, LANES:] + dk_ref[3, :, LANES:] + dk_ref[5, :, LANES:]
        d_kr = jnp.where(lo, even, 0.0) + pltpu.roll(jnp.where(lo, 0.0, odd), MROPE, 1)
        dt_ref[...] = jnp.where(lo, unrope(d_kr), 0.0)

    def full(shape):
        return pl.BlockSpec(shape, lambda i: (0,) * len(shape))

    return pl.pallas_call(
        body, grid=(s // tr,),
        in_specs=[pl.BlockSpec((tr, MQL), lambda i: (i, OMQ // MQL)),
                  pl.BlockSpec((tr, MKVL), lambda i: (i, OMKV // MKVL)),
                  pl.BlockSpec((tr, LANES), lambda i: (i, 0)),
                  pl.BlockSpec((tr, LANES), lambda i: (i, 0)),
                  full((1, MQL)), full((1, MKVL)), full((MQL, MQW)), full((MKVL, MKVW)),
                  pl.BlockSpec((MH, tr, 2 * LANES), lambda i: (0, i, 0)),
                  pl.BlockSpec((MH, tr, 2 * LANES), lambda i: (0, i, 0)),
                  pl.BlockSpec((MH, tr, LANES), lambda i: (0, i, 0))],
        out_specs=[pl.BlockSpec((tr, MQL), lambda i: (i, 0)), pl.BlockSpec((tr, MKVL), lambda i: (i, 0)),
                   pl.BlockSpec((tr, LANES), lambda i: (i, 0)),
                   full((MQL, MQW)), full((MKVL, MKVW)), full((1, MQL)), full((1, MKVL))],
        out_shape=[jax.ShapeDtypeStruct((s, MQL), F32), jax.ShapeDtypeStruct((s, MKVL), F32),
                   jax.ShapeDtypeStruct((s, LANES), F32),
                   jax.ShapeDtypeStruct((MQL, MQW), F32), jax.ShapeDtypeStruct((MKVL, MKVW), F32),
                   jax.ShapeDtypeStruct((1, MQL), F32), jax.ShapeDtypeStruct((1, MKVL), F32)],
        name=name, compiler_params=_cp(("arbitrary",)))(proj, proj, cos, sin, qg, kvg, w_uq, w_ukv, d_q, d_k, d_v)


ATT_SCALE = (MNOPE + MROPE) ** -0.5
ATT_TQ_FWD, ATT_TQ = 256, 512


def _attn_fwd(q, k, v, name):
    s = q.shape[1]
    tq = min(ATT_TQ_FWD, s)

    def body(q_ref, k_ref, v_ref, o_ref, lse_ref):
        sc = _dot_nt(q_ref[0], k_ref[0]) * ATT_SCALE
        m = jnp.max(sc, axis=-1, keepdims=True)
        p = jnp.exp(sc - m)
        l = jnp.sum(p, axis=-1, keepdims=True)
        o_ref[...] = _dot(p, v_ref[0]) / l
        lse_ref[0] = m + jnp.log(l)

    return pl.pallas_call(
        body, grid=(MH, s // tq),
        in_specs=[pl.BlockSpec((1, tq, 2 * LANES), lambda h, i: (h, i, 0)),
                  pl.BlockSpec((1, s, 2 * LANES), lambda h, i: (h, 0, 0)),
                  pl.BlockSpec((1, s, LANES), lambda h, i: (h, 0, 0))],
        out_specs=[pl.BlockSpec((tq, LANES), lambda h, i: (i, h)),
                   pl.BlockSpec((1, tq, 1), lambda h, i: (h, i, 0))],
        out_shape=[jax.ShapeDtypeStruct((s, MW), F32), jax.ShapeDtypeStruct((MH, s, 1), F32)],
        name=name, compiler_params=_cp(("parallel", "parallel")))(q, k, v)


def _attn_bwd(q, k, v, o, lse, d_o, name):
    s = q.shape[1]
    tq = min(ATT_TQ, s)

    def body(q_ref, k_ref, v_ref, o_ref, lse_ref, do_ref, dq_ref, dk_ref, dv_ref):
        @pl.when(pl.program_id(1) == 0)
        def _():
            dk_ref[...] = jnp.zeros_like(dk_ref)
            dv_ref[...] = jnp.zeros_like(dv_ref)

        qv, kv, do = q_ref[0], k_ref[0], do_ref[...]
        p = jnp.exp(_dot_nt(qv, kv) * ATT_SCALE - lse_ref[0])
        delta = jnp.sum(do * o_ref[...], axis=-1, keepdims=True)
        ds = p * (_dot_nt(do, v_ref[0]) - delta) * ATT_SCALE
        dq_ref[0] = _dot(ds, kv)
        dk_ref[0] += _dot_tn(ds, qv)
        dv_ref[0] += _dot_tn(p, do)

    return pl.pallas_call(
        body, grid=(MH, s // tq),
        in_specs=[pl.BlockSpec((1, tq, 2 * LANES), lambda h, i: (h, i, 0)),
                  pl.BlockSpec((1, s, 2 * LANES), lambda h, i: (h, 0, 0)),
                  pl.BlockSpec((1, s, LANES), lambda h, i: (h, 0, 0)),
                  pl.BlockSpec((tq, LANES), lambda h, i: (i, h)),
                  pl.BlockSpec((1, tq, 1), lambda h, i: (h, i, 0)),
                  pl.BlockSpec((tq, LANES), lambda h, i: (i, h))],
        out_specs=[pl.BlockSpec((1, tq, 2 * LANES), lambda h, i: (h, i, 0)),
                   pl.BlockSpec((1, s, 2 * LANES), lambda h, i: (h, 0, 0)),
                   pl.BlockSpec((1, s, LANES), lambda h, i: (h, 0, 0))],
        out_shape=[jax.ShapeDtypeStruct((MH, s, 2 * LANES), F32), jax.ShapeDtypeStruct((MH, s, 2 * LANES), F32),
                   jax.ShapeDtypeStruct((MH, s, LANES), F32)],
        name=name, compiler_params=_cp(("parallel", "arbitrary")))(q, k, v, o, lse, d_o)


def _merge_fwd(o_f, o_b, o_att, pre, proj, gng, mog, cog, name):
    s = proj.shape[0]
    tr = _rows(s)

    def body(of_ref, ob_ref, oa_ref, pre_ref, z_ref, gng_ref, mog_ref, cog_ref, y_ref):
        z = z_ref[...]
        sz = z * _sigmoid(z)
        osum = of_ref[...] + ob_ref[...]
        gg = gng_ref[...]
        for h in range(GH):
            sl = slice(h * GDV, (h + 1) * GDV)
            y_ref[:, sl] = (_rms(osum[:, sl], gg) * sz[:, sl]).astype(y_ref.dtype)
        y_ref[:, GW:GW + MW] = (_rms(oa_ref[...], mog_ref[...]) * sz[:, GW:GW + MW]).astype(y_ref.dtype)
        y_ref[:, GW + MW:] = (_rms(pre_ref[...], cog_ref[...]) * sz[:, GW + MW:]).astype(y_ref.dtype)

    def row(w):
        return pl.BlockSpec((tr, w), lambda i: (i, 0))

    def vec(w):
        return pl.BlockSpec((1, w), lambda i: (0, 0))

    return pl.pallas_call(
        body, grid=(s // tr,),
        in_specs=[row(GW), row(GW), row(MW), row(CONV_CH), row(D_MIX), vec(GDV), vec(MW), vec(CONV_CH)],
        out_specs=row(D_MIX), out_shape=jax.ShapeDtypeStruct((s, D_MIX), MXU),
        name=name, compiler_params=_cp(("parallel",)))(o_f, o_b, o_att, pre, proj, gng, mog, cog)


def _merge_bwd(d_y, o_f, o_b, o_att, pre, proj, gng, mog, cog, name):
    s = proj.shape[0]
    tr = _rows(s)

    def body(dy_ref, of_ref, ob_ref, oa_ref, pre_ref, z_ref, gng_ref, mog_ref, cog_ref,
             dz_ref, dos_ref, doa_ref, dpre_ref, dgng_ref, dmog_ref, dcog_ref):
        @pl.when(pl.program_id(0) == 0)
        def _():
            for r in (dgng_ref, dmog_ref, dcog_ref):
                r[...] = jnp.zeros_like(r)

        z, dy = z_ref[...], dy_ref[...]
        sg = _sigmoid(z)
        sz = z * sg
        dsz = sg * (1.0 + z * (1.0 - sg))
        dcat = dy * sz
        dyz = dy * dsz
        osum = of_ref[...] + ob_ref[...]
        gg = gng_ref[...]
        dgg = jnp.zeros_like(gg)
        for h in range(GH):
            sl = slice(h * GDV, (h + 1) * GDV)
            dz_ref[:, sl] = dyz[:, sl] * _rms(osum[:, sl], gg)
            dx, dg = _rms_bwd(dcat[:, sl], osum[:, sl], gg)
            dos_ref[:, sl] = dx
            dgg += dg
        dgng_ref[...] += dgg
        sl = slice(GW, GW + MW)
        oa, mg = oa_ref[...], mog_ref[...]
        dz_ref[:, sl] = dyz[:, sl] * _rms(oa, mg)
        dx, dg = _rms_bwd(dcat[:, sl], oa, mg)
        doa_ref[...] = dx
        dmog_ref[...] += dg
        sl = slice(GW + MW, D_MIX)
        pv, cg = pre_ref[...], cog_ref[...]
        dz_ref[:, sl] = dyz[:, sl] * _rms(pv, cg)
        dx, dg = _rms_bwd(dcat[:, sl], pv, cg)
        dpre_ref[...] = dx
        dcog_ref[...] += dg

    def row(w):
        return pl.BlockSpec((tr, w), lambda i: (i, 0))

    def vec(w):
        return pl.BlockSpec((1, w), lambda i: (0, 0))

    def rs(w):
        return jax.ShapeDtypeStruct((s, w), F32)

    def vs(w):
        return jax.ShapeDtypeStruct((1, w), F32)

    return pl.pallas_call(
        body, grid=(s // tr,),
        in_specs=[row(D_MIX), row(GW), row(GW), row(MW), row(CONV_CH), row(D_MIX), vec(GDV), vec(MW), vec(CONV_CH)],
        out_specs=[row(D_MIX), row(GW), row(MW), row(CONV_CH), vec(GDV), vec(MW), vec(CONV_CH)],
        out_shape=[rs(D_MIX), rs(GW), rs(MW), rs(CONV_CH), vs(GDV), vs(MW), vs(CONV_CH)],
        name=name, compiler_params=_cp(("arbitrary",)))(d_y, o_f, o_b, o_att, pre, proj, gng, mog, cog)


def _assemble_dproj(d_z, d_cb, d_cc, d_cx, d_mkv, dv_f, dv_b, dq_f, dq_b, dk_f, dk_b, d_mq, dt_m, dt_f, dt_b, name):
    s = d_z.shape[0]
    tr = _rows(s)

    def body(dz, dcb, dcc, dcx, dmkv, dvf, dvb, dqf, dqb, dkf, dkb, dmq, dtm, dtf, dtb, out):
        dt = out.dtype
        out[:, OZ:OZ + D_MIX] = dz[...].astype(dt)
        out[:, OCB:OCB + CONV_CH] = dcb[...].astype(dt)
        out[:, OCC:OCC + CONV_CH] = dcc[...].astype(dt)
        out[:, OCX:OCX + CONV_CH] = dcx[...].astype(dt)
        out[:, OMKV:OMKV + MKVL] = dmkv[...].astype(dt)
        out[:, OGV:OGV + GW] = (dvf[...] + dvb[...]).astype(dt)
        out[:, OGQ:OGQ + GQK] = (dqf[...] + dqb[...]).astype(dt)
        out[:, OGK:OGK + GQK] = (dkf[...] + dkb[...]).astype(dt)
        out[:, OMQ:OMQ + MQL] = dmq[...].astype(dt)
        out[:, OT:OT + LANES] = (dtm[...] + dtf[...] + dtb[...]).astype(dt)

    args = (d_z, d_cb, d_cc, d_cx, d_mkv, dv_f, dv_b, dq_f, dq_b, dk_f, dk_b, d_mq, dt_m, dt_f, dt_b)
    return pl.pallas_call(
        body, grid=(s // tr,),
        in_specs=[pl.BlockSpec((tr, a.shape[1]), lambda i: (i, 0)) for a in args],
        out_specs=pl.BlockSpec((tr, PW), lambda i: (i, 0)),
        out_shape=jax.ShapeDtypeStruct((s, PW), MXU), name=name, compiler_params=_cp(("parallel",)))(*args)


def _layer_fwd(x, mod, wt, cos, sin, tag, late=None, in_after=()):
    shift, scale, gate = mod
    h = _norm_mod(x, wt["norm_g"], scale, shift, f"norm_mod_{tag}")
    (proj,) = _matmul(h, wt["w_in"], dims="nn", tm=2048, tn=256, tk=2048, out_dtypes=(F32,), name=f"in_proj_{tag}",
                      after=in_after)
    if late is not None:
        wt.update(late(proj))
    o_f, st_f = _gla_fwd(proj, wt["wg_pad_f"], wt["bg_f"], False, f"gla_fwd_f_{tag}")
    o_b, st_b = _gla_fwd(proj, wt["wg_pad_b"], wt["bg_b"], True, f"gla_fwd_b_{tag}")
    q, k, v = _mla_prep(proj, cos, sin, wt["q_norm_g"], wt["kv_norm_g"], wt["w_uq"], wt["w_ukv"], f"mla_prep_{tag}")
    o_att, lse = _attn_fwd(q, k, v, f"attn_fwd_{tag}")
    pre = _conv_fwd(proj, wt["conv_w"], f"conv_fwd_{tag}")
    y = _merge_fwd(o_f, o_b, o_att, pre, proj, wt["gla_norm_g"], wt["mla_out_g"], wt["conv_out_g"], f"merge_fwd_{tag}")
    x_new, u = _matmul(y, wt["w_out"], dims="nn", tm=2048, tn=256, tk=2048, out_dtypes=(F32, F32),
                       name=f"out_proj_{tag}", epilogue=lambda acc, xv, gv: (xv + gv * acc, acc),
                       extras=(x, gate), extra_kinds=("mn", "n"))
    saved = dict(x=x, h=h, proj=proj, o_f=o_f, o_b=o_b, st_f=st_f, st_b=st_b, q=q, k=k, v=v,
                 o_att=o_att, lse=lse, pre=pre, y=y, u=u)
    return x_new, saved


def _layer_bwd(d_out, sv, mod, wt, cos, sin, tag, ship=None, dx_first=None):
    shift, scale, gate = mod
    proj = sv["proj"]
    d_u, d_gate = _gate_bwd(d_out, sv["u"], gate, f"gate_bwd_{tag}")
    (g_w_out,) = _matmul(sv["y"], d_u, dims="tn", tm=1024, tn=512, tk=2048, out_dtypes=(MXU,), name=f"out_proj_dw_{tag}")
    (d_y,) = _matmul(d_u, wt["w_out"], dims="nt", tm=2048, tn=256, tk=2048, out_dtypes=(F32,), name=f"out_proj_dx_{tag}",
                     after=(g_w_out,))
    d_z, d_osum, d_oatt, d_pre, d_gng, d_mog, d_cog = _merge_bwd(
        d_y, sv["o_f"], sv["o_b"], sv["o_att"], sv["pre"], proj, wt["gla_norm_g"], wt["mla_out_g"], wt["conv_out_g"],
        f"merge_bwd_{tag}")
    d_cb, d_cc, d_cx, d_conv_w = _conv_bwd(proj, wt["conv_w"], d_pre, f"conv_bwd_{tag}")
    d_q, d_k, d_v = _attn_bwd(sv["q"], sv["k"], sv["v"], sv["o_att"], sv["lse"], d_oatt, f"attn_bwd_{tag}")
    d_mq, d_mkv, dt_m, g_w_uq, g_w_ukv, d_qg, d_kvg = _mla_prep_bwd(
        proj, cos, sin, wt["q_norm_g"], wt["kv_norm_g"], wt["w_uq"], wt["w_ukv"], d_q, d_k, d_v, f"mla_prep_bwd_{tag}")
    dq_f, dk_f, dv_f, dt_f, d_wg_f, d_bg_f = _gla_bwd(proj, wt["wg_pad_f"], wt["bg_f"], sv["st_f"], d_osum, False,
                                                     f"gla_bwd_f_{tag}")
    dq_b, dk_b, dv_b, dt_b, d_wg_b, d_bg_b = _gla_bwd(proj, wt["wg_pad_b"], wt["bg_b"], sv["st_b"], d_osum, True,
                                                     f"gla_bwd_b_{tag}")
    d_proj = _assemble_dproj(d_z, d_cb, d_cc, d_cx, d_mkv, dv_f, dv_b, dq_f, dq_b, dk_f, dk_b, d_mq, dt_m, dt_f, dt_b,
                             f"assemble_dproj_{tag}")
    grads = dict(w_out=g_w_out, w_uq=g_w_uq, w_ukv=g_w_ukv,
                 wg_pad_f=d_wg_f, bg_f=d_bg_f, wg_pad_b=d_wg_b, bg_b=d_bg_b, gla_norm_g=d_gng,
                 q_norm_g=d_qg, kv_norm_g=d_kvg, mla_out_g=d_mog, conv_w=d_conv_w, conv_out_g=d_cog)

    def in_dw(after):
        (g_w_in,) = _matmul(sv["h"], d_proj, dims="tn", tm=2048, tn=256, tk=2048, out_dtypes=(MXU,),
                            name=f"in_proj_dw_{tag}", after=after)
        grads["w_in"] = g_w_in
        return dict(w_in=g_w_in, w_out=g_w_out, w_uq=g_w_uq, w_ukv=g_w_ukv)

    def in_dx(after):
        (d_h,) = _matmul(d_proj, wt["w_in"], dims="nt", tm=1024, tn=512, tk=PW, out_dtypes=(F32,),
                         name=f"in_proj_dx_{tag}", after=after)
        d_x, d_shift, d_scale, d_ng = _norm_mod_bwd(d_h, sv["x"], d_out, wt["norm_g"], scale, f"norm_mod_bwd_{tag}")
        grads["norm_g"] = d_ng
        return d_x, (d_shift, d_scale, d_gate)

    if dx_first is None:
        big = in_dw(())
        d_x, d_mod = in_dx((big["w_in"],) if ship is None else ship(big))
    else:
        d_x, d_mod = in_dx(())
        big = in_dw(dx_first(d_x, d_mod, grads))
        ship(big)
    return d_x, d_mod, grads


def _perm_in_cols(w):
    pad = jnp.zeros(w.shape[:-1] + (PW - IN_DIM,), w.dtype)
    return jnp.concatenate([w[..., 3808:5856], w[..., 2272:3808], w[..., 1952:2208], w[..., 768:1536], w[..., 0:768],
                            w[..., 1568:1952], w[..., 2208:2272], w[..., 1536:1568], pad], axis=-1)


def _unperm_in_cols(g):
    return jnp.concatenate([g[..., OGQ:OGQ + 2 * GQK], g[..., OGV:OGV + GW], g[..., OT + MROPE:OT + MROPE + 2 * GRANK],
                            g[..., OMQ:OMQ + MQL], g[..., OMKV:OMKV + MKVL], g[..., OT:OT + MROPE],
                            g[..., OCB:OCB + 3 * CONV_CH], g[..., OZ:OZ + D_MIX]], axis=-1)


IN_SEGS = ((3808, 5856), (2272, 3808), (1952, 2208), (768, 1536), (0, 768), (1568, 1952), (2208, 2272), (1536, 1568))
UQ_SEGS = (tuple((h * (MNOPE + MROPE), h * (MNOPE + MROPE) + MNOPE) for h in range(MH))
           + tuple((h * (MNOPE + MROPE) + MNOPE, (h + 1) * (MNOPE + MROPE)) for h in range(MH)))


def _perm_gathered(g, segs, width):
    per = g.shape[-1]
    parts, total = [], 0
    for a, b in segs:
        c = a
        while c < b:
            j = c // per
            hi = min(b, (j + 1) * per)
            parts.append(g[j, :, c - j * per:hi - j * per])
            c = hi
        total += b - a
    if width > total:
        parts.append(jnp.zeros((g.shape[1], width - total), g.dtype))
    return jnp.concatenate(parts, axis=1)


def _scatter_perm(gp, segs, per):
    offs, o = [], 0
    for a, b in segs:
        offs.append((a, b, o))
        o += b - a
    blocks = []
    for j in range(N_DEV):
        lo, hi = j * per, (j + 1) * per
        pieces = []
        for a, b, o in sorted(offs):
            s0, s1 = max(a, lo), min(b, hi)
            if s0 < s1:
                pieces.append(gp[:, o + s0 - a:o + s1 - a])
        blocks.append(jnp.concatenate(pieces, axis=1))
    return jnp.stack(blocks)


def _perm_uq_cols(w):
    w3 = w.reshape(w.shape[:-1] + (MH, MNOPE + MROPE))
    return jnp.concatenate([w3[..., :MNOPE].reshape(w.shape[:-1] + (MH * MNOPE,)),
                            w3[..., MNOPE:].reshape(w.shape[:-1] + (MH * MROPE,))], axis=-1)


def _unperm_uq_cols(g):
    nope = g[..., :MH * MNOPE].reshape(g.shape[:-1] + (MH, MNOPE))
    rope = g[..., MH * MNOPE:].reshape(g.shape[:-1] + (MH, MROPE))
    return jnp.concatenate([nope, rope], axis=-1).reshape(g.shape[:-1] + (MQW,))


def _prep_layer_weights(w_in, w_out, w_uq, w_ukv, small):
    def vec(v):
        return v.reshape(1, -1).astype(F32)

    zeros = functools.partial(jnp.zeros, dtype=F32)
    wg_f, wg_b = small["gla_wg_f"].astype(F32), small["gla_wg_b"].astype(F32)
    wg_pad_f = jnp.concatenate([zeros((MROPE, GQK)), wg_f, zeros((LANES - MROPE - GRANK, GQK))], axis=0)
    wg_pad_b = jnp.concatenate([zeros((MROPE + GRANK, GQK)), wg_b, zeros((LANES - MROPE - 2 * GRANK, GQK))], axis=0)
    wt = dict(norm_g=vec(small["norm_g"]), wg_pad_f=wg_pad_f, wg_pad_b=wg_pad_b,
              bg_f=vec(small["gla_bg_f"]), bg_b=vec(small["gla_bg_b"]), gla_norm_g=vec(small["gla_norm_g"]),
              q_norm_g=vec(small["mla_q_norm_g"]), kv_norm_g=vec(small["mla_kv_norm_g"]),
              mla_out_g=vec(small["mla_out_g"]), conv_w=small["conv_w"].astype(F32),
              conv_out_g=vec(small["conv_out_g"]))
    for name, w in (("w_in", w_in), ("w_out", w_out), ("w_uq", w_uq), ("w_ukv", w_ukv)):
        if w is not None:
            wt[name] = w.astype(MXU)
    return wt


def _natural_small(gr):
    return dict(norm_g=gr["norm_g"][0],
                gla_wg_f=gr["wg_pad_f"][MROPE:MROPE + GRANK], gla_bg_f=gr["bg_f"][0],
                gla_wg_b=gr["wg_pad_b"][MROPE + GRANK:MROPE + 2 * GRANK], gla_bg_b=gr["bg_b"][0],
                gla_norm_g=gr["gla_norm_g"][0], mla_q_norm_g=gr["q_norm_g"][0], mla_kv_norm_g=gr["kv_norm_g"][0],
                mla_out_g=gr["mla_out_g"][0], conv_w=gr["conv_w"], conv_out_g=gr["conv_out_g"][0])


def _natural_grads(gr):
    return dict(_natural_small(gr), w_in=_unperm_in_cols(gr["w_in"]), w_out=gr["w_out"],
                mla_w_uq=_unperm_uq_cols(gr["w_uq"]), mla_w_ukv=gr["w_ukv"])


def _exchange(arrs, name, scatter, space):
    n = len(arrs)

    def body(*refs):
        ins, outs = refs[:n], refs[n:2 * n]
        send_sems, recv_sems, loc_sems = refs[2 * n:]
        ax, ay, ac = lax.axis_index("x"), lax.axis_index("y"), lax.axis_index("c")
        me = 4 * ax + 2 * ay + ac

        def src(a, to):
            return ins[a].at[to] if scatter else ins[a]

        def remote(a, r, dst_slot):
            px = 1 - ax if r & 4 else ax
            py = 1 - ay if r & 2 else ay
            pc = 1 - ac if r & 1 else ac
            return pltpu.make_async_remote_copy(
                src_ref=src(a, 4 * px + 2 * py + pc), dst_ref=outs[a].at[dst_slot(4 * px + 2 * py + pc)],
                send_sem=send_sems.at[a, r - 1], recv_sem=recv_sems.at[a, r - 1],
                device_id=(px, py, pc), device_id_type=MESH)

        locs = [pltpu.make_async_copy(src(a, me), outs[a].at[me], loc_sems.at[a]) for a in range(n)]
        for cp in locs:
            cp.start()
        sends = [remote(a, r, lambda peer: me) for r in range(1, N_DEV) for a in range(n)]
        for cp in sends:
            cp.start()
        for r in range(1, N_DEV):
            for a in range(n):
                remote(a, r, lambda peer: peer).wait_recv()
        for cp in sends:
            cp.wait_send()
        for cp in locs:
            cp.wait()

    def out_shape(a):
        return jax.ShapeDtypeStruct(a.shape if scatter else (N_DEV,) + a.shape, a.dtype)

    spec = pl.BlockSpec(memory_space=space)
    return pl.pallas_call(
        body, in_specs=[spec] * n, out_specs=[spec] * n, out_shape=[out_shape(a) for a in arrs],
        scratch_shapes=[pltpu.SemaphoreType.DMA((n, N_DEV - 1)), pltpu.SemaphoreType.DMA((n, N_DEV - 1)),
                        pltpu.SemaphoreType.DMA((n,))],
        name=name, compiler_params=pltpu.CompilerParams(vmem_limit_bytes=VMEM_LIMIT))(*arrs)


def _peer(r):
    ax, ay, ac = lax.axis_index("x"), lax.axis_index("y"), lax.axis_index("c")
    px = 1 - ax if r & 4 else ax
    py = 1 - ay if r & 2 else ay
    pc = 1 - ac if r & 1 else ac
    return (px, py, pc), 4 * px + 2 * py + pc


def _slot(rel_div):
    rel, div = rel_div
    idx = _peer(rel)[1]
    return idx if div == 1 else idx // div


AG_SPREAD = tuple((r, None, (0, 1), (r, 1)) for r in (1, 2, 4, 6))
AG_FORWARD = tuple((1, (k, 1), (k, 1), (1 ^ k, 1)) for k in (2, 4, 6))
RS_PAIR = tuple((1, (1 ^ k, 1), (1 ^ k, 2), (k, 2)) for k in (0, 2, 4, 6))
RS_CHIPS = tuple((r, (r, 2), (0, 2), (r, 2)) for r in (2, 4, 6))


def _plan_copies(plan, n, src_refs, land_refs, send_sems, recv_sems, arriving):
    out = []
    for i, (r, src, dst, recv) in enumerate(plan):
        peer = _peer(r)[0]
        for a in range(n):
            out.append(pltpu.make_async_remote_copy(
                src_ref=src_refs[a] if src is None else src_refs[a].at[_slot(src)],
                dst_ref=land_refs[a].at[_slot(recv if arriving else dst)],
                send_sem=send_sems.at[i * n + a], recv_sem=recv_sems.at[i * n + a],
                device_id=peer, device_id_type=MESH))
    return out


def _exchange_hbm(plan, srcs, lands, name, after=()):
    n = len(lands)
    fresh = isinstance(lands[0], jax.ShapeDtypeStruct)
    ins = ([] if srcs is None else list(srcs)) + ([] if fresh else list(lands))
    ns = 0 if srcs is None else n
    n_data = len(ins)
    ins = ins + list(after)

    def body(*refs):
        outs = refs[len(ins):len(ins) + n]
        send_sems, recv_sems = refs[-2:]
        src_refs = refs[:n] if srcs is not None else refs[ns:ns + n]
        sends = _plan_copies(plan, n, src_refs, outs, send_sems, recv_sems, False)
        for cp in sends:
            cp.start()
        for cp in _plan_copies(plan, n, src_refs, outs, send_sems, recv_sems, True):
            cp.wait_recv()
        for cp in sends:
            cp.wait_send()

    hbm = pl.BlockSpec(memory_space=pltpu.HBM)
    k = len(plan) * n
    return pl.pallas_call(
        body, name=name, in_specs=[hbm] * n_data + [pl.BlockSpec(memory_space=pl.ANY)] * len(after), out_specs=[hbm] * n,
        out_shape=[jax.ShapeDtypeStruct(a.shape, a.dtype) for a in lands],
        scratch_shapes=[pltpu.SemaphoreType.DMA((k,)), pltpu.SemaphoreType.DMA((k,))],
        input_output_aliases={} if fresh else {ns + i: i for i in range(n)},
        compiler_params=pltpu.CompilerParams(vmem_limit_bytes=VMEM_LIMIT))(*ins)


def _plan_start(plan, srcs, land_shapes, after, name):
    n = len(srcs)

    def body(*refs):
        src_refs, land_refs = refs[:n], refs[n:2 * n]
        send_sems, recv_sems = refs[2 * n + 1], refs[2 * n + 2]
        for cp in _plan_copies(plan, n, src_refs, land_refs, send_sems, recv_sems, False):
            cp.start()
        refs[-1][...] = jnp.zeros_like(refs[-1])

    hbm = pl.BlockSpec(memory_space=pltpu.HBM)
    sem = pl.BlockSpec(memory_space=pltpu.SEMAPHORE)
    k = len(plan) * n
    srcs = [pltpu.with_memory_space_constraint(a, pltpu.HBM) for a in srcs]
    lands = [pltpu.with_memory_space_constraint(lax.empty(shp, a.dtype), pltpu.HBM) for shp, a in zip(land_shapes, srcs)]
    res = pl.pallas_call(
        body, name=name,
        in_specs=[hbm] * (2 * n) + [pl.BlockSpec(memory_space=pl.ANY)],
        out_specs=[sem, sem] + [hbm] * (2 * n) + [pl.BlockSpec(memory_space=pltpu.VMEM)],
        out_shape=[pltpu.SemaphoreType.DMA((k,)), pltpu.SemaphoreType.DMA((k,))]
        + [pltpu.HBM(a.shape, a.dtype) for a in srcs] + [pltpu.HBM(shp, a.dtype) for shp, a in zip(land_shapes, srcs)]
        + [jax.ShapeDtypeStruct((8, LANES), F32)],
        input_output_aliases={i: 2 + i for i in range(2 * n)},
        compiler_params=pltpu.CompilerParams(has_side_effects=pltpu.SideEffectType.DATAFLOW_SIDE_EFFECTING),
    )(*srcs, *lands, after)
    return res[0], res[1], list(res[2:2 + n]), list(res[2 + n:2 + 2 * n]), res[-1]


def _plan_wait(plan, handle, after, name):
    send_sems, recv_sems, srcs, lands, _ = handle
    n = len(srcs)
    after = list(after)

    def body(*refs):
        src_refs, land_refs = refs[:n], refs[n:2 * n]
        ssem, rsem = refs[2 * n], refs[2 * n + 1]
        for cp in _plan_copies(plan, n, src_refs, land_refs, ssem, rsem, False):
            cp.wait_send()
        for cp in _plan_copies(plan, n, src_refs, land_refs, ssem, rsem, True):
            cp.wait_recv()

    hbm = pl.BlockSpec(memory_space=pltpu.HBM)
    sem = pl.BlockSpec(memory_space=pltpu.SEMAPHORE)
    res = pl.pallas_call(
        body, name=name,
        in_specs=[hbm] * (2 * n) + [sem, sem] + [pl.BlockSpec(memory_space=pl.ANY)] * len(after),
        out_specs=[hbm] * (2 * n),
        out_shape=[pltpu.HBM(a.shape, a.dtype) for a in srcs] + [pltpu.HBM(a.shape, a.dtype) for a in lands],
        input_output_aliases={i: i for i in range(2 * n)},
        compiler_params=pltpu.CompilerParams(has_side_effects=pltpu.SideEffectType.DATAFLOW_SIDE_EFFECTING),
    )(*srcs, *lands, send_sems, recv_sems, *after)
    return list(res[n:])


def _pair_sum(send, got, core, name):
    _, r, c = send.shape
    tr = 256 if r % 256 == 0 else r

    def body(core_ref, s_ref, g_ref, o_ref):
        o_ref[0] = (s_ref[0].astype(F32) + g_ref[0].astype(F32)).astype(o_ref.dtype)

    return pl.pallas_call(
        body, name=name,
        grid_spec=pltpu.PrefetchScalarGridSpec(
            num_scalar_prefetch=1, grid=(N_DEV // 2, r // tr),
            in_specs=[pl.BlockSpec((1, tr, c), lambda kc, i, core_ref: (2 * kc + core_ref[0], i, 0)),
                      pl.BlockSpec((1, tr, c), lambda kc, i, core_ref: (kc, i, 0))],
            out_specs=pl.BlockSpec((1, tr, c), lambda kc, i, core_ref: (kc, i, 0))),
        out_shape=jax.ShapeDtypeStruct((N_DEV // 2, r, c), send.dtype),
        compiler_params=_cp(("parallel", "parallel")))(core, send, got)


def _ada_mod(c_all, ada_w, ada_b_cols, name):
    nl, d, wc = ada_w.shape

    def body(c_ref, w_ref, b_ref, ca_ref, mod_ref):
        cv = c_ref[...]
        ca = cv * _sigmoid(cv)
        ca_ref[...] = ca
        mod_ref[0] = _dotf(ca, w_ref[0]) + b_ref[0]

    return pl.pallas_call(
        body, grid=(nl,),
        in_specs=[pl.BlockSpec((N_DEV, d), lambda l: (0, 0)), pl.BlockSpec((1, d, wc), lambda l: (l, 0, 0)),
                  pl.BlockSpec((1, 1, wc), lambda l: (l, 0, 0))],
        out_specs=[pl.BlockSpec((N_DEV, d), lambda l: (0, 0)), pl.BlockSpec((1, N_DEV, wc), lambda l: (l, 0, 0))],
        out_shape=[jax.ShapeDtypeStruct((N_DEV, d), F32), jax.ShapeDtypeStruct((nl, N_DEV, wc), F32)],
        name=name, compiler_params=_cp(("arbitrary",)))(c_all, ada_w, ada_b_cols)


def _adam(w, g, m, v):
    m2 = ADAM_B1 * m + (1.0 - ADAM_B1) * g
    v2 = ADAM_B2 * v + (1.0 - ADAM_B2) * (g * g)
    m_hat = m2 / (1.0 - ADAM_B1 ** ADAM_STEP)
    v_hat = v2 / (1.0 - ADAM_B2 ** ADAM_STEP)
    delta = -ADAM_LR * (m_hat / (jnp.sqrt(v_hat) + ADAM_EPS) + ADAM_WD * w)
    return delta, m2, v2


def _ada_grad_adam(c_act, d_mod, w, m, v, name):
    nl, d, wc = w.shape
    tk = min(512, d)

    def body(c_ref, dm_ref, w_ref, m_ref, v_ref, g_ref, dl_ref, m2_ref, v2_ref):
        g = _dotf_tn(c_ref[...], dm_ref[0])
        delta, m2, v2 = _adam(w_ref[0], g, m_ref[0], v_ref[0])
        g_ref[0], dl_ref[0], m2_ref[0], v2_ref[0] = g, delta, m2, v2

    blk = pl.BlockSpec((1, tk, wc), lambda l, i: (l, i, 0))
    shp = jax.ShapeDtypeStruct(w.shape, F32)
    return pl.pallas_call(
        body, grid=(nl, d // tk),
        in_specs=[pl.BlockSpec((N_DEV, tk), lambda l, i: (0, i)), pl.BlockSpec((1, N_DEV, wc), lambda l, i: (l, 0, 0)),
                  blk, blk, blk],
        out_specs=[blk] * 4, out_shape=[shp] * 4, name=name,
        compiler_params=_cp(("parallel", "parallel")))(c_act, d_mod, w, m, v)


def _adam_big(recv, w, m, v, layer, prev, name, after=()):
    nl, r, c = w.shape
    tr = 256 if r % 256 == 0 else r
    nparts = recv.shape[0]

    def body(rc_ref, w_ref, m_ref, v_ref, *rest):
        g_ref, dl_ref, m2_ref, v2_ref = rest[-4:]
        g = rc_ref[0].astype(F32)
        for d in range(1, nparts):
            g = g + rc_ref[d].astype(F32)
        delta, m2, v2 = _adam(w_ref[0], g, m_ref[0], v_ref[0])
        g_ref[0], dl_ref[0], m2_ref[0], v2_ref[0] = g, delta, m2, v2

    blk = pl.BlockSpec((1, tr, c), lambda i: (layer, i, 0))
    shp = jax.ShapeDtypeStruct(w.shape, F32)
    prev = () if prev is None else tuple(prev)
    return pl.pallas_call(
        body, grid=(r // tr,),
        in_specs=[pl.BlockSpec((nparts, tr, c), lambda i: (0, i, 0)), blk, blk, blk]
        + [pl.BlockSpec(memory_space=pl.ANY)] * (len(prev) + len(after)),
        out_specs=[blk] * 4, out_shape=[shp] * 4, name=name,
        input_output_aliases={4 + j: j for j in range(len(prev))},
        compiler_params=_cp(("parallel",)))(recv, w, m, v, *prev, *after)


def _sum_devices(gathered, name):
    _, r, c = gathered.shape

    def body(g_ref, o_ref):
        acc = g_ref[0]
        for d in range(1, N_DEV):
            acc = acc + g_ref[d]
        o_ref[...] = acc

    spec = pl.BlockSpec(memory_space=pltpu.VMEM)
    return pl.pallas_call(body, in_specs=[spec], out_specs=spec, out_shape=jax.ShapeDtypeStruct((r, c), F32),
                          name=name, compiler_params=pltpu.CompilerParams(vmem_limit_bytes=VMEM_LIMIT))(gathered)


def _adam_small(ws, gs, ms, vs, name):
    n = len(ws)

    def body(*refs):
        for i in range(n):
            w_ref, g_ref, m_ref, v_ref = (refs[k * n + i] for k in range(4))
            dl_ref, m2_ref, v2_ref = (refs[(4 + k) * n + i] for k in range(3))
            dl_ref[...], m2_ref[...], v2_ref[...] = _adam(w_ref[...], g_ref[...], m_ref[...], v_ref[...])

    spec = pl.BlockSpec(memory_space=pltpu.VMEM)
    shapes = [jax.ShapeDtypeStruct(w.shape, F32) for w in ws]
    res = pl.pallas_call(body, in_specs=[spec] * (4 * n), out_specs=[spec] * (3 * n), out_shape=shapes * 3, name=name,
                         compiler_params=pltpu.CompilerParams(vmem_limit_bytes=VMEM_LIMIT))(*ws, *gs, *ms, *vs)
    return res[:n], res[n:2 * n], res[2 * n:]


def _pack(parts):
    flat = jnp.concatenate([p.reshape(-1).astype(F32) for p in parts])
    assert flat.shape[0] % LANES == 0, flat.shape
    return flat.reshape(-1, LANES)


def _unpack(packed, shapes):
    flat = packed.reshape(-1)
    out, off = [], 0
    for shp in shapes:
        size = 1
        for dim in shp:
            size *= dim
        out.append(flat[off:off + size].reshape(shp))
        off += size
    return out


def _gather_cols(g, per):
    g = jnp.moveaxis(g, 0, -2)
    return g.reshape(g.shape[:-2] + (N_DEV * per,))


def _scatter_cols(g, per):
    return jnp.moveaxis(g.reshape(g.shape[:-1] + (N_DEV, per)), -2, 0)


def _my_cols(full, me, per):
    return lax.dynamic_slice_in_dim(full, me * per, per, axis=full.ndim - 1)


def kernel(x, c, positions, ada_w, ada_b, norm_g, w_in, gla_wg_f, gla_bg_f, gla_wg_b, gla_bg_b, gla_norm_g, mla_q_norm_g, mla_kv_norm_g, mla_w_uq, mla_w_ukv, mla_out_g, conv_w, conv_out_g, w_out, final_g, loss_target, m_ada_w, m_ada_b, m_norm_g, m_w_in, m_gla_wg_f, m_gla_bg_f, m_gla_wg_b, m_gla_bg_b, m_gla_norm_g, m_mla_q_norm_g, m_mla_kv_norm_g, m_mla_w_uq, m_mla_w_ukv, m_mla_out_g, m_conv_w, m_conv_out_g, m_w_out, m_final_g, v_ada_w, v_ada_b, v_norm_g, v_w_in, v_gla_wg_f, v_gla_bg_f, v_gla_wg_b, v_gla_bg_b, v_gla_norm_g, v_mla_q_norm_g, v_mla_kv_norm_g, v_mla_w_uq, v_mla_w_ukv, v_mla_out_g, v_conv_w, v_conv_out_g, v_w_out, v_final_g):
    me = 4 * lax.axis_index("x") + 2 * lax.axis_index("y") + lax.axis_index("c")
    nl = ada_w.shape[0]
    s, d = x.shape[1], x.shape[2]
    ada_cols = ada_w.shape[2]
    wgc, cwc = gla_wg_f.shape[2], conv_w.shape[2]

    (g0,) = _exchange([_pack([c, gla_wg_f, gla_wg_b, conv_w])], "gather_small_in", False, pltpu.VMEM)
    g0 = g0.reshape(N_DEV, -1)
    o1, o2, o3 = d, d + gla_wg_f.size, d + 2 * gla_wg_f.size
    c_all = g0[:, :o1]
    wgf_full = _gather_cols(g0[:, o1:o2].reshape((N_DEV,) + gla_wg_f.shape), wgc)
    wgb_full = _gather_cols(g0[:, o2:o3].reshape((N_DEV,) + gla_wg_b.shape), wgc)
    convw_full = _gather_cols(g0[:, o3:].reshape((N_DEV,) + conv_w.shape), cwc)

    ada_b_cols = _my_cols(ada_b, me, ada_cols).reshape(nl, 1, ada_cols)
    c_act, mod_cols = _ada_mod(c_all, ada_w, ada_b_cols, "ada_mod")
    (g1,) = _exchange([_pack([mod_cols])], "gather_mod", False, pltpu.VMEM)
    mod_all = g1.reshape(N_DEV, nl, N_DEV, ada_cols)
    mod_mine = _gather_cols(lax.dynamic_index_in_dim(mod_all, me, axis=2, keepdims=False), ada_cols)

    inv_freq = ROPE_THETA ** (-jnp.arange(0, MROPE, 2, dtype=F32) / MROPE)
    ang = positions[0].astype(F32)[:, None] * inv_freq
    cos, sin = jnp.tile(jnp.cos(ang), (1, LANES * 2 // MROPE)), jnp.tile(jnp.sin(ang), (1, LANES * 2 // MROPE))

    big = [w_in, w_out, mla_w_uq, mla_w_ukv]
    big_names = ["w_in", "w_out", "mla_w_uq", "mla_w_ukv"]

    def local_blocks(l):
        return [w[l].astype(MXU) for w in big]

    def put_own(lands, own):
        return [lax.dynamic_update_index_in_dim(ld, o, me, 0) for ld, o in zip(lands, own)]

    def layer_weights(l, gw_in=None, gw_out=None, gw_uq=None, gw_ukv=None):
        small = dict(norm_g=norm_g[l], gla_wg_f=wgf_full[l], gla_bg_f=gla_bg_f[l], gla_wg_b=wgb_full[l],
                     gla_bg_b=gla_bg_b[l], gla_norm_g=gla_norm_g[l], mla_q_norm_g=mla_q_norm_g[l],
                     mla_kv_norm_g=mla_kv_norm_g[l], mla_out_g=mla_out_g[l], conv_w=convw_full[l],
                     conv_out_g=conv_out_g[l])
        return _prep_layer_weights(
            None if gw_in is None else _perm_gathered(gw_in, IN_SEGS, PW),
            None if gw_out is None else gw_out.reshape((-1,) + gw_out.shape[2:]),
            None if gw_uq is None else _perm_gathered(gw_uq, UQ_SEGS, MQW),
            None if gw_ukv is None else _gather_cols(gw_ukv, mla_w_ukv.shape[2]), small)

    def land_shapes(blocks, slots):
        return [jax.ShapeDtypeStruct((slots,) + b.shape, b.dtype) for b in blocks]

    def slots_of(blocks):
        return [(N_DEV,) + b.shape for b in blocks]

    def forwarded(lands, blocks, tag):
        return put_own(_exchange_hbm(AG_FORWARD, None, lands, f"gather_{tag}_forward"), blocks)

    first = local_blocks(0)
    w_in_start = _plan_start(AG_SPREAD, first[:1], slots_of(first[:1]), mod_mine, "gather_w_in_l0_start")
    adam_w_in = [a + w_in_start[-1][0, 0] for a in (w_in, m_w_in, v_w_in)]
    lands = _plan_wait(AG_SPREAD, w_in_start, adam_w_in, "gather_w_in_l0_wait")
    (gw_in,) = forwarded(lands, first[:1], "w_in_l0")
    rest = _plan_start(AG_SPREAD, first[1:], slots_of(first[1:]), gw_in, "gather_rest_l0_start")
    h = x[0]
    saved, layers, mods = [], [], []
    pending = {}
    for l in range(nl):
        shift, scale, gate = (mod_mine[l, i * d:(i + 1) * d].reshape(1, d) for i in range(3))
        nxt = local_blocks(l + 1) if l + 1 < nl else None

        def start_next(after, wt_late, l=l, nxt=nxt):
            if nxt is not None:
                pending[l + 1] = _plan_start(AG_SPREAD, nxt, slots_of(nxt), after, f"gather_weights_l{l + 1}_start")
                wt_late["q_norm_g"] = layers[l]["q_norm_g"] + pending[l + 1][-1][0, 0]
            return wt_late

        if l == 0:
            in_after = (rest[-1],)
            layers.append(layer_weights(0, gw_in))

            def late(proj):
                got = forwarded(_plan_wait(AG_SPREAD, rest, [proj], "gather_rest_l0_wait"), first[1:], "rest_l0")
                full = layer_weights(0, None, *got)
                return start_next(got[0], {k: full[k] for k in ("w_out", "w_uq", "w_ukv")})
        else:
            got = forwarded(_plan_wait(AG_SPREAD, pending.pop(l), [h], f"gather_weights_l{l}_wait"), blocks, f"weights_l{l}")
            layers.append(layer_weights(l, *got))
            in_after = ()

            def late(proj):
                return start_next(proj, {})
        mods.append((shift, scale, gate))
        h, sv = _layer_fwd(h, mods[l], layers[l], cos, sin, f"l{l}", late, in_after)
        saved.append(sv)
        blocks = nxt
    loss_part, d_h, d_final_g = _final_loss(h, final_g.reshape(1, d), loss_target[0], "final_loss")
    loss = lax.psum(loss_part[0, 0], ("x", "y", "c"))
    shift, scale, gate = mods[-1]
    mods[-1] = (shift, scale, gate + 0.0 * loss)

    def grad_sends(gr):
        return [_scatter_perm(gr["w_in"], IN_SEGS, w_in.shape[2]).astype(MXU),
                gr["w_out"].reshape((N_DEV,) + w_out.shape[1:]).astype(MXU),
                _scatter_perm(gr["w_uq"], UQ_SEGS, mla_w_uq.shape[2]).astype(MXU),
                _scatter_cols(gr["w_ukv"], mla_w_ukv.shape[2]).astype(MXU)]

    my_chip = me // 2
    my_core = (me % 2).astype(jnp.int32).reshape(1)

    def chip_sums(gr, tag):
        sends = grad_sends(gr)
        got = _exchange_hbm(RS_PAIR, sends, land_shapes([sd[0] for sd in sends], N_DEV // 2), f"scatter_grads_{tag}_pair")
        return [_pair_sum(sd, gt, my_core, f"pair_sum_{n}_{tag}") for sd, gt, n in zip(sends, got, big_names)]

    def with_own_chip(lands, sums):
        return [lax.dynamic_update_index_in_dim(ld, lax.dynamic_index_in_dim(sm, my_chip, axis=0, keepdims=False),
                                                my_chip, 0) for ld, sm in zip(lands, sums)]

    small_names = ["norm_g", "gla_wg_f", "gla_bg_f", "gla_wg_b", "gla_bg_b", "gla_norm_g", "mla_q_norm_g",
                   "mla_kv_norm_g", "mla_out_g", "conv_w", "conv_out_g"]
    d_mods, grads, recv = [None] * nl, [None] * nl, [None] * nl
    flight = {}
    small = {}

    def gather_small(d_x, d_mod0, gr0):
        d_mods[0], grads[0] = d_mod0, _natural_small(gr0)
        d_mod_mine = jnp.stack([jnp.concatenate(d_mods[l], axis=-1)[0] for l in range(nl)])
        parts = [d_mod_mine] + [jnp.stack([grads[l][n] for l in range(nl)]) for n in small_names] + [d_final_g]
        (g2,) = _exchange([_pack(parts)], "gather_small_grads", False, pltpu.VMEM)
        small["d_mod_all"] = g2.reshape(N_DEV, -1)[:, :d_mod_mine.size].reshape(N_DEV, nl, 3 * d)
        small["summed"] = dict(zip(["ada_b"] + small_names + ["final_g"],
                                   _unpack(_sum_devices(g2, "sum_small_grads"), [p.shape for p in parts])))
        return (g2,)

    for l in reversed(range(nl)):
        def ship(big_grads, l=l):
            if l + 1 in flight:
                pend, sm = flight.pop(l + 1)
                recv[l + 1] = with_own_chip(_plan_wait(RS_CHIPS, pend, list(big_grads.values()),
                                                       f"scatter_grads_l{l + 1}_wait"), sm)
            sm = chip_sums(big_grads, f"l{l}")
            after = recv[l + 1][0] if l + 1 < nl else big_grads["w_in"]
            flight[l] = (_plan_start(RS_CHIPS, sm, [a.shape for a in sm], after, f"scatter_grads_l{l}_start"), sm)
            return (flight[l][0][-1],)

        if l > 0:
            d_h, d_mods[l], gr = _layer_bwd(d_h, saved[l], mods[l], layers[l], cos, sin, f"l{l}", ship)
            grads[l] = _natural_small(gr)
        else:
            d_h, _, _ = _layer_bwd(d_h, saved[l], mods[l], layers[l], cos, sin, f"l{l}", ship, gather_small)
    pending, sums = flight.pop(0)
    grad_x = d_h[None]
    summed = small["summed"]
    summed["gla_wg_f"] = _my_cols(summed["gla_wg_f"], me, wgc)
    summed["gla_wg_b"] = _my_cols(summed["gla_wg_b"], me, wgc)
    summed["conv_w"] = _my_cols(summed["conv_w"], me, cwc)

    d_mod_cols = jnp.moveaxis(_my_cols(small["d_mod_all"], me, ada_cols), 0, 1) + pending[-1][0, 0]
    out = {}
    out["ada_w"] = _ada_grad_adam(c_act, d_mod_cols, ada_w, m_ada_w, v_ada_w, "ada_grad_adam")

    given = dict(ada_b=(ada_b, m_ada_b, v_ada_b), norm_g=(norm_g, m_norm_g, v_norm_g),
                 gla_wg_f=(gla_wg_f, m_gla_wg_f, v_gla_wg_f), gla_bg_f=(gla_bg_f, m_gla_bg_f, v_gla_bg_f),
                 gla_wg_b=(gla_wg_b, m_gla_wg_b, v_gla_wg_b), gla_bg_b=(gla_bg_b, m_gla_bg_b, v_gla_bg_b),
                 gla_norm_g=(gla_norm_g, m_gla_norm_g, v_gla_norm_g),
                 mla_q_norm_g=(mla_q_norm_g, m_mla_q_norm_g, v_mla_q_norm_g),
                 mla_kv_norm_g=(mla_kv_norm_g, m_mla_kv_norm_g, v_mla_kv_norm_g),
                 mla_out_g=(mla_out_g, m_mla_out_g, v_mla_out_g), conv_w=(conv_w, m_conv_w, v_conv_w),
                 conv_out_g=(conv_out_g, m_conv_out_g, v_conv_out_g), final_g=(final_g, m_final_g, v_final_g))
    names = list(given)

    def two_d(a):
        return a.reshape(1, -1) if a.ndim == 1 else a

    g_nat = [summed[n].reshape(given[n][0].shape) for n in names]
    res = _adam_small([two_d(given[n][0]) for n in names], [two_d(g) for g in g_nat],
                      [two_d(given[n][1]) for n in names], [two_d(given[n][2]) for n in names], "adam_small")
    for i, n in enumerate(names):
        out[n] = (g_nat[i],) + tuple(r[i].reshape(given[n][0].shape) for r in res)

    state = dict(w_in=adam_w_in, w_out=(w_out, m_w_out, v_w_out), mla_w_uq=(mla_w_uq, m_mla_w_uq, v_mla_w_uq),
                 mla_w_ukv=(mla_w_ukv, m_mla_w_ukv, v_mla_w_ukv))
    done = [out["ada_w"][0], res[0][0]]
    for l in reversed(range(nl)):
        if l == 0:
            recv[0] = with_own_chip(_plan_wait(RS_CHIPS, pending, done, "scatter_grads_l0_wait"), sums)
        for i, n in enumerate(big_names):
            out[n] = _adam_big(recv[l][i], *state[n], l, out.get(n), f"adam_{n}_l{l}",
                               (pending[-1],))
        done = done + [out[n][0] for n in big_names]

    order = ["ada_w", "ada_b", "norm_g", "w_in", "gla_wg_f", "gla_bg_f", "gla_wg_b", "gla_bg_b", "gla_norm_g",
             "mla_q_norm_g", "mla_kv_norm_g", "mla_w_uq", "mla_w_ukv", "mla_out_g", "conv_w", "conv_out_g", "w_out",
             "final_g"]
    return (loss, grad_x, *[out[n][0] for n in order], *[out[n][1] for n in order], *[out[n][2] for n in order],
            *[out[n][3] for n in order])
```

```python
import functools

import jax
import jax.numpy as jnp
from jax import lax
from jax.experimental import pallas as pl
from jax.experimental.pallas import tpu as pltpu

F32 = jnp.float32
MXU = jnp.bfloat16
HI = lax.Precision.HIGHEST
N_DEV = 8
MESH = pl.DeviceIdType.MESH

D_MIX = 2048
GH, GDK, GDV = 6, 64, 128
GW = GH * GDV
GQK = GH * GDK
GRANK = 16
GTEMP = 16.0
CHUNK = 64
MH, MQL, MKVL, MNOPE, MROPE, MDV = 6, 384, 256, 128, 64, 128
MW = MH * MDV
MQW = MH * (MNOPE + MROPE)
MKVW = MH * (MNOPE + MDV)
CONV_CH = 512
ROPE_THETA = 10000.0
EPS = 1e-6
IN_DIM = 5856
OZ, OCB, OCC, OCX, OMKV, OGV, OGQ, OGK, OMQ, OT = 0, 2048, 2560, 3072, 3584, 3840, 4608, 4992, 5376, 5760
PW = 5888
LANES = 128
VMEM_LIMIT = 56 * 1024 * 1024

ADAM_LR, ADAM_B1, ADAM_B2, ADAM_EPS, ADAM_WD, ADAM_STEP = 0.001, 0.9, 0.999, 1e-08, 0.01, 10


def _cp(sem=None):
    return pltpu.CompilerParams(dimension_semantics=sem, vmem_limit_bytes=VMEM_LIMIT)


def _dot(a, b):
    return jnp.dot(a.astype(MXU), b.astype(MXU), preferred_element_type=F32)


def _dot_nt(a, b):
    return lax.dot_general(a.astype(MXU), b.astype(MXU), (((1,), (1,)), ((), ())), preferred_element_type=F32)


def _dot_tn(a, b):
    return lax.dot_general(a.astype(MXU), b.astype(MXU), (((0,), (0,)), ((), ())), preferred_element_type=F32)


def _dotf(a, b):
    return jnp.dot(a, b, precision=HI, preferred_element_type=F32)


def _dotf_nt(a, b):
    return lax.dot_general(a, b, (((1,), (1,)), ((), ())), precision=HI, preferred_element_type=F32)


def _dotf_tn(a, b):
    return lax.dot_general(a, b, (((0,), (0,)), ((), ())), precision=HI, preferred_element_type=F32)


def _split3(x):
    hi = x.astype(jnp.bfloat16)
    r1 = x - hi.astype(F32)
    mid = r1.astype(jnp.bfloat16)
    lo = (r1 - mid.astype(F32)).astype(jnp.bfloat16)
    return hi, mid, lo


def _cum_dot(cum, x, transpose=False):
    dn = (((0,), (0,)), ((), ())) if transpose else (((1,), (0,)), ((), ()))
    cb = cum.astype(jnp.bfloat16)
    parts = [lax.dot_general(cb, p, dn, preferred_element_type=F32) for p in _split3(x)]
    return parts[0] + parts[1] + parts[2]


def _rows(s):
    return min(256, s)


def _rms(x, g):
    r = lax.rsqrt(jnp.mean(x * x, axis=-1, keepdims=True) + EPS)
    return x * r * g


def _rms_bwd(dy, x, g):
    r = lax.rsqrt(jnp.mean(x * x, axis=-1, keepdims=True) + EPS)
    xh = x * r
    dxh = dy * g
    dg = jnp.sum(dy * xh, axis=0, keepdims=True)
    dx = r * (dxh - xh * jnp.mean(dxh * xh, axis=-1, keepdims=True))
    return dx, dg


def _sigmoid(z):
    return 1.0 / (1.0 + jnp.exp(-z))


def _matmul(a, b, *, dims, tm, tn, tk, out_dtypes, name, epilogue=None, extras=(), extra_kinds=(), after=()):
    if dims == "nn":
        (m, k), n, mul = a.shape, b.shape[1], _dot
    elif dims == "nt":
        (m, k), n, mul = a.shape, b.shape[0], _dot_nt
    else:
        (k, m), n, mul = a.shape, b.shape[1], _dot_tn
    tm, tn, tk = min(tm, m), min(tn, n), min(tk, k)
    assert m % tm == 0 and n % tn == 0 and k % tk == 0, (m, n, k, tm, tn, tk)
    if dims == "nn":
        a_spec = pl.BlockSpec((tm, tk), lambda i, j, kk: (i, kk))
        b_spec = pl.BlockSpec((tk, tn), lambda i, j, kk: (kk, j))
    elif dims == "nt":
        a_spec = pl.BlockSpec((tm, tk), lambda i, j, kk: (i, kk))
        b_spec = pl.BlockSpec((tn, tk), lambda i, j, kk: (j, kk))
    else:
        a_spec = pl.BlockSpec((tk, tm), lambda i, j, kk: (kk, i))
        b_spec = pl.BlockSpec((tk, tn), lambda i, j, kk: (kk, j))
    nk = k // tk
    n_extra = len(extras)
    n_out = len(out_dtypes)
    n_after = len(after)
    extra_specs = []
    for kind in extra_kinds:
        if kind == "mn":
            extra_specs.append(pl.BlockSpec((tm, tn), lambda i, j, kk: (i, j)))
        else:
            extra_specs.append(pl.BlockSpec((1, tn), lambda i, j, kk: (0, j)))

    def finish(res, ex, outs):
        vals = (res,) if epilogue is None else epilogue(res, *[e[...] for e in ex])
        for o, v in zip(outs, vals):
            o[...] = v.astype(o.dtype)

    def body(*refs):
        a_ref, b_ref = refs[0], refs[1]
        ex = refs[2:2 + n_extra]
        outs = refs[2 + n_extra + n_after:2 + n_extra + n_after + n_out]
        if nk == 1:
            finish(mul(a_ref[...], b_ref[...]), ex, outs)
            return
        acc = refs[-1]
        kk = pl.program_id(2)

        @pl.when(kk == 0)
        def _():
            acc[...] = jnp.zeros_like(acc)

        acc[...] += mul(a_ref[...], b_ref[...])

        @pl.when(kk == nk - 1)
        def _():
            finish(acc[...], ex, outs)

    out_spec = pl.BlockSpec((tm, tn), lambda i, j, kk: (i, j))
    res = pl.pallas_call(
        body, grid=(m // tm, n // tn, nk),
        in_specs=[a_spec, b_spec] + extra_specs + [pl.BlockSpec(memory_space=pl.ANY)] * n_after,
        out_specs=[out_spec] * n_out,
        out_shape=[jax.ShapeDtypeStruct((m, n), dt) for dt in out_dtypes],
        scratch_shapes=[] if nk == 1 else [pltpu.VMEM((tm, tn), F32)],
        name=name, compiler_params=_cp(("parallel", "parallel", "arbitrary")),
    )(a, b, *extras, *after)
    return res


def _norm_mod(x, g, scale, shift, name):
    s, d = x.shape
    tr = _rows(s)

    def body(x_ref, g_ref, sc_ref, sh_ref, h_ref):
        h = _rms(x_ref[...], g_ref[...]) * (1.0 + sc_ref[...]) + sh_ref[...]
        h_ref[...] = h.astype(h_ref.dtype)

    row = pl.BlockSpec((tr, d), lambda i: (i, 0))
    vec = pl.BlockSpec((1, d), lambda i: (0, 0))
    return pl.pallas_call(body, grid=(s // tr,), in_specs=[row, vec, vec, vec], out_specs=row,
                          out_shape=jax.ShapeDtypeStruct((s, d), MXU), name=name,
                          compiler_params=_cp(("parallel",)))(x, g, scale, shift)


def _norm_mod_bwd(d_h, x, d_out, g, scale, name):
    s, d = x.shape
    tr = _rows(s)

    def body(dh_ref, x_ref, do_ref, g_ref, sc_ref, dx_ref, dsh_ref, dsc_ref, dg_ref):
        i = pl.program_id(0)

        @pl.when(i == 0)
        def _():
            dsh_ref[...] = jnp.zeros_like(dsh_ref)
            dsc_ref[...] = jnp.zeros_like(dsc_ref)
            dg_ref[...] = jnp.zeros_like(dg_ref)

        dh = dh_ref[...]
        xv = x_ref[...]
        gv = g_ref[...]
        r = lax.rsqrt(jnp.mean(xv * xv, axis=-1, keepdims=True) + EPS)
        xh = xv * r
        dsh_ref[...] += jnp.sum(dh, axis=0, keepdims=True)
        dsc_ref[...] += jnp.sum(dh * (xh * gv), axis=0, keepdims=True)
        dhn = dh * (1.0 + sc_ref[...])
        dg_ref[...] += jnp.sum(dhn * xh, axis=0, keepdims=True)
        dxh = dhn * gv
        dx_ref[...] = do_ref[...] + r * (dxh - xh * jnp.mean(dxh * xh, axis=-1, keepdims=True))

    row = pl.BlockSpec((tr, d), lambda i: (i, 0))
    vec = pl.BlockSpec((1, d), lambda i: (0, 0))
    vshape = jax.ShapeDtypeStruct((1, d), F32)
    return pl.pallas_call(body, grid=(s // tr,), in_specs=[row, row, row, vec, vec],
                          out_specs=[row, vec, vec, vec],
                          out_shape=[jax.ShapeDtypeStruct((s, d), F32), vshape, vshape, vshape],
                          name=name, compiler_params=_cp(("arbitrary",)))(d_h, x, d_out, g, scale)


def _gate_bwd(d_out, u, gate, name):
    s, d = d_out.shape
    tr = _rows(s)

    def body(do_ref, u_ref, gt_ref, du_ref, dgt_ref):
        @pl.when(pl.program_id(0) == 0)
        def _():
            dgt_ref[...] = jnp.zeros_like(dgt_ref)

        do = do_ref[...]
        du_ref[...] = (do * gt_ref[...]).astype(du_ref.dtype)
        dgt_ref[...] += jnp.sum(do * u_ref[...], axis=0, keepdims=True)

    row = pl.BlockSpec((tr, d), lambda i: (i, 0))
    vec = pl.BlockSpec((1, d), lambda i: (0, 0))
    return pl.pallas_call(body, grid=(s // tr,), in_specs=[row, row, vec], out_specs=[row, vec],
                          out_shape=[jax.ShapeDtypeStruct((s, d), MXU), jax.ShapeDtypeStruct((1, d), F32)],
                          name=name, compiler_params=_cp(("arbitrary",)))(d_out, u, gate)


def _final_loss(x, g, target, name):
    s, d = x.shape
    tr = _rows(s)

    def body(x_ref, g_ref, t_ref, loss_ref, dx_ref, dg_ref):
        @pl.when(pl.program_id(0) == 0)
        def _():
            loss_ref[...] = jnp.zeros_like(loss_ref)
            dg_ref[...] = jnp.zeros_like(dg_ref)

        xv = x_ref[...]
        gv = g_ref[...]
        diff = _rms(xv, gv) - t_ref[...]
        part = 0.5 * jnp.sum(jnp.sum(diff * diff, axis=-1, keepdims=True) / d, axis=0, keepdims=True)
        loss_ref[...] += jnp.broadcast_to(part, loss_ref.shape)
        dx, dg = _rms_bwd(diff / d, xv, gv)
        dx_ref[...] = dx
        dg_ref[...] += dg

    row = pl.BlockSpec((tr, d), lambda i: (i, 0))
    vec = pl.BlockSpec((1, d), lambda i: (0, 0))
    lvec = pl.BlockSpec((1, LANES), lambda i: (0, 0))
    return pl.pallas_call(body, grid=(s // tr,), in_specs=[row, vec, row], out_specs=[lvec, row, vec],
                          out_shape=[jax.ShapeDtypeStruct((1, LANES), F32), jax.ShapeDtypeStruct((s, d), F32),
                                     jax.ShapeDtypeStruct((1, d), F32)],
                          name=name, compiler_params=_cp(("arbitrary",)))(x, g, target)


def _shift_rows(u, s, down):
    ri = lax.broadcasted_iota(jnp.int32, u.shape, 0)
    if down:
        return jnp.where(ri == 0, 0.0, pltpu.roll(u, 1, 0))
    return jnp.where(ri == s - 1, 0.0, pltpu.roll(u, s - 1, 0))


def _conv_fwd(proj, conv_w, name):
    s = proj.shape[0]
    nt = CONV_CH // LANES

    def body(cb_ref, cc_ref, cx_ref, w_ref, pre_ref):
        u = cc_ref[...] * cx_ref[...]
        conv = _shift_rows(u, s, True) * w_ref[0:1, :] + u * w_ref[1:2, :] + _shift_rows(u, s, False) * w_ref[2:3, :]
        pre_ref[...] = cb_ref[...] * conv

    def col(off):
        return pl.BlockSpec((s, LANES), lambda j: (0, off // LANES + j))

    return pl.pallas_call(body, grid=(nt,), in_specs=[col(OCB), col(OCC), col(OCX), pl.BlockSpec((3, LANES), lambda j: (0, j))],
                          out_specs=pl.BlockSpec((s, LANES), lambda j: (0, j)),
                          out_shape=jax.ShapeDtypeStruct((s, CONV_CH), F32), name=name,
                          compiler_params=_cp(("parallel",)))(proj, proj, proj, conv_w)


def _conv_bwd(proj, conv_w, d_pre, name):
    s = proj.shape[0]
    nt = CONV_CH // LANES

    def body(cb_ref, cc_ref, cx_ref, w_ref, dp_ref, dcb_ref, dcc_ref, dcx_ref, dw_ref):
        cc, cx = cc_ref[...], cx_ref[...]
        u = cc * cx
        up, dn = _shift_rows(u, s, True), _shift_rows(u, s, False)
        w0, w1, w2 = w_ref[0:1, :], w_ref[1:2, :], w_ref[2:3, :]
        conv = up * w0 + u * w1 + dn * w2
        dp = dp_ref[...]
        dcb_ref[...] = dp * conv
        dconv = dp * cb_ref[...]
        du = _shift_rows(dconv, s, False) * w0 + dconv * w1 + _shift_rows(dconv, s, True) * w2
        dcc_ref[...] = du * cx
        dcx_ref[...] = du * cc
        dw_ref[0:1, :] = jnp.sum(dconv * up, axis=0, keepdims=True)
        dw_ref[1:2, :] = jnp.sum(dconv * u, axis=0, keepdims=True)
        dw_ref[2:3, :] = jnp.sum(dconv * dn, axis=0, keepdims=True)

    def col(off):
        return pl.BlockSpec((s, LANES), lambda j: (0, off // LANES + j))

    blk = pl.BlockSpec((s, LANES), lambda j: (0, j))
    wblk = pl.BlockSpec((3, LANES), lambda j: (0, j))
    full = jax.ShapeDtypeStruct((s, CONV_CH), F32)
    return pl.pallas_call(body, grid=(nt,), in_specs=[col(OCB), col(OCC), col(OCX), wblk, blk],
                          out_specs=[blk, blk, blk, wblk],
                          out_shape=[full, full, full, jax.ShapeDtypeStruct((3, CONV_CH), F32)],
                          name=name, compiler_params=_cp(("parallel",)))(proj, proj, proj, conv_w, d_pre)


GLA_SUB = 8


def _gla_gates(t_ref, wg_ref, bg_ref):
    t = t_ref[...]
    a = _dot(t, wg_ref[...]) + bg_ref[...]
    la = (jnp.minimum(a, 0.0) - jnp.log(1.0 + jnp.exp(-jnp.abs(a)))) / GTEMP
    return t, a, la


def _gla_masks(reverse):
    ri = lax.broadcasted_iota(jnp.int32, (CHUNK, CHUNK), 0)
    ci = lax.broadcasted_iota(jnp.int32, (CHUNK, CHUNK), 1)
    if reverse:
        cum, mask = ci >= ri, ci > ri
    else:
        cum, mask = ci <= ri, ci <= ri
    return cum.astype(F32), mask


def _gla_specs(s, reverse):
    nsub = min(GLA_SUB, s // CHUNK)
    nsteps = s // (CHUNK * nsub)

    def row(n):
        return nsteps - 1 - n if reverse else n

    def chunk(pi):
        return nsub - 1 - pi if reverse else pi

    return nsub, nsteps, row, chunk


def _gla_fwd(proj, wg_pad, bg, reverse, name):
    s = proj.shape[0]
    nsub, nsteps, row, chunk = _gla_specs(s, reverse)
    rb = nsub * CHUNK

    def body(q_ref, k_ref, v_ref, t_ref, wg_ref, bg_ref, o_ref, st_ref, state):
        @pl.when(pl.program_id(0) == 0)
        def _():
            state[...] = jnp.zeros_like(state)

        _, _, la = _gla_gates(t_ref, wg_ref, bg_ref)
        cumf, mask = _gla_masks(reverse)
        lane = lax.broadcasted_iota(jnp.int32, (CHUNK, LANES), 1)
        for pi in range(nsub):
            rows = slice(chunk(pi) * CHUNK, (chunk(pi) + 1) * CHUNK)
            la_c = la[rows]
            b = _cum_dot(cumf, la_c)
            bl = jnp.sum(la_c, axis=0, keepdims=True)
            q = q_ref[rows, :] * (GDK ** -0.5)
            k = k_ref[rows, :]
            qd = q * jnp.exp(b)
            ki = k * jnp.exp(-b)
            kte = k * jnp.exp(bl - b)
            decay = jnp.exp(bl)
            for h in range(GH):
                p = h // 2
                sl = slice(p * LANES, (p + 1) * LANES)
                lm = (lane < GDK) if h % 2 == 0 else (lane >= GDK)
                qd_h = jnp.where(lm, qd[:, sl], 0.0)
                kte_h = jnp.where(lm, kte[:, sl], 0.0)
                v_h = v_ref[rows, h * GDV:(h + 1) * GDV]
                st = state[h]
                a_mat = jnp.where(mask, _dot_nt(qd_h, ki[:, sl]), 0.0)
                o_ref[rows, h * GDV:(h + 1) * GDV] = _dot(a_mat, v_h) + _dot_nt(qd_h, st)
                st_ref[pi, h] = st
                state[h] = st * decay[:, sl] + _dot_tn(v_h, kte_h)

    return pl.pallas_call(
        body, grid=(nsteps,),
        in_specs=[pl.BlockSpec((rb, GQK), lambda n: (row(n), OGQ // GQK)),
                  pl.BlockSpec((rb, GQK), lambda n: (row(n), OGK // GQK)),
                  pl.BlockSpec((rb, GW), lambda n: (row(n), OGV // GW)),
                  pl.BlockSpec((rb, LANES), lambda n: (row(n), OT // LANES)),
                  pl.BlockSpec((LANES, GQK), lambda n: (0, 0)),
                  pl.BlockSpec((1, GQK), lambda n: (0, 0))],
        out_specs=[pl.BlockSpec((rb, GW), lambda n: (row(n), 0)),
                   pl.BlockSpec((nsub, GH, GDV, LANES), lambda n: (n, 0, 0, 0))],
        out_shape=[jax.ShapeDtypeStruct((s, GW), F32), jax.ShapeDtypeStruct((s // CHUNK, GH, GDV, LANES), F32)],
        scratch_shapes=[pltpu.VMEM((GH, GDV, LANES), F32)],
        name=name, compiler_params=_cp(("arbitrary",)))(proj, proj, proj, proj, wg_pad, bg)


def _gla_bwd(proj, wg_pad, bg, states, d_o, reverse, name):
    s = proj.shape[0]
    nsub, nsteps, row, chunk = _gla_specs(s, reverse)
    rb = nsub * CHUNK

    def body(q_ref, k_ref, v_ref, t_ref, wg_ref, bg_ref, st_ref, do_ref,
             dq_ref, dk_ref, dv_ref, dt_ref, dwg_ref, dbg_ref, dstate, da_buf):
        @pl.when(pl.program_id(0) == 0)
        def _():
            dstate[...] = jnp.zeros_like(dstate)
            dwg_ref[...] = jnp.zeros_like(dwg_ref)
            dbg_ref[...] = jnp.zeros_like(dbg_ref)

        t, a, la = _gla_gates(t_ref, wg_ref, bg_ref)
        cumf, mask = _gla_masks(reverse)
        lane = lax.broadcasted_iota(jnp.int32, (CHUNK, LANES), 1)
        for pi in reversed(range(nsub)):
            rows = slice(chunk(pi) * CHUNK, (chunk(pi) + 1) * CHUNK)
            la_c = la[rows]
            b = _cum_dot(cumf, la_c)
            bl = jnp.sum(la_c, axis=0, keepdims=True)
            q = q_ref[rows, :] * (GDK ** -0.5)
            k = k_ref[rows, :]
            e, ei, ee = jnp.exp(b), jnp.exp(-b), jnp.exp(bl - b)
            qd, ki, kte = q * e, k * ei, k * ee
            decay = jnp.exp(bl)
            for p in range(GH // 2):
                sl = slice(p * LANES, (p + 1) * LANES)
                dqd = jnp.zeros((CHUNK, LANES), F32)
                dki = jnp.zeros((CHUNK, LANES), F32)
                dkte = jnp.zeros((CHUNK, LANES), F32)
                ddecay = jnp.zeros((1, LANES), F32)
                for half in range(2):
                    h = 2 * p + half
                    lm = (lane < GDK) if half == 0 else (lane >= GDK)
                    qd_h = jnp.where(lm, qd[:, sl], 0.0)
                    ki_h = jnp.where(lm, ki[:, sl], 0.0)
                    kte_h = jnp.where(lm, kte[:, sl], 0.0)
                    v_h = v_ref[rows, h * GDV:(h + 1) * GDV]
                    do_h = do_ref[rows, h * GDV:(h + 1) * GDV]
                    st = st_ref[pi, h]
                    dst = dstate[h]
                    a_mat = jnp.where(mask, _dot_nt(qd_h, ki_h), 0.0)
                    da_mat = jnp.where(mask, _dot_nt(do_h, v_h), 0.0)
                    dv_ref[rows, h * GDV:(h + 1) * GDV] = _dot_tn(a_mat, do_h) + _dot_nt(kte_h, dst)
                    dqd += _dot(da_mat, ki_h) + _dot(do_h, st)
                    dki += _dot_tn(da_mat, qd_h)
                    dkte += _dot(v_h, dst)
                    ddecay += jnp.sum(dst * st, axis=0, keepdims=True)
                    dstate[h] = dst * decay[:, sl] + _dot_tn(do_h, qd_h)
                dq_ref[rows, sl] = dqd * e[:, sl] * (GDK ** -0.5)
                dk_ref[rows, sl] = dki * ei[:, sl] + dkte * ee[:, sl]
                db = dqd * qd[:, sl] - dki * ki[:, sl] - dkte * kte[:, sl]
                dbl = jnp.sum(dkte * kte[:, sl], axis=0, keepdims=True) + decay[:, sl] * ddecay
                da_buf[rows, sl] = _cum_dot(cumf, db, True) + dbl
        da = da_buf[...] * (1.0 / GTEMP) * _sigmoid(-a)
        dt_ref[...] = _dot_nt(da, wg_ref[...])
        dwg_ref[...] += _dot_tn(t, da)
        dbg_ref[...] += jnp.sum(da, axis=0, keepdims=True)

    def prow(j):
        return row(nsteps - 1 - j)

    return pl.pallas_call(
        body, grid=(nsteps,),
        in_specs=[pl.BlockSpec((rb, GQK), lambda j: (prow(j), OGQ // GQK)),
                  pl.BlockSpec((rb, GQK), lambda j: (prow(j), OGK // GQK)),
                  pl.BlockSpec((rb, GW), lambda j: (prow(j), OGV // GW)),
                  pl.BlockSpec((rb, LANES), lambda j: (prow(j), OT // LANES)),
                  pl.BlockSpec((LANES, GQK), lambda j: (0, 0)),
                  pl.BlockSpec((1, GQK), lambda j: (0, 0)),
                  pl.BlockSpec((nsub, GH, GDV, LANES), lambda j: (nsteps - 1 - j, 0, 0, 0)),
                  pl.BlockSpec((rb, GW), lambda j: (prow(j), 0))],
        out_specs=[pl.BlockSpec((rb, GQK), lambda j: (prow(j), 0)),
                   pl.BlockSpec((rb, GQK), lambda j: (prow(j), 0)),
                   pl.BlockSpec((rb, GW), lambda j: (prow(j), 0)),
                   pl.BlockSpec((rb, LANES), lambda j: (prow(j), 0)),
                   pl.BlockSpec((LANES, GQK), lambda j: (0, 0)),
                   pl.BlockSpec((1, GQK), lambda j: (0, 0))],
        out_shape=[jax.ShapeDtypeStruct((s, GQK), F32), jax.ShapeDtypeStruct((s, GQK), F32),
                   jax.ShapeDtypeStruct((s, GW), F32), jax.ShapeDtypeStruct((s, LANES), F32),
                   jax.ShapeDtypeStruct((LANES, GQK), F32), jax.ShapeDtypeStruct((1, GQK), F32)],
        scratch_shapes=[pltpu.VMEM((GH, GDV, LANES), F32), pltpu.VMEM((rb, GQK), F32)],
        name=name, compiler_params=_cp(("arbitrary",)))(proj, proj, proj, proj, wg_pad, bg, states, d_o)


def _rot_half(x):
    lane = lax.broadcasted_iota(jnp.int32, x.shape, 1)
    first = (lane % MROPE) < (MROPE // 2)
    return jnp.where(first, -pltpu.roll(x, LANES - MROPE // 2, 1), pltpu.roll(x, MROPE // 2, 1))


def _mla_prep(proj, cos, sin, qg, kvg, w_uq, w_ukv, name):
    s = proj.shape[0]
    tr = _rows(s)

    def body(mq_ref, mkv_ref, t_ref, cos_ref, sin_ref, qg_ref, kvg_ref, wuq_ref, wukv_ref, q_ref, k_ref, v_ref):
        cosv, sinv = cos_ref[...], sin_ref[...]
        lane = lax.broadcasted_iota(jnp.int32, (tr, LANES), 1)

        def rope(xv):
            return xv * cosv + _rot_half(xv) * sinv

        qm = _dot(_rms(mq_ref[...], qg_ref[...]), wuq_ref[...])
        kv = _dot(_rms(mkv_ref[...], kvg_ref[...]), wukv_ref[...])
        kr_lo = jnp.where(lane < MROPE, rope(t_ref[...]), 0.0)
        kr_hi = pltpu.roll(kr_lo, MROPE, 1)
        for p in range(MH // 2):
            r = rope(qm[:, MW + p * LANES:MW + (p + 1) * LANES]).astype(q_ref.dtype)
            q_ref[2 * p, :, LANES:] = r
            q_ref[2 * p + 1, :, LANES:] = r
        for h in range(MH):
            q_ref[h, :, :LANES] = qm[:, h * LANES:(h + 1) * LANES].astype(q_ref.dtype)
            k_ref[h, :, :LANES] = kv[:, 2 * h * LANES:(2 * h + 1) * LANES].astype(k_ref.dtype)
            k_ref[h, :, LANES:] = (kr_lo if h % 2 == 0 else kr_hi).astype(k_ref.dtype)
            v_ref[h] = kv[:, (2 * h + 1) * LANES:(2 * h + 2) * LANES].astype(v_ref.dtype)

    def full(shape):
        return pl.BlockSpec(shape, lambda i: (0,) * len(shape))

    return pl.pallas_call(
        body, grid=(s // tr,),
        in_specs=[pl.BlockSpec((tr, MQL), lambda i: (i, OMQ // MQL)),
                  pl.BlockSpec((tr, MKVL), lambda i: (i, OMKV // MKVL)),
                  pl.BlockSpec((tr, LANES), lambda i: (i, OT // LANES)),
                  pl.BlockSpec((tr, LANES), lambda i: (i, 0)),
                  pl.BlockSpec((tr, LANES), lambda i: (i, 0)),
                  full((1, MQL)), full((1, MKVL)), full((MQL, MQW)), full((MKVL, MKVW))],
        out_specs=[pl.BlockSpec((MH, tr, 2 * LANES), lambda i: (0, i, 0)),
                   pl.BlockSpec((MH, tr, 2 * LANES), lambda i: (0, i, 0)),
                   pl.BlockSpec((MH, tr, LANES), lambda i: (0, i, 0))],
        out_shape=[jax.ShapeDtypeStruct((MH, s, 2 * LANES), MXU), jax.ShapeDtypeStruct((MH, s, 2 * LANES), MXU),
                   jax.ShapeDtypeStruct((MH, s, LANES), MXU)],
        name=name, compiler_params=_cp(("parallel",)))(proj, proj, proj, cos, sin, qg, kvg, w_uq, w_ukv)


def _mla_prep_bwd(proj, cos, sin, qg, kvg, w_uq, w_ukv, d_q, d_k, d_v, name):
    s = proj.shape[0]
    tr = _rows(s)

    def body(mq_ref, mkv_ref, cos_ref, sin_ref, qg_ref, kvg_ref, wuq_ref, wukv_ref, dq_ref, dk_ref, dv_ref,
             dmq_ref, dmkv_ref, dt_ref, dwuq_ref, dwukv_ref, dqg_ref, dkvg_ref):
        @pl.when(pl.program_id(0) == 0)
        def _():
            for r in (dwuq_ref, dwukv_ref, dqg_ref, dkvg_ref):
                r[...] = jnp.zeros_like(r)

        cosv, sinv = cos_ref[...], sin_ref[...]
        lane = lax.broadcasted_iota(jnp.int32, (tr, LANES), 1)
        lo = lane < MROPE

        def unrope(dv):
            return dv * cosv - _rot_half(dv * sinv)

        parts = [dq_ref[h, :, :LANES] for h in range(MH)]
        for p in range(MH // 2):
            parts.append(unrope(jnp.where(lo, dq_ref[2 * p, :, LANES:], dq_ref[2 * p + 1, :, LANES:])))
        d_qm = jnp.concatenate(parts, axis=1)
        mq, qgv = mq_ref[...], qg_ref[...]
        cq = _rms(mq, qgv)
        dwuq_ref[...] += _dot_tn(cq, d_qm)
        dmq, dqg = _rms_bwd(_dot_nt(d_qm, wuq_ref[...]), mq, qgv)
        dmq_ref[...] = dmq
        dqg_ref[...] += dqg

        parts = []
        for h in range(MH):
            parts += [dk_ref[h, :, :LANES], dv_ref[h]]
        d_kv = jnp.concatenate(parts, axis=1)
        mkv, kvgv = mkv_ref[...], kvg_ref[...]
        ckv = _rms(mkv, kvgv)
        dwukv_ref[...] += _dot_tn(ckv, d_kv)
        dmkv, dkvg = _rms_bwd(_dot_nt(d_kv, wukv_ref[...]), mkv, kvgv)
        dmkv_ref[...] = dmkv
        dkvg_ref[...] += dkvg

        even = dk_ref[0, :, LANES:] + dk_ref[2, :, LANES:] + dk_ref[4, :, LANES:]
        odd = dk_ref[1, :, LANES:] + dk_ref[3, :, LANES:] + dk_ref[5, :, LANES:]
        d_kr = jnp.where(lo, even, 0.0) + pltpu.roll(jnp.where(lo, 0.0, odd), MROPE, 1)
        dt_ref[...] = jnp.where(lo, unrope(d_kr), 0.0)

    def full(shape):
        return pl.BlockSpec(shape, lambda i: (0,) * len(shape))

    return pl.pallas_call(
        body, grid=(s // tr,),
        in_specs=[pl.BlockSpec((tr, MQL), lambda i: (i, OMQ // MQL)),
                  pl.BlockSpec((tr, MKVL), lambda i: (i, OMKV // MKVL)),
                  pl.BlockSpec((tr, LANES), lambda i: (i, 0)),
                  pl.BlockSpec((tr, LANES), lambda i: (i, 0)),
                  full((1, MQL)), full((1, MKVL)), full((MQL, MQW)), full((MKVL, MKVW)),
                  pl.BlockSpec((MH, tr, 2 * LANES), lambda i: (0, i, 0)),
                  pl.BlockSpec((MH, tr, 2 * LANES), lambda i: (0, i, 0)),
                  pl.BlockSpec((MH, tr, LANES), lambda i: (0, i, 0))],
        out_specs=[pl.BlockSpec((tr, MQL), lambda i: (i, 0)), pl.BlockSpec((tr, MKVL), lambda i: (i, 0)),
                   pl.BlockSpec((tr, LANES), lambda i: (i, 0)),
                   full((MQL, MQW)), full((MKVL, MKVW)), full((1, MQL)), full((1, MKVL))],
        out_shape=[jax.ShapeDtypeStruct((s, MQL), F32), jax.ShapeDtypeStruct((s, MKVL), F32),
                   jax.ShapeDtypeStruct((s, LANES), F32),
                   jax.ShapeDtypeStruct((MQL, MQW), F32), jax.ShapeDtypeStruct((MKVL, MKVW), F32),
                   jax.ShapeDtypeStruct((1, MQL), F32), jax.ShapeDtypeStruct((1, MKVL), F32)],
        name=name, compiler_params=_cp(("arbitrary",)))(proj, proj, cos, sin, qg, kvg, w_uq, w_ukv, d_q, d_k, d_v)


ATT_SCALE = (MNOPE + MROPE) ** -0.5
ATT_SCALE_LOG2 = ATT_SCALE * 1.4426950408889634
ATT_TQ_FWD, ATT_TQ = 256, 512


def _attn_fwd(q, k, v, name):
    s = q.shape[1]
    tq = min(ATT_TQ_FWD, s)

    def body(q_ref, k_ref, v_ref, o_ref, lse_ref):
        sc = _dot_nt(q_ref[0], k_ref[0])
        m = jnp.max(sc, axis=-1, keepdims=True)
        p = jnp.exp2((sc - m) * ATT_SCALE_LOG2)
        l = jnp.sum(p, axis=-1, keepdims=True)
        o_ref[...] = _dot(p, v_ref[0]) / l
        lse_ref[0] = m * ATT_SCALE_LOG2 + jnp.log2(l)

    return pl.pallas_call(
        body, grid=(MH, s // tq),
        in_specs=[pl.BlockSpec((1, tq, 2 * LANES), lambda h, i: (h, i, 0)),
                  pl.BlockSpec((1, s, 2 * LANES), lambda h, i: (h, 0, 0)),
                  pl.BlockSpec((1, s, LANES), lambda h, i: (h, 0, 0))],
        out_specs=[pl.BlockSpec((tq, LANES), lambda h, i: (i, h)),
                   pl.BlockSpec((1, tq, 1), lambda h, i: (h, i, 0))],
        out_shape=[jax.ShapeDtypeStruct((s, MW), F32), jax.ShapeDtypeStruct((MH, s, 1), F32)],
        name=name, compiler_params=_cp(("parallel", "parallel")))(q, k, v)


def _attn_bwd(q, k, v, o, lse, d_o, name):
    s = q.shape[1]
    tq = min(ATT_TQ, s)

    def body(q_ref, k_ref, v_ref, o_ref, lse_ref, do_ref, dq_ref, dk_ref, dv_ref):
        @pl.when(pl.program_id(1) == 0)
        def _():
            dk_ref[...] = jnp.zeros_like(dk_ref)
            dv_ref[...] = jnp.zeros_like(dv_ref)

        qv, kv, do = q_ref[0], k_ref[0], do_ref[...]
        p = jnp.exp2(_dot_nt(qv, kv) * ATT_SCALE_LOG2 - lse_ref[0])
        delta = jnp.sum(do * o_ref[...], axis=-1, keepdims=True)
        ds = p * (_dot_nt(do, v_ref[0]) - delta)
        dq_ref[0] = _dot(ds, kv) * ATT_SCALE
        dk_ref[0] += _dot_tn(ds, qv) * ATT_SCALE
        dv_ref[0] += _dot_tn(p, do)

    return pl.pallas_call(
        body, grid=(MH, s // tq),
        in_specs=[pl.BlockSpec((1, tq, 2 * LANES), lambda h, i: (h, i, 0)),
                  pl.BlockSpec((1, s, 2 * LANES), lambda h, i: (h, 0, 0)),
                  pl.BlockSpec((1, s, LANES), lambda h, i: (h, 0, 0)),
                  pl.BlockSpec((tq, LANES), lambda h, i: (i, h)),
                  pl.BlockSpec((1, tq, 1), lambda h, i: (h, i, 0)),
                  pl.BlockSpec((tq, LANES), lambda h, i: (i, h))],
        out_specs=[pl.BlockSpec((1, tq, 2 * LANES), lambda h, i: (h, i, 0)),
                   pl.BlockSpec((1, s, 2 * LANES), lambda h, i: (h, 0, 0)),
                   pl.BlockSpec((1, s, LANES), lambda h, i: (h, 0, 0))],
        out_shape=[jax.ShapeDtypeStruct((MH, s, 2 * LANES), F32), jax.ShapeDtypeStruct((MH, s, 2 * LANES), F32),
                   jax.ShapeDtypeStruct((MH, s, LANES), F32)],
        name=name, compiler_params=_cp(("parallel", "arbitrary")))(q, k, v, o, lse, d_o)


def _merge_fwd(o_f, o_b, o_att, pre, proj, gng, mog, cog, name):
    s = proj.shape[0]
    tr = _rows(s)

    def body(of_ref, ob_ref, oa_ref, pre_ref, z_ref, gng_ref, mog_ref, cog_ref, y_ref):
        z = z_ref[...]
        sz = z * _sigmoid(z)
        osum = of_ref[...] + ob_ref[...]
        gg = gng_ref[...]
        for h in range(GH):
            sl = slice(h * GDV, (h + 1) * GDV)
            y_ref[:, sl] = (_rms(osum[:, sl], gg) * sz[:, sl]).astype(y_ref.dtype)
        y_ref[:, GW:GW + MW] = (_rms(oa_ref[...], mog_ref[...]) * sz[:, GW:GW + MW]).astype(y_ref.dtype)
        y_ref[:, GW + MW:] = (_rms(pre_ref[...], cog_ref[...]) * sz[:, GW + MW:]).astype(y_ref.dtype)

    def row(w):
        return pl.BlockSpec((tr, w), lambda i: (i, 0))

    def vec(w):
        return pl.BlockSpec((1, w), lambda i: (0, 0))

    return pl.pallas_call(
        body, grid=(s // tr,),
        in_specs=[row(GW), row(GW), row(MW), row(CONV_CH), row(D_MIX), vec(GDV), vec(MW), vec(CONV_CH)],
        out_specs=row(D_MIX), out_shape=jax.ShapeDtypeStruct((s, D_MIX), MXU),
        name=name, compiler_params=_cp(("parallel",)))(o_f, o_b, o_att, pre, proj, gng, mog, cog)


def _merge_bwd(d_y, o_f, o_b, o_att, pre, proj, gng, mog, cog, name):
    s = proj.shape[0]
    tr = _rows(s)

    def body(dy_ref, of_ref, ob_ref, oa_ref, pre_ref, z_ref, gng_ref, mog_ref, cog_ref,
             dz_ref, dos_ref, doa_ref, dpre_ref, dgng_ref, dmog_ref, dcog_ref):
        @pl.when(pl.program_id(0) == 0)
        def _():
            for r in (dgng_ref, dmog_ref, dcog_ref):
                r[...] = jnp.zeros_like(r)

        z, dy = z_ref[...], dy_ref[...]
        sg = _sigmoid(z)
        sz = z * sg
        dsz = sg * (1.0 + z * (1.0 - sg))
        dcat = dy * sz
        dyz = dy * dsz
        osum = of_ref[...] + ob_ref[...]
        gg = gng_ref[...]
        dgg = jnp.zeros_like(gg)
        for h in range(GH):
            sl = slice(h * GDV, (h + 1) * GDV)
            dz_ref[:, sl] = dyz[:, sl] * _rms(osum[:, sl], gg)
            dx, dg = _rms_bwd(dcat[:, sl], osum[:, sl], gg)
            dos_ref[:, sl] = dx
            dgg += dg
        dgng_ref[...] += dgg
        sl = slice(GW, GW + MW)
        oa, mg = oa_ref[...], mog_ref[...]
        dz_ref[:, sl] = dyz[:, sl] * _rms(oa, mg)
        dx, dg = _rms_bwd(dcat[:, sl], oa, mg)
        doa_ref[...] = dx
        dmog_ref[...] += dg
        sl = slice(GW + MW, D_MIX)
        pv, cg = pre_ref[...], cog_ref[...]
        dz_ref[:, sl] = dyz[:, sl] * _rms(pv, cg)
        dx, dg = _rms_bwd(dcat[:, sl], pv, cg)
        dpre_ref[...] = dx
        dcog_ref[...] += dg

    def row(w):
        return pl.BlockSpec((tr, w), lambda i: (i, 0))

    def vec(w):
        return pl.BlockSpec((1, w), lambda i: (0, 0))

    def rs(w):
        return jax.ShapeDtypeStruct((s, w), F32)

    def vs(w):
        return jax.ShapeDtypeStruct((1, w), F32)

    return pl.pallas_call(
        body, grid=(s // tr,),
        in_specs=[row(D_MIX), row(GW), row(GW), row(MW), row(CONV_CH), row(D_MIX), vec(GDV), vec(MW), vec(CONV_CH)],
        out_specs=[row(D_MIX), row(GW), row(MW), row(CONV_CH), vec(GDV), vec(MW), vec(CONV_CH)],
        out_shape=[rs(D_MIX), rs(GW), rs(MW), rs(CONV_CH), vs(GDV), vs(MW), vs(CONV_CH)],
        name=name, compiler_params=_cp(("arbitrary",)))(d_y, o_f, o_b, o_att, pre, proj, gng, mog, cog)


def _assemble_dproj(d_z, d_cb, d_cc, d_cx, d_mkv, dv_f, dv_b, dq_f, dq_b, dk_f, dk_b, d_mq, dt_m, dt_f, dt_b, name):
    s = d_z.shape[0]
    tr = _rows(s)

    def body(dz, dcb, dcc, dcx, dmkv, dvf, dvb, dqf, dqb, dkf, dkb, dmq, dtm, dtf, dtb, out):
        dt = out.dtype
        out[:, OZ:OZ + D_MIX] = dz[...].astype(dt)
        out[:, OCB:OCB + CONV_CH] = dcb[...].astype(dt)
        out[:, OCC:OCC + CONV_CH] = dcc[...].astype(dt)
        out[:, OCX:OCX + CONV_CH] = dcx[...].astype(dt)
        out[:, OMKV:OMKV + MKVL] = dmkv[...].astype(dt)
        out[:, OGV:OGV + GW] = (dvf[...] + dvb[...]).astype(dt)
        out[:, OGQ:OGQ + GQK] = (dqf[...] + dqb[...]).astype(dt)
        out[:, OGK:OGK + GQK] = (dkf[...] + dkb[...]).astype(dt)
        out[:, OMQ:OMQ + MQL] = dmq[...].astype(dt)
        out[:, OT:OT + LANES] = (dtm[...] + dtf[...] + dtb[...]).astype(dt)

    args = (d_z, d_cb, d_cc, d_cx, d_mkv, dv_f, dv_b, dq_f, dq_b, dk_f, dk_b, d_mq, dt_m, dt_f, dt_b)
    return pl.pallas_call(
        body, grid=(s // tr,),
        in_specs=[pl.BlockSpec((tr, a.shape[1]), lambda i: (i, 0)) for a in args],
        out_specs=pl.BlockSpec((tr, PW), lambda i: (i, 0)),
        out_shape=jax.ShapeDtypeStruct((s, PW), MXU), name=name, compiler_params=_cp(("parallel",)))(*args)


def _layer_fwd(x, mod, wt, cos, sin, tag, late=None, in_after=()):
    shift, scale, gate = mod
    h = _norm_mod(x, wt["norm_g"], scale, shift, f"norm_mod_{tag}")
    (proj,) = _matmul(h, wt["w_in"], dims="nn", tm=2048, tn=256, tk=2048, out_dtypes=(F32,), name=f"in_proj_{tag}",
                      after=in_after)
    if late is not None:
        wt.update(late(proj))
    o_f, st_f = _gla_fwd(proj, wt["wg_pad_f"], wt["bg_f"], False, f"gla_fwd_f_{tag}")
    o_b, st_b = _gla_fwd(proj, wt["wg_pad_b"], wt["bg_b"], True, f"gla_fwd_b_{tag}")
    q, k, v = _mla_prep(proj, cos, sin, wt["q_norm_g"], wt["kv_norm_g"], wt["w_uq"], wt["w_ukv"], f"mla_prep_{tag}")
    o_att, lse = _attn_fwd(q, k, v, f"attn_fwd_{tag}")
    pre = _conv_fwd(proj, wt["conv_w"], f"conv_fwd_{tag}")
    y = _merge_fwd(o_f, o_b, o_att, pre, proj, wt["gla_norm_g"], wt["mla_out_g"], wt["conv_out_g"], f"merge_fwd_{tag}")
    x_new, u = _matmul(y, wt["w_out"], dims="nn", tm=2048, tn=256, tk=2048, out_dtypes=(F32, F32),
                       name=f"out_proj_{tag}", epilogue=lambda acc, xv, gv: (xv + gv * acc, acc),
                       extras=(x, gate), extra_kinds=("mn", "n"))
    saved = dict(x=x, h=h, proj=proj, o_f=o_f, o_b=o_b, st_f=st_f, st_b=st_b, q=q, k=k, v=v,
                 o_att=o_att, lse=lse, pre=pre, y=y, u=u)
    return x_new, saved


def _layer_bwd(d_out, sv, mod, wt, cos, sin, tag, ship=None, dx_first=None):
    shift, scale, gate = mod
    proj = sv["proj"]
    d_u, d_gate = _gate_bwd(d_out, sv["u"], gate, f"gate_bwd_{tag}")
    (g_w_out,) = _matmul(sv["y"], d_u, dims="tn", tm=1024, tn=512, tk=2048, out_dtypes=(MXU,), name=f"out_proj_dw_{tag}")
    (d_y,) = _matmul(d_u, wt["w_out"], dims="nt", tm=2048, tn=256, tk=2048, out_dtypes=(F32,), name=f"out_proj_dx_{tag}",
                     after=(g_w_out,))
    d_z, d_osum, d_oatt, d_pre, d_gng, d_mog, d_cog = _merge_bwd(
        d_y, sv["o_f"], sv["o_b"], sv["o_att"], sv["pre"], proj, wt["gla_norm_g"], wt["mla_out_g"], wt["conv_out_g"],
        f"merge_bwd_{tag}")
    d_cb, d_cc, d_cx, d_conv_w = _conv_bwd(proj, wt["conv_w"], d_pre, f"conv_bwd_{tag}")
    d_q, d_k, d_v = _attn_bwd(sv["q"], sv["k"], sv["v"], sv["o_att"], sv["lse"], d_oatt, f"attn_bwd_{tag}")
    d_mq, d_mkv, dt_m, g_w_uq, g_w_ukv, d_qg, d_kvg = _mla_prep_bwd(
        proj, cos, sin, wt["q_norm_g"], wt["kv_norm_g"], wt["w_uq"], wt["w_ukv"], d_q, d_k, d_v, f"mla_prep_bwd_{tag}")
    dq_f, dk_f, dv_f, dt_f, d_wg_f, d_bg_f = _gla_bwd(proj, wt["wg_pad_f"], wt["bg_f"], sv["st_f"], d_osum, False,
                                                     f"gla_bwd_f_{tag}")
    dq_b, dk_b, dv_b, dt_b, d_wg_b, d_bg_b = _gla_bwd(proj, wt["wg_pad_b"], wt["bg_b"], sv["st_b"], d_osum, True,
                                                     f"gla_bwd_b_{tag}")
    d_proj = _assemble_dproj(d_z, d_cb, d_cc, d_cx, d_mkv, dv_f, dv_b, dq_f, dq_b, dk_f, dk_b, d_mq, dt_m, dt_f, dt_b,
                             f"assemble_dproj_{tag}")
    grads = dict(w_out=g_w_out, w_uq=g_w_uq, w_ukv=g_w_ukv,
                 wg_pad_f=d_wg_f, bg_f=d_bg_f, wg_pad_b=d_wg_b, bg_b=d_bg_b, gla_norm_g=d_gng,
                 q_norm_g=d_qg, kv_norm_g=d_kvg, mla_out_g=d_mog, conv_w=d_conv_w, conv_out_g=d_cog)

    def in_dw(after):
        (g_w_in,) = _matmul(sv["h"], d_proj, dims="tn", tm=2048, tn=256, tk=2048, out_dtypes=(MXU,),
                            name=f"in_proj_dw_{tag}", after=after)
        grads["w_in"] = g_w_in
        return dict(w_in=g_w_in, w_out=g_w_out, w_uq=g_w_uq, w_ukv=g_w_ukv)

    def in_dx(after):
        (d_h,) = _matmul(d_proj, wt["w_in"], dims="nt", tm=1024, tn=512, tk=PW, out_dtypes=(F32,),
                         name=f"in_proj_dx_{tag}", after=after)
        d_x, d_shift, d_scale, d_ng = _norm_mod_bwd(d_h, sv["x"], d_out, wt["norm_g"], scale, f"norm_mod_bwd_{tag}")
        grads["norm_g"] = d_ng
        return d_x, (d_shift, d_scale, d_gate)

    if dx_first is None:
        big = in_dw(())
        d_x, d_mod = in_dx((big["w_in"],) if ship is None else ship(big))
    else:
        d_x, d_mod = in_dx(())
        big = in_dw(dx_first(d_x, d_mod, grads))
        ship(big)
    return d_x, d_mod, grads


def _perm_in_cols(w):
    pad = jnp.zeros(w.shape[:-1] + (PW - IN_DIM,), w.dtype)
    return jnp.concatenate([w[..., 3808:5856], w[..., 2272:3808], w[..., 1952:2208], w[..., 768:1536], w[..., 0:768],
                            w[..., 1568:1952], w[..., 2208:2272], w[..., 1536:1568], pad], axis=-1)


def _unperm_in_cols(g):
    return jnp.concatenate([g[..., OGQ:OGQ + 2 * GQK], g[..., OGV:OGV + GW], g[..., OT + MROPE:OT + MROPE + 2 * GRANK],
                            g[..., OMQ:OMQ + MQL], g[..., OMKV:OMKV + MKVL], g[..., OT:OT + MROPE],
                            g[..., OCB:OCB + 3 * CONV_CH], g[..., OZ:OZ + D_MIX]], axis=-1)


IN_SEGS = ((3808, 5856), (2272, 3808), (1952, 2208), (768, 1536), (0, 768), (1568, 1952), (2208, 2272), (1536, 1568))
UQ_SEGS = (tuple((h * (MNOPE + MROPE), h * (MNOPE + MROPE) + MNOPE) for h in range(MH))
           + tuple((h * (MNOPE + MROPE) + MNOPE, (h + 1) * (MNOPE + MROPE)) for h in range(MH)))


def _perm_gathered(g, segs, width):
    per = g.shape[-1]
    parts, total = [], 0
    for a, b in segs:
        c = a
        while c < b:
            j = c // per
            hi = min(b, (j + 1) * per)
            parts.append(g[j, :, c - j * per:hi - j * per])
            c = hi
        total += b - a
    if width > total:
        parts.append(jnp.zeros((g.shape[1], width - total), g.dtype))
    return jnp.concatenate(parts, axis=1)


def _scatter_perm(gp, segs, per):
    offs, o = [], 0
    for a, b in segs:
        offs.append((a, b, o))
        o += b - a
    blocks = []
    for j in range(N_DEV):
        lo, hi = j * per, (j + 1) * per
        pieces = []
        for a, b, o in sorted(offs):
            s0, s1 = max(a, lo), min(b, hi)
            if s0 < s1:
                pieces.append(gp[:, o + s0 - a:o + s1 - a])
        blocks.append(jnp.concatenate(pieces, axis=1))
    return jnp.stack(blocks)


def _perm_uq_cols(w):
    w3 = w.reshape(w.shape[:-1] + (MH, MNOPE + MROPE))
    return jnp.concatenate([w3[..., :MNOPE].reshape(w.shape[:-1] + (MH * MNOPE,)),
                            w3[..., MNOPE:].reshape(w.shape[:-1] + (MH * MROPE,))], axis=-1)


def _unperm_uq_cols(g):
    nope = g[..., :MH * MNOPE].reshape(g.shape[:-1] + (MH, MNOPE))
    rope = g[..., MH * MNOPE:].reshape(g.shape[:-1] + (MH, MROPE))
    return jnp.concatenate([nope, rope], axis=-1).reshape(g.shape[:-1] + (MQW,))


def _prep_layer_weights(w_in, w_out, w_uq, w_ukv, small):
    def vec(v):
        return v.reshape(1, -1).astype(F32)

    zeros = functools.partial(jnp.zeros, dtype=F32)
    wg_f, wg_b = small["gla_wg_f"].astype(F32), small["gla_wg_b"].astype(F32)
    wg_pad_f = jnp.concatenate([zeros((MROPE, GQK)), wg_f, zeros((LANES - MROPE - GRANK, GQK))], axis=0)
    wg_pad_b = jnp.concatenate([zeros((MROPE + GRANK, GQK)), wg_b, zeros((LANES - MROPE - 2 * GRANK, GQK))], axis=0)
    wt = dict(norm_g=vec(small["norm_g"]), wg_pad_f=wg_pad_f, wg_pad_b=wg_pad_b,
              bg_f=vec(small["gla_bg_f"]), bg_b=vec(small["gla_bg_b"]), gla_norm_g=vec(small["gla_norm_g"]),
              q_norm_g=vec(small["mla_q_norm_g"]), kv_norm_g=vec(small["mla_kv_norm_g"]),
              mla_out_g=vec(small["mla_out_g"]), conv_w=small["conv_w"].astype(F32),
              conv_out_g=vec(small["conv_out_g"]))
    for name, w in (("w_in", w_in), ("w_out", w_out), ("w_uq", w_uq), ("w_ukv", w_ukv)):
        if w is not None:
            wt[name] = w.astype(MXU)
    return wt


def _natural_small(gr):
    return dict(norm_g=gr["norm_g"][0],
                gla_wg_f=gr["wg_pad_f"][MROPE:MROPE + GRANK], gla_bg_f=gr["bg_f"][0],
                gla_wg_b=gr["wg_pad_b"][MROPE + GRANK:MROPE + 2 * GRANK], gla_bg_b=gr["bg_b"][0],
                gla_norm_g=gr["gla_norm_g"][0], mla_q_norm_g=gr["q_norm_g"][0], mla_kv_norm_g=gr["kv_norm_g"][0],
                mla_out_g=gr["mla_out_g"][0], conv_w=gr["conv_w"], conv_out_g=gr["conv_out_g"][0])


def _natural_grads(gr):
    return dict(_natural_small(gr), w_in=_unperm_in_cols(gr["w_in"]), w_out=gr["w_out"],
                mla_w_uq=_unperm_uq_cols(gr["w_uq"]), mla_w_ukv=gr["w_ukv"])


def _exchange(arrs, name, scatter, space):
    n = len(arrs)

    def body(*refs):
        ins, outs = refs[:n], refs[n:2 * n]
        send_sems, recv_sems, loc_sems = refs[2 * n:]
        ax, ay, ac = lax.axis_index("x"), lax.axis_index("y"), lax.axis_index("c")
        me = 4 * ax + 2 * ay + ac

        def src(a, to):
            return ins[a].at[to] if scatter else ins[a]

        def remote(a, r, dst_slot):
            px = 1 - ax if r & 4 else ax
            py = 1 - ay if r & 2 else ay
            pc = 1 - ac if r & 1 else ac
            return pltpu.make_async_remote_copy(
                src_ref=src(a, 4 * px + 2 * py + pc), dst_ref=outs[a].at[dst_slot(4 * px + 2 * py + pc)],
                send_sem=send_sems.at[a, r - 1], recv_sem=recv_sems.at[a, r - 1],
                device_id=(px, py, pc), device_id_type=MESH)

        locs = [pltpu.make_async_copy(src(a, me), outs[a].at[me], loc_sems.at[a]) for a in range(n)]
        for cp in locs:
            cp.start()
        sends = [remote(a, r, lambda peer: me) for r in range(1, N_DEV) for a in range(n)]
        for cp in sends:
            cp.start()
        for r in range(1, N_DEV):
            for a in range(n):
                remote(a, r, lambda peer: peer).wait_recv()
        for cp in sends:
            cp.wait_send()
        for cp in locs:
            cp.wait()

    def out_shape(a):
        return jax.ShapeDtypeStruct(a.shape if scatter else (N_DEV,) + a.shape, a.dtype)

    spec = pl.BlockSpec(memory_space=space)
    return pl.pallas_call(
        body, in_specs=[spec] * n, out_specs=[spec] * n, out_shape=[out_shape(a) for a in arrs],
        scratch_shapes=[pltpu.SemaphoreType.DMA((n, N_DEV - 1)), pltpu.SemaphoreType.DMA((n, N_DEV - 1)),
                        pltpu.SemaphoreType.DMA((n,))],
        name=name, compiler_params=pltpu.CompilerParams(vmem_limit_bytes=VMEM_LIMIT))(*arrs)


def _peer(r):
    ax, ay, ac = lax.axis_index("x"), lax.axis_index("y"), lax.axis_index("c")
    px = 1 - ax if r & 4 else ax
    py = 1 - ay if r & 2 else ay
    pc = 1 - ac if r & 1 else ac
    return (px, py, pc), 4 * px + 2 * py + pc


def _slot(rel_div):
    rel, div = rel_div
    idx = _peer(rel)[1]
    return idx if div == 1 else idx // div


AG_SPREAD = tuple((r, None, (0, 1), (r, 1)) for r in (1, 2, 4, 6))
AG_FORWARD = tuple((1, (k, 1), (k, 1), (1 ^ k, 1)) for k in (2, 4, 6))
RS_PAIR = tuple((1, (1 ^ k, 1), (1 ^ k, 2), (k, 2)) for k in (0, 2, 4, 6))
RS_CHIPS = tuple((r, (r, 2), (0, 2), (r, 2)) for r in (2, 4, 6))


def _plan_copies(plan, n, src_refs, land_refs, send_sems, recv_sems, arriving):
    out = []
    for i, (r, src, dst, recv) in enumerate(plan):
        peer = _peer(r)[0]
        for a in range(n):
            out.append(pltpu.make_async_remote_copy(
                src_ref=src_refs[a] if src is None else src_refs[a].at[_slot(src)],
                dst_ref=land_refs[a].at[_slot(recv if arriving else dst)],
                send_sem=send_sems.at[i * n + a], recv_sem=recv_sems.at[i * n + a],
                device_id=peer, device_id_type=MESH))
    return out


def _exchange_hbm(plan, srcs, lands, name, after=(), own=()):
    n = len(lands)
    fresh = isinstance(lands[0], jax.ShapeDtypeStruct)
    ins = ([] if srcs is None else list(srcs)) + ([] if fresh else list(lands))
    ns = 0 if srcs is None else n
    n_data = len(ins) + len(own)
    ins = ins + list(own) + list(after)

    def body(*refs):
        outs = refs[len(ins):len(ins) + n]
        send_sems, recv_sems, own_sems = refs[-3:]
        src_refs = refs[:n] if srcs is not None else refs[ns:ns + n]
        own_refs = refs[n_data - len(own):n_data]
        me = _peer(0)[1]
        local = [pltpu.make_async_copy(o, outs[a].at[me], own_sems.at[a]) for a, o in enumerate(own_refs)]
        for cp in local:
            cp.start()
        sends = _plan_copies(plan, n, src_refs, outs, send_sems, recv_sems, False)
        for cp in sends:
            cp.start()
        for cp in _plan_copies(plan, n, src_refs, outs, send_sems, recv_sems, True):
            cp.wait_recv()
        for cp in sends:
            cp.wait_send()
        for cp in local:
            cp.wait()

    hbm = pl.BlockSpec(memory_space=pltpu.HBM)
    k = len(plan) * n
    return pl.pallas_call(
        body, name=name, in_specs=[hbm] * n_data + [pl.BlockSpec(memory_space=pl.ANY)] * len(after), out_specs=[hbm] * n,
        out_shape=[jax.ShapeDtypeStruct(a.shape, a.dtype) for a in lands],
        scratch_shapes=[pltpu.SemaphoreType.DMA((k,)), pltpu.SemaphoreType.DMA((k,)),
                        pltpu.SemaphoreType.DMA((max(len(own), 1),))],
        input_output_aliases={} if fresh else {ns + i: i for i in range(n)},
        compiler_params=pltpu.CompilerParams(vmem_limit_bytes=VMEM_LIMIT))(*ins)


def _plan_start(plan, srcs, land_shapes, after, name):
    n = len(srcs)

    def body(*refs):
        src_refs, land_refs = refs[:n], refs[n:2 * n]
        send_sems, recv_sems = refs[2 * n + 1], refs[2 * n + 2]
        for cp in _plan_copies(plan, n, src_refs, land_refs, send_sems, recv_sems, False):
            cp.start()
        refs[-1][...] = jnp.zeros_like(refs[-1])

    hbm = pl.BlockSpec(memory_space=pltpu.HBM)
    sem = pl.BlockSpec(memory_space=pltpu.SEMAPHORE)
    k = len(plan) * n
    srcs = [pltpu.with_memory_space_constraint(a, pltpu.HBM) for a in srcs]
    lands = [pltpu.with_memory_space_constraint(lax.empty(shp, a.dtype), pltpu.HBM) for shp, a in zip(land_shapes, srcs)]
    res = pl.pallas_call(
        body, name=name,
        in_specs=[hbm] * (2 * n) + [pl.BlockSpec(memory_space=pl.ANY)],
        out_specs=[sem, sem] + [hbm] * (2 * n) + [pl.BlockSpec(memory_space=pltpu.VMEM)],
        out_shape=[pltpu.SemaphoreType.DMA((k,)), pltpu.SemaphoreType.DMA((k,))]
        + [pltpu.HBM(a.shape, a.dtype) for a in srcs] + [pltpu.HBM(shp, a.dtype) for shp, a in zip(land_shapes, srcs)]
        + [jax.ShapeDtypeStruct((8, LANES), F32)],
        input_output_aliases={i: 2 + i for i in range(2 * n)},
        compiler_params=pltpu.CompilerParams(has_side_effects=pltpu.SideEffectType.DATAFLOW_SIDE_EFFECTING),
    )(*srcs, *lands, after)
    return res[0], res[1], list(res[2:2 + n]), list(res[2 + n:2 + 2 * n]), res[-1]


def _plan_wait(plan, handle, after, name):
    send_sems, recv_sems, srcs, lands, _ = handle
    n = len(srcs)
    after = list(after)

    def body(*refs):
        src_refs, land_refs = refs[:n], refs[n:2 * n]
        ssem, rsem = refs[2 * n], refs[2 * n + 1]
        for cp in _plan_copies(plan, n, src_refs, land_refs, ssem, rsem, False):
            cp.wait_send()
        for cp in _plan_copies(plan, n, src_refs, land_refs, ssem, rsem, True):
            cp.wait_recv()

    hbm = pl.BlockSpec(memory_space=pltpu.HBM)
    sem = pl.BlockSpec(memory_space=pltpu.SEMAPHORE)
    res = pl.pallas_call(
        body, name=name,
        in_specs=[hbm] * (2 * n) + [sem, sem] + [pl.BlockSpec(memory_space=pl.ANY)] * len(after),
        out_specs=[hbm] * (2 * n),
        out_shape=[pltpu.HBM(a.shape, a.dtype) for a in srcs] + [pltpu.HBM(a.shape, a.dtype) for a in lands],
        input_output_aliases={i: i for i in range(2 * n)},
        compiler_params=pltpu.CompilerParams(has_side_effects=pltpu.SideEffectType.DATAFLOW_SIDE_EFFECTING),
    )(*srcs, *lands, send_sems, recv_sems, *after)
    return list(res[n:])


def _pair_sum(send, got, core, name):
    _, r, c = send.shape
    tr = 256 if r % 256 == 0 else r

    def body(core_ref, s_ref, g_ref, o_ref):
        o_ref[0] = (s_ref[0].astype(F32) + g_ref[0].astype(F32)).astype(o_ref.dtype)

    return pl.pallas_call(
        body, name=name,
        grid_spec=pltpu.PrefetchScalarGridSpec(
            num_scalar_prefetch=1, grid=(N_DEV // 2, r // tr),
            in_specs=[pl.BlockSpec((1, tr, c), lambda kc, i, core_ref: (2 * kc + core_ref[0], i, 0)),
                      pl.BlockSpec((1, tr, c), lambda kc, i, core_ref: (kc, i, 0))],
            out_specs=pl.BlockSpec((1, tr, c), lambda kc, i, core_ref: (kc, i, 0))),
        out_shape=jax.ShapeDtypeStruct((N_DEV // 2, r, c), send.dtype),
        compiler_params=_cp(("parallel", "parallel")))(core, send, got)


def _ada_mod(c_all, ada_w, ada_b_cols, name):
    nl, d, wc = ada_w.shape

    def body(c_ref, w_ref, b_ref, ca_ref, mod_ref):
        cv = c_ref[...]
        ca = cv * _sigmoid(cv)
        ca_ref[...] = ca
        mod_ref[0] = _dotf(ca, w_ref[0]) + b_ref[0]

    return pl.pallas_call(
        body, grid=(nl,),
        in_specs=[pl.BlockSpec((N_DEV, d), lambda l: (0, 0)), pl.BlockSpec((1, d, wc), lambda l: (l, 0, 0)),
                  pl.BlockSpec((1, 1, wc), lambda l: (l, 0, 0))],
        out_specs=[pl.BlockSpec((N_DEV, d), lambda l: (0, 0)), pl.BlockSpec((1, N_DEV, wc), lambda l: (l, 0, 0))],
        out_shape=[jax.ShapeDtypeStruct((N_DEV, d), F32), jax.ShapeDtypeStruct((nl, N_DEV, wc), F32)],
        name=name, compiler_params=_cp(("arbitrary",)))(c_all, ada_w, ada_b_cols)


def _adam(w, g, m, v):
    m2 = ADAM_B1 * m + (1.0 - ADAM_B1) * g
    v2 = ADAM_B2 * v + (1.0 - ADAM_B2) * (g * g)
    m_hat = m2 / (1.0 - ADAM_B1 ** ADAM_STEP)
    v_hat = v2 / (1.0 - ADAM_B2 ** ADAM_STEP)
    delta = -ADAM_LR * (m_hat / (jnp.sqrt(v_hat) + ADAM_EPS) + ADAM_WD * w)
    return delta, m2, v2


def _ada_grad_adam(c_act, d_mod, w, m, v, name):
    nl, d, wc = w.shape
    tk = min(512, d)

    def body(c_ref, dm_ref, w_ref, m_ref, v_ref, g_ref, dl_ref, m2_ref, v2_ref):
        g = _dotf_tn(c_ref[...], dm_ref[0])
        delta, m2, v2 = _adam(w_ref[0], g, m_ref[0], v_ref[0])
        g_ref[0], dl_ref[0], m2_ref[0], v2_ref[0] = g, delta, m2, v2

    blk = pl.BlockSpec((1, tk, wc), lambda l, i: (l, i, 0))
    shp = jax.ShapeDtypeStruct(w.shape, F32)
    return pl.pallas_call(
        body, grid=(nl, d // tk),
        in_specs=[pl.BlockSpec((N_DEV, tk), lambda l, i: (0, i)), pl.BlockSpec((1, N_DEV, wc), lambda l, i: (l, 0, 0)),
                  blk, blk, blk],
        out_specs=[blk] * 4, out_shape=[shp] * 4, name=name,
        compiler_params=_cp(("parallel", "parallel")))(c_act, d_mod, w, m, v)


def _adam_big(recv, w, m, v, layer, prev, name, after=()):
    nl, r, c = w.shape
    tr = 256 if r % 256 == 0 else r
    nparts = recv.shape[0]

    def body(rc_ref, w_ref, m_ref, v_ref, *rest):
        g_ref, dl_ref, m2_ref, v2_ref = rest[-4:]
        g = rc_ref[0].astype(F32)
        for d in range(1, nparts):
            g = g + rc_ref[d].astype(F32)
        delta, m2, v2 = _adam(w_ref[0], g, m_ref[0], v_ref[0])
        g_ref[0], dl_ref[0], m2_ref[0], v2_ref[0] = g, delta, m2, v2

    blk = pl.BlockSpec((1, tr, c), lambda i: (layer, i, 0))
    shp = jax.ShapeDtypeStruct(w.shape, F32)
    prev = () if prev is None else tuple(prev)
    return pl.pallas_call(
        body, grid=(r // tr,),
        in_specs=[pl.BlockSpec((nparts, tr, c), lambda i: (0, i, 0)), blk, blk, blk]
        + [pl.BlockSpec(memory_space=pl.ANY)] * (len(prev) + len(after)),
        out_specs=[blk] * 4, out_shape=[shp] * 4, name=name,
        input_output_aliases={4 + j: j for j in range(len(prev))},
        compiler_params=_cp(("parallel",)))(recv, w, m, v, *prev, *after)


def _sum_devices(gathered, name):
    _, r, c = gathered.shape

    def body(g_ref, o_ref):
        acc = g_ref[0]
        for d in range(1, N_DEV):
            acc = acc + g_ref[d]
        o_ref[...] = acc

    spec = pl.BlockSpec(memory_space=pltpu.VMEM)
    return pl.pallas_call(body, in_specs=[spec], out_specs=spec, out_shape=jax.ShapeDtypeStruct((r, c), F32),
                          name=name, compiler_params=pltpu.CompilerParams(vmem_limit_bytes=VMEM_LIMIT))(gathered)


def _adam_small(ws, gs, ms, vs, name):
    n = len(ws)

    def body(*refs):
        for i in range(n):
            w_ref, g_ref, m_ref, v_ref = (refs[k * n + i] for k in range(4))
            dl_ref, m2_ref, v2_ref = (refs[(4 + k) * n + i] for k in range(3))
            dl_ref[...], m2_ref[...], v2_ref[...] = _adam(w_ref[...], g_ref[...], m_ref[...], v_ref[...])

    spec = pl.BlockSpec(memory_space=pltpu.VMEM)
    shapes = [jax.ShapeDtypeStruct(w.shape, F32) for w in ws]
    res = pl.pallas_call(body, in_specs=[spec] * (4 * n), out_specs=[spec] * (3 * n), out_shape=shapes * 3, name=name,
                         compiler_params=pltpu.CompilerParams(vmem_limit_bytes=VMEM_LIMIT))(*ws, *gs, *ms, *vs)
    return res[:n], res[n:2 * n], res[2 * n:]


def _pack(parts):
    flat = jnp.concatenate([p.reshape(-1).astype(F32) for p in parts])
    assert flat.shape[0] % LANES == 0, flat.shape
    return flat.reshape(-1, LANES)


def _unpack(packed, shapes):
    flat = packed.reshape(-1)
    out, off = [], 0
    for shp in shapes:
        size = 1
        for dim in shp:
            size *= dim
        out.append(flat[off:off + size].reshape(shp))
        off += size
    return out


def _gather_cols(g, per):
    g = jnp.moveaxis(g, 0, -2)
    return g.reshape(g.shape[:-2] + (N_DEV * per,))


def _scatter_cols(g, per):
    return jnp.moveaxis(g.reshape(g.shape[:-1] + (N_DEV, per)), -2, 0)


def _my_cols(full, me, per):
    return lax.dynamic_slice_in_dim(full, me * per, per, axis=full.ndim - 1)


def kernel(x, c, positions, ada_w, ada_b, norm_g, w_in, gla_wg_f, gla_bg_f, gla_wg_b, gla_bg_b, gla_norm_g, mla_q_norm_g, mla_kv_norm_g, mla_w_uq, mla_w_ukv, mla_out_g, conv_w, conv_out_g, w_out, final_g, loss_target, m_ada_w, m_ada_b, m_norm_g, m_w_in, m_gla_wg_f, m_gla_bg_f, m_gla_wg_b, m_gla_bg_b, m_gla_norm_g, m_mla_q_norm_g, m_mla_kv_norm_g, m_mla_w_uq, m_mla_w_ukv, m_mla_out_g, m_conv_w, m_conv_out_g, m_w_out, m_final_g, v_ada_w, v_ada_b, v_norm_g, v_w_in, v_gla_wg_f, v_gla_bg_f, v_gla_wg_b, v_gla_bg_b, v_gla_norm_g, v_mla_q_norm_g, v_mla_kv_norm_g, v_mla_w_uq, v_mla_w_ukv, v_mla_out_g, v_conv_w, v_conv_out_g, v_w_out, v_final_g):
    me = 4 * lax.axis_index("x") + 2 * lax.axis_index("y") + lax.axis_index("c")
    nl = ada_w.shape[0]
    s, d = x.shape[1], x.shape[2]
    ada_cols = ada_w.shape[2]
    wgc, cwc = gla_wg_f.shape[2], conv_w.shape[2]

    (g0,) = _exchange([_pack([c, gla_wg_f, gla_wg_b, conv_w])], "gather_small_in", False, pltpu.VMEM)
    g0 = g0.reshape(N_DEV, -1)
    o1, o2, o3 = d, d + gla_wg_f.size, d + 2 * gla_wg_f.size
    c_all = g0[:, :o1]
    wgf_full = _gather_cols(g0[:, o1:o2].reshape((N_DEV,) + gla_wg_f.shape), wgc)
    wgb_full = _gather_cols(g0[:, o2:o3].reshape((N_DEV,) + gla_wg_b.shape), wgc)
    convw_full = _gather_cols(g0[:, o3:].reshape((N_DEV,) + conv_w.shape), cwc)

    ada_b_cols = _my_cols(ada_b, me, ada_cols).reshape(nl, 1, ada_cols)
    c_act, mod_cols = _ada_mod(c_all, ada_w, ada_b_cols, "ada_mod")
    (g1,) = _exchange([_pack([mod_cols])], "gather_mod", False, pltpu.VMEM)
    mod_all = g1.reshape(N_DEV, nl, N_DEV, ada_cols)
    mod_mine = _gather_cols(lax.dynamic_index_in_dim(mod_all, me, axis=2, keepdims=False), ada_cols)

    inv_freq = ROPE_THETA ** (-jnp.arange(0, MROPE, 2, dtype=F32) / MROPE)
    ang = positions[0].astype(F32)[:, None] * inv_freq
    cos, sin = jnp.tile(jnp.cos(ang), (1, LANES * 2 // MROPE)), jnp.tile(jnp.sin(ang), (1, LANES * 2 // MROPE))

    big = [w_in, w_out, mla_w_uq, mla_w_ukv]
    big_names = ["w_in", "w_out", "mla_w_uq", "mla_w_ukv"]

    def local_blocks(l):
        return [w[l].astype(MXU) for w in big]

    def put_own(lands, own):
        return [lax.dynamic_update_index_in_dim(ld, o, me, 0) for ld, o in zip(lands, own)]

    def layer_weights(l, gw_in=None, gw_out=None, gw_uq=None, gw_ukv=None):
        small = dict(norm_g=norm_g[l], gla_wg_f=wgf_full[l], gla_bg_f=gla_bg_f[l], gla_wg_b=wgb_full[l],
                     gla_bg_b=gla_bg_b[l], gla_norm_g=gla_norm_g[l], mla_q_norm_g=mla_q_norm_g[l],
                     mla_kv_norm_g=mla_kv_norm_g[l], mla_out_g=mla_out_g[l], conv_w=convw_full[l],
                     conv_out_g=conv_out_g[l])
        return _prep_layer_weights(
            None if gw_in is None else _perm_gathered(gw_in, IN_SEGS, PW),
            None if gw_out is None else gw_out.reshape((-1,) + gw_out.shape[2:]),
            None if gw_uq is None else _perm_gathered(gw_uq, UQ_SEGS, MQW),
            None if gw_ukv is None else _gather_cols(gw_ukv, mla_w_ukv.shape[2]), small)

    def land_shapes(blocks, slots):
        return [jax.ShapeDtypeStruct((slots,) + b.shape, b.dtype) for b in blocks]

    def slots_of(blocks):
        return [(N_DEV,) + b.shape for b in blocks]

    def forwarded(lands, blocks, tag):
        return _exchange_hbm(AG_FORWARD, None, lands, f"gather_{tag}_forward", own=blocks)

    first = local_blocks(0)
    w_in_start = _plan_start(AG_SPREAD, first[:1], slots_of(first[:1]), mod_mine, "gather_w_in_l0_start")
    adam_w_in = [a + w_in_start[-1][0, 0] for a in (w_in, m_w_in, v_w_in)]
    lands = _plan_wait(AG_SPREAD, w_in_start, adam_w_in, "gather_w_in_l0_wait")
    (gw_in,) = forwarded(lands, first[:1], "w_in_l0")
    rest = _plan_start(AG_SPREAD, first[1:], slots_of(first[1:]), gw_in, "gather_rest_l0_start")
    h = x[0]
    saved, layers, mods = [], [], []
    pending = {}
    for l in range(nl):
        shift, scale, gate = (mod_mine[l, i * d:(i + 1) * d].reshape(1, d) for i in range(3))
        nxt = local_blocks(l + 1) if l + 1 < nl else None

        def start_next(after, wt_late, l=l, nxt=nxt):
            if nxt is not None:
                pending[l + 1] = _plan_start(AG_SPREAD, nxt, slots_of(nxt), after, f"gather_weights_l{l + 1}_start")
                wt_late["q_norm_g"] = layers[l]["q_norm_g"] + pending[l + 1][-1][0, 0]
            return wt_late

        if l == 0:
            in_after = (rest[-1],)
            layers.append(layer_weights(0, gw_in))

            def late(proj):
                got = forwarded(_plan_wait(AG_SPREAD, rest, [proj], "gather_rest_l0_wait"), first[1:], "rest_l0")
                full = layer_weights(0, None, *got)
                return start_next(got[0], {k: full[k] for k in ("w_out", "w_uq", "w_ukv")})
        else:
            got = forwarded(_plan_wait(AG_SPREAD, pending.pop(l), [h], f"gather_weights_l{l}_wait"), blocks, f"weights_l{l}")
            layers.append(layer_weights(l, *got))
            in_after = ()

            def late(proj):
                return start_next(proj, {})
        mods.append((shift, scale, gate))
        h, sv = _layer_fwd(h, mods[l], layers[l], cos, sin, f"l{l}", late, in_after)
        saved.append(sv)
        blocks = nxt
    loss_part, d_h, d_final_g = _final_loss(h, final_g.reshape(1, d), loss_target[0], "final_loss")
    loss = lax.psum(loss_part[0, 0], ("x", "y", "c"))
    shift, scale, gate = mods[-1]
    mods[-1] = (shift, scale, gate + 0.0 * loss)

    def grad_sends(gr):
        return [_scatter_perm(gr["w_in"], IN_SEGS, w_in.shape[2]).astype(MXU),
                gr["w_out"].reshape((N_DEV,) + w_out.shape[1:]).astype(MXU),
                _scatter_perm(gr["w_uq"], UQ_SEGS, mla_w_uq.shape[2]).astype(MXU),
                _scatter_cols(gr["w_ukv"], mla_w_ukv.shape[2]).astype(MXU)]

    my_chip = me // 2
    my_core = (me % 2).astype(jnp.int32).reshape(1)

    def chip_sums(gr, tag):
        sends = grad_sends(gr)
        got = _exchange_hbm(RS_PAIR, sends, land_shapes([sd[0] for sd in sends], N_DEV // 2), f"scatter_grads_{tag}_pair")
        return [_pair_sum(sd, gt, my_core, f"pair_sum_{n}_{tag}") for sd, gt, n in zip(sends, got, big_names)]

    def with_own_chip(lands, sums):
        return [lax.dynamic_update_index_in_dim(ld, lax.dynamic_index_in_dim(sm, my_chip, axis=0, keepdims=False),
                                                my_chip, 0) for ld, sm in zip(lands, sums)]

    small_names = ["norm_g", "gla_wg_f", "gla_bg_f", "gla_wg_b", "gla_bg_b", "gla_norm_g", "mla_q_norm_g",
                   "mla_kv_norm_g", "mla_out_g", "conv_w", "conv_out_g"]
    d_mods, grads, recv = [None] * nl, [None] * nl, [None] * nl
    flight = {}
    small = {}

    def gather_small(d_x, d_mod0, gr0):
        d_mods[0], grads[0] = d_mod0, _natural_small(gr0)
        d_mod_mine = jnp.stack([jnp.concatenate(d_mods[l], axis=-1)[0] for l in range(nl)])
        parts = [d_mod_mine] + [jnp.stack([grads[l][n] for l in range(nl)]) for n in small_names] + [d_final_g]
        (g2,) = _exchange([_pack(parts)], "gather_small_grads", False, pltpu.VMEM)
        small["d_mod_all"] = g2.reshape(N_DEV, -1)[:, :d_mod_mine.size].reshape(N_DEV, nl, 3 * d)
        small["summed"] = dict(zip(["ada_b"] + small_names + ["final_g"],
                                   _unpack(_sum_devices(g2, "sum_small_grads"), [p.shape for p in parts])))
        return (g2,)

    pairs = {}
    for l in reversed(range(nl)):
        def ship(big_grads, l=l):
            if l + 1 in flight:
                pend, sm = flight.pop(l + 1)
                recv[l + 1] = with_own_chip(_plan_wait(RS_CHIPS, pend, list(big_grads.values()),
                                                       f"scatter_grads_l{l + 1}_wait"), sm)
            if l > 0:
                sends = grad_sends(big_grads)
                pairs[l] = (_plan_start(RS_PAIR, sends, [(N_DEV // 2,) + sd.shape[1:] for sd in sends],
                                        big_grads["w_in"], f"scatter_grads_l{l}_pair_start"), sends)
                return (pairs[l][0][-1],)
            sm = chip_sums(big_grads, f"l{l}")
            flight[l] = (_plan_start(RS_CHIPS, sm, [a.shape for a in sm], recv[l + 1][0] if l + 1 < nl else sm[0],
                                     f"scatter_grads_l{l}_start"), sm)
            return (flight[l][0][-1],)

        shift, scale, gate = mods[l]
        if l + 1 in flight:
            gate = gate + flight[l + 1][0][-1][0, 0]
        if l > 0:
            d_h, d_mods[l], gr = _layer_bwd(d_h, saved[l], (shift, scale, gate), layers[l], cos, sin, f"l{l}", ship)
            grads[l] = _natural_small(gr)
            swap, sends = pairs.pop(l)
            got = _plan_wait(RS_PAIR, swap, [d_h], f"scatter_grads_l{l}_pair_wait")
            sm = [_pair_sum(sd, gt, my_core, f"pair_sum_{n}_l{l}") for sd, gt, n in zip(sends, got, big_names)]
            flight[l] = (_plan_start(RS_CHIPS, sm, [a.shape for a in sm], sm[0], f"scatter_grads_l{l}_start"), sm)
        else:
            d_h, _, _ = _layer_bwd(d_h, saved[l], (shift, scale, gate), layers[l], cos, sin, f"l{l}", ship, gather_small)
    pending, sums = flight.pop(0)
    grad_x = d_h[None]
    summed = small["summed"]
    summed["gla_wg_f"] = _my_cols(summed["gla_wg_f"], me, wgc)
    summed["gla_wg_b"] = _my_cols(summed["gla_wg_b"], me, wgc)
    summed["conv_w"] = _my_cols(summed["conv_w"], me, cwc)

    d_mod_cols = jnp.moveaxis(_my_cols(small["d_mod_all"], me, ada_cols), 0, 1) + pending[-1][0, 0]
    out = {}
    out["ada_w"] = _ada_grad_adam(c_act, d_mod_cols, ada_w, m_ada_w, v_ada_w, "ada_grad_adam")

    given = dict(ada_b=(ada_b, m_ada_b, v_ada_b), norm_g=(norm_g, m_norm_g, v_norm_g),
                 gla_wg_f=(gla_wg_f, m_gla_wg_f, v_gla_wg_f), gla_bg_f=(gla_bg_f, m_gla_bg_f, v_gla_bg_f),
                 gla_wg_b=(gla_wg_b, m_gla_wg_b, v_gla_wg_b), gla_bg_b=(gla_bg_b, m_gla_bg_b, v_gla_bg_b),
                 gla_norm_g=(gla_norm_g, m_gla_norm_g, v_gla_norm_g),
                 mla_q_norm_g=(mla_q_norm_g, m_mla_q_norm_g, v_mla_q_norm_g),
                 mla_kv_norm_g=(mla_kv_norm_g, m_mla_kv_norm_g, v_mla_kv_norm_g),
                 mla_out_g=(mla_out_g, m_mla_out_g, v_mla_out_g), conv_w=(conv_w, m_conv_w, v_conv_w),
                 conv_out_g=(conv_out_g, m_conv_out_g, v_conv_out_g), final_g=(final_g, m_final_g, v_final_g))
    names = list(given)

    def two_d(a):
        return a.reshape(1, -1) if a.ndim == 1 else a

    g_nat = [summed[n].reshape(given[n][0].shape) for n in names]
    res = _adam_small([two_d(given[n][0]) for n in names], [two_d(g) for g in g_nat],
                      [two_d(given[n][1]) for n in names], [two_d(given[n][2]) for n in names], "adam_small")
    for i, n in enumerate(names):
        out[n] = (g_nat[i],) + tuple(r[i].reshape(given[n][0].shape) for r in res)

    state = dict(w_in=adam_w_in, w_out=(w_out, m_w_out, v_w_out), mla_w_uq=(mla_w_uq, m_mla_w_uq, v_mla_w_uq),
                 mla_w_ukv=(mla_w_ukv, m_mla_w_ukv, v_mla_w_ukv))
    done = [out["ada_w"][0], res[0][0]]
    for l in reversed(range(nl)):
        if l == 0:
            recv[0] = with_own_chip(_plan_wait(RS_CHIPS, pending, done, "scatter_grads_l0_wait"), sums)
        for i, n in enumerate(big_names):
            out[n] = _adam_big(recv[l][i], *state[n], l, out.get(n), f"adam_{n}_l{l}",
                               (pending[-1],))
        done = done + [out[n][0] for n in big_names]

    order = ["ada_w", "ada_b", "norm_g", "w_in", "gla_wg_f", "gla_bg_f", "gla_wg_b", "gla_bg_b", "gla_norm_g",
             "mla_q_norm_g", "mla_kv_norm_g", "mla_w_uq", "mla_w_ukv", "mla_out_g", "conv_w", "conv_out_g", "w_out",
             "final_g"]
    return (loss, grad_x, *[out[n][0] for n in order], *[out[n][1] for n in order], *[out[n][2] for n in order],
            *[out[n][3] for n in order])
```

```python
import functools

import jax
import jax.numpy as jnp
from jax import lax
from jax.experimental import pallas as pl
from jax.experimental.pallas import tpu as pltpu

F32 = jnp.float32
MXU = jnp.bfloat16
HI = lax.Precision.HIGHEST
N_DEV = 8
MESH = pl.DeviceIdType.MESH

D_MIX = 2048
GH, GDK, GDV = 6, 64, 128
GW = GH * GDV
GQK = GH * GDK
GRANK = 16
GTEMP = 16.0
CHUNK = 64
MH, MQL, MKVL, MNOPE, MROPE, MDV = 6, 384, 256, 128, 64, 128
MW = MH * MDV
MQW = MH * (MNOPE + MROPE)
MKVW = MH * (MNOPE + MDV)
CONV_CH = 512
ROPE_THETA = 10000.0
EPS = 1e-6
IN_DIM = 5856
OZ, OCB, OCC, OCX, OMKV, OGV, OGQ, OGK, OMQ, OT = 0, 2048, 2560, 3072, 3584, 3840, 4608, 4992, 5376, 5760
PW = 5888
LANES = 128
VMEM_LIMIT = 56 * 1024 * 1024

ADAM_LR, ADAM_B1, ADAM_B2, ADAM_EPS, ADAM_WD, ADAM_STEP = 0.001, 0.9, 0.999, 1e-08, 0.01, 10


def _cp(sem=None):
    return pltpu.CompilerParams(dimension_semantics=sem, vmem_limit_bytes=VMEM_LIMIT)


def _dot(a, b):
    return jnp.dot(a.astype(MXU), b.astype(MXU), preferred_element_type=F32)


def _dot_nt(a, b):
    return lax.dot_general(a.astype(MXU), b.astype(MXU), (((1,), (1,)), ((), ())), preferred_element_type=F32)


def _dot_tn(a, b):
    return lax.dot_general(a.astype(MXU), b.astype(MXU), (((0,), (0,)), ((), ())), preferred_element_type=F32)


def _dotf(a, b):
    return jnp.dot(a, b, precision=HI, preferred_element_type=F32)


def _dotf_nt(a, b):
    return lax.dot_general(a, b, (((1,), (1,)), ((), ())), precision=HI, preferred_element_type=F32)


def _dotf_tn(a, b):
    return lax.dot_general(a, b, (((0,), (0,)), ((), ())), precision=HI, preferred_element_type=F32)


def _split3(x):
    hi = x.astype(jnp.bfloat16)
    r1 = x - hi.astype(F32)
    mid = r1.astype(jnp.bfloat16)
    lo = (r1 - mid.astype(F32)).astype(jnp.bfloat16)
    return hi, mid, lo


def _cum_dot(cum, x, transpose=False):
    dn = (((0,), (0,)), ((), ())) if transpose else (((1,), (0,)), ((), ()))
    cb = cum.astype(jnp.bfloat16)
    parts = [lax.dot_general(cb, p, dn, preferred_element_type=F32) for p in _split3(x)]
    return parts[0] + parts[1] + parts[2]


def _rows(s):
    return min(256, s)


def _rms(x, g):
    r = lax.rsqrt(jnp.mean(x * x, axis=-1, keepdims=True) + EPS)
    return x * r * g


def _rms_bwd(dy, x, g):
    r = lax.rsqrt(jnp.mean(x * x, axis=-1, keepdims=True) + EPS)
    xh = x * r
    dxh = dy * g
    dg = jnp.sum(dy * xh, axis=0, keepdims=True)
    dx = r * (dxh - xh * jnp.mean(dxh * xh, axis=-1, keepdims=True))
    return dx, dg


def _sigmoid(z):
    return 1.0 / (1.0 + jnp.exp(-z))


def _matmul(a, b, *, dims, tm, tn, tk, out_dtypes, name, epilogue=None, extras=(), extra_kinds=(), after=()):
    if dims == "nn":
        (m, k), n, mul = a.shape, b.shape[1], _dot
    elif dims == "nt":
        (m, k), n, mul = a.shape, b.shape[0], _dot_nt
    else:
        (k, m), n, mul = a.shape, b.shape[1], _dot_tn
    tm, tn, tk = min(tm, m), min(tn, n), min(tk, k)
    assert m % tm == 0 and n % tn == 0 and k % tk == 0, (m, n, k, tm, tn, tk)
    if dims == "nn":
        a_spec = pl.BlockSpec((tm, tk), lambda i, j, kk: (i, kk))
        b_spec = pl.BlockSpec((tk, tn), lambda i, j, kk: (kk, j))
    elif dims == "nt":
        a_spec = pl.BlockSpec((tm, tk), lambda i, j, kk: (i, kk))
        b_spec = pl.BlockSpec((tn, tk), lambda i, j, kk: (j, kk))
    else:
        a_spec = pl.BlockSpec((tk, tm), lambda i, j, kk: (kk, i))
        b_spec = pl.BlockSpec((tk, tn), lambda i, j, kk: (kk, j))
    nk = k // tk
    n_extra = len(extras)
    n_out = len(out_dtypes)
    n_after = len(after)
    extra_specs = []
    for kind in extra_kinds:
        if kind == "mn":
            extra_specs.append(pl.BlockSpec((tm, tn), lambda i, j, kk: (i, j)))
        else:
            extra_specs.append(pl.BlockSpec((1, tn), lambda i, j, kk: (0, j)))

    def finish(res, ex, outs):
        vals = (res,) if epilogue is None else epilogue(res, *[e[...] for e in ex])
        for o, v in zip(outs, vals):
            o[...] = v.astype(o.dtype)

    def body(*refs):
        a_ref, b_ref = refs[0], refs[1]
        ex = refs[2:2 + n_extra]
        outs = refs[2 + n_extra + n_after:2 + n_extra + n_after + n_out]
        if nk == 1:
            finish(mul(a_ref[...], b_ref[...]), ex, outs)
            return
        acc = refs[-1]
        kk = pl.program_id(2)

        @pl.when(kk == 0)
        def _():
            acc[...] = jnp.zeros_like(acc)

        acc[...] += mul(a_ref[...], b_ref[...])

        @pl.when(kk == nk - 1)
        def _():
            finish(acc[...], ex, outs)

    out_spec = pl.BlockSpec((tm, tn), lambda i, j, kk: (i, j))
    res = pl.pallas_call(
        body, grid=(m // tm, n // tn, nk),
        in_specs=[a_spec, b_spec] + extra_specs + [pl.BlockSpec(memory_space=pl.ANY)] * n_after,
        out_specs=[out_spec] * n_out,
        out_shape=[jax.ShapeDtypeStruct((m, n), dt) for dt in out_dtypes],
        scratch_shapes=[] if nk == 1 else [pltpu.VMEM((tm, tn), F32)],
        name=name, compiler_params=_cp(("parallel", "parallel", "arbitrary")),
    )(a, b, *extras, *after)
    return res


def _norm_mod(x, g, scale, shift, name):
    s, d = x.shape
    tr = _rows(s)

    def body(x_ref, g_ref, sc_ref, sh_ref, h_ref):
        h = _rms(x_ref[...], g_ref[...]) * (1.0 + sc_ref[...]) + sh_ref[...]
        h_ref[...] = h.astype(h_ref.dtype)

    row = pl.BlockSpec((tr, d), lambda i: (i, 0))
    vec = pl.BlockSpec((1, d), lambda i: (0, 0))
    return pl.pallas_call(body, grid=(s // tr,), in_specs=[row, vec, vec, vec], out_specs=row,
                          out_shape=jax.ShapeDtypeStruct((s, d), MXU), name=name,
                          compiler_params=_cp(("parallel",)))(x, g, scale, shift)


def _norm_mod_bwd(d_h, x, d_out, g, scale, name):
    s, d = x.shape
    tr = _rows(s)

    def body(dh_ref, x_ref, do_ref, g_ref, sc_ref, dx_ref, dsh_ref, dsc_ref, dg_ref):
        i = pl.program_id(0)

        @pl.when(i == 0)
        def _():
            dsh_ref[...] = jnp.zeros_like(dsh_ref)
            dsc_ref[...] = jnp.zeros_like(dsc_ref)
            dg_ref[...] = jnp.zeros_like(dg_ref)

        dh = dh_ref[...]
        xv = x_ref[...]
        gv = g_ref[...]
        r = lax.rsqrt(jnp.mean(xv * xv, axis=-1, keepdims=True) + EPS)
        xh = xv * r
        dsh_ref[...] += jnp.sum(dh, axis=0, keepdims=True)
        dsc_ref[...] += jnp.sum(dh * (xh * gv), axis=0, keepdims=True)
        dhn = dh * (1.0 + sc_ref[...])
        dg_ref[...] += jnp.sum(dhn * xh, axis=0, keepdims=True)
        dxh = dhn * gv
        dx_ref[...] = do_ref[...] + r * (dxh - xh * jnp.mean(dxh * xh, axis=-1, keepdims=True))

    row = pl.BlockSpec((tr, d), lambda i: (i, 0))
    vec = pl.BlockSpec((1, d), lambda i: (0, 0))
    vshape = jax.ShapeDtypeStruct((1, d), F32)
    return pl.pallas_call(body, grid=(s // tr,), in_specs=[row, row, row, vec, vec],
                          out_specs=[row, vec, vec, vec],
                          out_shape=[jax.ShapeDtypeStruct((s, d), F32), vshape, vshape, vshape],
                          name=name, compiler_params=_cp(("arbitrary",)))(d_h, x, d_out, g, scale)


def _gate_bwd(d_out, u, gate, name):
    s, d = d_out.shape
    tr = _rows(s)

    def body(do_ref, u_ref, gt_ref, du_ref, dgt_ref):
        @pl.when(pl.program_id(0) == 0)
        def _():
            dgt_ref[...] = jnp.zeros_like(dgt_ref)

        do = do_ref[...]
        du_ref[...] = (do * gt_ref[...]).astype(du_ref.dtype)
        dgt_ref[...] += jnp.sum(do * u_ref[...], axis=0, keepdims=True)

    row = pl.BlockSpec((tr, d), lambda i: (i, 0))
    vec = pl.BlockSpec((1, d), lambda i: (0, 0))
    return pl.pallas_call(body, grid=(s // tr,), in_specs=[row, row, vec], out_specs=[row, vec],
                          out_shape=[jax.ShapeDtypeStruct((s, d), MXU), jax.ShapeDtypeStruct((1, d), F32)],
                          name=name, compiler_params=_cp(("arbitrary",)))(d_out, u, gate)


def _final_loss(x, g, target, name):
    s, d = x.shape
    tr = _rows(s)

    def body(x_ref, g_ref, t_ref, loss_ref, dx_ref, dg_ref):
        @pl.when(pl.program_id(0) == 0)
        def _():
            loss_ref[...] = jnp.zeros_like(loss_ref)
            dg_ref[...] = jnp.zeros_like(dg_ref)

        xv = x_ref[...]
        gv = g_ref[...]
        diff = _rms(xv, gv) - t_ref[...]
        part = 0.5 * jnp.sum(jnp.sum(diff * diff, axis=-1, keepdims=True) / d, axis=0, keepdims=True)
        loss_ref[...] += jnp.broadcast_to(part, loss_ref.shape)
        dx, dg = _rms_bwd(diff / d, xv, gv)
        dx_ref[...] = dx
        dg_ref[...] += dg

    row = pl.BlockSpec((tr, d), lambda i: (i, 0))
    vec = pl.BlockSpec((1, d), lambda i: (0, 0))
    lvec = pl.BlockSpec((1, LANES), lambda i: (0, 0))
    return pl.pallas_call(body, grid=(s // tr,), in_specs=[row, vec, row], out_specs=[lvec, row, vec],
                          out_shape=[jax.ShapeDtypeStruct((1, LANES), F32), jax.ShapeDtypeStruct((s, d), F32),
                                     jax.ShapeDtypeStruct((1, d), F32)],
                          name=name, compiler_params=_cp(("arbitrary",)))(x, g, target)


def _shift_rows(u, s, down):
    ri = lax.broadcasted_iota(jnp.int32, u.shape, 0)
    if down:
        return jnp.where(ri == 0, 0.0, pltpu.roll(u, 1, 0))
    return jnp.where(ri == s - 1, 0.0, pltpu.roll(u, s - 1, 0))


def _conv_fwd(proj, conv_w, name):
    s = proj.shape[0]
    nt = CONV_CH // LANES

    def body(cb_ref, cc_ref, cx_ref, w_ref, pre_ref):
        u = cc_ref[...] * cx_ref[...]
        conv = _shift_rows(u, s, True) * w_ref[0:1, :] + u * w_ref[1:2, :] + _shift_rows(u, s, False) * w_ref[2:3, :]
        pre_ref[...] = cb_ref[...] * conv

    def col(off):
        return pl.BlockSpec((s, LANES), lambda j: (0, off // LANES + j))

    return pl.pallas_call(body, grid=(nt,), in_specs=[col(OCB), col(OCC), col(OCX), pl.BlockSpec((3, LANES), lambda j: (0, j))],
                          out_specs=pl.BlockSpec((s, LANES), lambda j: (0, j)),
                          out_shape=jax.ShapeDtypeStruct((s, CONV_CH), F32), name=name,
                          compiler_params=_cp(("parallel",)))(proj, proj, proj, conv_w)


def _conv_bwd(proj, conv_w, d_pre, name):
    s = proj.shape[0]
    nt = CONV_CH // LANES

    def body(cb_ref, cc_ref, cx_ref, w_ref, dp_ref, dcb_ref, dcc_ref, dcx_ref, dw_ref):
        cc, cx = cc_ref[...], cx_ref[...]
        u = cc * cx
        up, dn = _shift_rows(u, s, True), _shift_rows(u, s, False)
        w0, w1, w2 = w_ref[0:1, :], w_ref[1:2, :], w_ref[2:3, :]
        conv = up * w0 + u * w1 + dn * w2
        dp = dp_ref[...]
        dcb_ref[...] = dp * conv
        dconv = dp * cb_ref[...]
        du = _shift_rows(dconv, s, False) * w0 + dconv * w1 + _shift_rows(dconv, s, True) * w2
        dcc_ref[...] = du * cx
        dcx_ref[...] = du * cc
        dw_ref[0:1, :] = jnp.sum(dconv * up, axis=0, keepdims=True)
        dw_ref[1:2, :] = jnp.sum(dconv * u, axis=0, keepdims=True)
        dw_ref[2:3, :] = jnp.sum(dconv * dn, axis=0, keepdims=True)

    def col(off):
        return pl.BlockSpec((s, LANES), lambda j: (0, off // LANES + j))

    blk = pl.BlockSpec((s, LANES), lambda j: (0, j))
    wblk = pl.BlockSpec((3, LANES), lambda j: (0, j))
    full = jax.ShapeDtypeStruct((s, CONV_CH), F32)
    return pl.pallas_call(body, grid=(nt,), in_specs=[col(OCB), col(OCC), col(OCX), wblk, blk],
                          out_specs=[blk, blk, blk, wblk],
                          out_shape=[full, full, full, jax.ShapeDtypeStruct((3, CONV_CH), F32)],
                          name=name, compiler_params=_cp(("parallel",)))(proj, proj, proj, conv_w, d_pre)


GLA_SUB = 8


def _gla_gates(t_ref, wg_ref, bg_ref):
    t = t_ref[...]
    a = _dot(t, wg_ref[...]) + bg_ref[...]
    la = (jnp.minimum(a, 0.0) - jnp.log(1.0 + jnp.exp(-jnp.abs(a)))) / GTEMP
    return t, a, la


def _gla_masks(reverse):
    ri = lax.broadcasted_iota(jnp.int32, (CHUNK, CHUNK), 0)
    ci = lax.broadcasted_iota(jnp.int32, (CHUNK, CHUNK), 1)
    if reverse:
        cum, mask = ci >= ri, ci > ri
    else:
        cum, mask = ci <= ri, ci <= ri
    return cum.astype(F32), mask


def _gla_specs(s, reverse):
    nsub = min(GLA_SUB, s // CHUNK)
    nsteps = s // (CHUNK * nsub)

    def row(n):
        return nsteps - 1 - n if reverse else n

    def chunk(pi):
        return nsub - 1 - pi if reverse else pi

    return nsub, nsteps, row, chunk


def _gla_fwd(proj, wg_pad, bg, reverse, name):
    s = proj.shape[0]
    nsub, nsteps, row, chunk = _gla_specs(s, reverse)
    rb = nsub * CHUNK

    def body(q_ref, k_ref, v_ref, t_ref, wg_ref, bg_ref, o_ref, st_ref, state):
        @pl.when(pl.program_id(0) == 0)
        def _():
            state[...] = jnp.zeros_like(state)

        _, _, la = _gla_gates(t_ref, wg_ref, bg_ref)
        cumf, mask = _gla_masks(reverse)
        lane = lax.broadcasted_iota(jnp.int32, (CHUNK, LANES), 1)
        for pi in range(nsub):
            rows = slice(chunk(pi) * CHUNK, (chunk(pi) + 1) * CHUNK)
            la_c = la[rows]
            b = _cum_dot(cumf, la_c)
            bl = jnp.sum(la_c, axis=0, keepdims=True)
            q = q_ref[rows, :] * (GDK ** -0.5)
            k = k_ref[rows, :]
            qd = q * jnp.exp(b)
            ki = k * jnp.exp(-b)
            kte = k * jnp.exp(bl - b)
            decay = jnp.exp(bl)
            for h in range(GH):
                p = h // 2
                sl = slice(p * LANES, (p + 1) * LANES)
                lm = (lane < GDK) if h % 2 == 0 else (lane >= GDK)
                qd_h = jnp.where(lm, qd[:, sl], 0.0)
                kte_h = jnp.where(lm, kte[:, sl], 0.0)
                v_h = v_ref[rows, h * GDV:(h + 1) * GDV]
                st = state[h]
                a_mat = jnp.where(mask, _dot_nt(qd_h, ki[:, sl]), 0.0)
                o_ref[rows, h * GDV:(h + 1) * GDV] = _dot(a_mat, v_h) + _dot_nt(qd_h, st)
                st_ref[pi, h] = st
                state[h] = st * decay[:, sl] + _dot_tn(v_h, kte_h)

    return pl.pallas_call(
        body, grid=(nsteps,),
        in_specs=[pl.BlockSpec((rb, GQK), lambda n: (row(n), OGQ // GQK)),
                  pl.BlockSpec((rb, GQK), lambda n: (row(n), OGK // GQK)),
                  pl.BlockSpec((rb, GW), lambda n: (row(n), OGV // GW)),
                  pl.BlockSpec((rb, LANES), lambda n: (row(n), OT // LANES)),
                  pl.BlockSpec((LANES, GQK), lambda n: (0, 0)),
                  pl.BlockSpec((1, GQK), lambda n: (0, 0))],
        out_specs=[pl.BlockSpec((rb, GW), lambda n: (row(n), 0)),
                   pl.BlockSpec((nsub, GH, GDV, LANES), lambda n: (n, 0, 0, 0))],
        out_shape=[jax.ShapeDtypeStruct((s, GW), F32), jax.ShapeDtypeStruct((s // CHUNK, GH, GDV, LANES), F32)],
        scratch_shapes=[pltpu.VMEM((GH, GDV, LANES), F32)],
        name=name, compiler_params=_cp(("arbitrary",)))(proj, proj, proj, proj, wg_pad, bg)


def _gla_bwd(proj, wg_pad, bg, states, d_o, reverse, name):
    s = proj.shape[0]
    nsub, nsteps, row, chunk = _gla_specs(s, reverse)
    rb = nsub * CHUNK

    def body(q_ref, k_ref, v_ref, t_ref, wg_ref, bg_ref, st_ref, do_ref,
             dq_ref, dk_ref, dv_ref, dt_ref, dwg_ref, dbg_ref, dstate, da_buf):
        @pl.when(pl.program_id(0) == 0)
        def _():
            dstate[...] = jnp.zeros_like(dstate)
            dwg_ref[...] = jnp.zeros_like(dwg_ref)
            dbg_ref[...] = jnp.zeros_like(dbg_ref)

        t, a, la = _gla_gates(t_ref, wg_ref, bg_ref)
        cumf, mask = _gla_masks(reverse)
        lane = lax.broadcasted_iota(jnp.int32, (CHUNK, LANES), 1)
        for pi in reversed(range(nsub)):
            rows = slice(chunk(pi) * CHUNK, (chunk(pi) + 1) * CHUNK)
            la_c = la[rows]
            b = _cum_dot(cumf, la_c)
            bl = jnp.sum(la_c, axis=0, keepdims=True)
            q = q_ref[rows, :] * (GDK ** -0.5)
            k = k_ref[rows, :]
            e, ei, ee = jnp.exp(b), jnp.exp(-b), jnp.exp(bl - b)
            qd, ki, kte = q * e, k * ei, k * ee
            decay = jnp.exp(bl)
            for p in range(GH // 2):
                sl = slice(p * LANES, (p + 1) * LANES)
                dqd = jnp.zeros((CHUNK, LANES), F32)
                dki = jnp.zeros((CHUNK, LANES), F32)
                dkte = jnp.zeros((CHUNK, LANES), F32)
                ddecay = jnp.zeros((1, LANES), F32)
                for half in range(2):
                    h = 2 * p + half
                    lm = (lane < GDK) if half == 0 else (lane >= GDK)
                    qd_h = jnp.where(lm, qd[:, sl], 0.0)
                    ki_h = jnp.where(lm, ki[:, sl], 0.0)
                    kte_h = jnp.where(lm, kte[:, sl], 0.0)
                    v_h = v_ref[rows, h * GDV:(h + 1) * GDV]
                    do_h = do_ref[rows, h * GDV:(h + 1) * GDV]
                    st = st_ref[pi, h]
                    dst = dstate[h]
                    a_mat = jnp.where(mask, _dot_nt(qd_h, ki_h), 0.0)
                    da_mat = jnp.where(mask, _dot_nt(do_h, v_h), 0.0)
                    dv_ref[rows, h * GDV:(h + 1) * GDV] = _dot_tn(a_mat, do_h) + _dot_nt(kte_h, dst)
                    dqd += _dot(da_mat, ki_h) + _dot(do_h, st)
                    dki += _dot_tn(da_mat, qd_h)
                    dkte += _dot(v_h, dst)
                    ddecay += jnp.sum(dst * st, axis=0, keepdims=True)
                    dstate[h] = dst * decay[:, sl] + _dot_tn(do_h, qd_h)
                dq_ref[rows, sl] = dqd * e[:, sl] * (GDK ** -0.5)
                dk_ref[rows, sl] = dki * ei[:, sl] + dkte * ee[:, sl]
                db = dqd * qd[:, sl] - dki * ki[:, sl] - dkte * kte[:, sl]
                dbl = jnp.sum(dkte * kte[:, sl], axis=0, keepdims=True) + decay[:, sl] * ddecay
                da_buf[rows, sl] = _cum_dot(cumf, db, True) + dbl
        da = da_buf[...] * (1.0 / GTEMP) * _sigmoid(-a)
        dt_ref[...] = _dot_nt(da, wg_ref[...])
        dwg_ref[...] += _dot_tn(t, da)
        dbg_ref[...] += jnp.sum(da, axis=0, keepdims=True)

    def prow(j):
        return row(nsteps - 1 - j)

    return pl.pallas_call(
        body, grid=(nsteps,),
        in_specs=[pl.BlockSpec((rb, GQK), lambda j: (prow(j), OGQ // GQK)),
                  pl.BlockSpec((rb, GQK), lambda j: (prow(j), OGK // GQK)),
                  pl.BlockSpec((rb, GW), lambda j: (prow(j), OGV // GW)),
                  pl.BlockSpec((rb, LANES), lambda j: (prow(j), OT // LANES)),
                  pl.BlockSpec((LANES, GQK), lambda j: (0, 0)),
                  pl.BlockSpec((1, GQK), lambda j: (0, 0)),
                  pl.BlockSpec((nsub, GH, GDV, LANES), lambda j: (nsteps - 1 - j, 0, 0, 0)),
                  pl.BlockSpec((rb, GW), lambda j: (prow(j), 0))],
        out_specs=[pl.BlockSpec((rb, GQK), lambda j: (prow(j), 0)),
                   pl.BlockSpec((rb, GQK), lambda j: (prow(j), 0)),
                   pl.BlockSpec((rb, GW), lambda j: (prow(j), 0)),
                   pl.BlockSpec((rb, LANES), lambda j: (prow(j), 0)),
                   pl.BlockSpec((LANES, GQK), lambda j: (0, 0)),
                   pl.BlockSpec((1, GQK), lambda j: (0, 0))],
        out_shape=[jax.ShapeDtypeStruct((s, GQK), F32), jax.ShapeDtypeStruct((s, GQK), F32),
                   jax.ShapeDtypeStruct((s, GW), F32), jax.ShapeDtypeStruct((s, LANES), F32),
                   jax.ShapeDtypeStruct((LANES, GQK), F32), jax.ShapeDtypeStruct((1, GQK), F32)],
        scratch_shapes=[pltpu.VMEM((GH, GDV, LANES), F32), pltpu.VMEM((rb, GQK), F32)],
        name=name, compiler_params=_cp(("arbitrary",)))(proj, proj, proj, proj, wg_pad, bg, states, d_o)


def _rot_half(x):
    lane = lax.broadcasted_iota(jnp.int32, x.shape, 1)
    first = (lane % MROPE) < (MROPE // 2)
    return jnp.where(first, -pltpu.roll(x, LANES - MROPE // 2, 1), pltpu.roll(x, MROPE // 2, 1))


def _mla_prep(proj, cos, sin, qg, kvg, w_uq, w_ukv, name):
    s = proj.shape[0]
    tr = _rows(s)

    def body(mq_ref, mkv_ref, t_ref, cos_ref, sin_ref, qg_ref, kvg_ref, wuq_ref, wukv_ref, q_ref, k_ref, v_ref):
        cosv, sinv = cos_ref[...], sin_ref[...]
        lane = lax.broadcasted_iota(jnp.int32, (tr, LANES), 1)

        def rope(xv):
            return xv * cosv + _rot_half(xv) * sinv

        qm = _dot(_rms(mq_ref[...], qg_ref[...]), wuq_ref[...])
        kv = _dot(_rms(mkv_ref[...], kvg_ref[...]), wukv_ref[...])
        kr_lo = jnp.where(lane < MROPE, rope(t_ref[...]), 0.0)
        kr_hi = pltpu.roll(kr_lo, MROPE, 1)
        for p in range(MH // 2):
            r = rope(qm[:, MW + p * LANES:MW + (p + 1) * LANES]).astype(q_ref.dtype)
            q_ref[2 * p, :, LANES:] = r
            q_ref[2 * p + 1, :, LANES:] = r
        for h in range(MH):
            q_ref[h, :, :LANES] = qm[:, h * LANES:(h + 1) * LANES].astype(q_ref.dtype)
            k_ref[h, :, :LANES] = kv[:, 2 * h * LANES:(2 * h + 1) * LANES].astype(k_ref.dtype)
            k_ref[h, :, LANES:] = (kr_lo if h % 2 == 0 else kr_hi).astype(k_ref.dtype)
            v_ref[h] = kv[:, (2 * h + 1) * LANES:(2 * h + 2) * LANES].astype(v_ref.dtype)

    def full(shape):
        return pl.BlockSpec(shape, lambda i: (0,) * len(shape))

    return pl.pallas_call(
        body, grid=(s // tr,),
        in_specs=[pl.BlockSpec((tr, MQL), lambda i: (i, OMQ // MQL)),
                  pl.BlockSpec((tr, MKVL), lambda i: (i, OMKV // MKVL)),
                  pl.BlockSpec((tr, LANES), lambda i: (i, OT // LANES)),
                  pl.BlockSpec((tr, LANES), lambda i: (i, 0)),
                  pl.BlockSpec((tr, LANES), lambda i: (i, 0)),
                  full((1, MQL)), full((1, MKVL)), full((MQL, MQW)), full((MKVL, MKVW))],
        out_specs=[pl.BlockSpec((MH, tr, 2 * LANES), lambda i: (0, i, 0)),
                   pl.BlockSpec((MH, tr, 2 * LANES), lambda i: (0, i, 0)),
                   pl.BlockSpec((MH, tr, LANES), lambda i: (0, i, 0))],
        out_shape=[jax.ShapeDtypeStruct((MH, s, 2 * LANES), MXU), jax.ShapeDtypeStruct((MH, s, 2 * LANES), MXU),
                   jax.ShapeDtypeStruct((MH, s, LANES), MXU)],
        name=name, compiler_params=_cp(("parallel",)))(proj, proj, proj, cos, sin, qg, kvg, w_uq, w_ukv)


def _mla_prep_bwd(proj, cos, sin, qg, kvg, w_uq, w_ukv, d_q, d_k, d_v, name):
    s = proj.shape[0]
    tr = _rows(s)

    def body(mq_ref, mkv_ref, cos_ref, sin_ref, qg_ref, kvg_ref, wuq_ref, wukv_ref, dq_ref, dk_ref, dv_ref,
             dmq_ref, dmkv_ref, dt_ref, dwuq_ref, dwukv_ref, dqg_ref, dkvg_ref):
        @pl.when(pl.program_id(0) == 0)
        def _():
            for r in (dwuq_ref, dwukv_ref, dqg_ref, dkvg_ref):
                r[...] = jnp.zeros_like(r)

        cosv, sinv = cos_ref[...], sin_ref[...]
        lane = lax.broadcasted_iota(jnp.int32, (tr, LANES), 1)
        lo = lane < MROPE

        def unrope(dv):
            return dv * cosv - _rot_half(dv * sinv)

        parts = [dq_ref[h, :, :LANES] for h in range(MH)]
        for p in range(MH // 2):
            parts.append(unrope(jnp.where(lo, dq_ref[2 * p, :, LANES:], dq_ref[2 * p + 1, :, LANES:])))
        d_qm = jnp.concatenate(parts, axis=1)
        mq, qgv = mq_ref[...], qg_ref[...]
        cq = _rms(mq, qgv)
        dwuq_ref[...] += _dot_tn(cq, d_qm)
        dmq, dqg = _rms_bwd(_dot_nt(d_qm, wuq_ref[...]), mq, qgv)
        dmq_ref[...] = dmq
        dqg_ref[...] += dqg

        parts = []
        for h in range(MH):
            parts += [dk_ref[h, :, :LANES], dv_ref[h]]
        d_kv = jnp.concatenate(parts, axis=1)
        mkv, kvgv = mkv_ref[...], kvg_ref[...]
        ckv = _rms(mkv, kvgv)
        dwukv_ref[...] += _dot_tn(ckv, d_kv)
        dmkv, dkvg = _rms_bwd(_dot_nt(d_kv, wukv_ref[...]), mkv, kvgv)
        dmkv_ref[...] = dmkv
        dkvg_ref[...] += dkvg

        even = dk_ref[0, :, LANES:] + dk_ref[2, :, LANES:] + dk_ref[4, :, LANES:]
        odd = dk_ref[1, :, LANES:] + dk_ref[3, :, LANES:] + dk_ref[5, :, LANES:]
        d_kr = jnp.where(lo, even, 0.0) + pltpu.roll(jnp.where(lo, 0.0, odd), MROPE, 1)
        dt_ref[...] = jnp.where(lo, unrope(d_kr), 0.0)

    def full(shape):
        return pl.BlockSpec(shape, lambda i: (0,) * len(shape))

    return pl.pallas_call(
        body, grid=(s // tr,),
        in_specs=[pl.BlockSpec((tr, MQL), lambda i: (i, OMQ // MQL)),
                  pl.BlockSpec((tr, MKVL), lambda i: (i, OMKV // MKVL)),
                  pl.BlockSpec((tr, LANES), lambda i: (i, 0)),
                  pl.BlockSpec((tr, LANES), lambda i: (i, 0)),
                  full((1, MQL)), full((1, MKVL)), full((MQL, MQW)), full((MKVL, MKVW)),
                  pl.BlockSpec((MH, tr, 2 * LANES), lambda i: (0, i, 0)),
                  pl.BlockSpec((MH, tr, 2 * LANES), lambda i: (0, i, 0)),
                  pl.BlockSpec((MH, tr, LANES), lambda i: (0, i, 0))],
        out_specs=[pl.BlockSpec((tr, MQL), lambda i: (i, 0)), pl.BlockSpec((tr, MKVL), lambda i: (i, 0)),
                   pl.BlockSpec((tr, LANES), lambda i: (i, 0)),
                   full((MQL, MQW)), full((MKVL, MKVW)), full((1, MQL)), full((1, MKVL))],
        out_shape=[jax.ShapeDtypeStruct((s, MQL), F32), jax.ShapeDtypeStruct((s, MKVL), F32),
                   jax.ShapeDtypeStruct((s, LANES), F32),
                   jax.ShapeDtypeStruct((MQL, MQW), F32), jax.ShapeDtypeStruct((MKVL, MKVW), F32),
                   jax.ShapeDtypeStruct((1, MQL), F32), jax.ShapeDtypeStruct((1, MKVL), F32)],
        name=name, compiler_params=_cp(("arbitrary",)))(proj, proj, cos, sin, qg, kvg, w_uq, w_ukv, d_q, d_k, d_v)


ATT_SCALE = (MNOPE + MROPE) ** -0.5
ATT_SCALE_LOG2 = ATT_SCALE * 1.4426950408889634
ATT_TQ_FWD, ATT_TQ = 256, 512


def _attn_fwd(q, k, v, name):
    s = q.shape[1]
    tq = min(ATT_TQ_FWD, s)

    def body(q_ref, k_ref, v_ref, o_ref, lse_ref):
        sc = _dot_nt(q_ref[0], k_ref[0])
        m = jnp.max(sc, axis=-1, keepdims=True)
        p = jnp.exp2((sc - m) * ATT_SCALE_LOG2)
        l = jnp.sum(p, axis=-1, keepdims=True)
        o_ref[...] = _dot(p, v_ref[0]) / l
        lse_ref[0] = m * ATT_SCALE_LOG2 + jnp.log2(l)

    return pl.pallas_call(
        body, grid=(MH, s // tq),
        in_specs=[pl.BlockSpec((1, tq, 2 * LANES), lambda h, i: (h, i, 0)),
                  pl.BlockSpec((1, s, 2 * LANES), lambda h, i: (h, 0, 0)),
                  pl.BlockSpec((1, s, LANES), lambda h, i: (h, 0, 0))],
        out_specs=[pl.BlockSpec((tq, LANES), lambda h, i: (i, h)),
                   pl.BlockSpec((1, tq, 1), lambda h, i: (h, i, 0))],
        out_shape=[jax.ShapeDtypeStruct((s, MW), F32), jax.ShapeDtypeStruct((MH, s, 1), F32)],
        name=name, compiler_params=_cp(("parallel", "parallel")))(q, k, v)


def _attn_bwd(q, k, v, o, lse, d_o, name):
    s = q.shape[1]
    tq = min(ATT_TQ, s)

    def body(q_ref, k_ref, v_ref, o_ref, lse_ref, do_ref, dq_ref, dk_ref, dv_ref):
        @pl.when(pl.program_id(1) == 0)
        def _():
            dk_ref[...] = jnp.zeros_like(dk_ref)
            dv_ref[...] = jnp.zeros_like(dv_ref)

        qv, kv, do = q_ref[0], k_ref[0], do_ref[...]
        p = jnp.exp2(_dot_nt(qv, kv) * ATT_SCALE_LOG2 - lse_ref[0])
        delta = jnp.sum(do * o_ref[...], axis=-1, keepdims=True)
        ds = p * (_dot_nt(do, v_ref[0]) - delta)
        dq_ref[0] = _dot(ds, kv) * ATT_SCALE
        dk_ref[0] += _dot_tn(ds, qv) * ATT_SCALE
        dv_ref[0] += _dot_tn(p, do)

    return pl.pallas_call(
        body, grid=(MH, s // tq),
        in_specs=[pl.BlockSpec((1, tq, 2 * LANES), lambda h, i: (h, i, 0)),
                  pl.BlockSpec((1, s, 2 * LANES), lambda h, i: (h, 0, 0)),
                  pl.BlockSpec((1, s, LANES), lambda h, i: (h, 0, 0)),
                  pl.BlockSpec((tq, LANES), lambda h, i: (i, h)),
                  pl.BlockSpec((1, tq, 1), lambda h, i: (h, i, 0)),
                  pl.BlockSpec((tq, LANES), lambda h, i: (i, h))],
        out_specs=[pl.BlockSpec((1, tq, 2 * LANES), lambda h, i: (h, i, 0)),
                   pl.BlockSpec((1, s, 2 * LANES), lambda h, i: (h, 0, 0)),
                   pl.BlockSpec((1, s, LANES), lambda h, i: (h, 0, 0))],
        out_shape=[jax.ShapeDtypeStruct((MH, s, 2 * LANES), F32), jax.ShapeDtypeStruct((MH, s, 2 * LANES), F32),
                   jax.ShapeDtypeStruct((MH, s, LANES), F32)],
        name=name, compiler_params=_cp(("parallel", "arbitrary")))(q, k, v, o, lse, d_o)


def _merge_fwd(o_f, o_b, o_att, pre, proj, gng, mog, cog, name):
    s = proj.shape[0]
    tr = _rows(s)

    def body(of_ref, ob_ref, oa_ref, pre_ref, z_ref, gng_ref, mog_ref, cog_ref, y_ref):
        z = z_ref[...]
        sz = z * _sigmoid(z)
        osum = of_ref[...] + ob_ref[...]
        gg = gng_ref[...]
        for h in range(GH):
            sl = slice(h * GDV, (h + 1) * GDV)
            y_ref[:, sl] = (_rms(osum[:, sl], gg) * sz[:, sl]).astype(y_ref.dtype)
        y_ref[:, GW:GW + MW] = (_rms(oa_ref[...], mog_ref[...]) * sz[:, GW:GW + MW]).astype(y_ref.dtype)
        y_ref[:, GW + MW:] = (_rms(pre_ref[...], cog_ref[...]) * sz[:, GW + MW:]).astype(y_ref.dtype)

    def row(w):
        return pl.BlockSpec((tr, w), lambda i: (i, 0))

    def vec(w):
        return pl.BlockSpec((1, w), lambda i: (0, 0))

    return pl.pallas_call(
        body, grid=(s // tr,),
        in_specs=[row(GW), row(GW), row(MW), row(CONV_CH), row(D_MIX), vec(GDV), vec(MW), vec(CONV_CH)],
        out_specs=row(D_MIX), out_shape=jax.ShapeDtypeStruct((s, D_MIX), MXU),
        name=name, compiler_params=_cp(("parallel",)))(o_f, o_b, o_att, pre, proj, gng, mog, cog)


def _merge_bwd(d_y, o_f, o_b, o_att, pre, proj, gng, mog, cog, name):
    s = proj.shape[0]
    tr = _rows(s)

    def body(dy_ref, of_ref, ob_ref, oa_ref, pre_ref, z_ref, gng_ref, mog_ref, cog_ref,
             dz_ref, dos_ref, doa_ref, dpre_ref, dgng_ref, dmog_ref, dcog_ref):
        @pl.when(pl.program_id(0) == 0)
        def _():
            for r in (dgng_ref, dmog_ref, dcog_ref):
                r[...] = jnp.zeros_like(r)

        z, dy = z_ref[...], dy_ref[...]
        sg = _sigmoid(z)
        sz = z * sg
        dsz = sg * (1.0 + z * (1.0 - sg))
        dcat = dy * sz
        dyz = dy * dsz
        osum = of_ref[...] + ob_ref[...]
        gg = gng_ref[...]
        dgg = jnp.zeros_like(gg)
        for h in range(GH):
            sl = slice(h * GDV, (h + 1) * GDV)
            dz_ref[:, sl] = dyz[:, sl] * _rms(osum[:, sl], gg)
            dx, dg = _rms_bwd(dcat[:, sl], osum[:, sl], gg)
            dos_ref[:, sl] = dx
            dgg += dg
        dgng_ref[...] += dgg
        sl = slice(GW, GW + MW)
        oa, mg = oa_ref[...], mog_ref[...]
        dz_ref[:, sl] = dyz[:, sl] * _rms(oa, mg)
        dx, dg = _rms_bwd(dcat[:, sl], oa, mg)
        doa_ref[...] = dx
        dmog_ref[...] += dg
        sl = slice(GW + MW, D_MIX)
        pv, cg = pre_ref[...], cog_ref[...]
        dz_ref[:, sl] = dyz[:, sl] * _rms(pv, cg)
        dx, dg = _rms_bwd(dcat[:, sl], pv, cg)
        dpre_ref[...] = dx
        dcog_ref[...] += dg

    def row(w):
        return pl.BlockSpec((tr, w), lambda i: (i, 0))

    def vec(w):
        return pl.BlockSpec((1, w), lambda i: (0, 0))

    def rs(w):
        return jax.ShapeDtypeStruct((s, w), F32)

    def vs(w):
        return jax.ShapeDtypeStruct((1, w), F32)

    return pl.pallas_call(
        body, grid=(s // tr,),
        in_specs=[row(D_MIX), row(GW), row(GW), row(MW), row(CONV_CH), row(D_MIX), vec(GDV), vec(MW), vec(CONV_CH)],
        out_specs=[row(D_MIX), row(GW), row(MW), row(CONV_CH), vec(GDV), vec(MW), vec(CONV_CH)],
        out_shape=[rs(D_MIX), rs(GW), rs(MW), rs(CONV_CH), vs(GDV), vs(MW), vs(CONV_CH)],
        name=name, compiler_params=_cp(("arbitrary",)))(d_y, o_f, o_b, o_att, pre, proj, gng, mog, cog)


def _assemble_dproj(d_z, d_cb, d_cc, d_cx, d_mkv, dv_f, dv_b, dq_f, dq_b, dk_f, dk_b, d_mq, dt_m, dt_f, dt_b, name):
    s = d_z.shape[0]
    tr = _rows(s)

    def body(dz, dcb, dcc, dcx, dmkv, dvf, dvb, dqf, dqb, dkf, dkb, dmq, dtm, dtf, dtb, out):
        dt = out.dtype
        out[:, OZ:OZ + D_MIX] = dz[...].astype(dt)
        out[:, OCB:OCB + CONV_CH] = dcb[...].astype(dt)
        out[:, OCC:OCC + CONV_CH] = dcc[...].astype(dt)
        out[:, OCX:OCX + CONV_CH] = dcx[...].astype(dt)
        out[:, OMKV:OMKV + MKVL] = dmkv[...].astype(dt)
        out[:, OGV:OGV + GW] = (dvf[...] + dvb[...]).astype(dt)
        out[:, OGQ:OGQ + GQK] = (dqf[...] + dqb[...]).astype(dt)
        out[:, OGK:OGK + GQK] = (dkf[...] + dkb[...]).astype(dt)
        out[:, OMQ:OMQ + MQL] = dmq[...].astype(dt)
        out[:, OT:OT + LANES] = (dtm[...] + dtf[...] + dtb[...]).astype(dt)

    args = (d_z, d_cb, d_cc, d_cx, d_mkv, dv_f, dv_b, dq_f, dq_b, dk_f, dk_b, d_mq, dt_m, dt_f, dt_b)
    return pl.pallas_call(
        body, grid=(s // tr,),
        in_specs=[pl.BlockSpec((tr, a.shape[1]), lambda i: (i, 0)) for a in args],
        out_specs=pl.BlockSpec((tr, PW), lambda i: (i, 0)),
        out_shape=jax.ShapeDtypeStruct((s, PW), MXU), name=name, compiler_params=_cp(("parallel",)))(*args)


def _layer_fwd(x, mod, wt, cos, sin, tag, late=None, in_after=()):
    shift, scale, gate = mod
    h = _norm_mod(x, wt["norm_g"], scale, shift, f"norm_mod_{tag}")
    (proj,) = _matmul(h, wt["w_in"], dims="nn", tm=2048, tn=256, tk=2048, out_dtypes=(F32,), name=f"in_proj_{tag}",
                      after=in_after)
    if late is not None:
        wt.update(late(proj))
    o_f, st_f = _gla_fwd(proj, wt["wg_pad_f"], wt["bg_f"], False, f"gla_fwd_f_{tag}")
    o_b, st_b = _gla_fwd(proj, wt["wg_pad_b"], wt["bg_b"], True, f"gla_fwd_b_{tag}")
    q, k, v = _mla_prep(proj, cos, sin, wt["q_norm_g"], wt["kv_norm_g"], wt["w_uq"], wt["w_ukv"], f"mla_prep_{tag}")
    o_att, lse = _attn_fwd(q, k, v, f"attn_fwd_{tag}")
    pre = _conv_fwd(proj, wt["conv_w"], f"conv_fwd_{tag}")
    y = _merge_fwd(o_f, o_b, o_att, pre, proj, wt["gla_norm_g"], wt["mla_out_g"], wt["conv_out_g"], f"merge_fwd_{tag}")
    x_new, u = _matmul(y, wt["w_out"], dims="nn", tm=2048, tn=256, tk=2048, out_dtypes=(F32, F32),
                       name=f"out_proj_{tag}", epilogue=lambda acc, xv, gv: (xv + gv * acc, acc),
                       extras=(x, gate), extra_kinds=("mn", "n"))
    saved = dict(x=x, h=h, proj=proj, o_f=o_f, o_b=o_b, st_f=st_f, st_b=st_b, q=q, k=k, v=v,
                 o_att=o_att, lse=lse, pre=pre, y=y, u=u)
    return x_new, saved


def _layer_bwd(d_out, sv, mod, wt, cos, sin, tag, ship=None, dx_first=None):
    shift, scale, gate = mod
    proj = sv["proj"]
    d_u, d_gate = _gate_bwd(d_out, sv["u"], gate, f"gate_bwd_{tag}")
    (g_w_out,) = _matmul(sv["y"], d_u, dims="tn", tm=1024, tn=512, tk=2048, out_dtypes=(MXU,), name=f"out_proj_dw_{tag}")
    (d_y,) = _matmul(d_u, wt["w_out"], dims="nt", tm=2048, tn=256, tk=2048, out_dtypes=(F32,), name=f"out_proj_dx_{tag}",
                     after=(g_w_out,))
    d_z, d_osum, d_oatt, d_pre, d_gng, d_mog, d_cog = _merge_bwd(
        d_y, sv["o_f"], sv["o_b"], sv["o_att"], sv["pre"], proj, wt["gla_norm_g"], wt["mla_out_g"], wt["conv_out_g"],
        f"merge_bwd_{tag}")
    d_cb, d_cc, d_cx, d_conv_w = _conv_bwd(proj, wt["conv_w"], d_pre, f"conv_bwd_{tag}")
    d_q, d_k, d_v = _attn_bwd(sv["q"], sv["k"], sv["v"], sv["o_att"], sv["lse"], d_oatt, f"attn_bwd_{tag}")
    d_mq, d_mkv, dt_m, g_w_uq, g_w_ukv, d_qg, d_kvg = _mla_prep_bwd(
        proj, cos, sin, wt["q_norm_g"], wt["kv_norm_g"], wt["w_uq"], wt["w_ukv"], d_q, d_k, d_v, f"mla_prep_bwd_{tag}")
    dq_f, dk_f, dv_f, dt_f, d_wg_f, d_bg_f = _gla_bwd(proj, wt["wg_pad_f"], wt["bg_f"], sv["st_f"], d_osum, False,
                                                     f"gla_bwd_f_{tag}")
    dq_b, dk_b, dv_b, dt_b, d_wg_b, d_bg_b = _gla_bwd(proj, wt["wg_pad_b"], wt["bg_b"], sv["st_b"], d_osum, True,
                                                     f"gla_bwd_b_{tag}")
    d_proj = _assemble_dproj(d_z, d_cb, d_cc, d_cx, d_mkv, dv_f, dv_b, dq_f, dq_b, dk_f, dk_b, d_mq, dt_m, dt_f, dt_b,
                             f"assemble_dproj_{tag}")
    grads = dict(w_out=g_w_out, w_uq=g_w_uq, w_ukv=g_w_ukv,
                 wg_pad_f=d_wg_f, bg_f=d_bg_f, wg_pad_b=d_wg_b, bg_b=d_bg_b, gla_norm_g=d_gng,
                 q_norm_g=d_qg, kv_norm_g=d_kvg, mla_out_g=d_mog, conv_w=d_conv_w, conv_out_g=d_cog)

    def in_dw(after):
        (g_w_in,) = _matmul(sv["h"], d_proj, dims="tn", tm=2048, tn=256, tk=2048, out_dtypes=(MXU,),
                            name=f"in_proj_dw_{tag}", after=after)
        grads["w_in"] = g_w_in
        return dict(w_in=g_w_in, w_out=g_w_out, w_uq=g_w_uq, w_ukv=g_w_ukv)

    def in_dx(after):
        (d_h,) = _matmul(d_proj, wt["w_in"], dims="nt", tm=1024, tn=512, tk=PW, out_dtypes=(F32,),
                         name=f"in_proj_dx_{tag}", after=after)
        d_x, d_shift, d_scale, d_ng = _norm_mod_bwd(d_h, sv["x"], d_out, wt["norm_g"], scale, f"norm_mod_bwd_{tag}")
        grads["norm_g"] = d_ng
        return d_x, (d_shift, d_scale, d_gate)

    if dx_first is None:
        big = in_dw(())
        d_x, d_mod = in_dx((big["w_in"],) if ship is None else ship(big))
    else:
        d_x, d_mod = in_dx(())
        big = in_dw(dx_first(d_x, d_mod, grads))
        ship(big)
    return d_x, d_mod, grads


def _perm_in_cols(w):
    pad = jnp.zeros(w.shape[:-1] + (PW - IN_DIM,), w.dtype)
    return jnp.concatenate([w[..., 3808:5856], w[..., 2272:3808], w[..., 1952:2208], w[..., 768:1536], w[..., 0:768],
                            w[..., 1568:1952], w[..., 2208:2272], w[..., 1536:1568], pad], axis=-1)


def _unperm_in_cols(g):
    return jnp.concatenate([g[..., OGQ:OGQ + 2 * GQK], g[..., OGV:OGV + GW], g[..., OT + MROPE:OT + MROPE + 2 * GRANK],
                            g[..., OMQ:OMQ + MQL], g[..., OMKV:OMKV + MKVL], g[..., OT:OT + MROPE],
                            g[..., OCB:OCB + 3 * CONV_CH], g[..., OZ:OZ + D_MIX]], axis=-1)


IN_SEGS = ((3808, 5856), (2272, 3808), (1952, 2208), (768, 1536), (0, 768), (1568, 1952), (2208, 2272), (1536, 1568))
UQ_SEGS = (tuple((h * (MNOPE + MROPE), h * (MNOPE + MROPE) + MNOPE) for h in range(MH))
           + tuple((h * (MNOPE + MROPE) + MNOPE, (h + 1) * (MNOPE + MROPE)) for h in range(MH)))


def _perm_gathered(g, segs, width):
    per = g.shape[-1]
    parts, total = [], 0
    for a, b in segs:
        c = a
        while c < b:
            j = c // per
            hi = min(b, (j + 1) * per)
            parts.append(g[j, :, c - j * per:hi - j * per])
            c = hi
        total += b - a
    if width > total:
        parts.append(jnp.zeros((g.shape[1], width - total), g.dtype))
    return jnp.concatenate(parts, axis=1)


def _scatter_perm(gp, segs, per):
    offs, o = [], 0
    for a, b in segs:
        offs.append((a, b, o))
        o += b - a
    blocks = []
    for j in range(N_DEV):
        lo, hi = j * per, (j + 1) * per
        pieces = []
        for a, b, o in sorted(offs):
            s0, s1 = max(a, lo), min(b, hi)
            if s0 < s1:
                pieces.append(gp[:, o + s0 - a:o + s1 - a])
        blocks.append(jnp.concatenate(pieces, axis=1))
    return jnp.stack(blocks)


def _perm_uq_cols(w):
    w3 = w.reshape(w.shape[:-1] + (MH, MNOPE + MROPE))
    return jnp.concatenate([w3[..., :MNOPE].reshape(w.shape[:-1] + (MH * MNOPE,)),
                            w3[..., MNOPE:].reshape(w.shape[:-1] + (MH * MROPE,))], axis=-1)


def _unperm_uq_cols(g):
    nope = g[..., :MH * MNOPE].reshape(g.shape[:-1] + (MH, MNOPE))
    rope = g[..., MH * MNOPE:].reshape(g.shape[:-1] + (MH, MROPE))
    return jnp.concatenate([nope, rope], axis=-1).reshape(g.shape[:-1] + (MQW,))


def _prep_layer_weights(w_in, w_out, w_uq, w_ukv, small):
    def vec(v):
        return v.reshape(1, -1).astype(F32)

    zeros = functools.partial(jnp.zeros, dtype=F32)
    wg_f, wg_b = small["gla_wg_f"].astype(F32), small["gla_wg_b"].astype(F32)
    wg_pad_f = jnp.concatenate([zeros((MROPE, GQK)), wg_f, zeros((LANES - MROPE - GRANK, GQK))], axis=0)
    wg_pad_b = jnp.concatenate([zeros((MROPE + GRANK, GQK)), wg_b, zeros((LANES - MROPE - 2 * GRANK, GQK))], axis=0)
    wt = dict(norm_g=vec(small["norm_g"]), wg_pad_f=wg_pad_f, wg_pad_b=wg_pad_b,
              bg_f=vec(small["gla_bg_f"]), bg_b=vec(small["gla_bg_b"]), gla_norm_g=vec(small["gla_norm_g"]),
              q_norm_g=vec(small["mla_q_norm_g"]), kv_norm_g=vec(small["mla_kv_norm_g"]),
              mla_out_g=vec(small["mla_out_g"]), conv_w=small["conv_w"].astype(F32),
              conv_out_g=vec(small["conv_out_g"]))
    for name, w in (("w_in", w_in), ("w_out", w_out), ("w_uq", w_uq), ("w_ukv", w_ukv)):
        if w is not None:
            wt[name] = w.astype(MXU)
    return wt


def _natural_small(gr):
    return dict(norm_g=gr["norm_g"][0],
                gla_wg_f=gr["wg_pad_f"][MROPE:MROPE + GRANK], gla_bg_f=gr["bg_f"][0],
                gla_wg_b=gr["wg_pad_b"][MROPE + GRANK:MROPE + 2 * GRANK], gla_bg_b=gr["bg_b"][0],
                gla_norm_g=gr["gla_norm_g"][0], mla_q_norm_g=gr["q_norm_g"][0], mla_kv_norm_g=gr["kv_norm_g"][0],
                mla_out_g=gr["mla_out_g"][0], conv_w=gr["conv_w"], conv_out_g=gr["conv_out_g"][0])


def _natural_grads(gr):
    return dict(_natural_small(gr), w_in=_unperm_in_cols(gr["w_in"]), w_out=gr["w_out"],
                mla_w_uq=_unperm_uq_cols(gr["w_uq"]), mla_w_ukv=gr["w_ukv"])


def _exchange(arrs, name, scatter, space):
    n = len(arrs)

    def body(*refs):
        ins, outs = refs[:n], refs[n:2 * n]
        send_sems, recv_sems, loc_sems = refs[2 * n:]
        ax, ay, ac = lax.axis_index("x"), lax.axis_index("y"), lax.axis_index("c")
        me = 4 * ax + 2 * ay + ac

        def src(a, to):
            return ins[a].at[to] if scatter else ins[a]

        def remote(a, r, dst_slot):
            px = 1 - ax if r & 4 else ax
            py = 1 - ay if r & 2 else ay
            pc = 1 - ac if r & 1 else ac
            return pltpu.make_async_remote_copy(
                src_ref=src(a, 4 * px + 2 * py + pc), dst_ref=outs[a].at[dst_slot(4 * px + 2 * py + pc)],
                send_sem=send_sems.at[a, r - 1], recv_sem=recv_sems.at[a, r - 1],
                device_id=(px, py, pc), device_id_type=MESH)

        locs = [pltpu.make_async_copy(src(a, me), outs[a].at[me], loc_sems.at[a]) for a in range(n)]
        for cp in locs:
            cp.start()
        sends = [remote(a, r, lambda peer: me) for r in range(1, N_DEV) for a in range(n)]
        for cp in sends:
            cp.start()
        for r in range(1, N_DEV):
            for a in range(n):
                remote(a, r, lambda peer: peer).wait_recv()
        for cp in sends:
            cp.wait_send()
        for cp in locs:
            cp.wait()

    def out_shape(a):
        return jax.ShapeDtypeStruct(a.shape if scatter else (N_DEV,) + a.shape, a.dtype)

    spec = pl.BlockSpec(memory_space=space)
    return pl.pallas_call(
        body, in_specs=[spec] * n, out_specs=[spec] * n, out_shape=[out_shape(a) for a in arrs],
        scratch_shapes=[pltpu.SemaphoreType.DMA((n, N_DEV - 1)), pltpu.SemaphoreType.DMA((n, N_DEV - 1)),
                        pltpu.SemaphoreType.DMA((n,))],
        name=name, compiler_params=pltpu.CompilerParams(vmem_limit_bytes=VMEM_LIMIT))(*arrs)


def _peer(r):
    ax, ay, ac = lax.axis_index("x"), lax.axis_index("y"), lax.axis_index("c")
    px = 1 - ax if r & 4 else ax
    py = 1 - ay if r & 2 else ay
    pc = 1 - ac if r & 1 else ac
    return (px, py, pc), 4 * px + 2 * py + pc


def _slot(rel_div):
    rel, div = rel_div
    idx = _peer(rel)[1]
    return idx if div == 1 else idx // div


AG_SPREAD = tuple((r, None, (0, 1), (r, 1)) for r in (1, 2, 4, 6))
AG_FORWARD = tuple((1, (k, 1), (k, 1), (1 ^ k, 1)) for k in (2, 4, 6))
RS_PAIR = tuple((1, (1 ^ k, 1), (1 ^ k, 2), (k, 2)) for k in (0, 2, 4, 6))
RS_CHIPS = tuple((r, (r, 2), (0, 2), (r, 2)) for r in (2, 4, 6))


def _plan_copies(plan, n, src_refs, land_refs, send_sems, recv_sems, arriving):
    out = []
    for i, (r, src, dst, recv) in enumerate(plan):
        peer = _peer(r)[0]
        for a in range(n):
            out.append(pltpu.make_async_remote_copy(
                src_ref=src_refs[a] if src is None else src_refs[a].at[_slot(src)],
                dst_ref=land_refs[a].at[_slot(recv if arriving else dst)],
                send_sem=send_sems.at[i * n + a], recv_sem=recv_sems.at[i * n + a],
                device_id=peer, device_id_type=MESH))
    return out


def _exchange_hbm(plan, srcs, lands, name, after=()):
    n = len(lands)
    fresh = isinstance(lands[0], jax.ShapeDtypeStruct)
    ins = ([] if srcs is None else list(srcs)) + ([] if fresh else list(lands))
    ns = 0 if srcs is None else n
    n_data = len(ins)
    ins = ins + list(after)

    def body(*refs):
        outs = refs[len(ins):len(ins) + n]
        send_sems, recv_sems = refs[-2:]
        src_refs = refs[:n] if srcs is not None else refs[ns:ns + n]
        sends = _plan_copies(plan, n, src_refs, outs, send_sems, recv_sems, False)
        for cp in sends:
            cp.start()
        for cp in _plan_copies(plan, n, src_refs, outs, send_sems, recv_sems, True):
            cp.wait_recv()
        for cp in sends:
            cp.wait_send()

    hbm = pl.BlockSpec(memory_space=pltpu.HBM)
    k = len(plan) * n
    return pl.pallas_call(
        body, name=name, in_specs=[hbm] * n_data + [pl.BlockSpec(memory_space=pl.ANY)] * len(after), out_specs=[hbm] * n,
        out_shape=[jax.ShapeDtypeStruct(a.shape, a.dtype) for a in lands],
        scratch_shapes=[pltpu.SemaphoreType.DMA((k,)), pltpu.SemaphoreType.DMA((k,))],
        input_output_aliases={} if fresh else {ns + i: i for i in range(n)},
        compiler_params=pltpu.CompilerParams(vmem_limit_bytes=VMEM_LIMIT))(*ins)


def _plan_start(plan, srcs, land_shapes, after, name):
    n = len(srcs)

    def body(*refs):
        src_refs, land_refs = refs[:n], refs[n:2 * n]
        send_sems, recv_sems = refs[2 * n + 1], refs[2 * n + 2]
        for cp in _plan_copies(plan, n, src_refs, land_refs, send_sems, recv_sems, False):
            cp.start()
        refs[-1][...] = jnp.zeros_like(refs[-1])

    hbm = pl.BlockSpec(memory_space=pltpu.HBM)
    sem = pl.BlockSpec(memory_space=pltpu.SEMAPHORE)
    k = len(plan) * n
    srcs = [pltpu.with_memory_space_constraint(a, pltpu.HBM) for a in srcs]
    lands = [pltpu.with_memory_space_constraint(lax.empty(shp, a.dtype), pltpu.HBM) for shp, a in zip(land_shapes, srcs)]
    res = pl.pallas_call(
        body, name=name,
        in_specs=[hbm] * (2 * n) + [pl.BlockSpec(memory_space=pl.ANY)],
        out_specs=[sem, sem] + [hbm] * (2 * n) + [pl.BlockSpec(memory_space=pltpu.VMEM)],
        out_shape=[pltpu.SemaphoreType.DMA((k,)), pltpu.SemaphoreType.DMA((k,))]
        + [pltpu.HBM(a.shape, a.dtype) for a in srcs] + [pltpu.HBM(shp, a.dtype) for shp, a in zip(land_shapes, srcs)]
        + [jax.ShapeDtypeStruct((8, LANES), F32)],
        input_output_aliases={i: 2 + i for i in range(2 * n)},
        compiler_params=pltpu.CompilerParams(has_side_effects=pltpu.SideEffectType.DATAFLOW_SIDE_EFFECTING),
    )(*srcs, *lands, after)
    return res[0], res[1], list(res[2:2 + n]), list(res[2 + n:2 + 2 * n]), res[-1]


def _plan_wait(plan, handle, after, name):
    send_sems, recv_sems, srcs, lands, _ = handle
    n = len(srcs)
    after = list(after)

    def body(*refs):
        src_refs, land_refs = refs[:n], refs[n:2 * n]
        ssem, rsem = refs[2 * n], refs[2 * n + 1]
        for cp in _plan_copies(plan, n, src_refs, land_refs, ssem, rsem, False):
            cp.wait_send()
        for cp in _plan_copies(plan, n, src_refs, land_refs, ssem, rsem, True):
            cp.wait_recv()

    hbm = pl.BlockSpec(memory_space=pltpu.HBM)
    sem = pl.BlockSpec(memory_space=pltpu.SEMAPHORE)
    res = pl.pallas_call(
        body, name=name,
        in_specs=[hbm] * (2 * n) + [sem, sem] + [pl.BlockSpec(memory_space=pl.ANY)] * len(after),
        out_specs=[hbm] * (2 * n),
        out_shape=[pltpu.HBM(a.shape, a.dtype) for a in srcs] + [pltpu.HBM(a.shape, a.dtype) for a in lands],
        input_output_aliases={i: i for i in range(2 * n)},
        compiler_params=pltpu.CompilerParams(has_side_effects=pltpu.SideEffectType.DATAFLOW_SIDE_EFFECTING),
    )(*srcs, *lands, send_sems, recv_sems, *after)
    return list(res[n:])


def _pair_sum(send, got, core, name):
    _, r, c = send.shape
    tr = 256 if r % 256 == 0 else r

    def body(core_ref, s_ref, g_ref, o_ref):
        o_ref[0] = (s_ref[0].astype(F32) + g_ref[0].astype(F32)).astype(o_ref.dtype)

    return pl.pallas_call(
        body, name=name,
        grid_spec=pltpu.PrefetchScalarGridSpec(
            num_scalar_prefetch=1, grid=(N_DEV // 2, r // tr),
            in_specs=[pl.BlockSpec((1, tr, c), lambda kc, i, core_ref: (2 * kc + core_ref[0], i, 0)),
                      pl.BlockSpec((1, tr, c), lambda kc, i, core_ref: (kc, i, 0))],
            out_specs=pl.BlockSpec((1, tr, c), lambda kc, i, core_ref: (kc, i, 0))),
        out_shape=jax.ShapeDtypeStruct((N_DEV // 2, r, c), send.dtype),
        compiler_params=_cp(("parallel", "parallel")))(core, send, got)


def _ada_mod(c_all, ada_w, ada_b_cols, name):
    nl, d, wc = ada_w.shape

    def body(c_ref, w_ref, b_ref, ca_ref, mod_ref):
        cv = c_ref[...]
        ca = cv * _sigmoid(cv)
        ca_ref[...] = ca
        mod_ref[0] = _dotf(ca, w_ref[0]) + b_ref[0]

    return pl.pallas_call(
        body, grid=(nl,),
        in_specs=[pl.BlockSpec((N_DEV, d), lambda l: (0, 0)), pl.BlockSpec((1, d, wc), lambda l: (l, 0, 0)),
                  pl.BlockSpec((1, 1, wc), lambda l: (l, 0, 0))],
        out_specs=[pl.BlockSpec((N_DEV, d), lambda l: (0, 0)), pl.BlockSpec((1, N_DEV, wc), lambda l: (l, 0, 0))],
        out_shape=[jax.ShapeDtypeStruct((N_DEV, d), F32), jax.ShapeDtypeStruct((nl, N_DEV, wc), F32)],
        name=name, compiler_params=_cp(("arbitrary",)))(c_all, ada_w, ada_b_cols)


def _adam(w, g, m, v):
    m2 = ADAM_B1 * m + (1.0 - ADAM_B1) * g
    v2 = ADAM_B2 * v + (1.0 - ADAM_B2) * (g * g)
    m_hat = m2 / (1.0 - ADAM_B1 ** ADAM_STEP)
    v_hat = v2 / (1.0 - ADAM_B2 ** ADAM_STEP)
    delta = -ADAM_LR * (m_hat / (jnp.sqrt(v_hat) + ADAM_EPS) + ADAM_WD * w)
    return delta, m2, v2


def _ada_grad_adam(c_act, d_mod, w, m, v, name):
    nl, d, wc = w.shape
    tk = min(512, d)

    def body(c_ref, dm_ref, w_ref, m_ref, v_ref, g_ref, dl_ref, m2_ref, v2_ref):
        g = _dotf_tn(c_ref[...], dm_ref[0])
        delta, m2, v2 = _adam(w_ref[0], g, m_ref[0], v_ref[0])
        g_ref[0], dl_ref[0], m2_ref[0], v2_ref[0] = g, delta, m2, v2

    blk = pl.BlockSpec((1, tk, wc), lambda l, i: (l, i, 0))
    shp = jax.ShapeDtypeStruct(w.shape, F32)
    return pl.pallas_call(
        body, grid=(nl, d // tk),
        in_specs=[pl.BlockSpec((N_DEV, tk), lambda l, i: (0, i)), pl.BlockSpec((1, N_DEV, wc), lambda l, i: (l, 0, 0)),
                  blk, blk, blk],
        out_specs=[blk] * 4, out_shape=[shp] * 4, name=name,
        compiler_params=_cp(("parallel", "parallel")))(c_act, d_mod, w, m, v)


def _adam_big(recv, w, m, v, layer, prev, name, after=()):
    nl, r, c = w.shape
    tr = 256 if r % 256 == 0 else r
    nparts = recv.shape[0]

    def body(rc_ref, w_ref, m_ref, v_ref, *rest):
        g_ref, dl_ref, m2_ref, v2_ref = rest[-4:]
        g = rc_ref[0].astype(F32)
        for d in range(1, nparts):
            g = g + rc_ref[d].astype(F32)
        delta, m2, v2 = _adam(w_ref[0], g, m_ref[0], v_ref[0])
        g_ref[0], dl_ref[0], m2_ref[0], v2_ref[0] = g, delta, m2, v2

    blk = pl.BlockSpec((1, tr, c), lambda i: (layer, i, 0))
    shp = jax.ShapeDtypeStruct(w.shape, F32)
    prev = () if prev is None else tuple(prev)
    return pl.pallas_call(
        body, grid=(r // tr,),
        in_specs=[pl.BlockSpec((nparts, tr, c), lambda i: (0, i, 0)), blk, blk, blk]
        + [pl.BlockSpec(memory_space=pl.ANY)] * (len(prev) + len(after)),
        out_specs=[blk] * 4, out_shape=[shp] * 4, name=name,
        input_output_aliases={4 + j: j for j in range(len(prev))},
        compiler_params=_cp(("parallel",)))(recv, w, m, v, *prev, *after)


def _sum_devices(gathered, name):
    _, r, c = gathered.shape

    def body(g_ref, o_ref):
        acc = g_ref[0]
        for d in range(1, N_DEV):
            acc = acc + g_ref[d]
        o_ref[...] = acc

    spec = pl.BlockSpec(memory_space=pltpu.VMEM)
    return pl.pallas_call(body, in_specs=[spec], out_specs=spec, out_shape=jax.ShapeDtypeStruct((r, c), F32),
                          name=name, compiler_params=pltpu.CompilerParams(vmem_limit_bytes=VMEM_LIMIT))(gathered)


def _adam_small(ws, gs, ms, vs, name):
    n = len(ws)

    def body(*refs):
        for i in range(n):
            w_ref, g_ref, m_ref, v_ref = (refs[k * n + i] for k in range(4))
            dl_ref, m2_ref, v2_ref = (refs[(4 + k) * n + i] for k in range(3))
            dl_ref[...], m2_ref[...], v2_ref[...] = _adam(w_ref[...], g_ref[...], m_ref[...], v_ref[...])

    spec = pl.BlockSpec(memory_space=pltpu.VMEM)
    shapes = [jax.ShapeDtypeStruct(w.shape, F32) for w in ws]
    res = pl.pallas_call(body, in_specs=[spec] * (4 * n), out_specs=[spec] * (3 * n), out_shape=shapes * 3, name=name,
                         compiler_params=pltpu.CompilerParams(vmem_limit_bytes=VMEM_LIMIT))(*ws, *gs, *ms, *vs)
    return res[:n], res[n:2 * n], res[2 * n:]


def _pack(parts):
    flat = jnp.concatenate([p.reshape(-1).astype(F32) for p in parts])
    assert flat.shape[0] % LANES == 0, flat.shape
    return flat.reshape(-1, LANES)


def _unpack(packed, shapes):
    flat = packed.reshape(-1)
    out, off = [], 0
    for shp in shapes:
        size = 1
        for dim in shp:
            size *= dim
        out.append(flat[off:off + size].reshape(shp))
        off += size
    return out


def _gather_cols(g, per):
    g = jnp.moveaxis(g, 0, -2)
    return g.reshape(g.shape[:-2] + (N_DEV * per,))


def _scatter_cols(g, per):
    return jnp.moveaxis(g.reshape(g.shape[:-1] + (N_DEV, per)), -2, 0)


def _my_cols(full, me, per):
    return lax.dynamic_slice_in_dim(full, me * per, per, axis=full.ndim - 1)


def kernel(x, c, positions, ada_w, ada_b, norm_g, w_in, gla_wg_f, gla_bg_f, gla_wg_b, gla_bg_b, gla_norm_g, mla_q_norm_g, mla_kv_norm_g, mla_w_uq, mla_w_ukv, mla_out_g, conv_w, conv_out_g, w_out, final_g, loss_target, m_ada_w, m_ada_b, m_norm_g, m_w_in, m_gla_wg_f, m_gla_bg_f, m_gla_wg_b, m_gla_bg_b, m_gla_norm_g, m_mla_q_norm_g, m_mla_kv_norm_g, m_mla_w_uq, m_mla_w_ukv, m_mla_out_g, m_conv_w, m_conv_out_g, m_w_out, m_final_g, v_ada_w, v_ada_b, v_norm_g, v_w_in, v_gla_wg_f, v_gla_bg_f, v_gla_wg_b, v_gla_bg_b, v_gla_norm_g, v_mla_q_norm_g, v_mla_kv_norm_g, v_mla_w_uq, v_mla_w_ukv, v_mla_out_g, v_conv_w, v_conv_out_g, v_w_out, v_final_g):
    me = 4 * lax.axis_index("x") + 2 * lax.axis_index("y") + lax.axis_index("c")
    nl = ada_w.shape[0]
    s, d = x.shape[1], x.shape[2]
    ada_cols = ada_w.shape[2]
    wgc, cwc = gla_wg_f.shape[2], conv_w.shape[2]

    (g0,) = _exchange([_pack([c, gla_wg_f, gla_wg_b, conv_w])], "gather_small_in", False, pltpu.VMEM)
    g0 = g0.reshape(N_DEV, -1)
    o1, o2, o3 = d, d + gla_wg_f.size, d + 2 * gla_wg_f.size
    c_all = g0[:, :o1]
    wgf_full = _gather_cols(g0[:, o1:o2].reshape((N_DEV,) + gla_wg_f.shape), wgc)
    wgb_full = _gather_cols(g0[:, o2:o3].reshape((N_DEV,) + gla_wg_b.shape), wgc)
    convw_full = _gather_cols(g0[:, o3:].reshape((N_DEV,) + conv_w.shape), cwc)

    ada_b_cols = _my_cols(ada_b, me, ada_cols).reshape(nl, 1, ada_cols)
    c_act, mod_cols = _ada_mod(c_all, ada_w, ada_b_cols, "ada_mod")
    (g1,) = _exchange([_pack([mod_cols])], "gather_mod", False, pltpu.VMEM)
    mod_all = g1.reshape(N_DEV, nl, N_DEV, ada_cols)
    mod_mine = _gather_cols(lax.dynamic_index_in_dim(mod_all, me, axis=2, keepdims=False), ada_cols)

    inv_freq = ROPE_THETA ** (-jnp.arange(0, MROPE, 2, dtype=F32) / MROPE)
    ang = positions[0].astype(F32)[:, None] * inv_freq
    cos, sin = jnp.tile(jnp.cos(ang), (1, LANES * 2 // MROPE)), jnp.tile(jnp.sin(ang), (1, LANES * 2 // MROPE))

    big = [w_in, w_out, mla_w_uq, mla_w_ukv]
    big_names = ["w_in", "w_out", "mla_w_uq", "mla_w_ukv"]

    def local_blocks(l):
        return [w[l].astype(MXU) for w in big]

    def put_own(lands, own):
        return [lax.dynamic_update_index_in_dim(ld, o, me, 0) for ld, o in zip(lands, own)]

    def layer_weights(l, gw_in=None, gw_out=None, gw_uq=None, gw_ukv=None):
        small = dict(norm_g=norm_g[l], gla_wg_f=wgf_full[l], gla_bg_f=gla_bg_f[l], gla_wg_b=wgb_full[l],
                     gla_bg_b=gla_bg_b[l], gla_norm_g=gla_norm_g[l], mla_q_norm_g=mla_q_norm_g[l],
                     mla_kv_norm_g=mla_kv_norm_g[l], mla_out_g=mla_out_g[l], conv_w=convw_full[l],
                     conv_out_g=conv_out_g[l])
        return _prep_layer_weights(
            None if gw_in is None else _perm_gathered(gw_in, IN_SEGS, PW),
            None if gw_out is None else gw_out.reshape((-1,) + gw_out.shape[2:]),
            None if gw_uq is None else _perm_gathered(gw_uq, UQ_SEGS, MQW),
            None if gw_ukv is None else _gather_cols(gw_ukv, mla_w_ukv.shape[2]), small)

    def land_shapes(blocks, slots):
        return [jax.ShapeDtypeStruct((slots,) + b.shape, b.dtype) for b in blocks]

    def slots_of(blocks):
        return [(N_DEV,) + b.shape for b in blocks]

    def forwarded(lands, blocks, tag):
        return put_own(_exchange_hbm(AG_FORWARD, None, lands, f"gather_{tag}_forward"), blocks)

    first = local_blocks(0)
    w_in_start = _plan_start(AG_SPREAD, first[:1], slots_of(first[:1]), mod_mine, "gather_w_in_l0_start")
    adam_w_in = [a + w_in_start[-1][0, 0] for a in (w_in, m_w_in, v_w_in)]
    lands = _plan_wait(AG_SPREAD, w_in_start, adam_w_in, "gather_w_in_l0_wait")
    (gw_in,) = forwarded(lands, first[:1], "w_in_l0")
    rest = _plan_start(AG_SPREAD, first[1:], slots_of(first[1:]), gw_in, "gather_rest_l0_start")
    h = x[0]
    saved, layers, mods = [], [], []
    pending = {}
    for l in range(nl):
        shift, scale, gate = (mod_mine[l, i * d:(i + 1) * d].reshape(1, d) for i in range(3))
        nxt = local_blocks(l + 1) if l + 1 < nl else None

        def start_next(after, wt_late, l=l, nxt=nxt):
            if nxt is not None:
                pending[l + 1] = _plan_start(AG_SPREAD, nxt, slots_of(nxt), after, f"gather_weights_l{l + 1}_start")
                wt_late["q_norm_g"] = layers[l]["q_norm_g"] + pending[l + 1][-1][0, 0]
            return wt_late

        if l == 0:
            in_after = (rest[-1],)
            layers.append(layer_weights(0, gw_in))

            def late(proj):
                got = forwarded(_plan_wait(AG_SPREAD, rest, [proj], "gather_rest_l0_wait"), first[1:], "rest_l0")
                full = layer_weights(0, None, *got)
                return start_next(got[0], {k: full[k] for k in ("w_out", "w_uq", "w_ukv")})
        else:
            got = forwarded(_plan_wait(AG_SPREAD, pending.pop(l), [h], f"gather_weights_l{l}_wait"), blocks, f"weights_l{l}")
            layers.append(layer_weights(l, *got))
            in_after = ()

            def late(proj):
                return start_next(proj, {})
        mods.append((shift, scale, gate))
        h, sv = _layer_fwd(h, mods[l], layers[l], cos, sin, f"l{l}", late, in_after)
        saved.append(sv)
        blocks = nxt
    loss_part, d_h, d_final_g = _final_loss(h, final_g.reshape(1, d), loss_target[0], "final_loss")
    loss = lax.psum(loss_part[0, 0], ("x", "y", "c"))
    shift, scale, gate = mods[-1]
    mods[-1] = (shift, scale, gate + 0.0 * loss)

    def grad_sends(gr):
        return [_scatter_perm(gr["w_in"], IN_SEGS, w_in.shape[2]).astype(MXU),
                gr["w_out"].reshape((N_DEV,) + w_out.shape[1:]).astype(MXU),
                _scatter_perm(gr["w_uq"], UQ_SEGS, mla_w_uq.shape[2]).astype(MXU),
                _scatter_cols(gr["w_ukv"], mla_w_ukv.shape[2]).astype(MXU)]

    my_chip = me // 2
    my_core = (me % 2).astype(jnp.int32).reshape(1)

    def chip_sums(gr, tag):
        sends = grad_sends(gr)
        got = _exchange_hbm(RS_PAIR, sends, land_shapes([sd[0] for sd in sends], N_DEV // 2), f"scatter_grads_{tag}_pair")
        return [_pair_sum(sd, gt, my_core, f"pair_sum_{n}_{tag}") for sd, gt, n in zip(sends, got, big_names)]

    def with_own_chip(lands, sums):
        return [lax.dynamic_update_index_in_dim(ld, lax.dynamic_index_in_dim(sm, my_chip, axis=0, keepdims=False),
                                                my_chip, 0) for ld, sm in zip(lands, sums)]

    small_names = ["norm_g", "gla_wg_f", "gla_bg_f", "gla_wg_b", "gla_bg_b", "gla_norm_g", "mla_q_norm_g",
                   "mla_kv_norm_g", "mla_out_g", "conv_w", "conv_out_g"]
    d_mods, grads, recv = [None] * nl, [None] * nl, [None] * nl
    flight = {}
    small = {}

    def gather_small(d_x, d_mod0, gr0):
        d_mods[0], grads[0] = d_mod0, _natural_small(gr0)
        d_mod_mine = jnp.stack([jnp.concatenate(d_mods[l], axis=-1)[0] for l in range(nl)])
        parts = [d_mod_mine] + [jnp.stack([grads[l][n] for l in range(nl)]) for n in small_names] + [d_final_g]
        (g2,) = _exchange([_pack(parts)], "gather_small_grads", False, pltpu.VMEM)
        small["d_mod_all"] = g2.reshape(N_DEV, -1)[:, :d_mod_mine.size].reshape(N_DEV, nl, 3 * d)
        small["summed"] = dict(zip(["ada_b"] + small_names + ["final_g"],
                                   _unpack(_sum_devices(g2, "sum_small_grads"), [p.shape for p in parts])))
        return (g2,)

    pairs = {}
    for l in reversed(range(nl)):
        def ship(big_grads, l=l):
            if l + 1 in flight:
                pend, sm = flight.pop(l + 1)
                recv[l + 1] = with_own_chip(_plan_wait(RS_CHIPS, pend, list(big_grads.values()),
                                                       f"scatter_grads_l{l + 1}_wait"), sm)
            if l > 0:
                sends = grad_sends(big_grads)
                pairs[l] = (_plan_start(RS_PAIR, sends, [(N_DEV // 2,) + sd.shape[1:] for sd in sends],
                                        big_grads["w_in"], f"scatter_grads_l{l}_pair_start"), sends)
                return (pairs[l][0][-1],)
            sm = chip_sums(big_grads, f"l{l}")
            flight[l] = (_plan_start(RS_CHIPS, sm, [a.shape for a in sm], recv[l + 1][0] if l + 1 < nl else sm[0],
                                     f"scatter_grads_l{l}_start"), sm)
            return (flight[l][0][-1],)

        shift, scale, gate = mods[l]
        if l + 1 in flight:
            gate = gate + flight[l + 1][0][-1][0, 0]
        if l > 0:
            d_h, d_mods[l], gr = _layer_bwd(d_h, saved[l], (shift, scale, gate), layers[l], cos, sin, f"l{l}", ship)
            grads[l] = _natural_small(gr)
            swap, sends = pairs.pop(l)
            got = _plan_wait(RS_PAIR, swap, [d_h], f"scatter_grads_l{l}_pair_wait")
            sm = [_pair_sum(sd, gt, my_core, f"pair_sum_{n}_l{l}") for sd, gt, n in zip(sends, got, big_names)]
            flight[l] = (_plan_start(RS_CHIPS, sm, [a.shape for a in sm], sm[0], f"scatter_grads_l{l}_start"), sm)
        else:
            d_h, _, _ = _layer_bwd(d_h, saved[l], (shift, scale, gate), layers[l], cos, sin, f"l{l}", ship, gather_small)
    pending, sums = flight.pop(0)
    grad_x = d_h[None]
    summed = small["summed"]
    summed["gla_wg_f"] = _my_cols(summed["gla_wg_f"], me, wgc)
    summed["gla_wg_b"] = _my_cols(summed["gla_wg_b"], me, wgc)
    summed["conv_w"] = _my_cols(summed["conv_w"], me, cwc)

    d_mod_cols = jnp.moveaxis(_my_cols(small["d_mod_all"], me, ada_cols), 0, 1) + pending[-1][0, 0]
    out = {}
    out["ada_w"] = _ada_grad_adam(c_act, d_mod_cols, ada_w, m_ada_w, v_ada_w, "ada_grad_adam")

    given = dict(ada_b=(ada_b, m_ada_b, v_ada_b), norm_g=(norm_g, m_norm_g, v_norm_g),
                 gla_wg_f=(gla_wg_f, m_gla_wg_f, v_gla_wg_f), gla_bg_f=(gla_bg_f, m_gla_bg_f, v_gla_bg_f),
                 gla_wg_b=(gla_wg_b, m_gla_wg_b, v_gla_wg_b), gla_bg_b=(gla_bg_b, m_gla_bg_b, v_gla_bg_b),
                 gla_norm_g=(gla_norm_g, m_gla_norm_g, v_gla_norm_g),
                 mla_q_norm_g=(mla_q_norm_g, m_mla_q_norm_g, v_mla_q_norm_g),
                 mla_kv_norm_g=(mla_kv_norm_g, m_mla_kv_norm_g, v_mla_kv_norm_g),
                 mla_out_g=(mla_out_g, m_mla_out_g, v_mla_out_g), conv_w=(conv_w, m_conv_w, v_conv_w),
                 conv_out_g=(conv_out_g, m_conv_out_g, v_conv_out_g), final_g=(final_g, m_final_g, v_final_g))
    names = list(given)

    def two_d(a):
        return a.reshape(1, -1) if a.ndim == 1 else a

    g_nat = [summed[n].reshape(given[n][0].shape) for n in names]
    res = _adam_small([two_d(given[n][0]) for n in names], [two_d(g) for g in g_nat],
                      [two_d(given[n][1]) for n in names], [two_d(given[n][2]) for n in names], "adam_small")
    for i, n in enumerate(names):
        out[n] = (g_nat[i],) + tuple(r[i].reshape(given[n][0].shape) for r in res)

    state = dict(w_in=adam_w_in, w_out=(w_out, m_w_out, v_w_out), mla_w_uq=(mla_w_uq, m_mla_w_uq, v_mla_w_uq),
                 mla_w_ukv=(mla_w_ukv, m_mla_w_ukv, v_mla_w_ukv))
    done = [out["ada_w"][0], res[0][0]]
    for l in reversed(range(nl)):
        if l == 0:
            recv[0] = with_own_chip(_plan_wait(RS_CHIPS, pending, done, "scatter_grads_l0_wait"), sums)
        for i, n in enumerate(big_names):
            out[n] = _adam_big(recv[l][i], *state[n], l, out.get(n), f"adam_{n}_l{l}",
                               (pending[-1],))
        done = done + [out[n][0] for n in big_names]

    order = ["ada_w", "ada_b", "norm_g", "w_in", "gla_wg_f", "gla_bg_f", "gla_wg_b", "gla_bg_b", "gla_norm_g",
             "mla_q_norm_g", "mla_kv_norm_g", "mla_w_uq", "mla_w_ukv", "mla_out_g", "conv_w", "conv_out_g", "w_out",
             "final_g"]
    return (loss, grad_x, *[out[n][0] for n in order], *[out[n][1] for n in order], *[out[n][2] for n in order],
            *[out[n][3] for n in order])
```

```python
import functools

import jax
import jax.numpy as jnp
from jax import lax
from jax.experimental import pallas as pl
from jax.experimental.pallas import tpu as pltpu

F32 = jnp.float32
MXU = jnp.bfloat16
HI = lax.Precision.HIGHEST
N_DEV = 8
MESH = pl.DeviceIdType.MESH

D_MIX = 2048
GH, GDK, GDV = 6, 64, 128
GW = GH * GDV
GQK = GH * GDK
GRANK = 16
GTEMP = 16.0
CHUNK = 64
MH, MQL, MKVL, MNOPE, MROPE, MDV = 6, 384, 256, 128, 64, 128
MW = MH * MDV
MQW = MH * (MNOPE + MROPE)
MKVW = MH * (MNOPE + MDV)
CONV_CH = 512
ROPE_THETA = 10000.0
EPS = 1e-6
IN_DIM = 5856
OZ, OCB, OCC, OCX, OMKV, OGV, OGQ, OGK, OMQ, OT = 0, 2048, 2560, 3072, 3584, 3840, 4608, 4992, 5376, 5760
PW = 5888
LANES = 128
VMEM_LIMIT = 56 * 1024 * 1024

ADAM_LR, ADAM_B1, ADAM_B2, ADAM_EPS, ADAM_WD, ADAM_STEP = 0.001, 0.9, 0.999, 1e-08, 0.01, 10


def _cp(sem=None):
    return pltpu.CompilerParams(dimension_semantics=sem, vmem_limit_bytes=VMEM_LIMIT)


def _dot(a, b):
    return jnp.dot(a.astype(MXU), b.astype(MXU), preferred_element_type=F32)


def _dot_nt(a, b):
    return lax.dot_general(a.astype(MXU), b.astype(MXU), (((1,), (1,)), ((), ())), preferred_element_type=F32)


def _dot_tn(a, b):
    return lax.dot_general(a.astype(MXU), b.astype(MXU), (((0,), (0,)), ((), ())), preferred_element_type=F32)


def _dotf(a, b):
    return jnp.dot(a, b, precision=HI, preferred_element_type=F32)


def _dotf_nt(a, b):
    return lax.dot_general(a, b, (((1,), (1,)), ((), ())), precision=HI, preferred_element_type=F32)


def _dotf_tn(a, b):
    return lax.dot_general(a, b, (((0,), (0,)), ((), ())), precision=HI, preferred_element_type=F32)


def _split3(x):
    hi = x.astype(jnp.bfloat16)
    r1 = x - hi.astype(F32)
    mid = r1.astype(jnp.bfloat16)
    lo = (r1 - mid.astype(F32)).astype(jnp.bfloat16)
    return hi, mid, lo


def _cum_dot(cum, x, transpose=False):
    dn = (((0,), (0,)), ((), ())) if transpose else (((1,), (0,)), ((), ()))
    cb = cum.astype(jnp.bfloat16)
    parts = [lax.dot_general(cb, p, dn, preferred_element_type=F32) for p in _split3(x)]
    return parts[0] + parts[1] + parts[2]


def _rows(s):
    return min(256, s)


def _rms(x, g):
    r = lax.rsqrt(jnp.mean(x * x, axis=-1, keepdims=True) + EPS)
    return x * r * g


def _rms_bwd(dy, x, g):
    r = lax.rsqrt(jnp.mean(x * x, axis=-1, keepdims=True) + EPS)
    xh = x * r
    dxh = dy * g
    dg = jnp.sum(dy * xh, axis=0, keepdims=True)
    dx = r * (dxh - xh * jnp.mean(dxh * xh, axis=-1, keepdims=True))
    return dx, dg


def _sigmoid(z):
    return jax.nn.sigmoid(z)


def _matmul(a, b, *, dims, tm, tn, tk, out_dtypes, name, epilogue=None, extras=(), extra_kinds=(), after=()):
    if dims == "nn":
        (m, k), n, mul = a.shape, b.shape[1], _dot
    elif dims == "nt":
        (m, k), n, mul = a.shape, b.shape[0], _dot_nt
    else:
        (k, m), n, mul = a.shape, b.shape[1], _dot_tn
    tm, tn, tk = min(tm, m), min(tn, n), min(tk, k)
    assert m % tm == 0 and n % tn == 0 and k % tk == 0, (m, n, k, tm, tn, tk)
    if dims == "nn":
        a_spec = pl.BlockSpec((tm, tk), lambda i, j, kk: (i, kk))
        b_spec = pl.BlockSpec((tk, tn), lambda i, j, kk: (kk, j))
    elif dims == "nt":
        a_spec = pl.BlockSpec((tm, tk), lambda i, j, kk: (i, kk))
        b_spec = pl.BlockSpec((tn, tk), lambda i, j, kk: (j, kk))
    else:
        a_spec = pl.BlockSpec((tk, tm), lambda i, j, kk: (kk, i))
        b_spec = pl.BlockSpec((tk, tn), lambda i, j, kk: (kk, j))
    nk = k // tk
    n_extra = len(extras)
    n_out = len(out_dtypes)
    n_after = len(after)
    extra_specs = []
    for kind in extra_kinds:
        if kind == "mn":
            extra_specs.append(pl.BlockSpec((tm, tn), lambda i, j, kk: (i, j)))
        else:
            extra_specs.append(pl.BlockSpec((1, tn), lambda i, j, kk: (0, j)))

    def finish(res, ex, outs):
        vals = (res,) if epilogue is None else epilogue(res, *[e[...] for e in ex])
        for o, v in zip(outs, vals):
            o[...] = v.astype(o.dtype)

    def body(*refs):
        a_ref, b_ref = refs[0], refs[1]
        ex = refs[2:2 + n_extra]
        outs = refs[2 + n_extra + n_after:2 + n_extra + n_after + n_out]
        if nk == 1:
            finish(mul(a_ref[...], b_ref[...]), ex, outs)
            return
        acc = refs[-1]
        kk = pl.program_id(2)

        @pl.when(kk == 0)
        def _():
            acc[...] = jnp.zeros_like(acc)

        acc[...] += mul(a_ref[...], b_ref[...])

        @pl.when(kk == nk - 1)
        def _():
            finish(acc[...], ex, outs)

    out_spec = pl.BlockSpec((tm, tn), lambda i, j, kk: (i, j))
    res = pl.pallas_call(
        body, grid=(m // tm, n // tn, nk),
        in_specs=[a_spec, b_spec] + extra_specs + [pl.BlockSpec(memory_space=pl.ANY)] * n_after,
        out_specs=[out_spec] * n_out,
        out_shape=[jax.ShapeDtypeStruct((m, n), dt) for dt in out_dtypes],
        scratch_shapes=[] if nk == 1 else [pltpu.VMEM((tm, tn), F32)],
        name=name, compiler_params=_cp(("parallel", "parallel", "arbitrary")),
    )(a, b, *extras, *after)
    return res


def _norm_mod(x, g, scale, shift, name):
    s, d = x.shape
    tr = _rows(s)

    def body(x_ref, g_ref, sc_ref, sh_ref, h_ref):
        h = _rms(x_ref[...], g_ref[...]) * (1.0 + sc_ref[...]) + sh_ref[...]
        h_ref[...] = h.astype(h_ref.dtype)

    row = pl.BlockSpec((tr, d), lambda i: (i, 0))
    vec = pl.BlockSpec((1, d), lambda i: (0, 0))
    return pl.pallas_call(body, grid=(s // tr,), in_specs=[row, vec, vec, vec], out_specs=row,
                          out_shape=jax.ShapeDtypeStruct((s, d), MXU), name=name,
                          compiler_params=_cp(("parallel",)))(x, g, scale, shift)


def _norm_mod_bwd(d_h, x, d_out, g, scale, name):
    s, d = x.shape
    tr = _rows(s)

    def body(dh_ref, x_ref, do_ref, g_ref, sc_ref, dx_ref, dsh_ref, dsc_ref, dg_ref):
        i = pl.program_id(0)

        @pl.when(i == 0)
        def _():
            dsh_ref[...] = jnp.zeros_like(dsh_ref)
            dsc_ref[...] = jnp.zeros_like(dsc_ref)
            dg_ref[...] = jnp.zeros_like(dg_ref)

        dh = dh_ref[...]
        xv = x_ref[...]
        gv = g_ref[...]
        r = lax.rsqrt(jnp.mean(xv * xv, axis=-1, keepdims=True) + EPS)
        xh = xv * r
        dsh_ref[...] += jnp.sum(dh, axis=0, keepdims=True)
        dsc_ref[...] += jnp.sum(dh * (xh * gv), axis=0, keepdims=True)
        dhn = dh * (1.0 + sc_ref[...])
        dg_ref[...] += jnp.sum(dhn * xh, axis=0, keepdims=True)
        dxh = dhn * gv
        dx_ref[...] = do_ref[...] + r * (dxh - xh * jnp.mean(dxh * xh, axis=-1, keepdims=True))

    row = pl.BlockSpec((tr, d), lambda i: (i, 0))
    vec = pl.BlockSpec((1, d), lambda i: (0, 0))
    vshape = jax.ShapeDtypeStruct((1, d), F32)
    return pl.pallas_call(body, grid=(s // tr,), in_specs=[row, row, row, vec, vec],
                          out_specs=[row, vec, vec, vec],
                          out_shape=[jax.ShapeDtypeStruct((s, d), F32), vshape, vshape, vshape],
                          name=name, compiler_params=_cp(("arbitrary",)))(d_h, x, d_out, g, scale)


def _gate_bwd(d_out, u, gate, name):
    s, d = d_out.shape
    tr = _rows(s)

    def body(do_ref, u_ref, gt_ref, du_ref, dgt_ref):
        @pl.when(pl.program_id(0) == 0)
        def _():
            dgt_ref[...] = jnp.zeros_like(dgt_ref)

        do = do_ref[...]
        du_ref[...] = (do * gt_ref[...]).astype(du_ref.dtype)
        dgt_ref[...] += jnp.sum(do * u_ref[...], axis=0, keepdims=True)

    row = pl.BlockSpec((tr, d), lambda i: (i, 0))
    vec = pl.BlockSpec((1, d), lambda i: (0, 0))
    return pl.pallas_call(body, grid=(s // tr,), in_specs=[row, row, vec], out_specs=[row, vec],
                          out_shape=[jax.ShapeDtypeStruct((s, d), MXU), jax.ShapeDtypeStruct((1, d), F32)],
                          name=name, compiler_params=_cp(("arbitrary",)))(d_out, u, gate)


def _final_loss(x, g, target, name):
    s, d = x.shape
    tr = _rows(s)

    def body(x_ref, g_ref, t_ref, loss_ref, dx_ref, dg_ref):
        @pl.when(pl.program_id(0) == 0)
        def _():
            loss_ref[...] = jnp.zeros_like(loss_ref)
            dg_ref[...] = jnp.zeros_like(dg_ref)

        xv = x_ref[...]
        gv = g_ref[...]
        diff = _rms(xv, gv) - t_ref[...]
        part = 0.5 * jnp.sum(jnp.sum(diff * diff, axis=-1, keepdims=True) / d, axis=0, keepdims=True)
        loss_ref[...] += jnp.broadcast_to(part, loss_ref.shape)
        dx, dg = _rms_bwd(diff / d, xv, gv)
        dx_ref[...] = dx
        dg_ref[...] += dg

    row = pl.BlockSpec((tr, d), lambda i: (i, 0))
    vec = pl.BlockSpec((1, d), lambda i: (0, 0))
    lvec = pl.BlockSpec((1, LANES), lambda i: (0, 0))
    return pl.pallas_call(body, grid=(s // tr,), in_specs=[row, vec, row], out_specs=[lvec, row, vec],
                          out_shape=[jax.ShapeDtypeStruct((1, LANES), F32), jax.ShapeDtypeStruct((s, d), F32),
                                     jax.ShapeDtypeStruct((1, d), F32)],
                          name=name, compiler_params=_cp(("arbitrary",)))(x, g, target)


def _shift_rows(u, s, down):
    ri = lax.broadcasted_iota(jnp.int32, u.shape, 0)
    if down:
        return jnp.where(ri == 0, 0.0, pltpu.roll(u, 1, 0))
    return jnp.where(ri == s - 1, 0.0, pltpu.roll(u, s - 1, 0))


def _conv_fwd(proj, conv_w, name):
    s = proj.shape[0]
    nt = CONV_CH // LANES

    def body(cb_ref, cc_ref, cx_ref, w_ref, pre_ref):
        u = cc_ref[...] * cx_ref[...]
        conv = _shift_rows(u, s, True) * w_ref[0:1, :] + u * w_ref[1:2, :] + _shift_rows(u, s, False) * w_ref[2:3, :]
        pre_ref[...] = cb_ref[...] * conv

    def col(off):
        return pl.BlockSpec((s, LANES), lambda j: (0, off // LANES + j))

    return pl.pallas_call(body, grid=(nt,), in_specs=[col(OCB), col(OCC), col(OCX), pl.BlockSpec((3, LANES), lambda j: (0, j))],
                          out_specs=pl.BlockSpec((s, LANES), lambda j: (0, j)),
                          out_shape=jax.ShapeDtypeStruct((s, CONV_CH), F32), name=name,
                          compiler_params=_cp(("parallel",)))(proj, proj, proj, conv_w)


def _conv_bwd(proj, conv_w, d_pre, name):
    s = proj.shape[0]
    nt = CONV_CH // LANES

    def body(cb_ref, cc_ref, cx_ref, w_ref, dp_ref, dcb_ref, dcc_ref, dcx_ref, dw_ref):
        cc, cx = cc_ref[...], cx_ref[...]
        u = cc * cx
        up, dn = _shift_rows(u, s, True), _shift_rows(u, s, False)
        w0, w1, w2 = w_ref[0:1, :], w_ref[1:2, :], w_ref[2:3, :]
        conv = up * w0 + u * w1 + dn * w2
        dp = dp_ref[...]
        dcb_ref[...] = dp * conv
        dconv = dp * cb_ref[...]
        du = _shift_rows(dconv, s, False) * w0 + dconv * w1 + _shift_rows(dconv, s, True) * w2
        dcc_ref[...] = du * cx
        dcx_ref[...] = du * cc
        dw_ref[0:1, :] = jnp.sum(dconv * up, axis=0, keepdims=True)
        dw_ref[1:2, :] = jnp.sum(dconv * u, axis=0, keepdims=True)
        dw_ref[2:3, :] = jnp.sum(dconv * dn, axis=0, keepdims=True)

    def col(off):
        return pl.BlockSpec((s, LANES), lambda j: (0, off // LANES + j))

    blk = pl.BlockSpec((s, LANES), lambda j: (0, j))
    wblk = pl.BlockSpec((3, LANES), lambda j: (0, j))
    full = jax.ShapeDtypeStruct((s, CONV_CH), F32)
    return pl.pallas_call(body, grid=(nt,), in_specs=[col(OCB), col(OCC), col(OCX), wblk, blk],
                          out_specs=[blk, blk, blk, wblk],
                          out_shape=[full, full, full, jax.ShapeDtypeStruct((3, CONV_CH), F32)],
                          name=name, compiler_params=_cp(("parallel",)))(proj, proj, proj, conv_w, d_pre)


GLA_SUB = 8


def _gla_gates(t_ref, wg_ref, bg_ref):
    t = t_ref[...]
    a = _dot(t, wg_ref[...]) + bg_ref[...]
    la = (jnp.minimum(a, 0.0) - jnp.log(1.0 + jnp.exp(-jnp.abs(a)))) / GTEMP
    return t, a, la


def _gla_masks(reverse):
    ri = lax.broadcasted_iota(jnp.int32, (CHUNK, CHUNK), 0)
    ci = lax.broadcasted_iota(jnp.int32, (CHUNK, CHUNK), 1)
    if reverse:
        cum, mask = ci >= ri, ci > ri
    else:
        cum, mask = ci <= ri, ci <= ri
    return cum.astype(F32), mask


def _gla_specs(s, reverse):
    nsub = min(GLA_SUB, s // CHUNK)
    nsteps = s // (CHUNK * nsub)

    def row(n):
        return nsteps - 1 - n if reverse else n

    def chunk(pi):
        return nsub - 1 - pi if reverse else pi

    return nsub, nsteps, row, chunk


def _gla_fwd(proj, wg_pad, bg, reverse, name):
    s = proj.shape[0]
    nsub, nsteps, row, chunk = _gla_specs(s, reverse)
    rb = nsub * CHUNK

    def body(q_ref, k_ref, v_ref, t_ref, wg_ref, bg_ref, o_ref, st_ref, state):
        @pl.when(pl.program_id(0) == 0)
        def _():
            state[...] = jnp.zeros_like(state)

        _, _, la = _gla_gates(t_ref, wg_ref, bg_ref)
        cumf, mask = _gla_masks(reverse)
        lane = lax.broadcasted_iota(jnp.int32, (CHUNK, LANES), 1)
        for pi in range(nsub):
            rows = slice(chunk(pi) * CHUNK, (chunk(pi) + 1) * CHUNK)
            la_c = la[rows]
            b = _cum_dot(cumf, la_c)
            bl = jnp.sum(la_c, axis=0, keepdims=True)
            q = q_ref[rows, :] * (GDK ** -0.5)
            k = k_ref[rows, :]
            qd = q * jnp.exp(b)
            ki = k * jnp.exp(-b)
            kte = k * jnp.exp(bl - b)
            decay = jnp.exp(bl)
            for h in range(GH):
                p = h // 2
                sl = slice(p * LANES, (p + 1) * LANES)
                lm = (lane < GDK) if h % 2 == 0 else (lane >= GDK)
                qd_h = jnp.where(lm, qd[:, sl], 0.0)
                kte_h = jnp.where(lm, kte[:, sl], 0.0)
                v_h = v_ref[rows, h * GDV:(h + 1) * GDV]
                st = state[h]
                a_mat = jnp.where(mask, _dot_nt(qd_h, ki[:, sl]), 0.0)
                o_ref[rows, h * GDV:(h + 1) * GDV] = _dot(a_mat, v_h) + _dot_nt(qd_h, st)
                st_ref[pi, h] = st
                state[h] = st * decay[:, sl] + _dot_tn(v_h, kte_h)

    return pl.pallas_call(
        body, grid=(nsteps,),
        in_specs=[pl.BlockSpec((rb, GQK), lambda n: (row(n), OGQ // GQK)),
                  pl.BlockSpec((rb, GQK), lambda n: (row(n), OGK // GQK)),
                  pl.BlockSpec((rb, GW), lambda n: (row(n), OGV // GW)),
                  pl.BlockSpec((rb, LANES), lambda n: (row(n), OT // LANES)),
                  pl.BlockSpec((LANES, GQK), lambda n: (0, 0)),
                  pl.BlockSpec((1, GQK), lambda n: (0, 0))],
        out_specs=[pl.BlockSpec((rb, GW), lambda n: (row(n), 0)),
                   pl.BlockSpec((nsub, GH, GDV, LANES), lambda n: (n, 0, 0, 0))],
        out_shape=[jax.ShapeDtypeStruct((s, GW), F32), jax.ShapeDtypeStruct((s // CHUNK, GH, GDV, LANES), F32)],
        scratch_shapes=[pltpu.VMEM((GH, GDV, LANES), F32)],
        name=name, compiler_params=_cp(("arbitrary",)))(proj, proj, proj, proj, wg_pad, bg)


def _gla_bwd(proj, wg_pad, bg, states, d_o, reverse, name):
    s = proj.shape[0]
    nsub, nsteps, row, chunk = _gla_specs(s, reverse)
    rb = nsub * CHUNK

    def body(q_ref, k_ref, v_ref, t_ref, wg_ref, bg_ref, st_ref, do_ref,
             dq_ref, dk_ref, dv_ref, dt_ref, dwg_ref, dbg_ref, dstate, da_buf):
        @pl.when(pl.program_id(0) == 0)
        def _():
            dstate[...] = jnp.zeros_like(dstate)
            dwg_ref[...] = jnp.zeros_like(dwg_ref)
            dbg_ref[...] = jnp.zeros_like(dbg_ref)

        t, a, la = _gla_gates(t_ref, wg_ref, bg_ref)
        cumf, mask = _gla_masks(reverse)
        lane = lax.broadcasted_iota(jnp.int32, (CHUNK, LANES), 1)
        for pi in reversed(range(nsub)):
            rows = slice(chunk(pi) * CHUNK, (chunk(pi) + 1) * CHUNK)
            la_c = la[rows]
            b = _cum_dot(cumf, la_c)
            bl = jnp.sum(la_c, axis=0, keepdims=True)
            q = q_ref[rows, :] * (GDK ** -0.5)
            k = k_ref[rows, :]
            e, ei, ee = jnp.exp(b), jnp.exp(-b), jnp.exp(bl - b)
            qd, ki, kte = q * e, k * ei, k * ee
            decay = jnp.exp(bl)
            for p in range(GH // 2):
                sl = slice(p * LANES, (p + 1) * LANES)
                dqd = jnp.zeros((CHUNK, LANES), F32)
                dki = jnp.zeros((CHUNK, LANES), F32)
                dkte = jnp.zeros((CHUNK, LANES), F32)
                ddecay = jnp.zeros((1, LANES), F32)
                for half in range(2):
                    h = 2 * p + half
                    lm = (lane < GDK) if half == 0 else (lane >= GDK)
                    qd_h = jnp.where(lm, qd[:, sl], 0.0)
                    ki_h = jnp.where(lm, ki[:, sl], 0.0)
                    kte_h = jnp.where(lm, kte[:, sl], 0.0)
                    v_h = v_ref[rows, h * GDV:(h + 1) * GDV]
                    do_h = do_ref[rows, h * GDV:(h + 1) * GDV]
                    st = st_ref[pi, h]
                    dst = dstate[h]
                    a_mat = jnp.where(mask, _dot_nt(qd_h, ki_h), 0.0)
                    da_mat = jnp.where(mask, _dot_nt(do_h, v_h), 0.0)
                    dv_ref[rows, h * GDV:(h + 1) * GDV] = _dot_tn(a_mat, do_h) + _dot_nt(kte_h, dst)
                    dqd += _dot(da_mat, ki_h) + _dot(do_h, st)
                    dki += _dot_tn(da_mat, qd_h)
                    dkte += _dot(v_h, dst)
                    ddecay += jnp.sum(dst * st, axis=0, keepdims=True)
                    dstate[h] = dst * decay[:, sl] + _dot_tn(do_h, qd_h)
                dq_ref[rows, sl] = dqd * e[:, sl] * (GDK ** -0.5)
                dk_ref[rows, sl] = dki * ei[:, sl] + dkte * ee[:, sl]
                db = dqd * qd[:, sl] - dki * ki[:, sl] - dkte * kte[:, sl]
                dbl = jnp.sum(dkte * kte[:, sl], axis=0, keepdims=True) + decay[:, sl] * ddecay
                da_buf[rows, sl] = _cum_dot(cumf, db, True) + dbl
        da = da_buf[...] * (1.0 / GTEMP) * _sigmoid(-a)
        dt_ref[...] = _dot_nt(da, wg_ref[...])
        dwg_ref[...] += _dot_tn(t, da)
        dbg_ref[...] += jnp.sum(da, axis=0, keepdims=True)

    def prow(j):
        return row(nsteps - 1 - j)

    return pl.pallas_call(
        body, grid=(nsteps,),
        in_specs=[pl.BlockSpec((rb, GQK), lambda j: (prow(j), OGQ // GQK)),
                  pl.BlockSpec((rb, GQK), lambda j: (prow(j), OGK // GQK)),
                  pl.BlockSpec((rb, GW), lambda j: (prow(j), OGV // GW)),
                  pl.BlockSpec((rb, LANES), lambda j: (prow(j), OT // LANES)),
                  pl.BlockSpec((LANES, GQK), lambda j: (0, 0)),
                  pl.BlockSpec((1, GQK), lambda j: (0, 0)),
                  pl.BlockSpec((nsub, GH, GDV, LANES), lambda j: (nsteps - 1 - j, 0, 0, 0)),
                  pl.BlockSpec((rb, GW), lambda j: (prow(j), 0))],
        out_specs=[pl.BlockSpec((rb, GQK), lambda j: (prow(j), 0)),
                   pl.BlockSpec((rb, GQK), lambda j: (prow(j), 0)),
                   pl.BlockSpec((rb, GW), lambda j: (prow(j), 0)),
                   pl.BlockSpec((rb, LANES), lambda j: (prow(j), 0)),
                   pl.BlockSpec((LANES, GQK), lambda j: (0, 0)),
                   pl.BlockSpec((1, GQK), lambda j: (0, 0))],
        out_shape=[jax.ShapeDtypeStruct((s, GQK), F32), jax.ShapeDtypeStruct((s, GQK), F32),
                   jax.ShapeDtypeStruct((s, GW), F32), jax.ShapeDtypeStruct((s, LANES), F32),
                   jax.ShapeDtypeStruct((LANES, GQK), F32), jax.ShapeDtypeStruct((1, GQK), F32)],
        scratch_shapes=[pltpu.VMEM((GH, GDV, LANES), F32), pltpu.VMEM((rb, GQK), F32)],
        name=name, compiler_params=_cp(("arbitrary",)))(proj, proj, proj, proj, wg_pad, bg, states, d_o)


def _rot_half(x):
    lane = lax.broadcasted_iota(jnp.int32, x.shape, 1)
    first = (lane % MROPE) < (MROPE // 2)
    return jnp.where(first, -pltpu.roll(x, LANES - MROPE // 2, 1), pltpu.roll(x, MROPE // 2, 1))


def _mla_prep(proj, cos, sin, qg, kvg, w_uq, w_ukv, name):
    s = proj.shape[0]
    tr = _rows(s)

    def body(mq_ref, mkv_ref, t_ref, cos_ref, sin_ref, qg_ref, kvg_ref, wuq_ref, wukv_ref, q_ref, k_ref, v_ref):
        cosv, sinv = cos_ref[...], sin_ref[...]
        lane = lax.broadcasted_iota(jnp.int32, (tr, LANES), 1)

        def rope(xv):
            return xv * cosv + _rot_half(xv) * sinv

        qm = _dot(_rms(mq_ref[...], qg_ref[...]), wuq_ref[...])
        kv = _dot(_rms(mkv_ref[...], kvg_ref[...]), wukv_ref[...])
        kr_lo = jnp.where(lane < MROPE, rope(t_ref[...]), 0.0)
        kr_hi = pltpu.roll(kr_lo, MROPE, 1)
        for p in range(MH // 2):
            r = rope(qm[:, MW + p * LANES:MW + (p + 1) * LANES]).astype(q_ref.dtype)
            q_ref[2 * p, :, LANES:] = r
            q_ref[2 * p + 1, :, LANES:] = r
        for h in range(MH):
            q_ref[h, :, :LANES] = qm[:, h * LANES:(h + 1) * LANES].astype(q_ref.dtype)
            k_ref[h, :, :LANES] = kv[:, 2 * h * LANES:(2 * h + 1) * LANES].astype(k_ref.dtype)
            k_ref[h, :, LANES:] = (kr_lo if h % 2 == 0 else kr_hi).astype(k_ref.dtype)
            v_ref[h] = kv[:, (2 * h + 1) * LANES:(2 * h + 2) * LANES].astype(v_ref.dtype)

    def full(shape):
        return pl.BlockSpec(shape, lambda i: (0,) * len(shape))

    return pl.pallas_call(
        body, grid=(s // tr,),
        in_specs=[pl.BlockSpec((tr, MQL), lambda i: (i, OMQ // MQL)),
                  pl.BlockSpec((tr, MKVL), lambda i: (i, OMKV // MKVL)),
                  pl.BlockSpec((tr, LANES), lambda i: (i, OT // LANES)),
                  pl.BlockSpec((tr, LANES), lambda i: (i, 0)),
                  pl.BlockSpec((tr, LANES), lambda i: (i, 0)),
                  full((1, MQL)), full((1, MKVL)), full((MQL, MQW)), full((MKVL, MKVW))],
        out_specs=[pl.BlockSpec((MH, tr, 2 * LANES), lambda i: (0, i, 0)),
                   pl.BlockSpec((MH, tr, 2 * LANES), lambda i: (0, i, 0)),
                   pl.BlockSpec((MH, tr, LANES), lambda i: (0, i, 0))],
        out_shape=[jax.ShapeDtypeStruct((MH, s, 2 * LANES), MXU), jax.ShapeDtypeStruct((MH, s, 2 * LANES), MXU),
                   jax.ShapeDtypeStruct((MH, s, LANES), MXU)],
        name=name, compiler_params=_cp(("parallel",)))(proj, proj, proj, cos, sin, qg, kvg, w_uq, w_ukv)


def _mla_prep_bwd(proj, cos, sin, qg, kvg, w_uq, w_ukv, d_q, d_k, d_v, name):
    s = proj.shape[0]
    tr = _rows(s)

    def body(mq_ref, mkv_ref, cos_ref, sin_ref, qg_ref, kvg_ref, wuq_ref, wukv_ref, dq_ref, dk_ref, dv_ref,
             dmq_ref, dmkv_ref, dt_ref, dwuq_ref, dwukv_ref, dqg_ref, dkvg_ref):
        @pl.when(pl.program_id(0) == 0)
        def _():
            for r in (dwuq_ref, dwukv_ref, dqg_ref, dkvg_ref):
                r[...] = jnp.zeros_like(r)

        cosv, sinv = cos_ref[...], sin_ref[...]
        lane = lax.broadcasted_iota(jnp.int32, (tr, LANES), 1)
        lo = lane < MROPE

        def unrope(dv):
            return dv * cosv - _rot_half(dv * sinv)

        parts = [dq_ref[h, :, :LANES] for h in range(MH)]
        for p in range(MH // 2):
            parts.append(unrope(jnp.where(lo, dq_ref[2 * p, :, LANES:], dq_ref[2 * p + 1, :, LANES:])))
        d_qm = jnp.concatenate(parts, axis=1)
        mq, qgv = mq_ref[...], qg_ref[...]
        cq = _rms(mq, qgv)
        dwuq_ref[...] += _dot_tn(cq, d_qm)
        dmq, dqg = _rms_bwd(_dot_nt(d_qm, wuq_ref[...]), mq, qgv)
        dmq_ref[...] = dmq
        dqg_ref[...] += dqg

        parts = []
        for h in range(MH):
            parts += [dk_ref[h, :, :LANES], dv_ref[h]]
        d_kv = jnp.concatenate(parts, axis=1)
        mkv, kvgv = mkv_ref[...], kvg_ref[...]
        ckv = _rms(mkv, kvgv)
        dwukv_ref[...] += _dot_tn(ckv, d_kv)
        dmkv, dkvg = _rms_bwd(_dot_nt(d_kv, wukv_ref[...]), mkv, kvgv)
        dmkv_ref[...] = dmkv
        dkvg_ref[...] += dkvg

        even = dk_ref[0, :, LANES:] + dk_ref[2, :, LANES:] + dk_ref[4, :, LANES:]
        odd = dk_ref[1, :, LANES:] + dk_ref[3, :, LANES:] + dk_ref[5, :, LANES:]
        d_kr = jnp.where(lo, even, 0.0) + pltpu.roll(jnp.where(lo, 0.0, odd), MROPE, 1)
        dt_ref[...] = jnp.where(lo, unrope(d_kr), 0.0)

    def full(shape):
        return pl.BlockSpec(shape, lambda i: (0,) * len(shape))

    return pl.pallas_call(
        body, grid=(s // tr,),
        in_specs=[pl.BlockSpec((tr, MQL), lambda i: (i, OMQ // MQL)),
                  pl.BlockSpec((tr, MKVL), lambda i: (i, OMKV // MKVL)),
                  pl.BlockSpec((tr, LANES), lambda i: (i, 0)),
                  pl.BlockSpec((tr, LANES), lambda i: (i, 0)),
                  full((1, MQL)), full((1, MKVL)), full((MQL, MQW)), full((MKVL, MKVW)),
                  pl.BlockSpec((MH, tr, 2 * LANES), lambda i: (0, i, 0)),
                  pl.BlockSpec((MH, tr, 2 * LANES), lambda i: (0, i, 0)),
                  pl.BlockSpec((MH, tr, LANES), lambda i: (0, i, 0))],
        out_specs=[pl.BlockSpec((tr, MQL), lambda i: (i, 0)), pl.BlockSpec((tr, MKVL), lambda i: (i, 0)),
                   pl.BlockSpec((tr, LANES), lambda i: (i, 0)),
                   full((MQL, MQW)), full((MKVL, MKVW)), full((1, MQL)), full((1, MKVL))],
        out_shape=[jax.ShapeDtypeStruct((s, MQL), F32), jax.ShapeDtypeStruct((s, MKVL), F32),
                   jax.ShapeDtypeStruct((s, LANES), F32),
                   jax.ShapeDtypeStruct((MQL, MQW), F32), jax.ShapeDtypeStruct((MKVL, MKVW), F32),
                   jax.ShapeDtypeStruct((1, MQL), F32), jax.ShapeDtypeStruct((1, MKVL), F32)],
        name=name, compiler_params=_cp(("arbitrary",)))(proj, proj, cos, sin, qg, kvg, w_uq, w_ukv, d_q, d_k, d_v)


ATT_SCALE = (MNOPE + MROPE) ** -0.5
ATT_SCALE_LOG2 = ATT_SCALE * 1.4426950408889634
ATT_TQ_FWD, ATT_TQ = 256, 512


def _attn_fwd(q, k, v, name):
    s = q.shape[1]
    tq = min(ATT_TQ_FWD, s)

    def body(q_ref, k_ref, v_ref, o_ref, lse_ref):
        sc = _dot_nt(q_ref[0], k_ref[0])
        m = jnp.max(sc, axis=-1, keepdims=True)
        p = jnp.exp2((sc - m) * ATT_SCALE_LOG2)
        l = jnp.sum(p, axis=-1, keepdims=True)
        o_ref[...] = _dot(p, v_ref[0]) / l
        lse_ref[0] = m * ATT_SCALE_LOG2 + jnp.log2(l)

    return pl.pallas_call(
        body, grid=(MH, s // tq),
        in_specs=[pl.BlockSpec((1, tq, 2 * LANES), lambda h, i: (h, i, 0)),
                  pl.BlockSpec((1, s, 2 * LANES), lambda h, i: (h, 0, 0)),
                  pl.BlockSpec((1, s, LANES), lambda h, i: (h, 0, 0))],
        out_specs=[pl.BlockSpec((tq, LANES), lambda h, i: (i, h)),
                   pl.BlockSpec((1, tq, 1), lambda h, i: (h, i, 0))],
        out_shape=[jax.ShapeDtypeStruct((s, MW), F32), jax.ShapeDtypeStruct((MH, s, 1), F32)],
        name=name, compiler_params=_cp(("parallel", "parallel")))(q, k, v)


def _attn_bwd(q, k, v, o, lse, d_o, name):
    s = q.shape[1]
    tq = min(ATT_TQ, s)

    def body(q_ref, k_ref, v_ref, o_ref, lse_ref, do_ref, dq_ref, dk_ref, dv_ref):
        @pl.when(pl.program_id(1) == 0)
        def _():
            dk_ref[...] = jnp.zeros_like(dk_ref)
            dv_ref[...] = jnp.zeros_like(dv_ref)

        qv, kv, do = q_ref[0], k_ref[0], do_ref[...]
        p = jnp.exp2(_dot_nt(qv, kv) * ATT_SCALE_LOG2 - lse_ref[0])
        delta = jnp.sum(do * o_ref[...], axis=-1, keepdims=True)
        ds = p * (_dot_nt(do, v_ref[0]) - delta)
        dq_ref[0] = _dot(ds, kv) * ATT_SCALE
        dk_ref[0] += _dot_tn(ds, qv) * ATT_SCALE
        dv_ref[0] += _dot_tn(p, do)

    return pl.pallas_call(
        body, grid=(MH, s // tq),
        in_specs=[pl.BlockSpec((1, tq, 2 * LANES), lambda h, i: (h, i, 0)),
                  pl.BlockSpec((1, s, 2 * LANES), lambda h, i: (h, 0, 0)),
                  pl.BlockSpec((1, s, LANES), lambda h, i: (h, 0, 0)),
                  pl.BlockSpec((tq, LANES), lambda h, i: (i, h)),
                  pl.BlockSpec((1, tq, 1), lambda h, i: (h, i, 0)),
                  pl.BlockSpec((tq, LANES), lambda h, i: (i, h))],
        out_specs=[pl.BlockSpec((1, tq, 2 * LANES), lambda h, i: (h, i, 0)),
                   pl.BlockSpec((1, s, 2 * LANES), lambda h, i: (h, 0, 0)),
                   pl.BlockSpec((1, s, LANES), lambda h, i: (h, 0, 0))],
        out_shape=[jax.ShapeDtypeStruct((MH, s, 2 * LANES), F32), jax.ShapeDtypeStruct((MH, s, 2 * LANES), F32),
                   jax.ShapeDtypeStruct((MH, s, LANES), F32)],
        name=name, compiler_params=_cp(("parallel", "arbitrary")))(q, k, v, o, lse, d_o)


def _merge_fwd(o_f, o_b, o_att, pre, proj, gng, mog, cog, name):
    s = proj.shape[0]
    tr = _rows(s)

    def body(of_ref, ob_ref, oa_ref, pre_ref, z_ref, gng_ref, mog_ref, cog_ref, y_ref):
        z = z_ref[...]
        sz = z * _sigmoid(z)
        osum = of_ref[...] + ob_ref[...]
        gg = gng_ref[...]
        for h in range(GH):
            sl = slice(h * GDV, (h + 1) * GDV)
            y_ref[:, sl] = (_rms(osum[:, sl], gg) * sz[:, sl]).astype(y_ref.dtype)
        y_ref[:, GW:GW + MW] = (_rms(oa_ref[...], mog_ref[...]) * sz[:, GW:GW + MW]).astype(y_ref.dtype)
        y_ref[:, GW + MW:] = (_rms(pre_ref[...], cog_ref[...]) * sz[:, GW + MW:]).astype(y_ref.dtype)

    def row(w):
        return pl.BlockSpec((tr, w), lambda i: (i, 0))

    def vec(w):
        return pl.BlockSpec((1, w), lambda i: (0, 0))

    return pl.pallas_call(
        body, grid=(s // tr,),
        in_specs=[row(GW), row(GW), row(MW), row(CONV_CH), row(D_MIX), vec(GDV), vec(MW), vec(CONV_CH)],
        out_specs=row(D_MIX), out_shape=jax.ShapeDtypeStruct((s, D_MIX), MXU),
        name=name, compiler_params=_cp(("parallel",)))(o_f, o_b, o_att, pre, proj, gng, mog, cog)


def _merge_bwd(d_y, o_f, o_b, o_att, pre, proj, gng, mog, cog, name):
    s = proj.shape[0]
    tr = _rows(s)

    def body(dy_ref, of_ref, ob_ref, oa_ref, pre_ref, z_ref, gng_ref, mog_ref, cog_ref,
             dz_ref, dos_ref, doa_ref, dpre_ref, dgng_ref, dmog_ref, dcog_ref):
        @pl.when(pl.program_id(0) == 0)
        def _():
            for r in (dgng_ref, dmog_ref, dcog_ref):
                r[...] = jnp.zeros_like(r)

        z, dy = z_ref[...], dy_ref[...]
        sg = _sigmoid(z)
        sz = z * sg
        dsz = sg * (1.0 + z * (1.0 - sg))
        dcat = dy * sz
        dyz = dy * dsz
        osum = of_ref[...] + ob_ref[...]
        gg = gng_ref[...]
        dgg = jnp.zeros_like(gg)
        for h in range(GH):
            sl = slice(h * GDV, (h + 1) * GDV)
            dz_ref[:, sl] = dyz[:, sl] * _rms(osum[:, sl], gg)
            dx, dg = _rms_bwd(dcat[:, sl], osum[:, sl], gg)
            dos_ref[:, sl] = dx
            dgg += dg
        dgng_ref[...] += dgg
        sl = slice(GW, GW + MW)
        oa, mg = oa_ref[...], mog_ref[...]
        dz_ref[:, sl] = dyz[:, sl] * _rms(oa, mg)
        dx, dg = _rms_bwd(dcat[:, sl], oa, mg)
        doa_ref[...] = dx
        dmog_ref[...] += dg
        sl = slice(GW + MW, D_MIX)
        pv, cg = pre_ref[...], cog_ref[...]
        dz_ref[:, sl] = dyz[:, sl] * _rms(pv, cg)
        dx, dg = _rms_bwd(dcat[:, sl], pv, cg)
        dpre_ref[...] = dx
        dcog_ref[...] += dg

    def row(w):
        return pl.BlockSpec((tr, w), lambda i: (i, 0))

    def vec(w):
        return pl.BlockSpec((1, w), lambda i: (0, 0))

    def rs(w):
        return jax.ShapeDtypeStruct((s, w), F32)

    def vs(w):
        return jax.ShapeDtypeStruct((1, w), F32)

    return pl.pallas_call(
        body, grid=(s // tr,),
        in_specs=[row(D_MIX), row(GW), row(GW), row(MW), row(CONV_CH), row(D_MIX), vec(GDV), vec(MW), vec(CONV_CH)],
        out_specs=[row(D_MIX), row(GW), row(MW), row(CONV_CH), vec(GDV), vec(MW), vec(CONV_CH)],
        out_shape=[rs(D_MIX), rs(GW), rs(MW), rs(CONV_CH), vs(GDV), vs(MW), vs(CONV_CH)],
        name=name, compiler_params=_cp(("arbitrary",)))(d_y, o_f, o_b, o_att, pre, proj, gng, mog, cog)


def _assemble_dproj(d_z, d_cb, d_cc, d_cx, d_mkv, dv_f, dv_b, dq_f, dq_b, dk_f, dk_b, d_mq, dt_m, dt_f, dt_b, name):
    s = d_z.shape[0]
    tr = _rows(s)

    def body(dz, dcb, dcc, dcx, dmkv, dvf, dvb, dqf, dqb, dkf, dkb, dmq, dtm, dtf, dtb, out):
        dt = out.dtype
        out[:, OZ:OZ + D_MIX] = dz[...].astype(dt)
        out[:, OCB:OCB + CONV_CH] = dcb[...].astype(dt)
        out[:, OCC:OCC + CONV_CH] = dcc[...].astype(dt)
        out[:, OCX:OCX + CONV_CH] = dcx[...].astype(dt)
        out[:, OMKV:OMKV + MKVL] = dmkv[...].astype(dt)
        out[:, OGV:OGV + GW] = (dvf[...] + dvb[...]).astype(dt)
        out[:, OGQ:OGQ + GQK] = (dqf[...] + dqb[...]).astype(dt)
        out[:, OGK:OGK + GQK] = (dkf[...] + dkb[...]).astype(dt)
        out[:, OMQ:OMQ + MQL] = dmq[...].astype(dt)
        out[:, OT:OT + LANES] = (dtm[...] + dtf[...] + dtb[...]).astype(dt)

    args = (d_z, d_cb, d_cc, d_cx, d_mkv, dv_f, dv_b, dq_f, dq_b, dk_f, dk_b, d_mq, dt_m, dt_f, dt_b)
    return pl.pallas_call(
        body, grid=(s // tr,),
        in_specs=[pl.BlockSpec((tr, a.shape[1]), lambda i: (i, 0)) for a in args],
        out_specs=pl.BlockSpec((tr, PW), lambda i: (i, 0)),
        out_shape=jax.ShapeDtypeStruct((s, PW), MXU), name=name, compiler_params=_cp(("parallel",)))(*args)


def _layer_fwd(x, mod, wt, cos, sin, tag, late=None, in_after=()):
    shift, scale, gate = mod
    h = _norm_mod(x, wt["norm_g"], scale, shift, f"norm_mod_{tag}")
    (proj,) = _matmul(h, wt["w_in"], dims="nn", tm=2048, tn=256, tk=2048, out_dtypes=(F32,), name=f"in_proj_{tag}",
                      after=in_after)
    if late is not None:
        wt.update(late(proj))
    o_f, st_f = _gla_fwd(proj, wt["wg_pad_f"], wt["bg_f"], False, f"gla_fwd_f_{tag}")
    o_b, st_b = _gla_fwd(proj, wt["wg_pad_b"], wt["bg_b"], True, f"gla_fwd_b_{tag}")
    q, k, v = _mla_prep(proj, cos, sin, wt["q_norm_g"], wt["kv_norm_g"], wt["w_uq"], wt["w_ukv"], f"mla_prep_{tag}")
    o_att, lse = _attn_fwd(q, k, v, f"attn_fwd_{tag}")
    pre = _conv_fwd(proj, wt["conv_w"], f"conv_fwd_{tag}")
    y = _merge_fwd(o_f, o_b, o_att, pre, proj, wt["gla_norm_g"], wt["mla_out_g"], wt["conv_out_g"], f"merge_fwd_{tag}")
    x_new, u = _matmul(y, wt["w_out"], dims="nn", tm=2048, tn=256, tk=2048, out_dtypes=(F32, F32),
                       name=f"out_proj_{tag}", epilogue=lambda acc, xv, gv: (xv + gv * acc, acc),
                       extras=(x, gate), extra_kinds=("mn", "n"))
    saved = dict(x=x, h=h, proj=proj, o_f=o_f, o_b=o_b, st_f=st_f, st_b=st_b, q=q, k=k, v=v,
                 o_att=o_att, lse=lse, pre=pre, y=y, u=u)
    return x_new, saved


def _layer_bwd(d_out, sv, mod, wt, cos, sin, tag, ship=None, dx_first=None):
    shift, scale, gate = mod
    proj = sv["proj"]
    d_u, d_gate = _gate_bwd(d_out, sv["u"], gate, f"gate_bwd_{tag}")
    (g_w_out,) = _matmul(sv["y"], d_u, dims="tn", tm=1024, tn=512, tk=2048, out_dtypes=(MXU,), name=f"out_proj_dw_{tag}")
    (d_y,) = _matmul(d_u, wt["w_out"], dims="nt", tm=2048, tn=256, tk=2048, out_dtypes=(F32,), name=f"out_proj_dx_{tag}",
                     after=(g_w_out,))
    d_z, d_osum, d_oatt, d_pre, d_gng, d_mog, d_cog = _merge_bwd(
        d_y, sv["o_f"], sv["o_b"], sv["o_att"], sv["pre"], proj, wt["gla_norm_g"], wt["mla_out_g"], wt["conv_out_g"],
        f"merge_bwd_{tag}")
    d_cb, d_cc, d_cx, d_conv_w = _conv_bwd(proj, wt["conv_w"], d_pre, f"conv_bwd_{tag}")
    d_q, d_k, d_v = _attn_bwd(sv["q"], sv["k"], sv["v"], sv["o_att"], sv["lse"], d_oatt, f"attn_bwd_{tag}")
    d_mq, d_mkv, dt_m, g_w_uq, g_w_ukv, d_qg, d_kvg = _mla_prep_bwd(
        proj, cos, sin, wt["q_norm_g"], wt["kv_norm_g"], wt["w_uq"], wt["w_ukv"], d_q, d_k, d_v, f"mla_prep_bwd_{tag}")
    dq_f, dk_f, dv_f, dt_f, d_wg_f, d_bg_f = _gla_bwd(proj, wt["wg_pad_f"], wt["bg_f"], sv["st_f"], d_osum, False,
                                                     f"gla_bwd_f_{tag}")
    dq_b, dk_b, dv_b, dt_b, d_wg_b, d_bg_b = _gla_bwd(proj, wt["wg_pad_b"], wt["bg_b"], sv["st_b"], d_osum, True,
                                                     f"gla_bwd_b_{tag}")
    d_proj = _assemble_dproj(d_z, d_cb, d_cc, d_cx, d_mkv, dv_f, dv_b, dq_f, dq_b, dk_f, dk_b, d_mq, dt_m, dt_f, dt_b,
                             f"assemble_dproj_{tag}")
    grads = dict(w_out=g_w_out, w_uq=g_w_uq, w_ukv=g_w_ukv,
                 wg_pad_f=d_wg_f, bg_f=d_bg_f, wg_pad_b=d_wg_b, bg_b=d_bg_b, gla_norm_g=d_gng,
                 q_norm_g=d_qg, kv_norm_g=d_kvg, mla_out_g=d_mog, conv_w=d_conv_w, conv_out_g=d_cog)

    def in_dw(after):
        (g_w_in,) = _matmul(sv["h"], d_proj, dims="tn", tm=2048, tn=256, tk=2048, out_dtypes=(MXU,),
                            name=f"in_proj_dw_{tag}", after=after)
        grads["w_in"] = g_w_in
        return dict(w_in=g_w_in, w_out=g_w_out, w_uq=g_w_uq, w_ukv=g_w_ukv)

    def in_dx(after):
        (d_h,) = _matmul(d_proj, wt["w_in"], dims="nt", tm=1024, tn=512, tk=PW, out_dtypes=(F32,),
                         name=f"in_proj_dx_{tag}", after=after)
        d_x, d_shift, d_scale, d_ng = _norm_mod_bwd(d_h, sv["x"], d_out, wt["norm_g"], scale, f"norm_mod_bwd_{tag}")
        grads["norm_g"] = d_ng
        return d_x, (d_shift, d_scale, d_gate)

    if dx_first is None:
        big = in_dw(())
        d_x, d_mod = in_dx((big["w_in"],) if ship is None else ship(big))
    else:
        d_x, d_mod = in_dx(())
        big = in_dw(dx_first(d_x, d_mod, grads))
        ship(big)
    return d_x, d_mod, grads


def _perm_in_cols(w):
    pad = jnp.zeros(w.shape[:-1] + (PW - IN_DIM,), w.dtype)
    return jnp.concatenate([w[..., 3808:5856], w[..., 2272:3808], w[..., 1952:2208], w[..., 768:1536], w[..., 0:768],
                            w[..., 1568:1952], w[..., 2208:2272], w[..., 1536:1568], pad], axis=-1)


def _unperm_in_cols(g):
    return jnp.concatenate([g[..., OGQ:OGQ + 2 * GQK], g[..., OGV:OGV + GW], g[..., OT + MROPE:OT + MROPE + 2 * GRANK],
                            g[..., OMQ:OMQ + MQL], g[..., OMKV:OMKV + MKVL], g[..., OT:OT + MROPE],
                            g[..., OCB:OCB + 3 * CONV_CH], g[..., OZ:OZ + D_MIX]], axis=-1)


IN_SEGS = ((3808, 5856), (2272, 3808), (1952, 2208), (768, 1536), (0, 768), (1568, 1952), (2208, 2272), (1536, 1568))
UQ_SEGS = (tuple((h * (MNOPE + MROPE), h * (MNOPE + MROPE) + MNOPE) for h in range(MH))
           + tuple((h * (MNOPE + MROPE) + MNOPE, (h + 1) * (MNOPE + MROPE)) for h in range(MH)))


def _perm_gathered(g, segs, width):
    per = g.shape[-1]
    parts, total = [], 0
    for a, b in segs:
        c = a
        while c < b:
            j = c // per
            hi = min(b, (j + 1) * per)
            parts.append(g[j, :, c - j * per:hi - j * per])
            c = hi
        total += b - a
    if width > total:
        parts.append(jnp.zeros((g.shape[1], width - total), g.dtype))
    return jnp.concatenate(parts, axis=1)


def _scatter_perm(gp, segs, per):
    offs, o = [], 0
    for a, b in segs:
        offs.append((a, b, o))
        o += b - a
    blocks = []
    for j in range(N_DEV):
        lo, hi = j * per, (j + 1) * per
        pieces = []
        for a, b, o in sorted(offs):
            s0, s1 = max(a, lo), min(b, hi)
            if s0 < s1:
                pieces.append(gp[:, o + s0 - a:o + s1 - a])
        blocks.append(jnp.concatenate(pieces, axis=1))
    return jnp.stack(blocks)


def _perm_uq_cols(w):
    w3 = w.reshape(w.shape[:-1] + (MH, MNOPE + MROPE))
    return jnp.concatenate([w3[..., :MNOPE].reshape(w.shape[:-1] + (MH * MNOPE,)),
                            w3[..., MNOPE:].reshape(w.shape[:-1] + (MH * MROPE,))], axis=-1)


def _unperm_uq_cols(g):
    nope = g[..., :MH * MNOPE].reshape(g.shape[:-1] + (MH, MNOPE))
    rope = g[..., MH * MNOPE:].reshape(g.shape[:-1] + (MH, MROPE))
    return jnp.concatenate([nope, rope], axis=-1).reshape(g.shape[:-1] + (MQW,))


def _prep_layer_weights(w_in, w_out, w_uq, w_ukv, small):
    def vec(v):
        return v.reshape(1, -1).astype(F32)

    zeros = functools.partial(jnp.zeros, dtype=F32)
    wg_f, wg_b = small["gla_wg_f"].astype(F32), small["gla_wg_b"].astype(F32)
    wg_pad_f = jnp.concatenate([zeros((MROPE, GQK)), wg_f, zeros((LANES - MROPE - GRANK, GQK))], axis=0)
    wg_pad_b = jnp.concatenate([zeros((MROPE + GRANK, GQK)), wg_b, zeros((LANES - MROPE - 2 * GRANK, GQK))], axis=0)
    wt = dict(norm_g=vec(small["norm_g"]), wg_pad_f=wg_pad_f, wg_pad_b=wg_pad_b,
              bg_f=vec(small["gla_bg_f"]), bg_b=vec(small["gla_bg_b"]), gla_norm_g=vec(small["gla_norm_g"]),
              q_norm_g=vec(small["mla_q_norm_g"]), kv_norm_g=vec(small["mla_kv_norm_g"]),
              mla_out_g=vec(small["mla_out_g"]), conv_w=small["conv_w"].astype(F32),
              conv_out_g=vec(small["conv_out_g"]))
    for name, w in (("w_in", w_in), ("w_out", w_out), ("w_uq", w_uq), ("w_ukv", w_ukv)):
        if w is not None:
            wt[name] = w.astype(MXU)
    return wt


def _natural_small(gr):
    return dict(norm_g=gr["norm_g"][0],
                gla_wg_f=gr["wg_pad_f"][MROPE:MROPE + GRANK], gla_bg_f=gr["bg_f"][0],
                gla_wg_b=gr["wg_pad_b"][MROPE + GRANK:MROPE + 2 * GRANK], gla_bg_b=gr["bg_b"][0],
                gla_norm_g=gr["gla_norm_g"][0], mla_q_norm_g=gr["q_norm_g"][0], mla_kv_norm_g=gr["kv_norm_g"][0],
                mla_out_g=gr["mla_out_g"][0], conv_w=gr["conv_w"], conv_out_g=gr["conv_out_g"][0])


def _natural_grads(gr):
    return dict(_natural_small(gr), w_in=_unperm_in_cols(gr["w_in"]), w_out=gr["w_out"],
                mla_w_uq=_unperm_uq_cols(gr["w_uq"]), mla_w_ukv=gr["w_ukv"])


def _exchange(arrs, name, scatter, space):
    n = len(arrs)

    def body(*refs):
        ins, outs = refs[:n], refs[n:2 * n]
        send_sems, recv_sems, loc_sems = refs[2 * n:]
        ax, ay, ac = lax.axis_index("x"), lax.axis_index("y"), lax.axis_index("c")
        me = 4 * ax + 2 * ay + ac

        def src(a, to):
            return ins[a].at[to] if scatter else ins[a]

        def remote(a, r, dst_slot):
            px = 1 - ax if r & 4 else ax
            py = 1 - ay if r & 2 else ay
            pc = 1 - ac if r & 1 else ac
            return pltpu.make_async_remote_copy(
                src_ref=src(a, 4 * px + 2 * py + pc), dst_ref=outs[a].at[dst_slot(4 * px + 2 * py + pc)],
                send_sem=send_sems.at[a, r - 1], recv_sem=recv_sems.at[a, r - 1],
                device_id=(px, py, pc), device_id_type=MESH)

        locs = [pltpu.make_async_copy(src(a, me), outs[a].at[me], loc_sems.at[a]) for a in range(n)]
        for cp in locs:
            cp.start()
        sends = [remote(a, r, lambda peer: me) for r in range(1, N_DEV) for a in range(n)]
        for cp in sends:
            cp.start()
        for r in range(1, N_DEV):
            for a in range(n):
                remote(a, r, lambda peer: peer).wait_recv()
        for cp in sends:
            cp.wait_send()
        for cp in locs:
            cp.wait()

    def out_shape(a):
        return jax.ShapeDtypeStruct(a.shape if scatter else (N_DEV,) + a.shape, a.dtype)

    spec = pl.BlockSpec(memory_space=space)
    return pl.pallas_call(
        body, in_specs=[spec] * n, out_specs=[spec] * n, out_shape=[out_shape(a) for a in arrs],
        scratch_shapes=[pltpu.SemaphoreType.DMA((n, N_DEV - 1)), pltpu.SemaphoreType.DMA((n, N_DEV - 1)),
                        pltpu.SemaphoreType.DMA((n,))],
        name=name, compiler_params=pltpu.CompilerParams(vmem_limit_bytes=VMEM_LIMIT))(*arrs)


def _peer(r):
    ax, ay, ac = lax.axis_index("x"), lax.axis_index("y"), lax.axis_index("c")
    px = 1 - ax if r & 4 else ax
    py = 1 - ay if r & 2 else ay
    pc = 1 - ac if r & 1 else ac
    return (px, py, pc), 4 * px + 2 * py + pc


def _slot(rel_div):
    rel, div = rel_div
    idx = _peer(rel)[1]
    return idx if div == 1 else idx // div


AG_SPREAD = tuple((r, None, (0, 1), (r, 1)) for r in (1, 2, 4, 6))
AG_FORWARD = tuple((1, (k, 1), (k, 1), (1 ^ k, 1)) for k in (2, 4, 6))
RS_PAIR = tuple((1, (1 ^ k, 1), (1 ^ k, 2), (k, 2)) for k in (0, 2, 4, 6))
RS_CHIPS = tuple((r, (r, 2), (0, 2), (r, 2)) for r in (2, 4, 6))


def _plan_copies(plan, n, src_refs, land_refs, send_sems, recv_sems, arriving):
    out = []
    for i, (r, src, dst, recv) in enumerate(plan):
        peer = _peer(r)[0]
        for a in range(n):
            out.append(pltpu.make_async_remote_copy(
                src_ref=src_refs[a] if src is None else src_refs[a].at[_slot(src)],
                dst_ref=land_refs[a].at[_slot(recv if arriving else dst)],
                send_sem=send_sems.at[i * n + a], recv_sem=recv_sems.at[i * n + a],
                device_id=peer, device_id_type=MESH))
    return out


def _exchange_hbm(plan, srcs, lands, name, after=()):
    n = len(lands)
    fresh = isinstance(lands[0], jax.ShapeDtypeStruct)
    ins = ([] if srcs is None else list(srcs)) + ([] if fresh else list(lands))
    ns = 0 if srcs is None else n
    n_data = len(ins)
    ins = ins + list(after)

    def body(*refs):
        outs = refs[len(ins):len(ins) + n]
        send_sems, recv_sems = refs[-2:]
        src_refs = refs[:n] if srcs is not None else refs[ns:ns + n]
        sends = _plan_copies(plan, n, src_refs, outs, send_sems, recv_sems, False)
        for cp in sends:
            cp.start()
        for cp in _plan_copies(plan, n, src_refs, outs, send_sems, recv_sems, True):
            cp.wait_recv()
        for cp in sends:
            cp.wait_send()

    hbm = pl.BlockSpec(memory_space=pltpu.HBM)
    k = len(plan) * n
    return pl.pallas_call(
        body, name=name, in_specs=[hbm] * n_data + [pl.BlockSpec(memory_space=pl.ANY)] * len(after), out_specs=[hbm] * n,
        out_shape=[jax.ShapeDtypeStruct(a.shape, a.dtype) for a in lands],
        scratch_shapes=[pltpu.SemaphoreType.DMA((k,)), pltpu.SemaphoreType.DMA((k,))],
        input_output_aliases={} if fresh else {ns + i: i for i in range(n)},
        compiler_params=pltpu.CompilerParams(vmem_limit_bytes=VMEM_LIMIT))(*ins)


def _plan_start(plan, srcs, land_shapes, after, name):
    n = len(srcs)

    def body(*refs):
        src_refs, land_refs = refs[:n], refs[n:2 * n]
        send_sems, recv_sems = refs[2 * n + 1], refs[2 * n + 2]
        for cp in _plan_copies(plan, n, src_refs, land_refs, send_sems, recv_sems, False):
            cp.start()
        refs[-1][...] = jnp.zeros_like(refs[-1])

    hbm = pl.BlockSpec(memory_space=pltpu.HBM)
    sem = pl.BlockSpec(memory_space=pltpu.SEMAPHORE)
    k = len(plan) * n
    srcs = [pltpu.with_memory_space_constraint(a, pltpu.HBM) for a in srcs]
    lands = [pltpu.with_memory_space_constraint(lax.empty(shp, a.dtype), pltpu.HBM) for shp, a in zip(land_shapes, srcs)]
    res = pl.pallas_call(
        body, name=name,
        in_specs=[hbm] * (2 * n) + [pl.BlockSpec(memory_space=pl.ANY)],
        out_specs=[sem, sem] + [hbm] * (2 * n) + [pl.BlockSpec(memory_space=pltpu.VMEM)],
        out_shape=[pltpu.SemaphoreType.DMA((k,)), pltpu.SemaphoreType.DMA((k,))]
        + [pltpu.HBM(a.shape, a.dtype) for a in srcs] + [pltpu.HBM(shp, a.dtype) for shp, a in zip(land_shapes, srcs)]
        + [jax.ShapeDtypeStruct((8, LANES), F32)],
        input_output_aliases={i: 2 + i for i in range(2 * n)},
        compiler_params=pltpu.CompilerParams(has_side_effects=pltpu.SideEffectType.DATAFLOW_SIDE_EFFECTING),
    )(*srcs, *lands, after)
    return res[0], res[1], list(res[2:2 + n]), list(res[2 + n:2 + 2 * n]), res[-1]


def _plan_wait(plan, handle, after, name):
    send_sems, recv_sems, srcs, lands, _ = handle
    n = len(srcs)
    after = list(after)

    def body(*refs):
        src_refs, land_refs = refs[:n], refs[n:2 * n]
        ssem, rsem = refs[2 * n], refs[2 * n + 1]
        for cp in _plan_copies(plan, n, src_refs, land_refs, ssem, rsem, False):
            cp.wait_send()
        for cp in _plan_copies(plan, n, src_refs, land_refs, ssem, rsem, True):
            cp.wait_recv()

    hbm = pl.BlockSpec(memory_space=pltpu.HBM)
    sem = pl.BlockSpec(memory_space=pltpu.SEMAPHORE)
    res = pl.pallas_call(
        body, name=name,
        in_specs=[hbm] * (2 * n) + [sem, sem] + [pl.BlockSpec(memory_space=pl.ANY)] * len(after),
        out_specs=[hbm] * (2 * n),
        out_shape=[pltpu.HBM(a.shape, a.dtype) for a in srcs] + [pltpu.HBM(a.shape, a.dtype) for a in lands],
        input_output_aliases={i: i for i in range(2 * n)},
        compiler_params=pltpu.CompilerParams(has_side_effects=pltpu.SideEffectType.DATAFLOW_SIDE_EFFECTING),
    )(*srcs, *lands, send_sems, recv_sems, *after)
    return list(res[:n]), list(res[n:])


def _pair_sum(send, got, core, name):
    _, r, c = send.shape
    tr = 256 if r % 256 == 0 else r

    def body(core_ref, s_ref, g_ref, o_ref):
        o_ref[0] = (s_ref[0].astype(F32) + g_ref[0].astype(F32)).astype(o_ref.dtype)

    return pl.pallas_call(
        body, name=name,
        grid_spec=pltpu.PrefetchScalarGridSpec(
            num_scalar_prefetch=1, grid=(N_DEV // 2, r // tr),
            in_specs=[pl.BlockSpec((1, tr, c), lambda kc, i, core_ref: (2 * kc + core_ref[0], i, 0)),
                      pl.BlockSpec((1, tr, c), lambda kc, i, core_ref: (kc, i, 0))],
            out_specs=pl.BlockSpec((1, tr, c), lambda kc, i, core_ref: (kc, i, 0))),
        out_shape=jax.ShapeDtypeStruct((N_DEV // 2, r, c), send.dtype),
        compiler_params=_cp(("parallel", "parallel")))(core, send, got)


def _ada_mod(c_all, ada_w, ada_b_cols, name):
    nl, d, wc = ada_w.shape

    def body(c_ref, w_ref, b_ref, ca_ref, mod_ref):
        cv = c_ref[...]
        ca = cv * _sigmoid(cv)
        ca_ref[...] = ca
        mod_ref[0] = _dotf(ca, w_ref[0]) + b_ref[0]

    return pl.pallas_call(
        body, grid=(nl,),
        in_specs=[pl.BlockSpec((N_DEV, d), lambda l: (0, 0)), pl.BlockSpec((1, d, wc), lambda l: (l, 0, 0)),
                  pl.BlockSpec((1, 1, wc), lambda l: (l, 0, 0))],
        out_specs=[pl.BlockSpec((N_DEV, d), lambda l: (0, 0)), pl.BlockSpec((1, N_DEV, wc), lambda l: (l, 0, 0))],
        out_shape=[jax.ShapeDtypeStruct((N_DEV, d), F32), jax.ShapeDtypeStruct((nl, N_DEV, wc), F32)],
        name=name, compiler_params=_cp(("arbitrary",)))(c_all, ada_w, ada_b_cols)


def _adam(w, g, m, v):
    m2 = ADAM_B1 * m + (1.0 - ADAM_B1) * g
    v2 = ADAM_B2 * v + (1.0 - ADAM_B2) * (g * g)
    m_hat = m2 / (1.0 - ADAM_B1 ** ADAM_STEP)
    v_hat = v2 / (1.0 - ADAM_B2 ** ADAM_STEP)
    delta = -ADAM_LR * (m_hat / (jnp.sqrt(v_hat) + ADAM_EPS) + ADAM_WD * w)
    return delta, m2, v2


def _ada_grad_adam(c_act, d_mod, w, m, v, name):
    nl, d, wc = w.shape
    tk = min(512, d)

    def body(c_ref, dm_ref, w_ref, m_ref, v_ref, g_ref, dl_ref, m2_ref, v2_ref):
        g = _dotf_tn(c_ref[...], dm_ref[0])
        delta, m2, v2 = _adam(w_ref[0], g, m_ref[0], v_ref[0])
        g_ref[0], dl_ref[0], m2_ref[0], v2_ref[0] = g, delta, m2, v2

    blk = pl.BlockSpec((1, tk, wc), lambda l, i: (l, i, 0))
    shp = jax.ShapeDtypeStruct(w.shape, F32)
    return pl.pallas_call(
        body, grid=(nl, d // tk),
        in_specs=[pl.BlockSpec((N_DEV, tk), lambda l, i: (0, i)), pl.BlockSpec((1, N_DEV, wc), lambda l, i: (l, 0, 0)),
                  blk, blk, blk],
        out_specs=[blk] * 4, out_shape=[shp] * 4, name=name,
        compiler_params=_cp(("parallel", "parallel")))(c_act, d_mod, w, m, v)


def _adam_big(recv, w, m, v, layer, prev, name, after=()):
    nl, r, c = w.shape
    tr = 256 if r % 256 == 0 else r
    nparts = recv.shape[0]

    def body(rc_ref, w_ref, m_ref, v_ref, *rest):
        g_ref, dl_ref, m2_ref, v2_ref = rest[-4:]
        g = rc_ref[0].astype(F32)
        for d in range(1, nparts):
            g = g + rc_ref[d].astype(F32)
        delta, m2, v2 = _adam(w_ref[0], g, m_ref[0], v_ref[0])
        g_ref[0], dl_ref[0], m2_ref[0], v2_ref[0] = g, delta, m2, v2

    blk = pl.BlockSpec((1, tr, c), lambda i: (layer, i, 0))
    shp = jax.ShapeDtypeStruct(w.shape, F32)
    prev = () if prev is None else tuple(prev)
    return pl.pallas_call(
        body, grid=(r // tr,),
        in_specs=[pl.BlockSpec((nparts, tr, c), lambda i: (0, i, 0)), blk, blk, blk]
        + [pl.BlockSpec(memory_space=pl.ANY)] * (len(prev) + len(after)),
        out_specs=[blk] * 4, out_shape=[shp] * 4, name=name,
        input_output_aliases={4 + j: j for j in range(len(prev))},
        compiler_params=_cp(("parallel",)))(recv, w, m, v, *prev, *after)


def _sum_devices(gathered, name):
    _, r, c = gathered.shape

    def body(g_ref, o_ref):
        acc = g_ref[0]
        for d in range(1, N_DEV):
            acc = acc + g_ref[d]
        o_ref[...] = acc

    spec = pl.BlockSpec(memory_space=pltpu.VMEM)
    return pl.pallas_call(body, in_specs=[spec], out_specs=spec, out_shape=jax.ShapeDtypeStruct((r, c), F32),
                          name=name, compiler_params=pltpu.CompilerParams(vmem_limit_bytes=VMEM_LIMIT))(gathered)


def _adam_small(ws, gs, ms, vs, name):
    n = len(ws)

    def body(*refs):
        for i in range(n):
            w_ref, g_ref, m_ref, v_ref = (refs[k * n + i] for k in range(4))
            dl_ref, m2_ref, v2_ref = (refs[(4 + k) * n + i] for k in range(3))
            dl_ref[...], m2_ref[...], v2_ref[...] = _adam(w_ref[...], g_ref[...], m_ref[...], v_ref[...])

    spec = pl.BlockSpec(memory_space=pltpu.VMEM)
    shapes = [jax.ShapeDtypeStruct(w.shape, F32) for w in ws]
    res = pl.pallas_call(body, in_specs=[spec] * (4 * n), out_specs=[spec] * (3 * n), out_shape=shapes * 3, name=name,
                         compiler_params=pltpu.CompilerParams(vmem_limit_bytes=VMEM_LIMIT))(*ws, *gs, *ms, *vs)
    return res[:n], res[n:2 * n], res[2 * n:]


def _pack(parts):
    flat = jnp.concatenate([p.reshape(-1).astype(F32) for p in parts])
    assert flat.shape[0] % LANES == 0, flat.shape
    return flat.reshape(-1, LANES)


def _unpack(packed, shapes):
    flat = packed.reshape(-1)
    out, off = [], 0
    for shp in shapes:
        size = 1
        for dim in shp:
            size *= dim
        out.append(flat[off:off + size].reshape(shp))
        off += size
    return out


def _gather_cols(g, per):
    g = jnp.moveaxis(g, 0, -2)
    return g.reshape(g.shape[:-2] + (N_DEV * per,))


def _scatter_cols(g, per):
    return jnp.moveaxis(g.reshape(g.shape[:-1] + (N_DEV, per)), -2, 0)


def _my_cols(full, me, per):
    return lax.dynamic_slice_in_dim(full, me * per, per, axis=full.ndim - 1)


def kernel(x, c, positions, ada_w, ada_b, norm_g, w_in, gla_wg_f, gla_bg_f, gla_wg_b, gla_bg_b, gla_norm_g, mla_q_norm_g, mla_kv_norm_g, mla_w_uq, mla_w_ukv, mla_out_g, conv_w, conv_out_g, w_out, final_g, loss_target, m_ada_w, m_ada_b, m_norm_g, m_w_in, m_gla_wg_f, m_gla_bg_f, m_gla_wg_b, m_gla_bg_b, m_gla_norm_g, m_mla_q_norm_g, m_mla_kv_norm_g, m_mla_w_uq, m_mla_w_ukv, m_mla_out_g, m_conv_w, m_conv_out_g, m_w_out, m_final_g, v_ada_w, v_ada_b, v_norm_g, v_w_in, v_gla_wg_f, v_gla_bg_f, v_gla_wg_b, v_gla_bg_b, v_gla_norm_g, v_mla_q_norm_g, v_mla_kv_norm_g, v_mla_w_uq, v_mla_w_ukv, v_mla_out_g, v_conv_w, v_conv_out_g, v_w_out, v_final_g):
    me = 4 * lax.axis_index("x") + 2 * lax.axis_index("y") + lax.axis_index("c")
    nl = ada_w.shape[0]
    s, d = x.shape[1], x.shape[2]
    ada_cols = ada_w.shape[2]
    wgc, cwc = gla_wg_f.shape[2], conv_w.shape[2]

    (g0,) = _exchange([_pack([c, gla_wg_f, gla_wg_b, conv_w])], "gather_small_in", False, pltpu.VMEM)
    g0 = g0.reshape(N_DEV, -1)
    o1, o2, o3 = d, d + gla_wg_f.size, d + 2 * gla_wg_f.size
    c_all = g0[:, :o1]
    wgf_full = _gather_cols(g0[:, o1:o2].reshape((N_DEV,) + gla_wg_f.shape), wgc)
    wgb_full = _gather_cols(g0[:, o2:o3].reshape((N_DEV,) + gla_wg_b.shape), wgc)
    convw_full = _gather_cols(g0[:, o3:].reshape((N_DEV,) + conv_w.shape), cwc)

    ada_b_cols = _my_cols(ada_b, me, ada_cols).reshape(nl, 1, ada_cols)
    c_act, mod_cols = _ada_mod(c_all, ada_w, ada_b_cols, "ada_mod")
    (g1,) = _exchange([_pack([mod_cols])], "gather_mod", False, pltpu.VMEM)
    mod_all = g1.reshape(N_DEV, nl, N_DEV, ada_cols)
    mod_mine = _gather_cols(lax.dynamic_index_in_dim(mod_all, me, axis=2, keepdims=False), ada_cols)

    inv_freq = ROPE_THETA ** (-jnp.arange(0, MROPE, 2, dtype=F32) / MROPE)
    ang = positions[0].astype(F32)[:, None] * inv_freq
    cos, sin = jnp.tile(jnp.cos(ang), (1, LANES * 2 // MROPE)), jnp.tile(jnp.sin(ang), (1, LANES * 2 // MROPE))

    big = [w_in, w_out, mla_w_uq, mla_w_ukv]
    big_names = ["w_in", "w_out", "mla_w_uq", "mla_w_ukv"]

    def local_blocks(l):
        return [w[l].astype(MXU) for w in big]

    def put_own(lands, own):
        return [lax.dynamic_update_index_in_dim(ld, o, me, 0) for ld, o in zip(lands, own)]

    def layer_weights(l, gw_in=None, gw_out=None, gw_uq=None, gw_ukv=None):
        small = dict(norm_g=norm_g[l], gla_wg_f=wgf_full[l], gla_bg_f=gla_bg_f[l], gla_wg_b=wgb_full[l],
                     gla_bg_b=gla_bg_b[l], gla_norm_g=gla_norm_g[l], mla_q_norm_g=mla_q_norm_g[l],
                     mla_kv_norm_g=mla_kv_norm_g[l], mla_out_g=mla_out_g[l], conv_w=convw_full[l],
                     conv_out_g=conv_out_g[l])
        return _prep_layer_weights(
            None if gw_in is None else _perm_gathered(gw_in, IN_SEGS, PW),
            None if gw_out is None else gw_out.reshape((-1,) + gw_out.shape[2:]),
            None if gw_uq is None else _perm_gathered(gw_uq, UQ_SEGS, MQW),
            None if gw_ukv is None else _gather_cols(gw_ukv, mla_w_ukv.shape[2]), small)

    def land_shapes(blocks, slots):
        return [jax.ShapeDtypeStruct((slots,) + b.shape, b.dtype) for b in blocks]

    def slots_of(blocks):
        return [(N_DEV,) + b.shape for b in blocks]

    def forwarded(lands, blocks, tag):
        return put_own(_exchange_hbm(AG_FORWARD, None, lands, f"gather_{tag}_forward"), blocks)

    first = local_blocks(0)
    w_in_start = _plan_start(AG_SPREAD, first[:1], slots_of(first[:1]), mod_mine, "gather_w_in_l0_start")
    adam_w_in = [a + w_in_start[-1][0, 0] for a in (w_in, m_w_in, v_w_in)]
    (gw_in,) = forwarded(*reversed(_plan_wait(AG_SPREAD, w_in_start, adam_w_in, "gather_w_in_l0_wait")), "w_in_l0")
    rest = _plan_start(AG_SPREAD, first[1:], slots_of(first[1:]), gw_in, "gather_rest_l0_start")
    h = x[0]
    saved, layers, mods = [], [], []
    pending = {}
    for l in range(nl):
        shift, scale, gate = (mod_mine[l, i * d:(i + 1) * d].reshape(1, d) for i in range(3))
        nxt = local_blocks(l + 1) if l + 1 < nl else None

        def start_next(after, wt_late, l=l, nxt=nxt):
            if nxt is not None:
                pending[l + 1] = _plan_start(AG_SPREAD, nxt, slots_of(nxt), after, f"gather_weights_l{l + 1}_start")
                wt_late["q_norm_g"] = layers[l]["q_norm_g"] + pending[l + 1][-1][0, 0]
            return wt_late

        if l == 0:
            in_after = (rest[-1],)
            layers.append(layer_weights(0, gw_in))

            def late(proj):
                got = forwarded(*reversed(_plan_wait(AG_SPREAD, rest, [proj], "gather_rest_l0_wait")), "rest_l0")
                full = layer_weights(0, None, *got)
                return start_next(got[0], {k: full[k] for k in ("w_out", "w_uq", "w_ukv")})
        else:
            got = forwarded(*reversed(_plan_wait(AG_SPREAD, pending.pop(l), [h], f"gather_weights_l{l}_wait")), f"weights_l{l}")
            layers.append(layer_weights(l, *got))
            in_after = ()

            def late(proj):
                return start_next(proj, {})
        mods.append((shift, scale, gate))
        h, sv = _layer_fwd(h, mods[l], layers[l], cos, sin, f"l{l}", late, in_after)
        saved.append(sv)
        blocks = nxt
    loss_part, d_h, d_final_g = _final_loss(h, final_g.reshape(1, d), loss_target[0], "final_loss")
    loss = lax.psum(loss_part[0, 0], ("x", "y", "c"))
    shift, scale, gate = mods[-1]
    mods[-1] = (shift, scale, gate + 0.0 * loss)

    def grad_sends(gr):
        return [_scatter_perm(gr["w_in"], IN_SEGS, w_in.shape[2]).astype(MXU),
                gr["w_out"].reshape((N_DEV,) + w_out.shape[1:]).astype(MXU),
                _scatter_perm(gr["w_uq"], UQ_SEGS, mla_w_uq.shape[2]).astype(MXU),
                _scatter_cols(gr["w_ukv"], mla_w_ukv.shape[2]).astype(MXU)]

    my_chip = me // 2
    my_core = (me % 2).astype(jnp.int32).reshape(1)

    def chip_sums(gr, tag):
        sends = grad_sends(gr)
        got = _exchange_hbm(RS_PAIR, sends, land_shapes([sd[0] for sd in sends], N_DEV // 2), f"scatter_grads_{tag}_pair")
        return [_pair_sum(sd, gt, my_core, f"pair_sum_{n}_{tag}") for sd, gt, n in zip(sends, got, big_names)]

    def with_own_chip(lands, sums):
        return [lax.dynamic_update_index_in_dim(ld, lax.dynamic_index_in_dim(sm, my_chip, axis=0, keepdims=False),
                                                my_chip, 0) for ld, sm in zip(lands, sums)]

    small_names = ["norm_g", "gla_wg_f", "gla_bg_f", "gla_wg_b", "gla_bg_b", "gla_norm_g", "mla_q_norm_g",
                   "mla_kv_norm_g", "mla_out_g", "conv_w", "conv_out_g"]
    d_mods, grads, recv = [None] * nl, [None] * nl, [None] * nl
    flight = {}
    small = {}

    def gather_small(d_x, d_mod0, gr0):
        d_mods[0], grads[0] = d_mod0, _natural_small(gr0)
        d_mod_mine = jnp.stack([jnp.concatenate(d_mods[l], axis=-1)[0] for l in range(nl)])
        parts = [d_mod_mine] + [jnp.stack([grads[l][n] for l in range(nl)]) for n in small_names] + [d_final_g]
        (g2,) = _exchange([_pack(parts)], "gather_small_grads", False, pltpu.VMEM)
        small["d_mod_all"] = g2.reshape(N_DEV, -1)[:, :d_mod_mine.size].reshape(N_DEV, nl, 3 * d)
        small["summed"] = dict(zip(["ada_b"] + small_names + ["final_g"],
                                   _unpack(_sum_devices(g2, "sum_small_grads"), [p.shape for p in parts])))
        return (g2,)

    pairs = {}
    for l in reversed(range(nl)):
        def ship(big_grads, l=l):
            if l + 1 in flight:
                sm, lands = _plan_wait(RS_CHIPS, flight.pop(l + 1)[0], list(big_grads.values()),
                                       f"scatter_grads_l{l + 1}_wait")
                recv[l + 1] = with_own_chip(lands, sm)
            if l > 0:
                sends = grad_sends(big_grads)
                pairs[l] = (_plan_start(RS_PAIR, sends, [(N_DEV // 2,) + sd.shape[1:] for sd in sends],
                                        big_grads["w_in"], f"scatter_grads_l{l}_pair_start"), sends)
                return (pairs[l][0][-1],)
            sm = chip_sums(big_grads, f"l{l}")
            flight[l] = (_plan_start(RS_CHIPS, sm, [a.shape for a in sm], recv[l + 1][0] if l + 1 < nl else sm[0],
                                     f"scatter_grads_l{l}_start"), sm)
            return (flight[l][0][-1],)

        shift, scale, gate = mods[l]
        if l + 1 in flight:
            gate = gate + flight[l + 1][0][-1][0, 0]
        if l > 0:
            d_h, d_mods[l], gr = _layer_bwd(d_h, saved[l], (shift, scale, gate), layers[l], cos, sin, f"l{l}", ship)
            grads[l] = _natural_small(gr)
            sends, got = _plan_wait(RS_PAIR, pairs.pop(l)[0], [d_h], f"scatter_grads_l{l}_pair_wait")
            sm = [_pair_sum(sd, gt, my_core, f"pair_sum_{n}_l{l}") for sd, gt, n in zip(sends, got, big_names)]
            flight[l] = (_plan_start(RS_CHIPS, sm, [a.shape for a in sm], sm[0], f"scatter_grads_l{l}_start"), sm)
        else:
            d_h, _, _ = _layer_bwd(d_h, saved[l], (shift, scale, gate), layers[l], cos, sin, f"l{l}", ship, gather_small)
    pending, sums = flight.pop(0)
    grad_x = d_h[None]
    summed = small["summed"]
    summed["gla_wg_f"] = _my_cols(summed["gla_wg_f"], me, wgc)
    summed["gla_wg_b"] = _my_cols(summed["gla_wg_b"], me, wgc)
    summed["conv_w"] = _my_cols(summed["conv_w"], me, cwc)

    d_mod_cols = jnp.moveaxis(_my_cols(small["d_mod_all"], me, ada_cols), 0, 1) + pending[-1][0, 0]
    out = {}
    out["ada_w"] = _ada_grad_adam(c_act, d_mod_cols, ada_w, m_ada_w, v_ada_w, "ada_grad_adam")

    given = dict(ada_b=(ada_b, m_ada_b, v_ada_b), norm_g=(norm_g, m_norm_g, v_norm_g),
                 gla_wg_f=(gla_wg_f, m_gla_wg_f, v_gla_wg_f), gla_bg_f=(gla_bg_f, m_gla_bg_f, v_gla_bg_f),
                 gla_wg_b=(gla_wg_b, m_gla_wg_b, v_gla_wg_b), gla_bg_b=(gla_bg_b, m_gla_bg_b, v_gla_bg_b),
                 gla_norm_g=(gla_norm_g, m_gla_norm_g, v_gla_norm_g),
                 mla_q_norm_g=(mla_q_norm_g, m_mla_q_norm_g, v_mla_q_norm_g),
                 mla_kv_norm_g=(mla_kv_norm_g, m_mla_kv_norm_g, v_mla_kv_norm_g),
                 mla_out_g=(mla_out_g, m_mla_out_g, v_mla_out_g), conv_w=(conv_w, m_conv_w, v_conv_w),
                 conv_out_g=(conv_out_g, m_conv_out_g, v_conv_out_g), final_g=(final_g, m_final_g, v_final_g))
    names = list(given)

    def two_d(a):
        return a.reshape(1, -1) if a.ndim == 1 else a

    g_nat = [summed[n].reshape(given[n][0].shape) for n in names]
    res = _adam_small([two_d(given[n][0]) for n in names], [two_d(g) for g in g_nat],
                      [two_d(given[n][1]) for n in names], [two_d(given[n][2]) for n in names], "adam_small")
    for i, n in enumerate(names):
        out[n] = (g_nat[i],) + tuple(r[i].reshape(given[n][0].shape) for r in res)

    state = dict(w_in=adam_w_in, w_out=(w_out, m_w_out, v_w_out), mla_w_uq=(mla_w_uq, m_mla_w_uq, v_mla_w_uq),
                 mla_w_ukv=(mla_w_ukv, m_mla_w_ukv, v_mla_w_ukv))
    done = [out["ada_w"][0], res[0][0]]
    for l in reversed(range(nl)):
        if l == 0:
            recv[0] = with_own_chip(*reversed(_plan_wait(RS_CHIPS, pending, done, "scatter_grads_l0_wait")))
        for i, n in enumerate(big_names):
            out[n] = _adam_big(recv[l][i], *state[n], l, out.get(n), f"adam_{n}_l{l}",
                               (pending[-1],))
        done = done + [out[n][0] for n in big_names]

    order = ["ada_w", "ada_b", "norm_g", "w_in", "gla_wg_f", "gla_bg_f", "gla_wg_b", "gla_bg_b", "gla_norm_g",
             "mla_q_norm_g", "mla_kv_norm_g", "mla_w_uq", "mla_w_ukv", "mla_out_g", "conv_w", "conv_out_g", "w_out",
             "final_g"]
    return (loss, grad_x, *[out[n][0] for n in order], *[out[n][1] for n in order], *[out[n][2] for n in order],
            *[out[n][3] for n in order])
```

```python
import functools

import jax
import jax.numpy as jnp
from jax import lax
from jax.experimental import pallas as pl
from jax.experimental.pallas import tpu as pltpu

F32 = jnp.float32
MXU = jnp.bfloat16
HI = lax.Precision.HIGHEST
N_DEV = 8
MESH = pl.DeviceIdType.MESH

D_MIX = 2048
GH, GDK, GDV = 6, 64, 128
GW = GH * GDV
GQK = GH * GDK
GRANK = 16
GTEMP = 16.0
CHUNK = 64
MH, MQL, MKVL, MNOPE, MROPE, MDV = 6, 384, 256, 128, 64, 128
MW = MH * MDV
MQW = MH * (MNOPE + MROPE)
MKVW = MH * (MNOPE + MDV)
CONV_CH = 512
ROPE_THETA = 10000.0
EPS = 1e-6
IN_DIM = 5856
OZ, OCB, OCC, OCX, OMKV, OGV, OGQ, OGK, OMQ, OT = 0, 2048, 2560, 3072, 3584, 3840, 4608, 4992, 5376, 5760
PW = 5888
LANES = 128
VMEM_LIMIT = 56 * 1024 * 1024

ADAM_LR, ADAM_B1, ADAM_B2, ADAM_EPS, ADAM_WD, ADAM_STEP = 0.001, 0.9, 0.999, 1e-08, 0.01, 10


def _cp(sem=None):
    return pltpu.CompilerParams(dimension_semantics=sem, vmem_limit_bytes=VMEM_LIMIT)


def _dot(a, b):
    return jnp.dot(a.astype(MXU), b.astype(MXU), preferred_element_type=F32)


def _dot_nt(a, b):
    return lax.dot_general(a.astype(MXU), b.astype(MXU), (((1,), (1,)), ((), ())), preferred_element_type=F32)


def _dot_tn(a, b):
    return lax.dot_general(a.astype(MXU), b.astype(MXU), (((0,), (0,)), ((), ())), preferred_element_type=F32)


def _dotf(a, b):
    return jnp.dot(a, b, precision=HI, preferred_element_type=F32)


def _dotf_nt(a, b):
    return lax.dot_general(a, b, (((1,), (1,)), ((), ())), precision=HI, preferred_element_type=F32)


def _dotf_tn(a, b):
    return lax.dot_general(a, b, (((0,), (0,)), ((), ())), precision=HI, preferred_element_type=F32)


def _split3(x):
    hi = x.astype(jnp.bfloat16)
    r1 = x - hi.astype(F32)
    mid = r1.astype(jnp.bfloat16)
    lo = (r1 - mid.astype(F32)).astype(jnp.bfloat16)
    return hi, mid, lo


def _cum_dot(cum, x, transpose=False):
    dn = (((0,), (0,)), ((), ())) if transpose else (((1,), (0,)), ((), ()))
    cb = cum.astype(jnp.bfloat16)
    parts = [lax.dot_general(cb, p, dn, preferred_element_type=F32) for p in _split3(x)]
    return parts[0] + parts[1] + parts[2]


def _rows(s):
    return min(256, s)


def _rms(x, g):
    r = lax.rsqrt(jnp.mean(x * x, axis=-1, keepdims=True) + EPS)
    return x * r * g


def _rms_bwd(dy, x, g):
    r = lax.rsqrt(jnp.mean(x * x, axis=-1, keepdims=True) + EPS)
    xh = x * r
    dxh = dy * g
    dg = jnp.sum(dy * xh, axis=0, keepdims=True)
    dx = r * (dxh - xh * jnp.mean(dxh * xh, axis=-1, keepdims=True))
    return dx, dg


def _sigmoid(z):
    return jax.nn.sigmoid(z)


def _matmul(a, b, *, dims, tm, tn, tk, out_dtypes, name, epilogue=None, extras=(), extra_kinds=(), after=()):
    if dims == "nn":
        (m, k), n, mul = a.shape, b.shape[1], _dot
    elif dims == "nt":
        (m, k), n, mul = a.shape, b.shape[0], _dot_nt
    else:
        (k, m), n, mul = a.shape, b.shape[1], _dot_tn
    tm, tn, tk = min(tm, m), min(tn, n), min(tk, k)
    assert m % tm == 0 and n % tn == 0 and k % tk == 0, (m, n, k, tm, tn, tk)
    if dims == "nn":
        a_spec = pl.BlockSpec((tm, tk), lambda i, j, kk: (i, kk))
        b_spec = pl.BlockSpec((tk, tn), lambda i, j, kk: (kk, j))
    elif dims == "nt":
        a_spec = pl.BlockSpec((tm, tk), lambda i, j, kk: (i, kk))
        b_spec = pl.BlockSpec((tn, tk), lambda i, j, kk: (j, kk))
    else:
        a_spec = pl.BlockSpec((tk, tm), lambda i, j, kk: (kk, i))
        b_spec = pl.BlockSpec((tk, tn), lambda i, j, kk: (kk, j))
    nk = k // tk
    n_extra = len(extras)
    n_out = len(out_dtypes)
    n_after = len(after)
    extra_specs = []
    for kind in extra_kinds:
        if kind == "mn":
            extra_specs.append(pl.BlockSpec((tm, tn), lambda i, j, kk: (i, j)))
        else:
            extra_specs.append(pl.BlockSpec((1, tn), lambda i, j, kk: (0, j)))

    def finish(res, ex, outs):
        vals = (res,) if epilogue is None else epilogue(res, *[e[...] for e in ex])
        for o, v in zip(outs, vals):
            o[...] = v.astype(o.dtype)

    def body(*refs):
        a_ref, b_ref = refs[0], refs[1]
        ex = refs[2:2 + n_extra]
        outs = refs[2 + n_extra + n_after:2 + n_extra + n_after + n_out]
        if nk == 1:
            finish(mul(a_ref[...], b_ref[...]), ex, outs)
            return
        acc = refs[-1]
        kk = pl.program_id(2)

        @pl.when(kk == 0)
        def _():
            acc[...] = jnp.zeros_like(acc)

        acc[...] += mul(a_ref[...], b_ref[...])

        @pl.when(kk == nk - 1)
        def _():
            finish(acc[...], ex, outs)

    out_spec = pl.BlockSpec((tm, tn), lambda i, j, kk: (i, j))
    res = pl.pallas_call(
        body, grid=(m // tm, n // tn, nk),
        in_specs=[a_spec, b_spec] + extra_specs + [pl.BlockSpec(memory_space=pl.ANY)] * n_after,
        out_specs=[out_spec] * n_out,
        out_shape=[jax.ShapeDtypeStruct((m, n), dt) for dt in out_dtypes],
        scratch_shapes=[] if nk == 1 else [pltpu.VMEM((tm, tn), F32)],
        name=name, compiler_params=_cp(("parallel", "parallel", "arbitrary")),
    )(a, b, *extras, *after)
    return res


def _norm_mod(x, g, scale, shift, name):
    s, d = x.shape
    tr = _rows(s)

    def body(x_ref, g_ref, sc_ref, sh_ref, h_ref):
        h = _rms(x_ref[...], g_ref[...]) * (1.0 + sc_ref[...]) + sh_ref[...]
        h_ref[...] = h.astype(h_ref.dtype)

    row = pl.BlockSpec((tr, d), lambda i: (i, 0))
    vec = pl.BlockSpec((1, d), lambda i: (0, 0))
    return pl.pallas_call(body, grid=(s // tr,), in_specs=[row, vec, vec, vec], out_specs=row,
                          out_shape=jax.ShapeDtypeStruct((s, d), MXU), name=name,
                          compiler_params=_cp(("parallel",)))(x, g, scale, shift)


def _norm_mod_bwd(d_h, x, d_out, g, scale, name):
    s, d = x.shape
    tr = _rows(s)

    def body(dh_ref, x_ref, do_ref, g_ref, sc_ref, dx_ref, dsh_ref, dsc_ref, dg_ref):
        i = pl.program_id(0)

        @pl.when(i == 0)
        def _():
            dsh_ref[...] = jnp.zeros_like(dsh_ref)
            dsc_ref[...] = jnp.zeros_like(dsc_ref)
            dg_ref[...] = jnp.zeros_like(dg_ref)

        dh = dh_ref[...]
        xv = x_ref[...]
        gv = g_ref[...]
        r = lax.rsqrt(jnp.mean(xv * xv, axis=-1, keepdims=True) + EPS)
        xh = xv * r
        dsh_ref[...] += jnp.sum(dh, axis=0, keepdims=True)
        dsc_ref[...] += jnp.sum(dh * (xh * gv), axis=0, keepdims=True)
        dhn = dh * (1.0 + sc_ref[...])
        dg_ref[...] += jnp.sum(dhn * xh, axis=0, keepdims=True)
        dxh = dhn * gv
        dx_ref[...] = do_ref[...] + r * (dxh - xh * jnp.mean(dxh * xh, axis=-1, keepdims=True))

    row = pl.BlockSpec((tr, d), lambda i: (i, 0))
    vec = pl.BlockSpec((1, d), lambda i: (0, 0))
    vshape = jax.ShapeDtypeStruct((1, d), F32)
    return pl.pallas_call(body, grid=(s // tr,), in_specs=[row, row, row, vec, vec],
                          out_specs=[row, vec, vec, vec],
                          out_shape=[jax.ShapeDtypeStruct((s, d), F32), vshape, vshape, vshape],
                          name=name, compiler_params=_cp(("arbitrary",)))(d_h, x, d_out, g, scale)


def _gate_bwd(d_out, u, gate, name):
    s, d = d_out.shape
    tr = _rows(s)

    def body(do_ref, u_ref, gt_ref, du_ref, dgt_ref):
        @pl.when(pl.program_id(0) == 0)
        def _():
            dgt_ref[...] = jnp.zeros_like(dgt_ref)

        do = do_ref[...]
        du_ref[...] = (do * gt_ref[...]).astype(du_ref.dtype)
        dgt_ref[...] += jnp.sum(do * u_ref[...], axis=0, keepdims=True)

    row = pl.BlockSpec((tr, d), lambda i: (i, 0))
    vec = pl.BlockSpec((1, d), lambda i: (0, 0))
    return pl.pallas_call(body, grid=(s // tr,), in_specs=[row, row, vec], out_specs=[row, vec],
                          out_shape=[jax.ShapeDtypeStruct((s, d), MXU), jax.ShapeDtypeStruct((1, d), F32)],
                          name=name, compiler_params=_cp(("arbitrary",)))(d_out, u, gate)


def _final_loss(x, g, target, name):
    s, d = x.shape
    tr = _rows(s)

    def body(x_ref, g_ref, t_ref, loss_ref, dx_ref, dg_ref):
        @pl.when(pl.program_id(0) == 0)
        def _():
            loss_ref[...] = jnp.zeros_like(loss_ref)
            dg_ref[...] = jnp.zeros_like(dg_ref)

        xv = x_ref[...]
        gv = g_ref[...]
        diff = _rms(xv, gv) - t_ref[...]
        part = 0.5 * jnp.sum(jnp.sum(diff * diff, axis=-1, keepdims=True) / d, axis=0, keepdims=True)
        loss_ref[...] += jnp.broadcast_to(part, loss_ref.shape)
        dx, dg = _rms_bwd(diff / d, xv, gv)
        dx_ref[...] = dx
        dg_ref[...] += dg

    row = pl.BlockSpec((tr, d), lambda i: (i, 0))
    vec = pl.BlockSpec((1, d), lambda i: (0, 0))
    lvec = pl.BlockSpec((1, LANES), lambda i: (0, 0))
    return pl.pallas_call(body, grid=(s // tr,), in_specs=[row, vec, row], out_specs=[lvec, row, vec],
                          out_shape=[jax.ShapeDtypeStruct((1, LANES), F32), jax.ShapeDtypeStruct((s, d), F32),
                                     jax.ShapeDtypeStruct((1, d), F32)],
                          name=name, compiler_params=_cp(("arbitrary",)))(x, g, target)


def _shift_rows(u, s, down):
    ri = lax.broadcasted_iota(jnp.int32, u.shape, 0)
    if down:
        return jnp.where(ri == 0, 0.0, pltpu.roll(u, 1, 0))
    return jnp.where(ri == s - 1, 0.0, pltpu.roll(u, s - 1, 0))


def _conv_fwd(proj, conv_w, name):
    s = proj.shape[0]
    nt = CONV_CH // LANES

    def body(cb_ref, cc_ref, cx_ref, w_ref, pre_ref):
        u = cc_ref[...] * cx_ref[...]
        conv = _shift_rows(u, s, True) * w_ref[0:1, :] + u * w_ref[1:2, :] + _shift_rows(u, s, False) * w_ref[2:3, :]
        pre_ref[...] = cb_ref[...] * conv

    def col(off):
        return pl.BlockSpec((s, LANES), lambda j: (0, off // LANES + j))

    return pl.pallas_call(body, grid=(nt,), in_specs=[col(OCB), col(OCC), col(OCX), pl.BlockSpec((3, LANES), lambda j: (0, j))],
                          out_specs=pl.BlockSpec((s, LANES), lambda j: (0, j)),
                          out_shape=jax.ShapeDtypeStruct((s, CONV_CH), F32), name=name,
                          compiler_params=_cp(("parallel",)))(proj, proj, proj, conv_w)


def _conv_bwd(proj, conv_w, d_pre, name):
    s = proj.shape[0]
    nt = CONV_CH // LANES

    def body(cb_ref, cc_ref, cx_ref, w_ref, dp_ref, dcb_ref, dcc_ref, dcx_ref, dw_ref):
        cc, cx = cc_ref[...], cx_ref[...]
        u = cc * cx
        up, dn = _shift_rows(u, s, True), _shift_rows(u, s, False)
        w0, w1, w2 = w_ref[0:1, :], w_ref[1:2, :], w_ref[2:3, :]
        conv = up * w0 + u * w1 + dn * w2
        dp = dp_ref[...]
        dcb_ref[...] = dp * conv
        dconv = dp * cb_ref[...]
        du = _shift_rows(dconv, s, False) * w0 + dconv * w1 + _shift_rows(dconv, s, True) * w2
        dcc_ref[...] = du * cx
        dcx_ref[...] = du * cc
        dw_ref[0:1, :] = jnp.sum(dconv * up, axis=0, keepdims=True)
        dw_ref[1:2, :] = jnp.sum(dconv * u, axis=0, keepdims=True)
        dw_ref[2:3, :] = jnp.sum(dconv * dn, axis=0, keepdims=True)

    def col(off):
        return pl.BlockSpec((s, LANES), lambda j: (0, off // LANES + j))

    blk = pl.BlockSpec((s, LANES), lambda j: (0, j))
    wblk = pl.BlockSpec((3, LANES), lambda j: (0, j))
    full = jax.ShapeDtypeStruct((s, CONV_CH), F32)
    return pl.pallas_call(body, grid=(nt,), in_specs=[col(OCB), col(OCC), col(OCX), wblk, blk],
                          out_specs=[blk, blk, blk, wblk],
                          out_shape=[full, full, full, jax.ShapeDtypeStruct((3, CONV_CH), F32)],
                          name=name, compiler_params=_cp(("parallel",)))(proj, proj, proj, conv_w, d_pre)


GLA_SUB = 8


def _gla_gates(t_ref, wg_ref, bg_ref):
    t = t_ref[...]
    a = _dot(t, wg_ref[...]) + bg_ref[...]
    la = (jnp.minimum(a, 0.0) - jnp.log(1.0 + jnp.exp(-jnp.abs(a)))) / GTEMP
    return t, a, la


def _gla_masks(reverse):
    ri = lax.broadcasted_iota(jnp.int32, (CHUNK, CHUNK), 0)
    ci = lax.broadcasted_iota(jnp.int32, (CHUNK, CHUNK), 1)
    if reverse:
        cum, mask = ci >= ri, ci > ri
    else:
        cum, mask = ci <= ri, ci <= ri
    return cum.astype(F32), mask


def _gla_specs(s, reverse):
    nsub = min(GLA_SUB, s // CHUNK)
    nsteps = s // (CHUNK * nsub)

    def row(n):
        return nsteps - 1 - n if reverse else n

    def chunk(pi):
        return nsub - 1 - pi if reverse else pi

    return nsub, nsteps, row, chunk


def _gla_fwd(proj, wg_pad, bg, reverse, name):
    s = proj.shape[0]
    nsub, nsteps, row, chunk = _gla_specs(s, reverse)
    rb = nsub * CHUNK

    def body(q_ref, k_ref, v_ref, t_ref, wg_ref, bg_ref, o_ref, st_ref, state):
        @pl.when(pl.program_id(0) == 0)
        def _():
            state[...] = jnp.zeros_like(state)

        _, _, la = _gla_gates(t_ref, wg_ref, bg_ref)
        cumf, mask = _gla_masks(reverse)
        lane = lax.broadcasted_iota(jnp.int32, (CHUNK, LANES), 1)
        for pi in range(nsub):
            rows = slice(chunk(pi) * CHUNK, (chunk(pi) + 1) * CHUNK)
            la_c = la[rows]
            b = _cum_dot(cumf, la_c)
            bl = jnp.sum(la_c, axis=0, keepdims=True)
            q = q_ref[rows, :] * (GDK ** -0.5)
            k = k_ref[rows, :]
            qd = q * jnp.exp(b)
            ki = k * jnp.exp(-b)
            kte = k * jnp.exp(bl - b)
            decay = jnp.exp(bl)
            for h in range(GH):
                p = h // 2
                sl = slice(p * LANES, (p + 1) * LANES)
                lm = (lane < GDK) if h % 2 == 0 else (lane >= GDK)
                qd_h = jnp.where(lm, qd[:, sl], 0.0)
                kte_h = jnp.where(lm, kte[:, sl], 0.0)
                v_h = v_ref[rows, h * GDV:(h + 1) * GDV]
                st = state[h]
                a_mat = jnp.where(mask, _dot_nt(qd_h, ki[:, sl]), 0.0)
                o_ref[rows, h * GDV:(h + 1) * GDV] = _dot(a_mat, v_h) + _dot_nt(qd_h, st)
                st_ref[pi, h] = st
                state[h] = st * decay[:, sl] + _dot_tn(v_h, kte_h)

    return pl.pallas_call(
        body, grid=(nsteps,),
        in_specs=[pl.BlockSpec((rb, GQK), lambda n: (row(n), OGQ // GQK)),
                  pl.BlockSpec((rb, GQK), lambda n: (row(n), OGK // GQK)),
                  pl.BlockSpec((rb, GW), lambda n: (row(n), OGV // GW)),
                  pl.BlockSpec((rb, LANES), lambda n: (row(n), OT // LANES)),
                  pl.BlockSpec((LANES, GQK), lambda n: (0, 0)),
                  pl.BlockSpec((1, GQK), lambda n: (0, 0))],
        out_specs=[pl.BlockSpec((rb, GW), lambda n: (row(n), 0)),
                   pl.BlockSpec((nsub, GH, GDV, LANES), lambda n: (n, 0, 0, 0))],
        out_shape=[jax.ShapeDtypeStruct((s, GW), F32), jax.ShapeDtypeStruct((s // CHUNK, GH, GDV, LANES), F32)],
        scratch_shapes=[pltpu.VMEM((GH, GDV, LANES), F32)],
        name=name, compiler_params=_cp(("arbitrary",)))(proj, proj, proj, proj, wg_pad, bg)


def _gla_bwd(proj, wg_pad, bg, states, d_o, reverse, name):
    s = proj.shape[0]
    nsub, nsteps, row, chunk = _gla_specs(s, reverse)
    rb = nsub * CHUNK

    def body(q_ref, k_ref, v_ref, t_ref, wg_ref, bg_ref, st_ref, do_ref,
             dq_ref, dk_ref, dv_ref, dt_ref, dwg_ref, dbg_ref, dstate, da_buf):
        @pl.when(pl.program_id(0) == 0)
        def _():
            dstate[...] = jnp.zeros_like(dstate)
            dwg_ref[...] = jnp.zeros_like(dwg_ref)
            dbg_ref[...] = jnp.zeros_like(dbg_ref)

        t, a, la = _gla_gates(t_ref, wg_ref, bg_ref)
        cumf, mask = _gla_masks(reverse)
        lane = lax.broadcasted_iota(jnp.int32, (CHUNK, LANES), 1)
        for pi in reversed(range(nsub)):
            rows = slice(chunk(pi) * CHUNK, (chunk(pi) + 1) * CHUNK)
            la_c = la[rows]
            b = _cum_dot(cumf, la_c)
            bl = jnp.sum(la_c, axis=0, keepdims=True)
            q = q_ref[rows, :] * (GDK ** -0.5)
            k = k_ref[rows, :]
            e, ei, ee = jnp.exp(b), jnp.exp(-b), jnp.exp(bl - b)
            qd, ki, kte = q * e, k * ei, k * ee
            decay = jnp.exp(bl)
            for p in range(GH // 2):
                sl = slice(p * LANES, (p + 1) * LANES)
                dqd = jnp.zeros((CHUNK, LANES), F32)
                dki = jnp.zeros((CHUNK, LANES), F32)
                dkte = jnp.zeros((CHUNK, LANES), F32)
                ddecay = jnp.zeros((1, LANES), F32)
                for half in range(2):
                    h = 2 * p + half
                    lm = (lane < GDK) if half == 0 else (lane >= GDK)
                    qd_h = jnp.where(lm, qd[:, sl], 0.0)
                    ki_h = jnp.where(lm, ki[:, sl], 0.0)
                    kte_h = jnp.where(lm, kte[:, sl], 0.0)
                    v_h = v_ref[rows, h * GDV:(h + 1) * GDV]
                    do_h = do_ref[rows, h * GDV:(h + 1) * GDV]
                    st = st_ref[pi, h]
                    dst = dstate[h]
                    a_mat = jnp.where(mask, _dot_nt(qd_h, ki_h), 0.0)
                    da_mat = jnp.where(mask, _dot_nt(do_h, v_h), 0.0)
                    dv_ref[rows, h * GDV:(h + 1) * GDV] = _dot_tn(a_mat, do_h) + _dot_nt(kte_h, dst)
                    dqd += _dot(da_mat, ki_h) + _dot(do_h, st)
                    dki += _dot_tn(da_mat, qd_h)
                    dkte += _dot(v_h, dst)
                    ddecay += jnp.sum(dst * st, axis=0, keepdims=True)
                    dstate[h] = dst * decay[:, sl] + _dot_tn(do_h, qd_h)
                dq_ref[rows, sl] = dqd * e[:, sl] * (GDK ** -0.5)
                dk_ref[rows, sl] = dki * ei[:, sl] + dkte * ee[:, sl]
                db = dqd * qd[:, sl] - dki * ki[:, sl] - dkte * kte[:, sl]
                dbl = jnp.sum(dkte * kte[:, sl], axis=0, keepdims=True) + decay[:, sl] * ddecay
                da_buf[rows, sl] = _cum_dot(cumf, db, True) + dbl
        da = da_buf[...] * (1.0 / GTEMP) * _sigmoid(-a)
        dt_ref[...] = _dot_nt(da, wg_ref[...])
        dwg_ref[...] += _dot_tn(t, da)
        dbg_ref[...] += jnp.sum(da, axis=0, keepdims=True)

    def prow(j):
        return row(nsteps - 1 - j)

    return pl.pallas_call(
        body, grid=(nsteps,),
        in_specs=[pl.BlockSpec((rb, GQK), lambda j: (prow(j), OGQ // GQK)),
                  pl.BlockSpec((rb, GQK), lambda j: (prow(j), OGK // GQK)),
                  pl.BlockSpec((rb, GW), lambda j: (prow(j), OGV // GW)),
                  pl.BlockSpec((rb, LANES), lambda j: (prow(j), OT // LANES)),
                  pl.BlockSpec((LANES, GQK), lambda j: (0, 0)),
                  pl.BlockSpec((1, GQK), lambda j: (0, 0)),
                  pl.BlockSpec((nsub, GH, GDV, LANES), lambda j: (nsteps - 1 - j, 0, 0, 0)),
                  pl.BlockSpec((rb, GW), lambda j: (prow(j), 0))],
        out_specs=[pl.BlockSpec((rb, GQK), lambda j: (prow(j), 0)),
                   pl.BlockSpec((rb, GQK), lambda j: (prow(j), 0)),
                   pl.BlockSpec((rb, GW), lambda j: (prow(j), 0)),
                   pl.BlockSpec((rb, LANES), lambda j: (prow(j), 0)),
                   pl.BlockSpec((LANES, GQK), lambda j: (0, 0)),
                   pl.BlockSpec((1, GQK), lambda j: (0, 0))],
        out_shape=[jax.ShapeDtypeStruct((s, GQK), F32), jax.ShapeDtypeStruct((s, GQK), F32),
                   jax.ShapeDtypeStruct((s, GW), F32), jax.ShapeDtypeStruct((s, LANES), F32),
                   jax.ShapeDtypeStruct((LANES, GQK), F32), jax.ShapeDtypeStruct((1, GQK), F32)],
        scratch_shapes=[pltpu.VMEM((GH, GDV, LANES), F32), pltpu.VMEM((rb, GQK), F32)],
        name=name, compiler_params=_cp(("arbitrary",)))(proj, proj, proj, proj, wg_pad, bg, states, d_o)


def _rot_half(x):
    lane = lax.broadcasted_iota(jnp.int32, x.shape, 1)
    first = (lane % MROPE) < (MROPE // 2)
    return jnp.where(first, -pltpu.roll(x, LANES - MROPE // 2, 1), pltpu.roll(x, MROPE // 2, 1))


def _mla_prep(proj, cos, sin, qg, kvg, w_uq, w_ukv, name):
    s = proj.shape[0]
    tr = _rows(s)

    def body(mq_ref, mkv_ref, t_ref, cos_ref, sin_ref, qg_ref, kvg_ref, wuq_ref, wukv_ref, q_ref, k_ref, v_ref):
        cosv, sinv = cos_ref[...], sin_ref[...]
        lane = lax.broadcasted_iota(jnp.int32, (tr, LANES), 1)

        def rope(xv):
            return xv * cosv + _rot_half(xv) * sinv

        qm = _dot(_rms(mq_ref[...], qg_ref[...]), wuq_ref[...])
        kv = _dot(_rms(mkv_ref[...], kvg_ref[...]), wukv_ref[...])
        kr_lo = jnp.where(lane < MROPE, rope(t_ref[...]), 0.0)
        kr_hi = pltpu.roll(kr_lo, MROPE, 1)
        for p in range(MH // 2):
            r = rope(qm[:, MW + p * LANES:MW + (p + 1) * LANES]).astype(q_ref.dtype)
            q_ref[2 * p, :, LANES:] = r
            q_ref[2 * p + 1, :, LANES:] = r
        for h in range(MH):
            q_ref[h, :, :LANES] = qm[:, h * LANES:(h + 1) * LANES].astype(q_ref.dtype)
            k_ref[h, :, :LANES] = kv[:, 2 * h * LANES:(2 * h + 1) * LANES].astype(k_ref.dtype)
            k_ref[h, :, LANES:] = (kr_lo if h % 2 == 0 else kr_hi).astype(k_ref.dtype)
            v_ref[h] = kv[:, (2 * h + 1) * LANES:(2 * h + 2) * LANES].astype(v_ref.dtype)

    def full(shape):
        return pl.BlockSpec(shape, lambda i: (0,) * len(shape))

    return pl.pallas_call(
        body, grid=(s // tr,),
        in_specs=[pl.BlockSpec((tr, MQL), lambda i: (i, OMQ // MQL)),
                  pl.BlockSpec((tr, MKVL), lambda i: (i, OMKV // MKVL)),
                  pl.BlockSpec((tr, LANES), lambda i: (i, OT // LANES)),
                  pl.BlockSpec((tr, LANES), lambda i: (i, 0)),
                  pl.BlockSpec((tr, LANES), lambda i: (i, 0)),
                  full((1, MQL)), full((1, MKVL)), full((MQL, MQW)), full((MKVL, MKVW))],
        out_specs=[pl.BlockSpec((MH, tr, 2 * LANES), lambda i: (0, i, 0)),
                   pl.BlockSpec((MH, tr, 2 * LANES), lambda i: (0, i, 0)),
                   pl.BlockSpec((MH, tr, LANES), lambda i: (0, i, 0))],
        out_shape=[jax.ShapeDtypeStruct((MH, s, 2 * LANES), MXU), jax.ShapeDtypeStruct((MH, s, 2 * LANES), MXU),
                   jax.ShapeDtypeStruct((MH, s, LANES), MXU)],
        name=name, compiler_params=_cp(("parallel",)))(proj, proj, proj, cos, sin, qg, kvg, w_uq, w_ukv)


def _mla_prep_bwd(proj, cos, sin, qg, kvg, w_uq, w_ukv, d_q, d_k, d_v, name):
    s = proj.shape[0]
    tr = _rows(s)

    def body(mq_ref, mkv_ref, cos_ref, sin_ref, qg_ref, kvg_ref, wuq_ref, wukv_ref, dq_ref, dk_ref, dv_ref,
             dmq_ref, dmkv_ref, dt_ref, dwuq_ref, dwukv_ref, dqg_ref, dkvg_ref):
        @pl.when(pl.program_id(0) == 0)
        def _():
            for r in (dwuq_ref, dwukv_ref, dqg_ref, dkvg_ref):
                r[...] = jnp.zeros_like(r)

        cosv, sinv = cos_ref[...], sin_ref[...]
        lane = lax.broadcasted_iota(jnp.int32, (tr, LANES), 1)
        lo = lane < MROPE

        def unrope(dv):
            return dv * cosv - _rot_half(dv * sinv)

        parts = [dq_ref[h, :, :LANES] for h in range(MH)]
        for p in range(MH // 2):
            parts.append(unrope(jnp.where(lo, dq_ref[2 * p, :, LANES:], dq_ref[2 * p + 1, :, LANES:])))
        d_qm = jnp.concatenate(parts, axis=1)
        mq, qgv = mq_ref[...], qg_ref[...]
        cq = _rms(mq, qgv)
        dwuq_ref[...] += _dot_tn(cq, d_qm)
        dmq, dqg = _rms_bwd(_dot_nt(d_qm, wuq_ref[...]), mq, qgv)
        dmq_ref[...] = dmq
        dqg_ref[...] += dqg

        parts = []
        for h in range(MH):
            parts += [dk_ref[h, :, :LANES], dv_ref[h]]
        d_kv = jnp.concatenate(parts, axis=1)
        mkv, kvgv = mkv_ref[...], kvg_ref[...]
        ckv = _rms(mkv, kvgv)
        dwukv_ref[...] += _dot_tn(ckv, d_kv)
        dmkv, dkvg = _rms_bwd(_dot_nt(d_kv, wukv_ref[...]), mkv, kvgv)
        dmkv_ref[...] = dmkv
        dkvg_ref[...] += dkvg

        even = dk_ref[0, :, LANES:] + dk_ref[2, :, LANES:] + dk_ref[4, :, LANES:]
        odd = dk_ref[1, :, LANES:] + dk_ref[3, :, LANES:] + dk_ref[5, :, LANES:]
        d_kr = jnp.where(lo, even, 0.0) + pltpu.roll(jnp.where(lo, 0.0, odd), MROPE, 1)
        dt_ref[...] = jnp.where(lo, unrope(d_kr), 0.0)

    def full(shape):
        return pl.BlockSpec(shape, lambda i: (0,) * len(shape))

    return pl.pallas_call(
        body, grid=(s // tr,),
        in_specs=[pl.BlockSpec((tr, MQL), lambda i: (i, OMQ // MQL)),
                  pl.BlockSpec((tr, MKVL), lambda i: (i, OMKV // MKVL)),
                  pl.BlockSpec((tr, LANES), lambda i: (i, 0)),
                  pl.BlockSpec((tr, LANES), lambda i: (i, 0)),
                  full((1, MQL)), full((1, MKVL)), full((MQL, MQW)), full((MKVL, MKVW)),
                  pl.BlockSpec((MH, tr, 2 * LANES), lambda i: (0, i, 0)),
                  pl.BlockSpec((MH, tr, 2 * LANES), lambda i: (0, i, 0)),
                  pl.BlockSpec((MH, tr, LANES), lambda i: (0, i, 0))],
        out_specs=[pl.BlockSpec((tr, MQL), lambda i: (i, 0)), pl.BlockSpec((tr, MKVL), lambda i: (i, 0)),
                   pl.BlockSpec((tr, LANES), lambda i: (i, 0)),
                   full((MQL, MQW)), full((MKVL, MKVW)), full((1, MQL)), full((1, MKVL))],
        out_shape=[jax.ShapeDtypeStruct((s, MQL), F32), jax.ShapeDtypeStruct((s, MKVL), F32),
                   jax.ShapeDtypeStruct((s, LANES), F32),
                   jax.ShapeDtypeStruct((MQL, MQW), F32), jax.ShapeDtypeStruct((MKVL, MKVW), F32),
                   jax.ShapeDtypeStruct((1, MQL), F32), jax.ShapeDtypeStruct((1, MKVL), F32)],
        name=name, compiler_params=_cp(("arbitrary",)))(proj, proj, cos, sin, qg, kvg, w_uq, w_ukv, d_q, d_k, d_v)


ATT_SCALE = (MNOPE + MROPE) ** -0.5
ATT_SCALE_LOG2 = ATT_SCALE * 1.4426950408889634
ATT_TQ_FWD, ATT_TQ = 256, 512


def _attn_fwd(q, k, v, name):
    s = q.shape[1]
    tq = min(ATT_TQ_FWD, s)

    def body(q_ref, k_ref, v_ref, o_ref, lse_ref):
        sc = _dot_nt(q_ref[0], k_ref[0])
        m = jnp.max(sc, axis=-1, keepdims=True)
        p = jnp.exp2((sc - m) * ATT_SCALE_LOG2)
        l = jnp.sum(p, axis=-1, keepdims=True)
        o_ref[...] = _dot(p, v_ref[0]) / l
        lse_ref[0] = m * ATT_SCALE_LOG2 + jnp.log2(l)

    return pl.pallas_call(
        body, grid=(MH, s // tq),
        in_specs=[pl.BlockSpec((1, tq, 2 * LANES), lambda h, i: (h, i, 0)),
                  pl.BlockSpec((1, s, 2 * LANES), lambda h, i: (h, 0, 0)),
                  pl.BlockSpec((1, s, LANES), lambda h, i: (h, 0, 0))],
        out_specs=[pl.BlockSpec((tq, LANES), lambda h, i: (i, h)),
                   pl.BlockSpec((1, tq, 1), lambda h, i: (h, i, 0))],
        out_shape=[jax.ShapeDtypeStruct((s, MW), F32), jax.ShapeDtypeStruct((MH, s, 1), F32)],
        name=name, compiler_params=_cp(("parallel", "parallel")))(q, k, v)


def _attn_bwd(q, k, v, o, lse, d_o, name):
    s = q.shape[1]
    tq = min(ATT_TQ, s)

    def body(q_ref, k_ref, v_ref, o_ref, lse_ref, do_ref, dq_ref, dk_ref, dv_ref):
        @pl.when(pl.program_id(1) == 0)
        def _():
            dk_ref[...] = jnp.zeros_like(dk_ref)
            dv_ref[...] = jnp.zeros_like(dv_ref)

        qv, kv, do = q_ref[0], k_ref[0], do_ref[...]
        p = jnp.exp2(_dot_nt(qv, kv) * ATT_SCALE_LOG2 - lse_ref[0])
        delta = jnp.sum(do * o_ref[...], axis=-1, keepdims=True)
        ds = p * (_dot_nt(do, v_ref[0]) - delta)
        dq_ref[0] = _dot(ds, kv) * ATT_SCALE
        dk_ref[0] += _dot_tn(ds, qv) * ATT_SCALE
        dv_ref[0] += _dot_tn(p, do)

    return pl.pallas_call(
        body, grid=(MH, s // tq),
        in_specs=[pl.BlockSpec((1, tq, 2 * LANES), lambda h, i: (h, i, 0)),
                  pl.BlockSpec((1, s, 2 * LANES), lambda h, i: (h, 0, 0)),
                  pl.BlockSpec((1, s, LANES), lambda h, i: (h, 0, 0)),
                  pl.BlockSpec((tq, LANES), lambda h, i: (i, h)),
                  pl.BlockSpec((1, tq, 1), lambda h, i: (h, i, 0)),
                  pl.BlockSpec((tq, LANES), lambda h, i: (i, h))],
        out_specs=[pl.BlockSpec((1, tq, 2 * LANES), lambda h, i: (h, i, 0)),
                   pl.BlockSpec((1, s, 2 * LANES), lambda h, i: (h, 0, 0)),
                   pl.BlockSpec((1, s, LANES), lambda h, i: (h, 0, 0))],
        out_shape=[jax.ShapeDtypeStruct((MH, s, 2 * LANES), F32), jax.ShapeDtypeStruct((MH, s, 2 * LANES), F32),
                   jax.ShapeDtypeStruct((MH, s, LANES), F32)],
        name=name, compiler_params=_cp(("parallel", "arbitrary")))(q, k, v, o, lse, d_o)


def _merge_fwd(o_f, o_b, o_att, pre, proj, gng, mog, cog, name):
    s = proj.shape[0]
    tr = _rows(s)

    def body(of_ref, ob_ref, oa_ref, pre_ref, z_ref, gng_ref, mog_ref, cog_ref, y_ref):
        z = z_ref[...]
        sz = z * _sigmoid(z)
        osum = of_ref[...] + ob_ref[...]
        gg = gng_ref[...]
        for h in range(GH):
            sl = slice(h * GDV, (h + 1) * GDV)
            y_ref[:, sl] = (_rms(osum[:, sl], gg) * sz[:, sl]).astype(y_ref.dtype)
        y_ref[:, GW:GW + MW] = (_rms(oa_ref[...], mog_ref[...]) * sz[:, GW:GW + MW]).astype(y_ref.dtype)
        y_ref[:, GW + MW:] = (_rms(pre_ref[...], cog_ref[...]) * sz[:, GW + MW:]).astype(y_ref.dtype)

    def row(w):
        return pl.BlockSpec((tr, w), lambda i: (i, 0))

    def vec(w):
        return pl.BlockSpec((1, w), lambda i: (0, 0))

    return pl.pallas_call(
        body, grid=(s // tr,),
        in_specs=[row(GW), row(GW), row(MW), row(CONV_CH), row(D_MIX), vec(GDV), vec(MW), vec(CONV_CH)],
        out_specs=row(D_MIX), out_shape=jax.ShapeDtypeStruct((s, D_MIX), MXU),
        name=name, compiler_params=_cp(("parallel",)))(o_f, o_b, o_att, pre, proj, gng, mog, cog)


def _merge_bwd(d_y, o_f, o_b, o_att, pre, proj, gng, mog, cog, name):
    s = proj.shape[0]
    tr = _rows(s)

    def body(dy_ref, of_ref, ob_ref, oa_ref, pre_ref, z_ref, gng_ref, mog_ref, cog_ref,
             dz_ref, dos_ref, doa_ref, dpre_ref, dgng_ref, dmog_ref, dcog_ref):
        @pl.when(pl.program_id(0) == 0)
        def _():
            for r in (dgng_ref, dmog_ref, dcog_ref):
                r[...] = jnp.zeros_like(r)

        z, dy = z_ref[...], dy_ref[...]
        sg = _sigmoid(z)
        sz = z * sg
        dsz = sg * (1.0 + z * (1.0 - sg))
        dcat = dy * sz
        dyz = dy * dsz
        osum = of_ref[...] + ob_ref[...]
        gg = gng_ref[...]
        dgg = jnp.zeros_like(gg)
        for h in range(GH):
            sl = slice(h * GDV, (h + 1) * GDV)
            dz_ref[:, sl] = dyz[:, sl] * _rms(osum[:, sl], gg)
            dx, dg = _rms_bwd(dcat[:, sl], osum[:, sl], gg)
            dos_ref[:, sl] = dx
            dgg += dg
        dgng_ref[...] += dgg
        sl = slice(GW, GW + MW)
        oa, mg = oa_ref[...], mog_ref[...]
        dz_ref[:, sl] = dyz[:, sl] * _rms(oa, mg)
        dx, dg = _rms_bwd(dcat[:, sl], oa, mg)
        doa_ref[...] = dx
        dmog_ref[...] += dg
        sl = slice(GW + MW, D_MIX)
        pv, cg = pre_ref[...], cog_ref[...]
        dz_ref[:, sl] = dyz[:, sl] * _rms(pv, cg)
        dx, dg = _rms_bwd(dcat[:, sl], pv, cg)
        dpre_ref[...] = dx
        dcog_ref[...] += dg

    def row(w):
        return pl.BlockSpec((tr, w), lambda i: (i, 0))

    def vec(w):
        return pl.BlockSpec((1, w), lambda i: (0, 0))

    def rs(w):
        return jax.ShapeDtypeStruct((s, w), F32)

    def vs(w):
        return jax.ShapeDtypeStruct((1, w), F32)

    return pl.pallas_call(
        body, grid=(s // tr,),
        in_specs=[row(D_MIX), row(GW), row(GW), row(MW), row(CONV_CH), row(D_MIX), vec(GDV), vec(MW), vec(CONV_CH)],
        out_specs=[row(D_MIX), row(GW), row(MW), row(CONV_CH), vec(GDV), vec(MW), vec(CONV_CH)],
        out_shape=[rs(D_MIX), rs(GW), rs(MW), rs(CONV_CH), vs(GDV), vs(MW), vs(CONV_CH)],
        name=name, compiler_params=_cp(("arbitrary",)))(d_y, o_f, o_b, o_att, pre, proj, gng, mog, cog)


def _assemble_dproj(d_z, d_cb, d_cc, d_cx, d_mkv, dv_f, dv_b, dq_f, dq_b, dk_f, dk_b, d_mq, dt_m, dt_f, dt_b, name):
    s = d_z.shape[0]
    tr = _rows(s)

    def body(dz, dcb, dcc, dcx, dmkv, dvf, dvb, dqf, dqb, dkf, dkb, dmq, dtm, dtf, dtb, out):
        dt = out.dtype
        out[:, OZ:OZ + D_MIX] = dz[...].astype(dt)
        out[:, OCB:OCB + CONV_CH] = dcb[...].astype(dt)
        out[:, OCC:OCC + CONV_CH] = dcc[...].astype(dt)
        out[:, OCX:OCX + CONV_CH] = dcx[...].astype(dt)
        out[:, OMKV:OMKV + MKVL] = dmkv[...].astype(dt)
        out[:, OGV:OGV + GW] = (dvf[...] + dvb[...]).astype(dt)
        out[:, OGQ:OGQ + GQK] = (dqf[...] + dqb[...]).astype(dt)
        out[:, OGK:OGK + GQK] = (dkf[...] + dkb[...]).astype(dt)
        out[:, OMQ:OMQ + MQL] = dmq[...].astype(dt)
        out[:, OT:OT + LANES] = (dtm[...] + dtf[...] + dtb[...]).astype(dt)

    args = (d_z, d_cb, d_cc, d_cx, d_mkv, dv_f, dv_b, dq_f, dq_b, dk_f, dk_b, d_mq, dt_m, dt_f, dt_b)
    return pl.pallas_call(
        body, grid=(s // tr,),
        in_specs=[pl.BlockSpec((tr, a.shape[1]), lambda i: (i, 0)) for a in args],
        out_specs=pl.BlockSpec((tr, PW), lambda i: (i, 0)),
        out_shape=jax.ShapeDtypeStruct((s, PW), MXU), name=name, compiler_params=_cp(("parallel",)))(*args)


def _layer_fwd(x, mod, wt, cos, sin, tag, late=None, in_after=()):
    shift, scale, gate = mod
    h = _norm_mod(x, wt["norm_g"], scale, shift, f"norm_mod_{tag}")
    (proj,) = _matmul(h, wt["w_in"], dims="nn", tm=2048, tn=256, tk=2048, out_dtypes=(F32,), name=f"in_proj_{tag}",
                      after=in_after)
    if late is not None:
        wt.update(late(proj))
    o_f, st_f = _gla_fwd(proj, wt["wg_pad_f"], wt["bg_f"], False, f"gla_fwd_f_{tag}")
    o_b, st_b = _gla_fwd(proj, wt["wg_pad_b"], wt["bg_b"], True, f"gla_fwd_b_{tag}")
    q, k, v = _mla_prep(proj, cos, sin, wt["q_norm_g"], wt["kv_norm_g"], wt["w_uq"], wt["w_ukv"], f"mla_prep_{tag}")
    o_att, lse = _attn_fwd(q, k, v, f"attn_fwd_{tag}")
    pre = _conv_fwd(proj, wt["conv_w"], f"conv_fwd_{tag}")
    y = _merge_fwd(o_f, o_b, o_att, pre, proj, wt["gla_norm_g"], wt["mla_out_g"], wt["conv_out_g"], f"merge_fwd_{tag}")
    x_new, u = _matmul(y, wt["w_out"], dims="nn", tm=2048, tn=256, tk=2048, out_dtypes=(F32, F32),
                       name=f"out_proj_{tag}", epilogue=lambda acc, xv, gv: (xv + gv * acc, acc),
                       extras=(x, gate), extra_kinds=("mn", "n"))
    saved = dict(x=x, h=h, proj=proj, o_f=o_f, o_b=o_b, st_f=st_f, st_b=st_b, q=q, k=k, v=v,
                 o_att=o_att, lse=lse, pre=pre, y=y, u=u)
    return x_new, saved


def _layer_bwd(d_out, sv, mod, wt, cos, sin, tag, ship=None, dx_first=None, ship_rest=None):
    shift, scale, gate = mod
    proj = sv["proj"]
    d_u, d_gate = _gate_bwd(d_out, sv["u"], gate, f"gate_bwd_{tag}")
    (g_w_out,) = _matmul(sv["y"], d_u, dims="tn", tm=1024, tn=512, tk=2048, out_dtypes=(MXU,), name=f"out_proj_dw_{tag}")
    (d_y,) = _matmul(d_u, wt["w_out"], dims="nt", tm=2048, tn=256, tk=2048, out_dtypes=(F32,), name=f"out_proj_dx_{tag}",
                     after=(g_w_out,))
    d_z, d_osum, d_oatt, d_pre, d_gng, d_mog, d_cog = _merge_bwd(
        d_y, sv["o_f"], sv["o_b"], sv["o_att"], sv["pre"], proj, wt["gla_norm_g"], wt["mla_out_g"], wt["conv_out_g"],
        f"merge_bwd_{tag}")
    d_cb, d_cc, d_cx, d_conv_w = _conv_bwd(proj, wt["conv_w"], d_pre, f"conv_bwd_{tag}")
    d_q, d_k, d_v = _attn_bwd(sv["q"], sv["k"], sv["v"], sv["o_att"], sv["lse"], d_oatt, f"attn_bwd_{tag}")
    d_mq, d_mkv, dt_m, g_w_uq, g_w_ukv, d_qg, d_kvg = _mla_prep_bwd(
        proj, cos, sin, wt["q_norm_g"], wt["kv_norm_g"], wt["w_uq"], wt["w_ukv"], d_q, d_k, d_v, f"mla_prep_bwd_{tag}")
    bg_f, bg_b = wt["bg_f"], wt["bg_b"]
    if ship_rest is not None:
        tok = ship_rest(dict(w_out=g_w_out, w_uq=g_w_uq, w_ukv=g_w_ukv))
        bg_f, bg_b = bg_f + tok, bg_b + tok
    dq_f, dk_f, dv_f, dt_f, d_wg_f, d_bg_f = _gla_bwd(proj, wt["wg_pad_f"], bg_f, sv["st_f"], d_osum, False,
                                                     f"gla_bwd_f_{tag}")
    dq_b, dk_b, dv_b, dt_b, d_wg_b, d_bg_b = _gla_bwd(proj, wt["wg_pad_b"], bg_b, sv["st_b"], d_osum, True,
                                                     f"gla_bwd_b_{tag}")
    d_proj = _assemble_dproj(d_z, d_cb, d_cc, d_cx, d_mkv, dv_f, dv_b, dq_f, dq_b, dk_f, dk_b, d_mq, dt_m, dt_f, dt_b,
                             f"assemble_dproj_{tag}")
    grads = dict(w_out=g_w_out, w_uq=g_w_uq, w_ukv=g_w_ukv,
                 wg_pad_f=d_wg_f, bg_f=d_bg_f, wg_pad_b=d_wg_b, bg_b=d_bg_b, gla_norm_g=d_gng,
                 q_norm_g=d_qg, kv_norm_g=d_kvg, mla_out_g=d_mog, conv_w=d_conv_w, conv_out_g=d_cog)

    def in_dw(after):
        (g_w_in,) = _matmul(sv["h"], d_proj, dims="tn", tm=2048, tn=256, tk=2048, out_dtypes=(MXU,),
                            name=f"in_proj_dw_{tag}", after=after)
        grads["w_in"] = g_w_in
        return dict(w_in=g_w_in, w_out=g_w_out, w_uq=g_w_uq, w_ukv=g_w_ukv)

    def in_dx(after):
        (d_h,) = _matmul(d_proj, wt["w_in"], dims="nt", tm=1024, tn=512, tk=PW, out_dtypes=(F32,),
                         name=f"in_proj_dx_{tag}", after=after)
        d_x, d_shift, d_scale, d_ng = _norm_mod_bwd(d_h, sv["x"], d_out, wt["norm_g"], scale, f"norm_mod_bwd_{tag}")
        grads["norm_g"] = d_ng
        return d_x, (d_shift, d_scale, d_gate)

    if dx_first is None:
        big = in_dw(())
        d_x, d_mod = in_dx((big["w_in"],) if ship is None else ship(big))
    else:
        d_x, d_mod = in_dx(())
        big = in_dw(dx_first(d_x, d_mod, grads))
        ship(big)
    return d_x, d_mod, grads


def _perm_in_cols(w):
    pad = jnp.zeros(w.shape[:-1] + (PW - IN_DIM,), w.dtype)
    return jnp.concatenate([w[..., 3808:5856], w[..., 2272:3808], w[..., 1952:2208], w[..., 768:1536], w[..., 0:768],
                            w[..., 1568:1952], w[..., 2208:2272], w[..., 1536:1568], pad], axis=-1)


def _unperm_in_cols(g):
    return jnp.concatenate([g[..., OGQ:OGQ + 2 * GQK], g[..., OGV:OGV + GW], g[..., OT + MROPE:OT + MROPE + 2 * GRANK],
                            g[..., OMQ:OMQ + MQL], g[..., OMKV:OMKV + MKVL], g[..., OT:OT + MROPE],
                            g[..., OCB:OCB + 3 * CONV_CH], g[..., OZ:OZ + D_MIX]], axis=-1)


IN_SEGS = ((3808, 5856), (2272, 3808), (1952, 2208), (768, 1536), (0, 768), (1568, 1952), (2208, 2272), (1536, 1568))
UQ_SEGS = (tuple((h * (MNOPE + MROPE), h * (MNOPE + MROPE) + MNOPE) for h in range(MH))
           + tuple((h * (MNOPE + MROPE) + MNOPE, (h + 1) * (MNOPE + MROPE)) for h in range(MH)))


def _perm_gathered(g, segs, width):
    per = g.shape[-1]
    parts, total = [], 0
    for a, b in segs:
        c = a
        while c < b:
            j = c // per
            hi = min(b, (j + 1) * per)
            parts.append(g[j, :, c - j * per:hi - j * per])
            c = hi
        total += b - a
    if width > total:
        parts.append(jnp.zeros((g.shape[1], width - total), g.dtype))
    return jnp.concatenate(parts, axis=1)


def _scatter_perm(gp, segs, per):
    offs, o = [], 0
    for a, b in segs:
        offs.append((a, b, o))
        o += b - a
    blocks = []
    for j in range(N_DEV):
        lo, hi = j * per, (j + 1) * per
        pieces = []
        for a, b, o in sorted(offs):
            s0, s1 = max(a, lo), min(b, hi)
            if s0 < s1:
                pieces.append(gp[:, o + s0 - a:o + s1 - a])
        blocks.append(jnp.concatenate(pieces, axis=1))
    return jnp.stack(blocks)


def _perm_uq_cols(w):
    w3 = w.reshape(w.shape[:-1] + (MH, MNOPE + MROPE))
    return jnp.concatenate([w3[..., :MNOPE].reshape(w.shape[:-1] + (MH * MNOPE,)),
                            w3[..., MNOPE:].reshape(w.shape[:-1] + (MH * MROPE,))], axis=-1)


def _unperm_uq_cols(g):
    nope = g[..., :MH * MNOPE].reshape(g.shape[:-1] + (MH, MNOPE))
    rope = g[..., MH * MNOPE:].reshape(g.shape[:-1] + (MH, MROPE))
    return jnp.concatenate([nope, rope], axis=-1).reshape(g.shape[:-1] + (MQW,))


def _prep_layer_weights(w_in, w_out, w_uq, w_ukv, small):
    def vec(v):
        return v.reshape(1, -1).astype(F32)

    zeros = functools.partial(jnp.zeros, dtype=F32)
    wg_f, wg_b = small["gla_wg_f"].astype(F32), small["gla_wg_b"].astype(F32)
    wg_pad_f = jnp.concatenate([zeros((MROPE, GQK)), wg_f, zeros((LANES - MROPE - GRANK, GQK))], axis=0)
    wg_pad_b = jnp.concatenate([zeros((MROPE + GRANK, GQK)), wg_b, zeros((LANES - MROPE - 2 * GRANK, GQK))], axis=0)
    wt = dict(norm_g=vec(small["norm_g"]), wg_pad_f=wg_pad_f, wg_pad_b=wg_pad_b,
              bg_f=vec(small["gla_bg_f"]), bg_b=vec(small["gla_bg_b"]), gla_norm_g=vec(small["gla_norm_g"]),
              q_norm_g=vec(small["mla_q_norm_g"]), kv_norm_g=vec(small["mla_kv_norm_g"]),
              mla_out_g=vec(small["mla_out_g"]), conv_w=small["conv_w"].astype(F32),
              conv_out_g=vec(small["conv_out_g"]))
    for name, w in (("w_in", w_in), ("w_out", w_out), ("w_uq", w_uq), ("w_ukv", w_ukv)):
        if w is not None:
            wt[name] = w.astype(MXU)
    return wt


def _natural_small(gr):
    return dict(norm_g=gr["norm_g"][0],
                gla_wg_f=gr["wg_pad_f"][MROPE:MROPE + GRANK], gla_bg_f=gr["bg_f"][0],
                gla_wg_b=gr["wg_pad_b"][MROPE + GRANK:MROPE + 2 * GRANK], gla_bg_b=gr["bg_b"][0],
                gla_norm_g=gr["gla_norm_g"][0], mla_q_norm_g=gr["q_norm_g"][0], mla_kv_norm_g=gr["kv_norm_g"][0],
                mla_out_g=gr["mla_out_g"][0], conv_w=gr["conv_w"], conv_out_g=gr["conv_out_g"][0])


def _natural_grads(gr):
    return dict(_natural_small(gr), w_in=_unperm_in_cols(gr["w_in"]), w_out=gr["w_out"],
                mla_w_uq=_unperm_uq_cols(gr["w_uq"]), mla_w_ukv=gr["w_ukv"])


def _exchange(arrs, name, scatter, space):
    n = len(arrs)

    def body(*refs):
        ins, outs = refs[:n], refs[n:2 * n]
        send_sems, recv_sems, loc_sems = refs[2 * n:]
        ax, ay, ac = lax.axis_index("x"), lax.axis_index("y"), lax.axis_index("c")
        me = 4 * ax + 2 * ay + ac

        def src(a, to):
            return ins[a].at[to] if scatter else ins[a]

        def remote(a, r, dst_slot):
            px = 1 - ax if r & 4 else ax
            py = 1 - ay if r & 2 else ay
            pc = 1 - ac if r & 1 else ac
            return pltpu.make_async_remote_copy(
                src_ref=src(a, 4 * px + 2 * py + pc), dst_ref=outs[a].at[dst_slot(4 * px + 2 * py + pc)],
                send_sem=send_sems.at[a, r - 1], recv_sem=recv_sems.at[a, r - 1],
                device_id=(px, py, pc), device_id_type=MESH)

        locs = [pltpu.make_async_copy(src(a, me), outs[a].at[me], loc_sems.at[a]) for a in range(n)]
        for cp in locs:
            cp.start()
        sends = [remote(a, r, lambda peer: me) for r in range(1, N_DEV) for a in range(n)]
        for cp in sends:
            cp.start()
        for r in range(1, N_DEV):
            for a in range(n):
                remote(a, r, lambda peer: peer).wait_recv()
        for cp in sends:
            cp.wait_send()
        for cp in locs:
            cp.wait()

    def out_shape(a):
        return jax.ShapeDtypeStruct(a.shape if scatter else (N_DEV,) + a.shape, a.dtype)

    spec = pl.BlockSpec(memory_space=space)
    return pl.pallas_call(
        body, in_specs=[spec] * n, out_specs=[spec] * n, out_shape=[out_shape(a) for a in arrs],
        scratch_shapes=[pltpu.SemaphoreType.DMA((n, N_DEV - 1)), pltpu.SemaphoreType.DMA((n, N_DEV - 1)),
                        pltpu.SemaphoreType.DMA((n,))],
        name=name, compiler_params=pltpu.CompilerParams(vmem_limit_bytes=VMEM_LIMIT))(*arrs)


def _peer(r):
    ax, ay, ac = lax.axis_index("x"), lax.axis_index("y"), lax.axis_index("c")
    px = 1 - ax if r & 4 else ax
    py = 1 - ay if r & 2 else ay
    pc = 1 - ac if r & 1 else ac
    return (px, py, pc), 4 * px + 2 * py + pc


def _slot(rel_div):
    rel, div = rel_div
    idx = _peer(rel)[1]
    return idx if div == 1 else idx // div


AG_SPREAD = tuple((r, None, (0, 1), (r, 1)) for r in (1, 2, 4, 6))
AG_FORWARD = tuple((1, (k, 1), (k, 1), (1 ^ k, 1)) for k in (2, 4, 6))
RS_PAIR = tuple((1, (1 ^ k, 1), (1 ^ k, 2), (k, 2)) for k in (0, 2, 4, 6))
RS_CHIPS = tuple((r, (r, 2), (0, 2), (r, 2)) for r in (2, 4, 6))


def _plan_copies(plan, n, src_refs, land_refs, send_sems, recv_sems, arriving):
    out = []
    for i, (r, src, dst, recv) in enumerate(plan):
        peer = _peer(r)[0]
        for a in range(n):
            out.append(pltpu.make_async_remote_copy(
                src_ref=src_refs[a] if src is None else src_refs[a].at[_slot(src)],
                dst_ref=land_refs[a].at[_slot(recv if arriving else dst)],
                send_sem=send_sems.at[i * n + a], recv_sem=recv_sems.at[i * n + a],
                device_id=peer, device_id_type=MESH))
    return out


def _exchange_hbm(plan, srcs, lands, name, after=()):
    n = len(lands)
    fresh = isinstance(lands[0], jax.ShapeDtypeStruct)
    ins = ([] if srcs is None else list(srcs)) + ([] if fresh else list(lands))
    ns = 0 if srcs is None else n
    n_data = len(ins)
    ins = ins + list(after)

    def body(*refs):
        outs = refs[len(ins):len(ins) + n]
        send_sems, recv_sems = refs[-2:]
        src_refs = refs[:n] if srcs is not None else refs[ns:ns + n]
        sends = _plan_copies(plan, n, src_refs, outs, send_sems, recv_sems, False)
        for cp in sends:
            cp.start()
        for cp in _plan_copies(plan, n, src_refs, outs, send_sems, recv_sems, True):
            cp.wait_recv()
        for cp in sends:
            cp.wait_send()

    hbm = pl.BlockSpec(memory_space=pltpu.HBM)
    k = len(plan) * n
    return pl.pallas_call(
        body, name=name, in_specs=[hbm] * n_data + [pl.BlockSpec(memory_space=pl.ANY)] * len(after), out_specs=[hbm] * n,
        out_shape=[jax.ShapeDtypeStruct(a.shape, a.dtype) for a in lands],
        scratch_shapes=[pltpu.SemaphoreType.DMA((k,)), pltpu.SemaphoreType.DMA((k,))],
        input_output_aliases={} if fresh else {ns + i: i for i in range(n)},
        compiler_params=pltpu.CompilerParams(vmem_limit_bytes=VMEM_LIMIT))(*ins)


def _plan_start(plan, srcs, land_shapes, after, name):
    n = len(srcs)

    def body(*refs):
        src_refs, land_refs = refs[:n], refs[n:2 * n]
        send_sems, recv_sems = refs[2 * n + 1], refs[2 * n + 2]
        for cp in _plan_copies(plan, n, src_refs, land_refs, send_sems, recv_sems, False):
            cp.start()
        refs[-1][...] = jnp.zeros_like(refs[-1])

    hbm = pl.BlockSpec(memory_space=pltpu.HBM)
    sem = pl.BlockSpec(memory_space=pltpu.SEMAPHORE)
    k = len(plan) * n
    srcs = [pltpu.with_memory_space_constraint(a, pltpu.HBM) for a in srcs]
    lands = [pltpu.with_memory_space_constraint(lax.empty(shp, a.dtype), pltpu.HBM) for shp, a in zip(land_shapes, srcs)]
    res = pl.pallas_call(
        body, name=name,
        in_specs=[hbm] * (2 * n) + [pl.BlockSpec(memory_space=pl.ANY)],
        out_specs=[sem, sem] + [hbm] * (2 * n) + [pl.BlockSpec(memory_space=pltpu.VMEM)],
        out_shape=[pltpu.SemaphoreType.DMA((k,)), pltpu.SemaphoreType.DMA((k,))]
        + [pltpu.HBM(a.shape, a.dtype) for a in srcs] + [pltpu.HBM(shp, a.dtype) for shp, a in zip(land_shapes, srcs)]
        + [jax.ShapeDtypeStruct((8, LANES), F32)],
        input_output_aliases={i: 2 + i for i in range(2 * n)},
        compiler_params=pltpu.CompilerParams(has_side_effects=pltpu.SideEffectType.DATAFLOW_SIDE_EFFECTING),
    )(*srcs, *lands, after)
    return res[0], res[1], list(res[2:2 + n]), list(res[2 + n:2 + 2 * n]), res[-1]


def _plan_wait(plan, handle, after, name):
    send_sems, recv_sems, srcs, lands, _ = handle
    n = len(srcs)
    after = list(after)

    def body(*refs):
        src_refs, land_refs = refs[:n], refs[n:2 * n]
        ssem, rsem = refs[2 * n], refs[2 * n + 1]
        for cp in _plan_copies(plan, n, src_refs, land_refs, ssem, rsem, False):
            cp.wait_send()
        for cp in _plan_copies(plan, n, src_refs, land_refs, ssem, rsem, True):
            cp.wait_recv()

    hbm = pl.BlockSpec(memory_space=pltpu.HBM)
    sem = pl.BlockSpec(memory_space=pltpu.SEMAPHORE)
    res = pl.pallas_call(
        body, name=name,
        in_specs=[hbm] * (2 * n) + [sem, sem] + [pl.BlockSpec(memory_space=pl.ANY)] * len(after),
        out_specs=[hbm] * (2 * n),
        out_shape=[pltpu.HBM(a.shape, a.dtype) for a in srcs] + [pltpu.HBM(a.shape, a.dtype) for a in lands],
        input_output_aliases={i: i for i in range(2 * n)},
        compiler_params=pltpu.CompilerParams(has_side_effects=pltpu.SideEffectType.DATAFLOW_SIDE_EFFECTING),
    )(*srcs, *lands, send_sems, recv_sems, *after)
    return list(res[:n]), list(res[n:])


def _pair_sum(send, got, core, name):
    _, r, c = send.shape
    tr = 256 if r % 256 == 0 else r

    def body(core_ref, s_ref, g_ref, o_ref):
        o_ref[0] = (s_ref[0].astype(F32) + g_ref[0].astype(F32)).astype(o_ref.dtype)

    return pl.pallas_call(
        body, name=name,
        grid_spec=pltpu.PrefetchScalarGridSpec(
            num_scalar_prefetch=1, grid=(N_DEV // 2, r // tr),
            in_specs=[pl.BlockSpec((1, tr, c), lambda kc, i, core_ref: (2 * kc + core_ref[0], i, 0)),
                      pl.BlockSpec((1, tr, c), lambda kc, i, core_ref: (kc, i, 0))],
            out_specs=pl.BlockSpec((1, tr, c), lambda kc, i, core_ref: (kc, i, 0))),
        out_shape=jax.ShapeDtypeStruct((N_DEV // 2, r, c), send.dtype),
        compiler_params=_cp(("parallel", "parallel")))(core, send, got)


def _ada_mod(c_all, ada_w, ada_b_cols, name):
    nl, d, wc = ada_w.shape

    def body(c_ref, w_ref, b_ref, ca_ref, mod_ref):
        cv = c_ref[...]
        ca = cv * _sigmoid(cv)
        ca_ref[...] = ca
        mod_ref[0] = _dotf(ca, w_ref[0]) + b_ref[0]

    return pl.pallas_call(
        body, grid=(nl,),
        in_specs=[pl.BlockSpec((N_DEV, d), lambda l: (0, 0)), pl.BlockSpec((1, d, wc), lambda l: (l, 0, 0)),
                  pl.BlockSpec((1, 1, wc), lambda l: (l, 0, 0))],
        out_specs=[pl.BlockSpec((N_DEV, d), lambda l: (0, 0)), pl.BlockSpec((1, N_DEV, wc), lambda l: (l, 0, 0))],
        out_shape=[jax.ShapeDtypeStruct((N_DEV, d), F32), jax.ShapeDtypeStruct((nl, N_DEV, wc), F32)],
        name=name, compiler_params=_cp(("arbitrary",)))(c_all, ada_w, ada_b_cols)


def _adam(w, g, m, v):
    m2 = ADAM_B1 * m + (1.0 - ADAM_B1) * g
    v2 = ADAM_B2 * v + (1.0 - ADAM_B2) * (g * g)
    m_hat = m2 / (1.0 - ADAM_B1 ** ADAM_STEP)
    v_hat = v2 / (1.0 - ADAM_B2 ** ADAM_STEP)
    delta = -ADAM_LR * (m_hat / (jnp.sqrt(v_hat) + ADAM_EPS) + ADAM_WD * w)
    return delta, m2, v2


def _ada_grad_adam(c_act, d_mod, w, m, v, name):
    nl, d, wc = w.shape
    tk = min(512, d)

    def body(c_ref, dm_ref, w_ref, m_ref, v_ref, g_ref, dl_ref, m2_ref, v2_ref):
        g = _dotf_tn(c_ref[...], dm_ref[0])
        delta, m2, v2 = _adam(w_ref[0], g, m_ref[0], v_ref[0])
        g_ref[0], dl_ref[0], m2_ref[0], v2_ref[0] = g, delta, m2, v2

    blk = pl.BlockSpec((1, tk, wc), lambda l, i: (l, i, 0))
    shp = jax.ShapeDtypeStruct(w.shape, F32)
    return pl.pallas_call(
        body, grid=(nl, d // tk),
        in_specs=[pl.BlockSpec((N_DEV, tk), lambda l, i: (0, i)), pl.BlockSpec((1, N_DEV, wc), lambda l, i: (l, 0, 0)),
                  blk, blk, blk],
        out_specs=[blk] * 4, out_shape=[shp] * 4, name=name,
        compiler_params=_cp(("parallel", "parallel")))(c_act, d_mod, w, m, v)


def _adam_big(recv, w, m, v, layer, prev, name, after=()):
    nl, r, c = w.shape
    tr = 256 if r % 256 == 0 else r
    nparts = recv.shape[0]

    def body(rc_ref, w_ref, m_ref, v_ref, *rest):
        g_ref, dl_ref, m2_ref, v2_ref = rest[-4:]
        g = rc_ref[0].astype(F32)
        for d in range(1, nparts):
            g = g + rc_ref[d].astype(F32)
        delta, m2, v2 = _adam(w_ref[0], g, m_ref[0], v_ref[0])
        g_ref[0], dl_ref[0], m2_ref[0], v2_ref[0] = g, delta, m2, v2

    blk = pl.BlockSpec((1, tr, c), lambda i: (layer, i, 0))
    shp = jax.ShapeDtypeStruct(w.shape, F32)
    prev = () if prev is None else tuple(prev)
    return pl.pallas_call(
        body, grid=(r // tr,),
        in_specs=[pl.BlockSpec((nparts, tr, c), lambda i: (0, i, 0)), blk, blk, blk]
        + [pl.BlockSpec(memory_space=pl.ANY)] * (len(prev) + len(after)),
        out_specs=[blk] * 4, out_shape=[shp] * 4, name=name,
        input_output_aliases={4 + j: j for j in range(len(prev))},
        compiler_params=_cp(("parallel",)))(recv, w, m, v, *prev, *after)


def _sum_devices(gathered, name):
    _, r, c = gathered.shape

    def body(g_ref, o_ref):
        acc = g_ref[0]
        for d in range(1, N_DEV):
            acc = acc + g_ref[d]
        o_ref[...] = acc

    spec = pl.BlockSpec(memory_space=pltpu.VMEM)
    return pl.pallas_call(body, in_specs=[spec], out_specs=spec, out_shape=jax.ShapeDtypeStruct((r, c), F32),
                          name=name, compiler_params=pltpu.CompilerParams(vmem_limit_bytes=VMEM_LIMIT))(gathered)


def _adam_small(ws, gs, ms, vs, name):
    n = len(ws)

    def body(*refs):
        for i in range(n):
            w_ref, g_ref, m_ref, v_ref = (refs[k * n + i] for k in range(4))
            dl_ref, m2_ref, v2_ref = (refs[(4 + k) * n + i] for k in range(3))
            dl_ref[...], m2_ref[...], v2_ref[...] = _adam(w_ref[...], g_ref[...], m_ref[...], v_ref[...])

    spec = pl.BlockSpec(memory_space=pltpu.VMEM)
    shapes = [jax.ShapeDtypeStruct(w.shape, F32) for w in ws]
    res = pl.pallas_call(body, in_specs=[spec] * (4 * n), out_specs=[spec] * (3 * n), out_shape=shapes * 3, name=name,
                         compiler_params=pltpu.CompilerParams(vmem_limit_bytes=VMEM_LIMIT))(*ws, *gs, *ms, *vs)
    return res[:n], res[n:2 * n], res[2 * n:]


def _pack(parts):
    flat = jnp.concatenate([p.reshape(-1).astype(F32) for p in parts])
    assert flat.shape[0] % LANES == 0, flat.shape
    return flat.reshape(-1, LANES)


def _unpack(packed, shapes):
    flat = packed.reshape(-1)
    out, off = [], 0
    for shp in shapes:
        size = 1
        for dim in shp:
            size *= dim
        out.append(flat[off:off + size].reshape(shp))
        off += size
    return out


def _gather_cols(g, per):
    g = jnp.moveaxis(g, 0, -2)
    return g.reshape(g.shape[:-2] + (N_DEV * per,))


def _scatter_cols(g, per):
    return jnp.moveaxis(g.reshape(g.shape[:-1] + (N_DEV, per)), -2, 0)


def _my_cols(full, me, per):
    return lax.dynamic_slice_in_dim(full, me * per, per, axis=full.ndim - 1)


def kernel(x, c, positions, ada_w, ada_b, norm_g, w_in, gla_wg_f, gla_bg_f, gla_wg_b, gla_bg_b, gla_norm_g, mla_q_norm_g, mla_kv_norm_g, mla_w_uq, mla_w_ukv, mla_out_g, conv_w, conv_out_g, w_out, final_g, loss_target, m_ada_w, m_ada_b, m_norm_g, m_w_in, m_gla_wg_f, m_gla_bg_f, m_gla_wg_b, m_gla_bg_b, m_gla_norm_g, m_mla_q_norm_g, m_mla_kv_norm_g, m_mla_w_uq, m_mla_w_ukv, m_mla_out_g, m_conv_w, m_conv_out_g, m_w_out, m_final_g, v_ada_w, v_ada_b, v_norm_g, v_w_in, v_gla_wg_f, v_gla_bg_f, v_gla_wg_b, v_gla_bg_b, v_gla_norm_g, v_mla_q_norm_g, v_mla_kv_norm_g, v_mla_w_uq, v_mla_w_ukv, v_mla_out_g, v_conv_w, v_conv_out_g, v_w_out, v_final_g):
    me = 4 * lax.axis_index("x") + 2 * lax.axis_index("y") + lax.axis_index("c")
    nl = ada_w.shape[0]
    s, d = x.shape[1], x.shape[2]
    ada_cols = ada_w.shape[2]
    wgc, cwc = gla_wg_f.shape[2], conv_w.shape[2]

    (g0,) = _exchange([_pack([c, gla_wg_f, gla_wg_b, conv_w])], "gather_small_in", False, pltpu.VMEM)
    g0 = g0.reshape(N_DEV, -1)
    o1, o2, o3 = d, d + gla_wg_f.size, d + 2 * gla_wg_f.size
    c_all = g0[:, :o1]
    wgf_full = _gather_cols(g0[:, o1:o2].reshape((N_DEV,) + gla_wg_f.shape), wgc)
    wgb_full = _gather_cols(g0[:, o2:o3].reshape((N_DEV,) + gla_wg_b.shape), wgc)
    convw_full = _gather_cols(g0[:, o3:].reshape((N_DEV,) + conv_w.shape), cwc)

    ada_b_cols = _my_cols(ada_b, me, ada_cols).reshape(nl, 1, ada_cols)
    c_act, mod_cols = _ada_mod(c_all, ada_w, ada_b_cols, "ada_mod")
    (g1,) = _exchange([_pack([mod_cols])], "gather_mod", False, pltpu.VMEM)
    mod_all = g1.reshape(N_DEV, nl, N_DEV, ada_cols)
    mod_mine = _gather_cols(lax.dynamic_index_in_dim(mod_all, me, axis=2, keepdims=False), ada_cols)

    inv_freq = ROPE_THETA ** (-jnp.arange(0, MROPE, 2, dtype=F32) / MROPE)
    ang = positions[0].astype(F32)[:, None] * inv_freq
    cos, sin = jnp.tile(jnp.cos(ang), (1, LANES * 2 // MROPE)), jnp.tile(jnp.sin(ang), (1, LANES * 2 // MROPE))

    big = [w_in, w_out, mla_w_uq, mla_w_ukv]
    big_names = ["w_in", "w_out", "mla_w_uq", "mla_w_ukv"]

    def local_blocks(l):
        return [w[l].astype(MXU) for w in big]

    def put_own(lands, own):
        return [lax.dynamic_update_index_in_dim(ld, o, me, 0) for ld, o in zip(lands, own)]

    def layer_weights(l, gw_in=None, gw_out=None, gw_uq=None, gw_ukv=None):
        small = dict(norm_g=norm_g[l], gla_wg_f=wgf_full[l], gla_bg_f=gla_bg_f[l], gla_wg_b=wgb_full[l],
                     gla_bg_b=gla_bg_b[l], gla_norm_g=gla_norm_g[l], mla_q_norm_g=mla_q_norm_g[l],
                     mla_kv_norm_g=mla_kv_norm_g[l], mla_out_g=mla_out_g[l], conv_w=convw_full[l],
                     conv_out_g=conv_out_g[l])
        return _prep_layer_weights(
            None if gw_in is None else _perm_gathered(gw_in, IN_SEGS, PW),
            None if gw_out is None else gw_out.reshape((-1,) + gw_out.shape[2:]),
            None if gw_uq is None else _perm_gathered(gw_uq, UQ_SEGS, MQW),
            None if gw_ukv is None else _gather_cols(gw_ukv, mla_w_ukv.shape[2]), small)

    def land_shapes(blocks, slots):
        return [jax.ShapeDtypeStruct((slots,) + b.shape, b.dtype) for b in blocks]

    def slots_of(blocks):
        return [(N_DEV,) + b.shape for b in blocks]

    def forwarded(lands, blocks, tag):
        return put_own(_exchange_hbm(AG_FORWARD, None, lands, f"gather_{tag}_forward"), blocks)

    first = local_blocks(0)
    w_in_start = _plan_start(AG_SPREAD, first[:1], slots_of(first[:1]), mod_mine, "gather_w_in_l0_start")
    adam_w_in = [a + w_in_start[-1][0, 0] for a in (w_in, m_w_in, v_w_in)]
    (gw_in,) = forwarded(*reversed(_plan_wait(AG_SPREAD, w_in_start, adam_w_in, "gather_w_in_l0_wait")), "w_in_l0")
    rest = _plan_start(AG_SPREAD, first[1:], slots_of(first[1:]), gw_in, "gather_rest_l0_start")
    h = x[0]
    saved, layers, mods = [], [], []
    pending = {}
    for l in range(nl):
        shift, scale, gate = (mod_mine[l, i * d:(i + 1) * d].reshape(1, d) for i in range(3))
        nxt = local_blocks(l + 1) if l + 1 < nl else None

        def start_next(after, wt_late, l=l, nxt=nxt):
            if nxt is not None:
                pending[l + 1] = _plan_start(AG_SPREAD, nxt, slots_of(nxt), after, f"gather_weights_l{l + 1}_start")
                wt_late["q_norm_g"] = layers[l]["q_norm_g"] + pending[l + 1][-1][0, 0]
            return wt_late

        if l == 0:
            in_after = (rest[-1],)
            layers.append(layer_weights(0, gw_in))

            def late(proj):
                got = forwarded(*reversed(_plan_wait(AG_SPREAD, rest, [proj], "gather_rest_l0_wait")), "rest_l0")
                full = layer_weights(0, None, *got)
                return start_next(got[0], {k: full[k] for k in ("w_out", "w_uq", "w_ukv")})
        else:
            got = forwarded(*reversed(_plan_wait(AG_SPREAD, pending.pop(l), [h], f"gather_weights_l{l}_wait")), f"weights_l{l}")
            layers.append(layer_weights(l, *got))
            in_after = ()

            def late(proj):
                return start_next(proj, {})
        mods.append((shift, scale, gate))
        h, sv = _layer_fwd(h, mods[l], layers[l], cos, sin, f"l{l}", late, in_after)
        saved.append(sv)
        blocks = nxt
    loss_part, d_h, d_final_g = _final_loss(h, final_g.reshape(1, d), loss_target[0], "final_loss")
    loss = lax.psum(loss_part[0, 0], ("x", "y", "c"))
    shift, scale, gate = mods[-1]
    mods[-1] = (shift, scale, gate + 0.0 * loss)

    send_of = dict(w_in=lambda g: _scatter_perm(g, IN_SEGS, w_in.shape[2]),
                   w_out=lambda g: g.reshape((N_DEV,) + w_out.shape[1:]),
                   w_uq=lambda g: _scatter_perm(g, UQ_SEGS, mla_w_uq.shape[2]),
                   w_ukv=lambda g: _scatter_cols(g, mla_w_ukv.shape[2]))

    def grad_sends(gr):
        return [send_of[k](g).astype(MXU) for k, g in gr.items()]

    my_chip = me // 2
    my_core = (me % 2).astype(jnp.int32).reshape(1)

    def chip_sums(gr, tag):
        sends = grad_sends(gr)
        got = _exchange_hbm(RS_PAIR, sends, land_shapes([sd[0] for sd in sends], N_DEV // 2), f"scatter_grads_{tag}_pair")
        return [_pair_sum(sd, gt, my_core, f"pair_sum_{k}_{tag}") for sd, gt, k in zip(sends, got, gr)]

    def with_own_chip(lands, sums):
        return [lax.dynamic_update_index_in_dim(ld, lax.dynamic_index_in_dim(sm, my_chip, axis=0, keepdims=False),
                                                my_chip, 0) for ld, sm in zip(lands, sums)]

    small_names = ["norm_g", "gla_wg_f", "gla_bg_f", "gla_wg_b", "gla_bg_b", "gla_norm_g", "mla_q_norm_g",
                   "mla_kv_norm_g", "mla_out_g", "conv_w", "conv_out_g"]
    d_mods, grads, recv = [None] * nl, [None] * nl, [None] * nl
    flight = {}
    small = {}

    def gather_small(d_x, d_mod0, gr0):
        d_mods[0], grads[0] = d_mod0, _natural_small(gr0)
        d_mod_mine = jnp.stack([jnp.concatenate(d_mods[l], axis=-1)[0] for l in range(nl)])
        parts = [d_mod_mine] + [jnp.stack([grads[l][n] for l in range(nl)]) for n in small_names] + [d_final_g]
        (g2,) = _exchange([_pack(parts)], "gather_small_grads", False, pltpu.VMEM)
        small["d_mod_all"] = g2.reshape(N_DEV, -1)[:, :d_mod_mine.size].reshape(N_DEV, nl, 3 * d)
        small["summed"] = dict(zip(["ada_b"] + small_names + ["final_g"],
                                   _unpack(_sum_devices(g2, "sum_small_grads"), [p.shape for p in parts])))
        return (g2,)

    pairs = {}
    def end_flight(key, after, name):
        sm, lands = _plan_wait(RS_CHIPS, flight.pop(key)[0], after, name)
        return with_own_chip(lands, sm)

    for l in reversed(range(nl)):
        def ship(big_grads, l=l):
            if l > 0:
                sends = grad_sends(big_grads)
                pairs[l] = (_plan_start(RS_PAIR, sends, [(N_DEV // 2,) + sd.shape[1:] for sd in sends],
                                        big_grads["w_in"], f"scatter_grads_l{l}_pair_start"), sends)
                return (pairs[l][0][-1],)
            sm = chip_sums(dict(w_in=big_grads["w_in"]), f"l{l}")
            flight[l] = (_plan_start(RS_CHIPS, sm, [a.shape for a in sm], sm[0], f"scatter_grads_l{l}_start"), sm)
            return (flight[l][0][-1],)

        def ship_rest(rest_grads, l=l):
            if l + 1 in flight:
                recv[l + 1] = end_flight(l + 1, list(rest_grads.values()), f"scatter_grads_l{l + 1}_wait")
            sm = chip_sums(rest_grads, f"l{l}_rest")
            flight["rest"] = (_plan_start(RS_CHIPS, sm, [a.shape for a in sm], sm[0], f"scatter_grads_l{l}_rest_start"), sm)
            return flight["rest"][0][-1][0, 0]

        shift, scale, gate = mods[l]
        if l + 1 in flight:
            gate = gate + flight[l + 1][0][-1][0, 0]
        if l > 0:
            d_h, d_mods[l], gr = _layer_bwd(d_h, saved[l], (shift, scale, gate), layers[l], cos, sin, f"l{l}", ship)
            grads[l] = _natural_small(gr)
            sends, got = _plan_wait(RS_PAIR, pairs.pop(l)[0], [d_h], f"scatter_grads_l{l}_pair_wait")
            sm = [_pair_sum(sd, gt, my_core, f"pair_sum_{n}_l{l}") for sd, gt, n in zip(sends, got, big_names)]
            flight[l] = (_plan_start(RS_CHIPS, sm, [a.shape for a in sm], sm[0], f"scatter_grads_l{l}_start"), sm)
        else:
            d_h, _, _ = _layer_bwd(d_h, saved[l], (shift, scale, gate), layers[l], cos, sin, f"l{l}", ship, gather_small,
                                   ship_rest)
    pending = flight[0][0]
    grad_x = d_h[None]
    summed = small["summed"]
    summed["gla_wg_f"] = _my_cols(summed["gla_wg_f"], me, wgc)
    summed["gla_wg_b"] = _my_cols(summed["gla_wg_b"], me, wgc)
    summed["conv_w"] = _my_cols(summed["conv_w"], me, cwc)

    d_mod_cols = jnp.moveaxis(_my_cols(small["d_mod_all"], me, ada_cols), 0, 1) + pending[-1][0, 0]
    out = {}
    out["ada_w"] = _ada_grad_adam(c_act, d_mod_cols, ada_w, m_ada_w, v_ada_w, "ada_grad_adam")

    given = dict(ada_b=(ada_b, m_ada_b, v_ada_b), norm_g=(norm_g, m_norm_g, v_norm_g),
                 gla_wg_f=(gla_wg_f, m_gla_wg_f, v_gla_wg_f), gla_bg_f=(gla_bg_f, m_gla_bg_f, v_gla_bg_f),
                 gla_wg_b=(gla_wg_b, m_gla_wg_b, v_gla_wg_b), gla_bg_b=(gla_bg_b, m_gla_bg_b, v_gla_bg_b),
                 gla_norm_g=(gla_norm_g, m_gla_norm_g, v_gla_norm_g),
                 mla_q_norm_g=(mla_q_norm_g, m_mla_q_norm_g, v_mla_q_norm_g),
                 mla_kv_norm_g=(mla_kv_norm_g, m_mla_kv_norm_g, v_mla_kv_norm_g),
                 mla_out_g=(mla_out_g, m_mla_out_g, v_mla_out_g), conv_w=(conv_w, m_conv_w, v_conv_w),
                 conv_out_g=(conv_out_g, m_conv_out_g, v_conv_out_g), final_g=(final_g, m_final_g, v_final_g))
    names = list(given)

    def two_d(a):
        return a.reshape(1, -1) if a.ndim == 1 else a

    g_nat = [summed[n].reshape(given[n][0].shape) for n in names]
    res = _adam_small([two_d(given[n][0]) for n in names], [two_d(g) for g in g_nat],
                      [two_d(given[n][1]) for n in names], [two_d(given[n][2]) for n in names], "adam_small")
    for i, n in enumerate(names):
        out[n] = (g_nat[i],) + tuple(r[i].reshape(given[n][0].shape) for r in res)

    state = dict(w_in=adam_w_in, w_out=(w_out, m_w_out, v_w_out), mla_w_uq=(mla_w_uq, m_mla_w_uq, v_mla_w_uq),
                 mla_w_ukv=(mla_w_ukv, m_mla_w_ukv, v_mla_w_ukv))
    done = [out["ada_w"][0], res[0][0]]
    for l in reversed(range(nl)):
        if l == 0:
            rest = end_flight("rest", done, "scatter_grads_l0_rest_wait")
            recv[0] = end_flight(0, done + rest[:1], "scatter_grads_l0_wait") + rest
        for i, n in enumerate(big_names):
            out[n] = _adam_big(recv[l][i], *state[n], l, out.get(n), f"adam_{n}_l{l}",
                               (pending[-1],))
        done = done + [out[n][0] for n in big_names]

    order = ["ada_w", "ada_b", "norm_g", "w_in", "gla_wg_f", "gla_bg_f", "gla_wg_b", "gla_bg_b", "gla_norm_g",
             "mla_q_norm_g", "mla_kv_norm_g", "mla_w_uq", "mla_w_ukv", "mla_out_g", "conv_w", "conv_out_g", "w_out",
             "final_g"]
    return (loss, grad_x, *[out[n][0] for n in order], *[out[n][1] for n in order], *[out[n][2] for n in order],
            *[out[n][3] for n in order])
```

```python
import functools

import jax
import jax.numpy as jnp
from jax import lax
from jax.experimental import pallas as pl
from jax.experimental.pallas import tpu as pltpu

F32 = jnp.float32
MXU = jnp.bfloat16
HI = lax.Precision.HIGHEST
N_DEV = 8
MESH = pl.DeviceIdType.MESH

D_MIX = 2048
GH, GDK, GDV = 6, 64, 128
GW = GH * GDV
GQK = GH * GDK
GRANK = 16
GTEMP = 16.0
CHUNK = 64
MH, MQL, MKVL, MNOPE, MROPE, MDV = 6, 384, 256, 128, 64, 128
MW = MH * MDV
MQW = MH * (MNOPE + MROPE)
MKVW = MH * (MNOPE + MDV)
CONV_CH = 512
ROPE_THETA = 10000.0
EPS = 1e-6
IN_DIM = 5856
OZ, OCB, OCC, OCX, OMKV, OGV, OGQ, OGK, OMQ, OT = 0, 2048, 2560, 3072, 3584, 3840, 4608, 4992, 5376, 5760
PW = 5888
LANES = 128
VMEM_LIMIT = 56 * 1024 * 1024

ADAM_LR, ADAM_B1, ADAM_B2, ADAM_EPS, ADAM_WD, ADAM_STEP = 0.001, 0.9, 0.999, 1e-08, 0.01, 10


def _cp(sem=None):
    return pltpu.CompilerParams(dimension_semantics=sem, vmem_limit_bytes=VMEM_LIMIT)


def _dot(a, b):
    return jnp.dot(a.astype(MXU), b.astype(MXU), preferred_element_type=F32)


def _dot_nt(a, b):
    return lax.dot_general(a.astype(MXU), b.astype(MXU), (((1,), (1,)), ((), ())), preferred_element_type=F32)


def _dot_tn(a, b):
    return lax.dot_general(a.astype(MXU), b.astype(MXU), (((0,), (0,)), ((), ())), preferred_element_type=F32)


def _dotf(a, b):
    return jnp.dot(a, b, precision=HI, preferred_element_type=F32)


def _dotf_nt(a, b):
    return lax.dot_general(a, b, (((1,), (1,)), ((), ())), precision=HI, preferred_element_type=F32)


def _dotf_tn(a, b):
    return lax.dot_general(a, b, (((0,), (0,)), ((), ())), precision=HI, preferred_element_type=F32)


def _split3(x):
    hi = x.astype(jnp.bfloat16)
    r1 = x - hi.astype(F32)
    mid = r1.astype(jnp.bfloat16)
    lo = (r1 - mid.astype(F32)).astype(jnp.bfloat16)
    return hi, mid, lo


def _cum_dot(cum, x, transpose=False):
    dn = (((0,), (0,)), ((), ())) if transpose else (((1,), (0,)), ((), ()))
    cb = cum.astype(jnp.bfloat16)
    parts = [lax.dot_general(cb, p, dn, preferred_element_type=F32) for p in _split3(x)]
    return parts[0] + parts[1] + parts[2]


def _rows(s):
    return min(256, s)


def _rms(x, g):
    r = lax.rsqrt(jnp.mean(x * x, axis=-1, keepdims=True) + EPS)
    return x * r * g


def _rms_bwd(dy, x, g):
    r = lax.rsqrt(jnp.mean(x * x, axis=-1, keepdims=True) + EPS)
    xh = x * r
    dxh = dy * g
    dg = jnp.sum(dy * xh, axis=0, keepdims=True)
    dx = r * (dxh - xh * jnp.mean(dxh * xh, axis=-1, keepdims=True))
    return dx, dg


def _sigmoid(z):
    return jax.nn.sigmoid(z)


def _matmul(a, b, *, dims, tm, tn, tk, out_dtypes, name, epilogue=None, extras=(), extra_kinds=(), after=()):
    if dims == "nn":
        (m, k), n, mul = a.shape, b.shape[1], _dot
    elif dims == "nt":
        (m, k), n, mul = a.shape, b.shape[0], _dot_nt
    else:
        (k, m), n, mul = a.shape, b.shape[1], _dot_tn
    tm, tn, tk = min(tm, m), min(tn, n), min(tk, k)
    assert m % tm == 0 and n % tn == 0 and k % tk == 0, (m, n, k, tm, tn, tk)
    if dims == "nn":
        a_spec = pl.BlockSpec((tm, tk), lambda i, j, kk: (i, kk))
        b_spec = pl.BlockSpec((tk, tn), lambda i, j, kk: (kk, j))
    elif dims == "nt":
        a_spec = pl.BlockSpec((tm, tk), lambda i, j, kk: (i, kk))
        b_spec = pl.BlockSpec((tn, tk), lambda i, j, kk: (j, kk))
    else:
        a_spec = pl.BlockSpec((tk, tm), lambda i, j, kk: (kk, i))
        b_spec = pl.BlockSpec((tk, tn), lambda i, j, kk: (kk, j))
    nk = k // tk
    n_extra = len(extras)
    n_out = len(out_dtypes)
    n_after = len(after)
    extra_specs = []
    for kind in extra_kinds:
        if kind == "mn":
            extra_specs.append(pl.BlockSpec((tm, tn), lambda i, j, kk: (i, j)))
        else:
            extra_specs.append(pl.BlockSpec((1, tn), lambda i, j, kk: (0, j)))

    def finish(res, ex, outs):
        vals = (res,) if epilogue is None else epilogue(res, *[e[...] for e in ex])
        for o, v in zip(outs, vals):
            o[...] = v.astype(o.dtype)

    def body(*refs):
        a_ref, b_ref = refs[0], refs[1]
        ex = refs[2:2 + n_extra]
        outs = refs[2 + n_extra + n_after:2 + n_extra + n_after + n_out]
        if nk == 1:
            finish(mul(a_ref[...], b_ref[...]), ex, outs)
            return
        acc = refs[-1]
        kk = pl.program_id(2)

        @pl.when(kk == 0)
        def _():
            acc[...] = jnp.zeros_like(acc)

        acc[...] += mul(a_ref[...], b_ref[...])

        @pl.when(kk == nk - 1)
        def _():
            finish(acc[...], ex, outs)

    out_spec = pl.BlockSpec((tm, tn), lambda i, j, kk: (i, j))
    res = pl.pallas_call(
        body, grid=(m // tm, n // tn, nk),
        in_specs=[a_spec, b_spec] + extra_specs + [pl.BlockSpec(memory_space=pl.ANY)] * n_after,
        out_specs=[out_spec] * n_out,
        out_shape=[jax.ShapeDtypeStruct((m, n), dt) for dt in out_dtypes],
        scratch_shapes=[] if nk == 1 else [pltpu.VMEM((tm, tn), F32)],
        name=name, compiler_params=_cp(("parallel", "parallel", "arbitrary")),
    )(a, b, *extras, *after)
    return res


def _norm_mod(x, g, scale, shift, name):
    s, d = x.shape
    tr = _rows(s)

    def body(x_ref, g_ref, sc_ref, sh_ref, h_ref):
        h = _rms(x_ref[...], g_ref[...]) * (1.0 + sc_ref[...]) + sh_ref[...]
        h_ref[...] = h.astype(h_ref.dtype)

    row = pl.BlockSpec((tr, d), lambda i: (i, 0))
    vec = pl.BlockSpec((1, d), lambda i: (0, 0))
    return pl.pallas_call(body, grid=(s // tr,), in_specs=[row, vec, vec, vec], out_specs=row,
                          out_shape=jax.ShapeDtypeStruct((s, d), MXU), name=name,
                          compiler_params=_cp(("parallel",)))(x, g, scale, shift)


def _norm_mod_bwd(d_h, x, d_out, g, scale, name):
    s, d = x.shape
    tr = _rows(s)

    def body(dh_ref, x_ref, do_ref, g_ref, sc_ref, dx_ref, dsh_ref, dsc_ref, dg_ref):
        i = pl.program_id(0)

        @pl.when(i == 0)
        def _():
            dsh_ref[...] = jnp.zeros_like(dsh_ref)
            dsc_ref[...] = jnp.zeros_like(dsc_ref)
            dg_ref[...] = jnp.zeros_like(dg_ref)

        dh = dh_ref[...]
        xv = x_ref[...]
        gv = g_ref[...]
        r = lax.rsqrt(jnp.mean(xv * xv, axis=-1, keepdims=True) + EPS)
        xh = xv * r
        dsh_ref[...] += jnp.sum(dh, axis=0, keepdims=True)
        dsc_ref[...] += jnp.sum(dh * (xh * gv), axis=0, keepdims=True)
        dhn = dh * (1.0 + sc_ref[...])
        dg_ref[...] += jnp.sum(dhn * xh, axis=0, keepdims=True)
        dxh = dhn * gv
        dx_ref[...] = do_ref[...] + r * (dxh - xh * jnp.mean(dxh * xh, axis=-1, keepdims=True))

    row = pl.BlockSpec((tr, d), lambda i: (i, 0))
    vec = pl.BlockSpec((1, d), lambda i: (0, 0))
    vshape = jax.ShapeDtypeStruct((1, d), F32)
    return pl.pallas_call(body, grid=(s // tr,), in_specs=[row, row, row, vec, vec],
                          out_specs=[row, vec, vec, vec],
                          out_shape=[jax.ShapeDtypeStruct((s, d), F32), vshape, vshape, vshape],
                          name=name, compiler_params=_cp(("arbitrary",)))(d_h, x, d_out, g, scale)


def _gate_bwd(d_out, u, gate, name):
    s, d = d_out.shape
    tr = _rows(s)

    def body(do_ref, u_ref, gt_ref, du_ref, dgt_ref):
        @pl.when(pl.program_id(0) == 0)
        def _():
            dgt_ref[...] = jnp.zeros_like(dgt_ref)

        do = do_ref[...]
        du_ref[...] = (do * gt_ref[...]).astype(du_ref.dtype)
        dgt_ref[...] += jnp.sum(do * u_ref[...], axis=0, keepdims=True)

    row = pl.BlockSpec((tr, d), lambda i: (i, 0))
    vec = pl.BlockSpec((1, d), lambda i: (0, 0))
    return pl.pallas_call(body, grid=(s // tr,), in_specs=[row, row, vec], out_specs=[row, vec],
                          out_shape=[jax.ShapeDtypeStruct((s, d), MXU), jax.ShapeDtypeStruct((1, d), F32)],
                          name=name, compiler_params=_cp(("arbitrary",)))(d_out, u, gate)


def _final_loss(x, g, target, name):
    s, d = x.shape
    tr = _rows(s)

    def body(x_ref, g_ref, t_ref, loss_ref, dx_ref, dg_ref):
        @pl.when(pl.program_id(0) == 0)
        def _():
            loss_ref[...] = jnp.zeros_like(loss_ref)
            dg_ref[...] = jnp.zeros_like(dg_ref)

        xv = x_ref[...]
        gv = g_ref[...]
        diff = _rms(xv, gv) - t_ref[...]
        part = 0.5 * jnp.sum(jnp.sum(diff * diff, axis=-1, keepdims=True) / d, axis=0, keepdims=True)
        loss_ref[...] += jnp.broadcast_to(part, loss_ref.shape)
        dx, dg = _rms_bwd(diff / d, xv, gv)
        dx_ref[...] = dx
        dg_ref[...] += dg

    row = pl.BlockSpec((tr, d), lambda i: (i, 0))
    vec = pl.BlockSpec((1, d), lambda i: (0, 0))
    lvec = pl.BlockSpec((1, LANES), lambda i: (0, 0))
    return pl.pallas_call(body, grid=(s // tr,), in_specs=[row, vec, row], out_specs=[lvec, row, vec],
                          out_shape=[jax.ShapeDtypeStruct((1, LANES), F32), jax.ShapeDtypeStruct((s, d), F32),
                                     jax.ShapeDtypeStruct((1, d), F32)],
                          name=name, compiler_params=_cp(("arbitrary",)))(x, g, target)


def _shift_rows(u, s, down):
    ri = lax.broadcasted_iota(jnp.int32, u.shape, 0)
    if down:
        return jnp.where(ri == 0, 0.0, pltpu.roll(u, 1, 0))
    return jnp.where(ri == s - 1, 0.0, pltpu.roll(u, s - 1, 0))


def _conv_fwd(proj, conv_w, name):
    s = proj.shape[0]
    nt = CONV_CH // LANES

    def body(cb_ref, cc_ref, cx_ref, w_ref, pre_ref):
        u = cc_ref[...] * cx_ref[...]
        conv = _shift_rows(u, s, True) * w_ref[0:1, :] + u * w_ref[1:2, :] + _shift_rows(u, s, False) * w_ref[2:3, :]
        pre_ref[...] = cb_ref[...] * conv

    def col(off):
        return pl.BlockSpec((s, LANES), lambda j: (0, off // LANES + j))

    return pl.pallas_call(body, grid=(nt,), in_specs=[col(OCB), col(OCC), col(OCX), pl.BlockSpec((3, LANES), lambda j: (0, j))],
                          out_specs=pl.BlockSpec((s, LANES), lambda j: (0, j)),
                          out_shape=jax.ShapeDtypeStruct((s, CONV_CH), F32), name=name,
                          compiler_params=_cp(("parallel",)))(proj, proj, proj, conv_w)


def _conv_bwd(proj, conv_w, d_pre, name):
    s = proj.shape[0]
    nt = CONV_CH // LANES

    def body(cb_ref, cc_ref, cx_ref, w_ref, dp_ref, dcb_ref, dcc_ref, dcx_ref, dw_ref):
        cc, cx = cc_ref[...], cx_ref[...]
        u = cc * cx
        up, dn = _shift_rows(u, s, True), _shift_rows(u, s, False)
        w0, w1, w2 = w_ref[0:1, :], w_ref[1:2, :], w_ref[2:3, :]
        conv = up * w0 + u * w1 + dn * w2
        dp = dp_ref[...]
        dcb_ref[...] = dp * conv
        dconv = dp * cb_ref[...]
        du = _shift_rows(dconv, s, False) * w0 + dconv * w1 + _shift_rows(dconv, s, True) * w2
        dcc_ref[...] = du * cx
        dcx_ref[...] = du * cc
        dw_ref[0:1, :] = jnp.sum(dconv * up, axis=0, keepdims=True)
        dw_ref[1:2, :] = jnp.sum(dconv * u, axis=0, keepdims=True)
        dw_ref[2:3, :] = jnp.sum(dconv * dn, axis=0, keepdims=True)

    def col(off):
        return pl.BlockSpec((s, LANES), lambda j: (0, off // LANES + j))

    blk = pl.BlockSpec((s, LANES), lambda j: (0, j))
    wblk = pl.BlockSpec((3, LANES), lambda j: (0, j))
    full = jax.ShapeDtypeStruct((s, CONV_CH), F32)
    return pl.pallas_call(body, grid=(nt,), in_specs=[col(OCB), col(OCC), col(OCX), wblk, blk],
                          out_specs=[blk, blk, blk, wblk],
                          out_shape=[full, full, full, jax.ShapeDtypeStruct((3, CONV_CH), F32)],
                          name=name, compiler_params=_cp(("parallel",)))(proj, proj, proj, conv_w, d_pre)


GLA_SUB = 8


def _gla_gates(t_ref, wg_ref, bg_ref):
    t = t_ref[...]
    a = _dot(t, wg_ref[...]) + bg_ref[...]
    la = (jnp.minimum(a, 0.0) - jnp.log(1.0 + jnp.exp(-jnp.abs(a)))) / GTEMP
    return t, a, la


def _gla_masks(reverse):
    ri = lax.broadcasted_iota(jnp.int32, (CHUNK, CHUNK), 0)
    ci = lax.broadcasted_iota(jnp.int32, (CHUNK, CHUNK), 1)
    if reverse:
        cum, mask, mask_t = ci >= ri, ci > ri, ri > ci
    else:
        cum, mask, mask_t = ci <= ri, ci <= ri, ri <= ci
    return cum.astype(F32), mask, mask_t


def _gla_specs(s, reverse):
    nsub = min(GLA_SUB, s // CHUNK)
    nsteps = s // (CHUNK * nsub)

    def row(n):
        return nsteps - 1 - n if reverse else n

    def chunk(pi):
        return nsub - 1 - pi if reverse else pi

    return nsub, nsteps, row, chunk


def _gla_fwd(proj, wg_pad, bg, reverse, name):
    s = proj.shape[0]
    nsub, nsteps, row, chunk = _gla_specs(s, reverse)
    rb = nsub * CHUNK

    def body(q_ref, k_ref, v_ref, t_ref, wg_ref, bg_ref, o_ref, st_ref, state):
        @pl.when(pl.program_id(0) == 0)
        def _():
            state[...] = jnp.zeros_like(state)

        _, _, la = _gla_gates(t_ref, wg_ref, bg_ref)
        cumf, mask, _ = _gla_masks(reverse)
        lane = lax.broadcasted_iota(jnp.int32, (CHUNK, LANES), 1)
        for pi in range(nsub):
            rows = slice(chunk(pi) * CHUNK, (chunk(pi) + 1) * CHUNK)
            la_c = la[rows]
            b = _cum_dot(cumf, la_c)
            bl = jnp.sum(la_c, axis=0, keepdims=True)
            q = q_ref[rows, :] * (GDK ** -0.5)
            k = k_ref[rows, :]
            qd = q * jnp.exp(b)
            ki = k * jnp.exp(-b)
            kte = k * jnp.exp(bl - b)
            decay = jnp.exp(bl)
            for h in range(GH):
                p = h // 2
                sl = slice(p * LANES, (p + 1) * LANES)
                lm = (lane < GDK) if h % 2 == 0 else (lane >= GDK)
                qd_h = jnp.where(lm, qd[:, sl], 0.0)
                kte_h = jnp.where(lm, kte[:, sl], 0.0)
                v_h = v_ref[rows, h * GDV:(h + 1) * GDV]
                st = state[h]
                a_mat = jnp.where(mask, _dot_nt(qd_h, ki[:, sl]), 0.0)
                o_ref[rows, h * GDV:(h + 1) * GDV] = _dot(a_mat, v_h) + _dot_nt(qd_h, st)
                st_ref[pi, h] = st
                state[h] = st * decay[:, sl] + _dot_tn(v_h, kte_h)

    return pl.pallas_call(
        body, grid=(nsteps,),
        in_specs=[pl.BlockSpec((rb, GQK), lambda n: (row(n), OGQ // GQK)),
                  pl.BlockSpec((rb, GQK), lambda n: (row(n), OGK // GQK)),
                  pl.BlockSpec((rb, GW), lambda n: (row(n), OGV // GW)),
                  pl.BlockSpec((rb, LANES), lambda n: (row(n), OT // LANES)),
                  pl.BlockSpec((LANES, GQK), lambda n: (0, 0)),
                  pl.BlockSpec((1, GQK), lambda n: (0, 0))],
        out_specs=[pl.BlockSpec((rb, GW), lambda n: (row(n), 0)),
                   pl.BlockSpec((nsub, GH, GDV, LANES), lambda n: (n, 0, 0, 0))],
        out_shape=[jax.ShapeDtypeStruct((s, GW), F32), jax.ShapeDtypeStruct((s // CHUNK, GH, GDV, LANES), F32)],
        scratch_shapes=[pltpu.VMEM((GH, GDV, LANES), F32)],
        name=name, compiler_params=_cp(("arbitrary",)))(proj, proj, proj, proj, wg_pad, bg)


def _gla_bwd(proj, wg_pad, bg, states, d_o, reverse, name):
    s = proj.shape[0]
    nsub, nsteps, row, chunk = _gla_specs(s, reverse)
    rb = nsub * CHUNK

    def body(q_ref, k_ref, v_ref, t_ref, wg_ref, bg_ref, st_ref, do_ref,
             dq_ref, dk_ref, dv_ref, dt_ref, dwg_ref, dbg_ref, dstate, da_buf):
        @pl.when(pl.program_id(0) == 0)
        def _():
            dstate[...] = jnp.zeros_like(dstate)
            dwg_ref[...] = jnp.zeros_like(dwg_ref)
            dbg_ref[...] = jnp.zeros_like(dbg_ref)

        t, a, la = _gla_gates(t_ref, wg_ref, bg_ref)
        cumf, mask, mask_t = _gla_masks(reverse)
        lane = lax.broadcasted_iota(jnp.int32, (CHUNK, LANES), 1)
        for pi in reversed(range(nsub)):
            rows = slice(chunk(pi) * CHUNK, (chunk(pi) + 1) * CHUNK)
            la_c = la[rows]
            b = _cum_dot(cumf, la_c)
            bl = jnp.sum(la_c, axis=0, keepdims=True)
            q = q_ref[rows, :] * (GDK ** -0.5)
            k = k_ref[rows, :]
            e, ei, ee = jnp.exp(b), jnp.exp(-b), jnp.exp(bl - b)
            qd, ki, kte = q * e, k * ei, k * ee
            decay = jnp.exp(bl)
            for p in range(GH // 2):
                sl = slice(p * LANES, (p + 1) * LANES)
                dqd = jnp.zeros((CHUNK, LANES), F32)
                dki = jnp.zeros((CHUNK, LANES), F32)
                dkte = jnp.zeros((CHUNK, LANES), F32)
                ddecay = jnp.zeros((1, LANES), F32)
                for half in range(2):
                    h = 2 * p + half
                    lm = (lane < GDK) if half == 0 else (lane >= GDK)
                    qd_h = jnp.where(lm, qd[:, sl], 0.0)
                    ki_h = jnp.where(lm, ki[:, sl], 0.0)
                    kte_h = jnp.where(lm, kte[:, sl], 0.0)
                    v_h = v_ref[rows, h * GDV:(h + 1) * GDV]
                    do_h = do_ref[rows, h * GDV:(h + 1) * GDV]
                    st = st_ref[pi, h]
                    dst = dstate[h]
                    at_mat = jnp.where(mask_t, _dot_nt(ki_h, qd_h), 0.0)
                    da_mat = jnp.where(mask, _dot_nt(do_h, v_h), 0.0)
                    dat_mat = jnp.where(mask_t, _dot_nt(v_h, do_h), 0.0)
                    dv_ref[rows, h * GDV:(h + 1) * GDV] = _dot(at_mat, do_h) + _dot_nt(kte_h, dst)
                    dqd += _dot(da_mat, ki_h) + _dot(do_h, st)
                    dki += _dot(dat_mat, qd_h)
                    dkte += _dot(v_h, dst)
                    ddecay += jnp.sum(dst * st, axis=0, keepdims=True)
                    dstate[h] = dst * decay[:, sl] + _dot_tn(do_h, qd_h)
                dq_ref[rows, sl] = dqd * e[:, sl] * (GDK ** -0.5)
                dk_ref[rows, sl] = dki * ei[:, sl] + dkte * ee[:, sl]
                db = dqd * qd[:, sl] - dki * ki[:, sl] - dkte * kte[:, sl]
                dbl = jnp.sum(dkte * kte[:, sl], axis=0, keepdims=True) + decay[:, sl] * ddecay
                da_buf[rows, sl] = _cum_dot(cumf, db, True) + dbl
        da = da_buf[...] * (1.0 / GTEMP) * _sigmoid(-a)
        dt_ref[...] = _dot_nt(da, wg_ref[...])
        dwg_ref[...] += _dot_tn(t, da)
        dbg_ref[...] += jnp.sum(da, axis=0, keepdims=True)

    def prow(j):
        return row(nsteps - 1 - j)

    return pl.pallas_call(
        body, grid=(nsteps,),
        in_specs=[pl.BlockSpec((rb, GQK), lambda j: (prow(j), OGQ // GQK)),
                  pl.BlockSpec((rb, GQK), lambda j: (prow(j), OGK // GQK)),
                  pl.BlockSpec((rb, GW), lambda j: (prow(j), OGV // GW)),
                  pl.BlockSpec((rb, LANES), lambda j: (prow(j), OT // LANES)),
                  pl.BlockSpec((LANES, GQK), lambda j: (0, 0)),
                  pl.BlockSpec((1, GQK), lambda j: (0, 0)),
                  pl.BlockSpec((nsub, GH, GDV, LANES), lambda j: (nsteps - 1 - j, 0, 0, 0)),
                  pl.BlockSpec((rb, GW), lambda j: (prow(j), 0))],
        out_specs=[pl.BlockSpec((rb, GQK), lambda j: (prow(j), 0)),
                   pl.BlockSpec((rb, GQK), lambda j: (prow(j), 0)),
                   pl.BlockSpec((rb, GW), lambda j: (prow(j), 0)),
                   pl.BlockSpec((rb, LANES), lambda j: (prow(j), 0)),
                   pl.BlockSpec((LANES, GQK), lambda j: (0, 0)),
                   pl.BlockSpec((1, GQK), lambda j: (0, 0))],
        out_shape=[jax.ShapeDtypeStruct((s, GQK), F32), jax.ShapeDtypeStruct((s, GQK), F32),
                   jax.ShapeDtypeStruct((s, GW), F32), jax.ShapeDtypeStruct((s, LANES), F32),
                   jax.ShapeDtypeStruct((LANES, GQK), F32), jax.ShapeDtypeStruct((1, GQK), F32)],
        scratch_shapes=[pltpu.VMEM((GH, GDV, LANES), F32), pltpu.VMEM((rb, GQK), F32)],
        name=name, compiler_params=_cp(("arbitrary",)))(proj, proj, proj, proj, wg_pad, bg, states, d_o)


def _rot_half(x):
    lane = lax.broadcasted_iota(jnp.int32, x.shape, 1)
    first = (lane % MROPE) < (MROPE // 2)
    return jnp.where(first, -pltpu.roll(x, LANES - MROPE // 2, 1), pltpu.roll(x, MROPE // 2, 1))


def _mla_prep(proj, cos, sin, qg, kvg, w_uq, w_ukv, name):
    s = proj.shape[0]
    tr = _rows(s)

    def body(mq_ref, mkv_ref, t_ref, cos_ref, sin_ref, qg_ref, kvg_ref, wuq_ref, wukv_ref, q_ref, k_ref, v_ref):
        cosv, sinv = cos_ref[...], sin_ref[...]
        lane = lax.broadcasted_iota(jnp.int32, (tr, LANES), 1)

        def rope(xv):
            return xv * cosv + _rot_half(xv) * sinv

        qm = _dot(_rms(mq_ref[...], qg_ref[...]), wuq_ref[...])
        kv = _dot(_rms(mkv_ref[...], kvg_ref[...]), wukv_ref[...])
        kr_lo = jnp.where(lane < MROPE, rope(t_ref[...]), 0.0)
        kr_hi = pltpu.roll(kr_lo, MROPE, 1)
        for p in range(MH // 2):
            r = rope(qm[:, MW + p * LANES:MW + (p + 1) * LANES]).astype(q_ref.dtype)
            q_ref[2 * p, :, LANES:] = r
            q_ref[2 * p + 1, :, LANES:] = r
        for h in range(MH):
            q_ref[h, :, :LANES] = qm[:, h * LANES:(h + 1) * LANES].astype(q_ref.dtype)
            k_ref[h, :, :LANES] = kv[:, 2 * h * LANES:(2 * h + 1) * LANES].astype(k_ref.dtype)
            k_ref[h, :, LANES:] = (kr_lo if h % 2 == 0 else kr_hi).astype(k_ref.dtype)
            v_ref[h] = kv[:, (2 * h + 1) * LANES:(2 * h + 2) * LANES].astype(v_ref.dtype)

    def full(shape):
        return pl.BlockSpec(shape, lambda i: (0,) * len(shape))

    return pl.pallas_call(
        body, grid=(s // tr,),
        in_specs=[pl.BlockSpec((tr, MQL), lambda i: (i, OMQ // MQL)),
                  pl.BlockSpec((tr, MKVL), lambda i: (i, OMKV // MKVL)),
                  pl.BlockSpec((tr, LANES), lambda i: (i, OT // LANES)),
                  pl.BlockSpec((tr, LANES), lambda i: (i, 0)),
                  pl.BlockSpec((tr, LANES), lambda i: (i, 0)),
                  full((1, MQL)), full((1, MKVL)), full((MQL, MQW)), full((MKVL, MKVW))],
        out_specs=[pl.BlockSpec((MH, tr, 2 * LANES), lambda i: (0, i, 0)),
                   pl.BlockSpec((MH, tr, 2 * LANES), lambda i: (0, i, 0)),
                   pl.BlockSpec((MH, tr, LANES), lambda i: (0, i, 0))],
        out_shape=[jax.ShapeDtypeStruct((MH, s, 2 * LANES), MXU), jax.ShapeDtypeStruct((MH, s, 2 * LANES), MXU),
                   jax.ShapeDtypeStruct((MH, s, LANES), MXU)],
        name=name, compiler_params=_cp(("parallel",)))(proj, proj, proj, cos, sin, qg, kvg, w_uq, w_ukv)


def _mla_prep_bwd(proj, cos, sin, qg, kvg, w_uq, w_ukv, d_q, d_k, d_v, name):
    s = proj.shape[0]
    tr = _rows(s)

    def body(mq_ref, mkv_ref, cos_ref, sin_ref, qg_ref, kvg_ref, wuq_ref, wukv_ref, dq_ref, dk_ref, dv_ref,
             dmq_ref, dmkv_ref, dt_ref, dwuq_ref, dwukv_ref, dqg_ref, dkvg_ref):
        @pl.when(pl.program_id(0) == 0)
        def _():
            for r in (dwuq_ref, dwukv_ref, dqg_ref, dkvg_ref):
                r[...] = jnp.zeros_like(r)

        cosv, sinv = cos_ref[...], sin_ref[...]
        lane = lax.broadcasted_iota(jnp.int32, (tr, LANES), 1)
        lo = lane < MROPE

        def unrope(dv):
            return dv * cosv - _rot_half(dv * sinv)

        parts = [dq_ref[h, :, :LANES] for h in range(MH)]
        for p in range(MH // 2):
            parts.append(unrope(jnp.where(lo, dq_ref[2 * p, :, LANES:], dq_ref[2 * p + 1, :, LANES:])))
        d_qm = jnp.concatenate(parts, axis=1)
        mq, qgv = mq_ref[...], qg_ref[...]
        cq = _rms(mq, qgv)
        dwuq_ref[...] += _dot_tn(cq, d_qm)
        dmq, dqg = _rms_bwd(_dot_nt(d_qm, wuq_ref[...]), mq, qgv)
        dmq_ref[...] = dmq
        dqg_ref[...] += dqg

        parts = []
        for h in range(MH):
            parts += [dk_ref[h, :, :LANES], dv_ref[h]]
        d_kv = jnp.concatenate(parts, axis=1)
        mkv, kvgv = mkv_ref[...], kvg_ref[...]
        ckv = _rms(mkv, kvgv)
        dwukv_ref[...] += _dot_tn(ckv, d_kv)
        dmkv, dkvg = _rms_bwd(_dot_nt(d_kv, wukv_ref[...]), mkv, kvgv)
        dmkv_ref[...] = dmkv
        dkvg_ref[...] += dkvg

        even = dk_ref[0, :, LANES:] + dk_ref[2, :, LANES:] + dk_ref[4, :, LANES:]
        odd = dk_ref[1, :, LANES:] + dk_ref[3, :, LANES:] + dk_ref[5, :, LANES:]
        d_kr = jnp.where(lo, even, 0.0) + pltpu.roll(jnp.where(lo, 0.0, odd), MROPE, 1)
        dt_ref[...] = jnp.where(lo, unrope(d_kr), 0.0)

    def full(shape):
        return pl.BlockSpec(shape, lambda i: (0,) * len(shape))

    return pl.pallas_call(
        body, grid=(s // tr,),
        in_specs=[pl.BlockSpec((tr, MQL), lambda i: (i, OMQ // MQL)),
                  pl.BlockSpec((tr, MKVL), lambda i: (i, OMKV // MKVL)),
                  pl.BlockSpec((tr, LANES), lambda i: (i, 0)),
                  pl.BlockSpec((tr, LANES), lambda i: (i, 0)),
                  full((1, MQL)), full((1, MKVL)), full((MQL, MQW)), full((MKVL, MKVW)),
                  pl.BlockSpec((MH, tr, 2 * LANES), lambda i: (0, i, 0)),
                  pl.BlockSpec((MH, tr, 2 * LANES), lambda i: (0, i, 0)),
                  pl.BlockSpec((MH, tr, LANES), lambda i: (0, i, 0))],
        out_specs=[pl.BlockSpec((tr, MQL), lambda i: (i, 0)), pl.BlockSpec((tr, MKVL), lambda i: (i, 0)),
                   pl.BlockSpec((tr, LANES), lambda i: (i, 0)),
                   full((MQL, MQW)), full((MKVL, MKVW)), full((1, MQL)), full((1, MKVL))],
        out_shape=[jax.ShapeDtypeStruct((s, MQL), F32), jax.ShapeDtypeStruct((s, MKVL), F32),
                   jax.ShapeDtypeStruct((s, LANES), F32),
                   jax.ShapeDtypeStruct((MQL, MQW), F32), jax.ShapeDtypeStruct((MKVL, MKVW), F32),
                   jax.ShapeDtypeStruct((1, MQL), F32), jax.ShapeDtypeStruct((1, MKVL), F32)],
        name=name, compiler_params=_cp(("arbitrary",)))(proj, proj, cos, sin, qg, kvg, w_uq, w_ukv, d_q, d_k, d_v)


ATT_SCALE = (MNOPE + MROPE) ** -0.5
ATT_SCALE_LOG2 = ATT_SCALE * 1.4426950408889634
ATT_TQ_FWD, ATT_TQ = 256, 512


def _attn_fwd(q, k, v, name):
    s = q.shape[1]
    tq = min(ATT_TQ_FWD, s)

    def body(q_ref, k_ref, v_ref, o_ref, lse_ref):
        sc = _dot_nt(q_ref[0], k_ref[0])
        m = jnp.max(sc, axis=-1, keepdims=True)
        p = jnp.exp2((sc - m) * ATT_SCALE_LOG2)
        l = jnp.sum(p, axis=-1, keepdims=True)
        o_ref[...] = _dot(p, v_ref[0]) / l
        lse_ref[0] = m * ATT_SCALE_LOG2 + jnp.log2(l)

    return pl.pallas_call(
        body, grid=(MH, s // tq),
        in_specs=[pl.BlockSpec((1, tq, 2 * LANES), lambda h, i: (h, i, 0)),
                  pl.BlockSpec((1, s, 2 * LANES), lambda h, i: (h, 0, 0)),
                  pl.BlockSpec((1, s, LANES), lambda h, i: (h, 0, 0))],
        out_specs=[pl.BlockSpec((tq, LANES), lambda h, i: (i, h)),
                   pl.BlockSpec((1, tq, 1), lambda h, i: (h, i, 0))],
        out_shape=[jax.ShapeDtypeStruct((s, MW), F32), jax.ShapeDtypeStruct((MH, s, 1), F32)],
        name=name, compiler_params=_cp(("parallel", "parallel")))(q, k, v)


def _attn_bwd(q, k, v, o, lse, d_o, name):
    s = q.shape[1]
    tq = min(ATT_TQ, s)

    def body(q_ref, k_ref, v_ref, o_ref, lse_ref, do_ref, dq_ref, dk_ref, dv_ref):
        @pl.when(pl.program_id(1) == 0)
        def _():
            dk_ref[...] = jnp.zeros_like(dk_ref)
            dv_ref[...] = jnp.zeros_like(dv_ref)

        qv, kv, do = q_ref[0], k_ref[0], do_ref[...]
        p = jnp.exp2(_dot_nt(qv, kv) * ATT_SCALE_LOG2 - lse_ref[0])
        delta = jnp.sum(do * o_ref[...], axis=-1, keepdims=True)
        ds = p * (_dot_nt(do, v_ref[0]) - delta)
        dq_ref[0] = _dot(ds, kv) * ATT_SCALE
        dk_ref[0] += _dot_tn(ds, qv) * ATT_SCALE
        dv_ref[0] += _dot_tn(p, do)

    return pl.pallas_call(
        body, grid=(MH, s // tq),
        in_specs=[pl.BlockSpec((1, tq, 2 * LANES), lambda h, i: (h, i, 0)),
                  pl.BlockSpec((1, s, 2 * LANES), lambda h, i: (h, 0, 0)),
                  pl.BlockSpec((1, s, LANES), lambda h, i: (h, 0, 0)),
                  pl.BlockSpec((tq, LANES), lambda h, i: (i, h)),
                  pl.BlockSpec((1, tq, 1), lambda h, i: (h, i, 0)),
                  pl.BlockSpec((tq, LANES), lambda h, i: (i, h))],
        out_specs=[pl.BlockSpec((1, tq, 2 * LANES), lambda h, i: (h, i, 0)),
                   pl.BlockSpec((1, s, 2 * LANES), lambda h, i: (h, 0, 0)),
                   pl.BlockSpec((1, s, LANES), lambda h, i: (h, 0, 0))],
        out_shape=[jax.ShapeDtypeStruct((MH, s, 2 * LANES), F32), jax.ShapeDtypeStruct((MH, s, 2 * LANES), F32),
                   jax.ShapeDtypeStruct((MH, s, LANES), F32)],
        name=name, compiler_params=_cp(("parallel", "arbitrary")))(q, k, v, o, lse, d_o)


def _merge_fwd(o_f, o_b, o_att, pre, proj, gng, mog, cog, name):
    s = proj.shape[0]
    tr = _rows(s)

    def body(of_ref, ob_ref, oa_ref, pre_ref, z_ref, gng_ref, mog_ref, cog_ref, y_ref):
        z = z_ref[...]
        sz = z * _sigmoid(z)
        osum = of_ref[...] + ob_ref[...]
        gg = gng_ref[...]
        for h in range(GH):
            sl = slice(h * GDV, (h + 1) * GDV)
            y_ref[:, sl] = (_rms(osum[:, sl], gg) * sz[:, sl]).astype(y_ref.dtype)
        y_ref[:, GW:GW + MW] = (_rms(oa_ref[...], mog_ref[...]) * sz[:, GW:GW + MW]).astype(y_ref.dtype)
        y_ref[:, GW + MW:] = (_rms(pre_ref[...], cog_ref[...]) * sz[:, GW + MW:]).astype(y_ref.dtype)

    def row(w):
        return pl.BlockSpec((tr, w), lambda i: (i, 0))

    def vec(w):
        return pl.BlockSpec((1, w), lambda i: (0, 0))

    return pl.pallas_call(
        body, grid=(s // tr,),
        in_specs=[row(GW), row(GW), row(MW), row(CONV_CH), row(D_MIX), vec(GDV), vec(MW), vec(CONV_CH)],
        out_specs=row(D_MIX), out_shape=jax.ShapeDtypeStruct((s, D_MIX), MXU),
        name=name, compiler_params=_cp(("parallel",)))(o_f, o_b, o_att, pre, proj, gng, mog, cog)


def _merge_bwd(d_y, o_f, o_b, o_att, pre, proj, gng, mog, cog, name):
    s = proj.shape[0]
    tr = _rows(s)

    def body(dy_ref, of_ref, ob_ref, oa_ref, pre_ref, z_ref, gng_ref, mog_ref, cog_ref,
             dz_ref, dos_ref, doa_ref, dpre_ref, dgng_ref, dmog_ref, dcog_ref):
        @pl.when(pl.program_id(0) == 0)
        def _():
            for r in (dgng_ref, dmog_ref, dcog_ref):
                r[...] = jnp.zeros_like(r)

        z, dy = z_ref[...], dy_ref[...]
        sg = _sigmoid(z)
        sz = z * sg
        dsz = sg * (1.0 + z * (1.0 - sg))
        dcat = dy * sz
        dyz = dy * dsz
        osum = of_ref[...] + ob_ref[...]
        gg = gng_ref[...]
        dgg = jnp.zeros_like(gg)
        for h in range(GH):
            sl = slice(h * GDV, (h + 1) * GDV)
            dz_ref[:, sl] = dyz[:, sl] * _rms(osum[:, sl], gg)
            dx, dg = _rms_bwd(dcat[:, sl], osum[:, sl], gg)
            dos_ref[:, sl] = dx
            dgg += dg
        dgng_ref[...] += dgg
        sl = slice(GW, GW + MW)
        oa, mg = oa_ref[...], mog_ref[...]
        dz_ref[:, sl] = dyz[:, sl] * _rms(oa, mg)
        dx, dg = _rms_bwd(dcat[:, sl], oa, mg)
        doa_ref[...] = dx
        dmog_ref[...] += dg
        sl = slice(GW + MW, D_MIX)
        pv, cg = pre_ref[...], cog_ref[...]
        dz_ref[:, sl] = dyz[:, sl] * _rms(pv, cg)
        dx, dg = _rms_bwd(dcat[:, sl], pv, cg)
        dpre_ref[...] = dx
        dcog_ref[...] += dg

    def row(w):
        return pl.BlockSpec((tr, w), lambda i: (i, 0))

    def vec(w):
        return pl.BlockSpec((1, w), lambda i: (0, 0))

    def rs(w):
        return jax.ShapeDtypeStruct((s, w), F32)

    def vs(w):
        return jax.ShapeDtypeStruct((1, w), F32)

    return pl.pallas_call(
        body, grid=(s // tr,),
        in_specs=[row(D_MIX), row(GW), row(GW), row(MW), row(CONV_CH), row(D_MIX), vec(GDV), vec(MW), vec(CONV_CH)],
        out_specs=[row(D_MIX), row(GW), row(MW), row(CONV_CH), vec(GDV), vec(MW), vec(CONV_CH)],
        out_shape=[rs(D_MIX), rs(GW), rs(MW), rs(CONV_CH), vs(GDV), vs(MW), vs(CONV_CH)],
        name=name, compiler_params=_cp(("arbitrary",)))(d_y, o_f, o_b, o_att, pre, proj, gng, mog, cog)


def _assemble_dproj(d_z, d_cb, d_cc, d_cx, d_mkv, dv_f, dv_b, dq_f, dq_b, dk_f, dk_b, d_mq, dt_m, dt_f, dt_b, name):
    s = d_z.shape[0]
    tr = _rows(s)

    def body(dz, dcb, dcc, dcx, dmkv, dvf, dvb, dqf, dqb, dkf, dkb, dmq, dtm, dtf, dtb, out):
        dt = out.dtype
        out[:, OZ:OZ + D_MIX] = dz[...].astype(dt)
        out[:, OCB:OCB + CONV_CH] = dcb[...].astype(dt)
        out[:, OCC:OCC + CONV_CH] = dcc[...].astype(dt)
        out[:, OCX:OCX + CONV_CH] = dcx[...].astype(dt)
        out[:, OMKV:OMKV + MKVL] = dmkv[...].astype(dt)
        out[:, OGV:OGV + GW] = (dvf[...] + dvb[...]).astype(dt)
        out[:, OGQ:OGQ + GQK] = (dqf[...] + dqb[...]).astype(dt)
        out[:, OGK:OGK + GQK] = (dkf[...] + dkb[...]).astype(dt)
        out[:, OMQ:OMQ + MQL] = dmq[...].astype(dt)
        out[:, OT:OT + LANES] = (dtm[...] + dtf[...] + dtb[...]).astype(dt)

    args = (d_z, d_cb, d_cc, d_cx, d_mkv, dv_f, dv_b, dq_f, dq_b, dk_f, dk_b, d_mq, dt_m, dt_f, dt_b)
    return pl.pallas_call(
        body, grid=(s // tr,),
        in_specs=[pl.BlockSpec((tr, a.shape[1]), lambda i: (i, 0)) for a in args],
        out_specs=pl.BlockSpec((tr, PW), lambda i: (i, 0)),
        out_shape=jax.ShapeDtypeStruct((s, PW), MXU), name=name, compiler_params=_cp(("parallel",)))(*args)


def _layer_fwd(x, mod, wt, cos, sin, tag, late=None, in_after=()):
    shift, scale, gate = mod
    h = _norm_mod(x, wt["norm_g"], scale, shift, f"norm_mod_{tag}")
    (proj,) = _matmul(h, wt["w_in"], dims="nn", tm=2048, tn=256, tk=2048, out_dtypes=(F32,), name=f"in_proj_{tag}",
                      after=in_after)
    if late is not None:
        wt.update(late(proj))
    o_f, st_f = _gla_fwd(proj, wt["wg_pad_f"], wt["bg_f"], False, f"gla_fwd_f_{tag}")
    o_b, st_b = _gla_fwd(proj, wt["wg_pad_b"], wt["bg_b"], True, f"gla_fwd_b_{tag}")
    q, k, v = _mla_prep(proj, cos, sin, wt["q_norm_g"], wt["kv_norm_g"], wt["w_uq"], wt["w_ukv"], f"mla_prep_{tag}")
    o_att, lse = _attn_fwd(q, k, v, f"attn_fwd_{tag}")
    pre = _conv_fwd(proj, wt["conv_w"], f"conv_fwd_{tag}")
    y = _merge_fwd(o_f, o_b, o_att, pre, proj, wt["gla_norm_g"], wt["mla_out_g"], wt["conv_out_g"], f"merge_fwd_{tag}")
    x_new, u = _matmul(y, wt["w_out"], dims="nn", tm=2048, tn=256, tk=2048, out_dtypes=(F32, F32),
                       name=f"out_proj_{tag}", epilogue=lambda acc, xv, gv: (xv + gv * acc, acc),
                       extras=(x, gate), extra_kinds=("mn", "n"))
    saved = dict(x=x, h=h, proj=proj, o_f=o_f, o_b=o_b, st_f=st_f, st_b=st_b, q=q, k=k, v=v,
                 o_att=o_att, lse=lse, pre=pre, y=y, u=u)
    return x_new, saved


def _layer_bwd(d_out, sv, mod, wt, cos, sin, tag, ship=None, dx_first=None, ship_rest=None):
    shift, scale, gate = mod
    proj = sv["proj"]
    d_u, d_gate = _gate_bwd(d_out, sv["u"], gate, f"gate_bwd_{tag}")
    (g_w_out,) = _matmul(sv["y"], d_u, dims="tn", tm=1024, tn=512, tk=2048, out_dtypes=(MXU,), name=f"out_proj_dw_{tag}")
    (d_y,) = _matmul(d_u, wt["w_out"], dims="nt", tm=2048, tn=256, tk=2048, out_dtypes=(F32,), name=f"out_proj_dx_{tag}",
                     after=(g_w_out,))
    d_z, d_osum, d_oatt, d_pre, d_gng, d_mog, d_cog = _merge_bwd(
        d_y, sv["o_f"], sv["o_b"], sv["o_att"], sv["pre"], proj, wt["gla_norm_g"], wt["mla_out_g"], wt["conv_out_g"],
        f"merge_bwd_{tag}")
    d_cb, d_cc, d_cx, d_conv_w = _conv_bwd(proj, wt["conv_w"], d_pre, f"conv_bwd_{tag}")
    d_q, d_k, d_v = _attn_bwd(sv["q"], sv["k"], sv["v"], sv["o_att"], sv["lse"], d_oatt, f"attn_bwd_{tag}")
    d_mq, d_mkv, dt_m, g_w_uq, g_w_ukv, d_qg, d_kvg = _mla_prep_bwd(
        proj, cos, sin, wt["q_norm_g"], wt["kv_norm_g"], wt["w_uq"], wt["w_ukv"], d_q, d_k, d_v, f"mla_prep_bwd_{tag}")
    bg_f, bg_b = wt["bg_f"], wt["bg_b"]
    if ship_rest is not None:
        tok = ship_rest(dict(w_out=g_w_out, w_uq=g_w_uq, w_ukv=g_w_ukv))
        bg_f, bg_b = bg_f + tok, bg_b + tok
    dq_f, dk_f, dv_f, dt_f, d_wg_f, d_bg_f = _gla_bwd(proj, wt["wg_pad_f"], bg_f, sv["st_f"], d_osum, False,
                                                     f"gla_bwd_f_{tag}")
    dq_b, dk_b, dv_b, dt_b, d_wg_b, d_bg_b = _gla_bwd(proj, wt["wg_pad_b"], bg_b, sv["st_b"], d_osum, True,
                                                     f"gla_bwd_b_{tag}")
    d_proj = _assemble_dproj(d_z, d_cb, d_cc, d_cx, d_mkv, dv_f, dv_b, dq_f, dq_b, dk_f, dk_b, d_mq, dt_m, dt_f, dt_b,
                             f"assemble_dproj_{tag}")
    grads = dict(w_out=g_w_out, w_uq=g_w_uq, w_ukv=g_w_ukv,
                 wg_pad_f=d_wg_f, bg_f=d_bg_f, wg_pad_b=d_wg_b, bg_b=d_bg_b, gla_norm_g=d_gng,
                 q_norm_g=d_qg, kv_norm_g=d_kvg, mla_out_g=d_mog, conv_w=d_conv_w, conv_out_g=d_cog)

    def in_dw(after):
        (g_w_in,) = _matmul(sv["h"], d_proj, dims="tn", tm=2048, tn=256, tk=2048, out_dtypes=(MXU,),
                            name=f"in_proj_dw_{tag}", after=after)
        grads["w_in"] = g_w_in
        return dict(w_in=g_w_in, w_out=g_w_out, w_uq=g_w_uq, w_ukv=g_w_ukv)

    def in_dx(after):
        (d_h,) = _matmul(d_proj, wt["w_in"], dims="nt", tm=1024, tn=512, tk=PW, out_dtypes=(F32,),
                         name=f"in_proj_dx_{tag}", after=after)
        d_x, d_shift, d_scale, d_ng = _norm_mod_bwd(d_h, sv["x"], d_out, wt["norm_g"], scale, f"norm_mod_bwd_{tag}")
        grads["norm_g"] = d_ng
        return d_x, (d_shift, d_scale, d_gate)

    if dx_first is None:
        big = in_dw(())
        d_x, d_mod = in_dx((big["w_in"],) if ship is None else ship(big))
    else:
        d_x, d_mod = in_dx(())
        big = in_dw(dx_first(d_x, d_mod, grads))
        ship(big)
    return d_x, d_mod, grads


def _perm_in_cols(w):
    pad = jnp.zeros(w.shape[:-1] + (PW - IN_DIM,), w.dtype)
    return jnp.concatenate([w[..., 3808:5856], w[..., 2272:3808], w[..., 1952:2208], w[..., 768:1536], w[..., 0:768],
                            w[..., 1568:1952], w[..., 2208:2272], w[..., 1536:1568], pad], axis=-1)


def _unperm_in_cols(g):
    return jnp.concatenate([g[..., OGQ:OGQ + 2 * GQK], g[..., OGV:OGV + GW], g[..., OT + MROPE:OT + MROPE + 2 * GRANK],
                            g[..., OMQ:OMQ + MQL], g[..., OMKV:OMKV + MKVL], g[..., OT:OT + MROPE],
                            g[..., OCB:OCB + 3 * CONV_CH], g[..., OZ:OZ + D_MIX]], axis=-1)


IN_SEGS = ((3808, 5856), (2272, 3808), (1952, 2208), (768, 1536), (0, 768), (1568, 1952), (2208, 2272), (1536, 1568))
UQ_SEGS = (tuple((h * (MNOPE + MROPE), h * (MNOPE + MROPE) + MNOPE) for h in range(MH))
           + tuple((h * (MNOPE + MROPE) + MNOPE, (h + 1) * (MNOPE + MROPE)) for h in range(MH)))


def _perm_gathered(g, segs, width):
    per = g.shape[-1]
    parts, total = [], 0
    for a, b in segs:
        c = a
        while c < b:
            j = c // per
            hi = min(b, (j + 1) * per)
            parts.append(g[j, :, c - j * per:hi - j * per])
            c = hi
        total += b - a
    if width > total:
        parts.append(jnp.zeros((g.shape[1], width - total), g.dtype))
    return jnp.concatenate(parts, axis=1)


def _scatter_perm(gp, segs, per):
    offs, o = [], 0
    for a, b in segs:
        offs.append((a, b, o))
        o += b - a
    blocks = []
    for j in range(N_DEV):
        lo, hi = j * per, (j + 1) * per
        pieces = []
        for a, b, o in sorted(offs):
            s0, s1 = max(a, lo), min(b, hi)
            if s0 < s1:
                pieces.append(gp[:, o + s0 - a:o + s1 - a])
        blocks.append(jnp.concatenate(pieces, axis=1))
    return jnp.stack(blocks)


def _perm_uq_cols(w):
    w3 = w.reshape(w.shape[:-1] + (MH, MNOPE + MROPE))
    return jnp.concatenate([w3[..., :MNOPE].reshape(w.shape[:-1] + (MH * MNOPE,)),
                            w3[..., MNOPE:].reshape(w.shape[:-1] + (MH * MROPE,))], axis=-1)


def _unperm_uq_cols(g):
    nope = g[..., :MH * MNOPE].reshape(g.shape[:-1] + (MH, MNOPE))
    rope = g[..., MH * MNOPE:].reshape(g.shape[:-1] + (MH, MROPE))
    return jnp.concatenate([nope, rope], axis=-1).reshape(g.shape[:-1] + (MQW,))


def _prep_layer_weights(w_in, w_out, w_uq, w_ukv, small):
    def vec(v):
        return v.reshape(1, -1).astype(F32)

    zeros = functools.partial(jnp.zeros, dtype=F32)
    wg_f, wg_b = small["gla_wg_f"].astype(F32), small["gla_wg_b"].astype(F32)
    wg_pad_f = jnp.concatenate([zeros((MROPE, GQK)), wg_f, zeros((LANES - MROPE - GRANK, GQK))], axis=0)
    wg_pad_b = jnp.concatenate([zeros((MROPE + GRANK, GQK)), wg_b, zeros((LANES - MROPE - 2 * GRANK, GQK))], axis=0)
    wt = dict(norm_g=vec(small["norm_g"]), wg_pad_f=wg_pad_f, wg_pad_b=wg_pad_b,
              bg_f=vec(small["gla_bg_f"]), bg_b=vec(small["gla_bg_b"]), gla_norm_g=vec(small["gla_norm_g"]),
              q_norm_g=vec(small["mla_q_norm_g"]), kv_norm_g=vec(small["mla_kv_norm_g"]),
              mla_out_g=vec(small["mla_out_g"]), conv_w=small["conv_w"].astype(F32),
              conv_out_g=vec(small["conv_out_g"]))
    for name, w in (("w_in", w_in), ("w_out", w_out), ("w_uq", w_uq), ("w_ukv", w_ukv)):
        if w is not None:
            wt[name] = w.astype(MXU)
    return wt


def _natural_small(gr):
    return dict(norm_g=gr["norm_g"][0],
                gla_wg_f=gr["wg_pad_f"][MROPE:MROPE + GRANK], gla_bg_f=gr["bg_f"][0],
                gla_wg_b=gr["wg_pad_b"][MROPE + GRANK:MROPE + 2 * GRANK], gla_bg_b=gr["bg_b"][0],
                gla_norm_g=gr["gla_norm_g"][0], mla_q_norm_g=gr["q_norm_g"][0], mla_kv_norm_g=gr["kv_norm_g"][0],
                mla_out_g=gr["mla_out_g"][0], conv_w=gr["conv_w"], conv_out_g=gr["conv_out_g"][0])


def _natural_grads(gr):
    return dict(_natural_small(gr), w_in=_unperm_in_cols(gr["w_in"]), w_out=gr["w_out"],
                mla_w_uq=_unperm_uq_cols(gr["w_uq"]), mla_w_ukv=gr["w_ukv"])


def _exchange(arrs, name, scatter, space):
    n = len(arrs)

    def body(*refs):
        ins, outs = refs[:n], refs[n:2 * n]
        send_sems, recv_sems, loc_sems = refs[2 * n:]
        ax, ay, ac = lax.axis_index("x"), lax.axis_index("y"), lax.axis_index("c")
        me = 4 * ax + 2 * ay + ac

        def src(a, to):
            return ins[a].at[to] if scatter else ins[a]

        def remote(a, r, dst_slot):
            px = 1 - ax if r & 4 else ax
            py = 1 - ay if r & 2 else ay
            pc = 1 - ac if r & 1 else ac
            return pltpu.make_async_remote_copy(
                src_ref=src(a, 4 * px + 2 * py + pc), dst_ref=outs[a].at[dst_slot(4 * px + 2 * py + pc)],
                send_sem=send_sems.at[a, r - 1], recv_sem=recv_sems.at[a, r - 1],
                device_id=(px, py, pc), device_id_type=MESH)

        locs = [pltpu.make_async_copy(src(a, me), outs[a].at[me], loc_sems.at[a]) for a in range(n)]
        for cp in locs:
            cp.start()
        sends = [remote(a, r, lambda peer: me) for r in range(1, N_DEV) for a in range(n)]
        for cp in sends:
            cp.start()
        for r in range(1, N_DEV):
            for a in range(n):
                remote(a, r, lambda peer: peer).wait_recv()
        for cp in sends:
            cp.wait_send()
        for cp in locs:
            cp.wait()

    def out_shape(a):
        return jax.ShapeDtypeStruct(a.shape if scatter else (N_DEV,) + a.shape, a.dtype)

    spec = pl.BlockSpec(memory_space=space)
    return pl.pallas_call(
        body, in_specs=[spec] * n, out_specs=[spec] * n, out_shape=[out_shape(a) for a in arrs],
        scratch_shapes=[pltpu.SemaphoreType.DMA((n, N_DEV - 1)), pltpu.SemaphoreType.DMA((n, N_DEV - 1)),
                        pltpu.SemaphoreType.DMA((n,))],
        name=name, compiler_params=pltpu.CompilerParams(vmem_limit_bytes=VMEM_LIMIT))(*arrs)


def _peer(r):
    ax, ay, ac = lax.axis_index("x"), lax.axis_index("y"), lax.axis_index("c")
    px = 1 - ax if r & 4 else ax
    py = 1 - ay if r & 2 else ay
    pc = 1 - ac if r & 1 else ac
    return (px, py, pc), 4 * px + 2 * py + pc


def _slot(rel_div):
    rel, div = rel_div
    idx = _peer(rel)[1]
    return idx if div == 1 else idx // div


AG_SPREAD = tuple((r, None, (0, 1), (r, 1)) for r in (1, 2, 4, 6))
AG_FORWARD = tuple((1, (k, 1), (k, 1), (1 ^ k, 1)) for k in (2, 4, 6))
RS_PAIR = tuple((1, (1 ^ k, 1), (1 ^ k, 2), (k, 2)) for k in (0, 2, 4, 6))
RS_CHIPS = tuple((r, (r, 2), (0, 2), (r, 2)) for r in (2, 4, 6))


def _plan_copies(plan, n, src_refs, land_refs, send_sems, recv_sems, arriving):
    out = []
    for i, (r, src, dst, recv) in enumerate(plan):
        peer = _peer(r)[0]
        for a in range(n):
            out.append(pltpu.make_async_remote_copy(
                src_ref=src_refs[a] if src is None else src_refs[a].at[_slot(src)],
                dst_ref=land_refs[a].at[_slot(recv if arriving else dst)],
                send_sem=send_sems.at[i * n + a], recv_sem=recv_sems.at[i * n + a],
                device_id=peer, device_id_type=MESH))
    return out


def _exchange_hbm(plan, srcs, lands, name, after=()):
    n = len(lands)
    fresh = isinstance(lands[0], jax.ShapeDtypeStruct)
    ins = ([] if srcs is None else list(srcs)) + ([] if fresh else list(lands))
    ns = 0 if srcs is None else n
    n_data = len(ins)
    ins = ins + list(after)

    def body(*refs):
        outs = refs[len(ins):len(ins) + n]
        send_sems, recv_sems = refs[-2:]
        src_refs = refs[:n] if srcs is not None else refs[ns:ns + n]
        sends = _plan_copies(plan, n, src_refs, outs, send_sems, recv_sems, False)
        for cp in sends:
            cp.start()
        for cp in _plan_copies(plan, n, src_refs, outs, send_sems, recv_sems, True):
            cp.wait_recv()
        for cp in sends:
            cp.wait_send()

    hbm = pl.BlockSpec(memory_space=pltpu.HBM)
    k = len(plan) * n
    return pl.pallas_call(
        body, name=name, in_specs=[hbm] * n_data + [pl.BlockSpec(memory_space=pl.ANY)] * len(after), out_specs=[hbm] * n,
        out_shape=[jax.ShapeDtypeStruct(a.shape, a.dtype) for a in lands],
        scratch_shapes=[pltpu.SemaphoreType.DMA((k,)), pltpu.SemaphoreType.DMA((k,))],
        input_output_aliases={} if fresh else {ns + i: i for i in range(n)},
        compiler_params=pltpu.CompilerParams(vmem_limit_bytes=VMEM_LIMIT))(*ins)


def _plan_start(plan, srcs, land_shapes, after, name):
    n = len(srcs)

    def body(*refs):
        src_refs, land_refs = refs[:n], refs[n:2 * n]
        send_sems, recv_sems = refs[2 * n + 1], refs[2 * n + 2]
        for cp in _plan_copies(plan, n, src_refs, land_refs, send_sems, recv_sems, False):
            cp.start()
        refs[-1][...] = jnp.zeros_like(refs[-1])

    hbm = pl.BlockSpec(memory_space=pltpu.HBM)
    sem = pl.BlockSpec(memory_space=pltpu.SEMAPHORE)
    k = len(plan) * n
    srcs = [pltpu.with_memory_space_constraint(a, pltpu.HBM) for a in srcs]
    lands = [pltpu.with_memory_space_constraint(lax.empty(shp, a.dtype), pltpu.HBM) for shp, a in zip(land_shapes, srcs)]
    res = pl.pallas_call(
        body, name=name,
        in_specs=[hbm] * (2 * n) + [pl.BlockSpec(memory_space=pl.ANY)],
        out_specs=[sem, sem] + [hbm] * (2 * n) + [pl.BlockSpec(memory_space=pltpu.VMEM)],
        out_shape=[pltpu.SemaphoreType.DMA((k,)), pltpu.SemaphoreType.DMA((k,))]
        + [pltpu.HBM(a.shape, a.dtype) for a in srcs] + [pltpu.HBM(shp, a.dtype) for shp, a in zip(land_shapes, srcs)]
        + [jax.ShapeDtypeStruct((8, LANES), F32)],
        input_output_aliases={i: 2 + i for i in range(2 * n)},
        compiler_params=pltpu.CompilerParams(has_side_effects=pltpu.SideEffectType.DATAFLOW_SIDE_EFFECTING),
    )(*srcs, *lands, after)
    return res[0], res[1], list(res[2:2 + n]), list(res[2 + n:2 + 2 * n]), res[-1]


def _plan_wait(plan, handle, after, name):
    send_sems, recv_sems, srcs, lands, _ = handle
    n = len(srcs)
    after = list(after)

    def body(*refs):
        src_refs, land_refs = refs[:n], refs[n:2 * n]
        ssem, rsem = refs[2 * n], refs[2 * n + 1]
        for cp in _plan_copies(plan, n, src_refs, land_refs, ssem, rsem, False):
            cp.wait_send()
        for cp in _plan_copies(plan, n, src_refs, land_refs, ssem, rsem, True):
            cp.wait_recv()

    hbm = pl.BlockSpec(memory_space=pltpu.HBM)
    sem = pl.BlockSpec(memory_space=pltpu.SEMAPHORE)
    res = pl.pallas_call(
        body, name=name,
        in_specs=[hbm] * (2 * n) + [sem, sem] + [pl.BlockSpec(memory_space=pl.ANY)] * len(after),
        out_specs=[hbm] * (2 * n),
        out_shape=[pltpu.HBM(a.shape, a.dtype) for a in srcs] + [pltpu.HBM(a.shape, a.dtype) for a in lands],
        input_output_aliases={i: i for i in range(2 * n)},
        compiler_params=pltpu.CompilerParams(has_side_effects=pltpu.SideEffectType.DATAFLOW_SIDE_EFFECTING),
    )(*srcs, *lands, send_sems, recv_sems, *after)
    return list(res[:n]), list(res[n:])


def _pair_sum(send, got, core, name):
    _, r, c = send.shape
    tr = 256 if r % 256 == 0 else r

    def body(core_ref, s_ref, g_ref, o_ref):
        o_ref[0] = (s_ref[0].astype(F32) + g_ref[0].astype(F32)).astype(o_ref.dtype)

    return pl.pallas_call(
        body, name=name,
        grid_spec=pltpu.PrefetchScalarGridSpec(
            num_scalar_prefetch=1, grid=(N_DEV // 2, r // tr),
            in_specs=[pl.BlockSpec((1, tr, c), lambda kc, i, core_ref: (2 * kc + core_ref[0], i, 0)),
                      pl.BlockSpec((1, tr, c), lambda kc, i, core_ref: (kc, i, 0))],
            out_specs=pl.BlockSpec((1, tr, c), lambda kc, i, core_ref: (kc, i, 0))),
        out_shape=jax.ShapeDtypeStruct((N_DEV // 2, r, c), send.dtype),
        compiler_params=_cp(("parallel", "parallel")))(core, send, got)


def _ada_mod(c_all, ada_w, ada_b_cols, name):
    nl, d, wc = ada_w.shape

    def body(c_ref, w_ref, b_ref, ca_ref, mod_ref):
        cv = c_ref[...]
        ca = cv * _sigmoid(cv)
        ca_ref[...] = ca
        mod_ref[0] = _dotf(ca, w_ref[0]) + b_ref[0]

    return pl.pallas_call(
        body, grid=(nl,),
        in_specs=[pl.BlockSpec((N_DEV, d), lambda l: (0, 0)), pl.BlockSpec((1, d, wc), lambda l: (l, 0, 0)),
                  pl.BlockSpec((1, 1, wc), lambda l: (l, 0, 0))],
        out_specs=[pl.BlockSpec((N_DEV, d), lambda l: (0, 0)), pl.BlockSpec((1, N_DEV, wc), lambda l: (l, 0, 0))],
        out_shape=[jax.ShapeDtypeStruct((N_DEV, d), F32), jax.ShapeDtypeStruct((nl, N_DEV, wc), F32)],
        name=name, compiler_params=_cp(("arbitrary",)))(c_all, ada_w, ada_b_cols)


def _adam(w, g, m, v):
    m2 = ADAM_B1 * m + (1.0 - ADAM_B1) * g
    v2 = ADAM_B2 * v + (1.0 - ADAM_B2) * (g * g)
    m_hat = m2 / (1.0 - ADAM_B1 ** ADAM_STEP)
    v_hat = v2 / (1.0 - ADAM_B2 ** ADAM_STEP)
    delta = -ADAM_LR * (m_hat / (jnp.sqrt(v_hat) + ADAM_EPS) + ADAM_WD * w)
    return delta, m2, v2


def _ada_grad_adam(c_act, d_mod, w, m, v, name):
    nl, d, wc = w.shape
    tk = min(512, d)

    def body(c_ref, dm_ref, w_ref, m_ref, v_ref, g_ref, dl_ref, m2_ref, v2_ref):
        g = _dotf_tn(c_ref[...], dm_ref[0])
        delta, m2, v2 = _adam(w_ref[0], g, m_ref[0], v_ref[0])
        g_ref[0], dl_ref[0], m2_ref[0], v2_ref[0] = g, delta, m2, v2

    blk = pl.BlockSpec((1, tk, wc), lambda l, i: (l, i, 0))
    shp = jax.ShapeDtypeStruct(w.shape, F32)
    return pl.pallas_call(
        body, grid=(nl, d // tk),
        in_specs=[pl.BlockSpec((N_DEV, tk), lambda l, i: (0, i)), pl.BlockSpec((1, N_DEV, wc), lambda l, i: (l, 0, 0)),
                  blk, blk, blk],
        out_specs=[blk] * 4, out_shape=[shp] * 4, name=name,
        compiler_params=_cp(("parallel", "parallel")))(c_act, d_mod, w, m, v)


def _adam_big(recv, w, m, v, layer, prev, name, after=()):
    nl, r, c = w.shape
    tr = 256 if r % 256 == 0 else r
    nparts = recv.shape[0]

    def body(rc_ref, w_ref, m_ref, v_ref, *rest):
        g_ref, dl_ref, m2_ref, v2_ref = rest[-4:]
        g = rc_ref[0].astype(F32)
        for d in range(1, nparts):
            g = g + rc_ref[d].astype(F32)
        delta, m2, v2 = _adam(w_ref[0], g, m_ref[0], v_ref[0])
        g_ref[0], dl_ref[0], m2_ref[0], v2_ref[0] = g, delta, m2, v2

    blk = pl.BlockSpec((1, tr, c), lambda i: (layer, i, 0))
    shp = jax.ShapeDtypeStruct(w.shape, F32)
    prev = () if prev is None else tuple(prev)
    return pl.pallas_call(
        body, grid=(r // tr,),
        in_specs=[pl.BlockSpec((nparts, tr, c), lambda i: (0, i, 0)), blk, blk, blk]
        + [pl.BlockSpec(memory_space=pl.ANY)] * (len(prev) + len(after)),
        out_specs=[blk] * 4, out_shape=[shp] * 4, name=name,
        input_output_aliases={4 + j: j for j in range(len(prev))},
        compiler_params=_cp(("parallel",)))(recv, w, m, v, *prev, *after)


def _sum_devices(gathered, name):
    _, r, c = gathered.shape

    def body(g_ref, o_ref):
        acc = g_ref[0]
        for d in range(1, N_DEV):
            acc = acc + g_ref[d]
        o_ref[...] = acc

    spec = pl.BlockSpec(memory_space=pltpu.VMEM)
    return pl.pallas_call(body, in_specs=[spec], out_specs=spec, out_shape=jax.ShapeDtypeStruct((r, c), F32),
                          name=name, compiler_params=pltpu.CompilerParams(vmem_limit_bytes=VMEM_LIMIT))(gathered)


def _adam_small(ws, gs, ms, vs, name):
    n = len(ws)

    def body(*refs):
        for i in range(n):
            w_ref, g_ref, m_ref, v_ref = (refs[k * n + i] for k in range(4))
            dl_ref, m2_ref, v2_ref = (refs[(4 + k) * n + i] for k in range(3))
            dl_ref[...], m2_ref[...], v2_ref[...] = _adam(w_ref[...], g_ref[...], m_ref[...], v_ref[...])

    spec = pl.BlockSpec(memory_space=pltpu.VMEM)
    shapes = [jax.ShapeDtypeStruct(w.shape, F32) for w in ws]
    res = pl.pallas_call(body, in_specs=[spec] * (4 * n), out_specs=[spec] * (3 * n), out_shape=shapes * 3, name=name,
                         compiler_params=pltpu.CompilerParams(vmem_limit_bytes=VMEM_LIMIT))(*ws, *gs, *ms, *vs)
    return res[:n], res[n:2 * n], res[2 * n:]


def _pack(parts):
    flat = jnp.concatenate([p.reshape(-1).astype(F32) for p in parts])
    assert flat.shape[0] % LANES == 0, flat.shape
    return flat.reshape(-1, LANES)


def _unpack(packed, shapes):
    flat = packed.reshape(-1)
    out, off = [], 0
    for shp in shapes:
        size = 1
        for dim in shp:
            size *= dim
        out.append(flat[off:off + size].reshape(shp))
        off += size
    return out


def _gather_cols(g, per):
    g = jnp.moveaxis(g, 0, -2)
    return g.reshape(g.shape[:-2] + (N_DEV * per,))


def _scatter_cols(g, per):
    return jnp.moveaxis(g.reshape(g.shape[:-1] + (N_DEV, per)), -2, 0)


def _my_cols(full, me, per):
    return lax.dynamic_slice_in_dim(full, me * per, per, axis=full.ndim - 1)


def kernel(x, c, positions, ada_w, ada_b, norm_g, w_in, gla_wg_f, gla_bg_f, gla_wg_b, gla_bg_b, gla_norm_g, mla_q_norm_g, mla_kv_norm_g, mla_w_uq, mla_w_ukv, mla_out_g, conv_w, conv_out_g, w_out, final_g, loss_target, m_ada_w, m_ada_b, m_norm_g, m_w_in, m_gla_wg_f, m_gla_bg_f, m_gla_wg_b, m_gla_bg_b, m_gla_norm_g, m_mla_q_norm_g, m_mla_kv_norm_g, m_mla_w_uq, m_mla_w_ukv, m_mla_out_g, m_conv_w, m_conv_out_g, m_w_out, m_final_g, v_ada_w, v_ada_b, v_norm_g, v_w_in, v_gla_wg_f, v_gla_bg_f, v_gla_wg_b, v_gla_bg_b, v_gla_norm_g, v_mla_q_norm_g, v_mla_kv_norm_g, v_mla_w_uq, v_mla_w_ukv, v_mla_out_g, v_conv_w, v_conv_out_g, v_w_out, v_final_g):
    me = 4 * lax.axis_index("x") + 2 * lax.axis_index("y") + lax.axis_index("c")
    nl = ada_w.shape[0]
    s, d = x.shape[1], x.shape[2]
    ada_cols = ada_w.shape[2]
    wgc, cwc = gla_wg_f.shape[2], conv_w.shape[2]

    (g0,) = _exchange([_pack([c, gla_wg_f, gla_wg_b, conv_w])], "gather_small_in", False, pltpu.VMEM)
    g0 = g0.reshape(N_DEV, -1)
    o1, o2, o3 = d, d + gla_wg_f.size, d + 2 * gla_wg_f.size
    c_all = g0[:, :o1]
    wgf_full = _gather_cols(g0[:, o1:o2].reshape((N_DEV,) + gla_wg_f.shape), wgc)
    wgb_full = _gather_cols(g0[:, o2:o3].reshape((N_DEV,) + gla_wg_b.shape), wgc)
    convw_full = _gather_cols(g0[:, o3:].reshape((N_DEV,) + conv_w.shape), cwc)

    ada_b_cols = _my_cols(ada_b, me, ada_cols).reshape(nl, 1, ada_cols)
    c_act, mod_cols = _ada_mod(c_all, ada_w, ada_b_cols, "ada_mod")
    (g1,) = _exchange([_pack([mod_cols])], "gather_mod", False, pltpu.VMEM)
    mod_all = g1.reshape(N_DEV, nl, N_DEV, ada_cols)
    mod_mine = _gather_cols(lax.dynamic_index_in_dim(mod_all, me, axis=2, keepdims=False), ada_cols)

    inv_freq = ROPE_THETA ** (-jnp.arange(0, MROPE, 2, dtype=F32) / MROPE)
    ang = positions[0].astype(F32)[:, None] * inv_freq
    cos, sin = jnp.tile(jnp.cos(ang), (1, LANES * 2 // MROPE)), jnp.tile(jnp.sin(ang), (1, LANES * 2 // MROPE))

    big = [w_in, w_out, mla_w_uq, mla_w_ukv]
    big_names = ["w_in", "w_out", "mla_w_uq", "mla_w_ukv"]

    def local_blocks(l):
        return [w[l].astype(MXU) for w in big]

    def put_own(lands, own):
        return [lax.dynamic_update_index_in_dim(ld, o, me, 0) for ld, o in zip(lands, own)]

    def layer_weights(l, gw_in=None, gw_out=None, gw_uq=None, gw_ukv=None):
        small = dict(norm_g=norm_g[l], gla_wg_f=wgf_full[l], gla_bg_f=gla_bg_f[l], gla_wg_b=wgb_full[l],
                     gla_bg_b=gla_bg_b[l], gla_norm_g=gla_norm_g[l], mla_q_norm_g=mla_q_norm_g[l],
                     mla_kv_norm_g=mla_kv_norm_g[l], mla_out_g=mla_out_g[l], conv_w=convw_full[l],
                     conv_out_g=conv_out_g[l])
        return _prep_layer_weights(
            None if gw_in is None else _perm_gathered(gw_in, IN_SEGS, PW),
            None if gw_out is None else gw_out.reshape((-1,) + gw_out.shape[2:]),
            None if gw_uq is None else _perm_gathered(gw_uq, UQ_SEGS, MQW),
            None if gw_ukv is None else _gather_cols(gw_ukv, mla_w_ukv.shape[2]), small)

    def land_shapes(blocks, slots):
        return [jax.ShapeDtypeStruct((slots,) + b.shape, b.dtype) for b in blocks]

    def slots_of(blocks):
        return [(N_DEV,) + b.shape for b in blocks]

    def forwarded(lands, blocks, tag):
        return put_own(_exchange_hbm(AG_FORWARD, None, lands, f"gather_{tag}_forward"), blocks)

    first = local_blocks(0)
    w_in_start = _plan_start(AG_SPREAD, first[:1], slots_of(first[:1]), mod_mine, "gather_w_in_l0_start")
    adam_w_in = [a + w_in_start[-1][0, 0] for a in (w_in, m_w_in, v_w_in)]
    (gw_in,) = forwarded(*reversed(_plan_wait(AG_SPREAD, w_in_start, adam_w_in, "gather_w_in_l0_wait")), "w_in_l0")
    rest = _plan_start(AG_SPREAD, first[1:], slots_of(first[1:]), gw_in, "gather_rest_l0_start")
    h = x[0]
    saved, layers, mods = [], [], []
    pending = {}
    for l in range(nl):
        shift, scale, gate = (mod_mine[l, i * d:(i + 1) * d].reshape(1, d) for i in range(3))
        nxt = local_blocks(l + 1) if l + 1 < nl else None

        def start_next(after, wt_late, l=l, nxt=nxt):
            if nxt is not None:
                pending[l + 1] = _plan_start(AG_SPREAD, nxt, slots_of(nxt), after, f"gather_weights_l{l + 1}_start")
                wt_late["q_norm_g"] = layers[l]["q_norm_g"] + pending[l + 1][-1][0, 0]
            return wt_late

        if l == 0:
            in_after = (rest[-1],)
            layers.append(layer_weights(0, gw_in))

            def late(proj):
                got = forwarded(*reversed(_plan_wait(AG_SPREAD, rest, [proj], "gather_rest_l0_wait")), "rest_l0")
                full = layer_weights(0, None, *got)
                return start_next(got[0], {k: full[k] for k in ("w_out", "w_uq", "w_ukv")})
        else:
            got = forwarded(*reversed(_plan_wait(AG_SPREAD, pending.pop(l), [h], f"gather_weights_l{l}_wait")), f"weights_l{l}")
            layers.append(layer_weights(l, *got))
            in_after = ()

            def late(proj):
                return start_next(proj, {})
        mods.append((shift, scale, gate))
        h, sv = _layer_fwd(h, mods[l], layers[l], cos, sin, f"l{l}", late, in_after)
        saved.append(sv)
        blocks = nxt
    loss_part, d_h, d_final_g = _final_loss(h, final_g.reshape(1, d), loss_target[0], "final_loss")
    loss = lax.psum(loss_part[0, 0], ("x", "y", "c"))
    shift, scale, gate = mods[-1]
    mods[-1] = (shift, scale, gate + 0.0 * loss)

    send_of = dict(w_in=lambda g: _scatter_perm(g, IN_SEGS, w_in.shape[2]),
                   w_out=lambda g: g.reshape((N_DEV,) + w_out.shape[1:]),
                   w_uq=lambda g: _scatter_perm(g, UQ_SEGS, mla_w_uq.shape[2]),
                   w_ukv=lambda g: _scatter_cols(g, mla_w_ukv.shape[2]))

    def grad_sends(gr):
        return [send_of[k](g).astype(MXU) for k, g in gr.items()]

    my_chip = me // 2
    my_core = (me % 2).astype(jnp.int32).reshape(1)

    def chip_sums(gr, tag):
        sends = grad_sends(gr)
        got = _exchange_hbm(RS_PAIR, sends, land_shapes([sd[0] for sd in sends], N_DEV // 2), f"scatter_grads_{tag}_pair")
        return [_pair_sum(sd, gt, my_core, f"pair_sum_{k}_{tag}") for sd, gt, k in zip(sends, got, gr)]

    def with_own_chip(lands, sums):
        return [lax.dynamic_update_index_in_dim(ld, lax.dynamic_index_in_dim(sm, my_chip, axis=0, keepdims=False),
                                                my_chip, 0) for ld, sm in zip(lands, sums)]

    small_names = ["norm_g", "gla_wg_f", "gla_bg_f", "gla_wg_b", "gla_bg_b", "gla_norm_g", "mla_q_norm_g",
                   "mla_kv_norm_g", "mla_out_g", "conv_w", "conv_out_g"]
    d_mods, grads, recv = [None] * nl, [None] * nl, [None] * nl
    flight = {}
    small = {}

    def gather_small(d_x, d_mod0, gr0):
        d_mods[0], grads[0] = d_mod0, _natural_small(gr0)
        d_mod_mine = jnp.stack([jnp.concatenate(d_mods[l], axis=-1)[0] for l in range(nl)])
        parts = [d_mod_mine] + [jnp.stack([grads[l][n] for l in range(nl)]) for n in small_names] + [d_final_g]
        (g2,) = _exchange([_pack(parts)], "gather_small_grads", False, pltpu.VMEM)
        small["d_mod_all"] = g2.reshape(N_DEV, -1)[:, :d_mod_mine.size].reshape(N_DEV, nl, 3 * d)
        small["summed"] = dict(zip(["ada_b"] + small_names + ["final_g"],
                                   _unpack(_sum_devices(g2, "sum_small_grads"), [p.shape for p in parts])))
        return (g2,)

    pairs = {}
    def end_flight(key, after, name):
        sm, lands = _plan_wait(RS_CHIPS, flight.pop(key)[0], after, name)
        return with_own_chip(lands, sm)

    for l in reversed(range(nl)):
        def ship(big_grads, l=l):
            if l > 0:
                sends = grad_sends(big_grads)
                pairs[l] = (_plan_start(RS_PAIR, sends, [(N_DEV // 2,) + sd.shape[1:] for sd in sends],
                                        big_grads["w_in"], f"scatter_grads_l{l}_pair_start"), sends)
                return (pairs[l][0][-1],)
            sm = chip_sums(dict(w_in=big_grads["w_in"]), f"l{l}")
            flight[l] = (_plan_start(RS_CHIPS, sm, [a.shape for a in sm], big_grads["w_in"], f"scatter_grads_l{l}_start"),
                         sm)
            return (flight[l][0][-1],)

        def ship_rest(rest_grads, l=l):
            if l + 1 in flight:
                recv[l + 1] = end_flight(l + 1, list(rest_grads.values()), f"scatter_grads_l{l + 1}_wait")
            sm = chip_sums(rest_grads, f"l{l}_rest")
            flight["rest"] = (_plan_start(RS_CHIPS, sm, [a.shape for a in sm], rest_grads["w_out"],
                                          f"scatter_grads_l{l}_rest_start"), sm)
            return flight["rest"][0][-1][0, 0]

        shift, scale, gate = mods[l]
        if l + 1 in flight:
            gate = gate + flight[l + 1][0][-1][0, 0]
        if l > 0:
            d_h, d_mods[l], gr = _layer_bwd(d_h, saved[l], (shift, scale, gate), layers[l], cos, sin, f"l{l}", ship)
            grads[l] = _natural_small(gr)
            sends, got = _plan_wait(RS_PAIR, pairs.pop(l)[0], [d_h], f"scatter_grads_l{l}_pair_wait")
            sm = [_pair_sum(sd, gt, my_core, f"pair_sum_{n}_l{l}") for sd, gt, n in zip(sends, got, big_names)]
            flight[l] = (_plan_start(RS_CHIPS, sm, [a.shape for a in sm], d_h, f"scatter_grads_l{l}_start"), sm)
        else:
            d_h, _, _ = _layer_bwd(d_h, saved[l], (shift, scale, gate), layers[l], cos, sin, f"l{l}", ship, gather_small,
                                   ship_rest)
    pending = flight[0][0]
    grad_x = d_h[None]
    summed = small["summed"]
    summed["gla_wg_f"] = _my_cols(summed["gla_wg_f"], me, wgc)
    summed["gla_wg_b"] = _my_cols(summed["gla_wg_b"], me, wgc)
    summed["conv_w"] = _my_cols(summed["conv_w"], me, cwc)

    d_mod_cols = jnp.moveaxis(_my_cols(small["d_mod_all"], me, ada_cols), 0, 1) + pending[-1][0, 0]
    out = {}
    out["ada_w"] = _ada_grad_adam(c_act, d_mod_cols, ada_w, m_ada_w, v_ada_w, "ada_grad_adam")

    given = dict(ada_b=(ada_b, m_ada_b, v_ada_b), norm_g=(norm_g, m_norm_g, v_norm_g),
                 gla_wg_f=(gla_wg_f, m_gla_wg_f, v_gla_wg_f), gla_bg_f=(gla_bg_f, m_gla_bg_f, v_gla_bg_f),
                 gla_wg_b=(gla_wg_b, m_gla_wg_b, v_gla_wg_b), gla_bg_b=(gla_bg_b, m_gla_bg_b, v_gla_bg_b),
                 gla_norm_g=(gla_norm_g, m_gla_norm_g, v_gla_norm_g),
                 mla_q_norm_g=(mla_q_norm_g, m_mla_q_norm_g, v_mla_q_norm_g),
                 mla_kv_norm_g=(mla_kv_norm_g, m_mla_kv_norm_g, v_mla_kv_norm_g),
                 mla_out_g=(mla_out_g, m_mla_out_g, v_mla_out_g), conv_w=(conv_w, m_conv_w, v_conv_w),
                 conv_out_g=(conv_out_g, m_conv_out_g, v_conv_out_g), final_g=(final_g, m_final_g, v_final_g))
    names = list(given)

    def two_d(a):
        return a.reshape(1, -1) if a.ndim == 1 else a

    g_nat = [summed[n].reshape(given[n][0].shape) for n in names]
    res = _adam_small([two_d(given[n][0]) for n in names], [two_d(g) for g in g_nat],
                      [two_d(given[n][1]) for n in names], [two_d(given[n][2]) for n in names], "adam_small")
    for i, n in enumerate(names):
        out[n] = (g_nat[i],) + tuple(r[i].reshape(given[n][0].shape) for r in res)

    state = dict(w_in=adam_w_in, w_out=(w_out, m_w_out, v_w_out), mla_w_uq=(mla_w_uq, m_mla_w_uq, v_mla_w_uq),
                 mla_w_ukv=(mla_w_ukv, m_mla_w_ukv, v_mla_w_ukv))
    done = [out["ada_w"][0], res[0][0]]
    for l in reversed(range(nl)):
        if l == 0:
            rest = end_flight("rest", done, "scatter_grads_l0_rest_wait")
            recv[0] = end_flight(0, done + rest[:1], "scatter_grads_l0_wait") + rest
        for i, n in enumerate(big_names):
            out[n] = _adam_big(recv[l][i], *state[n], l, out.get(n), f"adam_{n}_l{l}",
                               (pending[-1],))
        done = done + [out[n][0] for n in big_names]

    order = ["ada_w", "ada_b", "norm_g", "w_in", "gla_wg_f", "gla_bg_f", "gla_wg_b", "gla_bg_b", "gla_norm_g",
             "mla_q_norm_g", "mla_kv_norm_g", "mla_w_uq", "mla_w_ukv", "mla_out_g", "conv_w", "conv_out_g", "w_out",
             "final_g"]
    return (loss, grad_x, *[out[n][0] for n in order], *[out[n][1] for n in order], *[out[n][2] for n in order],
            *[out[n][3] for n in order])
```

```python
import functools

import jax
import jax.numpy as jnp
from jax import lax
from jax.experimental import pallas as pl
from jax.experimental.pallas import tpu as pltpu

F32 = jnp.float32
MXU = jnp.bfloat16
HI = lax.Precision.HIGHEST
N_DEV = 8
MESH = pl.DeviceIdType.MESH

D_MIX = 2048
GH, GDK, GDV = 6, 64, 128
GW = GH * GDV
GQK = GH * GDK
GRANK = 16
GTEMP = 16.0
CHUNK = 64
MH, MQL, MKVL, MNOPE, MROPE, MDV = 6, 384, 256, 128, 64, 128
MW = MH * MDV
MQW = MH * (MNOPE + MROPE)
MKVW = MH * (MNOPE + MDV)
CONV_CH = 512
ROPE_THETA = 10000.0
EPS = 1e-6
IN_DIM = 5856
OZ, OCB, OCC, OCX, OMKV, OGV, OGQ, OGK, OMQ, OT = 0, 2048, 2560, 3072, 3584, 3840, 4608, 4992, 5376, 5760
PW = 5888
LANES = 128
V7X_VMEM_BYTES = 64 * 1024 * 1024
VMEM_LIMIT = V7X_VMEM_BYTES * 7 // 8

ADAM_LR, ADAM_B1, ADAM_B2, ADAM_EPS, ADAM_WD, ADAM_STEP = 0.001, 0.9, 0.999, 1e-08, 0.01, 10


def _cp(sem=None):
    return pltpu.CompilerParams(dimension_semantics=sem, vmem_limit_bytes=VMEM_LIMIT)


def _dot(a, b):
    return jnp.dot(a.astype(MXU), b.astype(MXU), preferred_element_type=F32)


def _dot_nt(a, b):
    return lax.dot_general(a.astype(MXU), b.astype(MXU), (((1,), (1,)), ((), ())), preferred_element_type=F32)


def _dot_tn(a, b):
    return lax.dot_general(a.astype(MXU), b.astype(MXU), (((0,), (0,)), ((), ())), preferred_element_type=F32)


def _dotf(a, b):
    return jnp.dot(a, b, precision=HI, preferred_element_type=F32)


def _dotf_tn(a, b):
    return lax.dot_general(a, b, (((0,), (0,)), ((), ())), precision=HI, preferred_element_type=F32)


def _split3(x):
    hi = x.astype(jnp.bfloat16)
    r1 = x - hi.astype(F32)
    mid = r1.astype(jnp.bfloat16)
    lo = (r1 - mid.astype(F32)).astype(jnp.bfloat16)
    return hi, mid, lo


def _cum_dot(cum, x, transpose=False):
    dn = (((0,), (0,)), ((), ())) if transpose else (((1,), (0,)), ((), ()))
    cb = cum.astype(jnp.bfloat16)
    parts = [lax.dot_general(cb, p, dn, preferred_element_type=F32) for p in _split3(x)]
    return parts[0] + parts[1] + parts[2]


def _rows(s):
    return min(256, s)


def _rms(x, g):
    r = lax.rsqrt(jnp.mean(x * x, axis=-1, keepdims=True) + EPS)
    return x * r * g


def _rms_bwd(dy, x, g):
    r = lax.rsqrt(jnp.mean(x * x, axis=-1, keepdims=True) + EPS)
    xh = x * r
    dxh = dy * g
    dg = jnp.sum(dy * xh, axis=0, keepdims=True)
    dx = r * (dxh - xh * jnp.mean(dxh * xh, axis=-1, keepdims=True))
    return dx, dg


def _sigmoid(z):
    return jax.nn.sigmoid(z)


def _matmul(a, b, *, dims, tm, tn, tk, out_dtypes, name, epilogue=None, extras=(), extra_kinds=(), after=()):
    if dims == "nn":
        (m, k), n, mul = a.shape, b.shape[1], _dot
    elif dims == "nt":
        (m, k), n, mul = a.shape, b.shape[0], _dot_nt
    else:
        (k, m), n, mul = a.shape, b.shape[1], _dot_tn
    tm, tn, tk = min(tm, m), min(tn, n), min(tk, k)
    assert m % tm == 0 and n % tn == 0 and k % tk == 0, (m, n, k, tm, tn, tk)
    if dims == "nn":
        a_spec = pl.BlockSpec((tm, tk), lambda i, j, kk: (i, kk))
        b_spec = pl.BlockSpec((tk, tn), lambda i, j, kk: (kk, j))
    elif dims == "nt":
        a_spec = pl.BlockSpec((tm, tk), lambda i, j, kk: (i, kk))
        b_spec = pl.BlockSpec((tn, tk), lambda i, j, kk: (j, kk))
    else:
        a_spec = pl.BlockSpec((tk, tm), lambda i, j, kk: (kk, i))
        b_spec = pl.BlockSpec((tk, tn), lambda i, j, kk: (kk, j))
    nk = k // tk
    n_extra = len(extras)
    n_out = len(out_dtypes)
    n_after = len(after)
    extra_specs = []
    for kind in extra_kinds:
        if kind == "mn":
            extra_specs.append(pl.BlockSpec((tm, tn), lambda i, j, kk: (i, j)))
        else:
            extra_specs.append(pl.BlockSpec((1, tn), lambda i, j, kk: (0, j)))

    def finish(res, ex, outs):
        vals = (res,) if epilogue is None else epilogue(res, *[e[...] for e in ex])
        for o, v in zip(outs, vals):
            o[...] = v.astype(o.dtype)

    def body(*refs):
        a_ref, b_ref = refs[0], refs[1]
        ex = refs[2:2 + n_extra]
        outs = refs[2 + n_extra + n_after:2 + n_extra + n_after + n_out]
        if nk == 1:
            finish(mul(a_ref[...], b_ref[...]), ex, outs)
            return
        acc = refs[-1]
        kk = pl.program_id(2)

        @pl.when(kk == 0)
        def _():
            acc[...] = jnp.zeros_like(acc)

        acc[...] += mul(a_ref[...], b_ref[...])

        @pl.when(kk == nk - 1)
        def _():
            finish(acc[...], ex, outs)

    out_spec = pl.BlockSpec((tm, tn), lambda i, j, kk: (i, j))
    res = pl.pallas_call(
        body, grid=(m // tm, n // tn, nk),
        in_specs=[a_spec, b_spec] + extra_specs + [pl.BlockSpec(memory_space=pl.ANY)] * n_after,
        out_specs=[out_spec] * n_out,
        out_shape=[jax.ShapeDtypeStruct((m, n), dt) for dt in out_dtypes],
        scratch_shapes=[] if nk == 1 else [pltpu.VMEM((tm, tn), F32)],
        name=name, compiler_params=_cp(("parallel", "parallel", "arbitrary")),
    )(a, b, *extras, *after)
    return res


def _norm_mod(x, g, scale, shift, name):
    s, d = x.shape
    tr = _rows(s)

    def body(x_ref, g_ref, sc_ref, sh_ref, h_ref):
        h = _rms(x_ref[...], g_ref[...]) * (1.0 + sc_ref[...]) + sh_ref[...]
        h_ref[...] = h.astype(h_ref.dtype)

    row = pl.BlockSpec((tr, d), lambda i: (i, 0))
    vec = pl.BlockSpec((1, d), lambda i: (0, 0))
    return pl.pallas_call(body, grid=(s // tr,), in_specs=[row, vec, vec, vec], out_specs=row,
                          out_shape=jax.ShapeDtypeStruct((s, d), MXU), name=name,
                          compiler_params=_cp(("parallel",)))(x, g, scale, shift)


def _norm_mod_bwd(d_h, x, d_out, g, scale, name):
    s, d = x.shape
    tr = _rows(s)

    def body(dh_ref, x_ref, do_ref, g_ref, sc_ref, dx_ref, dsh_ref, dsc_ref, dg_ref):
        i = pl.program_id(0)

        @pl.when(i == 0)
        def _():
            dsh_ref[...] = jnp.zeros_like(dsh_ref)
            dsc_ref[...] = jnp.zeros_like(dsc_ref)
            dg_ref[...] = jnp.zeros_like(dg_ref)

        dh = dh_ref[...]
        xv = x_ref[...]
        gv = g_ref[...]
        r = lax.rsqrt(jnp.mean(xv * xv, axis=-1, keepdims=True) + EPS)
        xh = xv * r
        dsh_ref[...] += jnp.sum(dh, axis=0, keepdims=True)
        dsc_ref[...] += jnp.sum(dh * (xh * gv), axis=0, keepdims=True)
        dhn = dh * (1.0 + sc_ref[...])
        dg_ref[...] += jnp.sum(dhn * xh, axis=0, keepdims=True)
        dxh = dhn * gv
        dx_ref[...] = do_ref[...] + r * (dxh - xh * jnp.mean(dxh * xh, axis=-1, keepdims=True))

    row = pl.BlockSpec((tr, d), lambda i: (i, 0))
    vec = pl.BlockSpec((1, d), lambda i: (0, 0))
    vshape = jax.ShapeDtypeStruct((1, d), F32)
    return pl.pallas_call(body, grid=(s // tr,), in_specs=[row, row, row, vec, vec],
                          out_specs=[row, vec, vec, vec],
                          out_shape=[jax.ShapeDtypeStruct((s, d), F32), vshape, vshape, vshape],
                          name=name, compiler_params=_cp(("arbitrary",)))(d_h, x, d_out, g, scale)


def _gate_bwd(d_out, u, gate, name):
    s, d = d_out.shape
    tr = _rows(s)

    def body(do_ref, u_ref, gt_ref, du_ref, dgt_ref):
        @pl.when(pl.program_id(0) == 0)
        def _():
            dgt_ref[...] = jnp.zeros_like(dgt_ref)

        do = do_ref[...]
        du_ref[...] = (do * gt_ref[...]).astype(du_ref.dtype)
        dgt_ref[...] += jnp.sum(do * u_ref[...], axis=0, keepdims=True)

    row = pl.BlockSpec((tr, d), lambda i: (i, 0))
    vec = pl.BlockSpec((1, d), lambda i: (0, 0))
    return pl.pallas_call(body, grid=(s // tr,), in_specs=[row, row, vec], out_specs=[row, vec],
                          out_shape=[jax.ShapeDtypeStruct((s, d), MXU), jax.ShapeDtypeStruct((1, d), F32)],
                          name=name, compiler_params=_cp(("arbitrary",)))(d_out, u, gate)


def _final_loss(x, g, target, name):
    s, d = x.shape
    tr = _rows(s)

    def body(x_ref, g_ref, t_ref, loss_ref, dx_ref, dg_ref):
        @pl.when(pl.program_id(0) == 0)
        def _():
            loss_ref[...] = jnp.zeros_like(loss_ref)
            dg_ref[...] = jnp.zeros_like(dg_ref)

        xv = x_ref[...]
        gv = g_ref[...]
        diff = _rms(xv, gv) - t_ref[...]
        part = 0.5 * jnp.sum(jnp.sum(diff * diff, axis=-1, keepdims=True) / d, axis=0, keepdims=True)
        loss_ref[...] += jnp.broadcast_to(part, loss_ref.shape)
        dx, dg = _rms_bwd(diff / d, xv, gv)
        dx_ref[...] = dx
        dg_ref[...] += dg

    row = pl.BlockSpec((tr, d), lambda i: (i, 0))
    vec = pl.BlockSpec((1, d), lambda i: (0, 0))
    lvec = pl.BlockSpec((1, LANES), lambda i: (0, 0))
    return pl.pallas_call(body, grid=(s // tr,), in_specs=[row, vec, row], out_specs=[lvec, row, vec],
                          out_shape=[jax.ShapeDtypeStruct((1, LANES), F32), jax.ShapeDtypeStruct((s, d), F32),
                                     jax.ShapeDtypeStruct((1, d), F32)],
                          name=name, compiler_params=_cp(("arbitrary",)))(x, g, target)


def _shift_rows(u, s, down):
    ri = lax.broadcasted_iota(jnp.int32, u.shape, 0)
    if down:
        return jnp.where(ri == 0, 0.0, pltpu.roll(u, 1, 0))
    return jnp.where(ri == s - 1, 0.0, pltpu.roll(u, s - 1, 0))


def _conv_fwd(proj, conv_w, name):
    s = proj.shape[0]
    nt = CONV_CH // LANES

    def body(cb_ref, cc_ref, cx_ref, w_ref, pre_ref):
        u = cc_ref[...] * cx_ref[...]
        conv = _shift_rows(u, s, True) * w_ref[0:1, :] + u * w_ref[1:2, :] + _shift_rows(u, s, False) * w_ref[2:3, :]
        pre_ref[...] = cb_ref[...] * conv

    def col(off):
        return pl.BlockSpec((s, LANES), lambda j: (0, off // LANES + j))

    return pl.pallas_call(body, grid=(nt,), in_specs=[col(OCB), col(OCC), col(OCX), pl.BlockSpec((3, LANES), lambda j: (0, j))],
                          out_specs=pl.BlockSpec((s, LANES), lambda j: (0, j)),
                          out_shape=jax.ShapeDtypeStruct((s, CONV_CH), F32), name=name,
                          compiler_params=_cp(("parallel",)))(proj, proj, proj, conv_w)


def _conv_bwd(proj, conv_w, d_pre, name):
    s = proj.shape[0]
    nt = CONV_CH // LANES

    def body(cb_ref, cc_ref, cx_ref, w_ref, dp_ref, dcb_ref, dcc_ref, dcx_ref, dw_ref):
        cc, cx = cc_ref[...], cx_ref[...]
        u = cc * cx
        up, dn = _shift_rows(u, s, True), _shift_rows(u, s, False)
        w0, w1, w2 = w_ref[0:1, :], w_ref[1:2, :], w_ref[2:3, :]
        conv = up * w0 + u * w1 + dn * w2
        dp = dp_ref[...]
        dcb_ref[...] = dp * conv
        dconv = dp * cb_ref[...]
        du = _shift_rows(dconv, s, False) * w0 + dconv * w1 + _shift_rows(dconv, s, True) * w2
        dcc_ref[...] = du * cx
        dcx_ref[...] = du * cc
        dw_ref[0:1, :] = jnp.sum(dconv * up, axis=0, keepdims=True)
        dw_ref[1:2, :] = jnp.sum(dconv * u, axis=0, keepdims=True)
        dw_ref[2:3, :] = jnp.sum(dconv * dn, axis=0, keepdims=True)

    def col(off):
        return pl.BlockSpec((s, LANES), lambda j: (0, off // LANES + j))

    blk = pl.BlockSpec((s, LANES), lambda j: (0, j))
    wblk = pl.BlockSpec((3, LANES), lambda j: (0, j))
    full = jax.ShapeDtypeStruct((s, CONV_CH), F32)
    return pl.pallas_call(body, grid=(nt,), in_specs=[col(OCB), col(OCC), col(OCX), wblk, blk],
                          out_specs=[blk, blk, blk, wblk],
                          out_shape=[full, full, full, jax.ShapeDtypeStruct((3, CONV_CH), F32)],
                          name=name, compiler_params=_cp(("parallel",)))(proj, proj, proj, conv_w, d_pre)


GLA_SUB = 8


def _gla_gates(t_ref, wg_ref, bg_ref):
    t = t_ref[...]
    a = _dot(t, wg_ref[...]) + bg_ref[...]
    la = (jnp.minimum(a, 0.0) - jnp.log(1.0 + jnp.exp(-jnp.abs(a)))) / GTEMP
    return t, a, la


def _gla_masks(reverse):
    ri = lax.broadcasted_iota(jnp.int32, (CHUNK, CHUNK), 0)
    ci = lax.broadcasted_iota(jnp.int32, (CHUNK, CHUNK), 1)
    if reverse:
        cum, mask, mask_t = ci >= ri, ci > ri, ri > ci
    else:
        cum, mask, mask_t = ci <= ri, ci <= ri, ri <= ci
    return cum.astype(F32), mask, mask_t


def _gla_specs(s, reverse):
    nsub = min(GLA_SUB, s // CHUNK)
    nsteps = s // (CHUNK * nsub)

    def row(n):
        return nsteps - 1 - n if reverse else n

    def chunk(pi):
        return nsub - 1 - pi if reverse else pi

    return nsub, nsteps, row, chunk


def _gla_fwd(proj, wg_pad, bg, reverse, name):
    s = proj.shape[0]
    nsub, nsteps, row, chunk = _gla_specs(s, reverse)
    rb = nsub * CHUNK

    def body(q_ref, k_ref, v_ref, t_ref, wg_ref, bg_ref, o_ref, st_ref, state):
        @pl.when(pl.program_id(0) == 0)
        def _():
            state[...] = jnp.zeros_like(state)

        _, _, la = _gla_gates(t_ref, wg_ref, bg_ref)
        cumf, mask, _ = _gla_masks(reverse)
        lane = lax.broadcasted_iota(jnp.int32, (CHUNK, LANES), 1)
        for pi in range(nsub):
            rows = slice(chunk(pi) * CHUNK, (chunk(pi) + 1) * CHUNK)
            la_c = la[rows]
            b_all = _cum_dot(cumf, la_c)
            bl_all = jnp.sum(la_c, axis=0, keepdims=True)
            for p in range(GH // 2):
                sl = slice(p * LANES, (p + 1) * LANES)
                b, bl = b_all[:, sl], bl_all[:, sl]
                qd = q_ref[rows, sl] * (GDK ** -0.5) * jnp.exp(b)
                ki = k_ref[rows, sl] * jnp.exp(-b)
                kte = k_ref[rows, sl] * jnp.exp(bl - b)
                decay = jnp.exp(bl)
                for half in range(2):
                    h = 2 * p + half
                    lm = (lane < GDK) if half == 0 else (lane >= GDK)
                    qd_h = jnp.where(lm, qd, 0.0)
                    kte_h = jnp.where(lm, kte, 0.0)
                    v_h = v_ref[rows, h * GDV:(h + 1) * GDV]
                    st = state[h]
                    a_mat = jnp.where(mask, _dot_nt(qd_h, ki), 0.0)
                    o_ref[rows, h * GDV:(h + 1) * GDV] = _dot(a_mat, v_h) + _dot_nt(qd_h, st)
                    st_ref[pi, h] = st
                    state[h] = st * decay + _dot_tn(v_h, kte_h)

    return pl.pallas_call(
        body, grid=(nsteps,),
        in_specs=[pl.BlockSpec((rb, GQK), lambda n: (row(n), OGQ // GQK)),
                  pl.BlockSpec((rb, GQK), lambda n: (row(n), OGK // GQK)),
                  pl.BlockSpec((rb, GW), lambda n: (row(n), OGV // GW)),
                  pl.BlockSpec((rb, LANES), lambda n: (row(n), OT // LANES)),
                  pl.BlockSpec((LANES, GQK), lambda n: (0, 0)),
                  pl.BlockSpec((1, GQK), lambda n: (0, 0))],
        out_specs=[pl.BlockSpec((rb, GW), lambda n: (row(n), 0)),
                   pl.BlockSpec((nsub, GH, GDV, LANES), lambda n: (n, 0, 0, 0))],
        out_shape=[jax.ShapeDtypeStruct((s, GW), F32), jax.ShapeDtypeStruct((s // CHUNK, GH, GDV, LANES), F32)],
        scratch_shapes=[pltpu.VMEM((GH, GDV, LANES), F32)],
        name=name, compiler_params=_cp(("arbitrary",)))(proj, proj, proj, proj, wg_pad, bg)


def _gla_bwd(proj, wg_pad, bg, states, d_o, reverse, name):
    s = proj.shape[0]
    nsub, nsteps, row, chunk = _gla_specs(s, reverse)
    rb = nsub * CHUNK

    def body(q_ref, k_ref, v_ref, t_ref, wg_ref, bg_ref, st_ref, do_ref,
             dq_ref, dk_ref, dv_ref, dt_ref, dwg_ref, dbg_ref, dstate, da_buf):
        @pl.when(pl.program_id(0) == 0)
        def _():
            dstate[...] = jnp.zeros_like(dstate)
            dwg_ref[...] = jnp.zeros_like(dwg_ref)
            dbg_ref[...] = jnp.zeros_like(dbg_ref)

        t, a, la = _gla_gates(t_ref, wg_ref, bg_ref)
        cumf, mask, mask_t = _gla_masks(reverse)
        lane = lax.broadcasted_iota(jnp.int32, (CHUNK, LANES), 1)
        for pi in reversed(range(nsub)):
            rows = slice(chunk(pi) * CHUNK, (chunk(pi) + 1) * CHUNK)
            la_c = la[rows]
            b_all = _cum_dot(cumf, la_c)
            bl_all = jnp.sum(la_c, axis=0, keepdims=True)
            for p in range(GH // 2):
                sl = slice(p * LANES, (p + 1) * LANES)
                b, bl = b_all[:, sl], bl_all[:, sl]
                e, ei, ee = jnp.exp(b), jnp.exp(-b), jnp.exp(bl - b)
                qd = q_ref[rows, sl] * (GDK ** -0.5) * e
                ki, kte = k_ref[rows, sl] * ei, k_ref[rows, sl] * ee
                decay = jnp.exp(bl)
                dqd = jnp.zeros((CHUNK, LANES), F32)
                dki = jnp.zeros((CHUNK, LANES), F32)
                dkte = jnp.zeros((CHUNK, LANES), F32)
                ddecay = jnp.zeros((1, LANES), F32)
                for half in range(2):
                    h = 2 * p + half
                    lm = (lane < GDK) if half == 0 else (lane >= GDK)
                    qd_h = jnp.where(lm, qd, 0.0)
                    ki_h = jnp.where(lm, ki, 0.0)
                    kte_h = jnp.where(lm, kte, 0.0)
                    v_h = v_ref[rows, h * GDV:(h + 1) * GDV]
                    do_h = do_ref[rows, h * GDV:(h + 1) * GDV]
                    st = st_ref[pi, h]
                    dst = dstate[h]
                    at_mat = jnp.where(mask_t, _dot_nt(ki_h, qd_h), 0.0)
                    da_mat = jnp.where(mask, _dot_nt(do_h, v_h), 0.0)
                    dat_mat = jnp.where(mask_t, _dot_nt(v_h, do_h), 0.0)
                    dv_ref[rows, h * GDV:(h + 1) * GDV] = _dot(at_mat, do_h) + _dot_nt(kte_h, dst)
                    dqd += _dot(da_mat, ki_h) + _dot(do_h, st)
                    dki += _dot(dat_mat, qd_h)
                    dkte += _dot(v_h, dst)
                    ddecay += jnp.sum(dst * st, axis=0, keepdims=True)
                    dstate[h] = dst * decay + _dot_tn(do_h, qd_h)
                dq_ref[rows, sl] = dqd * e * (GDK ** -0.5)
                dk_ref[rows, sl] = dki * ei + dkte * ee
                db = dqd * qd - dki * ki - dkte * kte
                dbl = jnp.sum(dkte * kte, axis=0, keepdims=True) + decay * ddecay
                da_buf[rows, sl] = _cum_dot(cumf, db, True) + dbl
        da = da_buf[...] * (1.0 / GTEMP) * _sigmoid(-a)
        dt_ref[...] = _dot_nt(da, wg_ref[...])
        dwg_ref[...] += _dot_tn(t, da)
        dbg_ref[...] += jnp.sum(da, axis=0, keepdims=True)

    def prow(j):
        return row(nsteps - 1 - j)

    return pl.pallas_call(
        body, grid=(nsteps,),
        in_specs=[pl.BlockSpec((rb, GQK), lambda j: (prow(j), OGQ // GQK)),
                  pl.BlockSpec((rb, GQK), lambda j: (prow(j), OGK // GQK)),
                  pl.BlockSpec((rb, GW), lambda j: (prow(j), OGV // GW)),
                  pl.BlockSpec((rb, LANES), lambda j: (prow(j), OT // LANES)),
                  pl.BlockSpec((LANES, GQK), lambda j: (0, 0)),
                  pl.BlockSpec((1, GQK), lambda j: (0, 0)),
                  pl.BlockSpec((nsub, GH, GDV, LANES), lambda j: (nsteps - 1 - j, 0, 0, 0)),
                  pl.BlockSpec((rb, GW), lambda j: (prow(j), 0))],
        out_specs=[pl.BlockSpec((rb, GQK), lambda j: (prow(j), 0)),
                   pl.BlockSpec((rb, GQK), lambda j: (prow(j), 0)),
                   pl.BlockSpec((rb, GW), lambda j: (prow(j), 0)),
                   pl.BlockSpec((rb, LANES), lambda j: (prow(j), 0)),
                   pl.BlockSpec((LANES, GQK), lambda j: (0, 0)),
                   pl.BlockSpec((1, GQK), lambda j: (0, 0))],
        out_shape=[jax.ShapeDtypeStruct((s, GQK), F32), jax.ShapeDtypeStruct((s, GQK), F32),
                   jax.ShapeDtypeStruct((s, GW), F32), jax.ShapeDtypeStruct((s, LANES), F32),
                   jax.ShapeDtypeStruct((LANES, GQK), F32), jax.ShapeDtypeStruct((1, GQK), F32)],
        scratch_shapes=[pltpu.VMEM((GH, GDV, LANES), F32), pltpu.VMEM((rb, GQK), F32)],
        name=name, compiler_params=_cp(("arbitrary",)))(proj, proj, proj, proj, wg_pad, bg, states, d_o)


def _rot_half(x):
    lane = lax.broadcasted_iota(jnp.int32, x.shape, 1)
    first = (lane % MROPE) < (MROPE // 2)
    return jnp.where(first, -pltpu.roll(x, LANES - MROPE // 2, 1), pltpu.roll(x, MROPE // 2, 1))


def _mla_prep(proj, cos, sin, qg, kvg, w_uq, w_ukv, name):
    s = proj.shape[0]
    tr = _rows(s)

    def body(mq_ref, mkv_ref, t_ref, cos_ref, sin_ref, qg_ref, kvg_ref, wuq_ref, wukv_ref, q_ref, k_ref, v_ref):
        cosv, sinv = cos_ref[...], sin_ref[...]
        lane = lax.broadcasted_iota(jnp.int32, (tr, LANES), 1)

        def rope(xv):
            return xv * cosv + _rot_half(xv) * sinv

        qm = _dot(_rms(mq_ref[...], qg_ref[...]), wuq_ref[...])
        kv = _dot(_rms(mkv_ref[...], kvg_ref[...]), wukv_ref[...])
        kr_lo = jnp.where(lane < MROPE, rope(t_ref[...]), 0.0)
        kr_hi = pltpu.roll(kr_lo, MROPE, 1)
        for p in range(MH // 2):
            r = rope(qm[:, MW + p * LANES:MW + (p + 1) * LANES]).astype(q_ref.dtype)
            q_ref[2 * p, :, LANES:] = r
            q_ref[2 * p + 1, :, LANES:] = r
        for h in range(MH):
            q_ref[h, :, :LANES] = qm[:, h * LANES:(h + 1) * LANES].astype(q_ref.dtype)
            k_ref[h, :, :LANES] = kv[:, 2 * h * LANES:(2 * h + 1) * LANES].astype(k_ref.dtype)
            k_ref[h, :, LANES:] = (kr_lo if h % 2 == 0 else kr_hi).astype(k_ref.dtype)
            v_ref[h] = kv[:, (2 * h + 1) * LANES:(2 * h + 2) * LANES].astype(v_ref.dtype)

    def full(shape):
        return pl.BlockSpec(shape, lambda i: (0,) * len(shape))

    return pl.pallas_call(
        body, grid=(s // tr,),
        in_specs=[pl.BlockSpec((tr, MQL), lambda i: (i, OMQ // MQL)),
                  pl.BlockSpec((tr, MKVL), lambda i: (i, OMKV // MKVL)),
                  pl.BlockSpec((tr, LANES), lambda i: (i, OT // LANES)),
                  pl.BlockSpec((tr, LANES), lambda i: (i, 0)),
                  pl.BlockSpec((tr, LANES), lambda i: (i, 0)),
                  full((1, MQL)), full((1, MKVL)), full((MQL, MQW)), full((MKVL, MKVW))],
        out_specs=[pl.BlockSpec((MH, tr, 2 * LANES), lambda i: (0, i, 0)),
                   pl.BlockSpec((MH, tr, 2 * LANES), lambda i: (0, i, 0)),
                   pl.BlockSpec((MH, tr, LANES), lambda i: (0, i, 0))],
        out_shape=[jax.ShapeDtypeStruct((MH, s, 2 * LANES), MXU), jax.ShapeDtypeStruct((MH, s, 2 * LANES), MXU),
                   jax.ShapeDtypeStruct((MH, s, LANES), MXU)],
        name=name, compiler_params=_cp(("parallel",)))(proj, proj, proj, cos, sin, qg, kvg, w_uq, w_ukv)


def _mla_prep_bwd(proj, cos, sin, qg, kvg, w_uq, w_ukv, d_q, d_k, d_v, name):
    s = proj.shape[0]
    tr = _rows(s)

    def body(mq_ref, mkv_ref, cos_ref, sin_ref, qg_ref, kvg_ref, wuq_ref, wukv_ref, dq_ref, dk_ref, dv_ref,
             dmq_ref, dmkv_ref, dt_ref, dwuq_ref, dwukv_ref, dqg_ref, dkvg_ref):
        @pl.when(pl.program_id(0) == 0)
        def _():
            for r in (dwuq_ref, dwukv_ref, dqg_ref, dkvg_ref):
                r[...] = jnp.zeros_like(r)

        cosv, sinv = cos_ref[...], sin_ref[...]
        lane = lax.broadcasted_iota(jnp.int32, (tr, LANES), 1)
        lo = lane < MROPE

        def unrope(dv):
            return dv * cosv - _rot_half(dv * sinv)

        parts = [dq_ref[h, :, :LANES] for h in range(MH)]
        for p in range(MH // 2):
            parts.append(unrope(jnp.where(lo, dq_ref[2 * p, :, LANES:], dq_ref[2 * p + 1, :, LANES:])))
        d_qm = jnp.concatenate(parts, axis=1)
        mq, qgv = mq_ref[...], qg_ref[...]
        cq = _rms(mq, qgv)
        dwuq_ref[...] += _dot_tn(cq, d_qm)
        dmq, dqg = _rms_bwd(_dot_nt(d_qm, wuq_ref[...]), mq, qgv)
        dmq_ref[...] = dmq
        dqg_ref[...] += dqg

        parts = []
        for h in range(MH):
            parts += [dk_ref[h, :, :LANES], dv_ref[h]]
        d_kv = jnp.concatenate(parts, axis=1)
        mkv, kvgv = mkv_ref[...], kvg_ref[...]
        ckv = _rms(mkv, kvgv)
        dwukv_ref[...] += _dot_tn(ckv, d_kv)
        dmkv, dkvg = _rms_bwd(_dot_nt(d_kv, wukv_ref[...]), mkv, kvgv)
        dmkv_ref[...] = dmkv
        dkvg_ref[...] += dkvg

        even = dk_ref[0, :, LANES:] + dk_ref[2, :, LANES:] + dk_ref[4, :, LANES:]
        odd = dk_ref[1, :, LANES:] + dk_ref[3, :, LANES:] + dk_ref[5, :, LANES:]
        d_kr = jnp.where(lo, even, 0.0) + pltpu.roll(jnp.where(lo, 0.0, odd), MROPE, 1)
        dt_ref[...] = jnp.where(lo, unrope(d_kr), 0.0)

    def full(shape):
        return pl.BlockSpec(shape, lambda i: (0,) * len(shape))

    return pl.pallas_call(
        body, grid=(s // tr,),
        in_specs=[pl.BlockSpec((tr, MQL), lambda i: (i, OMQ // MQL)),
                  pl.BlockSpec((tr, MKVL), lambda i: (i, OMKV // MKVL)),
                  pl.BlockSpec((tr, LANES), lambda i: (i, 0)),
                  pl.BlockSpec((tr, LANES), lambda i: (i, 0)),
                  full((1, MQL)), full((1, MKVL)), full((MQL, MQW)), full((MKVL, MKVW)),
                  pl.BlockSpec((MH, tr, 2 * LANES), lambda i: (0, i, 0)),
                  pl.BlockSpec((MH, tr, 2 * LANES), lambda i: (0, i, 0)),
                  pl.BlockSpec((MH, tr, LANES), lambda i: (0, i, 0))],
        out_specs=[pl.BlockSpec((tr, MQL), lambda i: (i, 0)), pl.BlockSpec((tr, MKVL), lambda i: (i, 0)),
                   pl.BlockSpec((tr, LANES), lambda i: (i, 0)),
                   full((MQL, MQW)), full((MKVL, MKVW)), full((1, MQL)), full((1, MKVL))],
        out_shape=[jax.ShapeDtypeStruct((s, MQL), F32), jax.ShapeDtypeStruct((s, MKVL), F32),
                   jax.ShapeDtypeStruct((s, LANES), F32),
                   jax.ShapeDtypeStruct((MQL, MQW), F32), jax.ShapeDtypeStruct((MKVL, MKVW), F32),
                   jax.ShapeDtypeStruct((1, MQL), F32), jax.ShapeDtypeStruct((1, MKVL), F32)],
        name=name, compiler_params=_cp(("arbitrary",)))(proj, proj, cos, sin, qg, kvg, w_uq, w_ukv, d_q, d_k, d_v)


ATT_SCALE = (MNOPE + MROPE) ** -0.5
ATT_SCALE_LOG2 = ATT_SCALE * 1.4426950408889634
ATT_TQ_FWD, ATT_TQ = 256, 512


def _attn_fwd(q, k, v, name):
    s = q.shape[1]
    tq = min(ATT_TQ_FWD, s)

    def body(q_ref, k_ref, v_ref, o_ref, lse_ref):
        sc = _dot_nt(q_ref[0], k_ref[0])
        m = jnp.max(sc, axis=-1, keepdims=True)
        p = jnp.exp2((sc - m) * ATT_SCALE_LOG2)
        l = jnp.sum(p, axis=-1, keepdims=True)
        o_ref[...] = _dot(p, v_ref[0]) / l
        lse_ref[0] = m * ATT_SCALE_LOG2 + jnp.log2(l)

    return pl.pallas_call(
        body, grid=(MH, s // tq),
        in_specs=[pl.BlockSpec((1, tq, 2 * LANES), lambda h, i: (h, i, 0)),
                  pl.BlockSpec((1, s, 2 * LANES), lambda h, i: (h, 0, 0)),
                  pl.BlockSpec((1, s, LANES), lambda h, i: (h, 0, 0))],
        out_specs=[pl.BlockSpec((tq, LANES), lambda h, i: (i, h)),
                   pl.BlockSpec((1, tq, 1), lambda h, i: (h, i, 0))],
        out_shape=[jax.ShapeDtypeStruct((s, MW), F32), jax.ShapeDtypeStruct((MH, s, 1), F32)],
        name=name, compiler_params=_cp(("parallel", "parallel")))(q, k, v)


def _attn_bwd(q, k, v, o, lse, d_o, name):
    s = q.shape[1]
    tq = min(ATT_TQ, s)

    def body(q_ref, k_ref, v_ref, o_ref, lse_ref, do_ref, dq_ref, dk_ref, dv_ref):
        @pl.when(pl.program_id(1) == 0)
        def _():
            dk_ref[...] = jnp.zeros_like(dk_ref)
            dv_ref[...] = jnp.zeros_like(dv_ref)

        qv, kv, do = q_ref[0], k_ref[0], do_ref[...]
        p = jnp.exp2(_dot_nt(qv, kv) * ATT_SCALE_LOG2 - lse_ref[0])
        delta = jnp.sum(do * o_ref[...], axis=-1, keepdims=True)
        ds = p * (_dot_nt(do, v_ref[0]) - delta)
        dq_ref[0] = _dot(ds, kv) * ATT_SCALE
        dk_ref[0] += _dot_tn(ds, qv) * ATT_SCALE
        dv_ref[0] += _dot_tn(p, do)

    return pl.pallas_call(
        body, grid=(MH, s // tq),
        in_specs=[pl.BlockSpec((1, tq, 2 * LANES), lambda h, i: (h, i, 0)),
                  pl.BlockSpec((1, s, 2 * LANES), lambda h, i: (h, 0, 0)),
                  pl.BlockSpec((1, s, LANES), lambda h, i: (h, 0, 0)),
                  pl.BlockSpec((tq, LANES), lambda h, i: (i, h)),
                  pl.BlockSpec((1, tq, 1), lambda h, i: (h, i, 0)),
                  pl.BlockSpec((tq, LANES), lambda h, i: (i, h))],
        out_specs=[pl.BlockSpec((1, tq, 2 * LANES), lambda h, i: (h, i, 0)),
                   pl.BlockSpec((1, s, 2 * LANES), lambda h, i: (h, 0, 0)),
                   pl.BlockSpec((1, s, LANES), lambda h, i: (h, 0, 0))],
        out_shape=[jax.ShapeDtypeStruct((MH, s, 2 * LANES), F32), jax.ShapeDtypeStruct((MH, s, 2 * LANES), F32),
                   jax.ShapeDtypeStruct((MH, s, LANES), F32)],
        name=name, compiler_params=_cp(("parallel", "arbitrary")))(q, k, v, o, lse, d_o)


def _merge_fwd(o_f, o_b, o_att, pre, proj, gng, mog, cog, name):
    s = proj.shape[0]
    tr = _rows(s)

    def body(of_ref, ob_ref, oa_ref, pre_ref, z_ref, gng_ref, mog_ref, cog_ref, y_ref):
        z = z_ref[...]
        sz = z * _sigmoid(z)
        osum = of_ref[...] + ob_ref[...]
        gg = gng_ref[...]
        for h in range(GH):
            sl = slice(h * GDV, (h + 1) * GDV)
            y_ref[:, sl] = (_rms(osum[:, sl], gg) * sz[:, sl]).astype(y_ref.dtype)
        y_ref[:, GW:GW + MW] = (_rms(oa_ref[...], mog_ref[...]) * sz[:, GW:GW + MW]).astype(y_ref.dtype)
        y_ref[:, GW + MW:] = (_rms(pre_ref[...], cog_ref[...]) * sz[:, GW + MW:]).astype(y_ref.dtype)

    def row(w):
        return pl.BlockSpec((tr, w), lambda i: (i, 0))

    def vec(w):
        return pl.BlockSpec((1, w), lambda i: (0, 0))

    return pl.pallas_call(
        body, grid=(s // tr,),
        in_specs=[row(GW), row(GW), row(MW), row(CONV_CH), row(D_MIX), vec(GDV), vec(MW), vec(CONV_CH)],
        out_specs=row(D_MIX), out_shape=jax.ShapeDtypeStruct((s, D_MIX), MXU),
        name=name, compiler_params=_cp(("parallel",)))(o_f, o_b, o_att, pre, proj, gng, mog, cog)


def _merge_bwd(d_y, o_f, o_b, o_att, pre, proj, gng, mog, cog, name):
    s = proj.shape[0]
    tr = _rows(s)

    def body(dy_ref, of_ref, ob_ref, oa_ref, pre_ref, z_ref, gng_ref, mog_ref, cog_ref,
             dz_ref, dos_ref, doa_ref, dpre_ref, dgng_ref, dmog_ref, dcog_ref):
        @pl.when(pl.program_id(0) == 0)
        def _():
            for r in (dgng_ref, dmog_ref, dcog_ref):
                r[...] = jnp.zeros_like(r)

        z, dy = z_ref[...], dy_ref[...]
        sg = _sigmoid(z)
        sz = z * sg
        dsz = sg * (1.0 + z * (1.0 - sg))
        dcat = dy * sz
        dyz = dy * dsz
        osum = of_ref[...] + ob_ref[...]
        gg = gng_ref[...]
        dgg = jnp.zeros_like(gg)
        for h in range(GH):
            sl = slice(h * GDV, (h + 1) * GDV)
            dz_ref[:, sl] = dyz[:, sl] * _rms(osum[:, sl], gg)
            dx, dg = _rms_bwd(dcat[:, sl], osum[:, sl], gg)
            dos_ref[:, sl] = dx
            dgg += dg
        dgng_ref[...] += dgg
        sl = slice(GW, GW + MW)
        oa, mg = oa_ref[...], mog_ref[...]
        dz_ref[:, sl] = dyz[:, sl] * _rms(oa, mg)
        dx, dg = _rms_bwd(dcat[:, sl], oa, mg)
        doa_ref[...] = dx
        dmog_ref[...] += dg
        sl = slice(GW + MW, D_MIX)
        pv, cg = pre_ref[...], cog_ref[...]
        dz_ref[:, sl] = dyz[:, sl] * _rms(pv, cg)
        dx, dg = _rms_bwd(dcat[:, sl], pv, cg)
        dpre_ref[...] = dx
        dcog_ref[...] += dg

    def row(w):
        return pl.BlockSpec((tr, w), lambda i: (i, 0))

    def vec(w):
        return pl.BlockSpec((1, w), lambda i: (0, 0))

    def rs(w):
        return jax.ShapeDtypeStruct((s, w), F32)

    def vs(w):
        return jax.ShapeDtypeStruct((1, w), F32)

    return pl.pallas_call(
        body, grid=(s // tr,),
        in_specs=[row(D_MIX), row(GW), row(GW), row(MW), row(CONV_CH), row(D_MIX), vec(GDV), vec(MW), vec(CONV_CH)],
        out_specs=[row(D_MIX), row(GW), row(MW), row(CONV_CH), vec(GDV), vec(MW), vec(CONV_CH)],
        out_shape=[rs(D_MIX), rs(GW), rs(MW), rs(CONV_CH), vs(GDV), vs(MW), vs(CONV_CH)],
        name=name, compiler_params=_cp(("arbitrary",)))(d_y, o_f, o_b, o_att, pre, proj, gng, mog, cog)


def _assemble_dproj(d_z, d_cb, d_cc, d_cx, d_mkv, dv_f, dv_b, dq_f, dq_b, dk_f, dk_b, d_mq, dt_m, dt_f, dt_b, name):
    s = d_z.shape[0]
    tr = _rows(s)

    def body(dz, dcb, dcc, dcx, dmkv, dvf, dvb, dqf, dqb, dkf, dkb, dmq, dtm, dtf, dtb, out):
        dt = out.dtype
        out[:, OZ:OZ + D_MIX] = dz[...].astype(dt)
        out[:, OCB:OCB + CONV_CH] = dcb[...].astype(dt)
        out[:, OCC:OCC + CONV_CH] = dcc[...].astype(dt)
        out[:, OCX:OCX + CONV_CH] = dcx[...].astype(dt)
        out[:, OMKV:OMKV + MKVL] = dmkv[...].astype(dt)
        out[:, OGV:OGV + GW] = (dvf[...] + dvb[...]).astype(dt)
        out[:, OGQ:OGQ + GQK] = (dqf[...] + dqb[...]).astype(dt)
        out[:, OGK:OGK + GQK] = (dkf[...] + dkb[...]).astype(dt)
        out[:, OMQ:OMQ + MQL] = dmq[...].astype(dt)
        out[:, OT:OT + LANES] = (dtm[...] + dtf[...] + dtb[...]).astype(dt)

    args = (d_z, d_cb, d_cc, d_cx, d_mkv, dv_f, dv_b, dq_f, dq_b, dk_f, dk_b, d_mq, dt_m, dt_f, dt_b)
    return pl.pallas_call(
        body, grid=(s // tr,),
        in_specs=[pl.BlockSpec((tr, a.shape[1]), lambda i: (i, 0)) for a in args],
        out_specs=pl.BlockSpec((tr, PW), lambda i: (i, 0)),
        out_shape=jax.ShapeDtypeStruct((s, PW), MXU), name=name, compiler_params=_cp(("parallel",)))(*args)


def _layer_fwd(x, mod, wt, cos, sin, tag, late=None, in_after=()):
    shift, scale, gate = mod
    h = _norm_mod(x, wt["norm_g"], scale, shift, f"norm_mod_{tag}")
    (proj,) = _matmul(h, wt["w_in"], dims="nn", tm=2048, tn=256, tk=2048, out_dtypes=(F32,), name=f"in_proj_{tag}",
                      after=in_after)
    if late is not None:
        wt.update(late(proj))
    o_f, st_f = _gla_fwd(proj, wt["wg_pad_f"], wt["bg_f"], False, f"gla_fwd_f_{tag}")
    o_b, st_b = _gla_fwd(proj, wt["wg_pad_b"], wt["bg_b"], True, f"gla_fwd_b_{tag}")
    q, k, v = _mla_prep(proj, cos, sin, wt["q_norm_g"], wt["kv_norm_g"], wt["w_uq"], wt["w_ukv"], f"mla_prep_{tag}")
    o_att, lse = _attn_fwd(q, k, v, f"attn_fwd_{tag}")
    pre = _conv_fwd(proj, wt["conv_w"], f"conv_fwd_{tag}")
    y = _merge_fwd(o_f, o_b, o_att, pre, proj, wt["gla_norm_g"], wt["mla_out_g"], wt["conv_out_g"], f"merge_fwd_{tag}")
    x_new, u = _matmul(y, wt["w_out"], dims="nn", tm=2048, tn=256, tk=2048, out_dtypes=(F32, F32),
                       name=f"out_proj_{tag}", epilogue=lambda acc, xv, gv: (xv + gv * acc, acc),
                       extras=(x, gate), extra_kinds=("mn", "n"))
    saved = dict(x=x, h=h, proj=proj, o_f=o_f, o_b=o_b, st_f=st_f, st_b=st_b, q=q, k=k, v=v,
                 o_att=o_att, lse=lse, pre=pre, y=y, u=u)
    return x_new, saved


def _layer_bwd(d_out, sv, mod, wt, cos, sin, tag, ship=None, dx_first=None, ship_rest=None):
    shift, scale, gate = mod
    proj = sv["proj"]
    d_u, d_gate = _gate_bwd(d_out, sv["u"], gate, f"gate_bwd_{tag}")
    (g_w_out,) = _matmul(sv["y"], d_u, dims="tn", tm=1024, tn=512, tk=2048, out_dtypes=(MXU,), name=f"out_proj_dw_{tag}")
    (d_y,) = _matmul(d_u, wt["w_out"], dims="nt", tm=2048, tn=256, tk=2048, out_dtypes=(F32,), name=f"out_proj_dx_{tag}",
                     after=(g_w_out,))
    d_z, d_osum, d_oatt, d_pre, d_gng, d_mog, d_cog = _merge_bwd(
        d_y, sv["o_f"], sv["o_b"], sv["o_att"], sv["pre"], proj, wt["gla_norm_g"], wt["mla_out_g"], wt["conv_out_g"],
        f"merge_bwd_{tag}")
    d_cb, d_cc, d_cx, d_conv_w = _conv_bwd(proj, wt["conv_w"], d_pre, f"conv_bwd_{tag}")
    d_q, d_k, d_v = _attn_bwd(sv["q"], sv["k"], sv["v"], sv["o_att"], sv["lse"], d_oatt, f"attn_bwd_{tag}")
    d_mq, d_mkv, dt_m, g_w_uq, g_w_ukv, d_qg, d_kvg = _mla_prep_bwd(
        proj, cos, sin, wt["q_norm_g"], wt["kv_norm_g"], wt["w_uq"], wt["w_ukv"], d_q, d_k, d_v, f"mla_prep_bwd_{tag}")
    bg_f, bg_b = wt["bg_f"], wt["bg_b"]
    if ship_rest is not None:
        tok = ship_rest(dict(w_out=g_w_out, w_uq=g_w_uq, w_ukv=g_w_ukv))
        bg_f, bg_b = bg_f + tok, bg_b + tok
    dq_f, dk_f, dv_f, dt_f, d_wg_f, d_bg_f = _gla_bwd(proj, wt["wg_pad_f"], bg_f, sv["st_f"], d_osum, False,
                                                     f"gla_bwd_f_{tag}")
    dq_b, dk_b, dv_b, dt_b, d_wg_b, d_bg_b = _gla_bwd(proj, wt["wg_pad_b"], bg_b, sv["st_b"], d_osum, True,
                                                     f"gla_bwd_b_{tag}")
    d_proj = _assemble_dproj(d_z, d_cb, d_cc, d_cx, d_mkv, dv_f, dv_b, dq_f, dq_b, dk_f, dk_b, d_mq, dt_m, dt_f, dt_b,
                             f"assemble_dproj_{tag}")
    grads = dict(w_out=g_w_out, w_uq=g_w_uq, w_ukv=g_w_ukv,
                 wg_pad_f=d_wg_f, bg_f=d_bg_f, wg_pad_b=d_wg_b, bg_b=d_bg_b, gla_norm_g=d_gng,
                 q_norm_g=d_qg, kv_norm_g=d_kvg, mla_out_g=d_mog, conv_w=d_conv_w, conv_out_g=d_cog)

    def in_dw(after):
        (g_w_in,) = _matmul(sv["h"], d_proj, dims="tn", tm=2048, tn=256, tk=2048, out_dtypes=(MXU,),
                            name=f"in_proj_dw_{tag}", after=after)
        grads["w_in"] = g_w_in
        return dict(w_in=g_w_in, w_out=g_w_out, w_uq=g_w_uq, w_ukv=g_w_ukv)

    def in_dx(after):
        (d_h,) = _matmul(d_proj, wt["w_in"], dims="nt", tm=1024, tn=512, tk=PW, out_dtypes=(F32,),
                         name=f"in_proj_dx_{tag}", after=after)
        d_x, d_shift, d_scale, d_ng = _norm_mod_bwd(d_h, sv["x"], d_out, wt["norm_g"], scale, f"norm_mod_bwd_{tag}")
        grads["norm_g"] = d_ng
        return d_x, (d_shift, d_scale, d_gate)

    if dx_first is None:
        big = in_dw(())
        d_x, d_mod = in_dx((big["w_in"],) if ship is None else ship(big))
    else:
        d_x, d_mod = in_dx(())
        big = in_dw(dx_first(d_x, d_mod, grads))
        ship(big)
    return d_x, d_mod, grads


IN_SEGS = ((3808, 5856), (2272, 3808), (1952, 2208), (768, 1536), (0, 768), (1568, 1952), (2208, 2272), (1536, 1568))
UQ_SEGS = (tuple((h * (MNOPE + MROPE), h * (MNOPE + MROPE) + MNOPE) for h in range(MH))
           + tuple((h * (MNOPE + MROPE) + MNOPE, (h + 1) * (MNOPE + MROPE)) for h in range(MH)))


def _perm_gathered(g, segs, width):
    per = g.shape[-1]
    parts, total = [], 0
    for a, b in segs:
        c = a
        while c < b:
            j = c // per
            hi = min(b, (j + 1) * per)
            parts.append(g[j, :, c - j * per:hi - j * per])
            c = hi
        total += b - a
    if width > total:
        parts.append(jnp.zeros((g.shape[1], width - total), g.dtype))
    return jnp.concatenate(parts, axis=1)


def _scatter_perm(gp, segs, per):
    offs, o = [], 0
    for a, b in segs:
        offs.append((a, b, o))
        o += b - a
    blocks = []
    for j in range(N_DEV):
        lo, hi = j * per, (j + 1) * per
        pieces = []
        for a, b, o in sorted(offs):
            s0, s1 = max(a, lo), min(b, hi)
            if s0 < s1:
                pieces.append(gp[:, o + s0 - a:o + s1 - a])
        blocks.append(jnp.concatenate(pieces, axis=1))
    return jnp.stack(blocks)


def _prep_layer_weights(w_in, w_out, w_uq, w_ukv, small):
    def vec(v):
        return v.reshape(1, -1).astype(F32)

    zeros = functools.partial(jnp.zeros, dtype=F32)
    wg_f, wg_b = small["gla_wg_f"].astype(F32), small["gla_wg_b"].astype(F32)
    wg_pad_f = jnp.concatenate([zeros((MROPE, GQK)), wg_f, zeros((LANES - MROPE - GRANK, GQK))], axis=0)
    wg_pad_b = jnp.concatenate([zeros((MROPE + GRANK, GQK)), wg_b, zeros((LANES - MROPE - 2 * GRANK, GQK))], axis=0)
    wt = dict(norm_g=vec(small["norm_g"]), wg_pad_f=wg_pad_f, wg_pad_b=wg_pad_b,
              bg_f=vec(small["gla_bg_f"]), bg_b=vec(small["gla_bg_b"]), gla_norm_g=vec(small["gla_norm_g"]),
              q_norm_g=vec(small["mla_q_norm_g"]), kv_norm_g=vec(small["mla_kv_norm_g"]),
              mla_out_g=vec(small["mla_out_g"]), conv_w=small["conv_w"].astype(F32),
              conv_out_g=vec(small["conv_out_g"]))
    for name, w in (("w_in", w_in), ("w_out", w_out), ("w_uq", w_uq), ("w_ukv", w_ukv)):
        if w is not None:
            wt[name] = w.astype(MXU)
    return wt


def _natural_small(gr):
    return dict(norm_g=gr["norm_g"][0],
                gla_wg_f=gr["wg_pad_f"][MROPE:MROPE + GRANK], gla_bg_f=gr["bg_f"][0],
                gla_wg_b=gr["wg_pad_b"][MROPE + GRANK:MROPE + 2 * GRANK], gla_bg_b=gr["bg_b"][0],
                gla_norm_g=gr["gla_norm_g"][0], mla_q_norm_g=gr["q_norm_g"][0], mla_kv_norm_g=gr["kv_norm_g"][0],
                mla_out_g=gr["mla_out_g"][0], conv_w=gr["conv_w"], conv_out_g=gr["conv_out_g"][0])


def _exchange(arrs, name, scatter, space):
    n = len(arrs)

    def body(*refs):
        ins, outs = refs[:n], refs[n:2 * n]
        send_sems, recv_sems, loc_sems = refs[2 * n:]
        ax, ay, ac = lax.axis_index("x"), lax.axis_index("y"), lax.axis_index("c")
        me = 4 * ax + 2 * ay + ac

        def src(a, to):
            return ins[a].at[to] if scatter else ins[a]

        def remote(a, r, dst_slot):
            px = 1 - ax if r & 4 else ax
            py = 1 - ay if r & 2 else ay
            pc = 1 - ac if r & 1 else ac
            return pltpu.make_async_remote_copy(
                src_ref=src(a, 4 * px + 2 * py + pc), dst_ref=outs[a].at[dst_slot(4 * px + 2 * py + pc)],
                send_sem=send_sems.at[a, r - 1], recv_sem=recv_sems.at[a, r - 1],
                device_id=(px, py, pc), device_id_type=MESH)

        locs = [pltpu.make_async_copy(src(a, me), outs[a].at[me], loc_sems.at[a]) for a in range(n)]
        for cp in locs:
            cp.start()
        sends = [remote(a, r, lambda peer: me) for r in range(1, N_DEV) for a in range(n)]
        for cp in sends:
            cp.start()
        for r in range(1, N_DEV):
            for a in range(n):
                remote(a, r, lambda peer: peer).wait_recv()
        for cp in sends:
            cp.wait_send()
        for cp in locs:
            cp.wait()

    def out_shape(a):
        return jax.ShapeDtypeStruct(a.shape if scatter else (N_DEV,) + a.shape, a.dtype)

    spec = pl.BlockSpec(memory_space=space)
    return pl.pallas_call(
        body, in_specs=[spec] * n, out_specs=[spec] * n, out_shape=[out_shape(a) for a in arrs],
        scratch_shapes=[pltpu.SemaphoreType.DMA((n, N_DEV - 1)), pltpu.SemaphoreType.DMA((n, N_DEV - 1)),
                        pltpu.SemaphoreType.DMA((n,))],
        name=name, compiler_params=pltpu.CompilerParams(vmem_limit_bytes=VMEM_LIMIT))(*arrs)


def _peer(r):
    ax, ay, ac = lax.axis_index("x"), lax.axis_index("y"), lax.axis_index("c")
    px = 1 - ax if r & 4 else ax
    py = 1 - ay if r & 2 else ay
    pc = 1 - ac if r & 1 else ac
    return (px, py, pc), 4 * px + 2 * py + pc


def _slot(rel_div):
    rel, div = rel_div
    idx = _peer(rel)[1]
    return idx if div == 1 else idx // div


AG_SPREAD = tuple((r, None, (0, 1), (r, 1)) for r in (1, 2, 4, 6))
AG_FORWARD = tuple((1, (k, 1), (k, 1), (1 ^ k, 1)) for k in (2, 4, 6))
RS_PAIR = tuple((1, (1 ^ k, 1), (1 ^ k, 2), (k, 2)) for k in (0, 2, 4, 6))
RS_CHIPS = tuple((r, (r, 2), (0, 2), (r, 2)) for r in (2, 4, 6))


def _plan_copies(plan, n, src_refs, land_refs, send_sems, recv_sems, arriving):
    out = []
    for i, (r, src, dst, recv) in enumerate(plan):
        peer = _peer(r)[0]
        for a in range(n):
            out.append(pltpu.make_async_remote_copy(
                src_ref=src_refs[a] if src is None else src_refs[a].at[_slot(src)],
                dst_ref=land_refs[a].at[_slot(recv if arriving else dst)],
                send_sem=send_sems.at[i * n + a], recv_sem=recv_sems.at[i * n + a],
                device_id=peer, device_id_type=MESH))
    return out


def _exchange_hbm(plan, srcs, lands, name, after=()):
    n = len(lands)
    fresh = isinstance(lands[0], jax.ShapeDtypeStruct)
    ins = ([] if srcs is None else list(srcs)) + ([] if fresh else list(lands))
    ns = 0 if srcs is None else n
    n_data = len(ins)
    ins = ins + list(after)

    def body(*refs):
        outs = refs[len(ins):len(ins) + n]
        send_sems, recv_sems = refs[-2:]
        src_refs = refs[:n] if srcs is not None else refs[ns:ns + n]
        sends = _plan_copies(plan, n, src_refs, outs, send_sems, recv_sems, False)
        for cp in sends:
            cp.start()
        for cp in _plan_copies(plan, n, src_refs, outs, send_sems, recv_sems, True):
            cp.wait_recv()
        for cp in sends:
            cp.wait_send()

    hbm = pl.BlockSpec(memory_space=pltpu.HBM)
    k = len(plan) * n
    return pl.pallas_call(
        body, name=name, in_specs=[hbm] * n_data + [pl.BlockSpec(memory_space=pl.ANY)] * len(after), out_specs=[hbm] * n,
        out_shape=[jax.ShapeDtypeStruct(a.shape, a.dtype) for a in lands],
        scratch_shapes=[pltpu.SemaphoreType.DMA((k,)), pltpu.SemaphoreType.DMA((k,))],
        input_output_aliases={} if fresh else {ns + i: i for i in range(n)},
        compiler_params=pltpu.CompilerParams(vmem_limit_bytes=VMEM_LIMIT))(*ins)


def _plan_start(plan, srcs, land_shapes, after, name):
    n = len(srcs)

    def body(*refs):
        src_refs, land_refs = refs[:n], refs[n:2 * n]
        send_sems, recv_sems = refs[2 * n + 1], refs[2 * n + 2]
        for cp in _plan_copies(plan, n, src_refs, land_refs, send_sems, recv_sems, False):
            cp.start()
        refs[-1][...] = jnp.zeros_like(refs[-1])

    hbm = pl.BlockSpec(memory_space=pltpu.HBM)
    sem = pl.BlockSpec(memory_space=pltpu.SEMAPHORE)
    k = len(plan) * n
    srcs = [pltpu.with_memory_space_constraint(a, pltpu.HBM) for a in srcs]
    lands = [pltpu.with_memory_space_constraint(lax.empty(shp, a.dtype), pltpu.HBM) for shp, a in zip(land_shapes, srcs)]
    res = pl.pallas_call(
        body, name=name,
        in_specs=[hbm] * (2 * n) + [pl.BlockSpec(memory_space=pl.ANY)],
        out_specs=[sem, sem] + [hbm] * (2 * n) + [pl.BlockSpec(memory_space=pltpu.VMEM)],
        out_shape=[pltpu.SemaphoreType.DMA((k,)), pltpu.SemaphoreType.DMA((k,))]
        + [pltpu.HBM(a.shape, a.dtype) for a in srcs] + [pltpu.HBM(shp, a.dtype) for shp, a in zip(land_shapes, srcs)]
        + [jax.ShapeDtypeStruct((8, LANES), F32)],
        input_output_aliases={i: 2 + i for i in range(2 * n)},
        compiler_params=pltpu.CompilerParams(has_side_effects=pltpu.SideEffectType.DATAFLOW_SIDE_EFFECTING),
    )(*srcs, *lands, after)
    return res[0], res[1], list(res[2:2 + n]), list(res[2 + n:2 + 2 * n]), res[-1]


def _plan_wait(plan, handle, after, name):
    send_sems, recv_sems, srcs, lands, _ = handle
    n = len(srcs)
    after = list(after)

    def body(*refs):
        src_refs, land_refs = refs[:n], refs[n:2 * n]
        ssem, rsem = refs[2 * n], refs[2 * n + 1]
        for cp in _plan_copies(plan, n, src_refs, land_refs, ssem, rsem, False):
            cp.wait_send()
        for cp in _plan_copies(plan, n, src_refs, land_refs, ssem, rsem, True):
            cp.wait_recv()

    hbm = pl.BlockSpec(memory_space=pltpu.HBM)
    sem = pl.BlockSpec(memory_space=pltpu.SEMAPHORE)
    res = pl.pallas_call(
        body, name=name,
        in_specs=[hbm] * (2 * n) + [sem, sem] + [pl.BlockSpec(memory_space=pl.ANY)] * len(after),
        out_specs=[hbm] * (2 * n),
        out_shape=[pltpu.HBM(a.shape, a.dtype) for a in srcs] + [pltpu.HBM(a.shape, a.dtype) for a in lands],
        input_output_aliases={i: i for i in range(2 * n)},
        compiler_params=pltpu.CompilerParams(has_side_effects=pltpu.SideEffectType.DATAFLOW_SIDE_EFFECTING),
    )(*srcs, *lands, send_sems, recv_sems, *after)
    return list(res[:n]), list(res[n:])


def _pair_sum(send, got, core, name):
    _, r, c = send.shape
    tr = 256 if r % 256 == 0 else r

    def body(core_ref, s_ref, g_ref, o_ref):
        o_ref[0] = (s_ref[0].astype(F32) + g_ref[0].astype(F32)).astype(o_ref.dtype)

    return pl.pallas_call(
        body, name=name,
        grid_spec=pltpu.PrefetchScalarGridSpec(
            num_scalar_prefetch=1, grid=(N_DEV // 2, r // tr),
            in_specs=[pl.BlockSpec((1, tr, c), lambda kc, i, core_ref: (2 * kc + core_ref[0], i, 0)),
                      pl.BlockSpec((1, tr, c), lambda kc, i, core_ref: (kc, i, 0))],
            out_specs=pl.BlockSpec((1, tr, c), lambda kc, i, core_ref: (kc, i, 0))),
        out_shape=jax.ShapeDtypeStruct((N_DEV // 2, r, c), send.dtype),
        compiler_params=_cp(("parallel", "parallel")))(core, send, got)


def _ada_mod(c_all, ada_w, ada_b_cols, name):
    nl, d, wc = ada_w.shape

    def body(c_ref, w_ref, b_ref, ca_ref, mod_ref):
        cv = c_ref[...]
        ca = cv * _sigmoid(cv)
        ca_ref[...] = ca
        mod_ref[0] = _dotf(ca, w_ref[0]) + b_ref[0]

    return pl.pallas_call(
        body, grid=(nl,),
        in_specs=[pl.BlockSpec((N_DEV, d), lambda l: (0, 0)), pl.BlockSpec((1, d, wc), lambda l: (l, 0, 0)),
                  pl.BlockSpec((1, 1, wc), lambda l: (l, 0, 0))],
        out_specs=[pl.BlockSpec((N_DEV, d), lambda l: (0, 0)), pl.BlockSpec((1, N_DEV, wc), lambda l: (l, 0, 0))],
        out_shape=[jax.ShapeDtypeStruct((N_DEV, d), F32), jax.ShapeDtypeStruct((nl, N_DEV, wc), F32)],
        name=name, compiler_params=_cp(("arbitrary",)))(c_all, ada_w, ada_b_cols)


def _adam(w, g, m, v):
    m2 = ADAM_B1 * m + (1.0 - ADAM_B1) * g
    v2 = ADAM_B2 * v + (1.0 - ADAM_B2) * (g * g)
    m_hat = m2 / (1.0 - ADAM_B1 ** ADAM_STEP)
    v_hat = v2 / (1.0 - ADAM_B2 ** ADAM_STEP)
    delta = -ADAM_LR * (m_hat / (jnp.sqrt(v_hat) + ADAM_EPS) + ADAM_WD * w)
    return delta, m2, v2


def _ada_grad_adam(c_act, d_mod, w, m, v, name):
    nl, d, wc = w.shape
    tk = min(512, d)

    def body(c_ref, dm_ref, w_ref, m_ref, v_ref, g_ref, dl_ref, m2_ref, v2_ref):
        g = _dotf_tn(c_ref[...], dm_ref[0])
        delta, m2, v2 = _adam(w_ref[0], g, m_ref[0], v_ref[0])
        g_ref[0], dl_ref[0], m2_ref[0], v2_ref[0] = g, delta, m2, v2

    blk = pl.BlockSpec((1, tk, wc), lambda l, i: (l, i, 0))
    shp = jax.ShapeDtypeStruct(w.shape, F32)
    return pl.pallas_call(
        body, grid=(nl, d // tk),
        in_specs=[pl.BlockSpec((N_DEV, tk), lambda l, i: (0, i)), pl.BlockSpec((1, N_DEV, wc), lambda l, i: (l, 0, 0)),
                  blk, blk, blk],
        out_specs=[blk] * 4, out_shape=[shp] * 4, name=name,
        compiler_params=_cp(("parallel", "parallel")))(c_act, d_mod, w, m, v)


def _adam_big(recv, w, m, v, layer, prev, name, after=()):
    nl, r, c = w.shape
    tr = 256 if r % 256 == 0 else r
    nparts = recv.shape[0]

    def body(rc_ref, w_ref, m_ref, v_ref, *rest):
        g_ref, dl_ref, m2_ref, v2_ref = rest[-4:]
        g = rc_ref[0].astype(F32)
        for d in range(1, nparts):
            g = g + rc_ref[d].astype(F32)
        delta, m2, v2 = _adam(w_ref[0], g, m_ref[0], v_ref[0])
        g_ref[0], dl_ref[0], m2_ref[0], v2_ref[0] = g, delta, m2, v2

    blk = pl.BlockSpec((1, tr, c), lambda i: (layer, i, 0))
    shp = jax.ShapeDtypeStruct(w.shape, F32)
    prev = () if prev is None else tuple(prev)
    return pl.pallas_call(
        body, grid=(r // tr,),
        in_specs=[pl.BlockSpec((nparts, tr, c), lambda i: (0, i, 0)), blk, blk, blk]
        + [pl.BlockSpec(memory_space=pl.ANY)] * (len(prev) + len(after)),
        out_specs=[blk] * 4, out_shape=[shp] * 4, name=name,
        input_output_aliases={4 + j: j for j in range(len(prev))},
        compiler_params=_cp(("parallel",)))(recv, w, m, v, *prev, *after)


def _sum_devices(gathered, name):
    _, r, c = gathered.shape

    def body(g_ref, o_ref):
        acc = g_ref[0]
        for d in range(1, N_DEV):
            acc = acc + g_ref[d]
        o_ref[...] = acc

    spec = pl.BlockSpec(memory_space=pltpu.VMEM)
    return pl.pallas_call(body, in_specs=[spec], out_specs=spec, out_shape=jax.ShapeDtypeStruct((r, c), F32),
                          name=name, compiler_params=pltpu.CompilerParams(vmem_limit_bytes=VMEM_LIMIT))(gathered)


def _adam_small(ws, gs, ms, vs, name):
    n = len(ws)

    def body(*refs):
        for i in range(n):
            w_ref, g_ref, m_ref, v_ref = (refs[k * n + i] for k in range(4))
            dl_ref, m2_ref, v2_ref = (refs[(4 + k) * n + i] for k in range(3))
            dl_ref[...], m2_ref[...], v2_ref[...] = _adam(w_ref[...], g_ref[...], m_ref[...], v_ref[...])

    spec = pl.BlockSpec(memory_space=pltpu.VMEM)
    shapes = [jax.ShapeDtypeStruct(w.shape, F32) for w in ws]
    res = pl.pallas_call(body, in_specs=[spec] * (4 * n), out_specs=[spec] * (3 * n), out_shape=shapes * 3, name=name,
                         compiler_params=pltpu.CompilerParams(vmem_limit_bytes=VMEM_LIMIT))(*ws, *gs, *ms, *vs)
    return res[:n], res[n:2 * n], res[2 * n:]


def _pack(parts):
    flat = jnp.concatenate([p.reshape(-1).astype(F32) for p in parts])
    assert flat.shape[0] % LANES == 0, flat.shape
    return flat.reshape(-1, LANES)


def _unpack(packed, shapes):
    flat = packed.reshape(-1)
    out, off = [], 0
    for shp in shapes:
        size = 1
        for dim in shp:
            size *= dim
        out.append(flat[off:off + size].reshape(shp))
        off += size
    return out


def _gather_cols(g, per):
    g = jnp.moveaxis(g, 0, -2)
    return g.reshape(g.shape[:-2] + (N_DEV * per,))


def _scatter_cols(g, per):
    return jnp.moveaxis(g.reshape(g.shape[:-1] + (N_DEV, per)), -2, 0)


def _my_cols(full, me, per):
    return lax.dynamic_slice_in_dim(full, me * per, per, axis=full.ndim - 1)


def kernel(x, c, positions, ada_w, ada_b, norm_g, w_in, gla_wg_f, gla_bg_f, gla_wg_b, gla_bg_b, gla_norm_g, mla_q_norm_g, mla_kv_norm_g, mla_w_uq, mla_w_ukv, mla_out_g, conv_w, conv_out_g, w_out, final_g, loss_target, m_ada_w, m_ada_b, m_norm_g, m_w_in, m_gla_wg_f, m_gla_bg_f, m_gla_wg_b, m_gla_bg_b, m_gla_norm_g, m_mla_q_norm_g, m_mla_kv_norm_g, m_mla_w_uq, m_mla_w_ukv, m_mla_out_g, m_conv_w, m_conv_out_g, m_w_out, m_final_g, v_ada_w, v_ada_b, v_norm_g, v_w_in, v_gla_wg_f, v_gla_bg_f, v_gla_wg_b, v_gla_bg_b, v_gla_norm_g, v_mla_q_norm_g, v_mla_kv_norm_g, v_mla_w_uq, v_mla_w_ukv, v_mla_out_g, v_conv_w, v_conv_out_g, v_w_out, v_final_g):
    me = 4 * lax.axis_index("x") + 2 * lax.axis_index("y") + lax.axis_index("c")
    nl = ada_w.shape[0]
    s, d = x.shape[1], x.shape[2]
    ada_cols = ada_w.shape[2]
    wgc, cwc = gla_wg_f.shape[2], conv_w.shape[2]

    (g0,) = _exchange([_pack([c, gla_wg_f, gla_wg_b, conv_w])], "gather_small_in", False, pltpu.VMEM)
    g0 = g0.reshape(N_DEV, -1)
    o1, o2, o3 = d, d + gla_wg_f.size, d + 2 * gla_wg_f.size
    c_all = g0[:, :o1]
    wgf_full = _gather_cols(g0[:, o1:o2].reshape((N_DEV,) + gla_wg_f.shape), wgc)
    wgb_full = _gather_cols(g0[:, o2:o3].reshape((N_DEV,) + gla_wg_b.shape), wgc)
    convw_full = _gather_cols(g0[:, o3:].reshape((N_DEV,) + conv_w.shape), cwc)

    ada_b_cols = _my_cols(ada_b, me, ada_cols).reshape(nl, 1, ada_cols)
    c_act, mod_cols = _ada_mod(c_all, ada_w, ada_b_cols, "ada_mod")
    (g1,) = _exchange([_pack([mod_cols])], "gather_mod", False, pltpu.VMEM)
    mod_all = g1.reshape(N_DEV, nl, N_DEV, ada_cols)
    mod_mine = _gather_cols(lax.dynamic_index_in_dim(mod_all, me, axis=2, keepdims=False), ada_cols)

    inv_freq = ROPE_THETA ** (-jnp.arange(0, MROPE, 2, dtype=F32) / MROPE)
    ang = positions[0].astype(F32)[:, None] * inv_freq
    cos, sin = jnp.tile(jnp.cos(ang), (1, LANES * 2 // MROPE)), jnp.tile(jnp.sin(ang), (1, LANES * 2 // MROPE))

    big = [w_in, w_out, mla_w_uq, mla_w_ukv]
    big_names = ["w_in", "w_out", "mla_w_uq", "mla_w_ukv"]

    def local_blocks(l):
        return [w[l].astype(MXU) for w in big]

    def put_own(lands, own):
        return [lax.dynamic_update_index_in_dim(ld, o, me, 0) for ld, o in zip(lands, own)]

    def layer_weights(l, gw_in=None, gw_out=None, gw_uq=None, gw_ukv=None):
        small = dict(norm_g=norm_g[l], gla_wg_f=wgf_full[l], gla_bg_f=gla_bg_f[l], gla_wg_b=wgb_full[l],
                     gla_bg_b=gla_bg_b[l], gla_norm_g=gla_norm_g[l], mla_q_norm_g=mla_q_norm_g[l],
                     mla_kv_norm_g=mla_kv_norm_g[l], mla_out_g=mla_out_g[l], conv_w=convw_full[l],
                     conv_out_g=conv_out_g[l])
        return _prep_layer_weights(
            None if gw_in is None else _perm_gathered(gw_in, IN_SEGS, PW),
            None if gw_out is None else gw_out.reshape((-1,) + gw_out.shape[2:]),
            None if gw_uq is None else _perm_gathered(gw_uq, UQ_SEGS, MQW),
            None if gw_ukv is None else _gather_cols(gw_ukv, mla_w_ukv.shape[2]), small)

    def land_shapes(blocks, slots):
        return [jax.ShapeDtypeStruct((slots,) + b.shape, b.dtype) for b in blocks]

    def slots_of(blocks):
        return [(N_DEV,) + b.shape for b in blocks]

    def forwarded(lands, blocks, tag):
        return put_own(_exchange_hbm(AG_FORWARD, None, lands, f"gather_{tag}_forward"), blocks)

    first = local_blocks(0)
    w_in_start = _plan_start(AG_SPREAD, first[:1], slots_of(first[:1]), mod_mine, "gather_w_in_l0_start")
    adam_w_in = [a + w_in_start[-1][0, 0] for a in (w_in, m_w_in, v_w_in)]
    (gw_in,) = forwarded(*reversed(_plan_wait(AG_SPREAD, w_in_start, adam_w_in, "gather_w_in_l0_wait")), "w_in_l0")
    rest = _plan_start(AG_SPREAD, first[1:], slots_of(first[1:]), gw_in, "gather_rest_l0_start")
    h = x[0]
    saved, layers, mods = [], [], []
    pending = {}
    for l in range(nl):
        shift, scale, gate = (mod_mine[l, i * d:(i + 1) * d].reshape(1, d) for i in range(3))
        nxt = local_blocks(l + 1) if l + 1 < nl else None

        def start_next(after, wt_late, l=l, nxt=nxt):
            if nxt is not None:
                pending[l + 1] = _plan_start(AG_SPREAD, nxt, slots_of(nxt), after, f"gather_weights_l{l + 1}_start")
                wt_late["q_norm_g"] = layers[l]["q_norm_g"] + pending[l + 1][-1][0, 0]
            return wt_late

        if l == 0:
            in_after = (rest[-1],)
            layers.append(layer_weights(0, gw_in))

            def late(proj):
                got = forwarded(*reversed(_plan_wait(AG_SPREAD, rest, [proj], "gather_rest_l0_wait")), "rest_l0")
                full = layer_weights(0, None, *got)
                return start_next(got[0], {k: full[k] for k in ("w_out", "w_uq", "w_ukv")})
        else:
            got = forwarded(*reversed(_plan_wait(AG_SPREAD, pending.pop(l), [h], f"gather_weights_l{l}_wait")), f"weights_l{l}")
            layers.append(layer_weights(l, *got))
            in_after = ()

            def late(proj):
                return start_next(proj, {})
        mods.append((shift, scale, gate))
        h, sv = _layer_fwd(h, mods[l], layers[l], cos, sin, f"l{l}", late, in_after)
        saved.append(sv)
        blocks = nxt
    loss_part, d_h, d_final_g = _final_loss(h, final_g.reshape(1, d), loss_target[0], "final_loss")
    loss = lax.psum(loss_part[0, 0], ("x", "y", "c"))
    shift, scale, gate = mods[-1]
    mods[-1] = (shift, scale, gate + 0.0 * loss)

    send_of = dict(w_in=lambda g: _scatter_perm(g, IN_SEGS, w_in.shape[2]),
                   w_out=lambda g: g.reshape((N_DEV,) + w_out.shape[1:]),
                   w_uq=lambda g: _scatter_perm(g, UQ_SEGS, mla_w_uq.shape[2]),
                   w_ukv=lambda g: _scatter_cols(g, mla_w_ukv.shape[2]))

    def grad_sends(gr):
        return [send_of[k](g).astype(MXU) for k, g in gr.items()]

    my_chip = me // 2
    my_core = (me % 2).astype(jnp.int32).reshape(1)

    def chip_sums(gr, tag):
        sends = grad_sends(gr)
        got = _exchange_hbm(RS_PAIR, sends, land_shapes([sd[0] for sd in sends], N_DEV // 2), f"scatter_grads_{tag}_pair")
        return [_pair_sum(sd, gt, my_core, f"pair_sum_{k}_{tag}") for sd, gt, k in zip(sends, got, gr)]

    def with_own_chip(lands, sums):
        return [lax.dynamic_update_index_in_dim(ld, lax.dynamic_index_in_dim(sm, my_chip, axis=0, keepdims=False),
                                                my_chip, 0) for ld, sm in zip(lands, sums)]

    small_names = ["norm_g", "gla_wg_f", "gla_bg_f", "gla_wg_b", "gla_bg_b", "gla_norm_g", "mla_q_norm_g",
                   "mla_kv_norm_g", "mla_out_g", "conv_w", "conv_out_g"]
    d_mods, grads, recv = [None] * nl, [None] * nl, [None] * nl
    flight = {}
    small = {}

    def gather_small(d_x, d_mod0, gr0):
        d_mods[0], grads[0] = d_mod0, _natural_small(gr0)
        d_mod_mine = jnp.stack([jnp.concatenate(d_mods[l], axis=-1)[0] for l in range(nl)])
        parts = [d_mod_mine] + [jnp.stack([grads[l][n] for l in range(nl)]) for n in small_names] + [d_final_g]
        (g2,) = _exchange([_pack(parts)], "gather_small_grads", False, pltpu.VMEM)
        small["d_mod_all"] = g2.reshape(N_DEV, -1)[:, :d_mod_mine.size].reshape(N_DEV, nl, 3 * d)
        small["summed"] = dict(zip(["ada_b"] + small_names + ["final_g"],
                                   _unpack(_sum_devices(g2, "sum_small_grads"), [p.shape for p in parts])))
        return (g2,)

    pairs = {}
    def end_flight(key, after, name):
        sm, lands = _plan_wait(RS_CHIPS, flight.pop(key)[0], after, name)
        return with_own_chip(lands, sm)

    for l in reversed(range(nl)):
        def ship(big_grads, l=l):
            if l > 0:
                sends = grad_sends(big_grads)
                pairs[l] = (_plan_start(RS_PAIR, sends, [(N_DEV // 2,) + sd.shape[1:] for sd in sends],
                                        big_grads["w_in"], f"scatter_grads_l{l}_pair_start"), sends)
                return (pairs[l][0][-1],)
            sm = chip_sums(dict(w_in=big_grads["w_in"]), f"l{l}")
            flight[l] = (_plan_start(RS_CHIPS, sm, [a.shape for a in sm], big_grads["w_in"], f"scatter_grads_l{l}_start"),
                         sm)
            return (flight[l][0][-1],)

        def ship_rest(rest_grads, l=l):
            if l + 1 in flight:
                recv[l + 1] = end_flight(l + 1, list(rest_grads.values()), f"scatter_grads_l{l + 1}_wait")
            sm = chip_sums(rest_grads, f"l{l}_rest")
            flight["rest"] = (_plan_start(RS_CHIPS, sm, [a.shape for a in sm], rest_grads["w_out"],
                                          f"scatter_grads_l{l}_rest_start"), sm)
            return flight["rest"][0][-1][0, 0]

        shift, scale, gate = mods[l]
        if l + 1 in flight:
            gate = gate + flight[l + 1][0][-1][0, 0]
        if l > 0:
            d_h, d_mods[l], gr = _layer_bwd(d_h, saved[l], (shift, scale, gate), layers[l], cos, sin, f"l{l}", ship)
            grads[l] = _natural_small(gr)
            sends, got = _plan_wait(RS_PAIR, pairs.pop(l)[0], [d_h], f"scatter_grads_l{l}_pair_wait")
            sm = [_pair_sum(sd, gt, my_core, f"pair_sum_{n}_l{l}") for sd, gt, n in zip(sends, got, big_names)]
            flight[l] = (_plan_start(RS_CHIPS, sm, [a.shape for a in sm], d_h, f"scatter_grads_l{l}_start"), sm)
        else:
            d_h, _, _ = _layer_bwd(d_h, saved[l], (shift, scale, gate), layers[l], cos, sin, f"l{l}", ship, gather_small,
                                   ship_rest)
    pending = flight[0][0]
    grad_x = d_h[None]
    summed = small["summed"]
    summed["gla_wg_f"] = _my_cols(summed["gla_wg_f"], me, wgc)
    summed["gla_wg_b"] = _my_cols(summed["gla_wg_b"], me, wgc)
    summed["conv_w"] = _my_cols(summed["conv_w"], me, cwc)

    d_mod_cols = jnp.moveaxis(_my_cols(small["d_mod_all"], me, ada_cols), 0, 1) + pending[-1][0, 0]
    out = {}
    out["ada_w"] = _ada_grad_adam(c_act, d_mod_cols, ada_w, m_ada_w, v_ada_w, "ada_grad_adam")

    given = dict(ada_b=(ada_b, m_ada_b, v_ada_b), norm_g=(norm_g, m_norm_g, v_norm_g),
                 gla_wg_f=(gla_wg_f, m_gla_wg_f, v_gla_wg_f), gla_bg_f=(gla_bg_f, m_gla_bg_f, v_gla_bg_f),
                 gla_wg_b=(gla_wg_b, m_gla_wg_b, v_gla_wg_b), gla_bg_b=(gla_bg_b, m_gla_bg_b, v_gla_bg_b),
                 gla_norm_g=(gla_norm_g, m_gla_norm_g, v_gla_norm_g),
                 mla_q_norm_g=(mla_q_norm_g, m_mla_q_norm_g, v_mla_q_norm_g),
                 mla_kv_norm_g=(mla_kv_norm_g, m_mla_kv_norm_g, v_mla_kv_norm_g),
                 mla_out_g=(mla_out_g, m_mla_out_g, v_mla_out_g), conv_w=(conv_w, m_conv_w, v_conv_w),
                 conv_out_g=(conv_out_g, m_conv_out_g, v_conv_out_g), final_g=(final_g, m_final_g, v_final_g))
    names = list(given)

    def two_d(a):
        return a.reshape(1, -1) if a.ndim == 1 else a

    g_nat = [summed[n].reshape(given[n][0].shape) for n in names]
    res = _adam_small([two_d(given[n][0]) for n in names], [two_d(g) for g in g_nat],
                      [two_d(given[n][1]) for n in names], [two_d(given[n][2]) for n in names], "adam_small")
    for i, n in enumerate(names):
        out[n] = (g_nat[i],) + tuple(r[i].reshape(given[n][0].shape) for r in res)

    state = dict(w_in=adam_w_in, w_out=(w_out, m_w_out, v_w_out), mla_w_uq=(mla_w_uq, m_mla_w_uq, v_mla_w_uq),
                 mla_w_ukv=(mla_w_ukv, m_mla_w_ukv, v_mla_w_ukv))
    done = [out["ada_w"][0], res[0][0]]
    for l in reversed(range(nl)):
        if l == 0:
            rest = end_flight("rest", done, "scatter_grads_l0_rest_wait")
            recv[0] = end_flight(0, done + rest[:1], "scatter_grads_l0_wait") + rest
        for i, n in enumerate(big_names):
            out[n] = _adam_big(recv[l][i], *state[n], l, out.get(n), f"adam_{n}_l{l}",
                               (pending[-1],))
        done = done + [out[n][0] for n in big_names]

    order = ["ada_w", "ada_b", "norm_g", "w_in", "gla_wg_f", "gla_bg_f", "gla_wg_b", "gla_bg_b", "gla_norm_g",
             "mla_q_norm_g", "mla_kv_norm_g", "mla_w_uq", "mla_w_ukv", "mla_out_g", "conv_w", "conv_out_g", "w_out",
             "final_g"]
    return (loss, grad_x, *[out[n][0] for n in order], *[out[n][1] for n in order], *[out[n][2] for n in order],
            *[out[n][3] for n in order])
```

```python
import functools

import jax
import jax.numpy as jnp
from jax import lax
from jax.experimental import pallas as pl
from jax.experimental.pallas import tpu as pltpu

F32 = jnp.float32
MXU = jnp.bfloat16
HI = lax.Precision.HIGHEST
N_DEV = 8
MESH = pl.DeviceIdType.MESH

D_MIX = 2048
GH, GDK, GDV = 6, 64, 128
GW = GH * GDV
GQK = GH * GDK
GRANK = 16
GTEMP = 16.0
CHUNK = 64
MH, MQL, MKVL, MNOPE, MROPE, MDV = 6, 384, 256, 128, 64, 128
MW = MH * MDV
MQW = MH * (MNOPE + MROPE)
MKVW = MH * (MNOPE + MDV)
CONV_CH = 512
ROPE_THETA = 10000.0
EPS = 1e-6
IN_DIM = 5856
OZ, OCB, OCC, OCX, OMKV, OGV, OGQ, OGK, OMQ, OT = 0, 2048, 2560, 3072, 3584, 3840, 4608, 4992, 5376, 5760
PW = 5888
LANES = 128
V7X_VMEM_BYTES = 64 * 1024 * 1024
VMEM_LIMIT = V7X_VMEM_BYTES * 7 // 8

ADAM_LR, ADAM_B1, ADAM_B2, ADAM_EPS, ADAM_WD, ADAM_STEP = 0.001, 0.9, 0.999, 1e-08, 0.01, 10


def _cp(sem=None):
    return pltpu.CompilerParams(dimension_semantics=sem, vmem_limit_bytes=VMEM_LIMIT)


def _dot(a, b):
    return jnp.dot(a.astype(MXU), b.astype(MXU), preferred_element_type=F32)


def _dot_nt(a, b):
    return lax.dot_general(a.astype(MXU), b.astype(MXU), (((1,), (1,)), ((), ())), preferred_element_type=F32)


def _dot_tn(a, b):
    return lax.dot_general(a.astype(MXU), b.astype(MXU), (((0,), (0,)), ((), ())), preferred_element_type=F32)


def _dotf(a, b):
    return jnp.dot(a, b, precision=HI, preferred_element_type=F32)


def _dotf_tn(a, b):
    return lax.dot_general(a, b, (((0,), (0,)), ((), ())), precision=HI, preferred_element_type=F32)


def _split3(x):
    hi = x.astype(jnp.bfloat16)
    r1 = x - hi.astype(F32)
    mid = r1.astype(jnp.bfloat16)
    lo = (r1 - mid.astype(F32)).astype(jnp.bfloat16)
    return hi, mid, lo


def _cum_dot(cum, x, transpose=False):
    dn = (((0,), (0,)), ((), ())) if transpose else (((1,), (0,)), ((), ()))
    cb = cum.astype(jnp.bfloat16)
    parts = [lax.dot_general(cb, p, dn, preferred_element_type=F32) for p in _split3(x)]
    return parts[0] + parts[1] + parts[2]


def _rows(s):
    return min(256, s)


def _rms(x, g):
    r = lax.rsqrt(jnp.mean(x * x, axis=-1, keepdims=True) + EPS)
    return x * r * g


def _rms_bwd(dy, x, g):
    r = lax.rsqrt(jnp.mean(x * x, axis=-1, keepdims=True) + EPS)
    xh = x * r
    dxh = dy * g
    dg = jnp.sum(dy * xh, axis=0, keepdims=True)
    dx = r * (dxh - xh * jnp.mean(dxh * xh, axis=-1, keepdims=True))
    return dx, dg


def _sigmoid(z):
    return jax.nn.sigmoid(z)


def _matmul(a, b, *, dims, tm, tn, tk, out_dtypes, name, epilogue=None, extras=(), extra_kinds=(), after=()):
    if dims == "nn":
        (m, k), n, mul = a.shape, b.shape[1], _dot
    elif dims == "nt":
        (m, k), n, mul = a.shape, b.shape[0], _dot_nt
    else:
        (k, m), n, mul = a.shape, b.shape[1], _dot_tn
    tm, tn, tk = min(tm, m), min(tn, n), min(tk, k)
    assert m % tm == 0 and n % tn == 0 and k % tk == 0, (m, n, k, tm, tn, tk)
    if dims == "nn":
        a_spec = pl.BlockSpec((tm, tk), lambda i, j, kk: (i, kk))
        b_spec = pl.BlockSpec((tk, tn), lambda i, j, kk: (kk, j))
    elif dims == "nt":
        a_spec = pl.BlockSpec((tm, tk), lambda i, j, kk: (i, kk))
        b_spec = pl.BlockSpec((tn, tk), lambda i, j, kk: (j, kk))
    else:
        a_spec = pl.BlockSpec((tk, tm), lambda i, j, kk: (kk, i))
        b_spec = pl.BlockSpec((tk, tn), lambda i, j, kk: (kk, j))
    nk = k // tk
    n_extra = len(extras)
    n_out = len(out_dtypes)
    n_after = len(after)
    extra_specs = []
    for kind in extra_kinds:
        if kind == "mn":
            extra_specs.append(pl.BlockSpec((tm, tn), lambda i, j, kk: (i, j)))
        else:
            extra_specs.append(pl.BlockSpec((1, tn), lambda i, j, kk: (0, j)))

    def finish(res, ex, outs):
        vals = (res,) if epilogue is None else epilogue(res, *[e[...] for e in ex])
        for o, v in zip(outs, vals):
            o[...] = v.astype(o.dtype)

    def body(*refs):
        a_ref, b_ref = refs[0], refs[1]
        ex = refs[2:2 + n_extra]
        outs = refs[2 + n_extra + n_after:2 + n_extra + n_after + n_out]
        if nk == 1:
            finish(mul(a_ref[...], b_ref[...]), ex, outs)
            return
        acc = refs[-1]
        kk = pl.program_id(2)

        @pl.when(kk == 0)
        def _():
            acc[...] = jnp.zeros_like(acc)

        acc[...] += mul(a_ref[...], b_ref[...])

        @pl.when(kk == nk - 1)
        def _():
            finish(acc[...], ex, outs)

    out_spec = pl.BlockSpec((tm, tn), lambda i, j, kk: (i, j))
    res = pl.pallas_call(
        body, grid=(m // tm, n // tn, nk),
        in_specs=[a_spec, b_spec] + extra_specs + [pl.BlockSpec(memory_space=pl.ANY)] * n_after,
        out_specs=[out_spec] * n_out,
        out_shape=[jax.ShapeDtypeStruct((m, n), dt) for dt in out_dtypes],
        scratch_shapes=[] if nk == 1 else [pltpu.VMEM((tm, tn), F32)],
        name=name, compiler_params=_cp(("parallel", "parallel", "arbitrary")),
    )(a, b, *extras, *after)
    return res


def _norm_mod(x, g, scale, shift, name):
    s, d = x.shape
    tr = _rows(s)

    def body(x_ref, g_ref, sc_ref, sh_ref, h_ref):
        h = _rms(x_ref[...], g_ref[...]) * (1.0 + sc_ref[...]) + sh_ref[...]
        h_ref[...] = h.astype(h_ref.dtype)

    row = pl.BlockSpec((tr, d), lambda i: (i, 0))
    vec = pl.BlockSpec((1, d), lambda i: (0, 0))
    return pl.pallas_call(body, grid=(s // tr,), in_specs=[row, vec, vec, vec], out_specs=row,
                          out_shape=jax.ShapeDtypeStruct((s, d), MXU), name=name,
                          compiler_params=_cp(("parallel",)))(x, g, scale, shift)


def _norm_mod_bwd(d_h, x, d_out, g, scale, name):
    s, d = x.shape
    tr = _rows(s)

    def body(dh_ref, x_ref, do_ref, g_ref, sc_ref, dx_ref, dsh_ref, dsc_ref, dg_ref):
        i = pl.program_id(0)

        @pl.when(i == 0)
        def _():
            dsh_ref[...] = jnp.zeros_like(dsh_ref)
            dsc_ref[...] = jnp.zeros_like(dsc_ref)
            dg_ref[...] = jnp.zeros_like(dg_ref)

        dh = dh_ref[...]
        xv = x_ref[...]
        gv = g_ref[...]
        r = lax.rsqrt(jnp.mean(xv * xv, axis=-1, keepdims=True) + EPS)
        xh = xv * r
        dsh_ref[...] += jnp.sum(dh, axis=0, keepdims=True)
        dsc_ref[...] += jnp.sum(dh * (xh * gv), axis=0, keepdims=True)
        dhn = dh * (1.0 + sc_ref[...])
        dg_ref[...] += jnp.sum(dhn * xh, axis=0, keepdims=True)
        dxh = dhn * gv
        dx_ref[...] = do_ref[...] + r * (dxh - xh * jnp.mean(dxh * xh, axis=-1, keepdims=True))

    row = pl.BlockSpec((tr, d), lambda i: (i, 0))
    vec = pl.BlockSpec((1, d), lambda i: (0, 0))
    vshape = jax.ShapeDtypeStruct((1, d), F32)
    return pl.pallas_call(body, grid=(s // tr,), in_specs=[row, row, row, vec, vec],
                          out_specs=[row, vec, vec, vec],
                          out_shape=[jax.ShapeDtypeStruct((s, d), F32), vshape, vshape, vshape],
                          name=name, compiler_params=_cp(("arbitrary",)))(d_h, x, d_out, g, scale)


def _gate_bwd(d_out, u, gate, name):
    s, d = d_out.shape
    tr = _rows(s)

    def body(do_ref, u_ref, gt_ref, du_ref, dgt_ref):
        @pl.when(pl.program_id(0) == 0)
        def _():
            dgt_ref[...] = jnp.zeros_like(dgt_ref)

        do = do_ref[...]
        du_ref[...] = (do * gt_ref[...]).astype(du_ref.dtype)
        dgt_ref[...] += jnp.sum(do * u_ref[...], axis=0, keepdims=True)

    row = pl.BlockSpec((tr, d), lambda i: (i, 0))
    vec = pl.BlockSpec((1, d), lambda i: (0, 0))
    return pl.pallas_call(body, grid=(s // tr,), in_specs=[row, row, vec], out_specs=[row, vec],
                          out_shape=[jax.ShapeDtypeStruct((s, d), MXU), jax.ShapeDtypeStruct((1, d), F32)],
                          name=name, compiler_params=_cp(("arbitrary",)))(d_out, u, gate)


def _final_loss(x, g, target, name):
    s, d = x.shape
    tr = _rows(s)

    def body(x_ref, g_ref, t_ref, loss_ref, dx_ref, dg_ref):
        @pl.when(pl.program_id(0) == 0)
        def _():
            loss_ref[...] = jnp.zeros_like(loss_ref)
            dg_ref[...] = jnp.zeros_like(dg_ref)

        xv = x_ref[...]
        gv = g_ref[...]
        diff = _rms(xv, gv) - t_ref[...]
        part = 0.5 * jnp.sum(jnp.sum(diff * diff, axis=-1, keepdims=True) / d, axis=0, keepdims=True)
        loss_ref[...] += jnp.broadcast_to(part, loss_ref.shape)
        dx, dg = _rms_bwd(diff / d, xv, gv)
        dx_ref[...] = dx
        dg_ref[...] += dg

    row = pl.BlockSpec((tr, d), lambda i: (i, 0))
    vec = pl.BlockSpec((1, d), lambda i: (0, 0))
    lvec = pl.BlockSpec((1, LANES), lambda i: (0, 0))
    return pl.pallas_call(body, grid=(s // tr,), in_specs=[row, vec, row], out_specs=[lvec, row, vec],
                          out_shape=[jax.ShapeDtypeStruct((1, LANES), F32), jax.ShapeDtypeStruct((s, d), F32),
                                     jax.ShapeDtypeStruct((1, d), F32)],
                          name=name, compiler_params=_cp(("arbitrary",)))(x, g, target)


def _shift_rows(u, s, down):
    ri = lax.broadcasted_iota(jnp.int32, u.shape, 0)
    if down:
        return jnp.where(ri == 0, 0.0, pltpu.roll(u, 1, 0))
    return jnp.where(ri == s - 1, 0.0, pltpu.roll(u, s - 1, 0))


def _conv_fwd(proj, conv_w, name):
    s = proj.shape[0]
    nt = CONV_CH // LANES

    def body(cb_ref, cc_ref, cx_ref, w_ref, pre_ref):
        u = cc_ref[...] * cx_ref[...]
        conv = _shift_rows(u, s, True) * w_ref[0:1, :] + u * w_ref[1:2, :] + _shift_rows(u, s, False) * w_ref[2:3, :]
        pre_ref[...] = cb_ref[...] * conv

    def col(off):
        return pl.BlockSpec((s, LANES), lambda j: (0, off // LANES + j))

    return pl.pallas_call(body, grid=(nt,), in_specs=[col(OCB), col(OCC), col(OCX), pl.BlockSpec((3, LANES), lambda j: (0, j))],
                          out_specs=pl.BlockSpec((s, LANES), lambda j: (0, j)),
                          out_shape=jax.ShapeDtypeStruct((s, CONV_CH), F32), name=name,
                          compiler_params=_cp(("parallel",)))(proj, proj, proj, conv_w)


def _conv_bwd(proj, conv_w, d_pre, name):
    s = proj.shape[0]
    nt = CONV_CH // LANES

    def body(cb_ref, cc_ref, cx_ref, w_ref, dp_ref, dcb_ref, dcc_ref, dcx_ref, dw_ref):
        cc, cx = cc_ref[...], cx_ref[...]
        u = cc * cx
        up, dn = _shift_rows(u, s, True), _shift_rows(u, s, False)
        w0, w1, w2 = w_ref[0:1, :], w_ref[1:2, :], w_ref[2:3, :]
        conv = up * w0 + u * w1 + dn * w2
        dp = dp_ref[...]
        dcb_ref[...] = dp * conv
        dconv = dp * cb_ref[...]
        du = _shift_rows(dconv, s, False) * w0 + dconv * w1 + _shift_rows(dconv, s, True) * w2
        dcc_ref[...] = du * cx
        dcx_ref[...] = du * cc
        dw_ref[0:1, :] = jnp.sum(dconv * up, axis=0, keepdims=True)
        dw_ref[1:2, :] = jnp.sum(dconv * u, axis=0, keepdims=True)
        dw_ref[2:3, :] = jnp.sum(dconv * dn, axis=0, keepdims=True)

    def col(off):
        return pl.BlockSpec((s, LANES), lambda j: (0, off // LANES + j))

    blk = pl.BlockSpec((s, LANES), lambda j: (0, j))
    wblk = pl.BlockSpec((3, LANES), lambda j: (0, j))
    full = jax.ShapeDtypeStruct((s, CONV_CH), F32)
    return pl.pallas_call(body, grid=(nt,), in_specs=[col(OCB), col(OCC), col(OCX), wblk, blk],
                          out_specs=[blk, blk, blk, wblk],
                          out_shape=[full, full, full, jax.ShapeDtypeStruct((3, CONV_CH), F32)],
                          name=name, compiler_params=_cp(("parallel",)))(proj, proj, proj, conv_w, d_pre)


GLA_SUB = 8


def _gla_gates(t_ref, wg_ref, bg_ref):
    t = t_ref[...]
    a = _dot(t, wg_ref[...]) + bg_ref[...]
    la = (jnp.minimum(a, 0.0) - jnp.log(1.0 + jnp.exp(-jnp.abs(a)))) / GTEMP
    return t, a, la


def _gla_masks(reverse):
    ri = lax.broadcasted_iota(jnp.int32, (CHUNK, CHUNK), 0)
    ci = lax.broadcasted_iota(jnp.int32, (CHUNK, CHUNK), 1)
    if reverse:
        cum, mask, mask_t = ci >= ri, ci > ri, ri > ci
    else:
        cum, mask, mask_t = ci <= ri, ci <= ri, ri <= ci
    return cum.astype(F32), mask, mask_t


def _gla_specs(s, reverse):
    nsub = min(GLA_SUB, s // CHUNK)
    nsteps = s // (CHUNK * nsub)

    def row(n):
        return nsteps - 1 - n if reverse else n

    def chunk(pi):
        return nsub - 1 - pi if reverse else pi

    return nsub, nsteps, row, chunk


def _gla_fwd(proj, wg_pad, bg, reverse, name):
    s = proj.shape[0]
    nsub, nsteps, row, chunk = _gla_specs(s, reverse)
    rb = nsub * CHUNK

    def body(q_ref, k_ref, v_ref, t_ref, wg_ref, bg_ref, o_ref, st_ref, state):
        @pl.when(pl.program_id(0) == 0)
        def _():
            state[...] = jnp.zeros_like(state)

        _, _, la = _gla_gates(t_ref, wg_ref, bg_ref)
        cumf, mask, _ = _gla_masks(reverse)
        lane = lax.broadcasted_iota(jnp.int32, (CHUNK, LANES), 1)
        for pi in range(nsub):
            rows = slice(chunk(pi) * CHUNK, (chunk(pi) + 1) * CHUNK)
            la_c = la[rows]
            b_all = _cum_dot(cumf, la_c)
            bl_all = jnp.sum(la_c, axis=0, keepdims=True)
            for p in range(GH // 2):
                sl = slice(p * LANES, (p + 1) * LANES)
                b, bl = b_all[:, sl], bl_all[:, sl]
                qd = q_ref[rows, sl] * (GDK ** -0.5) * jnp.exp(b)
                ki = k_ref[rows, sl] * jnp.exp(-b)
                kte = k_ref[rows, sl] * jnp.exp(bl - b)
                decay = jnp.exp(bl)
                for half in range(2):
                    h = 2 * p + half
                    lm = (lane < GDK) if half == 0 else (lane >= GDK)
                    qd_h = jnp.where(lm, qd, 0.0)
                    kte_h = jnp.where(lm, kte, 0.0)
                    v_h = v_ref[rows, h * GDV:(h + 1) * GDV]
                    st = state[h]
                    a_mat = jnp.where(mask, _dot_nt(qd_h, ki), 0.0)
                    o_ref[rows, h * GDV:(h + 1) * GDV] = _dot(a_mat, v_h) + _dot_nt(qd_h, st)
                    st_ref[pi, h] = st
                    state[h] = st * decay + _dot_tn(v_h, kte_h)

    return pl.pallas_call(
        body, grid=(nsteps,),
        in_specs=[pl.BlockSpec((rb, GQK), lambda n: (row(n), OGQ // GQK)),
                  pl.BlockSpec((rb, GQK), lambda n: (row(n), OGK // GQK)),
                  pl.BlockSpec((rb, GW), lambda n: (row(n), OGV // GW)),
                  pl.BlockSpec((rb, LANES), lambda n: (row(n), OT // LANES)),
                  pl.BlockSpec((LANES, GQK), lambda n: (0, 0)),
                  pl.BlockSpec((1, GQK), lambda n: (0, 0))],
        out_specs=[pl.BlockSpec((rb, GW), lambda n: (row(n), 0)),
                   pl.BlockSpec((nsub, GH, GDV, LANES), lambda n: (n, 0, 0, 0))],
        out_shape=[jax.ShapeDtypeStruct((s, GW), F32), jax.ShapeDtypeStruct((s // CHUNK, GH, GDV, LANES), F32)],
        scratch_shapes=[pltpu.VMEM((GH, GDV, LANES), F32)],
        name=name, compiler_params=_cp(("arbitrary",)))(proj, proj, proj, proj, wg_pad, bg)


def _gla_bwd(proj, wg_pad, bg, states, d_o, reverse, name):
    s = proj.shape[0]
    nsub, nsteps, row, chunk = _gla_specs(s, reverse)
    rb = nsub * CHUNK

    def body(q_ref, k_ref, v_ref, t_ref, wg_ref, bg_ref, st_ref, do_ref,
             dq_ref, dk_ref, dv_ref, dt_ref, dwg_ref, dbg_ref, dstate, da_buf):
        @pl.when(pl.program_id(0) == 0)
        def _():
            dstate[...] = jnp.zeros_like(dstate)
            dwg_ref[...] = jnp.zeros_like(dwg_ref)
            dbg_ref[...] = jnp.zeros_like(dbg_ref)

        t, a, la = _gla_gates(t_ref, wg_ref, bg_ref)
        cumf, mask, mask_t = _gla_masks(reverse)
        lane = lax.broadcasted_iota(jnp.int32, (CHUNK, LANES), 1)
        for pi in reversed(range(nsub)):
            rows = slice(chunk(pi) * CHUNK, (chunk(pi) + 1) * CHUNK)
            la_c = la[rows]
            b_all = _cum_dot(cumf, la_c)
            bl_all = jnp.sum(la_c, axis=0, keepdims=True)
            for p in range(GH // 2):
                sl = slice(p * LANES, (p + 1) * LANES)
                b, bl = b_all[:, sl], bl_all[:, sl]
                e, ei, ee = jnp.exp(b), jnp.exp(-b), jnp.exp(bl - b)
                qd = q_ref[rows, sl] * (GDK ** -0.5) * e
                ki, kte = k_ref[rows, sl] * ei, k_ref[rows, sl] * ee
                decay = jnp.exp(bl)
                dqd = jnp.zeros((CHUNK, LANES), F32)
                dki = jnp.zeros((CHUNK, LANES), F32)
                dkte = jnp.zeros((CHUNK, LANES), F32)
                ddecay = jnp.zeros((1, LANES), F32)
                for half in range(2):
                    h = 2 * p + half
                    lm = (lane < GDK) if half == 0 else (lane >= GDK)
                    qd_h = jnp.where(lm, qd, 0.0)
                    ki_h = jnp.where(lm, ki, 0.0)
                    kte_h = jnp.where(lm, kte, 0.0)
                    v_h = v_ref[rows, h * GDV:(h + 1) * GDV]
                    do_h = do_ref[rows, h * GDV:(h + 1) * GDV]
                    st = st_ref[pi, h]
                    dst = dstate[h]
                    at_mat = jnp.where(mask_t, _dot_nt(ki_h, qd_h), 0.0)
                    da_mat = jnp.where(mask, _dot_nt(do_h, v_h), 0.0)
                    dat_mat = jnp.where(mask_t, _dot_nt(v_h, do_h), 0.0)
                    dv_ref[rows, h * GDV:(h + 1) * GDV] = _dot(at_mat, do_h) + _dot_nt(kte_h, dst)
                    dqd += _dot(da_mat, ki_h) + _dot(do_h, st)
                    dki += _dot(dat_mat, qd_h)
                    dkte += _dot(v_h, dst)
                    ddecay += jnp.sum(dst * st, axis=0, keepdims=True)
                    dstate[h] = dst * decay + _dot_tn(do_h, qd_h)
                dq_ref[rows, sl] = dqd * e * (GDK ** -0.5)
                dk_ref[rows, sl] = dki * ei + dkte * ee
                db = dqd * qd - dki * ki - dkte * kte
                dbl = jnp.sum(dkte * kte, axis=0, keepdims=True) + decay * ddecay
                da_buf[rows, sl] = _cum_dot(cumf, db, True) + dbl
        da = da_buf[...] * (1.0 / GTEMP) * _sigmoid(-a)
        dt_ref[...] = _dot_nt(da, wg_ref[...])
        dwg_ref[...] += _dot_tn(t, da)
        dbg_ref[...] += jnp.sum(da, axis=0, keepdims=True)

    def prow(j):
        return row(nsteps - 1 - j)

    return pl.pallas_call(
        body, grid=(nsteps,),
        in_specs=[pl.BlockSpec((rb, GQK), lambda j: (prow(j), OGQ // GQK)),
                  pl.BlockSpec((rb, GQK), lambda j: (prow(j), OGK // GQK)),
                  pl.BlockSpec((rb, GW), lambda j: (prow(j), OGV // GW)),
                  pl.BlockSpec((rb, LANES), lambda j: (prow(j), OT // LANES)),
                  pl.BlockSpec((LANES, GQK), lambda j: (0, 0)),
                  pl.BlockSpec((1, GQK), lambda j: (0, 0)),
                  pl.BlockSpec((nsub, GH, GDV, LANES), lambda j: (nsteps - 1 - j, 0, 0, 0)),
                  pl.BlockSpec((rb, GW), lambda j: (prow(j), 0))],
        out_specs=[pl.BlockSpec((rb, GQK), lambda j: (prow(j), 0)),
                   pl.BlockSpec((rb, GQK), lambda j: (prow(j), 0)),
                   pl.BlockSpec((rb, GW), lambda j: (prow(j), 0)),
                   pl.BlockSpec((rb, LANES), lambda j: (prow(j), 0)),
                   pl.BlockSpec((LANES, GQK), lambda j: (0, 0)),
                   pl.BlockSpec((1, GQK), lambda j: (0, 0))],
        out_shape=[jax.ShapeDtypeStruct((s, GQK), F32), jax.ShapeDtypeStruct((s, GQK), F32),
                   jax.ShapeDtypeStruct((s, GW), F32), jax.ShapeDtypeStruct((s, LANES), F32),
                   jax.ShapeDtypeStruct((LANES, GQK), F32), jax.ShapeDtypeStruct((1, GQK), F32)],
        scratch_shapes=[pltpu.VMEM((GH, GDV, LANES), F32), pltpu.VMEM((rb, GQK), F32)],
        name=name, compiler_params=_cp(("arbitrary",)))(proj, proj, proj, proj, wg_pad, bg, states, d_o)


def _rot_half(x):
    lane = lax.broadcasted_iota(jnp.int32, x.shape, 1)
    first = (lane % MROPE) < (MROPE // 2)
    return jnp.where(first, -pltpu.roll(x, LANES - MROPE // 2, 1), pltpu.roll(x, MROPE // 2, 1))


def _mla_prep(proj, cos, sin, qg, kvg, w_uq, w_ukv, name):
    s = proj.shape[0]
    tr = _rows(s)

    def body(mq_ref, mkv_ref, t_ref, cos_ref, sin_ref, qg_ref, kvg_ref, wuq_ref, wukv_ref, q_ref, k_ref, v_ref):
        cosv, sinv = cos_ref[...], sin_ref[...]
        lane = lax.broadcasted_iota(jnp.int32, (tr, LANES), 1)

        def rope(xv):
            return xv * cosv + _rot_half(xv) * sinv

        qm = _dot(_rms(mq_ref[...], qg_ref[...]), wuq_ref[...])
        kv = _dot(_rms(mkv_ref[...], kvg_ref[...]), wukv_ref[...])
        kr_lo = jnp.where(lane < MROPE, rope(t_ref[...]), 0.0)
        kr_hi = pltpu.roll(kr_lo, MROPE, 1)
        for p in range(MH // 2):
            r = rope(qm[:, MW + p * LANES:MW + (p + 1) * LANES]).astype(q_ref.dtype)
            q_ref[2 * p, :, LANES:] = r
            q_ref[2 * p + 1, :, LANES:] = r
        for h in range(MH):
            q_ref[h, :, :LANES] = qm[:, h * LANES:(h + 1) * LANES].astype(q_ref.dtype)
            k_ref[h, :, :LANES] = kv[:, 2 * h * LANES:(2 * h + 1) * LANES].astype(k_ref.dtype)
            k_ref[h, :, LANES:] = (kr_lo if h % 2 == 0 else kr_hi).astype(k_ref.dtype)
            v_ref[h] = kv[:, (2 * h + 1) * LANES:(2 * h + 2) * LANES].astype(v_ref.dtype)

    def full(shape):
        return pl.BlockSpec(shape, lambda i: (0,) * len(shape))

    return pl.pallas_call(
        body, grid=(s // tr,),
        in_specs=[pl.BlockSpec((tr, MQL), lambda i: (i, OMQ // MQL)),
                  pl.BlockSpec((tr, MKVL), lambda i: (i, OMKV // MKVL)),
                  pl.BlockSpec((tr, LANES), lambda i: (i, OT // LANES)),
                  pl.BlockSpec((tr, LANES), lambda i: (i, 0)),
                  pl.BlockSpec((tr, LANES), lambda i: (i, 0)),
                  full((1, MQL)), full((1, MKVL)), full((MQL, MQW)), full((MKVL, MKVW))],
        out_specs=[pl.BlockSpec((MH, tr, 2 * LANES), lambda i: (0, i, 0)),
                   pl.BlockSpec((MH, tr, 2 * LANES), lambda i: (0, i, 0)),
                   pl.BlockSpec((MH, tr, LANES), lambda i: (0, i, 0))],
        out_shape=[jax.ShapeDtypeStruct((MH, s, 2 * LANES), MXU), jax.ShapeDtypeStruct((MH, s, 2 * LANES), MXU),
                   jax.ShapeDtypeStruct((MH, s, LANES), MXU)],
        name=name, compiler_params=_cp(("parallel",)))(proj, proj, proj, cos, sin, qg, kvg, w_uq, w_ukv)


def _mla_prep_bwd(proj, cos, sin, qg, kvg, w_uq, w_ukv, d_q, d_k, d_v, name):
    s = proj.shape[0]
    tr = _rows(s)

    def body(mq_ref, mkv_ref, cos_ref, sin_ref, qg_ref, kvg_ref, wuq_ref, wukv_ref, dq_ref, dk_ref, dv_ref,
             dmq_ref, dmkv_ref, dt_ref, dwuq_ref, dwukv_ref, dqg_ref, dkvg_ref):
        @pl.when(pl.program_id(0) == 0)
        def _():
            for r in (dwuq_ref, dwukv_ref, dqg_ref, dkvg_ref):
                r[...] = jnp.zeros_like(r)

        cosv, sinv = cos_ref[...], sin_ref[...]
        lane = lax.broadcasted_iota(jnp.int32, (tr, LANES), 1)
        lo = lane < MROPE

        def unrope(dv):
            return dv * cosv - _rot_half(dv * sinv)

        parts = [dq_ref[h, :, :LANES] for h in range(MH)]
        for p in range(MH // 2):
            parts.append(unrope(jnp.where(lo, dq_ref[2 * p, :, LANES:], dq_ref[2 * p + 1, :, LANES:])))
        d_qm = jnp.concatenate(parts, axis=1)
        mq, qgv = mq_ref[...], qg_ref[...]
        cq = _rms(mq, qgv)
        dwuq_ref[...] += _dot_tn(cq, d_qm)
        dmq, dqg = _rms_bwd(_dot_nt(d_qm, wuq_ref[...]), mq, qgv)
        dmq_ref[...] = dmq
        dqg_ref[...] += dqg

        parts = []
        for h in range(MH):
            parts += [dk_ref[h, :, :LANES], dv_ref[h]]
        d_kv = jnp.concatenate(parts, axis=1)
        mkv, kvgv = mkv_ref[...], kvg_ref[...]
        ckv = _rms(mkv, kvgv)
        dwukv_ref[...] += _dot_tn(ckv, d_kv)
        dmkv, dkvg = _rms_bwd(_dot_nt(d_kv, wukv_ref[...]), mkv, kvgv)
        dmkv_ref[...] = dmkv
        dkvg_ref[...] += dkvg

        even = dk_ref[0, :, LANES:] + dk_ref[2, :, LANES:] + dk_ref[4, :, LANES:]
        odd = dk_ref[1, :, LANES:] + dk_ref[3, :, LANES:] + dk_ref[5, :, LANES:]
        d_kr = jnp.where(lo, even, 0.0) + pltpu.roll(jnp.where(lo, 0.0, odd), MROPE, 1)
        dt_ref[...] = jnp.where(lo, unrope(d_kr), 0.0)

    def full(shape):
        return pl.BlockSpec(shape, lambda i: (0,) * len(shape))

    return pl.pallas_call(
        body, grid=(s // tr,),
        in_specs=[pl.BlockSpec((tr, MQL), lambda i: (i, OMQ // MQL)),
                  pl.BlockSpec((tr, MKVL), lambda i: (i, OMKV // MKVL)),
                  pl.BlockSpec((tr, LANES), lambda i: (i, 0)),
                  pl.BlockSpec((tr, LANES), lambda i: (i, 0)),
                  full((1, MQL)), full((1, MKVL)), full((MQL, MQW)), full((MKVL, MKVW)),
                  pl.BlockSpec((MH, tr, 2 * LANES), lambda i: (0, i, 0)),
                  pl.BlockSpec((MH, tr, 2 * LANES), lambda i: (0, i, 0)),
                  pl.BlockSpec((MH, tr, LANES), lambda i: (0, i, 0))],
        out_specs=[pl.BlockSpec((tr, MQL), lambda i: (i, 0)), pl.BlockSpec((tr, MKVL), lambda i: (i, 0)),
                   pl.BlockSpec((tr, LANES), lambda i: (i, 0)),
                   full((MQL, MQW)), full((MKVL, MKVW)), full((1, MQL)), full((1, MKVL))],
        out_shape=[jax.ShapeDtypeStruct((s, MQL), F32), jax.ShapeDtypeStruct((s, MKVL), F32),
                   jax.ShapeDtypeStruct((s, LANES), F32),
                   jax.ShapeDtypeStruct((MQL, MQW), F32), jax.ShapeDtypeStruct((MKVL, MKVW), F32),
                   jax.ShapeDtypeStruct((1, MQL), F32), jax.ShapeDtypeStruct((1, MKVL), F32)],
        name=name, compiler_params=_cp(("arbitrary",)))(proj, proj, cos, sin, qg, kvg, w_uq, w_ukv, d_q, d_k, d_v)


ATT_SCALE = (MNOPE + MROPE) ** -0.5
ATT_SCALE_LOG2 = ATT_SCALE * 1.4426950408889634
ATT_TQ_FWD, ATT_TQ = 2048, 2048
ATT_SUB, ATT_SUB_BWD = 256, 256


def _attn_fwd(q, k, v, name):
    s = q.shape[1]
    tq = min(ATT_TQ_FWD, s)
    sub = min(ATT_SUB, tq)

    def body(q_ref, k_ref, v_ref, o_ref, lse_ref):
        for r0 in range(0, tq, sub):
            rows = slice(r0, r0 + sub)
            sc = _dot_nt(q_ref[0, rows, :], k_ref[0])
            m = jnp.max(sc, axis=-1, keepdims=True)
            p = jnp.exp2((sc - m) * ATT_SCALE_LOG2)
            l = jnp.sum(p, axis=-1, keepdims=True)
            o_ref[rows, :] = _dot(p, v_ref[0]) / l
            lse_ref[0, rows, :] = m * ATT_SCALE_LOG2 + jnp.log2(l)

    return pl.pallas_call(
        body, grid=(MH, s // tq),
        in_specs=[pl.BlockSpec((1, tq, 2 * LANES), lambda h, i: (h, i, 0)),
                  pl.BlockSpec((1, s, 2 * LANES), lambda h, i: (h, 0, 0)),
                  pl.BlockSpec((1, s, LANES), lambda h, i: (h, 0, 0))],
        out_specs=[pl.BlockSpec((tq, LANES), lambda h, i: (i, h)),
                   pl.BlockSpec((1, tq, 1), lambda h, i: (h, i, 0))],
        out_shape=[jax.ShapeDtypeStruct((s, MW), F32), jax.ShapeDtypeStruct((MH, s, 1), F32)],
        name=name, compiler_params=_cp(("parallel", "parallel")))(q, k, v)


def _attn_bwd(q, k, v, o, lse, d_o, name):
    s = q.shape[1]
    tq = min(ATT_TQ, s)
    sub = min(ATT_SUB_BWD, tq)

    def body(q_ref, k_ref, v_ref, o_ref, lse_ref, do_ref, dq_ref, dk_ref, dv_ref):
        @pl.when(pl.program_id(1) == 0)
        def _():
            dk_ref[...] = jnp.zeros_like(dk_ref)
            dv_ref[...] = jnp.zeros_like(dv_ref)

        kv = k_ref[0]
        for r0 in range(0, tq, sub):
            rows = slice(r0, r0 + sub)
            qv, do = q_ref[0, rows, :], do_ref[rows, :]
            p = jnp.exp2(_dot_nt(qv, kv) * ATT_SCALE_LOG2 - lse_ref[0, rows, :])
            delta = jnp.sum(do * o_ref[rows, :], axis=-1, keepdims=True)
            ds = p * (_dot_nt(do, v_ref[0]) - delta)
            dq_ref[0, rows, :] = _dot(ds, kv) * ATT_SCALE
            dk_ref[0] += _dot_tn(ds, qv) * ATT_SCALE
            dv_ref[0] += _dot_tn(p, do)

    return pl.pallas_call(
        body, grid=(MH, s // tq),
        in_specs=[pl.BlockSpec((1, tq, 2 * LANES), lambda h, i: (h, i, 0)),
                  pl.BlockSpec((1, s, 2 * LANES), lambda h, i: (h, 0, 0)),
                  pl.BlockSpec((1, s, LANES), lambda h, i: (h, 0, 0)),
                  pl.BlockSpec((tq, LANES), lambda h, i: (i, h)),
                  pl.BlockSpec((1, tq, 1), lambda h, i: (h, i, 0)),
                  pl.BlockSpec((tq, LANES), lambda h, i: (i, h))],
        out_specs=[pl.BlockSpec((1, tq, 2 * LANES), lambda h, i: (h, i, 0)),
                   pl.BlockSpec((1, s, 2 * LANES), lambda h, i: (h, 0, 0)),
                   pl.BlockSpec((1, s, LANES), lambda h, i: (h, 0, 0))],
        out_shape=[jax.ShapeDtypeStruct((MH, s, 2 * LANES), F32), jax.ShapeDtypeStruct((MH, s, 2 * LANES), F32),
                   jax.ShapeDtypeStruct((MH, s, LANES), F32)],
        name=name, compiler_params=_cp(("parallel", "arbitrary")))(q, k, v, o, lse, d_o)


def _merge_fwd(o_f, o_b, o_att, pre, proj, gng, mog, cog, name):
    s = proj.shape[0]
    tr = _rows(s)

    def body(of_ref, ob_ref, oa_ref, pre_ref, z_ref, gng_ref, mog_ref, cog_ref, y_ref):
        z = z_ref[...]
        sz = z * _sigmoid(z)
        osum = of_ref[...] + ob_ref[...]
        gg = gng_ref[...]
        for h in range(GH):
            sl = slice(h * GDV, (h + 1) * GDV)
            y_ref[:, sl] = (_rms(osum[:, sl], gg) * sz[:, sl]).astype(y_ref.dtype)
        y_ref[:, GW:GW + MW] = (_rms(oa_ref[...], mog_ref[...]) * sz[:, GW:GW + MW]).astype(y_ref.dtype)
        y_ref[:, GW + MW:] = (_rms(pre_ref[...], cog_ref[...]) * sz[:, GW + MW:]).astype(y_ref.dtype)

    def row(w):
        return pl.BlockSpec((tr, w), lambda i: (i, 0))

    def vec(w):
        return pl.BlockSpec((1, w), lambda i: (0, 0))

    return pl.pallas_call(
        body, grid=(s // tr,),
        in_specs=[row(GW), row(GW), row(MW), row(CONV_CH), row(D_MIX), vec(GDV), vec(MW), vec(CONV_CH)],
        out_specs=row(D_MIX), out_shape=jax.ShapeDtypeStruct((s, D_MIX), MXU),
        name=name, compiler_params=_cp(("parallel",)))(o_f, o_b, o_att, pre, proj, gng, mog, cog)


def _merge_bwd(d_y, o_f, o_b, o_att, pre, proj, gng, mog, cog, name):
    s = proj.shape[0]
    tr = _rows(s)

    def body(dy_ref, of_ref, ob_ref, oa_ref, pre_ref, z_ref, gng_ref, mog_ref, cog_ref,
             dz_ref, dos_ref, doa_ref, dpre_ref, dgng_ref, dmog_ref, dcog_ref):
        @pl.when(pl.program_id(0) == 0)
        def _():
            for r in (dgng_ref, dmog_ref, dcog_ref):
                r[...] = jnp.zeros_like(r)

        z, dy = z_ref[...], dy_ref[...]
        sg = _sigmoid(z)
        sz = z * sg
        dsz = sg * (1.0 + z * (1.0 - sg))
        dcat = dy * sz
        dyz = dy * dsz
        osum = of_ref[...] + ob_ref[...]
        gg = gng_ref[...]
        dgg = jnp.zeros_like(gg)
        for h in range(GH):
            sl = slice(h * GDV, (h + 1) * GDV)
            dz_ref[:, sl] = dyz[:, sl] * _rms(osum[:, sl], gg)
            dx, dg = _rms_bwd(dcat[:, sl], osum[:, sl], gg)
            dos_ref[:, sl] = dx
            dgg += dg
        dgng_ref[...] += dgg
        sl = slice(GW, GW + MW)
        oa, mg = oa_ref[...], mog_ref[...]
        dz_ref[:, sl] = dyz[:, sl] * _rms(oa, mg)
        dx, dg = _rms_bwd(dcat[:, sl], oa, mg)
        doa_ref[...] = dx
        dmog_ref[...] += dg
        sl = slice(GW + MW, D_MIX)
        pv, cg = pre_ref[...], cog_ref[...]
        dz_ref[:, sl] = dyz[:, sl] * _rms(pv, cg)
        dx, dg = _rms_bwd(dcat[:, sl], pv, cg)
        dpre_ref[...] = dx
        dcog_ref[...] += dg

    def row(w):
        return pl.BlockSpec((tr, w), lambda i: (i, 0))

    def vec(w):
        return pl.BlockSpec((1, w), lambda i: (0, 0))

    def rs(w):
        return jax.ShapeDtypeStruct((s, w), F32)

    def vs(w):
        return jax.ShapeDtypeStruct((1, w), F32)

    return pl.pallas_call(
        body, grid=(s // tr,),
        in_specs=[row(D_MIX), row(GW), row(GW), row(MW), row(CONV_CH), row(D_MIX), vec(GDV), vec(MW), vec(CONV_CH)],
        out_specs=[row(D_MIX), row(GW), row(MW), row(CONV_CH), vec(GDV), vec(MW), vec(CONV_CH)],
        out_shape=[rs(D_MIX), rs(GW), rs(MW), rs(CONV_CH), vs(GDV), vs(MW), vs(CONV_CH)],
        name=name, compiler_params=_cp(("arbitrary",)))(d_y, o_f, o_b, o_att, pre, proj, gng, mog, cog)


def _assemble_dproj(d_z, d_cb, d_cc, d_cx, d_mkv, dv_f, dv_b, dq_f, dq_b, dk_f, dk_b, d_mq, dt_m, dt_f, dt_b, name):
    s = d_z.shape[0]
    tr = _rows(s)

    def body(dz, dcb, dcc, dcx, dmkv, dvf, dvb, dqf, dqb, dkf, dkb, dmq, dtm, dtf, dtb, out):
        dt = out.dtype
        out[:, OZ:OZ + D_MIX] = dz[...].astype(dt)
        out[:, OCB:OCB + CONV_CH] = dcb[...].astype(dt)
        out[:, OCC:OCC + CONV_CH] = dcc[...].astype(dt)
        out[:, OCX:OCX + CONV_CH] = dcx[...].astype(dt)
        out[:, OMKV:OMKV + MKVL] = dmkv[...].astype(dt)
        out[:, OGV:OGV + GW] = (dvf[...] + dvb[...]).astype(dt)
        out[:, OGQ:OGQ + GQK] = (dqf[...] + dqb[...]).astype(dt)
        out[:, OGK:OGK + GQK] = (dkf[...] + dkb[...]).astype(dt)
        out[:, OMQ:OMQ + MQL] = dmq[...].astype(dt)
        out[:, OT:OT + LANES] = (dtm[...] + dtf[...] + dtb[...]).astype(dt)

    args = (d_z, d_cb, d_cc, d_cx, d_mkv, dv_f, dv_b, dq_f, dq_b, dk_f, dk_b, d_mq, dt_m, dt_f, dt_b)
    return pl.pallas_call(
        body, grid=(s // tr,),
        in_specs=[pl.BlockSpec((tr, a.shape[1]), lambda i: (i, 0)) for a in args],
        out_specs=pl.BlockSpec((tr, PW), lambda i: (i, 0)),
        out_shape=jax.ShapeDtypeStruct((s, PW), MXU), name=name, compiler_params=_cp(("parallel",)))(*args)


def _layer_fwd(x, mod, wt, cos, sin, tag, late=None, in_after=()):
    shift, scale, gate = mod
    h = _norm_mod(x, wt["norm_g"], scale, shift, f"norm_mod_{tag}")
    (proj,) = _matmul(h, wt["w_in"], dims="nn", tm=2048, tn=256, tk=2048, out_dtypes=(F32,), name=f"in_proj_{tag}",
                      after=in_after)
    if late is not None:
        wt.update(late(proj))
    o_f, st_f = _gla_fwd(proj, wt["wg_pad_f"], wt["bg_f"], False, f"gla_fwd_f_{tag}")
    o_b, st_b = _gla_fwd(proj, wt["wg_pad_b"], wt["bg_b"], True, f"gla_fwd_b_{tag}")
    q, k, v = _mla_prep(proj, cos, sin, wt["q_norm_g"], wt["kv_norm_g"], wt["w_uq"], wt["w_ukv"], f"mla_prep_{tag}")
    o_att, lse = _attn_fwd(q, k, v, f"attn_fwd_{tag}")
    pre = _conv_fwd(proj, wt["conv_w"], f"conv_fwd_{tag}")
    y = _merge_fwd(o_f, o_b, o_att, pre, proj, wt["gla_norm_g"], wt["mla_out_g"], wt["conv_out_g"], f"merge_fwd_{tag}")
    x_new, u = _matmul(y, wt["w_out"], dims="nn", tm=2048, tn=256, tk=2048, out_dtypes=(F32, F32),
                       name=f"out_proj_{tag}", epilogue=lambda acc, xv, gv: (xv + gv * acc, acc),
                       extras=(x, gate), extra_kinds=("mn", "n"))
    saved = dict(x=x, h=h, proj=proj, o_f=o_f, o_b=o_b, st_f=st_f, st_b=st_b, q=q, k=k, v=v,
                 o_att=o_att, lse=lse, pre=pre, y=y, u=u)
    return x_new, saved


def _layer_bwd(d_out, sv, mod, wt, cos, sin, tag, ship=None, dx_first=None, ship_rest=None):
    shift, scale, gate = mod
    proj = sv["proj"]
    d_u, d_gate = _gate_bwd(d_out, sv["u"], gate, f"gate_bwd_{tag}")
    (g_w_out,) = _matmul(sv["y"], d_u, dims="tn", tm=1024, tn=512, tk=2048, out_dtypes=(MXU,), name=f"out_proj_dw_{tag}")
    (d_y,) = _matmul(d_u, wt["w_out"], dims="nt", tm=2048, tn=256, tk=2048, out_dtypes=(F32,), name=f"out_proj_dx_{tag}",
                     after=(g_w_out,))
    d_z, d_osum, d_oatt, d_pre, d_gng, d_mog, d_cog = _merge_bwd(
        d_y, sv["o_f"], sv["o_b"], sv["o_att"], sv["pre"], proj, wt["gla_norm_g"], wt["mla_out_g"], wt["conv_out_g"],
        f"merge_bwd_{tag}")
    d_cb, d_cc, d_cx, d_conv_w = _conv_bwd(proj, wt["conv_w"], d_pre, f"conv_bwd_{tag}")
    d_q, d_k, d_v = _attn_bwd(sv["q"], sv["k"], sv["v"], sv["o_att"], sv["lse"], d_oatt, f"attn_bwd_{tag}")
    d_mq, d_mkv, dt_m, g_w_uq, g_w_ukv, d_qg, d_kvg = _mla_prep_bwd(
        proj, cos, sin, wt["q_norm_g"], wt["kv_norm_g"], wt["w_uq"], wt["w_ukv"], d_q, d_k, d_v, f"mla_prep_bwd_{tag}")
    bg_f, bg_b = wt["bg_f"], wt["bg_b"]
    if ship_rest is not None:
        tok = ship_rest(dict(w_out=g_w_out, w_uq=g_w_uq, w_ukv=g_w_ukv))
        bg_f, bg_b = bg_f + tok, bg_b + tok
    dq_f, dk_f, dv_f, dt_f, d_wg_f, d_bg_f = _gla_bwd(proj, wt["wg_pad_f"], bg_f, sv["st_f"], d_osum, False,
                                                     f"gla_bwd_f_{tag}")
    dq_b, dk_b, dv_b, dt_b, d_wg_b, d_bg_b = _gla_bwd(proj, wt["wg_pad_b"], bg_b, sv["st_b"], d_osum, True,
                                                     f"gla_bwd_b_{tag}")
    d_proj = _assemble_dproj(d_z, d_cb, d_cc, d_cx, d_mkv, dv_f, dv_b, dq_f, dq_b, dk_f, dk_b, d_mq, dt_m, dt_f, dt_b,
                             f"assemble_dproj_{tag}")
    grads = dict(w_out=g_w_out, w_uq=g_w_uq, w_ukv=g_w_ukv,
                 wg_pad_f=d_wg_f, bg_f=d_bg_f, wg_pad_b=d_wg_b, bg_b=d_bg_b, gla_norm_g=d_gng,
                 q_norm_g=d_qg, kv_norm_g=d_kvg, mla_out_g=d_mog, conv_w=d_conv_w, conv_out_g=d_cog)

    def in_dw(after):
        (g_w_in,) = _matmul(sv["h"], d_proj, dims="tn", tm=2048, tn=256, tk=2048, out_dtypes=(MXU,),
                            name=f"in_proj_dw_{tag}", after=after)
        grads["w_in"] = g_w_in
        return dict(w_in=g_w_in, w_out=g_w_out, w_uq=g_w_uq, w_ukv=g_w_ukv)

    def in_dx(after):
        (d_h,) = _matmul(d_proj, wt["w_in"], dims="nt", tm=1024, tn=512, tk=PW, out_dtypes=(F32,),
                         name=f"in_proj_dx_{tag}", after=after)
        d_x, d_shift, d_scale, d_ng = _norm_mod_bwd(d_h, sv["x"], d_out, wt["norm_g"], scale, f"norm_mod_bwd_{tag}")
        grads["norm_g"] = d_ng
        return d_x, (d_shift, d_scale, d_gate)

    if dx_first is None:
        big = in_dw(())
        d_x, d_mod = in_dx((big["w_in"],) if ship is None else ship(big))
    else:
        d_x, d_mod = in_dx(())
        big = in_dw(dx_first(d_x, d_mod, grads))
        ship(big)
    return d_x, d_mod, grads


IN_SEGS = ((3808, 5856), (2272, 3808), (1952, 2208), (768, 1536), (0, 768), (1568, 1952), (2208, 2272), (1536, 1568))
UQ_SEGS = (tuple((h * (MNOPE + MROPE), h * (MNOPE + MROPE) + MNOPE) for h in range(MH))
           + tuple((h * (MNOPE + MROPE) + MNOPE, (h + 1) * (MNOPE + MROPE)) for h in range(MH)))


def _perm_gathered(g, segs, width):
    per = g.shape[-1]
    parts, total = [], 0
    for a, b in segs:
        c = a
        while c < b:
            j = c // per
            hi = min(b, (j + 1) * per)
            parts.append(g[j, :, c - j * per:hi - j * per])
            c = hi
        total += b - a
    if width > total:
        parts.append(jnp.zeros((g.shape[1], width - total), g.dtype))
    return jnp.concatenate(parts, axis=1)


def _scatter_perm(gp, segs, per):
    offs, o = [], 0
    for a, b in segs:
        offs.append((a, b, o))
        o += b - a
    blocks = []
    for j in range(N_DEV):
        lo, hi = j * per, (j + 1) * per
        pieces = []
        for a, b, o in sorted(offs):
            s0, s1 = max(a, lo), min(b, hi)
            if s0 < s1:
                pieces.append(gp[:, o + s0 - a:o + s1 - a])
        blocks.append(jnp.concatenate(pieces, axis=1))
    return jnp.stack(blocks)


def _prep_layer_weights(w_in, w_out, w_uq, w_ukv, small):
    def vec(v):
        return v.reshape(1, -1).astype(F32)

    zeros = functools.partial(jnp.zeros, dtype=F32)
    wg_f, wg_b = small["gla_wg_f"].astype(F32), small["gla_wg_b"].astype(F32)
    wg_pad_f = jnp.concatenate([zeros((MROPE, GQK)), wg_f, zeros((LANES - MROPE - GRANK, GQK))], axis=0)
    wg_pad_b = jnp.concatenate([zeros((MROPE + GRANK, GQK)), wg_b, zeros((LANES - MROPE - 2 * GRANK, GQK))], axis=0)
    wt = dict(norm_g=vec(small["norm_g"]), wg_pad_f=wg_pad_f, wg_pad_b=wg_pad_b,
              bg_f=vec(small["gla_bg_f"]), bg_b=vec(small["gla_bg_b"]), gla_norm_g=vec(small["gla_norm_g"]),
              q_norm_g=vec(small["mla_q_norm_g"]), kv_norm_g=vec(small["mla_kv_norm_g"]),
              mla_out_g=vec(small["mla_out_g"]), conv_w=small["conv_w"].astype(F32),
              conv_out_g=vec(small["conv_out_g"]))
    for name, w in (("w_in", w_in), ("w_out", w_out), ("w_uq", w_uq), ("w_ukv", w_ukv)):
        if w is not None:
            wt[name] = w.astype(MXU)
    return wt


def _natural_small(gr):
    return dict(norm_g=gr["norm_g"][0],
                gla_wg_f=gr["wg_pad_f"][MROPE:MROPE + GRANK], gla_bg_f=gr["bg_f"][0],
                gla_wg_b=gr["wg_pad_b"][MROPE + GRANK:MROPE + 2 * GRANK], gla_bg_b=gr["bg_b"][0],
                gla_norm_g=gr["gla_norm_g"][0], mla_q_norm_g=gr["q_norm_g"][0], mla_kv_norm_g=gr["kv_norm_g"][0],
                mla_out_g=gr["mla_out_g"][0], conv_w=gr["conv_w"], conv_out_g=gr["conv_out_g"][0])


def _exchange(arrs, name, scatter, space):
    n = len(arrs)

    def body(*refs):
        ins, outs = refs[:n], refs[n:2 * n]
        send_sems, recv_sems, loc_sems = refs[2 * n:]
        ax, ay, ac = lax.axis_index("x"), lax.axis_index("y"), lax.axis_index("c")
        me = 4 * ax + 2 * ay + ac

        def src(a, to):
            return ins[a].at[to] if scatter else ins[a]

        def remote(a, r, dst_slot):
            px = 1 - ax if r & 4 else ax
            py = 1 - ay if r & 2 else ay
            pc = 1 - ac if r & 1 else ac
            return pltpu.make_async_remote_copy(
                src_ref=src(a, 4 * px + 2 * py + pc), dst_ref=outs[a].at[dst_slot(4 * px + 2 * py + pc)],
                send_sem=send_sems.at[a, r - 1], recv_sem=recv_sems.at[a, r - 1],
                device_id=(px, py, pc), device_id_type=MESH)

        locs = [pltpu.make_async_copy(src(a, me), outs[a].at[me], loc_sems.at[a]) for a in range(n)]
        for cp in locs:
            cp.start()
        sends = [remote(a, r, lambda peer: me) for r in range(1, N_DEV) for a in range(n)]
        for cp in sends:
            cp.start()
        for r in range(1, N_DEV):
            for a in range(n):
                remote(a, r, lambda peer: peer).wait_recv()
        for cp in sends:
            cp.wait_send()
        for cp in locs:
            cp.wait()

    def out_shape(a):
        return jax.ShapeDtypeStruct(a.shape if scatter else (N_DEV,) + a.shape, a.dtype)

    spec = pl.BlockSpec(memory_space=space)
    return pl.pallas_call(
        body, in_specs=[spec] * n, out_specs=[spec] * n, out_shape=[out_shape(a) for a in arrs],
        scratch_shapes=[pltpu.SemaphoreType.DMA((n, N_DEV - 1)), pltpu.SemaphoreType.DMA((n, N_DEV - 1)),
                        pltpu.SemaphoreType.DMA((n,))],
        name=name, compiler_params=pltpu.CompilerParams(vmem_limit_bytes=VMEM_LIMIT))(*arrs)


def _peer(r):
    ax, ay, ac = lax.axis_index("x"), lax.axis_index("y"), lax.axis_index("c")
    px = 1 - ax if r & 4 else ax
    py = 1 - ay if r & 2 else ay
    pc = 1 - ac if r & 1 else ac
    return (px, py, pc), 4 * px + 2 * py + pc


def _slot(rel_div):
    rel, div = rel_div
    idx = _peer(rel)[1]
    return idx if div == 1 else idx // div


AG_SPREAD = tuple((r, None, (0, 1), (r, 1)) for r in (1, 2, 4, 6))
AG_FORWARD = tuple((1, (k, 1), (k, 1), (1 ^ k, 1)) for k in (2, 4, 6))
RS_PAIR = tuple((1, (1 ^ k, 1), (1 ^ k, 2), (k, 2)) for k in (0, 2, 4, 6))
RS_CHIPS = tuple((r, (r, 2), (0, 2), (r, 2)) for r in (2, 4, 6))


def _plan_copies(plan, n, src_refs, land_refs, send_sems, recv_sems, arriving):
    out = []
    for i, (r, src, dst, recv) in enumerate(plan):
        peer = _peer(r)[0]
        for a in range(n):
            out.append(pltpu.make_async_remote_copy(
                src_ref=src_refs[a] if src is None else src_refs[a].at[_slot(src)],
                dst_ref=land_refs[a].at[_slot(recv if arriving else dst)],
                send_sem=send_sems.at[i * n + a], recv_sem=recv_sems.at[i * n + a],
                device_id=peer, device_id_type=MESH))
    return out


def _exchange_hbm(plan, srcs, lands, name, after=()):
    n = len(lands)
    fresh = isinstance(lands[0], jax.ShapeDtypeStruct)
    ins = ([] if srcs is None else list(srcs)) + ([] if fresh else list(lands))
    ns = 0 if srcs is None else n
    n_data = len(ins)
    ins = ins + list(after)

    def body(*refs):
        outs = refs[len(ins):len(ins) + n]
        send_sems, recv_sems = refs[-2:]
        src_refs = refs[:n] if srcs is not None else refs[ns:ns + n]
        sends = _plan_copies(plan, n, src_refs, outs, send_sems, recv_sems, False)
        for cp in sends:
            cp.start()
        for cp in _plan_copies(plan, n, src_refs, outs, send_sems, recv_sems, True):
            cp.wait_recv()
        for cp in sends:
            cp.wait_send()

    hbm = pl.BlockSpec(memory_space=pltpu.HBM)
    k = len(plan) * n
    return pl.pallas_call(
        body, name=name, in_specs=[hbm] * n_data + [pl.BlockSpec(memory_space=pl.ANY)] * len(after), out_specs=[hbm] * n,
        out_shape=[jax.ShapeDtypeStruct(a.shape, a.dtype) for a in lands],
        scratch_shapes=[pltpu.SemaphoreType.DMA((k,)), pltpu.SemaphoreType.DMA((k,))],
        input_output_aliases={} if fresh else {ns + i: i for i in range(n)},
        compiler_params=pltpu.CompilerParams(vmem_limit_bytes=VMEM_LIMIT))(*ins)


def _plan_start(plan, srcs, land_shapes, after, name):
    n = len(srcs)

    def body(*refs):
        src_refs, land_refs = refs[:n], refs[n:2 * n]
        send_sems, recv_sems = refs[2 * n + 1], refs[2 * n + 2]
        for cp in _plan_copies(plan, n, src_refs, land_refs, send_sems, recv_sems, False):
            cp.start()
        refs[-1][...] = jnp.zeros_like(refs[-1])

    hbm = pl.BlockSpec(memory_space=pltpu.HBM)
    sem = pl.BlockSpec(memory_space=pltpu.SEMAPHORE)
    k = len(plan) * n
    srcs = [pltpu.with_memory_space_constraint(a, pltpu.HBM) for a in srcs]
    lands = [pltpu.with_memory_space_constraint(lax.empty(shp, a.dtype), pltpu.HBM) for shp, a in zip(land_shapes, srcs)]
    res = pl.pallas_call(
        body, name=name,
        in_specs=[hbm] * (2 * n) + [pl.BlockSpec(memory_space=pl.ANY)],
        out_specs=[sem, sem] + [hbm] * (2 * n) + [pl.BlockSpec(memory_space=pltpu.VMEM)],
        out_shape=[pltpu.SemaphoreType.DMA((k,)), pltpu.SemaphoreType.DMA((k,))]
        + [pltpu.HBM(a.shape, a.dtype) for a in srcs] + [pltpu.HBM(shp, a.dtype) for shp, a in zip(land_shapes, srcs)]
        + [jax.ShapeDtypeStruct((8, LANES), F32)],
        input_output_aliases={i: 2 + i for i in range(2 * n)},
        compiler_params=pltpu.CompilerParams(has_side_effects=pltpu.SideEffectType.DATAFLOW_SIDE_EFFECTING),
    )(*srcs, *lands, after)
    return res[0], res[1], list(res[2:2 + n]), list(res[2 + n:2 + 2 * n]), res[-1]


def _plan_wait(plan, handle, after, name):
    send_sems, recv_sems, srcs, lands, _ = handle
    n = len(srcs)
    after = list(after)

    def body(*refs):
        src_refs, land_refs = refs[:n], refs[n:2 * n]
        ssem, rsem = refs[2 * n], refs[2 * n + 1]
        for cp in _plan_copies(plan, n, src_refs, land_refs, ssem, rsem, False):
            cp.wait_send()
        for cp in _plan_copies(plan, n, src_refs, land_refs, ssem, rsem, True):
            cp.wait_recv()

    hbm = pl.BlockSpec(memory_space=pltpu.HBM)
    sem = pl.BlockSpec(memory_space=pltpu.SEMAPHORE)
    res = pl.pallas_call(
        body, name=name,
        in_specs=[hbm] * (2 * n) + [sem, sem] + [pl.BlockSpec(memory_space=pl.ANY)] * len(after),
        out_specs=[hbm] * (2 * n),
        out_shape=[pltpu.HBM(a.shape, a.dtype) for a in srcs] + [pltpu.HBM(a.shape, a.dtype) for a in lands],
        input_output_aliases={i: i for i in range(2 * n)},
        compiler_params=pltpu.CompilerParams(has_side_effects=pltpu.SideEffectType.DATAFLOW_SIDE_EFFECTING),
    )(*srcs, *lands, send_sems, recv_sems, *after)
    return list(res[:n]), list(res[n:])


def _pair_sum(send, got, core, name):
    _, r, c = send.shape
    tr = 256 if r % 256 == 0 else r

    def body(core_ref, s_ref, g_ref, o_ref):
        o_ref[0] = (s_ref[0].astype(F32) + g_ref[0].astype(F32)).astype(o_ref.dtype)

    return pl.pallas_call(
        body, name=name,
        grid_spec=pltpu.PrefetchScalarGridSpec(
            num_scalar_prefetch=1, grid=(N_DEV // 2, r // tr),
            in_specs=[pl.BlockSpec((1, tr, c), lambda kc, i, core_ref: (2 * kc + core_ref[0], i, 0)),
                      pl.BlockSpec((1, tr, c), lambda kc, i, core_ref: (kc, i, 0))],
            out_specs=pl.BlockSpec((1, tr, c), lambda kc, i, core_ref: (kc, i, 0))),
        out_shape=jax.ShapeDtypeStruct((N_DEV // 2, r, c), send.dtype),
        compiler_params=_cp(("parallel", "parallel")))(core, send, got)


def _ada_mod(c_all, ada_w, ada_b_cols, name):
    nl, d, wc = ada_w.shape

    def body(c_ref, w_ref, b_ref, ca_ref, mod_ref):
        cv = c_ref[...]
        ca = cv * _sigmoid(cv)
        ca_ref[...] = ca
        mod_ref[0] = _dotf(ca, w_ref[0]) + b_ref[0]

    return pl.pallas_call(
        body, grid=(nl,),
        in_specs=[pl.BlockSpec((N_DEV, d), lambda l: (0, 0)), pl.BlockSpec((1, d, wc), lambda l: (l, 0, 0)),
                  pl.BlockSpec((1, 1, wc), lambda l: (l, 0, 0))],
        out_specs=[pl.BlockSpec((N_DEV, d), lambda l: (0, 0)), pl.BlockSpec((1, N_DEV, wc), lambda l: (l, 0, 0))],
        out_shape=[jax.ShapeDtypeStruct((N_DEV, d), F32), jax.ShapeDtypeStruct((nl, N_DEV, wc), F32)],
        name=name, compiler_params=_cp(("arbitrary",)))(c_all, ada_w, ada_b_cols)


def _adam(w, g, m, v):
    m2 = ADAM_B1 * m + (1.0 - ADAM_B1) * g
    v2 = ADAM_B2 * v + (1.0 - ADAM_B2) * (g * g)
    m_hat = m2 / (1.0 - ADAM_B1 ** ADAM_STEP)
    v_hat = v2 / (1.0 - ADAM_B2 ** ADAM_STEP)
    delta = -ADAM_LR * (m_hat / (jnp.sqrt(v_hat) + ADAM_EPS) + ADAM_WD * w)
    return delta, m2, v2


def _ada_grad_adam(c_act, d_mod, w, m, v, name):
    nl, d, wc = w.shape
    tk = min(512, d)

    def body(c_ref, dm_ref, w_ref, m_ref, v_ref, g_ref, dl_ref, m2_ref, v2_ref):
        g = _dotf_tn(c_ref[...], dm_ref[0])
        delta, m2, v2 = _adam(w_ref[0], g, m_ref[0], v_ref[0])
        g_ref[0], dl_ref[0], m2_ref[0], v2_ref[0] = g, delta, m2, v2

    blk = pl.BlockSpec((1, tk, wc), lambda l, i: (l, i, 0))
    shp = jax.ShapeDtypeStruct(w.shape, F32)
    return pl.pallas_call(
        body, grid=(nl, d // tk),
        in_specs=[pl.BlockSpec((N_DEV, tk), lambda l, i: (0, i)), pl.BlockSpec((1, N_DEV, wc), lambda l, i: (l, 0, 0)),
                  blk, blk, blk],
        out_specs=[blk] * 4, out_shape=[shp] * 4, name=name,
        compiler_params=_cp(("parallel", "parallel")))(c_act, d_mod, w, m, v)


def _adam_big(recv, w, m, v, layer, prev, name, after=()):
    nl, r, c = w.shape
    tr = 256 if r % 256 == 0 else r
    nparts = recv.shape[0]

    def body(rc_ref, w_ref, m_ref, v_ref, *rest):
        g_ref, dl_ref, m2_ref, v2_ref = rest[-4:]
        g = rc_ref[0].astype(F32)
        for d in range(1, nparts):
            g = g + rc_ref[d].astype(F32)
        delta, m2, v2 = _adam(w_ref[0], g, m_ref[0], v_ref[0])
        g_ref[0], dl_ref[0], m2_ref[0], v2_ref[0] = g, delta, m2, v2

    blk = pl.BlockSpec((1, tr, c), lambda i: (layer, i, 0))
    shp = jax.ShapeDtypeStruct(w.shape, F32)
    prev = () if prev is None else tuple(prev)
    return pl.pallas_call(
        body, grid=(r // tr,),
        in_specs=[pl.BlockSpec((nparts, tr, c), lambda i: (0, i, 0)), blk, blk, blk]
        + [pl.BlockSpec(memory_space=pl.ANY)] * (len(prev) + len(after)),
        out_specs=[blk] * 4, out_shape=[shp] * 4, name=name,
        input_output_aliases={4 + j: j for j in range(len(prev))},
        compiler_params=_cp(("parallel",)))(recv, w, m, v, *prev, *after)


def _sum_devices(gathered, name):
    _, r, c = gathered.shape

    def body(g_ref, o_ref):
        acc = g_ref[0]
        for d in range(1, N_DEV):
            acc = acc + g_ref[d]
        o_ref[...] = acc

    spec = pl.BlockSpec(memory_space=pltpu.VMEM)
    return pl.pallas_call(body, in_specs=[spec], out_specs=spec, out_shape=jax.ShapeDtypeStruct((r, c), F32),
                          name=name, compiler_params=pltpu.CompilerParams(vmem_limit_bytes=VMEM_LIMIT))(gathered)


def _adam_small(ws, gs, ms, vs, name):
    n = len(ws)

    def body(*refs):
        for i in range(n):
            w_ref, g_ref, m_ref, v_ref = (refs[k * n + i] for k in range(4))
            dl_ref, m2_ref, v2_ref = (refs[(4 + k) * n + i] for k in range(3))
            dl_ref[...], m2_ref[...], v2_ref[...] = _adam(w_ref[...], g_ref[...], m_ref[...], v_ref[...])

    spec = pl.BlockSpec(memory_space=pltpu.VMEM)
    shapes = [jax.ShapeDtypeStruct(w.shape, F32) for w in ws]
    res = pl.pallas_call(body, in_specs=[spec] * (4 * n), out_specs=[spec] * (3 * n), out_shape=shapes * 3, name=name,
                         compiler_params=pltpu.CompilerParams(vmem_limit_bytes=VMEM_LIMIT))(*ws, *gs, *ms, *vs)
    return res[:n], res[n:2 * n], res[2 * n:]


def _pack(parts):
    flat = jnp.concatenate([p.reshape(-1).astype(F32) for p in parts])
    assert flat.shape[0] % LANES == 0, flat.shape
    return flat.reshape(-1, LANES)


def _unpack(packed, shapes):
    flat = packed.reshape(-1)
    out, off = [], 0
    for shp in shapes:
        size = 1
        for dim in shp:
            size *= dim
        out.append(flat[off:off + size].reshape(shp))
        off += size
    return out


def _gather_cols(g, per):
    g = jnp.moveaxis(g, 0, -2)
    return g.reshape(g.shape[:-2] + (N_DEV * per,))


def _scatter_cols(g, per):
    return jnp.moveaxis(g.reshape(g.shape[:-1] + (N_DEV, per)), -2, 0)


def _my_cols(full, me, per):
    return lax.dynamic_slice_in_dim(full, me * per, per, axis=full.ndim - 1)


def kernel(x, c, positions, ada_w, ada_b, norm_g, w_in, gla_wg_f, gla_bg_f, gla_wg_b, gla_bg_b, gla_norm_g, mla_q_norm_g, mla_kv_norm_g, mla_w_uq, mla_w_ukv, mla_out_g, conv_w, conv_out_g, w_out, final_g, loss_target, m_ada_w, m_ada_b, m_norm_g, m_w_in, m_gla_wg_f, m_gla_bg_f, m_gla_wg_b, m_gla_bg_b, m_gla_norm_g, m_mla_q_norm_g, m_mla_kv_norm_g, m_mla_w_uq, m_mla_w_ukv, m_mla_out_g, m_conv_w, m_conv_out_g, m_w_out, m_final_g, v_ada_w, v_ada_b, v_norm_g, v_w_in, v_gla_wg_f, v_gla_bg_f, v_gla_wg_b, v_gla_bg_b, v_gla_norm_g, v_mla_q_norm_g, v_mla_kv_norm_g, v_mla_w_uq, v_mla_w_ukv, v_mla_out_g, v_conv_w, v_conv_out_g, v_w_out, v_final_g):
    me = 4 * lax.axis_index("x") + 2 * lax.axis_index("y") + lax.axis_index("c")
    nl = ada_w.shape[0]
    s, d = x.shape[1], x.shape[2]
    ada_cols = ada_w.shape[2]
    wgc, cwc = gla_wg_f.shape[2], conv_w.shape[2]

    (g0,) = _exchange([_pack([c, gla_wg_f, gla_wg_b, conv_w])], "gather_small_in", False, pltpu.VMEM)
    g0 = g0.reshape(N_DEV, -1)
    o1, o2, o3 = d, d + gla_wg_f.size, d + 2 * gla_wg_f.size
    c_all = g0[:, :o1]
    wgf_full = _gather_cols(g0[:, o1:o2].reshape((N_DEV,) + gla_wg_f.shape), wgc)
    wgb_full = _gather_cols(g0[:, o2:o3].reshape((N_DEV,) + gla_wg_b.shape), wgc)
    convw_full = _gather_cols(g0[:, o3:].reshape((N_DEV,) + conv_w.shape), cwc)

    ada_b_cols = _my_cols(ada_b, me, ada_cols).reshape(nl, 1, ada_cols)
    c_act, mod_cols = _ada_mod(c_all, ada_w, ada_b_cols, "ada_mod")
    (g1,) = _exchange([_pack([mod_cols])], "gather_mod", False, pltpu.VMEM)
    mod_all = g1.reshape(N_DEV, nl, N_DEV, ada_cols)
    mod_mine = _gather_cols(lax.dynamic_index_in_dim(mod_all, me, axis=2, keepdims=False), ada_cols)

    inv_freq = ROPE_THETA ** (-jnp.arange(0, MROPE, 2, dtype=F32) / MROPE)
    ang = positions[0].astype(F32)[:, None] * inv_freq
    cos, sin = jnp.tile(jnp.cos(ang), (1, LANES * 2 // MROPE)), jnp.tile(jnp.sin(ang), (1, LANES * 2 // MROPE))

    big = [w_in, w_out, mla_w_uq, mla_w_ukv]
    big_names = ["w_in", "w_out", "mla_w_uq", "mla_w_ukv"]

    def local_blocks(l):
        return [w[l].astype(MXU) for w in big]

    def put_own(lands, own):
        return [lax.dynamic_update_index_in_dim(ld, o, me, 0) for ld, o in zip(lands, own)]

    def layer_weights(l, gw_in=None, gw_out=None, gw_uq=None, gw_ukv=None):
        small = dict(norm_g=norm_g[l], gla_wg_f=wgf_full[l], gla_bg_f=gla_bg_f[l], gla_wg_b=wgb_full[l],
                     gla_bg_b=gla_bg_b[l], gla_norm_g=gla_norm_g[l], mla_q_norm_g=mla_q_norm_g[l],
                     mla_kv_norm_g=mla_kv_norm_g[l], mla_out_g=mla_out_g[l], conv_w=convw_full[l],
                     conv_out_g=conv_out_g[l])
        return _prep_layer_weights(
            None if gw_in is None else _perm_gathered(gw_in, IN_SEGS, PW),
            None if gw_out is None else gw_out.reshape((-1,) + gw_out.shape[2:]),
            None if gw_uq is None else _perm_gathered(gw_uq, UQ_SEGS, MQW),
            None if gw_ukv is None else _gather_cols(gw_ukv, mla_w_ukv.shape[2]), small)

    def land_shapes(blocks, slots):
        return [jax.ShapeDtypeStruct((slots,) + b.shape, b.dtype) for b in blocks]

    def slots_of(blocks):
        return [(N_DEV,) + b.shape for b in blocks]

    def forwarded(lands, blocks, tag):
        return put_own(_exchange_hbm(AG_FORWARD, None, lands, f"gather_{tag}_forward"), blocks)

    first = local_blocks(0)
    w_in_start = _plan_start(AG_SPREAD, first[:1], slots_of(first[:1]), mod_mine, "gather_w_in_l0_start")
    adam_w_in = [a + w_in_start[-1][0, 0] for a in (w_in, m_w_in, v_w_in)]
    (gw_in,) = forwarded(*reversed(_plan_wait(AG_SPREAD, w_in_start, adam_w_in, "gather_w_in_l0_wait")), "w_in_l0")
    rest = _plan_start(AG_SPREAD, first[1:], slots_of(first[1:]), gw_in, "gather_rest_l0_start")
    h = x[0]
    saved, layers, mods = [], [], []
    pending = {}
    for l in range(nl):
        shift, scale, gate = (mod_mine[l, i * d:(i + 1) * d].reshape(1, d) for i in range(3))
        nxt = local_blocks(l + 1) if l + 1 < nl else None

        def start_next(after, wt_late, l=l, nxt=nxt):
            if nxt is not None:
                pending[l + 1] = _plan_start(AG_SPREAD, nxt, slots_of(nxt), after, f"gather_weights_l{l + 1}_start")
                wt_late["q_norm_g"] = layers[l]["q_norm_g"] + pending[l + 1][-1][0, 0]
            return wt_late

        if l == 0:
            in_after = (rest[-1],)
            layers.append(layer_weights(0, gw_in))

            def late(proj):
                got = forwarded(*reversed(_plan_wait(AG_SPREAD, rest, [proj], "gather_rest_l0_wait")), "rest_l0")
                full = layer_weights(0, None, *got)
                return start_next(got[0], {k: full[k] for k in ("w_out", "w_uq", "w_ukv")})
        else:
            got = forwarded(*reversed(_plan_wait(AG_SPREAD, pending.pop(l), [h], f"gather_weights_l{l}_wait")), f"weights_l{l}")
            layers.append(layer_weights(l, *got))
            in_after = ()

            def late(proj):
                return start_next(proj, {})
        mods.append((shift, scale, gate))
        h, sv = _layer_fwd(h, mods[l], layers[l], cos, sin, f"l{l}", late, in_after)
        saved.append(sv)
        blocks = nxt
    loss_part, d_h, d_final_g = _final_loss(h, final_g.reshape(1, d), loss_target[0], "final_loss")
    loss = lax.psum(loss_part[0, 0], ("x", "y", "c"))
    shift, scale, gate = mods[-1]
    mods[-1] = (shift, scale, gate + 0.0 * loss)

    send_of = dict(w_in=lambda g: _scatter_perm(g, IN_SEGS, w_in.shape[2]),
                   w_out=lambda g: g.reshape((N_DEV,) + w_out.shape[1:]),
                   w_uq=lambda g: _scatter_perm(g, UQ_SEGS, mla_w_uq.shape[2]),
                   w_ukv=lambda g: _scatter_cols(g, mla_w_ukv.shape[2]))

    def grad_sends(gr):
        return [send_of[k](g).astype(MXU) for k, g in gr.items()]

    my_chip = me // 2
    my_core = (me % 2).astype(jnp.int32).reshape(1)

    def chip_sums(gr, tag):
        sends = grad_sends(gr)
        got = _exchange_hbm(RS_PAIR, sends, land_shapes([sd[0] for sd in sends], N_DEV // 2), f"scatter_grads_{tag}_pair")
        return [_pair_sum(sd, gt, my_core, f"pair_sum_{k}_{tag}") for sd, gt, k in zip(sends, got, gr)]

    def with_own_chip(lands, sums):
        return [lax.dynamic_update_index_in_dim(ld, lax.dynamic_index_in_dim(sm, my_chip, axis=0, keepdims=False),
                                                my_chip, 0) for ld, sm in zip(lands, sums)]

    small_names = ["norm_g", "gla_wg_f", "gla_bg_f", "gla_wg_b", "gla_bg_b", "gla_norm_g", "mla_q_norm_g",
                   "mla_kv_norm_g", "mla_out_g", "conv_w", "conv_out_g"]
    d_mods, grads, recv = [None] * nl, [None] * nl, [None] * nl
    flight = {}
    small = {}

    def gather_small(d_x, d_mod0, gr0):
        d_mods[0], grads[0] = d_mod0, _natural_small(gr0)
        d_mod_mine = jnp.stack([jnp.concatenate(d_mods[l], axis=-1)[0] for l in range(nl)])
        parts = [d_mod_mine] + [jnp.stack([grads[l][n] for l in range(nl)]) for n in small_names] + [d_final_g]
        (g2,) = _exchange([_pack(parts)], "gather_small_grads", False, pltpu.VMEM)
        small["d_mod_all"] = g2.reshape(N_DEV, -1)[:, :d_mod_mine.size].reshape(N_DEV, nl, 3 * d)
        small["summed"] = dict(zip(["ada_b"] + small_names + ["final_g"],
                                   _unpack(_sum_devices(g2, "sum_small_grads"), [p.shape for p in parts])))
        return (g2,)

    pairs = {}
    def end_flight(key, after, name):
        sm, lands = _plan_wait(RS_CHIPS, flight.pop(key)[0], after, name)
        return with_own_chip(lands, sm)

    for l in reversed(range(nl)):
        def ship(big_grads, l=l):
            if l > 0:
                sends = grad_sends(big_grads)
                pairs[l] = (_plan_start(RS_PAIR, sends, [(N_DEV // 2,) + sd.shape[1:] for sd in sends],
                                        big_grads["w_in"], f"scatter_grads_l{l}_pair_start"), sends)
                return (pairs[l][0][-1],)
            sm = chip_sums(dict(w_in=big_grads["w_in"]), f"l{l}")
            flight[l] = (_plan_start(RS_CHIPS, sm, [a.shape for a in sm], big_grads["w_in"], f"scatter_grads_l{l}_start"),
                         sm)
            return (flight[l][0][-1],)

        def ship_rest(rest_grads, l=l):
            if l + 1 in flight:
                recv[l + 1] = end_flight(l + 1, list(rest_grads.values()), f"scatter_grads_l{l + 1}_wait")
            sm = chip_sums(rest_grads, f"l{l}_rest")
            flight["rest"] = (_plan_start(RS_CHIPS, sm, [a.shape for a in sm], rest_grads["w_out"],
                                          f"scatter_grads_l{l}_rest_start"), sm)
            return flight["rest"][0][-1][0, 0]

        shift, scale, gate = mods[l]
        if l + 1 in flight:
            gate = gate + flight[l + 1][0][-1][0, 0]
        if l > 0:
            d_h, d_mods[l], gr = _layer_bwd(d_h, saved[l], (shift, scale, gate), layers[l], cos, sin, f"l{l}", ship)
            grads[l] = _natural_small(gr)
            sends, got = _plan_wait(RS_PAIR, pairs.pop(l)[0], [d_h], f"scatter_grads_l{l}_pair_wait")
            sm = [_pair_sum(sd, gt, my_core, f"pair_sum_{n}_l{l}") for sd, gt, n in zip(sends, got, big_names)]
            flight[l] = (_plan_start(RS_CHIPS, sm, [a.shape for a in sm], d_h, f"scatter_grads_l{l}_start"), sm)
        else:
            d_h, _, _ = _layer_bwd(d_h, saved[l], (shift, scale, gate), layers[l], cos, sin, f"l{l}", ship, gather_small,
                                   ship_rest)
    pending = flight[0][0]
    grad_x = d_h[None]
    summed = small["summed"]
    summed["gla_wg_f"] = _my_cols(summed["gla_wg_f"], me, wgc)
    summed["gla_wg_b"] = _my_cols(summed["gla_wg_b"], me, wgc)
    summed["conv_w"] = _my_cols(summed["conv_w"], me, cwc)

    d_mod_cols = jnp.moveaxis(_my_cols(small["d_mod_all"], me, ada_cols), 0, 1) + pending[-1][0, 0]
    out = {}
    out["ada_w"] = _ada_grad_adam(c_act, d_mod_cols, ada_w, m_ada_w, v_ada_w, "ada_grad_adam")

    given = dict(ada_b=(ada_b, m_ada_b, v_ada_b), norm_g=(norm_g, m_norm_g, v_norm_g),
                 gla_wg_f=(gla_wg_f, m_gla_wg_f, v_gla_wg_f), gla_bg_f=(gla_bg_f, m_gla_bg_f, v_gla_bg_f),
                 gla_wg_b=(gla_wg_b, m_gla_wg_b, v_gla_wg_b), gla_bg_b=(gla_bg_b, m_gla_bg_b, v_gla_bg_b),
                 gla_norm_g=(gla_norm_g, m_gla_norm_g, v_gla_norm_g),
                 mla_q_norm_g=(mla_q_norm_g, m_mla_q_norm_g, v_mla_q_norm_g),
                 mla_kv_norm_g=(mla_kv_norm_g, m_mla_kv_norm_g, v_mla_kv_norm_g),
                 mla_out_g=(mla_out_g, m_mla_out_g, v_mla_out_g), conv_w=(conv_w, m_conv_w, v_conv_w),
                 conv_out_g=(conv_out_g, m_conv_out_g, v_conv_out_g), final_g=(final_g, m_final_g, v_final_g))
    names = list(given)

    def two_d(a):
        return a.reshape(1, -1) if a.ndim == 1 else a

    g_nat = [summed[n].reshape(given[n][0].shape) for n in names]
    res = _adam_small([two_d(given[n][0]) for n in names], [two_d(g) for g in g_nat],
                      [two_d(given[n][1]) for n in names], [two_d(given[n][2]) for n in names], "adam_small")
    for i, n in enumerate(names):
        out[n] = (g_nat[i],) + tuple(r[i].reshape(given[n][0].shape) for r in res)

    state = dict(w_in=adam_w_in, w_out=(w_out, m_w_out, v_w_out), mla_w_uq=(mla_w_uq, m_mla_w_uq, v_mla_w_uq),
                 mla_w_ukv=(mla_w_ukv, m_mla_w_ukv, v_mla_w_ukv))
    done = [out["ada_w"][0], res[0][0]]
    for l in reversed(range(nl)):
        if l == 0:
            rest = end_flight("rest", done, "scatter_grads_l0_rest_wait")
            recv[0] = end_flight(0, done + rest[:1], "scatter_grads_l0_wait") + rest
        for i, n in enumerate(big_names):
            out[n] = _adam_big(recv[l][i], *state[n], l, out.get(n), f"adam_{n}_l{l}",
                               (pending[-1],))
        done = done + [out[n][0] for n in big_names]

    order = ["ada_w", "ada_b", "norm_g", "w_in", "gla_wg_f", "gla_bg_f", "gla_wg_b", "gla_bg_b", "gla_norm_g",
             "mla_q_norm_g", "mla_kv_norm_g", "mla_w_uq", "mla_w_ukv", "mla_out_g", "conv_w", "conv_out_g", "w_out",
             "final_g"]
    return (loss, grad_x, *[out[n][0] for n in order], *[out[n][1] for n in order], *[out[n][2] for n in order],
            *[out[n][3] for n in order])
```

```python
import functools

import jax
import jax.numpy as jnp
from jax import lax
from jax.experimental import pallas as pl
from jax.experimental.pallas import tpu as pltpu

F32 = jnp.float32
MXU = jnp.bfloat16
HI = lax.Precision.HIGHEST
N_DEV = 8
MESH = pl.DeviceIdType.MESH

D_MIX = 2048
GH, GDK, GDV = 6, 64, 128
GW = GH * GDV
GQK = GH * GDK
GRANK = 16
GTEMP = 16.0
CHUNK = 64
MH, MQL, MKVL, MNOPE, MROPE, MDV = 6, 384, 256, 128, 64, 128
MW = MH * MDV
MQW = MH * (MNOPE + MROPE)
MKVW = MH * (MNOPE + MDV)
CONV_CH = 512
ROPE_THETA = 10000.0
EPS = 1e-6
IN_DIM = 5856
OZ, OCB, OCC, OCX, OMKV, OGV, OGQ, OGK, OMQ, OT = 0, 2048, 2560, 3072, 3584, 3840, 4608, 4992, 5376, 5760
PW = 5888
LANES = 128
V7X_VMEM_BYTES = 64 * 1024 * 1024
VMEM_LIMIT = V7X_VMEM_BYTES * 7 // 8

ADAM_LR, ADAM_B1, ADAM_B2, ADAM_EPS, ADAM_WD, ADAM_STEP = 0.001, 0.9, 0.999, 1e-08, 0.01, 10


def _cp(sem=None):
    return pltpu.CompilerParams(dimension_semantics=sem, vmem_limit_bytes=VMEM_LIMIT)


def _dot(a, b):
    return jnp.dot(a.astype(MXU), b.astype(MXU), preferred_element_type=F32)


def _dot_nt(a, b):
    return lax.dot_general(a.astype(MXU), b.astype(MXU), (((1,), (1,)), ((), ())), preferred_element_type=F32)


def _dot_tn(a, b):
    return lax.dot_general(a.astype(MXU), b.astype(MXU), (((0,), (0,)), ((), ())), preferred_element_type=F32)


def _dotf(a, b):
    return jnp.dot(a, b, precision=HI, preferred_element_type=F32)


def _dotf_tn(a, b):
    return lax.dot_general(a, b, (((0,), (0,)), ((), ())), precision=HI, preferred_element_type=F32)


def _split3(x):
    hi = x.astype(jnp.bfloat16)
    r1 = x - hi.astype(F32)
    mid = r1.astype(jnp.bfloat16)
    lo = (r1 - mid.astype(F32)).astype(jnp.bfloat16)
    return hi, mid, lo


def _cum_dot(cum, x, transpose=False):
    dn = (((0,), (0,)), ((), ())) if transpose else (((1,), (0,)), ((), ()))
    cb = cum.astype(jnp.bfloat16)
    parts = [lax.dot_general(cb, p, dn, preferred_element_type=F32) for p in _split3(x)]
    return parts[0] + parts[1] + parts[2]


def _rows(s):
    return min(256, s)


def _rms(x, g):
    r = lax.rsqrt(jnp.mean(x * x, axis=-1, keepdims=True) + EPS)
    return x * r * g


def _rms_bwd(dy, x, g):
    r = lax.rsqrt(jnp.mean(x * x, axis=-1, keepdims=True) + EPS)
    xh = x * r
    dxh = dy * g
    dg = jnp.sum(dy * xh, axis=0, keepdims=True)
    dx = r * (dxh - xh * jnp.mean(dxh * xh, axis=-1, keepdims=True))
    return dx, dg


def _sigmoid(z):
    return jax.nn.sigmoid(z)


def _matmul(a, b, *, dims, tm, tn, tk, out_dtypes, name, epilogue=None, extras=(), extra_kinds=(), after=()):
    if dims == "nn":
        (m, k), n, mul = a.shape, b.shape[1], _dot
    elif dims == "nt":
        (m, k), n, mul = a.shape, b.shape[0], _dot_nt
    else:
        (k, m), n, mul = a.shape, b.shape[1], _dot_tn
    tm, tn, tk = min(tm, m), min(tn, n), min(tk, k)
    assert m % tm == 0 and n % tn == 0 and k % tk == 0, (m, n, k, tm, tn, tk)
    if dims == "nn":
        a_spec = pl.BlockSpec((tm, tk), lambda i, j, kk: (i, kk))
        b_spec = pl.BlockSpec((tk, tn), lambda i, j, kk: (kk, j))
    elif dims == "nt":
        a_spec = pl.BlockSpec((tm, tk), lambda i, j, kk: (i, kk))
        b_spec = pl.BlockSpec((tn, tk), lambda i, j, kk: (j, kk))
    else:
        a_spec = pl.BlockSpec((tk, tm), lambda i, j, kk: (kk, i))
        b_spec = pl.BlockSpec((tk, tn), lambda i, j, kk: (kk, j))
    nk = k // tk
    n_extra = len(extras)
    n_out = len(out_dtypes)
    n_after = len(after)
    extra_specs = []
    for kind in extra_kinds:
        if kind == "mn":
            extra_specs.append(pl.BlockSpec((tm, tn), lambda i, j, kk: (i, j)))
        else:
            extra_specs.append(pl.BlockSpec((1, tn), lambda i, j, kk: (0, j)))

    def finish(res, ex, outs):
        vals = (res,) if epilogue is None else epilogue(res, *[e[...] for e in ex])
        for o, v in zip(outs, vals):
            o[...] = v.astype(o.dtype)

    def body(*refs):
        a_ref, b_ref = refs[0], refs[1]
        ex = refs[2:2 + n_extra]
        outs = refs[2 + n_extra + n_after:2 + n_extra + n_after + n_out]
        if nk == 1:
            finish(mul(a_ref[...], b_ref[...]), ex, outs)
            return
        acc = refs[-1]
        kk = pl.program_id(2)

        @pl.when(kk == 0)
        def _():
            acc[...] = jnp.zeros_like(acc)

        acc[...] += mul(a_ref[...], b_ref[...])

        @pl.when(kk == nk - 1)
        def _():
            finish(acc[...], ex, outs)

    out_spec = pl.BlockSpec((tm, tn), lambda i, j, kk: (i, j))
    res = pl.pallas_call(
        body, grid=(m // tm, n // tn, nk),
        in_specs=[a_spec, b_spec] + extra_specs + [pl.BlockSpec(memory_space=pl.ANY)] * n_after,
        out_specs=[out_spec] * n_out,
        out_shape=[jax.ShapeDtypeStruct((m, n), dt) for dt in out_dtypes],
        scratch_shapes=[] if nk == 1 else [pltpu.VMEM((tm, tn), F32)],
        name=name, compiler_params=_cp(("parallel", "parallel", "arbitrary")),
    )(a, b, *extras, *after)
    return res


def _norm_mod(x, g, scale, shift, name):
    s, d = x.shape
    tr = _rows(s)

    def body(x_ref, g_ref, sc_ref, sh_ref, h_ref):
        h = _rms(x_ref[...], g_ref[...]) * (1.0 + sc_ref[...]) + sh_ref[...]
        h_ref[...] = h.astype(h_ref.dtype)

    row = pl.BlockSpec((tr, d), lambda i: (i, 0))
    vec = pl.BlockSpec((1, d), lambda i: (0, 0))
    return pl.pallas_call(body, grid=(s // tr,), in_specs=[row, vec, vec, vec], out_specs=row,
                          out_shape=jax.ShapeDtypeStruct((s, d), MXU), name=name,
                          compiler_params=_cp(("parallel",)))(x, g, scale, shift)


def _norm_mod_bwd(d_h, x, d_out, g, scale, name):
    s, d = x.shape
    tr = _rows(s)

    def body(dh_ref, x_ref, do_ref, g_ref, sc_ref, dx_ref, dsh_ref, dsc_ref, dg_ref):
        i = pl.program_id(0)

        @pl.when(i == 0)
        def _():
            dsh_ref[...] = jnp.zeros_like(dsh_ref)
            dsc_ref[...] = jnp.zeros_like(dsc_ref)
            dg_ref[...] = jnp.zeros_like(dg_ref)

        dh = dh_ref[...]
        xv = x_ref[...]
        gv = g_ref[...]
        r = lax.rsqrt(jnp.mean(xv * xv, axis=-1, keepdims=True) + EPS)
        xh = xv * r
        dsh_ref[...] += jnp.sum(dh, axis=0, keepdims=True)
        dsc_ref[...] += jnp.sum(dh * (xh * gv), axis=0, keepdims=True)
        dhn = dh * (1.0 + sc_ref[...])
        dg_ref[...] += jnp.sum(dhn * xh, axis=0, keepdims=True)
        dxh = dhn * gv
        dx_ref[...] = do_ref[...] + r * (dxh - xh * jnp.mean(dxh * xh, axis=-1, keepdims=True))

    row = pl.BlockSpec((tr, d), lambda i: (i, 0))
    vec = pl.BlockSpec((1, d), lambda i: (0, 0))
    vshape = jax.ShapeDtypeStruct((1, d), F32)
    return pl.pallas_call(body, grid=(s // tr,), in_specs=[row, row, row, vec, vec],
                          out_specs=[row, vec, vec, vec],
                          out_shape=[jax.ShapeDtypeStruct((s, d), F32), vshape, vshape, vshape],
                          name=name, compiler_params=_cp(("arbitrary",)))(d_h, x, d_out, g, scale)


def _gate_bwd(d_out, u, gate, name):
    s, d = d_out.shape
    tr = _rows(s)

    def body(do_ref, u_ref, gt_ref, du_ref, dgt_ref):
        @pl.when(pl.program_id(0) == 0)
        def _():
            dgt_ref[...] = jnp.zeros_like(dgt_ref)

        do = do_ref[...]
        du_ref[...] = (do * gt_ref[...]).astype(du_ref.dtype)
        dgt_ref[...] += jnp.sum(do * u_ref[...], axis=0, keepdims=True)

    row = pl.BlockSpec((tr, d), lambda i: (i, 0))
    vec = pl.BlockSpec((1, d), lambda i: (0, 0))
    return pl.pallas_call(body, grid=(s // tr,), in_specs=[row, row, vec], out_specs=[row, vec],
                          out_shape=[jax.ShapeDtypeStruct((s, d), MXU), jax.ShapeDtypeStruct((1, d), F32)],
                          name=name, compiler_params=_cp(("arbitrary",)))(d_out, u, gate)


def _final_loss(x, g, target, name):
    s, d = x.shape
    tr = _rows(s)

    def body(x_ref, g_ref, t_ref, loss_ref, dx_ref, dg_ref):
        @pl.when(pl.program_id(0) == 0)
        def _():
            loss_ref[...] = jnp.zeros_like(loss_ref)
            dg_ref[...] = jnp.zeros_like(dg_ref)

        xv = x_ref[...]
        gv = g_ref[...]
        diff = _rms(xv, gv) - t_ref[...]
        part = 0.5 * jnp.sum(jnp.sum(diff * diff, axis=-1, keepdims=True) / d, axis=0, keepdims=True)
        loss_ref[...] += jnp.broadcast_to(part, loss_ref.shape)
        dx, dg = _rms_bwd(diff / d, xv, gv)
        dx_ref[...] = dx
        dg_ref[...] += dg

    row = pl.BlockSpec((tr, d), lambda i: (i, 0))
    vec = pl.BlockSpec((1, d), lambda i: (0, 0))
    lvec = pl.BlockSpec((1, LANES), lambda i: (0, 0))
    return pl.pallas_call(body, grid=(s // tr,), in_specs=[row, vec, row], out_specs=[lvec, row, vec],
                          out_shape=[jax.ShapeDtypeStruct((1, LANES), F32), jax.ShapeDtypeStruct((s, d), F32),
                                     jax.ShapeDtypeStruct((1, d), F32)],
                          name=name, compiler_params=_cp(("arbitrary",)))(x, g, target)


def _shift_rows(u, s, down):
    ri = lax.broadcasted_iota(jnp.int32, u.shape, 0)
    if down:
        return jnp.where(ri == 0, 0.0, pltpu.roll(u, 1, 0))
    return jnp.where(ri == s - 1, 0.0, pltpu.roll(u, s - 1, 0))


def _conv_fwd(proj, conv_w, name):
    s = proj.shape[0]
    nt = CONV_CH // LANES

    def body(cb_ref, cc_ref, cx_ref, w_ref, pre_ref):
        u = cc_ref[...] * cx_ref[...]
        conv = _shift_rows(u, s, True) * w_ref[0:1, :] + u * w_ref[1:2, :] + _shift_rows(u, s, False) * w_ref[2:3, :]
        pre_ref[...] = cb_ref[...] * conv

    def col(off):
        return pl.BlockSpec((s, LANES), lambda j: (0, off // LANES + j))

    return pl.pallas_call(body, grid=(nt,), in_specs=[col(OCB), col(OCC), col(OCX), pl.BlockSpec((3, LANES), lambda j: (0, j))],
                          out_specs=pl.BlockSpec((s, LANES), lambda j: (0, j)),
                          out_shape=jax.ShapeDtypeStruct((s, CONV_CH), F32), name=name,
                          compiler_params=_cp(("parallel",)))(proj, proj, proj, conv_w)


def _conv_bwd(proj, conv_w, d_pre, name):
    s = proj.shape[0]
    nt = CONV_CH // LANES

    def body(cb_ref, cc_ref, cx_ref, w_ref, dp_ref, dcb_ref, dcc_ref, dcx_ref, dw_ref):
        cc, cx = cc_ref[...], cx_ref[...]
        u = cc * cx
        up, dn = _shift_rows(u, s, True), _shift_rows(u, s, False)
        w0, w1, w2 = w_ref[0:1, :], w_ref[1:2, :], w_ref[2:3, :]
        conv = up * w0 + u * w1 + dn * w2
        dp = dp_ref[...]
        dcb_ref[...] = (dp * conv).astype(dcb_ref.dtype)
        dconv = dp * cb_ref[...]
        du = _shift_rows(dconv, s, False) * w0 + dconv * w1 + _shift_rows(dconv, s, True) * w2
        dcc_ref[...] = (du * cx).astype(dcc_ref.dtype)
        dcx_ref[...] = (du * cc).astype(dcx_ref.dtype)
        dw_ref[0:1, :] = jnp.sum(dconv * up, axis=0, keepdims=True)
        dw_ref[1:2, :] = jnp.sum(dconv * u, axis=0, keepdims=True)
        dw_ref[2:3, :] = jnp.sum(dconv * dn, axis=0, keepdims=True)

    def col(off):
        return pl.BlockSpec((s, LANES), lambda j: (0, off // LANES + j))

    blk = pl.BlockSpec((s, LANES), lambda j: (0, j))
    wblk = pl.BlockSpec((3, LANES), lambda j: (0, j))
    full = jax.ShapeDtypeStruct((s, CONV_CH), MXU)
    return pl.pallas_call(body, grid=(nt,), in_specs=[col(OCB), col(OCC), col(OCX), wblk, blk],
                          out_specs=[blk, blk, blk, wblk],
                          out_shape=[full, full, full, jax.ShapeDtypeStruct((3, CONV_CH), F32)],
                          name=name, compiler_params=_cp(("parallel",)))(proj, proj, proj, conv_w, d_pre)


GLA_SUB = 8


def _gla_gates(t_ref, wg_ref, bg_ref):
    t = t_ref[...]
    a = _dot(t, wg_ref[...]) + bg_ref[...]
    la = (jnp.minimum(a, 0.0) - jnp.log(1.0 + jnp.exp(-jnp.abs(a)))) / GTEMP
    return t, a, la


def _gla_masks(reverse):
    ri = lax.broadcasted_iota(jnp.int32, (CHUNK, CHUNK), 0)
    ci = lax.broadcasted_iota(jnp.int32, (CHUNK, CHUNK), 1)
    if reverse:
        cum, mask, mask_t = ci >= ri, ci > ri, ri > ci
    else:
        cum, mask, mask_t = ci <= ri, ci <= ri, ri <= ci
    return cum.astype(F32), mask, mask_t


def _gla_specs(s, reverse):
    nsub = min(GLA_SUB, s // CHUNK)
    nsteps = s // (CHUNK * nsub)

    def row(n):
        return nsteps - 1 - n if reverse else n

    def chunk(pi):
        return nsub - 1 - pi if reverse else pi

    return nsub, nsteps, row, chunk


def _gla_fwd(proj, wg_pad, bg, reverse, name):
    s = proj.shape[0]
    nsub, nsteps, row, chunk = _gla_specs(s, reverse)
    rb = nsub * CHUNK

    def body(q_ref, k_ref, v_ref, t_ref, wg_ref, bg_ref, o_ref, st_ref, state):
        @pl.when(pl.program_id(0) == 0)
        def _():
            state[...] = jnp.zeros_like(state)

        _, _, la = _gla_gates(t_ref, wg_ref, bg_ref)
        cumf, mask, _ = _gla_masks(reverse)
        lane = lax.broadcasted_iota(jnp.int32, (CHUNK, LANES), 1)
        for pi in range(nsub):
            rows = slice(chunk(pi) * CHUNK, (chunk(pi) + 1) * CHUNK)
            la_c = la[rows]
            b_all = _cum_dot(cumf, la_c)
            bl_all = jnp.sum(la_c, axis=0, keepdims=True)
            for p in range(GH // 2):
                sl = slice(p * LANES, (p + 1) * LANES)
                b, bl = b_all[:, sl], bl_all[:, sl]
                qd = q_ref[rows, sl] * (GDK ** -0.5) * jnp.exp(b)
                ki = k_ref[rows, sl] * jnp.exp(-b)
                kte = k_ref[rows, sl] * jnp.exp(bl - b)
                decay = jnp.exp(bl)
                for half in range(2):
                    h = 2 * p + half
                    lm = (lane < GDK) if half == 0 else (lane >= GDK)
                    qd_h = jnp.where(lm, qd, 0.0)
                    kte_h = jnp.where(lm, kte, 0.0)
                    v_h = v_ref[rows, h * GDV:(h + 1) * GDV]
                    st = state[h]
                    a_mat = jnp.where(mask, _dot_nt(qd_h, ki), 0.0)
                    o_ref[rows, h * GDV:(h + 1) * GDV] = _dot(a_mat, v_h) + _dot_nt(qd_h, st)
                    st_ref[pi, h] = st
                    state[h] = st * decay + _dot_tn(v_h, kte_h)

    return pl.pallas_call(
        body, grid=(nsteps,),
        in_specs=[pl.BlockSpec((rb, GQK), lambda n: (row(n), OGQ // GQK)),
                  pl.BlockSpec((rb, GQK), lambda n: (row(n), OGK // GQK)),
                  pl.BlockSpec((rb, GW), lambda n: (row(n), OGV // GW)),
                  pl.BlockSpec((rb, LANES), lambda n: (row(n), OT // LANES)),
                  pl.BlockSpec((LANES, GQK), lambda n: (0, 0)),
                  pl.BlockSpec((1, GQK), lambda n: (0, 0))],
        out_specs=[pl.BlockSpec((rb, GW), lambda n: (row(n), 0)),
                   pl.BlockSpec((nsub, GH, GDV, LANES), lambda n: (n, 0, 0, 0))],
        out_shape=[jax.ShapeDtypeStruct((s, GW), F32), jax.ShapeDtypeStruct((s // CHUNK, GH, GDV, LANES), F32)],
        scratch_shapes=[pltpu.VMEM((GH, GDV, LANES), F32)],
        name=name, compiler_params=_cp(("arbitrary",)))(proj, proj, proj, proj, wg_pad, bg)


def _gla_bwd(proj, wg_pad, bg, states, d_o, reverse, name):
    s = proj.shape[0]
    nsub, nsteps, row, chunk = _gla_specs(s, reverse)
    rb = nsub * CHUNK

    def body(q_ref, k_ref, v_ref, t_ref, wg_ref, bg_ref, st_ref, do_ref,
             dq_ref, dk_ref, dv_ref, dt_ref, dwg_ref, dbg_ref, dstate, da_buf):
        @pl.when(pl.program_id(0) == 0)
        def _():
            dstate[...] = jnp.zeros_like(dstate)
            dwg_ref[...] = jnp.zeros_like(dwg_ref)
            dbg_ref[...] = jnp.zeros_like(dbg_ref)

        t, a, la = _gla_gates(t_ref, wg_ref, bg_ref)
        cumf, mask, mask_t = _gla_masks(reverse)
        lane = lax.broadcasted_iota(jnp.int32, (CHUNK, LANES), 1)
        for pi in reversed(range(nsub)):
            rows = slice(chunk(pi) * CHUNK, (chunk(pi) + 1) * CHUNK)
            la_c = la[rows]
            b_all = _cum_dot(cumf, la_c)
            bl_all = jnp.sum(la_c, axis=0, keepdims=True)
            for p in range(GH // 2):
                sl = slice(p * LANES, (p + 1) * LANES)
                b, bl = b_all[:, sl], bl_all[:, sl]
                e, ei, ee = jnp.exp(b), jnp.exp(-b), jnp.exp(bl - b)
                qd = q_ref[rows, sl] * (GDK ** -0.5) * e
                ki, kte = k_ref[rows, sl] * ei, k_ref[rows, sl] * ee
                decay = jnp.exp(bl)
                dqd = jnp.zeros((CHUNK, LANES), F32)
                dki = jnp.zeros((CHUNK, LANES), F32)
                dkte = jnp.zeros((CHUNK, LANES), F32)
                ddecay = jnp.zeros((1, LANES), F32)
                for half in range(2):
                    h = 2 * p + half
                    lm = (lane < GDK) if half == 0 else (lane >= GDK)
                    qd_h = jnp.where(lm, qd, 0.0)
                    ki_h = jnp.where(lm, ki, 0.0)
                    kte_h = jnp.where(lm, kte, 0.0)
                    v_h = v_ref[rows, h * GDV:(h + 1) * GDV]
                    do_h = do_ref[rows, h * GDV:(h + 1) * GDV]
                    st = st_ref[pi, h]
                    dst = dstate[h]
                    at_mat = jnp.where(mask_t, _dot_nt(ki_h, qd_h), 0.0)
                    da_mat = jnp.where(mask, _dot_nt(do_h, v_h), 0.0)
                    dat_mat = jnp.where(mask_t, _dot_nt(v_h, do_h), 0.0)
                    dv_ref[rows, h * GDV:(h + 1) * GDV] = _dot(at_mat, do_h) + _dot_nt(kte_h, dst)
                    dqd += _dot(da_mat, ki_h) + _dot(do_h, st)
                    dki += _dot(dat_mat, qd_h)
                    dkte += _dot(v_h, dst)
                    ddecay += jnp.sum(dst * st, axis=0, keepdims=True)
                    dstate[h] = dst * decay + _dot_tn(do_h, qd_h)
                dq_ref[rows, sl] = dqd * e * (GDK ** -0.5)
                dk_ref[rows, sl] = dki * ei + dkte * ee
                db = dqd * qd - dki * ki - dkte * kte
                dbl = jnp.sum(dkte * kte, axis=0, keepdims=True) + decay * ddecay
                da_buf[rows, sl] = _cum_dot(cumf, db, True) + dbl
        da = da_buf[...] * (1.0 / GTEMP) * _sigmoid(-a)
        dt_ref[...] = _dot_nt(da, wg_ref[...])
        dwg_ref[...] += _dot_tn(t, da)
        dbg_ref[...] += jnp.sum(da, axis=0, keepdims=True)

    def prow(j):
        return row(nsteps - 1 - j)

    return pl.pallas_call(
        body, grid=(nsteps,),
        in_specs=[pl.BlockSpec((rb, GQK), lambda j: (prow(j), OGQ // GQK)),
                  pl.BlockSpec((rb, GQK), lambda j: (prow(j), OGK // GQK)),
                  pl.BlockSpec((rb, GW), lambda j: (prow(j), OGV // GW)),
                  pl.BlockSpec((rb, LANES), lambda j: (prow(j), OT // LANES)),
                  pl.BlockSpec((LANES, GQK), lambda j: (0, 0)),
                  pl.BlockSpec((1, GQK), lambda j: (0, 0)),
                  pl.BlockSpec((nsub, GH, GDV, LANES), lambda j: (nsteps - 1 - j, 0, 0, 0)),
                  pl.BlockSpec((rb, GW), lambda j: (prow(j), 0))],
        out_specs=[pl.BlockSpec((rb, GQK), lambda j: (prow(j), 0)),
                   pl.BlockSpec((rb, GQK), lambda j: (prow(j), 0)),
                   pl.BlockSpec((rb, GW), lambda j: (prow(j), 0)),
                   pl.BlockSpec((rb, LANES), lambda j: (prow(j), 0)),
                   pl.BlockSpec((LANES, GQK), lambda j: (0, 0)),
                   pl.BlockSpec((1, GQK), lambda j: (0, 0))],
        out_shape=[jax.ShapeDtypeStruct((s, GQK), F32), jax.ShapeDtypeStruct((s, GQK), F32),
                   jax.ShapeDtypeStruct((s, GW), F32), jax.ShapeDtypeStruct((s, LANES), F32),
                   jax.ShapeDtypeStruct((LANES, GQK), F32), jax.ShapeDtypeStruct((1, GQK), F32)],
        scratch_shapes=[pltpu.VMEM((GH, GDV, LANES), F32), pltpu.VMEM((rb, GQK), F32)],
        name=name, compiler_params=_cp(("arbitrary",)))(proj, proj, proj, proj, wg_pad, bg, states, d_o)


def _rot_half(x):
    lane = lax.broadcasted_iota(jnp.int32, x.shape, 1)
    first = (lane % MROPE) < (MROPE // 2)
    return jnp.where(first, -pltpu.roll(x, LANES - MROPE // 2, 1), pltpu.roll(x, MROPE // 2, 1))


def _mla_prep(proj, cos, sin, qg, kvg, w_uq, w_ukv, name):
    s = proj.shape[0]
    tr = _rows(s)

    def body(mq_ref, mkv_ref, t_ref, cos_ref, sin_ref, qg_ref, kvg_ref, wuq_ref, wukv_ref, q_ref, k_ref, v_ref):
        cosv, sinv = cos_ref[...], sin_ref[...]
        lane = lax.broadcasted_iota(jnp.int32, (tr, LANES), 1)

        def rope(xv):
            return xv * cosv + _rot_half(xv) * sinv

        qm = _dot(_rms(mq_ref[...], qg_ref[...]), wuq_ref[...])
        kv = _dot(_rms(mkv_ref[...], kvg_ref[...]), wukv_ref[...])
        kr_lo = jnp.where(lane < MROPE, rope(t_ref[...]), 0.0)
        kr_hi = pltpu.roll(kr_lo, MROPE, 1)
        for p in range(MH // 2):
            r = rope(qm[:, MW + p * LANES:MW + (p + 1) * LANES]).astype(q_ref.dtype)
            q_ref[2 * p, :, LANES:] = r
            q_ref[2 * p + 1, :, LANES:] = r
        for h in range(MH):
            q_ref[h, :, :LANES] = qm[:, h * LANES:(h + 1) * LANES].astype(q_ref.dtype)
            k_ref[h, :, :LANES] = kv[:, 2 * h * LANES:(2 * h + 1) * LANES].astype(k_ref.dtype)
            k_ref[h, :, LANES:] = (kr_lo if h % 2 == 0 else kr_hi).astype(k_ref.dtype)
            v_ref[h] = kv[:, (2 * h + 1) * LANES:(2 * h + 2) * LANES].astype(v_ref.dtype)

    def full(shape):
        return pl.BlockSpec(shape, lambda i: (0,) * len(shape))

    return pl.pallas_call(
        body, grid=(s // tr,),
        in_specs=[pl.BlockSpec((tr, MQL), lambda i: (i, OMQ // MQL)),
                  pl.BlockSpec((tr, MKVL), lambda i: (i, OMKV // MKVL)),
                  pl.BlockSpec((tr, LANES), lambda i: (i, OT // LANES)),
                  pl.BlockSpec((tr, LANES), lambda i: (i, 0)),
                  pl.BlockSpec((tr, LANES), lambda i: (i, 0)),
                  full((1, MQL)), full((1, MKVL)), full((MQL, MQW)), full((MKVL, MKVW))],
        out_specs=[pl.BlockSpec((MH, tr, 2 * LANES), lambda i: (0, i, 0)),
                   pl.BlockSpec((MH, tr, 2 * LANES), lambda i: (0, i, 0)),
                   pl.BlockSpec((MH, tr, LANES), lambda i: (0, i, 0))],
        out_shape=[jax.ShapeDtypeStruct((MH, s, 2 * LANES), MXU), jax.ShapeDtypeStruct((MH, s, 2 * LANES), MXU),
                   jax.ShapeDtypeStruct((MH, s, LANES), MXU)],
        name=name, compiler_params=_cp(("parallel",)))(proj, proj, proj, cos, sin, qg, kvg, w_uq, w_ukv)


def _mla_prep_bwd(proj, cos, sin, qg, kvg, w_uq, w_ukv, d_q, d_k, d_v, name):
    s = proj.shape[0]
    tr = _rows(s)

    def body(mq_ref, mkv_ref, cos_ref, sin_ref, qg_ref, kvg_ref, wuq_ref, wukv_ref, dq_ref, dk_ref, dv_ref,
             dmq_ref, dmkv_ref, dt_ref, dwuq_ref, dwukv_ref, dqg_ref, dkvg_ref):
        @pl.when(pl.program_id(0) == 0)
        def _():
            for r in (dwuq_ref, dwukv_ref, dqg_ref, dkvg_ref):
                r[...] = jnp.zeros_like(r)

        cosv, sinv = cos_ref[...], sin_ref[...]
        lane = lax.broadcasted_iota(jnp.int32, (tr, LANES), 1)
        lo = lane < MROPE

        def unrope(dv):
            return dv * cosv - _rot_half(dv * sinv)

        parts = [dq_ref[h, :, :LANES] for h in range(MH)]
        for p in range(MH // 2):
            parts.append(unrope(jnp.where(lo, dq_ref[2 * p, :, LANES:], dq_ref[2 * p + 1, :, LANES:])))
        d_qm = jnp.concatenate(parts, axis=1)
        mq, qgv = mq_ref[...], qg_ref[...]
        cq = _rms(mq, qgv)
        dwuq_ref[...] += _dot_tn(cq, d_qm)
        dmq, dqg = _rms_bwd(_dot_nt(d_qm, wuq_ref[...]), mq, qgv)
        dmq_ref[...] = dmq.astype(dmq_ref.dtype)
        dqg_ref[...] += dqg

        parts = []
        for h in range(MH):
            parts += [dk_ref[h, :, :LANES], dv_ref[h]]
        d_kv = jnp.concatenate(parts, axis=1)
        mkv, kvgv = mkv_ref[...], kvg_ref[...]
        ckv = _rms(mkv, kvgv)
        dwukv_ref[...] += _dot_tn(ckv, d_kv)
        dmkv, dkvg = _rms_bwd(_dot_nt(d_kv, wukv_ref[...]), mkv, kvgv)
        dmkv_ref[...] = dmkv.astype(dmkv_ref.dtype)
        dkvg_ref[...] += dkvg

        even = dk_ref[0, :, LANES:] + dk_ref[2, :, LANES:] + dk_ref[4, :, LANES:]
        odd = dk_ref[1, :, LANES:] + dk_ref[3, :, LANES:] + dk_ref[5, :, LANES:]
        d_kr = jnp.where(lo, even, 0.0) + pltpu.roll(jnp.where(lo, 0.0, odd), MROPE, 1)
        dt_ref[...] = jnp.where(lo, unrope(d_kr), 0.0)

    def full(shape):
        return pl.BlockSpec(shape, lambda i: (0,) * len(shape))

    return pl.pallas_call(
        body, grid=(s // tr,),
        in_specs=[pl.BlockSpec((tr, MQL), lambda i: (i, OMQ // MQL)),
                  pl.BlockSpec((tr, MKVL), lambda i: (i, OMKV // MKVL)),
                  pl.BlockSpec((tr, LANES), lambda i: (i, 0)),
                  pl.BlockSpec((tr, LANES), lambda i: (i, 0)),
                  full((1, MQL)), full((1, MKVL)), full((MQL, MQW)), full((MKVL, MKVW)),
                  pl.BlockSpec((MH, tr, 2 * LANES), lambda i: (0, i, 0)),
                  pl.BlockSpec((MH, tr, 2 * LANES), lambda i: (0, i, 0)),
                  pl.BlockSpec((MH, tr, LANES), lambda i: (0, i, 0))],
        out_specs=[pl.BlockSpec((tr, MQL), lambda i: (i, 0)), pl.BlockSpec((tr, MKVL), lambda i: (i, 0)),
                   pl.BlockSpec((tr, LANES), lambda i: (i, 0)),
                   full((MQL, MQW)), full((MKVL, MKVW)), full((1, MQL)), full((1, MKVL))],
        out_shape=[jax.ShapeDtypeStruct((s, MQL), MXU), jax.ShapeDtypeStruct((s, MKVL), MXU),
                   jax.ShapeDtypeStruct((s, LANES), F32),
                   jax.ShapeDtypeStruct((MQL, MQW), F32), jax.ShapeDtypeStruct((MKVL, MKVW), F32),
                   jax.ShapeDtypeStruct((1, MQL), F32), jax.ShapeDtypeStruct((1, MKVL), F32)],
        name=name, compiler_params=_cp(("arbitrary",)))(proj, proj, cos, sin, qg, kvg, w_uq, w_ukv, d_q, d_k, d_v)


ATT_SCALE = (MNOPE + MROPE) ** -0.5
ATT_SCALE_LOG2 = ATT_SCALE * 1.4426950408889634
ATT_TQ_FWD, ATT_TQ = 2048, 2048
ATT_SUB, ATT_SUB_BWD = 256, 256


def _attn_fwd(q, k, v, name):
    s = q.shape[1]
    tq = min(ATT_TQ_FWD, s)
    sub = min(ATT_SUB, tq)

    def body(q_ref, k_ref, v_ref, o_ref, lse_ref):
        for r0 in range(0, tq, sub):
            rows = slice(r0, r0 + sub)
            sc = _dot_nt(q_ref[0, rows, :], k_ref[0])
            m = jnp.max(sc, axis=-1, keepdims=True)
            p = jnp.exp2((sc - m) * ATT_SCALE_LOG2)
            l = jnp.sum(p, axis=-1, keepdims=True)
            o_ref[rows, :] = _dot(p, v_ref[0]) / l
            lse_ref[0, rows, :] = m * ATT_SCALE_LOG2 + jnp.log2(l)

    return pl.pallas_call(
        body, grid=(MH, s // tq),
        in_specs=[pl.BlockSpec((1, tq, 2 * LANES), lambda h, i: (h, i, 0)),
                  pl.BlockSpec((1, s, 2 * LANES), lambda h, i: (h, 0, 0)),
                  pl.BlockSpec((1, s, LANES), lambda h, i: (h, 0, 0))],
        out_specs=[pl.BlockSpec((tq, LANES), lambda h, i: (i, h)),
                   pl.BlockSpec((1, tq, 1), lambda h, i: (h, i, 0))],
        out_shape=[jax.ShapeDtypeStruct((s, MW), F32), jax.ShapeDtypeStruct((MH, s, 1), F32)],
        name=name, compiler_params=_cp(("parallel", "parallel")))(q, k, v)


def _attn_bwd(q, k, v, o, lse, d_o, name):
    s = q.shape[1]
    tq = min(ATT_TQ, s)
    sub = min(ATT_SUB_BWD, tq)

    def body(q_ref, k_ref, v_ref, o_ref, lse_ref, do_ref, dq_ref, dk_ref, dv_ref):
        @pl.when(pl.program_id(1) == 0)
        def _():
            dk_ref[...] = jnp.zeros_like(dk_ref)
            dv_ref[...] = jnp.zeros_like(dv_ref)

        kv = k_ref[0]
        for r0 in range(0, tq, sub):
            rows = slice(r0, r0 + sub)
            qv, do = q_ref[0, rows, :], do_ref[rows, :]
            p = jnp.exp2(_dot_nt(qv, kv) * ATT_SCALE_LOG2 - lse_ref[0, rows, :])
            delta = jnp.sum(do * o_ref[rows, :], axis=-1, keepdims=True)
            ds = p * (_dot_nt(do, v_ref[0]) - delta)
            dq_ref[0, rows, :] = _dot(ds, kv) * ATT_SCALE
            dk_ref[0] += _dot_tn(ds, qv) * ATT_SCALE
            dv_ref[0] += _dot_tn(p, do)

    return pl.pallas_call(
        body, grid=(MH, s // tq),
        in_specs=[pl.BlockSpec((1, tq, 2 * LANES), lambda h, i: (h, i, 0)),
                  pl.BlockSpec((1, s, 2 * LANES), lambda h, i: (h, 0, 0)),
                  pl.BlockSpec((1, s, LANES), lambda h, i: (h, 0, 0)),
                  pl.BlockSpec((tq, LANES), lambda h, i: (i, h)),
                  pl.BlockSpec((1, tq, 1), lambda h, i: (h, i, 0)),
                  pl.BlockSpec((tq, LANES), lambda h, i: (i, h))],
        out_specs=[pl.BlockSpec((1, tq, 2 * LANES), lambda h, i: (h, i, 0)),
                   pl.BlockSpec((1, s, 2 * LANES), lambda h, i: (h, 0, 0)),
                   pl.BlockSpec((1, s, LANES), lambda h, i: (h, 0, 0))],
        out_shape=[jax.ShapeDtypeStruct((MH, s, 2 * LANES), F32), jax.ShapeDtypeStruct((MH, s, 2 * LANES), F32),
                   jax.ShapeDtypeStruct((MH, s, LANES), F32)],
        name=name, compiler_params=_cp(("parallel", "arbitrary")))(q, k, v, o, lse, d_o)


def _merge_fwd(o_f, o_b, o_att, pre, proj, gng, mog, cog, name):
    s = proj.shape[0]
    tr = _rows(s)

    def body(of_ref, ob_ref, oa_ref, pre_ref, z_ref, gng_ref, mog_ref, cog_ref, y_ref):
        z = z_ref[...]
        sz = z * _sigmoid(z)
        osum = of_ref[...] + ob_ref[...]
        gg = gng_ref[...]
        for h in range(GH):
            sl = slice(h * GDV, (h + 1) * GDV)
            y_ref[:, sl] = (_rms(osum[:, sl], gg) * sz[:, sl]).astype(y_ref.dtype)
        y_ref[:, GW:GW + MW] = (_rms(oa_ref[...], mog_ref[...]) * sz[:, GW:GW + MW]).astype(y_ref.dtype)
        y_ref[:, GW + MW:] = (_rms(pre_ref[...], cog_ref[...]) * sz[:, GW + MW:]).astype(y_ref.dtype)

    def row(w):
        return pl.BlockSpec((tr, w), lambda i: (i, 0))

    def vec(w):
        return pl.BlockSpec((1, w), lambda i: (0, 0))

    return pl.pallas_call(
        body, grid=(s // tr,),
        in_specs=[row(GW), row(GW), row(MW), row(CONV_CH), row(D_MIX), vec(GDV), vec(MW), vec(CONV_CH)],
        out_specs=row(D_MIX), out_shape=jax.ShapeDtypeStruct((s, D_MIX), MXU),
        name=name, compiler_params=_cp(("parallel",)))(o_f, o_b, o_att, pre, proj, gng, mog, cog)


def _merge_bwd(d_y, o_f, o_b, o_att, pre, proj, gng, mog, cog, name):
    s = proj.shape[0]
    tr = _rows(s)

    def body(dy_ref, of_ref, ob_ref, oa_ref, pre_ref, z_ref, gng_ref, mog_ref, cog_ref,
             dz_ref, dos_ref, doa_ref, dpre_ref, dgng_ref, dmog_ref, dcog_ref):
        @pl.when(pl.program_id(0) == 0)
        def _():
            for r in (dgng_ref, dmog_ref, dcog_ref):
                r[...] = jnp.zeros_like(r)

        z, dy = z_ref[...], dy_ref[...]
        sg = _sigmoid(z)
        sz = z * sg
        dsz = sg * (1.0 + z * (1.0 - sg))
        dcat = dy * sz
        dyz = dy * dsz
        osum = of_ref[...] + ob_ref[...]
        gg = gng_ref[...]
        dgg = jnp.zeros_like(gg)
        for h in range(GH):
            sl = slice(h * GDV, (h + 1) * GDV)
            dz_ref[:, sl] = (dyz[:, sl] * _rms(osum[:, sl], gg)).astype(dz_ref.dtype)
            dx, dg = _rms_bwd(dcat[:, sl], osum[:, sl], gg)
            dos_ref[:, sl] = dx
            dgg += dg
        dgng_ref[...] += dgg
        sl = slice(GW, GW + MW)
        oa, mg = oa_ref[...], mog_ref[...]
        dz_ref[:, sl] = (dyz[:, sl] * _rms(oa, mg)).astype(dz_ref.dtype)
        dx, dg = _rms_bwd(dcat[:, sl], oa, mg)
        doa_ref[...] = dx
        dmog_ref[...] += dg
        sl = slice(GW + MW, D_MIX)
        pv, cg = pre_ref[...], cog_ref[...]
        dz_ref[:, sl] = (dyz[:, sl] * _rms(pv, cg)).astype(dz_ref.dtype)
        dx, dg = _rms_bwd(dcat[:, sl], pv, cg)
        dpre_ref[...] = dx
        dcog_ref[...] += dg

    def row(w):
        return pl.BlockSpec((tr, w), lambda i: (i, 0))

    def vec(w):
        return pl.BlockSpec((1, w), lambda i: (0, 0))

    def rs(w):
        return jax.ShapeDtypeStruct((s, w), F32)

    def vs(w):
        return jax.ShapeDtypeStruct((1, w), F32)

    return pl.pallas_call(
        body, grid=(s // tr,),
        in_specs=[row(D_MIX), row(GW), row(GW), row(MW), row(CONV_CH), row(D_MIX), vec(GDV), vec(MW), vec(CONV_CH)],
        out_specs=[row(D_MIX), row(GW), row(MW), row(CONV_CH), vec(GDV), vec(MW), vec(CONV_CH)],
        out_shape=[jax.ShapeDtypeStruct((s, D_MIX), MXU),
                   rs(GW), rs(MW), rs(CONV_CH), vs(GDV), vs(MW), vs(CONV_CH)],
        name=name, compiler_params=_cp(("arbitrary",)))(d_y, o_f, o_b, o_att, pre, proj, gng, mog, cog)


def _assemble_dproj(d_z, d_cb, d_cc, d_cx, d_mkv, dv_f, dv_b, dq_f, dq_b, dk_f, dk_b, d_mq, dt_m, dt_f, dt_b, name):
    s = d_z.shape[0]
    tr = _rows(s)

    def body(dz, dcb, dcc, dcx, dmkv, dvf, dvb, dqf, dqb, dkf, dkb, dmq, dtm, dtf, dtb, out):
        dt = out.dtype
        out[:, OZ:OZ + D_MIX] = dz[...].astype(dt)
        out[:, OCB:OCB + CONV_CH] = dcb[...].astype(dt)
        out[:, OCC:OCC + CONV_CH] = dcc[...].astype(dt)
        out[:, OCX:OCX + CONV_CH] = dcx[...].astype(dt)
        out[:, OMKV:OMKV + MKVL] = dmkv[...].astype(dt)
        out[:, OGV:OGV + GW] = (dvf[...] + dvb[...]).astype(dt)
        out[:, OGQ:OGQ + GQK] = (dqf[...] + dqb[...]).astype(dt)
        out[:, OGK:OGK + GQK] = (dkf[...] + dkb[...]).astype(dt)
        out[:, OMQ:OMQ + MQL] = dmq[...].astype(dt)
        out[:, OT:OT + LANES] = (dtm[...] + dtf[...] + dtb[...]).astype(dt)

    args = (d_z, d_cb, d_cc, d_cx, d_mkv, dv_f, dv_b, dq_f, dq_b, dk_f, dk_b, d_mq, dt_m, dt_f, dt_b)
    return pl.pallas_call(
        body, grid=(s // tr,),
        in_specs=[pl.BlockSpec((tr, a.shape[1]), lambda i: (i, 0)) for a in args],
        out_specs=pl.BlockSpec((tr, PW), lambda i: (i, 0)),
        out_shape=jax.ShapeDtypeStruct((s, PW), MXU), name=name, compiler_params=_cp(("parallel",)))(*args)


def _layer_fwd(x, mod, wt, cos, sin, tag, late=None, in_after=()):
    shift, scale, gate = mod
    h = _norm_mod(x, wt["norm_g"], scale, shift, f"norm_mod_{tag}")
    (proj,) = _matmul(h, wt["w_in"], dims="nn", tm=2048, tn=256, tk=2048, out_dtypes=(F32,), name=f"in_proj_{tag}",
                      after=in_after)
    if late is not None:
        wt.update(late(proj))
    o_f, st_f = _gla_fwd(proj, wt["wg_pad_f"], wt["bg_f"], False, f"gla_fwd_f_{tag}")
    o_b, st_b = _gla_fwd(proj, wt["wg_pad_b"], wt["bg_b"], True, f"gla_fwd_b_{tag}")
    q, k, v = _mla_prep(proj, cos, sin, wt["q_norm_g"], wt["kv_norm_g"], wt["w_uq"], wt["w_ukv"], f"mla_prep_{tag}")
    o_att, lse = _attn_fwd(q, k, v, f"attn_fwd_{tag}")
    pre = _conv_fwd(proj, wt["conv_w"], f"conv_fwd_{tag}")
    y = _merge_fwd(o_f, o_b, o_att, pre, proj, wt["gla_norm_g"], wt["mla_out_g"], wt["conv_out_g"], f"merge_fwd_{tag}")
    x_new, u = _matmul(y, wt["w_out"], dims="nn", tm=2048, tn=256, tk=2048, out_dtypes=(F32, F32),
                       name=f"out_proj_{tag}", epilogue=lambda acc, xv, gv: (xv + gv * acc, acc),
                       extras=(x, gate), extra_kinds=("mn", "n"))
    saved = dict(x=x, h=h, proj=proj, o_f=o_f, o_b=o_b, st_f=st_f, st_b=st_b, q=q, k=k, v=v,
                 o_att=o_att, lse=lse, pre=pre, y=y, u=u)
    return x_new, saved


def _layer_bwd(d_out, sv, mod, wt, cos, sin, tag, ship=None, dx_first=None, ship_rest=None):
    shift, scale, gate = mod
    proj = sv["proj"]
    d_u, d_gate = _gate_bwd(d_out, sv["u"], gate, f"gate_bwd_{tag}")
    (g_w_out,) = _matmul(sv["y"], d_u, dims="tn", tm=1024, tn=512, tk=2048, out_dtypes=(MXU,), name=f"out_proj_dw_{tag}")
    (d_y,) = _matmul(d_u, wt["w_out"], dims="nt", tm=2048, tn=256, tk=2048, out_dtypes=(F32,), name=f"out_proj_dx_{tag}",
                     after=(g_w_out,))
    d_z, d_osum, d_oatt, d_pre, d_gng, d_mog, d_cog = _merge_bwd(
        d_y, sv["o_f"], sv["o_b"], sv["o_att"], sv["pre"], proj, wt["gla_norm_g"], wt["mla_out_g"], wt["conv_out_g"],
        f"merge_bwd_{tag}")
    d_cb, d_cc, d_cx, d_conv_w = _conv_bwd(proj, wt["conv_w"], d_pre, f"conv_bwd_{tag}")
    d_q, d_k, d_v = _attn_bwd(sv["q"], sv["k"], sv["v"], sv["o_att"], sv["lse"], d_oatt, f"attn_bwd_{tag}")
    d_mq, d_mkv, dt_m, g_w_uq, g_w_ukv, d_qg, d_kvg = _mla_prep_bwd(
        proj, cos, sin, wt["q_norm_g"], wt["kv_norm_g"], wt["w_uq"], wt["w_ukv"], d_q, d_k, d_v, f"mla_prep_bwd_{tag}")
    bg_f, bg_b = wt["bg_f"], wt["bg_b"]
    if ship_rest is not None:
        tok = ship_rest(dict(w_out=g_w_out, w_uq=g_w_uq, w_ukv=g_w_ukv))
        bg_f, bg_b = bg_f + tok, bg_b + tok
    dq_f, dk_f, dv_f, dt_f, d_wg_f, d_bg_f = _gla_bwd(proj, wt["wg_pad_f"], bg_f, sv["st_f"], d_osum, False,
                                                     f"gla_bwd_f_{tag}")
    dq_b, dk_b, dv_b, dt_b, d_wg_b, d_bg_b = _gla_bwd(proj, wt["wg_pad_b"], bg_b, sv["st_b"], d_osum, True,
                                                     f"gla_bwd_b_{tag}")
    d_proj = _assemble_dproj(d_z, d_cb, d_cc, d_cx, d_mkv, dv_f, dv_b, dq_f, dq_b, dk_f, dk_b, d_mq, dt_m, dt_f, dt_b,
                             f"assemble_dproj_{tag}")
    grads = dict(w_out=g_w_out, w_uq=g_w_uq, w_ukv=g_w_ukv,
                 wg_pad_f=d_wg_f, bg_f=d_bg_f, wg_pad_b=d_wg_b, bg_b=d_bg_b, gla_norm_g=d_gng,
                 q_norm_g=d_qg, kv_norm_g=d_kvg, mla_out_g=d_mog, conv_w=d_conv_w, conv_out_g=d_cog)

    def in_dw(after):
        (g_w_in,) = _matmul(sv["h"], d_proj, dims="tn", tm=2048, tn=256, tk=2048, out_dtypes=(MXU,),
                            name=f"in_proj_dw_{tag}", after=after)
        grads["w_in"] = g_w_in
        return dict(w_in=g_w_in, w_out=g_w_out, w_uq=g_w_uq, w_ukv=g_w_ukv)

    def in_dx(after):
        (d_h,) = _matmul(d_proj, wt["w_in"], dims="nt", tm=1024, tn=512, tk=PW, out_dtypes=(F32,),
                         name=f"in_proj_dx_{tag}", after=after)
        d_x, d_shift, d_scale, d_ng = _norm_mod_bwd(d_h, sv["x"], d_out, wt["norm_g"], scale, f"norm_mod_bwd_{tag}")
        grads["norm_g"] = d_ng
        return d_x, (d_shift, d_scale, d_gate)

    if dx_first is None:
        big = in_dw(())
        d_x, d_mod = in_dx((big["w_in"],) if ship is None else ship(big))
    else:
        d_x, d_mod = in_dx(())
        big = in_dw(dx_first(d_x, d_mod, grads))
        ship(big)
    return d_x, d_mod, grads


IN_SEGS = ((3808, 5856), (2272, 3808), (1952, 2208), (768, 1536), (0, 768), (1568, 1952), (2208, 2272), (1536, 1568))
UQ_SEGS = (tuple((h * (MNOPE + MROPE), h * (MNOPE + MROPE) + MNOPE) for h in range(MH))
           + tuple((h * (MNOPE + MROPE) + MNOPE, (h + 1) * (MNOPE + MROPE)) for h in range(MH)))


def _perm_gathered(g, segs, width):
    per = g.shape[-1]
    parts, total = [], 0
    for a, b in segs:
        c = a
        while c < b:
            j = c // per
            hi = min(b, (j + 1) * per)
            parts.append(g[j, :, c - j * per:hi - j * per])
            c = hi
        total += b - a
    if width > total:
        parts.append(jnp.zeros((g.shape[1], width - total), g.dtype))
    return jnp.concatenate(parts, axis=1)


def _scatter_perm(gp, segs, per):
    offs, o = [], 0
    for a, b in segs:
        offs.append((a, b, o))
        o += b - a
    blocks = []
    for j in range(N_DEV):
        lo, hi = j * per, (j + 1) * per
        pieces = []
        for a, b, o in sorted(offs):
            s0, s1 = max(a, lo), min(b, hi)
            if s0 < s1:
                pieces.append(gp[:, o + s0 - a:o + s1 - a])
        blocks.append(jnp.concatenate(pieces, axis=1))
    return jnp.stack(blocks)


def _prep_layer_weights(w_in, w_out, w_uq, w_ukv, small):
    def vec(v):
        return v.reshape(1, -1).astype(F32)

    zeros = functools.partial(jnp.zeros, dtype=F32)
    wg_f, wg_b = small["gla_wg_f"].astype(F32), small["gla_wg_b"].astype(F32)
    wg_pad_f = jnp.concatenate([zeros((MROPE, GQK)), wg_f, zeros((LANES - MROPE - GRANK, GQK))], axis=0)
    wg_pad_b = jnp.concatenate([zeros((MROPE + GRANK, GQK)), wg_b, zeros((LANES - MROPE - 2 * GRANK, GQK))], axis=0)
    wt = dict(norm_g=vec(small["norm_g"]), wg_pad_f=wg_pad_f, wg_pad_b=wg_pad_b,
              bg_f=vec(small["gla_bg_f"]), bg_b=vec(small["gla_bg_b"]), gla_norm_g=vec(small["gla_norm_g"]),
              q_norm_g=vec(small["mla_q_norm_g"]), kv_norm_g=vec(small["mla_kv_norm_g"]),
              mla_out_g=vec(small["mla_out_g"]), conv_w=small["conv_w"].astype(F32),
              conv_out_g=vec(small["conv_out_g"]))
    for name, w in (("w_in", w_in), ("w_out", w_out), ("w_uq", w_uq), ("w_ukv", w_ukv)):
        if w is not None:
            wt[name] = w.astype(MXU)
    return wt


def _natural_small(gr):
    return dict(norm_g=gr["norm_g"][0],
                gla_wg_f=gr["wg_pad_f"][MROPE:MROPE + GRANK], gla_bg_f=gr["bg_f"][0],
                gla_wg_b=gr["wg_pad_b"][MROPE + GRANK:MROPE + 2 * GRANK], gla_bg_b=gr["bg_b"][0],
                gla_norm_g=gr["gla_norm_g"][0], mla_q_norm_g=gr["q_norm_g"][0], mla_kv_norm_g=gr["kv_norm_g"][0],
                mla_out_g=gr["mla_out_g"][0], conv_w=gr["conv_w"], conv_out_g=gr["conv_out_g"][0])


def _exchange(arrs, name, scatter, space):
    n = len(arrs)

    def body(*refs):
        ins, outs = refs[:n], refs[n:2 * n]
        send_sems, recv_sems, loc_sems = refs[2 * n:]
        ax, ay, ac = lax.axis_index("x"), lax.axis_index("y"), lax.axis_index("c")
        me = 4 * ax + 2 * ay + ac

        def src(a, to):
            return ins[a].at[to] if scatter else ins[a]

        def remote(a, r, dst_slot):
            px = 1 - ax if r & 4 else ax
            py = 1 - ay if r & 2 else ay
            pc = 1 - ac if r & 1 else ac
            return pltpu.make_async_remote_copy(
                src_ref=src(a, 4 * px + 2 * py + pc), dst_ref=outs[a].at[dst_slot(4 * px + 2 * py + pc)],
                send_sem=send_sems.at[a, r - 1], recv_sem=recv_sems.at[a, r - 1],
                device_id=(px, py, pc), device_id_type=MESH)

        locs = [pltpu.make_async_copy(src(a, me), outs[a].at[me], loc_sems.at[a]) for a in range(n)]
        for cp in locs:
            cp.start()
        sends = [remote(a, r, lambda peer: me) for r in range(1, N_DEV) for a in range(n)]
        for cp in sends:
            cp.start()
        for r in range(1, N_DEV):
            for a in range(n):
                remote(a, r, lambda peer: peer).wait_recv()
        for cp in sends:
            cp.wait_send()
        for cp in locs:
            cp.wait()

    def out_shape(a):
        return jax.ShapeDtypeStruct(a.shape if scatter else (N_DEV,) + a.shape, a.dtype)

    spec = pl.BlockSpec(memory_space=space)
    return pl.pallas_call(
        body, in_specs=[spec] * n, out_specs=[spec] * n, out_shape=[out_shape(a) for a in arrs],
        scratch_shapes=[pltpu.SemaphoreType.DMA((n, N_DEV - 1)), pltpu.SemaphoreType.DMA((n, N_DEV - 1)),
                        pltpu.SemaphoreType.DMA((n,))],
        name=name, compiler_params=pltpu.CompilerParams(vmem_limit_bytes=VMEM_LIMIT))(*arrs)


def _peer(r):
    ax, ay, ac = lax.axis_index("x"), lax.axis_index("y"), lax.axis_index("c")
    px = 1 - ax if r & 4 else ax
    py = 1 - ay if r & 2 else ay
    pc = 1 - ac if r & 1 else ac
    return (px, py, pc), 4 * px + 2 * py + pc


def _slot(rel_div):
    rel, div = rel_div
    idx = _peer(rel)[1]
    return idx if div == 1 else idx // div


AG_SPREAD = tuple((r, None, (0, 1), (r, 1)) for r in (1, 2, 4, 6))
AG_FORWARD = tuple((1, (k, 1), (k, 1), (1 ^ k, 1)) for k in (2, 4, 6))
RS_PAIR = tuple((1, (1 ^ k, 1), (1 ^ k, 2), (k, 2)) for k in (0, 2, 4, 6))
RS_CHIPS = tuple((r, (r, 2), (0, 2), (r, 2)) for r in (2, 4, 6))


def _plan_copies(plan, n, src_refs, land_refs, send_sems, recv_sems, arriving):
    out = []
    for i, (r, src, dst, recv) in enumerate(plan):
        peer = _peer(r)[0]
        for a in range(n):
            out.append(pltpu.make_async_remote_copy(
                src_ref=src_refs[a] if src is None else src_refs[a].at[_slot(src)],
                dst_ref=land_refs[a].at[_slot(recv if arriving else dst)],
                send_sem=send_sems.at[i * n + a], recv_sem=recv_sems.at[i * n + a],
                device_id=peer, device_id_type=MESH))
    return out


def _exchange_hbm(plan, srcs, lands, name, after=()):
    n = len(lands)
    fresh = isinstance(lands[0], jax.ShapeDtypeStruct)
    ins = ([] if srcs is None else list(srcs)) + ([] if fresh else list(lands))
    ns = 0 if srcs is None else n
    n_data = len(ins)
    ins = ins + list(after)

    def body(*refs):
        outs = refs[len(ins):len(ins) + n]
        send_sems, recv_sems = refs[-2:]
        src_refs = refs[:n] if srcs is not None else refs[ns:ns + n]
        sends = _plan_copies(plan, n, src_refs, outs, send_sems, recv_sems, False)
        for cp in sends:
            cp.start()
        for cp in _plan_copies(plan, n, src_refs, outs, send_sems, recv_sems, True):
            cp.wait_recv()
        for cp in sends:
            cp.wait_send()

    hbm = pl.BlockSpec(memory_space=pltpu.HBM)
    k = len(plan) * n
    return pl.pallas_call(
        body, name=name, in_specs=[hbm] * n_data + [pl.BlockSpec(memory_space=pl.ANY)] * len(after), out_specs=[hbm] * n,
        out_shape=[jax.ShapeDtypeStruct(a.shape, a.dtype) for a in lands],
        scratch_shapes=[pltpu.SemaphoreType.DMA((k,)), pltpu.SemaphoreType.DMA((k,))],
        input_output_aliases={} if fresh else {ns + i: i for i in range(n)},
        compiler_params=pltpu.CompilerParams(vmem_limit_bytes=VMEM_LIMIT))(*ins)


def _plan_start(plan, srcs, land_shapes, after, name):
    n = len(srcs)

    def body(*refs):
        src_refs, land_refs = refs[:n], refs[n:2 * n]
        send_sems, recv_sems = refs[2 * n + 1], refs[2 * n + 2]
        for cp in _plan_copies(plan, n, src_refs, land_refs, send_sems, recv_sems, False):
            cp.start()
        refs[-1][...] = jnp.zeros_like(refs[-1])

    hbm = pl.BlockSpec(memory_space=pltpu.HBM)
    sem = pl.BlockSpec(memory_space=pltpu.SEMAPHORE)
    k = len(plan) * n
    srcs = [pltpu.with_memory_space_constraint(a, pltpu.HBM) for a in srcs]
    lands = [pltpu.with_memory_space_constraint(lax.empty(shp, a.dtype), pltpu.HBM) for shp, a in zip(land_shapes, srcs)]
    res = pl.pallas_call(
        body, name=name,
        in_specs=[hbm] * (2 * n) + [pl.BlockSpec(memory_space=pl.ANY)],
        out_specs=[sem, sem] + [hbm] * (2 * n) + [pl.BlockSpec(memory_space=pltpu.VMEM)],
        out_shape=[pltpu.SemaphoreType.DMA((k,)), pltpu.SemaphoreType.DMA((k,))]
        + [pltpu.HBM(a.shape, a.dtype) for a in srcs] + [pltpu.HBM(shp, a.dtype) for shp, a in zip(land_shapes, srcs)]
        + [jax.ShapeDtypeStruct((8, LANES), F32)],
        input_output_aliases={i: 2 + i for i in range(2 * n)},
        compiler_params=pltpu.CompilerParams(has_side_effects=pltpu.SideEffectType.DATAFLOW_SIDE_EFFECTING),
    )(*srcs, *lands, after)
    return res[0], res[1], list(res[2:2 + n]), list(res[2 + n:2 + 2 * n]), res[-1]


def _plan_wait(plan, handle, after, name):
    send_sems, recv_sems, srcs, lands, _ = handle
    n = len(srcs)
    after = list(after)

    def body(*refs):
        src_refs, land_refs = refs[:n], refs[n:2 * n]
        ssem, rsem = refs[2 * n], refs[2 * n + 1]
        for cp in _plan_copies(plan, n, src_refs, land_refs, ssem, rsem, False):
            cp.wait_send()
        for cp in _plan_copies(plan, n, src_refs, land_refs, ssem, rsem, True):
            cp.wait_recv()

    hbm = pl.BlockSpec(memory_space=pltpu.HBM)
    sem = pl.BlockSpec(memory_space=pltpu.SEMAPHORE)
    res = pl.pallas_call(
        body, name=name,
        in_specs=[hbm] * (2 * n) + [sem, sem] + [pl.BlockSpec(memory_space=pl.ANY)] * len(after),
        out_specs=[hbm] * (2 * n),
        out_shape=[pltpu.HBM(a.shape, a.dtype) for a in srcs] + [pltpu.HBM(a.shape, a.dtype) for a in lands],
        input_output_aliases={i: i for i in range(2 * n)},
        compiler_params=pltpu.CompilerParams(has_side_effects=pltpu.SideEffectType.DATAFLOW_SIDE_EFFECTING),
    )(*srcs, *lands, send_sems, recv_sems, *after)
    return list(res[:n]), list(res[n:])


def _pair_sum(send, got, core, name):
    _, r, c = send.shape
    tr = 1024 if r % 1024 == 0 else r

    def body(core_ref, s_ref, g_ref, o_ref):
        o_ref[0] = (s_ref[0].astype(F32) + g_ref[0].astype(F32)).astype(o_ref.dtype)

    return pl.pallas_call(
        body, name=name,
        grid_spec=pltpu.PrefetchScalarGridSpec(
            num_scalar_prefetch=1, grid=(N_DEV // 2, r // tr),
            in_specs=[pl.BlockSpec((1, tr, c), lambda kc, i, core_ref: (2 * kc + core_ref[0], i, 0)),
                      pl.BlockSpec((1, tr, c), lambda kc, i, core_ref: (kc, i, 0))],
            out_specs=pl.BlockSpec((1, tr, c), lambda kc, i, core_ref: (kc, i, 0))),
        out_shape=jax.ShapeDtypeStruct((N_DEV // 2, r, c), send.dtype),
        compiler_params=_cp(("parallel", "parallel")))(core, send, got)


def _ada_mod(c_all, ada_w, ada_b_cols, name):
    nl, d, wc = ada_w.shape

    def body(c_ref, w_ref, b_ref, ca_ref, mod_ref):
        cv = c_ref[...]
        ca = cv * _sigmoid(cv)
        ca_ref[...] = ca
        mod_ref[0] = _dotf(ca, w_ref[0]) + b_ref[0]

    return pl.pallas_call(
        body, grid=(nl,),
        in_specs=[pl.BlockSpec((N_DEV, d), lambda l: (0, 0)), pl.BlockSpec((1, d, wc), lambda l: (l, 0, 0)),
                  pl.BlockSpec((1, 1, wc), lambda l: (l, 0, 0))],
        out_specs=[pl.BlockSpec((N_DEV, d), lambda l: (0, 0)), pl.BlockSpec((1, N_DEV, wc), lambda l: (l, 0, 0))],
        out_shape=[jax.ShapeDtypeStruct((N_DEV, d), F32), jax.ShapeDtypeStruct((nl, N_DEV, wc), F32)],
        name=name, compiler_params=_cp(("arbitrary",)))(c_all, ada_w, ada_b_cols)


def _adam(w, g, m, v):
    m2 = ADAM_B1 * m + (1.0 - ADAM_B1) * g
    v2 = ADAM_B2 * v + (1.0 - ADAM_B2) * (g * g)
    m_hat = m2 / (1.0 - ADAM_B1 ** ADAM_STEP)
    v_hat = v2 / (1.0 - ADAM_B2 ** ADAM_STEP)
    delta = -ADAM_LR * (m_hat / (jnp.sqrt(v_hat) + ADAM_EPS) + ADAM_WD * w)
    return delta, m2, v2


def _ada_grad_adam(c_act, d_mod, w, m, v, name):
    nl, d, wc = w.shape
    tk = min(512, d)

    def body(c_ref, dm_ref, w_ref, m_ref, v_ref, g_ref, dl_ref, m2_ref, v2_ref):
        g = _dotf_tn(c_ref[...], dm_ref[0])
        delta, m2, v2 = _adam(w_ref[0], g, m_ref[0], v_ref[0])
        g_ref[0], dl_ref[0], m2_ref[0], v2_ref[0] = g, delta, m2, v2

    blk = pl.BlockSpec((1, tk, wc), lambda l, i: (l, i, 0))
    shp = jax.ShapeDtypeStruct(w.shape, F32)
    return pl.pallas_call(
        body, grid=(nl, d // tk),
        in_specs=[pl.BlockSpec((N_DEV, tk), lambda l, i: (0, i)), pl.BlockSpec((1, N_DEV, wc), lambda l, i: (l, 0, 0)),
                  blk, blk, blk],
        out_specs=[blk] * 4, out_shape=[shp] * 4, name=name,
        compiler_params=_cp(("parallel", "parallel")))(c_act, d_mod, w, m, v)


def _adam_big(recv, w, m, v, layer, prev, name, after=()):
    nl, r, c = w.shape
    tr = 256 if r % 256 == 0 else r
    nparts = recv.shape[0]

    def body(rc_ref, w_ref, m_ref, v_ref, *rest):
        g_ref, dl_ref, m2_ref, v2_ref = rest[-4:]
        g = rc_ref[0].astype(F32)
        for d in range(1, nparts):
            g = g + rc_ref[d].astype(F32)
        delta, m2, v2 = _adam(w_ref[0], g, m_ref[0], v_ref[0])
        g_ref[0], dl_ref[0], m2_ref[0], v2_ref[0] = g, delta, m2, v2

    blk = pl.BlockSpec((1, tr, c), lambda i: (layer, i, 0))
    shp = jax.ShapeDtypeStruct(w.shape, F32)
    prev = () if prev is None else tuple(prev)
    return pl.pallas_call(
        body, grid=(r // tr,),
        in_specs=[pl.BlockSpec((nparts, tr, c), lambda i: (0, i, 0)), blk, blk, blk]
        + [pl.BlockSpec(memory_space=pl.ANY)] * (len(prev) + len(after)),
        out_specs=[blk] * 4, out_shape=[shp] * 4, name=name,
        input_output_aliases={4 + j: j for j in range(len(prev))},
        compiler_params=_cp(("parallel",)))(recv, w, m, v, *prev, *after)


def _sum_devices(gathered, name):
    _, r, c = gathered.shape

    def body(g_ref, o_ref):
        acc = g_ref[0]
        for d in range(1, N_DEV):
            acc = acc + g_ref[d]
        o_ref[...] = acc

    spec = pl.BlockSpec(memory_space=pltpu.VMEM)
    return pl.pallas_call(body, in_specs=[spec], out_specs=spec, out_shape=jax.ShapeDtypeStruct((r, c), F32),
                          name=name, compiler_params=pltpu.CompilerParams(vmem_limit_bytes=VMEM_LIMIT))(gathered)


def _adam_small(ws, gs, ms, vs, name):
    n = len(ws)

    def body(*refs):
        for i in range(n):
            w_ref, g_ref, m_ref, v_ref = (refs[k * n + i] for k in range(4))
            dl_ref, m2_ref, v2_ref = (refs[(4 + k) * n + i] for k in range(3))
            dl_ref[...], m2_ref[...], v2_ref[...] = _adam(w_ref[...], g_ref[...], m_ref[...], v_ref[...])

    spec = pl.BlockSpec(memory_space=pltpu.VMEM)
    shapes = [jax.ShapeDtypeStruct(w.shape, F32) for w in ws]
    res = pl.pallas_call(body, in_specs=[spec] * (4 * n), out_specs=[spec] * (3 * n), out_shape=shapes * 3, name=name,
                         compiler_params=pltpu.CompilerParams(vmem_limit_bytes=VMEM_LIMIT))(*ws, *gs, *ms, *vs)
    return res[:n], res[n:2 * n], res[2 * n:]


def _pack(parts):
    flat = jnp.concatenate([p.reshape(-1).astype(F32) for p in parts])
    assert flat.shape[0] % LANES == 0, flat.shape
    return flat.reshape(-1, LANES)


def _unpack(packed, shapes):
    flat = packed.reshape(-1)
    out, off = [], 0
    for shp in shapes:
        size = 1
        for dim in shp:
            size *= dim
        out.append(flat[off:off + size].reshape(shp))
        off += size
    return out


def _gather_cols(g, per):
    g = jnp.moveaxis(g, 0, -2)
    return g.reshape(g.shape[:-2] + (N_DEV * per,))


def _scatter_cols(g, per):
    return jnp.moveaxis(g.reshape(g.shape[:-1] + (N_DEV, per)), -2, 0)


def _my_cols(full, me, per):
    return lax.dynamic_slice_in_dim(full, me * per, per, axis=full.ndim - 1)


def kernel(x, c, positions, ada_w, ada_b, norm_g, w_in, gla_wg_f, gla_bg_f, gla_wg_b, gla_bg_b, gla_norm_g, mla_q_norm_g, mla_kv_norm_g, mla_w_uq, mla_w_ukv, mla_out_g, conv_w, conv_out_g, w_out, final_g, loss_target, m_ada_w, m_ada_b, m_norm_g, m_w_in, m_gla_wg_f, m_gla_bg_f, m_gla_wg_b, m_gla_bg_b, m_gla_norm_g, m_mla_q_norm_g, m_mla_kv_norm_g, m_mla_w_uq, m_mla_w_ukv, m_mla_out_g, m_conv_w, m_conv_out_g, m_w_out, m_final_g, v_ada_w, v_ada_b, v_norm_g, v_w_in, v_gla_wg_f, v_gla_bg_f, v_gla_wg_b, v_gla_bg_b, v_gla_norm_g, v_mla_q_norm_g, v_mla_kv_norm_g, v_mla_w_uq, v_mla_w_ukv, v_mla_out_g, v_conv_w, v_conv_out_g, v_w_out, v_final_g):
    me = 4 * lax.axis_index("x") + 2 * lax.axis_index("y") + lax.axis_index("c")
    nl = ada_w.shape[0]
    s, d = x.shape[1], x.shape[2]
    ada_cols = ada_w.shape[2]
    wgc, cwc = gla_wg_f.shape[2], conv_w.shape[2]

    (g0,) = _exchange([_pack([c, gla_wg_f, gla_wg_b, conv_w])], "gather_small_in", False, pltpu.VMEM)
    g0 = g0.reshape(N_DEV, -1)
    o1, o2, o3 = d, d + gla_wg_f.size, d + 2 * gla_wg_f.size
    c_all = g0[:, :o1]
    wgf_full = _gather_cols(g0[:, o1:o2].reshape((N_DEV,) + gla_wg_f.shape), wgc)
    wgb_full = _gather_cols(g0[:, o2:o3].reshape((N_DEV,) + gla_wg_b.shape), wgc)
    convw_full = _gather_cols(g0[:, o3:].reshape((N_DEV,) + conv_w.shape), cwc)

    ada_b_cols = _my_cols(ada_b, me, ada_cols).reshape(nl, 1, ada_cols)
    c_act, mod_cols = _ada_mod(c_all, ada_w, ada_b_cols, "ada_mod")
    (g1,) = _exchange([_pack([mod_cols])], "gather_mod", False, pltpu.VMEM)
    mod_all = g1.reshape(N_DEV, nl, N_DEV, ada_cols)
    mod_mine = _gather_cols(lax.dynamic_index_in_dim(mod_all, me, axis=2, keepdims=False), ada_cols)

    inv_freq = ROPE_THETA ** (-jnp.arange(0, MROPE, 2, dtype=F32) / MROPE)
    ang = positions[0].astype(F32)[:, None] * inv_freq
    cos, sin = jnp.tile(jnp.cos(ang), (1, LANES * 2 // MROPE)), jnp.tile(jnp.sin(ang), (1, LANES * 2 // MROPE))

    big = [w_in, w_out, mla_w_uq, mla_w_ukv]
    big_names = ["w_in", "w_out", "mla_w_uq", "mla_w_ukv"]

    def local_blocks(l):
        return [w[l].astype(MXU) for w in big]

    def put_own(lands, own):
        return [lax.dynamic_update_index_in_dim(ld, o, me, 0) for ld, o in zip(lands, own)]

    def layer_weights(l, gw_in=None, gw_out=None, gw_uq=None, gw_ukv=None):
        small = dict(norm_g=norm_g[l], gla_wg_f=wgf_full[l], gla_bg_f=gla_bg_f[l], gla_wg_b=wgb_full[l],
                     gla_bg_b=gla_bg_b[l], gla_norm_g=gla_norm_g[l], mla_q_norm_g=mla_q_norm_g[l],
                     mla_kv_norm_g=mla_kv_norm_g[l], mla_out_g=mla_out_g[l], conv_w=convw_full[l],
                     conv_out_g=conv_out_g[l])
        return _prep_layer_weights(
            None if gw_in is None else _perm_gathered(gw_in, IN_SEGS, PW),
            None if gw_out is None else gw_out.reshape((-1,) + gw_out.shape[2:]),
            None if gw_uq is None else _perm_gathered(gw_uq, UQ_SEGS, MQW),
            None if gw_ukv is None else _gather_cols(gw_ukv, mla_w_ukv.shape[2]), small)

    def land_shapes(blocks, slots):
        return [jax.ShapeDtypeStruct((slots,) + b.shape, b.dtype) for b in blocks]

    def slots_of(blocks):
        return [(N_DEV,) + b.shape for b in blocks]

    def forwarded(lands, blocks, tag):
        return put_own(_exchange_hbm(AG_FORWARD, None, lands, f"gather_{tag}_forward"), blocks)

    first = local_blocks(0)
    w_in_start = _plan_start(AG_SPREAD, first[:1], slots_of(first[:1]), mod_mine, "gather_w_in_l0_start")
    adam_w_in = [a + w_in_start[-1][0, 0] for a in (w_in, m_w_in, v_w_in)]
    (gw_in,) = forwarded(*reversed(_plan_wait(AG_SPREAD, w_in_start, adam_w_in, "gather_w_in_l0_wait")), "w_in_l0")
    rest = _plan_start(AG_SPREAD, first[1:], slots_of(first[1:]), gw_in, "gather_rest_l0_start")
    h = x[0]
    saved, layers, mods = [], [], []
    pending = {}
    for l in range(nl):
        shift, scale, gate = (mod_mine[l, i * d:(i + 1) * d].reshape(1, d) for i in range(3))
        nxt = local_blocks(l + 1) if l + 1 < nl else None

        def start_next(after, wt_late, l=l, nxt=nxt):
            if nxt is not None:
                pending[l + 1] = _plan_start(AG_SPREAD, nxt, slots_of(nxt), after, f"gather_weights_l{l + 1}_start")
                wt_late["q_norm_g"] = layers[l]["q_norm_g"] + pending[l + 1][-1][0, 0]
            return wt_late

        if l == 0:
            in_after = (rest[-1],)
            layers.append(layer_weights(0, gw_in))

            def late(proj):
                got = forwarded(*reversed(_plan_wait(AG_SPREAD, rest, [proj], "gather_rest_l0_wait")), "rest_l0")
                full = layer_weights(0, None, *got)
                return start_next(got[0], {k: full[k] for k in ("w_out", "w_uq", "w_ukv")})
        else:
            got = forwarded(*reversed(_plan_wait(AG_SPREAD, pending.pop(l), [h], f"gather_weights_l{l}_wait")), f"weights_l{l}")
            layers.append(layer_weights(l, *got))
            in_after = ()

            def late(proj):
                return start_next(proj, {})
        mods.append((shift, scale, gate))
        h, sv = _layer_fwd(h, mods[l], layers[l], cos, sin, f"l{l}", late, in_after)
        saved.append(sv)
        blocks = nxt
    loss_part, d_h, d_final_g = _final_loss(h, final_g.reshape(1, d), loss_target[0], "final_loss")
    loss = lax.psum(loss_part[0, 0], ("x", "y", "c"))
    shift, scale, gate = mods[-1]
    mods[-1] = (shift, scale, gate + 0.0 * loss)

    send_of = dict(w_in=lambda g: _scatter_perm(g, IN_SEGS, w_in.shape[2]),
                   w_out=lambda g: g.reshape((N_DEV,) + w_out.shape[1:]),
                   w_uq=lambda g: _scatter_perm(g, UQ_SEGS, mla_w_uq.shape[2]),
                   w_ukv=lambda g: _scatter_cols(g, mla_w_ukv.shape[2]))

    def grad_sends(gr):
        return [send_of[k](g).astype(MXU) for k, g in gr.items()]

    my_chip = me // 2
    my_core = (me % 2).astype(jnp.int32).reshape(1)

    def chip_sums(gr, tag):
        sends = grad_sends(gr)
        got = _exchange_hbm(RS_PAIR, sends, land_shapes([sd[0] for sd in sends], N_DEV // 2), f"scatter_grads_{tag}_pair")
        return [_pair_sum(sd, gt, my_core, f"pair_sum_{k}_{tag}") for sd, gt, k in zip(sends, got, gr)]

    def with_own_chip(lands, sums):
        return [lax.dynamic_update_index_in_dim(ld, lax.dynamic_index_in_dim(sm, my_chip, axis=0, keepdims=False),
                                                my_chip, 0) for ld, sm in zip(lands, sums)]

    small_names = ["norm_g", "gla_wg_f", "gla_bg_f", "gla_wg_b", "gla_bg_b", "gla_norm_g", "mla_q_norm_g",
                   "mla_kv_norm_g", "mla_out_g", "conv_w", "conv_out_g"]
    d_mods, grads, recv = [None] * nl, [None] * nl, [None] * nl
    flight = {}
    small = {}

    def gather_small(d_x, d_mod0, gr0):
        d_mods[0], grads[0] = d_mod0, _natural_small(gr0)
        d_mod_mine = jnp.stack([jnp.concatenate(d_mods[l], axis=-1)[0] for l in range(nl)])
        parts = [d_mod_mine] + [jnp.stack([grads[l][n] for l in range(nl)]) for n in small_names] + [d_final_g]
        (g2,) = _exchange([_pack(parts)], "gather_small_grads", False, pltpu.VMEM)
        small["d_mod_all"] = g2.reshape(N_DEV, -1)[:, :d_mod_mine.size].reshape(N_DEV, nl, 3 * d)
        small["summed"] = dict(zip(["ada_b"] + small_names + ["final_g"],
                                   _unpack(_sum_devices(g2, "sum_small_grads"), [p.shape for p in parts])))
        return (g2,)

    pairs = {}
    def end_flight(key, after, name):
        sm, lands = _plan_wait(RS_CHIPS, flight.pop(key)[0], after, name)
        return with_own_chip(lands, sm)

    for l in reversed(range(nl)):
        def ship(big_grads, l=l):
            if l > 0:
                sends = grad_sends(big_grads)
                pairs[l] = (_plan_start(RS_PAIR, sends, [(N_DEV // 2,) + sd.shape[1:] for sd in sends],
                                        big_grads["w_in"], f"scatter_grads_l{l}_pair_start"), sends)
                return (pairs[l][0][-1],)
            sm = chip_sums(dict(w_in=big_grads["w_in"]), f"l{l}")
            flight[l] = (_plan_start(RS_CHIPS, sm, [a.shape for a in sm], big_grads["w_in"], f"scatter_grads_l{l}_start"),
                         sm)
            return (flight[l][0][-1],)

        def ship_rest(rest_grads, l=l):
            if l + 1 in flight:
                recv[l + 1] = end_flight(l + 1, list(rest_grads.values()), f"scatter_grads_l{l + 1}_wait")
            sm = chip_sums(rest_grads, f"l{l}_rest")
            flight["rest"] = (_plan_start(RS_CHIPS, sm, [a.shape for a in sm], rest_grads["w_out"],
                                          f"scatter_grads_l{l}_rest_start"), sm)
            return flight["rest"][0][-1][0, 0]

        shift, scale, gate = mods[l]
        if l + 1 in flight:
            gate = gate + flight[l + 1][0][-1][0, 0]
        if l > 0:
            d_h, d_mods[l], gr = _layer_bwd(d_h, saved[l], (shift, scale, gate), layers[l], cos, sin, f"l{l}", ship)
            grads[l] = _natural_small(gr)
            sends, got = _plan_wait(RS_PAIR, pairs.pop(l)[0], [d_h], f"scatter_grads_l{l}_pair_wait")
            sm = [_pair_sum(sd, gt, my_core, f"pair_sum_{n}_l{l}") for sd, gt, n in zip(sends, got, big_names)]
            flight[l] = (_plan_start(RS_CHIPS, sm, [a.shape for a in sm], d_h, f"scatter_grads_l{l}_start"), sm)
        else:
            d_h, _, _ = _layer_bwd(d_h, saved[l], (shift, scale, gate), layers[l], cos, sin, f"l{l}", ship, gather_small,
                                   ship_rest)
    pending = flight[0][0]
    grad_x = d_h[None]
    summed = small["summed"]
    summed["gla_wg_f"] = _my_cols(summed["gla_wg_f"], me, wgc)
    summed["gla_wg_b"] = _my_cols(summed["gla_wg_b"], me, wgc)
    summed["conv_w"] = _my_cols(summed["conv_w"], me, cwc)

    d_mod_cols = jnp.moveaxis(_my_cols(small["d_mod_all"], me, ada_cols), 0, 1) + pending[-1][0, 0]
    out = {}
    out["ada_w"] = _ada_grad_adam(c_act, d_mod_cols, ada_w, m_ada_w, v_ada_w, "ada_grad_adam")

    given = dict(ada_b=(ada_b, m_ada_b, v_ada_b), norm_g=(norm_g, m_norm_g, v_norm_g),
                 gla_wg_f=(gla_wg_f, m_gla_wg_f, v_gla_wg_f), gla_bg_f=(gla_bg_f, m_gla_bg_f, v_gla_bg_f),
                 gla_wg_b=(gla_wg_b, m_gla_wg_b, v_gla_wg_b), gla_bg_b=(gla_bg_b, m_gla_bg_b, v_gla_bg_b),
                 gla_norm_g=(gla_norm_g, m_gla_norm_g, v_gla_norm_g),
                 mla_q_norm_g=(mla_q_norm_g, m_mla_q_norm_g, v_mla_q_norm_g),
                 mla_kv_norm_g=(mla_kv_norm_g, m_mla_kv_norm_g, v_mla_kv_norm_g),
                 mla_out_g=(mla_out_g, m_mla_out_g, v_mla_out_g), conv_w=(conv_w, m_conv_w, v_conv_w),
                 conv_out_g=(conv_out_g, m_conv_out_g, v_conv_out_g), final_g=(final_g, m_final_g, v_final_g))
    names = list(given)

    def two_d(a):
        return a.reshape(1, -1) if a.ndim == 1 else a

    g_nat = [summed[n].reshape(given[n][0].shape) for n in names]
    res = _adam_small([two_d(given[n][0]) for n in names], [two_d(g) for g in g_nat],
                      [two_d(given[n][1]) for n in names], [two_d(given[n][2]) for n in names], "adam_small")
    for i, n in enumerate(names):
        out[n] = (g_nat[i],) + tuple(r[i].reshape(given[n][0].shape) for r in res)

    state = dict(w_in=adam_w_in, w_out=(w_out, m_w_out, v_w_out), mla_w_uq=(mla_w_uq, m_mla_w_uq, v_mla_w_uq),
                 mla_w_ukv=(mla_w_ukv, m_mla_w_ukv, v_mla_w_ukv))
    done = [out["ada_w"][0], res[0][0]]
    for l in reversed(range(nl)):
        if l == 0:
            rest = end_flight("rest", done, "scatter_grads_l0_rest_wait")
            recv[0] = end_flight(0, done + rest[:1], "scatter_grads_l0_wait") + rest
        for i, n in enumerate(big_names):
            out[n] = _adam_big(recv[l][i], *state[n], l, out.get(n), f"adam_{n}_l{l}",
                               (pending[-1],))
        done = done + [out[n][0] for n in big_names]

    order = ["ada_w", "ada_b", "norm_g", "w_in", "gla_wg_f", "gla_bg_f", "gla_wg_b", "gla_bg_b", "gla_norm_g",
             "mla_q_norm_g", "mla_kv_norm_g", "mla_w_uq", "mla_w_ukv", "mla_out_g", "conv_w", "conv_out_g", "w_out",
             "final_g"]
    return (loss, grad_x, *[out[n][0] for n in order], *[out[n][1] for n in order], *[out[n][2] for n in order],
            *[out[n][3] for n in order])
```

```python
import functools

import jax
import jax.numpy as jnp
from jax import lax
from jax.experimental import pallas as pl
from jax.experimental.pallas import tpu as pltpu

F32 = jnp.float32
MXU = jnp.bfloat16
HI = lax.Precision.HIGHEST
N_DEV = 8
MESH = pl.DeviceIdType.MESH

D_MIX = 2048
GH, GDK, GDV = 6, 64, 128
GW = GH * GDV
GQK = GH * GDK
GRANK = 16
GTEMP = 16.0
CHUNK = 64
MH, MQL, MKVL, MNOPE, MROPE, MDV = 6, 384, 256, 128, 64, 128
MW = MH * MDV
MQW = MH * (MNOPE + MROPE)
MKVW = MH * (MNOPE + MDV)
CONV_CH = 512
ROPE_THETA = 10000.0
EPS = 1e-6
IN_DIM = 5856
OZ, OCB, OCC, OCX, OMKV, OGV, OGQ, OGK, OMQ, OT = 0, 2048, 2560, 3072, 3584, 3840, 4608, 4992, 5376, 5760
PW = 5888
LANES = 128
V7X_VMEM_BYTES = 64 * 1024 * 1024
VMEM_LIMIT = V7X_VMEM_BYTES * 7 // 8

ADAM_LR, ADAM_B1, ADAM_B2, ADAM_EPS, ADAM_WD, ADAM_STEP = 0.001, 0.9, 0.999, 1e-08, 0.01, 10


def _cp(sem=None):
    return pltpu.CompilerParams(dimension_semantics=sem, vmem_limit_bytes=VMEM_LIMIT)


def _dot(a, b):
    return jnp.dot(a.astype(MXU), b.astype(MXU), preferred_element_type=F32)


def _dot_nt(a, b):
    return lax.dot_general(a.astype(MXU), b.astype(MXU), (((1,), (1,)), ((), ())), preferred_element_type=F32)


def _dot_tn(a, b):
    return lax.dot_general(a.astype(MXU), b.astype(MXU), (((0,), (0,)), ((), ())), preferred_element_type=F32)


def _dotf(a, b):
    return jnp.dot(a, b, precision=HI, preferred_element_type=F32)


def _dotf_tn(a, b):
    return lax.dot_general(a, b, (((0,), (0,)), ((), ())), precision=HI, preferred_element_type=F32)


def _split3(x):
    hi = x.astype(jnp.bfloat16)
    r1 = x - hi.astype(F32)
    mid = r1.astype(jnp.bfloat16)
    lo = (r1 - mid.astype(F32)).astype(jnp.bfloat16)
    return hi, mid, lo


def _cum_dot(cum, x, transpose=False):
    dn = (((0,), (0,)), ((), ())) if transpose else (((1,), (0,)), ((), ()))
    cb = cum.astype(jnp.bfloat16)
    parts = [lax.dot_general(cb, p, dn, preferred_element_type=F32) for p in _split3(x)]
    return parts[0] + parts[1] + parts[2]


def _rows(s):
    return min(256, s)


def _rows_light(s):
    return min(512, s)


def _rms(x, g):
    r = lax.rsqrt(jnp.mean(x * x, axis=-1, keepdims=True) + EPS)
    return x * r * g


def _rms_bwd(dy, x, g):
    r = lax.rsqrt(jnp.mean(x * x, axis=-1, keepdims=True) + EPS)
    xh = x * r
    dxh = dy * g
    dg = jnp.sum(dy * xh, axis=0, keepdims=True)
    dx = r * (dxh - xh * jnp.mean(dxh * xh, axis=-1, keepdims=True))
    return dx, dg


def _sigmoid(z):
    return jax.nn.sigmoid(z)


def _matmul(a, b, *, dims, tm, tn, tk, out_dtypes, name, epilogue=None, extras=(), extra_kinds=(), after=()):
    if dims == "nn":
        (m, k), n, mul = a.shape, b.shape[1], _dot
    elif dims == "nt":
        (m, k), n, mul = a.shape, b.shape[0], _dot_nt
    else:
        (k, m), n, mul = a.shape, b.shape[1], _dot_tn
    tm, tn, tk = min(tm, m), min(tn, n), min(tk, k)
    assert m % tm == 0 and n % tn == 0 and k % tk == 0, (m, n, k, tm, tn, tk)
    if dims == "nn":
        a_spec = pl.BlockSpec((tm, tk), lambda i, j, kk: (i, kk))
        b_spec = pl.BlockSpec((tk, tn), lambda i, j, kk: (kk, j))
    elif dims == "nt":
        a_spec = pl.BlockSpec((tm, tk), lambda i, j, kk: (i, kk))
        b_spec = pl.BlockSpec((tn, tk), lambda i, j, kk: (j, kk))
    else:
        a_spec = pl.BlockSpec((tk, tm), lambda i, j, kk: (kk, i))
        b_spec = pl.BlockSpec((tk, tn), lambda i, j, kk: (kk, j))
    nk = k // tk
    n_extra = len(extras)
    n_out = len(out_dtypes)
    n_after = len(after)
    extra_specs = []
    for kind in extra_kinds:
        if kind == "mn":
            extra_specs.append(pl.BlockSpec((tm, tn), lambda i, j, kk: (i, j)))
        else:
            extra_specs.append(pl.BlockSpec((1, tn), lambda i, j, kk: (0, j)))

    def finish(res, ex, outs):
        vals = (res,) if epilogue is None else epilogue(res, *[e[...] for e in ex])
        for o, v in zip(outs, vals):
            o[...] = v.astype(o.dtype)

    def body(*refs):
        a_ref, b_ref = refs[0], refs[1]
        ex = refs[2:2 + n_extra]
        outs = refs[2 + n_extra + n_after:2 + n_extra + n_after + n_out]
        if nk == 1:
            finish(mul(a_ref[...], b_ref[...]), ex, outs)
            return
        acc = refs[-1]
        kk = pl.program_id(2)

        @pl.when(kk == 0)
        def _():
            acc[...] = jnp.zeros_like(acc)

        acc[...] += mul(a_ref[...], b_ref[...])

        @pl.when(kk == nk - 1)
        def _():
            finish(acc[...], ex, outs)

    out_spec = pl.BlockSpec((tm, tn), lambda i, j, kk: (i, j))
    res = pl.pallas_call(
        body, grid=(m // tm, n // tn, nk),
        in_specs=[a_spec, b_spec] + extra_specs + [pl.BlockSpec(memory_space=pl.ANY)] * n_after,
        out_specs=[out_spec] * n_out,
        out_shape=[jax.ShapeDtypeStruct((m, n), dt) for dt in out_dtypes],
        scratch_shapes=[] if nk == 1 else [pltpu.VMEM((tm, tn), F32)],
        name=name, compiler_params=_cp(("parallel", "parallel", "arbitrary")),
    )(a, b, *extras, *after)
    return res


def _norm_mod(x, g, scale, shift, name):
    s, d = x.shape
    tr = _rows_light(s)

    def body(x_ref, g_ref, sc_ref, sh_ref, h_ref):
        h = _rms(x_ref[...], g_ref[...]) * (1.0 + sc_ref[...]) + sh_ref[...]
        h_ref[...] = h.astype(h_ref.dtype)

    row = pl.BlockSpec((tr, d), lambda i: (i, 0))
    vec = pl.BlockSpec((1, d), lambda i: (0, 0))
    return pl.pallas_call(body, grid=(s // tr,), in_specs=[row, vec, vec, vec], out_specs=row,
                          out_shape=jax.ShapeDtypeStruct((s, d), MXU), name=name,
                          compiler_params=_cp(("parallel",)))(x, g, scale, shift)


def _norm_mod_bwd(d_h, x, d_out, g, scale, name):
    s, d = x.shape
    tr = _rows(s)

    def body(dh_ref, x_ref, do_ref, g_ref, sc_ref, dx_ref, dsh_ref, dsc_ref, dg_ref):
        i = pl.program_id(0)

        @pl.when(i == 0)
        def _():
            dsh_ref[...] = jnp.zeros_like(dsh_ref)
            dsc_ref[...] = jnp.zeros_like(dsc_ref)
            dg_ref[...] = jnp.zeros_like(dg_ref)

        dh = dh_ref[...]
        xv = x_ref[...]
        gv = g_ref[...]
        r = lax.rsqrt(jnp.mean(xv * xv, axis=-1, keepdims=True) + EPS)
        xh = xv * r
        dsh_ref[...] += jnp.sum(dh, axis=0, keepdims=True)
        dsc_ref[...] += jnp.sum(dh * (xh * gv), axis=0, keepdims=True)
        dhn = dh * (1.0 + sc_ref[...])
        dg_ref[...] += jnp.sum(dhn * xh, axis=0, keepdims=True)
        dxh = dhn * gv
        dx_ref[...] = do_ref[...] + r * (dxh - xh * jnp.mean(dxh * xh, axis=-1, keepdims=True))

    row = pl.BlockSpec((tr, d), lambda i: (i, 0))
    vec = pl.BlockSpec((1, d), lambda i: (0, 0))
    vshape = jax.ShapeDtypeStruct((1, d), F32)
    return pl.pallas_call(body, grid=(s // tr,), in_specs=[row, row, row, vec, vec],
                          out_specs=[row, vec, vec, vec],
                          out_shape=[jax.ShapeDtypeStruct((s, d), F32), vshape, vshape, vshape],
                          name=name, compiler_params=_cp(("arbitrary",)))(d_h, x, d_out, g, scale)


def _gate_bwd(d_out, u, gate, name):
    s, d = d_out.shape
    tr = _rows_light(s)

    def body(do_ref, u_ref, gt_ref, du_ref, dgt_ref):
        @pl.when(pl.program_id(0) == 0)
        def _():
            dgt_ref[...] = jnp.zeros_like(dgt_ref)

        do = do_ref[...]
        du_ref[...] = (do * gt_ref[...]).astype(du_ref.dtype)
        dgt_ref[...] += jnp.sum(do * u_ref[...], axis=0, keepdims=True)

    row = pl.BlockSpec((tr, d), lambda i: (i, 0))
    vec = pl.BlockSpec((1, d), lambda i: (0, 0))
    return pl.pallas_call(body, grid=(s // tr,), in_specs=[row, row, vec], out_specs=[row, vec],
                          out_shape=[jax.ShapeDtypeStruct((s, d), MXU), jax.ShapeDtypeStruct((1, d), F32)],
                          name=name, compiler_params=_cp(("arbitrary",)))(d_out, u, gate)


def _final_loss(x, g, target, name):
    s, d = x.shape
    tr = _rows_light(s)

    def body(x_ref, g_ref, t_ref, loss_ref, dx_ref, dg_ref):
        @pl.when(pl.program_id(0) == 0)
        def _():
            loss_ref[...] = jnp.zeros_like(loss_ref)
            dg_ref[...] = jnp.zeros_like(dg_ref)

        xv = x_ref[...]
        gv = g_ref[...]
        diff = _rms(xv, gv) - t_ref[...]
        part = 0.5 * jnp.sum(jnp.sum(diff * diff, axis=-1, keepdims=True) / d, axis=0, keepdims=True)
        loss_ref[...] += jnp.broadcast_to(part, loss_ref.shape)
        dx, dg = _rms_bwd(diff / d, xv, gv)
        dx_ref[...] = dx
        dg_ref[...] += dg

    row = pl.BlockSpec((tr, d), lambda i: (i, 0))
    vec = pl.BlockSpec((1, d), lambda i: (0, 0))
    lvec = pl.BlockSpec((1, LANES), lambda i: (0, 0))
    return pl.pallas_call(body, grid=(s // tr,), in_specs=[row, vec, row], out_specs=[lvec, row, vec],
                          out_shape=[jax.ShapeDtypeStruct((1, LANES), F32), jax.ShapeDtypeStruct((s, d), F32),
                                     jax.ShapeDtypeStruct((1, d), F32)],
                          name=name, compiler_params=_cp(("arbitrary",)))(x, g, target)


def _shift_rows(u, s, down):
    ri = lax.broadcasted_iota(jnp.int32, u.shape, 0)
    if down:
        return jnp.where(ri == 0, 0.0, pltpu.roll(u, 1, 0))
    return jnp.where(ri == s - 1, 0.0, pltpu.roll(u, s - 1, 0))


def _conv_fwd(proj, conv_w, name):
    s = proj.shape[0]
    nt = CONV_CH // LANES

    def body(cb_ref, cc_ref, cx_ref, w_ref, pre_ref):
        u = cc_ref[...] * cx_ref[...]
        conv = _shift_rows(u, s, True) * w_ref[0:1, :] + u * w_ref[1:2, :] + _shift_rows(u, s, False) * w_ref[2:3, :]
        pre_ref[...] = cb_ref[...] * conv

    def col(off):
        return pl.BlockSpec((s, LANES), lambda j: (0, off // LANES + j))

    return pl.pallas_call(body, grid=(nt,), in_specs=[col(OCB), col(OCC), col(OCX), pl.BlockSpec((3, LANES), lambda j: (0, j))],
                          out_specs=pl.BlockSpec((s, LANES), lambda j: (0, j)),
                          out_shape=jax.ShapeDtypeStruct((s, CONV_CH), F32), name=name,
                          compiler_params=_cp(("parallel",)))(proj, proj, proj, conv_w)


def _conv_bwd(proj, conv_w, d_pre, name):
    s = proj.shape[0]
    nt = CONV_CH // LANES

    def body(cb_ref, cc_ref, cx_ref, w_ref, dp_ref, dcb_ref, dcc_ref, dcx_ref, dw_ref):
        cc, cx = cc_ref[...], cx_ref[...]
        u = cc * cx
        up, dn = _shift_rows(u, s, True), _shift_rows(u, s, False)
        w0, w1, w2 = w_ref[0:1, :], w_ref[1:2, :], w_ref[2:3, :]
        conv = up * w0 + u * w1 + dn * w2
        dp = dp_ref[...]
        dcb_ref[...] = (dp * conv).astype(dcb_ref.dtype)
        dconv = dp * cb_ref[...]
        du = _shift_rows(dconv, s, False) * w0 + dconv * w1 + _shift_rows(dconv, s, True) * w2
        dcc_ref[...] = (du * cx).astype(dcc_ref.dtype)
        dcx_ref[...] = (du * cc).astype(dcx_ref.dtype)
        dw_ref[0:1, :] = jnp.sum(dconv * up, axis=0, keepdims=True)
        dw_ref[1:2, :] = jnp.sum(dconv * u, axis=0, keepdims=True)
        dw_ref[2:3, :] = jnp.sum(dconv * dn, axis=0, keepdims=True)

    def col(off):
        return pl.BlockSpec((s, LANES), lambda j: (0, off // LANES + j))

    blk = pl.BlockSpec((s, LANES), lambda j: (0, j))
    wblk = pl.BlockSpec((3, LANES), lambda j: (0, j))
    full = jax.ShapeDtypeStruct((s, CONV_CH), MXU)
    return pl.pallas_call(body, grid=(nt,), in_specs=[col(OCB), col(OCC), col(OCX), wblk, blk],
                          out_specs=[blk, blk, blk, wblk],
                          out_shape=[full, full, full, jax.ShapeDtypeStruct((3, CONV_CH), F32)],
                          name=name, compiler_params=_cp(("parallel",)))(proj, proj, proj, conv_w, d_pre)


GLA_SUB = 8


def _gla_gates(t_ref, wg_ref, bg_ref):
    t = t_ref[...]
    a = _dot(t, wg_ref[...]) + bg_ref[...]
    la = (jnp.minimum(a, 0.0) - jnp.log(1.0 + jnp.exp(-jnp.abs(a)))) / GTEMP
    return t, a, la


def _gla_masks(reverse):
    ri = lax.broadcasted_iota(jnp.int32, (CHUNK, CHUNK), 0)
    ci = lax.broadcasted_iota(jnp.int32, (CHUNK, CHUNK), 1)
    if reverse:
        cum, mask, mask_t = ci >= ri, ci > ri, ri > ci
    else:
        cum, mask, mask_t = ci <= ri, ci <= ri, ri <= ci
    return cum.astype(F32), mask, mask_t


def _gla_specs(s, reverse):
    nsub = min(GLA_SUB, s // CHUNK)
    nsteps = s // (CHUNK * nsub)

    def row(n):
        return nsteps - 1 - n if reverse else n

    def chunk(pi):
        return nsub - 1 - pi if reverse else pi

    return nsub, nsteps, row, chunk


def _gla_fwd(proj, wg_pad, bg, reverse, name):
    s = proj.shape[0]
    nsub, nsteps, row, chunk = _gla_specs(s, reverse)
    rb = nsub * CHUNK

    def body(q_ref, k_ref, v_ref, t_ref, wg_ref, bg_ref, o_ref, st_ref, state):
        @pl.when(pl.program_id(0) == 0)
        def _():
            state[...] = jnp.zeros_like(state)

        _, _, la = _gla_gates(t_ref, wg_ref, bg_ref)
        cumf, mask, _ = _gla_masks(reverse)
        lane = lax.broadcasted_iota(jnp.int32, (CHUNK, LANES), 1)
        for pi in range(nsub):
            rows = slice(chunk(pi) * CHUNK, (chunk(pi) + 1) * CHUNK)
            la_c = la[rows]
            b_all = _cum_dot(cumf, la_c)
            bl_all = jnp.sum(la_c, axis=0, keepdims=True)
            for p in range(GH // 2):
                sl = slice(p * LANES, (p + 1) * LANES)
                b, bl = b_all[:, sl], bl_all[:, sl]
                qd = q_ref[rows, sl] * (GDK ** -0.5) * jnp.exp(b)
                ki = k_ref[rows, sl] * jnp.exp(-b)
                kte = k_ref[rows, sl] * jnp.exp(bl - b)
                decay = jnp.exp(bl)
                for half in range(2):
                    h = 2 * p + half
                    lm = (lane < GDK) if half == 0 else (lane >= GDK)
                    qd_h = jnp.where(lm, qd, 0.0)
                    kte_h = jnp.where(lm, kte, 0.0)
                    v_h = v_ref[rows, h * GDV:(h + 1) * GDV]
                    st = state[h]
                    a_mat = jnp.where(mask, _dot_nt(qd_h, ki), 0.0)
                    o_ref[rows, h * GDV:(h + 1) * GDV] = _dot(a_mat, v_h) + _dot_nt(qd_h, st)
                    st_ref[pi, h] = st
                    state[h] = st * decay + _dot_tn(v_h, kte_h)

    return pl.pallas_call(
        body, grid=(nsteps,),
        in_specs=[pl.BlockSpec((rb, GQK), lambda n: (row(n), OGQ // GQK)),
                  pl.BlockSpec((rb, GQK), lambda n: (row(n), OGK // GQK)),
                  pl.BlockSpec((rb, GW), lambda n: (row(n), OGV // GW)),
                  pl.BlockSpec((rb, LANES), lambda n: (row(n), OT // LANES)),
                  pl.BlockSpec((LANES, GQK), lambda n: (0, 0)),
                  pl.BlockSpec((1, GQK), lambda n: (0, 0))],
        out_specs=[pl.BlockSpec((rb, GW), lambda n: (row(n), 0)),
                   pl.BlockSpec((nsub, GH, GDV, LANES), lambda n: (n, 0, 0, 0))],
        out_shape=[jax.ShapeDtypeStruct((s, GW), F32), jax.ShapeDtypeStruct((s // CHUNK, GH, GDV, LANES), F32)],
        scratch_shapes=[pltpu.VMEM((GH, GDV, LANES), F32)],
        name=name, compiler_params=_cp(("arbitrary",)))(proj, proj, proj, proj, wg_pad, bg)


def _gla_bwd(proj, wg_pad, bg, states, d_o, reverse, name):
    s = proj.shape[0]
    nsub, nsteps, row, chunk = _gla_specs(s, reverse)
    rb = nsub * CHUNK

    def body(q_ref, k_ref, v_ref, t_ref, wg_ref, bg_ref, st_ref, do_ref,
             dq_ref, dk_ref, dv_ref, dt_ref, dwg_ref, dbg_ref, dstate, da_buf):
        @pl.when(pl.program_id(0) == 0)
        def _():
            dstate[...] = jnp.zeros_like(dstate)
            dwg_ref[...] = jnp.zeros_like(dwg_ref)
            dbg_ref[...] = jnp.zeros_like(dbg_ref)

        t, a, la = _gla_gates(t_ref, wg_ref, bg_ref)
        cumf, mask, mask_t = _gla_masks(reverse)
        lane = lax.broadcasted_iota(jnp.int32, (CHUNK, LANES), 1)
        for pi in reversed(range(nsub)):
            rows = slice(chunk(pi) * CHUNK, (chunk(pi) + 1) * CHUNK)
            la_c = la[rows]
            b_all = _cum_dot(cumf, la_c)
            bl_all = jnp.sum(la_c, axis=0, keepdims=True)
            for p in range(GH // 2):
                sl = slice(p * LANES, (p + 1) * LANES)
                b, bl = b_all[:, sl], bl_all[:, sl]
                e, ei, ee = jnp.exp(b), jnp.exp(-b), jnp.exp(bl - b)
                qd = q_ref[rows, sl] * (GDK ** -0.5) * e
                ki, kte = k_ref[rows, sl] * ei, k_ref[rows, sl] * ee
                decay = jnp.exp(bl)
                dqd = jnp.zeros((CHUNK, LANES), F32)
                dki = jnp.zeros((CHUNK, LANES), F32)
                dkte = jnp.zeros((CHUNK, LANES), F32)
                ddecay = jnp.zeros((1, LANES), F32)
                for half in range(2):
                    h = 2 * p + half
                    lm = (lane < GDK) if half == 0 else (lane >= GDK)
                    qd_h = jnp.where(lm, qd, 0.0)
                    ki_h = jnp.where(lm, ki, 0.0)
                    kte_h = jnp.where(lm, kte, 0.0)
                    v_h = v_ref[rows, h * GDV:(h + 1) * GDV]
                    do_h = do_ref[rows, h * GDV:(h + 1) * GDV]
                    st = st_ref[pi, h]
                    dst = dstate[h]
                    at_mat = jnp.where(mask_t, _dot_nt(ki_h, qd_h), 0.0)
                    da_mat = jnp.where(mask, _dot_nt(do_h, v_h), 0.0)
                    dat_mat = jnp.where(mask_t, _dot_nt(v_h, do_h), 0.0)
                    dv_ref[rows, h * GDV:(h + 1) * GDV] = _dot(at_mat, do_h) + _dot_nt(kte_h, dst)
                    dqd += _dot(jnp.concatenate([do_h, da_mat], axis=1), jnp.concatenate([st, ki_h], axis=0))
                    dki += _dot(dat_mat, qd_h)
                    dkte += _dot(v_h, dst)
                    ddecay += jnp.sum(dst * st, axis=0, keepdims=True)
                    dstate[h] = dst * decay + _dot_tn(do_h, qd_h)
                dq_ref[rows, sl] = dqd * e * (GDK ** -0.5)
                dk_ref[rows, sl] = dki * ei + dkte * ee
                db = dqd * qd - dki * ki - dkte * kte
                dbl = jnp.sum(dkte * kte, axis=0, keepdims=True) + decay * ddecay
                da_buf[rows, sl] = _cum_dot(cumf, db, True) + dbl
        da = da_buf[...] * (1.0 / GTEMP) * _sigmoid(-a)
        dt_ref[...] = _dot_nt(da, wg_ref[...])
        dwg_ref[...] += _dot_tn(t, da)
        dbg_ref[...] += jnp.sum(da, axis=0, keepdims=True)

    def prow(j):
        return row(nsteps - 1 - j)

    return pl.pallas_call(
        body, grid=(nsteps,),
        in_specs=[pl.BlockSpec((rb, GQK), lambda j: (prow(j), OGQ // GQK)),
                  pl.BlockSpec((rb, GQK), lambda j: (prow(j), OGK // GQK)),
                  pl.BlockSpec((rb, GW), lambda j: (prow(j), OGV // GW)),
                  pl.BlockSpec((rb, LANES), lambda j: (prow(j), OT // LANES)),
                  pl.BlockSpec((LANES, GQK), lambda j: (0, 0)),
                  pl.BlockSpec((1, GQK), lambda j: (0, 0)),
                  pl.BlockSpec((nsub, GH, GDV, LANES), lambda j: (nsteps - 1 - j, 0, 0, 0)),
                  pl.BlockSpec((rb, GW), lambda j: (prow(j), 0))],
        out_specs=[pl.BlockSpec((rb, GQK), lambda j: (prow(j), 0)),
                   pl.BlockSpec((rb, GQK), lambda j: (prow(j), 0)),
                   pl.BlockSpec((rb, GW), lambda j: (prow(j), 0)),
                   pl.BlockSpec((rb, LANES), lambda j: (prow(j), 0)),
                   pl.BlockSpec((LANES, GQK), lambda j: (0, 0)),
                   pl.BlockSpec((1, GQK), lambda j: (0, 0))],
        out_shape=[jax.ShapeDtypeStruct((s, GQK), F32), jax.ShapeDtypeStruct((s, GQK), F32),
                   jax.ShapeDtypeStruct((s, GW), F32), jax.ShapeDtypeStruct((s, LANES), F32),
                   jax.ShapeDtypeStruct((LANES, GQK), F32), jax.ShapeDtypeStruct((1, GQK), F32)],
        scratch_shapes=[pltpu.VMEM((GH, GDV, LANES), F32), pltpu.VMEM((rb, GQK), F32)],
        name=name, compiler_params=_cp(("arbitrary",)))(proj, proj, proj, proj, wg_pad, bg, states, d_o)


def _rot_half(x):
    lane = lax.broadcasted_iota(jnp.int32, x.shape, 1)
    first = (lane % MROPE) < (MROPE // 2)
    return jnp.where(first, -pltpu.roll(x, LANES - MROPE // 2, 1), pltpu.roll(x, MROPE // 2, 1))


def _mla_prep(proj, cos, sin, qg, kvg, w_uq, w_ukv, name):
    s = proj.shape[0]
    tr = _rows(s)

    def body(mq_ref, mkv_ref, t_ref, cos_ref, sin_ref, qg_ref, kvg_ref, wuq_ref, wukv_ref, q_ref, k_ref, v_ref):
        cosv, sinv = cos_ref[...], sin_ref[...]
        lane = lax.broadcasted_iota(jnp.int32, (tr, LANES), 1)

        def rope(xv):
            return xv * cosv + _rot_half(xv) * sinv

        qm = _dot(_rms(mq_ref[...], qg_ref[...]), wuq_ref[...])
        kv = _dot(_rms(mkv_ref[...], kvg_ref[...]), wukv_ref[...])
        kr_lo = jnp.where(lane < MROPE, rope(t_ref[...]), 0.0)
        kr_hi = pltpu.roll(kr_lo, MROPE, 1)
        for p in range(MH // 2):
            r = rope(qm[:, MW + p * LANES:MW + (p + 1) * LANES]).astype(q_ref.dtype)
            q_ref[2 * p, :, LANES:] = r
            q_ref[2 * p + 1, :, LANES:] = r
        for h in range(MH):
            q_ref[h, :, :LANES] = qm[:, h * LANES:(h + 1) * LANES].astype(q_ref.dtype)
            k_ref[h, :, :LANES] = kv[:, 2 * h * LANES:(2 * h + 1) * LANES].astype(k_ref.dtype)
            k_ref[h, :, LANES:] = (kr_lo if h % 2 == 0 else kr_hi).astype(k_ref.dtype)
            v_ref[h] = kv[:, (2 * h + 1) * LANES:(2 * h + 2) * LANES].astype(v_ref.dtype)

    def full(shape):
        return pl.BlockSpec(shape, lambda i: (0,) * len(shape))

    return pl.pallas_call(
        body, grid=(s // tr,),
        in_specs=[pl.BlockSpec((tr, MQL), lambda i: (i, OMQ // MQL)),
                  pl.BlockSpec((tr, MKVL), lambda i: (i, OMKV // MKVL)),
                  pl.BlockSpec((tr, LANES), lambda i: (i, OT // LANES)),
                  pl.BlockSpec((tr, LANES), lambda i: (i, 0)),
                  pl.BlockSpec((tr, LANES), lambda i: (i, 0)),
                  full((1, MQL)), full((1, MKVL)), full((MQL, MQW)), full((MKVL, MKVW))],
        out_specs=[pl.BlockSpec((MH, tr, 2 * LANES), lambda i: (0, i, 0)),
                   pl.BlockSpec((MH, tr, 2 * LANES), lambda i: (0, i, 0)),
                   pl.BlockSpec((MH, tr, LANES), lambda i: (0, i, 0))],
        out_shape=[jax.ShapeDtypeStruct((MH, s, 2 * LANES), MXU), jax.ShapeDtypeStruct((MH, s, 2 * LANES), MXU),
                   jax.ShapeDtypeStruct((MH, s, LANES), MXU)],
        name=name, compiler_params=_cp(("parallel",)))(proj, proj, proj, cos, sin, qg, kvg, w_uq, w_ukv)


def _mla_prep_bwd(proj, cos, sin, qg, kvg, w_uq, w_ukv, d_q, d_k, d_v, name):
    s = proj.shape[0]
    tr = _rows(s)

    def body(mq_ref, mkv_ref, cos_ref, sin_ref, qg_ref, kvg_ref, wuq_ref, wukv_ref, dq_ref, dk_ref, dv_ref,
             dmq_ref, dmkv_ref, dt_ref, dwuq_ref, dwukv_ref, dqg_ref, dkvg_ref):
        @pl.when(pl.program_id(0) == 0)
        def _():
            for r in (dwuq_ref, dwukv_ref, dqg_ref, dkvg_ref):
                r[...] = jnp.zeros_like(r)

        cosv, sinv = cos_ref[...], sin_ref[...]
        lane = lax.broadcasted_iota(jnp.int32, (tr, LANES), 1)
        lo = lane < MROPE

        def unrope(dv):
            return dv * cosv - _rot_half(dv * sinv)

        parts = [dq_ref[h, :, :LANES] for h in range(MH)]
        for p in range(MH // 2):
            parts.append(unrope(jnp.where(lo, dq_ref[2 * p, :, LANES:], dq_ref[2 * p + 1, :, LANES:])))
        d_qm = jnp.concatenate(parts, axis=1)
        mq, qgv = mq_ref[...], qg_ref[...]
        cq = _rms(mq, qgv)
        dwuq_ref[...] += _dot_tn(cq, d_qm)
        dmq, dqg = _rms_bwd(_dot_nt(d_qm, wuq_ref[...]), mq, qgv)
        dmq_ref[...] = dmq.astype(dmq_ref.dtype)
        dqg_ref[...] += dqg

        parts = []
        for h in range(MH):
            parts += [dk_ref[h, :, :LANES], dv_ref[h]]
        d_kv = jnp.concatenate(parts, axis=1)
        mkv, kvgv = mkv_ref[...], kvg_ref[...]
        ckv = _rms(mkv, kvgv)
        dwukv_ref[...] += _dot_tn(ckv, d_kv)
        dmkv, dkvg = _rms_bwd(_dot_nt(d_kv, wukv_ref[...]), mkv, kvgv)
        dmkv_ref[...] = dmkv.astype(dmkv_ref.dtype)
        dkvg_ref[...] += dkvg

        even = dk_ref[0, :, LANES:] + dk_ref[2, :, LANES:] + dk_ref[4, :, LANES:]
        odd = dk_ref[1, :, LANES:] + dk_ref[3, :, LANES:] + dk_ref[5, :, LANES:]
        d_kr = jnp.where(lo, even, 0.0) + pltpu.roll(jnp.where(lo, 0.0, odd), MROPE, 1)
        dt_ref[...] = jnp.where(lo, unrope(d_kr), 0.0)

    def full(shape):
        return pl.BlockSpec(shape, lambda i: (0,) * len(shape))

    return pl.pallas_call(
        body, grid=(s // tr,),
        in_specs=[pl.BlockSpec((tr, MQL), lambda i: (i, OMQ // MQL)),
                  pl.BlockSpec((tr, MKVL), lambda i: (i, OMKV // MKVL)),
                  pl.BlockSpec((tr, LANES), lambda i: (i, 0)),
                  pl.BlockSpec((tr, LANES), lambda i: (i, 0)),
                  full((1, MQL)), full((1, MKVL)), full((MQL, MQW)), full((MKVL, MKVW)),
                  pl.BlockSpec((MH, tr, 2 * LANES), lambda i: (0, i, 0)),
                  pl.BlockSpec((MH, tr, 2 * LANES), lambda i: (0, i, 0)),
                  pl.BlockSpec((MH, tr, LANES), lambda i: (0, i, 0))],
        out_specs=[pl.BlockSpec((tr, MQL), lambda i: (i, 0)), pl.BlockSpec((tr, MKVL), lambda i: (i, 0)),
                   pl.BlockSpec((tr, LANES), lambda i: (i, 0)),
                   full((MQL, MQW)), full((MKVL, MKVW)), full((1, MQL)), full((1, MKVL))],
        out_shape=[jax.ShapeDtypeStruct((s, MQL), MXU), jax.ShapeDtypeStruct((s, MKVL), MXU),
                   jax.ShapeDtypeStruct((s, LANES), F32),
                   jax.ShapeDtypeStruct((MQL, MQW), F32), jax.ShapeDtypeStruct((MKVL, MKVW), F32),
                   jax.ShapeDtypeStruct((1, MQL), F32), jax.ShapeDtypeStruct((1, MKVL), F32)],
        name=name, compiler_params=_cp(("arbitrary",)))(proj, proj, cos, sin, qg, kvg, w_uq, w_ukv, d_q, d_k, d_v)


ATT_SCALE = (MNOPE + MROPE) ** -0.5
ATT_SCALE_LOG2 = ATT_SCALE * 1.4426950408889634
ATT_TQ_FWD, ATT_TQ = 2048, 2048
ATT_SUB, ATT_SUB_BWD = 256, 256


def _attn_fwd(q, k, v, name):
    s = q.shape[1]
    tq = min(ATT_TQ_FWD, s)
    sub = min(ATT_SUB, tq)

    def body(q_ref, k_ref, v_ref, o_ref, lse_ref):
        for r0 in range(0, tq, sub):
            rows = slice(r0, r0 + sub)
            sc = _dot_nt(q_ref[0, rows, :], k_ref[0])
            m = jnp.max(sc, axis=-1, keepdims=True)
            p = jnp.exp2((sc - m) * ATT_SCALE_LOG2)
            l = jnp.sum(p, axis=-1, keepdims=True)
            o_ref[rows, :] = _dot(p, v_ref[0]) / l
            lse_ref[0, rows, :] = m * ATT_SCALE_LOG2 + jnp.log2(l)

    return pl.pallas_call(
        body, grid=(MH, s // tq),
        in_specs=[pl.BlockSpec((1, tq, 2 * LANES), lambda h, i: (h, i, 0)),
                  pl.BlockSpec((1, s, 2 * LANES), lambda h, i: (h, 0, 0)),
                  pl.BlockSpec((1, s, LANES), lambda h, i: (h, 0, 0))],
        out_specs=[pl.BlockSpec((tq, LANES), lambda h, i: (i, h)),
                   pl.BlockSpec((1, tq, 1), lambda h, i: (h, i, 0))],
        out_shape=[jax.ShapeDtypeStruct((s, MW), F32), jax.ShapeDtypeStruct((MH, s, 1), F32)],
        name=name, compiler_params=_cp(("parallel", "parallel")))(q, k, v)


def _attn_bwd(q, k, v, o, lse, d_o, name):
    s = q.shape[1]
    tq = min(ATT_TQ, s)
    sub = min(ATT_SUB_BWD, tq)

    def body(q_ref, k_ref, v_ref, o_ref, lse_ref, do_ref, dq_ref, dk_ref, dv_ref):
        @pl.when(pl.program_id(1) == 0)
        def _():
            dk_ref[...] = jnp.zeros_like(dk_ref)
            dv_ref[...] = jnp.zeros_like(dv_ref)

        kv = k_ref[0]
        for r0 in range(0, tq, sub):
            rows = slice(r0, r0 + sub)
            qv, do = q_ref[0, rows, :], do_ref[rows, :]
            p = jnp.exp2(_dot_nt(qv, kv) * ATT_SCALE_LOG2 - lse_ref[0, rows, :])
            delta = jnp.sum(do * o_ref[rows, :], axis=-1, keepdims=True)
            ds = p * (_dot_nt(do, v_ref[0]) - delta)
            dq_ref[0, rows, :] = _dot(ds, kv) * ATT_SCALE
            dk_ref[0] += _dot_tn(ds, qv) * ATT_SCALE
            dv_ref[0] += _dot_tn(p, do)

    return pl.pallas_call(
        body, grid=(MH, s // tq),
        in_specs=[pl.BlockSpec((1, tq, 2 * LANES), lambda h, i: (h, i, 0)),
                  pl.BlockSpec((1, s, 2 * LANES), lambda h, i: (h, 0, 0)),
                  pl.BlockSpec((1, s, LANES), lambda h, i: (h, 0, 0)),
                  pl.BlockSpec((tq, LANES), lambda h, i: (i, h)),
                  pl.BlockSpec((1, tq, 1), lambda h, i: (h, i, 0)),
                  pl.BlockSpec((tq, LANES), lambda h, i: (i, h))],
        out_specs=[pl.BlockSpec((1, tq, 2 * LANES), lambda h, i: (h, i, 0)),
                   pl.BlockSpec((1, s, 2 * LANES), lambda h, i: (h, 0, 0)),
                   pl.BlockSpec((1, s, LANES), lambda h, i: (h, 0, 0))],
        out_shape=[jax.ShapeDtypeStruct((MH, s, 2 * LANES), F32), jax.ShapeDtypeStruct((MH, s, 2 * LANES), F32),
                   jax.ShapeDtypeStruct((MH, s, LANES), F32)],
        name=name, compiler_params=_cp(("parallel", "arbitrary")))(q, k, v, o, lse, d_o)


def _merge_fwd(o_f, o_b, o_att, pre, proj, gng, mog, cog, name):
    s = proj.shape[0]
    tr = _rows_light(s)

    def body(of_ref, ob_ref, oa_ref, pre_ref, z_ref, gng_ref, mog_ref, cog_ref, y_ref):
        z = z_ref[...]
        sz = z * _sigmoid(z)
        osum = of_ref[...] + ob_ref[...]
        gg = gng_ref[...]
        for h in range(GH):
            sl = slice(h * GDV, (h + 1) * GDV)
            y_ref[:, sl] = (_rms(osum[:, sl], gg) * sz[:, sl]).astype(y_ref.dtype)
        y_ref[:, GW:GW + MW] = (_rms(oa_ref[...], mog_ref[...]) * sz[:, GW:GW + MW]).astype(y_ref.dtype)
        y_ref[:, GW + MW:] = (_rms(pre_ref[...], cog_ref[...]) * sz[:, GW + MW:]).astype(y_ref.dtype)

    def row(w):
        return pl.BlockSpec((tr, w), lambda i: (i, 0))

    def vec(w):
        return pl.BlockSpec((1, w), lambda i: (0, 0))

    return pl.pallas_call(
        body, grid=(s // tr,),
        in_specs=[row(GW), row(GW), row(MW), row(CONV_CH), row(D_MIX), vec(GDV), vec(MW), vec(CONV_CH)],
        out_specs=row(D_MIX), out_shape=jax.ShapeDtypeStruct((s, D_MIX), MXU),
        name=name, compiler_params=_cp(("parallel",)))(o_f, o_b, o_att, pre, proj, gng, mog, cog)


def _merge_bwd(d_y, o_f, o_b, o_att, pre, proj, gng, mog, cog, name):
    s = proj.shape[0]
    tr = _rows(s)

    def body(dy_ref, of_ref, ob_ref, oa_ref, pre_ref, z_ref, gng_ref, mog_ref, cog_ref,
             dz_ref, dos_ref, doa_ref, dpre_ref, dgng_ref, dmog_ref, dcog_ref):
        @pl.when(pl.program_id(0) == 0)
        def _():
            for r in (dgng_ref, dmog_ref, dcog_ref):
                r[...] = jnp.zeros_like(r)

        z, dy = z_ref[...], dy_ref[...]
        sg = _sigmoid(z)
        sz = z * sg
        dsz = sg * (1.0 + z * (1.0 - sg))
        dcat = dy * sz
        dyz = dy * dsz
        osum = of_ref[...] + ob_ref[...]
        gg = gng_ref[...]
        dgg = jnp.zeros_like(gg)
        for h in range(GH):
            sl = slice(h * GDV, (h + 1) * GDV)
            dz_ref[:, sl] = (dyz[:, sl] * _rms(osum[:, sl], gg)).astype(dz_ref.dtype)
            dx, dg = _rms_bwd(dcat[:, sl], osum[:, sl], gg)
            dos_ref[:, sl] = dx
            dgg += dg
        dgng_ref[...] += dgg
        sl = slice(GW, GW + MW)
        oa, mg = oa_ref[...], mog_ref[...]
        dz_ref[:, sl] = (dyz[:, sl] * _rms(oa, mg)).astype(dz_ref.dtype)
        dx, dg = _rms_bwd(dcat[:, sl], oa, mg)
        doa_ref[...] = dx
        dmog_ref[...] += dg
        sl = slice(GW + MW, D_MIX)
        pv, cg = pre_ref[...], cog_ref[...]
        dz_ref[:, sl] = (dyz[:, sl] * _rms(pv, cg)).astype(dz_ref.dtype)
        dx, dg = _rms_bwd(dcat[:, sl], pv, cg)
        dpre_ref[...] = dx
        dcog_ref[...] += dg

    def row(w):
        return pl.BlockSpec((tr, w), lambda i: (i, 0))

    def vec(w):
        return pl.BlockSpec((1, w), lambda i: (0, 0))

    def rs(w):
        return jax.ShapeDtypeStruct((s, w), F32)

    def vs(w):
        return jax.ShapeDtypeStruct((1, w), F32)

    return pl.pallas_call(
        body, grid=(s // tr,),
        in_specs=[row(D_MIX), row(GW), row(GW), row(MW), row(CONV_CH), row(D_MIX), vec(GDV), vec(MW), vec(CONV_CH)],
        out_specs=[row(D_MIX), row(GW), row(MW), row(CONV_CH), vec(GDV), vec(MW), vec(CONV_CH)],
        out_shape=[jax.ShapeDtypeStruct((s, D_MIX), MXU),
                   rs(GW), rs(MW), rs(CONV_CH), vs(GDV), vs(MW), vs(CONV_CH)],
        name=name, compiler_params=_cp(("arbitrary",)))(d_y, o_f, o_b, o_att, pre, proj, gng, mog, cog)


def _assemble_dproj(d_z, d_cb, d_cc, d_cx, d_mkv, dv_f, dv_b, dq_f, dq_b, dk_f, dk_b, d_mq, dt_m, dt_f, dt_b, name):
    s = d_z.shape[0]
    tr = _rows_light(s)

    def body(dz, dcb, dcc, dcx, dmkv, dvf, dvb, dqf, dqb, dkf, dkb, dmq, dtm, dtf, dtb, out):
        dt = out.dtype
        out[:, OZ:OZ + D_MIX] = dz[...].astype(dt)
        out[:, OCB:OCB + CONV_CH] = dcb[...].astype(dt)
        out[:, OCC:OCC + CONV_CH] = dcc[...].astype(dt)
        out[:, OCX:OCX + CONV_CH] = dcx[...].astype(dt)
        out[:, OMKV:OMKV + MKVL] = dmkv[...].astype(dt)
        out[:, OGV:OGV + GW] = (dvf[...] + dvb[...]).astype(dt)
        out[:, OGQ:OGQ + GQK] = (dqf[...] + dqb[...]).astype(dt)
        out[:, OGK:OGK + GQK] = (dkf[...] + dkb[...]).astype(dt)
        out[:, OMQ:OMQ + MQL] = dmq[...].astype(dt)
        out[:, OT:OT + LANES] = (dtm[...] + dtf[...] + dtb[...]).astype(dt)

    args = (d_z, d_cb, d_cc, d_cx, d_mkv, dv_f, dv_b, dq_f, dq_b, dk_f, dk_b, d_mq, dt_m, dt_f, dt_b)
    return pl.pallas_call(
        body, grid=(s // tr,),
        in_specs=[pl.BlockSpec((tr, a.shape[1]), lambda i: (i, 0)) for a in args],
        out_specs=pl.BlockSpec((tr, PW), lambda i: (i, 0)),
        out_shape=jax.ShapeDtypeStruct((s, PW), MXU), name=name, compiler_params=_cp(("parallel",)))(*args)


def _layer_fwd(x, mod, wt, cos, sin, tag, late=None, in_after=()):
    shift, scale, gate = mod
    h = _norm_mod(x, wt["norm_g"], scale, shift, f"norm_mod_{tag}")
    (proj,) = _matmul(h, wt["w_in"], dims="nn", tm=2048, tn=256, tk=2048, out_dtypes=(F32,), name=f"in_proj_{tag}",
                      after=in_after)
    if late is not None:
        wt.update(late(proj))
    o_f, st_f = _gla_fwd(proj, wt["wg_pad_f"], wt["bg_f"], False, f"gla_fwd_f_{tag}")
    o_b, st_b = _gla_fwd(proj, wt["wg_pad_b"], wt["bg_b"], True, f"gla_fwd_b_{tag}")
    q, k, v = _mla_prep(proj, cos, sin, wt["q_norm_g"], wt["kv_norm_g"], wt["w_uq"], wt["w_ukv"], f"mla_prep_{tag}")
    o_att, lse = _attn_fwd(q, k, v, f"attn_fwd_{tag}")
    pre = _conv_fwd(proj, wt["conv_w"], f"conv_fwd_{tag}")
    y = _merge_fwd(o_f, o_b, o_att, pre, proj, wt["gla_norm_g"], wt["mla_out_g"], wt["conv_out_g"], f"merge_fwd_{tag}")
    x_new, u = _matmul(y, wt["w_out"], dims="nn", tm=2048, tn=256, tk=2048, out_dtypes=(F32, F32),
                       name=f"out_proj_{tag}", epilogue=lambda acc, xv, gv: (xv + gv * acc, acc),
                       extras=(x, gate), extra_kinds=("mn", "n"))
    saved = dict(x=x, h=h, proj=proj, o_f=o_f, o_b=o_b, st_f=st_f, st_b=st_b, q=q, k=k, v=v,
                 o_att=o_att, lse=lse, pre=pre, y=y, u=u)
    return x_new, saved


def _layer_bwd(d_out, sv, mod, wt, cos, sin, tag, ship=None, dx_first=None, ship_rest=None):
    shift, scale, gate = mod
    proj = sv["proj"]
    d_u, d_gate = _gate_bwd(d_out, sv["u"], gate, f"gate_bwd_{tag}")
    (g_w_out,) = _matmul(sv["y"], d_u, dims="tn", tm=1024, tn=512, tk=2048, out_dtypes=(MXU,), name=f"out_proj_dw_{tag}")
    (d_y,) = _matmul(d_u, wt["w_out"], dims="nt", tm=2048, tn=256, tk=2048, out_dtypes=(F32,), name=f"out_proj_dx_{tag}",
                     after=(g_w_out,))
    d_z, d_osum, d_oatt, d_pre, d_gng, d_mog, d_cog = _merge_bwd(
        d_y, sv["o_f"], sv["o_b"], sv["o_att"], sv["pre"], proj, wt["gla_norm_g"], wt["mla_out_g"], wt["conv_out_g"],
        f"merge_bwd_{tag}")
    d_cb, d_cc, d_cx, d_conv_w = _conv_bwd(proj, wt["conv_w"], d_pre, f"conv_bwd_{tag}")
    d_q, d_k, d_v = _attn_bwd(sv["q"], sv["k"], sv["v"], sv["o_att"], sv["lse"], d_oatt, f"attn_bwd_{tag}")
    d_mq, d_mkv, dt_m, g_w_uq, g_w_ukv, d_qg, d_kvg = _mla_prep_bwd(
        proj, cos, sin, wt["q_norm_g"], wt["kv_norm_g"], wt["w_uq"], wt["w_ukv"], d_q, d_k, d_v, f"mla_prep_bwd_{tag}")
    bg_f, bg_b = wt["bg_f"], wt["bg_b"]
    if ship_rest is not None:
        tok = ship_rest(dict(w_out=g_w_out, w_uq=g_w_uq, w_ukv=g_w_ukv))
        bg_f, bg_b = bg_f + tok, bg_b + tok
    dq_f, dk_f, dv_f, dt_f, d_wg_f, d_bg_f = _gla_bwd(proj, wt["wg_pad_f"], bg_f, sv["st_f"], d_osum, False,
                                                     f"gla_bwd_f_{tag}")
    dq_b, dk_b, dv_b, dt_b, d_wg_b, d_bg_b = _gla_bwd(proj, wt["wg_pad_b"], bg_b, sv["st_b"], d_osum, True,
                                                     f"gla_bwd_b_{tag}")
    d_proj = _assemble_dproj(d_z, d_cb, d_cc, d_cx, d_mkv, dv_f, dv_b, dq_f, dq_b, dk_f, dk_b, d_mq, dt_m, dt_f, dt_b,
                             f"assemble_dproj_{tag}")
    grads = dict(w_out=g_w_out, w_uq=g_w_uq, w_ukv=g_w_ukv,
                 wg_pad_f=d_wg_f, bg_f=d_bg_f, wg_pad_b=d_wg_b, bg_b=d_bg_b, gla_norm_g=d_gng,
                 q_norm_g=d_qg, kv_norm_g=d_kvg, mla_out_g=d_mog, conv_w=d_conv_w, conv_out_g=d_cog)

    def in_dw(after):
        (g_w_in,) = _matmul(sv["h"], d_proj, dims="tn", tm=2048, tn=256, tk=2048, out_dtypes=(MXU,),
                            name=f"in_proj_dw_{tag}", after=after)
        grads["w_in"] = g_w_in
        return dict(w_in=g_w_in, w_out=g_w_out, w_uq=g_w_uq, w_ukv=g_w_ukv)

    def in_dx(after):
        (d_h,) = _matmul(d_proj, wt["w_in"], dims="nt", tm=1024, tn=512, tk=PW, out_dtypes=(F32,),
                         name=f"in_proj_dx_{tag}", after=after)
        d_x, d_shift, d_scale, d_ng = _norm_mod_bwd(d_h, sv["x"], d_out, wt["norm_g"], scale, f"norm_mod_bwd_{tag}")
        grads["norm_g"] = d_ng
        return d_x, (d_shift, d_scale, d_gate)

    if dx_first is None:
        big = in_dw(())
        d_x, d_mod = in_dx((big["w_in"],) if ship is None else ship(big))
    else:
        d_x, d_mod = in_dx(())
        big = in_dw(dx_first(d_x, d_mod, grads))
        ship(big)
    return d_x, d_mod, grads


IN_SEGS = ((3808, 5856), (2272, 3808), (1952, 2208), (768, 1536), (0, 768), (1568, 1952), (2208, 2272), (1536, 1568))
UQ_SEGS = (tuple((h * (MNOPE + MROPE), h * (MNOPE + MROPE) + MNOPE) for h in range(MH))
           + tuple((h * (MNOPE + MROPE) + MNOPE, (h + 1) * (MNOPE + MROPE)) for h in range(MH)))


def _heads_apart(w):
    w3 = w.reshape(w.shape[:-1] + (MH, MNOPE + MROPE))
    return jnp.concatenate([w3[..., :MNOPE].reshape(w.shape[:-1] + (MH * MNOPE,)),
                            w3[..., MNOPE:].reshape(w.shape[:-1] + (MH * MROPE,))], axis=-1)


def _heads_together(g):
    nope = g[..., :MH * MNOPE].reshape(g.shape[:-1] + (MH, MNOPE))
    rope = g[..., MH * MNOPE:].reshape(g.shape[:-1] + (MH, MROPE))
    return jnp.concatenate([nope, rope], axis=-1).reshape(g.shape[:-1] + (MQW,))


def _perm_gathered(g, segs, width):
    per = g.shape[-1]
    parts, total = [], 0
    for a, b in segs:
        c = a
        while c < b:
            j = c // per
            hi = min(b, (j + 1) * per)
            parts.append(g[j, :, c - j * per:hi - j * per])
            c = hi
        total += b - a
    if width > total:
        parts.append(jnp.zeros((g.shape[1], width - total), g.dtype))
    return jnp.concatenate(parts, axis=1)


def _scatter_perm(gp, segs, per):
    offs, o = [], 0
    for a, b in segs:
        offs.append((a, b, o))
        o += b - a
    blocks = []
    for j in range(N_DEV):
        lo, hi = j * per, (j + 1) * per
        pieces = []
        for a, b, o in sorted(offs):
            s0, s1 = max(a, lo), min(b, hi)
            if s0 < s1:
                pieces.append(gp[:, o + s0 - a:o + s1 - a])
        blocks.append(jnp.concatenate(pieces, axis=1))
    return jnp.stack(blocks)


def _prep_layer_weights(w_in, w_out, w_uq, w_ukv, small):
    def vec(v):
        return v.reshape(1, -1).astype(F32)

    zeros = functools.partial(jnp.zeros, dtype=F32)
    wg_f, wg_b = small["gla_wg_f"].astype(F32), small["gla_wg_b"].astype(F32)
    wg_pad_f = jnp.concatenate([zeros((MROPE, GQK)), wg_f, zeros((LANES - MROPE - GRANK, GQK))], axis=0)
    wg_pad_b = jnp.concatenate([zeros((MROPE + GRANK, GQK)), wg_b, zeros((LANES - MROPE - 2 * GRANK, GQK))], axis=0)
    wt = dict(norm_g=vec(small["norm_g"]), wg_pad_f=wg_pad_f, wg_pad_b=wg_pad_b,
              bg_f=vec(small["gla_bg_f"]), bg_b=vec(small["gla_bg_b"]), gla_norm_g=vec(small["gla_norm_g"]),
              q_norm_g=vec(small["mla_q_norm_g"]), kv_norm_g=vec(small["mla_kv_norm_g"]),
              mla_out_g=vec(small["mla_out_g"]), conv_w=small["conv_w"].astype(F32),
              conv_out_g=vec(small["conv_out_g"]))
    for name, w in (("w_in", w_in), ("w_out", w_out), ("w_uq", w_uq), ("w_ukv", w_ukv)):
        if w is not None:
            wt[name] = w.astype(MXU)
    return wt


def _natural_small(gr):
    return dict(norm_g=gr["norm_g"][0],
                gla_wg_f=gr["wg_pad_f"][MROPE:MROPE + GRANK], gla_bg_f=gr["bg_f"][0],
                gla_wg_b=gr["wg_pad_b"][MROPE + GRANK:MROPE + 2 * GRANK], gla_bg_b=gr["bg_b"][0],
                gla_norm_g=gr["gla_norm_g"][0], mla_q_norm_g=gr["q_norm_g"][0], mla_kv_norm_g=gr["kv_norm_g"][0],
                mla_out_g=gr["mla_out_g"][0], conv_w=gr["conv_w"], conv_out_g=gr["conv_out_g"][0])


def _exchange(arrs, name, scatter, space):
    n = len(arrs)

    def body(*refs):
        ins, outs = refs[:n], refs[n:2 * n]
        send_sems, recv_sems, loc_sems = refs[2 * n:]
        ax, ay, ac = lax.axis_index("x"), lax.axis_index("y"), lax.axis_index("c")
        me = 4 * ax + 2 * ay + ac

        def src(a, to):
            return ins[a].at[to] if scatter else ins[a]

        def remote(a, r, dst_slot):
            px = 1 - ax if r & 4 else ax
            py = 1 - ay if r & 2 else ay
            pc = 1 - ac if r & 1 else ac
            return pltpu.make_async_remote_copy(
                src_ref=src(a, 4 * px + 2 * py + pc), dst_ref=outs[a].at[dst_slot(4 * px + 2 * py + pc)],
                send_sem=send_sems.at[a, r - 1], recv_sem=recv_sems.at[a, r - 1],
                device_id=(px, py, pc), device_id_type=MESH)

        locs = [pltpu.make_async_copy(src(a, me), outs[a].at[me], loc_sems.at[a]) for a in range(n)]
        for cp in locs:
            cp.start()
        sends = [remote(a, r, lambda peer: me) for r in range(1, N_DEV) for a in range(n)]
        for cp in sends:
            cp.start()
        for r in range(1, N_DEV):
            for a in range(n):
                remote(a, r, lambda peer: peer).wait_recv()
        for cp in sends:
            cp.wait_send()
        for cp in locs:
            cp.wait()

    def out_shape(a):
        return jax.ShapeDtypeStruct(a.shape if scatter else (N_DEV,) + a.shape, a.dtype)

    spec = pl.BlockSpec(memory_space=space)
    return pl.pallas_call(
        body, in_specs=[spec] * n, out_specs=[spec] * n, out_shape=[out_shape(a) for a in arrs],
        scratch_shapes=[pltpu.SemaphoreType.DMA((n, N_DEV - 1)), pltpu.SemaphoreType.DMA((n, N_DEV - 1)),
                        pltpu.SemaphoreType.DMA((n,))],
        name=name, compiler_params=pltpu.CompilerParams(vmem_limit_bytes=VMEM_LIMIT))(*arrs)


def _peer(r):
    ax, ay, ac = lax.axis_index("x"), lax.axis_index("y"), lax.axis_index("c")
    px = 1 - ax if r & 4 else ax
    py = 1 - ay if r & 2 else ay
    pc = 1 - ac if r & 1 else ac
    return (px, py, pc), 4 * px + 2 * py + pc


def _slot(rel_div):
    rel, div = rel_div
    idx = _peer(rel)[1]
    return idx if div == 1 else idx // div


AG_SPREAD = tuple((r, None, (0, 1), (r, 1)) for r in (1, 2, 4, 6))
AG_FORWARD = tuple((1, (k, 1), (k, 1), (1 ^ k, 1)) for k in (2, 4, 6))
RS_PAIR = tuple((1, (1 ^ k, 1), (1 ^ k, 2), (k, 2)) for k in (0, 2, 4, 6))
RS_CHIPS = tuple((r, (r, 2), (0, 2), (r, 2)) for r in (2, 4, 6))


def _plan_copies(plan, n, src_refs, land_refs, send_sems, recv_sems, arriving):
    out = []
    for i, (r, src, dst, recv) in enumerate(plan):
        peer = _peer(r)[0]
        for a in range(n):
            out.append(pltpu.make_async_remote_copy(
                src_ref=src_refs[a] if src is None else src_refs[a].at[_slot(src)],
                dst_ref=land_refs[a].at[_slot(recv if arriving else dst)],
                send_sem=send_sems.at[i * n + a], recv_sem=recv_sems.at[i * n + a],
                device_id=peer, device_id_type=MESH))
    return out


def _exchange_hbm(plan, srcs, lands, name, after=()):
    n = len(lands)
    fresh = isinstance(lands[0], jax.ShapeDtypeStruct)
    ins = ([] if srcs is None else list(srcs)) + ([] if fresh else list(lands))
    ns = 0 if srcs is None else n
    n_data = len(ins)
    ins = ins + list(after)

    def body(*refs):
        outs = refs[len(ins):len(ins) + n]
        send_sems, recv_sems = refs[-2:]
        src_refs = refs[:n] if srcs is not None else refs[ns:ns + n]
        sends = _plan_copies(plan, n, src_refs, outs, send_sems, recv_sems, False)
        for cp in sends:
            cp.start()
        for cp in _plan_copies(plan, n, src_refs, outs, send_sems, recv_sems, True):
            cp.wait_recv()
        for cp in sends:
            cp.wait_send()

    hbm = pl.BlockSpec(memory_space=pltpu.HBM)
    k = len(plan) * n
    return pl.pallas_call(
        body, name=name, in_specs=[hbm] * n_data + [pl.BlockSpec(memory_space=pl.ANY)] * len(after), out_specs=[hbm] * n,
        out_shape=[jax.ShapeDtypeStruct(a.shape, a.dtype) for a in lands],
        scratch_shapes=[pltpu.SemaphoreType.DMA((k,)), pltpu.SemaphoreType.DMA((k,))],
        input_output_aliases={} if fresh else {ns + i: i for i in range(n)},
        compiler_params=pltpu.CompilerParams(vmem_limit_bytes=VMEM_LIMIT))(*ins)


def _plan_start(plan, srcs, land_shapes, after, name):
    n = len(srcs)

    def body(*refs):
        src_refs, land_refs = refs[:n], refs[n:2 * n]
        send_sems, recv_sems = refs[2 * n + 1], refs[2 * n + 2]
        for cp in _plan_copies(plan, n, src_refs, land_refs, send_sems, recv_sems, False):
            cp.start()
        refs[-1][...] = jnp.zeros_like(refs[-1])

    hbm = pl.BlockSpec(memory_space=pltpu.HBM)
    sem = pl.BlockSpec(memory_space=pltpu.SEMAPHORE)
    k = len(plan) * n
    srcs = [pltpu.with_memory_space_constraint(a, pltpu.HBM) for a in srcs]
    lands = [pltpu.with_memory_space_constraint(lax.empty(shp, a.dtype), pltpu.HBM) for shp, a in zip(land_shapes, srcs)]
    res = pl.pallas_call(
        body, name=name,
        in_specs=[hbm] * (2 * n) + [pl.BlockSpec(memory_space=pl.ANY)],
        out_specs=[sem, sem] + [hbm] * (2 * n) + [pl.BlockSpec(memory_space=pltpu.VMEM)],
        out_shape=[pltpu.SemaphoreType.DMA((k,)), pltpu.SemaphoreType.DMA((k,))]
        + [pltpu.HBM(a.shape, a.dtype) for a in srcs] + [pltpu.HBM(shp, a.dtype) for shp, a in zip(land_shapes, srcs)]
        + [jax.ShapeDtypeStruct((8, LANES), F32)],
        input_output_aliases={i: 2 + i for i in range(2 * n)},
        compiler_params=pltpu.CompilerParams(has_side_effects=pltpu.SideEffectType.DATAFLOW_SIDE_EFFECTING),
    )(*srcs, *lands, after)
    return res[0], res[1], list(res[2:2 + n]), list(res[2 + n:2 + 2 * n]), res[-1]


def _plan_wait(plan, handle, after, name):
    send_sems, recv_sems, srcs, lands, _ = handle
    n = len(srcs)
    after = list(after)

    def body(*refs):
        src_refs, land_refs = refs[:n], refs[n:2 * n]
        ssem, rsem = refs[2 * n], refs[2 * n + 1]
        for cp in _plan_copies(plan, n, src_refs, land_refs, ssem, rsem, False):
            cp.wait_send()
        for cp in _plan_copies(plan, n, src_refs, land_refs, ssem, rsem, True):
            cp.wait_recv()

    hbm = pl.BlockSpec(memory_space=pltpu.HBM)
    sem = pl.BlockSpec(memory_space=pltpu.SEMAPHORE)
    res = pl.pallas_call(
        body, name=name,
        in_specs=[hbm] * (2 * n) + [sem, sem] + [pl.BlockSpec(memory_space=pl.ANY)] * len(after),
        out_specs=[hbm] * (2 * n),
        out_shape=[pltpu.HBM(a.shape, a.dtype) for a in srcs] + [pltpu.HBM(a.shape, a.dtype) for a in lands],
        input_output_aliases={i: i for i in range(2 * n)},
        compiler_params=pltpu.CompilerParams(has_side_effects=pltpu.SideEffectType.DATAFLOW_SIDE_EFFECTING),
    )(*srcs, *lands, send_sems, recv_sems, *after)
    return list(res[:n]), list(res[n:])


def _pair_sum(send, got, core, name):
    _, r, c = send.shape
    tr = 1024 if r % 1024 == 0 else r

    def body(core_ref, s_ref, g_ref, o_ref):
        o_ref[0] = (s_ref[0].astype(F32) + g_ref[0].astype(F32)).astype(o_ref.dtype)

    return pl.pallas_call(
        body, name=name,
        grid_spec=pltpu.PrefetchScalarGridSpec(
            num_scalar_prefetch=1, grid=(N_DEV // 2, r // tr),
            in_specs=[pl.BlockSpec((1, tr, c), lambda kc, i, core_ref: (2 * kc + core_ref[0], i, 0)),
                      pl.BlockSpec((1, tr, c), lambda kc, i, core_ref: (kc, i, 0))],
            out_specs=pl.BlockSpec((1, tr, c), lambda kc, i, core_ref: (kc, i, 0))),
        out_shape=jax.ShapeDtypeStruct((N_DEV // 2, r, c), send.dtype),
        compiler_params=_cp(("parallel", "parallel")))(core, send, got)


def _ada_mod(c_all, ada_w, ada_b_cols, name):
    nl, d, wc = ada_w.shape

    def body(c_ref, w_ref, b_ref, ca_ref, mod_ref):
        cv = c_ref[...]
        ca = cv * _sigmoid(cv)
        ca_ref[...] = ca
        mod_ref[0] = _dotf(ca, w_ref[0]) + b_ref[0]

    return pl.pallas_call(
        body, grid=(nl,),
        in_specs=[pl.BlockSpec((N_DEV, d), lambda l: (0, 0)), pl.BlockSpec((1, d, wc), lambda l: (l, 0, 0)),
                  pl.BlockSpec((1, 1, wc), lambda l: (l, 0, 0))],
        out_specs=[pl.BlockSpec((N_DEV, d), lambda l: (0, 0)), pl.BlockSpec((1, N_DEV, wc), lambda l: (l, 0, 0))],
        out_shape=[jax.ShapeDtypeStruct((N_DEV, d), F32), jax.ShapeDtypeStruct((nl, N_DEV, wc), F32)],
        name=name, compiler_params=_cp(("arbitrary",)))(c_all, ada_w, ada_b_cols)


def _adam(w, g, m, v):
    m2 = ADAM_B1 * m + (1.0 - ADAM_B1) * g
    v2 = ADAM_B2 * v + (1.0 - ADAM_B2) * (g * g)
    m_hat = m2 / (1.0 - ADAM_B1 ** ADAM_STEP)
    v_hat = v2 / (1.0 - ADAM_B2 ** ADAM_STEP)
    delta = -ADAM_LR * (m_hat / (jnp.sqrt(v_hat) + ADAM_EPS) + ADAM_WD * w)
    return delta, m2, v2


def _ada_grad_adam(c_act, d_mod, w, m, v, name):
    nl, d, wc = w.shape
    tk = min(1024, d)

    def body(c_ref, dm_ref, w_ref, m_ref, v_ref, g_ref, dl_ref, m2_ref, v2_ref):
        g = _dotf_tn(c_ref[...], dm_ref[0])
        delta, m2, v2 = _adam(w_ref[0], g, m_ref[0], v_ref[0])
        g_ref[0], dl_ref[0], m2_ref[0], v2_ref[0] = g, delta, m2, v2

    blk = pl.BlockSpec((1, tk, wc), lambda l, i: (l, i, 0))
    shp = jax.ShapeDtypeStruct(w.shape, F32)
    return pl.pallas_call(
        body, grid=(nl, d // tk),
        in_specs=[pl.BlockSpec((N_DEV, tk), lambda l, i: (0, i)), pl.BlockSpec((1, N_DEV, wc), lambda l, i: (l, 0, 0)),
                  blk, blk, blk],
        out_specs=[blk] * 4, out_shape=[shp] * 4, name=name,
        compiler_params=_cp(("parallel", "parallel")))(c_act, d_mod, w, m, v)


def _adam_big(recv, w, m, v, layer, prev, name, after=()):
    nl, r, c = w.shape
    tr = 512 if r % 512 == 0 else r
    nparts = recv.shape[0]

    def body(rc_ref, w_ref, m_ref, v_ref, *rest):
        g_ref, dl_ref, m2_ref, v2_ref = rest[-4:]
        g = rc_ref[0].astype(F32)
        for d in range(1, nparts):
            g = g + rc_ref[d].astype(F32)
        delta, m2, v2 = _adam(w_ref[0], g, m_ref[0], v_ref[0])
        g_ref[0], dl_ref[0], m2_ref[0], v2_ref[0] = g, delta, m2, v2

    blk = pl.BlockSpec((1, tr, c), lambda i: (layer, i, 0))
    shp = jax.ShapeDtypeStruct(w.shape, F32)
    prev = () if prev is None else tuple(prev)
    return pl.pallas_call(
        body, grid=(r // tr,),
        in_specs=[pl.BlockSpec((nparts, tr, c), lambda i: (0, i, 0)), blk, blk, blk]
        + [pl.BlockSpec(memory_space=pl.ANY)] * (len(prev) + len(after)),
        out_specs=[blk] * 4, out_shape=[shp] * 4, name=name,
        input_output_aliases={4 + j: j for j in range(len(prev))},
        compiler_params=_cp(("parallel",)))(recv, w, m, v, *prev, *after)


def _sum_devices(gathered, name):
    _, r, c = gathered.shape

    def body(g_ref, o_ref):
        acc = g_ref[0]
        for d in range(1, N_DEV):
            acc = acc + g_ref[d]
        o_ref[...] = acc

    spec = pl.BlockSpec(memory_space=pltpu.VMEM)
    return pl.pallas_call(body, in_specs=[spec], out_specs=spec, out_shape=jax.ShapeDtypeStruct((r, c), F32),
                          name=name, compiler_params=pltpu.CompilerParams(vmem_limit_bytes=VMEM_LIMIT))(gathered)


def _adam_small(ws, gs, ms, vs, name):
    n = len(ws)

    def body(*refs):
        for i in range(n):
            w_ref, g_ref, m_ref, v_ref = (refs[k * n + i] for k in range(4))
            dl_ref, m2_ref, v2_ref = (refs[(4 + k) * n + i] for k in range(3))
            dl_ref[...], m2_ref[...], v2_ref[...] = _adam(w_ref[...], g_ref[...], m_ref[...], v_ref[...])

    spec = pl.BlockSpec(memory_space=pltpu.VMEM)
    shapes = [jax.ShapeDtypeStruct(w.shape, F32) for w in ws]
    res = pl.pallas_call(body, in_specs=[spec] * (4 * n), out_specs=[spec] * (3 * n), out_shape=shapes * 3, name=name,
                         compiler_params=pltpu.CompilerParams(vmem_limit_bytes=VMEM_LIMIT))(*ws, *gs, *ms, *vs)
    return res[:n], res[n:2 * n], res[2 * n:]


def _pack(parts):
    flat = jnp.concatenate([p.reshape(-1).astype(F32) for p in parts])
    assert flat.shape[0] % LANES == 0, flat.shape
    return flat.reshape(-1, LANES)


def _unpack(packed, shapes):
    flat = packed.reshape(-1)
    out, off = [], 0
    for shp in shapes:
        size = 1
        for dim in shp:
            size *= dim
        out.append(flat[off:off + size].reshape(shp))
        off += size
    return out


def _gather_cols(g, per):
    g = jnp.moveaxis(g, 0, -2)
    return g.reshape(g.shape[:-2] + (N_DEV * per,))


def _scatter_cols(g, per):
    return jnp.moveaxis(g.reshape(g.shape[:-1] + (N_DEV, per)), -2, 0)


def _my_cols(full, me, per):
    return lax.dynamic_slice_in_dim(full, me * per, per, axis=full.ndim - 1)


def kernel(x, c, positions, ada_w, ada_b, norm_g, w_in, gla_wg_f, gla_bg_f, gla_wg_b, gla_bg_b, gla_norm_g, mla_q_norm_g, mla_kv_norm_g, mla_w_uq, mla_w_ukv, mla_out_g, conv_w, conv_out_g, w_out, final_g, loss_target, m_ada_w, m_ada_b, m_norm_g, m_w_in, m_gla_wg_f, m_gla_bg_f, m_gla_wg_b, m_gla_bg_b, m_gla_norm_g, m_mla_q_norm_g, m_mla_kv_norm_g, m_mla_w_uq, m_mla_w_ukv, m_mla_out_g, m_conv_w, m_conv_out_g, m_w_out, m_final_g, v_ada_w, v_ada_b, v_norm_g, v_w_in, v_gla_wg_f, v_gla_bg_f, v_gla_wg_b, v_gla_bg_b, v_gla_norm_g, v_mla_q_norm_g, v_mla_kv_norm_g, v_mla_w_uq, v_mla_w_ukv, v_mla_out_g, v_conv_w, v_conv_out_g, v_w_out, v_final_g):
    me = 4 * lax.axis_index("x") + 2 * lax.axis_index("y") + lax.axis_index("c")
    nl = ada_w.shape[0]
    s, d = x.shape[1], x.shape[2]
    ada_cols = ada_w.shape[2]
    wgc, cwc = gla_wg_f.shape[2], conv_w.shape[2]

    (g0,) = _exchange([_pack([c, gla_wg_f, gla_wg_b, conv_w])], "gather_small_in", False, pltpu.VMEM)
    g0 = g0.reshape(N_DEV, -1)
    o1, o2, o3 = d, d + gla_wg_f.size, d + 2 * gla_wg_f.size
    c_all = g0[:, :o1]
    wgf_full = _gather_cols(g0[:, o1:o2].reshape((N_DEV,) + gla_wg_f.shape), wgc)
    wgb_full = _gather_cols(g0[:, o2:o3].reshape((N_DEV,) + gla_wg_b.shape), wgc)
    convw_full = _gather_cols(g0[:, o3:].reshape((N_DEV,) + conv_w.shape), cwc)

    ada_b_cols = _my_cols(ada_b, me, ada_cols).reshape(nl, 1, ada_cols)
    c_act, mod_cols = _ada_mod(c_all, ada_w, ada_b_cols, "ada_mod")
    (g1,) = _exchange([_pack([mod_cols])], "gather_mod", False, pltpu.VMEM)
    mod_all = g1.reshape(N_DEV, nl, N_DEV, ada_cols)
    mod_mine = _gather_cols(lax.dynamic_index_in_dim(mod_all, me, axis=2, keepdims=False), ada_cols)

    inv_freq = ROPE_THETA ** (-jnp.arange(0, MROPE, 2, dtype=F32) / MROPE)
    ang = positions[0].astype(F32)[:, None] * inv_freq
    cos, sin = jnp.tile(jnp.cos(ang), (1, LANES * 2 // MROPE)), jnp.tile(jnp.sin(ang), (1, LANES * 2 // MROPE))

    big = [w_in, w_out, mla_w_uq, mla_w_ukv]
    big_names = ["w_in", "w_out", "mla_w_uq", "mla_w_ukv"]

    def local_blocks(l):
        return [w[l].astype(MXU) for w in big]

    def put_own(lands, own):
        return [lax.dynamic_update_index_in_dim(ld, o, me, 0) for ld, o in zip(lands, own)]

    def layer_weights(l, gw_in=None, gw_out=None, gw_uq=None, gw_ukv=None):
        small = dict(norm_g=norm_g[l], gla_wg_f=wgf_full[l], gla_bg_f=gla_bg_f[l], gla_wg_b=wgb_full[l],
                     gla_bg_b=gla_bg_b[l], gla_norm_g=gla_norm_g[l], mla_q_norm_g=mla_q_norm_g[l],
                     mla_kv_norm_g=mla_kv_norm_g[l], mla_out_g=mla_out_g[l], conv_w=convw_full[l],
                     conv_out_g=conv_out_g[l])
        return _prep_layer_weights(
            None if gw_in is None else _perm_gathered(gw_in, IN_SEGS, PW),
            None if gw_out is None else gw_out.reshape((-1,) + gw_out.shape[2:]),
            None if gw_uq is None else _heads_apart(_gather_cols(gw_uq, mla_w_uq.shape[2])),
            None if gw_ukv is None else _gather_cols(gw_ukv, mla_w_ukv.shape[2]), small)

    def land_shapes(blocks, slots):
        return [jax.ShapeDtypeStruct((slots,) + b.shape, b.dtype) for b in blocks]

    def slots_of(blocks):
        return [(N_DEV,) + b.shape for b in blocks]

    def forwarded(lands, blocks, tag):
        return put_own(_exchange_hbm(AG_FORWARD, None, lands, f"gather_{tag}_forward"), blocks)

    first = local_blocks(0)
    w_in_start = _plan_start(AG_SPREAD, first[:1], slots_of(first[:1]), mod_mine, "gather_w_in_l0_start")
    adam_w_in = [a + w_in_start[-1][0, 0] for a in (w_in, m_w_in, v_w_in)]
    (gw_in,) = forwarded(*reversed(_plan_wait(AG_SPREAD, w_in_start, adam_w_in, "gather_w_in_l0_wait")), "w_in_l0")
    rest = _plan_start(AG_SPREAD, first[1:], slots_of(first[1:]), gw_in, "gather_rest_l0_start")
    h = x[0]
    saved, layers, mods = [], [], []
    pending = {}
    for l in range(nl):
        shift, scale, gate = (mod_mine[l, i * d:(i + 1) * d].reshape(1, d) for i in range(3))
        nxt = local_blocks(l + 1) if l + 1 < nl else None

        def start_next(after, wt_late, l=l, nxt=nxt):
            if nxt is not None:
                pending[l + 1] = _plan_start(AG_SPREAD, nxt, slots_of(nxt), after, f"gather_weights_l{l + 1}_start")
                wt_late["q_norm_g"] = layers[l]["q_norm_g"] + pending[l + 1][-1][0, 0]
            return wt_late

        if l == 0:
            in_after = (rest[-1],)
            layers.append(layer_weights(0, gw_in))

            def late(proj):
                got = forwarded(*reversed(_plan_wait(AG_SPREAD, rest, [proj], "gather_rest_l0_wait")), "rest_l0")
                full = layer_weights(0, None, *got)
                return start_next(got[0], {k: full[k] for k in ("w_out", "w_uq", "w_ukv")})
        else:
            got = forwarded(*reversed(_plan_wait(AG_SPREAD, pending.pop(l), [h], f"gather_weights_l{l}_wait")), f"weights_l{l}")
            layers.append(layer_weights(l, *got))
            in_after = ()

            def late(proj):
                return start_next(proj, {})
        mods.append((shift, scale, gate))
        h, sv = _layer_fwd(h, mods[l], layers[l], cos, sin, f"l{l}", late, in_after)
        saved.append(sv)
        blocks = nxt
    loss_part, d_h, d_final_g = _final_loss(h, final_g.reshape(1, d), loss_target[0], "final_loss")
    loss = lax.psum(loss_part[0, 0], ("x", "y", "c"))
    shift, scale, gate = mods[-1]
    mods[-1] = (shift, scale, gate + 0.0 * loss)

    send_of = dict(w_in=lambda g: _scatter_perm(g, IN_SEGS, w_in.shape[2]),
                   w_out=lambda g: g.reshape((N_DEV,) + w_out.shape[1:]),
                   w_uq=lambda g: _scatter_cols(_heads_together(g), mla_w_uq.shape[2]),
                   w_ukv=lambda g: _scatter_cols(g, mla_w_ukv.shape[2]))

    def grad_sends(gr):
        return [send_of[k](g).astype(MXU) for k, g in gr.items()]

    my_chip = me // 2
    my_core = (me % 2).astype(jnp.int32).reshape(1)

    def chip_sums(gr, tag):
        sends = grad_sends(gr)
        got = _exchange_hbm(RS_PAIR, sends, land_shapes([sd[0] for sd in sends], N_DEV // 2), f"scatter_grads_{tag}_pair")
        return [_pair_sum(sd, gt, my_core, f"pair_sum_{k}_{tag}") for sd, gt, k in zip(sends, got, gr)]

    def with_own_chip(lands, sums):
        return [lax.dynamic_update_index_in_dim(ld, lax.dynamic_index_in_dim(sm, my_chip, axis=0, keepdims=False),
                                                my_chip, 0) for ld, sm in zip(lands, sums)]

    small_names = ["norm_g", "gla_wg_f", "gla_bg_f", "gla_wg_b", "gla_bg_b", "gla_norm_g", "mla_q_norm_g",
                   "mla_kv_norm_g", "mla_out_g", "conv_w", "conv_out_g"]
    d_mods, grads, recv = [None] * nl, [None] * nl, [None] * nl
    flight = {}
    small = {}

    def gather_small(d_x, d_mod0, gr0):
        d_mods[0], grads[0] = d_mod0, _natural_small(gr0)
        d_mod_mine = jnp.stack([jnp.concatenate(d_mods[l], axis=-1)[0] for l in range(nl)])
        parts = [d_mod_mine] + [jnp.stack([grads[l][n] for l in range(nl)]) for n in small_names] + [d_final_g]
        (g2,) = _exchange([_pack(parts)], "gather_small_grads", False, pltpu.VMEM)
        small["d_mod_all"] = g2.reshape(N_DEV, -1)[:, :d_mod_mine.size].reshape(N_DEV, nl, 3 * d)
        small["summed"] = dict(zip(["ada_b"] + small_names + ["final_g"],
                                   _unpack(_sum_devices(g2, "sum_small_grads"), [p.shape for p in parts])))
        return (g2,)

    pairs = {}
    def end_flight(key, after, name):
        sm, lands = _plan_wait(RS_CHIPS, flight.pop(key)[0], after, name)
        return with_own_chip(lands, sm)

    for l in reversed(range(nl)):
        def ship(big_grads, l=l):
            if l > 0:
                sends = grad_sends(big_grads)
                pairs[l] = (_plan_start(RS_PAIR, sends, [(N_DEV // 2,) + sd.shape[1:] for sd in sends],
                                        big_grads["w_in"], f"scatter_grads_l{l}_pair_start"), sends)
                return (pairs[l][0][-1],)
            sm = chip_sums(dict(w_in=big_grads["w_in"]), f"l{l}")
            flight[l] = (_plan_start(RS_CHIPS, sm, [a.shape for a in sm], big_grads["w_in"], f"scatter_grads_l{l}_start"),
                         sm)
            return (flight[l][0][-1],)

        def ship_rest(rest_grads, l=l):
            if l + 1 in flight:
                recv[l + 1] = end_flight(l + 1, list(rest_grads.values()), f"scatter_grads_l{l + 1}_wait")
            sm = chip_sums(rest_grads, f"l{l}_rest")
            flight["rest"] = (_plan_start(RS_CHIPS, sm, [a.shape for a in sm], rest_grads["w_out"],
                                          f"scatter_grads_l{l}_rest_start"), sm)
            return flight["rest"][0][-1][0, 0]

        shift, scale, gate = mods[l]
        if l + 1 in flight:
            gate = gate + flight[l + 1][0][-1][0, 0]
        if l > 0:
            d_h, d_mods[l], gr = _layer_bwd(d_h, saved[l], (shift, scale, gate), layers[l], cos, sin, f"l{l}", ship)
            grads[l] = _natural_small(gr)
            sends, got = _plan_wait(RS_PAIR, pairs.pop(l)[0], [d_h], f"scatter_grads_l{l}_pair_wait")
            sm = [_pair_sum(sd, gt, my_core, f"pair_sum_{n}_l{l}") for sd, gt, n in zip(sends, got, big_names)]
            flight[l] = (_plan_start(RS_CHIPS, sm, [a.shape for a in sm], d_h, f"scatter_grads_l{l}_start"), sm)
        else:
            d_h, _, _ = _layer_bwd(d_h, saved[l], (shift, scale, gate), layers[l], cos, sin, f"l{l}", ship, gather_small,
                                   ship_rest)
    pending = flight[0][0]
    grad_x = d_h[None]
    summed = small["summed"]
    summed["gla_wg_f"] = _my_cols(summed["gla_wg_f"], me, wgc)
    summed["gla_wg_b"] = _my_cols(summed["gla_wg_b"], me, wgc)
    summed["conv_w"] = _my_cols(summed["conv_w"], me, cwc)

    d_mod_cols = jnp.moveaxis(_my_cols(small["d_mod_all"], me, ada_cols), 0, 1) + pending[-1][0, 0]
    out = {}
    out["ada_w"] = _ada_grad_adam(c_act, d_mod_cols, ada_w, m_ada_w, v_ada_w, "ada_grad_adam")

    given = dict(ada_b=(ada_b, m_ada_b, v_ada_b), norm_g=(norm_g, m_norm_g, v_norm_g),
                 gla_wg_f=(gla_wg_f, m_gla_wg_f, v_gla_wg_f), gla_bg_f=(gla_bg_f, m_gla_bg_f, v_gla_bg_f),
                 gla_wg_b=(gla_wg_b, m_gla_wg_b, v_gla_wg_b), gla_bg_b=(gla_bg_b, m_gla_bg_b, v_gla_bg_b),
                 gla_norm_g=(gla_norm_g, m_gla_norm_g, v_gla_norm_g),
                 mla_q_norm_g=(mla_q_norm_g, m_mla_q_norm_g, v_mla_q_norm_g),
                 mla_kv_norm_g=(mla_kv_norm_g, m_mla_kv_norm_g, v_mla_kv_norm_g),
                 mla_out_g=(mla_out_g, m_mla_out_g, v_mla_out_g), conv_w=(conv_w, m_conv_w, v_conv_w),
                 conv_out_g=(conv_out_g, m_conv_out_g, v_conv_out_g), final_g=(final_g, m_final_g, v_final_g))
    names = list(given)

    def two_d(a):
        return a.reshape(1, -1) if a.ndim == 1 else a

    g_nat = [summed[n].reshape(given[n][0].shape) for n in names]
    res = _adam_small([two_d(given[n][0]) for n in names], [two_d(g) for g in g_nat],
                      [two_d(given[n][1]) for n in names], [two_d(given[n][2]) for n in names], "adam_small")
    for i, n in enumerate(names):
        out[n] = (g_nat[i],) + tuple(r[i].reshape(given[n][0].shape) for r in res)

    state = dict(w_in=adam_w_in, w_out=(w_out, m_w_out, v_w_out), mla_w_uq=(mla_w_uq, m_mla_w_uq, v_mla_w_uq),
                 mla_w_ukv=(mla_w_ukv, m_mla_w_ukv, v_mla_w_ukv))
    done = [out["ada_w"][0], res[0][0]]
    for l in reversed(range(nl)):
        if l == 0:
            rest = end_flight("rest", done, "scatter_grads_l0_rest_wait")
            recv[0] = end_flight(0, done + rest[:1], "scatter_grads_l0_wait") + rest
        for i, n in enumerate(big_names):
            out[n] = _adam_big(recv[l][i], *state[n], l, out.get(n), f"adam_{n}_l{l}",
                               (pending[-1],))
        done = done + [out[n][0] for n in big_names]

    order = ["ada_w", "ada_b", "norm_g", "w_in", "gla_wg_f", "gla_bg_f", "gla_wg_b", "gla_bg_b", "gla_norm_g",
             "mla_q_norm_g", "mla_kv_norm_g", "mla_w_uq", "mla_w_ukv", "mla_out_g", "conv_w", "conv_out_g", "w_out",
             "final_g"]
    return (loss, grad_x, *[out[n][0] for n in order], *[out[n][1] for n in order], *[out[n][2] for n in order],
            *[out[n][3] for n in order])
```

```python
import functools

import jax
import jax.numpy as jnp
from jax import lax
from jax.experimental import pallas as pl
from jax.experimental.pallas import tpu as pltpu

F32 = jnp.float32
MXU = jnp.bfloat16
HI = lax.Precision.HIGHEST
N_DEV = 8
MESH = pl.DeviceIdType.MESH

D_MIX = 2048
GH, GDK, GDV = 6, 64, 128
GW = GH * GDV
GQK = GH * GDK
GRANK = 16
GTEMP = 16.0
CHUNK = 64
MH, MQL, MKVL, MNOPE, MROPE, MDV = 6, 384, 256, 128, 64, 128
MW = MH * MDV
MQW = MH * (MNOPE + MROPE)
MKVW = MH * (MNOPE + MDV)
CONV_CH = 512
ROPE_THETA = 10000.0
EPS = 1e-6
IN_DIM = 5856
OZ, OCB, OCC, OCX, OMKV, OGV, OGQ, OGK, OMQ, OT = 0, 2048, 2560, 3072, 3584, 3840, 4608, 4992, 5376, 5760
PW = 5888
LANES = 128
V7X_VMEM_BYTES = 64 * 1024 * 1024
VMEM_LIMIT = V7X_VMEM_BYTES * 7 // 8

ADAM_LR, ADAM_B1, ADAM_B2, ADAM_EPS, ADAM_WD, ADAM_STEP = 0.001, 0.9, 0.999, 1e-08, 0.01, 10


def _cp(sem=None):
    return pltpu.CompilerParams(dimension_semantics=sem, vmem_limit_bytes=VMEM_LIMIT)


def _dot(a, b):
    return jnp.dot(a.astype(MXU), b.astype(MXU), preferred_element_type=F32)


def _dot_nt(a, b):
    return lax.dot_general(a.astype(MXU), b.astype(MXU), (((1,), (1,)), ((), ())), preferred_element_type=F32)


def _dot_tn(a, b):
    return lax.dot_general(a.astype(MXU), b.astype(MXU), (((0,), (0,)), ((), ())), preferred_element_type=F32)


def _dotf(a, b):
    return jnp.dot(a, b, precision=HI, preferred_element_type=F32)


def _dotf_tn(a, b):
    return lax.dot_general(a, b, (((0,), (0,)), ((), ())), precision=HI, preferred_element_type=F32)


def _split3(x):
    hi = x.astype(jnp.bfloat16)
    r1 = x - hi.astype(F32)
    mid = r1.astype(jnp.bfloat16)
    lo = (r1 - mid.astype(F32)).astype(jnp.bfloat16)
    return hi, mid, lo


def _cum_dot(cum, x, transpose=False):
    dn = (((0,), (0,)), ((), ())) if transpose else (((1,), (0,)), ((), ()))
    cb = cum.astype(jnp.bfloat16)
    parts = [lax.dot_general(cb, p, dn, preferred_element_type=F32) for p in _split3(x)]
    return parts[0] + parts[1] + parts[2]


def _rows(s):
    return min(256, s)


def _rows_light(s):
    return min(512, s)


def _rms(x, g):
    r = lax.rsqrt(jnp.mean(x * x, axis=-1, keepdims=True) + EPS)
    return x * r * g


def _rms_bwd(dy, x, g):
    r = lax.rsqrt(jnp.mean(x * x, axis=-1, keepdims=True) + EPS)
    xh = x * r
    dxh = dy * g
    dg = jnp.sum(dy * xh, axis=0, keepdims=True)
    dx = r * (dxh - xh * jnp.mean(dxh * xh, axis=-1, keepdims=True))
    return dx, dg


def _sigmoid(z):
    return jax.nn.sigmoid(z)


def _matmul(a, b, *, dims, tm, tn, tk, out_dtypes, name, epilogue=None, extras=(), extra_kinds=(), after=()):
    if dims == "nn":
        (m, k), n, mul = a.shape, b.shape[1], _dot
    elif dims == "nt":
        (m, k), n, mul = a.shape, b.shape[0], _dot_nt
    else:
        (k, m), n, mul = a.shape, b.shape[1], _dot_tn
    tm, tn, tk = min(tm, m), min(tn, n), min(tk, k)
    assert m % tm == 0 and n % tn == 0 and k % tk == 0, (m, n, k, tm, tn, tk)
    if dims == "nn":
        a_spec = pl.BlockSpec((tm, tk), lambda i, j, kk: (i, kk))
        b_spec = pl.BlockSpec((tk, tn), lambda i, j, kk: (kk, j))
    elif dims == "nt":
        a_spec = pl.BlockSpec((tm, tk), lambda i, j, kk: (i, kk))
        b_spec = pl.BlockSpec((tn, tk), lambda i, j, kk: (j, kk))
    else:
        a_spec = pl.BlockSpec((tk, tm), lambda i, j, kk: (kk, i))
        b_spec = pl.BlockSpec((tk, tn), lambda i, j, kk: (kk, j))
    nk = k // tk
    n_extra = len(extras)
    n_out = len(out_dtypes)
    n_after = len(after)
    extra_specs = []
    for kind in extra_kinds:
        if kind == "mn":
            extra_specs.append(pl.BlockSpec((tm, tn), lambda i, j, kk: (i, j)))
        else:
            extra_specs.append(pl.BlockSpec((1, tn), lambda i, j, kk: (0, j)))

    def finish(res, ex, outs):
        vals = (res,) if epilogue is None else epilogue(res, *[e[...] for e in ex])
        for o, v in zip(outs, vals):
            o[...] = v.astype(o.dtype)

    def body(*refs):
        a_ref, b_ref = refs[0], refs[1]
        ex = refs[2:2 + n_extra]
        outs = refs[2 + n_extra + n_after:2 + n_extra + n_after + n_out]
        if nk == 1:
            finish(mul(a_ref[...], b_ref[...]), ex, outs)
            return
        acc = refs[-1]
        kk = pl.program_id(2)

        @pl.when(kk == 0)
        def _():
            acc[...] = jnp.zeros_like(acc)

        acc[...] += mul(a_ref[...], b_ref[...])

        @pl.when(kk == nk - 1)
        def _():
            finish(acc[...], ex, outs)

    out_spec = pl.BlockSpec((tm, tn), lambda i, j, kk: (i, j))
    res = pl.pallas_call(
        body, grid=(m // tm, n // tn, nk),
        in_specs=[a_spec, b_spec] + extra_specs + [pl.BlockSpec(memory_space=pl.ANY)] * n_after,
        out_specs=[out_spec] * n_out,
        out_shape=[jax.ShapeDtypeStruct((m, n), dt) for dt in out_dtypes],
        scratch_shapes=[] if nk == 1 else [pltpu.VMEM((tm, tn), F32)],
        name=name, compiler_params=_cp(("parallel", "parallel", "arbitrary")),
    )(a, b, *extras, *after)
    return res


def _norm_mod(x, g, scale, shift, name):
    s, d = x.shape
    tr = _rows_light(s)

    def body(x_ref, g_ref, sc_ref, sh_ref, h_ref):
        h = _rms(x_ref[...], g_ref[...]) * (1.0 + sc_ref[...]) + sh_ref[...]
        h_ref[...] = h.astype(h_ref.dtype)

    row = pl.BlockSpec((tr, d), lambda i: (i, 0))
    vec = pl.BlockSpec((1, d), lambda i: (0, 0))
    return pl.pallas_call(body, grid=(s // tr,), in_specs=[row, vec, vec, vec], out_specs=row,
                          out_shape=jax.ShapeDtypeStruct((s, d), MXU), name=name,
                          compiler_params=_cp(("parallel",)))(x, g, scale, shift)


def _norm_mod_bwd(d_h, x, d_out, g, scale, name):
    s, d = x.shape
    tr = _rows(s)

    def body(dh_ref, x_ref, do_ref, g_ref, sc_ref, dx_ref, dsh_ref, dsc_ref, dg_ref):
        i = pl.program_id(0)

        @pl.when(i == 0)
        def _():
            dsh_ref[...] = jnp.zeros_like(dsh_ref)
            dsc_ref[...] = jnp.zeros_like(dsc_ref)
            dg_ref[...] = jnp.zeros_like(dg_ref)

        dh = dh_ref[...]
        xv = x_ref[...]
        gv = g_ref[...]
        r = lax.rsqrt(jnp.mean(xv * xv, axis=-1, keepdims=True) + EPS)
        xh = xv * r
        dsh_ref[...] += jnp.sum(dh, axis=0, keepdims=True)
        dsc_ref[...] += jnp.sum(dh * (xh * gv), axis=0, keepdims=True)
        dhn = dh * (1.0 + sc_ref[...])
        dg_ref[...] += jnp.sum(dhn * xh, axis=0, keepdims=True)
        dxh = dhn * gv
        dx_ref[...] = do_ref[...] + r * (dxh - xh * jnp.mean(dxh * xh, axis=-1, keepdims=True))

    row = pl.BlockSpec((tr, d), lambda i: (i, 0))
    vec = pl.BlockSpec((1, d), lambda i: (0, 0))
    vshape = jax.ShapeDtypeStruct((1, d), F32)
    return pl.pallas_call(body, grid=(s // tr,), in_specs=[row, row, row, vec, vec],
                          out_specs=[row, vec, vec, vec],
                          out_shape=[jax.ShapeDtypeStruct((s, d), F32), vshape, vshape, vshape],
                          name=name, compiler_params=_cp(("arbitrary",)))(d_h, x, d_out, g, scale)


def _gate_bwd(d_out, u, gate, name):
    s, d = d_out.shape
    tr = _rows_light(s)

    def body(do_ref, u_ref, gt_ref, du_ref, dgt_ref):
        @pl.when(pl.program_id(0) == 0)
        def _():
            dgt_ref[...] = jnp.zeros_like(dgt_ref)

        do = do_ref[...]
        du_ref[...] = (do * gt_ref[...]).astype(du_ref.dtype)
        dgt_ref[...] += jnp.sum(do * u_ref[...], axis=0, keepdims=True)

    row = pl.BlockSpec((tr, d), lambda i: (i, 0))
    vec = pl.BlockSpec((1, d), lambda i: (0, 0))
    return pl.pallas_call(body, grid=(s // tr,), in_specs=[row, row, vec], out_specs=[row, vec],
                          out_shape=[jax.ShapeDtypeStruct((s, d), MXU), jax.ShapeDtypeStruct((1, d), F32)],
                          name=name, compiler_params=_cp(("arbitrary",)))(d_out, u, gate)


def _final_loss(x, g, target, name):
    s, d = x.shape
    tr = _rows_light(s)

    def body(x_ref, g_ref, t_ref, loss_ref, dx_ref, dg_ref):
        @pl.when(pl.program_id(0) == 0)
        def _():
            loss_ref[...] = jnp.zeros_like(loss_ref)
            dg_ref[...] = jnp.zeros_like(dg_ref)

        xv = x_ref[...]
        gv = g_ref[...]
        diff = _rms(xv, gv) - t_ref[...]
        part = 0.5 * jnp.sum(jnp.sum(diff * diff, axis=-1, keepdims=True) / d, axis=0, keepdims=True)
        loss_ref[...] += jnp.broadcast_to(part, loss_ref.shape)
        dx, dg = _rms_bwd(diff / d, xv, gv)
        dx_ref[...] = dx
        dg_ref[...] += dg

    row = pl.BlockSpec((tr, d), lambda i: (i, 0))
    vec = pl.BlockSpec((1, d), lambda i: (0, 0))
    lvec = pl.BlockSpec((1, LANES), lambda i: (0, 0))
    return pl.pallas_call(body, grid=(s // tr,), in_specs=[row, vec, row], out_specs=[lvec, row, vec],
                          out_shape=[jax.ShapeDtypeStruct((1, LANES), F32), jax.ShapeDtypeStruct((s, d), F32),
                                     jax.ShapeDtypeStruct((1, d), F32)],
                          name=name, compiler_params=_cp(("arbitrary",)))(x, g, target)


def _shift_rows(u, s, down):
    ri = lax.broadcasted_iota(jnp.int32, u.shape, 0)
    if down:
        return jnp.where(ri == 0, 0.0, pltpu.roll(u, 1, 0))
    return jnp.where(ri == s - 1, 0.0, pltpu.roll(u, s - 1, 0))


def _conv_fwd(proj, conv_w, name):
    s = proj.shape[0]
    nt = CONV_CH // LANES

    def body(cb_ref, cc_ref, cx_ref, w_ref, pre_ref):
        u = cc_ref[...] * cx_ref[...]
        conv = _shift_rows(u, s, True) * w_ref[0:1, :] + u * w_ref[1:2, :] + _shift_rows(u, s, False) * w_ref[2:3, :]
        pre_ref[...] = cb_ref[...] * conv

    def col(off):
        return pl.BlockSpec((s, LANES), lambda j: (0, off // LANES + j))

    return pl.pallas_call(body, grid=(nt,), in_specs=[col(OCB), col(OCC), col(OCX), pl.BlockSpec((3, LANES), lambda j: (0, j))],
                          out_specs=pl.BlockSpec((s, LANES), lambda j: (0, j)),
                          out_shape=jax.ShapeDtypeStruct((s, CONV_CH), F32), name=name,
                          compiler_params=_cp(("parallel",)))(proj, proj, proj, conv_w)


def _conv_bwd(proj, conv_w, d_pre, name):
    s = proj.shape[0]
    nt = CONV_CH // LANES

    def body(cb_ref, cc_ref, cx_ref, w_ref, dp_ref, dcb_ref, dcc_ref, dcx_ref, dw_ref):
        cc, cx = cc_ref[...], cx_ref[...]
        u = cc * cx
        up, dn = _shift_rows(u, s, True), _shift_rows(u, s, False)
        w0, w1, w2 = w_ref[0:1, :], w_ref[1:2, :], w_ref[2:3, :]
        conv = up * w0 + u * w1 + dn * w2
        dp = dp_ref[...]
        dcb_ref[...] = (dp * conv).astype(dcb_ref.dtype)
        dconv = dp * cb_ref[...]
        du = _shift_rows(dconv, s, False) * w0 + dconv * w1 + _shift_rows(dconv, s, True) * w2
        dcc_ref[...] = (du * cx).astype(dcc_ref.dtype)
        dcx_ref[...] = (du * cc).astype(dcx_ref.dtype)
        dw_ref[0:1, :] = jnp.sum(dconv * up, axis=0, keepdims=True)
        dw_ref[1:2, :] = jnp.sum(dconv * u, axis=0, keepdims=True)
        dw_ref[2:3, :] = jnp.sum(dconv * dn, axis=0, keepdims=True)

    def col(off):
        return pl.BlockSpec((s, LANES), lambda j: (0, off // LANES + j))

    blk = pl.BlockSpec((s, LANES), lambda j: (0, j))
    wblk = pl.BlockSpec((3, LANES), lambda j: (0, j))
    full = jax.ShapeDtypeStruct((s, CONV_CH), MXU)
    return pl.pallas_call(body, grid=(nt,), in_specs=[col(OCB), col(OCC), col(OCX), wblk, blk],
                          out_specs=[blk, blk, blk, wblk],
                          out_shape=[full, full, full, jax.ShapeDtypeStruct((3, CONV_CH), F32)],
                          name=name, compiler_params=_cp(("parallel",)))(proj, proj, proj, conv_w, d_pre)


GLA_SUB = 8


def _gla_gates(t_ref, wg_ref, bg_ref):
    t = t_ref[...]
    a = _dot(t, wg_ref[...]) + bg_ref[...]
    la = (jnp.minimum(a, 0.0) - jnp.log(1.0 + jnp.exp(-jnp.abs(a)))) / GTEMP
    return t, a, la


def _gla_masks(reverse):
    ri = lax.broadcasted_iota(jnp.int32, (CHUNK, CHUNK), 0)
    ci = lax.broadcasted_iota(jnp.int32, (CHUNK, CHUNK), 1)
    if reverse:
        cum, mask, mask_t = ci >= ri, ci > ri, ri > ci
    else:
        cum, mask, mask_t = ci <= ri, ci <= ri, ri <= ci
    return cum.astype(F32), mask, mask_t


def _gla_specs(s, reverse):
    nsub = min(GLA_SUB, s // CHUNK)
    nsteps = s // (CHUNK * nsub)

    def row(n):
        return nsteps - 1 - n if reverse else n

    def chunk(pi):
        return nsub - 1 - pi if reverse else pi

    return nsub, nsteps, row, chunk


def _gla_fwd(proj, wg_pad, bg, reverse, name):
    s = proj.shape[0]
    nsub, nsteps, row, chunk = _gla_specs(s, reverse)
    rb = nsub * CHUNK

    def body(q_ref, k_ref, v_ref, t_ref, wg_ref, bg_ref, o_ref, st_ref, state):
        @pl.when(pl.program_id(0) == 0)
        def _():
            state[...] = jnp.zeros_like(state)

        _, _, la = _gla_gates(t_ref, wg_ref, bg_ref)
        cumf, mask, _ = _gla_masks(reverse)
        lane = lax.broadcasted_iota(jnp.int32, (CHUNK, LANES), 1)
        for pi in range(nsub):
            rows = slice(chunk(pi) * CHUNK, (chunk(pi) + 1) * CHUNK)
            la_c = la[rows]
            b_all = _cum_dot(cumf, la_c)
            bl_all = jnp.sum(la_c, axis=0, keepdims=True)
            for p in range(GH // 2):
                sl = slice(p * LANES, (p + 1) * LANES)
                b, bl = b_all[:, sl], bl_all[:, sl]
                qd = q_ref[rows, sl] * (GDK ** -0.5) * jnp.exp(b)
                ki = k_ref[rows, sl] * jnp.exp(-b)
                kte = k_ref[rows, sl] * jnp.exp(bl - b)
                decay = jnp.exp(bl)
                for half in range(2):
                    h = 2 * p + half
                    lm = (lane < GDK) if half == 0 else (lane >= GDK)
                    qd_h = jnp.where(lm, qd, 0.0)
                    kte_h = jnp.where(lm, kte, 0.0)
                    v_h = v_ref[rows, h * GDV:(h + 1) * GDV]
                    st = state[h]
                    a_mat = jnp.where(mask, _dot_nt(qd_h, ki), 0.0)
                    o_ref[rows, h * GDV:(h + 1) * GDV] = _dot(a_mat, v_h) + _dot_nt(qd_h, st)
                    st_ref[pi, h] = st
                    state[h] = st * decay + _dot_tn(v_h, kte_h)

    return pl.pallas_call(
        body, grid=(nsteps,),
        in_specs=[pl.BlockSpec((rb, GQK), lambda n: (row(n), OGQ // GQK)),
                  pl.BlockSpec((rb, GQK), lambda n: (row(n), OGK // GQK)),
                  pl.BlockSpec((rb, GW), lambda n: (row(n), OGV // GW)),
                  pl.BlockSpec((rb, LANES), lambda n: (row(n), OT // LANES)),
                  pl.BlockSpec((LANES, GQK), lambda n: (0, 0)),
                  pl.BlockSpec((1, GQK), lambda n: (0, 0))],
        out_specs=[pl.BlockSpec((rb, GW), lambda n: (row(n), 0)),
                   pl.BlockSpec((nsub, GH, GDV, LANES), lambda n: (n, 0, 0, 0))],
        out_shape=[jax.ShapeDtypeStruct((s, GW), F32), jax.ShapeDtypeStruct((s // CHUNK, GH, GDV, LANES), F32)],
        scratch_shapes=[pltpu.VMEM((GH, GDV, LANES), F32)],
        name=name, compiler_params=_cp(("arbitrary",)))(proj, proj, proj, proj, wg_pad, bg)


def _gla_bwd(proj, wg_pad, bg, states, d_o, reverse, name):
    s = proj.shape[0]
    nsub, nsteps, row, chunk = _gla_specs(s, reverse)
    rb = nsub * CHUNK

    def body(q_ref, k_ref, v_ref, t_ref, wg_ref, bg_ref, st_ref, do_ref,
             dq_ref, dk_ref, dv_ref, dt_ref, dwg_ref, dbg_ref, dstate, da_buf):
        @pl.when(pl.program_id(0) == 0)
        def _():
            dstate[...] = jnp.zeros_like(dstate)
            dwg_ref[...] = jnp.zeros_like(dwg_ref)
            dbg_ref[...] = jnp.zeros_like(dbg_ref)

        t, a, la = _gla_gates(t_ref, wg_ref, bg_ref)
        cumf, mask, mask_t = _gla_masks(reverse)
        lane = lax.broadcasted_iota(jnp.int32, (CHUNK, LANES), 1)
        for pi in reversed(range(nsub)):
            rows = slice(chunk(pi) * CHUNK, (chunk(pi) + 1) * CHUNK)
            la_c = la[rows]
            b_all = _cum_dot(cumf, la_c)
            bl_all = jnp.sum(la_c, axis=0, keepdims=True)
            for p in range(GH // 2):
                sl = slice(p * LANES, (p + 1) * LANES)
                b, bl = b_all[:, sl], bl_all[:, sl]
                e, ei, ee = jnp.exp(b), jnp.exp(-b), jnp.exp(bl - b)
                qd = q_ref[rows, sl] * (GDK ** -0.5) * e
                ki, kte = k_ref[rows, sl] * ei, k_ref[rows, sl] * ee
                decay = jnp.exp(bl)
                dqd = jnp.zeros((CHUNK, LANES), F32)
                dki = jnp.zeros((CHUNK, LANES), F32)
                dkte = jnp.zeros((CHUNK, LANES), F32)
                ddecay = jnp.zeros((1, LANES), F32)
                for half in range(2):
                    h = 2 * p + half
                    lm = (lane < GDK) if half == 0 else (lane >= GDK)
                    qd_h = jnp.where(lm, qd, 0.0)
                    ki_h = jnp.where(lm, ki, 0.0)
                    kte_h = jnp.where(lm, kte, 0.0)
                    v_h = v_ref[rows, h * GDV:(h + 1) * GDV]
                    do_h = do_ref[rows, h * GDV:(h + 1) * GDV]
                    st = st_ref[pi, h]
                    dst = dstate[h]
                    at_mat = jnp.where(mask_t, _dot_nt(ki_h, qd_h), 0.0)
                    da_mat = jnp.where(mask, _dot_nt(do_h, v_h), 0.0)
                    dat_mat = jnp.where(mask_t, _dot_nt(v_h, do_h), 0.0)
                    dv_ref[rows, h * GDV:(h + 1) * GDV] = _dot(at_mat, do_h) + _dot_nt(kte_h, dst)
                    dqd += _dot(jnp.concatenate([do_h, da_mat], axis=1), jnp.concatenate([st, ki_h], axis=0))
                    dki += _dot(dat_mat, qd_h)
                    dkte += _dot(v_h, dst)
                    ddecay += jnp.sum(dst * st, axis=0, keepdims=True)
                    dstate[h] = dst * decay + _dot_tn(do_h, qd_h)
                dq_ref[rows, sl] = dqd * e * (GDK ** -0.5)
                dk_ref[rows, sl] = dki * ei + dkte * ee
                db = dqd * qd - dki * ki - dkte * kte
                dbl = jnp.sum(dkte * kte, axis=0, keepdims=True) + decay * ddecay
                da_buf[rows, sl] = _cum_dot(cumf, db, True) + dbl
        da = da_buf[...] * (1.0 / GTEMP) * _sigmoid(-a)
        dt_ref[...] = _dot_nt(da, wg_ref[...])
        dwg_ref[...] += _dot_tn(t, da)
        dbg_ref[...] += jnp.sum(da, axis=0, keepdims=True)

    def prow(j):
        return row(nsteps - 1 - j)

    return pl.pallas_call(
        body, grid=(nsteps,),
        in_specs=[pl.BlockSpec((rb, GQK), lambda j: (prow(j), OGQ // GQK)),
                  pl.BlockSpec((rb, GQK), lambda j: (prow(j), OGK // GQK)),
                  pl.BlockSpec((rb, GW), lambda j: (prow(j), OGV // GW)),
                  pl.BlockSpec((rb, LANES), lambda j: (prow(j), OT // LANES)),
                  pl.BlockSpec((LANES, GQK), lambda j: (0, 0)),
                  pl.BlockSpec((1, GQK), lambda j: (0, 0)),
                  pl.BlockSpec((nsub, GH, GDV, LANES), lambda j: (nsteps - 1 - j, 0, 0, 0)),
                  pl.BlockSpec((rb, GW), lambda j: (prow(j), 0))],
        out_specs=[pl.BlockSpec((rb, GQK), lambda j: (prow(j), 0)),
                   pl.BlockSpec((rb, GQK), lambda j: (prow(j), 0)),
                   pl.BlockSpec((rb, GW), lambda j: (prow(j), 0)),
                   pl.BlockSpec((rb, LANES), lambda j: (prow(j), 0)),
                   pl.BlockSpec((LANES, GQK), lambda j: (0, 0)),
                   pl.BlockSpec((1, GQK), lambda j: (0, 0))],
        out_shape=[jax.ShapeDtypeStruct((s, GQK), F32), jax.ShapeDtypeStruct((s, GQK), F32),
                   jax.ShapeDtypeStruct((s, GW), F32), jax.ShapeDtypeStruct((s, LANES), F32),
                   jax.ShapeDtypeStruct((LANES, GQK), F32), jax.ShapeDtypeStruct((1, GQK), F32)],
        scratch_shapes=[pltpu.VMEM((GH, GDV, LANES), F32), pltpu.VMEM((rb, GQK), F32)],
        name=name, compiler_params=_cp(("arbitrary",)))(proj, proj, proj, proj, wg_pad, bg, states, d_o)


def _rot_half(x):
    lane = lax.broadcasted_iota(jnp.int32, x.shape, 1)
    first = (lane % MROPE) < (MROPE // 2)
    return jnp.where(first, -pltpu.roll(x, LANES - MROPE // 2, 1), pltpu.roll(x, MROPE // 2, 1))


def _mla_prep(proj, cos, sin, qg, kvg, w_uq, w_ukv, name):
    s = proj.shape[0]
    tr = _rows(s)

    def body(mq_ref, mkv_ref, t_ref, cos_ref, sin_ref, qg_ref, kvg_ref, wuq_ref, wukv_ref, q_ref, k_ref, v_ref):
        cosv, sinv = cos_ref[...], sin_ref[...]
        lane = lax.broadcasted_iota(jnp.int32, (tr, LANES), 1)

        def rope(xv):
            return xv * cosv + _rot_half(xv) * sinv

        qm = _dot(_rms(mq_ref[...], qg_ref[...]), wuq_ref[...])
        kv = _dot(_rms(mkv_ref[...], kvg_ref[...]), wukv_ref[...])
        kr_lo = jnp.where(lane < MROPE, rope(t_ref[...]), 0.0)
        kr_hi = pltpu.roll(kr_lo, MROPE, 1)
        for p in range(MH // 2):
            r = rope(qm[:, MW + p * LANES:MW + (p + 1) * LANES]).astype(q_ref.dtype)
            q_ref[2 * p, :, LANES:] = r
            q_ref[2 * p + 1, :, LANES:] = r
        for h in range(MH):
            q_ref[h, :, :LANES] = qm[:, h * LANES:(h + 1) * LANES].astype(q_ref.dtype)
            k_ref[h, :, :LANES] = kv[:, 2 * h * LANES:(2 * h + 1) * LANES].astype(k_ref.dtype)
            k_ref[h, :, LANES:] = (kr_lo if h % 2 == 0 else kr_hi).astype(k_ref.dtype)
            v_ref[h] = kv[:, (2 * h + 1) * LANES:(2 * h + 2) * LANES].astype(v_ref.dtype)

    def full(shape):
        return pl.BlockSpec(shape, lambda i: (0,) * len(shape))

    return pl.pallas_call(
        body, grid=(s // tr,),
        in_specs=[pl.BlockSpec((tr, MQL), lambda i: (i, OMQ // MQL)),
                  pl.BlockSpec((tr, MKVL), lambda i: (i, OMKV // MKVL)),
                  pl.BlockSpec((tr, LANES), lambda i: (i, OT // LANES)),
                  pl.BlockSpec((tr, LANES), lambda i: (i, 0)),
                  pl.BlockSpec((tr, LANES), lambda i: (i, 0)),
                  full((1, MQL)), full((1, MKVL)), full((MQL, MQW)), full((MKVL, MKVW))],
        out_specs=[pl.BlockSpec((MH, tr, 2 * LANES), lambda i: (0, i, 0)),
                   pl.BlockSpec((MH, tr, 2 * LANES), lambda i: (0, i, 0)),
                   pl.BlockSpec((MH, tr, LANES), lambda i: (0, i, 0))],
        out_shape=[jax.ShapeDtypeStruct((MH, s, 2 * LANES), MXU), jax.ShapeDtypeStruct((MH, s, 2 * LANES), MXU),
                   jax.ShapeDtypeStruct((MH, s, LANES), MXU)],
        name=name, compiler_params=_cp(("parallel",)))(proj, proj, proj, cos, sin, qg, kvg, w_uq, w_ukv)


def _mla_prep_bwd(proj, cos, sin, qg, kvg, w_uq, w_ukv, d_q, d_k, d_v, name):
    s = proj.shape[0]
    tr = _rows(s)

    def body(mq_ref, mkv_ref, cos_ref, sin_ref, qg_ref, kvg_ref, wuq_ref, wukv_ref, dq_ref, dk_ref, dv_ref,
             dmq_ref, dmkv_ref, dt_ref, dwuq_ref, dwukv_ref, dqg_ref, dkvg_ref):
        @pl.when(pl.program_id(0) == 0)
        def _():
            for r in (dwuq_ref, dwukv_ref, dqg_ref, dkvg_ref):
                r[...] = jnp.zeros_like(r)

        cosv, sinv = cos_ref[...], sin_ref[...]
        lane = lax.broadcasted_iota(jnp.int32, (tr, LANES), 1)
        lo = lane < MROPE

        def unrope(dv):
            return dv * cosv - _rot_half(dv * sinv)

        parts = [dq_ref[h, :, :LANES] for h in range(MH)]
        for p in range(MH // 2):
            parts.append(unrope(jnp.where(lo, dq_ref[2 * p, :, LANES:], dq_ref[2 * p + 1, :, LANES:])))
        d_qm = jnp.concatenate(parts, axis=1)
        mq, qgv = mq_ref[...], qg_ref[...]
        cq = _rms(mq, qgv)
        dwuq_ref[...] += _dot_tn(cq, d_qm)
        dmq, dqg = _rms_bwd(_dot_nt(d_qm, wuq_ref[...]), mq, qgv)
        dmq_ref[...] = dmq.astype(dmq_ref.dtype)
        dqg_ref[...] += dqg

        parts = []
        for h in range(MH):
            parts += [dk_ref[h, :, :LANES], dv_ref[h]]
        d_kv = jnp.concatenate(parts, axis=1)
        mkv, kvgv = mkv_ref[...], kvg_ref[...]
        ckv = _rms(mkv, kvgv)
        dwukv_ref[...] += _dot_tn(ckv, d_kv)
        dmkv, dkvg = _rms_bwd(_dot_nt(d_kv, wukv_ref[...]), mkv, kvgv)
        dmkv_ref[...] = dmkv.astype(dmkv_ref.dtype)
        dkvg_ref[...] += dkvg

        even = dk_ref[0, :, LANES:] + dk_ref[2, :, LANES:] + dk_ref[4, :, LANES:]
        odd = dk_ref[1, :, LANES:] + dk_ref[3, :, LANES:] + dk_ref[5, :, LANES:]
        d_kr = jnp.where(lo, even, 0.0) + pltpu.roll(jnp.where(lo, 0.0, odd), MROPE, 1)
        dt_ref[...] = jnp.where(lo, unrope(d_kr), 0.0)

    def full(shape):
        return pl.BlockSpec(shape, lambda i: (0,) * len(shape))

    return pl.pallas_call(
        body, grid=(s // tr,),
        in_specs=[pl.BlockSpec((tr, MQL), lambda i: (i, OMQ // MQL)),
                  pl.BlockSpec((tr, MKVL), lambda i: (i, OMKV // MKVL)),
                  pl.BlockSpec((tr, LANES), lambda i: (i, 0)),
                  pl.BlockSpec((tr, LANES), lambda i: (i, 0)),
                  full((1, MQL)), full((1, MKVL)), full((MQL, MQW)), full((MKVL, MKVW)),
                  pl.BlockSpec((MH, tr, 2 * LANES), lambda i: (0, i, 0)),
                  pl.BlockSpec((MH, tr, 2 * LANES), lambda i: (0, i, 0)),
                  pl.BlockSpec((MH, tr, LANES), lambda i: (0, i, 0))],
        out_specs=[pl.BlockSpec((tr, MQL), lambda i: (i, 0)), pl.BlockSpec((tr, MKVL), lambda i: (i, 0)),
                   pl.BlockSpec((tr, LANES), lambda i: (i, 0)),
                   full((MQL, MQW)), full((MKVL, MKVW)), full((1, MQL)), full((1, MKVL))],
        out_shape=[jax.ShapeDtypeStruct((s, MQL), MXU), jax.ShapeDtypeStruct((s, MKVL), MXU),
                   jax.ShapeDtypeStruct((s, LANES), F32),
                   jax.ShapeDtypeStruct((MQL, MQW), F32), jax.ShapeDtypeStruct((MKVL, MKVW), F32),
                   jax.ShapeDtypeStruct((1, MQL), F32), jax.ShapeDtypeStruct((1, MKVL), F32)],
        name=name, compiler_params=_cp(("arbitrary",)))(proj, proj, cos, sin, qg, kvg, w_uq, w_ukv, d_q, d_k, d_v)


ATT_SCALE = (MNOPE + MROPE) ** -0.5
ATT_SCALE_LOG2 = ATT_SCALE * 1.4426950408889634
ATT_TQ_FWD, ATT_TQ = 2048, 2048
ATT_SUB, ATT_SUB_BWD = 256, 256


def _attn_fwd(q, k, v, name):
    s = q.shape[1]
    tq = min(ATT_TQ_FWD, s)
    sub = min(ATT_SUB, tq)

    def body(q_ref, k_ref, v_ref, o_ref, lse_ref):
        for r0 in range(0, tq, sub):
            rows = slice(r0, r0 + sub)
            sc = _dot_nt(q_ref[0, rows, :], k_ref[0])
            m = jnp.max(sc, axis=-1, keepdims=True)
            p = jnp.exp2((sc - m) * ATT_SCALE_LOG2)
            l = jnp.sum(p, axis=-1, keepdims=True)
            o_ref[rows, :] = _dot(p, v_ref[0]) / l
            lse_ref[0, rows, :] = m * ATT_SCALE_LOG2 + jnp.log2(l)

    return pl.pallas_call(
        body, grid=(MH, s // tq),
        in_specs=[pl.BlockSpec((1, tq, 2 * LANES), lambda h, i: (h, i, 0)),
                  pl.BlockSpec((1, s, 2 * LANES), lambda h, i: (h, 0, 0)),
                  pl.BlockSpec((1, s, LANES), lambda h, i: (h, 0, 0))],
        out_specs=[pl.BlockSpec((tq, LANES), lambda h, i: (i, h)),
                   pl.BlockSpec((1, tq, 1), lambda h, i: (h, i, 0))],
        out_shape=[jax.ShapeDtypeStruct((s, MW), F32), jax.ShapeDtypeStruct((MH, s, 1), F32)],
        name=name, compiler_params=_cp(("parallel", "parallel")))(q, k, v)


def _attn_bwd(q, k, v, o, lse, d_o, name):
    s = q.shape[1]
    tq = min(ATT_TQ, s)
    sub = min(ATT_SUB_BWD, tq)

    def body(q_ref, k_ref, v_ref, o_ref, lse_ref, do_ref, dq_ref, dk_ref, dv_ref):
        @pl.when(pl.program_id(1) == 0)
        def _():
            dk_ref[...] = jnp.zeros_like(dk_ref)
            dv_ref[...] = jnp.zeros_like(dv_ref)

        kv = k_ref[0]
        for r0 in range(0, tq, sub):
            rows = slice(r0, r0 + sub)
            qv, do = q_ref[0, rows, :], do_ref[rows, :]
            p = jnp.exp2(_dot_nt(qv, kv) * ATT_SCALE_LOG2 - lse_ref[0, rows, :])
            delta = jnp.sum(do * o_ref[rows, :], axis=-1, keepdims=True)
            ds = p * (_dot_nt(do, v_ref[0]) - delta)
            dq_ref[0, rows, :] = _dot(ds, kv) * ATT_SCALE
            dk_ref[0] += _dot_tn(ds, qv) * ATT_SCALE
            dv_ref[0] += _dot_tn(p, do)

    return pl.pallas_call(
        body, grid=(MH, s // tq),
        in_specs=[pl.BlockSpec((1, tq, 2 * LANES), lambda h, i: (h, i, 0)),
                  pl.BlockSpec((1, s, 2 * LANES), lambda h, i: (h, 0, 0)),
                  pl.BlockSpec((1, s, LANES), lambda h, i: (h, 0, 0)),
                  pl.BlockSpec((tq, LANES), lambda h, i: (i, h)),
                  pl.BlockSpec((1, tq, 1), lambda h, i: (h, i, 0)),
                  pl.BlockSpec((tq, LANES), lambda h, i: (i, h))],
        out_specs=[pl.BlockSpec((1, tq, 2 * LANES), lambda h, i: (h, i, 0)),
                   pl.BlockSpec((1, s, 2 * LANES), lambda h, i: (h, 0, 0)),
                   pl.BlockSpec((1, s, LANES), lambda h, i: (h, 0, 0))],
        out_shape=[jax.ShapeDtypeStruct((MH, s, 2 * LANES), F32), jax.ShapeDtypeStruct((MH, s, 2 * LANES), F32),
                   jax.ShapeDtypeStruct((MH, s, LANES), F32)],
        name=name, compiler_params=_cp(("parallel", "arbitrary")))(q, k, v, o, lse, d_o)


def _merge_fwd(o_f, o_b, o_att, pre, proj, gng, mog, cog, name):
    s = proj.shape[0]
    tr = _rows_light(s)

    def body(of_ref, ob_ref, oa_ref, pre_ref, z_ref, gng_ref, mog_ref, cog_ref, y_ref):
        z = z_ref[...]
        sz = z * _sigmoid(z)
        osum = of_ref[...] + ob_ref[...]
        gg = gng_ref[...]
        for h in range(GH):
            sl = slice(h * GDV, (h + 1) * GDV)
            y_ref[:, sl] = (_rms(osum[:, sl], gg) * sz[:, sl]).astype(y_ref.dtype)
        y_ref[:, GW:GW + MW] = (_rms(oa_ref[...], mog_ref[...]) * sz[:, GW:GW + MW]).astype(y_ref.dtype)
        y_ref[:, GW + MW:] = (_rms(pre_ref[...], cog_ref[...]) * sz[:, GW + MW:]).astype(y_ref.dtype)

    def row(w):
        return pl.BlockSpec((tr, w), lambda i: (i, 0))

    def vec(w):
        return pl.BlockSpec((1, w), lambda i: (0, 0))

    return pl.pallas_call(
        body, grid=(s // tr,),
        in_specs=[row(GW), row(GW), row(MW), row(CONV_CH), row(D_MIX), vec(GDV), vec(MW), vec(CONV_CH)],
        out_specs=row(D_MIX), out_shape=jax.ShapeDtypeStruct((s, D_MIX), MXU),
        name=name, compiler_params=_cp(("parallel",)))(o_f, o_b, o_att, pre, proj, gng, mog, cog)


def _merge_bwd(d_y, o_f, o_b, o_att, pre, proj, gng, mog, cog, name):
    s = proj.shape[0]
    tr = _rows(s)

    def body(dy_ref, of_ref, ob_ref, oa_ref, pre_ref, z_ref, gng_ref, mog_ref, cog_ref,
             dz_ref, dos_ref, doa_ref, dpre_ref, dgng_ref, dmog_ref, dcog_ref):
        @pl.when(pl.program_id(0) == 0)
        def _():
            for r in (dgng_ref, dmog_ref, dcog_ref):
                r[...] = jnp.zeros_like(r)

        z, dy = z_ref[...], dy_ref[...]
        sg = _sigmoid(z)
        sz = z * sg
        dsz = sg * (1.0 + z * (1.0 - sg))
        dcat = dy * sz
        dyz = dy * dsz
        osum = of_ref[...] + ob_ref[...]
        gg = gng_ref[...]
        dgg = jnp.zeros_like(gg)
        for h in range(GH):
            sl = slice(h * GDV, (h + 1) * GDV)
            dz_ref[:, sl] = (dyz[:, sl] * _rms(osum[:, sl], gg)).astype(dz_ref.dtype)
            dx, dg = _rms_bwd(dcat[:, sl], osum[:, sl], gg)
            dos_ref[:, sl] = dx
            dgg += dg
        dgng_ref[...] += dgg
        sl = slice(GW, GW + MW)
        oa, mg = oa_ref[...], mog_ref[...]
        dz_ref[:, sl] = (dyz[:, sl] * _rms(oa, mg)).astype(dz_ref.dtype)
        dx, dg = _rms_bwd(dcat[:, sl], oa, mg)
        doa_ref[...] = dx
        dmog_ref[...] += dg
        sl = slice(GW + MW, D_MIX)
        pv, cg = pre_ref[...], cog_ref[...]
        dz_ref[:, sl] = (dyz[:, sl] * _rms(pv, cg)).astype(dz_ref.dtype)
        dx, dg = _rms_bwd(dcat[:, sl], pv, cg)
        dpre_ref[...] = dx
        dcog_ref[...] += dg

    def row(w):
        return pl.BlockSpec((tr, w), lambda i: (i, 0))

    def vec(w):
        return pl.BlockSpec((1, w), lambda i: (0, 0))

    def rs(w):
        return jax.ShapeDtypeStruct((s, w), F32)

    def vs(w):
        return jax.ShapeDtypeStruct((1, w), F32)

    return pl.pallas_call(
        body, grid=(s // tr,),
        in_specs=[row(D_MIX), row(GW), row(GW), row(MW), row(CONV_CH), row(D_MIX), vec(GDV), vec(MW), vec(CONV_CH)],
        out_specs=[row(D_MIX), row(GW), row(MW), row(CONV_CH), vec(GDV), vec(MW), vec(CONV_CH)],
        out_shape=[jax.ShapeDtypeStruct((s, D_MIX), MXU),
                   rs(GW), rs(MW), rs(CONV_CH), vs(GDV), vs(MW), vs(CONV_CH)],
        name=name, compiler_params=_cp(("arbitrary",)))(d_y, o_f, o_b, o_att, pre, proj, gng, mog, cog)


def _assemble_dproj(d_z, d_cb, d_cc, d_cx, d_mkv, dv_f, dv_b, dq_f, dq_b, dk_f, dk_b, d_mq, dt_m, dt_f, dt_b, name):
    s = d_z.shape[0]
    tr = _rows_light(s)

    def body(dz, dcb, dcc, dcx, dmkv, dvf, dvb, dqf, dqb, dkf, dkb, dmq, dtm, dtf, dtb, out):
        dt = out.dtype
        out[:, OZ:OZ + D_MIX] = dz[...].astype(dt)
        out[:, OCB:OCB + CONV_CH] = dcb[...].astype(dt)
        out[:, OCC:OCC + CONV_CH] = dcc[...].astype(dt)
        out[:, OCX:OCX + CONV_CH] = dcx[...].astype(dt)
        out[:, OMKV:OMKV + MKVL] = dmkv[...].astype(dt)
        out[:, OGV:OGV + GW] = (dvf[...] + dvb[...]).astype(dt)
        out[:, OGQ:OGQ + GQK] = (dqf[...] + dqb[...]).astype(dt)
        out[:, OGK:OGK + GQK] = (dkf[...] + dkb[...]).astype(dt)
        out[:, OMQ:OMQ + MQL] = dmq[...].astype(dt)
        out[:, OT:OT + LANES] = (dtm[...] + dtf[...] + dtb[...]).astype(dt)

    args = (d_z, d_cb, d_cc, d_cx, d_mkv, dv_f, dv_b, dq_f, dq_b, dk_f, dk_b, d_mq, dt_m, dt_f, dt_b)
    return pl.pallas_call(
        body, grid=(s // tr,),
        in_specs=[pl.BlockSpec((tr, a.shape[1]), lambda i: (i, 0)) for a in args],
        out_specs=pl.BlockSpec((tr, PW), lambda i: (i, 0)),
        out_shape=jax.ShapeDtypeStruct((s, PW), MXU), name=name, compiler_params=_cp(("parallel",)))(*args)


def _layer_fwd(x, mod, wt, cos, sin, tag, late=None, in_after=(), h=None):
    shift, scale, gate = mod
    if h is None:
        h = _norm_mod(x, wt["norm_g"], scale, shift, f"norm_mod_{tag}")
    (proj,) = _matmul(h, wt["w_in"], dims="nn", tm=2048, tn=256, tk=2048, out_dtypes=(F32,), name=f"in_proj_{tag}",
                      after=in_after)
    if late is not None:
        wt.update(late(proj))
    o_f, st_f = _gla_fwd(proj, wt["wg_pad_f"], wt["bg_f"], False, f"gla_fwd_f_{tag}")
    o_b, st_b = _gla_fwd(proj, wt["wg_pad_b"], wt["bg_b"], True, f"gla_fwd_b_{tag}")
    q, k, v = _mla_prep(proj, cos, sin, wt["q_norm_g"], wt["kv_norm_g"], wt["w_uq"], wt["w_ukv"], f"mla_prep_{tag}")
    o_att, lse = _attn_fwd(q, k, v, f"attn_fwd_{tag}")
    pre = _conv_fwd(proj, wt["conv_w"], f"conv_fwd_{tag}")
    y = _merge_fwd(o_f, o_b, o_att, pre, proj, wt["gla_norm_g"], wt["mla_out_g"], wt["conv_out_g"], f"merge_fwd_{tag}")
    x_new, u = _matmul(y, wt["w_out"], dims="nn", tm=2048, tn=256, tk=2048, out_dtypes=(F32, F32),
                       name=f"out_proj_{tag}", epilogue=lambda acc, xv, gv: (xv + gv * acc, acc),
                       extras=(x, gate), extra_kinds=("mn", "n"))
    saved = dict(x=x, h=h, proj=proj, o_f=o_f, o_b=o_b, st_f=st_f, st_b=st_b, q=q, k=k, v=v,
                 o_att=o_att, lse=lse, pre=pre, y=y, u=u)
    return x_new, saved


def _layer_bwd(d_out, sv, mod, wt, cos, sin, tag, ship=None, dx_first=None, ship_rest=None):
    shift, scale, gate = mod
    proj = sv["proj"]
    d_u, d_gate = _gate_bwd(d_out, sv["u"], gate, f"gate_bwd_{tag}")
    (g_w_out,) = _matmul(sv["y"], d_u, dims="tn", tm=1024, tn=512, tk=2048, out_dtypes=(MXU,), name=f"out_proj_dw_{tag}")
    (d_y,) = _matmul(d_u, wt["w_out"], dims="nt", tm=2048, tn=256, tk=2048, out_dtypes=(F32,), name=f"out_proj_dx_{tag}",
                     after=(g_w_out,))
    d_z, d_osum, d_oatt, d_pre, d_gng, d_mog, d_cog = _merge_bwd(
        d_y, sv["o_f"], sv["o_b"], sv["o_att"], sv["pre"], proj, wt["gla_norm_g"], wt["mla_out_g"], wt["conv_out_g"],
        f"merge_bwd_{tag}")
    d_cb, d_cc, d_cx, d_conv_w = _conv_bwd(proj, wt["conv_w"], d_pre, f"conv_bwd_{tag}")
    d_q, d_k, d_v = _attn_bwd(sv["q"], sv["k"], sv["v"], sv["o_att"], sv["lse"], d_oatt, f"attn_bwd_{tag}")
    d_mq, d_mkv, dt_m, g_w_uq, g_w_ukv, d_qg, d_kvg = _mla_prep_bwd(
        proj, cos, sin, wt["q_norm_g"], wt["kv_norm_g"], wt["w_uq"], wt["w_ukv"], d_q, d_k, d_v, f"mla_prep_bwd_{tag}")
    bg_f, bg_b = wt["bg_f"], wt["bg_b"]
    if ship_rest is not None:
        tok = ship_rest(dict(w_out=g_w_out, w_uq=g_w_uq, w_ukv=g_w_ukv))
        bg_f, bg_b = bg_f + tok, bg_b + tok
    dq_f, dk_f, dv_f, dt_f, d_wg_f, d_bg_f = _gla_bwd(proj, wt["wg_pad_f"], bg_f, sv["st_f"], d_osum, False,
                                                     f"gla_bwd_f_{tag}")
    dq_b, dk_b, dv_b, dt_b, d_wg_b, d_bg_b = _gla_bwd(proj, wt["wg_pad_b"], bg_b, sv["st_b"], d_osum, True,
                                                     f"gla_bwd_b_{tag}")
    d_proj = _assemble_dproj(d_z, d_cb, d_cc, d_cx, d_mkv, dv_f, dv_b, dq_f, dq_b, dk_f, dk_b, d_mq, dt_m, dt_f, dt_b,
                             f"assemble_dproj_{tag}")
    grads = dict(w_out=g_w_out, w_uq=g_w_uq, w_ukv=g_w_ukv,
                 wg_pad_f=d_wg_f, bg_f=d_bg_f, wg_pad_b=d_wg_b, bg_b=d_bg_b, gla_norm_g=d_gng,
                 q_norm_g=d_qg, kv_norm_g=d_kvg, mla_out_g=d_mog, conv_w=d_conv_w, conv_out_g=d_cog)

    def in_dw(after):
        (g_w_in,) = _matmul(sv["h"], d_proj, dims="tn", tm=2048, tn=256, tk=2048, out_dtypes=(MXU,),
                            name=f"in_proj_dw_{tag}", after=after)
        grads["w_in"] = g_w_in
        return dict(w_in=g_w_in, w_out=g_w_out, w_uq=g_w_uq, w_ukv=g_w_ukv)

    def in_dx(after):
        (d_h,) = _matmul(d_proj, wt["w_in"], dims="nt", tm=1024, tn=512, tk=PW, out_dtypes=(F32,),
                         name=f"in_proj_dx_{tag}", after=after)
        d_x, d_shift, d_scale, d_ng = _norm_mod_bwd(d_h, sv["x"], d_out, wt["norm_g"], scale, f"norm_mod_bwd_{tag}")
        grads["norm_g"] = d_ng
        return d_x, (d_shift, d_scale, d_gate)

    if dx_first is None:
        big = in_dw(())
        d_x, d_mod = in_dx((big["w_in"],) if ship is None else ship(big))
    else:
        d_x, d_mod = in_dx(())
        big = in_dw(dx_first(d_x, d_mod, grads))
        ship(big)
    return d_x, d_mod, grads


IN_SEGS = ((3808, 5856), (2272, 3808), (1952, 2208), (768, 1536), (0, 768), (1568, 1952), (2208, 2272), (1536, 1568))
UQ_SEGS = (tuple((h * (MNOPE + MROPE), h * (MNOPE + MROPE) + MNOPE) for h in range(MH))
           + tuple((h * (MNOPE + MROPE) + MNOPE, (h + 1) * (MNOPE + MROPE)) for h in range(MH)))


def _heads_apart(w):
    w3 = w.reshape(w.shape[:-1] + (MH, MNOPE + MROPE))
    return jnp.concatenate([w3[..., :MNOPE].reshape(w.shape[:-1] + (MH * MNOPE,)),
                            w3[..., MNOPE:].reshape(w.shape[:-1] + (MH * MROPE,))], axis=-1)


def _heads_together(g):
    nope = g[..., :MH * MNOPE].reshape(g.shape[:-1] + (MH, MNOPE))
    rope = g[..., MH * MNOPE:].reshape(g.shape[:-1] + (MH, MROPE))
    return jnp.concatenate([nope, rope], axis=-1).reshape(g.shape[:-1] + (MQW,))


def _perm_gathered(g, segs, width):
    per = g.shape[-1]
    parts, total = [], 0
    for a, b in segs:
        c = a
        while c < b:
            j = c // per
            hi = min(b, (j + 1) * per)
            parts.append(g[j, :, c - j * per:hi - j * per])
            c = hi
        total += b - a
    if width > total:
        parts.append(jnp.zeros((g.shape[1], width - total), g.dtype))
    return jnp.concatenate(parts, axis=1)


def _scatter_perm(gp, segs, per):
    offs, o = [], 0
    for a, b in segs:
        offs.append((a, b, o))
        o += b - a
    blocks = []
    for j in range(N_DEV):
        lo, hi = j * per, (j + 1) * per
        pieces = []
        for a, b, o in sorted(offs):
            s0, s1 = max(a, lo), min(b, hi)
            if s0 < s1:
                pieces.append(gp[:, o + s0 - a:o + s1 - a])
        blocks.append(jnp.concatenate(pieces, axis=1))
    return jnp.stack(blocks)


def _prep_layer_weights(w_in, w_out, w_uq, w_ukv, small):
    def vec(v):
        return v.reshape(1, -1).astype(F32)

    zeros = functools.partial(jnp.zeros, dtype=F32)
    wg_f, wg_b = small["gla_wg_f"].astype(F32), small["gla_wg_b"].astype(F32)
    wg_pad_f = jnp.concatenate([zeros((MROPE, GQK)), wg_f, zeros((LANES - MROPE - GRANK, GQK))], axis=0)
    wg_pad_b = jnp.concatenate([zeros((MROPE + GRANK, GQK)), wg_b, zeros((LANES - MROPE - 2 * GRANK, GQK))], axis=0)
    wt = dict(norm_g=vec(small["norm_g"]), wg_pad_f=wg_pad_f, wg_pad_b=wg_pad_b,
              bg_f=vec(small["gla_bg_f"]), bg_b=vec(small["gla_bg_b"]), gla_norm_g=vec(small["gla_norm_g"]),
              q_norm_g=vec(small["mla_q_norm_g"]), kv_norm_g=vec(small["mla_kv_norm_g"]),
              mla_out_g=vec(small["mla_out_g"]), conv_w=small["conv_w"].astype(F32),
              conv_out_g=vec(small["conv_out_g"]))
    for name, w in (("w_in", w_in), ("w_out", w_out), ("w_uq", w_uq), ("w_ukv", w_ukv)):
        if w is not None:
            wt[name] = w.astype(MXU)
    return wt


def _natural_small(gr):
    return dict(norm_g=gr["norm_g"][0],
                gla_wg_f=gr["wg_pad_f"][MROPE:MROPE + GRANK], gla_bg_f=gr["bg_f"][0],
                gla_wg_b=gr["wg_pad_b"][MROPE + GRANK:MROPE + 2 * GRANK], gla_bg_b=gr["bg_b"][0],
                gla_norm_g=gr["gla_norm_g"][0], mla_q_norm_g=gr["q_norm_g"][0], mla_kv_norm_g=gr["kv_norm_g"][0],
                mla_out_g=gr["mla_out_g"][0], conv_w=gr["conv_w"], conv_out_g=gr["conv_out_g"][0])


def _exchange(arrs, name, scatter, space):
    n = len(arrs)

    def body(*refs):
        ins, outs = refs[:n], refs[n:2 * n]
        send_sems, recv_sems, loc_sems = refs[2 * n:]
        ax, ay, ac = lax.axis_index("x"), lax.axis_index("y"), lax.axis_index("c")
        me = 4 * ax + 2 * ay + ac

        def src(a, to):
            return ins[a].at[to] if scatter else ins[a]

        def remote(a, r, dst_slot):
            px = 1 - ax if r & 4 else ax
            py = 1 - ay if r & 2 else ay
            pc = 1 - ac if r & 1 else ac
            return pltpu.make_async_remote_copy(
                src_ref=src(a, 4 * px + 2 * py + pc), dst_ref=outs[a].at[dst_slot(4 * px + 2 * py + pc)],
                send_sem=send_sems.at[a, r - 1], recv_sem=recv_sems.at[a, r - 1],
                device_id=(px, py, pc), device_id_type=MESH)

        locs = [pltpu.make_async_copy(src(a, me), outs[a].at[me], loc_sems.at[a]) for a in range(n)]
        for cp in locs:
            cp.start()
        sends = [remote(a, r, lambda peer: me) for r in range(1, N_DEV) for a in range(n)]
        for cp in sends:
            cp.start()
        for r in range(1, N_DEV):
            for a in range(n):
                remote(a, r, lambda peer: peer).wait_recv()
        for cp in sends:
            cp.wait_send()
        for cp in locs:
            cp.wait()

    def out_shape(a):
        return jax.ShapeDtypeStruct(a.shape if scatter else (N_DEV,) + a.shape, a.dtype)

    spec = pl.BlockSpec(memory_space=space)
    return pl.pallas_call(
        body, in_specs=[spec] * n, out_specs=[spec] * n, out_shape=[out_shape(a) for a in arrs],
        scratch_shapes=[pltpu.SemaphoreType.DMA((n, N_DEV - 1)), pltpu.SemaphoreType.DMA((n, N_DEV - 1)),
                        pltpu.SemaphoreType.DMA((n,))],
        name=name, compiler_params=pltpu.CompilerParams(vmem_limit_bytes=VMEM_LIMIT))(*arrs)


def _peer(r):
    ax, ay, ac = lax.axis_index("x"), lax.axis_index("y"), lax.axis_index("c")
    px = 1 - ax if r & 4 else ax
    py = 1 - ay if r & 2 else ay
    pc = 1 - ac if r & 1 else ac
    return (px, py, pc), 4 * px + 2 * py + pc


def _slot(rel_div):
    rel, div = rel_div
    idx = _peer(rel)[1]
    return idx if div == 1 else idx // div


AG_SPREAD = tuple((r, None, (0, 1), (r, 1)) for r in (1, 2, 4, 6))
AG_FORWARD = tuple((1, (k, 1), (k, 1), (1 ^ k, 1)) for k in (2, 4, 6))
RS_PAIR = tuple((1, (1 ^ k, 1), (1 ^ k, 2), (k, 2)) for k in (0, 2, 4, 6))
RS_CHIPS = tuple((r, (r, 2), (0, 2), (r, 2)) for r in (2, 4, 6))


def _plan_copies(plan, n, src_refs, land_refs, send_sems, recv_sems, arriving):
    out = []
    for i, (r, src, dst, recv) in enumerate(plan):
        peer = _peer(r)[0]
        for a in range(n):
            out.append(pltpu.make_async_remote_copy(
                src_ref=src_refs[a] if src is None else src_refs[a].at[_slot(src)],
                dst_ref=land_refs[a].at[_slot(recv if arriving else dst)],
                send_sem=send_sems.at[i * n + a], recv_sem=recv_sems.at[i * n + a],
                device_id=peer, device_id_type=MESH))
    return out


def _exchange_hbm(plan, srcs, lands, name, after=()):
    n = len(lands)
    fresh = isinstance(lands[0], jax.ShapeDtypeStruct)
    ins = ([] if srcs is None else list(srcs)) + ([] if fresh else list(lands))
    ns = 0 if srcs is None else n
    n_data = len(ins)
    ins = ins + list(after)

    def body(*refs):
        outs = refs[len(ins):len(ins) + n]
        send_sems, recv_sems = refs[-2:]
        src_refs = refs[:n] if srcs is not None else refs[ns:ns + n]
        sends = _plan_copies(plan, n, src_refs, outs, send_sems, recv_sems, False)
        for cp in sends:
            cp.start()
        for cp in _plan_copies(plan, n, src_refs, outs, send_sems, recv_sems, True):
            cp.wait_recv()
        for cp in sends:
            cp.wait_send()

    hbm = pl.BlockSpec(memory_space=pltpu.HBM)
    k = len(plan) * n
    return pl.pallas_call(
        body, name=name, in_specs=[hbm] * n_data + [pl.BlockSpec(memory_space=pl.ANY)] * len(after), out_specs=[hbm] * n,
        out_shape=[jax.ShapeDtypeStruct(a.shape, a.dtype) for a in lands],
        scratch_shapes=[pltpu.SemaphoreType.DMA((k,)), pltpu.SemaphoreType.DMA((k,))],
        input_output_aliases={} if fresh else {ns + i: i for i in range(n)},
        compiler_params=pltpu.CompilerParams(vmem_limit_bytes=VMEM_LIMIT))(*ins)


def _plan_start(plan, srcs, land_shapes, after, name):
    n = len(srcs)

    def body(*refs):
        src_refs, land_refs = refs[:n], refs[n:2 * n]
        send_sems, recv_sems = refs[2 * n + 1], refs[2 * n + 2]
        for cp in _plan_copies(plan, n, src_refs, land_refs, send_sems, recv_sems, False):
            cp.start()
        refs[-1][...] = jnp.zeros_like(refs[-1])

    hbm = pl.BlockSpec(memory_space=pltpu.HBM)
    sem = pl.BlockSpec(memory_space=pltpu.SEMAPHORE)
    k = len(plan) * n
    srcs = [pltpu.with_memory_space_constraint(a, pltpu.HBM) for a in srcs]
    lands = [pltpu.with_memory_space_constraint(lax.empty(shp, a.dtype), pltpu.HBM) for shp, a in zip(land_shapes, srcs)]
    res = pl.pallas_call(
        body, name=name,
        in_specs=[hbm] * (2 * n) + [pl.BlockSpec(memory_space=pl.ANY)],
        out_specs=[sem, sem] + [hbm] * (2 * n) + [pl.BlockSpec(memory_space=pltpu.VMEM)],
        out_shape=[pltpu.SemaphoreType.DMA((k,)), pltpu.SemaphoreType.DMA((k,))]
        + [pltpu.HBM(a.shape, a.dtype) for a in srcs] + [pltpu.HBM(shp, a.dtype) for shp, a in zip(land_shapes, srcs)]
        + [jax.ShapeDtypeStruct((8, LANES), F32)],
        input_output_aliases={i: 2 + i for i in range(2 * n)},
        compiler_params=pltpu.CompilerParams(has_side_effects=pltpu.SideEffectType.DATAFLOW_SIDE_EFFECTING),
    )(*srcs, *lands, after)
    return res[0], res[1], list(res[2:2 + n]), list(res[2 + n:2 + 2 * n]), res[-1]


def _plan_wait(plan, handle, after, name):
    send_sems, recv_sems, srcs, lands, _ = handle
    n = len(srcs)
    after = list(after)

    def body(*refs):
        src_refs, land_refs = refs[:n], refs[n:2 * n]
        ssem, rsem = refs[2 * n], refs[2 * n + 1]
        for cp in _plan_copies(plan, n, src_refs, land_refs, ssem, rsem, False):
            cp.wait_send()
        for cp in _plan_copies(plan, n, src_refs, land_refs, ssem, rsem, True):
            cp.wait_recv()

    hbm = pl.BlockSpec(memory_space=pltpu.HBM)
    sem = pl.BlockSpec(memory_space=pltpu.SEMAPHORE)
    res = pl.pallas_call(
        body, name=name,
        in_specs=[hbm] * (2 * n) + [sem, sem] + [pl.BlockSpec(memory_space=pl.ANY)] * len(after),
        out_specs=[hbm] * (2 * n),
        out_shape=[pltpu.HBM(a.shape, a.dtype) for a in srcs] + [pltpu.HBM(a.shape, a.dtype) for a in lands],
        input_output_aliases={i: i for i in range(2 * n)},
        compiler_params=pltpu.CompilerParams(has_side_effects=pltpu.SideEffectType.DATAFLOW_SIDE_EFFECTING),
    )(*srcs, *lands, send_sems, recv_sems, *after)
    return list(res[:n]), list(res[n:])


def _pair_sum(send, got, core, name):
    _, r, c = send.shape
    tr = 1024 if r % 1024 == 0 else r

    def body(core_ref, s_ref, g_ref, o_ref):
        o_ref[0] = (s_ref[0].astype(F32) + g_ref[0].astype(F32)).astype(o_ref.dtype)

    return pl.pallas_call(
        body, name=name,
        grid_spec=pltpu.PrefetchScalarGridSpec(
            num_scalar_prefetch=1, grid=(N_DEV // 2, r // tr),
            in_specs=[pl.BlockSpec((1, tr, c), lambda kc, i, core_ref: (2 * kc + core_ref[0], i, 0)),
                      pl.BlockSpec((1, tr, c), lambda kc, i, core_ref: (kc, i, 0))],
            out_specs=pl.BlockSpec((1, tr, c), lambda kc, i, core_ref: (kc, i, 0))),
        out_shape=jax.ShapeDtypeStruct((N_DEV // 2, r, c), send.dtype),
        compiler_params=_cp(("parallel", "parallel")))(core, send, got)


def _ada_mod(c_all, ada_w, ada_b_cols, name):
    nl, d, wc = ada_w.shape

    def body(c_ref, w_ref, b_ref, ca_ref, mod_ref):
        cv = c_ref[...]
        ca = cv * _sigmoid(cv)
        ca_ref[...] = ca
        mod_ref[0] = _dotf(ca, w_ref[0]) + b_ref[0]

    return pl.pallas_call(
        body, grid=(nl,),
        in_specs=[pl.BlockSpec((N_DEV, d), lambda l: (0, 0)), pl.BlockSpec((1, d, wc), lambda l: (l, 0, 0)),
                  pl.BlockSpec((1, 1, wc), lambda l: (l, 0, 0))],
        out_specs=[pl.BlockSpec((N_DEV, d), lambda l: (0, 0)), pl.BlockSpec((1, N_DEV, wc), lambda l: (l, 0, 0))],
        out_shape=[jax.ShapeDtypeStruct((N_DEV, d), F32), jax.ShapeDtypeStruct((nl, N_DEV, wc), F32)],
        name=name, compiler_params=_cp(("arbitrary",)))(c_all, ada_w, ada_b_cols)


def _adam(w, g, m, v):
    m2 = ADAM_B1 * m + (1.0 - ADAM_B1) * g
    v2 = ADAM_B2 * v + (1.0 - ADAM_B2) * (g * g)
    m_hat = m2 / (1.0 - ADAM_B1 ** ADAM_STEP)
    v_hat = v2 / (1.0 - ADAM_B2 ** ADAM_STEP)
    delta = -ADAM_LR * (m_hat / (jnp.sqrt(v_hat) + ADAM_EPS) + ADAM_WD * w)
    return delta, m2, v2


def _ada_grad_adam(c_act, d_mod, w, m, v, name):
    nl, d, wc = w.shape
    tk = min(1024, d)

    def body(c_ref, dm_ref, w_ref, m_ref, v_ref, g_ref, dl_ref, m2_ref, v2_ref):
        g = _dotf_tn(c_ref[...], dm_ref[0])
        delta, m2, v2 = _adam(w_ref[0], g, m_ref[0], v_ref[0])
        g_ref[0], dl_ref[0], m2_ref[0], v2_ref[0] = g, delta, m2, v2

    blk = pl.BlockSpec((1, tk, wc), lambda l, i: (l, i, 0))
    shp = jax.ShapeDtypeStruct(w.shape, F32)
    return pl.pallas_call(
        body, grid=(nl, d // tk),
        in_specs=[pl.BlockSpec((N_DEV, tk), lambda l, i: (0, i)), pl.BlockSpec((1, N_DEV, wc), lambda l, i: (l, 0, 0)),
                  blk, blk, blk],
        out_specs=[blk] * 4, out_shape=[shp] * 4, name=name,
        compiler_params=_cp(("parallel", "parallel")))(c_act, d_mod, w, m, v)


def _adam_big(recv, w, m, v, layer, prev, name, after=()):
    nl, r, c = w.shape
    tr = 512 if r % 512 == 0 else r
    nparts = recv.shape[0]

    def body(rc_ref, w_ref, m_ref, v_ref, *rest):
        g_ref, dl_ref, m2_ref, v2_ref = rest[-4:]
        g = rc_ref[0].astype(F32)
        for d in range(1, nparts):
            g = g + rc_ref[d].astype(F32)
        delta, m2, v2 = _adam(w_ref[0], g, m_ref[0], v_ref[0])
        g_ref[0], dl_ref[0], m2_ref[0], v2_ref[0] = g, delta, m2, v2

    blk = pl.BlockSpec((1, tr, c), lambda i: (layer, i, 0))
    shp = jax.ShapeDtypeStruct(w.shape, F32)
    prev = () if prev is None else tuple(prev)
    return pl.pallas_call(
        body, grid=(r // tr,),
        in_specs=[pl.BlockSpec((nparts, tr, c), lambda i: (0, i, 0)), blk, blk, blk]
        + [pl.BlockSpec(memory_space=pl.ANY)] * (len(prev) + len(after)),
        out_specs=[blk] * 4, out_shape=[shp] * 4, name=name,
        input_output_aliases={4 + j: j for j in range(len(prev))},
        compiler_params=_cp(("parallel",)))(recv, w, m, v, *prev, *after)


def _sum_devices(gathered, name):
    _, r, c = gathered.shape

    def body(g_ref, o_ref):
        acc = g_ref[0]
        for d in range(1, N_DEV):
            acc = acc + g_ref[d]
        o_ref[...] = acc

    spec = pl.BlockSpec(memory_space=pltpu.VMEM)
    return pl.pallas_call(body, in_specs=[spec], out_specs=spec, out_shape=jax.ShapeDtypeStruct((r, c), F32),
                          name=name, compiler_params=pltpu.CompilerParams(vmem_limit_bytes=VMEM_LIMIT))(gathered)


def _adam_small(ws, gs, ms, vs, name):
    n = len(ws)

    def body(*refs):
        for i in range(n):
            w_ref, g_ref, m_ref, v_ref = (refs[k * n + i] for k in range(4))
            dl_ref, m2_ref, v2_ref = (refs[(4 + k) * n + i] for k in range(3))
            dl_ref[...], m2_ref[...], v2_ref[...] = _adam(w_ref[...], g_ref[...], m_ref[...], v_ref[...])

    spec = pl.BlockSpec(memory_space=pltpu.VMEM)
    shapes = [jax.ShapeDtypeStruct(w.shape, F32) for w in ws]
    res = pl.pallas_call(body, in_specs=[spec] * (4 * n), out_specs=[spec] * (3 * n), out_shape=shapes * 3, name=name,
                         compiler_params=pltpu.CompilerParams(vmem_limit_bytes=VMEM_LIMIT))(*ws, *gs, *ms, *vs)
    return res[:n], res[n:2 * n], res[2 * n:]


def _pack(parts):
    flat = jnp.concatenate([p.reshape(-1).astype(F32) for p in parts])
    assert flat.shape[0] % LANES == 0, flat.shape
    return flat.reshape(-1, LANES)


def _unpack(packed, shapes):
    flat = packed.reshape(-1)
    out, off = [], 0
    for shp in shapes:
        size = 1
        for dim in shp:
            size *= dim
        out.append(flat[off:off + size].reshape(shp))
        off += size
    return out


def _gather_cols(g, per):
    g = jnp.moveaxis(g, 0, -2)
    return g.reshape(g.shape[:-2] + (N_DEV * per,))


def _scatter_cols(g, per):
    return jnp.moveaxis(g.reshape(g.shape[:-1] + (N_DEV, per)), -2, 0)


def _my_cols(full, me, per):
    return lax.dynamic_slice_in_dim(full, me * per, per, axis=full.ndim - 1)


def kernel(x, c, positions, ada_w, ada_b, norm_g, w_in, gla_wg_f, gla_bg_f, gla_wg_b, gla_bg_b, gla_norm_g, mla_q_norm_g, mla_kv_norm_g, mla_w_uq, mla_w_ukv, mla_out_g, conv_w, conv_out_g, w_out, final_g, loss_target, m_ada_w, m_ada_b, m_norm_g, m_w_in, m_gla_wg_f, m_gla_bg_f, m_gla_wg_b, m_gla_bg_b, m_gla_norm_g, m_mla_q_norm_g, m_mla_kv_norm_g, m_mla_w_uq, m_mla_w_ukv, m_mla_out_g, m_conv_w, m_conv_out_g, m_w_out, m_final_g, v_ada_w, v_ada_b, v_norm_g, v_w_in, v_gla_wg_f, v_gla_bg_f, v_gla_wg_b, v_gla_bg_b, v_gla_norm_g, v_mla_q_norm_g, v_mla_kv_norm_g, v_mla_w_uq, v_mla_w_ukv, v_mla_out_g, v_conv_w, v_conv_out_g, v_w_out, v_final_g):
    me = 4 * lax.axis_index("x") + 2 * lax.axis_index("y") + lax.axis_index("c")
    nl = ada_w.shape[0]
    s, d = x.shape[1], x.shape[2]
    ada_cols = ada_w.shape[2]
    wgc, cwc = gla_wg_f.shape[2], conv_w.shape[2]

    (g0,) = _exchange([_pack([c, gla_wg_f, gla_wg_b, conv_w])], "gather_small_in", False, pltpu.VMEM)
    g0 = g0.reshape(N_DEV, -1)
    o1, o2, o3 = d, d + gla_wg_f.size, d + 2 * gla_wg_f.size
    c_all = g0[:, :o1]
    wgf_full = _gather_cols(g0[:, o1:o2].reshape((N_DEV,) + gla_wg_f.shape), wgc)
    wgb_full = _gather_cols(g0[:, o2:o3].reshape((N_DEV,) + gla_wg_b.shape), wgc)
    convw_full = _gather_cols(g0[:, o3:].reshape((N_DEV,) + conv_w.shape), cwc)

    ada_b_cols = _my_cols(ada_b, me, ada_cols).reshape(nl, 1, ada_cols)
    c_act, mod_cols = _ada_mod(c_all, ada_w, ada_b_cols, "ada_mod")
    (g1,) = _exchange([_pack([mod_cols])], "gather_mod", False, pltpu.VMEM)
    mod_all = g1.reshape(N_DEV, nl, N_DEV, ada_cols)
    mod_mine = _gather_cols(lax.dynamic_index_in_dim(mod_all, me, axis=2, keepdims=False), ada_cols)

    inv_freq = ROPE_THETA ** (-jnp.arange(0, MROPE, 2, dtype=F32) / MROPE)
    ang = positions[0].astype(F32)[:, None] * inv_freq
    cos, sin = jnp.tile(jnp.cos(ang), (1, LANES * 2 // MROPE)), jnp.tile(jnp.sin(ang), (1, LANES * 2 // MROPE))

    big = [w_in, w_out, mla_w_uq, mla_w_ukv]
    big_names = ["w_in", "w_out", "mla_w_uq", "mla_w_ukv"]

    def local_blocks(l):
        return [w[l].astype(MXU) for w in big]

    def put_own(lands, own):
        return [lax.dynamic_update_index_in_dim(ld, o, me, 0) for ld, o in zip(lands, own)]

    def layer_weights(l, gw_in=None, gw_out=None, gw_uq=None, gw_ukv=None):
        small = dict(norm_g=norm_g[l], gla_wg_f=wgf_full[l], gla_bg_f=gla_bg_f[l], gla_wg_b=wgb_full[l],
                     gla_bg_b=gla_bg_b[l], gla_norm_g=gla_norm_g[l], mla_q_norm_g=mla_q_norm_g[l],
                     mla_kv_norm_g=mla_kv_norm_g[l], mla_out_g=mla_out_g[l], conv_w=convw_full[l],
                     conv_out_g=conv_out_g[l])
        return _prep_layer_weights(
            None if gw_in is None else _perm_gathered(gw_in, IN_SEGS, PW),
            None if gw_out is None else gw_out.reshape((-1,) + gw_out.shape[2:]),
            None if gw_uq is None else _heads_apart(_gather_cols(gw_uq, mla_w_uq.shape[2])),
            None if gw_ukv is None else _gather_cols(gw_ukv, mla_w_ukv.shape[2]), small)

    def land_shapes(blocks, slots):
        return [jax.ShapeDtypeStruct((slots,) + b.shape, b.dtype) for b in blocks]

    def slots_of(blocks):
        return [(N_DEV,) + b.shape for b in blocks]

    def forwarded(lands, blocks, tag):
        return put_own(_exchange_hbm(AG_FORWARD, None, lands, f"gather_{tag}_forward"), blocks)

    first = local_blocks(0)
    w_in_start = _plan_start(AG_SPREAD, first[:1], slots_of(first[:1]), mod_mine, "gather_w_in_l0_start")
    adam_w_in = [a + w_in_start[-1][0, 0] for a in (w_in, m_w_in, v_w_in)]
    shift0, scale0 = (mod_mine[0, i * d:(i + 1) * d].reshape(1, d) for i in range(2))
    h_first = _norm_mod(x[0], norm_g[0].reshape(1, d), scale0 + w_in_start[-1][0, 0], shift0, "norm_mod_l0")
    (gw_in,) = forwarded(*reversed(_plan_wait(AG_SPREAD, w_in_start, adam_w_in + [h_first], "gather_w_in_l0_wait")),
                         "w_in_l0")
    rest = _plan_start(AG_SPREAD, first[1:], slots_of(first[1:]), gw_in, "gather_rest_l0_start")
    h = x[0]
    saved, layers, mods = [], [], []
    pending = {}
    for l in range(nl):
        shift, scale, gate = (mod_mine[l, i * d:(i + 1) * d].reshape(1, d) for i in range(3))
        nxt = local_blocks(l + 1) if l + 1 < nl else None

        def start_next(after, wt_late, l=l, nxt=nxt):
            if nxt is not None:
                pending[l + 1] = _plan_start(AG_SPREAD, nxt, slots_of(nxt), after, f"gather_weights_l{l + 1}_start")
                wt_late["q_norm_g"] = layers[l]["q_norm_g"] + pending[l + 1][-1][0, 0]
            return wt_late

        if l == 0:
            in_after = (rest[-1],)
            layers.append(layer_weights(0, gw_in))

            def late(proj):
                got = forwarded(*reversed(_plan_wait(AG_SPREAD, rest, [proj], "gather_rest_l0_wait")), "rest_l0")
                full = layer_weights(0, None, *got)
                return start_next(got[0], {k: full[k] for k in ("w_out", "w_uq", "w_ukv")})
        else:
            got = forwarded(*reversed(_plan_wait(AG_SPREAD, pending.pop(l), [h], f"gather_weights_l{l}_wait")), f"weights_l{l}")
            layers.append(layer_weights(l, *got))
            in_after = ()

            def late(proj):
                return start_next(proj, {})
        mods.append((shift, scale, gate))
        h, sv = _layer_fwd(h, mods[l], layers[l], cos, sin, f"l{l}", late, in_after, h_first if l == 0 else None)
        saved.append(sv)
        blocks = nxt
    loss_part, d_h, d_final_g = _final_loss(h, final_g.reshape(1, d), loss_target[0], "final_loss")
    loss = lax.psum(loss_part[0, 0], ("x", "y", "c"))
    shift, scale, gate = mods[-1]
    mods[-1] = (shift, scale, gate + 0.0 * loss)

    send_of = dict(w_in=lambda g: _scatter_perm(g, IN_SEGS, w_in.shape[2]),
                   w_out=lambda g: g.reshape((N_DEV,) + w_out.shape[1:]),
                   w_uq=lambda g: _scatter_cols(_heads_together(g), mla_w_uq.shape[2]),
                   w_ukv=lambda g: _scatter_cols(g, mla_w_ukv.shape[2]))

    def grad_sends(gr):
        return [send_of[k](g).astype(MXU) for k, g in gr.items()]

    my_chip = me // 2
    my_core = (me % 2).astype(jnp.int32).reshape(1)

    def chip_sums(gr, tag):
        sends = grad_sends(gr)
        got = _exchange_hbm(RS_PAIR, sends, land_shapes([sd[0] for sd in sends], N_DEV // 2), f"scatter_grads_{tag}_pair")
        return [_pair_sum(sd, gt, my_core, f"pair_sum_{k}_{tag}") for sd, gt, k in zip(sends, got, gr)]

    def with_own_chip(lands, sums):
        return [lax.dynamic_update_index_in_dim(ld, lax.dynamic_index_in_dim(sm, my_chip, axis=0, keepdims=False),
                                                my_chip, 0) for ld, sm in zip(lands, sums)]

    small_names = ["norm_g", "gla_wg_f", "gla_bg_f", "gla_wg_b", "gla_bg_b", "gla_norm_g", "mla_q_norm_g",
                   "mla_kv_norm_g", "mla_out_g", "conv_w", "conv_out_g"]
    d_mods, grads, recv = [None] * nl, [None] * nl, [None] * nl
    flight = {}
    small = {}

    def gather_small(d_x, d_mod0, gr0):
        d_mods[0], grads[0] = d_mod0, _natural_small(gr0)
        d_mod_mine = jnp.stack([jnp.concatenate(d_mods[l], axis=-1)[0] for l in range(nl)])
        parts = [d_mod_mine] + [jnp.stack([grads[l][n] for l in range(nl)]) for n in small_names] + [d_final_g]
        (g2,) = _exchange([_pack(parts)], "gather_small_grads", False, pltpu.VMEM)
        small["d_mod_all"] = g2.reshape(N_DEV, -1)[:, :d_mod_mine.size].reshape(N_DEV, nl, 3 * d)
        small["summed"] = dict(zip(["ada_b"] + small_names + ["final_g"],
                                   _unpack(_sum_devices(g2, "sum_small_grads"), [p.shape for p in parts])))
        return (g2,)

    pairs = {}
    def end_flight(key, after, name):
        sm, lands = _plan_wait(RS_CHIPS, flight.pop(key)[0], after, name)
        return with_own_chip(lands, sm)

    for l in reversed(range(nl)):
        def ship(big_grads, l=l):
            if l > 0:
                sends = grad_sends(big_grads)
                pairs[l] = (_plan_start(RS_PAIR, sends, [(N_DEV // 2,) + sd.shape[1:] for sd in sends],
                                        big_grads["w_in"], f"scatter_grads_l{l}_pair_start"), sends)
                return (pairs[l][0][-1],)
            sm = chip_sums(dict(w_in=big_grads["w_in"]), f"l{l}")
            flight[l] = (_plan_start(RS_CHIPS, sm, [a.shape for a in sm], big_grads["w_in"], f"scatter_grads_l{l}_start"),
                         sm)
            return (flight[l][0][-1],)

        def ship_rest(rest_grads, l=l):
            if l + 1 in flight:
                recv[l + 1] = end_flight(l + 1, list(rest_grads.values()), f"scatter_grads_l{l + 1}_wait")
            sm = chip_sums(rest_grads, f"l{l}_rest")
            flight["rest"] = (_plan_start(RS_CHIPS, sm, [a.shape for a in sm], rest_grads["w_out"],
                                          f"scatter_grads_l{l}_rest_start"), sm)
            return flight["rest"][0][-1][0, 0]

        shift, scale, gate = mods[l]
        if l + 1 in flight:
            gate = gate + flight[l + 1][0][-1][0, 0]
        if l > 0:
            d_h, d_mods[l], gr = _layer_bwd(d_h, saved[l], (shift, scale, gate), layers[l], cos, sin, f"l{l}", ship)
            grads[l] = _natural_small(gr)
            sends, got = _plan_wait(RS_PAIR, pairs.pop(l)[0], [d_h], f"scatter_grads_l{l}_pair_wait")
            sm = [_pair_sum(sd, gt, my_core, f"pair_sum_{n}_l{l}") for sd, gt, n in zip(sends, got, big_names)]
            flight[l] = (_plan_start(RS_CHIPS, sm, [a.shape for a in sm], d_h, f"scatter_grads_l{l}_start"), sm)
        else:
            d_h, _, _ = _layer_bwd(d_h, saved[l], (shift, scale, gate), layers[l], cos, sin, f"l{l}", ship, gather_small,
                                   ship_rest)
    pending = flight[0][0]
    grad_x = d_h[None]
    summed = small["summed"]
    summed["gla_wg_f"] = _my_cols(summed["gla_wg_f"], me, wgc)
    summed["gla_wg_b"] = _my_cols(summed["gla_wg_b"], me, wgc)
    summed["conv_w"] = _my_cols(summed["conv_w"], me, cwc)

    d_mod_cols = jnp.moveaxis(_my_cols(small["d_mod_all"], me, ada_cols), 0, 1) + pending[-1][0, 0]
    out = {}
    out["ada_w"] = _ada_grad_adam(c_act, d_mod_cols, ada_w, m_ada_w, v_ada_w, "ada_grad_adam")

    given = dict(ada_b=(ada_b, m_ada_b, v_ada_b), norm_g=(norm_g, m_norm_g, v_norm_g),
                 gla_wg_f=(gla_wg_f, m_gla_wg_f, v_gla_wg_f), gla_bg_f=(gla_bg_f, m_gla_bg_f, v_gla_bg_f),
                 gla_wg_b=(gla_wg_b, m_gla_wg_b, v_gla_wg_b), gla_bg_b=(gla_bg_b, m_gla_bg_b, v_gla_bg_b),
                 gla_norm_g=(gla_norm_g, m_gla_norm_g, v_gla_norm_g),
                 mla_q_norm_g=(mla_q_norm_g, m_mla_q_norm_g, v_mla_q_norm_g),
                 mla_kv_norm_g=(mla_kv_norm_g, m_mla_kv_norm_g, v_mla_kv_norm_g),
                 mla_out_g=(mla_out_g, m_mla_out_g, v_mla_out_g), conv_w=(conv_w, m_conv_w, v_conv_w),
                 conv_out_g=(conv_out_g, m_conv_out_g, v_conv_out_g), final_g=(final_g, m_final_g, v_final_g))
    names = list(given)

    def two_d(a):
        return a.reshape(1, -1) if a.ndim == 1 else a

    g_nat = [summed[n].reshape(given[n][0].shape) for n in names]
    res = _adam_small([two_d(given[n][0]) for n in names], [two_d(g) for g in g_nat],
                      [two_d(given[n][1]) for n in names], [two_d(given[n][2]) for n in names], "adam_small")
    for i, n in enumerate(names):
        out[n] = (g_nat[i],) + tuple(r[i].reshape(given[n][0].shape) for r in res)

    state = dict(w_in=adam_w_in, w_out=(w_out, m_w_out, v_w_out), mla_w_uq=(mla_w_uq, m_mla_w_uq, v_mla_w_uq),
                 mla_w_ukv=(mla_w_ukv, m_mla_w_ukv, v_mla_w_ukv))
    done = [out["ada_w"][0], res[0][0]]
    for l in reversed(range(nl)):
        if l == 0:
            rest = end_flight("rest", done, "scatter_grads_l0_rest_wait")
            recv[0] = end_flight(0, done + rest[:1], "scatter_grads_l0_wait") + rest
        for i, n in enumerate(big_names):
            out[n] = _adam_big(recv[l][i], *state[n], l, out.get(n), f"adam_{n}_l{l}",
                               (pending[-1],))
        done = done + [out[n][0] for n in big_names]

    order = ["ada_w", "ada_b", "norm_g", "w_in", "gla_wg_f", "gla_bg_f", "gla_wg_b", "gla_bg_b", "gla_norm_g",
             "mla_q_norm_g", "mla_kv_norm_g", "mla_w_uq", "mla_w_ukv", "mla_out_g", "conv_w", "conv_out_g", "w_out",
             "final_g"]
    return (loss, grad_x, *[out[n][0] for n in order], *[out[n][1] for n in order], *[out[n][2] for n in order],
            *[out[n][3] for n in order])
```

```python
import functools

import jax
import jax.numpy as jnp
from jax import lax
from jax.experimental import pallas as pl
from jax.experimental.pallas import tpu as pltpu

F32 = jnp.float32
MXU = jnp.bfloat16
HI = lax.Precision.HIGHEST
N_DEV = 8
MESH = pl.DeviceIdType.MESH

D_MIX = 2048
GH, GDK, GDV = 6, 64, 128
GW = GH * GDV
GQK = GH * GDK
GRANK = 16
GTEMP = 16.0
CHUNK = 64
MH, MQL, MKVL, MNOPE, MROPE, MDV = 6, 384, 256, 128, 64, 128
MW = MH * MDV
MQW = MH * (MNOPE + MROPE)
MKVW = MH * (MNOPE + MDV)
CONV_CH = 512
ROPE_THETA = 10000.0
EPS = 1e-6
IN_DIM = 5856
OZ, OCB, OCC, OCX, OMKV, OGV, OGQ, OGK, OMQ, OT = 0, 2048, 2560, 3072, 3584, 3840, 4608, 4992, 5376, 5760
PW = 5888
LANES = 128
V7X_VMEM_BYTES = 64 * 1024 * 1024
VMEM_LIMIT = V7X_VMEM_BYTES * 7 // 8

ADAM_LR, ADAM_B1, ADAM_B2, ADAM_EPS, ADAM_WD, ADAM_STEP = 0.001, 0.9, 0.999, 1e-08, 0.01, 10


def _cp(sem=None):
    return pltpu.CompilerParams(dimension_semantics=sem, vmem_limit_bytes=VMEM_LIMIT)


def _dot(a, b):
    return jnp.dot(a.astype(MXU), b.astype(MXU), preferred_element_type=F32)


def _dot_nt(a, b):
    return lax.dot_general(a.astype(MXU), b.astype(MXU), (((1,), (1,)), ((), ())), preferred_element_type=F32)


def _dot_tn(a, b):
    return lax.dot_general(a.astype(MXU), b.astype(MXU), (((0,), (0,)), ((), ())), preferred_element_type=F32)


def _dotf(a, b):
    return jnp.dot(a, b, precision=HI, preferred_element_type=F32)


def _dotf_tn(a, b):
    return lax.dot_general(a, b, (((0,), (0,)), ((), ())), precision=HI, preferred_element_type=F32)


def _split3(x):
    hi = x.astype(jnp.bfloat16)
    r1 = x - hi.astype(F32)
    mid = r1.astype(jnp.bfloat16)
    lo = (r1 - mid.astype(F32)).astype(jnp.bfloat16)
    return hi, mid, lo


def _cum_dot(cum, x, transpose=False):
    dn = (((0,), (0,)), ((), ())) if transpose else (((1,), (0,)), ((), ()))
    cb = cum.astype(jnp.bfloat16)
    parts = [lax.dot_general(cb, p, dn, preferred_element_type=F32) for p in _split3(x)]
    return parts[0] + parts[1] + parts[2]


def _rows(s):
    return min(256, s)


def _rows_light(s):
    return min(512, s)


def _rms(x, g):
    r = lax.rsqrt(jnp.mean(x * x, axis=-1, keepdims=True) + EPS)
    return x * r * g


def _rms_bwd(dy, x, g):
    r = lax.rsqrt(jnp.mean(x * x, axis=-1, keepdims=True) + EPS)
    xh = x * r
    dxh = dy * g
    dg = jnp.sum(dy * xh, axis=0, keepdims=True)
    dx = r * (dxh - xh * jnp.mean(dxh * xh, axis=-1, keepdims=True))
    return dx, dg


def _sigmoid(z):
    return jax.nn.sigmoid(z)


def _matmul(a, b, *, dims, tm, tn, tk, out_dtypes, name, epilogue=None, extras=(), extra_kinds=(), after=()):
    if dims == "nn":
        (m, k), n, mul = a.shape, b.shape[1], _dot
    elif dims == "nt":
        (m, k), n, mul = a.shape, b.shape[0], _dot_nt
    else:
        (k, m), n, mul = a.shape, b.shape[1], _dot_tn
    tm, tn, tk = min(tm, m), min(tn, n), min(tk, k)
    assert m % tm == 0 and n % tn == 0 and k % tk == 0, (m, n, k, tm, tn, tk)
    if dims == "nn":
        a_spec = pl.BlockSpec((tm, tk), lambda i, j, kk: (i, kk))
        b_spec = pl.BlockSpec((tk, tn), lambda i, j, kk: (kk, j))
    elif dims == "nt":
        a_spec = pl.BlockSpec((tm, tk), lambda i, j, kk: (i, kk))
        b_spec = pl.BlockSpec((tn, tk), lambda i, j, kk: (j, kk))
    else:
        a_spec = pl.BlockSpec((tk, tm), lambda i, j, kk: (kk, i))
        b_spec = pl.BlockSpec((tk, tn), lambda i, j, kk: (kk, j))
    nk = k // tk
    n_extra = len(extras)
    n_out = len(out_dtypes)
    n_after = len(after)
    extra_specs = []
    for kind in extra_kinds:
        if kind == "mn":
            extra_specs.append(pl.BlockSpec((tm, tn), lambda i, j, kk: (i, j)))
        else:
            extra_specs.append(pl.BlockSpec((1, tn), lambda i, j, kk: (0, j)))

    def finish(res, ex, outs):
        vals = (res,) if epilogue is None else epilogue(res, *[e[...] for e in ex])
        for o, v in zip(outs, vals):
            o[...] = v.astype(o.dtype)

    def body(*refs):
        a_ref, b_ref = refs[0], refs[1]
        ex = refs[2:2 + n_extra]
        outs = refs[2 + n_extra + n_after:2 + n_extra + n_after + n_out]
        if nk == 1:
            finish(mul(a_ref[...], b_ref[...]), ex, outs)
            return
        acc = refs[-1]
        kk = pl.program_id(2)

        @pl.when(kk == 0)
        def _():
            acc[...] = jnp.zeros_like(acc)

        acc[...] += mul(a_ref[...], b_ref[...])

        @pl.when(kk == nk - 1)
        def _():
            finish(acc[...], ex, outs)

    out_spec = pl.BlockSpec((tm, tn), lambda i, j, kk: (i, j))
    res = pl.pallas_call(
        body, grid=(m // tm, n // tn, nk),
        in_specs=[a_spec, b_spec] + extra_specs + [pl.BlockSpec(memory_space=pl.ANY)] * n_after,
        out_specs=[out_spec] * n_out,
        out_shape=[jax.ShapeDtypeStruct((m, n), dt) for dt in out_dtypes],
        scratch_shapes=[] if nk == 1 else [pltpu.VMEM((tm, tn), F32)],
        name=name, compiler_params=_cp(("parallel", "parallel", "arbitrary")),
    )(a, b, *extras, *after)
    return res


def _norm_mod(x, g, scale, shift, name):
    s, d = x.shape
    tr = _rows_light(s)

    def body(x_ref, g_ref, sc_ref, sh_ref, h_ref):
        h = _rms(x_ref[...], g_ref[...]) * (1.0 + sc_ref[...]) + sh_ref[...]
        h_ref[...] = h.astype(h_ref.dtype)

    row = pl.BlockSpec((tr, d), lambda i: (i, 0))
    vec = pl.BlockSpec((1, d), lambda i: (0, 0))
    return pl.pallas_call(body, grid=(s // tr,), in_specs=[row, vec, vec, vec], out_specs=row,
                          out_shape=jax.ShapeDtypeStruct((s, d), MXU), name=name,
                          compiler_params=_cp(("parallel",)))(x, g, scale, shift)


def _norm_mod_bwd(d_h, x, d_out, g, scale, name):
    s, d = x.shape
    tr = _rows(s)

    def body(dh_ref, x_ref, do_ref, g_ref, sc_ref, dx_ref, dsh_ref, dsc_ref, dg_ref):
        i = pl.program_id(0)

        @pl.when(i == 0)
        def _():
            dsh_ref[...] = jnp.zeros_like(dsh_ref)
            dsc_ref[...] = jnp.zeros_like(dsc_ref)
            dg_ref[...] = jnp.zeros_like(dg_ref)

        dh = dh_ref[...]
        xv = x_ref[...]
        gv = g_ref[...]
        r = lax.rsqrt(jnp.mean(xv * xv, axis=-1, keepdims=True) + EPS)
        xh = xv * r
        dsh_ref[...] += jnp.sum(dh, axis=0, keepdims=True)
        dsc_ref[...] += jnp.sum(dh * (xh * gv), axis=0, keepdims=True)
        dhn = dh * (1.0 + sc_ref[...])
        dg_ref[...] += jnp.sum(dhn * xh, axis=0, keepdims=True)
        dxh = dhn * gv
        dx_ref[...] = do_ref[...] + r * (dxh - xh * jnp.mean(dxh * xh, axis=-1, keepdims=True))

    row = pl.BlockSpec((tr, d), lambda i: (i, 0))
    vec = pl.BlockSpec((1, d), lambda i: (0, 0))
    vshape = jax.ShapeDtypeStruct((1, d), F32)
    return pl.pallas_call(body, grid=(s // tr,), in_specs=[row, row, row, vec, vec],
                          out_specs=[row, vec, vec, vec],
                          out_shape=[jax.ShapeDtypeStruct((s, d), F32), vshape, vshape, vshape],
                          name=name, compiler_params=_cp(("arbitrary",)))(d_h, x, d_out, g, scale)


def _gate_bwd(d_out, u, gate, name):
    s, d = d_out.shape
    tr = _rows_light(s)

    def body(do_ref, u_ref, gt_ref, du_ref, dgt_ref):
        @pl.when(pl.program_id(0) == 0)
        def _():
            dgt_ref[...] = jnp.zeros_like(dgt_ref)

        do = do_ref[...]
        du_ref[...] = (do * gt_ref[...]).astype(du_ref.dtype)
        dgt_ref[...] += jnp.sum(do * u_ref[...], axis=0, keepdims=True)

    row = pl.BlockSpec((tr, d), lambda i: (i, 0))
    vec = pl.BlockSpec((1, d), lambda i: (0, 0))
    return pl.pallas_call(body, grid=(s // tr,), in_specs=[row, row, vec], out_specs=[row, vec],
                          out_shape=[jax.ShapeDtypeStruct((s, d), MXU), jax.ShapeDtypeStruct((1, d), F32)],
                          name=name, compiler_params=_cp(("arbitrary",)))(d_out, u, gate)


def _final_loss(x, g, target, name):
    s, d = x.shape
    tr = _rows_light(s)

    def body(x_ref, g_ref, t_ref, loss_ref, dx_ref, dg_ref):
        @pl.when(pl.program_id(0) == 0)
        def _():
            loss_ref[...] = jnp.zeros_like(loss_ref)
            dg_ref[...] = jnp.zeros_like(dg_ref)

        xv = x_ref[...]
        gv = g_ref[...]
        diff = _rms(xv, gv) - t_ref[...]
        part = 0.5 * jnp.sum(jnp.sum(diff * diff, axis=-1, keepdims=True) / d, axis=0, keepdims=True)
        loss_ref[...] += jnp.broadcast_to(part, loss_ref.shape)
        dx, dg = _rms_bwd(diff / d, xv, gv)
        dx_ref[...] = dx
        dg_ref[...] += dg

    row = pl.BlockSpec((tr, d), lambda i: (i, 0))
    vec = pl.BlockSpec((1, d), lambda i: (0, 0))
    lvec = pl.BlockSpec((1, LANES), lambda i: (0, 0))
    return pl.pallas_call(body, grid=(s // tr,), in_specs=[row, vec, row], out_specs=[lvec, row, vec],
                          out_shape=[jax.ShapeDtypeStruct((1, LANES), F32), jax.ShapeDtypeStruct((s, d), F32),
                                     jax.ShapeDtypeStruct((1, d), F32)],
                          name=name, compiler_params=_cp(("arbitrary",)))(x, g, target)


def _shift_rows(u, s, down):
    ri = lax.broadcasted_iota(jnp.int32, u.shape, 0)
    if down:
        return jnp.where(ri == 0, 0.0, pltpu.roll(u, 1, 0))
    return jnp.where(ri == s - 1, 0.0, pltpu.roll(u, s - 1, 0))


def _conv_fwd(proj, conv_w, name):
    s = proj.shape[0]
    nt = CONV_CH // LANES

    def body(cb_ref, cc_ref, cx_ref, w_ref, pre_ref):
        u = cc_ref[...] * cx_ref[...]
        conv = _shift_rows(u, s, True) * w_ref[0:1, :] + u * w_ref[1:2, :] + _shift_rows(u, s, False) * w_ref[2:3, :]
        pre_ref[...] = cb_ref[...] * conv

    def col(off):
        return pl.BlockSpec((s, LANES), lambda j: (0, off // LANES + j))

    return pl.pallas_call(body, grid=(nt,), in_specs=[col(OCB), col(OCC), col(OCX), pl.BlockSpec((3, LANES), lambda j: (0, j))],
                          out_specs=pl.BlockSpec((s, LANES), lambda j: (0, j)),
                          out_shape=jax.ShapeDtypeStruct((s, CONV_CH), F32), name=name,
                          compiler_params=_cp(("parallel",)))(proj, proj, proj, conv_w)


def _conv_bwd(proj, conv_w, d_pre, name):
    s = proj.shape[0]
    nt = CONV_CH // LANES

    def body(cb_ref, cc_ref, cx_ref, w_ref, dp_ref, dcb_ref, dcc_ref, dcx_ref, dw_ref):
        cc, cx = cc_ref[...], cx_ref[...]
        u = cc * cx
        up, dn = _shift_rows(u, s, True), _shift_rows(u, s, False)
        w0, w1, w2 = w_ref[0:1, :], w_ref[1:2, :], w_ref[2:3, :]
        conv = up * w0 + u * w1 + dn * w2
        dp = dp_ref[...]
        dcb_ref[...] = (dp * conv).astype(dcb_ref.dtype)
        dconv = dp * cb_ref[...]
        du = _shift_rows(dconv, s, False) * w0 + dconv * w1 + _shift_rows(dconv, s, True) * w2
        dcc_ref[...] = (du * cx).astype(dcc_ref.dtype)
        dcx_ref[...] = (du * cc).astype(dcx_ref.dtype)
        dw_ref[0:1, :] = jnp.sum(dconv * up, axis=0, keepdims=True)
        dw_ref[1:2, :] = jnp.sum(dconv * u, axis=0, keepdims=True)
        dw_ref[2:3, :] = jnp.sum(dconv * dn, axis=0, keepdims=True)

    def col(off):
        return pl.BlockSpec((s, LANES), lambda j: (0, off // LANES + j))

    blk = pl.BlockSpec((s, LANES), lambda j: (0, j))
    wblk = pl.BlockSpec((3, LANES), lambda j: (0, j))
    full = jax.ShapeDtypeStruct((s, CONV_CH), MXU)
    return pl.pallas_call(body, grid=(nt,), in_specs=[col(OCB), col(OCC), col(OCX), wblk, blk],
                          out_specs=[blk, blk, blk, wblk],
                          out_shape=[full, full, full, jax.ShapeDtypeStruct((3, CONV_CH), F32)],
                          name=name, compiler_params=_cp(("parallel",)))(proj, proj, proj, conv_w, d_pre)


GLA_SUB = 8


def _gla_gates(t_ref, wg_ref, bg_ref):
    t = t_ref[...]
    a = _dot(t, wg_ref[...]) + bg_ref[...]
    la = (jnp.minimum(a, 0.0) - jnp.log(1.0 + jnp.exp(-jnp.abs(a)))) / GTEMP
    return t, a, la


def _gla_masks(reverse):
    ri = lax.broadcasted_iota(jnp.int32, (CHUNK, CHUNK), 0)
    ci = lax.broadcasted_iota(jnp.int32, (CHUNK, CHUNK), 1)
    if reverse:
        cum, mask, mask_t = ci >= ri, ci > ri, ri > ci
    else:
        cum, mask, mask_t = ci <= ri, ci <= ri, ri <= ci
    return cum.astype(F32), mask, mask_t


def _gla_specs(s, reverse):
    nsub = min(GLA_SUB, s // CHUNK)
    nsteps = s // (CHUNK * nsub)

    def row(n):
        return nsteps - 1 - n if reverse else n

    def chunk(pi):
        return nsub - 1 - pi if reverse else pi

    return nsub, nsteps, row, chunk


def _gla_fwd(proj, wg_pad, bg, reverse, name):
    s = proj.shape[0]
    nsub, nsteps, row, chunk = _gla_specs(s, reverse)
    rb = nsub * CHUNK

    def body(q_ref, k_ref, v_ref, t_ref, wg_ref, bg_ref, o_ref, st_ref, state):
        @pl.when(pl.program_id(0) == 0)
        def _():
            state[...] = jnp.zeros_like(state)

        _, _, la = _gla_gates(t_ref, wg_ref, bg_ref)
        cumf, mask, _ = _gla_masks(reverse)
        lane = lax.broadcasted_iota(jnp.int32, (CHUNK, LANES), 1)
        for pi in range(nsub):
            rows = slice(chunk(pi) * CHUNK, (chunk(pi) + 1) * CHUNK)
            la_c = la[rows]
            b_all = _cum_dot(cumf, la_c)
            bl_all = jnp.sum(la_c, axis=0, keepdims=True)
            for p in range(GH // 2):
                sl = slice(p * LANES, (p + 1) * LANES)
                b, bl = b_all[:, sl], bl_all[:, sl]
                qd = q_ref[rows, sl] * (GDK ** -0.5) * jnp.exp(b)
                ki = k_ref[rows, sl] * jnp.exp(-b)
                kte = k_ref[rows, sl] * jnp.exp(bl - b)
                decay = jnp.exp(bl)
                for half in range(2):
                    h = 2 * p + half
                    lm = (lane < GDK) if half == 0 else (lane >= GDK)
                    qd_h = jnp.where(lm, qd, 0.0)
                    kte_h = jnp.where(lm, kte, 0.0)
                    v_h = v_ref[rows, h * GDV:(h + 1) * GDV]
                    st = state[h]
                    a_mat = jnp.where(mask, _dot_nt(qd_h, ki), 0.0)
                    o_ref[rows, h * GDV:(h + 1) * GDV] = _dot(a_mat, v_h) + _dot_nt(qd_h, st)
                    st_ref[pi, h] = st
                    state[h] = st * decay + _dot_tn(v_h, kte_h)

    return pl.pallas_call(
        body, grid=(nsteps,),
        in_specs=[pl.BlockSpec((rb, GQK), lambda n: (row(n), OGQ // GQK)),
                  pl.BlockSpec((rb, GQK), lambda n: (row(n), OGK // GQK)),
                  pl.BlockSpec((rb, GW), lambda n: (row(n), OGV // GW)),
                  pl.BlockSpec((rb, LANES), lambda n: (row(n), OT // LANES)),
                  pl.BlockSpec((LANES, GQK), lambda n: (0, 0)),
                  pl.BlockSpec((1, GQK), lambda n: (0, 0))],
        out_specs=[pl.BlockSpec((rb, GW), lambda n: (row(n), 0)),
                   pl.BlockSpec((nsub, GH, GDV, LANES), lambda n: (n, 0, 0, 0))],
        out_shape=[jax.ShapeDtypeStruct((s, GW), F32), jax.ShapeDtypeStruct((s // CHUNK, GH, GDV, LANES), F32)],
        scratch_shapes=[pltpu.VMEM((GH, GDV, LANES), F32)],
        name=name, compiler_params=_cp(("arbitrary",)))(proj, proj, proj, proj, wg_pad, bg)


def _gla_bwd(proj, wg_pad, bg, states, d_o, reverse, name):
    s = proj.shape[0]
    nsub, nsteps, row, chunk = _gla_specs(s, reverse)
    rb = nsub * CHUNK

    def body(q_ref, k_ref, v_ref, t_ref, wg_ref, bg_ref, st_ref, do_ref,
             dq_ref, dk_ref, dv_ref, dt_ref, dwg_ref, dbg_ref, dstate, da_buf):
        @pl.when(pl.program_id(0) == 0)
        def _():
            dstate[...] = jnp.zeros_like(dstate)
            dwg_ref[...] = jnp.zeros_like(dwg_ref)
            dbg_ref[...] = jnp.zeros_like(dbg_ref)

        t, a, la = _gla_gates(t_ref, wg_ref, bg_ref)
        cumf, mask, mask_t = _gla_masks(reverse)
        lane = lax.broadcasted_iota(jnp.int32, (CHUNK, LANES), 1)
        for pi in reversed(range(nsub)):
            rows = slice(chunk(pi) * CHUNK, (chunk(pi) + 1) * CHUNK)
            la_c = la[rows]
            b_all = _cum_dot(cumf, la_c)
            bl_all = jnp.sum(la_c, axis=0, keepdims=True)
            for p in range(GH // 2):
                sl = slice(p * LANES, (p + 1) * LANES)
                b, bl = b_all[:, sl], bl_all[:, sl]
                e, ei, ee = jnp.exp(b), jnp.exp(-b), jnp.exp(bl - b)
                qd = q_ref[rows, sl] * (GDK ** -0.5) * e
                ki, kte = k_ref[rows, sl] * ei, k_ref[rows, sl] * ee
                decay = jnp.exp(bl)
                dqd = jnp.zeros((CHUNK, LANES), F32)
                dki = jnp.zeros((CHUNK, LANES), F32)
                dkte = jnp.zeros((CHUNK, LANES), F32)
                ddecay = jnp.zeros((1, LANES), F32)
                for half in range(2):
                    h = 2 * p + half
                    lm = (lane < GDK) if half == 0 else (lane >= GDK)
                    qd_h = jnp.where(lm, qd, 0.0)
                    ki_h = jnp.where(lm, ki, 0.0)
                    kte_h = jnp.where(lm, kte, 0.0)
                    v_h = v_ref[rows, h * GDV:(h + 1) * GDV]
                    do_h = do_ref[rows, h * GDV:(h + 1) * GDV]
                    st = st_ref[pi, h]
                    dst = dstate[h]
                    at_mat = jnp.where(mask_t, _dot_nt(ki_h, qd_h), 0.0)
                    da_mat = jnp.where(mask, _dot_nt(do_h, v_h), 0.0)
                    dat_mat = jnp.where(mask_t, _dot_nt(v_h, do_h), 0.0)
                    dv_ref[rows, h * GDV:(h + 1) * GDV] = _dot(at_mat, do_h) + _dot_nt(kte_h, dst)
                    dqd += _dot(jnp.concatenate([do_h, da_mat], axis=1), jnp.concatenate([st, ki_h], axis=0))
                    dki += _dot(dat_mat, qd_h)
                    dkte += _dot(v_h, dst)
                    ddecay += jnp.sum(dst * st, axis=0, keepdims=True)
                    dstate[h] = dst * decay + _dot_tn(do_h, qd_h)
                dq_ref[rows, sl] = dqd * e * (GDK ** -0.5)
                dk_ref[rows, sl] = dki * ei + dkte * ee
                db = dqd * qd - dki * ki - dkte * kte
                dbl = jnp.sum(dkte * kte, axis=0, keepdims=True) + decay * ddecay
                da_buf[rows, sl] = _cum_dot(cumf, db, True) + dbl
        da = da_buf[...] * (1.0 / GTEMP) * _sigmoid(-a)
        dt_ref[...] = _dot_nt(da, wg_ref[...])
        dwg_ref[...] += _dot_tn(t, da)
        dbg_ref[...] += jnp.sum(da, axis=0, keepdims=True)

    def prow(j):
        return row(nsteps - 1 - j)

    return pl.pallas_call(
        body, grid=(nsteps,),
        in_specs=[pl.BlockSpec((rb, GQK), lambda j: (prow(j), OGQ // GQK)),
                  pl.BlockSpec((rb, GQK), lambda j: (prow(j), OGK // GQK)),
                  pl.BlockSpec((rb, GW), lambda j: (prow(j), OGV // GW)),
                  pl.BlockSpec((rb, LANES), lambda j: (prow(j), OT // LANES)),
                  pl.BlockSpec((LANES, GQK), lambda j: (0, 0)),
                  pl.BlockSpec((1, GQK), lambda j: (0, 0)),
                  pl.BlockSpec((nsub, GH, GDV, LANES), lambda j: (nsteps - 1 - j, 0, 0, 0)),
                  pl.BlockSpec((rb, GW), lambda j: (prow(j), 0))],
        out_specs=[pl.BlockSpec((rb, GQK), lambda j: (prow(j), 0)),
                   pl.BlockSpec((rb, GQK), lambda j: (prow(j), 0)),
                   pl.BlockSpec((rb, GW), lambda j: (prow(j), 0)),
                   pl.BlockSpec((rb, LANES), lambda j: (prow(j), 0)),
                   pl.BlockSpec((LANES, GQK), lambda j: (0, 0)),
                   pl.BlockSpec((1, GQK), lambda j: (0, 0))],
        out_shape=[jax.ShapeDtypeStruct((s, GQK), F32), jax.ShapeDtypeStruct((s, GQK), F32),
                   jax.ShapeDtypeStruct((s, GW), F32), jax.ShapeDtypeStruct((s, LANES), F32),
                   jax.ShapeDtypeStruct((LANES, GQK), F32), jax.ShapeDtypeStruct((1, GQK), F32)],
        scratch_shapes=[pltpu.VMEM((GH, GDV, LANES), F32), pltpu.VMEM((rb, GQK), F32)],
        name=name, compiler_params=_cp(("arbitrary",)))(proj, proj, proj, proj, wg_pad, bg, states, d_o)


def _rot_half(x):
    lane = lax.broadcasted_iota(jnp.int32, x.shape, 1)
    first = (lane % MROPE) < (MROPE // 2)
    return jnp.where(first, -pltpu.roll(x, LANES - MROPE // 2, 1), pltpu.roll(x, MROPE // 2, 1))


def _mla_prep(proj, cos, sin, qg, kvg, w_uq, w_ukv, name):
    s = proj.shape[0]
    tr = _rows(s)

    def body(mq_ref, mkv_ref, t_ref, cos_ref, sin_ref, qg_ref, kvg_ref, wuq_ref, wukv_ref, q_ref, k_ref, v_ref):
        cosv, sinv = cos_ref[...], sin_ref[...]
        lane = lax.broadcasted_iota(jnp.int32, (tr, LANES), 1)

        def rope(xv):
            return xv * cosv + _rot_half(xv) * sinv

        qm = _dot(_rms(mq_ref[...], qg_ref[...]), wuq_ref[...])
        kv = _dot(_rms(mkv_ref[...], kvg_ref[...]), wukv_ref[...])
        kr_lo = jnp.where(lane < MROPE, rope(t_ref[...]), 0.0)
        kr_hi = pltpu.roll(kr_lo, MROPE, 1)
        for p in range(MH // 2):
            r = rope(qm[:, MW + p * LANES:MW + (p + 1) * LANES]).astype(q_ref.dtype)
            q_ref[2 * p, :, LANES:] = r
            q_ref[2 * p + 1, :, LANES:] = r
        for h in range(MH):
            q_ref[h, :, :LANES] = qm[:, h * LANES:(h + 1) * LANES].astype(q_ref.dtype)
            k_ref[h, :, :LANES] = kv[:, 2 * h * LANES:(2 * h + 1) * LANES].astype(k_ref.dtype)
            k_ref[h, :, LANES:] = (kr_lo if h % 2 == 0 else kr_hi).astype(k_ref.dtype)
            v_ref[h] = kv[:, (2 * h + 1) * LANES:(2 * h + 2) * LANES].astype(v_ref.dtype)

    def full(shape):
        return pl.BlockSpec(shape, lambda i: (0,) * len(shape))

    return pl.pallas_call(
        body, grid=(s // tr,),
        in_specs=[pl.BlockSpec((tr, MQL), lambda i: (i, OMQ // MQL)),
                  pl.BlockSpec((tr, MKVL), lambda i: (i, OMKV // MKVL)),
                  pl.BlockSpec((tr, LANES), lambda i: (i, OT // LANES)),
                  pl.BlockSpec((tr, LANES), lambda i: (i, 0)),
                  pl.BlockSpec((tr, LANES), lambda i: (i, 0)),
                  full((1, MQL)), full((1, MKVL)), full((MQL, MQW)), full((MKVL, MKVW))],
        out_specs=[pl.BlockSpec((MH, tr, 2 * LANES), lambda i: (0, i, 0)),
                   pl.BlockSpec((MH, tr, 2 * LANES), lambda i: (0, i, 0)),
                   pl.BlockSpec((MH, tr, LANES), lambda i: (0, i, 0))],
        out_shape=[jax.ShapeDtypeStruct((MH, s, 2 * LANES), MXU), jax.ShapeDtypeStruct((MH, s, 2 * LANES), MXU),
                   jax.ShapeDtypeStruct((MH, s, LANES), MXU)],
        name=name, compiler_params=_cp(("parallel",)))(proj, proj, proj, cos, sin, qg, kvg, w_uq, w_ukv)


def _mla_prep_bwd(proj, cos, sin, qg, kvg, w_uq, w_ukv, d_q, d_k, d_v, name):
    s = proj.shape[0]
    tr = _rows(s)

    def body(mq_ref, mkv_ref, cos_ref, sin_ref, qg_ref, kvg_ref, wuq_ref, wukv_ref, dq_ref, dk_ref, dv_ref,
             dmq_ref, dmkv_ref, dt_ref, dwuq_ref, dwukv_ref, dqg_ref, dkvg_ref):
        @pl.when(pl.program_id(0) == 0)
        def _():
            for r in (dwuq_ref, dwukv_ref, dqg_ref, dkvg_ref):
                r[...] = jnp.zeros_like(r)

        cosv, sinv = cos_ref[...], sin_ref[...]
        lane = lax.broadcasted_iota(jnp.int32, (tr, LANES), 1)
        lo = lane < MROPE

        def unrope(dv):
            return dv * cosv - _rot_half(dv * sinv)

        parts = [dq_ref[h, :, :LANES] for h in range(MH)]
        for p in range(MH // 2):
            parts.append(unrope(jnp.where(lo, dq_ref[2 * p, :, LANES:], dq_ref[2 * p + 1, :, LANES:])))
        d_qm = jnp.concatenate(parts, axis=1)
        mq, qgv = mq_ref[...], qg_ref[...]
        cq = _rms(mq, qgv)
        dwuq_ref[...] += _dot_tn(cq, d_qm)
        dmq, dqg = _rms_bwd(_dot_nt(d_qm, wuq_ref[...]), mq, qgv)
        dmq_ref[...] = dmq.astype(dmq_ref.dtype)
        dqg_ref[...] += dqg

        parts = []
        for h in range(MH):
            parts += [dk_ref[h, :, :LANES], dv_ref[h]]
        d_kv = jnp.concatenate(parts, axis=1)
        mkv, kvgv = mkv_ref[...], kvg_ref[...]
        ckv = _rms(mkv, kvgv)
        dwukv_ref[...] += _dot_tn(ckv, d_kv)
        dmkv, dkvg = _rms_bwd(_dot_nt(d_kv, wukv_ref[...]), mkv, kvgv)
        dmkv_ref[...] = dmkv.astype(dmkv_ref.dtype)
        dkvg_ref[...] += dkvg

        even = dk_ref[0, :, LANES:] + dk_ref[2, :, LANES:] + dk_ref[4, :, LANES:]
        odd = dk_ref[1, :, LANES:] + dk_ref[3, :, LANES:] + dk_ref[5, :, LANES:]
        d_kr = jnp.where(lo, even, 0.0) + pltpu.roll(jnp.where(lo, 0.0, odd), MROPE, 1)
        dt_ref[...] = jnp.where(lo, unrope(d_kr), 0.0)

    def full(shape):
        return pl.BlockSpec(shape, lambda i: (0,) * len(shape))

    return pl.pallas_call(
        body, grid=(s // tr,),
        in_specs=[pl.BlockSpec((tr, MQL), lambda i: (i, OMQ // MQL)),
                  pl.BlockSpec((tr, MKVL), lambda i: (i, OMKV // MKVL)),
                  pl.BlockSpec((tr, LANES), lambda i: (i, 0)),
                  pl.BlockSpec((tr, LANES), lambda i: (i, 0)),
                  full((1, MQL)), full((1, MKVL)), full((MQL, MQW)), full((MKVL, MKVW)),
                  pl.BlockSpec((MH, tr, 2 * LANES), lambda i: (0, i, 0)),
                  pl.BlockSpec((MH, tr, 2 * LANES), lambda i: (0, i, 0)),
                  pl.BlockSpec((MH, tr, LANES), lambda i: (0, i, 0))],
        out_specs=[pl.BlockSpec((tr, MQL), lambda i: (i, 0)), pl.BlockSpec((tr, MKVL), lambda i: (i, 0)),
                   pl.BlockSpec((tr, LANES), lambda i: (i, 0)),
                   full((MQL, MQW)), full((MKVL, MKVW)), full((1, MQL)), full((1, MKVL))],
        out_shape=[jax.ShapeDtypeStruct((s, MQL), MXU), jax.ShapeDtypeStruct((s, MKVL), MXU),
                   jax.ShapeDtypeStruct((s, LANES), F32),
                   jax.ShapeDtypeStruct((MQL, MQW), F32), jax.ShapeDtypeStruct((MKVL, MKVW), F32),
                   jax.ShapeDtypeStruct((1, MQL), F32), jax.ShapeDtypeStruct((1, MKVL), F32)],
        name=name, compiler_params=_cp(("arbitrary",)))(proj, proj, cos, sin, qg, kvg, w_uq, w_ukv, d_q, d_k, d_v)


ATT_SCALE = (MNOPE + MROPE) ** -0.5
ATT_SCALE_LOG2 = ATT_SCALE * 1.4426950408889634
ATT_TQ_FWD, ATT_TQ = 2048, 2048
ATT_SUB, ATT_SUB_BWD = 256, 256


def _attn_fwd(q, k, v, name):
    s = q.shape[1]
    tq = min(ATT_TQ_FWD, s)
    sub = min(ATT_SUB, tq)

    def body(q_ref, k_ref, v_ref, o_ref, lse_ref):
        for r0 in range(0, tq, sub):
            rows = slice(r0, r0 + sub)
            sc = _dot_nt(q_ref[0, rows, :], k_ref[0])
            m = jnp.max(sc, axis=-1, keepdims=True)
            p = jnp.exp2((sc - m) * ATT_SCALE_LOG2)
            l = jnp.sum(p, axis=-1, keepdims=True)
            o_ref[rows, :] = _dot(p, v_ref[0]) / l
            lse_ref[0, rows, :] = m * ATT_SCALE_LOG2 + jnp.log2(l)

    return pl.pallas_call(
        body, grid=(MH, s // tq),
        in_specs=[pl.BlockSpec((1, tq, 2 * LANES), lambda h, i: (h, i, 0)),
                  pl.BlockSpec((1, s, 2 * LANES), lambda h, i: (h, 0, 0)),
                  pl.BlockSpec((1, s, LANES), lambda h, i: (h, 0, 0))],
        out_specs=[pl.BlockSpec((tq, LANES), lambda h, i: (i, h)),
                   pl.BlockSpec((1, tq, 1), lambda h, i: (h, i, 0))],
        out_shape=[jax.ShapeDtypeStruct((s, MW), F32), jax.ShapeDtypeStruct((MH, s, 1), F32)],
        name=name, compiler_params=_cp(("parallel", "parallel")))(q, k, v)


def _attn_bwd(q, k, v, o, lse, d_o, name):
    s = q.shape[1]
    tq = min(ATT_TQ, s)
    sub = min(ATT_SUB_BWD, tq)

    def body(q_ref, k_ref, v_ref, o_ref, lse_ref, do_ref, dq_ref, dk_ref, dv_ref):
        @pl.when(pl.program_id(1) == 0)
        def _():
            dk_ref[...] = jnp.zeros_like(dk_ref)
            dv_ref[...] = jnp.zeros_like(dv_ref)

        kv = k_ref[0]
        for r0 in range(0, tq, sub):
            rows = slice(r0, r0 + sub)
            qv, do = q_ref[0, rows, :], do_ref[rows, :]
            p = jnp.exp2(_dot_nt(qv, kv) * ATT_SCALE_LOG2 - lse_ref[0, rows, :])
            delta = jnp.sum(do * o_ref[rows, :], axis=-1, keepdims=True)
            ds = p * (_dot_nt(do, v_ref[0]) - delta)
            dq_ref[0, rows, :] = _dot(ds, kv) * ATT_SCALE
            dk_ref[0] += _dot_tn(ds, qv) * ATT_SCALE
            dv_ref[0] += _dot_tn(p, do)

    return pl.pallas_call(
        body, grid=(MH, s // tq),
        in_specs=[pl.BlockSpec((1, tq, 2 * LANES), lambda h, i: (h, i, 0)),
                  pl.BlockSpec((1, s, 2 * LANES), lambda h, i: (h, 0, 0)),
                  pl.BlockSpec((1, s, LANES), lambda h, i: (h, 0, 0)),
                  pl.BlockSpec((tq, LANES), lambda h, i: (i, h)),
                  pl.BlockSpec((1, tq, 1), lambda h, i: (h, i, 0)),
                  pl.BlockSpec((tq, LANES), lambda h, i: (i, h))],
        out_specs=[pl.BlockSpec((1, tq, 2 * LANES), lambda h, i: (h, i, 0)),
                   pl.BlockSpec((1, s, 2 * LANES), lambda h, i: (h, 0, 0)),
                   pl.BlockSpec((1, s, LANES), lambda h, i: (h, 0, 0))],
        out_shape=[jax.ShapeDtypeStruct((MH, s, 2 * LANES), F32), jax.ShapeDtypeStruct((MH, s, 2 * LANES), F32),
                   jax.ShapeDtypeStruct((MH, s, LANES), F32)],
        name=name, compiler_params=_cp(("parallel", "arbitrary")))(q, k, v, o, lse, d_o)


def _merge_fwd(o_f, o_b, o_att, pre, proj, gng, mog, cog, name):
    s = proj.shape[0]
    tr = _rows_light(s)

    def body(of_ref, ob_ref, oa_ref, pre_ref, z_ref, gng_ref, mog_ref, cog_ref, y_ref):
        z = z_ref[...]
        sz = z * _sigmoid(z)
        osum = of_ref[...] + ob_ref[...]
        gg = gng_ref[...]
        for h in range(GH):
            sl = slice(h * GDV, (h + 1) * GDV)
            y_ref[:, sl] = (_rms(osum[:, sl], gg) * sz[:, sl]).astype(y_ref.dtype)
        y_ref[:, GW:GW + MW] = (_rms(oa_ref[...], mog_ref[...]) * sz[:, GW:GW + MW]).astype(y_ref.dtype)
        y_ref[:, GW + MW:] = (_rms(pre_ref[...], cog_ref[...]) * sz[:, GW + MW:]).astype(y_ref.dtype)

    def row(w):
        return pl.BlockSpec((tr, w), lambda i: (i, 0))

    def vec(w):
        return pl.BlockSpec((1, w), lambda i: (0, 0))

    return pl.pallas_call(
        body, grid=(s // tr,),
        in_specs=[row(GW), row(GW), row(MW), row(CONV_CH), row(D_MIX), vec(GDV), vec(MW), vec(CONV_CH)],
        out_specs=row(D_MIX), out_shape=jax.ShapeDtypeStruct((s, D_MIX), MXU),
        name=name, compiler_params=_cp(("parallel",)))(o_f, o_b, o_att, pre, proj, gng, mog, cog)


def _merge_bwd(d_y, o_f, o_b, o_att, pre, proj, gng, mog, cog, name):
    s = proj.shape[0]
    tr = _rows(s)

    def body(dy_ref, of_ref, ob_ref, oa_ref, pre_ref, z_ref, gng_ref, mog_ref, cog_ref,
             dz_ref, dos_ref, doa_ref, dpre_ref, dgng_ref, dmog_ref, dcog_ref):
        @pl.when(pl.program_id(0) == 0)
        def _():
            for r in (dgng_ref, dmog_ref, dcog_ref):
                r[...] = jnp.zeros_like(r)

        z, dy = z_ref[...], dy_ref[...]
        sg = _sigmoid(z)
        sz = z * sg
        dsz = sg * (1.0 + z * (1.0 - sg))
        dcat = dy * sz
        dyz = dy * dsz
        osum = of_ref[...] + ob_ref[...]
        gg = gng_ref[...]
        dgg = jnp.zeros_like(gg)
        for h in range(GH):
            sl = slice(h * GDV, (h + 1) * GDV)
            dz_ref[:, sl] = (dyz[:, sl] * _rms(osum[:, sl], gg)).astype(dz_ref.dtype)
            dx, dg = _rms_bwd(dcat[:, sl], osum[:, sl], gg)
            dos_ref[:, sl] = dx
            dgg += dg
        dgng_ref[...] += dgg
        sl = slice(GW, GW + MW)
        oa, mg = oa_ref[...], mog_ref[...]
        dz_ref[:, sl] = (dyz[:, sl] * _rms(oa, mg)).astype(dz_ref.dtype)
        dx, dg = _rms_bwd(dcat[:, sl], oa, mg)
        doa_ref[...] = dx
        dmog_ref[...] += dg
        sl = slice(GW + MW, D_MIX)
        pv, cg = pre_ref[...], cog_ref[...]
        dz_ref[:, sl] = (dyz[:, sl] * _rms(pv, cg)).astype(dz_ref.dtype)
        dx, dg = _rms_bwd(dcat[:, sl], pv, cg)
        dpre_ref[...] = dx
        dcog_ref[...] += dg

    def row(w):
        return pl.BlockSpec((tr, w), lambda i: (i, 0))

    def vec(w):
        return pl.BlockSpec((1, w), lambda i: (0, 0))

    def rs(w):
        return jax.ShapeDtypeStruct((s, w), F32)

    def vs(w):
        return jax.ShapeDtypeStruct((1, w), F32)

    return pl.pallas_call(
        body, grid=(s // tr,),
        in_specs=[row(D_MIX), row(GW), row(GW), row(MW), row(CONV_CH), row(D_MIX), vec(GDV), vec(MW), vec(CONV_CH)],
        out_specs=[row(D_MIX), row(GW), row(MW), row(CONV_CH), vec(GDV), vec(MW), vec(CONV_CH)],
        out_shape=[jax.ShapeDtypeStruct((s, D_MIX), MXU),
                   rs(GW), rs(MW), rs(CONV_CH), vs(GDV), vs(MW), vs(CONV_CH)],
        name=name, compiler_params=_cp(("arbitrary",)))(d_y, o_f, o_b, o_att, pre, proj, gng, mog, cog)


def _assemble_dproj(d_z, d_cb, d_cc, d_cx, d_mkv, dv_f, dv_b, dq_f, dq_b, dk_f, dk_b, d_mq, dt_m, dt_f, dt_b, name):
    s = d_z.shape[0]
    tr = _rows_light(s)

    def body(dz, dcb, dcc, dcx, dmkv, dvf, dvb, dqf, dqb, dkf, dkb, dmq, dtm, dtf, dtb, out):
        dt = out.dtype
        out[:, OZ:OZ + D_MIX] = dz[...].astype(dt)
        out[:, OCB:OCB + CONV_CH] = dcb[...].astype(dt)
        out[:, OCC:OCC + CONV_CH] = dcc[...].astype(dt)
        out[:, OCX:OCX + CONV_CH] = dcx[...].astype(dt)
        out[:, OMKV:OMKV + MKVL] = dmkv[...].astype(dt)
        out[:, OGV:OGV + GW] = (dvf[...] + dvb[...]).astype(dt)
        out[:, OGQ:OGQ + GQK] = (dqf[...] + dqb[...]).astype(dt)
        out[:, OGK:OGK + GQK] = (dkf[...] + dkb[...]).astype(dt)
        out[:, OMQ:OMQ + MQL] = dmq[...].astype(dt)
        out[:, OT:OT + LANES] = (dtm[...] + dtf[...] + dtb[...]).astype(dt)

    args = (d_z, d_cb, d_cc, d_cx, d_mkv, dv_f, dv_b, dq_f, dq_b, dk_f, dk_b, d_mq, dt_m, dt_f, dt_b)
    return pl.pallas_call(
        body, grid=(s // tr,),
        in_specs=[pl.BlockSpec((tr, a.shape[1]), lambda i: (i, 0)) for a in args],
        out_specs=pl.BlockSpec((tr, PW), lambda i: (i, 0)),
        out_shape=jax.ShapeDtypeStruct((s, PW), MXU), name=name, compiler_params=_cp(("parallel",)))(*args)


def _layer_fwd(x, mod, wt, cos, sin, tag, late=None, in_after=(), h=None):
    shift, scale, gate = mod
    if h is None:
        h = _norm_mod(x, wt["norm_g"], scale, shift, f"norm_mod_{tag}")
    (proj,) = _matmul(h, wt["w_in"], dims="nn", tm=2048, tn=256, tk=2048, out_dtypes=(F32,), name=f"in_proj_{tag}",
                      after=in_after)
    if late is not None:
        wt.update(late(proj))
    o_f, st_f = _gla_fwd(proj, wt["wg_pad_f"], wt["bg_f"], False, f"gla_fwd_f_{tag}")
    o_b, st_b = _gla_fwd(proj, wt["wg_pad_b"], wt["bg_b"], True, f"gla_fwd_b_{tag}")
    q, k, v = _mla_prep(proj, cos, sin, wt["q_norm_g"], wt["kv_norm_g"], wt["w_uq"], wt["w_ukv"], f"mla_prep_{tag}")
    o_att, lse = _attn_fwd(q, k, v, f"attn_fwd_{tag}")
    pre = _conv_fwd(proj, wt["conv_w"], f"conv_fwd_{tag}")
    y = _merge_fwd(o_f, o_b, o_att, pre, proj, wt["gla_norm_g"], wt["mla_out_g"], wt["conv_out_g"], f"merge_fwd_{tag}")
    x_new, u = _matmul(y, wt["w_out"], dims="nn", tm=2048, tn=256, tk=2048, out_dtypes=(F32, F32),
                       name=f"out_proj_{tag}", epilogue=lambda acc, xv, gv: (xv + gv * acc, acc),
                       extras=(x, gate), extra_kinds=("mn", "n"))
    saved = dict(x=x, h=h, proj=proj, o_f=o_f, o_b=o_b, st_f=st_f, st_b=st_b, q=q, k=k, v=v,
                 o_att=o_att, lse=lse, pre=pre, y=y, u=u)
    return x_new, saved


def _layer_bwd(d_out, sv, mod, wt, cos, sin, tag, ship=None, dx_first=None, ship_rest=None):
    shift, scale, gate = mod
    proj = sv["proj"]
    d_u, d_gate = _gate_bwd(d_out, sv["u"], gate, f"gate_bwd_{tag}")
    (g_w_out,) = _matmul(sv["y"], d_u, dims="tn", tm=1024, tn=512, tk=2048, out_dtypes=(MXU,), name=f"out_proj_dw_{tag}")
    (d_y,) = _matmul(d_u, wt["w_out"], dims="nt", tm=2048, tn=256, tk=2048, out_dtypes=(F32,), name=f"out_proj_dx_{tag}",
                     after=(g_w_out,))
    d_z, d_osum, d_oatt, d_pre, d_gng, d_mog, d_cog = _merge_bwd(
        d_y, sv["o_f"], sv["o_b"], sv["o_att"], sv["pre"], proj, wt["gla_norm_g"], wt["mla_out_g"], wt["conv_out_g"],
        f"merge_bwd_{tag}")
    d_cb, d_cc, d_cx, d_conv_w = _conv_bwd(proj, wt["conv_w"], d_pre, f"conv_bwd_{tag}")
    d_q, d_k, d_v = _attn_bwd(sv["q"], sv["k"], sv["v"], sv["o_att"], sv["lse"], d_oatt, f"attn_bwd_{tag}")
    d_mq, d_mkv, dt_m, g_w_uq, g_w_ukv, d_qg, d_kvg = _mla_prep_bwd(
        proj, cos, sin, wt["q_norm_g"], wt["kv_norm_g"], wt["w_uq"], wt["w_ukv"], d_q, d_k, d_v, f"mla_prep_bwd_{tag}")
    bg_f, bg_b = wt["bg_f"], wt["bg_b"]
    if ship_rest is not None:
        tok = ship_rest(dict(w_out=g_w_out, w_uq=g_w_uq, w_ukv=g_w_ukv))
        bg_f, bg_b = bg_f + tok, bg_b + tok
    dq_f, dk_f, dv_f, dt_f, d_wg_f, d_bg_f = _gla_bwd(proj, wt["wg_pad_f"], bg_f, sv["st_f"], d_osum, False,
                                                     f"gla_bwd_f_{tag}")
    dq_b, dk_b, dv_b, dt_b, d_wg_b, d_bg_b = _gla_bwd(proj, wt["wg_pad_b"], bg_b, sv["st_b"], d_osum, True,
                                                     f"gla_bwd_b_{tag}")
    d_proj = _assemble_dproj(d_z, d_cb, d_cc, d_cx, d_mkv, dv_f, dv_b, dq_f, dq_b, dk_f, dk_b, d_mq, dt_m, dt_f, dt_b,
                             f"assemble_dproj_{tag}")
    grads = dict(w_out=g_w_out, w_uq=g_w_uq, w_ukv=g_w_ukv,
                 wg_pad_f=d_wg_f, bg_f=d_bg_f, wg_pad_b=d_wg_b, bg_b=d_bg_b, gla_norm_g=d_gng,
                 q_norm_g=d_qg, kv_norm_g=d_kvg, mla_out_g=d_mog, conv_w=d_conv_w, conv_out_g=d_cog)

    def in_dw(after):
        (g_w_in,) = _matmul(sv["h"], d_proj, dims="tn", tm=2048, tn=256, tk=2048, out_dtypes=(MXU,),
                            name=f"in_proj_dw_{tag}", after=after)
        grads["w_in"] = g_w_in
        return dict(w_in=g_w_in, w_out=g_w_out, w_uq=g_w_uq, w_ukv=g_w_ukv)

    def in_dx(after):
        (d_h,) = _matmul(d_proj, wt["w_in"], dims="nt", tm=1024, tn=512, tk=PW, out_dtypes=(F32,),
                         name=f"in_proj_dx_{tag}", after=after)
        d_x, d_shift, d_scale, d_ng = _norm_mod_bwd(d_h, sv["x"], d_out, wt["norm_g"], scale, f"norm_mod_bwd_{tag}")
        grads["norm_g"] = d_ng
        return d_x, (d_shift, d_scale, d_gate)

    if dx_first is None:
        big = in_dw(())
        d_x, d_mod = in_dx((big["w_in"],) if ship is None else ship(big))
    else:
        d_x, d_mod = in_dx(())
        big = in_dw(dx_first(d_x, d_mod, grads))
        ship(big)
    return d_x, d_mod, grads


IN_SEGS = ((3808, 5856), (2272, 3808), (1952, 2208), (768, 1536), (0, 768), (1568, 1952), (2208, 2272), (1536, 1568))
UQ_SEGS = (tuple((h * (MNOPE + MROPE), h * (MNOPE + MROPE) + MNOPE) for h in range(MH))
           + tuple((h * (MNOPE + MROPE) + MNOPE, (h + 1) * (MNOPE + MROPE)) for h in range(MH)))


def _heads_apart(w):
    w3 = w.reshape(w.shape[:-1] + (MH, MNOPE + MROPE))
    return jnp.concatenate([w3[..., :MNOPE].reshape(w.shape[:-1] + (MH * MNOPE,)),
                            w3[..., MNOPE:].reshape(w.shape[:-1] + (MH * MROPE,))], axis=-1)


def _heads_together(g):
    nope = g[..., :MH * MNOPE].reshape(g.shape[:-1] + (MH, MNOPE))
    rope = g[..., MH * MNOPE:].reshape(g.shape[:-1] + (MH, MROPE))
    return jnp.concatenate([nope, rope], axis=-1).reshape(g.shape[:-1] + (MQW,))


def _perm_gathered(g, segs, width):
    per = g.shape[-1]
    parts, total = [], 0
    for a, b in segs:
        c = a
        while c < b:
            j = c // per
            hi = min(b, (j + 1) * per)
            parts.append(g[j, :, c - j * per:hi - j * per])
            c = hi
        total += b - a
    if width > total:
        parts.append(jnp.zeros((g.shape[1], width - total), g.dtype))
    return jnp.concatenate(parts, axis=1)


def _scatter_perm(gp, segs, per):
    offs, o = [], 0
    for a, b in segs:
        offs.append((a, b, o))
        o += b - a
    blocks = []
    for j in range(N_DEV):
        lo, hi = j * per, (j + 1) * per
        pieces = []
        for a, b, o in sorted(offs):
            s0, s1 = max(a, lo), min(b, hi)
            if s0 < s1:
                pieces.append(gp[:, o + s0 - a:o + s1 - a])
        blocks.append(jnp.concatenate(pieces, axis=1))
    return jnp.stack(blocks)


def _prep_layer_weights(w_in, w_out, w_uq, w_ukv, small):
    def vec(v):
        return v.reshape(1, -1).astype(F32)

    zeros = functools.partial(jnp.zeros, dtype=F32)
    wg_f, wg_b = small["gla_wg_f"].astype(F32), small["gla_wg_b"].astype(F32)
    wg_pad_f = jnp.concatenate([zeros((MROPE, GQK)), wg_f, zeros((LANES - MROPE - GRANK, GQK))], axis=0)
    wg_pad_b = jnp.concatenate([zeros((MROPE + GRANK, GQK)), wg_b, zeros((LANES - MROPE - 2 * GRANK, GQK))], axis=0)
    wt = dict(norm_g=vec(small["norm_g"]), wg_pad_f=wg_pad_f, wg_pad_b=wg_pad_b,
              bg_f=vec(small["gla_bg_f"]), bg_b=vec(small["gla_bg_b"]), gla_norm_g=vec(small["gla_norm_g"]),
              q_norm_g=vec(small["mla_q_norm_g"]), kv_norm_g=vec(small["mla_kv_norm_g"]),
              mla_out_g=vec(small["mla_out_g"]), conv_w=small["conv_w"].astype(F32),
              conv_out_g=vec(small["conv_out_g"]))
    for name, w in (("w_in", w_in), ("w_out", w_out), ("w_uq", w_uq), ("w_ukv", w_ukv)):
        if w is not None:
            wt[name] = w.astype(MXU)
    return wt


def _natural_small(gr):
    return dict(norm_g=gr["norm_g"][0],
                gla_wg_f=gr["wg_pad_f"][MROPE:MROPE + GRANK], gla_bg_f=gr["bg_f"][0],
                gla_wg_b=gr["wg_pad_b"][MROPE + GRANK:MROPE + 2 * GRANK], gla_bg_b=gr["bg_b"][0],
                gla_norm_g=gr["gla_norm_g"][0], mla_q_norm_g=gr["q_norm_g"][0], mla_kv_norm_g=gr["kv_norm_g"][0],
                mla_out_g=gr["mla_out_g"][0], conv_w=gr["conv_w"], conv_out_g=gr["conv_out_g"][0])


def _exchange(arrs, name, scatter, space):
    n = len(arrs)

    def body(*refs):
        ins, outs = refs[:n], refs[n:2 * n]
        send_sems, recv_sems, loc_sems = refs[2 * n:]
        ax, ay, ac = lax.axis_index("x"), lax.axis_index("y"), lax.axis_index("c")
        me = 4 * ax + 2 * ay + ac

        def src(a, to):
            return ins[a].at[to] if scatter else ins[a]

        def remote(a, r, dst_slot):
            px = 1 - ax if r & 4 else ax
            py = 1 - ay if r & 2 else ay
            pc = 1 - ac if r & 1 else ac
            return pltpu.make_async_remote_copy(
                src_ref=src(a, 4 * px + 2 * py + pc), dst_ref=outs[a].at[dst_slot(4 * px + 2 * py + pc)],
                send_sem=send_sems.at[a, r - 1], recv_sem=recv_sems.at[a, r - 1],
                device_id=(px, py, pc), device_id_type=MESH)

        locs = [pltpu.make_async_copy(src(a, me), outs[a].at[me], loc_sems.at[a]) for a in range(n)]
        for cp in locs:
            cp.start()
        sends = [remote(a, r, lambda peer: me) for r in range(1, N_DEV) for a in range(n)]
        for cp in sends:
            cp.start()
        for r in range(1, N_DEV):
            for a in range(n):
                remote(a, r, lambda peer: peer).wait_recv()
        for cp in sends:
            cp.wait_send()
        for cp in locs:
            cp.wait()

    def out_shape(a):
        return jax.ShapeDtypeStruct(a.shape if scatter else (N_DEV,) + a.shape, a.dtype)

    spec = pl.BlockSpec(memory_space=space)
    return pl.pallas_call(
        body, in_specs=[spec] * n, out_specs=[spec] * n, out_shape=[out_shape(a) for a in arrs],
        scratch_shapes=[pltpu.SemaphoreType.DMA((n, N_DEV - 1)), pltpu.SemaphoreType.DMA((n, N_DEV - 1)),
                        pltpu.SemaphoreType.DMA((n,))],
        name=name, compiler_params=pltpu.CompilerParams(vmem_limit_bytes=VMEM_LIMIT))(*arrs)


def _peer(r):
    ax, ay, ac = lax.axis_index("x"), lax.axis_index("y"), lax.axis_index("c")
    px = 1 - ax if r & 4 else ax
    py = 1 - ay if r & 2 else ay
    pc = 1 - ac if r & 1 else ac
    return (px, py, pc), 4 * px + 2 * py + pc


def _slot(rel_div):
    rel, div = rel_div
    idx = _peer(rel)[1]
    return idx if div == 1 else idx // div


AG_SPREAD = tuple((r, None, (0, 1), (r, 1)) for r in (1, 2, 4, 6))
AG_FORWARD = tuple((1, (k, 1), (k, 1), (1 ^ k, 1)) for k in (2, 4, 6))
RS_PAIR = tuple((1, (1 ^ k, 1), (1 ^ k, 2), (k, 2)) for k in (0, 2, 4, 6))
RS_CHIPS = tuple((r, (r, 2), (0, 2), (r, 2)) for r in (2, 4, 6))


def _plan_copies(plan, n, src_refs, land_refs, send_sems, recv_sems, arriving):
    out = []
    for i, (r, src, dst, recv) in enumerate(plan):
        peer = _peer(r)[0]
        for a in range(n):
            out.append(pltpu.make_async_remote_copy(
                src_ref=src_refs[a] if src is None else src_refs[a].at[_slot(src)],
                dst_ref=land_refs[a].at[_slot(recv if arriving else dst)],
                send_sem=send_sems.at[i * n + a], recv_sem=recv_sems.at[i * n + a],
                device_id=peer, device_id_type=MESH))
    return out


def _exchange_hbm(plan, srcs, lands, name, after=()):
    n = len(lands)
    fresh = isinstance(lands[0], jax.ShapeDtypeStruct)
    ins = ([] if srcs is None else list(srcs)) + ([] if fresh else list(lands))
    ns = 0 if srcs is None else n
    n_data = len(ins)
    ins = ins + list(after)

    def body(*refs):
        outs = refs[len(ins):len(ins) + n]
        send_sems, recv_sems = refs[-2:]
        src_refs = refs[:n] if srcs is not None else refs[ns:ns + n]
        sends = _plan_copies(plan, n, src_refs, outs, send_sems, recv_sems, False)
        for cp in sends:
            cp.start()
        for cp in _plan_copies(plan, n, src_refs, outs, send_sems, recv_sems, True):
            cp.wait_recv()
        for cp in sends:
            cp.wait_send()

    hbm = pl.BlockSpec(memory_space=pltpu.HBM)
    k = len(plan) * n
    return pl.pallas_call(
        body, name=name, in_specs=[hbm] * n_data + [pl.BlockSpec(memory_space=pl.ANY)] * len(after), out_specs=[hbm] * n,
        out_shape=[jax.ShapeDtypeStruct(a.shape, a.dtype) for a in lands],
        scratch_shapes=[pltpu.SemaphoreType.DMA((k,)), pltpu.SemaphoreType.DMA((k,))],
        input_output_aliases={} if fresh else {ns + i: i for i in range(n)},
        compiler_params=pltpu.CompilerParams(vmem_limit_bytes=VMEM_LIMIT))(*ins)


def _plan_start(plan, srcs, land_shapes, after, name):
    n = len(srcs)

    def body(*refs):
        src_refs, land_refs = refs[:n], refs[n:2 * n]
        send_sems, recv_sems = refs[2 * n + 1], refs[2 * n + 2]
        for cp in _plan_copies(plan, n, src_refs, land_refs, send_sems, recv_sems, False):
            cp.start()
        refs[-1][...] = jnp.zeros_like(refs[-1])

    hbm = pl.BlockSpec(memory_space=pltpu.HBM)
    sem = pl.BlockSpec(memory_space=pltpu.SEMAPHORE)
    k = len(plan) * n
    srcs = [pltpu.with_memory_space_constraint(a, pltpu.HBM) for a in srcs]
    lands = [pltpu.with_memory_space_constraint(lax.empty(shp, a.dtype), pltpu.HBM) for shp, a in zip(land_shapes, srcs)]
    res = pl.pallas_call(
        body, name=name,
        in_specs=[hbm] * (2 * n) + [pl.BlockSpec(memory_space=pl.ANY)],
        out_specs=[sem, sem] + [hbm] * (2 * n) + [pl.BlockSpec(memory_space=pltpu.VMEM)],
        out_shape=[pltpu.SemaphoreType.DMA((k,)), pltpu.SemaphoreType.DMA((k,))]
        + [pltpu.HBM(a.shape, a.dtype) for a in srcs] + [pltpu.HBM(shp, a.dtype) for shp, a in zip(land_shapes, srcs)]
        + [jax.ShapeDtypeStruct((8, LANES), F32)],
        input_output_aliases={i: 2 + i for i in range(2 * n)},
        compiler_params=pltpu.CompilerParams(has_side_effects=pltpu.SideEffectType.DATAFLOW_SIDE_EFFECTING),
    )(*srcs, *lands, after)
    return res[0], res[1], list(res[2:2 + n]), list(res[2 + n:2 + 2 * n]), res[-1]


def _plan_wait(plan, handle, after, name):
    send_sems, recv_sems, srcs, lands, _ = handle
    n = len(srcs)
    after = list(after)

    def body(*refs):
        src_refs, land_refs = refs[:n], refs[n:2 * n]
        ssem, rsem = refs[2 * n], refs[2 * n + 1]
        for cp in _plan_copies(plan, n, src_refs, land_refs, ssem, rsem, False):
            cp.wait_send()
        for cp in _plan_copies(plan, n, src_refs, land_refs, ssem, rsem, True):
            cp.wait_recv()

    hbm = pl.BlockSpec(memory_space=pltpu.HBM)
    sem = pl.BlockSpec(memory_space=pltpu.SEMAPHORE)
    res = pl.pallas_call(
        body, name=name,
        in_specs=[hbm] * (2 * n) + [sem, sem] + [pl.BlockSpec(memory_space=pl.ANY)] * len(after),
        out_specs=[hbm] * (2 * n),
        out_shape=[pltpu.HBM(a.shape, a.dtype) for a in srcs] + [pltpu.HBM(a.shape, a.dtype) for a in lands],
        input_output_aliases={i: i for i in range(2 * n)},
        compiler_params=pltpu.CompilerParams(has_side_effects=pltpu.SideEffectType.DATAFLOW_SIDE_EFFECTING),
    )(*srcs, *lands, send_sems, recv_sems, *after)
    return list(res[:n]), list(res[n:])


def _pair_sum(send, got, core, name):
    _, r, c = send.shape
    tr = 1024 if r % 1024 == 0 else r

    def body(core_ref, s_ref, g_ref, o_ref):
        o_ref[0] = (s_ref[0].astype(F32) + g_ref[0].astype(F32)).astype(o_ref.dtype)

    return pl.pallas_call(
        body, name=name,
        grid_spec=pltpu.PrefetchScalarGridSpec(
            num_scalar_prefetch=1, grid=(N_DEV // 2, r // tr),
            in_specs=[pl.BlockSpec((1, tr, c), lambda kc, i, core_ref: (2 * kc + core_ref[0], i, 0)),
                      pl.BlockSpec((1, tr, c), lambda kc, i, core_ref: (kc, i, 0))],
            out_specs=pl.BlockSpec((1, tr, c), lambda kc, i, core_ref: (kc, i, 0))),
        out_shape=jax.ShapeDtypeStruct((N_DEV // 2, r, c), send.dtype),
        compiler_params=_cp(("parallel", "parallel")))(core, send, got)


def _ada_mod(c_all, ada_w, ada_b_cols, name):
    nl, d, wc = ada_w.shape

    def body(c_ref, w_ref, b_ref, ca_ref, mod_ref):
        cv = c_ref[...]
        ca = cv * _sigmoid(cv)
        ca_ref[...] = ca
        mod_ref[0] = _dotf(ca, w_ref[0]) + b_ref[0]

    return pl.pallas_call(
        body, grid=(nl,),
        in_specs=[pl.BlockSpec((N_DEV, d), lambda l: (0, 0)), pl.BlockSpec((1, d, wc), lambda l: (l, 0, 0)),
                  pl.BlockSpec((1, 1, wc), lambda l: (l, 0, 0))],
        out_specs=[pl.BlockSpec((N_DEV, d), lambda l: (0, 0)), pl.BlockSpec((1, N_DEV, wc), lambda l: (l, 0, 0))],
        out_shape=[jax.ShapeDtypeStruct((N_DEV, d), F32), jax.ShapeDtypeStruct((nl, N_DEV, wc), F32)],
        name=name, compiler_params=_cp(("arbitrary",)))(c_all, ada_w, ada_b_cols)


def _adam(w, g, m, v):
    m2 = ADAM_B1 * m + (1.0 - ADAM_B1) * g
    v2 = ADAM_B2 * v + (1.0 - ADAM_B2) * (g * g)
    m_hat = m2 / (1.0 - ADAM_B1 ** ADAM_STEP)
    v_hat = v2 / (1.0 - ADAM_B2 ** ADAM_STEP)
    delta = -ADAM_LR * (m_hat / (jnp.sqrt(v_hat) + ADAM_EPS) + ADAM_WD * w)
    return delta, m2, v2


def _ada_grad_adam(c_act, d_mod, w, m, v, name):
    nl, d, wc = w.shape
    tk = min(1024, d)

    def body(c_ref, dm_ref, w_ref, m_ref, v_ref, g_ref, dl_ref, m2_ref, v2_ref):
        g = _dotf_tn(c_ref[...], dm_ref[0])
        delta, m2, v2 = _adam(w_ref[0], g, m_ref[0], v_ref[0])
        g_ref[0], dl_ref[0], m2_ref[0], v2_ref[0] = g, delta, m2, v2

    blk = pl.BlockSpec((1, tk, wc), lambda l, i: (l, i, 0))
    shp = jax.ShapeDtypeStruct(w.shape, F32)
    return pl.pallas_call(
        body, grid=(nl, d // tk),
        in_specs=[pl.BlockSpec((N_DEV, tk), lambda l, i: (0, i)), pl.BlockSpec((1, N_DEV, wc), lambda l, i: (l, 0, 0)),
                  blk, blk, blk],
        out_specs=[blk] * 4, out_shape=[shp] * 4, name=name,
        compiler_params=_cp(("parallel", "parallel")))(c_act, d_mod, w, m, v)


def _adam_big(recv, w, m, v, layer, prev, name, after=()):
    nl, r, c = w.shape
    tr = 512 if r % 512 == 0 else r
    nparts = recv.shape[0]

    def body(rc_ref, w_ref, m_ref, v_ref, *rest):
        g_ref, dl_ref, m2_ref, v2_ref = rest[-4:]
        g = rc_ref[0].astype(F32)
        for d in range(1, nparts):
            g = g + rc_ref[d].astype(F32)
        delta, m2, v2 = _adam(w_ref[0], g, m_ref[0], v_ref[0])
        g_ref[0], dl_ref[0], m2_ref[0], v2_ref[0] = g, delta, m2, v2

    blk = pl.BlockSpec((1, tr, c), lambda i: (layer, i, 0))
    shp = jax.ShapeDtypeStruct(w.shape, F32)
    prev = () if prev is None else tuple(prev)
    return pl.pallas_call(
        body, grid=(r // tr,),
        in_specs=[pl.BlockSpec((nparts, tr, c), lambda i: (0, i, 0)), blk, blk, blk]
        + [pl.BlockSpec(memory_space=pl.ANY)] * (len(prev) + len(after)),
        out_specs=[blk] * 4, out_shape=[shp] * 4, name=name,
        input_output_aliases={4 + j: j for j in range(len(prev))},
        compiler_params=_cp(("parallel",)))(recv, w, m, v, *prev, *after)


def _sum_devices(gathered, name):
    _, r, c = gathered.shape

    def body(g_ref, o_ref):
        acc = g_ref[0]
        for d in range(1, N_DEV):
            acc = acc + g_ref[d]
        o_ref[...] = acc

    spec = pl.BlockSpec(memory_space=pltpu.VMEM)
    return pl.pallas_call(body, in_specs=[spec], out_specs=spec, out_shape=jax.ShapeDtypeStruct((r, c), F32),
                          name=name, compiler_params=pltpu.CompilerParams(vmem_limit_bytes=VMEM_LIMIT))(gathered)


def _adam_small(ws, gs, ms, vs, name):
    n = len(ws)

    def body(*refs):
        for i in range(n):
            w_ref, g_ref, m_ref, v_ref = (refs[k * n + i] for k in range(4))
            dl_ref, m2_ref, v2_ref = (refs[(4 + k) * n + i] for k in range(3))
            dl_ref[...], m2_ref[...], v2_ref[...] = _adam(w_ref[...], g_ref[...], m_ref[...], v_ref[...])

    spec = pl.BlockSpec(memory_space=pltpu.VMEM)
    shapes = [jax.ShapeDtypeStruct(w.shape, F32) for w in ws]
    res = pl.pallas_call(body, in_specs=[spec] * (4 * n), out_specs=[spec] * (3 * n), out_shape=shapes * 3, name=name,
                         compiler_params=pltpu.CompilerParams(vmem_limit_bytes=VMEM_LIMIT))(*ws, *gs, *ms, *vs)
    return res[:n], res[n:2 * n], res[2 * n:]


def _pack(parts):
    flat = jnp.concatenate([p.reshape(-1).astype(F32) for p in parts])
    assert flat.shape[0] % LANES == 0, flat.shape
    return flat.reshape(-1, LANES)


def _unpack(packed, shapes):
    flat = packed.reshape(-1)
    out, off = [], 0
    for shp in shapes:
        size = 1
        for dim in shp:
            size *= dim
        out.append(flat[off:off + size].reshape(shp))
        off += size
    return out


def _gather_cols(g, per):
    g = jnp.moveaxis(g, 0, -2)
    return g.reshape(g.shape[:-2] + (N_DEV * per,))


def _scatter_cols(g, per):
    return jnp.moveaxis(g.reshape(g.shape[:-1] + (N_DEV, per)), -2, 0)


def _my_cols(full, me, per):
    return lax.dynamic_slice_in_dim(full, me * per, per, axis=full.ndim - 1)


def kernel(x, c, positions, ada_w, ada_b, norm_g, w_in, gla_wg_f, gla_bg_f, gla_wg_b, gla_bg_b, gla_norm_g, mla_q_norm_g, mla_kv_norm_g, mla_w_uq, mla_w_ukv, mla_out_g, conv_w, conv_out_g, w_out, final_g, loss_target, m_ada_w, m_ada_b, m_norm_g, m_w_in, m_gla_wg_f, m_gla_bg_f, m_gla_wg_b, m_gla_bg_b, m_gla_norm_g, m_mla_q_norm_g, m_mla_kv_norm_g, m_mla_w_uq, m_mla_w_ukv, m_mla_out_g, m_conv_w, m_conv_out_g, m_w_out, m_final_g, v_ada_w, v_ada_b, v_norm_g, v_w_in, v_gla_wg_f, v_gla_bg_f, v_gla_wg_b, v_gla_bg_b, v_gla_norm_g, v_mla_q_norm_g, v_mla_kv_norm_g, v_mla_w_uq, v_mla_w_ukv, v_mla_out_g, v_conv_w, v_conv_out_g, v_w_out, v_final_g):
    me = 4 * lax.axis_index("x") + 2 * lax.axis_index("y") + lax.axis_index("c")
    nl = ada_w.shape[0]
    s, d = x.shape[1], x.shape[2]
    ada_cols = ada_w.shape[2]
    wgc, cwc = gla_wg_f.shape[2], conv_w.shape[2]

    (g0,) = _exchange([_pack([c, gla_wg_f, gla_wg_b, conv_w])], "gather_small_in", False, pltpu.VMEM)
    g0 = g0.reshape(N_DEV, -1)
    o1, o2, o3 = d, d + gla_wg_f.size, d + 2 * gla_wg_f.size
    c_all = g0[:, :o1]
    wgf_full = _gather_cols(g0[:, o1:o2].reshape((N_DEV,) + gla_wg_f.shape), wgc)
    wgb_full = _gather_cols(g0[:, o2:o3].reshape((N_DEV,) + gla_wg_b.shape), wgc)
    convw_full = _gather_cols(g0[:, o3:].reshape((N_DEV,) + conv_w.shape), cwc)

    ada_b_cols = _my_cols(ada_b, me, ada_cols).reshape(nl, 1, ada_cols)
    c_act, mod_cols = _ada_mod(c_all, ada_w, ada_b_cols, "ada_mod")
    (g1,) = _exchange([_pack([mod_cols])], "gather_mod", False, pltpu.VMEM)
    mod_all = g1.reshape(N_DEV, nl, N_DEV, ada_cols)
    mod_mine = _gather_cols(lax.dynamic_index_in_dim(mod_all, me, axis=2, keepdims=False), ada_cols)

    inv_freq = ROPE_THETA ** (-jnp.arange(0, MROPE, 2, dtype=F32) / MROPE)
    ang = positions[0].astype(F32)[:, None] * inv_freq
    cos, sin = jnp.tile(jnp.cos(ang), (1, LANES * 2 // MROPE)), jnp.tile(jnp.sin(ang), (1, LANES * 2 // MROPE))

    big = [w_in, w_out, mla_w_uq, mla_w_ukv]
    big_names = ["w_in", "w_out", "mla_w_uq", "mla_w_ukv"]

    def local_blocks(l):
        return [w[l].astype(MXU) for w in big]

    def put_own(lands, own):
        return [lax.dynamic_update_index_in_dim(ld, o, me, 0) for ld, o in zip(lands, own)]

    def layer_weights(l, gw_in=None, gw_out=None, gw_uq=None, gw_ukv=None):
        small = dict(norm_g=norm_g[l], gla_wg_f=wgf_full[l], gla_bg_f=gla_bg_f[l], gla_wg_b=wgb_full[l],
                     gla_bg_b=gla_bg_b[l], gla_norm_g=gla_norm_g[l], mla_q_norm_g=mla_q_norm_g[l],
                     mla_kv_norm_g=mla_kv_norm_g[l], mla_out_g=mla_out_g[l], conv_w=convw_full[l],
                     conv_out_g=conv_out_g[l])
        return _prep_layer_weights(
            None if gw_in is None else _perm_gathered(gw_in, IN_SEGS, PW),
            None if gw_out is None else gw_out.reshape((-1,) + gw_out.shape[2:]),
            None if gw_uq is None else _heads_apart(_gather_cols(gw_uq, mla_w_uq.shape[2])),
            None if gw_ukv is None else _gather_cols(gw_ukv, mla_w_ukv.shape[2]), small)

    def land_shapes(blocks, slots):
        return [jax.ShapeDtypeStruct((slots,) + b.shape, b.dtype) for b in blocks]

    def slots_of(blocks):
        return [(N_DEV,) + b.shape for b in blocks]

    def forwarded(lands, blocks, tag):
        return put_own(_exchange_hbm(AG_FORWARD, None, lands, f"gather_{tag}_forward"), blocks)

    first = local_blocks(0)
    w_in_start = _plan_start(AG_SPREAD, first[:1], slots_of(first[:1]), mod_mine, "gather_w_in_l0_start")
    adam_w_in = [a + w_in_start[-1][0, 0] for a in (w_in, m_w_in, v_w_in)]
    shift0, scale0 = (mod_mine[0, i * d:(i + 1) * d].reshape(1, d) for i in range(2))
    h_first = _norm_mod(x[0], norm_g[0].reshape(1, d), scale0 + w_in_start[-1][0, 0], shift0, "norm_mod_l0")
    (gw_in,) = forwarded(*reversed(_plan_wait(AG_SPREAD, w_in_start, adam_w_in + [h_first], "gather_w_in_l0_wait")),
                         "w_in_l0")
    rest = _plan_start(AG_SPREAD, first[1:], slots_of(first[1:]), gw_in, "gather_rest_l0_start")
    h = x[0]
    saved, layers, mods = [], [], []
    pending = {}
    for l in range(nl):
        shift, scale, gate = (mod_mine[l, i * d:(i + 1) * d].reshape(1, d) for i in range(3))
        nxt = local_blocks(l + 1) if l + 1 < nl else None

        def start_next(after, wt_late, l=l, nxt=nxt):
            if nxt is not None:
                pending[l + 1] = _plan_start(AG_SPREAD, nxt, slots_of(nxt), after, f"gather_weights_l{l + 1}_start")
                wt_late["q_norm_g"] = layers[l]["q_norm_g"] + pending[l + 1][-1][0, 0]
            return wt_late

        if l == 0:
            in_after = (rest[-1],)
            layers.append(layer_weights(0, gw_in))

            def late(proj):
                got = forwarded(*reversed(_plan_wait(AG_SPREAD, rest, [proj], "gather_rest_l0_wait")), "rest_l0")
                full = layer_weights(0, None, *got)
                return start_next(got[0], {k: full[k] for k in ("w_out", "w_uq", "w_ukv")})
        else:
            got = forwarded(*reversed(_plan_wait(AG_SPREAD, pending.pop(l), [h], f"gather_weights_l{l}_wait")), f"weights_l{l}")
            layers.append(layer_weights(l, *got))
            in_after = ()

            def late(proj):
                return start_next(proj, {})
        mods.append((shift, scale, gate))
        h, sv = _layer_fwd(h, mods[l], layers[l], cos, sin, f"l{l}", late, in_after, h_first if l == 0 else None)
        saved.append(sv)
        blocks = nxt
    loss_part, d_h, d_final_g = _final_loss(h, final_g.reshape(1, d), loss_target[0], "final_loss")

    send_of = dict(w_in=lambda g: _scatter_perm(g, IN_SEGS, w_in.shape[2]),
                   w_out=lambda g: g.reshape((N_DEV,) + w_out.shape[1:]),
                   w_uq=lambda g: _scatter_cols(_heads_together(g), mla_w_uq.shape[2]),
                   w_ukv=lambda g: _scatter_cols(g, mla_w_ukv.shape[2]))

    def grad_sends(gr):
        return [send_of[k](g).astype(MXU) for k, g in gr.items()]

    my_chip = me // 2
    my_core = (me % 2).astype(jnp.int32).reshape(1)

    def chip_sums(gr, tag):
        sends = grad_sends(gr)
        got = _exchange_hbm(RS_PAIR, sends, land_shapes([sd[0] for sd in sends], N_DEV // 2), f"scatter_grads_{tag}_pair")
        return [_pair_sum(sd, gt, my_core, f"pair_sum_{k}_{tag}") for sd, gt, k in zip(sends, got, gr)]

    def with_own_chip(lands, sums):
        return [lax.dynamic_update_index_in_dim(ld, lax.dynamic_index_in_dim(sm, my_chip, axis=0, keepdims=False),
                                                my_chip, 0) for ld, sm in zip(lands, sums)]

    small_names = ["norm_g", "gla_wg_f", "gla_bg_f", "gla_wg_b", "gla_bg_b", "gla_norm_g", "mla_q_norm_g",
                   "mla_kv_norm_g", "mla_out_g", "conv_w", "conv_out_g"]
    d_mods, grads, recv = [None] * nl, [None] * nl, [None] * nl
    flight = {}
    small = {}

    def gather_small(d_x, d_mod0, gr0):
        d_mods[0], grads[0] = d_mod0, _natural_small(gr0)
        d_mod_mine = jnp.stack([jnp.concatenate(d_mods[l], axis=-1)[0] for l in range(nl)])
        parts = ([d_mod_mine] + [jnp.stack([grads[l][n] for l in range(nl)]) for n in small_names]
                 + [d_final_g, loss_part])
        (g2,) = _exchange([_pack(parts)], "gather_small_grads", False, pltpu.VMEM)
        small["d_mod_all"] = g2.reshape(N_DEV, -1)[:, :d_mod_mine.size].reshape(N_DEV, nl, 3 * d)
        small["summed"] = dict(zip(["ada_b"] + small_names + ["final_g", "loss"],
                                   _unpack(_sum_devices(g2, "sum_small_grads"), [p.shape for p in parts])))
        return (g2,)

    pairs = {}
    def end_flight(key, after, name):
        sm, lands = _plan_wait(RS_CHIPS, flight.pop(key)[0], after, name)
        return with_own_chip(lands, sm)

    for l in reversed(range(nl)):
        def ship(big_grads, l=l):
            if l > 0:
                sends = grad_sends(big_grads)
                pairs[l] = (_plan_start(RS_PAIR, sends, [(N_DEV // 2,) + sd.shape[1:] for sd in sends],
                                        big_grads["w_in"], f"scatter_grads_l{l}_pair_start"), sends)
                return (pairs[l][0][-1],)
            sm = chip_sums(dict(w_in=big_grads["w_in"]), f"l{l}")
            flight[l] = (_plan_start(RS_CHIPS, sm, [a.shape for a in sm], big_grads["w_in"], f"scatter_grads_l{l}_start"),
                         sm)
            return (flight[l][0][-1],)

        def ship_rest(rest_grads, l=l):
            if l + 1 in flight:
                recv[l + 1] = end_flight(l + 1, list(rest_grads.values()), f"scatter_grads_l{l + 1}_wait")
            sm = chip_sums(rest_grads, f"l{l}_rest")
            flight["rest"] = (_plan_start(RS_CHIPS, sm, [a.shape for a in sm], rest_grads["w_out"],
                                          f"scatter_grads_l{l}_rest_start"), sm)
            return flight["rest"][0][-1][0, 0]

        shift, scale, gate = mods[l]
        if l + 1 in flight:
            gate = gate + flight[l + 1][0][-1][0, 0]
        if l > 0:
            d_h, d_mods[l], gr = _layer_bwd(d_h, saved[l], (shift, scale, gate), layers[l], cos, sin, f"l{l}", ship)
            grads[l] = _natural_small(gr)
            sends, got = _plan_wait(RS_PAIR, pairs.pop(l)[0], [d_h], f"scatter_grads_l{l}_pair_wait")
            sm = [_pair_sum(sd, gt, my_core, f"pair_sum_{n}_l{l}") for sd, gt, n in zip(sends, got, big_names)]
            flight[l] = (_plan_start(RS_CHIPS, sm, [a.shape for a in sm], d_h, f"scatter_grads_l{l}_start"), sm)
        else:
            d_h, _, _ = _layer_bwd(d_h, saved[l], (shift, scale, gate), layers[l], cos, sin, f"l{l}", ship, gather_small,
                                   ship_rest)
    pending = flight[0][0]
    grad_x = d_h[None]
    summed = small["summed"]
    loss = summed["loss"][0, 0]
    summed["gla_wg_f"] = _my_cols(summed["gla_wg_f"], me, wgc)
    summed["gla_wg_b"] = _my_cols(summed["gla_wg_b"], me, wgc)
    summed["conv_w"] = _my_cols(summed["conv_w"], me, cwc)

    d_mod_cols = jnp.moveaxis(_my_cols(small["d_mod_all"], me, ada_cols), 0, 1) + pending[-1][0, 0]
    out = {}
    out["ada_w"] = _ada_grad_adam(c_act, d_mod_cols, ada_w, m_ada_w, v_ada_w, "ada_grad_adam")

    given = dict(ada_b=(ada_b, m_ada_b, v_ada_b), norm_g=(norm_g, m_norm_g, v_norm_g),
                 gla_wg_f=(gla_wg_f, m_gla_wg_f, v_gla_wg_f), gla_bg_f=(gla_bg_f, m_gla_bg_f, v_gla_bg_f),
                 gla_wg_b=(gla_wg_b, m_gla_wg_b, v_gla_wg_b), gla_bg_b=(gla_bg_b, m_gla_bg_b, v_gla_bg_b),
                 gla_norm_g=(gla_norm_g, m_gla_norm_g, v_gla_norm_g),
                 mla_q_norm_g=(mla_q_norm_g, m_mla_q_norm_g, v_mla_q_norm_g),
                 mla_kv_norm_g=(mla_kv_norm_g, m_mla_kv_norm_g, v_mla_kv_norm_g),
                 mla_out_g=(mla_out_g, m_mla_out_g, v_mla_out_g), conv_w=(conv_w, m_conv_w, v_conv_w),
                 conv_out_g=(conv_out_g, m_conv_out_g, v_conv_out_g), final_g=(final_g, m_final_g, v_final_g))
    names = list(given)

    def two_d(a):
        return a.reshape(1, -1) if a.ndim == 1 else a

    g_nat = [summed[n].reshape(given[n][0].shape) for n in names]
    res = _adam_small([two_d(given[n][0]) for n in names], [two_d(g) for g in g_nat],
                      [two_d(given[n][1]) for n in names], [two_d(given[n][2]) for n in names], "adam_small")
    for i, n in enumerate(names):
        out[n] = (g_nat[i],) + tuple(r[i].reshape(given[n][0].shape) for r in res)

    state = dict(w_in=adam_w_in, w_out=(w_out, m_w_out, v_w_out), mla_w_uq=(mla_w_uq, m_mla_w_uq, v_mla_w_uq),
                 mla_w_ukv=(mla_w_ukv, m_mla_w_ukv, v_mla_w_ukv))
    done = [out["ada_w"][0], res[0][0]]
    for l in reversed(range(nl)):
        if l == 0:
            rest = end_flight("rest", done, "scatter_grads_l0_rest_wait")
            recv[0] = end_flight(0, done + rest[:1], "scatter_grads_l0_wait") + rest
        for i, n in enumerate(big_names):
            out[n] = _adam_big(recv[l][i], *state[n], l, out.get(n), f"adam_{n}_l{l}",
                               (pending[-1],))
        done = done + [out[n][0] for n in big_names]

    order = ["ada_w", "ada_b", "norm_g", "w_in", "gla_wg_f", "gla_bg_f", "gla_wg_b", "gla_bg_b", "gla_norm_g",
             "mla_q_norm_g", "mla_kv_norm_g", "mla_w_uq", "mla_w_ukv", "mla_out_g", "conv_w", "conv_out_g", "w_out",
             "final_g"]
    return (loss, grad_x, *[out[n][0] for n in order], *[out[n][1] for n in order], *[out[n][2] for n in order],
            *[out[n][3] for n in order])
```

```python
import functools

import jax
import jax.numpy as jnp
from jax import lax
from jax.experimental import pallas as pl
from jax.experimental.pallas import tpu as pltpu

F32 = jnp.float32
MXU = jnp.bfloat16
HI = lax.Precision.HIGHEST
N_DEV = 8
MESH = pl.DeviceIdType.MESH

D_MIX = 2048
GH, GDK, GDV = 6, 64, 128
GW = GH * GDV
GQK = GH * GDK
GRANK = 16
GTEMP = 16.0
CHUNK = 64
MH, MQL, MKVL, MNOPE, MROPE, MDV = 6, 384, 256, 128, 64, 128
MW = MH * MDV
MQW = MH * (MNOPE + MROPE)
MKVW = MH * (MNOPE + MDV)
CONV_CH = 512
ROPE_THETA = 10000.0
EPS = 1e-6
IN_DIM = 5856
OZ, OCB, OCC, OCX, OMKV, OGV, OGQ, OGK, OMQ, OT = 0, 2048, 2560, 3072, 3584, 3840, 4608, 4992, 5376, 5760
PW = 5888
LANES = 128
V7X_VMEM_BYTES = 64 * 1024 * 1024
VMEM_LIMIT = V7X_VMEM_BYTES * 7 // 8

ADAM_LR, ADAM_B1, ADAM_B2, ADAM_EPS, ADAM_WD, ADAM_STEP = 0.001, 0.9, 0.999, 1e-08, 0.01, 10


def _cp(sem=None):
    return pltpu.CompilerParams(dimension_semantics=sem, vmem_limit_bytes=VMEM_LIMIT)


def _dot(a, b):
    return jnp.dot(a.astype(MXU), b.astype(MXU), preferred_element_type=F32)


def _dot_nt(a, b):
    return lax.dot_general(a.astype(MXU), b.astype(MXU), (((1,), (1,)), ((), ())), preferred_element_type=F32)


def _dot_tn(a, b):
    return lax.dot_general(a.astype(MXU), b.astype(MXU), (((0,), (0,)), ((), ())), preferred_element_type=F32)


def _dotf(a, b):
    return jnp.dot(a, b, precision=HI, preferred_element_type=F32)


def _dotf_tn(a, b):
    return lax.dot_general(a, b, (((0,), (0,)), ((), ())), precision=HI, preferred_element_type=F32)


def _split3(x):
    hi = x.astype(jnp.bfloat16)
    r1 = x - hi.astype(F32)
    mid = r1.astype(jnp.bfloat16)
    lo = (r1 - mid.astype(F32)).astype(jnp.bfloat16)
    return hi, mid, lo


def _cum_dot(cum, x, transpose=False):
    dn = (((0,), (0,)), ((), ())) if transpose else (((1,), (0,)), ((), ()))
    cb = cum.astype(jnp.bfloat16)
    parts = [lax.dot_general(cb, p, dn, preferred_element_type=F32) for p in _split3(x)]
    return parts[0] + parts[1] + parts[2]


def _rows(s):
    return min(256, s)


def _rows_light(s):
    return min(512, s)


def _rms(x, g):
    r = lax.rsqrt(jnp.mean(x * x, axis=-1, keepdims=True) + EPS)
    return x * r * g


def _rms_bwd(dy, x, g):
    r = lax.rsqrt(jnp.mean(x * x, axis=-1, keepdims=True) + EPS)
    xh = x * r
    dxh = dy * g
    dg = jnp.sum(dy * xh, axis=0, keepdims=True)
    dx = r * (dxh - xh * jnp.mean(dxh * xh, axis=-1, keepdims=True))
    return dx, dg


def _sigmoid(z):
    return jax.nn.sigmoid(z)


def _matmul(a, b, *, dims, tm, tn, tk, out_dtypes, name, epilogue=None, extras=(), extra_kinds=(), after=()):
    if dims == "nn":
        (m, k), n, mul = a.shape, b.shape[1], _dot
    elif dims == "nt":
        (m, k), n, mul = a.shape, b.shape[0], _dot_nt
    else:
        (k, m), n, mul = a.shape, b.shape[1], _dot_tn
    tm, tn, tk = min(tm, m), min(tn, n), min(tk, k)
    assert m % tm == 0 and n % tn == 0 and k % tk == 0, (m, n, k, tm, tn, tk)
    if dims == "nn":
        a_spec = pl.BlockSpec((tm, tk), lambda i, j, kk: (i, kk))
        b_spec = pl.BlockSpec((tk, tn), lambda i, j, kk: (kk, j))
    elif dims == "nt":
        a_spec = pl.BlockSpec((tm, tk), lambda i, j, kk: (i, kk))
        b_spec = pl.BlockSpec((tn, tk), lambda i, j, kk: (j, kk))
    else:
        a_spec = pl.BlockSpec((tk, tm), lambda i, j, kk: (kk, i))
        b_spec = pl.BlockSpec((tk, tn), lambda i, j, kk: (kk, j))
    nk = k // tk
    n_extra = len(extras)
    n_out = len(out_dtypes)
    n_after = len(after)
    extra_specs = []
    for kind in extra_kinds:
        if kind == "mn":
            extra_specs.append(pl.BlockSpec((tm, tn), lambda i, j, kk: (i, j)))
        else:
            extra_specs.append(pl.BlockSpec((1, tn), lambda i, j, kk: (0, j)))

    def finish(res, ex, outs):
        vals = (res,) if epilogue is None else epilogue(res, *[e[...] for e in ex])
        for o, v in zip(outs, vals):
            o[...] = v.astype(o.dtype)

    def body(*refs):
        a_ref, b_ref = refs[0], refs[1]
        ex = refs[2:2 + n_extra]
        outs = refs[2 + n_extra + n_after:2 + n_extra + n_after + n_out]
        if nk == 1:
            finish(mul(a_ref[...], b_ref[...]), ex, outs)
            return
        acc = refs[-1]
        kk = pl.program_id(2)

        @pl.when(kk == 0)
        def _():
            acc[...] = jnp.zeros_like(acc)

        acc[...] += mul(a_ref[...], b_ref[...])

        @pl.when(kk == nk - 1)
        def _():
            finish(acc[...], ex, outs)

    out_spec = pl.BlockSpec((tm, tn), lambda i, j, kk: (i, j))
    res = pl.pallas_call(
        body, grid=(m // tm, n // tn, nk),
        in_specs=[a_spec, b_spec] + extra_specs + [pl.BlockSpec(memory_space=pl.ANY)] * n_after,
        out_specs=[out_spec] * n_out,
        out_shape=[jax.ShapeDtypeStruct((m, n), dt) for dt in out_dtypes],
        scratch_shapes=[] if nk == 1 else [pltpu.VMEM((tm, tn), F32)],
        name=name, compiler_params=_cp(("parallel", "parallel", "arbitrary")),
    )(a, b, *extras, *after)
    return res


def _norm_mod(x, g, scale, shift, name):
    s, d = x.shape
    tr = _rows_light(s)

    def body(x_ref, g_ref, sc_ref, sh_ref, h_ref):
        h = _rms(x_ref[...], g_ref[...]) * (1.0 + sc_ref[...]) + sh_ref[...]
        h_ref[...] = h.astype(h_ref.dtype)

    row = pl.BlockSpec((tr, d), lambda i: (i, 0))
    vec = pl.BlockSpec((1, d), lambda i: (0, 0))
    return pl.pallas_call(body, grid=(s // tr,), in_specs=[row, vec, vec, vec], out_specs=row,
                          out_shape=jax.ShapeDtypeStruct((s, d), MXU), name=name,
                          compiler_params=_cp(("parallel",)))(x, g, scale, shift)


def _norm_mod_bwd(d_h, x, d_out, g, scale, name):
    s, d = x.shape
    tr = _rows(s)

    def body(dh_ref, x_ref, do_ref, g_ref, sc_ref, dx_ref, dsh_ref, dsc_ref, dg_ref):
        i = pl.program_id(0)

        @pl.when(i == 0)
        def _():
            dsh_ref[...] = jnp.zeros_like(dsh_ref)
            dsc_ref[...] = jnp.zeros_like(dsc_ref)
            dg_ref[...] = jnp.zeros_like(dg_ref)

        dh = dh_ref[...]
        xv = x_ref[...]
        gv = g_ref[...]
        r = lax.rsqrt(jnp.mean(xv * xv, axis=-1, keepdims=True) + EPS)
        xh = xv * r
        dsh_ref[...] += jnp.sum(dh, axis=0, keepdims=True)
        dsc_ref[...] += jnp.sum(dh * (xh * gv), axis=0, keepdims=True)
        dhn = dh * (1.0 + sc_ref[...])
        dg_ref[...] += jnp.sum(dhn * xh, axis=0, keepdims=True)
        dxh = dhn * gv
        dx_ref[...] = do_ref[...] + r * (dxh - xh * jnp.mean(dxh * xh, axis=-1, keepdims=True))

    row = pl.BlockSpec((tr, d), lambda i: (i, 0))
    vec = pl.BlockSpec((1, d), lambda i: (0, 0))
    vshape = jax.ShapeDtypeStruct((1, d), F32)
    return pl.pallas_call(body, grid=(s // tr,), in_specs=[row, row, row, vec, vec],
                          out_specs=[row, vec, vec, vec],
                          out_shape=[jax.ShapeDtypeStruct((s, d), F32), vshape, vshape, vshape],
                          name=name, compiler_params=_cp(("arbitrary",)))(d_h, x, d_out, g, scale)


def _gate_bwd(d_out, u, gate, name):
    s, d = d_out.shape
    tr = _rows_light(s)

    def body(do_ref, u_ref, gt_ref, du_ref, dgt_ref):
        @pl.when(pl.program_id(0) == 0)
        def _():
            dgt_ref[...] = jnp.zeros_like(dgt_ref)

        do = do_ref[...]
        du_ref[...] = (do * gt_ref[...]).astype(du_ref.dtype)
        dgt_ref[...] += jnp.sum(do * u_ref[...], axis=0, keepdims=True)

    row = pl.BlockSpec((tr, d), lambda i: (i, 0))
    vec = pl.BlockSpec((1, d), lambda i: (0, 0))
    return pl.pallas_call(body, grid=(s // tr,), in_specs=[row, row, vec], out_specs=[row, vec],
                          out_shape=[jax.ShapeDtypeStruct((s, d), MXU), jax.ShapeDtypeStruct((1, d), F32)],
                          name=name, compiler_params=_cp(("arbitrary",)))(d_out, u, gate)


def _final_loss(x, g, target, name):
    s, d = x.shape
    tr = _rows_light(s)

    def body(x_ref, g_ref, t_ref, loss_ref, dx_ref, dg_ref):
        @pl.when(pl.program_id(0) == 0)
        def _():
            loss_ref[...] = jnp.zeros_like(loss_ref)
            dg_ref[...] = jnp.zeros_like(dg_ref)

        xv = x_ref[...]
        gv = g_ref[...]
        diff = _rms(xv, gv) - t_ref[...]
        part = 0.5 * jnp.sum(jnp.sum(diff * diff, axis=-1, keepdims=True) / d, axis=0, keepdims=True)
        loss_ref[...] += jnp.broadcast_to(part, loss_ref.shape)
        dx, dg = _rms_bwd(diff / d, xv, gv)
        dx_ref[...] = dx
        dg_ref[...] += dg

    row = pl.BlockSpec((tr, d), lambda i: (i, 0))
    vec = pl.BlockSpec((1, d), lambda i: (0, 0))
    lvec = pl.BlockSpec((1, LANES), lambda i: (0, 0))
    return pl.pallas_call(body, grid=(s // tr,), in_specs=[row, vec, row], out_specs=[lvec, row, vec],
                          out_shape=[jax.ShapeDtypeStruct((1, LANES), F32), jax.ShapeDtypeStruct((s, d), F32),
                                     jax.ShapeDtypeStruct((1, d), F32)],
                          name=name, compiler_params=_cp(("arbitrary",)))(x, g, target)


def _shift_rows(u, s, down):
    ri = lax.broadcasted_iota(jnp.int32, u.shape, 0)
    if down:
        return jnp.where(ri == 0, 0.0, pltpu.roll(u, 1, 0))
    return jnp.where(ri == s - 1, 0.0, pltpu.roll(u, s - 1, 0))


def _conv_fwd(proj, conv_w, name):
    s = proj.shape[0]
    nt = CONV_CH // LANES

    def body(cb_ref, cc_ref, cx_ref, w_ref, pre_ref):
        u = cc_ref[...] * cx_ref[...]
        conv = _shift_rows(u, s, True) * w_ref[0:1, :] + u * w_ref[1:2, :] + _shift_rows(u, s, False) * w_ref[2:3, :]
        pre_ref[...] = cb_ref[...] * conv

    def col(off):
        return pl.BlockSpec((s, LANES), lambda j: (0, off // LANES + j))

    return pl.pallas_call(body, grid=(nt,), in_specs=[col(OCB), col(OCC), col(OCX), pl.BlockSpec((3, LANES), lambda j: (0, j))],
                          out_specs=pl.BlockSpec((s, LANES), lambda j: (0, j)),
                          out_shape=jax.ShapeDtypeStruct((s, CONV_CH), F32), name=name,
                          compiler_params=_cp(("parallel",)))(proj, proj, proj, conv_w)


def _conv_bwd(proj, conv_w, d_pre, name):
    s = proj.shape[0]
    nt = CONV_CH // LANES

    def body(cb_ref, cc_ref, cx_ref, w_ref, dp_ref, dcb_ref, dcc_ref, dcx_ref, dw_ref):
        cc, cx = cc_ref[...], cx_ref[...]
        u = cc * cx
        up, dn = _shift_rows(u, s, True), _shift_rows(u, s, False)
        w0, w1, w2 = w_ref[0:1, :], w_ref[1:2, :], w_ref[2:3, :]
        conv = up * w0 + u * w1 + dn * w2
        dp = dp_ref[...]
        dcb_ref[...] = (dp * conv).astype(dcb_ref.dtype)
        dconv = dp * cb_ref[...]
        du = _shift_rows(dconv, s, False) * w0 + dconv * w1 + _shift_rows(dconv, s, True) * w2
        dcc_ref[...] = (du * cx).astype(dcc_ref.dtype)
        dcx_ref[...] = (du * cc).astype(dcx_ref.dtype)
        dw_ref[0:1, :] = jnp.sum(dconv * up, axis=0, keepdims=True)
        dw_ref[1:2, :] = jnp.sum(dconv * u, axis=0, keepdims=True)
        dw_ref[2:3, :] = jnp.sum(dconv * dn, axis=0, keepdims=True)

    def col(off):
        return pl.BlockSpec((s, LANES), lambda j: (0, off // LANES + j))

    blk = pl.BlockSpec((s, LANES), lambda j: (0, j))
    wblk = pl.BlockSpec((3, LANES), lambda j: (0, j))
    full = jax.ShapeDtypeStruct((s, CONV_CH), MXU)
    return pl.pallas_call(body, grid=(nt,), in_specs=[col(OCB), col(OCC), col(OCX), wblk, blk],
                          out_specs=[blk, blk, blk, wblk],
                          out_shape=[full, full, full, jax.ShapeDtypeStruct((3, CONV_CH), F32)],
                          name=name, compiler_params=_cp(("parallel",)))(proj, proj, proj, conv_w, d_pre)


GLA_SUB = 8


def _gla_gates(t_ref, wg_ref, bg_ref):
    t = t_ref[...]
    a = _dot(t, wg_ref[...]) + bg_ref[...]
    la = (jnp.minimum(a, 0.0) - jnp.log(1.0 + jnp.exp(-jnp.abs(a)))) / GTEMP
    return t, a, la


def _gla_masks(reverse):
    ri = lax.broadcasted_iota(jnp.int32, (CHUNK, CHUNK), 0)
    ci = lax.broadcasted_iota(jnp.int32, (CHUNK, CHUNK), 1)
    if reverse:
        cum, mask, mask_t = ci >= ri, ci > ri, ri > ci
    else:
        cum, mask, mask_t = ci <= ri, ci <= ri, ri <= ci
    return cum.astype(F32), mask, mask_t


def _gla_specs(s, reverse):
    nsub = min(GLA_SUB, s // CHUNK)
    nsteps = s // (CHUNK * nsub)

    def row(n):
        return nsteps - 1 - n if reverse else n

    def chunk(pi):
        return nsub - 1 - pi if reverse else pi

    return nsub, nsteps, row, chunk


def _gla_fwd(proj, wg_pad, bg, reverse, name):
    s = proj.shape[0]
    nsub, nsteps, row, chunk = _gla_specs(s, reverse)
    rb = nsub * CHUNK

    def body(q_ref, k_ref, v_ref, t_ref, wg_ref, bg_ref, o_ref, st_ref, state):
        @pl.when(pl.program_id(0) == 0)
        def _():
            state[...] = jnp.zeros_like(state)

        _, _, la = _gla_gates(t_ref, wg_ref, bg_ref)
        cumf, mask, _ = _gla_masks(reverse)
        lane = lax.broadcasted_iota(jnp.int32, (CHUNK, LANES), 1)
        for pi in range(nsub):
            rows = slice(chunk(pi) * CHUNK, (chunk(pi) + 1) * CHUNK)
            la_c = la[rows]
            b_all = _cum_dot(cumf, la_c)
            bl_all = jnp.sum(la_c, axis=0, keepdims=True)
            for p in range(GH // 2):
                sl = slice(p * LANES, (p + 1) * LANES)
                b, bl = b_all[:, sl], bl_all[:, sl]
                qd = q_ref[rows, sl] * (GDK ** -0.5) * jnp.exp(b)
                ki = k_ref[rows, sl] * jnp.exp(-b)
                kte = k_ref[rows, sl] * jnp.exp(bl - b)
                decay = jnp.exp(bl)
                for half in range(2):
                    h = 2 * p + half
                    lm = (lane < GDK) if half == 0 else (lane >= GDK)
                    qd_h = jnp.where(lm, qd, 0.0)
                    kte_h = jnp.where(lm, kte, 0.0)
                    v_h = v_ref[rows, h * GDV:(h + 1) * GDV]
                    st = state[h]
                    a_mat = jnp.where(mask, _dot_nt(qd_h, ki), 0.0)
                    o_ref[rows, h * GDV:(h + 1) * GDV] = _dot(a_mat, v_h) + _dot_nt(qd_h, st)
                    st_ref[pi, h] = st
                    state[h] = st * decay + _dot_tn(v_h, kte_h)

    return pl.pallas_call(
        body, grid=(nsteps,),
        in_specs=[pl.BlockSpec((rb, GQK), lambda n: (row(n), OGQ // GQK)),
                  pl.BlockSpec((rb, GQK), lambda n: (row(n), OGK // GQK)),
                  pl.BlockSpec((rb, GW), lambda n: (row(n), OGV // GW)),
                  pl.BlockSpec((rb, LANES), lambda n: (row(n), OT // LANES)),
                  pl.BlockSpec((LANES, GQK), lambda n: (0, 0)),
                  pl.BlockSpec((1, GQK), lambda n: (0, 0))],
        out_specs=[pl.BlockSpec((rb, GW), lambda n: (row(n), 0)),
                   pl.BlockSpec((nsub, GH, GDV, LANES), lambda n: (n, 0, 0, 0))],
        out_shape=[jax.ShapeDtypeStruct((s, GW), F32), jax.ShapeDtypeStruct((s // CHUNK, GH, GDV, LANES), F32)],
        scratch_shapes=[pltpu.VMEM((GH, GDV, LANES), F32)],
        name=name, compiler_params=_cp(("arbitrary",)))(proj, proj, proj, proj, wg_pad, bg)


def _gla_bwd(proj, wg_pad, bg, states, d_o, reverse, name):
    s = proj.shape[0]
    nsub, nsteps, row, chunk = _gla_specs(s, reverse)
    rb = nsub * CHUNK

    def body(q_ref, k_ref, v_ref, t_ref, wg_ref, bg_ref, st_ref, do_ref,
             dq_ref, dk_ref, dv_ref, dt_ref, dwg_ref, dbg_ref, dstate, da_buf):
        @pl.when(pl.program_id(0) == 0)
        def _():
            dstate[...] = jnp.zeros_like(dstate)
            dwg_ref[...] = jnp.zeros_like(dwg_ref)
            dbg_ref[...] = jnp.zeros_like(dbg_ref)

        t, a, la = _gla_gates(t_ref, wg_ref, bg_ref)
        cumf, mask, mask_t = _gla_masks(reverse)
        lane = lax.broadcasted_iota(jnp.int32, (CHUNK, LANES), 1)
        for pi in reversed(range(nsub)):
            rows = slice(chunk(pi) * CHUNK, (chunk(pi) + 1) * CHUNK)
            la_c = la[rows]
            b_all = _cum_dot(cumf, la_c)
            bl_all = jnp.sum(la_c, axis=0, keepdims=True)
            for p in range(GH // 2):
                sl = slice(p * LANES, (p + 1) * LANES)
                b, bl = b_all[:, sl], bl_all[:, sl]
                e, ei, ee = jnp.exp(b), jnp.exp(-b), jnp.exp(bl - b)
                qd = q_ref[rows, sl] * (GDK ** -0.5) * e
                ki, kte = k_ref[rows, sl] * ei, k_ref[rows, sl] * ee
                decay = jnp.exp(bl)
                dqd = jnp.zeros((CHUNK, LANES), F32)
                dki = jnp.zeros((CHUNK, LANES), F32)
                dkte = jnp.zeros((CHUNK, LANES), F32)
                ddecay = jnp.zeros((1, LANES), F32)
                for half in range(2):
                    h = 2 * p + half
                    lm = (lane < GDK) if half == 0 else (lane >= GDK)
                    qd_h = jnp.where(lm, qd, 0.0)
                    ki_h = jnp.where(lm, ki, 0.0)
                    kte_h = jnp.where(lm, kte, 0.0)
                    v_h = v_ref[rows, h * GDV:(h + 1) * GDV]
                    do_h = do_ref[rows, h * GDV:(h + 1) * GDV]
                    st = st_ref[pi, h]
                    dst = dstate[h]
                    at_mat = jnp.where(mask_t, _dot_nt(ki_h, qd_h), 0.0)
                    da_mat = jnp.where(mask, _dot_nt(do_h, v_h), 0.0)
                    dat_mat = jnp.where(mask_t, _dot_nt(v_h, do_h), 0.0)
                    dv_ref[rows, h * GDV:(h + 1) * GDV] = _dot(at_mat, do_h) + _dot_nt(kte_h, dst)
                    dqd += _dot(jnp.concatenate([do_h, da_mat], axis=1), jnp.concatenate([st, ki_h], axis=0))
                    dki += _dot(dat_mat, qd_h)
                    dkte += _dot(v_h, dst)
                    ddecay += jnp.sum(dst * st, axis=0, keepdims=True)
                    dstate[h] = dst * decay + _dot_tn(do_h, qd_h)
                dq_ref[rows, sl] = dqd * e * (GDK ** -0.5)
                dk_ref[rows, sl] = dki * ei + dkte * ee
                db = dqd * qd - dki * ki - dkte * kte
                dbl = jnp.sum(dkte * kte, axis=0, keepdims=True) + decay * ddecay
                da_buf[rows, sl] = _cum_dot(cumf, db, True) + dbl
        da = da_buf[...] * (1.0 / GTEMP) * _sigmoid(-a)
        dt_ref[...] = _dot_nt(da, wg_ref[...])
        dwg_ref[...] += _dot_tn(t, da)
        dbg_ref[...] += jnp.sum(da, axis=0, keepdims=True)

    def prow(j):
        return row(nsteps - 1 - j)

    return pl.pallas_call(
        body, grid=(nsteps,),
        in_specs=[pl.BlockSpec((rb, GQK), lambda j: (prow(j), OGQ // GQK)),
                  pl.BlockSpec((rb, GQK), lambda j: (prow(j), OGK // GQK)),
                  pl.BlockSpec((rb, GW), lambda j: (prow(j), OGV // GW)),
                  pl.BlockSpec((rb, LANES), lambda j: (prow(j), OT // LANES)),
                  pl.BlockSpec((LANES, GQK), lambda j: (0, 0)),
                  pl.BlockSpec((1, GQK), lambda j: (0, 0)),
                  pl.BlockSpec((nsub, GH, GDV, LANES), lambda j: (nsteps - 1 - j, 0, 0, 0)),
                  pl.BlockSpec((rb, GW), lambda j: (prow(j), 0))],
        out_specs=[pl.BlockSpec((rb, GQK), lambda j: (prow(j), 0)),
                   pl.BlockSpec((rb, GQK), lambda j: (prow(j), 0)),
                   pl.BlockSpec((rb, GW), lambda j: (prow(j), 0)),
                   pl.BlockSpec((rb, LANES), lambda j: (prow(j), 0)),
                   pl.BlockSpec((LANES, GQK), lambda j: (0, 0)),
                   pl.BlockSpec((1, GQK), lambda j: (0, 0))],
        out_shape=[jax.ShapeDtypeStruct((s, GQK), F32), jax.ShapeDtypeStruct((s, GQK), F32),
                   jax.ShapeDtypeStruct((s, GW), F32), jax.ShapeDtypeStruct((s, LANES), F32),
                   jax.ShapeDtypeStruct((LANES, GQK), F32), jax.ShapeDtypeStruct((1, GQK), F32)],
        scratch_shapes=[pltpu.VMEM((GH, GDV, LANES), F32), pltpu.VMEM((rb, GQK), F32)],
        name=name, compiler_params=_cp(("arbitrary",)))(proj, proj, proj, proj, wg_pad, bg, states, d_o)


def _rot_half(x):
    lane = lax.broadcasted_iota(jnp.int32, x.shape, 1)
    first = (lane % MROPE) < (MROPE // 2)
    return jnp.where(first, -pltpu.roll(x, LANES - MROPE // 2, 1), pltpu.roll(x, MROPE // 2, 1))


def _mla_prep(proj, cos, sin, qg, kvg, w_uq, w_ukv, name):
    s = proj.shape[0]
    tr = _rows(s)

    def body(mq_ref, mkv_ref, t_ref, cos_ref, sin_ref, qg_ref, kvg_ref, wuq_ref, wukv_ref, q_ref, k_ref, v_ref):
        cosv, sinv = cos_ref[...], sin_ref[...]
        lane = lax.broadcasted_iota(jnp.int32, (tr, LANES), 1)

        def rope(xv):
            return xv * cosv + _rot_half(xv) * sinv

        qm = _dot(_rms(mq_ref[...], qg_ref[...]), wuq_ref[...])
        kv = _dot(_rms(mkv_ref[...], kvg_ref[...]), wukv_ref[...])
        kr_lo = jnp.where(lane < MROPE, rope(t_ref[...]), 0.0)
        kr_hi = pltpu.roll(kr_lo, MROPE, 1)
        for p in range(MH // 2):
            r = rope(qm[:, MW + p * LANES:MW + (p + 1) * LANES]).astype(q_ref.dtype)
            q_ref[2 * p, :, LANES:] = r
            q_ref[2 * p + 1, :, LANES:] = r
        for h in range(MH):
            q_ref[h, :, :LANES] = qm[:, h * LANES:(h + 1) * LANES].astype(q_ref.dtype)
            k_ref[h, :, :LANES] = kv[:, 2 * h * LANES:(2 * h + 1) * LANES].astype(k_ref.dtype)
            k_ref[h, :, LANES:] = (kr_lo if h % 2 == 0 else kr_hi).astype(k_ref.dtype)
            v_ref[h] = kv[:, (2 * h + 1) * LANES:(2 * h + 2) * LANES].astype(v_ref.dtype)

    def full(shape):
        return pl.BlockSpec(shape, lambda i: (0,) * len(shape))

    return pl.pallas_call(
        body, grid=(s // tr,),
        in_specs=[pl.BlockSpec((tr, MQL), lambda i: (i, OMQ // MQL)),
                  pl.BlockSpec((tr, MKVL), lambda i: (i, OMKV // MKVL)),
                  pl.BlockSpec((tr, LANES), lambda i: (i, OT // LANES)),
                  pl.BlockSpec((tr, LANES), lambda i: (i, 0)),
                  pl.BlockSpec((tr, LANES), lambda i: (i, 0)),
                  full((1, MQL)), full((1, MKVL)), full((MQL, MQW)), full((MKVL, MKVW))],
        out_specs=[pl.BlockSpec((MH, tr, 2 * LANES), lambda i: (0, i, 0)),
                   pl.BlockSpec((MH, tr, 2 * LANES), lambda i: (0, i, 0)),
                   pl.BlockSpec((MH, tr, LANES), lambda i: (0, i, 0))],
        out_shape=[jax.ShapeDtypeStruct((MH, s, 2 * LANES), MXU), jax.ShapeDtypeStruct((MH, s, 2 * LANES), MXU),
                   jax.ShapeDtypeStruct((MH, s, LANES), MXU)],
        name=name, compiler_params=_cp(("parallel",)))(proj, proj, proj, cos, sin, qg, kvg, w_uq, w_ukv)


def _mla_prep_bwd(proj, cos, sin, qg, kvg, w_uq, w_ukv, d_q, d_k, d_v, name):
    s = proj.shape[0]
    tr = _rows(s)

    def body(mq_ref, mkv_ref, cos_ref, sin_ref, qg_ref, kvg_ref, wuq_ref, wukv_ref, dq_ref, dk_ref, dv_ref,
             dmq_ref, dmkv_ref, dt_ref, dwuq_ref, dwukv_ref, dqg_ref, dkvg_ref):
        @pl.when(pl.program_id(0) == 0)
        def _():
            for r in (dwuq_ref, dwukv_ref, dqg_ref, dkvg_ref):
                r[...] = jnp.zeros_like(r)

        cosv, sinv = cos_ref[...], sin_ref[...]
        lane = lax.broadcasted_iota(jnp.int32, (tr, LANES), 1)
        lo = lane < MROPE

        def unrope(dv):
            return dv * cosv - _rot_half(dv * sinv)

        parts = [dq_ref[h, :, :LANES] for h in range(MH)]
        for p in range(MH // 2):
            parts.append(unrope(jnp.where(lo, dq_ref[2 * p, :, LANES:], dq_ref[2 * p + 1, :, LANES:])))
        d_qm = jnp.concatenate(parts, axis=1)
        mq, qgv = mq_ref[...], qg_ref[...]
        cq = _rms(mq, qgv)
        dwuq_ref[...] += _dot_tn(cq, d_qm)
        dmq, dqg = _rms_bwd(_dot_nt(d_qm, wuq_ref[...]), mq, qgv)
        dmq_ref[...] = dmq.astype(dmq_ref.dtype)
        dqg_ref[...] += dqg

        parts = []
        for h in range(MH):
            parts += [dk_ref[h, :, :LANES], dv_ref[h]]
        d_kv = jnp.concatenate(parts, axis=1)
        mkv, kvgv = mkv_ref[...], kvg_ref[...]
        ckv = _rms(mkv, kvgv)
        dwukv_ref[...] += _dot_tn(ckv, d_kv)
        dmkv, dkvg = _rms_bwd(_dot_nt(d_kv, wukv_ref[...]), mkv, kvgv)
        dmkv_ref[...] = dmkv.astype(dmkv_ref.dtype)
        dkvg_ref[...] += dkvg

        even = dk_ref[0, :, LANES:] + dk_ref[2, :, LANES:] + dk_ref[4, :, LANES:]
        odd = dk_ref[1, :, LANES:] + dk_ref[3, :, LANES:] + dk_ref[5, :, LANES:]
        d_kr = jnp.where(lo, even, 0.0) + pltpu.roll(jnp.where(lo, 0.0, odd), MROPE, 1)
        dt_ref[...] = jnp.where(lo, unrope(d_kr), 0.0)

    def full(shape):
        return pl.BlockSpec(shape, lambda i: (0,) * len(shape))

    return pl.pallas_call(
        body, grid=(s // tr,),
        in_specs=[pl.BlockSpec((tr, MQL), lambda i: (i, OMQ // MQL)),
                  pl.BlockSpec((tr, MKVL), lambda i: (i, OMKV // MKVL)),
                  pl.BlockSpec((tr, LANES), lambda i: (i, 0)),
                  pl.BlockSpec((tr, LANES), lambda i: (i, 0)),
                  full((1, MQL)), full((1, MKVL)), full((MQL, MQW)), full((MKVL, MKVW)),
                  pl.BlockSpec((MH, tr, 2 * LANES), lambda i: (0, i, 0)),
                  pl.BlockSpec((MH, tr, 2 * LANES), lambda i: (0, i, 0)),
                  pl.BlockSpec((MH, tr, LANES), lambda i: (0, i, 0))],
        out_specs=[pl.BlockSpec((tr, MQL), lambda i: (i, 0)), pl.BlockSpec((tr, MKVL), lambda i: (i, 0)),
                   pl.BlockSpec((tr, LANES), lambda i: (i, 0)),
                   full((MQL, MQW)), full((MKVL, MKVW)), full((1, MQL)), full((1, MKVL))],
        out_shape=[jax.ShapeDtypeStruct((s, MQL), MXU), jax.ShapeDtypeStruct((s, MKVL), MXU),
                   jax.ShapeDtypeStruct((s, LANES), F32),
                   jax.ShapeDtypeStruct((MQL, MQW), F32), jax.ShapeDtypeStruct((MKVL, MKVW), F32),
                   jax.ShapeDtypeStruct((1, MQL), F32), jax.ShapeDtypeStruct((1, MKVL), F32)],
        name=name, compiler_params=_cp(("arbitrary",)))(proj, proj, cos, sin, qg, kvg, w_uq, w_ukv, d_q, d_k, d_v)


ATT_SCALE = (MNOPE + MROPE) ** -0.5
ATT_SCALE_LOG2 = ATT_SCALE * 1.4426950408889634
ATT_TQ_FWD, ATT_TQ = 2048, 2048
ATT_SUB, ATT_SUB_BWD = 256, 256


def _attn_fwd(q, k, v, name):
    s = q.shape[1]
    tq = min(ATT_TQ_FWD, s)
    sub = min(ATT_SUB, tq)

    def body(q_ref, k_ref, v_ref, o_ref, lse_ref):
        for r0 in range(0, tq, sub):
            rows = slice(r0, r0 + sub)
            sc = _dot_nt(q_ref[0, rows, :], k_ref[0])
            m = jnp.max(sc, axis=-1, keepdims=True)
            p = jnp.exp2((sc - m) * ATT_SCALE_LOG2)
            l = jnp.sum(p, axis=-1, keepdims=True)
            o_ref[rows, :] = _dot(p, v_ref[0]) / l
            lse_ref[0, rows, :] = m * ATT_SCALE_LOG2 + jnp.log2(l)

    return pl.pallas_call(
        body, grid=(MH, s // tq),
        in_specs=[pl.BlockSpec((1, tq, 2 * LANES), lambda h, i: (h, i, 0)),
                  pl.BlockSpec((1, s, 2 * LANES), lambda h, i: (h, 0, 0)),
                  pl.BlockSpec((1, s, LANES), lambda h, i: (h, 0, 0))],
        out_specs=[pl.BlockSpec((tq, LANES), lambda h, i: (i, h)),
                   pl.BlockSpec((1, tq, 1), lambda h, i: (h, i, 0))],
        out_shape=[jax.ShapeDtypeStruct((s, MW), F32), jax.ShapeDtypeStruct((MH, s, 1), F32)],
        name=name, compiler_params=_cp(("parallel", "parallel")))(q, k, v)


def _attn_bwd(q, k, v, o, lse, d_o, name):
    s = q.shape[1]
    tq = min(ATT_TQ, s)
    sub = min(ATT_SUB_BWD, tq)

    def body(q_ref, k_ref, v_ref, o_ref, lse_ref, do_ref, dq_ref, dk_ref, dv_ref):
        @pl.when(pl.program_id(1) == 0)
        def _():
            dk_ref[...] = jnp.zeros_like(dk_ref)
            dv_ref[...] = jnp.zeros_like(dv_ref)

        kv = k_ref[0]
        for r0 in range(0, tq, sub):
            rows = slice(r0, r0 + sub)
            qv, do = q_ref[0, rows, :], do_ref[rows, :]
            p = jnp.exp2(_dot_nt(qv, kv) * ATT_SCALE_LOG2 - lse_ref[0, rows, :])
            delta = jnp.sum(do * o_ref[rows, :], axis=-1, keepdims=True)
            ds = p * (_dot_nt(do, v_ref[0]) - delta)
            dq_ref[0, rows, :] = _dot(ds, kv) * ATT_SCALE
            dk_ref[0] += _dot_tn(ds, qv) * ATT_SCALE
            dv_ref[0] += _dot_tn(p, do)

    return pl.pallas_call(
        body, grid=(MH, s // tq),
        in_specs=[pl.BlockSpec((1, tq, 2 * LANES), lambda h, i: (h, i, 0)),
                  pl.BlockSpec((1, s, 2 * LANES), lambda h, i: (h, 0, 0)),
                  pl.BlockSpec((1, s, LANES), lambda h, i: (h, 0, 0)),
                  pl.BlockSpec((tq, LANES), lambda h, i: (i, h)),
                  pl.BlockSpec((1, tq, 1), lambda h, i: (h, i, 0)),
                  pl.BlockSpec((tq, LANES), lambda h, i: (i, h))],
        out_specs=[pl.BlockSpec((1, tq, 2 * LANES), lambda h, i: (h, i, 0)),
                   pl.BlockSpec((1, s, 2 * LANES), lambda h, i: (h, 0, 0)),
                   pl.BlockSpec((1, s, LANES), lambda h, i: (h, 0, 0))],
        out_shape=[jax.ShapeDtypeStruct((MH, s, 2 * LANES), F32), jax.ShapeDtypeStruct((MH, s, 2 * LANES), F32),
                   jax.ShapeDtypeStruct((MH, s, LANES), F32)],
        name=name, compiler_params=_cp(("parallel", "arbitrary")))(q, k, v, o, lse, d_o)


def _merge_fwd(o_f, o_b, o_att, pre, proj, gng, mog, cog, name):
    s = proj.shape[0]
    tr = _rows_light(s)

    def body(of_ref, ob_ref, oa_ref, pre_ref, z_ref, gng_ref, mog_ref, cog_ref, y_ref):
        z = z_ref[...]
        sz = z * _sigmoid(z)
        osum = of_ref[...] + ob_ref[...]
        gg = gng_ref[...]
        for h in range(GH):
            sl = slice(h * GDV, (h + 1) * GDV)
            y_ref[:, sl] = (_rms(osum[:, sl], gg) * sz[:, sl]).astype(y_ref.dtype)
        y_ref[:, GW:GW + MW] = (_rms(oa_ref[...], mog_ref[...]) * sz[:, GW:GW + MW]).astype(y_ref.dtype)
        y_ref[:, GW + MW:] = (_rms(pre_ref[...], cog_ref[...]) * sz[:, GW + MW:]).astype(y_ref.dtype)

    def row(w):
        return pl.BlockSpec((tr, w), lambda i: (i, 0))

    def vec(w):
        return pl.BlockSpec((1, w), lambda i: (0, 0))

    return pl.pallas_call(
        body, grid=(s // tr,),
        in_specs=[row(GW), row(GW), row(MW), row(CONV_CH), row(D_MIX), vec(GDV), vec(MW), vec(CONV_CH)],
        out_specs=row(D_MIX), out_shape=jax.ShapeDtypeStruct((s, D_MIX), MXU),
        name=name, compiler_params=_cp(("parallel",)))(o_f, o_b, o_att, pre, proj, gng, mog, cog)


def _merge_bwd(d_y, o_f, o_b, o_att, pre, proj, gng, mog, cog, name):
    s = proj.shape[0]
    tr = _rows(s)

    def body(dy_ref, of_ref, ob_ref, oa_ref, pre_ref, z_ref, gng_ref, mog_ref, cog_ref,
             dz_ref, dos_ref, doa_ref, dpre_ref, dgng_ref, dmog_ref, dcog_ref):
        @pl.when(pl.program_id(0) == 0)
        def _():
            for r in (dgng_ref, dmog_ref, dcog_ref):
                r[...] = jnp.zeros_like(r)

        z, dy = z_ref[...], dy_ref[...]
        sg = _sigmoid(z)
        sz = z * sg
        dsz = sg * (1.0 + z * (1.0 - sg))
        dcat = dy * sz
        dyz = dy * dsz
        osum = of_ref[...] + ob_ref[...]
        gg = gng_ref[...]
        dgg = jnp.zeros_like(gg)
        for h in range(GH):
            sl = slice(h * GDV, (h + 1) * GDV)
            dz_ref[:, sl] = (dyz[:, sl] * _rms(osum[:, sl], gg)).astype(dz_ref.dtype)
            dx, dg = _rms_bwd(dcat[:, sl], osum[:, sl], gg)
            dos_ref[:, sl] = dx
            dgg += dg
        dgng_ref[...] += dgg
        sl = slice(GW, GW + MW)
        oa, mg = oa_ref[...], mog_ref[...]
        dz_ref[:, sl] = (dyz[:, sl] * _rms(oa, mg)).astype(dz_ref.dtype)
        dx, dg = _rms_bwd(dcat[:, sl], oa, mg)
        doa_ref[...] = dx
        dmog_ref[...] += dg
        sl = slice(GW + MW, D_MIX)
        pv, cg = pre_ref[...], cog_ref[...]
        dz_ref[:, sl] = (dyz[:, sl] * _rms(pv, cg)).astype(dz_ref.dtype)
        dx, dg = _rms_bwd(dcat[:, sl], pv, cg)
        dpre_ref[...] = dx
        dcog_ref[...] += dg

    def row(w):
        return pl.BlockSpec((tr, w), lambda i: (i, 0))

    def vec(w):
        return pl.BlockSpec((1, w), lambda i: (0, 0))

    def rs(w):
        return jax.ShapeDtypeStruct((s, w), F32)

    def vs(w):
        return jax.ShapeDtypeStruct((1, w), F32)

    return pl.pallas_call(
        body, grid=(s // tr,),
        in_specs=[row(D_MIX), row(GW), row(GW), row(MW), row(CONV_CH), row(D_MIX), vec(GDV), vec(MW), vec(CONV_CH)],
        out_specs=[row(D_MIX), row(GW), row(MW), row(CONV_CH), vec(GDV), vec(MW), vec(CONV_CH)],
        out_shape=[jax.ShapeDtypeStruct((s, D_MIX), MXU),
                   rs(GW), rs(MW), rs(CONV_CH), vs(GDV), vs(MW), vs(CONV_CH)],
        name=name, compiler_params=_cp(("arbitrary",)))(d_y, o_f, o_b, o_att, pre, proj, gng, mog, cog)


def _assemble_dproj(d_z, d_cb, d_cc, d_cx, d_mkv, dv_f, dv_b, dq_f, dq_b, dk_f, dk_b, d_mq, dt_m, dt_f, dt_b, name):
    s = d_z.shape[0]
    tr = _rows_light(s)

    def body(dz, dcb, dcc, dcx, dmkv, dvf, dvb, dqf, dqb, dkf, dkb, dmq, dtm, dtf, dtb, out):
        dt = out.dtype
        out[:, OZ:OZ + D_MIX] = dz[...].astype(dt)
        out[:, OCB:OCB + CONV_CH] = dcb[...].astype(dt)
        out[:, OCC:OCC + CONV_CH] = dcc[...].astype(dt)
        out[:, OCX:OCX + CONV_CH] = dcx[...].astype(dt)
        out[:, OMKV:OMKV + MKVL] = dmkv[...].astype(dt)
        out[:, OGV:OGV + GW] = (dvf[...] + dvb[...]).astype(dt)
        out[:, OGQ:OGQ + GQK] = (dqf[...] + dqb[...]).astype(dt)
        out[:, OGK:OGK + GQK] = (dkf[...] + dkb[...]).astype(dt)
        out[:, OMQ:OMQ + MQL] = dmq[...].astype(dt)
        out[:, OT:OT + LANES] = (dtm[...] + dtf[...] + dtb[...]).astype(dt)

    args = (d_z, d_cb, d_cc, d_cx, d_mkv, dv_f, dv_b, dq_f, dq_b, dk_f, dk_b, d_mq, dt_m, dt_f, dt_b)
    return pl.pallas_call(
        body, grid=(s // tr,),
        in_specs=[pl.BlockSpec((tr, a.shape[1]), lambda i: (i, 0)) for a in args],
        out_specs=pl.BlockSpec((tr, PW), lambda i: (i, 0)),
        out_shape=jax.ShapeDtypeStruct((s, PW), MXU), name=name, compiler_params=_cp(("parallel",)))(*args)


def _layer_fwd(x, mod, wt, cos, sin, tag, late=None, in_after=(), h=None):
    shift, scale, gate = mod
    if h is None:
        h = _norm_mod(x, wt["norm_g"], scale, shift, f"norm_mod_{tag}")
    (proj,) = _matmul(h, wt["w_in"], dims="nn", tm=2048, tn=256, tk=2048, out_dtypes=(F32,), name=f"in_proj_{tag}",
                      after=in_after)
    if late is not None:
        wt.update(late(proj))
    o_f, st_f = _gla_fwd(proj, wt["wg_pad_f"], wt["bg_f"], False, f"gla_fwd_f_{tag}")
    o_b, st_b = _gla_fwd(proj, wt["wg_pad_b"], wt["bg_b"], True, f"gla_fwd_b_{tag}")
    q, k, v = _mla_prep(proj, cos, sin, wt["q_norm_g"], wt["kv_norm_g"], wt["w_uq"], wt["w_ukv"], f"mla_prep_{tag}")
    o_att, lse = _attn_fwd(q, k, v, f"attn_fwd_{tag}")
    pre = _conv_fwd(proj, wt["conv_w"], f"conv_fwd_{tag}")
    y = _merge_fwd(o_f, o_b, o_att, pre, proj, wt["gla_norm_g"], wt["mla_out_g"], wt["conv_out_g"], f"merge_fwd_{tag}")
    x_new, u = _matmul(y, wt["w_out"], dims="nn", tm=2048, tn=256, tk=2048, out_dtypes=(F32, F32),
                       name=f"out_proj_{tag}", epilogue=lambda acc, xv, gv: (xv + gv * acc, acc),
                       extras=(x, gate), extra_kinds=("mn", "n"))
    saved = dict(x=x, h=h, proj=proj, o_f=o_f, o_b=o_b, st_f=st_f, st_b=st_b, q=q, k=k, v=v,
                 o_att=o_att, lse=lse, pre=pre, y=y, u=u)
    return x_new, saved


def _layer_bwd(d_out, sv, mod, wt, cos, sin, tag, ship=None, dx_first=None, ship_rest=None):
    shift, scale, gate = mod
    proj = sv["proj"]
    d_u, d_gate = _gate_bwd(d_out, sv["u"], gate, f"gate_bwd_{tag}")
    (g_w_out,) = _matmul(sv["y"], d_u, dims="tn", tm=1024, tn=512, tk=2048, out_dtypes=(MXU,), name=f"out_proj_dw_{tag}")
    (d_y,) = _matmul(d_u, wt["w_out"], dims="nt", tm=2048, tn=256, tk=2048, out_dtypes=(F32,), name=f"out_proj_dx_{tag}",
                     after=(g_w_out,))
    d_z, d_osum, d_oatt, d_pre, d_gng, d_mog, d_cog = _merge_bwd(
        d_y, sv["o_f"], sv["o_b"], sv["o_att"], sv["pre"], proj, wt["gla_norm_g"], wt["mla_out_g"], wt["conv_out_g"],
        f"merge_bwd_{tag}")
    d_cb, d_cc, d_cx, d_conv_w = _conv_bwd(proj, wt["conv_w"], d_pre, f"conv_bwd_{tag}")
    d_q, d_k, d_v = _attn_bwd(sv["q"], sv["k"], sv["v"], sv["o_att"], sv["lse"], d_oatt, f"attn_bwd_{tag}")
    d_mq, d_mkv, dt_m, g_w_uq, g_w_ukv, d_qg, d_kvg = _mla_prep_bwd(
        proj, cos, sin, wt["q_norm_g"], wt["kv_norm_g"], wt["w_uq"], wt["w_ukv"], d_q, d_k, d_v, f"mla_prep_bwd_{tag}")
    bg_f, bg_b = wt["bg_f"], wt["bg_b"]
    if ship_rest is not None:
        tok = ship_rest(dict(w_out=g_w_out, w_uq=g_w_uq, w_ukv=g_w_ukv))
        bg_f, bg_b = bg_f + tok, bg_b + tok
    dq_f, dk_f, dv_f, dt_f, d_wg_f, d_bg_f = _gla_bwd(proj, wt["wg_pad_f"], bg_f, sv["st_f"], d_osum, False,
                                                     f"gla_bwd_f_{tag}")
    dq_b, dk_b, dv_b, dt_b, d_wg_b, d_bg_b = _gla_bwd(proj, wt["wg_pad_b"], bg_b, sv["st_b"], d_osum, True,
                                                     f"gla_bwd_b_{tag}")
    d_proj = _assemble_dproj(d_z, d_cb, d_cc, d_cx, d_mkv, dv_f, dv_b, dq_f, dq_b, dk_f, dk_b, d_mq, dt_m, dt_f, dt_b,
                             f"assemble_dproj_{tag}")
    grads = dict(w_out=g_w_out, w_uq=g_w_uq, w_ukv=g_w_ukv,
                 wg_pad_f=d_wg_f, bg_f=d_bg_f, wg_pad_b=d_wg_b, bg_b=d_bg_b, gla_norm_g=d_gng,
                 q_norm_g=d_qg, kv_norm_g=d_kvg, mla_out_g=d_mog, conv_w=d_conv_w, conv_out_g=d_cog)

    def in_dw(after):
        (g_w_in,) = _matmul(sv["h"], d_proj, dims="tn", tm=2048, tn=256, tk=2048, out_dtypes=(MXU,),
                            name=f"in_proj_dw_{tag}", after=after)
        grads["w_in"] = g_w_in
        return dict(w_in=g_w_in, w_out=g_w_out, w_uq=g_w_uq, w_ukv=g_w_ukv)

    def in_dx(after):
        (d_h,) = _matmul(d_proj, wt["w_in"], dims="nt", tm=1024, tn=512, tk=PW, out_dtypes=(F32,),
                         name=f"in_proj_dx_{tag}", after=after)
        d_x, d_shift, d_scale, d_ng = _norm_mod_bwd(d_h, sv["x"], d_out, wt["norm_g"], scale, f"norm_mod_bwd_{tag}")
        grads["norm_g"] = d_ng
        return d_x, (d_shift, d_scale, d_gate)

    if dx_first is None:
        big = in_dw(())
        d_x, d_mod = in_dx((big["w_in"],) if ship is None else ship(big))
    else:
        d_x, d_mod = in_dx(())
        big = in_dw(dx_first(d_x, d_mod, grads))
        ship(big)
    return d_x, d_mod, grads


IN_SEGS = ((3808, 5856), (2272, 3808), (1952, 2208), (768, 1536), (0, 768), (1568, 1952), (2208, 2272), (1536, 1568))


def _heads_apart(w):
    w3 = w.reshape(w.shape[:-1] + (MH, MNOPE + MROPE))
    return jnp.concatenate([w3[..., :MNOPE].reshape(w.shape[:-1] + (MH * MNOPE,)),
                            w3[..., MNOPE:].reshape(w.shape[:-1] + (MH * MROPE,))], axis=-1)


def _heads_together(g):
    nope = g[..., :MH * MNOPE].reshape(g.shape[:-1] + (MH, MNOPE))
    rope = g[..., MH * MNOPE:].reshape(g.shape[:-1] + (MH, MROPE))
    return jnp.concatenate([nope, rope], axis=-1).reshape(g.shape[:-1] + (MQW,))


def _perm_gathered(g, segs, width):
    per = g.shape[-1]
    parts, total = [], 0
    for a, b in segs:
        c = a
        while c < b:
            j = c // per
            hi = min(b, (j + 1) * per)
            parts.append(g[j, :, c - j * per:hi - j * per])
            c = hi
        total += b - a
    if width > total:
        parts.append(jnp.zeros((g.shape[1], width - total), g.dtype))
    return jnp.concatenate(parts, axis=1)


def _scatter_perm(gp, segs, per):
    offs, o = [], 0
    for a, b in segs:
        offs.append((a, b, o))
        o += b - a
    blocks = []
    for j in range(N_DEV):
        lo, hi = j * per, (j + 1) * per
        pieces = []
        for a, b, o in sorted(offs):
            s0, s1 = max(a, lo), min(b, hi)
            if s0 < s1:
                pieces.append(gp[:, o + s0 - a:o + s1 - a])
        blocks.append(jnp.concatenate(pieces, axis=1))
    return jnp.stack(blocks)


def _prep_layer_weights(w_in, w_out, w_uq, w_ukv, small):
    def vec(v):
        return v.reshape(1, -1).astype(F32)

    zeros = functools.partial(jnp.zeros, dtype=F32)
    wg_f, wg_b = small["gla_wg_f"].astype(F32), small["gla_wg_b"].astype(F32)
    wg_pad_f = jnp.concatenate([zeros((MROPE, GQK)), wg_f, zeros((LANES - MROPE - GRANK, GQK))], axis=0)
    wg_pad_b = jnp.concatenate([zeros((MROPE + GRANK, GQK)), wg_b, zeros((LANES - MROPE - 2 * GRANK, GQK))], axis=0)
    wt = dict(norm_g=vec(small["norm_g"]), wg_pad_f=wg_pad_f, wg_pad_b=wg_pad_b,
              bg_f=vec(small["gla_bg_f"]), bg_b=vec(small["gla_bg_b"]), gla_norm_g=vec(small["gla_norm_g"]),
              q_norm_g=vec(small["mla_q_norm_g"]), kv_norm_g=vec(small["mla_kv_norm_g"]),
              mla_out_g=vec(small["mla_out_g"]), conv_w=small["conv_w"].astype(F32),
              conv_out_g=vec(small["conv_out_g"]))
    for name, w in (("w_in", w_in), ("w_out", w_out), ("w_uq", w_uq), ("w_ukv", w_ukv)):
        if w is not None:
            wt[name] = w.astype(MXU)
    return wt


def _natural_small(gr):
    return dict(norm_g=gr["norm_g"][0],
                gla_wg_f=gr["wg_pad_f"][MROPE:MROPE + GRANK], gla_bg_f=gr["bg_f"][0],
                gla_wg_b=gr["wg_pad_b"][MROPE + GRANK:MROPE + 2 * GRANK], gla_bg_b=gr["bg_b"][0],
                gla_norm_g=gr["gla_norm_g"][0], mla_q_norm_g=gr["q_norm_g"][0], mla_kv_norm_g=gr["kv_norm_g"][0],
                mla_out_g=gr["mla_out_g"][0], conv_w=gr["conv_w"], conv_out_g=gr["conv_out_g"][0])


def _peer(r):
    ax, ay, ac = lax.axis_index("x"), lax.axis_index("y"), lax.axis_index("c")
    px = 1 - ax if r & 4 else ax
    py = 1 - ay if r & 2 else ay
    pc = 1 - ac if r & 1 else ac
    return (px, py, pc), 4 * px + 2 * py + pc


def _gather_small(arrs, name):
    n = len(arrs)

    def body(*refs):
        ins, outs = refs[:n], refs[n:2 * n]
        send_sems, recv_sems, loc_sems = refs[2 * n:]
        me = _peer(0)[1]

        def remote(a, r, slot):
            return pltpu.make_async_remote_copy(
                src_ref=ins[a], dst_ref=outs[a].at[slot], send_sem=send_sems.at[a, r - 1],
                recv_sem=recv_sems.at[a, r - 1], device_id=_peer(r)[0], device_id_type=MESH)

        locs = [pltpu.make_async_copy(ins[a], outs[a].at[me], loc_sems.at[a]) for a in range(n)]
        for cp in locs:
            cp.start()
        sends = [remote(a, r, me) for r in range(1, N_DEV) for a in range(n)]
        for cp in sends:
            cp.start()
        for r in range(1, N_DEV):
            for a in range(n):
                remote(a, r, _peer(r)[1]).wait_recv()
        for cp in sends:
            cp.wait_send()
        for cp in locs:
            cp.wait()

    spec = pl.BlockSpec(memory_space=pltpu.VMEM)
    return pl.pallas_call(
        body, in_specs=[spec] * n, out_specs=[spec] * n,
        out_shape=[jax.ShapeDtypeStruct((N_DEV,) + a.shape, a.dtype) for a in arrs],
        scratch_shapes=[pltpu.SemaphoreType.DMA((n, N_DEV - 1)), pltpu.SemaphoreType.DMA((n, N_DEV - 1)),
                        pltpu.SemaphoreType.DMA((n,))],
        name=name, compiler_params=pltpu.CompilerParams(vmem_limit_bytes=VMEM_LIMIT))(*arrs)


def _slot(rel_div):
    rel, div = rel_div
    idx = _peer(rel)[1]
    return idx if div == 1 else idx // div


AG_SPREAD = tuple((r, None, (0, 1), (r, 1)) for r in (1, 2, 4, 6))
AG_FORWARD = tuple((1, (k, 1), (k, 1), (1 ^ k, 1)) for k in (2, 4, 6))
RS_PAIR = tuple((1, (1 ^ k, 1), (1 ^ k, 2), (k, 2)) for k in (0, 2, 4, 6))
RS_CHIPS = tuple((r, (r, 2), (0, 2), (r, 2)) for r in (2, 4, 6))


def _plan_copies(plan, n, src_refs, land_refs, send_sems, recv_sems, arriving):
    out = []
    for i, (r, src, dst, recv) in enumerate(plan):
        peer = _peer(r)[0]
        for a in range(n):
            out.append(pltpu.make_async_remote_copy(
                src_ref=src_refs[a] if src is None else src_refs[a].at[_slot(src)],
                dst_ref=land_refs[a].at[_slot(recv if arriving else dst)],
                send_sem=send_sems.at[i * n + a], recv_sem=recv_sems.at[i * n + a],
                device_id=peer, device_id_type=MESH))
    return out


def _exchange_hbm(plan, srcs, lands, name, after=()):
    n = len(lands)
    fresh = isinstance(lands[0], jax.ShapeDtypeStruct)
    ins = ([] if srcs is None else list(srcs)) + ([] if fresh else list(lands))
    ns = 0 if srcs is None else n
    n_data = len(ins)
    ins = ins + list(after)

    def body(*refs):
        outs = refs[len(ins):len(ins) + n]
        send_sems, recv_sems = refs[-2:]
        src_refs = refs[:n] if srcs is not None else refs[ns:ns + n]
        sends = _plan_copies(plan, n, src_refs, outs, send_sems, recv_sems, False)
        for cp in sends:
            cp.start()
        for cp in _plan_copies(plan, n, src_refs, outs, send_sems, recv_sems, True):
            cp.wait_recv()
        for cp in sends:
            cp.wait_send()

    hbm = pl.BlockSpec(memory_space=pltpu.HBM)
    k = len(plan) * n
    return pl.pallas_call(
        body, name=name, in_specs=[hbm] * n_data + [pl.BlockSpec(memory_space=pl.ANY)] * len(after), out_specs=[hbm] * n,
        out_shape=[jax.ShapeDtypeStruct(a.shape, a.dtype) for a in lands],
        scratch_shapes=[pltpu.SemaphoreType.DMA((k,)), pltpu.SemaphoreType.DMA((k,))],
        input_output_aliases={} if fresh else {ns + i: i for i in range(n)},
        compiler_params=pltpu.CompilerParams(vmem_limit_bytes=VMEM_LIMIT))(*ins)


def _plan_start(plan, srcs, land_shapes, after, name):
    n = len(srcs)

    def body(*refs):
        src_refs, land_refs = refs[:n], refs[n:2 * n]
        send_sems, recv_sems = refs[2 * n + 1], refs[2 * n + 2]
        for cp in _plan_copies(plan, n, src_refs, land_refs, send_sems, recv_sems, False):
            cp.start()
        refs[-1][...] = jnp.zeros_like(refs[-1])

    hbm = pl.BlockSpec(memory_space=pltpu.HBM)
    sem = pl.BlockSpec(memory_space=pltpu.SEMAPHORE)
    k = len(plan) * n
    srcs = [pltpu.with_memory_space_constraint(a, pltpu.HBM) for a in srcs]
    lands = [pltpu.with_memory_space_constraint(lax.empty(shp, a.dtype), pltpu.HBM) for shp, a in zip(land_shapes, srcs)]
    res = pl.pallas_call(
        body, name=name,
        in_specs=[hbm] * (2 * n) + [pl.BlockSpec(memory_space=pl.ANY)],
        out_specs=[sem, sem] + [hbm] * (2 * n) + [pl.BlockSpec(memory_space=pltpu.VMEM)],
        out_shape=[pltpu.SemaphoreType.DMA((k,)), pltpu.SemaphoreType.DMA((k,))]
        + [pltpu.HBM(a.shape, a.dtype) for a in srcs] + [pltpu.HBM(shp, a.dtype) for shp, a in zip(land_shapes, srcs)]
        + [jax.ShapeDtypeStruct((8, LANES), F32)],
        input_output_aliases={i: 2 + i for i in range(2 * n)},
        compiler_params=pltpu.CompilerParams(has_side_effects=pltpu.SideEffectType.DATAFLOW_SIDE_EFFECTING),
    )(*srcs, *lands, after)
    return res[0], res[1], list(res[2:2 + n]), list(res[2 + n:2 + 2 * n]), res[-1]


def _plan_wait(plan, handle, after, name):
    send_sems, recv_sems, srcs, lands, _ = handle
    n = len(srcs)
    after = list(after)

    def body(*refs):
        src_refs, land_refs = refs[:n], refs[n:2 * n]
        ssem, rsem = refs[2 * n], refs[2 * n + 1]
        for cp in _plan_copies(plan, n, src_refs, land_refs, ssem, rsem, False):
            cp.wait_send()
        for cp in _plan_copies(plan, n, src_refs, land_refs, ssem, rsem, True):
            cp.wait_recv()

    hbm = pl.BlockSpec(memory_space=pltpu.HBM)
    sem = pl.BlockSpec(memory_space=pltpu.SEMAPHORE)
    res = pl.pallas_call(
        body, name=name,
        in_specs=[hbm] * (2 * n) + [sem, sem] + [pl.BlockSpec(memory_space=pl.ANY)] * len(after),
        out_specs=[hbm] * (2 * n),
        out_shape=[pltpu.HBM(a.shape, a.dtype) for a in srcs] + [pltpu.HBM(a.shape, a.dtype) for a in lands],
        input_output_aliases={i: i for i in range(2 * n)},
        compiler_params=pltpu.CompilerParams(has_side_effects=pltpu.SideEffectType.DATAFLOW_SIDE_EFFECTING),
    )(*srcs, *lands, send_sems, recv_sems, *after)
    return list(res[:n]), list(res[n:])


def _pair_sum(send, got, core, name):
    _, r, c = send.shape
    tr = 1024 if r % 1024 == 0 else r

    def body(core_ref, s_ref, g_ref, o_ref):
        o_ref[0] = (s_ref[0].astype(F32) + g_ref[0].astype(F32)).astype(o_ref.dtype)

    return pl.pallas_call(
        body, name=name,
        grid_spec=pltpu.PrefetchScalarGridSpec(
            num_scalar_prefetch=1, grid=(N_DEV // 2, r // tr),
            in_specs=[pl.BlockSpec((1, tr, c), lambda kc, i, core_ref: (2 * kc + core_ref[0], i, 0)),
                      pl.BlockSpec((1, tr, c), lambda kc, i, core_ref: (kc, i, 0))],
            out_specs=pl.BlockSpec((1, tr, c), lambda kc, i, core_ref: (kc, i, 0))),
        out_shape=jax.ShapeDtypeStruct((N_DEV // 2, r, c), send.dtype),
        compiler_params=_cp(("parallel", "parallel")))(core, send, got)


def _ada_mod(c_all, ada_w, ada_b_cols, name):
    nl, d, wc = ada_w.shape

    def body(c_ref, w_ref, b_ref, ca_ref, mod_ref):
        cv = c_ref[...]
        ca = cv * _sigmoid(cv)
        ca_ref[...] = ca
        mod_ref[0] = _dotf(ca, w_ref[0]) + b_ref[0]

    return pl.pallas_call(
        body, grid=(nl,),
        in_specs=[pl.BlockSpec((N_DEV, d), lambda l: (0, 0)), pl.BlockSpec((1, d, wc), lambda l: (l, 0, 0)),
                  pl.BlockSpec((1, 1, wc), lambda l: (l, 0, 0))],
        out_specs=[pl.BlockSpec((N_DEV, d), lambda l: (0, 0)), pl.BlockSpec((1, N_DEV, wc), lambda l: (l, 0, 0))],
        out_shape=[jax.ShapeDtypeStruct((N_DEV, d), F32), jax.ShapeDtypeStruct((nl, N_DEV, wc), F32)],
        name=name, compiler_params=_cp(("arbitrary",)))(c_all, ada_w, ada_b_cols)


def _adam(w, g, m, v):
    m2 = ADAM_B1 * m + (1.0 - ADAM_B1) * g
    v2 = ADAM_B2 * v + (1.0 - ADAM_B2) * (g * g)
    m_hat = m2 / (1.0 - ADAM_B1 ** ADAM_STEP)
    v_hat = v2 / (1.0 - ADAM_B2 ** ADAM_STEP)
    delta = -ADAM_LR * (m_hat / (jnp.sqrt(v_hat) + ADAM_EPS) + ADAM_WD * w)
    return delta, m2, v2


def _ada_grad_adam(c_act, d_mod, w, m, v, name):
    nl, d, wc = w.shape
    tk = min(1024, d)

    def body(c_ref, dm_ref, w_ref, m_ref, v_ref, g_ref, dl_ref, m2_ref, v2_ref):
        g = _dotf_tn(c_ref[...], dm_ref[0])
        delta, m2, v2 = _adam(w_ref[0], g, m_ref[0], v_ref[0])
        g_ref[0], dl_ref[0], m2_ref[0], v2_ref[0] = g, delta, m2, v2

    blk = pl.BlockSpec((1, tk, wc), lambda l, i: (l, i, 0))
    shp = jax.ShapeDtypeStruct(w.shape, F32)
    return pl.pallas_call(
        body, grid=(nl, d // tk),
        in_specs=[pl.BlockSpec((N_DEV, tk), lambda l, i: (0, i)), pl.BlockSpec((1, N_DEV, wc), lambda l, i: (l, 0, 0)),
                  blk, blk, blk],
        out_specs=[blk] * 4, out_shape=[shp] * 4, name=name,
        compiler_params=_cp(("parallel", "parallel")))(c_act, d_mod, w, m, v)


def _adam_big(recv, w, m, v, layer, prev, name, after=()):
    nl, r, c = w.shape
    tr = 512 if r % 512 == 0 else r
    nparts = recv.shape[0]

    def body(rc_ref, w_ref, m_ref, v_ref, *rest):
        g_ref, dl_ref, m2_ref, v2_ref = rest[-4:]
        g = rc_ref[0].astype(F32)
        for d in range(1, nparts):
            g = g + rc_ref[d].astype(F32)
        delta, m2, v2 = _adam(w_ref[0], g, m_ref[0], v_ref[0])
        g_ref[0], dl_ref[0], m2_ref[0], v2_ref[0] = g, delta, m2, v2

    blk = pl.BlockSpec((1, tr, c), lambda i: (layer, i, 0))
    shp = jax.ShapeDtypeStruct(w.shape, F32)
    prev = () if prev is None else tuple(prev)
    return pl.pallas_call(
        body, grid=(r // tr,),
        in_specs=[pl.BlockSpec((nparts, tr, c), lambda i: (0, i, 0)), blk, blk, blk]
        + [pl.BlockSpec(memory_space=pl.ANY)] * (len(prev) + len(after)),
        out_specs=[blk] * 4, out_shape=[shp] * 4, name=name,
        input_output_aliases={4 + j: j for j in range(len(prev))},
        compiler_params=_cp(("parallel",)))(recv, w, m, v, *prev, *after)


def _sum_devices(gathered, name):
    _, r, c = gathered.shape

    def body(g_ref, o_ref):
        acc = g_ref[0]
        for d in range(1, N_DEV):
            acc = acc + g_ref[d]
        o_ref[...] = acc

    spec = pl.BlockSpec(memory_space=pltpu.VMEM)
    return pl.pallas_call(body, in_specs=[spec], out_specs=spec, out_shape=jax.ShapeDtypeStruct((r, c), F32),
                          name=name, compiler_params=pltpu.CompilerParams(vmem_limit_bytes=VMEM_LIMIT))(gathered)


def _adam_small(ws, gs, ms, vs, name):
    n = len(ws)

    def body(*refs):
        for i in range(n):
            w_ref, g_ref, m_ref, v_ref = (refs[k * n + i] for k in range(4))
            dl_ref, m2_ref, v2_ref = (refs[(4 + k) * n + i] for k in range(3))
            dl_ref[...], m2_ref[...], v2_ref[...] = _adam(w_ref[...], g_ref[...], m_ref[...], v_ref[...])

    spec = pl.BlockSpec(memory_space=pltpu.VMEM)
    shapes = [jax.ShapeDtypeStruct(w.shape, F32) for w in ws]
    res = pl.pallas_call(body, in_specs=[spec] * (4 * n), out_specs=[spec] * (3 * n), out_shape=shapes * 3, name=name,
                         compiler_params=pltpu.CompilerParams(vmem_limit_bytes=VMEM_LIMIT))(*ws, *gs, *ms, *vs)
    return res[:n], res[n:2 * n], res[2 * n:]


def _pack(parts):
    flat = jnp.concatenate([p.reshape(-1).astype(F32) for p in parts])
    assert flat.shape[0] % LANES == 0, flat.shape
    return flat.reshape(-1, LANES)


def _unpack(packed, shapes):
    flat = packed.reshape(-1)
    out, off = [], 0
    for shp in shapes:
        size = 1
        for dim in shp:
            size *= dim
        out.append(flat[off:off + size].reshape(shp))
        off += size
    return out


def _gather_cols(g, per):
    g = jnp.moveaxis(g, 0, -2)
    return g.reshape(g.shape[:-2] + (N_DEV * per,))


def _scatter_cols(g, per):
    return jnp.moveaxis(g.reshape(g.shape[:-1] + (N_DEV, per)), -2, 0)


def _my_cols(full, me, per):
    return lax.dynamic_slice_in_dim(full, me * per, per, axis=full.ndim - 1)


def kernel(x, c, positions, ada_w, ada_b, norm_g, w_in, gla_wg_f, gla_bg_f, gla_wg_b, gla_bg_b, gla_norm_g, mla_q_norm_g, mla_kv_norm_g, mla_w_uq, mla_w_ukv, mla_out_g, conv_w, conv_out_g, w_out, final_g, loss_target, m_ada_w, m_ada_b, m_norm_g, m_w_in, m_gla_wg_f, m_gla_bg_f, m_gla_wg_b, m_gla_bg_b, m_gla_norm_g, m_mla_q_norm_g, m_mla_kv_norm_g, m_mla_w_uq, m_mla_w_ukv, m_mla_out_g, m_conv_w, m_conv_out_g, m_w_out, m_final_g, v_ada_w, v_ada_b, v_norm_g, v_w_in, v_gla_wg_f, v_gla_bg_f, v_gla_wg_b, v_gla_bg_b, v_gla_norm_g, v_mla_q_norm_g, v_mla_kv_norm_g, v_mla_w_uq, v_mla_w_ukv, v_mla_out_g, v_conv_w, v_conv_out_g, v_w_out, v_final_g):
    me = 4 * lax.axis_index("x") + 2 * lax.axis_index("y") + lax.axis_index("c")
    nl = ada_w.shape[0]
    s, d = x.shape[1], x.shape[2]
    ada_cols = ada_w.shape[2]
    wgc, cwc = gla_wg_f.shape[2], conv_w.shape[2]

    (g0,) = _gather_small([_pack([c, gla_wg_f, gla_wg_b, conv_w])], "gather_small_in")
    g0 = g0.reshape(N_DEV, -1)
    o1, o2, o3 = d, d + gla_wg_f.size, d + 2 * gla_wg_f.size
    c_all = g0[:, :o1]
    wgf_full = _gather_cols(g0[:, o1:o2].reshape((N_DEV,) + gla_wg_f.shape), wgc)
    wgb_full = _gather_cols(g0[:, o2:o3].reshape((N_DEV,) + gla_wg_b.shape), wgc)
    convw_full = _gather_cols(g0[:, o3:].reshape((N_DEV,) + conv_w.shape), cwc)

    ada_b_cols = _my_cols(ada_b, me, ada_cols).reshape(nl, 1, ada_cols)
    c_act, mod_cols = _ada_mod(c_all, ada_w, ada_b_cols, "ada_mod")
    (g1,) = _gather_small([_pack([mod_cols])], "gather_mod")
    mod_all = g1.reshape(N_DEV, nl, N_DEV, ada_cols)
    mod_mine = _gather_cols(lax.dynamic_index_in_dim(mod_all, me, axis=2, keepdims=False), ada_cols)

    inv_freq = ROPE_THETA ** (-jnp.arange(0, MROPE, 2, dtype=F32) / MROPE)
    ang = positions[0].astype(F32)[:, None] * inv_freq
    cos, sin = jnp.tile(jnp.cos(ang), (1, LANES * 2 // MROPE)), jnp.tile(jnp.sin(ang), (1, LANES * 2 // MROPE))

    big = [w_in, w_out, mla_w_uq, mla_w_ukv]
    big_names = ["w_in", "w_out", "mla_w_uq", "mla_w_ukv"]

    def local_blocks(l):
        return [w[l].astype(MXU) for w in big]

    def put_own(lands, own):
        return [lax.dynamic_update_index_in_dim(ld, o, me, 0) for ld, o in zip(lands, own)]

    def layer_weights(l, gw_in=None, gw_out=None, gw_uq=None, gw_ukv=None):
        small = dict(norm_g=norm_g[l], gla_wg_f=wgf_full[l], gla_bg_f=gla_bg_f[l], gla_wg_b=wgb_full[l],
                     gla_bg_b=gla_bg_b[l], gla_norm_g=gla_norm_g[l], mla_q_norm_g=mla_q_norm_g[l],
                     mla_kv_norm_g=mla_kv_norm_g[l], mla_out_g=mla_out_g[l], conv_w=convw_full[l],
                     conv_out_g=conv_out_g[l])
        return _prep_layer_weights(
            None if gw_in is None else _perm_gathered(gw_in, IN_SEGS, PW),
            None if gw_out is None else gw_out.reshape((-1,) + gw_out.shape[2:]),
            None if gw_uq is None else _heads_apart(_gather_cols(gw_uq, mla_w_uq.shape[2])),
            None if gw_ukv is None else _gather_cols(gw_ukv, mla_w_ukv.shape[2]), small)

    def land_shapes(blocks, slots):
        return [jax.ShapeDtypeStruct((slots,) + b.shape, b.dtype) for b in blocks]

    def slots_of(blocks):
        return [(N_DEV,) + b.shape for b in blocks]

    def forwarded(lands, blocks, tag):
        return put_own(_exchange_hbm(AG_FORWARD, None, lands, f"gather_{tag}_forward"), blocks)

    first = local_blocks(0)
    w_in_start = _plan_start(AG_SPREAD, first[:1], slots_of(first[:1]), mod_mine, "gather_w_in_l0_start")
    adam_w_in = [a + w_in_start[-1][0, 0] for a in (w_in, m_w_in, v_w_in)]
    shift0, scale0 = (mod_mine[0, i * d:(i + 1) * d].reshape(1, d) for i in range(2))
    h_first = _norm_mod(x[0], norm_g[0].reshape(1, d), scale0 + w_in_start[-1][0, 0], shift0, "norm_mod_l0")
    (gw_in,) = forwarded(*reversed(_plan_wait(AG_SPREAD, w_in_start, adam_w_in + [h_first], "gather_w_in_l0_wait")),
                         "w_in_l0")
    rest = _plan_start(AG_SPREAD, first[1:], slots_of(first[1:]), gw_in, "gather_rest_l0_start")
    h = x[0]
    saved, layers, mods = [], [], []
    pending = {}
    for l in range(nl):
        shift, scale, gate = (mod_mine[l, i * d:(i + 1) * d].reshape(1, d) for i in range(3))
        nxt = local_blocks(l + 1) if l + 1 < nl else None

        def start_next(after, wt_late, l=l, nxt=nxt):
            if nxt is not None:
                pending[l + 1] = _plan_start(AG_SPREAD, nxt, slots_of(nxt), after, f"gather_weights_l{l + 1}_start")
                wt_late["q_norm_g"] = layers[l]["q_norm_g"] + pending[l + 1][-1][0, 0]
            return wt_late

        if l == 0:
            in_after = (rest[-1],)
            layers.append(layer_weights(0, gw_in))

            def late(proj):
                got = forwarded(*reversed(_plan_wait(AG_SPREAD, rest, [proj], "gather_rest_l0_wait")), "rest_l0")
                full = layer_weights(0, None, *got)
                return start_next(got[0], {k: full[k] for k in ("w_out", "w_uq", "w_ukv")})
        else:
            got = forwarded(*reversed(_plan_wait(AG_SPREAD, pending.pop(l), [h], f"gather_weights_l{l}_wait")), f"weights_l{l}")
            layers.append(layer_weights(l, *got))
            in_after = ()

            def late(proj):
                return start_next(proj, {})
        mods.append((shift, scale, gate))
        h, sv = _layer_fwd(h, mods[l], layers[l], cos, sin, f"l{l}", late, in_after, h_first if l == 0 else None)
        saved.append(sv)
        blocks = nxt
    loss_part, d_h, d_final_g = _final_loss(h, final_g.reshape(1, d), loss_target[0], "final_loss")
    loss = lax.psum(loss_part[0, 0], ("x", "y", "c"))
    shift, scale, gate = mods[-1]
    mods[-1] = (shift, scale, gate + 0.0 * loss)

    send_of = dict(w_in=lambda g: _scatter_perm(g, IN_SEGS, w_in.shape[2]),
                   w_out=lambda g: g.reshape((N_DEV,) + w_out.shape[1:]),
                   w_uq=lambda g: _scatter_cols(_heads_together(g), mla_w_uq.shape[2]),
                   w_ukv=lambda g: _scatter_cols(g, mla_w_ukv.shape[2]))

    def grad_sends(gr):
        return [send_of[k](g).astype(MXU) for k, g in gr.items()]

    my_chip = me // 2
    my_core = (me % 2).astype(jnp.int32).reshape(1)

    def chip_sums(gr, tag):
        sends = grad_sends(gr)
        got = _exchange_hbm(RS_PAIR, sends, land_shapes([sd[0] for sd in sends], N_DEV // 2), f"scatter_grads_{tag}_pair")
        return [_pair_sum(sd, gt, my_core, f"pair_sum_{k}_{tag}") for sd, gt, k in zip(sends, got, gr)]

    def with_own_chip(lands, sums):
        return [lax.dynamic_update_index_in_dim(ld, lax.dynamic_index_in_dim(sm, my_chip, axis=0, keepdims=False),
                                                my_chip, 0) for ld, sm in zip(lands, sums)]

    small_names = ["norm_g", "gla_wg_f", "gla_bg_f", "gla_wg_b", "gla_bg_b", "gla_norm_g", "mla_q_norm_g",
                   "mla_kv_norm_g", "mla_out_g", "conv_w", "conv_out_g"]
    d_mods, grads, recv = [None] * nl, [None] * nl, [None] * nl
    flight = {}
    small = {}

    def gather_small(d_x, d_mod0, gr0):
        d_mods[0], grads[0] = d_mod0, _natural_small(gr0)
        d_mod_mine = jnp.stack([jnp.concatenate(d_mods[l], axis=-1)[0] for l in range(nl)])
        parts = [d_mod_mine] + [jnp.stack([grads[l][n] for l in range(nl)]) for n in small_names] + [d_final_g]
        (g2,) = _gather_small([_pack(parts)], "gather_small_grads")
        small["d_mod_all"] = g2.reshape(N_DEV, -1)[:, :d_mod_mine.size].reshape(N_DEV, nl, 3 * d)
        small["summed"] = dict(zip(["ada_b"] + small_names + ["final_g"],
                                   _unpack(_sum_devices(g2, "sum_small_grads"), [p.shape for p in parts])))
        return (g2,)

    pairs = {}
    def end_flight(key, after, name):
        sm, lands = _plan_wait(RS_CHIPS, flight.pop(key)[0], after, name)
        return with_own_chip(lands, sm)

    for l in reversed(range(nl)):
        def ship(big_grads, l=l):
            if l > 0:
                sends = grad_sends(big_grads)
                pairs[l] = (_plan_start(RS_PAIR, sends, [(N_DEV // 2,) + sd.shape[1:] for sd in sends],
                                        big_grads["w_in"], f"scatter_grads_l{l}_pair_start"), sends)
                return (pairs[l][0][-1],)
            sm = chip_sums(dict(w_in=big_grads["w_in"]), f"l{l}")
            flight[l] = (_plan_start(RS_CHIPS, sm, [a.shape for a in sm], big_grads["w_in"], f"scatter_grads_l{l}_start"),
                         sm)
            return (flight[l][0][-1],)

        def ship_rest(rest_grads, l=l):
            if l + 1 in flight:
                recv[l + 1] = end_flight(l + 1, list(rest_grads.values()), f"scatter_grads_l{l + 1}_wait")
            sm = chip_sums(rest_grads, f"l{l}_rest")
            flight["rest"] = (_plan_start(RS_CHIPS, sm, [a.shape for a in sm], rest_grads["w_out"],
                                          f"scatter_grads_l{l}_rest_start"), sm)
            return flight["rest"][0][-1][0, 0]

        shift, scale, gate = mods[l]
        if l + 1 in flight:
            gate = gate + flight[l + 1][0][-1][0, 0]
        if l > 0:
            d_h, d_mods[l], gr = _layer_bwd(d_h, saved[l], (shift, scale, gate), layers[l], cos, sin, f"l{l}", ship)
            grads[l] = _natural_small(gr)
            sends, got = _plan_wait(RS_PAIR, pairs.pop(l)[0], [d_h], f"scatter_grads_l{l}_pair_wait")
            sm = [_pair_sum(sd, gt, my_core, f"pair_sum_{n}_l{l}") for sd, gt, n in zip(sends, got, big_names)]
            flight[l] = (_plan_start(RS_CHIPS, sm, [a.shape for a in sm], d_h, f"scatter_grads_l{l}_start"), sm)
        else:
            d_h, _, _ = _layer_bwd(d_h, saved[l], (shift, scale, gate), layers[l], cos, sin, f"l{l}", ship, gather_small,
                                   ship_rest)
    pending = flight[0][0]
    grad_x = d_h[None]
    summed = small["summed"]
    summed["gla_wg_f"] = _my_cols(summed["gla_wg_f"], me, wgc)
    summed["gla_wg_b"] = _my_cols(summed["gla_wg_b"], me, wgc)
    summed["conv_w"] = _my_cols(summed["conv_w"], me, cwc)

    d_mod_cols = jnp.moveaxis(_my_cols(small["d_mod_all"], me, ada_cols), 0, 1) + pending[-1][0, 0]
    out = {}
    out["ada_w"] = _ada_grad_adam(c_act, d_mod_cols, ada_w, m_ada_w, v_ada_w, "ada_grad_adam")

    given = dict(ada_b=(ada_b, m_ada_b, v_ada_b), norm_g=(norm_g, m_norm_g, v_norm_g),
                 gla_wg_f=(gla_wg_f, m_gla_wg_f, v_gla_wg_f), gla_bg_f=(gla_bg_f, m_gla_bg_f, v_gla_bg_f),
                 gla_wg_b=(gla_wg_b, m_gla_wg_b, v_gla_wg_b), gla_bg_b=(gla_bg_b, m_gla_bg_b, v_gla_bg_b),
                 gla_norm_g=(gla_norm_g, m_gla_norm_g, v_gla_norm_g),
                 mla_q_norm_g=(mla_q_norm_g, m_mla_q_norm_g, v_mla_q_norm_g),
                 mla_kv_norm_g=(mla_kv_norm_g, m_mla_kv_norm_g, v_mla_kv_norm_g),
                 mla_out_g=(mla_out_g, m_mla_out_g, v_mla_out_g), conv_w=(conv_w, m_conv_w, v_conv_w),
                 conv_out_g=(conv_out_g, m_conv_out_g, v_conv_out_g), final_g=(final_g, m_final_g, v_final_g))
    names = list(given)

    def two_d(a):
        return a.reshape(1, -1) if a.ndim == 1 else a

    g_nat = [summed[n].reshape(given[n][0].shape) for n in names]
    res = _adam_small([two_d(given[n][0]) for n in names], [two_d(g) for g in g_nat],
                      [two_d(given[n][1]) for n in names], [two_d(given[n][2]) for n in names], "adam_small")
    for i, n in enumerate(names):
        out[n] = (g_nat[i],) + tuple(r[i].reshape(given[n][0].shape) for r in res)

    state = dict(w_in=adam_w_in, w_out=(w_out, m_w_out, v_w_out), mla_w_uq=(mla_w_uq, m_mla_w_uq, v_mla_w_uq),
                 mla_w_ukv=(mla_w_ukv, m_mla_w_ukv, v_mla_w_ukv))
    done = [out["ada_w"][0], res[0][0]]
    for l in reversed(range(nl)):
        if l == 0:
            rest = end_flight("rest", done, "scatter_grads_l0_rest_wait")
            recv[0] = end_flight(0, done + rest[:1], "scatter_grads_l0_wait") + rest
        for i, n in enumerate(big_names):
            out[n] = _adam_big(recv[l][i], *state[n], l, out.get(n), f"adam_{n}_l{l}",
                               (pending[-1],))
        done = done + [out[n][0] for n in big_names]

    order = ["ada_w", "ada_b", "norm_g", "w_in", "gla_wg_f", "gla_bg_f", "gla_wg_b", "gla_bg_b", "gla_norm_g",
             "mla_q_norm_g", "mla_kv_norm_g", "mla_w_uq", "mla_w_ukv", "mla_out_g", "conv_w", "conv_out_g", "w_out",
             "final_g"]
    return (loss, grad_x, *[out[n][0] for n in order], *[out[n][1] for n in order], *[out[n][2] for n in order],
            *[out[n][3] for n in order])
```

```python
import functools

import jax
import jax.numpy as jnp
from jax import lax
from jax.experimental import pallas as pl
from jax.experimental.pallas import tpu as pltpu

F32 = jnp.float32
MXU = jnp.bfloat16
HI = lax.Precision.HIGHEST
N_DEV = 8
MESH = pl.DeviceIdType.MESH

D_MIX = 2048
GH, GDK, GDV = 6, 64, 128
GW = GH * GDV
GQK = GH * GDK
GRANK = 16
GTEMP = 16.0
CHUNK = 64
MH, MQL, MKVL, MNOPE, MROPE, MDV = 6, 384, 256, 128, 64, 128
MW = MH * MDV
MQW = MH * (MNOPE + MROPE)
MKVW = MH * (MNOPE + MDV)
CONV_CH = 512
ROPE_THETA = 10000.0
EPS = 1e-6
IN_DIM = 5856
OZ, OCB, OCC, OCX, OMKV, OGV, OGQ, OGK, OMQ, OT = 0, 2048, 2560, 3072, 3584, 3840, 4608, 4992, 5376, 5760
PW = 5888
LANES = 128
V7X_VMEM_BYTES = 64 * 1024 * 1024
VMEM_LIMIT = V7X_VMEM_BYTES * 7 // 8

ADAM_LR, ADAM_B1, ADAM_B2, ADAM_EPS, ADAM_WD, ADAM_STEP = 0.001, 0.9, 0.999, 1e-08, 0.01, 10


def _cp(sem=None):
    return pltpu.CompilerParams(dimension_semantics=sem, vmem_limit_bytes=VMEM_LIMIT)


def _dot(a, b):
    return jnp.dot(a.astype(MXU), b.astype(MXU), preferred_element_type=F32)


def _dot_nt(a, b):
    return lax.dot_general(a.astype(MXU), b.astype(MXU), (((1,), (1,)), ((), ())), preferred_element_type=F32)


def _dot_tn(a, b):
    return lax.dot_general(a.astype(MXU), b.astype(MXU), (((0,), (0,)), ((), ())), preferred_element_type=F32)


def _dotf(a, b):
    return jnp.dot(a, b, precision=HI, preferred_element_type=F32)


def _dotf_tn(a, b):
    return lax.dot_general(a, b, (((0,), (0,)), ((), ())), precision=HI, preferred_element_type=F32)


def _split3(x):
    hi = x.astype(jnp.bfloat16)
    r1 = x - hi.astype(F32)
    mid = r1.astype(jnp.bfloat16)
    lo = (r1 - mid.astype(F32)).astype(jnp.bfloat16)
    return hi, mid, lo


def _cum_dot(cum, x, transpose=False):
    dn = (((0,), (0,)), ((), ())) if transpose else (((1,), (0,)), ((), ()))
    cb = cum.astype(jnp.bfloat16)
    parts = [lax.dot_general(cb, p, dn, preferred_element_type=F32) for p in _split3(x)]
    return parts[0] + parts[1] + parts[2]


def _rows(s):
    return min(256, s)


def _rows_light(s):
    return min(512, s)


def _rms(x, g):
    r = lax.rsqrt(jnp.mean(x * x, axis=-1, keepdims=True) + EPS)
    return x * r * g


def _rms_bwd(dy, x, g):
    r = lax.rsqrt(jnp.mean(x * x, axis=-1, keepdims=True) + EPS)
    xh = x * r
    dxh = dy * g
    dg = jnp.sum(dy * xh, axis=0, keepdims=True)
    dx = r * (dxh - xh * jnp.mean(dxh * xh, axis=-1, keepdims=True))
    return dx, dg


def _sigmoid(z):
    return jax.nn.sigmoid(z)


def _matmul(a, b, *, dims, tm, tn, tk, out_dtypes, name, epilogue=None, extras=(), extra_kinds=(), after=()):
    if dims == "nn":
        (m, k), n, mul = a.shape, b.shape[1], _dot
    elif dims == "nt":
        (m, k), n, mul = a.shape, b.shape[0], _dot_nt
    else:
        (k, m), n, mul = a.shape, b.shape[1], _dot_tn
    tm, tn, tk = min(tm, m), min(tn, n), min(tk, k)
    assert m % tm == 0 and n % tn == 0 and k % tk == 0, (m, n, k, tm, tn, tk)
    if dims == "nn":
        a_spec = pl.BlockSpec((tm, tk), lambda i, j, kk: (i, kk))
        b_spec = pl.BlockSpec((tk, tn), lambda i, j, kk: (kk, j))
    elif dims == "nt":
        a_spec = pl.BlockSpec((tm, tk), lambda i, j, kk: (i, kk))
        b_spec = pl.BlockSpec((tn, tk), lambda i, j, kk: (j, kk))
    else:
        a_spec = pl.BlockSpec((tk, tm), lambda i, j, kk: (kk, i))
        b_spec = pl.BlockSpec((tk, tn), lambda i, j, kk: (kk, j))
    nk = k // tk
    n_extra = len(extras)
    n_out = len(out_dtypes)
    n_after = len(after)
    extra_specs = []
    for kind in extra_kinds:
        if kind == "mn":
            extra_specs.append(pl.BlockSpec((tm, tn), lambda i, j, kk: (i, j)))
        else:
            extra_specs.append(pl.BlockSpec((1, tn), lambda i, j, kk: (0, j)))

    def finish(res, ex, outs):
        vals = (res,) if epilogue is None else epilogue(res, *[e[...] for e in ex])
        for o, v in zip(outs, vals):
            o[...] = v.astype(o.dtype)

    def body(*refs):
        a_ref, b_ref = refs[0], refs[1]
        ex = refs[2:2 + n_extra]
        outs = refs[2 + n_extra + n_after:2 + n_extra + n_after + n_out]
        if nk == 1:
            finish(mul(a_ref[...], b_ref[...]), ex, outs)
            return
        acc = refs[-1]
        kk = pl.program_id(2)

        @pl.when(kk == 0)
        def _():
            acc[...] = jnp.zeros_like(acc)

        acc[...] += mul(a_ref[...], b_ref[...])

        @pl.when(kk == nk - 1)
        def _():
            finish(acc[...], ex, outs)

    out_spec = pl.BlockSpec((tm, tn), lambda i, j, kk: (i, j))
    res = pl.pallas_call(
        body, grid=(m // tm, n // tn, nk),
        in_specs=[a_spec, b_spec] + extra_specs + [pl.BlockSpec(memory_space=pl.ANY)] * n_after,
        out_specs=[out_spec] * n_out,
        out_shape=[jax.ShapeDtypeStruct((m, n), dt) for dt in out_dtypes],
        scratch_shapes=[] if nk == 1 else [pltpu.VMEM((tm, tn), F32)],
        name=name, compiler_params=_cp(("parallel", "parallel", "arbitrary")),
    )(a, b, *extras, *after)
    return res


def _norm_mod(x, g, scale, shift, name):
    s, d = x.shape
    tr = _rows_light(s)

    def body(x_ref, g_ref, sc_ref, sh_ref, h_ref):
        h = _rms(x_ref[...], g_ref[...]) * (1.0 + sc_ref[...]) + sh_ref[...]
        h_ref[...] = h.astype(h_ref.dtype)

    row = pl.BlockSpec((tr, d), lambda i: (i, 0))
    vec = pl.BlockSpec((1, d), lambda i: (0, 0))
    return pl.pallas_call(body, grid=(s // tr,), in_specs=[row, vec, vec, vec], out_specs=row,
                          out_shape=jax.ShapeDtypeStruct((s, d), MXU), name=name,
                          compiler_params=_cp(("parallel",)))(x, g, scale, shift)


def _norm_mod_bwd(d_h, x, d_out, g, scale, name):
    s, d = x.shape
    tr = _rows(s)

    def body(dh_ref, x_ref, do_ref, g_ref, sc_ref, dx_ref, dsh_ref, dsc_ref, dg_ref):
        i = pl.program_id(0)

        @pl.when(i == 0)
        def _():
            dsh_ref[...] = jnp.zeros_like(dsh_ref)
            dsc_ref[...] = jnp.zeros_like(dsc_ref)
            dg_ref[...] = jnp.zeros_like(dg_ref)

        dh = dh_ref[...]
        xv = x_ref[...]
        gv = g_ref[...]
        r = lax.rsqrt(jnp.mean(xv * xv, axis=-1, keepdims=True) + EPS)
        xh = xv * r
        dsh_ref[...] += jnp.sum(dh, axis=0, keepdims=True)
        dsc_ref[...] += jnp.sum(dh * (xh * gv), axis=0, keepdims=True)
        dhn = dh * (1.0 + sc_ref[...])
        dg_ref[...] += jnp.sum(dhn * xh, axis=0, keepdims=True)
        dxh = dhn * gv
        dx_ref[...] = do_ref[...] + r * (dxh - xh * jnp.mean(dxh * xh, axis=-1, keepdims=True))

    row = pl.BlockSpec((tr, d), lambda i: (i, 0))
    vec = pl.BlockSpec((1, d), lambda i: (0, 0))
    vshape = jax.ShapeDtypeStruct((1, d), F32)
    return pl.pallas_call(body, grid=(s // tr,), in_specs=[row, row, row, vec, vec],
                          out_specs=[row, vec, vec, vec],
                          out_shape=[jax.ShapeDtypeStruct((s, d), F32), vshape, vshape, vshape],
                          name=name, compiler_params=_cp(("arbitrary",)))(d_h, x, d_out, g, scale)


def _gate_bwd(d_out, u, gate, name):
    s, d = d_out.shape
    tr = _rows_light(s)

    def body(do_ref, u_ref, gt_ref, du_ref, dgt_ref):
        @pl.when(pl.program_id(0) == 0)
        def _():
            dgt_ref[...] = jnp.zeros_like(dgt_ref)

        do = do_ref[...]
        du_ref[...] = (do * gt_ref[...]).astype(du_ref.dtype)
        dgt_ref[...] += jnp.sum(do * u_ref[...], axis=0, keepdims=True)

    row = pl.BlockSpec((tr, d), lambda i: (i, 0))
    vec = pl.BlockSpec((1, d), lambda i: (0, 0))
    return pl.pallas_call(body, grid=(s // tr,), in_specs=[row, row, vec], out_specs=[row, vec],
                          out_shape=[jax.ShapeDtypeStruct((s, d), MXU), jax.ShapeDtypeStruct((1, d), F32)],
                          name=name, compiler_params=_cp(("arbitrary",)))(d_out, u, gate)


def _final_loss(x, g, target, name):
    s, d = x.shape
    tr = _rows_light(s)

    def body(x_ref, g_ref, t_ref, loss_ref, dx_ref, dg_ref):
        @pl.when(pl.program_id(0) == 0)
        def _():
            loss_ref[...] = jnp.zeros_like(loss_ref)
            dg_ref[...] = jnp.zeros_like(dg_ref)

        xv = x_ref[...]
        gv = g_ref[...]
        diff = _rms(xv, gv) - t_ref[...]
        part = 0.5 * jnp.sum(jnp.sum(diff * diff, axis=-1, keepdims=True) / d, axis=0, keepdims=True)
        loss_ref[...] += jnp.broadcast_to(part, loss_ref.shape)
        dx, dg = _rms_bwd(diff / d, xv, gv)
        dx_ref[...] = dx
        dg_ref[...] += dg

    row = pl.BlockSpec((tr, d), lambda i: (i, 0))
    vec = pl.BlockSpec((1, d), lambda i: (0, 0))
    lvec = pl.BlockSpec((1, LANES), lambda i: (0, 0))
    return pl.pallas_call(body, grid=(s // tr,), in_specs=[row, vec, row], out_specs=[lvec, row, vec],
                          out_shape=[jax.ShapeDtypeStruct((1, LANES), F32), jax.ShapeDtypeStruct((s, d), F32),
                                     jax.ShapeDtypeStruct((1, d), F32)],
                          name=name, compiler_params=_cp(("arbitrary",)))(x, g, target)


def _shift_rows(u, s, down):
    ri = lax.broadcasted_iota(jnp.int32, u.shape, 0)
    if down:
        return jnp.where(ri == 0, 0.0, pltpu.roll(u, 1, 0))
    return jnp.where(ri == s - 1, 0.0, pltpu.roll(u, s - 1, 0))


def _conv_fwd(proj, conv_w, name):
    s = proj.shape[0]
    nt = CONV_CH // LANES

    def body(cb_ref, cc_ref, cx_ref, w_ref, pre_ref):
        u = cc_ref[...] * cx_ref[...]
        conv = _shift_rows(u, s, True) * w_ref[0:1, :] + u * w_ref[1:2, :] + _shift_rows(u, s, False) * w_ref[2:3, :]
        pre_ref[...] = cb_ref[...] * conv

    def col(off):
        return pl.BlockSpec((s, LANES), lambda j: (0, off // LANES + j))

    return pl.pallas_call(body, grid=(nt,), in_specs=[col(OCB), col(OCC), col(OCX), pl.BlockSpec((3, LANES), lambda j: (0, j))],
                          out_specs=pl.BlockSpec((s, LANES), lambda j: (0, j)),
                          out_shape=jax.ShapeDtypeStruct((s, CONV_CH), F32), name=name,
                          compiler_params=_cp(("parallel",)))(proj, proj, proj, conv_w)


def _conv_bwd(proj, conv_w, d_pre, name):
    s = proj.shape[0]
    nt = CONV_CH // LANES

    def body(cb_ref, cc_ref, cx_ref, w_ref, dp_ref, dcb_ref, dcc_ref, dcx_ref, dw_ref):
        cc, cx = cc_ref[...], cx_ref[...]
        u = cc * cx
        up, dn = _shift_rows(u, s, True), _shift_rows(u, s, False)
        w0, w1, w2 = w_ref[0:1, :], w_ref[1:2, :], w_ref[2:3, :]
        conv = up * w0 + u * w1 + dn * w2
        dp = dp_ref[...]
        dcb_ref[...] = (dp * conv).astype(dcb_ref.dtype)
        dconv = dp * cb_ref[...]
        du = _shift_rows(dconv, s, False) * w0 + dconv * w1 + _shift_rows(dconv, s, True) * w2
        dcc_ref[...] = (du * cx).astype(dcc_ref.dtype)
        dcx_ref[...] = (du * cc).astype(dcx_ref.dtype)
        dw_ref[0:1, :] = jnp.sum(dconv * up, axis=0, keepdims=True)
        dw_ref[1:2, :] = jnp.sum(dconv * u, axis=0, keepdims=True)
        dw_ref[2:3, :] = jnp.sum(dconv * dn, axis=0, keepdims=True)

    def col(off):
        return pl.BlockSpec((s, LANES), lambda j: (0, off // LANES + j))

    blk = pl.BlockSpec((s, LANES), lambda j: (0, j))
    wblk = pl.BlockSpec((3, LANES), lambda j: (0, j))
    full = jax.ShapeDtypeStruct((s, CONV_CH), MXU)
    return pl.pallas_call(body, grid=(nt,), in_specs=[col(OCB), col(OCC), col(OCX), wblk, blk],
                          out_specs=[blk, blk, blk, wblk],
                          out_shape=[full, full, full, jax.ShapeDtypeStruct((3, CONV_CH), F32)],
                          name=name, compiler_params=_cp(("parallel",)))(proj, proj, proj, conv_w, d_pre)


GLA_SUB = 8


def _gla_gates(t_ref, wg_ref, bg_ref):
    t = t_ref[...]
    a = _dot(t, wg_ref[...]) + bg_ref[...]
    la = (jnp.minimum(a, 0.0) - jnp.log(1.0 + jnp.exp(-jnp.abs(a)))) / GTEMP
    return t, a, la


def _gla_masks(reverse):
    ri = lax.broadcasted_iota(jnp.int32, (CHUNK, CHUNK), 0)
    ci = lax.broadcasted_iota(jnp.int32, (CHUNK, CHUNK), 1)
    if reverse:
        cum, mask, mask_t = ci >= ri, ci > ri, ri > ci
    else:
        cum, mask, mask_t = ci <= ri, ci <= ri, ri <= ci
    return cum.astype(F32), mask, mask_t


def _gla_specs(s, reverse):
    nsub = min(GLA_SUB, s // CHUNK)
    nsteps = s // (CHUNK * nsub)

    def row(n):
        return nsteps - 1 - n if reverse else n

    def chunk(pi):
        return nsub - 1 - pi if reverse else pi

    return nsub, nsteps, row, chunk


def _gla_fwd(proj, wg_pad, bg, reverse, name):
    s = proj.shape[0]
    nsub, nsteps, row, chunk = _gla_specs(s, reverse)
    rb = nsub * CHUNK

    def body(q_ref, k_ref, v_ref, t_ref, wg_ref, bg_ref, o_ref, st_ref, state):
        @pl.when(pl.program_id(0) == 0)
        def _():
            state[...] = jnp.zeros_like(state)

        _, _, la = _gla_gates(t_ref, wg_ref, bg_ref)
        cumf, mask, _ = _gla_masks(reverse)
        lane = lax.broadcasted_iota(jnp.int32, (CHUNK, LANES), 1)
        for pi in range(nsub):
            rows = slice(chunk(pi) * CHUNK, (chunk(pi) + 1) * CHUNK)
            la_c = la[rows]
            b_all = _cum_dot(cumf, la_c)
            bl_all = jnp.sum(la_c, axis=0, keepdims=True)
            for p in range(GH // 2):
                sl = slice(p * LANES, (p + 1) * LANES)
                b, bl = b_all[:, sl], bl_all[:, sl]
                qd = q_ref[rows, sl] * (GDK ** -0.5) * jnp.exp(b)
                ki = k_ref[rows, sl] * jnp.exp(-b)
                kte = k_ref[rows, sl] * jnp.exp(bl - b)
                decay = jnp.exp(bl)
                for half in range(2):
                    h = 2 * p + half
                    lm = (lane < GDK) if half == 0 else (lane >= GDK)
                    qd_h = jnp.where(lm, qd, 0.0)
                    kte_h = jnp.where(lm, kte, 0.0)
                    v_h = v_ref[rows, h * GDV:(h + 1) * GDV]
                    st = state[h]
                    a_mat = jnp.where(mask, _dot_nt(qd_h, ki), 0.0)
                    o_ref[rows, h * GDV:(h + 1) * GDV] = _dot(a_mat, v_h) + _dot_nt(qd_h, st)
                    st_ref[pi, h] = st
                    state[h] = st * decay + _dot_tn(v_h, kte_h)

    return pl.pallas_call(
        body, grid=(nsteps,),
        in_specs=[pl.BlockSpec((rb, GQK), lambda n: (row(n), OGQ // GQK)),
                  pl.BlockSpec((rb, GQK), lambda n: (row(n), OGK // GQK)),
                  pl.BlockSpec((rb, GW), lambda n: (row(n), OGV // GW)),
                  pl.BlockSpec((rb, LANES), lambda n: (row(n), OT // LANES)),
                  pl.BlockSpec((LANES, GQK), lambda n: (0, 0)),
                  pl.BlockSpec((1, GQK), lambda n: (0, 0))],
        out_specs=[pl.BlockSpec((rb, GW), lambda n: (row(n), 0)),
                   pl.BlockSpec((nsub, GH, GDV, LANES), lambda n: (n, 0, 0, 0))],
        out_shape=[jax.ShapeDtypeStruct((s, GW), F32), jax.ShapeDtypeStruct((s // CHUNK, GH, GDV, LANES), F32)],
        scratch_shapes=[pltpu.VMEM((GH, GDV, LANES), F32)],
        name=name, compiler_params=_cp(("arbitrary",)))(proj, proj, proj, proj, wg_pad, bg)


def _gla_bwd(proj, wg_pad, bg, states, d_o, reverse, name):
    s = proj.shape[0]
    nsub, nsteps, row, chunk = _gla_specs(s, reverse)
    rb = nsub * CHUNK

    def body(q_ref, k_ref, v_ref, t_ref, wg_ref, bg_ref, st_ref, do_ref,
             dq_ref, dk_ref, dv_ref, dt_ref, dwg_ref, dbg_ref, dstate, da_buf):
        @pl.when(pl.program_id(0) == 0)
        def _():
            dstate[...] = jnp.zeros_like(dstate)
            dwg_ref[...] = jnp.zeros_like(dwg_ref)
            dbg_ref[...] = jnp.zeros_like(dbg_ref)

        t, a, la = _gla_gates(t_ref, wg_ref, bg_ref)
        cumf, mask, mask_t = _gla_masks(reverse)
        lane = lax.broadcasted_iota(jnp.int32, (CHUNK, LANES), 1)
        for pi in reversed(range(nsub)):
            rows = slice(chunk(pi) * CHUNK, (chunk(pi) + 1) * CHUNK)
            la_c = la[rows]
            b_all = _cum_dot(cumf, la_c)
            bl_all = jnp.sum(la_c, axis=0, keepdims=True)
            for p in range(GH // 2):
                sl = slice(p * LANES, (p + 1) * LANES)
                b, bl = b_all[:, sl], bl_all[:, sl]
                e, ei, ee = jnp.exp(b), jnp.exp(-b), jnp.exp(bl - b)
                qd = q_ref[rows, sl] * (GDK ** -0.5) * e
                ki, kte = k_ref[rows, sl] * ei, k_ref[rows, sl] * ee
                decay = jnp.exp(bl)
                dqd = jnp.zeros((CHUNK, LANES), F32)
                dki = jnp.zeros((CHUNK, LANES), F32)
                dkte = jnp.zeros((CHUNK, LANES), F32)
                ddecay = jnp.zeros((1, LANES), F32)
                for half in range(2):
                    h = 2 * p + half
                    lm = (lane < GDK) if half == 0 else (lane >= GDK)
                    qd_h = jnp.where(lm, qd, 0.0)
                    ki_h = jnp.where(lm, ki, 0.0)
                    kte_h = jnp.where(lm, kte, 0.0)
                    v_h = v_ref[rows, h * GDV:(h + 1) * GDV]
                    do_h = do_ref[rows, h * GDV:(h + 1) * GDV]
                    st = st_ref[pi, h]
                    dst = dstate[h]
                    at_mat = jnp.where(mask_t, _dot_nt(ki_h, qd_h), 0.0)
                    da_mat = jnp.where(mask, _dot_nt(do_h, v_h), 0.0)
                    dat_mat = jnp.where(mask_t, _dot_nt(v_h, do_h), 0.0)
                    dv_ref[rows, h * GDV:(h + 1) * GDV] = _dot(at_mat, do_h) + _dot_nt(kte_h, dst)
                    dqd += _dot(jnp.concatenate([do_h, da_mat], axis=1), jnp.concatenate([st, ki_h], axis=0))
                    dki += _dot(dat_mat, qd_h)
                    dkte += _dot(v_h, dst)
                    ddecay += jnp.sum(dst * st, axis=0, keepdims=True)
                    dstate[h] = dst * decay + _dot_tn(do_h, qd_h)
                dq_ref[rows, sl] = dqd * e * (GDK ** -0.5)
                dk_ref[rows, sl] = dki * ei + dkte * ee
                db = dqd * qd - dki * ki - dkte * kte
                dbl = jnp.sum(dkte * kte, axis=0, keepdims=True) + decay * ddecay
                da_buf[rows, sl] = _cum_dot(cumf, db, True) + dbl
        da = da_buf[...] * (1.0 / GTEMP) * _sigmoid(-a)
        dt_ref[...] = _dot_nt(da, wg_ref[...])
        dwg_ref[...] += _dot_tn(t, da)
        dbg_ref[...] += jnp.sum(da, axis=0, keepdims=True)

    def prow(j):
        return row(nsteps - 1 - j)

    return pl.pallas_call(
        body, grid=(nsteps,),
        in_specs=[pl.BlockSpec((rb, GQK), lambda j: (prow(j), OGQ // GQK)),
                  pl.BlockSpec((rb, GQK), lambda j: (prow(j), OGK // GQK)),
                  pl.BlockSpec((rb, GW), lambda j: (prow(j), OGV // GW)),
                  pl.BlockSpec((rb, LANES), lambda j: (prow(j), OT // LANES)),
                  pl.BlockSpec((LANES, GQK), lambda j: (0, 0)),
                  pl.BlockSpec((1, GQK), lambda j: (0, 0)),
                  pl.BlockSpec((nsub, GH, GDV, LANES), lambda j: (nsteps - 1 - j, 0, 0, 0)),
                  pl.BlockSpec((rb, GW), lambda j: (prow(j), 0))],
        out_specs=[pl.BlockSpec((rb, GQK), lambda j: (prow(j), 0)),
                   pl.BlockSpec((rb, GQK), lambda j: (prow(j), 0)),
                   pl.BlockSpec((rb, GW), lambda j: (prow(j), 0)),
                   pl.BlockSpec((rb, LANES), lambda j: (prow(j), 0)),
                   pl.BlockSpec((LANES, GQK), lambda j: (0, 0)),
                   pl.BlockSpec((1, GQK), lambda j: (0, 0))],
        out_shape=[jax.ShapeDtypeStruct((s, GQK), F32), jax.ShapeDtypeStruct((s, GQK), F32),
                   jax.ShapeDtypeStruct((s, GW), F32), jax.ShapeDtypeStruct((s, LANES), F32),
                   jax.ShapeDtypeStruct((LANES, GQK), F32), jax.ShapeDtypeStruct((1, GQK), F32)],
        scratch_shapes=[pltpu.VMEM((GH, GDV, LANES), F32), pltpu.VMEM((rb, GQK), F32)],
        name=name, compiler_params=_cp(("arbitrary",)))(proj, proj, proj, proj, wg_pad, bg, states, d_o)


def _rot_half(x):
    lane = lax.broadcasted_iota(jnp.int32, x.shape, 1)
    first = (lane % MROPE) < (MROPE // 2)
    return jnp.where(first, -pltpu.roll(x, LANES - MROPE // 2, 1), pltpu.roll(x, MROPE // 2, 1))


def _mla_prep(proj, cos, sin, qg, kvg, w_uq, w_ukv, name):
    s = proj.shape[0]
    tr = _rows(s)

    def body(mq_ref, mkv_ref, t_ref, cos_ref, sin_ref, qg_ref, kvg_ref, wuq_ref, wukv_ref, q_ref, k_ref, v_ref):
        cosv, sinv = cos_ref[...], sin_ref[...]
        lane = lax.broadcasted_iota(jnp.int32, (tr, LANES), 1)

        def rope(xv):
            return xv * cosv + _rot_half(xv) * sinv

        qm = _dot(_rms(mq_ref[...], qg_ref[...]), wuq_ref[...])
        kv = _dot(_rms(mkv_ref[...], kvg_ref[...]), wukv_ref[...])
        kr_lo = jnp.where(lane < MROPE, rope(t_ref[...]), 0.0)
        kr_hi = pltpu.roll(kr_lo, MROPE, 1)
        for p in range(MH // 2):
            r = rope(qm[:, MW + p * LANES:MW + (p + 1) * LANES]).astype(q_ref.dtype)
            q_ref[2 * p, :, LANES:] = r
            q_ref[2 * p + 1, :, LANES:] = r
        for h in range(MH):
            q_ref[h, :, :LANES] = qm[:, h * LANES:(h + 1) * LANES].astype(q_ref.dtype)
            k_ref[h, :, :LANES] = kv[:, 2 * h * LANES:(2 * h + 1) * LANES].astype(k_ref.dtype)
            k_ref[h, :, LANES:] = (kr_lo if h % 2 == 0 else kr_hi).astype(k_ref.dtype)
            v_ref[h] = kv[:, (2 * h + 1) * LANES:(2 * h + 2) * LANES].astype(v_ref.dtype)

    def full(shape):
        return pl.BlockSpec(shape, lambda i: (0,) * len(shape))

    return pl.pallas_call(
        body, grid=(s // tr,),
        in_specs=[pl.BlockSpec((tr, MQL), lambda i: (i, OMQ // MQL)),
                  pl.BlockSpec((tr, MKVL), lambda i: (i, OMKV // MKVL)),
                  pl.BlockSpec((tr, LANES), lambda i: (i, OT // LANES)),
                  pl.BlockSpec((tr, LANES), lambda i: (i, 0)),
                  pl.BlockSpec((tr, LANES), lambda i: (i, 0)),
                  full((1, MQL)), full((1, MKVL)), full((MQL, MQW)), full((MKVL, MKVW))],
        out_specs=[pl.BlockSpec((MH, tr, 2 * LANES), lambda i: (0, i, 0)),
                   pl.BlockSpec((MH, tr, 2 * LANES), lambda i: (0, i, 0)),
                   pl.BlockSpec((MH, tr, LANES), lambda i: (0, i, 0))],
        out_shape=[jax.ShapeDtypeStruct((MH, s, 2 * LANES), MXU), jax.ShapeDtypeStruct((MH, s, 2 * LANES), MXU),
                   jax.ShapeDtypeStruct((MH, s, LANES), MXU)],
        name=name, compiler_params=_cp(("parallel",)))(proj, proj, proj, cos, sin, qg, kvg, w_uq, w_ukv)


def _mla_prep_bwd(proj, cos, sin, qg, kvg, w_uq, w_ukv, d_q, d_k, d_v, name):
    s = proj.shape[0]
    tr = _rows(s)

    def body(mq_ref, mkv_ref, cos_ref, sin_ref, qg_ref, kvg_ref, wuq_ref, wukv_ref, dq_ref, dk_ref, dv_ref,
             dmq_ref, dmkv_ref, dt_ref, dwuq_ref, dwukv_ref, dqg_ref, dkvg_ref):
        @pl.when(pl.program_id(0) == 0)
        def _():
            for r in (dwuq_ref, dwukv_ref, dqg_ref, dkvg_ref):
                r[...] = jnp.zeros_like(r)

        cosv, sinv = cos_ref[...], sin_ref[...]
        lane = lax.broadcasted_iota(jnp.int32, (tr, LANES), 1)
        lo = lane < MROPE

        def unrope(dv):
            return dv * cosv - _rot_half(dv * sinv)

        parts = [dq_ref[h, :, :LANES] for h in range(MH)]
        for p in range(MH // 2):
            parts.append(unrope(jnp.where(lo, dq_ref[2 * p, :, LANES:], dq_ref[2 * p + 1, :, LANES:])))
        d_qm = jnp.concatenate(parts, axis=1)
        mq, qgv = mq_ref[...], qg_ref[...]
        cq = _rms(mq, qgv)
        dwuq_ref[...] += _dot_tn(cq, d_qm)
        dmq, dqg = _rms_bwd(_dot_nt(d_qm, wuq_ref[...]), mq, qgv)
        dmq_ref[...] = dmq.astype(dmq_ref.dtype)
        dqg_ref[...] += dqg

        parts = []
        for h in range(MH):
            parts += [dk_ref[h, :, :LANES], dv_ref[h]]
        d_kv = jnp.concatenate(parts, axis=1)
        mkv, kvgv = mkv_ref[...], kvg_ref[...]
        ckv = _rms(mkv, kvgv)
        dwukv_ref[...] += _dot_tn(ckv, d_kv)
        dmkv, dkvg = _rms_bwd(_dot_nt(d_kv, wukv_ref[...]), mkv, kvgv)
        dmkv_ref[...] = dmkv.astype(dmkv_ref.dtype)
        dkvg_ref[...] += dkvg

        even = dk_ref[0, :, LANES:] + dk_ref[2, :, LANES:] + dk_ref[4, :, LANES:]
        odd = dk_ref[1, :, LANES:] + dk_ref[3, :, LANES:] + dk_ref[5, :, LANES:]
        d_kr = jnp.where(lo, even, 0.0) + pltpu.roll(jnp.where(lo, 0.0, odd), MROPE, 1)
        dt_ref[...] = jnp.where(lo, unrope(d_kr), 0.0)

    def full(shape):
        return pl.BlockSpec(shape, lambda i: (0,) * len(shape))

    return pl.pallas_call(
        body, grid=(s // tr,),
        in_specs=[pl.BlockSpec((tr, MQL), lambda i: (i, OMQ // MQL)),
                  pl.BlockSpec((tr, MKVL), lambda i: (i, OMKV // MKVL)),
                  pl.BlockSpec((tr, LANES), lambda i: (i, 0)),
                  pl.BlockSpec((tr, LANES), lambda i: (i, 0)),
                  full((1, MQL)), full((1, MKVL)), full((MQL, MQW)), full((MKVL, MKVW)),
                  pl.BlockSpec((MH, tr, 2 * LANES), lambda i: (0, i, 0)),
                  pl.BlockSpec((MH, tr, 2 * LANES), lambda i: (0, i, 0)),
                  pl.BlockSpec((MH, tr, LANES), lambda i: (0, i, 0))],
        out_specs=[pl.BlockSpec((tr, MQL), lambda i: (i, 0)), pl.BlockSpec((tr, MKVL), lambda i: (i, 0)),
                   pl.BlockSpec((tr, LANES), lambda i: (i, 0)),
                   full((MQL, MQW)), full((MKVL, MKVW)), full((1, MQL)), full((1, MKVL))],
        out_shape=[jax.ShapeDtypeStruct((s, MQL), MXU), jax.ShapeDtypeStruct((s, MKVL), MXU),
                   jax.ShapeDtypeStruct((s, LANES), F32),
                   jax.ShapeDtypeStruct((MQL, MQW), F32), jax.ShapeDtypeStruct((MKVL, MKVW), F32),
                   jax.ShapeDtypeStruct((1, MQL), F32), jax.ShapeDtypeStruct((1, MKVL), F32)],
        name=name, compiler_params=_cp(("arbitrary",)))(proj, proj, cos, sin, qg, kvg, w_uq, w_ukv, d_q, d_k, d_v)


ATT_SCALE = (MNOPE + MROPE) ** -0.5
ATT_SCALE_LOG2 = ATT_SCALE * 1.4426950408889634
ATT_TQ_FWD, ATT_TQ = 2048, 2048
ATT_SUB, ATT_SUB_BWD = 256, 256


def _attn_fwd(q, k, v, name):
    s = q.shape[1]
    tq = min(ATT_TQ_FWD, s)
    sub = min(ATT_SUB, tq)

    def body(q_ref, k_ref, v_ref, o_ref, lse_ref):
        for r0 in range(0, tq, sub):
            rows = slice(r0, r0 + sub)
            sc = _dot_nt(q_ref[0, rows, :], k_ref[0])
            m = jnp.max(sc, axis=-1, keepdims=True)
            p = jnp.exp2((sc - m) * ATT_SCALE_LOG2)
            l = jnp.sum(p, axis=-1, keepdims=True)
            o_ref[rows, :] = _dot(p, v_ref[0]) / l
            lse_ref[0, rows, :] = m * ATT_SCALE_LOG2 + jnp.log2(l)

    return pl.pallas_call(
        body, grid=(MH, s // tq),
        in_specs=[pl.BlockSpec((1, tq, 2 * LANES), lambda h, i: (h, i, 0)),
                  pl.BlockSpec((1, s, 2 * LANES), lambda h, i: (h, 0, 0)),
                  pl.BlockSpec((1, s, LANES), lambda h, i: (h, 0, 0))],
        out_specs=[pl.BlockSpec((tq, LANES), lambda h, i: (i, h)),
                   pl.BlockSpec((1, tq, 1), lambda h, i: (h, i, 0))],
        out_shape=[jax.ShapeDtypeStruct((s, MW), F32), jax.ShapeDtypeStruct((MH, s, 1), F32)],
        name=name, compiler_params=_cp(("parallel", "parallel")))(q, k, v)


def _attn_bwd(q, k, v, o, lse, d_o, name):
    s = q.shape[1]
    tq = min(ATT_TQ, s)
    sub = min(ATT_SUB_BWD, tq)

    def body(q_ref, k_ref, v_ref, o_ref, lse_ref, do_ref, dq_ref, dk_ref, dv_ref):
        @pl.when(pl.program_id(1) == 0)
        def _():
            dk_ref[...] = jnp.zeros_like(dk_ref)
            dv_ref[...] = jnp.zeros_like(dv_ref)

        kv = k_ref[0]
        for r0 in range(0, tq, sub):
            rows = slice(r0, r0 + sub)
            qv, do = q_ref[0, rows, :], do_ref[rows, :]
            p = jnp.exp2(_dot_nt(qv, kv) * ATT_SCALE_LOG2 - lse_ref[0, rows, :])
            delta = jnp.sum(do * o_ref[rows, :], axis=-1, keepdims=True)
            ds = p * (_dot_nt(do, v_ref[0]) - delta)
            dq_ref[0, rows, :] = _dot(ds, kv) * ATT_SCALE
            dk_ref[0] += _dot_tn(ds, qv) * ATT_SCALE
            dv_ref[0] += _dot_tn(p, do)

    return pl.pallas_call(
        body, grid=(MH, s // tq),
        in_specs=[pl.BlockSpec((1, tq, 2 * LANES), lambda h, i: (h, i, 0)),
                  pl.BlockSpec((1, s, 2 * LANES), lambda h, i: (h, 0, 0)),
                  pl.BlockSpec((1, s, LANES), lambda h, i: (h, 0, 0)),
                  pl.BlockSpec((tq, LANES), lambda h, i: (i, h)),
                  pl.BlockSpec((1, tq, 1), lambda h, i: (h, i, 0)),
                  pl.BlockSpec((tq, LANES), lambda h, i: (i, h))],
        out_specs=[pl.BlockSpec((1, tq, 2 * LANES), lambda h, i: (h, i, 0)),
                   pl.BlockSpec((1, s, 2 * LANES), lambda h, i: (h, 0, 0)),
                   pl.BlockSpec((1, s, LANES), lambda h, i: (h, 0, 0))],
        out_shape=[jax.ShapeDtypeStruct((MH, s, 2 * LANES), F32), jax.ShapeDtypeStruct((MH, s, 2 * LANES), F32),
                   jax.ShapeDtypeStruct((MH, s, LANES), F32)],
        name=name, compiler_params=_cp(("parallel", "arbitrary")))(q, k, v, o, lse, d_o)


def _merge_fwd(o_f, o_b, o_att, pre, proj, gng, mog, cog, name):
    s = proj.shape[0]
    tr = _rows_light(s)

    def body(of_ref, ob_ref, oa_ref, pre_ref, z_ref, gng_ref, mog_ref, cog_ref, y_ref):
        z = z_ref[...]
        sz = z * _sigmoid(z)
        osum = of_ref[...] + ob_ref[...]
        gg = gng_ref[...]
        for h in range(GH):
            sl = slice(h * GDV, (h + 1) * GDV)
            y_ref[:, sl] = (_rms(osum[:, sl], gg) * sz[:, sl]).astype(y_ref.dtype)
        y_ref[:, GW:GW + MW] = (_rms(oa_ref[...], mog_ref[...]) * sz[:, GW:GW + MW]).astype(y_ref.dtype)
        y_ref[:, GW + MW:] = (_rms(pre_ref[...], cog_ref[...]) * sz[:, GW + MW:]).astype(y_ref.dtype)

    def row(w):
        return pl.BlockSpec((tr, w), lambda i: (i, 0))

    def vec(w):
        return pl.BlockSpec((1, w), lambda i: (0, 0))

    return pl.pallas_call(
        body, grid=(s // tr,),
        in_specs=[row(GW), row(GW), row(MW), row(CONV_CH), row(D_MIX), vec(GDV), vec(MW), vec(CONV_CH)],
        out_specs=row(D_MIX), out_shape=jax.ShapeDtypeStruct((s, D_MIX), MXU),
        name=name, compiler_params=_cp(("parallel",)))(o_f, o_b, o_att, pre, proj, gng, mog, cog)


def _merge_bwd(d_y, o_f, o_b, o_att, pre, proj, gng, mog, cog, name):
    s = proj.shape[0]
    tr = _rows(s)

    def body(dy_ref, of_ref, ob_ref, oa_ref, pre_ref, z_ref, gng_ref, mog_ref, cog_ref,
             dz_ref, dos_ref, doa_ref, dpre_ref, dgng_ref, dmog_ref, dcog_ref):
        @pl.when(pl.program_id(0) == 0)
        def _():
            for r in (dgng_ref, dmog_ref, dcog_ref):
                r[...] = jnp.zeros_like(r)

        z, dy = z_ref[...], dy_ref[...]
        sg = _sigmoid(z)
        sz = z * sg
        dsz = sg * (1.0 + z * (1.0 - sg))
        dcat = dy * sz
        dyz = dy * dsz
        osum = of_ref[...] + ob_ref[...]
        gg = gng_ref[...]
        dgg = jnp.zeros_like(gg)
        for h in range(GH):
            sl = slice(h * GDV, (h + 1) * GDV)
            dz_ref[:, sl] = (dyz[:, sl] * _rms(osum[:, sl], gg)).astype(dz_ref.dtype)
            dx, dg = _rms_bwd(dcat[:, sl], osum[:, sl], gg)
            dos_ref[:, sl] = dx
            dgg += dg
        dgng_ref[...] += dgg
        sl = slice(GW, GW + MW)
        oa, mg = oa_ref[...], mog_ref[...]
        dz_ref[:, sl] = (dyz[:, sl] * _rms(oa, mg)).astype(dz_ref.dtype)
        dx, dg = _rms_bwd(dcat[:, sl], oa, mg)
        doa_ref[...] = dx
        dmog_ref[...] += dg
        sl = slice(GW + MW, D_MIX)
        pv, cg = pre_ref[...], cog_ref[...]
        dz_ref[:, sl] = (dyz[:, sl] * _rms(pv, cg)).astype(dz_ref.dtype)
        dx, dg = _rms_bwd(dcat[:, sl], pv, cg)
        dpre_ref[...] = dx
        dcog_ref[...] += dg

    def row(w):
        return pl.BlockSpec((tr, w), lambda i: (i, 0))

    def vec(w):
        return pl.BlockSpec((1, w), lambda i: (0, 0))

    def rs(w):
        return jax.ShapeDtypeStruct((s, w), F32)

    def vs(w):
        return jax.ShapeDtypeStruct((1, w), F32)

    return pl.pallas_call(
        body, grid=(s // tr,),
        in_specs=[row(D_MIX), row(GW), row(GW), row(MW), row(CONV_CH), row(D_MIX), vec(GDV), vec(MW), vec(CONV_CH)],
        out_specs=[row(D_MIX), row(GW), row(MW), row(CONV_CH), vec(GDV), vec(MW), vec(CONV_CH)],
        out_shape=[jax.ShapeDtypeStruct((s, D_MIX), MXU),
                   rs(GW), rs(MW), rs(CONV_CH), vs(GDV), vs(MW), vs(CONV_CH)],
        name=name, compiler_params=_cp(("arbitrary",)))(d_y, o_f, o_b, o_att, pre, proj, gng, mog, cog)


def _assemble_dproj(d_z, d_cb, d_cc, d_cx, d_mkv, dv_f, dv_b, dq_f, dq_b, dk_f, dk_b, d_mq, dt_m, dt_f, dt_b, name):
    s = d_z.shape[0]
    tr = _rows_light(s)

    def body(dz, dcb, dcc, dcx, dmkv, dvf, dvb, dqf, dqb, dkf, dkb, dmq, dtm, dtf, dtb, out):
        dt = out.dtype
        out[:, OZ:OZ + D_MIX] = dz[...].astype(dt)
        out[:, OCB:OCB + CONV_CH] = dcb[...].astype(dt)
        out[:, OCC:OCC + CONV_CH] = dcc[...].astype(dt)
        out[:, OCX:OCX + CONV_CH] = dcx[...].astype(dt)
        out[:, OMKV:OMKV + MKVL] = dmkv[...].astype(dt)
        out[:, OGV:OGV + GW] = (dvf[...] + dvb[...]).astype(dt)
        out[:, OGQ:OGQ + GQK] = (dqf[...] + dqb[...]).astype(dt)
        out[:, OGK:OGK + GQK] = (dkf[...] + dkb[...]).astype(dt)
        out[:, OMQ:OMQ + MQL] = dmq[...].astype(dt)
        out[:, OT:OT + LANES] = (dtm[...] + dtf[...] + dtb[...]).astype(dt)

    args = (d_z, d_cb, d_cc, d_cx, d_mkv, dv_f, dv_b, dq_f, dq_b, dk_f, dk_b, d_mq, dt_m, dt_f, dt_b)
    return pl.pallas_call(
        body, grid=(s // tr,),
        in_specs=[pl.BlockSpec((tr, a.shape[1]), lambda i: (i, 0)) for a in args],
        out_specs=pl.BlockSpec((tr, PW), lambda i: (i, 0)),
        out_shape=jax.ShapeDtypeStruct((s, PW), MXU), name=name, compiler_params=_cp(("parallel",)))(*args)


def _layer_fwd(x, mod, wt, cos, sin, tag, late=None, in_after=(), h=None):
    shift, scale, gate = mod
    if h is None:
        h = _norm_mod(x, wt["norm_g"], scale, shift, f"norm_mod_{tag}")
    (proj,) = _matmul(h, wt["w_in"], dims="nn", tm=2048, tn=256, tk=2048, out_dtypes=(F32,), name=f"in_proj_{tag}",
                      after=in_after)
    if late is not None:
        wt.update(late(proj))
    o_f, st_f = _gla_fwd(proj, wt["wg_pad_f"], wt["bg_f"], False, f"gla_fwd_f_{tag}")
    o_b, st_b = _gla_fwd(proj, wt["wg_pad_b"], wt["bg_b"], True, f"gla_fwd_b_{tag}")
    q, k, v = _mla_prep(proj, cos, sin, wt["q_norm_g"], wt["kv_norm_g"], wt["w_uq"], wt["w_ukv"], f"mla_prep_{tag}")
    o_att, lse = _attn_fwd(q, k, v, f"attn_fwd_{tag}")
    pre = _conv_fwd(proj, wt["conv_w"], f"conv_fwd_{tag}")
    y = _merge_fwd(o_f, o_b, o_att, pre, proj, wt["gla_norm_g"], wt["mla_out_g"], wt["conv_out_g"], f"merge_fwd_{tag}")
    x_new, u = _matmul(y, wt["w_out"], dims="nn", tm=2048, tn=256, tk=2048, out_dtypes=(F32, F32),
                       name=f"out_proj_{tag}", epilogue=lambda acc, xv, gv: (xv + gv * acc, acc),
                       extras=(x, gate), extra_kinds=("mn", "n"))
    saved = dict(x=x, h=h, proj=proj, o_f=o_f, o_b=o_b, st_f=st_f, st_b=st_b, q=q, k=k, v=v,
                 o_att=o_att, lse=lse, pre=pre, y=y, u=u)
    return x_new, saved


def _layer_bwd(d_out, sv, mod, wt, cos, sin, tag, ship=None, dx_first=None, ship_rest=None):
    shift, scale, gate = mod
    proj = sv["proj"]
    d_u, d_gate = _gate_bwd(d_out, sv["u"], gate, f"gate_bwd_{tag}")
    (g_w_out,) = _matmul(sv["y"], d_u, dims="tn", tm=1024, tn=512, tk=2048, out_dtypes=(MXU,), name=f"out_proj_dw_{tag}")
    (d_y,) = _matmul(d_u, wt["w_out"], dims="nt", tm=2048, tn=256, tk=2048, out_dtypes=(F32,), name=f"out_proj_dx_{tag}",
                     after=(g_w_out,))
    d_z, d_osum, d_oatt, d_pre, d_gng, d_mog, d_cog = _merge_bwd(
        d_y, sv["o_f"], sv["o_b"], sv["o_att"], sv["pre"], proj, wt["gla_norm_g"], wt["mla_out_g"], wt["conv_out_g"],
        f"merge_bwd_{tag}")
    d_cb, d_cc, d_cx, d_conv_w = _conv_bwd(proj, wt["conv_w"], d_pre, f"conv_bwd_{tag}")
    d_q, d_k, d_v = _attn_bwd(sv["q"], sv["k"], sv["v"], sv["o_att"], sv["lse"], d_oatt, f"attn_bwd_{tag}")
    d_mq, d_mkv, dt_m, g_w_uq, g_w_ukv, d_qg, d_kvg = _mla_prep_bwd(
        proj, cos, sin, wt["q_norm_g"], wt["kv_norm_g"], wt["w_uq"], wt["w_ukv"], d_q, d_k, d_v, f"mla_prep_bwd_{tag}")
    bg_f, bg_b = wt["bg_f"], wt["bg_b"]
    if ship_rest is not None:
        tok = ship_rest(dict(w_out=g_w_out, w_uq=g_w_uq, w_ukv=g_w_ukv))
        bg_f, bg_b = bg_f + tok, bg_b + tok
    dq_f, dk_f, dv_f, dt_f, d_wg_f, d_bg_f = _gla_bwd(proj, wt["wg_pad_f"], bg_f, sv["st_f"], d_osum, False,
                                                     f"gla_bwd_f_{tag}")
    dq_b, dk_b, dv_b, dt_b, d_wg_b, d_bg_b = _gla_bwd(proj, wt["wg_pad_b"], bg_b, sv["st_b"], d_osum, True,
                                                     f"gla_bwd_b_{tag}")
    d_proj = _assemble_dproj(d_z, d_cb, d_cc, d_cx, d_mkv, dv_f, dv_b, dq_f, dq_b, dk_f, dk_b, d_mq, dt_m, dt_f, dt_b,
                             f"assemble_dproj_{tag}")
    grads = dict(w_out=g_w_out, w_uq=g_w_uq, w_ukv=g_w_ukv,
                 wg_pad_f=d_wg_f, bg_f=d_bg_f, wg_pad_b=d_wg_b, bg_b=d_bg_b, gla_norm_g=d_gng,
                 q_norm_g=d_qg, kv_norm_g=d_kvg, mla_out_g=d_mog, conv_w=d_conv_w, conv_out_g=d_cog)

    def in_dw(after):
        (g_w_in,) = _matmul(sv["h"], d_proj, dims="tn", tm=2048, tn=256, tk=2048, out_dtypes=(MXU,),
                            name=f"in_proj_dw_{tag}", after=after)
        grads["w_in"] = g_w_in
        return dict(w_in=g_w_in, w_out=g_w_out, w_uq=g_w_uq, w_ukv=g_w_ukv)

    def in_dx(after):
        (d_h,) = _matmul(d_proj, wt["w_in"], dims="nt", tm=1024, tn=512, tk=PW, out_dtypes=(F32,),
                         name=f"in_proj_dx_{tag}", after=after)
        d_x, d_shift, d_scale, d_ng = _norm_mod_bwd(d_h, sv["x"], d_out, wt["norm_g"], scale, f"norm_mod_bwd_{tag}")
        grads["norm_g"] = d_ng
        return d_x, (d_shift, d_scale, d_gate)

    if dx_first is None:
        big = in_dw(())
        d_x, d_mod = in_dx((big["w_in"],) if ship is None else ship(big))
    else:
        d_x, d_mod = in_dx(())
        big = in_dw(dx_first(d_x, d_mod, grads))
        ship(big)
    return d_x, d_mod, grads


IN_SEGS = ((3808, 5856), (2272, 3808), (1952, 2208), (768, 1536), (0, 768), (1568, 1952), (2208, 2272), (1536, 1568))


def _heads_apart(w):
    w3 = w.reshape(w.shape[:-1] + (MH, MNOPE + MROPE))
    return jnp.concatenate([w3[..., :MNOPE].reshape(w.shape[:-1] + (MH * MNOPE,)),
                            w3[..., MNOPE:].reshape(w.shape[:-1] + (MH * MROPE,))], axis=-1)


def _heads_together(g):
    nope = g[..., :MH * MNOPE].reshape(g.shape[:-1] + (MH, MNOPE))
    rope = g[..., MH * MNOPE:].reshape(g.shape[:-1] + (MH, MROPE))
    return jnp.concatenate([nope, rope], axis=-1).reshape(g.shape[:-1] + (MQW,))


def _perm_gathered(g, segs, width):
    per = g.shape[-1]
    parts, total = [], 0
    for a, b in segs:
        c = a
        while c < b:
            j = c // per
            hi = min(b, (j + 1) * per)
            parts.append(g[j, :, c - j * per:hi - j * per])
            c = hi
        total += b - a
    if width > total:
        parts.append(jnp.zeros((g.shape[1], width - total), g.dtype))
    return jnp.concatenate(parts, axis=1)


def _scatter_perm(gp, segs, per):
    offs, o = [], 0
    for a, b in segs:
        offs.append((a, b, o))
        o += b - a
    blocks = []
    for j in range(N_DEV):
        lo, hi = j * per, (j + 1) * per
        pieces = []
        for a, b, o in sorted(offs):
            s0, s1 = max(a, lo), min(b, hi)
            if s0 < s1:
                pieces.append(gp[:, o + s0 - a:o + s1 - a])
        blocks.append(jnp.concatenate(pieces, axis=1))
    return jnp.stack(blocks)


def _prep_layer_weights(w_in, w_out, w_uq, w_ukv, small):
    def vec(v):
        return v.reshape(1, -1).astype(F32)

    zeros = functools.partial(jnp.zeros, dtype=F32)
    wg_f, wg_b = small["gla_wg_f"].astype(F32), small["gla_wg_b"].astype(F32)
    wg_pad_f = jnp.concatenate([zeros((MROPE, GQK)), wg_f, zeros((LANES - MROPE - GRANK, GQK))], axis=0)
    wg_pad_b = jnp.concatenate([zeros((MROPE + GRANK, GQK)), wg_b, zeros((LANES - MROPE - 2 * GRANK, GQK))], axis=0)
    wt = dict(norm_g=vec(small["norm_g"]), wg_pad_f=wg_pad_f, wg_pad_b=wg_pad_b,
              bg_f=vec(small["gla_bg_f"]), bg_b=vec(small["gla_bg_b"]), gla_norm_g=vec(small["gla_norm_g"]),
              q_norm_g=vec(small["mla_q_norm_g"]), kv_norm_g=vec(small["mla_kv_norm_g"]),
              mla_out_g=vec(small["mla_out_g"]), conv_w=small["conv_w"].astype(F32),
              conv_out_g=vec(small["conv_out_g"]))
    for name, w in (("w_in", w_in), ("w_out", w_out), ("w_uq", w_uq), ("w_ukv", w_ukv)):
        if w is not None:
            wt[name] = w.astype(MXU)
    return wt


def _natural_small(gr):
    return dict(norm_g=gr["norm_g"][0],
                gla_wg_f=gr["wg_pad_f"][MROPE:MROPE + GRANK], gla_bg_f=gr["bg_f"][0],
                gla_wg_b=gr["wg_pad_b"][MROPE + GRANK:MROPE + 2 * GRANK], gla_bg_b=gr["bg_b"][0],
                gla_norm_g=gr["gla_norm_g"][0], mla_q_norm_g=gr["q_norm_g"][0], mla_kv_norm_g=gr["kv_norm_g"][0],
                mla_out_g=gr["mla_out_g"][0], conv_w=gr["conv_w"], conv_out_g=gr["conv_out_g"][0])


def _peer(r):
    ax, ay, ac = lax.axis_index("x"), lax.axis_index("y"), lax.axis_index("c")
    px = 1 - ax if r & 4 else ax
    py = 1 - ay if r & 2 else ay
    pc = 1 - ac if r & 1 else ac
    return (px, py, pc), 4 * px + 2 * py + pc


def _gather_small(arrs, name):
    n = len(arrs)

    def body(*refs):
        ins, outs = refs[:n], refs[n:2 * n]
        send_sems, recv_sems, loc_sems = refs[2 * n:]
        me = _peer(0)[1]

        def remote(a, r, slot):
            return pltpu.make_async_remote_copy(
                src_ref=ins[a], dst_ref=outs[a].at[slot], send_sem=send_sems.at[a, r - 1],
                recv_sem=recv_sems.at[a, r - 1], device_id=_peer(r)[0], device_id_type=MESH)

        locs = [pltpu.make_async_copy(ins[a], outs[a].at[me], loc_sems.at[a]) for a in range(n)]
        for cp in locs:
            cp.start()
        sends = [remote(a, r, me) for r in range(1, N_DEV) for a in range(n)]
        for cp in sends:
            cp.start()
        for r in range(1, N_DEV):
            for a in range(n):
                remote(a, r, _peer(r)[1]).wait_recv()
        for cp in sends:
            cp.wait_send()
        for cp in locs:
            cp.wait()

    spec = pl.BlockSpec(memory_space=pltpu.VMEM)
    return pl.pallas_call(
        body, in_specs=[spec] * n, out_specs=[spec] * n,
        out_shape=[jax.ShapeDtypeStruct((N_DEV,) + a.shape, a.dtype) for a in arrs],
        scratch_shapes=[pltpu.SemaphoreType.DMA((n, N_DEV - 1)), pltpu.SemaphoreType.DMA((n, N_DEV - 1)),
                        pltpu.SemaphoreType.DMA((n,))],
        name=name, compiler_params=pltpu.CompilerParams(vmem_limit_bytes=VMEM_LIMIT))(*arrs)


def _slot(rel_div):
    rel, div = rel_div
    idx = _peer(rel)[1]
    return idx if div == 1 else idx // div


AG_SPREAD = tuple((r, None, (0, 1), (r, 1)) for r in (1, 2, 4, 6))
AG_FORWARD = tuple((1, (k, 1), (k, 1), (1 ^ k, 1)) for k in (2, 4, 6))
RS_PAIR = tuple((1, (1 ^ k, 1), (1 ^ k, 2), (k, 2)) for k in (0, 2, 4, 6))
RS_CHIPS = tuple((r, (r, 2), (0, 2), (r, 2)) for r in (2, 4, 6))


def _plan_copies(plan, n, src_refs, land_refs, send_sems, recv_sems, arriving):
    out = []
    for i, (r, src, dst, recv) in enumerate(plan):
        peer = _peer(r)[0]
        for a in range(n):
            out.append(pltpu.make_async_remote_copy(
                src_ref=src_refs[a] if src is None else src_refs[a].at[_slot(src)],
                dst_ref=land_refs[a].at[_slot(recv if arriving else dst)],
                send_sem=send_sems.at[i * n + a], recv_sem=recv_sems.at[i * n + a],
                device_id=peer, device_id_type=MESH))
    return out


def _exchange_hbm(plan, srcs, lands, name, after=()):
    n = len(lands)
    fresh = isinstance(lands[0], jax.ShapeDtypeStruct)
    ins = ([] if srcs is None else list(srcs)) + ([] if fresh else list(lands))
    ns = 0 if srcs is None else n
    n_data = len(ins)
    ins = ins + list(after)

    def body(*refs):
        outs = refs[len(ins):len(ins) + n]
        send_sems, recv_sems = refs[-2:]
        src_refs = refs[:n] if srcs is not None else refs[ns:ns + n]
        sends = _plan_copies(plan, n, src_refs, outs, send_sems, recv_sems, False)
        for cp in sends:
            cp.start()
        for cp in _plan_copies(plan, n, src_refs, outs, send_sems, recv_sems, True):
            cp.wait_recv()
        for cp in sends:
            cp.wait_send()

    hbm = pl.BlockSpec(memory_space=pltpu.HBM)
    k = len(plan) * n
    return pl.pallas_call(
        body, name=name, in_specs=[hbm] * n_data + [pl.BlockSpec(memory_space=pl.ANY)] * len(after), out_specs=[hbm] * n,
        out_shape=[jax.ShapeDtypeStruct(a.shape, a.dtype) for a in lands],
        scratch_shapes=[pltpu.SemaphoreType.DMA((k,)), pltpu.SemaphoreType.DMA((k,))],
        input_output_aliases={} if fresh else {ns + i: i for i in range(n)},
        compiler_params=pltpu.CompilerParams(vmem_limit_bytes=VMEM_LIMIT))(*ins)


def _plan_start(plan, srcs, land_shapes, after, name):
    n = len(srcs)

    def body(*refs):
        src_refs, land_refs = refs[:n], refs[n:2 * n]
        send_sems, recv_sems = refs[2 * n + 1], refs[2 * n + 2]
        for cp in _plan_copies(plan, n, src_refs, land_refs, send_sems, recv_sems, False):
            cp.start()
        refs[-1][...] = jnp.zeros_like(refs[-1])

    hbm = pl.BlockSpec(memory_space=pltpu.HBM)
    sem = pl.BlockSpec(memory_space=pltpu.SEMAPHORE)
    k = len(plan) * n
    srcs = [pltpu.with_memory_space_constraint(a, pltpu.HBM) for a in srcs]
    lands = [pltpu.with_memory_space_constraint(lax.empty(shp, a.dtype), pltpu.HBM) for shp, a in zip(land_shapes, srcs)]
    res = pl.pallas_call(
        body, name=name,
        in_specs=[hbm] * (2 * n) + [pl.BlockSpec(memory_space=pl.ANY)],
        out_specs=[sem, sem] + [hbm] * (2 * n) + [pl.BlockSpec(memory_space=pltpu.VMEM)],
        out_shape=[pltpu.SemaphoreType.DMA((k,)), pltpu.SemaphoreType.DMA((k,))]
        + [pltpu.HBM(a.shape, a.dtype) for a in srcs] + [pltpu.HBM(shp, a.dtype) for shp, a in zip(land_shapes, srcs)]
        + [jax.ShapeDtypeStruct((8, LANES), F32)],
        input_output_aliases={i: 2 + i for i in range(2 * n)},
        compiler_params=pltpu.CompilerParams(has_side_effects=pltpu.SideEffectType.DATAFLOW_SIDE_EFFECTING),
    )(*srcs, *lands, after)
    return res[0], res[1], list(res[2:2 + n]), list(res[2 + n:2 + 2 * n]), res[-1]


def _plan_wait(plan, handle, after, name):
    send_sems, recv_sems, srcs, lands, _ = handle
    n = len(srcs)
    after = list(after)

    def body(*refs):
        src_refs, land_refs = refs[:n], refs[n:2 * n]
        ssem, rsem = refs[2 * n], refs[2 * n + 1]
        for cp in _plan_copies(plan, n, src_refs, land_refs, ssem, rsem, False):
            cp.wait_send()
        for cp in _plan_copies(plan, n, src_refs, land_refs, ssem, rsem, True):
            cp.wait_recv()

    hbm = pl.BlockSpec(memory_space=pltpu.HBM)
    sem = pl.BlockSpec(memory_space=pltpu.SEMAPHORE)
    res = pl.pallas_call(
        body, name=name,
        in_specs=[hbm] * (2 * n) + [sem, sem] + [pl.BlockSpec(memory_space=pl.ANY)] * len(after),
        out_specs=[hbm] * (2 * n),
        out_shape=[pltpu.HBM(a.shape, a.dtype) for a in srcs] + [pltpu.HBM(a.shape, a.dtype) for a in lands],
        input_output_aliases={i: i for i in range(2 * n)},
        compiler_params=pltpu.CompilerParams(has_side_effects=pltpu.SideEffectType.DATAFLOW_SIDE_EFFECTING),
    )(*srcs, *lands, send_sems, recv_sems, *after)
    return list(res[:n]), list(res[n:])


def _pair_sum(send, got, core, name):
    _, r, c = send.shape
    tr = 1024 if r % 1024 == 0 else r

    def body(core_ref, s_ref, g_ref, o_ref):
        o_ref[0] = (s_ref[0].astype(F32) + g_ref[0].astype(F32)).astype(o_ref.dtype)

    return pl.pallas_call(
        body, name=name,
        grid_spec=pltpu.PrefetchScalarGridSpec(
            num_scalar_prefetch=1, grid=(N_DEV // 2, r // tr),
            in_specs=[pl.BlockSpec((1, tr, c), lambda kc, i, core_ref: (2 * kc + core_ref[0], i, 0)),
                      pl.BlockSpec((1, tr, c), lambda kc, i, core_ref: (kc, i, 0))],
            out_specs=pl.BlockSpec((1, tr, c), lambda kc, i, core_ref: (kc, i, 0))),
        out_shape=jax.ShapeDtypeStruct((N_DEV // 2, r, c), send.dtype),
        compiler_params=_cp(("parallel", "parallel")))(core, send, got)


def _ada_mod(c_all, ada_w, ada_b_cols, name):
    nl, d, wc = ada_w.shape

    def body(c_ref, w_ref, b_ref, ca_ref, mod_ref):
        cv = c_ref[...]
        ca = cv * _sigmoid(cv)
        ca_ref[...] = ca
        mod_ref[0] = _dotf(ca, w_ref[0]) + b_ref[0]

    return pl.pallas_call(
        body, grid=(nl,),
        in_specs=[pl.BlockSpec((N_DEV, d), lambda l: (0, 0)), pl.BlockSpec((1, d, wc), lambda l: (l, 0, 0)),
                  pl.BlockSpec((1, 1, wc), lambda l: (l, 0, 0))],
        out_specs=[pl.BlockSpec((N_DEV, d), lambda l: (0, 0)), pl.BlockSpec((1, N_DEV, wc), lambda l: (l, 0, 0))],
        out_shape=[jax.ShapeDtypeStruct((N_DEV, d), F32), jax.ShapeDtypeStruct((nl, N_DEV, wc), F32)],
        name=name, compiler_params=_cp(("arbitrary",)))(c_all, ada_w, ada_b_cols)


def _adam(w, g, m, v):
    m2 = ADAM_B1 * m + (1.0 - ADAM_B1) * g
    v2 = ADAM_B2 * v + (1.0 - ADAM_B2) * (g * g)
    m_hat = m2 / (1.0 - ADAM_B1 ** ADAM_STEP)
    v_hat = v2 / (1.0 - ADAM_B2 ** ADAM_STEP)
    delta = -ADAM_LR * (m_hat / (jnp.sqrt(v_hat) + ADAM_EPS) + ADAM_WD * w)
    return delta, m2, v2


def _ada_grad_adam(c_act, d_mod, w, m, v, name):
    nl, d, wc = w.shape
    tk = min(1024, d)

    def body(c_ref, dm_ref, w_ref, m_ref, v_ref, g_ref, dl_ref, m2_ref, v2_ref):
        g = _dotf_tn(c_ref[...], dm_ref[0])
        delta, m2, v2 = _adam(w_ref[0], g, m_ref[0], v_ref[0])
        g_ref[0], dl_ref[0], m2_ref[0], v2_ref[0] = g, delta, m2, v2

    blk = pl.BlockSpec((1, tk, wc), lambda l, i: (l, i, 0))
    shp = jax.ShapeDtypeStruct(w.shape, F32)
    return pl.pallas_call(
        body, grid=(nl, d // tk),
        in_specs=[pl.BlockSpec((N_DEV, tk), lambda l, i: (0, i)), pl.BlockSpec((1, N_DEV, wc), lambda l, i: (l, 0, 0)),
                  blk, blk, blk],
        out_specs=[blk] * 4, out_shape=[shp] * 4, name=name,
        compiler_params=_cp(("parallel", "parallel")))(c_act, d_mod, w, m, v)


def _adam_big(recv, w, m, v, layer, prev, name, after=()):
    nl, r, c = w.shape
    tr = 512 if r % 512 == 0 else r
    nparts = recv.shape[0]

    def body(rc_ref, w_ref, m_ref, v_ref, *rest):
        g_ref, dl_ref, m2_ref, v2_ref = rest[-4:]
        g = rc_ref[0].astype(F32)
        for d in range(1, nparts):
            g = g + rc_ref[d].astype(F32)
        delta, m2, v2 = _adam(w_ref[0], g, m_ref[0], v_ref[0])
        g_ref[0], dl_ref[0], m2_ref[0], v2_ref[0] = g, delta, m2, v2

    blk = pl.BlockSpec((1, tr, c), lambda i: (layer, i, 0))
    shp = jax.ShapeDtypeStruct(w.shape, F32)
    prev = () if prev is None else tuple(prev)
    return pl.pallas_call(
        body, grid=(r // tr,),
        in_specs=[pl.BlockSpec((nparts, tr, c), lambda i: (0, i, 0)), blk, blk, blk]
        + [pl.BlockSpec(memory_space=pl.ANY)] * (len(prev) + len(after)),
        out_specs=[blk] * 4, out_shape=[shp] * 4, name=name,
        input_output_aliases={4 + j: j for j in range(len(prev))},
        compiler_params=_cp(("parallel",)))(recv, w, m, v, *prev, *after)


def _sum_devices(gathered, name):
    _, r, c = gathered.shape

    def body(g_ref, o_ref):
        acc = g_ref[0]
        for d in range(1, N_DEV):
            acc = acc + g_ref[d]
        o_ref[...] = acc

    spec = pl.BlockSpec(memory_space=pltpu.VMEM)
    return pl.pallas_call(body, in_specs=[spec], out_specs=spec, out_shape=jax.ShapeDtypeStruct((r, c), F32),
                          name=name, compiler_params=pltpu.CompilerParams(vmem_limit_bytes=VMEM_LIMIT))(gathered)


def _adam_small(ws, gs, ms, vs, name):
    n = len(ws)

    def body(*refs):
        for i in range(n):
            w_ref, g_ref, m_ref, v_ref = (refs[k * n + i] for k in range(4))
            dl_ref, m2_ref, v2_ref = (refs[(4 + k) * n + i] for k in range(3))
            dl_ref[...], m2_ref[...], v2_ref[...] = _adam(w_ref[...], g_ref[...], m_ref[...], v_ref[...])

    spec = pl.BlockSpec(memory_space=pltpu.VMEM)
    shapes = [jax.ShapeDtypeStruct(w.shape, F32) for w in ws]
    res = pl.pallas_call(body, in_specs=[spec] * (4 * n), out_specs=[spec] * (3 * n), out_shape=shapes * 3, name=name,
                         compiler_params=pltpu.CompilerParams(vmem_limit_bytes=VMEM_LIMIT))(*ws, *gs, *ms, *vs)
    return res[:n], res[n:2 * n], res[2 * n:]


def _pack(parts):
    flat = jnp.concatenate([p.reshape(-1).astype(F32) for p in parts])
    assert flat.shape[0] % LANES == 0, flat.shape
    return flat.reshape(-1, LANES)


def _unpack(packed, shapes):
    flat = packed.reshape(-1)
    out, off = [], 0
    for shp in shapes:
        size = 1
        for dim in shp:
            size *= dim
        out.append(flat[off:off + size].reshape(shp))
        off += size
    return out


def _gather_cols(g, per):
    g = jnp.moveaxis(g, 0, -2)
    return g.reshape(g.shape[:-2] + (N_DEV * per,))


def _scatter_cols(g, per):
    return jnp.moveaxis(g.reshape(g.shape[:-1] + (N_DEV, per)), -2, 0)


def _my_cols(full, me, per):
    return lax.dynamic_slice_in_dim(full, me * per, per, axis=full.ndim - 1)


def kernel(x, c, positions, ada_w, ada_b, norm_g, w_in, gla_wg_f, gla_bg_f, gla_wg_b, gla_bg_b, gla_norm_g, mla_q_norm_g, mla_kv_norm_g, mla_w_uq, mla_w_ukv, mla_out_g, conv_w, conv_out_g, w_out, final_g, loss_target, m_ada_w, m_ada_b, m_norm_g, m_w_in, m_gla_wg_f, m_gla_bg_f, m_gla_wg_b, m_gla_bg_b, m_gla_norm_g, m_mla_q_norm_g, m_mla_kv_norm_g, m_mla_w_uq, m_mla_w_ukv, m_mla_out_g, m_conv_w, m_conv_out_g, m_w_out, m_final_g, v_ada_w, v_ada_b, v_norm_g, v_w_in, v_gla_wg_f, v_gla_bg_f, v_gla_wg_b, v_gla_bg_b, v_gla_norm_g, v_mla_q_norm_g, v_mla_kv_norm_g, v_mla_w_uq, v_mla_w_ukv, v_mla_out_g, v_conv_w, v_conv_out_g, v_w_out, v_final_g):
    me = 4 * lax.axis_index("x") + 2 * lax.axis_index("y") + lax.axis_index("c")
    nl = ada_w.shape[0]
    s, d = x.shape[1], x.shape[2]
    ada_cols = ada_w.shape[2]
    wgc, cwc = gla_wg_f.shape[2], conv_w.shape[2]

    (g0,) = _gather_small([_pack([c, gla_wg_f, gla_wg_b, conv_w])], "gather_small_in")
    g0 = g0.reshape(N_DEV, -1)
    o1, o2, o3 = d, d + gla_wg_f.size, d + 2 * gla_wg_f.size
    c_all = g0[:, :o1]
    wgf_full = _gather_cols(g0[:, o1:o2].reshape((N_DEV,) + gla_wg_f.shape), wgc)
    wgb_full = _gather_cols(g0[:, o2:o3].reshape((N_DEV,) + gla_wg_b.shape), wgc)
    convw_full = _gather_cols(g0[:, o3:].reshape((N_DEV,) + conv_w.shape), cwc)

    ada_b_cols = _my_cols(ada_b, me, ada_cols).reshape(nl, 1, ada_cols)
    c_act, mod_cols = _ada_mod(c_all, ada_w, ada_b_cols, "ada_mod")
    (g1,) = _gather_small([_pack([mod_cols])], "gather_mod")
    mod_all = g1.reshape(N_DEV, nl, N_DEV, ada_cols)
    mod_mine = _gather_cols(lax.dynamic_index_in_dim(mod_all, me, axis=2, keepdims=False), ada_cols)

    inv_freq = ROPE_THETA ** (-jnp.arange(0, MROPE, 2, dtype=F32) / MROPE)
    ang = positions[0].astype(F32)[:, None] * inv_freq
    cos, sin = jnp.tile(jnp.cos(ang), (1, LANES * 2 // MROPE)), jnp.tile(jnp.sin(ang), (1, LANES * 2 // MROPE))

    big = [w_in, w_out, mla_w_uq, mla_w_ukv]
    big_names = ["w_in", "w_out", "mla_w_uq", "mla_w_ukv"]

    def local_blocks(l):
        return [w[l].astype(MXU) for w in big]

    def put_own(lands, own):
        return [lax.dynamic_update_index_in_dim(ld, o, me, 0) for ld, o in zip(lands, own)]

    def layer_weights(l, gw_in=None, gw_out=None, gw_uq=None, gw_ukv=None):
        small = dict(norm_g=norm_g[l], gla_wg_f=wgf_full[l], gla_bg_f=gla_bg_f[l], gla_wg_b=wgb_full[l],
                     gla_bg_b=gla_bg_b[l], gla_norm_g=gla_norm_g[l], mla_q_norm_g=mla_q_norm_g[l],
                     mla_kv_norm_g=mla_kv_norm_g[l], mla_out_g=mla_out_g[l], conv_w=convw_full[l],
                     conv_out_g=conv_out_g[l])
        return _prep_layer_weights(
            None if gw_in is None else _perm_gathered(gw_in, IN_SEGS, PW),
            None if gw_out is None else gw_out.reshape((-1,) + gw_out.shape[2:]),
            None if gw_uq is None else _heads_apart(_gather_cols(gw_uq, mla_w_uq.shape[2])),
            None if gw_ukv is None else _gather_cols(gw_ukv, mla_w_ukv.shape[2]), small)

    def land_shapes(blocks, slots):
        return [jax.ShapeDtypeStruct((slots,) + b.shape, b.dtype) for b in blocks]

    def slots_of(blocks):
        return [(N_DEV,) + b.shape for b in blocks]

    def forwarded(lands, blocks, tag):
        return put_own(_exchange_hbm(AG_FORWARD, None, lands, f"gather_{tag}_forward"), blocks)

    first = local_blocks(0)
    w_in_start = _plan_start(AG_SPREAD, first[:1], slots_of(first[:1]), mod_mine, "gather_w_in_l0_start")
    adam_w_in = [a + w_in_start[-1][0, 0] for a in (w_in, m_w_in, v_w_in)]
    shift0, scale0 = (mod_mine[0, i * d:(i + 1) * d].reshape(1, d) for i in range(2))
    h_first = _norm_mod(x[0], norm_g[0].reshape(1, d), scale0 + w_in_start[-1][0, 0], shift0, "norm_mod_l0")
    (gw_in,) = forwarded(*reversed(_plan_wait(AG_SPREAD, w_in_start, adam_w_in + [h_first], "gather_w_in_l0_wait")),
                         "w_in_l0")
    rest = _plan_start(AG_SPREAD, first[1:], slots_of(first[1:]), gw_in, "gather_rest_l0_start")
    h = x[0]
    saved, layers, mods = [], [], []
    pending = {}
    for l in range(nl):
        shift, scale, gate = (mod_mine[l, i * d:(i + 1) * d].reshape(1, d) for i in range(3))
        nxt = local_blocks(l + 1) if l + 1 < nl else None

        def start_next(after, wt_late, l=l, nxt=nxt):
            if nxt is not None:
                pending[l + 1] = _plan_start(AG_SPREAD, nxt, slots_of(nxt), after, f"gather_weights_l{l + 1}_start")
                wt_late["q_norm_g"] = layers[l]["q_norm_g"] + pending[l + 1][-1][0, 0]
            return wt_late

        if l == 0:
            in_after = (rest[-1],)
            layers.append(layer_weights(0, gw_in))

            def late(proj):
                got = forwarded(*reversed(_plan_wait(AG_SPREAD, rest, [proj], "gather_rest_l0_wait")), "rest_l0")
                full = layer_weights(0, None, *got)
                return start_next(got[0], {k: full[k] for k in ("w_out", "w_uq", "w_ukv")})
        else:
            got = forwarded(*reversed(_plan_wait(AG_SPREAD, pending.pop(l), [h], f"gather_weights_l{l}_wait")), f"weights_l{l}")
            layers.append(layer_weights(l, *got))
            in_after = ()

            def late(proj):
                return start_next(proj, {})
        mods.append((shift, scale, gate))
        h, sv = _layer_fwd(h, mods[l], layers[l], cos, sin, f"l{l}", late, in_after, h_first if l == 0 else None)
        saved.append(sv)
        blocks = nxt
    loss_part, d_h, d_final_g = _final_loss(h, final_g.reshape(1, d), loss_target[0], "final_loss")
    loss = lax.psum(loss_part[0, 0], ("x", "y", "c"))
    shift, scale, gate = mods[-1]
    mods[-1] = (shift, scale, gate + 0.0 * loss)

    send_of = dict(w_in=lambda g: _scatter_perm(g, IN_SEGS, w_in.shape[2]),
                   w_out=lambda g: g.reshape((N_DEV,) + w_out.shape[1:]),
                   w_uq=lambda g: _scatter_cols(_heads_together(g), mla_w_uq.shape[2]),
                   w_ukv=lambda g: _scatter_cols(g, mla_w_ukv.shape[2]))

    def grad_sends(gr):
        return [send_of[k](g).astype(MXU) for k, g in gr.items()]

    my_chip = me // 2
    my_core = (me % 2).astype(jnp.int32).reshape(1)

    def chip_sums(gr, tag):
        sends = grad_sends(gr)
        got = _exchange_hbm(RS_PAIR, sends, land_shapes([sd[0] for sd in sends], N_DEV // 2), f"scatter_grads_{tag}_pair")
        return [_pair_sum(sd, gt, my_core, f"pair_sum_{k}_{tag}") for sd, gt, k in zip(sends, got, gr)]

    def with_own_chip(lands, sums):
        return [lax.dynamic_update_index_in_dim(ld, lax.dynamic_index_in_dim(sm, my_chip, axis=0, keepdims=False),
                                                my_chip, 0) for ld, sm in zip(lands, sums)]

    small_names = ["norm_g", "gla_wg_f", "gla_bg_f", "gla_wg_b", "gla_bg_b", "gla_norm_g", "mla_q_norm_g",
                   "mla_kv_norm_g", "mla_out_g", "conv_w", "conv_out_g"]
    d_mods, grads, recv = [None] * nl, [None] * nl, [None] * nl
    flight = {}
    small = {}

    def gather_small(d_x, d_mod0, gr0):
        d_mods[0], grads[0] = d_mod0, _natural_small(gr0)
        d_mod_mine = jnp.stack([jnp.concatenate(d_mods[l], axis=-1)[0] for l in range(nl)])
        parts = [d_mod_mine] + [jnp.stack([grads[l][n] for l in range(nl)]) for n in small_names] + [d_final_g]
        (g2,) = _gather_small([_pack(parts)], "gather_small_grads")
        small["d_mod_all"] = g2.reshape(N_DEV, -1)[:, :d_mod_mine.size].reshape(N_DEV, nl, 3 * d)
        small["summed"] = dict(zip(["ada_b"] + small_names + ["final_g"],
                                   _unpack(_sum_devices(g2, "sum_small_grads"), [p.shape for p in parts])))
        return (g2,)

    pairs = {}
    def end_flight(key, after, name):
        sm, lands = _plan_wait(RS_CHIPS, flight.pop(key)[0], after, name)
        return with_own_chip(lands, sm)

    for l in reversed(range(nl)):
        def ship(big_grads, l=l):
            if l > 0:
                sends = grad_sends(big_grads)
                pairs[l] = (_plan_start(RS_PAIR, sends, [(N_DEV // 2,) + sd.shape[1:] for sd in sends],
                                        big_grads["w_in"], f"scatter_grads_l{l}_pair_start"), sends)
                return (pairs[l][0][-1],)
            sm = chip_sums(dict(w_in=big_grads["w_in"]), f"l{l}")
            flight[l] = (_plan_start(RS_CHIPS, sm, [a.shape for a in sm], big_grads["w_in"], f"scatter_grads_l{l}_start"),
                         sm)
            return (flight[l][0][-1],)

        def ship_rest(rest_grads, l=l):
            if l + 1 in flight:
                recv[l + 1] = end_flight(l + 1, list(rest_grads.values()), f"scatter_grads_l{l + 1}_wait")
            sm = chip_sums(rest_grads, f"l{l}_rest")
            flight["rest"] = (_plan_start(RS_CHIPS, sm, [a.shape for a in sm], rest_grads["w_out"],
                                          f"scatter_grads_l{l}_rest_start"), sm)
            return flight["rest"][0][-1][0, 0]

        shift, scale, gate = mods[l]
        if l + 1 in flight:
            gate = gate + flight[l + 1][0][-1][0, 0]
        if l > 0:
            d_h, d_mods[l], gr = _layer_bwd(d_h, saved[l], (shift, scale, gate), layers[l], cos, sin, f"l{l}", ship)
            grads[l] = _natural_small(gr)
            sends, got = _plan_wait(RS_PAIR, pairs.pop(l)[0], [d_h], f"scatter_grads_l{l}_pair_wait")
            sm = [_pair_sum(sd, gt, my_core, f"pair_sum_{n}_l{l}") for sd, gt, n in zip(sends, got, big_names)]
            flight[l] = (_plan_start(RS_CHIPS, sm, [a.shape for a in sm], d_h, f"scatter_grads_l{l}_start"), sm)
        else:
            d_h, _, _ = _layer_bwd(d_h, saved[l], (shift, scale, gate), layers[l], cos, sin, f"l{l}", ship, gather_small,
                                   ship_rest)
    pending = flight[0][0]
    grad_x = d_h[None]
    summed = small["summed"]
    summed["gla_wg_f"] = _my_cols(summed["gla_wg_f"], me, wgc)
    summed["gla_wg_b"] = _my_cols(summed["gla_wg_b"], me, wgc)
    summed["conv_w"] = _my_cols(summed["conv_w"], me, cwc)

    d_mod_cols = jnp.moveaxis(_my_cols(small["d_mod_all"], me, ada_cols), 0, 1) + pending[-1][0, 0]
    out = {}
    out["ada_w"] = _ada_grad_adam(c_act, d_mod_cols, ada_w, m_ada_w, v_ada_w, "ada_grad_adam")

    given = dict(ada_b=(ada_b, m_ada_b, v_ada_b), norm_g=(norm_g, m_norm_g, v_norm_g),
                 gla_wg_f=(gla_wg_f, m_gla_wg_f, v_gla_wg_f), gla_bg_f=(gla_bg_f, m_gla_bg_f, v_gla_bg_f),
                 gla_wg_b=(gla_wg_b, m_gla_wg_b, v_gla_wg_b), gla_bg_b=(gla_bg_b, m_gla_bg_b, v_gla_bg_b),
                 gla_norm_g=(gla_norm_g, m_gla_norm_g, v_gla_norm_g),
                 mla_q_norm_g=(mla_q_norm_g, m_mla_q_norm_g, v_mla_q_norm_g),
                 mla_kv_norm_g=(mla_kv_norm_g, m_mla_kv_norm_g, v_mla_kv_norm_g),
                 mla_out_g=(mla_out_g, m_mla_out_g, v_mla_out_g), conv_w=(conv_w, m_conv_w, v_conv_w),
                 conv_out_g=(conv_out_g, m_conv_out_g, v_conv_out_g), final_g=(final_g, m_final_g, v_final_g))
    names = list(given)

    def two_d(a):
        return a.reshape(1, -1) if a.ndim == 1 else a

    g_nat = [summed[n].reshape(given[n][0].shape) for n in names]
    res = _adam_small([two_d(given[n][0]) for n in names], [two_d(g) for g in g_nat],
                      [two_d(given[n][1]) for n in names], [two_d(given[n][2]) for n in names], "adam_small")
    for i, n in enumerate(names):
        out[n] = (g_nat[i],) + tuple(r[i].reshape(given[n][0].shape) for r in res)

    state = dict(w_in=adam_w_in, w_out=(w_out, m_w_out, v_w_out), mla_w_uq=(mla_w_uq, m_mla_w_uq, v_mla_w_uq),
                 mla_w_ukv=(mla_w_ukv, m_mla_w_ukv, v_mla_w_ukv))
    done = [out["ada_w"][0], res[0][0]]
    def adam_layer(l, names, partials):
        for n, rc in zip(names, partials):
            out[n] = _adam_big(rc, *state[n], l, out.get(n), f"adam_{n}_l{l}",
                               (pending[-1],))
        return [out[n][0] for n in names]

    for l in reversed(range(nl)):
        if l > 0:
            done = done + adam_layer(l, big_names, recv[l])
        else:
            done = done + adam_layer(0, big_names[1:], end_flight("rest", done, "scatter_grads_l0_rest_wait"))
            adam_layer(0, big_names[:1], end_flight(0, done, "scatter_grads_l0_wait"))

    order = ["ada_w", "ada_b", "norm_g", "w_in", "gla_wg_f", "gla_bg_f", "gla_wg_b", "gla_bg_b", "gla_norm_g",
             "mla_q_norm_g", "mla_kv_norm_g", "mla_w_uq", "mla_w_ukv", "mla_out_g", "conv_w", "conv_out_g", "w_out",
             "final_g"]
    return (loss, grad_x, *[out[n][0] for n in order], *[out[n][1] for n in order], *[out[n][2] for n in order],
            *[out[n][3] for n in order])
```

```python
import functools

import jax
import jax.numpy as jnp
from jax import lax
from jax.experimental import pallas as pl
from jax.experimental.pallas import tpu as pltpu

F32 = jnp.float32
MXU = jnp.bfloat16
HI = lax.Precision.HIGHEST
N_DEV = 8
MESH = pl.DeviceIdType.MESH

D_MIX = 2048
GH, GDK, GDV = 6, 64, 128
GW = GH * GDV
GQK = GH * GDK
GRANK = 16
GTEMP = 16.0
CHUNK = 64
MH, MQL, MKVL, MNOPE, MROPE, MDV = 6, 384, 256, 128, 64, 128
MW = MH * MDV
MQW = MH * (MNOPE + MROPE)
MKVW = MH * (MNOPE + MDV)
CONV_CH = 512
ROPE_THETA = 10000.0
EPS = 1e-6
IN_DIM = 5856
OZ, OCB, OCC, OCX, OMKV, OGV, OGQ, OGK, OMQ, OT = 0, 2048, 2560, 3072, 3584, 3840, 4608, 4992, 5376, 5760
PW = 5888
LANES = 128
V7X_VMEM_BYTES = 64 * 1024 * 1024
VMEM_LIMIT = V7X_VMEM_BYTES * 7 // 8

ADAM_LR, ADAM_B1, ADAM_B2, ADAM_EPS, ADAM_WD, ADAM_STEP = 0.001, 0.9, 0.999, 1e-08, 0.01, 10


def _cp(sem=None):
    return pltpu.CompilerParams(dimension_semantics=sem, vmem_limit_bytes=VMEM_LIMIT)


def _dot(a, b):
    return jnp.dot(a.astype(MXU), b.astype(MXU), preferred_element_type=F32)


def _dot_nt(a, b):
    return lax.dot_general(a.astype(MXU), b.astype(MXU), (((1,), (1,)), ((), ())), preferred_element_type=F32)


def _dot_tn(a, b):
    return lax.dot_general(a.astype(MXU), b.astype(MXU), (((0,), (0,)), ((), ())), preferred_element_type=F32)


def _dotf(a, b):
    return jnp.dot(a, b, precision=HI, preferred_element_type=F32)


def _dotf_tn(a, b):
    return lax.dot_general(a, b, (((0,), (0,)), ((), ())), precision=HI, preferred_element_type=F32)


def _split3(x):
    hi = x.astype(jnp.bfloat16)
    r1 = x - hi.astype(F32)
    mid = r1.astype(jnp.bfloat16)
    lo = (r1 - mid.astype(F32)).astype(jnp.bfloat16)
    return hi, mid, lo


def _cum_dot(cum, x, transpose=False):
    dn = (((0,), (0,)), ((), ())) if transpose else (((1,), (0,)), ((), ()))
    cb = cum.astype(jnp.bfloat16)
    parts = [lax.dot_general(cb, p, dn, preferred_element_type=F32) for p in _split3(x)]
    return parts[0] + parts[1] + parts[2]


def _rows(s):
    return min(256, s)


def _rows_light(s):
    return min(512, s)


def _rms(x, g):
    r = lax.rsqrt(jnp.mean(x * x, axis=-1, keepdims=True) + EPS)
    return x * r * g


def _rms_bwd(dy, x, g):
    r = lax.rsqrt(jnp.mean(x * x, axis=-1, keepdims=True) + EPS)
    xh = x * r
    dxh = dy * g
    dg = jnp.sum(dy * xh, axis=0, keepdims=True)
    dx = r * (dxh - xh * jnp.mean(dxh * xh, axis=-1, keepdims=True))
    return dx, dg


def _sigmoid(z):
    return jax.nn.sigmoid(z)


def _matmul(a, b, *, dims, tm, tn, tk, out_dtypes, name, epilogue=None, extras=(), extra_kinds=(), after=()):
    if dims == "nn":
        (m, k), n, mul = a.shape, b.shape[1], _dot
    elif dims == "nt":
        (m, k), n, mul = a.shape, b.shape[0], _dot_nt
    else:
        (k, m), n, mul = a.shape, b.shape[1], _dot_tn
    tm, tn, tk = min(tm, m), min(tn, n), min(tk, k)
    assert m % tm == 0 and n % tn == 0 and k % tk == 0, (m, n, k, tm, tn, tk)
    if dims == "nn":
        a_spec = pl.BlockSpec((tm, tk), lambda i, j, kk: (i, kk))
        b_spec = pl.BlockSpec((tk, tn), lambda i, j, kk: (kk, j))
    elif dims == "nt":
        a_spec = pl.BlockSpec((tm, tk), lambda i, j, kk: (i, kk))
        b_spec = pl.BlockSpec((tn, tk), lambda i, j, kk: (j, kk))
    else:
        a_spec = pl.BlockSpec((tk, tm), lambda i, j, kk: (kk, i))
        b_spec = pl.BlockSpec((tk, tn), lambda i, j, kk: (kk, j))
    nk = k // tk
    n_extra = len(extras)
    n_out = len(out_dtypes)
    n_after = len(after)
    extra_specs = []
    for kind in extra_kinds:
        if kind == "mn":
            extra_specs.append(pl.BlockSpec((tm, tn), lambda i, j, kk: (i, j)))
        else:
            extra_specs.append(pl.BlockSpec((1, tn), lambda i, j, kk: (0, j)))

    def finish(res, ex, outs):
        vals = (res,) if epilogue is None else epilogue(res, *[e[...] for e in ex])
        for o, v in zip(outs, vals):
            o[...] = v.astype(o.dtype)

    def body(*refs):
        a_ref, b_ref = refs[0], refs[1]
        ex = refs[2:2 + n_extra]
        outs = refs[2 + n_extra + n_after:2 + n_extra + n_after + n_out]
        if nk == 1:
            finish(mul(a_ref[...], b_ref[...]), ex, outs)
            return
        acc = refs[-1]
        kk = pl.program_id(2)

        @pl.when(kk == 0)
        def _():
            acc[...] = jnp.zeros_like(acc)

        acc[...] += mul(a_ref[...], b_ref[...])

        @pl.when(kk == nk - 1)
        def _():
            finish(acc[...], ex, outs)

    out_spec = pl.BlockSpec((tm, tn), lambda i, j, kk: (i, j))
    res = pl.pallas_call(
        body, grid=(m // tm, n // tn, nk),
        in_specs=[a_spec, b_spec] + extra_specs + [pl.BlockSpec(memory_space=pl.ANY)] * n_after,
        out_specs=[out_spec] * n_out,
        out_shape=[jax.ShapeDtypeStruct((m, n), dt) for dt in out_dtypes],
        scratch_shapes=[] if nk == 1 else [pltpu.VMEM((tm, tn), F32)],
        name=name, compiler_params=_cp(("parallel", "parallel", "arbitrary")),
    )(a, b, *extras, *after)
    return res


def _norm_mod(x, g, scale, shift, name):
    s, d = x.shape
    tr = _rows_light(s)

    def body(x_ref, g_ref, sc_ref, sh_ref, h_ref):
        h = _rms(x_ref[...], g_ref[...]) * (1.0 + sc_ref[...]) + sh_ref[...]
        h_ref[...] = h.astype(h_ref.dtype)

    row = pl.BlockSpec((tr, d), lambda i: (i, 0))
    vec = pl.BlockSpec((1, d), lambda i: (0, 0))
    return pl.pallas_call(body, grid=(s // tr,), in_specs=[row, vec, vec, vec], out_specs=row,
                          out_shape=jax.ShapeDtypeStruct((s, d), MXU), name=name,
                          compiler_params=_cp(("parallel",)))(x, g, scale, shift)


def _gate_tail(dx, below):
    u_ref, gt_ref, du_ref, dgt_ref = below
    du_ref[...] = (dx * gt_ref[...]).astype(du_ref.dtype)
    dgt_ref[...] += jnp.sum(dx * u_ref[...], axis=0, keepdims=True)


def _norm_mod_bwd(d_h, x, d_out, g, scale, name, below=None):
    s, d = x.shape
    tr = _rows(s)

    def body(dh_ref, x_ref, do_ref, g_ref, sc_ref, *rest):
        dx_ref, dsh_ref, dsc_ref, dg_ref = rest[-6:-2] if below is not None else rest
        i = pl.program_id(0)

        @pl.when(i == 0)
        def _():
            dsh_ref[...] = jnp.zeros_like(dsh_ref)
            dsc_ref[...] = jnp.zeros_like(dsc_ref)
            dg_ref[...] = jnp.zeros_like(dg_ref)
            if below is not None:
                rest[-1][...] = jnp.zeros_like(rest[-1])

        dh = dh_ref[...]
        xv = x_ref[...]
        gv = g_ref[...]
        r = lax.rsqrt(jnp.mean(xv * xv, axis=-1, keepdims=True) + EPS)
        xh = xv * r
        dsh_ref[...] += jnp.sum(dh, axis=0, keepdims=True)
        dsc_ref[...] += jnp.sum(dh * (xh * gv), axis=0, keepdims=True)
        dhn = dh * (1.0 + sc_ref[...])
        dg_ref[...] += jnp.sum(dhn * xh, axis=0, keepdims=True)
        dxh = dhn * gv
        dx = do_ref[...] + r * (dxh - xh * jnp.mean(dxh * xh, axis=-1, keepdims=True))
        dx_ref[...] = dx
        if below is not None:
            _gate_tail(dx, (rest[0], rest[1], rest[-2], rest[-1]))

    row = pl.BlockSpec((tr, d), lambda i: (i, 0))
    vec = pl.BlockSpec((1, d), lambda i: (0, 0))
    vshape = jax.ShapeDtypeStruct((1, d), F32)
    extra = () if below is None else tuple(below)
    return pl.pallas_call(body, grid=(s // tr,), in_specs=[row, row, row, vec, vec] + [row, vec][:len(extra)],
                          out_specs=[row, vec, vec, vec] + [row, vec][:len(extra)],
                          out_shape=[jax.ShapeDtypeStruct((s, d), F32), vshape, vshape, vshape]
                          + [jax.ShapeDtypeStruct((s, d), MXU), vshape][:len(extra)],
                          name=name, compiler_params=_cp(("arbitrary",)))(d_h, x, d_out, g, scale, *extra)


def _gate_bwd(d_out, u, gate, name):
    s, d = d_out.shape
    tr = _rows_light(s)

    def body(do_ref, u_ref, gt_ref, du_ref, dgt_ref):
        @pl.when(pl.program_id(0) == 0)
        def _():
            dgt_ref[...] = jnp.zeros_like(dgt_ref)

        do = do_ref[...]
        du_ref[...] = (do * gt_ref[...]).astype(du_ref.dtype)
        dgt_ref[...] += jnp.sum(do * u_ref[...], axis=0, keepdims=True)

    row = pl.BlockSpec((tr, d), lambda i: (i, 0))
    vec = pl.BlockSpec((1, d), lambda i: (0, 0))
    return pl.pallas_call(body, grid=(s // tr,), in_specs=[row, row, vec], out_specs=[row, vec],
                          out_shape=[jax.ShapeDtypeStruct((s, d), MXU), jax.ShapeDtypeStruct((1, d), F32)],
                          name=name, compiler_params=_cp(("arbitrary",)))(d_out, u, gate)


def _final_loss(x, g, target, name, below=None):
    s, d = x.shape
    tr = _rows(s) if below is not None else _rows_light(s)

    def body(x_ref, g_ref, t_ref, *rest):
        loss_ref, dx_ref, dg_ref = rest[-5:-2] if below is not None else rest

        @pl.when(pl.program_id(0) == 0)
        def _():
            loss_ref[...] = jnp.zeros_like(loss_ref)
            dg_ref[...] = jnp.zeros_like(dg_ref)
            if below is not None:
                rest[-1][...] = jnp.zeros_like(rest[-1])

        xv = x_ref[...]
        gv = g_ref[...]
        diff = _rms(xv, gv) - t_ref[...]
        part = 0.5 * jnp.sum(jnp.sum(diff * diff, axis=-1, keepdims=True) / d, axis=0, keepdims=True)
        loss_ref[...] += jnp.broadcast_to(part, loss_ref.shape)
        dx, dg = _rms_bwd(diff / d, xv, gv)
        dx_ref[...] = dx
        dg_ref[...] += dg
        if below is not None:
            _gate_tail(dx, (rest[0], rest[1], rest[-2], rest[-1]))

    row = pl.BlockSpec((tr, d), lambda i: (i, 0))
    vec = pl.BlockSpec((1, d), lambda i: (0, 0))
    lvec = pl.BlockSpec((1, LANES), lambda i: (0, 0))
    vshape = jax.ShapeDtypeStruct((1, d), F32)
    extra = () if below is None else tuple(below)
    return pl.pallas_call(body, grid=(s // tr,), in_specs=[row, vec, row] + [row, vec][:len(extra)],
                          out_specs=[lvec, row, vec] + [row, vec][:len(extra)],
                          out_shape=[jax.ShapeDtypeStruct((1, LANES), F32), jax.ShapeDtypeStruct((s, d), F32), vshape]
                          + [jax.ShapeDtypeStruct((s, d), MXU), vshape][:len(extra)],
                          name=name, compiler_params=_cp(("arbitrary",)))(x, g, target, *extra)


def _shift_rows(u, s, down):
    ri = lax.broadcasted_iota(jnp.int32, u.shape, 0)
    if down:
        return jnp.where(ri == 0, 0.0, pltpu.roll(u, 1, 0))
    return jnp.where(ri == s - 1, 0.0, pltpu.roll(u, s - 1, 0))


def _conv_fwd(proj, conv_w, name):
    s = proj.shape[0]
    nt = CONV_CH // LANES

    def body(cb_ref, cc_ref, cx_ref, w_ref, pre_ref):
        u = cc_ref[...] * cx_ref[...]
        conv = _shift_rows(u, s, True) * w_ref[0:1, :] + u * w_ref[1:2, :] + _shift_rows(u, s, False) * w_ref[2:3, :]
        pre_ref[...] = cb_ref[...] * conv

    def col(off):
        return pl.BlockSpec((s, LANES), lambda j: (0, off // LANES + j))

    return pl.pallas_call(body, grid=(nt,), in_specs=[col(OCB), col(OCC), col(OCX), pl.BlockSpec((3, LANES), lambda j: (0, j))],
                          out_specs=pl.BlockSpec((s, LANES), lambda j: (0, j)),
                          out_shape=jax.ShapeDtypeStruct((s, CONV_CH), F32), name=name,
                          compiler_params=_cp(("parallel",)))(proj, proj, proj, conv_w)


def _conv_bwd(proj, conv_w, d_pre, name):
    s = proj.shape[0]
    nt = CONV_CH // LANES

    def body(cb_ref, cc_ref, cx_ref, w_ref, dp_ref, dcb_ref, dcc_ref, dcx_ref, dw_ref):
        cc, cx = cc_ref[...], cx_ref[...]
        u = cc * cx
        up, dn = _shift_rows(u, s, True), _shift_rows(u, s, False)
        w0, w1, w2 = w_ref[0:1, :], w_ref[1:2, :], w_ref[2:3, :]
        conv = up * w0 + u * w1 + dn * w2
        dp = dp_ref[...]
        dcb_ref[...] = (dp * conv).astype(dcb_ref.dtype)
        dconv = dp * cb_ref[...]
        du = _shift_rows(dconv, s, False) * w0 + dconv * w1 + _shift_rows(dconv, s, True) * w2
        dcc_ref[...] = (du * cx).astype(dcc_ref.dtype)
        dcx_ref[...] = (du * cc).astype(dcx_ref.dtype)
        dw_ref[0:1, :] = jnp.sum(dconv * up, axis=0, keepdims=True)
        dw_ref[1:2, :] = jnp.sum(dconv * u, axis=0, keepdims=True)
        dw_ref[2:3, :] = jnp.sum(dconv * dn, axis=0, keepdims=True)

    def col(off):
        return pl.BlockSpec((s, LANES), lambda j: (0, off // LANES + j))

    blk = pl.BlockSpec((s, LANES), lambda j: (0, j))
    wblk = pl.BlockSpec((3, LANES), lambda j: (0, j))
    full = jax.ShapeDtypeStruct((s, CONV_CH), MXU)
    return pl.pallas_call(body, grid=(nt,), in_specs=[col(OCB), col(OCC), col(OCX), wblk, blk],
                          out_specs=[blk, blk, blk, wblk],
                          out_shape=[full, full, full, jax.ShapeDtypeStruct((3, CONV_CH), F32)],
                          name=name, compiler_params=_cp(("parallel",)))(proj, proj, proj, conv_w, d_pre)


GLA_SUB = 8


def _gla_gates(t_ref, wg_ref, bg_ref):
    t = t_ref[...]
    a = _dot(t, wg_ref[...]) + bg_ref[...]
    la = (jnp.minimum(a, 0.0) - jnp.log(1.0 + jnp.exp(-jnp.abs(a)))) / GTEMP
    return t, a, la


def _gla_masks(reverse):
    ri = lax.broadcasted_iota(jnp.int32, (CHUNK, CHUNK), 0)
    ci = lax.broadcasted_iota(jnp.int32, (CHUNK, CHUNK), 1)
    if reverse:
        cum, mask, mask_t = ci >= ri, ci > ri, ri > ci
    else:
        cum, mask, mask_t = ci <= ri, ci <= ri, ri <= ci
    return cum.astype(F32), mask, mask_t


def _gla_specs(s, reverse):
    nsub = min(GLA_SUB, s // CHUNK)
    nsteps = s // (CHUNK * nsub)

    def row(n):
        return nsteps - 1 - n if reverse else n

    def chunk(pi):
        return nsub - 1 - pi if reverse else pi

    return nsub, nsteps, row, chunk


def _gla_fwd(proj, wg_pad, bg, reverse, name):
    s = proj.shape[0]
    nsub, nsteps, row, chunk = _gla_specs(s, reverse)
    rb = nsub * CHUNK

    def body(q_ref, k_ref, v_ref, t_ref, wg_ref, bg_ref, o_ref, st_ref, state):
        @pl.when(pl.program_id(0) == 0)
        def _():
            state[...] = jnp.zeros_like(state)

        _, _, la = _gla_gates(t_ref, wg_ref, bg_ref)
        cumf, mask, _ = _gla_masks(reverse)
        lane = lax.broadcasted_iota(jnp.int32, (CHUNK, LANES), 1)
        for pi in range(nsub):
            rows = slice(chunk(pi) * CHUNK, (chunk(pi) + 1) * CHUNK)
            la_c = la[rows]
            b_all = _cum_dot(cumf, la_c)
            bl_all = jnp.sum(la_c, axis=0, keepdims=True)
            for p in range(GH // 2):
                sl = slice(p * LANES, (p + 1) * LANES)
                b, bl = b_all[:, sl], bl_all[:, sl]
                qd = q_ref[rows, sl] * (GDK ** -0.5) * jnp.exp(b)
                ki = k_ref[rows, sl] * jnp.exp(-b)
                kte = k_ref[rows, sl] * jnp.exp(bl - b)
                decay = jnp.exp(bl)
                for half in range(2):
                    h = 2 * p + half
                    lm = (lane < GDK) if half == 0 else (lane >= GDK)
                    qd_h = jnp.where(lm, qd, 0.0)
                    kte_h = jnp.where(lm, kte, 0.0)
                    v_h = v_ref[rows, h * GDV:(h + 1) * GDV]
                    st = state[h]
                    a_mat = jnp.where(mask, _dot_nt(qd_h, ki), 0.0)
                    o_ref[rows, h * GDV:(h + 1) * GDV] = _dot(a_mat, v_h) + _dot_nt(qd_h, st)
                    st_ref[pi, h] = st
                    state[h] = st * decay + _dot_tn(v_h, kte_h)

    return pl.pallas_call(
        body, grid=(nsteps,),
        in_specs=[pl.BlockSpec((rb, GQK), lambda n: (row(n), OGQ // GQK)),
                  pl.BlockSpec((rb, GQK), lambda n: (row(n), OGK // GQK)),
                  pl.BlockSpec((rb, GW), lambda n: (row(n), OGV // GW)),
                  pl.BlockSpec((rb, LANES), lambda n: (row(n), OT // LANES)),
                  pl.BlockSpec((LANES, GQK), lambda n: (0, 0)),
                  pl.BlockSpec((1, GQK), lambda n: (0, 0))],
        out_specs=[pl.BlockSpec((rb, GW), lambda n: (row(n), 0)),
                   pl.BlockSpec((nsub, GH, GDV, LANES), lambda n: (n, 0, 0, 0))],
        out_shape=[jax.ShapeDtypeStruct((s, GW), F32), jax.ShapeDtypeStruct((s // CHUNK, GH, GDV, LANES), F32)],
        scratch_shapes=[pltpu.VMEM((GH, GDV, LANES), F32)],
        name=name, compiler_params=_cp(("arbitrary",)))(proj, proj, proj, proj, wg_pad, bg)


def _gla_bwd(proj, wg_pad, bg, states, d_o, reverse, name):
    s = proj.shape[0]
    nsub, nsteps, row, chunk = _gla_specs(s, reverse)
    rb = nsub * CHUNK

    def body(q_ref, k_ref, v_ref, t_ref, wg_ref, bg_ref, st_ref, do_ref,
             dq_ref, dk_ref, dv_ref, dt_ref, dwg_ref, dbg_ref, dstate, da_buf):
        @pl.when(pl.program_id(0) == 0)
        def _():
            dstate[...] = jnp.zeros_like(dstate)
            dwg_ref[...] = jnp.zeros_like(dwg_ref)
            dbg_ref[...] = jnp.zeros_like(dbg_ref)

        t, a, la = _gla_gates(t_ref, wg_ref, bg_ref)
        cumf, mask, mask_t = _gla_masks(reverse)
        lane = lax.broadcasted_iota(jnp.int32, (CHUNK, LANES), 1)
        for pi in reversed(range(nsub)):
            rows = slice(chunk(pi) * CHUNK, (chunk(pi) + 1) * CHUNK)
            la_c = la[rows]
            b_all = _cum_dot(cumf, la_c)
            bl_all = jnp.sum(la_c, axis=0, keepdims=True)
            for p in range(GH // 2):
                sl = slice(p * LANES, (p + 1) * LANES)
                b, bl = b_all[:, sl], bl_all[:, sl]
                e, ei, ee = jnp.exp(b), jnp.exp(-b), jnp.exp(bl - b)
                qd = q_ref[rows, sl] * (GDK ** -0.5) * e
                ki, kte = k_ref[rows, sl] * ei, k_ref[rows, sl] * ee
                decay = jnp.exp(bl)
                dqd = jnp.zeros((CHUNK, LANES), F32)
                dki = jnp.zeros((CHUNK, LANES), F32)
                dkte = jnp.zeros((CHUNK, LANES), F32)
                ddecay = jnp.zeros((1, LANES), F32)
                for half in range(2):
                    h = 2 * p + half
                    lm = (lane < GDK) if half == 0 else (lane >= GDK)
                    qd_h = jnp.where(lm, qd, 0.0)
                    ki_h = jnp.where(lm, ki, 0.0)
                    kte_h = jnp.where(lm, kte, 0.0)
                    v_h = v_ref[rows, h * GDV:(h + 1) * GDV]
                    do_h = do_ref[rows, h * GDV:(h + 1) * GDV]
                    st = st_ref[pi, h]
                    dst = dstate[h]
                    at_mat = jnp.where(mask_t, _dot_nt(ki_h, qd_h), 0.0)
                    da_mat = jnp.where(mask, _dot_nt(do_h, v_h), 0.0)
                    dat_mat = jnp.where(mask_t, _dot_nt(v_h, do_h), 0.0)
                    dv_ref[rows, h * GDV:(h + 1) * GDV] = _dot(at_mat, do_h) + _dot_nt(kte_h, dst)
                    dqd += _dot(jnp.concatenate([do_h, da_mat], axis=1), jnp.concatenate([st, ki_h], axis=0))
                    dki += _dot(dat_mat, qd_h)
                    dkte += _dot(v_h, dst)
                    ddecay += jnp.sum(dst * st, axis=0, keepdims=True)
                    dstate[h] = dst * decay + _dot_tn(do_h, qd_h)
                dq_ref[rows, sl] = dqd * e * (GDK ** -0.5)
                dk_ref[rows, sl] = dki * ei + dkte * ee
                db = dqd * qd - dki * ki - dkte * kte
                dbl = jnp.sum(dkte * kte, axis=0, keepdims=True) + decay * ddecay
                da_buf[rows, sl] = _cum_dot(cumf, db, True) + dbl
        da = da_buf[...] * (1.0 / GTEMP) * _sigmoid(-a)
        dt_ref[...] = _dot_nt(da, wg_ref[...])
        dwg_ref[...] += _dot_tn(t, da)
        dbg_ref[...] += jnp.sum(da, axis=0, keepdims=True)

    def prow(j):
        return row(nsteps - 1 - j)

    return pl.pallas_call(
        body, grid=(nsteps,),
        in_specs=[pl.BlockSpec((rb, GQK), lambda j: (prow(j), OGQ // GQK)),
                  pl.BlockSpec((rb, GQK), lambda j: (prow(j), OGK // GQK)),
                  pl.BlockSpec((rb, GW), lambda j: (prow(j), OGV // GW)),
                  pl.BlockSpec((rb, LANES), lambda j: (prow(j), OT // LANES)),
                  pl.BlockSpec((LANES, GQK), lambda j: (0, 0)),
                  pl.BlockSpec((1, GQK), lambda j: (0, 0)),
                  pl.BlockSpec((nsub, GH, GDV, LANES), lambda j: (nsteps - 1 - j, 0, 0, 0)),
                  pl.BlockSpec((rb, GW), lambda j: (prow(j), 0))],
        out_specs=[pl.BlockSpec((rb, GQK), lambda j: (prow(j), 0)),
                   pl.BlockSpec((rb, GQK), lambda j: (prow(j), 0)),
                   pl.BlockSpec((rb, GW), lambda j: (prow(j), 0)),
                   pl.BlockSpec((rb, LANES), lambda j: (prow(j), 0)),
                   pl.BlockSpec((LANES, GQK), lambda j: (0, 0)),
                   pl.BlockSpec((1, GQK), lambda j: (0, 0))],
        out_shape=[jax.ShapeDtypeStruct((s, GQK), F32), jax.ShapeDtypeStruct((s, GQK), F32),
                   jax.ShapeDtypeStruct((s, GW), F32), jax.ShapeDtypeStruct((s, LANES), F32),
                   jax.ShapeDtypeStruct((LANES, GQK), F32), jax.ShapeDtypeStruct((1, GQK), F32)],
        scratch_shapes=[pltpu.VMEM((GH, GDV, LANES), F32), pltpu.VMEM((rb, GQK), F32)],
        name=name, compiler_params=_cp(("arbitrary",)))(proj, proj, proj, proj, wg_pad, bg, states, d_o)


def _rot_half(x):
    lane = lax.broadcasted_iota(jnp.int32, x.shape, 1)
    first = (lane % MROPE) < (MROPE // 2)
    return jnp.where(first, -pltpu.roll(x, LANES - MROPE // 2, 1), pltpu.roll(x, MROPE // 2, 1))


def _mla_prep(proj, cos, sin, qg, kvg, w_uq, w_ukv, name):
    s = proj.shape[0]
    tr = _rows(s)

    def body(mq_ref, mkv_ref, t_ref, cos_ref, sin_ref, qg_ref, kvg_ref, wuq_ref, wukv_ref, q_ref, k_ref, v_ref):
        cosv, sinv = cos_ref[...], sin_ref[...]
        lane = lax.broadcasted_iota(jnp.int32, (tr, LANES), 1)

        def rope(xv):
            return xv * cosv + _rot_half(xv) * sinv

        qm = _dot(_rms(mq_ref[...], qg_ref[...]), wuq_ref[...])
        kv = _dot(_rms(mkv_ref[...], kvg_ref[...]), wukv_ref[...])
        kr_lo = jnp.where(lane < MROPE, rope(t_ref[...]), 0.0)
        kr_hi = pltpu.roll(kr_lo, MROPE, 1)
        for p in range(MH // 2):
            r = rope(qm[:, MW + p * LANES:MW + (p + 1) * LANES]).astype(q_ref.dtype)
            q_ref[2 * p, :, LANES:] = r
            q_ref[2 * p + 1, :, LANES:] = r
        for h in range(MH):
            q_ref[h, :, :LANES] = qm[:, h * LANES:(h + 1) * LANES].astype(q_ref.dtype)
            k_ref[h, :, :LANES] = kv[:, 2 * h * LANES:(2 * h + 1) * LANES].astype(k_ref.dtype)
            k_ref[h, :, LANES:] = (kr_lo if h % 2 == 0 else kr_hi).astype(k_ref.dtype)
            v_ref[h] = kv[:, (2 * h + 1) * LANES:(2 * h + 2) * LANES].astype(v_ref.dtype)

    def full(shape):
        return pl.BlockSpec(shape, lambda i: (0,) * len(shape))

    return pl.pallas_call(
        body, grid=(s // tr,),
        in_specs=[pl.BlockSpec((tr, MQL), lambda i: (i, OMQ // MQL)),
                  pl.BlockSpec((tr, MKVL), lambda i: (i, OMKV // MKVL)),
                  pl.BlockSpec((tr, LANES), lambda i: (i, OT // LANES)),
                  pl.BlockSpec((tr, LANES), lambda i: (i, 0)),
                  pl.BlockSpec((tr, LANES), lambda i: (i, 0)),
                  full((1, MQL)), full((1, MKVL)), full((MQL, MQW)), full((MKVL, MKVW))],
        out_specs=[pl.BlockSpec((MH, tr, 2 * LANES), lambda i: (0, i, 0)),
                   pl.BlockSpec((MH, tr, 2 * LANES), lambda i: (0, i, 0)),
                   pl.BlockSpec((MH, tr, LANES), lambda i: (0, i, 0))],
        out_shape=[jax.ShapeDtypeStruct((MH, s, 2 * LANES), MXU), jax.ShapeDtypeStruct((MH, s, 2 * LANES), MXU),
                   jax.ShapeDtypeStruct((MH, s, LANES), MXU)],
        name=name, compiler_params=_cp(("parallel",)))(proj, proj, proj, cos, sin, qg, kvg, w_uq, w_ukv)


def _mla_prep_bwd(proj, cos, sin, qg, kvg, w_uq, w_ukv, d_q, d_k, d_v, name):
    s = proj.shape[0]
    tr = _rows(s)

    def body(mq_ref, mkv_ref, cos_ref, sin_ref, qg_ref, kvg_ref, wuq_ref, wukv_ref, dq_ref, dk_ref, dv_ref,
             dmq_ref, dmkv_ref, dt_ref, dwuq_ref, dwukv_ref, dqg_ref, dkvg_ref):
        @pl.when(pl.program_id(0) == 0)
        def _():
            for r in (dwuq_ref, dwukv_ref, dqg_ref, dkvg_ref):
                r[...] = jnp.zeros_like(r)

        cosv, sinv = cos_ref[...], sin_ref[...]
        lane = lax.broadcasted_iota(jnp.int32, (tr, LANES), 1)
        lo = lane < MROPE

        def unrope(dv):
            return dv * cosv - _rot_half(dv * sinv)

        parts = [dq_ref[h, :, :LANES] for h in range(MH)]
        for p in range(MH // 2):
            parts.append(unrope(jnp.where(lo, dq_ref[2 * p, :, LANES:], dq_ref[2 * p + 1, :, LANES:])))
        d_qm = jnp.concatenate(parts, axis=1)
        mq, qgv = mq_ref[...], qg_ref[...]
        cq = _rms(mq, qgv)
        dwuq_ref[...] += _dot_tn(cq, d_qm)
        dmq, dqg = _rms_bwd(_dot_nt(d_qm, wuq_ref[...]), mq, qgv)
        dmq_ref[...] = dmq.astype(dmq_ref.dtype)
        dqg_ref[...] += dqg

        parts = []
        for h in range(MH):
            parts += [dk_ref[h, :, :LANES], dv_ref[h]]
        d_kv = jnp.concatenate(parts, axis=1)
        mkv, kvgv = mkv_ref[...], kvg_ref[...]
        ckv = _rms(mkv, kvgv)
        dwukv_ref[...] += _dot_tn(ckv, d_kv)
        dmkv, dkvg = _rms_bwd(_dot_nt(d_kv, wukv_ref[...]), mkv, kvgv)
        dmkv_ref[...] = dmkv.astype(dmkv_ref.dtype)
        dkvg_ref[...] += dkvg

        even = dk_ref[0, :, LANES:] + dk_ref[2, :, LANES:] + dk_ref[4, :, LANES:]
        odd = dk_ref[1, :, LANES:] + dk_ref[3, :, LANES:] + dk_ref[5, :, LANES:]
        d_kr = jnp.where(lo, even, 0.0) + pltpu.roll(jnp.where(lo, 0.0, odd), MROPE, 1)
        dt_ref[...] = jnp.where(lo, unrope(d_kr), 0.0)

    def full(shape):
        return pl.BlockSpec(shape, lambda i: (0,) * len(shape))

    return pl.pallas_call(
        body, grid=(s // tr,),
        in_specs=[pl.BlockSpec((tr, MQL), lambda i: (i, OMQ // MQL)),
                  pl.BlockSpec((tr, MKVL), lambda i: (i, OMKV // MKVL)),
                  pl.BlockSpec((tr, LANES), lambda i: (i, 0)),
                  pl.BlockSpec((tr, LANES), lambda i: (i, 0)),
                  full((1, MQL)), full((1, MKVL)), full((MQL, MQW)), full((MKVL, MKVW)),
                  pl.BlockSpec((MH, tr, 2 * LANES), lambda i: (0, i, 0)),
                  pl.BlockSpec((MH, tr, 2 * LANES), lambda i: (0, i, 0)),
                  pl.BlockSpec((MH, tr, LANES), lambda i: (0, i, 0))],
        out_specs=[pl.BlockSpec((tr, MQL), lambda i: (i, 0)), pl.BlockSpec((tr, MKVL), lambda i: (i, 0)),
                   pl.BlockSpec((tr, LANES), lambda i: (i, 0)),
                   full((MQL, MQW)), full((MKVL, MKVW)), full((1, MQL)), full((1, MKVL))],
        out_shape=[jax.ShapeDtypeStruct((s, MQL), MXU), jax.ShapeDtypeStruct((s, MKVL), MXU),
                   jax.ShapeDtypeStruct((s, LANES), F32),
                   jax.ShapeDtypeStruct((MQL, MQW), F32), jax.ShapeDtypeStruct((MKVL, MKVW), F32),
                   jax.ShapeDtypeStruct((1, MQL), F32), jax.ShapeDtypeStruct((1, MKVL), F32)],
        name=name, compiler_params=_cp(("arbitrary",)))(proj, proj, cos, sin, qg, kvg, w_uq, w_ukv, d_q, d_k, d_v)


ATT_SCALE = (MNOPE + MROPE) ** -0.5
ATT_SCALE_LOG2 = ATT_SCALE * 1.4426950408889634
ATT_TQ_FWD, ATT_TQ = 2048, 2048
ATT_SUB, ATT_SUB_BWD = 256, 256


def _attn_fwd(q, k, v, name):
    s = q.shape[1]
    tq = min(ATT_TQ_FWD, s)
    sub = min(ATT_SUB, tq)

    def body(q_ref, k_ref, v_ref, o_ref, lse_ref):
        for r0 in range(0, tq, sub):
            rows = slice(r0, r0 + sub)
            sc = _dot_nt(q_ref[0, rows, :], k_ref[0])
            m = jnp.max(sc, axis=-1, keepdims=True)
            p = jnp.exp2((sc - m) * ATT_SCALE_LOG2)
            l = jnp.sum(p, axis=-1, keepdims=True)
            o_ref[rows, :] = _dot(p, v_ref[0]) / l
            lse_ref[0, rows, :] = m * ATT_SCALE_LOG2 + jnp.log2(l)

    return pl.pallas_call(
        body, grid=(MH, s // tq),
        in_specs=[pl.BlockSpec((1, tq, 2 * LANES), lambda h, i: (h, i, 0)),
                  pl.BlockSpec((1, s, 2 * LANES), lambda h, i: (h, 0, 0)),
                  pl.BlockSpec((1, s, LANES), lambda h, i: (h, 0, 0))],
        out_specs=[pl.BlockSpec((tq, LANES), lambda h, i: (i, h)),
                   pl.BlockSpec((1, tq, 1), lambda h, i: (h, i, 0))],
        out_shape=[jax.ShapeDtypeStruct((s, MW), F32), jax.ShapeDtypeStruct((MH, s, 1), F32)],
        name=name, compiler_params=_cp(("parallel", "parallel")))(q, k, v)


def _attn_bwd(q, k, v, o, lse, d_o, name):
    s = q.shape[1]
    tq = min(ATT_TQ, s)
    sub = min(ATT_SUB_BWD, tq)

    def body(q_ref, k_ref, v_ref, o_ref, lse_ref, do_ref, dq_ref, dk_ref, dv_ref):
        @pl.when(pl.program_id(1) == 0)
        def _():
            dk_ref[...] = jnp.zeros_like(dk_ref)
            dv_ref[...] = jnp.zeros_like(dv_ref)

        kv = k_ref[0]
        for r0 in range(0, tq, sub):
            rows = slice(r0, r0 + sub)
            qv, do = q_ref[0, rows, :], do_ref[rows, :]
            p = jnp.exp2(_dot_nt(qv, kv) * ATT_SCALE_LOG2 - lse_ref[0, rows, :])
            delta = jnp.sum(do * o_ref[rows, :], axis=-1, keepdims=True)
            ds = p * (_dot_nt(do, v_ref[0]) - delta)
            dq_ref[0, rows, :] = _dot(ds, kv) * ATT_SCALE
            dk_ref[0] += _dot_tn(ds, qv) * ATT_SCALE
            dv_ref[0] += _dot_tn(p, do)

    return pl.pallas_call(
        body, grid=(MH, s // tq),
        in_specs=[pl.BlockSpec((1, tq, 2 * LANES), lambda h, i: (h, i, 0)),
                  pl.BlockSpec((1, s, 2 * LANES), lambda h, i: (h, 0, 0)),
                  pl.BlockSpec((1, s, LANES), lambda h, i: (h, 0, 0)),
                  pl.BlockSpec((tq, LANES), lambda h, i: (i, h)),
                  pl.BlockSpec((1, tq, 1), lambda h, i: (h, i, 0)),
                  pl.BlockSpec((tq, LANES), lambda h, i: (i, h))],
        out_specs=[pl.BlockSpec((1, tq, 2 * LANES), lambda h, i: (h, i, 0)),
                   pl.BlockSpec((1, s, 2 * LANES), lambda h, i: (h, 0, 0)),
                   pl.BlockSpec((1, s, LANES), lambda h, i: (h, 0, 0))],
        out_shape=[jax.ShapeDtypeStruct((MH, s, 2 * LANES), F32), jax.ShapeDtypeStruct((MH, s, 2 * LANES), F32),
                   jax.ShapeDtypeStruct((MH, s, LANES), F32)],
        name=name, compiler_params=_cp(("parallel", "arbitrary")))(q, k, v, o, lse, d_o)


def _merge_fwd(o_f, o_b, o_att, pre, proj, gng, mog, cog, name):
    s = proj.shape[0]
    tr = _rows_light(s)

    def body(of_ref, ob_ref, oa_ref, pre_ref, z_ref, gng_ref, mog_ref, cog_ref, y_ref):
        z = z_ref[...]
        sz = z * _sigmoid(z)
        osum = of_ref[...] + ob_ref[...]
        gg = gng_ref[...]
        for h in range(GH):
            sl = slice(h * GDV, (h + 1) * GDV)
            y_ref[:, sl] = (_rms(osum[:, sl], gg) * sz[:, sl]).astype(y_ref.dtype)
        y_ref[:, GW:GW + MW] = (_rms(oa_ref[...], mog_ref[...]) * sz[:, GW:GW + MW]).astype(y_ref.dtype)
        y_ref[:, GW + MW:] = (_rms(pre_ref[...], cog_ref[...]) * sz[:, GW + MW:]).astype(y_ref.dtype)

    def row(w):
        return pl.BlockSpec((tr, w), lambda i: (i, 0))

    def vec(w):
        return pl.BlockSpec((1, w), lambda i: (0, 0))

    return pl.pallas_call(
        body, grid=(s // tr,),
        in_specs=[row(GW), row(GW), row(MW), row(CONV_CH), row(D_MIX), vec(GDV), vec(MW), vec(CONV_CH)],
        out_specs=row(D_MIX), out_shape=jax.ShapeDtypeStruct((s, D_MIX), MXU),
        name=name, compiler_params=_cp(("parallel",)))(o_f, o_b, o_att, pre, proj, gng, mog, cog)


def _merge_bwd(d_y, o_f, o_b, o_att, pre, proj, gng, mog, cog, name):
    s = proj.shape[0]
    tr = _rows(s)

    def body(dy_ref, of_ref, ob_ref, oa_ref, pre_ref, z_ref, gng_ref, mog_ref, cog_ref,
             dz_ref, dos_ref, doa_ref, dpre_ref, dgng_ref, dmog_ref, dcog_ref):
        @pl.when(pl.program_id(0) == 0)
        def _():
            for r in (dgng_ref, dmog_ref, dcog_ref):
                r[...] = jnp.zeros_like(r)

        z, dy = z_ref[...], dy_ref[...]
        sg = _sigmoid(z)
        sz = z * sg
        dsz = sg * (1.0 + z * (1.0 - sg))
        dcat = dy * sz
        dyz = dy * dsz
        osum = of_ref[...] + ob_ref[...]
        gg = gng_ref[...]
        dgg = jnp.zeros_like(gg)
        for h in range(GH):
            sl = slice(h * GDV, (h + 1) * GDV)
            dz_ref[:, sl] = (dyz[:, sl] * _rms(osum[:, sl], gg)).astype(dz_ref.dtype)
            dx, dg = _rms_bwd(dcat[:, sl], osum[:, sl], gg)
            dos_ref[:, sl] = dx
            dgg += dg
        dgng_ref[...] += dgg
        sl = slice(GW, GW + MW)
        oa, mg = oa_ref[...], mog_ref[...]
        dz_ref[:, sl] = (dyz[:, sl] * _rms(oa, mg)).astype(dz_ref.dtype)
        dx, dg = _rms_bwd(dcat[:, sl], oa, mg)
        doa_ref[...] = dx
        dmog_ref[...] += dg
        sl = slice(GW + MW, D_MIX)
        pv, cg = pre_ref[...], cog_ref[...]
        dz_ref[:, sl] = (dyz[:, sl] * _rms(pv, cg)).astype(dz_ref.dtype)
        dx, dg = _rms_bwd(dcat[:, sl], pv, cg)
        dpre_ref[...] = dx
        dcog_ref[...] += dg

    def row(w):
        return pl.BlockSpec((tr, w), lambda i: (i, 0))

    def vec(w):
        return pl.BlockSpec((1, w), lambda i: (0, 0))

    def rs(w):
        return jax.ShapeDtypeStruct((s, w), F32)

    def vs(w):
        return jax.ShapeDtypeStruct((1, w), F32)

    return pl.pallas_call(
        body, grid=(s // tr,),
        in_specs=[row(D_MIX), row(GW), row(GW), row(MW), row(CONV_CH), row(D_MIX), vec(GDV), vec(MW), vec(CONV_CH)],
        out_specs=[row(D_MIX), row(GW), row(MW), row(CONV_CH), vec(GDV), vec(MW), vec(CONV_CH)],
        out_shape=[jax.ShapeDtypeStruct((s, D_MIX), MXU),
                   rs(GW), rs(MW), rs(CONV_CH), vs(GDV), vs(MW), vs(CONV_CH)],
        name=name, compiler_params=_cp(("arbitrary",)))(d_y, o_f, o_b, o_att, pre, proj, gng, mog, cog)


def _assemble_dproj(d_z, d_cb, d_cc, d_cx, d_mkv, dv_f, dv_b, dq_f, dq_b, dk_f, dk_b, d_mq, dt_m, dt_f, dt_b, name):
    s = d_z.shape[0]
    tr = _rows_light(s)

    def body(dz, dcb, dcc, dcx, dmkv, dvf, dvb, dqf, dqb, dkf, dkb, dmq, dtm, dtf, dtb, out):
        dt = out.dtype
        out[:, OZ:OZ + D_MIX] = dz[...].astype(dt)
        out[:, OCB:OCB + CONV_CH] = dcb[...].astype(dt)
        out[:, OCC:OCC + CONV_CH] = dcc[...].astype(dt)
        out[:, OCX:OCX + CONV_CH] = dcx[...].astype(dt)
        out[:, OMKV:OMKV + MKVL] = dmkv[...].astype(dt)
        out[:, OGV:OGV + GW] = (dvf[...] + dvb[...]).astype(dt)
        out[:, OGQ:OGQ + GQK] = (dqf[...] + dqb[...]).astype(dt)
        out[:, OGK:OGK + GQK] = (dkf[...] + dkb[...]).astype(dt)
        out[:, OMQ:OMQ + MQL] = dmq[...].astype(dt)
        out[:, OT:OT + LANES] = (dtm[...] + dtf[...] + dtb[...]).astype(dt)

    args = (d_z, d_cb, d_cc, d_cx, d_mkv, dv_f, dv_b, dq_f, dq_b, dk_f, dk_b, d_mq, dt_m, dt_f, dt_b)
    return pl.pallas_call(
        body, grid=(s // tr,),
        in_specs=[pl.BlockSpec((tr, a.shape[1]), lambda i: (i, 0)) for a in args],
        out_specs=pl.BlockSpec((tr, PW), lambda i: (i, 0)),
        out_shape=jax.ShapeDtypeStruct((s, PW), MXU), name=name, compiler_params=_cp(("parallel",)))(*args)


def _layer_fwd(x, mod, wt, cos, sin, tag, late=None, in_after=(), h=None):
    shift, scale, gate = mod
    if h is None:
        h = _norm_mod(x, wt["norm_g"], scale, shift, f"norm_mod_{tag}")
    (proj,) = _matmul(h, wt["w_in"], dims="nn", tm=2048, tn=256, tk=2048, out_dtypes=(F32,), name=f"in_proj_{tag}",
                      after=in_after)
    if late is not None:
        wt.update(late(proj))
    o_f, st_f = _gla_fwd(proj, wt["wg_pad_f"], wt["bg_f"], False, f"gla_fwd_f_{tag}")
    o_b, st_b = _gla_fwd(proj, wt["wg_pad_b"], wt["bg_b"], True, f"gla_fwd_b_{tag}")
    q, k, v = _mla_prep(proj, cos, sin, wt["q_norm_g"], wt["kv_norm_g"], wt["w_uq"], wt["w_ukv"], f"mla_prep_{tag}")
    o_att, lse = _attn_fwd(q, k, v, f"attn_fwd_{tag}")
    pre = _conv_fwd(proj, wt["conv_w"], f"conv_fwd_{tag}")
    y = _merge_fwd(o_f, o_b, o_att, pre, proj, wt["gla_norm_g"], wt["mla_out_g"], wt["conv_out_g"], f"merge_fwd_{tag}")
    x_new, u = _matmul(y, wt["w_out"], dims="nn", tm=2048, tn=256, tk=2048, out_dtypes=(F32, F32),
                       name=f"out_proj_{tag}", epilogue=lambda acc, xv, gv: (xv + gv * acc, acc),
                       extras=(x, gate), extra_kinds=("mn", "n"))
    saved = dict(x=x, h=h, proj=proj, o_f=o_f, o_b=o_b, st_f=st_f, st_b=st_b, q=q, k=k, v=v,
                 o_att=o_att, lse=lse, pre=pre, y=y, u=u)
    return x_new, saved


def _layer_bwd(d_out, sv, mod, wt, cos, sin, tag, ship=None, dx_first=None, ship_rest=None, started=None, below=None,
               first_after=()):
    shift, scale, gate = mod
    proj = sv["proj"]
    if started is None:
        d_u, d_gate = _gate_bwd(d_out, sv["u"], gate, f"gate_bwd_{tag}")
    else:
        d_u, d_gate = started
    (g_w_out,) = _matmul(sv["y"], d_u, dims="tn", tm=1024, tn=512, tk=2048, out_dtypes=(MXU,), name=f"out_proj_dw_{tag}",
                         after=first_after)
    (d_y,) = _matmul(d_u, wt["w_out"], dims="nt", tm=2048, tn=256, tk=2048, out_dtypes=(F32,), name=f"out_proj_dx_{tag}",
                     after=(g_w_out,))
    d_z, d_osum, d_oatt, d_pre, d_gng, d_mog, d_cog = _merge_bwd(
        d_y, sv["o_f"], sv["o_b"], sv["o_att"], sv["pre"], proj, wt["gla_norm_g"], wt["mla_out_g"], wt["conv_out_g"],
        f"merge_bwd_{tag}")
    d_cb, d_cc, d_cx, d_conv_w = _conv_bwd(proj, wt["conv_w"], d_pre, f"conv_bwd_{tag}")
    d_q, d_k, d_v = _attn_bwd(sv["q"], sv["k"], sv["v"], sv["o_att"], sv["lse"], d_oatt, f"attn_bwd_{tag}")
    d_mq, d_mkv, dt_m, g_w_uq, g_w_ukv, d_qg, d_kvg = _mla_prep_bwd(
        proj, cos, sin, wt["q_norm_g"], wt["kv_norm_g"], wt["w_uq"], wt["w_ukv"], d_q, d_k, d_v, f"mla_prep_bwd_{tag}")
    bg_f, bg_b = wt["bg_f"], wt["bg_b"]
    if ship_rest is not None:
        tok = ship_rest(dict(w_out=g_w_out, w_uq=g_w_uq, w_ukv=g_w_ukv))
        bg_f, bg_b = bg_f + tok, bg_b + tok
    dq_f, dk_f, dv_f, dt_f, d_wg_f, d_bg_f = _gla_bwd(proj, wt["wg_pad_f"], bg_f, sv["st_f"], d_osum, False,
                                                     f"gla_bwd_f_{tag}")
    dq_b, dk_b, dv_b, dt_b, d_wg_b, d_bg_b = _gla_bwd(proj, wt["wg_pad_b"], bg_b, sv["st_b"], d_osum, True,
                                                     f"gla_bwd_b_{tag}")
    d_proj = _assemble_dproj(d_z, d_cb, d_cc, d_cx, d_mkv, dv_f, dv_b, dq_f, dq_b, dk_f, dk_b, d_mq, dt_m, dt_f, dt_b,
                             f"assemble_dproj_{tag}")
    grads = dict(w_out=g_w_out, w_uq=g_w_uq, w_ukv=g_w_ukv,
                 wg_pad_f=d_wg_f, bg_f=d_bg_f, wg_pad_b=d_wg_b, bg_b=d_bg_b, gla_norm_g=d_gng,
                 q_norm_g=d_qg, kv_norm_g=d_kvg, mla_out_g=d_mog, conv_w=d_conv_w, conv_out_g=d_cog)

    def in_dw(after):
        (g_w_in,) = _matmul(sv["h"], d_proj, dims="tn", tm=2048, tn=256, tk=2048, out_dtypes=(MXU,),
                            name=f"in_proj_dw_{tag}", after=after)
        grads["w_in"] = g_w_in
        return dict(w_in=g_w_in, w_out=g_w_out, w_uq=g_w_uq, w_ukv=g_w_ukv)

    def in_dx(after):
        (d_h,) = _matmul(d_proj, wt["w_in"], dims="nt", tm=1024, tn=512, tk=PW, out_dtypes=(F32,),
                         name=f"in_proj_dx_{tag}", after=after)
        d_x, d_shift, d_scale, d_ng, *handed = _norm_mod_bwd(d_h, sv["x"], d_out, wt["norm_g"], scale,
                                                             f"norm_mod_bwd_{tag}", below)
        grads["norm_g"] = d_ng
        grads["started_below"] = tuple(handed)
        return d_x, (d_shift, d_scale, d_gate)

    if dx_first is None:
        big = in_dw(())
        d_x, d_mod = in_dx((big["w_in"],) if ship is None else ship(big))
    else:
        d_x, d_mod = in_dx(())
        big = in_dw(dx_first(d_x, d_mod, grads))
        ship(big)
    return d_x, d_mod, grads


IN_SEGS = ((3808, 5856), (2272, 3808), (1952, 2208), (768, 1536), (0, 768), (1568, 1952), (2208, 2272), (1536, 1568))


def _heads_apart(w):
    w3 = w.reshape(w.shape[:-1] + (MH, MNOPE + MROPE))
    return jnp.concatenate([w3[..., :MNOPE].reshape(w.shape[:-1] + (MH * MNOPE,)),
                            w3[..., MNOPE:].reshape(w.shape[:-1] + (MH * MROPE,))], axis=-1)


def _heads_together(g):
    nope = g[..., :MH * MNOPE].reshape(g.shape[:-1] + (MH, MNOPE))
    rope = g[..., MH * MNOPE:].reshape(g.shape[:-1] + (MH, MROPE))
    return jnp.concatenate([nope, rope], axis=-1).reshape(g.shape[:-1] + (MQW,))


def _perm_gathered(g, segs, width):
    per = g.shape[-1]
    parts, total = [], 0
    for a, b in segs:
        c = a
        while c < b:
            j = c // per
            hi = min(b, (j + 1) * per)
            parts.append(g[j, :, c - j * per:hi - j * per])
            c = hi
        total += b - a
    if width > total:
        parts.append(jnp.zeros((g.shape[1], width - total), g.dtype))
    return jnp.concatenate(parts, axis=1)


def _scatter_perm(gp, segs, per):
    offs, o = [], 0
    for a, b in segs:
        offs.append((a, b, o))
        o += b - a
    blocks = []
    for j in range(N_DEV):
        lo, hi = j * per, (j + 1) * per
        pieces = []
        for a, b, o in sorted(offs):
            s0, s1 = max(a, lo), min(b, hi)
            if s0 < s1:
                pieces.append(gp[:, o + s0 - a:o + s1 - a])
        blocks.append(jnp.concatenate(pieces, axis=1))
    return jnp.stack(blocks)


def _prep_layer_weights(w_in, w_out, w_uq, w_ukv, small):
    def vec(v):
        return v.reshape(1, -1).astype(F32)

    zeros = functools.partial(jnp.zeros, dtype=F32)
    wg_f, wg_b = small["gla_wg_f"].astype(F32), small["gla_wg_b"].astype(F32)
    wg_pad_f = jnp.concatenate([zeros((MROPE, GQK)), wg_f, zeros((LANES - MROPE - GRANK, GQK))], axis=0)
    wg_pad_b = jnp.concatenate([zeros((MROPE + GRANK, GQK)), wg_b, zeros((LANES - MROPE - 2 * GRANK, GQK))], axis=0)
    wt = dict(norm_g=vec(small["norm_g"]), wg_pad_f=wg_pad_f, wg_pad_b=wg_pad_b,
              bg_f=vec(small["gla_bg_f"]), bg_b=vec(small["gla_bg_b"]), gla_norm_g=vec(small["gla_norm_g"]),
              q_norm_g=vec(small["mla_q_norm_g"]), kv_norm_g=vec(small["mla_kv_norm_g"]),
              mla_out_g=vec(small["mla_out_g"]), conv_w=small["conv_w"].astype(F32),
              conv_out_g=vec(small["conv_out_g"]))
    for name, w in (("w_in", w_in), ("w_out", w_out), ("w_uq", w_uq), ("w_ukv", w_ukv)):
        if w is not None:
            wt[name] = w.astype(MXU)
    return wt


def _natural_small(gr):
    return dict(norm_g=gr["norm_g"][0],
                gla_wg_f=gr["wg_pad_f"][MROPE:MROPE + GRANK], gla_bg_f=gr["bg_f"][0],
                gla_wg_b=gr["wg_pad_b"][MROPE + GRANK:MROPE + 2 * GRANK], gla_bg_b=gr["bg_b"][0],
                gla_norm_g=gr["gla_norm_g"][0], mla_q_norm_g=gr["q_norm_g"][0], mla_kv_norm_g=gr["kv_norm_g"][0],
                mla_out_g=gr["mla_out_g"][0], conv_w=gr["conv_w"], conv_out_g=gr["conv_out_g"][0])


def _peer(r):
    ax, ay, ac = lax.axis_index("x"), lax.axis_index("y"), lax.axis_index("c")
    px = 1 - ax if r & 4 else ax
    py = 1 - ay if r & 2 else ay
    pc = 1 - ac if r & 1 else ac
    return (px, py, pc), 4 * px + 2 * py + pc


def _gather_small(arrs, name):
    n = len(arrs)

    def body(*refs):
        ins, outs = refs[:n], refs[n:2 * n]
        send_sems, recv_sems, loc_sems = refs[2 * n:]
        me = _peer(0)[1]

        def remote(a, r, slot):
            return pltpu.make_async_remote_copy(
                src_ref=ins[a], dst_ref=outs[a].at[slot], send_sem=send_sems.at[a, r - 1],
                recv_sem=recv_sems.at[a, r - 1], device_id=_peer(r)[0], device_id_type=MESH)

        locs = [pltpu.make_async_copy(ins[a], outs[a].at[me], loc_sems.at[a]) for a in range(n)]
        for cp in locs:
            cp.start()
        sends = [remote(a, r, me) for r in range(1, N_DEV) for a in range(n)]
        for cp in sends:
            cp.start()
        for r in range(1, N_DEV):
            for a in range(n):
                remote(a, r, _peer(r)[1]).wait_recv()
        for cp in sends:
            cp.wait_send()
        for cp in locs:
            cp.wait()

    spec = pl.BlockSpec(memory_space=pltpu.VMEM)
    return pl.pallas_call(
        body, in_specs=[spec] * n, out_specs=[spec] * n,
        out_shape=[jax.ShapeDtypeStruct((N_DEV,) + a.shape, a.dtype) for a in arrs],
        scratch_shapes=[pltpu.SemaphoreType.DMA((n, N_DEV - 1)), pltpu.SemaphoreType.DMA((n, N_DEV - 1)),
                        pltpu.SemaphoreType.DMA((n,))],
        name=name, compiler_params=pltpu.CompilerParams(vmem_limit_bytes=VMEM_LIMIT))(*arrs)


def _slot(rel_div):
    rel, div = rel_div
    idx = _peer(rel)[1]
    return idx if div == 1 else idx // div


AG_SPREAD = tuple((r, None, (0, 1), (r, 1)) for r in (1, 2, 4, 6))
AG_FORWARD = tuple((1, (k, 1), (k, 1), (1 ^ k, 1)) for k in (2, 4, 6))
RS_PAIR = tuple((1, (1 ^ k, 1), (1 ^ k, 2), (k, 2)) for k in (0, 2, 4, 6))
RS_CHIPS = tuple((r, (r, 2), (0, 2), (r, 2)) for r in (2, 4, 6))


def _plan_copies(plan, n, src_refs, land_refs, send_sems, recv_sems, arriving):
    out = []
    for i, (r, src, dst, recv) in enumerate(plan):
        peer = _peer(r)[0]
        for a in range(n):
            out.append(pltpu.make_async_remote_copy(
                src_ref=src_refs[a] if src is None else src_refs[a].at[_slot(src)],
                dst_ref=land_refs[a].at[_slot(recv if arriving else dst)],
                send_sem=send_sems.at[i * n + a], recv_sem=recv_sems.at[i * n + a],
                device_id=peer, device_id_type=MESH))
    return out


def _exchange_hbm(plan, srcs, lands, name, after=()):
    n = len(lands)
    fresh = isinstance(lands[0], jax.ShapeDtypeStruct)
    ins = ([] if srcs is None else list(srcs)) + ([] if fresh else list(lands))
    ns = 0 if srcs is None else n
    n_data = len(ins)
    ins = ins + list(after)

    def body(*refs):
        outs = refs[len(ins):len(ins) + n]
        send_sems, recv_sems = refs[-2:]
        src_refs = refs[:n] if srcs is not None else refs[ns:ns + n]
        sends = _plan_copies(plan, n, src_refs, outs, send_sems, recv_sems, False)
        for cp in sends:
            cp.start()
        for cp in _plan_copies(plan, n, src_refs, outs, send_sems, recv_sems, True):
            cp.wait_recv()
        for cp in sends:
            cp.wait_send()

    hbm = pl.BlockSpec(memory_space=pltpu.HBM)
    k = len(plan) * n
    return pl.pallas_call(
        body, name=name, in_specs=[hbm] * n_data + [pl.BlockSpec(memory_space=pl.ANY)] * len(after), out_specs=[hbm] * n,
        out_shape=[jax.ShapeDtypeStruct(a.shape, a.dtype) for a in lands],
        scratch_shapes=[pltpu.SemaphoreType.DMA((k,)), pltpu.SemaphoreType.DMA((k,))],
        input_output_aliases={} if fresh else {ns + i: i for i in range(n)},
        compiler_params=pltpu.CompilerParams(vmem_limit_bytes=VMEM_LIMIT))(*ins)


def _plan_start(plan, srcs, land_shapes, after, name):
    n = len(srcs)

    def body(*refs):
        src_refs, land_refs = refs[:n], refs[n:2 * n]
        send_sems, recv_sems = refs[2 * n + 1], refs[2 * n + 2]
        for cp in _plan_copies(plan, n, src_refs, land_refs, send_sems, recv_sems, False):
            cp.start()
        refs[-1][...] = jnp.zeros_like(refs[-1])

    hbm = pl.BlockSpec(memory_space=pltpu.HBM)
    sem = pl.BlockSpec(memory_space=pltpu.SEMAPHORE)
    k = len(plan) * n
    srcs = [pltpu.with_memory_space_constraint(a, pltpu.HBM) for a in srcs]
    lands = [pltpu.with_memory_space_constraint(lax.empty(shp, a.dtype), pltpu.HBM) for shp, a in zip(land_shapes, srcs)]
    res = pl.pallas_call(
        body, name=name,
        in_specs=[hbm] * (2 * n) + [pl.BlockSpec(memory_space=pl.ANY)],
        out_specs=[sem, sem] + [hbm] * (2 * n) + [pl.BlockSpec(memory_space=pltpu.VMEM)],
        out_shape=[pltpu.SemaphoreType.DMA((k,)), pltpu.SemaphoreType.DMA((k,))]
        + [pltpu.HBM(a.shape, a.dtype) for a in srcs] + [pltpu.HBM(shp, a.dtype) for shp, a in zip(land_shapes, srcs)]
        + [jax.ShapeDtypeStruct((8, LANES), F32)],
        input_output_aliases={i: 2 + i for i in range(2 * n)},
        compiler_params=pltpu.CompilerParams(has_side_effects=pltpu.SideEffectType.DATAFLOW_SIDE_EFFECTING),
    )(*srcs, *lands, after)
    return res[0], res[1], list(res[2:2 + n]), list(res[2 + n:2 + 2 * n]), res[-1]


def _plan_wait(plan, handle, after, name):
    send_sems, recv_sems, srcs, lands, _ = handle
    n = len(srcs)
    after = list(after)

    def body(*refs):
        src_refs, land_refs = refs[:n], refs[n:2 * n]
        ssem, rsem = refs[2 * n], refs[2 * n + 1]
        for cp in _plan_copies(plan, n, src_refs, land_refs, ssem, rsem, False):
            cp.wait_send()
        for cp in _plan_copies(plan, n, src_refs, land_refs, ssem, rsem, True):
            cp.wait_recv()

    hbm = pl.BlockSpec(memory_space=pltpu.HBM)
    sem = pl.BlockSpec(memory_space=pltpu.SEMAPHORE)
    res = pl.pallas_call(
        body, name=name,
        in_specs=[hbm] * (2 * n) + [sem, sem] + [pl.BlockSpec(memory_space=pl.ANY)] * len(after),
        out_specs=[hbm] * (2 * n),
        out_shape=[pltpu.HBM(a.shape, a.dtype) for a in srcs] + [pltpu.HBM(a.shape, a.dtype) for a in lands],
        input_output_aliases={i: i for i in range(2 * n)},
        compiler_params=pltpu.CompilerParams(has_side_effects=pltpu.SideEffectType.DATAFLOW_SIDE_EFFECTING),
    )(*srcs, *lands, send_sems, recv_sems, *after)
    return list(res[:n]), list(res[n:])


def _pair_sum(send, got, core, name):
    _, r, c = send.shape
    tr = 1024 if r % 1024 == 0 else r

    def body(core_ref, s_ref, g_ref, o_ref):
        o_ref[0] = (s_ref[0].astype(F32) + g_ref[0].astype(F32)).astype(o_ref.dtype)

    return pl.pallas_call(
        body, name=name,
        grid_spec=pltpu.PrefetchScalarGridSpec(
            num_scalar_prefetch=1, grid=(N_DEV // 2, r // tr),
            in_specs=[pl.BlockSpec((1, tr, c), lambda kc, i, core_ref: (2 * kc + core_ref[0], i, 0)),
                      pl.BlockSpec((1, tr, c), lambda kc, i, core_ref: (kc, i, 0))],
            out_specs=pl.BlockSpec((1, tr, c), lambda kc, i, core_ref: (kc, i, 0))),
        out_shape=jax.ShapeDtypeStruct((N_DEV // 2, r, c), send.dtype),
        compiler_params=_cp(("parallel", "parallel")))(core, send, got)


def _ada_mod(c_all, ada_w, ada_b_cols, name):
    nl, d, wc = ada_w.shape

    def body(c_ref, w_ref, b_ref, ca_ref, mod_ref):
        cv = c_ref[...]
        ca = cv * _sigmoid(cv)
        ca_ref[...] = ca
        mod_ref[0] = _dotf(ca, w_ref[0]) + b_ref[0]

    return pl.pallas_call(
        body, grid=(nl,),
        in_specs=[pl.BlockSpec((N_DEV, d), lambda l: (0, 0)), pl.BlockSpec((1, d, wc), lambda l: (l, 0, 0)),
                  pl.BlockSpec((1, 1, wc), lambda l: (l, 0, 0))],
        out_specs=[pl.BlockSpec((N_DEV, d), lambda l: (0, 0)), pl.BlockSpec((1, N_DEV, wc), lambda l: (l, 0, 0))],
        out_shape=[jax.ShapeDtypeStruct((N_DEV, d), F32), jax.ShapeDtypeStruct((nl, N_DEV, wc), F32)],
        name=name, compiler_params=_cp(("arbitrary",)))(c_all, ada_w, ada_b_cols)


def _adam(w, g, m, v):
    m2 = ADAM_B1 * m + (1.0 - ADAM_B1) * g
    v2 = ADAM_B2 * v + (1.0 - ADAM_B2) * (g * g)
    m_hat = m2 / (1.0 - ADAM_B1 ** ADAM_STEP)
    v_hat = v2 / (1.0 - ADAM_B2 ** ADAM_STEP)
    delta = -ADAM_LR * (m_hat / (jnp.sqrt(v_hat) + ADAM_EPS) + ADAM_WD * w)
    return delta, m2, v2


def _ada_grad_adam(c_act, d_mod, w, m, v, name):
    nl, d, wc = w.shape
    tk = min(1024, d)

    def body(c_ref, dm_ref, w_ref, m_ref, v_ref, g_ref, dl_ref, m2_ref, v2_ref):
        g = _dotf_tn(c_ref[...], dm_ref[0])
        delta, m2, v2 = _adam(w_ref[0], g, m_ref[0], v_ref[0])
        g_ref[0], dl_ref[0], m2_ref[0], v2_ref[0] = g, delta, m2, v2

    blk = pl.BlockSpec((1, tk, wc), lambda l, i: (l, i, 0))
    shp = jax.ShapeDtypeStruct(w.shape, F32)
    return pl.pallas_call(
        body, grid=(nl, d // tk),
        in_specs=[pl.BlockSpec((N_DEV, tk), lambda l, i: (0, i)), pl.BlockSpec((1, N_DEV, wc), lambda l, i: (l, 0, 0)),
                  blk, blk, blk],
        out_specs=[blk] * 4, out_shape=[shp] * 4, name=name,
        compiler_params=_cp(("parallel", "parallel")))(c_act, d_mod, w, m, v)


def _adam_big(recv, w, m, v, layer, prev, name, after=()):
    nl, r, c = w.shape
    tr = 512 if r % 512 == 0 else r
    nparts = recv.shape[0]

    def body(rc_ref, w_ref, m_ref, v_ref, *rest):
        g_ref, dl_ref, m2_ref, v2_ref = rest[-4:]
        g = rc_ref[0].astype(F32)
        for d in range(1, nparts):
            g = g + rc_ref[d].astype(F32)
        delta, m2, v2 = _adam(w_ref[0], g, m_ref[0], v_ref[0])
        g_ref[0], dl_ref[0], m2_ref[0], v2_ref[0] = g, delta, m2, v2

    blk = pl.BlockSpec((1, tr, c), lambda i: (layer, i, 0))
    shp = jax.ShapeDtypeStruct(w.shape, F32)
    prev = () if prev is None else tuple(prev)
    return pl.pallas_call(
        body, grid=(r // tr,),
        in_specs=[pl.BlockSpec((nparts, tr, c), lambda i: (0, i, 0)), blk, blk, blk]
        + [pl.BlockSpec(memory_space=pl.ANY)] * (len(prev) + len(after)),
        out_specs=[blk] * 4, out_shape=[shp] * 4, name=name,
        input_output_aliases={4 + j: j for j in range(len(prev))},
        compiler_params=_cp(("parallel",)))(recv, w, m, v, *prev, *after)


def _sum_devices(gathered, name):
    _, r, c = gathered.shape

    def body(g_ref, o_ref):
        acc = g_ref[0]
        for d in range(1, N_DEV):
            acc = acc + g_ref[d]
        o_ref[...] = acc

    spec = pl.BlockSpec(memory_space=pltpu.VMEM)
    return pl.pallas_call(body, in_specs=[spec], out_specs=spec, out_shape=jax.ShapeDtypeStruct((r, c), F32),
                          name=name, compiler_params=pltpu.CompilerParams(vmem_limit_bytes=VMEM_LIMIT))(gathered)


def _adam_small(ws, gs, ms, vs, name):
    n = len(ws)

    def body(*refs):
        for i in range(n):
            w_ref, g_ref, m_ref, v_ref = (refs[k * n + i] for k in range(4))
            dl_ref, m2_ref, v2_ref = (refs[(4 + k) * n + i] for k in range(3))
            dl_ref[...], m2_ref[...], v2_ref[...] = _adam(w_ref[...], g_ref[...], m_ref[...], v_ref[...])

    spec = pl.BlockSpec(memory_space=pltpu.VMEM)
    shapes = [jax.ShapeDtypeStruct(w.shape, F32) for w in ws]
    res = pl.pallas_call(body, in_specs=[spec] * (4 * n), out_specs=[spec] * (3 * n), out_shape=shapes * 3, name=name,
                         compiler_params=pltpu.CompilerParams(vmem_limit_bytes=VMEM_LIMIT))(*ws, *gs, *ms, *vs)
    return res[:n], res[n:2 * n], res[2 * n:]


def _pack(parts):
    flat = jnp.concatenate([p.reshape(-1).astype(F32) for p in parts])
    assert flat.shape[0] % LANES == 0, flat.shape
    return flat.reshape(-1, LANES)


def _unpack(packed, shapes):
    flat = packed.reshape(-1)
    out, off = [], 0
    for shp in shapes:
        size = 1
        for dim in shp:
            size *= dim
        out.append(flat[off:off + size].reshape(shp))
        off += size
    return out


def _gather_cols(g, per):
    g = jnp.moveaxis(g, 0, -2)
    return g.reshape(g.shape[:-2] + (N_DEV * per,))


def _scatter_cols(g, per):
    return jnp.moveaxis(g.reshape(g.shape[:-1] + (N_DEV, per)), -2, 0)


def _my_cols(full, me, per):
    return lax.dynamic_slice_in_dim(full, me * per, per, axis=full.ndim - 1)


def kernel(x, c, positions, ada_w, ada_b, norm_g, w_in, gla_wg_f, gla_bg_f, gla_wg_b, gla_bg_b, gla_norm_g, mla_q_norm_g, mla_kv_norm_g, mla_w_uq, mla_w_ukv, mla_out_g, conv_w, conv_out_g, w_out, final_g, loss_target, m_ada_w, m_ada_b, m_norm_g, m_w_in, m_gla_wg_f, m_gla_bg_f, m_gla_wg_b, m_gla_bg_b, m_gla_norm_g, m_mla_q_norm_g, m_mla_kv_norm_g, m_mla_w_uq, m_mla_w_ukv, m_mla_out_g, m_conv_w, m_conv_out_g, m_w_out, m_final_g, v_ada_w, v_ada_b, v_norm_g, v_w_in, v_gla_wg_f, v_gla_bg_f, v_gla_wg_b, v_gla_bg_b, v_gla_norm_g, v_mla_q_norm_g, v_mla_kv_norm_g, v_mla_w_uq, v_mla_w_ukv, v_mla_out_g, v_conv_w, v_conv_out_g, v_w_out, v_final_g):
    me = 4 * lax.axis_index("x") + 2 * lax.axis_index("y") + lax.axis_index("c")
    nl = ada_w.shape[0]
    s, d = x.shape[1], x.shape[2]
    ada_cols = ada_w.shape[2]
    wgc, cwc = gla_wg_f.shape[2], conv_w.shape[2]

    (g0,) = _gather_small([_pack([c, gla_wg_f, gla_wg_b, conv_w])], "gather_small_in")
    g0 = g0.reshape(N_DEV, -1)
    o1, o2, o3 = d, d + gla_wg_f.size, d + 2 * gla_wg_f.size
    c_all = g0[:, :o1]
    wgf_full = _gather_cols(g0[:, o1:o2].reshape((N_DEV,) + gla_wg_f.shape), wgc)
    wgb_full = _gather_cols(g0[:, o2:o3].reshape((N_DEV,) + gla_wg_b.shape), wgc)
    convw_full = _gather_cols(g0[:, o3:].reshape((N_DEV,) + conv_w.shape), cwc)

    ada_b_cols = _my_cols(ada_b, me, ada_cols).reshape(nl, 1, ada_cols)
    c_act, mod_cols = _ada_mod(c_all, ada_w, ada_b_cols, "ada_mod")
    (g1,) = _gather_small([_pack([mod_cols])], "gather_mod")
    mod_all = g1.reshape(N_DEV, nl, N_DEV, ada_cols)
    mod_mine = _gather_cols(lax.dynamic_index_in_dim(mod_all, me, axis=2, keepdims=False), ada_cols)

    inv_freq = ROPE_THETA ** (-jnp.arange(0, MROPE, 2, dtype=F32) / MROPE)
    ang = positions[0].astype(F32)[:, None] * inv_freq
    cos, sin = jnp.tile(jnp.cos(ang), (1, LANES * 2 // MROPE)), jnp.tile(jnp.sin(ang), (1, LANES * 2 // MROPE))

    big = [w_in, w_out, mla_w_uq, mla_w_ukv]
    big_names = ["w_in", "w_out", "mla_w_uq", "mla_w_ukv"]

    def local_blocks(l):
        return [w[l].astype(MXU) for w in big]

    def put_own(lands, own):
        return [lax.dynamic_update_index_in_dim(ld, o, me, 0) for ld, o in zip(lands, own)]

    def layer_weights(l, gw_in=None, gw_out=None, gw_uq=None, gw_ukv=None):
        small = dict(norm_g=norm_g[l], gla_wg_f=wgf_full[l], gla_bg_f=gla_bg_f[l], gla_wg_b=wgb_full[l],
                     gla_bg_b=gla_bg_b[l], gla_norm_g=gla_norm_g[l], mla_q_norm_g=mla_q_norm_g[l],
                     mla_kv_norm_g=mla_kv_norm_g[l], mla_out_g=mla_out_g[l], conv_w=convw_full[l],
                     conv_out_g=conv_out_g[l])
        return _prep_layer_weights(
            None if gw_in is None else _perm_gathered(gw_in, IN_SEGS, PW),
            None if gw_out is None else gw_out.reshape((-1,) + gw_out.shape[2:]),
            None if gw_uq is None else _heads_apart(_gather_cols(gw_uq, mla_w_uq.shape[2])),
            None if gw_ukv is None else _gather_cols(gw_ukv, mla_w_ukv.shape[2]), small)

    def land_shapes(blocks, slots):
        return [jax.ShapeDtypeStruct((slots,) + b.shape, b.dtype) for b in blocks]

    def slots_of(blocks):
        return [(N_DEV,) + b.shape for b in blocks]

    def forwarded(lands, blocks, tag):
        return put_own(_exchange_hbm(AG_FORWARD, None, lands, f"gather_{tag}_forward"), blocks)

    first = local_blocks(0)
    w_in_start = _plan_start(AG_SPREAD, first[:1], slots_of(first[:1]), mod_mine, "gather_w_in_l0_start")
    adam_w_in = [a + w_in_start[-1][0, 0] for a in (w_in, m_w_in, v_w_in)]
    shift0, scale0 = (mod_mine[0, i * d:(i + 1) * d].reshape(1, d) for i in range(2))
    h_first = _norm_mod(x[0], norm_g[0].reshape(1, d), scale0 + w_in_start[-1][0, 0], shift0, "norm_mod_l0")
    (gw_in,) = forwarded(*reversed(_plan_wait(AG_SPREAD, w_in_start, adam_w_in + [h_first], "gather_w_in_l0_wait")),
                         "w_in_l0")
    rest = _plan_start(AG_SPREAD, first[1:], slots_of(first[1:]), gw_in, "gather_rest_l0_start")
    h = x[0]
    saved, layers, mods = [], [], []
    pending = {}
    for l in range(nl):
        shift, scale, gate = (mod_mine[l, i * d:(i + 1) * d].reshape(1, d) for i in range(3))
        nxt = local_blocks(l + 1) if l + 1 < nl else None

        def start_next(after, wt_late, l=l, nxt=nxt):
            if nxt is not None:
                pending[l + 1] = _plan_start(AG_SPREAD, nxt, slots_of(nxt), after, f"gather_weights_l{l + 1}_start")
                wt_late["q_norm_g"] = layers[l]["q_norm_g"] + pending[l + 1][-1][0, 0]
            return wt_late

        if l == 0:
            in_after = (rest[-1],)
            layers.append(layer_weights(0, gw_in))

            def late(proj):
                got = forwarded(*reversed(_plan_wait(AG_SPREAD, rest, [proj], "gather_rest_l0_wait")), "rest_l0")
                full = layer_weights(0, None, *got)
                return start_next(got[0], {k: full[k] for k in ("w_out", "w_uq", "w_ukv")})
        else:
            got = forwarded(*reversed(_plan_wait(AG_SPREAD, pending.pop(l), [h], f"gather_weights_l{l}_wait")), f"weights_l{l}")
            layers.append(layer_weights(l, *got))
            in_after = ()

            def late(proj):
                return start_next(proj, {})
        mods.append((shift, scale, gate))
        h, sv = _layer_fwd(h, mods[l], layers[l], cos, sin, f"l{l}", late, in_after, h_first if l == 0 else None)
        saved.append(sv)
        blocks = nxt
    loss_part, d_h, d_final_g, *started = _final_loss(h, final_g.reshape(1, d), loss_target[0], "final_loss",
                                                      (saved[-1]["u"], mods[-1][2]))
    loss = lax.psum(loss_part[0, 0], ("x", "y", "c"))
    first_after = (loss.reshape(1, 1),)

    send_of = dict(w_in=lambda g: _scatter_perm(g, IN_SEGS, w_in.shape[2]),
                   w_out=lambda g: g.reshape((N_DEV,) + w_out.shape[1:]),
                   w_uq=lambda g: _scatter_cols(_heads_together(g), mla_w_uq.shape[2]),
                   w_ukv=lambda g: _scatter_cols(g, mla_w_ukv.shape[2]))

    def grad_sends(gr):
        return [send_of[k](g).astype(MXU) for k, g in gr.items()]

    my_chip = me // 2
    my_core = (me % 2).astype(jnp.int32).reshape(1)

    def chip_sums(gr, tag):
        sends = grad_sends(gr)
        got = _exchange_hbm(RS_PAIR, sends, land_shapes([sd[0] for sd in sends], N_DEV // 2), f"scatter_grads_{tag}_pair")
        return [_pair_sum(sd, gt, my_core, f"pair_sum_{k}_{tag}") for sd, gt, k in zip(sends, got, gr)]

    def with_own_chip(lands, sums):
        return [lax.dynamic_update_index_in_dim(ld, lax.dynamic_index_in_dim(sm, my_chip, axis=0, keepdims=False),
                                                my_chip, 0) for ld, sm in zip(lands, sums)]

    small_names = ["norm_g", "gla_wg_f", "gla_bg_f", "gla_wg_b", "gla_bg_b", "gla_norm_g", "mla_q_norm_g",
                   "mla_kv_norm_g", "mla_out_g", "conv_w", "conv_out_g"]
    d_mods, grads, recv = [None] * nl, [None] * nl, [None] * nl
    flight = {}
    small = {}

    def gather_small(d_x, d_mod0, gr0):
        d_mods[0], grads[0] = d_mod0, _natural_small(gr0)
        d_mod_mine = jnp.stack([jnp.concatenate(d_mods[l], axis=-1)[0] for l in range(nl)])
        parts = [d_mod_mine] + [jnp.stack([grads[l][n] for l in range(nl)]) for n in small_names] + [d_final_g]
        (g2,) = _gather_small([_pack(parts)], "gather_small_grads")
        small["d_mod_all"] = g2.reshape(N_DEV, -1)[:, :d_mod_mine.size].reshape(N_DEV, nl, 3 * d)
        small["summed"] = dict(zip(["ada_b"] + small_names + ["final_g"],
                                   _unpack(_sum_devices(g2, "sum_small_grads"), [p.shape for p in parts])))
        return (g2,)

    pairs = {}
    def end_flight(key, after, name):
        sm, lands = _plan_wait(RS_CHIPS, flight.pop(key)[0], after, name)
        return with_own_chip(lands, sm)

    for l in reversed(range(nl)):
        def ship(big_grads, l=l):
            if l > 0:
                sends = grad_sends(big_grads)
                pairs[l] = (_plan_start(RS_PAIR, sends, [(N_DEV // 2,) + sd.shape[1:] for sd in sends],
                                        big_grads["w_in"], f"scatter_grads_l{l}_pair_start"), sends)
                return (pairs[l][0][-1],)
            sm = chip_sums(dict(w_in=big_grads["w_in"]), f"l{l}")
            flight[l] = (_plan_start(RS_CHIPS, sm, [a.shape for a in sm], big_grads["w_in"], f"scatter_grads_l{l}_start"),
                         sm)
            return (flight[l][0][-1],)

        def ship_rest(rest_grads, l=l):
            if l + 1 in flight:
                recv[l + 1] = end_flight(l + 1, list(rest_grads.values()), f"scatter_grads_l{l + 1}_wait")
            sm = chip_sums(rest_grads, f"l{l}_rest")
            flight["rest"] = (_plan_start(RS_CHIPS, sm, [a.shape for a in sm], rest_grads["w_out"],
                                          f"scatter_grads_l{l}_rest_start"), sm)
            return flight["rest"][0][-1][0, 0]

        if l + 1 in flight:
            first_after = (flight[l + 1][0][-1],)
        if l > 0:
            d_h, d_mods[l], gr = _layer_bwd(d_h, saved[l], mods[l], layers[l], cos, sin, f"l{l}", ship,
                                            started=started, below=(saved[l - 1]["u"], mods[l - 1][2]),
                                            first_after=first_after)
            started = gr["started_below"]
            grads[l] = _natural_small(gr)
            sends, got = _plan_wait(RS_PAIR, pairs.pop(l)[0], [d_h], f"scatter_grads_l{l}_pair_wait")
            sm = [_pair_sum(sd, gt, my_core, f"pair_sum_{n}_l{l}") for sd, gt, n in zip(sends, got, big_names)]
            flight[l] = (_plan_start(RS_CHIPS, sm, [a.shape for a in sm], d_h, f"scatter_grads_l{l}_start"), sm)
        else:
            d_h, _, _ = _layer_bwd(d_h, saved[l], mods[l], layers[l], cos, sin, f"l{l}", ship, gather_small,
                                   ship_rest, started=started, first_after=first_after)
    pending = flight[0][0]
    grad_x = d_h[None]
    summed = small["summed"]
    summed["gla_wg_f"] = _my_cols(summed["gla_wg_f"], me, wgc)
    summed["gla_wg_b"] = _my_cols(summed["gla_wg_b"], me, wgc)
    summed["conv_w"] = _my_cols(summed["conv_w"], me, cwc)

    d_mod_cols = jnp.moveaxis(_my_cols(small["d_mod_all"], me, ada_cols), 0, 1) + pending[-1][0, 0]
    out = {}
    out["ada_w"] = _ada_grad_adam(c_act, d_mod_cols, ada_w, m_ada_w, v_ada_w, "ada_grad_adam")

    given = dict(ada_b=(ada_b, m_ada_b, v_ada_b), norm_g=(norm_g, m_norm_g, v_norm_g),
                 gla_wg_f=(gla_wg_f, m_gla_wg_f, v_gla_wg_f), gla_bg_f=(gla_bg_f, m_gla_bg_f, v_gla_bg_f),
                 gla_wg_b=(gla_wg_b, m_gla_wg_b, v_gla_wg_b), gla_bg_b=(gla_bg_b, m_gla_bg_b, v_gla_bg_b),
                 gla_norm_g=(gla_norm_g, m_gla_norm_g, v_gla_norm_g),
                 mla_q_norm_g=(mla_q_norm_g, m_mla_q_norm_g, v_mla_q_norm_g),
                 mla_kv_norm_g=(mla_kv_norm_g, m_mla_kv_norm_g, v_mla_kv_norm_g),
                 mla_out_g=(mla_out_g, m_mla_out_g, v_mla_out_g), conv_w=(conv_w, m_conv_w, v_conv_w),
                 conv_out_g=(conv_out_g, m_conv_out_g, v_conv_out_g), final_g=(final_g, m_final_g, v_final_g))
    names = list(given)

    def two_d(a):
        return a.reshape(1, -1) if a.ndim == 1 else a

    g_nat = [summed[n].reshape(given[n][0].shape) for n in names]
    res = _adam_small([two_d(given[n][0]) for n in names], [two_d(g) for g in g_nat],
                      [two_d(given[n][1]) for n in names], [two_d(given[n][2]) for n in names], "adam_small")
    for i, n in enumerate(names):
        out[n] = (g_nat[i],) + tuple(r[i].reshape(given[n][0].shape) for r in res)

    state = dict(w_in=adam_w_in, w_out=(w_out, m_w_out, v_w_out), mla_w_uq=(mla_w_uq, m_mla_w_uq, v_mla_w_uq),
                 mla_w_ukv=(mla_w_ukv, m_mla_w_ukv, v_mla_w_ukv))
    done = [out["ada_w"][0], res[0][0]]
    def adam_layer(l, names, partials):
        for n, rc in zip(names, partials):
            out[n] = _adam_big(rc, *state[n], l, out.get(n), f"adam_{n}_l{l}",
                               (pending[-1],))
        return [out[n][0] for n in names]

    for l in reversed(range(nl)):
        if l > 0:
            done = done + adam_layer(l, big_names, recv[l])
        else:
            done = done + adam_layer(0, big_names[1:], end_flight("rest", done, "scatter_grads_l0_rest_wait"))
            adam_layer(0, big_names[:1], end_flight(0, done, "scatter_grads_l0_wait"))

    order = ["ada_w", "ada_b", "norm_g", "w_in", "gla_wg_f", "gla_bg_f", "gla_wg_b", "gla_bg_b", "gla_norm_g",
             "mla_q_norm_g", "mla_kv_norm_g", "mla_w_uq", "mla_w_ukv", "mla_out_g", "conv_w", "conv_out_g", "w_out",
             "final_g"]
    return (loss, grad_x, *[out[n][0] for n in order], *[out[n][1] for n in order], *[out[n][2] for n in order],
            *[out[n][3] for n in order])
```

```python
import functools

import jax
import jax.numpy as jnp
from jax import lax
from jax.experimental import pallas as pl
from jax.experimental.pallas import tpu as pltpu

F32 = jnp.float32
MXU = jnp.bfloat16
HI = lax.Precision.HIGHEST
N_DEV = 8
MESH = pl.DeviceIdType.MESH

D_MIX = 2048
GH, GDK, GDV = 6, 64, 128
GW = GH * GDV
GQK = GH * GDK
GRANK = 16
GTEMP = 16.0
CHUNK = 64
MH, MQL, MKVL, MNOPE, MROPE, MDV = 6, 384, 256, 128, 64, 128
MW = MH * MDV
MQW = MH * (MNOPE + MROPE)
MKVW = MH * (MNOPE + MDV)
CONV_CH = 512
ROPE_THETA = 10000.0
EPS = 1e-6
IN_DIM = 5856
OZ, OCB, OCC, OCX, OMKV, OGV, OGQ, OGK, OMQ, OT = 0, 2048, 2560, 3072, 3584, 3840, 4608, 4992, 5376, 5760
PW = 5888
LANES = 128
V7X_VMEM_BYTES = 64 * 1024 * 1024
VMEM_LIMIT = V7X_VMEM_BYTES * 7 // 8

ADAM_LR, ADAM_B1, ADAM_B2, ADAM_EPS, ADAM_WD, ADAM_STEP = 0.001, 0.9, 0.999, 1e-08, 0.01, 10


def _cp(sem=None):
    return pltpu.CompilerParams(dimension_semantics=sem, vmem_limit_bytes=VMEM_LIMIT)


def _dot(a, b):
    return jnp.dot(a.astype(MXU), b.astype(MXU), preferred_element_type=F32)


def _dot_nt(a, b):
    return lax.dot_general(a.astype(MXU), b.astype(MXU), (((1,), (1,)), ((), ())), preferred_element_type=F32)


def _dot_tn(a, b):
    return lax.dot_general(a.astype(MXU), b.astype(MXU), (((0,), (0,)), ((), ())), preferred_element_type=F32)


def _dotf(a, b):
    return jnp.dot(a, b, precision=HI, preferred_element_type=F32)


def _dotf_tn(a, b):
    return lax.dot_general(a, b, (((0,), (0,)), ((), ())), precision=HI, preferred_element_type=F32)


def _split3(x):
    hi = x.astype(jnp.bfloat16)
    r1 = x - hi.astype(F32)
    mid = r1.astype(jnp.bfloat16)
    lo = (r1 - mid.astype(F32)).astype(jnp.bfloat16)
    return hi, mid, lo


def _cum_dot(cum, x, transpose=False):
    dn = (((0,), (0,)), ((), ())) if transpose else (((1,), (0,)), ((), ()))
    cb = cum.astype(jnp.bfloat16)
    parts = [lax.dot_general(cb, p, dn, preferred_element_type=F32) for p in _split3(x)]
    return parts[0] + parts[1] + parts[2]


def _rows(s):
    return min(256, s)


def _rows_light(s):
    return min(512, s)


def _rms(x, g):
    r = lax.rsqrt(jnp.mean(x * x, axis=-1, keepdims=True) + EPS)
    return x * r * g


def _rms_bwd(dy, x, g):
    r = lax.rsqrt(jnp.mean(x * x, axis=-1, keepdims=True) + EPS)
    xh = x * r
    dxh = dy * g
    dg = jnp.sum(dy * xh, axis=0, keepdims=True)
    dx = r * (dxh - xh * jnp.mean(dxh * xh, axis=-1, keepdims=True))
    return dx, dg


def _sigmoid(z):
    return jax.nn.sigmoid(z)


def _matmul(a, b, *, dims, tm, tn, tk, out_dtypes, name, epilogue=None, extras=(), extra_kinds=(), after=()):
    if dims == "nn":
        (m, k), n, mul = a.shape, b.shape[1], _dot
    elif dims == "nt":
        (m, k), n, mul = a.shape, b.shape[0], _dot_nt
    else:
        (k, m), n, mul = a.shape, b.shape[1], _dot_tn
    tm, tn, tk = min(tm, m), min(tn, n), min(tk, k)
    assert m % tm == 0 and n % tn == 0 and k % tk == 0, (m, n, k, tm, tn, tk)
    if dims == "nn":
        a_spec = pl.BlockSpec((tm, tk), lambda i, j, kk: (i, kk))
        b_spec = pl.BlockSpec((tk, tn), lambda i, j, kk: (kk, j))
    elif dims == "nt":
        a_spec = pl.BlockSpec((tm, tk), lambda i, j, kk: (i, kk))
        b_spec = pl.BlockSpec((tn, tk), lambda i, j, kk: (j, kk))
    else:
        a_spec = pl.BlockSpec((tk, tm), lambda i, j, kk: (kk, i))
        b_spec = pl.BlockSpec((tk, tn), lambda i, j, kk: (kk, j))
    nk = k // tk
    n_extra = len(extras)
    n_out = len(out_dtypes)
    n_after = len(after)
    extra_specs = []
    for kind in extra_kinds:
        if kind == "mn":
            extra_specs.append(pl.BlockSpec((tm, tn), lambda i, j, kk: (i, j)))
        else:
            extra_specs.append(pl.BlockSpec((1, tn), lambda i, j, kk: (0, j)))

    def finish(res, ex, outs):
        vals = (res,) if epilogue is None else epilogue(res, *[e[...] for e in ex])
        for o, v in zip(outs, vals):
            o[...] = v.astype(o.dtype)

    def body(*refs):
        a_ref, b_ref = refs[0], refs[1]
        ex = refs[2:2 + n_extra]
        outs = refs[2 + n_extra + n_after:2 + n_extra + n_after + n_out]
        if nk == 1:
            finish(mul(a_ref[...], b_ref[...]), ex, outs)
            return
        acc = refs[-1]
        kk = pl.program_id(2)

        @pl.when(kk == 0)
        def _():
            acc[...] = jnp.zeros_like(acc)

        acc[...] += mul(a_ref[...], b_ref[...])

        @pl.when(kk == nk - 1)
        def _():
            finish(acc[...], ex, outs)

    out_spec = pl.BlockSpec((tm, tn), lambda i, j, kk: (i, j))
    res = pl.pallas_call(
        body, grid=(m // tm, n // tn, nk),
        in_specs=[a_spec, b_spec] + extra_specs + [pl.BlockSpec(memory_space=pl.ANY)] * n_after,
        out_specs=[out_spec] * n_out,
        out_shape=[jax.ShapeDtypeStruct((m, n), dt) for dt in out_dtypes],
        scratch_shapes=[] if nk == 1 else [pltpu.VMEM((tm, tn), F32)],
        name=name, compiler_params=_cp(("parallel", "parallel", "arbitrary")),
    )(a, b, *extras, *after)
    return res


def _norm_mod(x, g, scale, shift, name):
    s, d = x.shape
    tr = _rows_light(s)

    def body(x_ref, g_ref, sc_ref, sh_ref, h_ref):
        h = _rms(x_ref[...], g_ref[...]) * (1.0 + sc_ref[...]) + sh_ref[...]
        h_ref[...] = h.astype(h_ref.dtype)

    row = pl.BlockSpec((tr, d), lambda i: (i, 0))
    vec = pl.BlockSpec((1, d), lambda i: (0, 0))
    return pl.pallas_call(body, grid=(s // tr,), in_specs=[row, vec, vec, vec], out_specs=row,
                          out_shape=jax.ShapeDtypeStruct((s, d), MXU), name=name,
                          compiler_params=_cp(("parallel",)))(x, g, scale, shift)


def _gate_tail(dx, below):
    u_ref, gt_ref, du_ref, dgt_ref = below
    du_ref[...] = (dx * gt_ref[...]).astype(du_ref.dtype)
    dgt_ref[...] += jnp.sum(dx * u_ref[...], axis=0, keepdims=True)


def _norm_mod_bwd(d_h, x, d_out, g, scale, name, below=None):
    s, d = x.shape
    tr = _rows(s)

    def body(dh_ref, x_ref, do_ref, g_ref, sc_ref, *rest):
        dx_ref, dsh_ref, dsc_ref, dg_ref = rest[-6:-2] if below is not None else rest
        i = pl.program_id(0)

        @pl.when(i == 0)
        def _():
            dsh_ref[...] = jnp.zeros_like(dsh_ref)
            dsc_ref[...] = jnp.zeros_like(dsc_ref)
            dg_ref[...] = jnp.zeros_like(dg_ref)
            if below is not None:
                rest[-1][...] = jnp.zeros_like(rest[-1])

        dh = dh_ref[...]
        xv = x_ref[...]
        gv = g_ref[...]
        r = lax.rsqrt(jnp.mean(xv * xv, axis=-1, keepdims=True) + EPS)
        xh = xv * r
        dsh_ref[...] += jnp.sum(dh, axis=0, keepdims=True)
        dsc_ref[...] += jnp.sum(dh * (xh * gv), axis=0, keepdims=True)
        dhn = dh * (1.0 + sc_ref[...])
        dg_ref[...] += jnp.sum(dhn * xh, axis=0, keepdims=True)
        dxh = dhn * gv
        dx = do_ref[...] + r * (dxh - xh * jnp.mean(dxh * xh, axis=-1, keepdims=True))
        dx_ref[...] = dx
        if below is not None:
            _gate_tail(dx, (rest[0], rest[1], rest[-2], rest[-1]))

    row = pl.BlockSpec((tr, d), lambda i: (i, 0))
    vec = pl.BlockSpec((1, d), lambda i: (0, 0))
    vshape = jax.ShapeDtypeStruct((1, d), F32)
    extra = () if below is None else tuple(below)
    return pl.pallas_call(body, grid=(s // tr,), in_specs=[row, row, row, vec, vec] + [row, vec][:len(extra)],
                          out_specs=[row, vec, vec, vec] + [row, vec][:len(extra)],
                          out_shape=[jax.ShapeDtypeStruct((s, d), F32), vshape, vshape, vshape]
                          + [jax.ShapeDtypeStruct((s, d), MXU), vshape][:len(extra)],
                          name=name, compiler_params=_cp(("arbitrary",)))(d_h, x, d_out, g, scale, *extra)


def _gate_bwd(d_out, u, gate, name):
    s, d = d_out.shape
    tr = _rows_light(s)

    def body(do_ref, u_ref, gt_ref, du_ref, dgt_ref):
        @pl.when(pl.program_id(0) == 0)
        def _():
            dgt_ref[...] = jnp.zeros_like(dgt_ref)

        do = do_ref[...]
        du_ref[...] = (do * gt_ref[...]).astype(du_ref.dtype)
        dgt_ref[...] += jnp.sum(do * u_ref[...], axis=0, keepdims=True)

    row = pl.BlockSpec((tr, d), lambda i: (i, 0))
    vec = pl.BlockSpec((1, d), lambda i: (0, 0))
    return pl.pallas_call(body, grid=(s // tr,), in_specs=[row, row, vec], out_specs=[row, vec],
                          out_shape=[jax.ShapeDtypeStruct((s, d), MXU), jax.ShapeDtypeStruct((1, d), F32)],
                          name=name, compiler_params=_cp(("arbitrary",)))(d_out, u, gate)


def _final_loss(x, g, target, name, below=None):
    s, d = x.shape
    tr = _rows(s) if below is not None else _rows_light(s)

    def body(x_ref, g_ref, t_ref, *rest):
        loss_ref, dx_ref, dg_ref = rest[-5:-2] if below is not None else rest

        @pl.when(pl.program_id(0) == 0)
        def _():
            loss_ref[...] = jnp.zeros_like(loss_ref)
            dg_ref[...] = jnp.zeros_like(dg_ref)
            if below is not None:
                rest[-1][...] = jnp.zeros_like(rest[-1])

        xv = x_ref[...]
        gv = g_ref[...]
        diff = _rms(xv, gv) - t_ref[...]
        part = 0.5 * jnp.sum(jnp.sum(diff * diff, axis=-1, keepdims=True) / d, axis=0, keepdims=True)
        loss_ref[...] += jnp.broadcast_to(part, loss_ref.shape)
        dx, dg = _rms_bwd(diff / d, xv, gv)
        dx_ref[...] = dx
        dg_ref[...] += dg
        if below is not None:
            _gate_tail(dx, (rest[0], rest[1], rest[-2], rest[-1]))

    row = pl.BlockSpec((tr, d), lambda i: (i, 0))
    vec = pl.BlockSpec((1, d), lambda i: (0, 0))
    lvec = pl.BlockSpec((1, LANES), lambda i: (0, 0))
    vshape = jax.ShapeDtypeStruct((1, d), F32)
    extra = () if below is None else tuple(below)
    return pl.pallas_call(body, grid=(s // tr,), in_specs=[row, vec, row] + [row, vec][:len(extra)],
                          out_specs=[lvec, row, vec] + [row, vec][:len(extra)],
                          out_shape=[jax.ShapeDtypeStruct((1, LANES), F32), jax.ShapeDtypeStruct((s, d), F32), vshape]
                          + [jax.ShapeDtypeStruct((s, d), MXU), vshape][:len(extra)],
                          name=name, compiler_params=_cp(("arbitrary",)))(x, g, target, *extra)


def _shift_rows(u, s, down):
    ri = lax.broadcasted_iota(jnp.int32, u.shape, 0)
    if down:
        return jnp.where(ri == 0, 0.0, pltpu.roll(u, 1, 0))
    return jnp.where(ri == s - 1, 0.0, pltpu.roll(u, s - 1, 0))


def _conv_fwd(proj, conv_w, name):
    s = proj.shape[0]
    nt = CONV_CH // LANES

    def body(cb_ref, cc_ref, cx_ref, w_ref, pre_ref):
        u = cc_ref[...] * cx_ref[...]
        conv = _shift_rows(u, s, True) * w_ref[0:1, :] + u * w_ref[1:2, :] + _shift_rows(u, s, False) * w_ref[2:3, :]
        pre_ref[...] = cb_ref[...] * conv

    def col(off):
        return pl.BlockSpec((s, LANES), lambda j: (0, off // LANES + j))

    return pl.pallas_call(body, grid=(nt,), in_specs=[col(OCB), col(OCC), col(OCX), pl.BlockSpec((3, LANES), lambda j: (0, j))],
                          out_specs=pl.BlockSpec((s, LANES), lambda j: (0, j)),
                          out_shape=jax.ShapeDtypeStruct((s, CONV_CH), F32), name=name,
                          compiler_params=_cp(("parallel",)))(proj, proj, proj, conv_w)


def _conv_bwd(proj, conv_w, d_pre, name):
    s = proj.shape[0]
    nt = CONV_CH // LANES

    def body(cb_ref, cc_ref, cx_ref, w_ref, dp_ref, dcb_ref, dcc_ref, dcx_ref, dw_ref):
        cc, cx = cc_ref[...], cx_ref[...]
        u = cc * cx
        up, dn = _shift_rows(u, s, True), _shift_rows(u, s, False)
        w0, w1, w2 = w_ref[0:1, :], w_ref[1:2, :], w_ref[2:3, :]
        conv = up * w0 + u * w1 + dn * w2
        dp = dp_ref[...]
        dcb_ref[...] = (dp * conv).astype(dcb_ref.dtype)
        dconv = dp * cb_ref[...]
        du = _shift_rows(dconv, s, False) * w0 + dconv * w1 + _shift_rows(dconv, s, True) * w2
        dcc_ref[...] = (du * cx).astype(dcc_ref.dtype)
        dcx_ref[...] = (du * cc).astype(dcx_ref.dtype)
        dw_ref[0:1, :] = jnp.sum(dconv * up, axis=0, keepdims=True)
        dw_ref[1:2, :] = jnp.sum(dconv * u, axis=0, keepdims=True)
        dw_ref[2:3, :] = jnp.sum(dconv * dn, axis=0, keepdims=True)

    def col(off):
        return pl.BlockSpec((s, LANES), lambda j: (0, off // LANES + j))

    blk = pl.BlockSpec((s, LANES), lambda j: (0, j))
    wblk = pl.BlockSpec((3, LANES), lambda j: (0, j))
    full = jax.ShapeDtypeStruct((s, CONV_CH), MXU)
    return pl.pallas_call(body, grid=(nt,), in_specs=[col(OCB), col(OCC), col(OCX), wblk, blk],
                          out_specs=[blk, blk, blk, wblk],
                          out_shape=[full, full, full, jax.ShapeDtypeStruct((3, CONV_CH), F32)],
                          name=name, compiler_params=_cp(("parallel",)))(proj, proj, proj, conv_w, d_pre)


GLA_SUB = 8


def _gla_gates(t_ref, wg_ref, bg_ref):
    t = t_ref[...]
    a = _dot(t, wg_ref[...]) + bg_ref[...]
    la = (jnp.minimum(a, 0.0) - jnp.log(1.0 + jnp.exp(-jnp.abs(a)))) / GTEMP
    return t, a, la


def _gla_masks(reverse):
    ri = lax.broadcasted_iota(jnp.int32, (CHUNK, CHUNK), 0)
    ci = lax.broadcasted_iota(jnp.int32, (CHUNK, CHUNK), 1)
    if reverse:
        cum, mask, mask_t = ci >= ri, ci > ri, ri > ci
    else:
        cum, mask, mask_t = ci <= ri, ci <= ri, ri <= ci
    return cum.astype(F32), mask, mask_t


def _gla_specs(s, reverse):
    nsub = min(GLA_SUB, s // CHUNK)
    nsteps = s // (CHUNK * nsub)

    def row(n):
        return nsteps - 1 - n if reverse else n

    def chunk(pi):
        return nsub - 1 - pi if reverse else pi

    return nsub, nsteps, row, chunk


def _gla_fwd(proj, wg_pad, bg, reverse, name):
    s = proj.shape[0]
    nsub, nsteps, row, chunk = _gla_specs(s, reverse)
    rb = nsub * CHUNK

    def body(q_ref, k_ref, v_ref, t_ref, wg_ref, bg_ref, o_ref, st_ref, state):
        @pl.when(pl.program_id(0) == 0)
        def _():
            state[...] = jnp.zeros_like(state)

        _, _, la = _gla_gates(t_ref, wg_ref, bg_ref)
        cumf, mask, _ = _gla_masks(reverse)
        lane = lax.broadcasted_iota(jnp.int32, (CHUNK, LANES), 1)
        for pi in range(nsub):
            rows = slice(chunk(pi) * CHUNK, (chunk(pi) + 1) * CHUNK)
            la_c = la[rows]
            b_all = _cum_dot(cumf, la_c)
            bl_all = jnp.sum(la_c, axis=0, keepdims=True)
            for p in range(GH // 2):
                sl = slice(p * LANES, (p + 1) * LANES)
                b, bl = b_all[:, sl], bl_all[:, sl]
                qd = q_ref[rows, sl] * (GDK ** -0.5) * jnp.exp(b)
                ki = k_ref[rows, sl] * jnp.exp(-b)
                kte = k_ref[rows, sl] * jnp.exp(bl - b)
                decay = jnp.exp(bl)
                for half in range(2):
                    h = 2 * p + half
                    lm = (lane < GDK) if half == 0 else (lane >= GDK)
                    qd_h = jnp.where(lm, qd, 0.0)
                    kte_h = jnp.where(lm, kte, 0.0)
                    v_h = v_ref[rows, h * GDV:(h + 1) * GDV]
                    st = state[h]
                    a_mat = jnp.where(mask, _dot_nt(qd_h, ki), 0.0)
                    o_ref[rows, h * GDV:(h + 1) * GDV] = _dot(a_mat, v_h) + _dot_nt(qd_h, st)
                    st_ref[pi, h] = st
                    state[h] = st * decay + _dot_tn(v_h, kte_h)

    return pl.pallas_call(
        body, grid=(nsteps,),
        in_specs=[pl.BlockSpec((rb, GQK), lambda n: (row(n), OGQ // GQK)),
                  pl.BlockSpec((rb, GQK), lambda n: (row(n), OGK // GQK)),
                  pl.BlockSpec((rb, GW), lambda n: (row(n), OGV // GW)),
                  pl.BlockSpec((rb, LANES), lambda n: (row(n), OT // LANES)),
                  pl.BlockSpec((LANES, GQK), lambda n: (0, 0)),
                  pl.BlockSpec((1, GQK), lambda n: (0, 0))],
        out_specs=[pl.BlockSpec((rb, GW), lambda n: (row(n), 0)),
                   pl.BlockSpec((nsub, GH, GDV, LANES), lambda n: (n, 0, 0, 0))],
        out_shape=[jax.ShapeDtypeStruct((s, GW), F32), jax.ShapeDtypeStruct((s // CHUNK, GH, GDV, LANES), F32)],
        scratch_shapes=[pltpu.VMEM((GH, GDV, LANES), F32)],
        name=name, compiler_params=_cp(("arbitrary",)))(proj, proj, proj, proj, wg_pad, bg)


def _gla_bwd(proj, wg_pad, bg, states, d_o, reverse, name):
    s = proj.shape[0]
    nsub, nsteps, row, chunk = _gla_specs(s, reverse)
    rb = nsub * CHUNK

    def body(q_ref, k_ref, v_ref, t_ref, wg_ref, bg_ref, st_ref, do_ref,
             dq_ref, dk_ref, dv_ref, dt_ref, dwg_ref, dbg_ref, dstate, da_buf):
        @pl.when(pl.program_id(0) == 0)
        def _():
            dstate[...] = jnp.zeros_like(dstate)
            dwg_ref[...] = jnp.zeros_like(dwg_ref)
            dbg_ref[...] = jnp.zeros_like(dbg_ref)

        t, a, la = _gla_gates(t_ref, wg_ref, bg_ref)
        cumf, mask, mask_t = _gla_masks(reverse)
        lane = lax.broadcasted_iota(jnp.int32, (CHUNK, LANES), 1)
        for pi in reversed(range(nsub)):
            rows = slice(chunk(pi) * CHUNK, (chunk(pi) + 1) * CHUNK)
            la_c = la[rows]
            b_all = _cum_dot(cumf, la_c)
            bl_all = jnp.sum(la_c, axis=0, keepdims=True)
            for p in range(GH // 2):
                sl = slice(p * LANES, (p + 1) * LANES)
                b, bl = b_all[:, sl], bl_all[:, sl]
                e, ei, ee = jnp.exp(b), jnp.exp(-b), jnp.exp(bl - b)
                qd = q_ref[rows, sl] * (GDK ** -0.5) * e
                ki, kte = k_ref[rows, sl] * ei, k_ref[rows, sl] * ee
                decay = jnp.exp(bl)
                dqd = jnp.zeros((CHUNK, LANES), F32)
                dki = jnp.zeros((CHUNK, LANES), F32)
                dkte = jnp.zeros((CHUNK, LANES), F32)
                ddecay = jnp.zeros((1, LANES), F32)
                for half in range(2):
                    h = 2 * p + half
                    lm = (lane < GDK) if half == 0 else (lane >= GDK)
                    qd_h = jnp.where(lm, qd, 0.0)
                    ki_h = jnp.where(lm, ki, 0.0)
                    kte_h = jnp.where(lm, kte, 0.0)
                    v_h = v_ref[rows, h * GDV:(h + 1) * GDV]
                    do_h = do_ref[rows, h * GDV:(h + 1) * GDV]
                    st = st_ref[pi, h]
                    dst = dstate[h]
                    at_mat = jnp.where(mask_t, _dot_nt(ki_h, qd_h), 0.0)
                    da_mat = jnp.where(mask, _dot_nt(do_h, v_h), 0.0)
                    dat_mat = jnp.where(mask_t, _dot_nt(v_h, do_h), 0.0)
                    dv_ref[rows, h * GDV:(h + 1) * GDV] = _dot(at_mat, do_h) + _dot_nt(kte_h, dst)
                    dqd += _dot(jnp.concatenate([do_h, da_mat], axis=1), jnp.concatenate([st, ki_h], axis=0))
                    dki += _dot(dat_mat, qd_h)
                    dkte += _dot(v_h, dst)
                    ddecay += jnp.sum(dst * st, axis=0, keepdims=True)
                    dstate[h] = dst * decay + _dot_tn(do_h, qd_h)
                dq_ref[rows, sl] = dqd * e * (GDK ** -0.5)
                dk_ref[rows, sl] = dki * ei + dkte * ee
                db = dqd * qd - dki * ki - dkte * kte
                dbl = jnp.sum(dkte * kte, axis=0, keepdims=True) + decay * ddecay
                da_buf[rows, sl] = _cum_dot(cumf, db, True) + dbl
        da = da_buf[...] * (1.0 / GTEMP) * _sigmoid(-a)
        dt_ref[...] = _dot_nt(da, wg_ref[...])
        dwg_ref[...] += _dot_tn(t, da)
        dbg_ref[...] += jnp.sum(da, axis=0, keepdims=True)

    def prow(j):
        return row(nsteps - 1 - j)

    return pl.pallas_call(
        body, grid=(nsteps,),
        in_specs=[pl.BlockSpec((rb, GQK), lambda j: (prow(j), OGQ // GQK)),
                  pl.BlockSpec((rb, GQK), lambda j: (prow(j), OGK // GQK)),
                  pl.BlockSpec((rb, GW), lambda j: (prow(j), OGV // GW)),
                  pl.BlockSpec((rb, LANES), lambda j: (prow(j), OT // LANES)),
                  pl.BlockSpec((LANES, GQK), lambda j: (0, 0)),
                  pl.BlockSpec((1, GQK), lambda j: (0, 0)),
                  pl.BlockSpec((nsub, GH, GDV, LANES), lambda j: (nsteps - 1 - j, 0, 0, 0)),
                  pl.BlockSpec((rb, GW), lambda j: (prow(j), 0))],
        out_specs=[pl.BlockSpec((rb, GQK), lambda j: (prow(j), 0)),
                   pl.BlockSpec((rb, GQK), lambda j: (prow(j), 0)),
                   pl.BlockSpec((rb, GW), lambda j: (prow(j), 0)),
                   pl.BlockSpec((rb, LANES), lambda j: (prow(j), 0)),
                   pl.BlockSpec((LANES, GQK), lambda j: (0, 0)),
                   pl.BlockSpec((1, GQK), lambda j: (0, 0))],
        out_shape=[jax.ShapeDtypeStruct((s, GQK), F32), jax.ShapeDtypeStruct((s, GQK), F32),
                   jax.ShapeDtypeStruct((s, GW), F32), jax.ShapeDtypeStruct((s, LANES), F32),
                   jax.ShapeDtypeStruct((LANES, GQK), F32), jax.ShapeDtypeStruct((1, GQK), F32)],
        scratch_shapes=[pltpu.VMEM((GH, GDV, LANES), F32), pltpu.VMEM((rb, GQK), F32)],
        name=name, compiler_params=_cp(("arbitrary",)))(proj, proj, proj, proj, wg_pad, bg, states, d_o)


def _rot_half(x):
    lane = lax.broadcasted_iota(jnp.int32, x.shape, 1)
    first = (lane % MROPE) < (MROPE // 2)
    return jnp.where(first, -pltpu.roll(x, LANES - MROPE // 2, 1), pltpu.roll(x, MROPE // 2, 1))


def _mla_prep(proj, cos, sin, qg, kvg, w_uq, w_ukv, name):
    s = proj.shape[0]
    tr = _rows(s)

    def body(mq_ref, mkv_ref, t_ref, cos_ref, sin_ref, qg_ref, kvg_ref, wuq_ref, wukv_ref, q_ref, k_ref, v_ref):
        cosv, sinv = cos_ref[...], sin_ref[...]
        lane = lax.broadcasted_iota(jnp.int32, (tr, LANES), 1)

        def rope(xv):
            return xv * cosv + _rot_half(xv) * sinv

        qm = _dot(_rms(mq_ref[...], qg_ref[...]), wuq_ref[...])
        kv = _dot(_rms(mkv_ref[...], kvg_ref[...]), wukv_ref[...])
        kr_lo = jnp.where(lane < MROPE, rope(t_ref[...]), 0.0)
        kr_hi = pltpu.roll(kr_lo, MROPE, 1)
        for p in range(MH // 2):
            r = rope(qm[:, MW + p * LANES:MW + (p + 1) * LANES]).astype(q_ref.dtype)
            q_ref[2 * p, :, LANES:] = r
            q_ref[2 * p + 1, :, LANES:] = r
        for h in range(MH):
            q_ref[h, :, :LANES] = qm[:, h * LANES:(h + 1) * LANES].astype(q_ref.dtype)
            k_ref[h, :, :LANES] = kv[:, 2 * h * LANES:(2 * h + 1) * LANES].astype(k_ref.dtype)
            k_ref[h, :, LANES:] = (kr_lo if h % 2 == 0 else kr_hi).astype(k_ref.dtype)
            v_ref[h] = kv[:, (2 * h + 1) * LANES:(2 * h + 2) * LANES].astype(v_ref.dtype)

    def full(shape):
        return pl.BlockSpec(shape, lambda i: (0,) * len(shape))

    return pl.pallas_call(
        body, grid=(s // tr,),
        in_specs=[pl.BlockSpec((tr, MQL), lambda i: (i, OMQ // MQL)),
                  pl.BlockSpec((tr, MKVL), lambda i: (i, OMKV // MKVL)),
                  pl.BlockSpec((tr, LANES), lambda i: (i, OT // LANES)),
                  pl.BlockSpec((tr, LANES), lambda i: (i, 0)),
                  pl.BlockSpec((tr, LANES), lambda i: (i, 0)),
                  full((1, MQL)), full((1, MKVL)), full((MQL, MQW)), full((MKVL, MKVW))],
        out_specs=[pl.BlockSpec((MH, tr, 2 * LANES), lambda i: (0, i, 0)),
                   pl.BlockSpec((MH, tr, 2 * LANES), lambda i: (0, i, 0)),
                   pl.BlockSpec((MH, tr, LANES), lambda i: (0, i, 0))],
        out_shape=[jax.ShapeDtypeStruct((MH, s, 2 * LANES), MXU), jax.ShapeDtypeStruct((MH, s, 2 * LANES), MXU),
                   jax.ShapeDtypeStruct((MH, s, LANES), MXU)],
        name=name, compiler_params=_cp(("parallel",)))(proj, proj, proj, cos, sin, qg, kvg, w_uq, w_ukv)


def _mla_prep_bwd(proj, cos, sin, qg, kvg, w_uq, w_ukv, d_q, d_k, d_v, name):
    s = proj.shape[0]
    tr = _rows(s)

    def body(mq_ref, mkv_ref, cos_ref, sin_ref, qg_ref, kvg_ref, wuq_ref, wukv_ref, dq_ref, dk_ref, dv_ref,
             dmq_ref, dmkv_ref, dt_ref, dwuq_ref, dwukv_ref, dqg_ref, dkvg_ref):
        @pl.when(pl.program_id(0) == 0)
        def _():
            for r in (dwuq_ref, dwukv_ref, dqg_ref, dkvg_ref):
                r[...] = jnp.zeros_like(r)

        cosv, sinv = cos_ref[...], sin_ref[...]
        lane = lax.broadcasted_iota(jnp.int32, (tr, LANES), 1)
        lo = lane < MROPE

        def unrope(dv):
            return dv * cosv - _rot_half(dv * sinv)

        parts = [dq_ref[h, :, :LANES] for h in range(MH)]
        for p in range(MH // 2):
            parts.append(unrope(jnp.where(lo, dq_ref[2 * p, :, LANES:], dq_ref[2 * p + 1, :, LANES:])))
        d_qm = jnp.concatenate(parts, axis=1)
        mq, qgv = mq_ref[...], qg_ref[...]
        cq = _rms(mq, qgv)
        dwuq_ref[...] += _dot_tn(cq, d_qm)
        dmq, dqg = _rms_bwd(_dot_nt(d_qm, wuq_ref[...]), mq, qgv)
        dmq_ref[...] = dmq.astype(dmq_ref.dtype)
        dqg_ref[...] += dqg

        parts = []
        for h in range(MH):
            parts += [dk_ref[h, :, :LANES], dv_ref[h]]
        d_kv = jnp.concatenate(parts, axis=1)
        mkv, kvgv = mkv_ref[...], kvg_ref[...]
        ckv = _rms(mkv, kvgv)
        dwukv_ref[...] += _dot_tn(ckv, d_kv)
        dmkv, dkvg = _rms_bwd(_dot_nt(d_kv, wukv_ref[...]), mkv, kvgv)
        dmkv_ref[...] = dmkv.astype(dmkv_ref.dtype)
        dkvg_ref[...] += dkvg

        even = dk_ref[0, :, LANES:] + dk_ref[2, :, LANES:] + dk_ref[4, :, LANES:]
        odd = dk_ref[1, :, LANES:] + dk_ref[3, :, LANES:] + dk_ref[5, :, LANES:]
        d_kr = jnp.where(lo, even, 0.0) + pltpu.roll(jnp.where(lo, 0.0, odd), MROPE, 1)
        dt_ref[...] = jnp.where(lo, unrope(d_kr), 0.0)

    def full(shape):
        return pl.BlockSpec(shape, lambda i: (0,) * len(shape))

    return pl.pallas_call(
        body, grid=(s // tr,),
        in_specs=[pl.BlockSpec((tr, MQL), lambda i: (i, OMQ // MQL)),
                  pl.BlockSpec((tr, MKVL), lambda i: (i, OMKV // MKVL)),
                  pl.BlockSpec((tr, LANES), lambda i: (i, 0)),
                  pl.BlockSpec((tr, LANES), lambda i: (i, 0)),
                  full((1, MQL)), full((1, MKVL)), full((MQL, MQW)), full((MKVL, MKVW)),
                  pl.BlockSpec((MH, tr, 2 * LANES), lambda i: (0, i, 0)),
                  pl.BlockSpec((MH, tr, 2 * LANES), lambda i: (0, i, 0)),
                  pl.BlockSpec((MH, tr, LANES), lambda i: (0, i, 0))],
        out_specs=[pl.BlockSpec((tr, MQL), lambda i: (i, 0)), pl.BlockSpec((tr, MKVL), lambda i: (i, 0)),
                   pl.BlockSpec((tr, LANES), lambda i: (i, 0)),
                   full((MQL, MQW)), full((MKVL, MKVW)), full((1, MQL)), full((1, MKVL))],
        out_shape=[jax.ShapeDtypeStruct((s, MQL), MXU), jax.ShapeDtypeStruct((s, MKVL), MXU),
                   jax.ShapeDtypeStruct((s, LANES), F32),
                   jax.ShapeDtypeStruct((MQL, MQW), F32), jax.ShapeDtypeStruct((MKVL, MKVW), F32),
                   jax.ShapeDtypeStruct((1, MQL), F32), jax.ShapeDtypeStruct((1, MKVL), F32)],
        name=name, compiler_params=_cp(("arbitrary",)))(proj, proj, cos, sin, qg, kvg, w_uq, w_ukv, d_q, d_k, d_v)


ATT_SCALE = (MNOPE + MROPE) ** -0.5
ATT_SCALE_LOG2 = ATT_SCALE * 1.4426950408889634
ATT_TQ_FWD, ATT_TQ = 2048, 2048
ATT_SUB, ATT_SUB_BWD = 256, 256


def _attn_fwd(q, k, v, name):
    s = q.shape[1]
    tq = min(ATT_TQ_FWD, s)
    sub = min(ATT_SUB, tq)

    def body(q_ref, k_ref, v_ref, o_ref, lse_ref):
        for r0 in range(0, tq, sub):
            rows = slice(r0, r0 + sub)
            sc = _dot_nt(q_ref[0, rows, :], k_ref[0])
            m = jnp.max(sc, axis=-1, keepdims=True)
            p = jnp.exp2((sc - m) * ATT_SCALE_LOG2)
            l = jnp.sum(p, axis=-1, keepdims=True)
            o_ref[rows, :] = _dot(p, v_ref[0]) / l
            lse_ref[0, rows, :] = m * ATT_SCALE_LOG2 + jnp.log2(l)

    return pl.pallas_call(
        body, grid=(MH, s // tq),
        in_specs=[pl.BlockSpec((1, tq, 2 * LANES), lambda h, i: (h, i, 0)),
                  pl.BlockSpec((1, s, 2 * LANES), lambda h, i: (h, 0, 0)),
                  pl.BlockSpec((1, s, LANES), lambda h, i: (h, 0, 0))],
        out_specs=[pl.BlockSpec((tq, LANES), lambda h, i: (i, h)),
                   pl.BlockSpec((1, tq, 1), lambda h, i: (h, i, 0))],
        out_shape=[jax.ShapeDtypeStruct((s, MW), F32), jax.ShapeDtypeStruct((MH, s, 1), F32)],
        name=name, compiler_params=_cp(("parallel", "parallel")))(q, k, v)


def _attn_bwd(q, k, v, o, lse, d_o, name):
    s = q.shape[1]
    tq = min(ATT_TQ, s)
    sub = min(ATT_SUB_BWD, tq)

    def body(q_ref, k_ref, v_ref, o_ref, lse_ref, do_ref, dq_ref, dk_ref, dv_ref):
        @pl.when(pl.program_id(1) == 0)
        def _():
            dk_ref[...] = jnp.zeros_like(dk_ref)
            dv_ref[...] = jnp.zeros_like(dv_ref)

        kv = k_ref[0]
        for r0 in range(0, tq, sub):
            rows = slice(r0, r0 + sub)
            qv, do = q_ref[0, rows, :], do_ref[rows, :]
            p = jnp.exp2(_dot_nt(qv, kv) * ATT_SCALE_LOG2 - lse_ref[0, rows, :])
            delta = jnp.sum(do * o_ref[rows, :], axis=-1, keepdims=True)
            ds = p * (_dot_nt(do, v_ref[0]) - delta)
            dq_ref[0, rows, :] = _dot(ds, kv) * ATT_SCALE
            dk_ref[0] += _dot_tn(ds, qv) * ATT_SCALE
            dv_ref[0] += _dot_tn(p, do)

    return pl.pallas_call(
        body, grid=(MH, s // tq),
        in_specs=[pl.BlockSpec((1, tq, 2 * LANES), lambda h, i: (h, i, 0)),
                  pl.BlockSpec((1, s, 2 * LANES), lambda h, i: (h, 0, 0)),
                  pl.BlockSpec((1, s, LANES), lambda h, i: (h, 0, 0)),
                  pl.BlockSpec((tq, LANES), lambda h, i: (i, h)),
                  pl.BlockSpec((1, tq, 1), lambda h, i: (h, i, 0)),
                  pl.BlockSpec((tq, LANES), lambda h, i: (i, h))],
        out_specs=[pl.BlockSpec((1, tq, 2 * LANES), lambda h, i: (h, i, 0)),
                   pl.BlockSpec((1, s, 2 * LANES), lambda h, i: (h, 0, 0)),
                   pl.BlockSpec((1, s, LANES), lambda h, i: (h, 0, 0))],
        out_shape=[jax.ShapeDtypeStruct((MH, s, 2 * LANES), F32), jax.ShapeDtypeStruct((MH, s, 2 * LANES), F32),
                   jax.ShapeDtypeStruct((MH, s, LANES), F32)],
        name=name, compiler_params=_cp(("parallel", "arbitrary")))(q, k, v, o, lse, d_o)


def _merge_fwd(o_f, o_b, o_att, pre, proj, gng, mog, cog, name):
    s = proj.shape[0]
    tr = _rows_light(s)

    def body(of_ref, ob_ref, oa_ref, pre_ref, z_ref, gng_ref, mog_ref, cog_ref, y_ref):
        z = z_ref[...]
        sz = z * _sigmoid(z)
        osum = of_ref[...] + ob_ref[...]
        gg = gng_ref[...]
        for h in range(GH):
            sl = slice(h * GDV, (h + 1) * GDV)
            y_ref[:, sl] = (_rms(osum[:, sl], gg) * sz[:, sl]).astype(y_ref.dtype)
        y_ref[:, GW:GW + MW] = (_rms(oa_ref[...], mog_ref[...]) * sz[:, GW:GW + MW]).astype(y_ref.dtype)
        y_ref[:, GW + MW:] = (_rms(pre_ref[...], cog_ref[...]) * sz[:, GW + MW:]).astype(y_ref.dtype)

    def row(w):
        return pl.BlockSpec((tr, w), lambda i: (i, 0))

    def vec(w):
        return pl.BlockSpec((1, w), lambda i: (0, 0))

    return pl.pallas_call(
        body, grid=(s // tr,),
        in_specs=[row(GW), row(GW), row(MW), row(CONV_CH), row(D_MIX), vec(GDV), vec(MW), vec(CONV_CH)],
        out_specs=row(D_MIX), out_shape=jax.ShapeDtypeStruct((s, D_MIX), MXU),
        name=name, compiler_params=_cp(("parallel",)))(o_f, o_b, o_att, pre, proj, gng, mog, cog)


def _merge_bwd(d_y, o_f, o_b, o_att, pre, proj, gng, mog, cog, name):
    s = proj.shape[0]
    tr = _rows(s)

    def body(dy_ref, of_ref, ob_ref, oa_ref, pre_ref, z_ref, gng_ref, mog_ref, cog_ref,
             dz_ref, dos_ref, doa_ref, dpre_ref, dgng_ref, dmog_ref, dcog_ref):
        @pl.when(pl.program_id(0) == 0)
        def _():
            for r in (dgng_ref, dmog_ref, dcog_ref):
                r[...] = jnp.zeros_like(r)

        def silu_terms(sl):
            z, dy = z_ref[:, sl], dy_ref[:, sl]
            sg = _sigmoid(z)
            return dy * (z * sg), dy * (sg * (1.0 + z * (1.0 - sg)))

        gg = gng_ref[...]
        dgg = jnp.zeros_like(gg)
        for h in range(GH):
            sl = slice(h * GDV, (h + 1) * GDV)
            dcat, dyz = silu_terms(sl)
            osum = of_ref[:, sl] + ob_ref[:, sl]
            dz_ref[:, sl] = (dyz * _rms(osum, gg)).astype(dz_ref.dtype)
            dx, dg = _rms_bwd(dcat, osum, gg)
            dos_ref[:, sl] = dx
            dgg += dg
        dgng_ref[...] += dgg
        sl = slice(GW, GW + MW)
        dcat, dyz = silu_terms(sl)
        oa, mg = oa_ref[...], mog_ref[...]
        dz_ref[:, sl] = (dyz * _rms(oa, mg)).astype(dz_ref.dtype)
        dx, dg = _rms_bwd(dcat, oa, mg)
        doa_ref[...] = dx
        dmog_ref[...] += dg
        sl = slice(GW + MW, D_MIX)
        dcat, dyz = silu_terms(sl)
        pv, cg = pre_ref[...], cog_ref[...]
        dz_ref[:, sl] = (dyz * _rms(pv, cg)).astype(dz_ref.dtype)
        dx, dg = _rms_bwd(dcat, pv, cg)
        dpre_ref[...] = dx
        dcog_ref[...] += dg

    def row(w):
        return pl.BlockSpec((tr, w), lambda i: (i, 0))

    def vec(w):
        return pl.BlockSpec((1, w), lambda i: (0, 0))

    def rs(w):
        return jax.ShapeDtypeStruct((s, w), F32)

    def vs(w):
        return jax.ShapeDtypeStruct((1, w), F32)

    return pl.pallas_call(
        body, grid=(s // tr,),
        in_specs=[row(D_MIX), row(GW), row(GW), row(MW), row(CONV_CH), row(D_MIX), vec(GDV), vec(MW), vec(CONV_CH)],
        out_specs=[row(D_MIX), row(GW), row(MW), row(CONV_CH), vec(GDV), vec(MW), vec(CONV_CH)],
        out_shape=[jax.ShapeDtypeStruct((s, D_MIX), MXU),
                   rs(GW), rs(MW), rs(CONV_CH), vs(GDV), vs(MW), vs(CONV_CH)],
        name=name, compiler_params=_cp(("arbitrary",)))(d_y, o_f, o_b, o_att, pre, proj, gng, mog, cog)


def _assemble_dproj(d_z, d_cb, d_cc, d_cx, d_mkv, dv_f, dv_b, dq_f, dq_b, dk_f, dk_b, d_mq, dt_m, dt_f, dt_b, name):
    s = d_z.shape[0]
    tr = _rows_light(s)

    def body(dz, dcb, dcc, dcx, dmkv, dvf, dvb, dqf, dqb, dkf, dkb, dmq, dtm, dtf, dtb, out):
        dt = out.dtype
        out[:, OZ:OZ + D_MIX] = dz[...].astype(dt)
        out[:, OCB:OCB + CONV_CH] = dcb[...].astype(dt)
        out[:, OCC:OCC + CONV_CH] = dcc[...].astype(dt)
        out[:, OCX:OCX + CONV_CH] = dcx[...].astype(dt)
        out[:, OMKV:OMKV + MKVL] = dmkv[...].astype(dt)
        out[:, OGV:OGV + GW] = (dvf[...] + dvb[...]).astype(dt)
        out[:, OGQ:OGQ + GQK] = (dqf[...] + dqb[...]).astype(dt)
        out[:, OGK:OGK + GQK] = (dkf[...] + dkb[...]).astype(dt)
        out[:, OMQ:OMQ + MQL] = dmq[...].astype(dt)
        out[:, OT:OT + LANES] = (dtm[...] + dtf[...] + dtb[...]).astype(dt)

    args = (d_z, d_cb, d_cc, d_cx, d_mkv, dv_f, dv_b, dq_f, dq_b, dk_f, dk_b, d_mq, dt_m, dt_f, dt_b)
    return pl.pallas_call(
        body, grid=(s // tr,),
        in_specs=[pl.BlockSpec((tr, a.shape[1]), lambda i: (i, 0)) for a in args],
        out_specs=pl.BlockSpec((tr, PW), lambda i: (i, 0)),
        out_shape=jax.ShapeDtypeStruct((s, PW), MXU), name=name, compiler_params=_cp(("parallel",)))(*args)


def _layer_fwd(x, mod, wt, cos, sin, tag, late=None, in_after=(), h=None):
    shift, scale, gate = mod
    if h is None:
        h = _norm_mod(x, wt["norm_g"], scale, shift, f"norm_mod_{tag}")
    (proj,) = _matmul(h, wt["w_in"], dims="nn", tm=2048, tn=256, tk=2048, out_dtypes=(F32,), name=f"in_proj_{tag}",
                      after=in_after)
    if late is not None:
        wt.update(late(proj))
    o_f, st_f = _gla_fwd(proj, wt["wg_pad_f"], wt["bg_f"], False, f"gla_fwd_f_{tag}")
    o_b, st_b = _gla_fwd(proj, wt["wg_pad_b"], wt["bg_b"], True, f"gla_fwd_b_{tag}")
    q, k, v = _mla_prep(proj, cos, sin, wt["q_norm_g"], wt["kv_norm_g"], wt["w_uq"], wt["w_ukv"], f"mla_prep_{tag}")
    o_att, lse = _attn_fwd(q, k, v, f"attn_fwd_{tag}")
    pre = _conv_fwd(proj, wt["conv_w"], f"conv_fwd_{tag}")
    y = _merge_fwd(o_f, o_b, o_att, pre, proj, wt["gla_norm_g"], wt["mla_out_g"], wt["conv_out_g"], f"merge_fwd_{tag}")
    x_new, u = _matmul(y, wt["w_out"], dims="nn", tm=2048, tn=256, tk=2048, out_dtypes=(F32, F32),
                       name=f"out_proj_{tag}", epilogue=lambda acc, xv, gv: (xv + gv * acc, acc),
                       extras=(x, gate), extra_kinds=("mn", "n"))
    saved = dict(x=x, h=h, proj=proj, o_f=o_f, o_b=o_b, st_f=st_f, st_b=st_b, q=q, k=k, v=v,
                 o_att=o_att, lse=lse, pre=pre, y=y, u=u)
    return x_new, saved


def _layer_bwd(d_out, sv, mod, wt, cos, sin, tag, ship=None, dx_first=None, ship_rest=None, started=None, below=None,
               first_after=()):
    shift, scale, gate = mod
    proj = sv["proj"]
    if started is None:
        d_u, d_gate = _gate_bwd(d_out, sv["u"], gate, f"gate_bwd_{tag}")
    else:
        d_u, d_gate = started
    (g_w_out,) = _matmul(sv["y"], d_u, dims="tn", tm=1024, tn=512, tk=2048, out_dtypes=(MXU,), name=f"out_proj_dw_{tag}",
                         after=first_after)
    (d_y,) = _matmul(d_u, wt["w_out"], dims="nt", tm=2048, tn=256, tk=2048, out_dtypes=(F32,), name=f"out_proj_dx_{tag}",
                     after=(g_w_out,))
    d_z, d_osum, d_oatt, d_pre, d_gng, d_mog, d_cog = _merge_bwd(
        d_y, sv["o_f"], sv["o_b"], sv["o_att"], sv["pre"], proj, wt["gla_norm_g"], wt["mla_out_g"], wt["conv_out_g"],
        f"merge_bwd_{tag}")
    d_cb, d_cc, d_cx, d_conv_w = _conv_bwd(proj, wt["conv_w"], d_pre, f"conv_bwd_{tag}")
    d_q, d_k, d_v = _attn_bwd(sv["q"], sv["k"], sv["v"], sv["o_att"], sv["lse"], d_oatt, f"attn_bwd_{tag}")
    d_mq, d_mkv, dt_m, g_w_uq, g_w_ukv, d_qg, d_kvg = _mla_prep_bwd(
        proj, cos, sin, wt["q_norm_g"], wt["kv_norm_g"], wt["w_uq"], wt["w_ukv"], d_q, d_k, d_v, f"mla_prep_bwd_{tag}")
    bg_f, bg_b = wt["bg_f"], wt["bg_b"]
    if ship_rest is not None:
        tok = ship_rest(dict(w_out=g_w_out, w_uq=g_w_uq, w_ukv=g_w_ukv))
        bg_f, bg_b = bg_f + tok, bg_b + tok
    dq_f, dk_f, dv_f, dt_f, d_wg_f, d_bg_f = _gla_bwd(proj, wt["wg_pad_f"], bg_f, sv["st_f"], d_osum, False,
                                                     f"gla_bwd_f_{tag}")
    dq_b, dk_b, dv_b, dt_b, d_wg_b, d_bg_b = _gla_bwd(proj, wt["wg_pad_b"], bg_b, sv["st_b"], d_osum, True,
                                                     f"gla_bwd_b_{tag}")
    d_proj = _assemble_dproj(d_z, d_cb, d_cc, d_cx, d_mkv, dv_f, dv_b, dq_f, dq_b, dk_f, dk_b, d_mq, dt_m, dt_f, dt_b,
                             f"assemble_dproj_{tag}")
    grads = dict(w_out=g_w_out, w_uq=g_w_uq, w_ukv=g_w_ukv,
                 wg_pad_f=d_wg_f, bg_f=d_bg_f, wg_pad_b=d_wg_b, bg_b=d_bg_b, gla_norm_g=d_gng,
                 q_norm_g=d_qg, kv_norm_g=d_kvg, mla_out_g=d_mog, conv_w=d_conv_w, conv_out_g=d_cog)

    def in_dw(after):
        (g_w_in,) = _matmul(sv["h"], d_proj, dims="tn", tm=2048, tn=256, tk=2048, out_dtypes=(MXU,),
                            name=f"in_proj_dw_{tag}", after=after)
        grads["w_in"] = g_w_in
        return dict(w_in=g_w_in, w_out=g_w_out, w_uq=g_w_uq, w_ukv=g_w_ukv)

    def in_dx(after):
        (d_h,) = _matmul(d_proj, wt["w_in"], dims="nt", tm=1024, tn=512, tk=PW, out_dtypes=(F32,),
                         name=f"in_proj_dx_{tag}", after=after)
        d_x, d_shift, d_scale, d_ng, *handed = _norm_mod_bwd(d_h, sv["x"], d_out, wt["norm_g"], scale,
                                                             f"norm_mod_bwd_{tag}", below)
        grads["norm_g"] = d_ng
        grads["started_below"] = tuple(handed)
        return d_x, (d_shift, d_scale, d_gate)

    if dx_first is None:
        big = in_dw(())
        d_x, d_mod = in_dx((big["w_in"],) if ship is None else ship(big))
    else:
        d_x, d_mod = in_dx(())
        big = in_dw(dx_first(d_x, d_mod, grads))
        ship(big)
    return d_x, d_mod, grads


IN_SEGS = ((3808, 5856), (2272, 3808), (1952, 2208), (768, 1536), (0, 768), (1568, 1952), (2208, 2272), (1536, 1568))


def _heads_apart(w):
    w3 = w.reshape(w.shape[:-1] + (MH, MNOPE + MROPE))
    return jnp.concatenate([w3[..., :MNOPE].reshape(w.shape[:-1] + (MH * MNOPE,)),
                            w3[..., MNOPE:].reshape(w.shape[:-1] + (MH * MROPE,))], axis=-1)


def _heads_together(g):
    nope = g[..., :MH * MNOPE].reshape(g.shape[:-1] + (MH, MNOPE))
    rope = g[..., MH * MNOPE:].reshape(g.shape[:-1] + (MH, MROPE))
    return jnp.concatenate([nope, rope], axis=-1).reshape(g.shape[:-1] + (MQW,))


def _perm_gathered(g, segs, width):
    per = g.shape[-1]
    parts, total = [], 0
    for a, b in segs:
        c = a
        while c < b:
            j = c // per
            hi = min(b, (j + 1) * per)
            parts.append(g[j, :, c - j * per:hi - j * per])
            c = hi
        total += b - a
    if width > total:
        parts.append(jnp.zeros((g.shape[1], width - total), g.dtype))
    return jnp.concatenate(parts, axis=1)


def _scatter_perm(gp, segs, per):
    offs, o = [], 0
    for a, b in segs:
        offs.append((a, b, o))
        o += b - a
    blocks = []
    for j in range(N_DEV):
        lo, hi = j * per, (j + 1) * per
        pieces = []
        for a, b, o in sorted(offs):
            s0, s1 = max(a, lo), min(b, hi)
            if s0 < s1:
                pieces.append(gp[:, o + s0 - a:o + s1 - a])
        blocks.append(jnp.concatenate(pieces, axis=1))
    return jnp.stack(blocks)


def _prep_layer_weights(w_in, w_out, w_uq, w_ukv, small):
    def vec(v):
        return v.reshape(1, -1).astype(F32)

    zeros = functools.partial(jnp.zeros, dtype=F32)
    wg_f, wg_b = small["gla_wg_f"].astype(F32), small["gla_wg_b"].astype(F32)
    wg_pad_f = jnp.concatenate([zeros((MROPE, GQK)), wg_f, zeros((LANES - MROPE - GRANK, GQK))], axis=0)
    wg_pad_b = jnp.concatenate([zeros((MROPE + GRANK, GQK)), wg_b, zeros((LANES - MROPE - 2 * GRANK, GQK))], axis=0)
    wt = dict(norm_g=vec(small["norm_g"]), wg_pad_f=wg_pad_f, wg_pad_b=wg_pad_b,
              bg_f=vec(small["gla_bg_f"]), bg_b=vec(small["gla_bg_b"]), gla_norm_g=vec(small["gla_norm_g"]),
              q_norm_g=vec(small["mla_q_norm_g"]), kv_norm_g=vec(small["mla_kv_norm_g"]),
              mla_out_g=vec(small["mla_out_g"]), conv_w=small["conv_w"].astype(F32),
              conv_out_g=vec(small["conv_out_g"]))
    for name, w in (("w_in", w_in), ("w_out", w_out), ("w_uq", w_uq), ("w_ukv", w_ukv)):
        if w is not None:
            wt[name] = w.astype(MXU)
    return wt


def _natural_small(gr):
    return dict(norm_g=gr["norm_g"][0],
                gla_wg_f=gr["wg_pad_f"][MROPE:MROPE + GRANK], gla_bg_f=gr["bg_f"][0],
                gla_wg_b=gr["wg_pad_b"][MROPE + GRANK:MROPE + 2 * GRANK], gla_bg_b=gr["bg_b"][0],
                gla_norm_g=gr["gla_norm_g"][0], mla_q_norm_g=gr["q_norm_g"][0], mla_kv_norm_g=gr["kv_norm_g"][0],
                mla_out_g=gr["mla_out_g"][0], conv_w=gr["conv_w"], conv_out_g=gr["conv_out_g"][0])


def _peer(r):
    ax, ay, ac = lax.axis_index("x"), lax.axis_index("y"), lax.axis_index("c")
    px = 1 - ax if r & 4 else ax
    py = 1 - ay if r & 2 else ay
    pc = 1 - ac if r & 1 else ac
    return (px, py, pc), 4 * px + 2 * py + pc


def _gather_small(arrs, name):
    n = len(arrs)

    def body(*refs):
        ins, outs = refs[:n], refs[n:2 * n]
        send_sems, recv_sems, loc_sems = refs[2 * n:]
        me = _peer(0)[1]

        def remote(a, r, slot):
            return pltpu.make_async_remote_copy(
                src_ref=ins[a], dst_ref=outs[a].at[slot], send_sem=send_sems.at[a, r - 1],
                recv_sem=recv_sems.at[a, r - 1], device_id=_peer(r)[0], device_id_type=MESH)

        locs = [pltpu.make_async_copy(ins[a], outs[a].at[me], loc_sems.at[a]) for a in range(n)]
        for cp in locs:
            cp.start()
        sends = [remote(a, r, me) for r in range(1, N_DEV) for a in range(n)]
        for cp in sends:
            cp.start()
        for r in range(1, N_DEV):
            for a in range(n):
                remote(a, r, _peer(r)[1]).wait_recv()
        for cp in sends:
            cp.wait_send()
        for cp in locs:
            cp.wait()

    spec = pl.BlockSpec(memory_space=pltpu.VMEM)
    return pl.pallas_call(
        body, in_specs=[spec] * n, out_specs=[spec] * n,
        out_shape=[jax.ShapeDtypeStruct((N_DEV,) + a.shape, a.dtype) for a in arrs],
        scratch_shapes=[pltpu.SemaphoreType.DMA((n, N_DEV - 1)), pltpu.SemaphoreType.DMA((n, N_DEV - 1)),
                        pltpu.SemaphoreType.DMA((n,))],
        name=name, compiler_params=pltpu.CompilerParams(vmem_limit_bytes=VMEM_LIMIT))(*arrs)


def _slot(rel_div):
    rel, div = rel_div
    idx = _peer(rel)[1]
    return idx if div == 1 else idx // div


AG_SPREAD = tuple((r, None, (0, 1), (r, 1)) for r in (1, 2, 4, 6))
AG_FORWARD = tuple((1, (k, 1), (k, 1), (1 ^ k, 1)) for k in (2, 4, 6))
RS_PAIR = tuple((1, (1 ^ k, 1), (1 ^ k, 2), (k, 2)) for k in (0, 2, 4, 6))
RS_CHIPS = tuple((r, (r, 2), (0, 2), (r, 2)) for r in (2, 4, 6))


def _plan_copies(plan, n, src_refs, land_refs, send_sems, recv_sems, arriving):
    out = []
    for i, (r, src, dst, recv) in enumerate(plan):
        peer = _peer(r)[0]
        for a in range(n):
            out.append(pltpu.make_async_remote_copy(
                src_ref=src_refs[a] if src is None else src_refs[a].at[_slot(src)],
                dst_ref=land_refs[a].at[_slot(recv if arriving else dst)],
                send_sem=send_sems.at[i * n + a], recv_sem=recv_sems.at[i * n + a],
                device_id=peer, device_id_type=MESH))
    return out


def _exchange_hbm(plan, srcs, lands, name, after=()):
    n = len(lands)
    fresh = isinstance(lands[0], jax.ShapeDtypeStruct)
    ins = ([] if srcs is None else list(srcs)) + ([] if fresh else list(lands))
    ns = 0 if srcs is None else n
    n_data = len(ins)
    ins = ins + list(after)

    def body(*refs):
        outs = refs[len(ins):len(ins) + n]
        send_sems, recv_sems = refs[-2:]
        src_refs = refs[:n] if srcs is not None else refs[ns:ns + n]
        sends = _plan_copies(plan, n, src_refs, outs, send_sems, recv_sems, False)
        for cp in sends:
            cp.start()
        for cp in _plan_copies(plan, n, src_refs, outs, send_sems, recv_sems, True):
            cp.wait_recv()
        for cp in sends:
            cp.wait_send()

    hbm = pl.BlockSpec(memory_space=pltpu.HBM)
    k = len(plan) * n
    return pl.pallas_call(
        body, name=name, in_specs=[hbm] * n_data + [pl.BlockSpec(memory_space=pl.ANY)] * len(after), out_specs=[hbm] * n,
        out_shape=[jax.ShapeDtypeStruct(a.shape, a.dtype) for a in lands],
        scratch_shapes=[pltpu.SemaphoreType.DMA((k,)), pltpu.SemaphoreType.DMA((k,))],
        input_output_aliases={} if fresh else {ns + i: i for i in range(n)},
        compiler_params=pltpu.CompilerParams(vmem_limit_bytes=VMEM_LIMIT))(*ins)


def _plan_start(plan, srcs, land_shapes, after, name):
    n = len(srcs)

    def body(*refs):
        src_refs, land_refs = refs[:n], refs[n:2 * n]
        send_sems, recv_sems = refs[2 * n + 1], refs[2 * n + 2]
        for cp in _plan_copies(plan, n, src_refs, land_refs, send_sems, recv_sems, False):
            cp.start()
        refs[-1][...] = jnp.zeros_like(refs[-1])

    hbm = pl.BlockSpec(memory_space=pltpu.HBM)
    sem = pl.BlockSpec(memory_space=pltpu.SEMAPHORE)
    k = len(plan) * n
    srcs = [pltpu.with_memory_space_constraint(a, pltpu.HBM) for a in srcs]
    lands = [pltpu.with_memory_space_constraint(lax.empty(shp, a.dtype), pltpu.HBM) for shp, a in zip(land_shapes, srcs)]
    res = pl.pallas_call(
        body, name=name,
        in_specs=[hbm] * (2 * n) + [pl.BlockSpec(memory_space=pl.ANY)],
        out_specs=[sem, sem] + [hbm] * (2 * n) + [pl.BlockSpec(memory_space=pltpu.VMEM)],
        out_shape=[pltpu.SemaphoreType.DMA((k,)), pltpu.SemaphoreType.DMA((k,))]
        + [pltpu.HBM(a.shape, a.dtype) for a in srcs] + [pltpu.HBM(shp, a.dtype) for shp, a in zip(land_shapes, srcs)]
        + [jax.ShapeDtypeStruct((8, LANES), F32)],
        input_output_aliases={i: 2 + i for i in range(2 * n)},
        compiler_params=pltpu.CompilerParams(has_side_effects=pltpu.SideEffectType.DATAFLOW_SIDE_EFFECTING),
    )(*srcs, *lands, after)
    return res[0], res[1], list(res[2:2 + n]), list(res[2 + n:2 + 2 * n]), res[-1]


def _plan_wait(plan, handle, after, name):
    send_sems, recv_sems, srcs, lands, _ = handle
    n = len(srcs)
    after = list(after)

    def body(*refs):
        src_refs, land_refs = refs[:n], refs[n:2 * n]
        ssem, rsem = refs[2 * n], refs[2 * n + 1]
        for cp in _plan_copies(plan, n, src_refs, land_refs, ssem, rsem, False):
            cp.wait_send()
        for cp in _plan_copies(plan, n, src_refs, land_refs, ssem, rsem, True):
            cp.wait_recv()

    hbm = pl.BlockSpec(memory_space=pltpu.HBM)
    sem = pl.BlockSpec(memory_space=pltpu.SEMAPHORE)
    res = pl.pallas_call(
        body, name=name,
        in_specs=[hbm] * (2 * n) + [sem, sem] + [pl.BlockSpec(memory_space=pl.ANY)] * len(after),
        out_specs=[hbm] * (2 * n),
        out_shape=[pltpu.HBM(a.shape, a.dtype) for a in srcs] + [pltpu.HBM(a.shape, a.dtype) for a in lands],
        input_output_aliases={i: i for i in range(2 * n)},
        compiler_params=pltpu.CompilerParams(has_side_effects=pltpu.SideEffectType.DATAFLOW_SIDE_EFFECTING),
    )(*srcs, *lands, send_sems, recv_sems, *after)
    return list(res[:n]), list(res[n:])


def _pair_sum(send, got, core, name):
    _, r, c = send.shape
    tr = 1024 if r % 1024 == 0 else r

    def body(core_ref, s_ref, g_ref, o_ref):
        o_ref[0] = (s_ref[0].astype(F32) + g_ref[0].astype(F32)).astype(o_ref.dtype)

    return pl.pallas_call(
        body, name=name,
        grid_spec=pltpu.PrefetchScalarGridSpec(
            num_scalar_prefetch=1, grid=(N_DEV // 2, r // tr),
            in_specs=[pl.BlockSpec((1, tr, c), lambda kc, i, core_ref: (2 * kc + core_ref[0], i, 0)),
                      pl.BlockSpec((1, tr, c), lambda kc, i, core_ref: (kc, i, 0))],
            out_specs=pl.BlockSpec((1, tr, c), lambda kc, i, core_ref: (kc, i, 0))),
        out_shape=jax.ShapeDtypeStruct((N_DEV // 2, r, c), send.dtype),
        compiler_params=_cp(("parallel", "parallel")))(core, send, got)


def _ada_mod(c_all, ada_w, ada_b_cols, name):
    nl, d, wc = ada_w.shape

    def body(c_ref, w_ref, b_ref, ca_ref, mod_ref):
        cv = c_ref[...]
        ca = cv * _sigmoid(cv)
        ca_ref[...] = ca
        mod_ref[0] = _dotf(ca, w_ref[0]) + b_ref[0]

    return pl.pallas_call(
        body, grid=(nl,),
        in_specs=[pl.BlockSpec((N_DEV, d), lambda l: (0, 0)), pl.BlockSpec((1, d, wc), lambda l: (l, 0, 0)),
                  pl.BlockSpec((1, 1, wc), lambda l: (l, 0, 0))],
        out_specs=[pl.BlockSpec((N_DEV, d), lambda l: (0, 0)), pl.BlockSpec((1, N_DEV, wc), lambda l: (l, 0, 0))],
        out_shape=[jax.ShapeDtypeStruct((N_DEV, d), F32), jax.ShapeDtypeStruct((nl, N_DEV, wc), F32)],
        name=name, compiler_params=_cp(("arbitrary",)))(c_all, ada_w, ada_b_cols)


def _adam(w, g, m, v):
    m2 = ADAM_B1 * m + (1.0 - ADAM_B1) * g
    v2 = ADAM_B2 * v + (1.0 - ADAM_B2) * (g * g)
    m_hat = m2 / (1.0 - ADAM_B1 ** ADAM_STEP)
    v_hat = v2 / (1.0 - ADAM_B2 ** ADAM_STEP)
    delta = -ADAM_LR * (m_hat / (jnp.sqrt(v_hat) + ADAM_EPS) + ADAM_WD * w)
    return delta, m2, v2


def _ada_grad_adam(c_act, d_mod, w, m, v, name):
    nl, d, wc = w.shape
    tk = min(1024, d)

    def body(c_ref, dm_ref, w_ref, m_ref, v_ref, g_ref, dl_ref, m2_ref, v2_ref):
        g = _dotf_tn(c_ref[...], dm_ref[0])
        delta, m2, v2 = _adam(w_ref[0], g, m_ref[0], v_ref[0])
        g_ref[0], dl_ref[0], m2_ref[0], v2_ref[0] = g, delta, m2, v2

    blk = pl.BlockSpec((1, tk, wc), lambda l, i: (l, i, 0))
    shp = jax.ShapeDtypeStruct(w.shape, F32)
    return pl.pallas_call(
        body, grid=(nl, d // tk),
        in_specs=[pl.BlockSpec((N_DEV, tk), lambda l, i: (0, i)), pl.BlockSpec((1, N_DEV, wc), lambda l, i: (l, 0, 0)),
                  blk, blk, blk],
        out_specs=[blk] * 4, out_shape=[shp] * 4, name=name,
        compiler_params=_cp(("parallel", "parallel")))(c_act, d_mod, w, m, v)


def _adam_big(recv, w, m, v, layer, prev, name, after=()):
    nl, r, c = w.shape
    tr = 512 if r % 512 == 0 else r
    nparts = recv.shape[0]

    def body(rc_ref, w_ref, m_ref, v_ref, *rest):
        g_ref, dl_ref, m2_ref, v2_ref = rest[-4:]
        g = rc_ref[0].astype(F32)
        for d in range(1, nparts):
            g = g + rc_ref[d].astype(F32)
        delta, m2, v2 = _adam(w_ref[0], g, m_ref[0], v_ref[0])
        g_ref[0], dl_ref[0], m2_ref[0], v2_ref[0] = g, delta, m2, v2

    blk = pl.BlockSpec((1, tr, c), lambda i: (layer, i, 0))
    shp = jax.ShapeDtypeStruct(w.shape, F32)
    prev = () if prev is None else tuple(prev)
    return pl.pallas_call(
        body, grid=(r // tr,),
        in_specs=[pl.BlockSpec((nparts, tr, c), lambda i: (0, i, 0)), blk, blk, blk]
        + [pl.BlockSpec(memory_space=pl.ANY)] * (len(prev) + len(after)),
        out_specs=[blk] * 4, out_shape=[shp] * 4, name=name,
        input_output_aliases={4 + j: j for j in range(len(prev))},
        compiler_params=_cp(("parallel",)))(recv, w, m, v, *prev, *after)


def _sum_devices(gathered, name):
    _, r, c = gathered.shape

    def body(g_ref, o_ref):
        acc = g_ref[0]
        for d in range(1, N_DEV):
            acc = acc + g_ref[d]
        o_ref[...] = acc

    spec = pl.BlockSpec(memory_space=pltpu.VMEM)
    return pl.pallas_call(body, in_specs=[spec], out_specs=spec, out_shape=jax.ShapeDtypeStruct((r, c), F32),
                          name=name, compiler_params=pltpu.CompilerParams(vmem_limit_bytes=VMEM_LIMIT))(gathered)


def _adam_small(ws, gs, ms, vs, name):
    n = len(ws)

    def body(*refs):
        for i in range(n):
            w_ref, g_ref, m_ref, v_ref = (refs[k * n + i] for k in range(4))
            dl_ref, m2_ref, v2_ref = (refs[(4 + k) * n + i] for k in range(3))
            dl_ref[...], m2_ref[...], v2_ref[...] = _adam(w_ref[...], g_ref[...], m_ref[...], v_ref[...])

    spec = pl.BlockSpec(memory_space=pltpu.VMEM)
    shapes = [jax.ShapeDtypeStruct(w.shape, F32) for w in ws]
    res = pl.pallas_call(body, in_specs=[spec] * (4 * n), out_specs=[spec] * (3 * n), out_shape=shapes * 3, name=name,
                         compiler_params=pltpu.CompilerParams(vmem_limit_bytes=VMEM_LIMIT))(*ws, *gs, *ms, *vs)
    return res[:n], res[n:2 * n], res[2 * n:]


def _pack(parts):
    flat = jnp.concatenate([p.reshape(-1).astype(F32) for p in parts])
    assert flat.shape[0] % LANES == 0, flat.shape
    return flat.reshape(-1, LANES)


def _unpack(packed, shapes):
    flat = packed.reshape(-1)
    out, off = [], 0
    for shp in shapes:
        size = 1
        for dim in shp:
            size *= dim
        out.append(flat[off:off + size].reshape(shp))
        off += size
    return out


def _gather_cols(g, per):
    g = jnp.moveaxis(g, 0, -2)
    return g.reshape(g.shape[:-2] + (N_DEV * per,))


def _scatter_cols(g, per):
    return jnp.moveaxis(g.reshape(g.shape[:-1] + (N_DEV, per)), -2, 0)


def _my_cols(full, me, per):
    return lax.dynamic_slice_in_dim(full, me * per, per, axis=full.ndim - 1)


def kernel(x, c, positions, ada_w, ada_b, norm_g, w_in, gla_wg_f, gla_bg_f, gla_wg_b, gla_bg_b, gla_norm_g, mla_q_norm_g, mla_kv_norm_g, mla_w_uq, mla_w_ukv, mla_out_g, conv_w, conv_out_g, w_out, final_g, loss_target, m_ada_w, m_ada_b, m_norm_g, m_w_in, m_gla_wg_f, m_gla_bg_f, m_gla_wg_b, m_gla_bg_b, m_gla_norm_g, m_mla_q_norm_g, m_mla_kv_norm_g, m_mla_w_uq, m_mla_w_ukv, m_mla_out_g, m_conv_w, m_conv_out_g, m_w_out, m_final_g, v_ada_w, v_ada_b, v_norm_g, v_w_in, v_gla_wg_f, v_gla_bg_f, v_gla_wg_b, v_gla_bg_b, v_gla_norm_g, v_mla_q_norm_g, v_mla_kv_norm_g, v_mla_w_uq, v_mla_w_ukv, v_mla_out_g, v_conv_w, v_conv_out_g, v_w_out, v_final_g):
    me = 4 * lax.axis_index("x") + 2 * lax.axis_index("y") + lax.axis_index("c")
    nl = ada_w.shape[0]
    s, d = x.shape[1], x.shape[2]
    ada_cols = ada_w.shape[2]
    wgc, cwc = gla_wg_f.shape[2], conv_w.shape[2]

    (g0,) = _gather_small([_pack([c, gla_wg_f, gla_wg_b, conv_w])], "gather_small_in")
    g0 = g0.reshape(N_DEV, -1)
    o1, o2, o3 = d, d + gla_wg_f.size, d + 2 * gla_wg_f.size
    c_all = g0[:, :o1]
    wgf_full = _gather_cols(g0[:, o1:o2].reshape((N_DEV,) + gla_wg_f.shape), wgc)
    wgb_full = _gather_cols(g0[:, o2:o3].reshape((N_DEV,) + gla_wg_b.shape), wgc)
    convw_full = _gather_cols(g0[:, o3:].reshape((N_DEV,) + conv_w.shape), cwc)

    ada_b_cols = _my_cols(ada_b, me, ada_cols).reshape(nl, 1, ada_cols)
    c_act, mod_cols = _ada_mod(c_all, ada_w, ada_b_cols, "ada_mod")
    (g1,) = _gather_small([_pack([mod_cols])], "gather_mod")
    mod_all = g1.reshape(N_DEV, nl, N_DEV, ada_cols)
    mod_mine = _gather_cols(lax.dynamic_index_in_dim(mod_all, me, axis=2, keepdims=False), ada_cols)

    inv_freq = ROPE_THETA ** (-jnp.arange(0, MROPE, 2, dtype=F32) / MROPE)
    ang = positions[0].astype(F32)[:, None] * inv_freq
    cos, sin = jnp.tile(jnp.cos(ang), (1, LANES * 2 // MROPE)), jnp.tile(jnp.sin(ang), (1, LANES * 2 // MROPE))

    big = [w_in, w_out, mla_w_uq, mla_w_ukv]
    big_names = ["w_in", "w_out", "mla_w_uq", "mla_w_ukv"]

    def local_blocks(l):
        return [w[l].astype(MXU) for w in big]

    def put_own(lands, own):
        return [lax.dynamic_update_index_in_dim(ld, o, me, 0) for ld, o in zip(lands, own)]

    def layer_weights(l, gw_in=None, gw_out=None, gw_uq=None, gw_ukv=None):
        small = dict(norm_g=norm_g[l], gla_wg_f=wgf_full[l], gla_bg_f=gla_bg_f[l], gla_wg_b=wgb_full[l],
                     gla_bg_b=gla_bg_b[l], gla_norm_g=gla_norm_g[l], mla_q_norm_g=mla_q_norm_g[l],
                     mla_kv_norm_g=mla_kv_norm_g[l], mla_out_g=mla_out_g[l], conv_w=convw_full[l],
                     conv_out_g=conv_out_g[l])
        return _prep_layer_weights(
            None if gw_in is None else _perm_gathered(gw_in, IN_SEGS, PW),
            None if gw_out is None else gw_out.reshape((-1,) + gw_out.shape[2:]),
            None if gw_uq is None else _heads_apart(_gather_cols(gw_uq, mla_w_uq.shape[2])),
            None if gw_ukv is None else _gather_cols(gw_ukv, mla_w_ukv.shape[2]), small)

    def land_shapes(blocks, slots):
        return [jax.ShapeDtypeStruct((slots,) + b.shape, b.dtype) for b in blocks]

    def slots_of(blocks):
        return [(N_DEV,) + b.shape for b in blocks]

    def forwarded(lands, blocks, tag):
        return put_own(_exchange_hbm(AG_FORWARD, None, lands, f"gather_{tag}_forward"), blocks)

    first = local_blocks(0)
    w_in_start = _plan_start(AG_SPREAD, first[:1], slots_of(first[:1]), mod_mine, "gather_w_in_l0_start")
    adam_w_in = [a + w_in_start[-1][0, 0] for a in (w_in, m_w_in, v_w_in)]
    shift0, scale0 = (mod_mine[0, i * d:(i + 1) * d].reshape(1, d) for i in range(2))
    h_first = _norm_mod(x[0], norm_g[0].reshape(1, d), scale0 + w_in_start[-1][0, 0], shift0, "norm_mod_l0")
    (gw_in,) = forwarded(*reversed(_plan_wait(AG_SPREAD, w_in_start, adam_w_in + [h_first], "gather_w_in_l0_wait")),
                         "w_in_l0")
    rest = _plan_start(AG_SPREAD, first[1:], slots_of(first[1:]), gw_in, "gather_rest_l0_start")
    h = x[0]
    saved, layers, mods = [], [], []
    pending = {}
    for l in range(nl):
        shift, scale, gate = (mod_mine[l, i * d:(i + 1) * d].reshape(1, d) for i in range(3))
        nxt = local_blocks(l + 1) if l + 1 < nl else None

        def start_next(after, wt_late, l=l, nxt=nxt):
            if nxt is not None:
                pending[l + 1] = _plan_start(AG_SPREAD, nxt, slots_of(nxt), after, f"gather_weights_l{l + 1}_start")
                wt_late["q_norm_g"] = layers[l]["q_norm_g"] + pending[l + 1][-1][0, 0]
            return wt_late

        if l == 0:
            in_after = (rest[-1],)
            layers.append(layer_weights(0, gw_in))

            def late(proj):
                got = forwarded(*reversed(_plan_wait(AG_SPREAD, rest, [proj], "gather_rest_l0_wait")), "rest_l0")
                full = layer_weights(0, None, *got)
                return start_next(got[0], {k: full[k] for k in ("w_out", "w_uq", "w_ukv")})
        else:
            got = forwarded(*reversed(_plan_wait(AG_SPREAD, pending.pop(l), [h], f"gather_weights_l{l}_wait")), f"weights_l{l}")
            layers.append(layer_weights(l, *got))
            in_after = ()

            def late(proj):
                return start_next(proj, {})
        mods.append((shift, scale, gate))
        h, sv = _layer_fwd(h, mods[l], layers[l], cos, sin, f"l{l}", late, in_after, h_first if l == 0 else None)
        saved.append(sv)
        blocks = nxt
    loss_part, d_h, d_final_g, *started = _final_loss(h, final_g.reshape(1, d), loss_target[0], "final_loss",
                                                      (saved[-1]["u"], mods[-1][2]))
    loss = lax.psum(loss_part[0, 0], ("x", "y", "c"))
    first_after = (loss.reshape(1, 1),)

    send_of = dict(w_in=lambda g: _scatter_perm(g, IN_SEGS, w_in.shape[2]),
                   w_out=lambda g: g.reshape((N_DEV,) + w_out.shape[1:]),
                   w_uq=lambda g: _scatter_cols(_heads_together(g), mla_w_uq.shape[2]),
                   w_ukv=lambda g: _scatter_cols(g, mla_w_ukv.shape[2]))

    def grad_sends(gr):
        return [send_of[k](g).astype(MXU) for k, g in gr.items()]

    my_chip = me // 2
    my_core = (me % 2).astype(jnp.int32).reshape(1)

    def chip_sums(gr, tag):
        sends = grad_sends(gr)
        got = _exchange_hbm(RS_PAIR, sends, land_shapes([sd[0] for sd in sends], N_DEV // 2), f"scatter_grads_{tag}_pair")
        return [_pair_sum(sd, gt, my_core, f"pair_sum_{k}_{tag}") for sd, gt, k in zip(sends, got, gr)]

    def with_own_chip(lands, sums):
        return [lax.dynamic_update_index_in_dim(ld, lax.dynamic_index_in_dim(sm, my_chip, axis=0, keepdims=False),
                                                my_chip, 0) for ld, sm in zip(lands, sums)]

    small_names = ["norm_g", "gla_wg_f", "gla_bg_f", "gla_wg_b", "gla_bg_b", "gla_norm_g", "mla_q_norm_g",
                   "mla_kv_norm_g", "mla_out_g", "conv_w", "conv_out_g"]
    d_mods, grads, recv = [None] * nl, [None] * nl, [None] * nl
    flight = {}
    small = {}

    def gather_small(d_x, d_mod0, gr0):
        d_mods[0], grads[0] = d_mod0, _natural_small(gr0)
        d_mod_mine = jnp.stack([jnp.concatenate(d_mods[l], axis=-1)[0] for l in range(nl)])
        parts = [d_mod_mine] + [jnp.stack([grads[l][n] for l in range(nl)]) for n in small_names] + [d_final_g]
        (g2,) = _gather_small([_pack(parts)], "gather_small_grads")
        small["d_mod_all"] = g2.reshape(N_DEV, -1)[:, :d_mod_mine.size].reshape(N_DEV, nl, 3 * d)
        small["summed"] = dict(zip(["ada_b"] + small_names + ["final_g"],
                                   _unpack(_sum_devices(g2, "sum_small_grads"), [p.shape for p in parts])))
        return (g2,)

    pairs = {}
    def end_flight(key, after, name):
        sm, lands = _plan_wait(RS_CHIPS, flight.pop(key)[0], after, name)
        return with_own_chip(lands, sm)

    for l in reversed(range(nl)):
        def ship(big_grads, l=l):
            if l > 0:
                sends = grad_sends(big_grads)
                pairs[l] = (_plan_start(RS_PAIR, sends, [(N_DEV // 2,) + sd.shape[1:] for sd in sends],
                                        big_grads["w_in"], f"scatter_grads_l{l}_pair_start"), sends)
                return (pairs[l][0][-1],)
            sm = chip_sums(dict(w_in=big_grads["w_in"]), f"l{l}")
            flight[l] = (_plan_start(RS_CHIPS, sm, [a.shape for a in sm], big_grads["w_in"], f"scatter_grads_l{l}_start"),
                         sm)
            return (flight[l][0][-1],)

        def ship_rest(rest_grads, l=l):
            if l + 1 in flight:
                recv[l + 1] = end_flight(l + 1, list(rest_grads.values()), f"scatter_grads_l{l + 1}_wait")
            sm = chip_sums(rest_grads, f"l{l}_rest")
            flight["rest"] = (_plan_start(RS_CHIPS, sm, [a.shape for a in sm], rest_grads["w_out"],
                                          f"scatter_grads_l{l}_rest_start"), sm)
            return flight["rest"][0][-1][0, 0]

        if l + 1 in flight:
            first_after = (flight[l + 1][0][-1],)
        if l > 0:
            d_h, d_mods[l], gr = _layer_bwd(d_h, saved[l], mods[l], layers[l], cos, sin, f"l{l}", ship,
                                            started=started, below=(saved[l - 1]["u"], mods[l - 1][2]),
                                            first_after=first_after)
            started = gr["started_below"]
            grads[l] = _natural_small(gr)
            sends, got = _plan_wait(RS_PAIR, pairs.pop(l)[0], [d_h], f"scatter_grads_l{l}_pair_wait")
            sm = [_pair_sum(sd, gt, my_core, f"pair_sum_{n}_l{l}") for sd, gt, n in zip(sends, got, big_names)]
            flight[l] = (_plan_start(RS_CHIPS, sm, [a.shape for a in sm], d_h, f"scatter_grads_l{l}_start"), sm)
        else:
            d_h, _, _ = _layer_bwd(d_h, saved[l], mods[l], layers[l], cos, sin, f"l{l}", ship, gather_small,
                                   ship_rest, started=started, first_after=first_after)
    pending = flight[0][0]
    grad_x = d_h[None]
    summed = small["summed"]
    summed["gla_wg_f"] = _my_cols(summed["gla_wg_f"], me, wgc)
    summed["gla_wg_b"] = _my_cols(summed["gla_wg_b"], me, wgc)
    summed["conv_w"] = _my_cols(summed["conv_w"], me, cwc)

    d_mod_cols = jnp.moveaxis(_my_cols(small["d_mod_all"], me, ada_cols), 0, 1) + pending[-1][0, 0]
    out = {}
    out["ada_w"] = _ada_grad_adam(c_act, d_mod_cols, ada_w, m_ada_w, v_ada_w, "ada_grad_adam")

    given = dict(ada_b=(ada_b, m_ada_b, v_ada_b), norm_g=(norm_g, m_norm_g, v_norm_g),
                 gla_wg_f=(gla_wg_f, m_gla_wg_f, v_gla_wg_f), gla_bg_f=(gla_bg_f, m_gla_bg_f, v_gla_bg_f),
                 gla_wg_b=(gla_wg_b, m_gla_wg_b, v_gla_wg_b), gla_bg_b=(gla_bg_b, m_gla_bg_b, v_gla_bg_b),
                 gla_norm_g=(gla_norm_g, m_gla_norm_g, v_gla_norm_g),
                 mla_q_norm_g=(mla_q_norm_g, m_mla_q_norm_g, v_mla_q_norm_g),
                 mla_kv_norm_g=(mla_kv_norm_g, m_mla_kv_norm_g, v_mla_kv_norm_g),
                 mla_out_g=(mla_out_g, m_mla_out_g, v_mla_out_g), conv_w=(conv_w, m_conv_w, v_conv_w),
                 conv_out_g=(conv_out_g, m_conv_out_g, v_conv_out_g), final_g=(final_g, m_final_g, v_final_g))
    names = list(given)

    def two_d(a):
        return a.reshape(1, -1) if a.ndim == 1 else a

    g_nat = [summed[n].reshape(given[n][0].shape) for n in names]
    res = _adam_small([two_d(given[n][0]) for n in names], [two_d(g) for g in g_nat],
                      [two_d(given[n][1]) for n in names], [two_d(given[n][2]) for n in names], "adam_small")
    for i, n in enumerate(names):
        out[n] = (g_nat[i],) + tuple(r[i].reshape(given[n][0].shape) for r in res)

    state = dict(w_in=adam_w_in, w_out=(w_out, m_w_out, v_w_out), mla_w_uq=(mla_w_uq, m_mla_w_uq, v_mla_w_uq),
                 mla_w_ukv=(mla_w_ukv, m_mla_w_ukv, v_mla_w_ukv))
    done = [out["ada_w"][0], res[0][0]]
    def adam_layer(l, names, partials):
        for n, rc in zip(names, partials):
            out[n] = _adam_big(rc, *state[n], l, out.get(n), f"adam_{n}_l{l}",
                               (pending[-1],))
        return [out[n][0] for n in names]

    for l in reversed(range(nl)):
        if l > 0:
            done = done + adam_layer(l, big_names, recv[l])
        else:
            done = done + adam_layer(0, big_names[1:], end_flight("rest", done, "scatter_grads_l0_rest_wait"))
            adam_layer(0, big_names[:1], end_flight(0, done, "scatter_grads_l0_wait"))

    order = ["ada_w", "ada_b", "norm_g", "w_in", "gla_wg_f", "gla_bg_f", "gla_wg_b", "gla_bg_b", "gla_norm_g",
             "mla_q_norm_g", "mla_kv_norm_g", "mla_w_uq", "mla_w_ukv", "mla_out_g", "conv_w", "conv_out_g", "w_out",
             "final_g"]
    return (loss, grad_x, *[out[n][0] for n in order], *[out[n][1] for n in order], *[out[n][2] for n in order],
            *[out[n][3] for n in order])
```
